```python
import jax, jax.numpy as jnp
from jax import lax
import numpy as np

D_MODEL = 1024
BATCH = 8
SEQ = 4096
DEPTH = 1

CTX_LEN = 256
GRID_W = 64
CONV_WIDTH = D_MODEL
CONV_K = 3
RET_HEADS = 8
RET_DV = D_MODEL // RET_HEADS
RET_DK = RET_DV // 2
RET_QK_WIDTH = RET_HEADS * RET_DK
RET_V_WIDTH = RET_HEADS * RET_DV
CHUNK = 128
ROPE_BASE = 10000.0
EPS = 1e-6
IN_WIDTHS = (CONV_WIDTH, CONV_WIDTH, CONV_WIDTH, CONV_WIDTH,
             RET_QK_WIDTH, RET_QK_WIDTH, RET_V_WIDTH, RET_V_WIDTH, D_MODEL, D_MODEL)
SPLIT_POINTS = tuple(int(s) for s in np.cumsum(IN_WIDTHS)[:-1])
IN_WIDTH = int(sum(IN_WIDTHS))

kernel_name = "hybrid_conv_retention_dit_block"


def rmsnorm(x, w):
    x32 = x.astype(jnp.float32)
    y = x32 * lax.rsqrt(jnp.mean(x32 * x32, axis=-1, keepdims=True) + EPS)
    return (y * w.astype(jnp.float32)).astype(x.dtype)


def dwconv_centred(u, w, b):
    L = u.shape[1]
    pad = CONV_K // 2
    up = jnp.pad(u, ((0, 0), (pad, pad), (0, 0)))
    return sum(up[:, j:j + L] * w[j] for j in range(CONV_K)) + b


def to_heads(t, d):
    B_, L, _ = t.shape
    return t.reshape(B_, L, RET_HEADS, d).transpose(0, 2, 1, 3)


def rope2d(t):
    L = t.shape[2]
    rows = L // GRID_W
    row = jnp.repeat(jnp.arange(rows), GRID_W).astype(jnp.float32)
    col = jnp.tile(jnp.arange(GRID_W), rows).astype(jnp.float32)
    nf = RET_DK // 4
    inv = ROPE_BASE ** (-jnp.arange(nf, dtype=jnp.float32) / nf)
    ang = jnp.concatenate([row[:, None] * inv, col[:, None] * inv], axis=-1)
    cos = jnp.cos(ang).astype(t.dtype)
    sin = jnp.sin(ang).astype(t.dtype)
    half = RET_DK // 2
    t1, t2 = t[..., :half], t[..., half:]
    return jnp.concatenate([t1 * cos - t2 * sin, t1 * sin + t2 * cos], axis=-1)


def retention_scan(q, k, v, log_gamma, s0):
    B_, H, L, dk = q.shape
    dv = v.shape[-1]
    n = L // CHUNK
    qc = q.astype(jnp.float32).reshape(B_, H, n, CHUNK, dk)
    kc = k.astype(jnp.float32).reshape(B_, H, n, CHUNK, dk)
    vc = v.astype(jnp.float32).reshape(B_, H, n, CHUNK, dv)
    idx = jnp.arange(CHUNK, dtype=jnp.float32)
    diff = idx[:, None] - idx[None, :]
    dmask = jnp.where(diff >= 0, jnp.exp(log_gamma[:, None, None] * jnp.maximum(diff, 0.0)), 0.0)
    scores = jnp.einsum('bhnid,bhnjd->bhnij', qc, kc) * dmask[None, :, None]
    inner = jnp.einsum('bhnij,bhnjv->bhniv', scores, vc)
    k_dec = jnp.exp(log_gamma[:, None] * (CHUNK - 1 - idx))
    kv = jnp.einsum('bhnjd,hj,bhnjv->bhndv', kc, k_dec, vc)
    chunk_decay = jnp.exp(log_gamma * CHUNK)[None, :, None, None]

    def step(s, kv_c):
        return chunk_decay * s + kv_c, s

    _, s_prev = lax.scan(step, s0.astype(jnp.float32), jnp.moveaxis(kv, 2, 0))
    s_prev = jnp.moveaxis(s_prev, 0, 2)
    q_dec = jnp.exp(log_gamma[:, None] * (idx + 1.0))
    cross = jnp.einsum('bhnid,hi,bhndv->bhniv', qc, q_dec, s_prev)
    return (inner + cross).reshape(B_, H, L, dv)


def bidir_retention(q, k, v, lg_f, lg_b, s0_f, s0_b):
    o_f = retention_scan(q, k, v, lg_f, s0_f)
    flip = lambda t: t[:, :, ::-1]
    o_b = retention_scan(flip(q), flip(k), flip(v), lg_b, s0_b)
    return o_f + flip(o_b)


def ctx_final_states(k, v, lg_f, lg_b):
    Lc = k.shape[2]
    m = jnp.arange(Lc, dtype=jnp.float32)
    k32 = k.astype(jnp.float32)
    v32 = v.astype(jnp.float32)
    dec_f = jnp.exp(lg_f[:, None] * (Lc - 1 - m))
    dec_b = jnp.exp(lg_b[:, None] * m)
    s_f = jnp.einsum('bhmd,hm,bhmv->bhdv', k32, dec_f, v32)
    s_b = jnp.einsum('bhmd,hm,bhmv->bhdv', k32, dec_b, v32)
    return s_f, s_b


def retention_groupnorm(ret, gn_w, dtype):
    mu = jnp.mean(ret, axis=-1, keepdims=True)
    var = jnp.mean(jnp.square(ret - mu), axis=-1, keepdims=True)
    rn = (ret - mu) * lax.rsqrt(var + EPS)
    B_, H, L, dv = rn.shape
    rn = rn.transpose(0, 2, 1, 3).reshape(B_, L, H * dv)
    return (rn * gn_w.astype(jnp.float32)).astype(dtype)


def split_heads(proj, rotary):
    h, bg, cg, za, q, k, v, zb, ga, gb = jnp.split(proj, SPLIT_POINTS, axis=-1)
    q = to_heads(q, RET_DK)
    k = to_heads(k, RET_DK) * (RET_DK ** -0.5)
    v = to_heads(v, RET_DV)
    if rotary:
        q, k = rope2d(q), rope2d(k)
    return (h, bg, cg, za, zb, ga, gb), (q, k, v)


def merge_branches(parts, ret, conv_w, conv_b, gn_w, w_a, w_b, w_out):
    h, bg, cg, za, zb, ga, gb = parts
    conv_out = dwconv_centred(cg * h, conv_w, conv_b)
    y_a = (jax.nn.silu(za) * bg * conv_out) @ w_a
    ret_n = retention_groupnorm(ret, gn_w, h.dtype)
    y_b = (jax.nn.silu(zb) * ret_n) @ w_b
    return (jax.nn.sigmoid(ga) * y_a + jax.nn.sigmoid(gb) * y_b) @ w_out


def _fwd_setup_inputs(seed: int = 0) -> dict:
    key = jax.random.key(seed)
    ks = jax.random.split(key, 20)
    f32 = jnp.float32
    nrm = lambda k, shape, s: jax.random.normal(k, shape, f32) * s
    base_gamma = 1.0 - 2.0 ** (-5.0 - np.arange(RET_HEADS, dtype=np.float32))
    base_logit = jnp.asarray(np.log(base_gamma / (1.0 - base_gamma)), f32)
    decay_logit = base_logit[None, None, :] + nrm(ks[10], (DEPTH, 2, RET_HEADS), 0.1)
    return {
        "x": nrm(ks[0], (BATCH, SEQ, D_MODEL), 1.0),
        "c": nrm(ks[1], (BATCH, D_MODEL), 1.0),
        "ctx": nrm(ks[2], (BATCH, CTX_LEN, D_MODEL), 1.0),
        "c_ctx": nrm(ks[3], (D_MODEL,), 1.0),
        "norm_w": 1.0 + nrm(ks[4], (DEPTH, D_MODEL), 0.02),
        "ada_w": nrm(ks[5], (DEPTH, D_MODEL, 3 * D_MODEL), 0.5 * D_MODEL ** -0.5),
        "ada_b": nrm(ks[6], (DEPTH, 3 * D_MODEL), 0.02),
        "w_in": nrm(ks[7], (DEPTH, D_MODEL, IN_WIDTH), D_MODEL ** -0.5),
        "conv_w": nrm(ks[8], (DEPTH, CONV_K, CONV_WIDTH), CONV_K ** -0.5),
        "conv_b": nrm(ks[9], (DEPTH, CONV_WIDTH), 0.02),
        "decay_logit": decay_logit,
        "gn_w": 1.0 + nrm(ks[11], (DEPTH, RET_V_WIDTH), 0.02),
        "w_a": nrm(ks[12], (DEPTH, CONV_WIDTH, D_MODEL), CONV_WIDTH ** -0.5),
        "w_b": nrm(ks[13], (DEPTH, RET_V_WIDTH, D_MODEL), RET_V_WIDTH ** -0.5),
        "w_out": nrm(ks[14], (DEPTH, D_MODEL, D_MODEL), D_MODEL ** -0.5),
        "final_norm_w": 1.0 + nrm(ks[15], (D_MODEL,), 0.02),
    }


def _fwd_reference(x, c, ctx, c_ctx, norm_w, ada_w, ada_b, w_in, conv_w, conv_b,
              decay_logit, gn_w, w_a, w_b, w_out, final_norm_w):
    for l in range(DEPTH):
        mod_x = jax.nn.silu(c) @ ada_w[l] + ada_b[l]
        sh_x, sc_x, g_x = jnp.split(mod_x[:, None, :], 3, axis=-1)
        mod_c = jax.nn.silu(c_ctx) @ ada_w[l] + ada_b[l]
        sh_c, sc_c, g_c = jnp.split(mod_c, 3, axis=-1)
        xm = rmsnorm(x, norm_w[l]) * (1.0 + sc_x) + sh_x
        cm = rmsnorm(ctx, norm_w[l]) * (1.0 + sc_c) + sh_c
        parts_x, (qx, kx, vx) = split_heads(xm @ w_in[l], rotary=True)
        parts_c, (qc, kc, vc) = split_heads(cm @ w_in[l], rotary=False)
        lg = jax.nn.log_sigmoid(decay_logit[l].astype(jnp.float32))
        s_f, s_b = ctx_final_states(kc, vc, lg[0], lg[1])
        ret_x = bidir_retention(qx, kx, vx, lg[0], lg[1], s_f, s_b)
        y_x = merge_branches(parts_x, ret_x, conv_w[l], conv_b[l], gn_w[l], w_a[l], w_b[l], w_out[l])
        if l < DEPTH - 1:
            zeros = jnp.zeros_like(s_f)
            ret_c = bidir_retention(qc, kc, vc, lg[0], lg[1], zeros, zeros)
            y_c = merge_branches(parts_c, ret_c, conv_w[l], conv_b[l], gn_w[l], w_a[l], w_b[l], w_out[l])
            ctx = ctx + g_c * y_c
        x = x + g_x * y_x
    return rmsnorm(x, final_norm_w)


import jax as _jax
import jax.numpy as _jnp

TWIN_FORMAT = 'train_step'
FWD_PARAMS = ['x', 'c', 'ctx', 'c_ctx', 'norm_w', 'ada_w', 'ada_b', 'w_in', 'conv_w', 'conv_b', 'decay_logit', 'gn_w', 'w_a', 'w_b', 'w_out', 'final_norm_w']
TWIN_WEIGHTS = ['c_ctx', 'norm_w', 'ada_w', 'ada_b', 'w_in', 'conv_w', 'conv_b', 'decay_logit', 'gn_w', 'w_a', 'w_b', 'w_out', 'final_norm_w']
TWIN_DIFF_INPUT = 'x'
TWIN_INPUTS = ['x', 'c', 'ctx', 'c_ctx', 'norm_w', 'ada_w', 'ada_b', 'w_in', 'conv_w', 'conv_b', 'decay_logit', 'gn_w', 'w_a', 'w_b', 'w_out', 'final_norm_w', 'loss_target', 'm_c_ctx', 'm_norm_w', 'm_ada_w', 'm_ada_b', 'm_w_in', 'm_conv_w', 'm_conv_b', 'm_decay_logit', 'm_gn_w', 'm_w_a', 'm_w_b', 'm_w_out', 'm_final_norm_w', 'v_c_ctx', 'v_norm_w', 'v_ada_w', 'v_ada_b', 'v_w_in', 'v_conv_w', 'v_conv_b', 'v_decay_logit', 'v_gn_w', 'v_w_a', 'v_w_b', 'v_w_out', 'v_final_norm_w']
TWIN_OUTPUTS = ['loss', 'grad_x', 'grad_c_ctx', 'grad_norm_w', 'grad_ada_w', 'grad_ada_b', 'grad_w_in', 'grad_conv_w', 'grad_conv_b', 'grad_decay_logit', 'grad_gn_w', 'grad_w_a', 'grad_w_b', 'grad_w_out', 'grad_final_norm_w', 'delta_c_ctx', 'delta_norm_w', 'delta_ada_w', 'delta_ada_b', 'delta_w_in', 'delta_conv_w', 'delta_conv_b', 'delta_decay_logit', 'delta_gn_w', 'delta_w_a', 'delta_w_b', 'delta_w_out', 'delta_final_norm_w', 'new_m_c_ctx', 'new_m_norm_w', 'new_m_ada_w', 'new_m_ada_b', 'new_m_w_in', 'new_m_conv_w', 'new_m_conv_b', 'new_m_decay_logit', 'new_m_gn_w', 'new_m_w_a', 'new_m_w_b', 'new_m_w_out', 'new_m_final_norm_w', 'new_v_c_ctx', 'new_v_norm_w', 'new_v_ada_w', 'new_v_ada_b', 'new_v_w_in', 'new_v_conv_w', 'new_v_conv_b', 'new_v_decay_logit', 'new_v_gn_w', 'new_v_w_a', 'new_v_w_b', 'new_v_w_out', 'new_v_final_norm_w']
TWIN_LEAF_KINDS = {'loss': 'loss', 'grad_x': 'grad_x', 'grad_c_ctx': 'grad_w', 'grad_norm_w': 'grad_w', 'grad_ada_w': 'grad_w', 'grad_ada_b': 'grad_w', 'grad_w_in': 'grad_w', 'grad_conv_w': 'grad_w', 'grad_conv_b': 'grad_w', 'grad_decay_logit': 'grad_w', 'grad_gn_w': 'grad_w', 'grad_w_a': 'grad_w', 'grad_w_b': 'grad_w', 'grad_w_out': 'grad_w', 'grad_final_norm_w': 'grad_w', 'delta_c_ctx': 'delta_w', 'delta_norm_w': 'delta_w', 'delta_ada_w': 'delta_w', 'delta_ada_b': 'delta_w', 'delta_w_in': 'delta_w', 'delta_conv_w': 'delta_w', 'delta_conv_b': 'delta_w', 'delta_decay_logit': 'delta_w', 'delta_gn_w': 'delta_w', 'delta_w_a': 'delta_w', 'delta_w_b': 'delta_w', 'delta_w_out': 'delta_w', 'delta_final_norm_w': 'delta_w', 'new_m_c_ctx': 'new_m', 'new_m_norm_w': 'new_m', 'new_m_ada_w': 'new_m', 'new_m_ada_b': 'new_m', 'new_m_w_in': 'new_m', 'new_m_conv_w': 'new_m', 'new_m_conv_b': 'new_m', 'new_m_decay_logit': 'new_m', 'new_m_gn_w': 'new_m', 'new_m_w_a': 'new_m', 'new_m_w_b': 'new_m', 'new_m_w_out': 'new_m', 'new_m_final_norm_w': 'new_m', 'new_v_c_ctx': 'new_v', 'new_v_norm_w': 'new_v', 'new_v_ada_w': 'new_v', 'new_v_ada_b': 'new_v', 'new_v_w_in': 'new_v', 'new_v_conv_w': 'new_v', 'new_v_conv_b': 'new_v', 'new_v_decay_logit': 'new_v', 'new_v_gn_w': 'new_v', 'new_v_w_a': 'new_v', 'new_v_w_b': 'new_v', 'new_v_w_out': 'new_v', 'new_v_final_norm_w': 'new_v'}


def _forward(args):
    return _fwd_reference(*[args[k] for k in FWD_PARAMS])


def _output_shape():
    def fwd():
        inp = _fwd_setup_inputs(0)
        return _fwd_reference(*[inp[k] for k in FWD_PARAMS])
    out = _jax.eval_shape(fwd)
    return out.shape, out.dtype

N_MICROBATCH = 1
ADAM_LR = 0.001
ADAM_B1 = 0.9
ADAM_B2 = 0.999
ADAM_EPS = 1e-08
ADAM_WD = 0.01
ADAM_STEP = 10
PER_EXAMPLE_BATCH_AXIS = {'x': 0, 'c': 0, 'ctx': 0, 'loss_target': 0}
SHARED_INPUTS = []
_WEIGHT_DTYPES = {'c_ctx': _jnp.float32, 'norm_w': _jnp.float32, 'ada_w': _jnp.float32, 'ada_b': _jnp.float32, 'w_in': _jnp.float32, 'conv_w': _jnp.float32, 'conv_b': _jnp.float32, 'decay_logit': _jnp.float32, 'gn_w': _jnp.float32, 'w_a': _jnp.float32, 'w_b': _jnp.float32, 'w_out': _jnp.float32, 'final_norm_w': _jnp.float32}
MOMENT_SCALE = {'c_ctx': 1.409210e-02, 'norm_w': 5.906941e-02, 'ada_w': 5.134411e-02, 'ada_b': 8.444308e-02, 'w_in': 2.125826e-02, 'conv_w': 2.455606e-02, 'conv_b': 2.063886e-02, 'decay_logit': 5.089706e-02, 'gn_w': 2.030872e-02, 'w_a': 2.376215e-02, 'w_b': 1.876660e-02, 'w_out': 3.020748e-02, 'final_norm_w': 3.199398e+01}


def _to_microbatches(a, axis):
    t = _jnp.moveaxis(a, axis, 0)
    t = t.reshape((N_MICROBATCH, t.shape[0] // N_MICROBATCH) + t.shape[1:])
    return _jnp.moveaxis(t, 1, axis + 1)


def setup_inputs(seed: int = 0) -> dict:
    inp = _fwd_setup_inputs(seed)
    key = _jax.random.fold_in(_jax.random.key(seed), 7919)
    shape, _ = _output_shape()
    out = dict(inp)
    out["loss_target"] = _jax.random.normal(_jax.random.fold_in(key, 0), shape, _jnp.float32)
    for i, name in enumerate(TWIN_WEIGHTS):
        w = inp[name].astype(_jnp.float32)
        if MOMENT_SCALE is None:
            s = _jnp.sqrt(_jnp.mean(_jnp.square(w)) + 1e-30)
        else:
            s = MOMENT_SCALE[name]
        km, kv = _jax.random.split(_jax.random.fold_in(key, i + 1))
        out[name] = w
        out["m_" + name] = s * _jax.random.normal(km, w.shape, _jnp.float32)
        out["v_" + name] = (s * s) * _jax.random.uniform(kv, w.shape, _jnp.float32, 0.5, 1.5)
    if N_MICROBATCH > 1:
        for name, axis in PER_EXAMPLE_BATCH_AXIS.items():
            out[name] = _to_microbatches(out[name], axis)
    return {'x': out['x'], 'c': out['c'], 'ctx': out['ctx'], 'c_ctx': out['c_ctx'], 'norm_w': out['norm_w'], 'ada_w': out['ada_w'], 'ada_b': out['ada_b'], 'w_in': out['w_in'], 'conv_w': out['conv_w'], 'conv_b': out['conv_b'], 'decay_logit': out['decay_logit'], 'gn_w': out['gn_w'], 'w_a': out['w_a'], 'w_b': out['w_b'], 'w_out': out['w_out'], 'final_norm_w': out['final_norm_w'], 'loss_target': out['loss_target'], 'm_c_ctx': out['m_c_ctx'], 'm_norm_w': out['m_norm_w'], 'm_ada_w': out['m_ada_w'], 'm_ada_b': out['m_ada_b'], 'm_w_in': out['m_w_in'], 'm_conv_w': out['m_conv_w'], 'm_conv_b': out['m_conv_b'], 'm_decay_logit': out['m_decay_logit'], 'm_gn_w': out['m_gn_w'], 'm_w_a': out['m_w_a'], 'm_w_b': out['m_w_b'], 'm_w_out': out['m_w_out'], 'm_final_norm_w': out['m_final_norm_w'], 'v_c_ctx': out['v_c_ctx'], 'v_norm_w': out['v_norm_w'], 'v_ada_w': out['v_ada_w'], 'v_ada_b': out['v_ada_b'], 'v_w_in': out['v_w_in'], 'v_conv_w': out['v_conv_w'], 'v_conv_b': out['v_conv_b'], 'v_decay_logit': out['v_decay_logit'], 'v_gn_w': out['v_gn_w'], 'v_w_a': out['v_w_a'], 'v_w_b': out['v_w_b'], 'v_w_out': out['v_w_out'], 'v_final_norm_w': out['v_final_norm_w']}


def _loss(weights, diff, rest, loss_target):
    with _jax.named_scope("forward"):
        args = {**rest, TWIN_DIFF_INPUT: diff, **{k: w.astype(_WEIGHT_DTYPES[k]) for k, w in weights.items()}}
        y = _forward(args)
    with _jax.named_scope("loss_head"):
        err = _jnp.square(y.astype(_jnp.float32) - loss_target)
        return 0.5 * _jnp.sum(_jnp.mean(err, axis=-1)) if err.ndim else 0.5 * err


def _adamw(w, g, m, v):
    m = ADAM_B1 * m + (1.0 - ADAM_B1) * g
    v = ADAM_B2 * v + (1.0 - ADAM_B2) * _jnp.square(g)
    m_hat = m / (1.0 - ADAM_B1 ** ADAM_STEP)
    v_hat = v / (1.0 - ADAM_B2 ** ADAM_STEP)
    delta = -ADAM_LR * (m_hat / (_jnp.sqrt(v_hat) + ADAM_EPS) + ADAM_WD * w)
    return delta, m, v


def reference(x, c, ctx, c_ctx, norm_w, ada_w, ada_b, w_in, conv_w, conv_b, decay_logit, gn_w, w_a, w_b, w_out, final_norm_w, loss_target, m_c_ctx, m_norm_w, m_ada_w, m_ada_b, m_w_in, m_conv_w, m_conv_b, m_decay_logit, m_gn_w, m_w_a, m_w_b, m_w_out, m_final_norm_w, v_c_ctx, v_norm_w, v_ada_w, v_ada_b, v_w_in, v_conv_w, v_conv_b, v_decay_logit, v_gn_w, v_w_a, v_w_b, v_w_out, v_final_norm_w):
    given = dict(x=x, c=c, ctx=ctx, c_ctx=c_ctx, norm_w=norm_w, ada_w=ada_w, ada_b=ada_b, w_in=w_in, conv_w=conv_w, conv_b=conv_b, decay_logit=decay_logit, gn_w=gn_w, w_a=w_a, w_b=w_b, w_out=w_out, final_norm_w=final_norm_w, loss_target=loss_target, m_c_ctx=m_c_ctx, m_norm_w=m_norm_w, m_ada_w=m_ada_w, m_ada_b=m_ada_b, m_w_in=m_w_in, m_conv_w=m_conv_w, m_conv_b=m_conv_b, m_decay_logit=m_decay_logit, m_gn_w=m_gn_w, m_w_a=m_w_a, m_w_b=m_w_b, m_w_out=m_w_out, m_final_norm_w=m_final_norm_w, v_c_ctx=v_c_ctx, v_norm_w=v_norm_w, v_ada_w=v_ada_w, v_ada_b=v_ada_b, v_w_in=v_w_in, v_conv_w=v_conv_w, v_conv_b=v_conv_b, v_decay_logit=v_decay_logit, v_gn_w=v_gn_w, v_w_a=v_w_a, v_w_b=v_w_b, v_w_out=v_w_out, v_final_norm_w=v_final_norm_w)
    weights = {n: given[n] for n in TWIN_WEIGHTS}
    shared = {n: given[n] for n in SHARED_INPUTS}
    per_example = {n: given[n] for n in ['x', 'c', 'ctx']}
    grad_fn = _jax.value_and_grad(_loss, argnums=(0, 1))

    def one_microbatch(ex, loss_target):
        ex = dict(ex)
        diff = ex.pop(TWIN_DIFF_INPUT)
        return grad_fn(weights, diff, {**shared, **ex}, loss_target)

    if N_MICROBATCH == 1:
        loss, (grad_w, grad_x) = one_microbatch(per_example, given["loss_target"])
    else:
        def body(carry, xs):
            loss_sum, grad_sum = carry
            l_k, (gw_k, gx_k) = one_microbatch(xs[0], xs[1])
            with _jax.named_scope("update"):
                return (loss_sum + l_k, _jax.tree.map(_jnp.add, grad_sum, gw_k)), gx_k

        init = (_jnp.zeros((), _jnp.float32), _jax.tree.map(_jnp.zeros_like, weights))
        (loss, grad_w), grad_x = _jax.lax.scan(body, init, (per_example, given["loss_target"]))
    with _jax.named_scope("update"):
        delta_w, new_m, new_v = {}, {}, {}
        for n in TWIN_WEIGHTS:
            delta_w[n], new_m[n], new_v[n] = _adamw(weights[n], grad_w[n], given["m_" + n], given["v_" + n])
    return (loss, grad_x, *[grad_w[n] for n in TWIN_WEIGHTS], *[delta_w[n] for n in TWIN_WEIGHTS],
            *[new_m[n] for n in TWIN_WEIGHTS], *[new_v[n] for n in TWIN_WEIGHTS])
```

```python
import functools
import math

import jax
import jax.numpy as jnp
from jax import lax
from jax.experimental import pallas as pl
from jax.experimental.pallas import tpu as pltpu

F32 = jnp.float32
BF16 = jnp.bfloat16
MESH = pl.DeviceIdType.MESH

N_DEV = 8
D = 1024
N_HEADS = 8
DK = 64
DV = 128
GRID_W = 64
ROPE_BASE = 10000.0
EPS = 1e-6
PW = 9 * D
WSH = PW // N_DEV
RSH = D // N_DEV
ADA_SH = 3 * D // N_DEV
TM = 256
RET_C = 256
HALO = 16
LANES = 128
VMEM_LIMIT = 60 * 1024 * 1024

ADAM_LR = 0.001
ADAM_B1 = 0.9
ADAM_B2 = 0.999
ADAM_EPS = 1e-08
ADAM_WD = 0.01
ADAM_STEP = 10

CB_H, CB_BG, CB_CG, CB_ZA, CB_QK, CB_V, CB_ZB, CB_GA, CB_GB = range(9)


def _cparams(**kw):
    return pltpu.CompilerParams(vmem_limit_bytes=VMEM_LIMIT, **kw)


def _dot(a, b):
    return jnp.dot(a, b, preferred_element_type=F32)


def _dot_nt(a, b):
    return lax.dot_general(a, b, (((1,), (1,)), ((), ())), preferred_element_type=F32)


def _dot_tn(a, b):
    return lax.dot_general(a, b, (((0,), (0,)), ((), ())), preferred_element_type=F32)


def _sigmoid(z):
    return 1.0 / (1.0 + jnp.exp(-z))


def _row_tile(rows):
    return TM if rows % TM == 0 else rows


def _coords():
    return lax.axis_index("x"), lax.axis_index("y"), lax.axis_index("c")


def _flip(v, bit):
    return 1 - v if bit else v


def _all_gather_small(blk, name):
    rows, cols = blk.shape

    def body(x_ref, out_ref, send_sems, recv_sems, local_sem):
        x, y, c = _coords()
        me = 4 * x + 2 * y + c
        mine = pltpu.make_async_copy(x_ref, out_ref.at[me], local_sem)
        mine.start()

        def copy(k, slot):
            peer = (_flip(x, k & 4), _flip(y, k & 2), _flip(c, k & 1))
            return pltpu.make_async_remote_copy(
                src_ref=x_ref, dst_ref=out_ref.at[slot], send_sem=send_sems.at[k - 1],
                recv_sem=recv_sems.at[k - 1], device_id=peer, device_id_type=MESH)

        for k in range(1, N_DEV):
            copy(k, me).start()
        for k in range(1, N_DEV):
            copy(k, jnp.bitwise_xor(me, k)).wait_recv()
        for k in range(1, N_DEV):
            copy(k, me).wait_send()
        mine.wait()

    return pl.pallas_call(
        body, name=name,
        out_shape=jax.ShapeDtypeStruct((N_DEV, rows, cols), blk.dtype),
        in_specs=[pl.BlockSpec(memory_space=pltpu.VMEM)],
        out_specs=pl.BlockSpec(memory_space=pltpu.VMEM),
        scratch_shapes=[pltpu.SemaphoreType.DMA((N_DEV - 1,)), pltpu.SemaphoreType.DMA((N_DEV - 1,)),
                        pltpu.SemaphoreType.DMA],
    )(blk)


def _all_gather_weights(shards):
    n = len(shards)

    def body(*refs):
        srcs, outs = refs[:n], refs[n:2 * n]
        send_sems, recv_sems, local_sems = refs[2 * n:]
        x, y, c = _coords()
        me = 4 * x + 2 * y + c
        sib = (x, y, 1 - c)
        chips = [(1 - x, y), (x, 1 - y), (1 - x, 1 - y)]

        def slot(px, py, pc):
            return 4 * px + 2 * py + pc

        def copy(a, k, block_slot, to, own=False):
            return pltpu.make_async_remote_copy(
                src_ref=srcs[a] if own else outs[a].at[block_slot], dst_ref=outs[a].at[block_slot],
                send_sem=send_sems.at[7 * a + k], recv_sem=recv_sems.at[7 * a + k],
                device_id=to, device_id_type=MESH)

        locals_ = [pltpu.make_async_copy(srcs[a], outs[a].at[me], local_sems.at[a]) for a in range(n)]
        for cp in locals_:
            cp.start()
        sent = []
        for a in range(n):
            sent.append(copy(a, 0, me, sib, own=True))
            for j, chip in enumerate(chips):
                sent.append(copy(a, 1 + j, me, (*chip, c), own=True))
        for cp in sent:
            cp.start()
        for j, chip in enumerate(chips):
            for a in range(n):
                copy(a, 1 + j, slot(*chip, c), sib).wait_recv()
                fwd = copy(a, 4 + j, slot(*chip, c), sib)
                fwd.start()
                sent.append(fwd)
        for a in range(n):
            copy(a, 0, slot(x, y, 1 - c), sib).wait_recv()
            for j, chip in enumerate(chips):
                copy(a, 4 + j, slot(*chip, 1 - c), sib).wait_recv()
        for cp in sent:
            cp.wait_send()
        for cp in locals_:
            cp.wait()

    anyspec = pl.BlockSpec(memory_space=pl.ANY)
    return pl.pallas_call(
        body, name="all_gather_weights",
        out_shape=[jax.ShapeDtypeStruct((N_DEV,) + s.shape, s.dtype) for s in shards],
        in_specs=[anyspec] * n, out_specs=[anyspec] * n,
        scratch_shapes=[pltpu.SemaphoreType.DMA((7 * n,)), pltpu.SemaphoreType.DMA((7 * n,)),
                        pltpu.SemaphoreType.DMA((n,))],
    )(*shards)


def _reduce_scatter_pair(grads):
    n = len(grads)

    def body(*refs):
        srcs, outs = refs[:n], refs[n:2 * n]
        send_sems, recv_sems = refs[2 * n:]
        x, y, c = _coords()
        chip = 2 * x + y
        sib = (x, y, 1 - c)
        copies = []
        for a in range(n):
            for j in range(4):
                owner = 2 * jnp.bitwise_xor(chip, j) + (1 - c)
                copies.append(pltpu.make_async_remote_copy(
                    src_ref=srcs[a].at[owner], dst_ref=outs[a].at[j],
                    send_sem=send_sems.at[4 * a + j], recv_sem=recv_sems.at[4 * a + j],
                    device_id=sib, device_id_type=MESH))
        for cp in copies:
            cp.start()
        for cp in copies:
            cp.wait_recv()
        for cp in copies:
            cp.wait_send()

    anyspec = pl.BlockSpec(memory_space=pl.ANY)
    return pl.pallas_call(
        body, name="reduce_scatter_pair",
        out_shape=[jax.ShapeDtypeStruct((4,) + g.shape[1:], g.dtype) for g in grads],
        in_specs=[anyspec] * n, out_specs=[anyspec] * n,
        scratch_shapes=[pltpu.SemaphoreType.DMA((4 * n,)), pltpu.SemaphoreType.DMA((4 * n,))],
    )(*grads)


def _reduce_scatter_chips(parts):
    n = len(parts)

    def body(*refs):
        srcs, outs = refs[:n], refs[n:2 * n]
        send_sems, recv_sems = refs[2 * n:]
        x, y, c = _coords()
        copies = []
        for a in range(n):
            for j in range(1, 4):
                peer = (_flip(x, j & 2), _flip(y, j & 1), c)
                copies.append(pltpu.make_async_remote_copy(
                    src_ref=srcs[a].at[j], dst_ref=outs[a].at[j - 1],
                    send_sem=send_sems.at[3 * a + j - 1], recv_sem=recv_sems.at[3 * a + j - 1],
                    device_id=peer, device_id_type=MESH))
        for cp in copies:
            cp.start()
        for cp in copies:
            cp.wait_recv()
        for cp in copies:
            cp.wait_send()

    anyspec = pl.BlockSpec(memory_space=pl.ANY)
    return pl.pallas_call(
        body, name="reduce_scatter_chips",
        out_shape=[jax.ShapeDtypeStruct((3,) + p.shape[1:], p.dtype) for p in parts],
        in_specs=[anyspec] * n, out_specs=[anyspec] * n,
        scratch_shapes=[pltpu.SemaphoreType.DMA((3 * n,)), pltpu.SemaphoreType.DMA((3 * n,))],
    )(*parts)


def _pair_sum(grad, recv, owners, name):
    _, rows, cols = grad.shape
    tr = _row_tile(rows)

    def body(own_ref, g_ref, r_ref, o_ref):
        o_ref[...] = g_ref[...] + r_ref[...]

    grid_spec = pltpu.PrefetchScalarGridSpec(
        num_scalar_prefetch=1, grid=(4, rows // tr),
        in_specs=[pl.BlockSpec((1, tr, cols), lambda j, i, own: (own[j], i, 0)),
                  pl.BlockSpec((1, tr, cols), lambda j, i, own: (j, i, 0))],
        out_specs=pl.BlockSpec((1, tr, cols), lambda j, i, own: (j, i, 0)))
    return pl.pallas_call(
        body, name=name, grid_spec=grid_spec,
        out_shape=jax.ShapeDtypeStruct((4, rows, cols), F32),
        compiler_params=_cparams(dimension_semantics=("arbitrary", "arbitrary")),
    )(owners, grad, recv)


def _modulation(c16, ada_w_sh, ada_b_sh, decay_logit):
    def body(c_ref, w_ref, b_ref, dl_ref, mod_ref, act_ref, lg_ref):
        cv = c_ref[...]
        act = cv * _sigmoid(cv)
        act_ref[...] = act
        mod_ref[...] = jnp.dot(act, w_ref[...], preferred_element_type=F32,
                               precision=lax.Precision.HIGHEST) + b_ref[...]
        z = dl_ref[...]
        lg_ref[...] = jnp.minimum(z, 0.0) - jnp.log(1.0 + jnp.exp(-jnp.abs(z)))

    return pl.pallas_call(
        body, name="modulation",
        out_shape=[jax.ShapeDtypeStruct((16, ADA_SH), F32), jax.ShapeDtypeStruct((16, D), F32),
                   jax.ShapeDtypeStruct(decay_logit.shape, F32)],
        compiler_params=_cparams(),
    )(c16, ada_w_sh, ada_b_sh, decay_logit)


def _adam_update(w, g, m, v):
    m2 = ADAM_B1 * m + (1.0 - ADAM_B1) * g
    v2 = ADAM_B2 * v + (1.0 - ADAM_B2) * (g * g)
    m_hat = m2 / (1.0 - ADAM_B1 ** ADAM_STEP)
    v_hat = v2 / (1.0 - ADAM_B2 ** ADAM_STEP)
    delta = -ADAM_LR * (m_hat / (jnp.sqrt(v_hat) + ADAM_EPS) + ADAM_WD * w)
    return delta, m2, v2


def _adam_sharded(w, m, v, own, recv, name):
    rows, cols = w.shape
    tr = _row_tile(rows)

    def body(w_ref, m_ref, v_ref, g0, g1, g2, g3, g_ref, d_ref, m_out, v_out):
        g = ((g0[0] + g1[0]) + g2[0]) + g3[0]
        delta, m2, v2 = _adam_update(w_ref[...], g, m_ref[...], v_ref[...])
        g_ref[...] = g
        d_ref[...] = delta
        m_out[...] = m2
        v_out[...] = v2

    flat = pl.BlockSpec((tr, cols), lambda i: (i, 0))
    part = lambda j: pl.BlockSpec((1, tr, cols), lambda i, j=j: (j, i, 0))
    return pl.pallas_call(
        body, name=name, grid=(rows // tr,),
        in_specs=[flat, flat, flat, part(0), part(0), part(1), part(2)],
        out_specs=[flat] * 4,
        out_shape=[jax.ShapeDtypeStruct((rows, cols), F32)] * 4,
        compiler_params=_cparams(dimension_semantics=("arbitrary",)),
    )(w, m, v, own, recv, recv, recv)


def _ada_backward(act_t, dmod16, dmodc8, ada_w_sh, m, v):
    def body(at_ref, dm_ref, dc_ref, w_ref, m_ref, v_ref, g_ref, d_ref, m_out, v_out, pc_ref):
        g = jnp.dot(at_ref[...], dm_ref[...], preferred_element_type=F32, precision=lax.Precision.HIGHEST)
        w = w_ref[...]
        delta, m2, v2 = _adam_update(w, g, m_ref[...], v_ref[...])
        g_ref[...] = g
        d_ref[...] = delta
        m_out[...] = m2
        v_out[...] = v2
        pc_ref[...] = lax.dot_general(dc_ref[...], w, (((1,), (1,)), ((), ())), preferred_element_type=F32,
                                      precision=lax.Precision.HIGHEST)

    return pl.pallas_call(
        body, name="ada_backward",
        out_shape=[jax.ShapeDtypeStruct((D, ADA_SH), F32)] * 4 + [jax.ShapeDtypeStruct((8, D), F32)],
        compiler_params=_cparams(),
    )(act_t, dmod16, dmodc8, ada_w_sh, m, v)


def _sum_devices(gathered, name):
    _, rows, cols = gathered.shape

    def body(g_ref, o_ref):
        acc = g_ref[0]
        for d in range(1, N_DEV):
            acc = acc + g_ref[d]
        o_ref[...] = acc

    return pl.pallas_call(body, name=name, out_shape=jax.ShapeDtypeStruct((rows, cols), F32),
                          compiler_params=_cparams())(gathered)


def _adam_small(w, g_raw, g_raw2, m, v, silu_row, sigm_row):
    rows, cols = w.shape

    def body(w_ref, g_ref, g2_ref, m_ref, v_ref, g_out, d_ref, m_out, v_out):
        wv = w_ref[...]
        row = lax.broadcasted_iota(jnp.int32, (rows, cols), 0)
        s = _sigmoid(wv)
        dsilu = s * (1.0 + wv * (1.0 - s))
        scale = jnp.where(row == silu_row, dsilu, jnp.where(row == sigm_row, 1.0 - s, 1.0))
        g = (g_ref[...] + g2_ref[...]) * scale
        delta, m2, v2 = _adam_update(wv, g, m_ref[...], v_ref[...])
        g_out[...] = g
        d_ref[...] = delta
        m_out[...] = m2
        v_out[...] = v2

    return pl.pallas_call(body, name="adam_small", out_shape=[jax.ShapeDtypeStruct((rows, cols), F32)] * 4,
                          compiler_params=_cparams())(w, g_raw, g_raw2, m, v)


def _rope(t, cos, sin):
    lane = lax.broadcasted_iota(jnp.int32, (1, LANES), 1)
    first_half = jnp.bitwise_and(lane, DK // 2) == 0
    partner = jnp.where(first_half, pltpu.roll(t, LANES - DK // 2, 1), pltpu.roll(t, DK // 2, 1))
    return t * cos + partner * sin


def _in_projection(x_ext, modx, modc, norm_w, w_in, cos_t, sin_t, n_x_tiles):
    lext = x_ext.shape[0]
    q_lo, k_lo = 4 * D, 4 * D + N_HEADS * DK

    def body(x_ref, mx_ref, mc_ref, nw_ref, w_ref, cos_ref, sin_ref, p_ref, xmt_ref):
        is_ctx = pl.program_id(0) >= n_x_tiles
        x = x_ref[...]
        r = lax.rsqrt(jnp.mean(x * x, axis=-1, keepdims=True) + EPS)
        sh = jnp.where(is_ctx, mc_ref[0:1, :], mx_ref[0:1, :])
        sc = jnp.where(is_ctx, mc_ref[1:2, :], mx_ref[1:2, :])
        xm = (x * r * nw_ref[...]) * (1.0 + sc) + sh
        xmb = xm.astype(BF16)
        xmt_ref[...] = xm.T.astype(BF16)
        cos, sin = cos_ref[...], sin_ref[...]
        for d in range(N_DEV):
            acc = _dot(xmb, w_ref[d])
            lo = d * WSH
            col = 0
            while col < WSH:
                g = lo + col
                if q_lo <= g < k_lo + N_HEADS * DK:
                    piece = acc[:, col:col + LANES]
                    if g >= k_lo:
                        piece = piece * (DK ** -0.5)
                    p_ref[:, g:g + LANES] = _rope(piece, cos, sin).astype(BF16)
                    col += LANES
                else:
                    if g < q_lo:
                        end = min(WSH, q_lo - lo)
                    else:
                        end = WSH
                    p_ref[:, g:lo + end] = acc[:, col:end].astype(BF16)
                    col = end

    row = lambda w: pl.BlockSpec((TM, w), lambda i: (i, 0))
    full = lambda a: pl.BlockSpec(a.shape, lambda i: (0,) * a.ndim)
    return pl.pallas_call(
        body, name="in_projection", grid=(lext // TM,),
        in_specs=[row(D), full(modx), full(modc), full(norm_w), full(w_in), row(LANES), row(LANES)],
        out_specs=[row(PW), pl.BlockSpec((D, TM), lambda i: (0, i))],
        out_shape=[jax.ShapeDtypeStruct((lext, PW), BF16), jax.ShapeDtypeStruct((D, lext), BF16)],
        compiler_params=_cparams(dimension_semantics=("arbitrary",)),
    )(x_ext, modx, modc, norm_w, w_in, cos_t, sin_t)


def _decay_tables(lgf, lgb, n):
    i = lax.broadcasted_iota(jnp.int32, (n, 1), 0).astype(F32)
    return dict(i=i, k_f=jnp.exp(lgf * (n - 1.0 - i)), k_b=jnp.exp(lgb * i),
                q_f=jnp.exp(lgf * (i + 1.0)), q_b=jnp.exp(lgb * (n - i)))


def _decay_matrix(lgf, lgb, n, transposed=False):
    ii = lax.broadcasted_iota(jnp.int32, (n, n), 0)
    jj = lax.broadcasted_iota(jnp.int32, (n, n), 1)
    diff = (jj - ii if transposed else ii - jj).astype(F32)
    low = jnp.exp(lgf * jnp.maximum(diff, 0.0))
    up = jnp.exp(lgb * jnp.maximum(-diff, 0.0))
    return jnp.where(diff > 0, low, jnp.where(diff < 0, up, 2.0)), diff


def _cat_lanes(a, b):
    return jnp.concatenate([a.astype(BF16), b.astype(BF16)], axis=1)


def _retention_forward(p_ext, lg, seq, ctx_len):
    lext = seq + ctx_len
    n_chunks = seq // RET_C
    C = RET_C

    def body(lg_ref, q_ref, k_ref, v_ref, rn_ref, rstd_ref, kv_scr, sf_scr, sb_scr):
        pair = pl.program_id(0)
        lane = lax.broadcasted_iota(jnp.int32, (1, LANES), 1)
        for hh in range(2):
            hmask = (lane // DK == hh).astype(F32)
            lgf = lg_ref[0, 2 * pair + hh]
            lgb = lg_ref[1, 2 * pair + hh]
            vs = slice(hh * DV, (hh + 1) * DV)
            tc = _decay_tables(lgf, lgb, ctx_len)
            kc = k_ref[seq:lext, :].astype(F32) * hmask
            s0 = _dot_tn(_cat_lanes(kc * tc["k_f"], kc * tc["k_b"]), v_ref[seq:lext, vs])
            t = _decay_tables(lgf, lgb, C)

            def increments(c, carry):
                rows = pl.ds(pl.multiple_of(c * C, C), C)
                k = k_ref[rows, :].astype(F32) * hmask
                kv_scr[c] = _dot_tn(_cat_lanes(k * t["k_f"], k * t["k_b"]), v_ref[rows, vs])
                return carry

            lax.fori_loop(0, n_chunks, increments, 0)
            gf_c, gb_c = jnp.exp(lgf * C), jnp.exp(lgb * C)

            def scan_f(c, s):
                sf_scr[c] = s
                return gf_c * s + kv_scr[c, 0:LANES, :]

            def scan_b(n, s):
                c = n_chunks - 1 - n
                sb_scr[c] = s
                return gb_c * s + kv_scr[c, LANES:2 * LANES, :]

            lax.fori_loop(0, n_chunks, scan_f, s0[0:LANES])
            lax.fori_loop(0, n_chunks, scan_b, s0[LANES:2 * LANES])
            dmat, _ = _decay_matrix(lgf, lgb, C)

            def outputs(c, carry):
                rows = pl.ds(pl.multiple_of(c * C, C), C)
                q = q_ref[rows, :].astype(F32) * hmask
                v = v_ref[rows, vs]
                s = _dot_nt(q.astype(BF16), k_ref[rows, :])
                inner = _dot((s * dmat).astype(BF16), v)
                states = jnp.concatenate([sf_scr[c], sb_scr[c]], axis=0).astype(BF16)
                ret = inner + _dot(_cat_lanes(q * t["q_f"], q * t["q_b"]), states)
                mu = jnp.mean(ret, axis=-1, keepdims=True)
                cen = ret - mu
                rstd = lax.rsqrt(jnp.mean(cen * cen, axis=-1, keepdims=True) + EPS)
                rn_ref[rows, vs] = cen * rstd
                rstd_ref[rows, vs] = jnp.broadcast_to(rstd, (C, DV))
                return carry

            lax.fori_loop(0, n_chunks, outputs, 0)

    qk0 = CB_QK * D // LANES
    return pl.pallas_call(
        body, name="retention_forward", grid=(N_HEADS // 2,),
        in_specs=[pl.BlockSpec(memory_space=pltpu.SMEM),
                  pl.BlockSpec((lext, LANES), lambda g: (0, qk0 + g)),
                  pl.BlockSpec((lext, LANES), lambda g: (0, qk0 + N_HEADS // 2 + g)),
                  pl.BlockSpec((lext, 2 * DV), lambda g: (0, CB_V * D // (2 * DV) + g))],
        out_specs=[pl.BlockSpec((seq, 2 * DV), lambda g: (0, g))] * 2,
        out_shape=[jax.ShapeDtypeStruct((seq, D), F32)] * 2,
        scratch_shapes=[pltpu.VMEM((n_chunks, 2 * LANES, DV), F32), pltpu.VMEM((n_chunks, LANES, DV), F32),
                        pltpu.VMEM((n_chunks, LANES, DV), F32)],
        compiler_params=_cparams(dimension_semantics=("arbitrary",)),
    )(lg, p_ext, p_ext, p_ext)


def _retention_backward(p_ext, dret, lg, cos_t, sin_t, seq, ctx_len):
    lext = seq + ctx_len
    n_chunks = seq // RET_C
    C = RET_C

    def body(lg_ref, q_ref, k_ref, v_ref, do_ref, cos_ref, sin_ref, dq_ref, dk_ref, dv_ref, dlg_ref,
             kv_scr, g_scr, sf_scr, sb_scr, gfn_scr, gbp_scr):
        pair = pl.program_id(0)
        lane = lax.broadcasted_iota(jnp.int32, (1, LANES), 1)
        dlg = []
        for hh in range(2):
            in_head = lane // DK == hh
            hmask = in_head.astype(F32)
            lgf = lg_ref[0, 2 * pair + hh]
            lgb = lg_ref[1, 2 * pair + hh]
            vs = slice(hh * DV, (hh + 1) * DV)
            tc = _decay_tables(lgf, lgb, ctx_len)
            kc = k_ref[seq:lext, :].astype(F32) * hmask
            vc = v_ref[seq:lext, vs]
            kc_cat = _cat_lanes(kc * tc["k_f"], kc * tc["k_b"])
            s0 = _dot_tn(kc_cat, vc)
            t = _decay_tables(lgf, lgb, C)

            def increments(c, carry):
                rows = pl.ds(pl.multiple_of(c * C, C), C)
                k = k_ref[rows, :].astype(F32) * hmask
                q = q_ref[rows, :].astype(F32) * hmask
                kv_scr[c] = _dot_tn(_cat_lanes(k * t["k_f"], k * t["k_b"]), v_ref[rows, vs])
                g_scr[c] = _dot_tn(_cat_lanes(q * t["q_f"], q * t["q_b"]), do_ref[rows, vs])
                return carry

            lax.fori_loop(0, n_chunks, increments, 0)
            gf_c, gb_c = jnp.exp(lgf * C), jnp.exp(lgb * C)

            def scan_f(c, s):
                sf_scr[c] = s
                return gf_c * s + kv_scr[c, 0:LANES, :]

            def scan_b(n, s):
                c = n_chunks - 1 - n
                sb_scr[c] = s
                return gb_c * s + kv_scr[c, LANES:2 * LANES, :]

            def scan_gf(n, carry):
                c = n_chunks - 1 - n
                gfn_scr[c] = carry
                return g_scr[c, 0:LANES, :] + gf_c * carry

            def scan_gb(c, carry):
                gbp_scr[c] = carry
                return g_scr[c, LANES:2 * LANES, :] + gb_c * carry

            lax.fori_loop(0, n_chunks, scan_f, s0[0:LANES])
            lax.fori_loop(0, n_chunks, scan_b, s0[LANES:2 * LANES])
            zero_state = jnp.zeros((LANES, DV), F32)
            gf_first = lax.fori_loop(0, n_chunks, scan_gf, zero_state)
            gb_last = lax.fori_loop(0, n_chunks, scan_gb, zero_state)

            dmat, diff = _decay_matrix(lgf, lgb, C)
            dmat_t, _ = _decay_matrix(lgf, lgb, C, transposed=True)
            w_f = jnp.where(diff > 0, diff * dmat, 0.0)
            w_b = jnp.where(diff < 0, -diff * dmat, 0.0)
            cos_all, sin_all = cos_ref, sin_ref

            def total(a):
                return jnp.sum(jnp.sum(a, axis=1, keepdims=True), axis=0, keepdims=True)

            def chunk(c, carry):
                dlgf, dlgb = carry
                rows = pl.ds(pl.multiple_of(c * C, C), C)
                qm = q_ref[rows, :].astype(F32) * hmask
                km = k_ref[rows, :].astype(F32) * hmask
                qb, kb = qm.astype(BF16), km.astype(BF16)
                v = v_ref[rows, vs]
                do = do_ref[rows, vs]
                s = _dot_nt(qb, kb)
                s_t = _dot_nt(kb, qb)
                dsd = _dot_nt(do, v)
                dsd_t = _dot_nt(v, do)
                dq_in = _dot((dsd * dmat).astype(BF16), kb)
                dk_in = _dot((dsd_t * dmat_t).astype(BF16), qb)
                dv_in = _dot((s_t * dmat_t).astype(BF16), do)
                prod = s * dsd
                dlgf = dlgf + total(prod * w_f)
                dlgb = dlgb + total(prod * w_b)
                sf, sb = sf_scr[c], sb_scr[c]
                states = jnp.concatenate([sf, sb], axis=0).astype(BF16)
                dqc = _dot_nt(do, states)
                dqf = dqc[:, 0:LANES] * t["q_f"]
                dqb = dqc[:, LANES:2 * LANES] * t["q_b"]
                dq = (dq_in + dqf + dqb) * hmask
                dlgf = dlgf + total((t["i"] + 1.0) * (qm * dqf))
                dlgb = dlgb + total((C - t["i"]) * (qm * dqb))
                gfn, gbp = gfn_scr[c], gbp_scr[c]
                gstates = jnp.concatenate([gfn, gbp], axis=0).astype(BF16)
                dkc = _dot_nt(v, gstates)
                dkf = dkc[:, 0:LANES] * t["k_f"]
                dkb = dkc[:, LANES:2 * LANES] * t["k_b"]
                dk = dk_in + dkf + dkb
                dlgf = dlgf + total((C - 1.0 - t["i"]) * (km * dkf)) + C * gf_c * total(gfn * sf)
                dlgb = dlgb + total(t["i"] * (km * dkb)) + C * gb_c * total(gbp * sb)
                dv = dv_in + _dot(_cat_lanes(km * t["k_f"], km * t["k_b"]), gstates)
                cos, sin = cos_all[rows, :], sin_all[rows, :]
                dq = _rope(dq, cos, -sin)
                dk = _rope(dk, cos, -sin) * (DK ** -0.5)
                if hh == 0:
                    dq_ref[rows, :] = dq.astype(BF16)
                    dk_ref[rows, :] = dk.astype(BF16)
                else:
                    dq_ref[rows, :] = jnp.where(in_head, dq, dq_ref[rows, :].astype(F32)).astype(BF16)
                    dk_ref[rows, :] = jnp.where(in_head, dk, dk_ref[rows, :].astype(F32)).astype(BF16)
                dv_ref[rows, vs] = dv.astype(BF16)
                return dlgf, dlgb

            zero = jnp.zeros((1, 1), F32)
            dlgf, dlgb = lax.fori_loop(0, n_chunks, chunk, (zero, zero))

            g0 = jnp.concatenate([gf_first, gb_last], axis=0).astype(BF16)
            dkcc = _dot_nt(vc, g0)
            dkcf = dkcc[:, 0:LANES] * tc["k_f"]
            dkcb = dkcc[:, LANES:2 * LANES] * tc["k_b"]
            dlgf = dlgf + total((ctx_len - 1.0 - tc["i"]) * (kc * dkcf))
            dlgb = dlgb + total(tc["i"] * (kc * dkcb))
            dkctx = (dkcf + dkcb) * (DK ** -0.5)
            if hh == 0:
                dk_ref[seq:lext, :] = dkctx.astype(BF16)
                dq_ref[seq:lext, :] = jnp.zeros((ctx_len, LANES), BF16)
            else:
                dk_ref[seq:lext, :] = jnp.where(in_head, dkctx, dk_ref[seq:lext, :].astype(F32)).astype(BF16)
            dv_ref[seq:lext, vs] = _dot(kc_cat, g0).astype(BF16)
            dlg += [dlgf, dlgb]

        lane8 = lax.broadcasted_iota(jnp.int32, (8, LANES), 1)
        out = jnp.zeros((8, LANES), F32)
        for n, val in enumerate(dlg):
            out = jnp.where(lane8 == n, val, out)
        dlg_ref[0] = out

    qk0 = CB_QK * D // LANES
    q_spec = pl.BlockSpec((lext, LANES), lambda g: (0, qk0 + g))
    k_spec = pl.BlockSpec((lext, LANES), lambda g: (0, qk0 + N_HEADS // 2 + g))
    v_spec = pl.BlockSpec((lext, 2 * DV), lambda g: (0, CB_V * D // (2 * DV) + g))
    table = pl.BlockSpec((lext, LANES), lambda g: (0, 0))
    state = pltpu.VMEM((n_chunks, LANES, DV), F32)
    return pl.pallas_call(
        body, name="retention_backward", grid=(N_HEADS // 2,),
        in_specs=[pl.BlockSpec(memory_space=pltpu.SMEM), q_spec, k_spec, v_spec,
                  pl.BlockSpec((seq, 2 * DV), lambda g: (0, g)), table, table],
        out_specs=[pl.BlockSpec((lext, LANES), lambda g: (0, g)), pl.BlockSpec((lext, LANES), lambda g: (0, g)),
                   pl.BlockSpec((lext, 2 * DV), lambda g: (0, g)), pl.BlockSpec((1, 8, LANES), lambda g: (g, 0, 0))],
        out_shape=[jax.ShapeDtypeStruct((lext, N_HEADS * DK), BF16), jax.ShapeDtypeStruct((lext, N_HEADS * DK), BF16),
                   jax.ShapeDtypeStruct((lext, D), BF16), jax.ShapeDtypeStruct((N_HEADS // 2, 8, LANES), F32)],
        scratch_shapes=[pltpu.VMEM((n_chunks, 2 * LANES, DV), F32), pltpu.VMEM((n_chunks, 2 * LANES, DV), F32),
                        state, state, state, state],
        compiler_params=_cparams(dimension_semantics=("arbitrary",)),
    )(lg, p_ext, p_ext, p_ext, dret, cos_t, sin_t)


def _merge(p_ext, rn, rstd, x, target, w_a, w_b, w_out, vecs, seq):
    n_tiles = seq // TM
    hb = TM // HALO

    def body(h_ref, bg_ref, cg_ref, za_ref, zb_ref, ga_ref, gb_ref, hp_ref, hn_ref, cp_ref, cn_ref,
             rn_ref, rstd_ref, x_ref, t_ref, wa_ref, wb_ref, wo_ref, vec_ref,
             dx1_ref, dpa_ref, dconv_ref, dret_ref, at_ref, b_ref, part_ref):
        i = pl.program_id(0)
        f = lambda ref: ref[...].astype(F32)
        h, bg, cg, za, zb, ga, gb = f(h_ref), f(bg_ref), f(cg_ref), f(za_ref), f(zb_ref), f(ga_ref), f(gb_ref)
        gx, w0, w1, w2 = vec_ref[0:1, :], vec_ref[1:2, :], vec_ref[2:3, :], vec_ref[3:4, :]
        cb, gnw, fw = vec_ref[4:5, :], vec_ref[5:6, :], vec_ref[6:7, :]
        u = cg * h
        row = lax.broadcasted_iota(jnp.int32, (TM, 1), 0)
        u_prev = (f(cp_ref) * f(hp_ref))[HALO - 1:HALO, :]
        u_next = (f(cn_ref) * f(hn_ref))[0:1, :]
        u_prev = jnp.where(i == 0, 0.0, u_prev)
        u_next = jnp.where(i == n_tiles - 1, 0.0, u_next)
        u_up = jnp.where(row == 0, u_prev, pltpu.roll(u, 1, 0))
        u_dn = jnp.where(row == TM - 1, u_next, pltpu.roll(u, TM - 1, 0))
        conv = w0 * u_up + w1 * u + w2 * u_dn + cb
        sza = _sigmoid(za)
        silu_za = za * sza
        a_act = silu_za * bg * conv
        rn = rn_ref[...]
        szb = _sigmoid(zb)
        silu_zb = zb * szb
        rg = rn * gnw
        b_act = silu_zb * rg
        y_a = _dot(a_act.astype(BF16), wa_ref[...])
        y_b = _dot(b_act.astype(BF16), wb_ref[...])
        sga, sgb = _sigmoid(ga), _sigmoid(gb)
        mix = sga * y_a + sgb * y_b
        y = _dot(mix.astype(BF16), wo_ref[...])
        x1 = x_ref[...] + gx * y
        r1 = lax.rsqrt(jnp.mean(x1 * x1, axis=-1, keepdims=True) + EPS)
        xh1 = x1 * r1
        err = xh1 * fw - t_ref[...]
        loss = jnp.sum(jnp.sum(err * err, axis=1, keepdims=True), axis=0, keepdims=True) * (0.5 / D)
        dout = err * (1.0 / D)
        dxh = dout * fw
        dx1 = r1 * (dxh - xh1 * jnp.mean(dxh * xh1, axis=-1, keepdims=True))
        dx1_ref[...] = dx1
        dy = (dx1 * gx).astype(BF16)
        dmix = _dot_nt(dy, wo_ref[...])
        dya = (dmix * sga).astype(BF16)
        dyb = (dmix * sgb).astype(BF16)
        da = _dot_nt(dya, wa_ref[...])
        db = _dot_nt(dyb, wb_ref[...])
        dpa_ref[:, 0:D] = (da * silu_za * conv).astype(BF16)
        dpa_ref[:, D:2 * D] = (da * bg * conv * (sza * (1.0 + za * (1.0 - sza)))).astype(BF16)
        dpa_ref[:, 2 * D:3 * D] = (db * rg * (szb * (1.0 + zb * (1.0 - szb)))).astype(BF16)
        dpa_ref[:, 3 * D:4 * D] = (dmix * y_a * sga * (1.0 - sga)).astype(BF16)
        dpa_ref[:, 4 * D:5 * D] = (dmix * y_b * sgb * (1.0 - sgb)).astype(BF16)
        dconv_ref[...] = (da * silu_za * bg).astype(BF16)
        drn_n = db * silu_zb
        drn = drn_n * gnw
        rstd = rstd_ref[...]
        for hd in range(N_HEADS):
            sl = slice(hd * DV, (hd + 1) * DV)
            dh_, rh = drn[:, sl], rn[:, sl]
            m1 = jnp.mean(dh_, axis=-1, keepdims=True)
            m2 = jnp.mean(dh_ * rh, axis=-1, keepdims=True)
            dret_ref[:, sl] = (rstd[:, sl] * (dh_ - m1 - rh * m2)).astype(BF16)
        at_ref[0] = a_act.T.astype(BF16)
        at_ref[1] = b_act.T.astype(BF16)
        at_ref[2] = mix.T.astype(BF16)
        b_ref[0] = dya
        b_ref[1] = dyb
        b_ref[2] = dy

        @pl.when(i == 0)
        def _():
            part_ref[...] = jnp.zeros((8, D), F32)

        part_ref[0:1, :] += jnp.sum(dout * xh1, axis=0, keepdims=True)
        part_ref[1:2, :] += jnp.sum(dx1 * y, axis=0, keepdims=True)
        part_ref[2:3, :] += jnp.sum(drn_n * rn, axis=0, keepdims=True)
        part_ref[3:4, :] += jnp.broadcast_to(loss, (1, D))

    col = lambda cb_: pl.BlockSpec((TM, D), lambda i, cb_=cb_: (i, cb_))
    prev = lambda cb_: pl.BlockSpec((HALO, D), lambda i, cb_=cb_: (jnp.maximum(i * hb - 1, 0), cb_))
    nxt = lambda cb_: pl.BlockSpec((HALO, D), lambda i, cb_=cb_: (jnp.minimum((i + 1) * hb, n_tiles * hb - 1), cb_))
    tile = pl.BlockSpec((TM, D), lambda i: (i, 0))
    full = lambda a: pl.BlockSpec(a.shape, lambda i: (0,) * a.ndim)
    return pl.pallas_call(
        body, name="merge", grid=(n_tiles,),
        in_specs=[col(CB_H), col(CB_BG), col(CB_CG), col(CB_ZA), col(CB_ZB), col(CB_GA), col(CB_GB),
                  prev(CB_H), nxt(CB_H), prev(CB_CG), nxt(CB_CG),
                  tile, tile, tile, tile, full(w_a), full(w_b), full(w_out), full(vecs)],
        out_specs=[tile, pl.BlockSpec((TM, 5 * D), lambda i: (i, 0)), tile, tile,
                   pl.BlockSpec((3, D, TM), lambda i: (0, 0, i)), pl.BlockSpec((3, TM, D), lambda i: (0, i, 0)),
                   pl.BlockSpec((8, D), lambda i: (0, 0))],
        out_shape=[jax.ShapeDtypeStruct((seq, D), F32), jax.ShapeDtypeStruct((seq, 5 * D), BF16),
                   jax.ShapeDtypeStruct((seq, D), BF16), jax.ShapeDtypeStruct((seq, D), BF16),
                   jax.ShapeDtypeStruct((3, D, seq), BF16), jax.ShapeDtypeStruct((3, seq, D), BF16),
                   jax.ShapeDtypeStruct((8, D), F32)],
        compiler_params=_cparams(dimension_semantics=("arbitrary",)),
    )(p_ext, p_ext, p_ext, p_ext, p_ext, p_ext, p_ext, p_ext, p_ext, p_ext, p_ext,
      rn, rstd, x, target, w_a, w_b, w_out, vecs)


def _conv_backward(p_ext, dconv, vecs, seq):
    n_tiles = seq // TM
    hb = TM // HALO

    def body(h_ref, cg_ref, dc_ref, dcp_ref, dcn_ref, vec_ref, dpb_ref, part_ref):
        i = pl.program_id(0)
        f = lambda ref: ref[...].astype(F32)
        h, cg, dc = f(h_ref), f(cg_ref), f(dc_ref)
        w0, w1, w2 = vec_ref[1:2, :], vec_ref[2:3, :], vec_ref[3:4, :]
        row = lax.broadcasted_iota(jnp.int32, (TM, 1), 0)
        dc_prev = jnp.where(i == 0, 0.0, f(dcp_ref)[HALO - 1:HALO, :])
        dc_next = jnp.where(i == n_tiles - 1, 0.0, f(dcn_ref)[0:1, :])
        dc_up = jnp.where(row == 0, dc_prev, pltpu.roll(dc, 1, 0))
        dc_dn = jnp.where(row == TM - 1, dc_next, pltpu.roll(dc, TM - 1, 0))
        du = w0 * dc_dn + w1 * dc + w2 * dc_up
        u = cg * h
        dpb_ref[:, 0:D] = (du * cg).astype(BF16)
        dpb_ref[:, D:2 * D] = (du * h).astype(BF16)

        @pl.when(i == 0)
        def _():
            part_ref[...] = jnp.zeros((8, D), F32)

        part_ref[0:1, :] += jnp.sum(u * dc_dn, axis=0, keepdims=True)
        part_ref[1:2, :] += jnp.sum(u * dc, axis=0, keepdims=True)
        part_ref[2:3, :] += jnp.sum(u * dc_up, axis=0, keepdims=True)
        part_ref[3:4, :] += jnp.sum(dc, axis=0, keepdims=True)

    col = lambda cb_: pl.BlockSpec((TM, D), lambda i, cb_=cb_: (i, cb_))
    tile = pl.BlockSpec((TM, D), lambda i: (i, 0))
    return pl.pallas_call(
        body, name="conv_backward", grid=(n_tiles,),
        in_specs=[col(CB_H), col(CB_CG), tile,
                  pl.BlockSpec((HALO, D), lambda i: (jnp.maximum(i * hb - 1, 0), 0)),
                  pl.BlockSpec((HALO, D), lambda i: (jnp.minimum((i + 1) * hb, n_tiles * hb - 1), 0)),
                  pl.BlockSpec(vecs.shape, lambda i: (0, 0))],
        out_specs=[pl.BlockSpec((TM, 2 * D), lambda i: (i, 0)), pl.BlockSpec((8, D), lambda i: (0, 0))],
        out_shape=[jax.ShapeDtypeStruct((seq, 2 * D), BF16), jax.ShapeDtypeStruct((8, D), F32)],
        compiler_params=_cparams(dimension_semantics=("arbitrary",)),
    )(p_ext, p_ext, dconv, dconv, dconv, vecs)


def _input_backward(dpa, dpb, dq, dk, dv, w_in, x_ext, dx1, modx, modc, norm_w, seq):
    lext = x_ext.shape[0]
    n_x = seq // TM

    def body(dpa_ref, dpb_ref, dq_ref, dk_ref, dv_ref, w_ref, x_ref, dx1_ref, mx_ref, mc_ref, nw_ref,
             dp_ref, gx_ref, part_ref):
        i = pl.program_id(0)
        is_ctx = i >= n_x
        qk = N_HEADS * DK

        @pl.when(jnp.logical_not(is_ctx))
        def _():
            dp_ref[:, CB_H * D:(CB_H + 1) * D] = dpb_ref[:, 0:D]
            dp_ref[:, CB_BG * D:(CB_BG + 1) * D] = dpa_ref[:, 0:D]
            dp_ref[:, CB_CG * D:(CB_CG + 1) * D] = dpb_ref[:, D:2 * D]
            dp_ref[:, CB_ZA * D:(CB_ZA + 1) * D] = dpa_ref[:, D:2 * D]
            dp_ref[:, CB_ZB * D:(CB_GB + 1) * D] = dpa_ref[:, 2 * D:5 * D]

        @pl.when(is_ctx)
        def _():
            dp_ref[:, 0:CB_QK * D] = jnp.zeros((TM, CB_QK * D), BF16)
            dp_ref[:, CB_ZB * D:PW] = jnp.zeros((TM, PW - CB_ZB * D), BF16)

        dp_ref[:, CB_QK * D:CB_QK * D + qk] = dq_ref[...]
        dp_ref[:, CB_QK * D + qk:CB_V * D] = dk_ref[...]
        dp_ref[:, CB_V * D:CB_ZB * D] = dv_ref[...]
        dxm = _dot_nt(dp_ref[:, 0:WSH], w_ref[0])
        for d in range(1, N_DEV):
            dxm = dxm + _dot_nt(dp_ref[:, d * WSH:(d + 1) * WSH], w_ref[d])
        x = x_ref[...]
        r = lax.rsqrt(jnp.mean(x * x, axis=-1, keepdims=True) + EPS)
        xh = x * r
        nw = nw_ref[...]
        sc = jnp.where(is_ctx, mc_ref[1:2, :], mx_ref[1:2, :])
        dxn = dxm * (1.0 + sc)
        dxh = dxn * nw
        dx = r * (dxh - xh * jnp.mean(dxh * xh, axis=-1, keepdims=True))

        @pl.when(jnp.logical_not(is_ctx))
        def _():
            gx_ref[...] = dx1_ref[...] + dx

        @pl.when(i == 0)
        def _():
            part_ref[...] = jnp.zeros((8, D), F32)

        fx = jnp.where(is_ctx, 0.0, 1.0)
        d_shift = jnp.sum(dxm, axis=0, keepdims=True)
        d_scale = jnp.sum(dxm * (xh * nw), axis=0, keepdims=True)
        part_ref[0:1, :] += fx * d_shift
        part_ref[1:2, :] += fx * d_scale
        part_ref[2:3, :] += jnp.sum(dxn * xh, axis=0, keepdims=True)
        part_ref[3:4, :] += (1.0 - fx) * d_shift
        part_ref[4:5, :] += (1.0 - fx) * d_scale

    lat = lambda w: pl.BlockSpec((TM, w), lambda i: (jnp.minimum(i, n_x - 1), 0))
    ext = lambda w: pl.BlockSpec((TM, w), lambda i: (i, 0))
    full = lambda a: pl.BlockSpec(a.shape, lambda i: (0,) * a.ndim)
    return pl.pallas_call(
        body, name="input_backward", grid=(lext // TM,),
        in_specs=[lat(5 * D), lat(2 * D), ext(N_HEADS * DK), ext(N_HEADS * DK), ext(D), full(w_in), ext(D), lat(D),
                  full(modx), full(modc), full(norm_w)],
        out_specs=[ext(PW), lat(D), pl.BlockSpec((8, D), lambda i: (0, 0))],
        out_shape=[jax.ShapeDtypeStruct((lext, PW), BF16), jax.ShapeDtypeStruct((seq, D), F32),
                   jax.ShapeDtypeStruct((8, D), F32)],
        compiler_params=_cparams(dimension_semantics=("arbitrary",)),
    )(dpa, dpb, dq, dk, dv, w_in, x_ext, dx1, modx, modc, norm_w)


def _weight_grad_in(xm_t, dp):
    lext = xm_t.shape[1]
    tk = lext // 2

    def body(a_ref, b_ref, o_ref, acc):
        k = pl.program_id(1)

        @pl.when(k == 0)
        def _():
            acc[...] = jnp.zeros_like(acc)

        acc[...] += _dot(a_ref[...], b_ref[...])

        @pl.when(k == pl.num_programs(1) - 1)
        def _():
            o_ref[0] = acc[...]

    return pl.pallas_call(
        body, name="weight_grad_in", grid=(N_DEV, lext // tk),
        in_specs=[pl.BlockSpec((D, tk), lambda d, k: (0, k)), pl.BlockSpec((tk, WSH), lambda d, k: (k, d))],
        out_specs=pl.BlockSpec((1, D, WSH), lambda d, k: (d, 0, 0)),
        out_shape=jax.ShapeDtypeStruct((N_DEV, D, WSH), F32),
        scratch_shapes=[pltpu.VMEM((D, WSH), F32)],
        compiler_params=_cparams(dimension_semantics=("arbitrary", "arbitrary")),
    )(xm_t, dp)


def _weight_grad_square(a_t, b):
    seq = a_t.shape[2]
    tk = seq // 2

    def body(a_ref, b_ref, o_ref, acc):
        k = pl.program_id(1)

        @pl.when(k == 0)
        def _():
            acc[...] = jnp.zeros_like(acc)

        acc[...] += _dot(a_ref[0], b_ref[0])

        @pl.when(k == pl.num_programs(1) - 1)
        def _():
            o_ref[:, 0] = acc[...].reshape(N_DEV, RSH, D)

    return pl.pallas_call(
        body, name="weight_grad_square", grid=(3, seq // tk),
        in_specs=[pl.BlockSpec((1, D, tk), lambda t, k: (t, 0, k)), pl.BlockSpec((1, tk, D), lambda t, k: (t, k, 0))],
        out_specs=pl.BlockSpec((N_DEV, 1, RSH, D), lambda t, k: (0, t, 0, 0)),
        out_shape=jax.ShapeDtypeStruct((N_DEV, 3, RSH, D), F32),
        scratch_shapes=[pltpu.VMEM((D, D), F32)],
        compiler_params=_cparams(dimension_semantics=("arbitrary", "arbitrary")),
    )(a_t, b)


def _rope_tables(seq, ctx_len):
    rows = seq // GRID_W
    row = jnp.repeat(jnp.arange(rows), GRID_W).astype(F32)
    col = jnp.tile(jnp.arange(GRID_W), rows).astype(F32)
    nf = DK // 4
    inv = ROPE_BASE ** (-jnp.arange(nf, dtype=F32) / nf)
    ang = jnp.concatenate([row[:, None] * inv, col[:, None] * inv], axis=-1)
    cos, sin = jnp.cos(ang), jnp.sin(ang)
    cos_t = jnp.tile(cos, (1, 4))
    sin_t = jnp.tile(jnp.concatenate([-sin, sin], axis=-1), (1, 2))
    cos_t = jnp.concatenate([cos_t, jnp.ones((ctx_len, LANES), F32)], axis=0)
    sin_t = jnp.concatenate([sin_t, jnp.zeros((ctx_len, LANES), F32)], axis=0)
    return cos_t, sin_t


def _pad_lanes(a):
    return jnp.pad(a, ((0, 0), (0, D - a.shape[1])))


def kernel(x, c, ctx, c_ctx, norm_w, ada_w, ada_b, w_in, conv_w, conv_b, decay_logit, gn_w, w_a, w_b, w_out, final_norm_w, loss_target, m_c_ctx, m_norm_w, m_ada_w, m_ada_b, m_w_in, m_conv_w, m_conv_b, m_decay_logit, m_gn_w, m_w_a, m_w_b, m_w_out, m_final_norm_w, v_c_ctx, v_norm_w, v_ada_w, v_ada_b, v_w_in, v_conv_w, v_conv_b, v_decay_logit, v_gn_w, v_w_a, v_w_b, v_w_out, v_final_norm_w):
    xi, yi, ci = _coords()
    me = 4 * xi + 2 * yi + ci
    chip = 2 * xi + yi
    seq, ctx_len = x.shape[1], ctx.shape[1]
    assert seq % TM == 0 and seq % RET_C == 0 and ctx_len == TM and seq % GRID_W == 0
    csh = D // N_DEV

    w_in_all, w_a_all, w_b_all, w_out_all = _all_gather_weights(
        [w_in[0].astype(BF16), w_a[0].astype(BF16), w_b[0].astype(BF16), w_out[0].astype(BF16)])
    w_a_all, w_b_all, w_out_all = (w.reshape(D, D) for w in (w_a_all, w_b_all, w_out_all))

    blk = jnp.concatenate([c, _pad_lanes(conv_w[0]), jnp.zeros((4, D), F32)], axis=0)
    got = _all_gather_small(blk, "gather_cond")
    conv_w_all = got[:, 1:4, 0:csh].transpose(1, 0, 2).reshape(3, D)
    c16 = jnp.concatenate([got[:, 0, :], c_ctx[None], jnp.zeros((7, D), F32)], axis=0)
    ada_b_sh = lax.dynamic_slice(ada_b, (0, me * ADA_SH), (1, ADA_SH))
    mod_sh, act16, lg = _modulation(c16, ada_w[0], ada_b_sh, decay_logit[0])
    mod_all = _all_gather_small(mod_sh, "gather_mod").transpose(1, 0, 2).reshape(16, 3 * D)
    modx = lax.dynamic_slice(mod_all, (me, 0), (1, 3 * D)).reshape(3, D)
    modc = mod_all[8].reshape(3, D)

    x2, tgt = x[0], loss_target[0]
    x_ext = jnp.concatenate([x2, ctx[0]], axis=0)
    cos_t, sin_t = _rope_tables(seq, ctx_len)
    p_ext, xm_t = _in_projection(x_ext, modx, modc, norm_w, w_in_all, cos_t, sin_t, seq // TM)
    rn, rstd = _retention_forward(p_ext, lg, seq, ctx_len)
    vecs = jnp.concatenate([modx[2:3], conv_w_all, conv_b, gn_w, final_norm_w[None], jnp.zeros((1, D), F32)], axis=0)
    dx1, dpa, dconv, dret, op_at, op_b, part_m = _merge(p_ext, rn, rstd, x2, tgt, w_a_all, w_b_all, w_out_all, vecs, seq)
    dpb, part_c = _conv_backward(p_ext, dconv, vecs, seq)
    dq, dk, dv, dlg = _retention_backward(p_ext, dret, lg, cos_t, sin_t, seq, ctx_len)
    dp, grad_x, part_i = _input_backward(dpa, dpb, dq, dk, dv, w_in_all, x_ext, dx1, modx, modc, norm_w, seq)
    gw_in = _weight_grad_in(xm_t, dp)
    gw_sq = _weight_grad_square(op_at, op_b).reshape(N_DEV, 3 * RSH, D)

    r1_in, r1_sq = _reduce_scatter_pair([gw_in, gw_sq])
    owners = (2 * jnp.bitwise_xor(chip, jnp.arange(4, dtype=jnp.int32)) + ci).astype(jnp.int32)
    t_in = _pair_sum(gw_in, r1_in, owners, "pair_sum_in")
    t_sq = _pair_sum(gw_sq, r1_sq, owners, "pair_sum_square")
    r2_in, r2_sq = _reduce_scatter_chips([t_in, t_sq])
    g_w_in, d_w_in, nm_w_in, nv_w_in = _adam_sharded(w_in[0], m_w_in[0], v_w_in[0], t_in, r2_in, "adam_w_in")
    stack = lambda a, b, c_: jnp.concatenate([a[0], b[0], c_[0]], axis=0)
    sq = _adam_sharded(stack(w_a, w_b, w_out), stack(m_w_a, m_w_b, m_w_out), stack(v_w_a, v_w_b, v_w_out),
                       t_sq, r2_sq, "adam_square")
    sq = [o.reshape(3, 1, RSH, D) for o in sq]

    dl = dlg[:, 0, 0:4]
    dlg_row = _pad_lanes(jnp.concatenate([dl[:, 0::2].reshape(1, N_HEADS), dl[:, 1::2].reshape(1, N_HEADS)], axis=1))
    partials = jnp.concatenate([
        part_m[0:1], part_i[2:3], part_c[3:4], part_m[2:3], part_c[0:3],
        part_i[0:2], part_m[1:2], part_i[3:5], jnp.zeros((1, D), F32),
        part_m[3:4], dlg_row, jnp.zeros((1, D), F32)], axis=0)
    got = _all_gather_small(partials, "gather_partials")
    tot = _sum_devices(got, "sum_partials")
    dmodc = tot[10:13].reshape(1, 3 * D)
    dmod16 = jnp.concatenate([got[:, 7:10, :].reshape(N_DEV, 3 * D), dmodc, jnp.zeros((7, 3 * D), F32)], axis=0)
    dmod16 = lax.dynamic_slice(dmod16, (0, me * ADA_SH), (16, ADA_SH))
    dmodc8 = jnp.concatenate([dmod16[8:9], jnp.zeros((7, ADA_SH), F32)], axis=0)
    g_ada_w, d_ada_w, nm_ada_w, nv_ada_w, cctx_part = _ada_backward(
        act16.T, dmod16, dmodc8, ada_w[0], m_ada_w[0], v_ada_w[0])
    dact_ctx = _sum_devices(_all_gather_small(cctx_part, "gather_cctx"), "sum_cctx")[0:1]

    my_lanes = lambda a: _pad_lanes(lax.dynamic_slice(a, (0, me * csh), (a.shape[0], csh)))
    small = lambda cc, nw, cb, gn, fn, ab, dlt, cw: jnp.concatenate(
        [cc[None], nw, cb, gn, fn[None], ab.reshape(3, D), _pad_lanes(dlt.reshape(1, 2 * N_HEADS)),
         _pad_lanes(cw[0]), jnp.zeros((4, D), F32)], axis=0)
    w_sm = small(c_ctx, norm_w, conv_b, gn_w, final_norm_w, ada_b, decay_logit, conv_w)
    m_sm = small(m_c_ctx, m_norm_w, m_conv_b, m_gn_w, m_final_norm_w, m_ada_b, m_decay_logit, m_conv_w)
    v_sm = small(v_c_ctx, v_norm_w, v_conv_b, v_gn_w, v_final_norm_w, v_ada_b, v_decay_logit, v_conv_w)
    g_sm = jnp.concatenate([dact_ctx, tot[1:4], tot[0:1], tot[7:10], tot[14:15], my_lanes(tot[4:7]),
                            jnp.zeros((4, D), F32)], axis=0)
    g_sm2 = jnp.concatenate([jnp.zeros((5, D), F32), tot[10:13], jnp.zeros((8, D), F32)], axis=0)
    sm = _adam_small(w_sm, g_sm, g_sm2, m_sm, v_sm, 0, 8)

    def unpack(a, sq_i):
        return [a[0], a[1:2], None, a[5:8].reshape(1, 3 * D), None, a[9:12, 0:csh][None], a[2:3],
                a[8, 0:2 * N_HEADS].reshape(1, 2, N_HEADS), a[3:4], sq_i[0], sq_i[1], sq_i[2], a[4]]

    outs = []
    for a, sq_i, ada_i, in_i in zip(sm, sq, (g_ada_w, d_ada_w, nm_ada_w, nv_ada_w), (g_w_in, d_w_in, nm_w_in, nv_w_in)):
        group = unpack(a, sq_i)
        group[2] = ada_i[None]
        group[4] = in_i[None]
        outs += group
    return (tot[13, 0], grad_x[None], *outs)
```

```python
import functools
import math

import jax
import jax.numpy as jnp
from jax import lax
from jax.experimental import pallas as pl
from jax.experimental.pallas import tpu as pltpu

F32 = jnp.float32
BF16 = jnp.bfloat16
MESH = pl.DeviceIdType.MESH

N_DEV = 8
D = 1024
N_HEADS = 8
DK = 64
DV = 128
GRID_W = 64
ROPE_BASE = 10000.0
EPS = 1e-6
PW = 9 * D
WSH = PW // N_DEV
RSH = D // N_DEV
ADA_SH = 3 * D // N_DEV
TM = 256
RET_C = 256
HALO = 16
LANES = 128
VMEM_LIMIT = 60 * 1024 * 1024

ADAM_LR = 0.001
ADAM_B1 = 0.9
ADAM_B2 = 0.999
ADAM_EPS = 1e-08
ADAM_WD = 0.01
ADAM_STEP = 10

CB_H, CB_BG, CB_CG, CB_ZA, CB_QK, CB_V, CB_ZB, CB_GA, CB_GB = range(9)


def _cparams(**kw):
    return pltpu.CompilerParams(vmem_limit_bytes=VMEM_LIMIT, **kw)


def _dot(a, b):
    return jnp.dot(a, b, preferred_element_type=F32)


def _dot_nt(a, b):
    return lax.dot_general(a, b, (((1,), (1,)), ((), ())), preferred_element_type=F32)


def _dot_tn(a, b):
    return lax.dot_general(a, b, (((0,), (0,)), ((), ())), preferred_element_type=F32)


def _sigmoid(z):
    return 1.0 / (1.0 + jnp.exp(-z))


def _row_tile(rows):
    return TM if rows % TM == 0 else rows


def _coords():
    return lax.axis_index("x"), lax.axis_index("y"), lax.axis_index("c")


def _flip(v, bit):
    return 1 - v if bit else v


def _all_gather_small(blk, name):
    rows, cols = blk.shape

    def body(x_ref, out_ref, send_sems, recv_sems, local_sem):
        x, y, c = _coords()
        me = 4 * x + 2 * y + c
        mine = pltpu.make_async_copy(x_ref, out_ref.at[me], local_sem)
        mine.start()

        def copy(k, slot):
            peer = (_flip(x, k & 4), _flip(y, k & 2), _flip(c, k & 1))
            return pltpu.make_async_remote_copy(
                src_ref=x_ref, dst_ref=out_ref.at[slot], send_sem=send_sems.at[k - 1],
                recv_sem=recv_sems.at[k - 1], device_id=peer, device_id_type=MESH)

        for k in range(1, N_DEV):
            copy(k, me).start()
        for k in range(1, N_DEV):
            copy(k, jnp.bitwise_xor(me, k)).wait_recv()
        for k in range(1, N_DEV):
            copy(k, me).wait_send()
        mine.wait()

    return pl.pallas_call(
        body, name=name,
        out_shape=jax.ShapeDtypeStruct((N_DEV, rows, cols), blk.dtype),
        in_specs=[pl.BlockSpec(memory_space=pltpu.VMEM)],
        out_specs=pl.BlockSpec(memory_space=pltpu.VMEM),
        scratch_shapes=[pltpu.SemaphoreType.DMA((N_DEV - 1,)), pltpu.SemaphoreType.DMA((N_DEV - 1,)),
                        pltpu.SemaphoreType.DMA],
    )(blk)


def _all_gather_weights(shards):
    n = len(shards)

    def body(*refs):
        srcs, outs = refs[:n], refs[n:2 * n]
        send_sems, recv_sems, local_sems = refs[2 * n:]
        x, y, c = _coords()
        me = 4 * x + 2 * y + c
        sib = (x, y, 1 - c)
        chips = [(1 - x, y), (x, 1 - y), (1 - x, 1 - y)]

        def slot(px, py, pc):
            return 4 * px + 2 * py + pc

        def copy(a, k, block_slot, to, own=False):
            return pltpu.make_async_remote_copy(
                src_ref=srcs[a] if own else outs[a].at[block_slot], dst_ref=outs[a].at[block_slot],
                send_sem=send_sems.at[7 * a + k], recv_sem=recv_sems.at[7 * a + k],
                device_id=to, device_id_type=MESH)

        locals_ = [pltpu.make_async_copy(srcs[a], outs[a].at[me], local_sems.at[a]) for a in range(n)]
        for cp in locals_:
            cp.start()
        sent = []
        for a in range(n):
            sent.append(copy(a, 0, me, sib, own=True))
            for j, chip in enumerate(chips):
                sent.append(copy(a, 1 + j, me, (*chip, c), own=True))
        for cp in sent:
            cp.start()
        for j, chip in enumerate(chips):
            for a in range(n):
                copy(a, 1 + j, slot(*chip, c), sib).wait_recv()
                fwd = copy(a, 4 + j, slot(*chip, c), sib)
                fwd.start()
                sent.append(fwd)
        for a in range(n):
            copy(a, 0, slot(x, y, 1 - c), sib).wait_recv()
            for j, chip in enumerate(chips):
                copy(a, 4 + j, slot(*chip, 1 - c), sib).wait_recv()
        for cp in sent:
            cp.wait_send()
        for cp in locals_:
            cp.wait()

    anyspec = pl.BlockSpec(memory_space=pl.ANY)
    return pl.pallas_call(
        body, name="all_gather_weights",
        out_shape=[jax.ShapeDtypeStruct((N_DEV,) + s.shape, s.dtype) for s in shards],
        in_specs=[anyspec] * n, out_specs=[anyspec] * n,
        scratch_shapes=[pltpu.SemaphoreType.DMA((7 * n,)), pltpu.SemaphoreType.DMA((7 * n,)),
                        pltpu.SemaphoreType.DMA((n,))],
    )(*shards)


def _reduce_scatter_pair(grads):
    n = len(grads)

    def body(*refs):
        srcs, outs = refs[:n], refs[n:2 * n]
        send_sems, recv_sems = refs[2 * n:]
        x, y, c = _coords()
        chip = 2 * x + y
        sib = (x, y, 1 - c)
        copies = []
        for a in range(n):
            for j in range(4):
                owner = 2 * jnp.bitwise_xor(chip, j) + (1 - c)
                copies.append(pltpu.make_async_remote_copy(
                    src_ref=srcs[a].at[owner], dst_ref=outs[a].at[j],
                    send_sem=send_sems.at[4 * a + j], recv_sem=recv_sems.at[4 * a + j],
                    device_id=sib, device_id_type=MESH))
        for cp in copies:
            cp.start()
        for cp in copies:
            cp.wait_recv()
        for cp in copies:
            cp.wait_send()

    anyspec = pl.BlockSpec(memory_space=pl.ANY)
    return pl.pallas_call(
        body, name="reduce_scatter_pair",
        out_shape=[jax.ShapeDtypeStruct((4,) + g.shape[1:], g.dtype) for g in grads],
        in_specs=[anyspec] * n, out_specs=[anyspec] * n,
        scratch_shapes=[pltpu.SemaphoreType.DMA((4 * n,)), pltpu.SemaphoreType.DMA((4 * n,))],
    )(*grads)


def _reduce_scatter_chips(parts):
    n = len(parts)

    def body(*refs):
        srcs, outs = refs[:n], refs[n:2 * n]
        send_sems, recv_sems = refs[2 * n:]
        x, y, c = _coords()
        copies = []
        for a in range(n):
            for j in range(1, 4):
                peer = (_flip(x, j & 2), _flip(y, j & 1), c)
                copies.append(pltpu.make_async_remote_copy(
                    src_ref=srcs[a].at[j - 1], dst_ref=outs[a].at[j - 1],
                    send_sem=send_sems.at[3 * a + j - 1], recv_sem=recv_sems.at[3 * a + j - 1],
                    device_id=peer, device_id_type=MESH))
        for cp in copies:
            cp.start()
        for cp in copies:
            cp.wait_recv()
        for cp in copies:
            cp.wait_send()

    anyspec = pl.BlockSpec(memory_space=pl.ANY)
    return pl.pallas_call(
        body, name="reduce_scatter_chips",
        out_shape=[jax.ShapeDtypeStruct((3,) + p.shape[1:], p.dtype) for p in parts],
        in_specs=[anyspec] * n, out_specs=[anyspec] * n,
        scratch_shapes=[pltpu.SemaphoreType.DMA((3 * n,)), pltpu.SemaphoreType.DMA((3 * n,))],
    )(*parts)


def _pair_sum(grad, recv, owners, name):
    _, rows, cols = grad.shape
    tr = _row_tile(rows)

    def body(own_ref, g_ref, r_ref, mine_ref, send_ref):
        j = pl.program_id(1)
        total = g_ref[...] + r_ref[...]

        @pl.when(j == 0)
        def _():
            mine_ref[...] = total

        @pl.when(j > 0)
        def _():
            send_ref[...] = total.astype(BF16)

    grid_spec = pltpu.PrefetchScalarGridSpec(
        num_scalar_prefetch=1, grid=(rows // tr, 4),
        in_specs=[pl.BlockSpec((1, tr, cols), lambda i, j, own: (own[j], i, 0)),
                  pl.BlockSpec((1, tr, cols), lambda i, j, own: (j, i, 0))],
        out_specs=[pl.BlockSpec((1, tr, cols), lambda i, j, own: (0, i, 0)),
                   pl.BlockSpec((1, tr, cols), lambda i, j, own: (jnp.maximum(j - 1, 0), i, 0))])
    return pl.pallas_call(
        body, name=name, grid_spec=grid_spec,
        out_shape=[jax.ShapeDtypeStruct((1, rows, cols), F32), jax.ShapeDtypeStruct((3, rows, cols), BF16)],
        compiler_params=_cparams(dimension_semantics=("arbitrary", "arbitrary")),
    )(owners, grad, recv)


def _modulation(c16, ada_w_sh, ada_b_sh, decay_logit):
    def body(c_ref, w_ref, b_ref, dl_ref, mod_ref, act_ref, lg_ref):
        cv = c_ref[...]
        act = cv * _sigmoid(cv)
        act_ref[...] = act
        mod_ref[...] = jnp.dot(act, w_ref[...], preferred_element_type=F32,
                               precision=lax.Precision.HIGHEST) + b_ref[...]
        z = dl_ref[...]
        lg_ref[...] = jnp.minimum(z, 0.0) - jnp.log(1.0 + jnp.exp(-jnp.abs(z)))

    return pl.pallas_call(
        body, name="modulation",
        out_shape=[jax.ShapeDtypeStruct((16, ADA_SH), F32), jax.ShapeDtypeStruct((16, D), F32),
                   jax.ShapeDtypeStruct(decay_logit.shape, F32)],
        compiler_params=_cparams(),
    )(c16, ada_w_sh, ada_b_sh, decay_logit)


def _adam_update(w, g, m, v):
    m2 = ADAM_B1 * m + (1.0 - ADAM_B1) * g
    v2 = ADAM_B2 * v + (1.0 - ADAM_B2) * (g * g)
    m_hat = m2 / (1.0 - ADAM_B1 ** ADAM_STEP)
    v_hat = v2 / (1.0 - ADAM_B2 ** ADAM_STEP)
    delta = -ADAM_LR * (m_hat / (jnp.sqrt(v_hat) + ADAM_EPS) + ADAM_WD * w)
    return delta, m2, v2


def _adam_sharded(w, m, v, own, recv, name):
    rows, cols = w.shape
    tr = _row_tile(rows)

    def body(w_ref, m_ref, v_ref, g0, g1, g2, g3, g_ref, d_ref, m_out, v_out):
        g = ((g0[0] + g1[0].astype(F32)) + g2[0].astype(F32)) + g3[0].astype(F32)
        delta, m2, v2 = _adam_update(w_ref[...], g, m_ref[...], v_ref[...])
        g_ref[...] = g
        d_ref[...] = delta
        m_out[...] = m2
        v_out[...] = v2

    flat = pl.BlockSpec((tr, cols), lambda i: (i, 0))
    part = lambda j: pl.BlockSpec((1, tr, cols), lambda i, j=j: (j, i, 0))
    return pl.pallas_call(
        body, name=name, grid=(rows // tr,),
        in_specs=[flat, flat, flat, part(0), part(0), part(1), part(2)],
        out_specs=[flat] * 4,
        out_shape=[jax.ShapeDtypeStruct((rows, cols), F32)] * 4,
        compiler_params=_cparams(dimension_semantics=("arbitrary",)),
    )(w, m, v, own, recv, recv, recv)


def _ada_backward(act_t, dmod16, dmodc8, ada_w_sh, m, v):
    def body(at_ref, dm_ref, dc_ref, w_ref, m_ref, v_ref, g_ref, d_ref, m_out, v_out, pc_ref):
        g = jnp.dot(at_ref[...], dm_ref[...], preferred_element_type=F32, precision=lax.Precision.HIGHEST)
        w = w_ref[...]
        delta, m2, v2 = _adam_update(w, g, m_ref[...], v_ref[...])
        g_ref[...] = g
        d_ref[...] = delta
        m_out[...] = m2
        v_out[...] = v2
        pc_ref[...] = lax.dot_general(dc_ref[...], w, (((1,), (1,)), ((), ())), preferred_element_type=F32,
                                      precision=lax.Precision.HIGHEST)

    return pl.pallas_call(
        body, name="ada_backward",
        out_shape=[jax.ShapeDtypeStruct((D, ADA_SH), F32)] * 4 + [jax.ShapeDtypeStruct((8, D), F32)],
        compiler_params=_cparams(),
    )(act_t, dmod16, dmodc8, ada_w_sh, m, v)


def _sum_devices(gathered, name):
    _, rows, cols = gathered.shape

    def body(g_ref, o_ref):
        acc = g_ref[0]
        for d in range(1, N_DEV):
            acc = acc + g_ref[d]
        o_ref[...] = acc

    return pl.pallas_call(body, name=name, out_shape=jax.ShapeDtypeStruct((rows, cols), F32),
                          compiler_params=_cparams())(gathered)


def _adam_small(w, g_raw, g_raw2, m, v, silu_row, sigm_row):
    rows, cols = w.shape

    def body(w_ref, g_ref, g2_ref, m_ref, v_ref, g_out, d_ref, m_out, v_out):
        wv = w_ref[...]
        row = lax.broadcasted_iota(jnp.int32, (rows, cols), 0)
        s = _sigmoid(wv)
        dsilu = s * (1.0 + wv * (1.0 - s))
        scale = jnp.where(row == silu_row, dsilu, jnp.where(row == sigm_row, 1.0 - s, 1.0))
        g = (g_ref[...] + g2_ref[...]) * scale
        delta, m2, v2 = _adam_update(wv, g, m_ref[...], v_ref[...])
        g_out[...] = g
        d_ref[...] = delta
        m_out[...] = m2
        v_out[...] = v2

    return pl.pallas_call(body, name="adam_small", out_shape=[jax.ShapeDtypeStruct((rows, cols), F32)] * 4,
                          compiler_params=_cparams())(w, g_raw, g_raw2, m, v)


def _rope(t, cos, sin):
    lane = lax.broadcasted_iota(jnp.int32, (1, LANES), 1)
    first_half = jnp.bitwise_and(lane, DK // 2) == 0
    partner = jnp.where(first_half, pltpu.roll(t, LANES - DK // 2, 1), pltpu.roll(t, DK // 2, 1))
    return t * cos + partner * sin


def _in_projection(x_ext, modx, modc, norm_w, w_in, cos_t, sin_t, n_x_tiles):
    lext = x_ext.shape[0]
    q_lo, k_lo = 4 * D, 4 * D + N_HEADS * DK

    def body(x_ref, mx_ref, mc_ref, nw_ref, w_ref, cos_ref, sin_ref, p_ref, xmt_ref):
        is_ctx = pl.program_id(0) >= n_x_tiles
        x = x_ref[...]
        r = lax.rsqrt(jnp.mean(x * x, axis=-1, keepdims=True) + EPS)
        sh = jnp.where(is_ctx, mc_ref[0:1, :], mx_ref[0:1, :])
        sc = jnp.where(is_ctx, mc_ref[1:2, :], mx_ref[1:2, :])
        xm = (x * r * nw_ref[...]) * (1.0 + sc) + sh
        xmb = xm.astype(BF16)
        xmt_ref[...] = xm.T.astype(BF16)
        cos, sin = cos_ref[...], sin_ref[...]
        for d in range(N_DEV):
            acc = _dot(xmb, w_ref[d])
            lo = d * WSH
            col = 0
            while col < WSH:
                g = lo + col
                if q_lo <= g < k_lo + N_HEADS * DK:
                    piece = acc[:, col:col + LANES]
                    if g >= k_lo:
                        piece = piece * (DK ** -0.5)
                    p_ref[:, g:g + LANES] = _rope(piece, cos, sin).astype(BF16)
                    col += LANES
                else:
                    if g < q_lo:
                        end = min(WSH, q_lo - lo)
                    else:
                        end = WSH
                    p_ref[:, g:lo + end] = acc[:, col:end].astype(BF16)
                    col = end

    row = lambda w: pl.BlockSpec((TM, w), lambda i: (i, 0))
    full = lambda a: pl.BlockSpec(a.shape, lambda i: (0,) * a.ndim)
    return pl.pallas_call(
        body, name="in_projection", grid=(lext // TM,),
        in_specs=[row(D), full(modx), full(modc), full(norm_w), full(w_in), row(LANES), row(LANES)],
        out_specs=[row(PW), pl.BlockSpec((D, TM), lambda i: (0, i))],
        out_shape=[jax.ShapeDtypeStruct((lext, PW), BF16), jax.ShapeDtypeStruct((D, lext), BF16)],
        compiler_params=_cparams(dimension_semantics=("arbitrary",)),
    )(x_ext, modx, modc, norm_w, w_in, cos_t, sin_t)


def _decay_tables(lgf, lgb, n):
    i = lax.broadcasted_iota(jnp.int32, (n, 1), 0).astype(F32)
    return dict(i=i, k_f=jnp.exp(lgf * (n - 1.0 - i)), k_b=jnp.exp(lgb * i),
                q_f=jnp.exp(lgf * (i + 1.0)), q_b=jnp.exp(lgb * (n - i)))


def _decay_matrix(lgf, lgb, n, transposed=False):
    ii = lax.broadcasted_iota(jnp.int32, (n, n), 0)
    jj = lax.broadcasted_iota(jnp.int32, (n, n), 1)
    diff = (jj - ii if transposed else ii - jj).astype(F32)
    low = jnp.exp(lgf * jnp.maximum(diff, 0.0))
    up = jnp.exp(lgb * jnp.maximum(-diff, 0.0))
    return jnp.where(diff > 0, low, jnp.where(diff < 0, up, 2.0)), diff


def _cat_lanes(a, b):
    return jnp.concatenate([a.astype(BF16), b.astype(BF16)], axis=1)


def _retention_forward(p_ext, lg, seq, ctx_len):
    lext = seq + ctx_len
    n_chunks = seq // RET_C
    C = RET_C

    def body(lg_ref, q_ref, k_ref, v_ref, rn_ref, rstd_ref, kv_scr, sf_scr, sb_scr):
        pair = pl.program_id(0)
        lane = lax.broadcasted_iota(jnp.int32, (1, LANES), 1)
        for hh in range(2):
            hmask = (lane // DK == hh).astype(F32)
            lgf = lg_ref[0, 2 * pair + hh]
            lgb = lg_ref[1, 2 * pair + hh]
            vs = slice(hh * DV, (hh + 1) * DV)
            tc = _decay_tables(lgf, lgb, ctx_len)
            kc = k_ref[seq:lext, :].astype(F32) * hmask
            s0 = _dot_tn(_cat_lanes(kc * tc["k_f"], kc * tc["k_b"]), v_ref[seq:lext, vs])
            t = _decay_tables(lgf, lgb, C)

            def increments(c, carry):
                rows = pl.ds(pl.multiple_of(c * C, C), C)
                k = k_ref[rows, :].astype(F32) * hmask
                kv_scr[c] = _dot_tn(_cat_lanes(k * t["k_f"], k * t["k_b"]), v_ref[rows, vs])
                return carry

            lax.fori_loop(0, n_chunks, increments, 0, unroll=2)
            gf_c, gb_c = jnp.exp(lgf * C), jnp.exp(lgb * C)

            def scan_f(c, s):
                sf_scr[c] = s
                return gf_c * s + kv_scr[c, 0:LANES, :]

            def scan_b(n, s):
                c = n_chunks - 1 - n
                sb_scr[c] = s
                return gb_c * s + kv_scr[c, LANES:2 * LANES, :]

            lax.fori_loop(0, n_chunks, scan_f, s0[0:LANES])
            lax.fori_loop(0, n_chunks, scan_b, s0[LANES:2 * LANES])
            dmat, _ = _decay_matrix(lgf, lgb, C)

            def outputs(c, carry):
                rows = pl.ds(pl.multiple_of(c * C, C), C)
                q = q_ref[rows, :].astype(F32) * hmask
                v = v_ref[rows, vs]
                s = _dot_nt(q.astype(BF16), k_ref[rows, :])
                inner = _dot((s * dmat).astype(BF16), v)
                states = jnp.concatenate([sf_scr[c], sb_scr[c]], axis=0).astype(BF16)
                ret = inner + _dot(_cat_lanes(q * t["q_f"], q * t["q_b"]), states)
                mu = jnp.mean(ret, axis=-1, keepdims=True)
                cen = ret - mu
                rstd = lax.rsqrt(jnp.mean(cen * cen, axis=-1, keepdims=True) + EPS)
                rn_ref[rows, vs] = cen * rstd
                rstd_ref[rows, vs] = jnp.broadcast_to(rstd, (C, DV))
                return carry

            lax.fori_loop(0, n_chunks, outputs, 0, unroll=2)

    qk0 = CB_QK * D // LANES
    return pl.pallas_call(
        body, name="retention_forward", grid=(N_HEADS // 2,),
        in_specs=[pl.BlockSpec(memory_space=pltpu.SMEM),
                  pl.BlockSpec((lext, LANES), lambda g: (0, qk0 + g)),
                  pl.BlockSpec((lext, LANES), lambda g: (0, qk0 + N_HEADS // 2 + g)),
                  pl.BlockSpec((lext, 2 * DV), lambda g: (0, CB_V * D // (2 * DV) + g))],
        out_specs=[pl.BlockSpec((seq, 2 * DV), lambda g: (0, g))] * 2,
        out_shape=[jax.ShapeDtypeStruct((seq, D), F32)] * 2,
        scratch_shapes=[pltpu.VMEM((n_chunks, 2 * LANES, DV), F32), pltpu.VMEM((n_chunks, LANES, DV), F32),
                        pltpu.VMEM((n_chunks, LANES, DV), F32)],
        compiler_params=_cparams(dimension_semantics=("arbitrary",)),
    )(lg, p_ext, p_ext, p_ext)


def _retention_backward(p_ext, dret, lg, cos_t, sin_t, seq, ctx_len):
    lext = seq + ctx_len
    n_chunks = seq // RET_C
    C = RET_C

    def body(lg_ref, q_ref, k_ref, v_ref, do_ref, cos_ref, sin_ref, dq_ref, dk_ref, dv_ref, dlg_ref,
             kv_scr, g_scr, sf_scr, sb_scr, gfn_scr, gbp_scr):
        pair = pl.program_id(0)
        lane = lax.broadcasted_iota(jnp.int32, (1, LANES), 1)
        dlg = []
        for hh in range(2):
            in_head = lane // DK == hh
            hmask = in_head.astype(F32)
            lgf = lg_ref[0, 2 * pair + hh]
            lgb = lg_ref[1, 2 * pair + hh]
            vs = slice(hh * DV, (hh + 1) * DV)
            tc = _decay_tables(lgf, lgb, ctx_len)
            kc = k_ref[seq:lext, :].astype(F32) * hmask
            vc = v_ref[seq:lext, vs]
            kc_cat = _cat_lanes(kc * tc["k_f"], kc * tc["k_b"])
            s0 = _dot_tn(kc_cat, vc)
            t = _decay_tables(lgf, lgb, C)

            def increments(c, carry):
                rows = pl.ds(pl.multiple_of(c * C, C), C)
                k = k_ref[rows, :].astype(F32) * hmask
                q = q_ref[rows, :].astype(F32) * hmask
                kv_scr[c] = _dot_tn(_cat_lanes(k * t["k_f"], k * t["k_b"]), v_ref[rows, vs])
                g_scr[c] = _dot_tn(_cat_lanes(q * t["q_f"], q * t["q_b"]), do_ref[rows, vs])
                return carry

            lax.fori_loop(0, n_chunks, increments, 0, unroll=2)
            gf_c, gb_c = jnp.exp(lgf * C), jnp.exp(lgb * C)

            def scan_f(c, s):
                sf_scr[c] = s
                return gf_c * s + kv_scr[c, 0:LANES, :]

            def scan_b(n, s):
                c = n_chunks - 1 - n
                sb_scr[c] = s
                return gb_c * s + kv_scr[c, LANES:2 * LANES, :]

            def scan_gf(n, carry):
                c = n_chunks - 1 - n
                gfn_scr[c] = carry
                return g_scr[c, 0:LANES, :] + gf_c * carry

            def scan_gb(c, carry):
                gbp_scr[c] = carry
                return g_scr[c, LANES:2 * LANES, :] + gb_c * carry

            lax.fori_loop(0, n_chunks, scan_f, s0[0:LANES])
            lax.fori_loop(0, n_chunks, scan_b, s0[LANES:2 * LANES])
            zero_state = jnp.zeros((LANES, DV), F32)
            gf_first = lax.fori_loop(0, n_chunks, scan_gf, zero_state)
            gb_last = lax.fori_loop(0, n_chunks, scan_gb, zero_state)

            dmat, diff = _decay_matrix(lgf, lgb, C)
            dmat_t, _ = _decay_matrix(lgf, lgb, C, transposed=True)
            w_f = jnp.where(diff > 0, diff * dmat, 0.0)
            w_b = jnp.where(diff < 0, -diff * dmat, 0.0)
            cos_all, sin_all = cos_ref, sin_ref

            def total(a):
                rows_, width = a.shape
                part = jnp.sum(a.reshape(rows_ // 8, 8, width), axis=0)
                return part[:, 0:LANES] + part[:, LANES:2 * LANES] if width == 2 * LANES else part

            def chunk(c, carry):
                dlgf, dlgb = carry
                rows = pl.ds(pl.multiple_of(c * C, C), C)
                qm = q_ref[rows, :].astype(F32) * hmask
                km = k_ref[rows, :].astype(F32) * hmask
                qb, kb = qm.astype(BF16), km.astype(BF16)
                v = v_ref[rows, vs]
                do = do_ref[rows, vs]
                s = _dot_nt(qb, kb)
                s_t = _dot_nt(kb, qb)
                dsd = _dot_nt(do, v)
                dsd_t = _dot_nt(v, do)
                dq_in = _dot((dsd * dmat).astype(BF16), kb)
                dk_in = _dot((dsd_t * dmat_t).astype(BF16), qb)
                dv_in = _dot((s_t * dmat_t).astype(BF16), do)
                prod = s * dsd
                dlgf = dlgf + total(prod * w_f)
                dlgb = dlgb + total(prod * w_b)
                sf, sb = sf_scr[c], sb_scr[c]
                states = jnp.concatenate([sf, sb], axis=0).astype(BF16)
                dqc = _dot_nt(do, states)
                dqf = dqc[:, 0:LANES] * t["q_f"]
                dqb = dqc[:, LANES:2 * LANES] * t["q_b"]
                dq = (dq_in + dqf + dqb) * hmask
                dlgf = dlgf + total((t["i"] + 1.0) * (qm * dqf))
                dlgb = dlgb + total((C - t["i"]) * (qm * dqb))
                gfn, gbp = gfn_scr[c], gbp_scr[c]
                gstates = jnp.concatenate([gfn, gbp], axis=0).astype(BF16)
                dkc = _dot_nt(v, gstates)
                dkf = dkc[:, 0:LANES] * t["k_f"]
                dkb = dkc[:, LANES:2 * LANES] * t["k_b"]
                dk = dk_in + dkf + dkb
                dlgf = dlgf + total((C - 1.0 - t["i"]) * (km * dkf)) + C * gf_c * total(gfn * sf)
                dlgb = dlgb + total(t["i"] * (km * dkb)) + C * gb_c * total(gbp * sb)
                dv = dv_in + _dot(_cat_lanes(km * t["k_f"], km * t["k_b"]), gstates)
                cos, sin = cos_all[rows, :], sin_all[rows, :]
                dq = _rope(dq, cos, -sin)
                dk = _rope(dk, cos, -sin) * (DK ** -0.5)
                if hh == 0:
                    dq_ref[rows, :] = dq.astype(BF16)
                    dk_ref[rows, :] = dk.astype(BF16)
                else:
                    dq_ref[rows, :] = jnp.where(in_head, dq, dq_ref[rows, :].astype(F32)).astype(BF16)
                    dk_ref[rows, :] = jnp.where(in_head, dk, dk_ref[rows, :].astype(F32)).astype(BF16)
                dv_ref[rows, vs] = dv.astype(BF16)
                return dlgf, dlgb

            zero = jnp.zeros((8, LANES), F32)
            dlgf, dlgb = lax.fori_loop(0, n_chunks, chunk, (zero, zero), unroll=2)

            g0 = jnp.concatenate([gf_first, gb_last], axis=0).astype(BF16)
            dkcc = _dot_nt(vc, g0)
            dkcf = dkcc[:, 0:LANES] * tc["k_f"]
            dkcb = dkcc[:, LANES:2 * LANES] * tc["k_b"]
            dlgf = dlgf + total((ctx_len - 1.0 - tc["i"]) * (kc * dkcf))
            dlgb = dlgb + total(tc["i"] * (kc * dkcb))
            dkctx = (dkcf + dkcb) * (DK ** -0.5)
            if hh == 0:
                dk_ref[seq:lext, :] = dkctx.astype(BF16)
                dq_ref[seq:lext, :] = jnp.zeros((ctx_len, LANES), BF16)
            else:
                dk_ref[seq:lext, :] = jnp.where(in_head, dkctx, dk_ref[seq:lext, :].astype(F32)).astype(BF16)
            dv_ref[seq:lext, vs] = _dot(kc_cat, g0).astype(BF16)
            dlg += [jnp.sum(jnp.sum(a, axis=1, keepdims=True), axis=0, keepdims=True) for a in (dlgf, dlgb)]

        lane8 = lax.broadcasted_iota(jnp.int32, (8, LANES), 1)
        out = jnp.zeros((8, LANES), F32)
        for n, val in enumerate(dlg):
            out = jnp.where(lane8 == n, val, out)
        dlg_ref[0] = out

    qk0 = CB_QK * D // LANES
    q_spec = pl.BlockSpec((lext, LANES), lambda g: (0, qk0 + g))
    k_spec = pl.BlockSpec((lext, LANES), lambda g: (0, qk0 + N_HEADS // 2 + g))
    v_spec = pl.BlockSpec((lext, 2 * DV), lambda g: (0, CB_V * D // (2 * DV) + g))
    table = pl.BlockSpec((lext, LANES), lambda g: (0, 0))
    state = pltpu.VMEM((n_chunks, LANES, DV), F32)
    return pl.pallas_call(
        body, name="retention_backward", grid=(N_HEADS // 2,),
        in_specs=[pl.BlockSpec(memory_space=pltpu.SMEM), q_spec, k_spec, v_spec,
                  pl.BlockSpec((seq, 2 * DV), lambda g: (0, g)), table, table],
        out_specs=[pl.BlockSpec((lext, LANES), lambda g: (0, g)), pl.BlockSpec((lext, LANES), lambda g: (0, g)),
                   pl.BlockSpec((lext, 2 * DV), lambda g: (0, g)), pl.BlockSpec((1, 8, LANES), lambda g: (g, 0, 0))],
        out_shape=[jax.ShapeDtypeStruct((lext, N_HEADS * DK), BF16), jax.ShapeDtypeStruct((lext, N_HEADS * DK), BF16),
                   jax.ShapeDtypeStruct((lext, D), BF16), jax.ShapeDtypeStruct((N_HEADS // 2, 8, LANES), F32)],
        scratch_shapes=[pltpu.VMEM((n_chunks, 2 * LANES, DV), F32), pltpu.VMEM((n_chunks, 2 * LANES, DV), F32),
                        state, state, state, state],
        compiler_params=_cparams(dimension_semantics=("arbitrary",)),
    )(lg, p_ext, p_ext, p_ext, dret, cos_t, sin_t)


def _merge(p_ext, rn, rstd, x, target, w_a, w_b, w_out, vecs, seq):
    n_tiles = seq // TM
    hb = TM // HALO

    def body(h_ref, bg_ref, cg_ref, za_ref, zb_ref, ga_ref, gb_ref, hp_ref, hn_ref, cp_ref, cn_ref,
             rn_ref, rstd_ref, x_ref, t_ref, wa_ref, wb_ref, wo_ref, vec_ref,
             dx1_ref, dpa_ref, dconv_ref, dret_ref, at_ref, b_ref, part_ref):
        i = pl.program_id(0)
        f = lambda ref: ref[...].astype(F32)
        h, bg, cg, za, zb, ga, gb = f(h_ref), f(bg_ref), f(cg_ref), f(za_ref), f(zb_ref), f(ga_ref), f(gb_ref)
        gx, w0, w1, w2 = vec_ref[0:1, :], vec_ref[1:2, :], vec_ref[2:3, :], vec_ref[3:4, :]
        cb, gnw, fw = vec_ref[4:5, :], vec_ref[5:6, :], vec_ref[6:7, :]
        u = cg * h
        row = lax.broadcasted_iota(jnp.int32, (TM, 1), 0)
        u_prev = (f(cp_ref) * f(hp_ref))[HALO - 1:HALO, :]
        u_next = (f(cn_ref) * f(hn_ref))[0:1, :]
        u_prev = jnp.where(i == 0, 0.0, u_prev)
        u_next = jnp.where(i == n_tiles - 1, 0.0, u_next)
        u_up = jnp.where(row == 0, u_prev, pltpu.roll(u, 1, 0))
        u_dn = jnp.where(row == TM - 1, u_next, pltpu.roll(u, TM - 1, 0))
        conv = w0 * u_up + w1 * u + w2 * u_dn + cb
        sza = _sigmoid(za)
        silu_za = za * sza
        a_act = silu_za * bg * conv
        rn = rn_ref[...]
        szb = _sigmoid(zb)
        silu_zb = zb * szb
        rg = rn * gnw
        b_act = silu_zb * rg
        y_a = _dot(a_act.astype(BF16), wa_ref[...])
        y_b = _dot(b_act.astype(BF16), wb_ref[...])
        sga, sgb = _sigmoid(ga), _sigmoid(gb)
        mix = sga * y_a + sgb * y_b
        y = _dot(mix.astype(BF16), wo_ref[...])
        x1 = x_ref[...] + gx * y
        r1 = lax.rsqrt(jnp.mean(x1 * x1, axis=-1, keepdims=True) + EPS)
        xh1 = x1 * r1
        err = xh1 * fw - t_ref[...]
        loss = jnp.sum(jnp.sum(err * err, axis=1, keepdims=True), axis=0, keepdims=True) * (0.5 / D)
        dout = err * (1.0 / D)
        dxh = dout * fw
        dx1 = r1 * (dxh - xh1 * jnp.mean(dxh * xh1, axis=-1, keepdims=True))
        dx1_ref[...] = dx1
        dy = (dx1 * gx).astype(BF16)
        dmix = _dot_nt(dy, wo_ref[...])
        dya = (dmix * sga).astype(BF16)
        dyb = (dmix * sgb).astype(BF16)
        da = _dot_nt(dya, wa_ref[...])
        db = _dot_nt(dyb, wb_ref[...])
        dpa_ref[:, 0:D] = (da * silu_za * conv).astype(BF16)
        dpa_ref[:, D:2 * D] = (da * bg * conv * (sza * (1.0 + za * (1.0 - sza)))).astype(BF16)
        dpa_ref[:, 2 * D:3 * D] = (db * rg * (szb * (1.0 + zb * (1.0 - szb)))).astype(BF16)
        dpa_ref[:, 3 * D:4 * D] = (dmix * y_a * sga * (1.0 - sga)).astype(BF16)
        dpa_ref[:, 4 * D:5 * D] = (dmix * y_b * sgb * (1.0 - sgb)).astype(BF16)
        dconv_ref[...] = (da * silu_za * bg).astype(BF16)
        drn_n = db * silu_zb
        drn = drn_n * gnw
        rstd = rstd_ref[...]
        for hd in range(N_HEADS):
            sl = slice(hd * DV, (hd + 1) * DV)
            dh_, rh = drn[:, sl], rn[:, sl]
            m1 = jnp.mean(dh_, axis=-1, keepdims=True)
            m2 = jnp.mean(dh_ * rh, axis=-1, keepdims=True)
            dret_ref[:, sl] = (rstd[:, sl] * (dh_ - m1 - rh * m2)).astype(BF16)
        at_ref[0] = a_act.T.astype(BF16)
        at_ref[1] = b_act.T.astype(BF16)
        at_ref[2] = mix.T.astype(BF16)
        b_ref[0] = dya
        b_ref[1] = dyb
        b_ref[2] = dy

        @pl.when(i == 0)
        def _():
            part_ref[...] = jnp.zeros((8, D), F32)

        part_ref[0:1, :] += jnp.sum(dout * xh1, axis=0, keepdims=True)
        part_ref[1:2, :] += jnp.sum(dx1 * y, axis=0, keepdims=True)
        part_ref[2:3, :] += jnp.sum(drn_n * rn, axis=0, keepdims=True)
        part_ref[3:4, :] += jnp.broadcast_to(loss, (1, D))

    col = lambda cb_: pl.BlockSpec((TM, D), lambda i, cb_=cb_: (i, cb_))
    prev = lambda cb_: pl.BlockSpec((HALO, D), lambda i, cb_=cb_: (jnp.maximum(i * hb - 1, 0), cb_))
    nxt = lambda cb_: pl.BlockSpec((HALO, D), lambda i, cb_=cb_: (jnp.minimum((i + 1) * hb, n_tiles * hb - 1), cb_))
    tile = pl.BlockSpec((TM, D), lambda i: (i, 0))
    full = lambda a: pl.BlockSpec(a.shape, lambda i: (0,) * a.ndim)
    return pl.pallas_call(
        body, name="merge", grid=(n_tiles,),
        in_specs=[col(CB_H), col(CB_BG), col(CB_CG), col(CB_ZA), col(CB_ZB), col(CB_GA), col(CB_GB),
                  prev(CB_H), nxt(CB_H), prev(CB_CG), nxt(CB_CG),
                  tile, tile, tile, tile, full(w_a), full(w_b), full(w_out), full(vecs)],
        out_specs=[tile, pl.BlockSpec((TM, 5 * D), lambda i: (i, 0)), tile, tile,
                   pl.BlockSpec((3, D, TM), lambda i: (0, 0, i)), pl.BlockSpec((3, TM, D), lambda i: (0, i, 0)),
                   pl.BlockSpec((8, D), lambda i: (0, 0))],
        out_shape=[jax.ShapeDtypeStruct((seq, D), F32), jax.ShapeDtypeStruct((seq, 5 * D), BF16),
                   jax.ShapeDtypeStruct((seq, D), BF16), jax.ShapeDtypeStruct((seq, D), BF16),
                   jax.ShapeDtypeStruct((3, D, seq), BF16), jax.ShapeDtypeStruct((3, seq, D), BF16),
                   jax.ShapeDtypeStruct((8, D), F32)],
        compiler_params=_cparams(dimension_semantics=("arbitrary",)),
    )(p_ext, p_ext, p_ext, p_ext, p_ext, p_ext, p_ext, p_ext, p_ext, p_ext, p_ext,
      rn, rstd, x, target, w_a, w_b, w_out, vecs)


def _conv_backward(p_ext, dconv, vecs, seq):
    n_tiles = seq // TM
    hb = TM // HALO

    def body(h_ref, cg_ref, dc_ref, dcp_ref, dcn_ref, vec_ref, dpb_ref, part_ref):
        i = pl.program_id(0)
        f = lambda ref: ref[...].astype(F32)
        h, cg, dc = f(h_ref), f(cg_ref), f(dc_ref)
        w0, w1, w2 = vec_ref[1:2, :], vec_ref[2:3, :], vec_ref[3:4, :]
        row = lax.broadcasted_iota(jnp.int32, (TM, 1), 0)
        dc_prev = jnp.where(i == 0, 0.0, f(dcp_ref)[HALO - 1:HALO, :])
        dc_next = jnp.where(i == n_tiles - 1, 0.0, f(dcn_ref)[0:1, :])
        dc_up = jnp.where(row == 0, dc_prev, pltpu.roll(dc, 1, 0))
        dc_dn = jnp.where(row == TM - 1, dc_next, pltpu.roll(dc, TM - 1, 0))
        du = w0 * dc_dn + w1 * dc + w2 * dc_up
        u = cg * h
        dpb_ref[:, 0:D] = (du * cg).astype(BF16)
        dpb_ref[:, D:2 * D] = (du * h).astype(BF16)

        @pl.when(i == 0)
        def _():
            part_ref[...] = jnp.zeros((8, D), F32)

        part_ref[0:1, :] += jnp.sum(u * dc_dn, axis=0, keepdims=True)
        part_ref[1:2, :] += jnp.sum(u * dc, axis=0, keepdims=True)
        part_ref[2:3, :] += jnp.sum(u * dc_up, axis=0, keepdims=True)
        part_ref[3:4, :] += jnp.sum(dc, axis=0, keepdims=True)

    col = lambda cb_: pl.BlockSpec((TM, D), lambda i, cb_=cb_: (i, cb_))
    tile = pl.BlockSpec((TM, D), lambda i: (i, 0))
    return pl.pallas_call(
        body, name="conv_backward", grid=(n_tiles,),
        in_specs=[col(CB_H), col(CB_CG), tile,
                  pl.BlockSpec((HALO, D), lambda i: (jnp.maximum(i * hb - 1, 0), 0)),
                  pl.BlockSpec((HALO, D), lambda i: (jnp.minimum((i + 1) * hb, n_tiles * hb - 1), 0)),
                  pl.BlockSpec(vecs.shape, lambda i: (0, 0))],
        out_specs=[pl.BlockSpec((TM, 2 * D), lambda i: (i, 0)), pl.BlockSpec((8, D), lambda i: (0, 0))],
        out_shape=[jax.ShapeDtypeStruct((seq, 2 * D), BF16), jax.ShapeDtypeStruct((8, D), F32)],
        compiler_params=_cparams(dimension_semantics=("arbitrary",)),
    )(p_ext, p_ext, dconv, dconv, dconv, vecs)


def _input_backward(dpa, dpb, dq, dk, dv, w_in, x_ext, dx1, modx, modc, norm_w, seq):
    lext = x_ext.shape[0]
    n_x = seq // TM

    def body(dpa_ref, dpb_ref, dq_ref, dk_ref, dv_ref, w_ref, x_ref, dx1_ref, mx_ref, mc_ref, nw_ref,
             dp_ref, gx_ref, part_ref):
        i = pl.program_id(0)
        is_ctx = i >= n_x
        qk = N_HEADS * DK

        @pl.when(jnp.logical_not(is_ctx))
        def _():
            dp_ref[:, CB_H * D:(CB_H + 1) * D] = dpb_ref[:, 0:D]
            dp_ref[:, CB_BG * D:(CB_BG + 1) * D] = dpa_ref[:, 0:D]
            dp_ref[:, CB_CG * D:(CB_CG + 1) * D] = dpb_ref[:, D:2 * D]
            dp_ref[:, CB_ZA * D:(CB_ZA + 1) * D] = dpa_ref[:, D:2 * D]
            dp_ref[:, CB_ZB * D:(CB_GB + 1) * D] = dpa_ref[:, 2 * D:5 * D]

        @pl.when(is_ctx)
        def _():
            dp_ref[:, 0:CB_QK * D] = jnp.zeros((TM, CB_QK * D), BF16)
            dp_ref[:, CB_ZB * D:PW] = jnp.zeros((TM, PW - CB_ZB * D), BF16)

        dp_ref[:, CB_QK * D:CB_QK * D + qk] = dq_ref[...]
        dp_ref[:, CB_QK * D + qk:CB_V * D] = dk_ref[...]
        dp_ref[:, CB_V * D:CB_ZB * D] = dv_ref[...]
        dxm = _dot_nt(dp_ref[:, 0:WSH], w_ref[0])
        for d in range(1, N_DEV):
            dxm = dxm + _dot_nt(dp_ref[:, d * WSH:(d + 1) * WSH], w_ref[d])
        x = x_ref[...]
        r = lax.rsqrt(jnp.mean(x * x, axis=-1, keepdims=True) + EPS)
        xh = x * r
        nw = nw_ref[...]
        sc = jnp.where(is_ctx, mc_ref[1:2, :], mx_ref[1:2, :])
        dxn = dxm * (1.0 + sc)
        dxh = dxn * nw
        dx = r * (dxh - xh * jnp.mean(dxh * xh, axis=-1, keepdims=True))

        @pl.when(jnp.logical_not(is_ctx))
        def _():
            gx_ref[...] = dx1_ref[...] + dx

        @pl.when(i == 0)
        def _():
            part_ref[...] = jnp.zeros((8, D), F32)

        fx = jnp.where(is_ctx, 0.0, 1.0)
        d_shift = jnp.sum(dxm, axis=0, keepdims=True)
        d_scale = jnp.sum(dxm * (xh * nw), axis=0, keepdims=True)
        part_ref[0:1, :] += fx * d_shift
        part_ref[1:2, :] += fx * d_scale
        part_ref[2:3, :] += jnp.sum(dxn * xh, axis=0, keepdims=True)
        part_ref[3:4, :] += (1.0 - fx) * d_shift
        part_ref[4:5, :] += (1.0 - fx) * d_scale

    lat = lambda w: pl.BlockSpec((TM, w), lambda i: (jnp.minimum(i, n_x - 1), 0))
    ext = lambda w: pl.BlockSpec((TM, w), lambda i: (i, 0))
    full = lambda a: pl.BlockSpec(a.shape, lambda i: (0,) * a.ndim)
    return pl.pallas_call(
        body, name="input_backward", grid=(lext // TM,),
        in_specs=[lat(5 * D), lat(2 * D), ext(N_HEADS * DK), ext(N_HEADS * DK), ext(D), full(w_in), ext(D), lat(D),
                  full(modx), full(modc), full(norm_w)],
        out_specs=[ext(PW), lat(D), pl.BlockSpec((8, D), lambda i: (0, 0))],
        out_shape=[jax.ShapeDtypeStruct((lext, PW), BF16), jax.ShapeDtypeStruct((seq, D), F32),
                   jax.ShapeDtypeStruct((8, D), F32)],
        compiler_params=_cparams(dimension_semantics=("arbitrary",)),
    )(dpa, dpb, dq, dk, dv, w_in, x_ext, dx1, modx, modc, norm_w)


def _weight_grad_in(xm_t, dp):
    lext = xm_t.shape[1]
    tk = lext // 2

    def body(a_ref, b_ref, o_ref, acc):
        k = pl.program_id(1)

        @pl.when(k == 0)
        def _():
            acc[...] = jnp.zeros_like(acc)

        acc[...] += _dot(a_ref[...], b_ref[...])

        @pl.when(k == pl.num_programs(1) - 1)
        def _():
            o_ref[0] = acc[...]

    return pl.pallas_call(
        body, name="weight_grad_in", grid=(N_DEV, lext // tk),
        in_specs=[pl.BlockSpec((D, tk), lambda d, k: (0, k)), pl.BlockSpec((tk, WSH), lambda d, k: (k, d))],
        out_specs=pl.BlockSpec((1, D, WSH), lambda d, k: (d, 0, 0)),
        out_shape=jax.ShapeDtypeStruct((N_DEV, D, WSH), F32),
        scratch_shapes=[pltpu.VMEM((D, WSH), F32)],
        compiler_params=_cparams(dimension_semantics=("arbitrary", "arbitrary")),
    )(xm_t, dp)


def _weight_grad_square(a_t, b):
    seq = a_t.shape[2]
    tk = seq // 2

    def body(a_ref, b_ref, o_ref, acc):
        k = pl.program_id(1)

        @pl.when(k == 0)
        def _():
            acc[...] = jnp.zeros_like(acc)

        acc[...] += _dot(a_ref[0], b_ref[0])

        @pl.when(k == pl.num_programs(1) - 1)
        def _():
            o_ref[:, 0] = acc[...].reshape(N_DEV, RSH, D)

    return pl.pallas_call(
        body, name="weight_grad_square", grid=(3, seq // tk),
        in_specs=[pl.BlockSpec((1, D, tk), lambda t, k: (t, 0, k)), pl.BlockSpec((1, tk, D), lambda t, k: (t, k, 0))],
        out_specs=pl.BlockSpec((N_DEV, 1, RSH, D), lambda t, k: (0, t, 0, 0)),
        out_shape=jax.ShapeDtypeStruct((N_DEV, 3, RSH, D), F32),
        scratch_shapes=[pltpu.VMEM((D, D), F32)],
        compiler_params=_cparams(dimension_semantics=("arbitrary", "arbitrary")),
    )(a_t, b)


def _rope_tables(seq, ctx_len):
    rows = seq // GRID_W
    row = jnp.repeat(jnp.arange(rows), GRID_W).astype(F32)
    col = jnp.tile(jnp.arange(GRID_W), rows).astype(F32)
    nf = DK // 4
    inv = ROPE_BASE ** (-jnp.arange(nf, dtype=F32) / nf)
    ang = jnp.concatenate([row[:, None] * inv, col[:, None] * inv], axis=-1)
    cos, sin = jnp.cos(ang), jnp.sin(ang)
    cos_t = jnp.tile(cos, (1, 4))
    sin_t = jnp.tile(jnp.concatenate([-sin, sin], axis=-1), (1, 2))
    cos_t = jnp.concatenate([cos_t, jnp.ones((ctx_len, LANES), F32)], axis=0)
    sin_t = jnp.concatenate([sin_t, jnp.zeros((ctx_len, LANES), F32)], axis=0)
    return cos_t, sin_t


def _pad_lanes(a):
    return jnp.pad(a, ((0, 0), (0, D - a.shape[1])))


def kernel(x, c, ctx, c_ctx, norm_w, ada_w, ada_b, w_in, conv_w, conv_b, decay_logit, gn_w, w_a, w_b, w_out, final_norm_w, loss_target, m_c_ctx, m_norm_w, m_ada_w, m_ada_b, m_w_in, m_conv_w, m_conv_b, m_decay_logit, m_gn_w, m_w_a, m_w_b, m_w_out, m_final_norm_w, v_c_ctx, v_norm_w, v_ada_w, v_ada_b, v_w_in, v_conv_w, v_conv_b, v_decay_logit, v_gn_w, v_w_a, v_w_b, v_w_out, v_final_norm_w):
    xi, yi, ci = _coords()
    me = 4 * xi + 2 * yi + ci
    chip = 2 * xi + yi
    seq, ctx_len = x.shape[1], ctx.shape[1]
    assert seq % TM == 0 and seq % RET_C == 0 and ctx_len == TM and seq % GRID_W == 0
    csh = D // N_DEV

    w_in_all, w_a_all, w_b_all, w_out_all = _all_gather_weights(
        [w_in[0].astype(BF16), w_a[0].astype(BF16), w_b[0].astype(BF16), w_out[0].astype(BF16)])
    w_a_all, w_b_all, w_out_all = (w.reshape(D, D) for w in (w_a_all, w_b_all, w_out_all))

    blk = jnp.pad(c, ((0, 7), (0, 0))) + jnp.pad(conv_w[0], ((1, 4), (0, D - csh)))
    got = _all_gather_small(blk, "gather_cond")
    conv_w_all = got[:, 1:4, 0:csh].transpose(1, 0, 2).reshape(3, D)
    c16 = jnp.concatenate([got[:, 0, :], c_ctx[None], jnp.zeros((7, D), F32)], axis=0)
    ada_b_sh = lax.dynamic_slice(ada_b, (0, me * ADA_SH), (1, ADA_SH))
    mod_sh, act16, lg = _modulation(c16, ada_w[0], ada_b_sh, decay_logit[0])
    mod_all = _all_gather_small(mod_sh, "gather_mod").transpose(1, 0, 2).reshape(16, 3 * D)
    modx = lax.dynamic_slice(mod_all, (me, 0), (1, 3 * D)).reshape(3, D)
    modc = mod_all[8].reshape(3, D)

    x2, tgt = x[0], loss_target[0]
    x_ext = jnp.concatenate([x2, ctx[0]], axis=0)
    cos_t, sin_t = _rope_tables(seq, ctx_len)
    p_ext, xm_t = _in_projection(x_ext, modx, modc, norm_w, w_in_all, cos_t, sin_t, seq // TM)
    rn, rstd = _retention_forward(p_ext, lg, seq, ctx_len)
    vecs = jnp.concatenate([modx[2:3], conv_w_all, conv_b, gn_w, final_norm_w[None], jnp.zeros((1, D), F32)], axis=0)
    dx1, dpa, dconv, dret, op_at, op_b, part_m = _merge(p_ext, rn, rstd, x2, tgt, w_a_all, w_b_all, w_out_all, vecs, seq)
    dpb, part_c = _conv_backward(p_ext, dconv, vecs, seq)
    dq, dk, dv, dlg = _retention_backward(p_ext, dret, lg, cos_t, sin_t, seq, ctx_len)
    dp, grad_x, part_i = _input_backward(dpa, dpb, dq, dk, dv, w_in_all, x_ext, dx1, modx, modc, norm_w, seq)
    gw_in = _weight_grad_in(xm_t, dp)
    gw_sq = _weight_grad_square(op_at, op_b).reshape(N_DEV, 3 * RSH, D)

    r1_in, r1_sq = _reduce_scatter_pair([gw_in, gw_sq])
    owners = (2 * jnp.bitwise_xor(chip, jnp.arange(4, dtype=jnp.int32)) + ci).astype(jnp.int32)
    own_in, send_in = _pair_sum(gw_in, r1_in, owners, "pair_sum_in")
    own_sq, send_sq = _pair_sum(gw_sq, r1_sq, owners, "pair_sum_square")
    r2_in, r2_sq = _reduce_scatter_chips([send_in, send_sq])
    g_w_in, d_w_in, nm_w_in, nv_w_in = _adam_sharded(w_in[0], m_w_in[0], v_w_in[0], own_in, r2_in, "adam_w_in")
    stack = lambda a, b, c_: jnp.concatenate([a[0], b[0], c_[0]], axis=0)
    sq = _adam_sharded(stack(w_a, w_b, w_out), stack(m_w_a, m_w_b, m_w_out), stack(v_w_a, v_w_b, v_w_out),
                       own_sq, r2_sq, "adam_square")
    sq = [o.reshape(3, 1, RSH, D) for o in sq]

    dl = dlg[:, 0, 0:4]
    dlg_row = _pad_lanes(jnp.concatenate([dl[:, 0::2].reshape(1, N_HEADS), dl[:, 1::2].reshape(1, N_HEADS)], axis=1))
    partials = jnp.concatenate([
        part_m[0:1], part_i[2:3], part_c[3:4], part_m[2:3], part_c[0:3],
        part_i[0:2], part_m[1:2], part_i[3:5], jnp.zeros((1, D), F32),
        part_m[3:4], dlg_row, jnp.zeros((1, D), F32)], axis=0)
    got = _all_gather_small(partials, "gather_partials")
    tot = _sum_devices(got, "sum_partials")
    dmodc = tot[10:13].reshape(1, 3 * D)
    dmod16 = jnp.concatenate([got[:, 7:10, :].reshape(N_DEV, 3 * D), dmodc, jnp.zeros((7, 3 * D), F32)], axis=0)
    dmod16 = lax.dynamic_slice(dmod16, (0, me * ADA_SH), (16, ADA_SH))
    dmodc8 = jnp.concatenate([dmod16[8:9], jnp.zeros((7, ADA_SH), F32)], axis=0)
    g_ada_w, d_ada_w, nm_ada_w, nv_ada_w, cctx_part = _ada_backward(
        act16.T, dmod16, dmodc8, ada_w[0], m_ada_w[0], v_ada_w[0])
    dact_ctx = _sum_devices(_all_gather_small(cctx_part, "gather_cctx"), "sum_cctx")[0:1]

    my_lanes = lambda a: _pad_lanes(lax.dynamic_slice(a, (0, me * csh), (a.shape[0], csh)))
    small = lambda cc, nw, cb, gn, fn, ab, dlt, cw: jnp.concatenate(
        [cc[None], nw, cb, gn, fn[None], ab.reshape(3, D), _pad_lanes(dlt.reshape(1, 2 * N_HEADS)),
         _pad_lanes(cw[0]), jnp.zeros((4, D), F32)], axis=0)
    w_sm = small(c_ctx, norm_w, conv_b, gn_w, final_norm_w, ada_b, decay_logit, conv_w)
    m_sm = small(m_c_ctx, m_norm_w, m_conv_b, m_gn_w, m_final_norm_w, m_ada_b, m_decay_logit, m_conv_w)
    v_sm = small(v_c_ctx, v_norm_w, v_conv_b, v_gn_w, v_final_norm_w, v_ada_b, v_decay_logit, v_conv_w)
    g_sm = jnp.concatenate([dact_ctx, tot[1:4], tot[0:1], tot[7:10], tot[14:15], my_lanes(tot[4:7]),
                            jnp.zeros((4, D), F32)], axis=0)
    g_sm2 = jnp.concatenate([jnp.zeros((5, D), F32), tot[10:13], jnp.zeros((8, D), F32)], axis=0)
    sm = _adam_small(w_sm, g_sm, g_sm2, m_sm, v_sm, 0, 8)

    def unpack(a, sq_i):
        return [a[0], a[1:2], None, a[5:8].reshape(1, 3 * D), None, a[9:12, 0:csh][None], a[2:3],
                a[8, 0:2 * N_HEADS].reshape(1, 2, N_HEADS), a[3:4], sq_i[0], sq_i[1], sq_i[2], a[4]]

    outs = []
    for a, sq_i, ada_i, in_i in zip(sm, sq, (g_ada_w, d_ada_w, nm_ada_w, nv_ada_w), (g_w_in, d_w_in, nm_w_in, nv_w_in)):
        group = unpack(a, sq_i)
        group[2] = ada_i[None]
        group[4] = in_i[None]
        outs += group
    return (tot[13, 0], grad_x[None], *outs)
```

```python
import functools
import math

import jax
import jax.numpy as jnp
from jax import lax
from jax.experimental import pallas as pl
from jax.experimental.pallas import tpu as pltpu

F32 = jnp.float32
BF16 = jnp.bfloat16
MESH = pl.DeviceIdType.MESH
_HBM_SPEC = pl.BlockSpec(memory_space=pltpu.HBM)
_SEM_SPEC = pl.BlockSpec(memory_space=pltpu.SEMAPHORE)
_EFFECT = pltpu.SideEffectType.DATAFLOW_SIDE_EFFECTING

N_DEV = 8
D = 1024
N_HEADS = 8
DK = 64
DV = 128
GRID_W = 64
ROPE_BASE = 10000.0
EPS = 1e-6
PW = 9 * D
WSH = PW // N_DEV
RSH = D // N_DEV
ADA_SH = 3 * D // N_DEV
TM = 256
RET_C = 256
HALO = 16
LANES = 128
VMEM_LIMIT = 60 * 1024 * 1024

ADAM_LR = 0.001
ADAM_B1 = 0.9
ADAM_B2 = 0.999
ADAM_EPS = 1e-08
ADAM_WD = 0.01
ADAM_STEP = 10

CB_H, CB_BG, CB_CG, CB_ZA, CB_QK, CB_V, CB_ZB, CB_GA, CB_GB = range(9)


def _cparams(**kw):
    return pltpu.CompilerParams(vmem_limit_bytes=VMEM_LIMIT, **kw)


def _dot(a, b):
    return jnp.dot(a, b, preferred_element_type=F32)


def _dot_nt(a, b):
    return lax.dot_general(a, b, (((1,), (1,)), ((), ())), preferred_element_type=F32)


def _dot_tn(a, b):
    return lax.dot_general(a, b, (((0,), (0,)), ((), ())), preferred_element_type=F32)


def _sigmoid(z):
    return 1.0 / (1.0 + jnp.exp(-z))


def _row_tile(rows):
    return TM if rows % TM == 0 else rows


def _coords():
    return lax.axis_index("x"), lax.axis_index("y"), lax.axis_index("c")


def _flip(v, bit):
    return 1 - v if bit else v


def _all_gather_small(blk, name):
    rows, cols = blk.shape

    def body(x_ref, out_ref, send_sems, recv_sems, local_sem):
        x, y, c = _coords()
        me = 4 * x + 2 * y + c
        mine = pltpu.make_async_copy(x_ref, out_ref.at[me], local_sem)
        mine.start()

        def copy(k, slot):
            peer = (_flip(x, k & 4), _flip(y, k & 2), _flip(c, k & 1))
            return pltpu.make_async_remote_copy(
                src_ref=x_ref, dst_ref=out_ref.at[slot], send_sem=send_sems.at[k - 1],
                recv_sem=recv_sems.at[k - 1], device_id=peer, device_id_type=MESH)

        for k in range(1, N_DEV):
            copy(k, me).start()
        for k in range(1, N_DEV):
            copy(k, jnp.bitwise_xor(me, k)).wait_recv()
        for k in range(1, N_DEV):
            copy(k, me).wait_send()
        mine.wait()

    return pl.pallas_call(
        body, name=name,
        out_shape=jax.ShapeDtypeStruct((N_DEV, rows, cols), blk.dtype),
        in_specs=[pl.BlockSpec(memory_space=pltpu.VMEM)],
        out_specs=pl.BlockSpec(memory_space=pltpu.VMEM),
        scratch_shapes=[pltpu.SemaphoreType.DMA((N_DEV - 1,)), pltpu.SemaphoreType.DMA((N_DEV - 1,)),
                        pltpu.SemaphoreType.DMA],
    )(blk)


def _all_gather_weights(shards):
    n = len(shards)

    def body(*refs):
        srcs, outs = refs[:n], refs[n:2 * n]
        send_sems, recv_sems, local_sems = refs[2 * n:]
        x, y, c = _coords()
        me = 4 * x + 2 * y + c
        sib = (x, y, 1 - c)
        chips = [(1 - x, y), (x, 1 - y), (1 - x, 1 - y)]

        def slot(px, py, pc):
            return 4 * px + 2 * py + pc

        def copy(a, k, block_slot, to, own=False):
            return pltpu.make_async_remote_copy(
                src_ref=srcs[a] if own else outs[a].at[block_slot], dst_ref=outs[a].at[block_slot],
                send_sem=send_sems.at[7 * a + k], recv_sem=recv_sems.at[7 * a + k],
                device_id=to, device_id_type=MESH)

        locals_ = [pltpu.make_async_copy(srcs[a], outs[a].at[me], local_sems.at[a]) for a in range(n)]
        for cp in locals_:
            cp.start()
        sent = []
        for a in range(n):
            sent.append(copy(a, 0, me, sib, own=True))
            for j, chip in enumerate(chips):
                sent.append(copy(a, 1 + j, me, (*chip, c), own=True))
        for cp in sent:
            cp.start()
        for j, chip in enumerate(chips):
            for a in range(n):
                copy(a, 1 + j, slot(*chip, c), sib).wait_recv()
                fwd = copy(a, 4 + j, slot(*chip, c), sib)
                fwd.start()
                sent.append(fwd)
        for a in range(n):
            copy(a, 0, slot(x, y, 1 - c), sib).wait_recv()
            for j, chip in enumerate(chips):
                copy(a, 4 + j, slot(*chip, 1 - c), sib).wait_recv()
        for cp in sent:
            cp.wait_send()
        for cp in locals_:
            cp.wait()

    anyspec = pl.BlockSpec(memory_space=pl.ANY)
    return pl.pallas_call(
        body, name="all_gather_weights",
        out_shape=[jax.ShapeDtypeStruct((N_DEV,) + s.shape, s.dtype) for s in shards],
        in_specs=[anyspec] * n, out_specs=[anyspec] * n,
        scratch_shapes=[pltpu.SemaphoreType.DMA((7 * n,)), pltpu.SemaphoreType.DMA((7 * n,)),
                        pltpu.SemaphoreType.DMA((n,))],
    )(*shards)


class _Exchange:
    def __init__(self, name, srcs, land_shapes, n_copies, plan):
        self.name, self.plan, self.n_copies = name, plan, n_copies
        self.n_src, self.n_land = len(srcs), len(land_shapes)
        hbm = lambda a: pltpu.HBM(a.shape, a.dtype)
        n = self.n_src + self.n_land
        lands = [pltpu.with_memory_space_constraint(lax.empty(s.shape, s.dtype), pltpu.HBM) for s in land_shapes]
        srcs = [pltpu.with_memory_space_constraint(s, pltpu.HBM) for s in srcs]

        def body(*refs):
            send_sems, recv_sems = refs[n], refs[n + 1]
            for cp in self._copies(refs, send_sems, recv_sems):
                cp.start()
            refs[-1][...] = jnp.zeros((8, LANES), F32)

        outs = pl.pallas_call(
            body, name=name + "_start",
            out_shape=(pltpu.SemaphoreType.DMA((n_copies,)), pltpu.SemaphoreType.DMA((n_copies,)),
                       *[hbm(a) for a in srcs], *[hbm(a) for a in lands], jax.ShapeDtypeStruct((8, LANES), F32)),
            in_specs=[_HBM_SPEC] * n,
            out_specs=(_SEM_SPEC, _SEM_SPEC, *[_HBM_SPEC] * n, pl.BlockSpec(memory_space=pltpu.VMEM)),
            input_output_aliases={i: 2 + i for i in range(n)},
            compiler_params=pltpu.CompilerParams(has_side_effects=_EFFECT),
        )(*srcs, *lands)
        self.send_sems, self.recv_sems = outs[0], outs[1]
        self.buffers = list(outs[2:2 + n])
        self.token = outs[-1]

    def _copies(self, refs, send_sems, recv_sems):
        src_refs, land_refs = refs[:self.n_src], refs[self.n_src:self.n_src + self.n_land]
        return [pltpu.make_async_remote_copy(src_ref=s, dst_ref=d, send_sem=send_sems.at[k], recv_sem=recv_sems.at[k],
                                             device_id=dev, device_id_type=MESH)
                for k, (s, d, dev) in enumerate(self.plan(src_refs, land_refs))]

    def wait(self, after):
        n = self.n_src + self.n_land

        def body(*refs):
            for cp in self._copies(refs, refs[n], refs[n + 1]):
                cp.wait_send()
                cp.wait_recv()

        outs = pl.pallas_call(
            body, name=self.name + "_wait",
            out_shape=tuple(pltpu.HBM(b.shape, b.dtype) for b in self.buffers),
            in_specs=[_HBM_SPEC] * n + [_SEM_SPEC, _SEM_SPEC, pl.BlockSpec(memory_space=pl.ANY)],
            out_specs=tuple([_HBM_SPEC] * n),
            input_output_aliases={i: i for i in range(n)},
            compiler_params=pltpu.CompilerParams(has_side_effects=_EFFECT),
        )(*self.buffers, self.send_sems, self.recv_sems, after)
        return list(outs[:self.n_src]), list(outs[self.n_src:])


def _pair_plan(src_index):
    def plan(srcs, lands):
        x, y, c = _coords()
        chip = 2 * x + y
        return [(srcs[0].at[src_index(j, chip, c)], lands[0].at[j], (x, y, 1 - c)) for j in range(4)]
    return plan


def _chips_plan(srcs, lands):
    x, y, c = _coords()
    return [(srcs[0].at[j - 1], lands[0].at[j - 1], (_flip(x, j & 2), _flip(y, j & 1), c)) for j in range(1, 4)]


def _gather_plan(srcs, lands):
    x, y, c = _coords()
    me = 4 * x + 2 * y + c
    return [(srcs[a], lands[a].at[me], (_flip(x, k & 4), _flip(y, k & 2), _flip(c, k & 1)))
            for a in range(len(srcs)) for k in range(1, N_DEV)]


def _pair_sum(grad, recv, owners, name):
    _, rows, cols = grad.shape
    tr = _row_tile(rows)

    def body(own_ref, g_ref, r_ref, mine_ref, send_ref):
        j = pl.program_id(1)
        total = g_ref[...] + r_ref[...]

        @pl.when(j == 0)
        def _():
            mine_ref[...] = total

        @pl.when(j > 0)
        def _():
            send_ref[...] = total.astype(BF16)

    grid_spec = pltpu.PrefetchScalarGridSpec(
        num_scalar_prefetch=1, grid=(rows // tr, 4),
        in_specs=[pl.BlockSpec((1, tr, cols), lambda i, j, own: (own[j], i, 0)),
                  pl.BlockSpec((1, tr, cols), lambda i, j, own: (j, i, 0))],
        out_specs=[pl.BlockSpec((1, tr, cols), lambda i, j, own: (0, i, 0)),
                   pl.BlockSpec((1, tr, cols), lambda i, j, own: (jnp.maximum(j - 1, 0), i, 0))])
    return pl.pallas_call(
        body, name=name, grid_spec=grid_spec,
        out_shape=[jax.ShapeDtypeStruct((1, rows, cols), F32), jax.ShapeDtypeStruct((3, rows, cols), BF16)],
        compiler_params=_cparams(dimension_semantics=("arbitrary", "arbitrary")),
    )(owners, grad, recv)


def _modulation(c16, ada_w_sh, ada_b_sh, decay_logit):
    def body(c_ref, w_ref, b_ref, dl_ref, mod_ref, act_ref, lg_ref):
        cv = c_ref[...]
        act = cv * _sigmoid(cv)
        act_ref[...] = act
        mod_ref[...] = jnp.dot(act, w_ref[...], preferred_element_type=F32,
                               precision=lax.Precision.HIGHEST) + b_ref[...]
        z = dl_ref[...]
        lg_ref[...] = jnp.minimum(z, 0.0) - jnp.log(1.0 + jnp.exp(-jnp.abs(z)))

    return pl.pallas_call(
        body, name="modulation",
        out_shape=[jax.ShapeDtypeStruct((16, ADA_SH), F32), jax.ShapeDtypeStruct((16, D), F32),
                   jax.ShapeDtypeStruct(decay_logit.shape, F32)],
        compiler_params=_cparams(),
    )(c16, ada_w_sh, ada_b_sh, decay_logit)


def _adam_update(w, g, m, v):
    m2 = ADAM_B1 * m + (1.0 - ADAM_B1) * g
    v2 = ADAM_B2 * v + (1.0 - ADAM_B2) * (g * g)
    m_hat = m2 / (1.0 - ADAM_B1 ** ADAM_STEP)
    v_hat = v2 / (1.0 - ADAM_B2 ** ADAM_STEP)
    delta = -ADAM_LR * (m_hat / (jnp.sqrt(v_hat) + ADAM_EPS) + ADAM_WD * w)
    return delta, m2, v2


def _adam_sharded(w, m, v, own, recv, name):
    rows, cols = w.shape
    tr = _row_tile(rows)

    def body(w_ref, m_ref, v_ref, g0, g1, g2, g3, g_ref, d_ref, m_out, v_out):
        g = ((g0[0] + g1[0].astype(F32)) + g2[0].astype(F32)) + g3[0].astype(F32)
        delta, m2, v2 = _adam_update(w_ref[...], g, m_ref[...], v_ref[...])
        g_ref[...] = g
        d_ref[...] = delta
        m_out[...] = m2
        v_out[...] = v2

    flat = pl.BlockSpec((tr, cols), lambda i: (i, 0))
    part = lambda j: pl.BlockSpec((1, tr, cols), lambda i, j=j: (j, i, 0))
    return pl.pallas_call(
        body, name=name, grid=(rows // tr,),
        in_specs=[flat, flat, flat, part(0), part(0), part(1), part(2)],
        out_specs=[flat] * 4,
        out_shape=[jax.ShapeDtypeStruct((rows, cols), F32)] * 4,
        compiler_params=_cparams(dimension_semantics=("arbitrary",)),
    )(w, m, v, own, recv, recv, recv)


def _ada_backward(act_t, dmod16, dmodc8, ada_w_sh, m, v):
    def body(at_ref, dm_ref, dc_ref, w_ref, m_ref, v_ref, g_ref, d_ref, m_out, v_out, pc_ref):
        g = jnp.dot(at_ref[...], dm_ref[...], preferred_element_type=F32, precision=lax.Precision.HIGHEST)
        w = w_ref[...]
        delta, m2, v2 = _adam_update(w, g, m_ref[...], v_ref[...])
        g_ref[...] = g
        d_ref[...] = delta
        m_out[...] = m2
        v_out[...] = v2
        pc_ref[...] = lax.dot_general(dc_ref[...], w, (((1,), (1,)), ((), ())), preferred_element_type=F32,
                                      precision=lax.Precision.HIGHEST)

    return pl.pallas_call(
        body, name="ada_backward",
        out_shape=[jax.ShapeDtypeStruct((D, ADA_SH), F32)] * 4 + [jax.ShapeDtypeStruct((8, D), F32)],
        compiler_params=_cparams(),
    )(act_t, dmod16, dmodc8, ada_w_sh, m, v)


def _sum_devices(gathered, name):
    _, rows, cols = gathered.shape

    def body(g_ref, o_ref):
        acc = g_ref[0]
        for d in range(1, N_DEV):
            acc = acc + g_ref[d]
        o_ref[...] = acc

    return pl.pallas_call(body, name=name, out_shape=jax.ShapeDtypeStruct((rows, cols), F32),
                          compiler_params=_cparams())(gathered)


def _adam_small(w, g_raw, g_raw2, m, v, silu_row, sigm_row):
    rows, cols = w.shape

    def body(w_ref, g_ref, g2_ref, m_ref, v_ref, g_out, d_ref, m_out, v_out):
        wv = w_ref[...]
        row = lax.broadcasted_iota(jnp.int32, (rows, cols), 0)
        s = _sigmoid(wv)
        dsilu = s * (1.0 + wv * (1.0 - s))
        scale = jnp.where(row == silu_row, dsilu, jnp.where(row == sigm_row, 1.0 - s, 1.0))
        g = (g_ref[...] + g2_ref[...]) * scale
        delta, m2, v2 = _adam_update(wv, g, m_ref[...], v_ref[...])
        g_out[...] = g
        d_ref[...] = delta
        m_out[...] = m2
        v_out[...] = v2

    return pl.pallas_call(body, name="adam_small", out_shape=[jax.ShapeDtypeStruct((rows, cols), F32)] * 4,
                          compiler_params=_cparams())(w, g_raw, g_raw2, m, v)


def _rope(t, cos, sin):
    lane = lax.broadcasted_iota(jnp.int32, (1, LANES), 1)
    first_half = jnp.bitwise_and(lane, DK // 2) == 0
    partner = jnp.where(first_half, pltpu.roll(t, LANES - DK // 2, 1), pltpu.roll(t, DK // 2, 1))
    return t * cos + partner * sin


def _in_projection(x, ctx, modx, modc, norm_w, w_in, cos_t, sin_t):
    n_x_tiles = x.shape[0] // TM
    lext = x.shape[0] + ctx.shape[0]
    q_lo, k_lo = 4 * D, 4 * D + N_HEADS * DK

    def body(x_ref, ctx_ref, mx_ref, mc_ref, nw_ref, w_ref, cos_ref, sin_ref, p_ref, xmt_ref):
        is_ctx = pl.program_id(0) >= n_x_tiles
        x = jnp.where(is_ctx, ctx_ref[...], x_ref[...])
        r = lax.rsqrt(jnp.mean(x * x, axis=-1, keepdims=True) + EPS)
        sh = jnp.where(is_ctx, mc_ref[0:1, :], mx_ref[0:1, :])
        sc = jnp.where(is_ctx, mc_ref[1:2, :], mx_ref[1:2, :])
        xm = (x * r * nw_ref[...]) * (1.0 + sc) + sh
        xmb = xm.astype(BF16)
        xmt_ref[...] = xm.T.astype(BF16)
        cos, sin = cos_ref[...], sin_ref[...]
        for d in range(N_DEV):
            acc = _dot(xmb, w_ref[d])
            lo = d * WSH
            col = 0
            while col < WSH:
                g = lo + col
                if q_lo <= g < k_lo + N_HEADS * DK:
                    piece = acc[:, col:col + LANES]
                    if g >= k_lo:
                        piece = piece * (DK ** -0.5)
                    p_ref[:, g:g + LANES] = _rope(piece, cos, sin).astype(BF16)
                    col += LANES
                else:
                    if g < q_lo:
                        end = min(WSH, q_lo - lo)
                    else:
                        end = WSH
                    p_ref[:, g:lo + end] = acc[:, col:end].astype(BF16)
                    col = end

    row = lambda w: pl.BlockSpec((TM, w), lambda i: (i, 0))
    full = lambda a: pl.BlockSpec(a.shape, lambda i: (0,) * a.ndim)
    return pl.pallas_call(
        body, name="in_projection", grid=(lext // TM,),
        in_specs=[pl.BlockSpec((TM, D), lambda i: (jnp.minimum(i, n_x_tiles - 1), 0)), full(ctx),
                  full(modx), full(modc), full(norm_w), full(w_in), row(LANES), row(LANES)],
        out_specs=[row(PW), pl.BlockSpec((D, TM), lambda i: (0, i))],
        out_shape=[jax.ShapeDtypeStruct((lext, PW), BF16), jax.ShapeDtypeStruct((D, lext), BF16)],
        compiler_params=_cparams(dimension_semantics=("arbitrary",)),
    )(x, ctx, modx, modc, norm_w, w_in, cos_t, sin_t)


def _decay_tables(lgf, lgb, n):
    i = lax.broadcasted_iota(jnp.int32, (n, 1), 0).astype(F32)
    return dict(i=i, k_f=jnp.exp(lgf * (n - 1.0 - i)), k_b=jnp.exp(lgb * i),
                q_f=jnp.exp(lgf * (i + 1.0)), q_b=jnp.exp(lgb * (n - i)))


def _decay_matrix(lgf, lgb, n, transposed=False):
    ii = lax.broadcasted_iota(jnp.int32, (n, n), 0)
    jj = lax.broadcasted_iota(jnp.int32, (n, n), 1)
    diff = (jj - ii if transposed else ii - jj).astype(F32)
    low = jnp.exp(lgf * jnp.maximum(diff, 0.0))
    up = jnp.exp(lgb * jnp.maximum(-diff, 0.0))
    return jnp.where(diff > 0, low, jnp.where(diff < 0, up, 2.0)), diff


def _cat_lanes(a, b):
    return jnp.concatenate([a.astype(BF16), b.astype(BF16)], axis=1)


def _retention_forward(p_ext, lg, seq, ctx_len):
    lext = seq + ctx_len
    n_chunks = seq // RET_C
    C = RET_C

    def body(lg_ref, q_ref, k_ref, v_ref, rn_ref, rstd_ref, kv_scr, sf_scr, sb_scr):
        pair = pl.program_id(0)
        lane = lax.broadcasted_iota(jnp.int32, (1, LANES), 1)
        for hh in range(2):
            hmask = (lane // DK == hh).astype(F32)
            lgf = lg_ref[0, 2 * pair + hh]
            lgb = lg_ref[1, 2 * pair + hh]
            vs = slice(hh * DV, (hh + 1) * DV)
            tc = _decay_tables(lgf, lgb, ctx_len)
            kc = k_ref[seq:lext, :].astype(F32) * hmask
            s0 = _dot_tn(_cat_lanes(kc * tc["k_f"], kc * tc["k_b"]), v_ref[seq:lext, vs])
            t = _decay_tables(lgf, lgb, C)

            def increments(c, carry):
                rows = pl.ds(pl.multiple_of(c * C, C), C)
                k = k_ref[rows, :].astype(F32) * hmask
                kv_scr[c] = _dot_tn(_cat_lanes(k * t["k_f"], k * t["k_b"]), v_ref[rows, vs])
                return carry

            lax.fori_loop(0, n_chunks, increments, 0, unroll=2)
            gf_c, gb_c = jnp.exp(lgf * C), jnp.exp(lgb * C)

            def scan_f(c, s):
                sf_scr[c] = s
                return gf_c * s + kv_scr[c, 0:LANES, :]

            def scan_b(n, s):
                c = n_chunks - 1 - n
                sb_scr[c] = s
                return gb_c * s + kv_scr[c, LANES:2 * LANES, :]

            lax.fori_loop(0, n_chunks, scan_f, s0[0:LANES])
            lax.fori_loop(0, n_chunks, scan_b, s0[LANES:2 * LANES])
            dmat, _ = _decay_matrix(lgf, lgb, C)

            def outputs(c, carry):
                rows = pl.ds(pl.multiple_of(c * C, C), C)
                q = q_ref[rows, :].astype(F32) * hmask
                v = v_ref[rows, vs]
                s = _dot_nt(q.astype(BF16), k_ref[rows, :])
                inner = _dot((s * dmat).astype(BF16), v)
                states = jnp.concatenate([sf_scr[c], sb_scr[c]], axis=0).astype(BF16)
                ret = inner + _dot(_cat_lanes(q * t["q_f"], q * t["q_b"]), states)
                mu = jnp.mean(ret, axis=-1, keepdims=True)
                cen = ret - mu
                rstd = lax.rsqrt(jnp.mean(cen * cen, axis=-1, keepdims=True) + EPS)
                rn_ref[rows, vs] = cen * rstd
                rstd_ref[rows, vs] = jnp.broadcast_to(rstd, (C, DV))
                return carry

            lax.fori_loop(0, n_chunks, outputs, 0, unroll=2)

    qk0 = CB_QK * D // LANES
    return pl.pallas_call(
        body, name="retention_forward", grid=(N_HEADS // 2,),
        in_specs=[pl.BlockSpec(memory_space=pltpu.SMEM),
                  pl.BlockSpec((lext, LANES), lambda g: (0, qk0 + g)),
                  pl.BlockSpec((lext, LANES), lambda g: (0, qk0 + N_HEADS // 2 + g)),
                  pl.BlockSpec((lext, 2 * DV), lambda g: (0, CB_V * D // (2 * DV) + g))],
        out_specs=[pl.BlockSpec((seq, 2 * DV), lambda g: (0, g))] * 2,
        out_shape=[jax.ShapeDtypeStruct((seq, D), F32)] * 2,
        scratch_shapes=[pltpu.VMEM((n_chunks, 2 * LANES, DV), F32), pltpu.VMEM((n_chunks, LANES, DV), F32),
                        pltpu.VMEM((n_chunks, LANES, DV), F32)],
        compiler_params=_cparams(dimension_semantics=("arbitrary",)),
    )(lg, p_ext, p_ext, p_ext)


def _retention_backward(p_ext, dret, lg, cos_t, sin_t, seq, ctx_len):
    lext = seq + ctx_len
    n_chunks = seq // RET_C
    C = RET_C

    def body(lg_ref, q_ref, k_ref, v_ref, do_ref, cos_ref, sin_ref, dq_ref, dk_ref, dv_ref, dlg_ref,
             kv_scr, g_scr, sf_scr, sb_scr, gfn_scr, gbp_scr):
        pair = pl.program_id(0)
        lane = lax.broadcasted_iota(jnp.int32, (1, LANES), 1)
        dlg = []
        for hh in range(2):
            in_head = lane // DK == hh
            hmask = in_head.astype(F32)
            lgf = lg_ref[0, 2 * pair + hh]
            lgb = lg_ref[1, 2 * pair + hh]
            vs = slice(hh * DV, (hh + 1) * DV)
            tc = _decay_tables(lgf, lgb, ctx_len)
            kc = k_ref[seq:lext, :].astype(F32) * hmask
            vc = v_ref[seq:lext, vs]
            kc_cat = _cat_lanes(kc * tc["k_f"], kc * tc["k_b"])
            s0 = _dot_tn(kc_cat, vc)
            t = _decay_tables(lgf, lgb, C)

            def increments(c, carry):
                rows = pl.ds(pl.multiple_of(c * C, C), C)
                k = k_ref[rows, :].astype(F32) * hmask
                q = q_ref[rows, :].astype(F32) * hmask
                kv_scr[c] = _dot_tn(_cat_lanes(k * t["k_f"], k * t["k_b"]), v_ref[rows, vs])
                g_scr[c] = _dot_tn(_cat_lanes(q * t["q_f"], q * t["q_b"]), do_ref[rows, vs])
                return carry

            lax.fori_loop(0, n_chunks, increments, 0, unroll=2)
            gf_c, gb_c = jnp.exp(lgf * C), jnp.exp(lgb * C)

            def scan_f(c, s):
                sf_scr[c] = s
                return gf_c * s + kv_scr[c, 0:LANES, :]

            def scan_b(n, s):
                c = n_chunks - 1 - n
                sb_scr[c] = s
                return gb_c * s + kv_scr[c, LANES:2 * LANES, :]

            def scan_gf(n, carry):
                c = n_chunks - 1 - n
                gfn_scr[c] = carry
                return g_scr[c, 0:LANES, :] + gf_c * carry

            def scan_gb(c, carry):
                gbp_scr[c] = carry
                return g_scr[c, LANES:2 * LANES, :] + gb_c * carry

            lax.fori_loop(0, n_chunks, scan_f, s0[0:LANES])
            lax.fori_loop(0, n_chunks, scan_b, s0[LANES:2 * LANES])
            zero_state = jnp.zeros((LANES, DV), F32)
            gf_first = lax.fori_loop(0, n_chunks, scan_gf, zero_state)
            gb_last = lax.fori_loop(0, n_chunks, scan_gb, zero_state)

            dmat, diff = _decay_matrix(lgf, lgb, C)
            dmat_t, _ = _decay_matrix(lgf, lgb, C, transposed=True)
            w_f = jnp.where(diff > 0, diff * dmat, 0.0)
            w_b = jnp.where(diff < 0, -diff * dmat, 0.0)
            cos_all, sin_all = cos_ref, sin_ref

            def total(a):
                rows_, width = a.shape
                part = jnp.sum(a.reshape(rows_ // 8, 8, width), axis=0)
                return part[:, 0:LANES] + part[:, LANES:2 * LANES] if width == 2 * LANES else part

            def chunk(c, carry):
                dlgf, dlgb = carry
                rows = pl.ds(pl.multiple_of(c * C, C), C)
                qm = q_ref[rows, :].astype(F32) * hmask
                km = k_ref[rows, :].astype(F32) * hmask
                qb, kb = qm.astype(BF16), km.astype(BF16)
                v = v_ref[rows, vs]
                do = do_ref[rows, vs]
                s = _dot_nt(qb, kb)
                s_t = _dot_nt(kb, qb)
                dsd = _dot_nt(do, v)
                dsd_t = _dot_nt(v, do)
                dq_in = _dot((dsd * dmat).astype(BF16), kb)
                dk_in = _dot((dsd_t * dmat_t).astype(BF16), qb)
                dv_in = _dot((s_t * dmat_t).astype(BF16), do)
                prod = s * dsd
                dlgf = dlgf + total(prod * w_f)
                dlgb = dlgb + total(prod * w_b)
                sf, sb = sf_scr[c], sb_scr[c]
                states = jnp.concatenate([sf, sb], axis=0).astype(BF16)
                dqc = _dot_nt(do, states)
                dqf = dqc[:, 0:LANES] * t["q_f"]
                dqb = dqc[:, LANES:2 * LANES] * t["q_b"]
                dq = (dq_in + dqf + dqb) * hmask
                dlgf = dlgf + total((t["i"] + 1.0) * (qm * dqf))
                dlgb = dlgb + total((C - t["i"]) * (qm * dqb))
                gfn, gbp = gfn_scr[c], gbp_scr[c]
                gstates = jnp.concatenate([gfn, gbp], axis=0).astype(BF16)
                dkc = _dot_nt(v, gstates)
                dkf = dkc[:, 0:LANES] * t["k_f"]
                dkb = dkc[:, LANES:2 * LANES] * t["k_b"]
                dk = dk_in + dkf + dkb
                dlgf = dlgf + total((C - 1.0 - t["i"]) * (km * dkf)) + C * gf_c * total(gfn * sf)
                dlgb = dlgb + total(t["i"] * (km * dkb)) + C * gb_c * total(gbp * sb)
                dv = dv_in + _dot(_cat_lanes(km * t["k_f"], km * t["k_b"]), gstates)
                cos, sin = cos_all[rows, :], sin_all[rows, :]
                dq = _rope(dq, cos, -sin)
                dk = _rope(dk, cos, -sin) * (DK ** -0.5)
                if hh == 0:
                    dq_ref[rows, :] = dq.astype(BF16)
                    dk_ref[rows, :] = dk.astype(BF16)
                else:
                    dq_ref[rows, :] = jnp.where(in_head, dq, dq_ref[rows, :].astype(F32)).astype(BF16)
                    dk_ref[rows, :] = jnp.where(in_head, dk, dk_ref[rows, :].astype(F32)).astype(BF16)
                dv_ref[rows, vs] = dv.astype(BF16)
                return dlgf, dlgb

            zero = jnp.zeros((8, LANES), F32)
            dlgf, dlgb = lax.fori_loop(0, n_chunks, chunk, (zero, zero), unroll=2)

            g0 = jnp.concatenate([gf_first, gb_last], axis=0).astype(BF16)
            dkcc = _dot_nt(vc, g0)
            dkcf = dkcc[:, 0:LANES] * tc["k_f"]
            dkcb = dkcc[:, LANES:2 * LANES] * tc["k_b"]
            dlgf = dlgf + total((ctx_len - 1.0 - tc["i"]) * (kc * dkcf))
            dlgb = dlgb + total(tc["i"] * (kc * dkcb))
            dkctx = (dkcf + dkcb) * (DK ** -0.5)
            if hh == 0:
                dk_ref[seq:lext, :] = dkctx.astype(BF16)
                dq_ref[seq:lext, :] = jnp.zeros((ctx_len, LANES), BF16)
            else:
                dk_ref[seq:lext, :] = jnp.where(in_head, dkctx, dk_ref[seq:lext, :].astype(F32)).astype(BF16)
            dv_ref[seq:lext, vs] = _dot(kc_cat, g0).astype(BF16)
            dlg += [jnp.sum(jnp.sum(a, axis=1, keepdims=True), axis=0, keepdims=True) for a in (dlgf, dlgb)]

        lane8 = lax.broadcasted_iota(jnp.int32, (8, LANES), 1)
        out = jnp.zeros((8, LANES), F32)
        for n, val in enumerate(dlg):
            out = jnp.where(lane8 == n, val, out)
        dlg_ref[0] = out

    qk0 = CB_QK * D // LANES
    q_spec = pl.BlockSpec((lext, LANES), lambda g: (0, qk0 + g))
    k_spec = pl.BlockSpec((lext, LANES), lambda g: (0, qk0 + N_HEADS // 2 + g))
    v_spec = pl.BlockSpec((lext, 2 * DV), lambda g: (0, CB_V * D // (2 * DV) + g))
    table = pl.BlockSpec((lext, LANES), lambda g: (0, 0))
    state = pltpu.VMEM((n_chunks, LANES, DV), F32)
    return pl.pallas_call(
        body, name="retention_backward", grid=(N_HEADS // 2,),
        in_specs=[pl.BlockSpec(memory_space=pltpu.SMEM), q_spec, k_spec, v_spec,
                  pl.BlockSpec((seq, 2 * DV), lambda g: (0, g)), table, table],
        out_specs=[pl.BlockSpec((lext, LANES), lambda g: (0, g)), pl.BlockSpec((lext, LANES), lambda g: (0, g)),
                   pl.BlockSpec((lext, 2 * DV), lambda g: (0, g)), pl.BlockSpec((1, 8, LANES), lambda g: (g, 0, 0))],
        out_shape=[jax.ShapeDtypeStruct((lext, N_HEADS * DK), BF16), jax.ShapeDtypeStruct((lext, N_HEADS * DK), BF16),
                   jax.ShapeDtypeStruct((lext, D), BF16), jax.ShapeDtypeStruct((N_HEADS // 2, 8, LANES), F32)],
        scratch_shapes=[pltpu.VMEM((n_chunks, 2 * LANES, DV), F32), pltpu.VMEM((n_chunks, 2 * LANES, DV), F32),
                        state, state, state, state],
        compiler_params=_cparams(dimension_semantics=("arbitrary",)),
    )(lg, p_ext, p_ext, p_ext, dret, cos_t, sin_t)


def _merge(p_ext, rn, rstd, x, target, w_a, w_b, w_out, vecs, seq):
    n_tiles = seq // TM
    hb = TM // HALO

    def body(h_ref, bg_ref, cg_ref, za_ref, zb_ref, ga_ref, gb_ref, hp_ref, hn_ref, cp_ref, cn_ref,
             rn_ref, rstd_ref, x_ref, t_ref, wa_ref, wb_ref, wo_ref, vec_ref,
             dx1_ref, dpa_ref, dconv_ref, dret_ref, at_ref, b_ref, part_ref):
        i = pl.program_id(0)
        f = lambda ref: ref[...].astype(F32)
        h, bg, cg, za, zb, ga, gb = f(h_ref), f(bg_ref), f(cg_ref), f(za_ref), f(zb_ref), f(ga_ref), f(gb_ref)
        gx, w0, w1, w2 = vec_ref[0:1, :], vec_ref[1:2, :], vec_ref[2:3, :], vec_ref[3:4, :]
        cb, gnw, fw = vec_ref[4:5, :], vec_ref[5:6, :], vec_ref[6:7, :]
        u = cg * h
        row = lax.broadcasted_iota(jnp.int32, (TM, 1), 0)
        u_prev = (f(cp_ref) * f(hp_ref))[HALO - 1:HALO, :]
        u_next = (f(cn_ref) * f(hn_ref))[0:1, :]
        u_prev = jnp.where(i == 0, 0.0, u_prev)
        u_next = jnp.where(i == n_tiles - 1, 0.0, u_next)
        u_up = jnp.where(row == 0, u_prev, pltpu.roll(u, 1, 0))
        u_dn = jnp.where(row == TM - 1, u_next, pltpu.roll(u, TM - 1, 0))
        conv = w0 * u_up + w1 * u + w2 * u_dn + cb
        sza = _sigmoid(za)
        silu_za = za * sza
        a_act = silu_za * bg * conv
        rn = rn_ref[...]
        szb = _sigmoid(zb)
        silu_zb = zb * szb
        rg = rn * gnw
        b_act = silu_zb * rg
        y_a = _dot(a_act.astype(BF16), wa_ref[...])
        y_b = _dot(b_act.astype(BF16), wb_ref[...])
        sga, sgb = _sigmoid(ga), _sigmoid(gb)
        mix = sga * y_a + sgb * y_b
        y = _dot(mix.astype(BF16), wo_ref[...])
        x1 = x_ref[...] + gx * y
        r1 = lax.rsqrt(jnp.mean(x1 * x1, axis=-1, keepdims=True) + EPS)
        xh1 = x1 * r1
        err = xh1 * fw - t_ref[...]
        loss = jnp.sum(jnp.sum(err * err, axis=1, keepdims=True), axis=0, keepdims=True) * (0.5 / D)
        dout = err * (1.0 / D)
        dxh = dout * fw
        dx1 = r1 * (dxh - xh1 * jnp.mean(dxh * xh1, axis=-1, keepdims=True))
        dx1_ref[...] = dx1
        dy = (dx1 * gx).astype(BF16)
        dmix = _dot_nt(dy, wo_ref[...])
        dya = (dmix * sga).astype(BF16)
        dyb = (dmix * sgb).astype(BF16)
        da = _dot_nt(dya, wa_ref[...])
        db = _dot_nt(dyb, wb_ref[...])
        dpa_ref[:, 0:D] = (da * silu_za * conv).astype(BF16)
        dpa_ref[:, D:2 * D] = (da * bg * conv * (sza * (1.0 + za * (1.0 - sza)))).astype(BF16)
        dpa_ref[:, 2 * D:3 * D] = (db * rg * (szb * (1.0 + zb * (1.0 - szb)))).astype(BF16)
        dpa_ref[:, 3 * D:4 * D] = (dmix * y_a * sga * (1.0 - sga)).astype(BF16)
        dpa_ref[:, 4 * D:5 * D] = (dmix * y_b * sgb * (1.0 - sgb)).astype(BF16)
        dconv_ref[...] = (da * silu_za * bg).astype(BF16)
        drn_n = db * silu_zb
        drn = drn_n * gnw
        rstd = rstd_ref[...]
        for hd in range(N_HEADS):
            sl = slice(hd * DV, (hd + 1) * DV)
            dh_, rh = drn[:, sl], rn[:, sl]
            m1 = jnp.mean(dh_, axis=-1, keepdims=True)
            m2 = jnp.mean(dh_ * rh, axis=-1, keepdims=True)
            dret_ref[:, sl] = (rstd[:, sl] * (dh_ - m1 - rh * m2)).astype(BF16)
        at_ref[0] = a_act.T.astype(BF16)
        at_ref[1] = b_act.T.astype(BF16)
        at_ref[2] = mix.T.astype(BF16)
        b_ref[0] = dya
        b_ref[1] = dyb
        b_ref[2] = dy

        @pl.when(i == 0)
        def _():
            part_ref[...] = jnp.zeros((8, D), F32)

        part_ref[0:1, :] += jnp.sum(dout * xh1, axis=0, keepdims=True)
        part_ref[1:2, :] += jnp.sum(dx1 * y, axis=0, keepdims=True)
        part_ref[2:3, :] += jnp.sum(drn_n * rn, axis=0, keepdims=True)
        part_ref[3:4, :] += jnp.broadcast_to(loss, (1, D))

    col = lambda cb_: pl.BlockSpec((TM, D), lambda i, cb_=cb_: (i, cb_))
    prev = lambda cb_: pl.BlockSpec((HALO, D), lambda i, cb_=cb_: (jnp.maximum(i * hb - 1, 0), cb_))
    nxt = lambda cb_: pl.BlockSpec((HALO, D), lambda i, cb_=cb_: (jnp.minimum((i + 1) * hb, n_tiles * hb - 1), cb_))
    tile = pl.BlockSpec((TM, D), lambda i: (i, 0))
    full = lambda a: pl.BlockSpec(a.shape, lambda i: (0,) * a.ndim)
    return pl.pallas_call(
        body, name="merge", grid=(n_tiles,),
        in_specs=[col(CB_H), col(CB_BG), col(CB_CG), col(CB_ZA), col(CB_ZB), col(CB_GA), col(CB_GB),
                  prev(CB_H), nxt(CB_H), prev(CB_CG), nxt(CB_CG),
                  tile, tile, tile, tile, full(w_a), full(w_b), full(w_out), full(vecs)],
        out_specs=[tile, pl.BlockSpec((TM, 5 * D), lambda i: (i, 0)), tile, tile,
                   pl.BlockSpec((3, D, TM), lambda i: (0, 0, i)), pl.BlockSpec((3, TM, D), lambda i: (0, i, 0)),
                   pl.BlockSpec((8, D), lambda i: (0, 0))],
        out_shape=[jax.ShapeDtypeStruct((seq, D), F32), jax.ShapeDtypeStruct((seq, 5 * D), BF16),
                   jax.ShapeDtypeStruct((seq, D), BF16), jax.ShapeDtypeStruct((seq, D), BF16),
                   jax.ShapeDtypeStruct((3, D, seq), BF16), jax.ShapeDtypeStruct((3, seq, D), BF16),
                   jax.ShapeDtypeStruct((8, D), F32)],
        compiler_params=_cparams(dimension_semantics=("arbitrary",)),
    )(p_ext, p_ext, p_ext, p_ext, p_ext, p_ext, p_ext, p_ext, p_ext, p_ext, p_ext,
      rn, rstd, x, target, w_a, w_b, w_out, vecs)


def _conv_backward(p_ext, dconv, vecs, seq):
    n_tiles = seq // TM
    hb = TM // HALO

    def body(h_ref, cg_ref, dc_ref, dcp_ref, dcn_ref, vec_ref, dpb_ref, part_ref):
        i = pl.program_id(0)
        f = lambda ref: ref[...].astype(F32)
        h, cg, dc = f(h_ref), f(cg_ref), f(dc_ref)
        w0, w1, w2 = vec_ref[1:2, :], vec_ref[2:3, :], vec_ref[3:4, :]
        row = lax.broadcasted_iota(jnp.int32, (TM, 1), 0)
        dc_prev = jnp.where(i == 0, 0.0, f(dcp_ref)[HALO - 1:HALO, :])
        dc_next = jnp.where(i == n_tiles - 1, 0.0, f(dcn_ref)[0:1, :])
        dc_up = jnp.where(row == 0, dc_prev, pltpu.roll(dc, 1, 0))
        dc_dn = jnp.where(row == TM - 1, dc_next, pltpu.roll(dc, TM - 1, 0))
        du = w0 * dc_dn + w1 * dc + w2 * dc_up
        u = cg * h
        dpb_ref[:, 0:D] = (du * cg).astype(BF16)
        dpb_ref[:, D:2 * D] = (du * h).astype(BF16)

        @pl.when(i == 0)
        def _():
            part_ref[...] = jnp.zeros((8, D), F32)

        part_ref[0:1, :] += jnp.sum(u * dc_dn, axis=0, keepdims=True)
        part_ref[1:2, :] += jnp.sum(u * dc, axis=0, keepdims=True)
        part_ref[2:3, :] += jnp.sum(u * dc_up, axis=0, keepdims=True)
        part_ref[3:4, :] += jnp.sum(dc, axis=0, keepdims=True)

    col = lambda cb_: pl.BlockSpec((TM, D), lambda i, cb_=cb_: (i, cb_))
    tile = pl.BlockSpec((TM, D), lambda i: (i, 0))
    return pl.pallas_call(
        body, name="conv_backward", grid=(n_tiles,),
        in_specs=[col(CB_H), col(CB_CG), tile,
                  pl.BlockSpec((HALO, D), lambda i: (jnp.maximum(i * hb - 1, 0), 0)),
                  pl.BlockSpec((HALO, D), lambda i: (jnp.minimum((i + 1) * hb, n_tiles * hb - 1), 0)),
                  pl.BlockSpec(vecs.shape, lambda i: (0, 0))],
        out_specs=[pl.BlockSpec((TM, 2 * D), lambda i: (i, 0)), pl.BlockSpec((8, D), lambda i: (0, 0))],
        out_shape=[jax.ShapeDtypeStruct((seq, 2 * D), BF16), jax.ShapeDtypeStruct((8, D), F32)],
        compiler_params=_cparams(dimension_semantics=("arbitrary",)),
    )(p_ext, p_ext, dconv, dconv, dconv, vecs)


def _input_backward(dpa, dpb, dq, dk, dv, w_in, x, ctx, dx1, modx, modc, norm_w):
    seq = x.shape[0]
    lext = seq + ctx.shape[0]
    n_x = seq // TM

    def body(dpa_ref, dpb_ref, dq_ref, dk_ref, dv_ref, w_ref, x_ref, ctx_ref, dx1_ref, mx_ref, mc_ref, nw_ref,
             dp_ref, gx_ref, part_ref):
        i = pl.program_id(0)
        is_ctx = i >= n_x
        qk = N_HEADS * DK

        @pl.when(jnp.logical_not(is_ctx))
        def _():
            dp_ref[:, CB_H * D:(CB_H + 1) * D] = dpb_ref[:, 0:D]
            dp_ref[:, CB_BG * D:(CB_BG + 1) * D] = dpa_ref[:, 0:D]
            dp_ref[:, CB_CG * D:(CB_CG + 1) * D] = dpb_ref[:, D:2 * D]
            dp_ref[:, CB_ZA * D:(CB_ZA + 1) * D] = dpa_ref[:, D:2 * D]
            dp_ref[:, CB_ZB * D:(CB_GB + 1) * D] = dpa_ref[:, 2 * D:5 * D]

        @pl.when(is_ctx)
        def _():
            dp_ref[:, 0:CB_QK * D] = jnp.zeros((TM, CB_QK * D), BF16)
            dp_ref[:, CB_ZB * D:PW] = jnp.zeros((TM, PW - CB_ZB * D), BF16)

        dp_ref[:, CB_QK * D:CB_QK * D + qk] = dq_ref[...]
        dp_ref[:, CB_QK * D + qk:CB_V * D] = dk_ref[...]
        dp_ref[:, CB_V * D:CB_ZB * D] = dv_ref[...]
        dxm = _dot_nt(dp_ref[:, 0:WSH], w_ref[0])
        for d in range(1, N_DEV):
            dxm = dxm + _dot_nt(dp_ref[:, d * WSH:(d + 1) * WSH], w_ref[d])
        x = jnp.where(is_ctx, ctx_ref[...], x_ref[...])
        r = lax.rsqrt(jnp.mean(x * x, axis=-1, keepdims=True) + EPS)
        xh = x * r
        nw = nw_ref[...]
        sc = jnp.where(is_ctx, mc_ref[1:2, :], mx_ref[1:2, :])
        dxn = dxm * (1.0 + sc)
        dxh = dxn * nw
        dx = r * (dxh - xh * jnp.mean(dxh * xh, axis=-1, keepdims=True))

        @pl.when(jnp.logical_not(is_ctx))
        def _():
            gx_ref[...] = dx1_ref[...] + dx

        @pl.when(i == 0)
        def _():
            part_ref[...] = jnp.zeros((8, D), F32)

        fx = jnp.where(is_ctx, 0.0, 1.0)
        d_shift = jnp.sum(dxm, axis=0, keepdims=True)
        d_scale = jnp.sum(dxm * (xh * nw), axis=0, keepdims=True)
        part_ref[0:1, :] += fx * d_shift
        part_ref[1:2, :] += fx * d_scale
        part_ref[2:3, :] += jnp.sum(dxn * xh, axis=0, keepdims=True)
        part_ref[3:4, :] += (1.0 - fx) * d_shift
        part_ref[4:5, :] += (1.0 - fx) * d_scale

    lat = lambda w: pl.BlockSpec((TM, w), lambda i: (jnp.minimum(i, n_x - 1), 0))
    ext = lambda w: pl.BlockSpec((TM, w), lambda i: (i, 0))
    full = lambda a: pl.BlockSpec(a.shape, lambda i: (0,) * a.ndim)
    return pl.pallas_call(
        body, name="input_backward", grid=(lext // TM,),
        in_specs=[lat(5 * D), lat(2 * D), ext(N_HEADS * DK), ext(N_HEADS * DK), ext(D), full(w_in), lat(D), full(ctx),
                  lat(D), full(modx), full(modc), full(norm_w)],
        out_specs=[ext(PW), lat(D), pl.BlockSpec((8, D), lambda i: (0, 0))],
        out_shape=[jax.ShapeDtypeStruct((lext, PW), BF16), jax.ShapeDtypeStruct((seq, D), F32),
                   jax.ShapeDtypeStruct((8, D), F32)],
        compiler_params=_cparams(dimension_semantics=("arbitrary",)),
    )(dpa, dpb, dq, dk, dv, w_in, x, ctx, dx1, modx, modc, norm_w)


def _weight_grad_in(xm_t, dp, owners, name):
    lext = xm_t.shape[1]
    tk = lext // 2
    n_k = lext // tk

    def body(own_ref, a_ref, b_ref, o_ref, acc):
        k = pl.program_id(1)

        @pl.when(k == 0)
        def _():
            acc[...] = jnp.zeros_like(acc)

        acc[...] += _dot(a_ref[...], b_ref[...])

        @pl.when(k == n_k - 1)
        def _():
            o_ref[0] = acc[...]

    grid_spec = pltpu.PrefetchScalarGridSpec(
        num_scalar_prefetch=1, grid=(4, n_k),
        in_specs=[pl.BlockSpec((D, tk), lambda j, k, own: (0, k)),
                  pl.BlockSpec((tk, WSH), lambda j, k, own: (k, own[j]))],
        out_specs=pl.BlockSpec((1, D, WSH), lambda j, k, own: (j, 0, 0)),
        scratch_shapes=[pltpu.VMEM((D, WSH), F32)])
    return pl.pallas_call(
        body, name=name, grid_spec=grid_spec,
        out_shape=jax.ShapeDtypeStruct((4, D, WSH), F32),
        compiler_params=_cparams(dimension_semantics=("arbitrary", "arbitrary")),
    )(owners, xm_t, dp)


def _weight_grad_square(a_t, b):
    seq = a_t.shape[2]
    tk = seq // 2

    def body(a_ref, b_ref, o_ref, acc):
        k = pl.program_id(1)

        @pl.when(k == 0)
        def _():
            acc[...] = jnp.zeros_like(acc)

        acc[...] += _dot(a_ref[0], b_ref[0])

        @pl.when(k == pl.num_programs(1) - 1)
        def _():
            o_ref[:, 0] = acc[...].reshape(N_DEV, RSH, D)

    return pl.pallas_call(
        body, name="weight_grad_square", grid=(3, seq // tk),
        in_specs=[pl.BlockSpec((1, D, tk), lambda t, k: (t, 0, k)), pl.BlockSpec((1, tk, D), lambda t, k: (t, k, 0))],
        out_specs=pl.BlockSpec((N_DEV, 1, RSH, D), lambda t, k: (0, t, 0, 0)),
        out_shape=jax.ShapeDtypeStruct((N_DEV, 3, RSH, D), F32),
        scratch_shapes=[pltpu.VMEM((D, D), F32)],
        compiler_params=_cparams(dimension_semantics=("arbitrary", "arbitrary")),
    )(a_t, b)


def _rope_tables(seq, ctx_len):
    rows = seq // GRID_W
    row = jnp.repeat(jnp.arange(rows), GRID_W).astype(F32)
    col = jnp.tile(jnp.arange(GRID_W), rows).astype(F32)
    nf = DK // 4
    inv = ROPE_BASE ** (-jnp.arange(nf, dtype=F32) / nf)
    ang = jnp.concatenate([row[:, None] * inv, col[:, None] * inv], axis=-1)
    cos, sin = jnp.cos(ang), jnp.sin(ang)
    cos_t = jnp.tile(cos, (1, 4))
    sin_t = jnp.tile(jnp.concatenate([-sin, sin], axis=-1), (1, 2))
    cos_t = jnp.concatenate([cos_t, jnp.ones((ctx_len, LANES), F32)], axis=0)
    sin_t = jnp.concatenate([sin_t, jnp.zeros((ctx_len, LANES), F32)], axis=0)
    return cos_t, sin_t


def _tie(value, token):
    return lax.optimization_barrier((value, token))[0]


def _pad_lanes(a):
    return jnp.pad(a, ((0, 0), (0, D - a.shape[1])))


def _rows(total, *parts):
    width = max(a.shape[1] for _, a in parts)
    out = None
    for row, a in parts:
        padded = jnp.pad(a, ((row, total - row - a.shape[0]), (0, width - a.shape[1])))
        out = padded if out is None else out + padded
    return out


def kernel(x, c, ctx, c_ctx, norm_w, ada_w, ada_b, w_in, conv_w, conv_b, decay_logit, gn_w, w_a, w_b, w_out, final_norm_w, loss_target, m_c_ctx, m_norm_w, m_ada_w, m_ada_b, m_w_in, m_conv_w, m_conv_b, m_decay_logit, m_gn_w, m_w_a, m_w_b, m_w_out, m_final_norm_w, v_c_ctx, v_norm_w, v_ada_w, v_ada_b, v_w_in, v_conv_w, v_conv_b, v_decay_logit, v_gn_w, v_w_a, v_w_b, v_w_out, v_final_norm_w):
    xi, yi, ci = _coords()
    me = 4 * xi + 2 * yi + ci
    chip = 2 * xi + yi
    seq, ctx_len = x.shape[1], ctx.shape[1]
    assert seq % TM == 0 and seq % RET_C == 0 and ctx_len == TM and seq % GRID_W == 0
    csh = D // N_DEV

    (w_in_all,) = _all_gather_weights([w_in[0].astype(BF16)])
    sq_shards = [w[0].astype(BF16) for w in (w_a, w_b, w_out)]
    w_in_all, sq_shards = lax.optimization_barrier((w_in_all, sq_shards))
    ag_sq = _Exchange("gather_square", sq_shards, [jax.ShapeDtypeStruct((N_DEV, RSH, D), BF16)] * 3,
                      3 * (N_DEV - 1), _gather_plan)

    blk = jnp.pad(c, ((0, 7), (0, 0))) + jnp.pad(conv_w[0], ((1, 4), (0, D - csh)))
    blk = _tie(blk, ag_sq.token)
    got = _all_gather_small(blk, "gather_cond")
    conv_w_all = got[:, 1:4, 0:csh].transpose(1, 0, 2).reshape(3, D)
    c16 = _rows(16, (0, got[:, 0, :]), (N_DEV, c_ctx[None]))
    ada_b_sh = lax.dynamic_slice(ada_b, (0, me * ADA_SH), (1, ADA_SH))
    mod_sh, act16, lg = _modulation(c16, ada_w[0], ada_b_sh, decay_logit[0])
    mod_all = _all_gather_small(mod_sh, "gather_mod").transpose(1, 0, 2).reshape(16, 3 * D)
    modx = lax.dynamic_slice(mod_all, (me, 0), (1, 3 * D)).reshape(3, D)
    modc = mod_all[8].reshape(3, D)

    x2, tgt = x[0], loss_target[0]
    cos_t, sin_t = _rope_tables(seq, ctx_len)
    p_ext, xm_t = _in_projection(x2, ctx[0], modx, modc, norm_w, w_in_all, cos_t, sin_t)
    rn, rstd = _retention_forward(p_ext, lg, seq, ctx_len)
    _, sq_lands = ag_sq.wait(rstd)
    w_a_all, w_b_all, w_out_all = (
        lax.dynamic_update_slice(land, shard[None], (me, 0, 0)).reshape(D, D) for land, shard in zip(sq_lands, sq_shards))
    vecs = _rows(8, (0, modx[2:3]), (1, conv_w_all), (4, conv_b), (5, gn_w), (6, final_norm_w[None]))
    dx1, dpa, dconv, dret, op_at, op_b, part_m = _merge(p_ext, rn, rstd, x2, tgt, w_a_all, w_b_all, w_out_all, vecs, seq)

    j4 = jnp.arange(4, dtype=jnp.int32)
    owners = (2 * jnp.bitwise_xor(chip, j4) + ci).astype(jnp.int32)
    owners_sib = (2 * jnp.bitwise_xor(chip, j4) + (1 - ci)).astype(jnp.int32)
    gw_sq = _weight_grad_square(op_at, op_b).reshape(N_DEV, 3 * RSH, D)
    rs_sq_pair = _Exchange("rs_square_pair", [gw_sq], [jax.ShapeDtypeStruct((4, 3 * RSH, D), F32)], 4,
                           _pair_plan(lambda j, chip_, c_: 2 * jnp.bitwise_xor(chip_, j) + (1 - c_)))
    dpb, part_c = _conv_backward(p_ext, _tie(dconv, rs_sq_pair.token), vecs, seq)
    (gw_sq,), (r1_sq,) = rs_sq_pair.wait(part_c)
    own_sq, send_sq = _pair_sum(gw_sq, r1_sq, owners, "pair_sum_square")
    rs_sq_chips = _Exchange("rs_square_chips", [send_sq], [jax.ShapeDtypeStruct((3, 3 * RSH, D), BF16)], 3, _chips_plan)
    dq, dk, dv, dlg = _retention_backward(p_ext, _tie(dret, rs_sq_chips.token), lg, cos_t, sin_t, seq, ctx_len)
    dp, grad_x, part_i = _input_backward(dpa, dpb, dq, dk, dv, w_in_all, x2, ctx[0], dx1, modx, modc, norm_w)
    gw_sib = _weight_grad_in(xm_t, dp, owners_sib, "weight_grad_in_sibling")
    rs_in_pair = _Exchange("rs_in_pair", [gw_sib], [jax.ShapeDtypeStruct((4, D, WSH), F32)], 4,
                           _pair_plan(lambda j, chip_, c_: j))
    gw_own = _weight_grad_in(_tie(xm_t, rs_in_pair.token), dp, owners, "weight_grad_in_own")
    _, (r1_in,) = rs_in_pair.wait(gw_own)
    own_in, send_in = _pair_sum(gw_own, r1_in, j4, "pair_sum_in")
    rs_in_chips = _Exchange("rs_in_chips", [send_in], [jax.ShapeDtypeStruct((3, D, WSH), BF16)], 3, _chips_plan)

    dl = _tie(dlg, rs_in_chips.token)[:, 0, 0:4]
    dlg_row = jnp.pad(dl[:, 0::2].reshape(1, N_HEADS), ((0, 0), (0, D - N_HEADS))) + jnp.pad(
        dl[:, 1::2].reshape(1, N_HEADS), ((0, 0), (N_HEADS, D - 2 * N_HEADS)))
    partials = _rows(16, (0, part_m[0:1]), (1, part_i[2:3]), (2, part_c[3:4]), (3, part_m[2:3]), (4, part_c[0:3]),
                     (7, part_i[0:2]), (9, part_m[1:2]), (10, part_i[3:5]), (13, part_m[3:4]), (14, dlg_row))
    got = _all_gather_small(partials, "gather_partials")
    tot = _sum_devices(got, "sum_partials")
    dmodc = tot[10:13].reshape(1, 3 * D)
    dmod16 = _rows(16, (0, got[:, 7:10, :].reshape(N_DEV, 3 * D)), (N_DEV, dmodc))
    dmod16 = lax.dynamic_slice(dmod16, (0, me * ADA_SH), (16, ADA_SH))
    dmodc8 = _rows(8, (0, dmod16[8:9]))
    g_ada_w, d_ada_w, nm_ada_w, nv_ada_w, cctx_part = _ada_backward(
        act16.T, dmod16, dmodc8, ada_w[0], m_ada_w[0], v_ada_w[0])
    dact_ctx = _sum_devices(_all_gather_small(cctx_part, "gather_cctx"), "sum_cctx")[0:1]

    my_lanes = lambda a: _pad_lanes(lax.dynamic_slice(a, (0, me * csh), (a.shape[0], csh)))
    small = lambda cc, nw, cb, gn, fn, ab, dlt, cw: _rows(
        16, (0, cc[None]), (1, nw), (2, cb), (3, gn), (4, fn[None]), (5, ab.reshape(3, D)),
        (8, dlt.reshape(1, 2 * N_HEADS)), (9, cw[0]))
    w_sm = small(c_ctx, norm_w, conv_b, gn_w, final_norm_w, ada_b, decay_logit, conv_w)
    m_sm = small(m_c_ctx, m_norm_w, m_conv_b, m_gn_w, m_final_norm_w, m_ada_b, m_decay_logit, m_conv_w)
    v_sm = small(v_c_ctx, v_norm_w, v_conv_b, v_gn_w, v_final_norm_w, v_ada_b, v_decay_logit, v_conv_w)
    g_sm = _rows(16, (0, dact_ctx), (1, tot[1:4]), (4, tot[0:1]), (5, tot[7:10]), (8, tot[14:15]), (9, my_lanes(tot[4:7])))
    g_sm2 = _rows(16, (5, tot[10:13]))
    sm = _adam_small(w_sm, g_sm, g_sm2, m_sm, v_sm, 0, 8)

    _, (r2_sq,) = rs_sq_chips.wait(sm[0])
    _, (r2_in,) = rs_in_chips.wait(sm[0])
    g_w_in, d_w_in, nm_w_in, nv_w_in = _adam_sharded(w_in[0], m_w_in[0], v_w_in[0], own_in, r2_in, "adam_w_in")
    stack = lambda a, b, c_: jnp.concatenate([a[0], b[0], c_[0]], axis=0)
    sq = _adam_sharded(stack(w_a, w_b, w_out), stack(m_w_a, m_w_b, m_w_out), stack(v_w_a, v_w_b, v_w_out),
                       own_sq, r2_sq, "adam_square")
    sq = [o.reshape(3, 1, RSH, D) for o in sq]

    def unpack(a, sq_i):
        return [a[0], a[1:2], None, a[5:8].reshape(1, 3 * D), None, a[9:12, 0:csh][None], a[2:3],
                a[8, 0:2 * N_HEADS].reshape(1, 2, N_HEADS), a[3:4], sq_i[0], sq_i[1], sq_i[2], a[4]]

    outs = []
    for a, sq_i, ada_i, in_i in zip(sm, sq, (g_ada_w, d_ada_w, nm_ada_w, nv_ada_w), (g_w_in, d_w_in, nm_w_in, nv_w_in)):
        group = unpack(a, sq_i)
        group[2] = ada_i[None]
        group[4] = in_i[None]
        outs += group
    return (tot[13, 0], grad_x[None], *outs)
```

```python
import functools
import math

import jax
import jax.numpy as jnp
from jax import lax
from jax.experimental import pallas as pl
from jax.experimental.pallas import tpu as pltpu

F32 = jnp.float32
BF16 = jnp.bfloat16
MESH = pl.DeviceIdType.MESH
_HBM_SPEC = pl.BlockSpec(memory_space=pltpu.HBM)
_SEM_SPEC = pl.BlockSpec(memory_space=pltpu.SEMAPHORE)
_EFFECT = pltpu.SideEffectType.DATAFLOW_SIDE_EFFECTING

N_DEV = 8
D = 1024
N_HEADS = 8
DK = 64
DV = 128
GRID_W = 64
ROPE_BASE = 10000.0
EPS = 1e-6
PW = 9 * D
WSH = PW // N_DEV
RSH = D // N_DEV
ADA_SH = 3 * D // N_DEV
TM = 256
RET_C = 256
HALO = 16
LANES = 128
VMEM_LIMIT = 60 * 1024 * 1024

ADAM_LR = 0.001
ADAM_B1 = 0.9
ADAM_B2 = 0.999
ADAM_EPS = 1e-08
ADAM_WD = 0.01
ADAM_STEP = 10

CB_H, CB_BG, CB_CG, CB_ZA, CB_QK, CB_V, CB_ZB, CB_GA, CB_GB = range(9)


def _cparams(**kw):
    return pltpu.CompilerParams(vmem_limit_bytes=VMEM_LIMIT, **kw)


def _dot(a, b):
    return jnp.dot(a, b, preferred_element_type=F32)


def _dot_nt(a, b):
    return lax.dot_general(a, b, (((1,), (1,)), ((), ())), preferred_element_type=F32)


def _dot_tn(a, b):
    return lax.dot_general(a, b, (((0,), (0,)), ((), ())), preferred_element_type=F32)


def _sigmoid(z):
    return 1.0 / (1.0 + jnp.exp(-z))


def _row_tile(rows):
    return TM if rows % TM == 0 else rows


def _coords():
    return lax.axis_index("x"), lax.axis_index("y"), lax.axis_index("c")


def _flip(v, bit):
    return 1 - v if bit else v


def _all_gather_small(blk, name):
    rows, cols = blk.shape

    def body(x_ref, out_ref, send_sems, recv_sems, local_sem):
        x, y, c = _coords()
        me = 4 * x + 2 * y + c
        mine = pltpu.make_async_copy(x_ref, out_ref.at[me], local_sem)
        mine.start()

        def copy(k, slot):
            peer = (_flip(x, k & 4), _flip(y, k & 2), _flip(c, k & 1))
            return pltpu.make_async_remote_copy(
                src_ref=x_ref, dst_ref=out_ref.at[slot], send_sem=send_sems.at[k - 1],
                recv_sem=recv_sems.at[k - 1], device_id=peer, device_id_type=MESH)

        for k in range(1, N_DEV):
            copy(k, me).start()
        for k in range(1, N_DEV):
            copy(k, jnp.bitwise_xor(me, k)).wait_recv()
        for k in range(1, N_DEV):
            copy(k, me).wait_send()
        mine.wait()

    return pl.pallas_call(
        body, name=name,
        out_shape=jax.ShapeDtypeStruct((N_DEV, rows, cols), blk.dtype),
        in_specs=[pl.BlockSpec(memory_space=pltpu.VMEM)],
        out_specs=pl.BlockSpec(memory_space=pltpu.VMEM),
        scratch_shapes=[pltpu.SemaphoreType.DMA((N_DEV - 1,)), pltpu.SemaphoreType.DMA((N_DEV - 1,)),
                        pltpu.SemaphoreType.DMA],
    )(blk)


def _all_gather_weights(shards):
    n = len(shards)

    def body(*refs):
        srcs, outs, done = refs[:n], refs[n:2 * n], refs[2 * n]
        send_sems, recv_sems, local_sems = refs[2 * n + 1:]
        x, y, c = _coords()
        me = 4 * x + 2 * y + c
        sib = (x, y, 1 - c)
        chips = [(1 - x, y), (x, 1 - y), (1 - x, 1 - y)]

        def slot(px, py, pc):
            return 4 * px + 2 * py + pc

        def copy(a, k, block_slot, to, own=False):
            return pltpu.make_async_remote_copy(
                src_ref=srcs[a] if own else outs[a].at[block_slot], dst_ref=outs[a].at[block_slot],
                send_sem=send_sems.at[7 * a + k], recv_sem=recv_sems.at[7 * a + k],
                device_id=to, device_id_type=MESH)

        locals_ = [pltpu.make_async_copy(srcs[a], outs[a].at[me], local_sems.at[a]) for a in range(n)]
        for cp in locals_:
            cp.start()
        sent = []
        for a in range(n):
            sent.append(copy(a, 0, me, sib, own=True))
            for j, chip in enumerate(chips):
                sent.append(copy(a, 1 + j, me, (*chip, c), own=True))
        for cp in sent:
            cp.start()
        for j, chip in enumerate(chips):
            for a in range(n):
                copy(a, 1 + j, slot(*chip, c), sib).wait_recv()
                fwd = copy(a, 4 + j, slot(*chip, c), sib)
                fwd.start()
                sent.append(fwd)
        for a in range(n):
            copy(a, 0, slot(x, y, 1 - c), sib).wait_recv()
            for j, chip in enumerate(chips):
                copy(a, 4 + j, slot(*chip, 1 - c), sib).wait_recv()
        for cp in sent:
            cp.wait_send()
        for cp in locals_:
            cp.wait()
        done[...] = jnp.zeros((8, LANES), F32)

    anyspec = pl.BlockSpec(memory_space=pl.ANY)
    return pl.pallas_call(
        body, name="all_gather_weights",
        out_shape=[jax.ShapeDtypeStruct((N_DEV,) + s.shape, s.dtype) for s in shards]
        + [jax.ShapeDtypeStruct((8, LANES), F32)],
        in_specs=[anyspec] * n, out_specs=[anyspec] * n + [pl.BlockSpec(memory_space=pltpu.VMEM)],
        scratch_shapes=[pltpu.SemaphoreType.DMA((7 * n,)), pltpu.SemaphoreType.DMA((7 * n,)),
                        pltpu.SemaphoreType.DMA((n,))],
    )(*shards)


class _Exchange:
    def __init__(self, name, srcs, land_shapes, n_copies, plan):
        self.name, self.plan, self.n_copies = name, plan, n_copies
        self.n_src, self.n_land = len(srcs), len(land_shapes)
        hbm = lambda a: pltpu.HBM(a.shape, a.dtype)
        n = self.n_src + self.n_land
        lands = [pltpu.with_memory_space_constraint(lax.empty(s.shape, s.dtype), pltpu.HBM) for s in land_shapes]
        srcs = [pltpu.with_memory_space_constraint(s, pltpu.HBM) for s in srcs]

        def body(*refs):
            send_sems, recv_sems = refs[n], refs[n + 1]
            for cp in self._copies(refs, send_sems, recv_sems):
                cp.start()
            refs[-1][...] = jnp.zeros((8, LANES), F32)

        outs = pl.pallas_call(
            body, name=name + "_start",
            out_shape=(pltpu.SemaphoreType.DMA((n_copies,)), pltpu.SemaphoreType.DMA((n_copies,)),
                       *[hbm(a) for a in srcs], *[hbm(a) for a in lands], jax.ShapeDtypeStruct((8, LANES), F32)),
            in_specs=[_HBM_SPEC] * n,
            out_specs=(_SEM_SPEC, _SEM_SPEC, *[_HBM_SPEC] * n, pl.BlockSpec(memory_space=pltpu.VMEM)),
            input_output_aliases={i: 2 + i for i in range(n)},
            compiler_params=pltpu.CompilerParams(has_side_effects=_EFFECT),
        )(*srcs, *lands)
        self.send_sems, self.recv_sems = outs[0], outs[1]
        self.buffers = list(outs[2:2 + n])
        self.token = outs[-1]

    def _copies(self, refs, send_sems, recv_sems):
        src_refs, land_refs = refs[:self.n_src], refs[self.n_src:self.n_src + self.n_land]
        return [pltpu.make_async_remote_copy(src_ref=s, dst_ref=d, send_sem=send_sems.at[k], recv_sem=recv_sems.at[k],
                                             device_id=dev, device_id_type=MESH)
                for k, (s, d, dev) in enumerate(self.plan(src_refs, land_refs))]

    def wait(self, after):
        n = self.n_src + self.n_land

        def body(*refs):
            for cp in self._copies(refs, refs[n], refs[n + 1]):
                cp.wait_send()
                cp.wait_recv()

        outs = pl.pallas_call(
            body, name=self.name + "_wait",
            out_shape=tuple(pltpu.HBM(b.shape, b.dtype) for b in self.buffers),
            in_specs=[_HBM_SPEC] * n + [_SEM_SPEC, _SEM_SPEC, pl.BlockSpec(memory_space=pl.ANY)],
            out_specs=tuple([_HBM_SPEC] * n),
            input_output_aliases={i: i for i in range(n)},
            compiler_params=pltpu.CompilerParams(has_side_effects=_EFFECT),
        )(*self.buffers, self.send_sems, self.recv_sems, after)
        return list(outs[:self.n_src]), list(outs[self.n_src:])


def _pair_plan(src_index):
    def plan(srcs, lands):
        x, y, c = _coords()
        chip = 2 * x + y
        return [(srcs[0].at[src_index(j, chip, c)], lands[0].at[j], (x, y, 1 - c)) for j in range(4)]
    return plan


def _chips_plan(srcs, lands):
    x, y, c = _coords()
    return [(srcs[0].at[j - 1], lands[0].at[j - 1], (_flip(x, j & 2), _flip(y, j & 1), c)) for j in range(1, 4)]


def _gather_plan(srcs, lands):
    x, y, c = _coords()
    me = 4 * x + 2 * y + c
    return [(srcs[a], lands[a].at[me], (_flip(x, k & 4), _flip(y, k & 2), _flip(c, k & 1)))
            for a in range(len(srcs)) for k in range(1, N_DEV)]


def _pair_sum(grad, recv, owners, name):
    _, rows, cols = grad.shape
    tr = _row_tile(rows)

    def body(own_ref, g_ref, r_ref, mine_ref, send_ref):
        j = pl.program_id(1)
        total = g_ref[...] + r_ref[...]

        @pl.when(j == 0)
        def _():
            mine_ref[...] = total

        @pl.when(j > 0)
        def _():
            send_ref[...] = total.astype(BF16)

    grid_spec = pltpu.PrefetchScalarGridSpec(
        num_scalar_prefetch=1, grid=(rows // tr, 4),
        in_specs=[pl.BlockSpec((1, tr, cols), lambda i, j, own: (own[j], i, 0)),
                  pl.BlockSpec((1, tr, cols), lambda i, j, own: (j, i, 0))],
        out_specs=[pl.BlockSpec((1, tr, cols), lambda i, j, own: (0, i, 0)),
                   pl.BlockSpec((1, tr, cols), lambda i, j, own: (jnp.maximum(j - 1, 0), i, 0))])
    return pl.pallas_call(
        body, name=name, grid_spec=grid_spec,
        out_shape=[jax.ShapeDtypeStruct((1, rows, cols), F32), jax.ShapeDtypeStruct((3, rows, cols), BF16)],
        compiler_params=_cparams(dimension_semantics=("arbitrary", "arbitrary")),
    )(owners, grad, recv)


def _modulation(c16, ada_w_sh, ada_b_sh, decay_logit):
    def body(c_ref, w_ref, b_ref, dl_ref, mod_ref, act_ref, lg_ref):
        cv = c_ref[...]
        act = cv * _sigmoid(cv)
        act_ref[...] = act
        mod_ref[...] = jnp.dot(act, w_ref[...], preferred_element_type=F32,
                               precision=lax.Precision.HIGHEST) + b_ref[...]
        z = dl_ref[...]
        lg_ref[...] = jnp.minimum(z, 0.0) - jnp.log(1.0 + jnp.exp(-jnp.abs(z)))

    return pl.pallas_call(
        body, name="modulation",
        out_shape=[jax.ShapeDtypeStruct((16, ADA_SH), F32), jax.ShapeDtypeStruct((16, D), F32),
                   jax.ShapeDtypeStruct(decay_logit.shape, F32)],
        compiler_params=_cparams(),
    )(c16, ada_w_sh, ada_b_sh, decay_logit)


def _adam_update(w, g, m, v):
    m2 = ADAM_B1 * m + (1.0 - ADAM_B1) * g
    v2 = ADAM_B2 * v + (1.0 - ADAM_B2) * (g * g)
    m_hat = m2 / (1.0 - ADAM_B1 ** ADAM_STEP)
    v_hat = v2 / (1.0 - ADAM_B2 ** ADAM_STEP)
    delta = -ADAM_LR * (m_hat / (jnp.sqrt(v_hat) + ADAM_EPS) + ADAM_WD * w)
    return delta, m2, v2


def _adam_sharded(w, m, v, own, recv, name):
    rows, cols = w.shape
    tr = _row_tile(rows)

    def body(w_ref, m_ref, v_ref, g0, g1, g2, g3, g_ref, d_ref, m_out, v_out):
        g = ((g0[0] + g1[0].astype(F32)) + g2[0].astype(F32)) + g3[0].astype(F32)
        delta, m2, v2 = _adam_update(w_ref[...], g, m_ref[...], v_ref[...])
        g_ref[...] = g
        d_ref[...] = delta
        m_out[...] = m2
        v_out[...] = v2

    flat = pl.BlockSpec((tr, cols), lambda i: (i, 0))
    part = lambda j: pl.BlockSpec((1, tr, cols), lambda i, j=j: (j, i, 0))
    return pl.pallas_call(
        body, name=name, grid=(rows // tr,),
        in_specs=[flat, flat, flat, part(0), part(0), part(1), part(2)],
        out_specs=[flat] * 4,
        out_shape=[jax.ShapeDtypeStruct((rows, cols), F32)] * 4,
        compiler_params=_cparams(dimension_semantics=("arbitrary",)),
    )(w, m, v, own, recv, recv, recv)


def _ada_backward(act_t, dmod16, dmodc8, ada_w_sh, m, v):
    def body(at_ref, dm_ref, dc_ref, w_ref, m_ref, v_ref, g_ref, d_ref, m_out, v_out, pc_ref):
        g = jnp.dot(at_ref[...], dm_ref[...], preferred_element_type=F32, precision=lax.Precision.HIGHEST)
        w = w_ref[...]
        delta, m2, v2 = _adam_update(w, g, m_ref[...], v_ref[...])
        g_ref[...] = g
        d_ref[...] = delta
        m_out[...] = m2
        v_out[...] = v2
        pc_ref[...] = lax.dot_general(dc_ref[...], w, (((1,), (1,)), ((), ())), preferred_element_type=F32,
                                      precision=lax.Precision.HIGHEST)

    return pl.pallas_call(
        body, name="ada_backward",
        out_shape=[jax.ShapeDtypeStruct((D, ADA_SH), F32)] * 4 + [jax.ShapeDtypeStruct((8, D), F32)],
        compiler_params=_cparams(),
    )(act_t, dmod16, dmodc8, ada_w_sh, m, v)


def _sum_devices(gathered, name):
    _, rows, cols = gathered.shape

    def body(g_ref, o_ref):
        acc = g_ref[0]
        for d in range(1, N_DEV):
            acc = acc + g_ref[d]
        o_ref[...] = acc

    return pl.pallas_call(body, name=name, out_shape=jax.ShapeDtypeStruct((rows, cols), F32),
                          compiler_params=_cparams())(gathered)


def _adam_small(w, g_raw, g_raw2, m, v, silu_row, sigm_row):
    rows, cols = w.shape

    def body(w_ref, g_ref, g2_ref, m_ref, v_ref, g_out, d_ref, m_out, v_out):
        wv = w_ref[...]
        row = lax.broadcasted_iota(jnp.int32, (rows, cols), 0)
        s = _sigmoid(wv)
        dsilu = s * (1.0 + wv * (1.0 - s))
        scale = jnp.where(row == silu_row, dsilu, jnp.where(row == sigm_row, 1.0 - s, 1.0))
        g = (g_ref[...] + g2_ref[...]) * scale
        delta, m2, v2 = _adam_update(wv, g, m_ref[...], v_ref[...])
        g_out[...] = g
        d_ref[...] = delta
        m_out[...] = m2
        v_out[...] = v2

    return pl.pallas_call(body, name="adam_small", out_shape=[jax.ShapeDtypeStruct((rows, cols), F32)] * 4,
                          compiler_params=_cparams())(w, g_raw, g_raw2, m, v)


def _rope(t, cos, sin):
    lane = lax.broadcasted_iota(jnp.int32, (1, LANES), 1)
    first_half = jnp.bitwise_and(lane, DK // 2) == 0
    partner = jnp.where(first_half, pltpu.roll(t, LANES - DK // 2, 1), pltpu.roll(t, DK // 2, 1))
    return t * cos + partner * sin


def _in_projection(x, ctx, modx, modc, norm_w, w_in, cos_t, sin_t):
    n_x_tiles = x.shape[0] // TM
    lext = x.shape[0] + ctx.shape[0]
    q_lo, k_lo = 4 * D, 4 * D + N_HEADS * DK

    def body(x_ref, ctx_ref, mx_ref, mc_ref, nw_ref, w_ref, cos_ref, sin_ref, p_ref, xmt_ref):
        is_ctx = pl.program_id(0) >= n_x_tiles
        x = jnp.where(is_ctx, ctx_ref[...], x_ref[...])
        r = lax.rsqrt(jnp.mean(x * x, axis=-1, keepdims=True) + EPS)
        sh = jnp.where(is_ctx, mc_ref[0:1, :], mx_ref[0:1, :])
        sc = jnp.where(is_ctx, mc_ref[1:2, :], mx_ref[1:2, :])
        xm = (x * r * nw_ref[...]) * (1.0 + sc) + sh
        xmb = xm.astype(BF16)
        xmt_ref[...] = xm.T.astype(BF16)
        cos, sin = cos_ref[...], sin_ref[...]
        for d in range(N_DEV):
            acc = _dot(xmb, w_ref[d])
            lo = d * WSH
            col = 0
            while col < WSH:
                g = lo + col
                if q_lo <= g < k_lo + N_HEADS * DK:
                    piece = acc[:, col:col + LANES]
                    if g >= k_lo:
                        piece = piece * (DK ** -0.5)
                    p_ref[:, g:g + LANES] = _rope(piece, cos, sin).astype(BF16)
                    col += LANES
                else:
                    if g < q_lo:
                        end = min(WSH, q_lo - lo)
                    else:
                        end = WSH
                    p_ref[:, g:lo + end] = acc[:, col:end].astype(BF16)
                    col = end

    row = lambda w: pl.BlockSpec((TM, w), lambda i: (i, 0))
    full = lambda a: pl.BlockSpec(a.shape, lambda i: (0,) * a.ndim)
    return pl.pallas_call(
        body, name="in_projection", grid=(lext // TM,),
        in_specs=[pl.BlockSpec((TM, D), lambda i: (jnp.minimum(i, n_x_tiles - 1), 0)), full(ctx),
                  full(modx), full(modc), full(norm_w), full(w_in), row(LANES), row(LANES)],
        out_specs=[row(PW), pl.BlockSpec((D, TM), lambda i: (0, i))],
        out_shape=[jax.ShapeDtypeStruct((lext, PW), BF16), jax.ShapeDtypeStruct((D, lext), BF16)],
        compiler_params=_cparams(dimension_semantics=("arbitrary",)),
    )(x, ctx, modx, modc, norm_w, w_in, cos_t, sin_t)


def _decay_tables(lgf, lgb, n):
    i = lax.broadcasted_iota(jnp.int32, (n, 1), 0).astype(F32)
    return dict(i=i, k_f=jnp.exp(lgf * (n - 1.0 - i)), k_b=jnp.exp(lgb * i),
                q_f=jnp.exp(lgf * (i + 1.0)), q_b=jnp.exp(lgb * (n - i)))


def _decay_matrix(lgf, lgb, n, transposed=False):
    ii = lax.broadcasted_iota(jnp.int32, (n, n), 0)
    jj = lax.broadcasted_iota(jnp.int32, (n, n), 1)
    diff = (jj - ii if transposed else ii - jj).astype(F32)
    low = jnp.exp(lgf * jnp.maximum(diff, 0.0))
    up = jnp.exp(lgb * jnp.maximum(-diff, 0.0))
    return jnp.where(diff > 0, low, jnp.where(diff < 0, up, 2.0)), diff


def _cat_lanes(a, b):
    return jnp.concatenate([a.astype(BF16), b.astype(BF16)], axis=1)


def _retention_forward(p_ext, lg, seq, ctx_len):
    lext = seq + ctx_len
    n_chunks = seq // RET_C
    C = RET_C

    def body(lg_ref, q_ref, k_ref, v_ref, rn_ref, rstd_ref, kv_scr, sf_scr, sb_scr):
        pair = pl.program_id(0)
        lane = lax.broadcasted_iota(jnp.int32, (1, LANES), 1)
        for hh in range(2):
            hmask = (lane // DK == hh).astype(F32)
            lgf = lg_ref[0, 2 * pair + hh]
            lgb = lg_ref[1, 2 * pair + hh]
            vs = slice(hh * DV, (hh + 1) * DV)
            tc = _decay_tables(lgf, lgb, ctx_len)
            kc = k_ref[seq:lext, :].astype(F32) * hmask
            s0 = _dot_tn(_cat_lanes(kc * tc["k_f"], kc * tc["k_b"]), v_ref[seq:lext, vs])
            t = _decay_tables(lgf, lgb, C)

            def increments(c, carry):
                rows = pl.ds(pl.multiple_of(c * C, C), C)
                k = k_ref[rows, :].astype(F32) * hmask
                kv_scr[c] = _dot_tn(_cat_lanes(k * t["k_f"], k * t["k_b"]), v_ref[rows, vs])
                return carry

            lax.fori_loop(0, n_chunks, increments, 0, unroll=2)
            gf_c, gb_c = jnp.exp(lgf * C), jnp.exp(lgb * C)

            def scan_f(c, s):
                sf_scr[c] = s
                return gf_c * s + kv_scr[c, 0:LANES, :]

            def scan_b(n, s):
                c = n_chunks - 1 - n
                sb_scr[c] = s
                return gb_c * s + kv_scr[c, LANES:2 * LANES, :]

            lax.fori_loop(0, n_chunks, scan_f, s0[0:LANES])
            lax.fori_loop(0, n_chunks, scan_b, s0[LANES:2 * LANES])
            dmat, _ = _decay_matrix(lgf, lgb, C)

            def outputs(c, carry):
                rows = pl.ds(pl.multiple_of(c * C, C), C)
                q = q_ref[rows, :].astype(F32) * hmask
                v = v_ref[rows, vs]
                s = _dot_nt(q.astype(BF16), k_ref[rows, :])
                inner = _dot((s * dmat).astype(BF16), v)
                states = jnp.concatenate([sf_scr[c], sb_scr[c]], axis=0).astype(BF16)
                ret = inner + _dot(_cat_lanes(q * t["q_f"], q * t["q_b"]), states)
                mu = jnp.mean(ret, axis=-1, keepdims=True)
                cen = ret - mu
                rstd = lax.rsqrt(jnp.mean(cen * cen, axis=-1, keepdims=True) + EPS)
                rn_ref[rows, vs] = cen * rstd
                rstd_ref[rows, vs] = jnp.broadcast_to(rstd, (C, DV))
                return carry

            lax.fori_loop(0, n_chunks, outputs, 0, unroll=2)

    qk0 = CB_QK * D // LANES
    return pl.pallas_call(
        body, name="retention_forward", grid=(N_HEADS // 2,),
        in_specs=[pl.BlockSpec(memory_space=pltpu.SMEM),
                  pl.BlockSpec((lext, LANES), lambda g: (0, qk0 + g)),
                  pl.BlockSpec((lext, LANES), lambda g: (0, qk0 + N_HEADS // 2 + g)),
                  pl.BlockSpec((lext, 2 * DV), lambda g: (0, CB_V * D // (2 * DV) + g))],
        out_specs=[pl.BlockSpec((seq, 2 * DV), lambda g: (0, g))] * 2,
        out_shape=[jax.ShapeDtypeStruct((seq, D), F32)] * 2,
        scratch_shapes=[pltpu.VMEM((n_chunks, 2 * LANES, DV), F32), pltpu.VMEM((n_chunks, LANES, DV), F32),
                        pltpu.VMEM((n_chunks, LANES, DV), F32)],
        compiler_params=_cparams(dimension_semantics=("arbitrary",)),
    )(lg, p_ext, p_ext, p_ext)


def _retention_backward(p_ext, dret, lg, cos_t, sin_t, seq, ctx_len):
    lext = seq + ctx_len
    n_chunks = seq // RET_C
    C = RET_C

    def body(lg_ref, q_ref, k_ref, v_ref, do_ref, cos_ref, sin_ref, dq_ref, dk_ref, dv_ref, dlg_ref,
             kv_scr, g_scr, sf_scr, sb_scr, gfn_scr, gbp_scr):
        pair = pl.program_id(0)
        lane = lax.broadcasted_iota(jnp.int32, (1, LANES), 1)
        dlg = []
        for hh in range(2):
            in_head = lane // DK == hh
            hmask = in_head.astype(F32)
            lgf = lg_ref[0, 2 * pair + hh]
            lgb = lg_ref[1, 2 * pair + hh]
            vs = slice(hh * DV, (hh + 1) * DV)
            tc = _decay_tables(lgf, lgb, ctx_len)
            kc = k_ref[seq:lext, :].astype(F32) * hmask
            vc = v_ref[seq:lext, vs]
            kc_cat = _cat_lanes(kc * tc["k_f"], kc * tc["k_b"])
            s0 = _dot_tn(kc_cat, vc)
            t = _decay_tables(lgf, lgb, C)

            def increments(c, carry):
                rows = pl.ds(pl.multiple_of(c * C, C), C)
                k = k_ref[rows, :].astype(F32) * hmask
                q = q_ref[rows, :].astype(F32) * hmask
                kv_scr[c] = _dot_tn(_cat_lanes(k * t["k_f"], k * t["k_b"]), v_ref[rows, vs])
                g_scr[c] = _dot_tn(_cat_lanes(q * t["q_f"], q * t["q_b"]), do_ref[rows, vs])
                return carry

            lax.fori_loop(0, n_chunks, increments, 0, unroll=2)
            gf_c, gb_c = jnp.exp(lgf * C), jnp.exp(lgb * C)

            def scan_f(c, s):
                sf_scr[c] = s
                return gf_c * s + kv_scr[c, 0:LANES, :]

            def scan_b(n, s):
                c = n_chunks - 1 - n
                sb_scr[c] = s
                return gb_c * s + kv_scr[c, LANES:2 * LANES, :]

            def scan_gf(n, carry):
                c = n_chunks - 1 - n
                gfn_scr[c] = carry
                return g_scr[c, 0:LANES, :] + gf_c * carry

            def scan_gb(c, carry):
                gbp_scr[c] = carry
                return g_scr[c, LANES:2 * LANES, :] + gb_c * carry

            lax.fori_loop(0, n_chunks, scan_f, s0[0:LANES])
            lax.fori_loop(0, n_chunks, scan_b, s0[LANES:2 * LANES])
            zero_state = jnp.zeros((LANES, DV), F32)
            gf_first = lax.fori_loop(0, n_chunks, scan_gf, zero_state)
            gb_last = lax.fori_loop(0, n_chunks, scan_gb, zero_state)

            dmat, diff = _decay_matrix(lgf, lgb, C)
            dmat_t, _ = _decay_matrix(lgf, lgb, C, transposed=True)
            w_f = jnp.where(diff > 0, diff * dmat, 0.0)
            w_b = jnp.where(diff < 0, -diff * dmat, 0.0)
            cos_all, sin_all = cos_ref, sin_ref

            def total(a):
                rows_, width = a.shape
                part = jnp.sum(a.reshape(rows_ // 8, 8, width), axis=0)
                return part[:, 0:LANES] + part[:, LANES:2 * LANES] if width == 2 * LANES else part

            def chunk(c, carry):
                dlgf, dlgb = carry
                rows = pl.ds(pl.multiple_of(c * C, C), C)
                qm = q_ref[rows, :].astype(F32) * hmask
                km = k_ref[rows, :].astype(F32) * hmask
                qb, kb = qm.astype(BF16), km.astype(BF16)
                v = v_ref[rows, vs]
                do = do_ref[rows, vs]
                s = _dot_nt(qb, kb)
                s_t = _dot_nt(kb, qb)
                dsd = _dot_nt(do, v)
                dsd_t = _dot_nt(v, do)
                dq_in = _dot((dsd * dmat).astype(BF16), kb)
                dk_in = _dot((dsd_t * dmat_t).astype(BF16), qb)
                dv_in = _dot((s_t * dmat_t).astype(BF16), do)
                prod = s * dsd
                dlgf = dlgf + total(prod * w_f)
                dlgb = dlgb + total(prod * w_b)
                sf, sb = sf_scr[c], sb_scr[c]
                states = jnp.concatenate([sf, sb], axis=0).astype(BF16)
                dqc = _dot_nt(do, states)
                dqf = dqc[:, 0:LANES] * t["q_f"]
                dqb = dqc[:, LANES:2 * LANES] * t["q_b"]
                dq = (dq_in + dqf + dqb) * hmask
                dlgf = dlgf + total((t["i"] + 1.0) * (qm * dqf))
                dlgb = dlgb + total((C - t["i"]) * (qm * dqb))
                gfn, gbp = gfn_scr[c], gbp_scr[c]
                gstates = jnp.concatenate([gfn, gbp], axis=0).astype(BF16)
                dkc = _dot_nt(v, gstates)
                dkf = dkc[:, 0:LANES] * t["k_f"]
                dkb = dkc[:, LANES:2 * LANES] * t["k_b"]
                dk = dk_in + dkf + dkb
                dlgf = dlgf + total((C - 1.0 - t["i"]) * (km * dkf)) + C * gf_c * total(gfn * sf)
                dlgb = dlgb + total(t["i"] * (km * dkb)) + C * gb_c * total(gbp * sb)
                dv = dv_in + _dot(_cat_lanes(km * t["k_f"], km * t["k_b"]), gstates)
                cos, sin = cos_all[rows, :], sin_all[rows, :]
                dq = _rope(dq, cos, -sin)
                dk = _rope(dk, cos, -sin) * (DK ** -0.5)
                if hh == 0:
                    dq_ref[rows, :] = dq.astype(BF16)
                    dk_ref[rows, :] = dk.astype(BF16)
                else:
                    dq_ref[rows, :] = jnp.where(in_head, dq, dq_ref[rows, :].astype(F32)).astype(BF16)
                    dk_ref[rows, :] = jnp.where(in_head, dk, dk_ref[rows, :].astype(F32)).astype(BF16)
                dv_ref[rows, vs] = dv.astype(BF16)
                return dlgf, dlgb

            zero = jnp.zeros((8, LANES), F32)
            dlgf, dlgb = lax.fori_loop(0, n_chunks, chunk, (zero, zero), unroll=2)

            g0 = jnp.concatenate([gf_first, gb_last], axis=0).astype(BF16)
            dkcc = _dot_nt(vc, g0)
            dkcf = dkcc[:, 0:LANES] * tc["k_f"]
            dkcb = dkcc[:, LANES:2 * LANES] * tc["k_b"]
            dlgf = dlgf + total((ctx_len - 1.0 - tc["i"]) * (kc * dkcf))
            dlgb = dlgb + total(tc["i"] * (kc * dkcb))
            dkctx = (dkcf + dkcb) * (DK ** -0.5)
            if hh == 0:
                dk_ref[seq:lext, :] = dkctx.astype(BF16)
                dq_ref[seq:lext, :] = jnp.zeros((ctx_len, LANES), BF16)
            else:
                dk_ref[seq:lext, :] = jnp.where(in_head, dkctx, dk_ref[seq:lext, :].astype(F32)).astype(BF16)
            dv_ref[seq:lext, vs] = _dot(kc_cat, g0).astype(BF16)
            dlg += [jnp.sum(jnp.sum(a, axis=1, keepdims=True), axis=0, keepdims=True) for a in (dlgf, dlgb)]

        lane8 = lax.broadcasted_iota(jnp.int32, (8, LANES), 1)
        out = jnp.zeros((8, LANES), F32)
        for n, val in enumerate(dlg):
            out = jnp.where(lane8 == n, val, out)
        dlg_ref[0] = out

    qk0 = CB_QK * D // LANES
    q_spec = pl.BlockSpec((lext, LANES), lambda g: (0, qk0 + g))
    k_spec = pl.BlockSpec((lext, LANES), lambda g: (0, qk0 + N_HEADS // 2 + g))
    v_spec = pl.BlockSpec((lext, 2 * DV), lambda g: (0, CB_V * D // (2 * DV) + g))
    table = pl.BlockSpec((lext, LANES), lambda g: (0, 0))
    state = pltpu.VMEM((n_chunks, LANES, DV), F32)
    return pl.pallas_call(
        body, name="retention_backward", grid=(N_HEADS // 2,),
        in_specs=[pl.BlockSpec(memory_space=pltpu.SMEM), q_spec, k_spec, v_spec,
                  pl.BlockSpec((seq, 2 * DV), lambda g: (0, g)), table, table],
        out_specs=[pl.BlockSpec((lext, LANES), lambda g: (0, g)), pl.BlockSpec((lext, LANES), lambda g: (0, g)),
                   pl.BlockSpec((lext, 2 * DV), lambda g: (0, g)), pl.BlockSpec((1, 8, LANES), lambda g: (g, 0, 0))],
        out_shape=[jax.ShapeDtypeStruct((lext, N_HEADS * DK), BF16), jax.ShapeDtypeStruct((lext, N_HEADS * DK), BF16),
                   jax.ShapeDtypeStruct((lext, D), BF16), jax.ShapeDtypeStruct((N_HEADS // 2, 8, LANES), F32)],
        scratch_shapes=[pltpu.VMEM((n_chunks, 2 * LANES, DV), F32), pltpu.VMEM((n_chunks, 2 * LANES, DV), F32),
                        state, state, state, state],
        compiler_params=_cparams(dimension_semantics=("arbitrary",)),
    )(lg, p_ext, p_ext, p_ext, dret, cos_t, sin_t)


def _merge(p_ext, rn, rstd, x, target, w_a, w_b, w_out, vecs, seq):
    n_tiles = seq // TM
    hb = TM // HALO

    def body(h_ref, bg_ref, cg_ref, za_ref, zb_ref, ga_ref, gb_ref, hp_ref, hn_ref, cp_ref, cn_ref,
             rn_ref, rstd_ref, x_ref, t_ref, wa_ref, wb_ref, wo_ref, vec_ref,
             dx1_ref, dpa_ref, dconv_ref, dret_ref, at_ref, b_ref, part_ref):
        i = pl.program_id(0)
        f = lambda ref: ref[...].astype(F32)
        h, bg, cg, za, zb, ga, gb = f(h_ref), f(bg_ref), f(cg_ref), f(za_ref), f(zb_ref), f(ga_ref), f(gb_ref)
        gx, w0, w1, w2 = vec_ref[0:1, :], vec_ref[1:2, :], vec_ref[2:3, :], vec_ref[3:4, :]
        cb, gnw, fw = vec_ref[4:5, :], vec_ref[5:6, :], vec_ref[6:7, :]
        u = cg * h
        row = lax.broadcasted_iota(jnp.int32, (TM, 1), 0)
        u_prev = (f(cp_ref) * f(hp_ref))[HALO - 1:HALO, :]
        u_next = (f(cn_ref) * f(hn_ref))[0:1, :]
        u_prev = jnp.where(i == 0, 0.0, u_prev)
        u_next = jnp.where(i == n_tiles - 1, 0.0, u_next)
        u_up = jnp.where(row == 0, u_prev, pltpu.roll(u, 1, 0))
        u_dn = jnp.where(row == TM - 1, u_next, pltpu.roll(u, TM - 1, 0))
        conv = w0 * u_up + w1 * u + w2 * u_dn + cb
        sza = _sigmoid(za)
        silu_za = za * sza
        a_act = silu_za * bg * conv
        rn = rn_ref[...]
        szb = _sigmoid(zb)
        silu_zb = zb * szb
        rg = rn * gnw
        b_act = silu_zb * rg
        y_a = _dot(a_act.astype(BF16), wa_ref[...])
        y_b = _dot(b_act.astype(BF16), wb_ref[...])
        sga, sgb = _sigmoid(ga), _sigmoid(gb)
        mix = sga * y_a + sgb * y_b
        y = _dot(mix.astype(BF16), wo_ref[...])
        x1 = x_ref[...] + gx * y
        r1 = lax.rsqrt(jnp.mean(x1 * x1, axis=-1, keepdims=True) + EPS)
        xh1 = x1 * r1
        err = xh1 * fw - t_ref[...]
        loss = jnp.sum(jnp.sum(err * err, axis=1, keepdims=True), axis=0, keepdims=True) * (0.5 / D)
        dout = err * (1.0 / D)
        dxh = dout * fw
        dx1 = r1 * (dxh - xh1 * jnp.mean(dxh * xh1, axis=-1, keepdims=True))
        dx1_ref[...] = dx1
        dy = (dx1 * gx).astype(BF16)
        dmix = _dot_nt(dy, wo_ref[...])
        dya = (dmix * sga).astype(BF16)
        dyb = (dmix * sgb).astype(BF16)
        da = _dot_nt(dya, wa_ref[...])
        db = _dot_nt(dyb, wb_ref[...])
        dpa_ref[:, 0:D] = (da * silu_za * conv).astype(BF16)
        dpa_ref[:, D:2 * D] = (da * bg * conv * (sza * (1.0 + za * (1.0 - sza)))).astype(BF16)
        dpa_ref[:, 2 * D:3 * D] = (db * rg * (szb * (1.0 + zb * (1.0 - szb)))).astype(BF16)
        dpa_ref[:, 3 * D:4 * D] = (dmix * y_a * sga * (1.0 - sga)).astype(BF16)
        dpa_ref[:, 4 * D:5 * D] = (dmix * y_b * sgb * (1.0 - sgb)).astype(BF16)
        dconv_ref[...] = (da * silu_za * bg).astype(BF16)
        drn_n = db * silu_zb
        drn = drn_n * gnw
        rstd = rstd_ref[...]
        for hd in range(N_HEADS):
            sl = slice(hd * DV, (hd + 1) * DV)
            dh_, rh = drn[:, sl], rn[:, sl]
            m1 = jnp.mean(dh_, axis=-1, keepdims=True)
            m2 = jnp.mean(dh_ * rh, axis=-1, keepdims=True)
            dret_ref[:, sl] = (rstd[:, sl] * (dh_ - m1 - rh * m2)).astype(BF16)
        at_ref[0] = a_act.T.astype(BF16)
        at_ref[1] = b_act.T.astype(BF16)
        at_ref[2] = mix.T.astype(BF16)
        b_ref[0] = dya
        b_ref[1] = dyb
        b_ref[2] = dy

        @pl.when(i == 0)
        def _():
            part_ref[...] = jnp.zeros((8, D), F32)

        part_ref[0:1, :] += jnp.sum(dout * xh1, axis=0, keepdims=True)
        part_ref[1:2, :] += jnp.sum(dx1 * y, axis=0, keepdims=True)
        part_ref[2:3, :] += jnp.sum(drn_n * rn, axis=0, keepdims=True)
        part_ref[3:4, :] += jnp.broadcast_to(loss, (1, D))

    col = lambda cb_: pl.BlockSpec((TM, D), lambda i, cb_=cb_: (i, cb_))
    prev = lambda cb_: pl.BlockSpec((HALO, D), lambda i, cb_=cb_: (jnp.maximum(i * hb - 1, 0), cb_))
    nxt = lambda cb_: pl.BlockSpec((HALO, D), lambda i, cb_=cb_: (jnp.minimum((i + 1) * hb, n_tiles * hb - 1), cb_))
    tile = pl.BlockSpec((TM, D), lambda i: (i, 0))
    full = lambda a: pl.BlockSpec(a.shape, lambda i: (0,) * a.ndim)
    return pl.pallas_call(
        body, name="merge", grid=(n_tiles,),
        in_specs=[col(CB_H), col(CB_BG), col(CB_CG), col(CB_ZA), col(CB_ZB), col(CB_GA), col(CB_GB),
                  prev(CB_H), nxt(CB_H), prev(CB_CG), nxt(CB_CG),
                  tile, tile, tile, tile, full(w_a), full(w_b), full(w_out), full(vecs)],
        out_specs=[tile, pl.BlockSpec((TM, 5 * D), lambda i: (i, 0)), tile, tile,
                   pl.BlockSpec((3, D, TM), lambda i: (0, 0, i)), pl.BlockSpec((3, TM, D), lambda i: (0, i, 0)),
                   pl.BlockSpec((8, D), lambda i: (0, 0))],
        out_shape=[jax.ShapeDtypeStruct((seq, D), F32), jax.ShapeDtypeStruct((seq, 5 * D), BF16),
                   jax.ShapeDtypeStruct((seq, D), BF16), jax.ShapeDtypeStruct((seq, D), BF16),
                   jax.ShapeDtypeStruct((3, D, seq), BF16), jax.ShapeDtypeStruct((3, seq, D), BF16),
                   jax.ShapeDtypeStruct((8, D), F32)],
        compiler_params=_cparams(dimension_semantics=("arbitrary",)),
    )(p_ext, p_ext, p_ext, p_ext, p_ext, p_ext, p_ext, p_ext, p_ext, p_ext, p_ext,
      rn, rstd, x, target, w_a, w_b, w_out, vecs)


def _conv_backward(p_ext, dconv, vecs, seq):
    n_tiles = seq // TM
    hb = TM // HALO

    def body(h_ref, cg_ref, dc_ref, dcp_ref, dcn_ref, vec_ref, dpb_ref, part_ref):
        i = pl.program_id(0)
        f = lambda ref: ref[...].astype(F32)
        h, cg, dc = f(h_ref), f(cg_ref), f(dc_ref)
        w0, w1, w2 = vec_ref[1:2, :], vec_ref[2:3, :], vec_ref[3:4, :]
        row = lax.broadcasted_iota(jnp.int32, (TM, 1), 0)
        dc_prev = jnp.where(i == 0, 0.0, f(dcp_ref)[HALO - 1:HALO, :])
        dc_next = jnp.where(i == n_tiles - 1, 0.0, f(dcn_ref)[0:1, :])
        dc_up = jnp.where(row == 0, dc_prev, pltpu.roll(dc, 1, 0))
        dc_dn = jnp.where(row == TM - 1, dc_next, pltpu.roll(dc, TM - 1, 0))
        du = w0 * dc_dn + w1 * dc + w2 * dc_up
        u = cg * h
        dpb_ref[:, 0:D] = (du * cg).astype(BF16)
        dpb_ref[:, D:2 * D] = (du * h).astype(BF16)

        @pl.when(i == 0)
        def _():
            part_ref[...] = jnp.zeros((8, D), F32)

        part_ref[0:1, :] += jnp.sum(u * dc_dn, axis=0, keepdims=True)
        part_ref[1:2, :] += jnp.sum(u * dc, axis=0, keepdims=True)
        part_ref[2:3, :] += jnp.sum(u * dc_up, axis=0, keepdims=True)
        part_ref[3:4, :] += jnp.sum(dc, axis=0, keepdims=True)

    col = lambda cb_: pl.BlockSpec((TM, D), lambda i, cb_=cb_: (i, cb_))
    tile = pl.BlockSpec((TM, D), lambda i: (i, 0))
    return pl.pallas_call(
        body, name="conv_backward", grid=(n_tiles,),
        in_specs=[col(CB_H), col(CB_CG), tile,
                  pl.BlockSpec((HALO, D), lambda i: (jnp.maximum(i * hb - 1, 0), 0)),
                  pl.BlockSpec((HALO, D), lambda i: (jnp.minimum((i + 1) * hb, n_tiles * hb - 1), 0)),
                  pl.BlockSpec(vecs.shape, lambda i: (0, 0))],
        out_specs=[pl.BlockSpec((TM, 2 * D), lambda i: (i, 0)), pl.BlockSpec((8, D), lambda i: (0, 0))],
        out_shape=[jax.ShapeDtypeStruct((seq, 2 * D), BF16), jax.ShapeDtypeStruct((8, D), F32)],
        compiler_params=_cparams(dimension_semantics=("arbitrary",)),
    )(p_ext, p_ext, dconv, dconv, dconv, vecs)


def _input_backward(dpa, dpb, dq, dk, dv, w_in, x, ctx, dx1, modx, modc, norm_w):
    seq = x.shape[0]
    lext = seq + ctx.shape[0]
    n_x = seq // TM

    def body(dpa_ref, dpb_ref, dq_ref, dk_ref, dv_ref, w_ref, x_ref, ctx_ref, dx1_ref, mx_ref, mc_ref, nw_ref,
             dp_ref, gx_ref, part_ref):
        i = pl.program_id(0)
        is_ctx = i >= n_x
        qk = N_HEADS * DK

        @pl.when(jnp.logical_not(is_ctx))
        def _():
            dp_ref[:, CB_H * D:(CB_H + 1) * D] = dpb_ref[:, 0:D]
            dp_ref[:, CB_BG * D:(CB_BG + 1) * D] = dpa_ref[:, 0:D]
            dp_ref[:, CB_CG * D:(CB_CG + 1) * D] = dpb_ref[:, D:2 * D]
            dp_ref[:, CB_ZA * D:(CB_ZA + 1) * D] = dpa_ref[:, D:2 * D]
            dp_ref[:, CB_ZB * D:(CB_GB + 1) * D] = dpa_ref[:, 2 * D:5 * D]

        @pl.when(is_ctx)
        def _():
            dp_ref[:, 0:CB_QK * D] = jnp.zeros((TM, CB_QK * D), BF16)
            dp_ref[:, CB_ZB * D:PW] = jnp.zeros((TM, PW - CB_ZB * D), BF16)

        dp_ref[:, CB_QK * D:CB_QK * D + qk] = dq_ref[...]
        dp_ref[:, CB_QK * D + qk:CB_V * D] = dk_ref[...]
        dp_ref[:, CB_V * D:CB_ZB * D] = dv_ref[...]
        dxm = _dot_nt(dp_ref[:, 0:WSH], w_ref[0])
        for d in range(1, N_DEV):
            dxm = dxm + _dot_nt(dp_ref[:, d * WSH:(d + 1) * WSH], w_ref[d])
        x = jnp.where(is_ctx, ctx_ref[...], x_ref[...])
        r = lax.rsqrt(jnp.mean(x * x, axis=-1, keepdims=True) + EPS)
        xh = x * r
        nw = nw_ref[...]
        sc = jnp.where(is_ctx, mc_ref[1:2, :], mx_ref[1:2, :])
        dxn = dxm * (1.0 + sc)
        dxh = dxn * nw
        dx = r * (dxh - xh * jnp.mean(dxh * xh, axis=-1, keepdims=True))

        @pl.when(jnp.logical_not(is_ctx))
        def _():
            gx_ref[...] = dx1_ref[...] + dx

        @pl.when(i == 0)
        def _():
            part_ref[...] = jnp.zeros((8, D), F32)

        fx = jnp.where(is_ctx, 0.0, 1.0)
        d_shift = jnp.sum(dxm, axis=0, keepdims=True)
        d_scale = jnp.sum(dxm * (xh * nw), axis=0, keepdims=True)
        part_ref[0:1, :] += fx * d_shift
        part_ref[1:2, :] += fx * d_scale
        part_ref[2:3, :] += jnp.sum(dxn * xh, axis=0, keepdims=True)
        part_ref[3:4, :] += (1.0 - fx) * d_shift
        part_ref[4:5, :] += (1.0 - fx) * d_scale

    lat = lambda w: pl.BlockSpec((TM, w), lambda i: (jnp.minimum(i, n_x - 1), 0))
    ext = lambda w: pl.BlockSpec((TM, w), lambda i: (i, 0))
    full = lambda a: pl.BlockSpec(a.shape, lambda i: (0,) * a.ndim)
    return pl.pallas_call(
        body, name="input_backward", grid=(lext // TM,),
        in_specs=[lat(5 * D), lat(2 * D), ext(N_HEADS * DK), ext(N_HEADS * DK), ext(D), full(w_in), lat(D), full(ctx),
                  lat(D), full(modx), full(modc), full(norm_w)],
        out_specs=[ext(PW), lat(D), pl.BlockSpec((8, D), lambda i: (0, 0))],
        out_shape=[jax.ShapeDtypeStruct((lext, PW), BF16), jax.ShapeDtypeStruct((seq, D), F32),
                   jax.ShapeDtypeStruct((8, D), F32)],
        compiler_params=_cparams(dimension_semantics=("arbitrary",)),
    )(dpa, dpb, dq, dk, dv, w_in, x, ctx, dx1, modx, modc, norm_w)


def _weight_grad_in(xm_t, dp, owners, name):
    lext = xm_t.shape[1]
    tk = lext // 2
    n_k = lext // tk

    def body(own_ref, a_ref, b_ref, o_ref, acc):
        k = pl.program_id(1)

        @pl.when(k == 0)
        def _():
            acc[...] = jnp.zeros_like(acc)

        acc[...] += _dot(a_ref[...], b_ref[...])

        @pl.when(k == n_k - 1)
        def _():
            o_ref[0] = acc[...]

    grid_spec = pltpu.PrefetchScalarGridSpec(
        num_scalar_prefetch=1, grid=(4, n_k),
        in_specs=[pl.BlockSpec((D, tk), lambda j, k, own: (0, k)),
                  pl.BlockSpec((tk, WSH), lambda j, k, own: (k, own[j]))],
        out_specs=pl.BlockSpec((1, D, WSH), lambda j, k, own: (j, 0, 0)),
        scratch_shapes=[pltpu.VMEM((D, WSH), F32)])
    return pl.pallas_call(
        body, name=name, grid_spec=grid_spec,
        out_shape=jax.ShapeDtypeStruct((4, D, WSH), F32),
        compiler_params=_cparams(dimension_semantics=("arbitrary", "arbitrary")),
    )(owners, xm_t, dp)


def _weight_grad_square(a_t, b):
    seq = a_t.shape[2]
    tk = seq // 2

    def body(a_ref, b_ref, o_ref, acc):
        k = pl.program_id(1)

        @pl.when(k == 0)
        def _():
            acc[...] = jnp.zeros_like(acc)

        acc[...] += _dot(a_ref[0], b_ref[0])

        @pl.when(k == pl.num_programs(1) - 1)
        def _():
            o_ref[:, 0] = acc[...].reshape(N_DEV, RSH, D)

    return pl.pallas_call(
        body, name="weight_grad_square", grid=(3, seq // tk),
        in_specs=[pl.BlockSpec((1, D, tk), lambda t, k: (t, 0, k)), pl.BlockSpec((1, tk, D), lambda t, k: (t, k, 0))],
        out_specs=pl.BlockSpec((N_DEV, 1, RSH, D), lambda t, k: (0, t, 0, 0)),
        out_shape=jax.ShapeDtypeStruct((N_DEV, 3, RSH, D), F32),
        scratch_shapes=[pltpu.VMEM((D, D), F32)],
        compiler_params=_cparams(dimension_semantics=("arbitrary", "arbitrary")),
    )(a_t, b)


def _rope_tables(seq, ctx_len):
    rows = seq // GRID_W
    row = jnp.repeat(jnp.arange(rows), GRID_W).astype(F32)
    col = jnp.tile(jnp.arange(GRID_W), rows).astype(F32)
    nf = DK // 4
    inv = ROPE_BASE ** (-jnp.arange(nf, dtype=F32) / nf)
    ang = jnp.concatenate([row[:, None] * inv, col[:, None] * inv], axis=-1)
    cos, sin = jnp.cos(ang), jnp.sin(ang)
    cos_t = jnp.tile(cos, (1, 4))
    sin_t = jnp.tile(jnp.concatenate([-sin, sin], axis=-1), (1, 2))
    cos_t = jnp.concatenate([cos_t, jnp.ones((ctx_len, LANES), F32)], axis=0)
    sin_t = jnp.concatenate([sin_t, jnp.zeros((ctx_len, LANES), F32)], axis=0)
    return cos_t, sin_t


def _tie(value, token):
    return value + token[0, 0].astype(value.dtype)


def _pad_lanes(a):
    return jnp.pad(a, ((0, 0), (0, D - a.shape[1])))


def _rows(total, *parts):
    width = max(a.shape[1] for _, a in parts)
    out = None
    for row, a in parts:
        padded = jnp.pad(a, ((row, total - row - a.shape[0]), (0, width - a.shape[1])))
        out = padded if out is None else out + padded
    return out


def kernel(x, c, ctx, c_ctx, norm_w, ada_w, ada_b, w_in, conv_w, conv_b, decay_logit, gn_w, w_a, w_b, w_out, final_norm_w, loss_target, m_c_ctx, m_norm_w, m_ada_w, m_ada_b, m_w_in, m_conv_w, m_conv_b, m_decay_logit, m_gn_w, m_w_a, m_w_b, m_w_out, m_final_norm_w, v_c_ctx, v_norm_w, v_ada_w, v_ada_b, v_w_in, v_conv_w, v_conv_b, v_decay_logit, v_gn_w, v_w_a, v_w_b, v_w_out, v_final_norm_w):
    xi, yi, ci = _coords()
    me = 4 * xi + 2 * yi + ci
    chip = 2 * xi + yi
    seq, ctx_len = x.shape[1], ctx.shape[1]
    assert seq % TM == 0 and seq % RET_C == 0 and ctx_len == TM and seq % GRID_W == 0
    csh = D // N_DEV

    w_in_all, gathered = _all_gather_weights([w_in[0].astype(BF16)])
    sq_shards = [_tie(w[0].astype(BF16), gathered) for w in (w_a, w_b, w_out)]
    ag_sq = _Exchange("gather_square", sq_shards, [jax.ShapeDtypeStruct((N_DEV, RSH, D), BF16)] * 3,
                      3 * (N_DEV - 1), _gather_plan)

    blk = jnp.pad(c, ((0, 7), (0, 0))) + jnp.pad(conv_w[0], ((1, 4), (0, D - csh)))
    blk = _tie(blk, ag_sq.token)
    got = _all_gather_small(blk, "gather_cond")
    conv_w_all = got[:, 1:4, 0:csh].transpose(1, 0, 2).reshape(3, D)
    c16 = _rows(16, (0, got[:, 0, :]), (N_DEV, c_ctx[None]))
    ada_b_sh = lax.dynamic_slice(ada_b, (0, me * ADA_SH), (1, ADA_SH))
    mod_sh, act16, lg = _modulation(c16, ada_w[0], ada_b_sh, decay_logit[0])
    mod_all = _all_gather_small(mod_sh, "gather_mod").transpose(1, 0, 2).reshape(16, 3 * D)
    modx = lax.dynamic_slice(mod_all, (me, 0), (1, 3 * D)).reshape(3, D)
    modc = mod_all[8].reshape(3, D)

    x2, tgt = x[0], loss_target[0]
    cos_t, sin_t = _rope_tables(seq, ctx_len)
    p_ext, xm_t = _in_projection(x2, ctx[0], modx, modc, norm_w, w_in_all, cos_t, sin_t)
    rn, rstd = _retention_forward(p_ext, lg, seq, ctx_len)
    _, sq_lands = ag_sq.wait(rstd)
    w_a_all, w_b_all, w_out_all = (
        lax.dynamic_update_slice(land, shard[None], (me, 0, 0)).reshape(D, D) for land, shard in zip(sq_lands, sq_shards))
    vecs = _rows(8, (0, modx[2:3]), (1, conv_w_all), (4, conv_b), (5, gn_w), (6, final_norm_w[None]))
    dx1, dpa, dconv, dret, op_at, op_b, part_m = _merge(p_ext, rn, rstd, x2, tgt, w_a_all, w_b_all, w_out_all, vecs, seq)

    j4 = jnp.arange(4, dtype=jnp.int32)
    owners = (2 * jnp.bitwise_xor(chip, j4) + ci).astype(jnp.int32)
    owners_sib = (2 * jnp.bitwise_xor(chip, j4) + (1 - ci)).astype(jnp.int32)
    gw_sq = _weight_grad_square(op_at, op_b).reshape(N_DEV, 3 * RSH, D)
    rs_sq_pair = _Exchange("rs_square_pair", [gw_sq], [jax.ShapeDtypeStruct((4, 3 * RSH, D), F32)], 4,
                           _pair_plan(lambda j, chip_, c_: 2 * jnp.bitwise_xor(chip_, j) + (1 - c_)))
    dpb, part_c = _conv_backward(p_ext, dconv, _tie(vecs, rs_sq_pair.token), seq)
    (gw_sq,), (r1_sq,) = rs_sq_pair.wait(part_c)
    own_sq, send_sq = _pair_sum(gw_sq, r1_sq, owners, "pair_sum_square")
    rs_sq_chips = _Exchange("rs_square_chips", [send_sq], [jax.ShapeDtypeStruct((3, 3 * RSH, D), BF16)], 3, _chips_plan)
    dq, dk, dv, dlg = _retention_backward(p_ext, dret, _tie(lg, rs_sq_chips.token), cos_t, sin_t, seq, ctx_len)
    dp, grad_x, part_i = _input_backward(dpa, dpb, dq, dk, dv, w_in_all, x2, ctx[0], dx1, modx, modc, norm_w)
    gw_sib = _weight_grad_in(xm_t, dp, owners_sib, "weight_grad_in_sibling")
    rs_in_pair = _Exchange("rs_in_pair", [gw_sib], [jax.ShapeDtypeStruct((4, D, WSH), F32)], 4,
                           _pair_plan(lambda j, chip_, c_: j))
    gw_own = _weight_grad_in(xm_t, dp, _tie(owners, rs_in_pair.token), "weight_grad_in_own")
    _, (r1_in,) = rs_in_pair.wait(gw_own)
    own_in, send_in = _pair_sum(gw_own, r1_in, j4, "pair_sum_in")
    rs_in_chips = _Exchange("rs_in_chips", [send_in], [jax.ShapeDtypeStruct((3, D, WSH), BF16)], 3, _chips_plan)

    dl = _tie(dlg, rs_in_chips.token)[:, 0, 0:4]
    dlg_row = jnp.pad(dl[:, 0::2].reshape(1, N_HEADS), ((0, 0), (0, D - N_HEADS))) + jnp.pad(
        dl[:, 1::2].reshape(1, N_HEADS), ((0, 0), (N_HEADS, D - 2 * N_HEADS)))
    partials = _rows(16, (0, part_m[0:1]), (1, part_i[2:3]), (2, part_c[3:4]), (3, part_m[2:3]), (4, part_c[0:3]),
                     (7, part_i[0:2]), (9, part_m[1:2]), (10, part_i[3:5]), (13, part_m[3:4]), (14, dlg_row))
    got = _all_gather_small(partials, "gather_partials")
    tot = _sum_devices(got, "sum_partials")
    dmodc = tot[10:13].reshape(1, 3 * D)
    dmod16 = _rows(16, (0, got[:, 7:10, :].reshape(N_DEV, 3 * D)), (N_DEV, dmodc))
    dmod16 = lax.dynamic_slice(dmod16, (0, me * ADA_SH), (16, ADA_SH))
    dmodc8 = _rows(8, (0, dmod16[8:9]))
    g_ada_w, d_ada_w, nm_ada_w, nv_ada_w, cctx_part = _ada_backward(
        act16.T, dmod16, dmodc8, ada_w[0], m_ada_w[0], v_ada_w[0])
    dact_ctx = _sum_devices(_all_gather_small(cctx_part, "gather_cctx"), "sum_cctx")[0:1]

    my_lanes = lambda a: _pad_lanes(lax.dynamic_slice(a, (0, me * csh), (a.shape[0], csh)))
    small = lambda cc, nw, cb, gn, fn, ab, dlt, cw: _rows(
        16, (0, cc[None]), (1, nw), (2, cb), (3, gn), (4, fn[None]), (5, ab.reshape(3, D)),
        (8, dlt.reshape(1, 2 * N_HEADS)), (9, cw[0]))
    w_sm = small(c_ctx, norm_w, conv_b, gn_w, final_norm_w, ada_b, decay_logit, conv_w)
    m_sm = small(m_c_ctx, m_norm_w, m_conv_b, m_gn_w, m_final_norm_w, m_ada_b, m_decay_logit, m_conv_w)
    v_sm = small(v_c_ctx, v_norm_w, v_conv_b, v_gn_w, v_final_norm_w, v_ada_b, v_decay_logit, v_conv_w)
    g_sm = _rows(16, (0, dact_ctx), (1, tot[1:4]), (4, tot[0:1]), (5, tot[7:10]), (8, tot[14:15]), (9, my_lanes(tot[4:7])))
    g_sm2 = _rows(16, (5, tot[10:13]))
    sm = _adam_small(w_sm, g_sm, g_sm2, m_sm, v_sm, 0, 8)

    _, (r2_sq,) = rs_sq_chips.wait(sm[0])
    _, (r2_in,) = rs_in_chips.wait(sm[0])
    g_w_in, d_w_in, nm_w_in, nv_w_in = _adam_sharded(w_in[0], m_w_in[0], v_w_in[0], own_in, r2_in, "adam_w_in")
    stack = lambda a, b, c_: jnp.concatenate([a[0], b[0], c_[0]], axis=0)
    sq = _adam_sharded(stack(w_a, w_b, w_out), stack(m_w_a, m_w_b, m_w_out), stack(v_w_a, v_w_b, v_w_out),
                       own_sq, r2_sq, "adam_square")
    sq = [o.reshape(3, 1, RSH, D) for o in sq]

    def unpack(a, sq_i):
        return [a[0], a[1:2], None, a[5:8].reshape(1, 3 * D), None, a[9:12, 0:csh][None], a[2:3],
                a[8, 0:2 * N_HEADS].reshape(1, 2, N_HEADS), a[3:4], sq_i[0], sq_i[1], sq_i[2], a[4]]

    outs = []
    for a, sq_i, ada_i, in_i in zip(sm, sq, (g_ada_w, d_ada_w, nm_ada_w, nv_ada_w), (g_w_in, d_w_in, nm_w_in, nv_w_in)):
        group = unpack(a, sq_i)
        group[2] = ada_i[None]
        group[4] = in_i[None]
        outs += group
    return (tot[13, 0], grad_x[None], *outs)
```

```python
import functools
import math

import jax
import jax.numpy as jnp
from jax import lax
from jax.experimental import pallas as pl
from jax.experimental.pallas import tpu as pltpu

F32 = jnp.float32
BF16 = jnp.bfloat16
MESH = pl.DeviceIdType.MESH
_HBM_SPEC = pl.BlockSpec(memory_space=pltpu.HBM)
_SEM_SPEC = pl.BlockSpec(memory_space=pltpu.SEMAPHORE)
_EFFECT = pltpu.SideEffectType.DATAFLOW_SIDE_EFFECTING

N_DEV = 8
D = 1024
N_HEADS = 8
DK = 64
DV = 128
GRID_W = 64
ROPE_BASE = 10000.0
EPS = 1e-6
PW = 9 * D
WSH = PW // N_DEV
RSH = D // N_DEV
ADA_SH = 3 * D // N_DEV
TM = 256
RET_C = 256
HALO = 16
LANES = 128
VMEM_LIMIT = 60 * 1024 * 1024

ADAM_LR = 0.001
ADAM_B1 = 0.9
ADAM_B2 = 0.999
ADAM_EPS = 1e-08
ADAM_WD = 0.01
ADAM_STEP = 10

CB_H, CB_BG, CB_CG, CB_ZA, CB_QK, CB_V, CB_ZB, CB_GA, CB_GB = range(9)


def _cparams(**kw):
    return pltpu.CompilerParams(vmem_limit_bytes=VMEM_LIMIT, **kw)


def _dot(a, b):
    return jnp.dot(a, b, preferred_element_type=F32)


def _dot_nt(a, b):
    return lax.dot_general(a, b, (((1,), (1,)), ((), ())), preferred_element_type=F32)


def _dot_tn(a, b):
    return lax.dot_general(a, b, (((0,), (0,)), ((), ())), preferred_element_type=F32)


def _sigmoid(z):
    return 1.0 / (1.0 + jnp.exp(-z))


def _row_tile(rows):
    return TM if rows % TM == 0 else rows


def _coords():
    return lax.axis_index("x"), lax.axis_index("y"), lax.axis_index("c")


def _flip(v, bit):
    return 1 - v if bit else v


def _all_gather_small(blk, name):
    rows, cols = blk.shape

    def body(x_ref, out_ref, send_sems, recv_sems, local_sem):
        x, y, c = _coords()
        me = 4 * x + 2 * y + c
        mine = pltpu.make_async_copy(x_ref, out_ref.at[me], local_sem)
        mine.start()

        def copy(k, slot):
            peer = (_flip(x, k & 4), _flip(y, k & 2), _flip(c, k & 1))
            return pltpu.make_async_remote_copy(
                src_ref=x_ref, dst_ref=out_ref.at[slot], send_sem=send_sems.at[k - 1],
                recv_sem=recv_sems.at[k - 1], device_id=peer, device_id_type=MESH)

        for k in range(1, N_DEV):
            copy(k, me).start()
        for k in range(1, N_DEV):
            copy(k, jnp.bitwise_xor(me, k)).wait_recv()
        for k in range(1, N_DEV):
            copy(k, me).wait_send()
        mine.wait()

    return pl.pallas_call(
        body, name=name,
        out_shape=jax.ShapeDtypeStruct((N_DEV, rows, cols), blk.dtype),
        in_specs=[pl.BlockSpec(memory_space=pltpu.VMEM)],
        out_specs=pl.BlockSpec(memory_space=pltpu.VMEM),
        scratch_shapes=[pltpu.SemaphoreType.DMA((N_DEV - 1,)), pltpu.SemaphoreType.DMA((N_DEV - 1,)),
                        pltpu.SemaphoreType.DMA],
    )(blk)


def _all_gather_weights(shards):
    n = len(shards)

    def body(*refs):
        srcs, outs, done = refs[:n], refs[n:2 * n], refs[2 * n]
        send_sems, recv_sems, local_sems = refs[2 * n + 1:]
        x, y, c = _coords()
        me = 4 * x + 2 * y + c
        sib = (x, y, 1 - c)
        chips = [(1 - x, y), (x, 1 - y), (1 - x, 1 - y)]

        def slot(px, py, pc):
            return 4 * px + 2 * py + pc

        def copy(a, k, block_slot, to, own=False):
            return pltpu.make_async_remote_copy(
                src_ref=srcs[a] if own else outs[a].at[block_slot], dst_ref=outs[a].at[block_slot],
                send_sem=send_sems.at[7 * a + k], recv_sem=recv_sems.at[7 * a + k],
                device_id=to, device_id_type=MESH)

        locals_ = [pltpu.make_async_copy(srcs[a], outs[a].at[me], local_sems.at[a]) for a in range(n)]
        for cp in locals_:
            cp.start()
        sent = []
        for a in range(n):
            sent.append(copy(a, 0, me, sib, own=True))
            for j, chip in enumerate(chips):
                sent.append(copy(a, 1 + j, me, (*chip, c), own=True))
        for cp in sent:
            cp.start()
        for j, chip in enumerate(chips):
            for a in range(n):
                copy(a, 1 + j, slot(*chip, c), sib).wait_recv()
                fwd = copy(a, 4 + j, slot(*chip, c), sib)
                fwd.start()
                sent.append(fwd)
        for a in range(n):
            copy(a, 0, slot(x, y, 1 - c), sib).wait_recv()
            for j, chip in enumerate(chips):
                copy(a, 4 + j, slot(*chip, 1 - c), sib).wait_recv()
        for cp in sent:
            cp.wait_send()
        for cp in locals_:
            cp.wait()
        done[...] = jnp.zeros((8, LANES), F32)

    anyspec = pl.BlockSpec(memory_space=pl.ANY)
    return pl.pallas_call(
        body, name="all_gather_weights",
        out_shape=[jax.ShapeDtypeStruct((N_DEV,) + s.shape, s.dtype) for s in shards]
        + [jax.ShapeDtypeStruct((8, LANES), F32)],
        in_specs=[anyspec] * n, out_specs=[anyspec] * n + [pl.BlockSpec(memory_space=pltpu.VMEM)],
        scratch_shapes=[pltpu.SemaphoreType.DMA((7 * n,)), pltpu.SemaphoreType.DMA((7 * n,)),
                        pltpu.SemaphoreType.DMA((n,))],
    )(*shards)


class _Exchange:
    def __init__(self, name, srcs, land_shapes, n_copies, plan):
        self.name, self.plan, self.n_copies = name, plan, n_copies
        self.n_src, self.n_land = len(srcs), len(land_shapes)
        hbm = lambda a: pltpu.HBM(a.shape, a.dtype)
        n = self.n_src + self.n_land
        lands = [pltpu.with_memory_space_constraint(lax.empty(s.shape, s.dtype), pltpu.HBM) for s in land_shapes]
        srcs = [pltpu.with_memory_space_constraint(s, pltpu.HBM) for s in srcs]

        def body(*refs):
            send_sems, recv_sems = refs[n], refs[n + 1]
            for cp in self._copies(refs, send_sems, recv_sems):
                cp.start()
            refs[-1][...] = jnp.zeros((8, LANES), F32)

        outs = pl.pallas_call(
            body, name=name + "_start",
            out_shape=(pltpu.SemaphoreType.DMA((n_copies,)), pltpu.SemaphoreType.DMA((n_copies,)),
                       *[hbm(a) for a in srcs], *[hbm(a) for a in lands], jax.ShapeDtypeStruct((8, LANES), F32)),
            in_specs=[_HBM_SPEC] * n,
            out_specs=(_SEM_SPEC, _SEM_SPEC, *[_HBM_SPEC] * n, pl.BlockSpec(memory_space=pltpu.VMEM)),
            input_output_aliases={i: 2 + i for i in range(n)},
            compiler_params=pltpu.CompilerParams(has_side_effects=_EFFECT),
        )(*srcs, *lands)
        self.send_sems, self.recv_sems = outs[0], outs[1]
        self.buffers = list(outs[2:2 + n])
        self.token = outs[-1]

    def _copies(self, refs, send_sems, recv_sems):
        src_refs, land_refs = refs[:self.n_src], refs[self.n_src:self.n_src + self.n_land]
        return [pltpu.make_async_remote_copy(src_ref=s, dst_ref=d, send_sem=send_sems.at[k], recv_sem=recv_sems.at[k],
                                             device_id=dev, device_id_type=MESH)
                for k, (s, d, dev) in enumerate(self.plan(src_refs, land_refs))]

    def wait(self, after):
        n = self.n_src + self.n_land

        def body(*refs):
            for cp in self._copies(refs, refs[n], refs[n + 1]):
                cp.wait_send()
                cp.wait_recv()

        outs = pl.pallas_call(
            body, name=self.name + "_wait",
            out_shape=tuple(pltpu.HBM(b.shape, b.dtype) for b in self.buffers),
            in_specs=[_HBM_SPEC] * n + [_SEM_SPEC, _SEM_SPEC, pl.BlockSpec(memory_space=pl.ANY)],
            out_specs=tuple([_HBM_SPEC] * n),
            input_output_aliases={i: i for i in range(n)},
            compiler_params=pltpu.CompilerParams(has_side_effects=_EFFECT),
        )(*self.buffers, self.send_sems, self.recv_sems, after)
        return list(outs[:self.n_src]), list(outs[self.n_src:])


def _pair_plan(src_index):
    def plan(srcs, lands):
        x, y, c = _coords()
        chip = 2 * x + y
        return [(srcs[0].at[src_index(j, chip, c)], lands[0].at[j], (x, y, 1 - c)) for j in range(4)]
    return plan


def _chips_plan(srcs, lands):
    x, y, c = _coords()
    return [(srcs[0].at[j - 1], lands[0].at[j - 1], (_flip(x, j & 2), _flip(y, j & 1), c)) for j in range(1, 4)]


def _gather_plan(srcs, lands):
    x, y, c = _coords()
    me = 4 * x + 2 * y + c
    return [(srcs[a], lands[a].at[me], (_flip(x, k & 4), _flip(y, k & 2), _flip(c, k & 1)))
            for a in range(len(srcs)) for k in range(1, N_DEV)]


def _pair_sum(grad, recv, owners, name):
    _, rows, cols = grad.shape
    tr = _row_tile(rows)

    def body(own_ref, g_ref, r_ref, mine_ref, send_ref):
        j = pl.program_id(1)
        total = g_ref[...] + r_ref[...]

        @pl.when(j == 0)
        def _():
            mine_ref[...] = total

        @pl.when(j > 0)
        def _():
            send_ref[...] = total.astype(BF16)

    grid_spec = pltpu.PrefetchScalarGridSpec(
        num_scalar_prefetch=1, grid=(rows // tr, 4),
        in_specs=[pl.BlockSpec((1, tr, cols), lambda i, j, own: (own[j], i, 0)),
                  pl.BlockSpec((1, tr, cols), lambda i, j, own: (j, i, 0))],
        out_specs=[pl.BlockSpec((1, tr, cols), lambda i, j, own: (0, i, 0)),
                   pl.BlockSpec((1, tr, cols), lambda i, j, own: (jnp.maximum(j - 1, 0), i, 0))])
    return pl.pallas_call(
        body, name=name, grid_spec=grid_spec,
        out_shape=[jax.ShapeDtypeStruct((1, rows, cols), F32), jax.ShapeDtypeStruct((3, rows, cols), BF16)],
        compiler_params=_cparams(dimension_semantics=("arbitrary", "arbitrary")),
    )(owners, grad, recv)


def _modulation(c16, ada_w_sh, ada_b_sh, decay_logit):
    def body(c_ref, w_ref, b_ref, dl_ref, mod_ref, act_ref, lg_ref):
        cv = c_ref[...]
        act = cv * _sigmoid(cv)
        act_ref[...] = act
        mod_ref[...] = jnp.dot(act, w_ref[...], preferred_element_type=F32,
                               precision=lax.Precision.HIGHEST) + b_ref[...]
        z = dl_ref[...]
        lg_ref[...] = jnp.minimum(z, 0.0) - jnp.log(1.0 + jnp.exp(-jnp.abs(z)))

    return pl.pallas_call(
        body, name="modulation",
        out_shape=[jax.ShapeDtypeStruct((16, ADA_SH), F32), jax.ShapeDtypeStruct((16, D), F32),
                   jax.ShapeDtypeStruct(decay_logit.shape, F32)],
        compiler_params=_cparams(),
    )(c16, ada_w_sh, ada_b_sh, decay_logit)


def _adam_update(w, g, m, v):
    m2 = ADAM_B1 * m + (1.0 - ADAM_B1) * g
    v2 = ADAM_B2 * v + (1.0 - ADAM_B2) * (g * g)
    m_hat = m2 / (1.0 - ADAM_B1 ** ADAM_STEP)
    v_hat = v2 / (1.0 - ADAM_B2 ** ADAM_STEP)
    delta = -ADAM_LR * (m_hat / (jnp.sqrt(v_hat) + ADAM_EPS) + ADAM_WD * w)
    return delta, m2, v2


def _adam_sharded(w, m, v, own, recv, name):
    rows, cols = w.shape
    tr = _row_tile(rows)

    def body(w_ref, m_ref, v_ref, g0, g1, g2, g3, g_ref, d_ref, m_out, v_out):
        g = ((g0[0] + g1[0].astype(F32)) + g2[0].astype(F32)) + g3[0].astype(F32)
        delta, m2, v2 = _adam_update(w_ref[...], g, m_ref[...], v_ref[...])
        g_ref[...] = g
        d_ref[...] = delta
        m_out[...] = m2
        v_out[...] = v2

    flat = pl.BlockSpec((tr, cols), lambda i: (i, 0))
    part = lambda j: pl.BlockSpec((1, tr, cols), lambda i, j=j: (j, i, 0))
    return pl.pallas_call(
        body, name=name, grid=(rows // tr,),
        in_specs=[flat, flat, flat, part(0), part(0), part(1), part(2)],
        out_specs=[flat] * 4,
        out_shape=[jax.ShapeDtypeStruct((rows, cols), F32)] * 4,
        compiler_params=_cparams(dimension_semantics=("arbitrary",)),
    )(w, m, v, own, recv, recv, recv)


def _ada_backward(act_t, dmod16, dmodc8, ada_w_sh, m, v):
    def body(at_ref, dm_ref, dc_ref, w_ref, m_ref, v_ref, g_ref, d_ref, m_out, v_out, pc_ref):
        g = jnp.dot(at_ref[...], dm_ref[...], preferred_element_type=F32, precision=lax.Precision.HIGHEST)
        w = w_ref[...]
        delta, m2, v2 = _adam_update(w, g, m_ref[...], v_ref[...])
        g_ref[...] = g
        d_ref[...] = delta
        m_out[...] = m2
        v_out[...] = v2
        pc_ref[...] = lax.dot_general(dc_ref[...], w, (((1,), (1,)), ((), ())), preferred_element_type=F32,
                                      precision=lax.Precision.HIGHEST)

    return pl.pallas_call(
        body, name="ada_backward",
        out_shape=[jax.ShapeDtypeStruct((D, ADA_SH), F32)] * 4 + [jax.ShapeDtypeStruct((8, D), F32)],
        compiler_params=_cparams(),
    )(act_t, dmod16, dmodc8, ada_w_sh, m, v)


def _sum_devices(gathered, name):
    _, rows, cols = gathered.shape

    def body(g_ref, o_ref):
        acc = g_ref[0]
        for d in range(1, N_DEV):
            acc = acc + g_ref[d]
        o_ref[...] = acc

    return pl.pallas_call(body, name=name, out_shape=jax.ShapeDtypeStruct((rows, cols), F32),
                          compiler_params=_cparams())(gathered)


def _adam_small(w, g_raw, g_raw2, m, v, silu_row, sigm_row):
    rows, cols = w.shape

    def body(w_ref, g_ref, g2_ref, m_ref, v_ref, g_out, d_ref, m_out, v_out):
        wv = w_ref[...]
        row = lax.broadcasted_iota(jnp.int32, (rows, cols), 0)
        s = _sigmoid(wv)
        dsilu = s * (1.0 + wv * (1.0 - s))
        scale = jnp.where(row == silu_row, dsilu, jnp.where(row == sigm_row, 1.0 - s, 1.0))
        g = (g_ref[...] + g2_ref[...]) * scale
        delta, m2, v2 = _adam_update(wv, g, m_ref[...], v_ref[...])
        g_out[...] = g
        d_ref[...] = delta
        m_out[...] = m2
        v_out[...] = v2

    return pl.pallas_call(body, name="adam_small", out_shape=[jax.ShapeDtypeStruct((rows, cols), F32)] * 4,
                          compiler_params=_cparams())(w, g_raw, g_raw2, m, v)


def _rope(t, cos, sin):
    lane = lax.broadcasted_iota(jnp.int32, (1, LANES), 1)
    first_half = jnp.bitwise_and(lane, DK // 2) == 0
    partner = jnp.where(first_half, pltpu.roll(t, LANES - DK // 2, 1), pltpu.roll(t, DK // 2, 1))
    return t * cos + partner * sin


def _in_projection(x, ctx, modx, modc, norm_w, w_in, cos_t, sin_t):
    n_x_tiles = x.shape[0] // TM
    lext = x.shape[0] + ctx.shape[0]
    q_lo, k_lo = 4 * D, 4 * D + N_HEADS * DK

    def body(x_ref, ctx_ref, mx_ref, mc_ref, nw_ref, w_ref, cos_ref, sin_ref, p_ref, xmt_ref):
        is_ctx = pl.program_id(0) >= n_x_tiles
        x = jnp.where(is_ctx, ctx_ref[...], x_ref[...])
        r = lax.rsqrt(jnp.mean(x * x, axis=-1, keepdims=True) + EPS)
        sh = jnp.where(is_ctx, mc_ref[0:1, :], mx_ref[0:1, :])
        sc = jnp.where(is_ctx, mc_ref[1:2, :], mx_ref[1:2, :])
        xm = (x * r * nw_ref[...]) * (1.0 + sc) + sh
        xmb = xm.astype(BF16)
        xmt_ref[...] = xm.T.astype(BF16)
        cos, sin = cos_ref[...], sin_ref[...]
        for d in range(N_DEV):
            acc = _dot(xmb, w_ref[d])
            lo = d * WSH
            col = 0
            while col < WSH:
                g = lo + col
                if q_lo <= g < k_lo + N_HEADS * DK:
                    piece = acc[:, col:col + LANES]
                    if g >= k_lo:
                        piece = piece * (DK ** -0.5)
                    p_ref[:, g:g + LANES] = _rope(piece, cos, sin).astype(BF16)
                    col += LANES
                else:
                    if g < q_lo:
                        end = min(WSH, q_lo - lo)
                    else:
                        end = WSH
                    p_ref[:, g:lo + end] = acc[:, col:end].astype(BF16)
                    col = end

    row = lambda w: pl.BlockSpec((TM, w), lambda i: (i, 0))
    full = lambda a: pl.BlockSpec(a.shape, lambda i: (0,) * a.ndim)
    return pl.pallas_call(
        body, name="in_projection", grid=(lext // TM,),
        in_specs=[pl.BlockSpec((TM, D), lambda i: (jnp.minimum(i, n_x_tiles - 1), 0)), full(ctx),
                  full(modx), full(modc), full(norm_w), full(w_in), row(LANES), row(LANES)],
        out_specs=[row(PW), pl.BlockSpec((D, TM), lambda i: (0, i))],
        out_shape=[jax.ShapeDtypeStruct((lext, PW), BF16), jax.ShapeDtypeStruct((D, lext), BF16)],
        compiler_params=_cparams(dimension_semantics=("arbitrary",)),
    )(x, ctx, modx, modc, norm_w, w_in, cos_t, sin_t)


def _decay_tables(lgf, lgb, n):
    i = lax.broadcasted_iota(jnp.int32, (n, 1), 0).astype(F32)
    return dict(i=i, k_f=jnp.exp(lgf * (n - 1.0 - i)), k_b=jnp.exp(lgb * i),
                q_f=jnp.exp(lgf * (i + 1.0)), q_b=jnp.exp(lgb * (n - i)))


def _decay_matrix(lgf, lgb, n, transposed=False):
    ii = lax.broadcasted_iota(jnp.int32, (n, n), 0)
    jj = lax.broadcasted_iota(jnp.int32, (n, n), 1)
    diff = (jj - ii if transposed else ii - jj).astype(F32)
    low = jnp.exp(lgf * jnp.maximum(diff, 0.0))
    up = jnp.exp(lgb * jnp.maximum(-diff, 0.0))
    return jnp.where(diff > 0, low, jnp.where(diff < 0, up, 2.0)), diff


def _cat_lanes(a, b):
    return jnp.concatenate([a.astype(BF16), b.astype(BF16)], axis=1)


def _retention_forward(p_ext, lg, seq, ctx_len):
    lext = seq + ctx_len
    n_chunks = seq // RET_C
    C = RET_C

    def body(lg_ref, q_ref, k_ref, v_ref, rn_ref, rstd_ref, kv_scr, sf_scr, sb_scr):
        pair = pl.program_id(0)
        lane = lax.broadcasted_iota(jnp.int32, (1, LANES), 1)
        for hh in range(2):
            hmask = (lane // DK == hh).astype(F32)
            lgf = lg_ref[0, 2 * pair + hh]
            lgb = lg_ref[1, 2 * pair + hh]
            vs = slice(hh * DV, (hh + 1) * DV)
            tc = _decay_tables(lgf, lgb, ctx_len)
            kc = k_ref[seq:lext, :].astype(F32) * hmask
            s0 = _dot_tn(_cat_lanes(kc * tc["k_f"], kc * tc["k_b"]), v_ref[seq:lext, vs])
            t = _decay_tables(lgf, lgb, C)

            def increments(c, carry):
                rows = pl.ds(pl.multiple_of(c * C, C), C)
                k = k_ref[rows, :].astype(F32) * hmask
                kv_scr[c] = _dot_tn(_cat_lanes(k * t["k_f"], k * t["k_b"]), v_ref[rows, vs])
                return carry

            lax.fori_loop(0, n_chunks, increments, 0, unroll=2)
            gf_c, gb_c = jnp.exp(lgf * C), jnp.exp(lgb * C)

            def scan_f(c, s):
                sf_scr[c] = s
                return gf_c * s + kv_scr[c, 0:LANES, :]

            def scan_b(n, s):
                c = n_chunks - 1 - n
                sb_scr[c] = s
                return gb_c * s + kv_scr[c, LANES:2 * LANES, :]

            lax.fori_loop(0, n_chunks, scan_f, s0[0:LANES])
            lax.fori_loop(0, n_chunks, scan_b, s0[LANES:2 * LANES])
            dmat, _ = _decay_matrix(lgf, lgb, C)

            def outputs(c, carry):
                rows = pl.ds(pl.multiple_of(c * C, C), C)
                q = q_ref[rows, :].astype(F32) * hmask
                v = v_ref[rows, vs]
                s = _dot_nt(q.astype(BF16), k_ref[rows, :])
                inner = _dot((s * dmat).astype(BF16), v)
                states = jnp.concatenate([sf_scr[c], sb_scr[c]], axis=0).astype(BF16)
                ret = inner + _dot(_cat_lanes(q * t["q_f"], q * t["q_b"]), states)
                mu = jnp.mean(ret, axis=-1, keepdims=True)
                cen = ret - mu
                rstd = lax.rsqrt(jnp.mean(cen * cen, axis=-1, keepdims=True) + EPS)
                rn_ref[rows, vs] = cen * rstd
                rstd_ref[rows, vs] = jnp.broadcast_to(rstd, (C, DV))
                return carry

            lax.fori_loop(0, n_chunks, outputs, 0, unroll=2)

    qk0 = CB_QK * D // LANES
    return pl.pallas_call(
        body, name="retention_forward", grid=(N_HEADS // 2,),
        in_specs=[pl.BlockSpec(memory_space=pltpu.SMEM),
                  pl.BlockSpec((lext, LANES), lambda g: (0, qk0 + g)),
                  pl.BlockSpec((lext, LANES), lambda g: (0, qk0 + N_HEADS // 2 + g)),
                  pl.BlockSpec((lext, 2 * DV), lambda g: (0, CB_V * D // (2 * DV) + g))],
        out_specs=[pl.BlockSpec((seq, 2 * DV), lambda g: (0, g))] * 2,
        out_shape=[jax.ShapeDtypeStruct((seq, D), F32)] * 2,
        scratch_shapes=[pltpu.VMEM((n_chunks, 2 * LANES, DV), F32), pltpu.VMEM((n_chunks, LANES, DV), F32),
                        pltpu.VMEM((n_chunks, LANES, DV), F32)],
        compiler_params=_cparams(dimension_semantics=("arbitrary",)),
    )(lg, p_ext, p_ext, p_ext)


def _retention_backward(p_ext, dret, lg, cos_t, sin_t, seq, ctx_len):
    lext = seq + ctx_len
    n_chunks = seq // RET_C
    C = RET_C

    def body(lg_ref, q_ref, k_ref, v_ref, do_ref, cos_ref, sin_ref, dq_ref, dk_ref, dv_ref, dlg_ref,
             kv_scr, g_scr, sf_scr, sb_scr, gfn_scr, gbp_scr):
        pair = pl.program_id(0)
        lane = lax.broadcasted_iota(jnp.int32, (1, LANES), 1)
        dlg = []
        for hh in range(2):
            in_head = lane // DK == hh
            hmask = in_head.astype(F32)
            lgf = lg_ref[0, 2 * pair + hh]
            lgb = lg_ref[1, 2 * pair + hh]
            vs = slice(hh * DV, (hh + 1) * DV)
            tc = _decay_tables(lgf, lgb, ctx_len)
            kc = k_ref[seq:lext, :].astype(F32) * hmask
            vc = v_ref[seq:lext, vs]
            kc_cat = _cat_lanes(kc * tc["k_f"], kc * tc["k_b"])
            s0 = _dot_tn(kc_cat, vc)
            t = _decay_tables(lgf, lgb, C)

            def increments(c, carry):
                rows = pl.ds(pl.multiple_of(c * C, C), C)
                k = k_ref[rows, :].astype(F32) * hmask
                q = q_ref[rows, :].astype(F32) * hmask
                kv_scr[c] = _dot_tn(_cat_lanes(k * t["k_f"], k * t["k_b"]), v_ref[rows, vs])
                g_scr[c] = _dot_tn(_cat_lanes(q * t["q_f"], q * t["q_b"]), do_ref[rows, vs])
                return carry

            lax.fori_loop(0, n_chunks, increments, 0, unroll=2)
            gf_c, gb_c = jnp.exp(lgf * C), jnp.exp(lgb * C)

            def scan_f(c, s):
                sf_scr[c] = s
                return gf_c * s + kv_scr[c, 0:LANES, :]

            def scan_b(n, s):
                c = n_chunks - 1 - n
                sb_scr[c] = s
                return gb_c * s + kv_scr[c, LANES:2 * LANES, :]

            def scan_gf(n, carry):
                c = n_chunks - 1 - n
                gfn_scr[c] = carry
                return g_scr[c, 0:LANES, :] + gf_c * carry

            def scan_gb(c, carry):
                gbp_scr[c] = carry
                return g_scr[c, LANES:2 * LANES, :] + gb_c * carry

            lax.fori_loop(0, n_chunks, scan_f, s0[0:LANES])
            lax.fori_loop(0, n_chunks, scan_b, s0[LANES:2 * LANES])
            zero_state = jnp.zeros((LANES, DV), F32)
            gf_first = lax.fori_loop(0, n_chunks, scan_gf, zero_state)
            gb_last = lax.fori_loop(0, n_chunks, scan_gb, zero_state)

            dmat, diff = _decay_matrix(lgf, lgb, C)
            dmat_t, _ = _decay_matrix(lgf, lgb, C, transposed=True)
            w_f = jnp.where(diff > 0, diff * dmat, 0.0)
            w_b = jnp.where(diff < 0, -diff * dmat, 0.0)
            cos_all, sin_all = cos_ref, sin_ref

            def total(a):
                rows_, width = a.shape
                part = jnp.sum(a.reshape(rows_ // 8, 8, width), axis=0)
                return part[:, 0:LANES] + part[:, LANES:2 * LANES] if width == 2 * LANES else part

            def chunk(c, carry):
                dlgf, dlgb = carry
                rows = pl.ds(pl.multiple_of(c * C, C), C)
                qm = q_ref[rows, :].astype(F32) * hmask
                km = k_ref[rows, :].astype(F32) * hmask
                qb, kb = qm.astype(BF16), km.astype(BF16)
                v = v_ref[rows, vs]
                do = do_ref[rows, vs]
                s = _dot_nt(qb, kb)
                s_t = _dot_nt(kb, qb)
                dsd = _dot_nt(do, v)
                dsd_t = _dot_nt(v, do)
                dq_in = _dot((dsd * dmat).astype(BF16), kb)
                dk_in = _dot((dsd_t * dmat_t).astype(BF16), qb)
                dv_in = _dot((s_t * dmat_t).astype(BF16), do)
                prod = s * dsd
                dlgf = dlgf + total(prod * w_f)
                dlgb = dlgb + total(prod * w_b)
                sf, sb = sf_scr[c], sb_scr[c]
                states = jnp.concatenate([sf, sb], axis=0).astype(BF16)
                dqc = _dot_nt(do, states)
                dqf = dqc[:, 0:LANES] * t["q_f"]
                dqb = dqc[:, LANES:2 * LANES] * t["q_b"]
                dq = (dq_in + dqf + dqb) * hmask
                dlgf = dlgf + total((t["i"] + 1.0) * (qm * dqf))
                dlgb = dlgb + total((C - t["i"]) * (qm * dqb))
                gfn, gbp = gfn_scr[c], gbp_scr[c]
                gstates = jnp.concatenate([gfn, gbp], axis=0).astype(BF16)
                dkc = _dot_nt(v, gstates)
                dkf = dkc[:, 0:LANES] * t["k_f"]
                dkb = dkc[:, LANES:2 * LANES] * t["k_b"]
                dk = dk_in + dkf + dkb
                dlgf = dlgf + total((C - 1.0 - t["i"]) * (km * dkf)) + C * gf_c * total(gfn * sf)
                dlgb = dlgb + total(t["i"] * (km * dkb)) + C * gb_c * total(gbp * sb)
                dv = dv_in + _dot(_cat_lanes(km * t["k_f"], km * t["k_b"]), gstates)
                cos, sin = cos_all[rows, :], sin_all[rows, :]
                dq = _rope(dq, cos, -sin)
                dk = _rope(dk, cos, -sin) * (DK ** -0.5)
                if hh == 0:
                    dq_ref[rows, :] = dq.astype(BF16)
                    dk_ref[rows, :] = dk.astype(BF16)
                else:
                    dq_ref[rows, :] = jnp.where(in_head, dq, dq_ref[rows, :].astype(F32)).astype(BF16)
                    dk_ref[rows, :] = jnp.where(in_head, dk, dk_ref[rows, :].astype(F32)).astype(BF16)
                dv_ref[rows, vs] = dv.astype(BF16)
                return dlgf, dlgb

            zero = jnp.zeros((8, LANES), F32)
            dlgf, dlgb = lax.fori_loop(0, n_chunks, chunk, (zero, zero), unroll=2)

            g0 = jnp.concatenate([gf_first, gb_last], axis=0).astype(BF16)
            dkcc = _dot_nt(vc, g0)
            dkcf = dkcc[:, 0:LANES] * tc["k_f"]
            dkcb = dkcc[:, LANES:2 * LANES] * tc["k_b"]
            dlgf = dlgf + total((ctx_len - 1.0 - tc["i"]) * (kc * dkcf))
            dlgb = dlgb + total(tc["i"] * (kc * dkcb))
            dkctx = (dkcf + dkcb) * (DK ** -0.5)
            if hh == 0:
                dk_ref[seq:lext, :] = dkctx.astype(BF16)
                dq_ref[seq:lext, :] = jnp.zeros((ctx_len, LANES), BF16)
            else:
                dk_ref[seq:lext, :] = jnp.where(in_head, dkctx, dk_ref[seq:lext, :].astype(F32)).astype(BF16)
            dv_ref[seq:lext, vs] = _dot(kc_cat, g0).astype(BF16)
            dlg += [jnp.sum(jnp.sum(a, axis=1, keepdims=True), axis=0, keepdims=True) for a in (dlgf, dlgb)]

        lane8 = lax.broadcasted_iota(jnp.int32, (8, LANES), 1)
        out = jnp.zeros((8, LANES), F32)
        for n, val in enumerate(dlg):
            out = jnp.where(lane8 == n, val, out)
        dlg_ref[0] = out

    qk0 = CB_QK * D // LANES
    q_spec = pl.BlockSpec((lext, LANES), lambda g: (0, qk0 + g))
    k_spec = pl.BlockSpec((lext, LANES), lambda g: (0, qk0 + N_HEADS // 2 + g))
    v_spec = pl.BlockSpec((lext, 2 * DV), lambda g: (0, CB_V * D // (2 * DV) + g))
    table = pl.BlockSpec((lext, LANES), lambda g: (0, 0))
    state = pltpu.VMEM((n_chunks, LANES, DV), F32)
    return pl.pallas_call(
        body, name="retention_backward", grid=(N_HEADS // 2,),
        in_specs=[pl.BlockSpec(memory_space=pltpu.SMEM), q_spec, k_spec, v_spec,
                  pl.BlockSpec((seq, 2 * DV), lambda g: (0, g)), table, table],
        out_specs=[pl.BlockSpec((lext, LANES), lambda g: (0, g)), pl.BlockSpec((lext, LANES), lambda g: (0, g)),
                   pl.BlockSpec((lext, 2 * DV), lambda g: (0, g)), pl.BlockSpec((1, 8, LANES), lambda g: (g, 0, 0))],
        out_shape=[jax.ShapeDtypeStruct((lext, N_HEADS * DK), BF16), jax.ShapeDtypeStruct((lext, N_HEADS * DK), BF16),
                   jax.ShapeDtypeStruct((lext, D), BF16), jax.ShapeDtypeStruct((N_HEADS // 2, 8, LANES), F32)],
        scratch_shapes=[pltpu.VMEM((n_chunks, 2 * LANES, DV), F32), pltpu.VMEM((n_chunks, 2 * LANES, DV), F32),
                        state, state, state, state],
        compiler_params=_cparams(dimension_semantics=("arbitrary",)),
    )(lg, p_ext, p_ext, p_ext, dret, cos_t, sin_t)


def _merge(p_ext, rn, rstd, x, target, w_a, w_b, w_out, vecs, seq):
    n_tiles = seq // TM
    hb = TM // HALO

    def body(h_ref, bg_ref, cg_ref, za_ref, zb_ref, ga_ref, gb_ref, hp_ref, hn_ref, cp_ref, cn_ref,
             rn_ref, rstd_ref, x_ref, t_ref, wa_ref, wb_ref, wo_ref, vec_ref,
             dx1_ref, dpa_ref, dconv_ref, dret_ref, at_ref, b_ref, part_ref):
        i = pl.program_id(0)
        f = lambda ref: ref[...].astype(F32)
        h, bg, cg, za, zb, ga, gb = f(h_ref), f(bg_ref), f(cg_ref), f(za_ref), f(zb_ref), f(ga_ref), f(gb_ref)
        gx, w0, w1, w2 = vec_ref[0:1, :], vec_ref[1:2, :], vec_ref[2:3, :], vec_ref[3:4, :]
        cb, gnw, fw = vec_ref[4:5, :], vec_ref[5:6, :], vec_ref[6:7, :]
        u = cg * h
        row = lax.broadcasted_iota(jnp.int32, (TM, 1), 0)
        u_prev = (f(cp_ref) * f(hp_ref))[HALO - 1:HALO, :]
        u_next = (f(cn_ref) * f(hn_ref))[0:1, :]
        u_prev = jnp.where(i == 0, 0.0, u_prev)
        u_next = jnp.where(i == n_tiles - 1, 0.0, u_next)
        u_up = jnp.where(row == 0, u_prev, pltpu.roll(u, 1, 0))
        u_dn = jnp.where(row == TM - 1, u_next, pltpu.roll(u, TM - 1, 0))
        conv = w0 * u_up + w1 * u + w2 * u_dn + cb
        sza = _sigmoid(za)
        silu_za = za * sza
        a_act = silu_za * bg * conv
        rn = rn_ref[...]
        szb = _sigmoid(zb)
        silu_zb = zb * szb
        rg = rn * gnw
        b_act = silu_zb * rg
        y_a = _dot(a_act.astype(BF16), wa_ref[...])
        y_b = _dot(b_act.astype(BF16), wb_ref[...])
        sga, sgb = _sigmoid(ga), _sigmoid(gb)
        mix = sga * y_a + sgb * y_b
        y = _dot(mix.astype(BF16), wo_ref[...])
        x1 = x_ref[...] + gx * y
        r1 = lax.rsqrt(jnp.mean(x1 * x1, axis=-1, keepdims=True) + EPS)
        xh1 = x1 * r1
        err = xh1 * fw - t_ref[...]
        loss = jnp.sum(jnp.sum(err * err, axis=1, keepdims=True), axis=0, keepdims=True) * (0.5 / D)
        dout = err * (1.0 / D)
        dxh = dout * fw
        dx1 = r1 * (dxh - xh1 * jnp.mean(dxh * xh1, axis=-1, keepdims=True))
        dx1_ref[...] = dx1
        dy = (dx1 * gx).astype(BF16)
        dmix = _dot_nt(dy, wo_ref[...])
        dya = (dmix * sga).astype(BF16)
        dyb = (dmix * sgb).astype(BF16)
        da = _dot_nt(dya, wa_ref[...])
        db = _dot_nt(dyb, wb_ref[...])
        dpa_ref[:, 0:D] = (da * silu_za * conv).astype(BF16)
        dpa_ref[:, D:2 * D] = (da * bg * conv * (sza * (1.0 + za * (1.0 - sza)))).astype(BF16)
        dpa_ref[:, 2 * D:3 * D] = (db * rg * (szb * (1.0 + zb * (1.0 - szb)))).astype(BF16)
        dpa_ref[:, 3 * D:4 * D] = (dmix * y_a * sga * (1.0 - sga)).astype(BF16)
        dpa_ref[:, 4 * D:5 * D] = (dmix * y_b * sgb * (1.0 - sgb)).astype(BF16)
        dconv_ref[...] = (da * silu_za * bg).astype(BF16)
        drn_n = db * silu_zb
        drn = drn_n * gnw
        rstd = rstd_ref[...]
        for hd in range(N_HEADS):
            sl = slice(hd * DV, (hd + 1) * DV)
            dh_, rh = drn[:, sl], rn[:, sl]
            m1 = jnp.mean(dh_, axis=-1, keepdims=True)
            m2 = jnp.mean(dh_ * rh, axis=-1, keepdims=True)
            dret_ref[:, sl] = (rstd[:, sl] * (dh_ - m1 - rh * m2)).astype(BF16)
        at_ref[0] = a_act.T.astype(BF16)
        at_ref[1] = b_act.T.astype(BF16)
        at_ref[2] = mix.T.astype(BF16)
        b_ref[0] = dya
        b_ref[1] = dyb
        b_ref[2] = dy

        @pl.when(i == 0)
        def _():
            part_ref[...] = jnp.zeros((8, D), F32)

        part_ref[0:1, :] += jnp.sum(dout * xh1, axis=0, keepdims=True)
        part_ref[1:2, :] += jnp.sum(dx1 * y, axis=0, keepdims=True)
        part_ref[2:3, :] += jnp.sum(drn_n * rn, axis=0, keepdims=True)
        part_ref[3:4, :] += jnp.broadcast_to(loss, (1, D))

    col = lambda cb_: pl.BlockSpec((TM, D), lambda i, cb_=cb_: (i, cb_))
    prev = lambda cb_: pl.BlockSpec((HALO, D), lambda i, cb_=cb_: (jnp.maximum(i * hb - 1, 0), cb_))
    nxt = lambda cb_: pl.BlockSpec((HALO, D), lambda i, cb_=cb_: (jnp.minimum((i + 1) * hb, n_tiles * hb - 1), cb_))
    tile = pl.BlockSpec((TM, D), lambda i: (i, 0))
    full = lambda a: pl.BlockSpec(a.shape, lambda i: (0,) * a.ndim)
    return pl.pallas_call(
        body, name="merge", grid=(n_tiles,),
        in_specs=[col(CB_H), col(CB_BG), col(CB_CG), col(CB_ZA), col(CB_ZB), col(CB_GA), col(CB_GB),
                  prev(CB_H), nxt(CB_H), prev(CB_CG), nxt(CB_CG),
                  tile, tile, tile, tile, full(w_a), full(w_b), full(w_out), full(vecs)],
        out_specs=[tile, pl.BlockSpec((TM, 5 * D), lambda i: (i, 0)), tile, tile,
                   pl.BlockSpec((3, D, TM), lambda i: (0, 0, i)), pl.BlockSpec((3, TM, D), lambda i: (0, i, 0)),
                   pl.BlockSpec((8, D), lambda i: (0, 0))],
        out_shape=[jax.ShapeDtypeStruct((seq, D), F32), jax.ShapeDtypeStruct((seq, 5 * D), BF16),
                   jax.ShapeDtypeStruct((seq, D), BF16), jax.ShapeDtypeStruct((seq, D), BF16),
                   jax.ShapeDtypeStruct((3, D, seq), BF16), jax.ShapeDtypeStruct((3, seq, D), BF16),
                   jax.ShapeDtypeStruct((8, D), F32)],
        compiler_params=_cparams(dimension_semantics=("arbitrary",)),
    )(p_ext, p_ext, p_ext, p_ext, p_ext, p_ext, p_ext, p_ext, p_ext, p_ext, p_ext,
      rn, rstd, x, target, w_a, w_b, w_out, vecs)


def _conv_backward(p_ext, dconv, dpa, dq, dk, dv, vecs, seq):
    n_tiles = seq // TM
    hb = TM // HALO
    qk = N_HEADS * DK

    def body(h_ref, cg_ref, dc_ref, dcp_ref, dcn_ref, dpa_ref, dq_ref, dk_ref, dv_ref, vec_ref, dp_ref, part_ref):
        i = pl.program_id(0)
        dp_ref[:, CB_QK * D:CB_QK * D + qk] = dq_ref[...]
        dp_ref[:, CB_QK * D + qk:CB_V * D] = dk_ref[...]
        dp_ref[:, CB_V * D:CB_ZB * D] = dv_ref[...]

        @pl.when(i == 0)
        def _():
            part_ref[...] = jnp.zeros((8, D), F32)

        @pl.when(i == n_tiles)
        def _():
            dp_ref[:, 0:CB_QK * D] = jnp.zeros((TM, CB_QK * D), BF16)
            dp_ref[:, CB_ZB * D:PW] = jnp.zeros((TM, PW - CB_ZB * D), BF16)

        @pl.when(i < n_tiles)
        def _():
            f = lambda ref: ref[...].astype(F32)
            h, cg, dc = f(h_ref), f(cg_ref), f(dc_ref)
            w0, w1, w2 = vec_ref[1:2, :], vec_ref[2:3, :], vec_ref[3:4, :]
            row = lax.broadcasted_iota(jnp.int32, (TM, 1), 0)
            dc_prev = jnp.where(i == 0, 0.0, f(dcp_ref)[HALO - 1:HALO, :])
            dc_next = jnp.where(i == n_tiles - 1, 0.0, f(dcn_ref)[0:1, :])
            dc_up = jnp.where(row == 0, dc_prev, pltpu.roll(dc, 1, 0))
            dc_dn = jnp.where(row == TM - 1, dc_next, pltpu.roll(dc, TM - 1, 0))
            du = w0 * dc_dn + w1 * dc + w2 * dc_up
            u = cg * h
            dp_ref[:, CB_H * D:(CB_H + 1) * D] = (du * cg).astype(BF16)
            dp_ref[:, CB_CG * D:(CB_CG + 1) * D] = (du * h).astype(BF16)
            dp_ref[:, CB_BG * D:(CB_BG + 1) * D] = dpa_ref[:, 0:D]
            dp_ref[:, CB_ZA * D:(CB_ZA + 1) * D] = dpa_ref[:, D:2 * D]
            dp_ref[:, CB_ZB * D:(CB_GB + 1) * D] = dpa_ref[:, 2 * D:5 * D]
            part_ref[0:1, :] += jnp.sum(u * dc_dn, axis=0, keepdims=True)
            part_ref[1:2, :] += jnp.sum(u * dc, axis=0, keepdims=True)
            part_ref[2:3, :] += jnp.sum(u * dc_up, axis=0, keepdims=True)
            part_ref[3:4, :] += jnp.sum(dc, axis=0, keepdims=True)

    last = n_tiles - 1
    col = lambda cb_: pl.BlockSpec((TM, D), lambda i, cb_=cb_: (jnp.minimum(i, last), cb_))
    lat = lambda w: pl.BlockSpec((TM, w), lambda i: (jnp.minimum(i, last), 0))
    ext = lambda w: pl.BlockSpec((TM, w), lambda i: (i, 0))
    return pl.pallas_call(
        body, name="conv_backward", grid=(n_tiles + 1,),
        in_specs=[col(CB_H), col(CB_CG), lat(D),
                  pl.BlockSpec((HALO, D), lambda i: (jnp.clip(i * hb - 1, 0, n_tiles * hb - 1), 0)),
                  pl.BlockSpec((HALO, D), lambda i: (jnp.minimum((i + 1) * hb, n_tiles * hb - 1), 0)),
                  lat(5 * D), ext(qk), ext(qk), ext(D),
                  pl.BlockSpec(vecs.shape, lambda i: (0, 0))],
        out_specs=[ext(PW), pl.BlockSpec((8, D), lambda i: (0, 0))],
        out_shape=[jax.ShapeDtypeStruct((seq + TM, PW), BF16), jax.ShapeDtypeStruct((8, D), F32)],
        compiler_params=_cparams(dimension_semantics=("arbitrary",)),
    )(p_ext, p_ext, dconv, dconv, dconv, dpa, dq, dk, dv, vecs)


def _input_backward(dp, w_in, x, ctx, dx1, modx, modc, norm_w):
    seq = x.shape[0]
    lext = seq + ctx.shape[0]
    n_x = seq // TM

    def body(dp_ref, w_ref, x_ref, ctx_ref, dx1_ref, mx_ref, mc_ref, nw_ref, gx_ref, part_ref):
        i = pl.program_id(0)
        is_ctx = i >= n_x
        dxm = _dot_nt(dp_ref[:, 0:WSH], w_ref[0])
        for d in range(1, N_DEV):
            dxm = dxm + _dot_nt(dp_ref[:, d * WSH:(d + 1) * WSH], w_ref[d])
        x = jnp.where(is_ctx, ctx_ref[...], x_ref[...])
        r = lax.rsqrt(jnp.mean(x * x, axis=-1, keepdims=True) + EPS)
        xh = x * r
        nw = nw_ref[...]
        sc = jnp.where(is_ctx, mc_ref[1:2, :], mx_ref[1:2, :])
        dxn = dxm * (1.0 + sc)
        dxh = dxn * nw
        dx = r * (dxh - xh * jnp.mean(dxh * xh, axis=-1, keepdims=True))

        @pl.when(jnp.logical_not(is_ctx))
        def _():
            gx_ref[...] = dx1_ref[...] + dx

        @pl.when(i == 0)
        def _():
            part_ref[...] = jnp.zeros((8, D), F32)

        fx = jnp.where(is_ctx, 0.0, 1.0)
        d_shift = jnp.sum(dxm, axis=0, keepdims=True)
        d_scale = jnp.sum(dxm * (xh * nw), axis=0, keepdims=True)
        part_ref[0:1, :] += fx * d_shift
        part_ref[1:2, :] += fx * d_scale
        part_ref[2:3, :] += jnp.sum(dxn * xh, axis=0, keepdims=True)
        part_ref[3:4, :] += (1.0 - fx) * d_shift
        part_ref[4:5, :] += (1.0 - fx) * d_scale

    lat = lambda w: pl.BlockSpec((TM, w), lambda i: (jnp.minimum(i, n_x - 1), 0))
    ext = lambda w: pl.BlockSpec((TM, w), lambda i: (i, 0))
    full = lambda a: pl.BlockSpec(a.shape, lambda i: (0,) * a.ndim)
    return pl.pallas_call(
        body, name="input_backward", grid=(lext // TM,),
        in_specs=[ext(PW), full(w_in), lat(D), full(ctx), lat(D), full(modx), full(modc), full(norm_w)],
        out_specs=[lat(D), pl.BlockSpec((8, D), lambda i: (0, 0))],
        out_shape=[jax.ShapeDtypeStruct((seq, D), F32), jax.ShapeDtypeStruct((8, D), F32)],
        compiler_params=_cparams(dimension_semantics=("arbitrary",)),
    )(dp, w_in, x, ctx, dx1, modx, modc, norm_w)


def _weight_grad_in(xm_t, dp, owners, name):
    lext = xm_t.shape[1]
    tk = lext // 2
    n_k = lext // tk

    def body(own_ref, a_ref, b_ref, o_ref, acc):
        k = pl.program_id(1)

        @pl.when(k == 0)
        def _():
            acc[...] = jnp.zeros_like(acc)

        acc[...] += _dot(a_ref[...], b_ref[...])

        @pl.when(k == n_k - 1)
        def _():
            o_ref[0] = acc[...]

    grid_spec = pltpu.PrefetchScalarGridSpec(
        num_scalar_prefetch=1, grid=(4, n_k),
        in_specs=[pl.BlockSpec((D, tk), lambda j, k, own: (0, k)),
                  pl.BlockSpec((tk, WSH), lambda j, k, own: (k, own[j]))],
        out_specs=pl.BlockSpec((1, D, WSH), lambda j, k, own: (j, 0, 0)),
        scratch_shapes=[pltpu.VMEM((D, WSH), F32)])
    return pl.pallas_call(
        body, name=name, grid_spec=grid_spec,
        out_shape=jax.ShapeDtypeStruct((4, D, WSH), F32),
        compiler_params=_cparams(dimension_semantics=("arbitrary", "arbitrary")),
    )(owners, xm_t, dp)


def _weight_grad_square(a_t, b):
    seq = a_t.shape[2]
    tk = seq // 2

    def body(a_ref, b_ref, o_ref, acc):
        k = pl.program_id(1)

        @pl.when(k == 0)
        def _():
            acc[...] = jnp.zeros_like(acc)

        acc[...] += _dot(a_ref[0], b_ref[0])

        @pl.when(k == pl.num_programs(1) - 1)
        def _():
            o_ref[:, 0] = acc[...].reshape(N_DEV, RSH, D)

    return pl.pallas_call(
        body, name="weight_grad_square", grid=(3, seq // tk),
        in_specs=[pl.BlockSpec((1, D, tk), lambda t, k: (t, 0, k)), pl.BlockSpec((1, tk, D), lambda t, k: (t, k, 0))],
        out_specs=pl.BlockSpec((N_DEV, 1, RSH, D), lambda t, k: (0, t, 0, 0)),
        out_shape=jax.ShapeDtypeStruct((N_DEV, 3, RSH, D), F32),
        scratch_shapes=[pltpu.VMEM((D, D), F32)],
        compiler_params=_cparams(dimension_semantics=("arbitrary", "arbitrary")),
    )(a_t, b)


def _rope_tables(seq, ctx_len):
    rows = seq // GRID_W
    row = jnp.repeat(jnp.arange(rows), GRID_W).astype(F32)
    col = jnp.tile(jnp.arange(GRID_W), rows).astype(F32)
    nf = DK // 4
    inv = ROPE_BASE ** (-jnp.arange(nf, dtype=F32) / nf)
    ang = jnp.concatenate([row[:, None] * inv, col[:, None] * inv], axis=-1)
    cos, sin = jnp.cos(ang), jnp.sin(ang)
    cos_t = jnp.tile(cos, (1, 4))
    sin_t = jnp.tile(jnp.concatenate([-sin, sin], axis=-1), (1, 2))
    cos_t = jnp.concatenate([cos_t, jnp.ones((ctx_len, LANES), F32)], axis=0)
    sin_t = jnp.concatenate([sin_t, jnp.zeros((ctx_len, LANES), F32)], axis=0)
    return cos_t, sin_t


def _tie(value, token):
    return value + token[0, 0].astype(value.dtype)


def _pad_lanes(a):
    return jnp.pad(a, ((0, 0), (0, D - a.shape[1])))


def _rows(total, *parts):
    width = max(a.shape[1] for _, a in parts)
    out = None
    for row, a in parts:
        padded = jnp.pad(a, ((row, total - row - a.shape[0]), (0, width - a.shape[1])))
        out = padded if out is None else out + padded
    return out


def kernel(x, c, ctx, c_ctx, norm_w, ada_w, ada_b, w_in, conv_w, conv_b, decay_logit, gn_w, w_a, w_b, w_out, final_norm_w, loss_target, m_c_ctx, m_norm_w, m_ada_w, m_ada_b, m_w_in, m_conv_w, m_conv_b, m_decay_logit, m_gn_w, m_w_a, m_w_b, m_w_out, m_final_norm_w, v_c_ctx, v_norm_w, v_ada_w, v_ada_b, v_w_in, v_conv_w, v_conv_b, v_decay_logit, v_gn_w, v_w_a, v_w_b, v_w_out, v_final_norm_w):
    xi, yi, ci = _coords()
    me = 4 * xi + 2 * yi + ci
    chip = 2 * xi + yi
    seq, ctx_len = x.shape[1], ctx.shape[1]
    assert seq % TM == 0 and seq % RET_C == 0 and ctx_len == TM and seq % GRID_W == 0
    csh = D // N_DEV

    blk = jnp.pad(c, ((0, 7), (0, 0))) + jnp.pad(conv_w[0], ((1, 4), (0, D - csh)))
    got = _all_gather_small(blk, "gather_cond")
    conv_w_all = got[:, 1:4, 0:csh].transpose(1, 0, 2).reshape(3, D)
    c16 = _rows(16, (0, got[:, 0, :]), (N_DEV, c_ctx[None]))
    ada_b_sh = lax.dynamic_slice(ada_b, (0, me * ADA_SH), (1, ADA_SH))
    mod_sh, act16, lg = _modulation(c16, ada_w[0], ada_b_sh, decay_logit[0])
    mod_all = _all_gather_small(mod_sh, "gather_mod").transpose(1, 0, 2).reshape(16, 3 * D)
    modx = lax.dynamic_slice(mod_all, (me, 0), (1, 3 * D)).reshape(3, D)
    modc = mod_all[8].reshape(3, D)

    w_in_all, gathered = _all_gather_weights([w_in[0].astype(BF16)])
    sq_shards = [_tie(w[0].astype(BF16), gathered) for w in (w_a, w_b, w_out)]
    ag_sq = _Exchange("gather_square", sq_shards, [jax.ShapeDtypeStruct((N_DEV, RSH, D), BF16)] * 3,
                      3 * (N_DEV - 1), _gather_plan)

    x2, tgt = x[0], loss_target[0]
    cos_t, sin_t = _rope_tables(seq, ctx_len)
    modx = _tie(modx, ag_sq.token)
    p_ext, xm_t = _in_projection(x2, ctx[0], modx, modc, norm_w, w_in_all, cos_t, sin_t)
    rn, rstd = _retention_forward(p_ext, lg, seq, ctx_len)
    _, sq_lands = ag_sq.wait(rstd)
    w_a_all, w_b_all, w_out_all = (
        lax.dynamic_update_slice(land, shard[None], (me, 0, 0)).reshape(D, D) for land, shard in zip(sq_lands, sq_shards))
    vecs = _rows(8, (0, modx[2:3]), (1, conv_w_all), (4, conv_b), (5, gn_w), (6, final_norm_w[None]))
    dx1, dpa, dconv, dret, op_at, op_b, part_m = _merge(p_ext, rn, rstd, x2, tgt, w_a_all, w_b_all, w_out_all, vecs, seq)

    j4 = jnp.arange(4, dtype=jnp.int32)
    owners = (2 * jnp.bitwise_xor(chip, j4) + ci).astype(jnp.int32)
    owners_sib = (2 * jnp.bitwise_xor(chip, j4) + (1 - ci)).astype(jnp.int32)
    gw_sq = _weight_grad_square(op_at, op_b).reshape(N_DEV, 3 * RSH, D)
    rs_sq_pair = _Exchange("rs_square_pair", [gw_sq], [jax.ShapeDtypeStruct((4, 3 * RSH, D), F32)], 4,
                           _pair_plan(lambda j, chip_, c_: 2 * jnp.bitwise_xor(chip_, j) + (1 - c_)))
    dq, dk, dv, dlg = _retention_backward(p_ext, dret, _tie(lg, rs_sq_pair.token), cos_t, sin_t, seq, ctx_len)
    (gw_sq,), (r1_sq,) = rs_sq_pair.wait(dlg)
    own_sq, send_sq = _pair_sum(gw_sq, r1_sq, owners, "pair_sum_square")
    rs_sq_chips = _Exchange("rs_square_chips", [send_sq], [jax.ShapeDtypeStruct((3, 3 * RSH, D), BF16)], 3, _chips_plan)
    dp, part_c = _conv_backward(p_ext, dconv, dpa, dq, dk, dv, _tie(vecs, rs_sq_chips.token), seq)
    gw_sib = _weight_grad_in(xm_t, dp, owners_sib, "weight_grad_in_sibling")
    rs_in_pair = _Exchange("rs_in_pair", [gw_sib], [jax.ShapeDtypeStruct((4, D, WSH), F32)], 4,
                           _pair_plan(lambda j, chip_, c_: j))
    gw_own = _weight_grad_in(xm_t, dp, _tie(owners, rs_in_pair.token), "weight_grad_in_own")
    _, (r1_in,) = rs_in_pair.wait(gw_own)
    own_in, send_in = _pair_sum(gw_own, r1_in, j4, "pair_sum_in")
    rs_in_chips = _Exchange("rs_in_chips", [send_in], [jax.ShapeDtypeStruct((3, D, WSH), BF16)], 3, _chips_plan)
    grad_x, part_i = _input_backward(dp, w_in_all, x2, ctx[0], dx1, _tie(modx, rs_in_chips.token), modc, norm_w)

    dl = dlg[:, 0, 0:4]
    dlg_row = jnp.pad(dl[:, 0::2].reshape(1, N_HEADS), ((0, 0), (0, D - N_HEADS))) + jnp.pad(
        dl[:, 1::2].reshape(1, N_HEADS), ((0, 0), (N_HEADS, D - 2 * N_HEADS)))
    partials = _rows(16, (0, part_m[0:1]), (1, part_i[2:3]), (2, part_c[3:4]), (3, part_m[2:3]), (4, part_c[0:3]),
                     (7, part_i[0:2]), (9, part_m[1:2]), (10, part_i[3:5]), (13, part_m[3:4]), (14, dlg_row))
    got = _all_gather_small(partials, "gather_partials")
    tot = _sum_devices(got, "sum_partials")
    dmodc = tot[10:13].reshape(1, 3 * D)
    dmod16 = _rows(16, (0, got[:, 7:10, :].reshape(N_DEV, 3 * D)), (N_DEV, dmodc))
    dmod16 = lax.dynamic_slice(dmod16, (0, me * ADA_SH), (16, ADA_SH))
    dmodc8 = _rows(8, (0, dmod16[8:9]))
    g_ada_w, d_ada_w, nm_ada_w, nv_ada_w, cctx_part = _ada_backward(
        act16.T, dmod16, dmodc8, ada_w[0], m_ada_w[0], v_ada_w[0])
    dact_ctx = _sum_devices(_all_gather_small(cctx_part, "gather_cctx"), "sum_cctx")[0:1]

    my_lanes = lambda a: _pad_lanes(lax.dynamic_slice(a, (0, me * csh), (a.shape[0], csh)))
    small = lambda cc, nw, cb, gn, fn, ab, dlt, cw: _rows(
        16, (0, cc[None]), (1, nw), (2, cb), (3, gn), (4, fn[None]), (5, ab.reshape(3, D)),
        (8, dlt.reshape(1, 2 * N_HEADS)), (9, cw[0]))
    w_sm = small(c_ctx, norm_w, conv_b, gn_w, final_norm_w, ada_b, decay_logit, conv_w)
    m_sm = small(m_c_ctx, m_norm_w, m_conv_b, m_gn_w, m_final_norm_w, m_ada_b, m_decay_logit, m_conv_w)
    v_sm = small(v_c_ctx, v_norm_w, v_conv_b, v_gn_w, v_final_norm_w, v_ada_b, v_decay_logit, v_conv_w)
    g_sm = _rows(16, (0, dact_ctx), (1, tot[1:4]), (4, tot[0:1]), (5, tot[7:10]), (8, tot[14:15]), (9, my_lanes(tot[4:7])))
    g_sm2 = _rows(16, (5, tot[10:13]))
    sm = _adam_small(w_sm, g_sm, g_sm2, m_sm, v_sm, 0, 8)

    _, (r2_sq,) = rs_sq_chips.wait(sm[0])
    _, (r2_in,) = rs_in_chips.wait(sm[0])
    g_w_in, d_w_in, nm_w_in, nv_w_in = _adam_sharded(w_in[0], m_w_in[0], v_w_in[0], own_in, r2_in, "adam_w_in")
    stack = lambda a, b, c_: jnp.concatenate([a[0], b[0], c_[0]], axis=0)
    sq = _adam_sharded(stack(w_a, w_b, w_out), stack(m_w_a, m_w_b, m_w_out), stack(v_w_a, v_w_b, v_w_out),
                       own_sq, r2_sq, "adam_square")
    sq = [o.reshape(3, 1, RSH, D) for o in sq]

    def unpack(a, sq_i):
        return [a[0], a[1:2], None, a[5:8].reshape(1, 3 * D), None, a[9:12, 0:csh][None], a[2:3],
                a[8, 0:2 * N_HEADS].reshape(1, 2, N_HEADS), a[3:4], sq_i[0], sq_i[1], sq_i[2], a[4]]

    outs = []
    for a, sq_i, ada_i, in_i in zip(sm, sq, (g_ada_w, d_ada_w, nm_ada_w, nv_ada_w), (g_w_in, d_w_in, nm_w_in, nv_w_in)):
        group = unpack(a, sq_i)
        group[2] = ada_i[None]
        group[4] = in_i[None]
        outs += group
    return (tot[13, 0], grad_x[None], *outs)
```

```python
import functools
import math

import jax
import jax.numpy as jnp
from jax import lax
from jax.experimental import pallas as pl
from jax.experimental.pallas import tpu as pltpu

F32 = jnp.float32
BF16 = jnp.bfloat16
MESH = pl.DeviceIdType.MESH
_HBM_SPEC = pl.BlockSpec(memory_space=pltpu.HBM)
_SEM_SPEC = pl.BlockSpec(memory_space=pltpu.SEMAPHORE)
_EFFECT = pltpu.SideEffectType.DATAFLOW_SIDE_EFFECTING

N_DEV = 8
D = 1024
N_HEADS = 8
DK = 64
DV = 128
GRID_W = 64
ROPE_BASE = 10000.0
EPS = 1e-6
PW = 9 * D
WSH = PW // N_DEV
RSH = D // N_DEV
ADA_SH = 3 * D // N_DEV
TM = 256
RET_C = 256
HALO = 16
LANES = 128
VMEM_LIMIT = 60 * 1024 * 1024

ADAM_LR = 0.001
ADAM_B1 = 0.9
ADAM_B2 = 0.999
ADAM_EPS = 1e-08
ADAM_WD = 0.01
ADAM_STEP = 10

CB_H, CB_BG, CB_CG, CB_ZA, CB_QK, CB_V, CB_ZB, CB_GA, CB_GB = range(9)


def _cparams(**kw):
    return pltpu.CompilerParams(vmem_limit_bytes=VMEM_LIMIT, **kw)


def _dot(a, b):
    return jnp.dot(a, b, preferred_element_type=F32)


def _dot_nt(a, b):
    return lax.dot_general(a, b, (((1,), (1,)), ((), ())), preferred_element_type=F32)


def _dot_tn(a, b):
    return lax.dot_general(a, b, (((0,), (0,)), ((), ())), preferred_element_type=F32)


def _sigmoid(z):
    return 1.0 / (1.0 + jnp.exp(-z))


def _row_tile(rows):
    return TM if rows % TM == 0 else rows


def _coords():
    return lax.axis_index("x"), lax.axis_index("y"), lax.axis_index("c")


def _flip(v, bit):
    return 1 - v if bit else v


def _all_gather_small(blk, name):
    rows, cols = blk.shape

    def body(x_ref, out_ref, send_sems, recv_sems, local_sem):
        x, y, c = _coords()
        me = 4 * x + 2 * y + c
        mine = pltpu.make_async_copy(x_ref, out_ref.at[me], local_sem)
        mine.start()

        def copy(k, slot):
            peer = (_flip(x, k & 4), _flip(y, k & 2), _flip(c, k & 1))
            return pltpu.make_async_remote_copy(
                src_ref=x_ref, dst_ref=out_ref.at[slot], send_sem=send_sems.at[k - 1],
                recv_sem=recv_sems.at[k - 1], device_id=peer, device_id_type=MESH)

        for k in range(1, N_DEV):
            copy(k, me).start()
        for k in range(1, N_DEV):
            copy(k, jnp.bitwise_xor(me, k)).wait_recv()
        for k in range(1, N_DEV):
            copy(k, me).wait_send()
        mine.wait()

    return pl.pallas_call(
        body, name=name,
        out_shape=jax.ShapeDtypeStruct((N_DEV, rows, cols), blk.dtype),
        in_specs=[pl.BlockSpec(memory_space=pltpu.VMEM)],
        out_specs=pl.BlockSpec(memory_space=pltpu.VMEM),
        scratch_shapes=[pltpu.SemaphoreType.DMA((N_DEV - 1,)), pltpu.SemaphoreType.DMA((N_DEV - 1,)),
                        pltpu.SemaphoreType.DMA],
    )(blk)


def _all_gather_weights(shards):
    n = len(shards)

    def body(*refs):
        srcs, outs, done = refs[:n], refs[n:2 * n], refs[2 * n]
        send_sems, recv_sems, local_sems = refs[2 * n + 1:]
        x, y, c = _coords()
        me = 4 * x + 2 * y + c
        sib = (x, y, 1 - c)
        chips = [(1 - x, y), (x, 1 - y), (1 - x, 1 - y)]

        def slot(px, py, pc):
            return 4 * px + 2 * py + pc

        def copy(a, k, block_slot, to, own=False):
            return pltpu.make_async_remote_copy(
                src_ref=srcs[a] if own else outs[a].at[block_slot], dst_ref=outs[a].at[block_slot],
                send_sem=send_sems.at[7 * a + k], recv_sem=recv_sems.at[7 * a + k],
                device_id=to, device_id_type=MESH)

        locals_ = [pltpu.make_async_copy(srcs[a], outs[a].at[me], local_sems.at[a]) for a in range(n)]
        for cp in locals_:
            cp.start()
        sent = []
        for a in range(n):
            sent.append(copy(a, 0, me, sib, own=True))
            for j, chip in enumerate(chips):
                sent.append(copy(a, 1 + j, me, (*chip, c), own=True))
        for cp in sent:
            cp.start()
        for j, chip in enumerate(chips):
            for a in range(n):
                copy(a, 1 + j, slot(*chip, c), sib).wait_recv()
                fwd = copy(a, 4 + j, slot(*chip, c), sib)
                fwd.start()
                sent.append(fwd)
        for a in range(n):
            copy(a, 0, slot(x, y, 1 - c), sib).wait_recv()
            for j, chip in enumerate(chips):
                copy(a, 4 + j, slot(*chip, 1 - c), sib).wait_recv()
        for cp in sent:
            cp.wait_send()
        for cp in locals_:
            cp.wait()
        done[...] = jnp.zeros((8, LANES), F32)

    anyspec = pl.BlockSpec(memory_space=pl.ANY)
    return pl.pallas_call(
        body, name="all_gather_weights",
        out_shape=[jax.ShapeDtypeStruct((N_DEV,) + s.shape, s.dtype) for s in shards]
        + [jax.ShapeDtypeStruct((8, LANES), F32)],
        in_specs=[anyspec] * n, out_specs=[anyspec] * n + [pl.BlockSpec(memory_space=pltpu.VMEM)],
        scratch_shapes=[pltpu.SemaphoreType.DMA((7 * n,)), pltpu.SemaphoreType.DMA((7 * n,)),
                        pltpu.SemaphoreType.DMA((n,))],
    )(*shards)


class _Exchange:
    def __init__(self, name, srcs, land_shapes, n_copies, plan):
        self.name, self.plan, self.n_copies = name, plan, n_copies
        self.n_src, self.n_land = len(srcs), len(land_shapes)
        hbm = lambda a: pltpu.HBM(a.shape, a.dtype)
        n = self.n_src + self.n_land
        lands = [pltpu.with_memory_space_constraint(lax.empty(s.shape, s.dtype), pltpu.HBM) for s in land_shapes]
        srcs = [pltpu.with_memory_space_constraint(s, pltpu.HBM) for s in srcs]

        def body(*refs):
            send_sems, recv_sems = refs[n], refs[n + 1]
            for cp in self._copies(refs, send_sems, recv_sems):
                cp.start()
            refs[-1][...] = jnp.zeros((8, LANES), F32)

        outs = pl.pallas_call(
            body, name=name + "_start",
            out_shape=(pltpu.SemaphoreType.DMA((n_copies,)), pltpu.SemaphoreType.DMA((n_copies,)),
                       *[hbm(a) for a in srcs], *[hbm(a) for a in lands], jax.ShapeDtypeStruct((8, LANES), F32)),
            in_specs=[_HBM_SPEC] * n,
            out_specs=(_SEM_SPEC, _SEM_SPEC, *[_HBM_SPEC] * n, pl.BlockSpec(memory_space=pltpu.VMEM)),
            input_output_aliases={i: 2 + i for i in range(n)},
            compiler_params=pltpu.CompilerParams(has_side_effects=_EFFECT),
        )(*srcs, *lands)
        self.send_sems, self.recv_sems = outs[0], outs[1]
        self.buffers = list(outs[2:2 + n])
        self.token = outs[-1]

    def _copies(self, refs, send_sems, recv_sems):
        src_refs, land_refs = refs[:self.n_src], refs[self.n_src:self.n_src + self.n_land]
        return [pltpu.make_async_remote_copy(src_ref=s, dst_ref=d, send_sem=send_sems.at[k], recv_sem=recv_sems.at[k],
                                             device_id=dev, device_id_type=MESH)
                for k, (s, d, dev) in enumerate(self.plan(src_refs, land_refs))]

    def wait(self, after):
        n = self.n_src + self.n_land

        def body(*refs):
            for cp in self._copies(refs, refs[n], refs[n + 1]):
                cp.wait_send()
                cp.wait_recv()

        outs = pl.pallas_call(
            body, name=self.name + "_wait",
            out_shape=tuple(pltpu.HBM(b.shape, b.dtype) for b in self.buffers),
            in_specs=[_HBM_SPEC] * n + [_SEM_SPEC, _SEM_SPEC, pl.BlockSpec(memory_space=pl.ANY)],
            out_specs=tuple([_HBM_SPEC] * n),
            input_output_aliases={i: i for i in range(n)},
            compiler_params=pltpu.CompilerParams(has_side_effects=_EFFECT),
        )(*self.buffers, self.send_sems, self.recv_sems, after)
        return list(outs[:self.n_src]), list(outs[self.n_src:])


def _pair_plan(src_index):
    def plan(srcs, lands):
        x, y, c = _coords()
        chip = 2 * x + y
        return [(srcs[0].at[src_index(j, chip, c)], lands[0].at[j], (x, y, 1 - c)) for j in range(4)]
    return plan


def _chips_plan(srcs, lands):
    x, y, c = _coords()
    return [(srcs[0].at[j - 1], lands[0].at[j - 1], (_flip(x, j & 2), _flip(y, j & 1), c)) for j in range(1, 4)]


def _gather_plan(srcs, lands):
    x, y, c = _coords()
    me = 4 * x + 2 * y + c
    return [(srcs[a], lands[a].at[me], (_flip(x, k & 4), _flip(y, k & 2), _flip(c, k & 1)))
            for a in range(len(srcs)) for k in range(1, N_DEV)]


def _pair_sum(grad, recv, owners, name):
    _, rows, cols = grad.shape
    tr = _row_tile(rows)

    def body(own_ref, g_ref, r_ref, mine_ref, send_ref):
        j = pl.program_id(1)
        total = g_ref[...] + r_ref[...]

        @pl.when(j == 0)
        def _():
            mine_ref[...] = total

        @pl.when(j > 0)
        def _():
            send_ref[...] = total.astype(BF16)

    grid_spec = pltpu.PrefetchScalarGridSpec(
        num_scalar_prefetch=1, grid=(rows // tr, 4),
        in_specs=[pl.BlockSpec((1, tr, cols), lambda i, j, own: (own[j], i, 0)),
                  pl.BlockSpec((1, tr, cols), lambda i, j, own: (j, i, 0))],
        out_specs=[pl.BlockSpec((1, tr, cols), lambda i, j, own: (0, i, 0)),
                   pl.BlockSpec((1, tr, cols), lambda i, j, own: (jnp.maximum(j - 1, 0), i, 0))])
    return pl.pallas_call(
        body, name=name, grid_spec=grid_spec,
        out_shape=[jax.ShapeDtypeStruct((1, rows, cols), F32), jax.ShapeDtypeStruct((3, rows, cols), BF16)],
        compiler_params=_cparams(dimension_semantics=("arbitrary", "arbitrary")),
    )(owners, grad, recv)


def _modulation(c16, ada_w_sh, ada_b_sh, decay_logit):
    def body(c_ref, w_ref, b_ref, dl_ref, mod_ref, act_ref, lg_ref):
        cv = c_ref[...]
        act = cv * _sigmoid(cv)
        act_ref[...] = act
        mod_ref[...] = jnp.dot(act, w_ref[...], preferred_element_type=F32,
                               precision=lax.Precision.HIGHEST) + b_ref[...]
        z = dl_ref[...]
        lg_ref[...] = jnp.minimum(z, 0.0) - jnp.log(1.0 + jnp.exp(-jnp.abs(z)))

    return pl.pallas_call(
        body, name="modulation",
        out_shape=[jax.ShapeDtypeStruct((16, ADA_SH), F32), jax.ShapeDtypeStruct((16, D), F32),
                   jax.ShapeDtypeStruct(decay_logit.shape, F32)],
        compiler_params=_cparams(),
    )(c16, ada_w_sh, ada_b_sh, decay_logit)


def _adam_update(w, g, m, v):
    m2 = ADAM_B1 * m + (1.0 - ADAM_B1) * g
    v2 = ADAM_B2 * v + (1.0 - ADAM_B2) * (g * g)
    m_hat = m2 / (1.0 - ADAM_B1 ** ADAM_STEP)
    v_hat = v2 / (1.0 - ADAM_B2 ** ADAM_STEP)
    delta = -ADAM_LR * (m_hat / (jnp.sqrt(v_hat) + ADAM_EPS) + ADAM_WD * w)
    return delta, m2, v2


def _adam_sharded(w, m, v, own, recv, name):
    rows, cols = w.shape
    tr = _row_tile(rows)

    def body(w_ref, m_ref, v_ref, g0, g1, g2, g3, g_ref, d_ref, m_out, v_out):
        g = ((g0[0] + g1[0].astype(F32)) + g2[0].astype(F32)) + g3[0].astype(F32)
        delta, m2, v2 = _adam_update(w_ref[...], g, m_ref[...], v_ref[...])
        g_ref[...] = g
        d_ref[...] = delta
        m_out[...] = m2
        v_out[...] = v2

    flat = pl.BlockSpec((tr, cols), lambda i: (i, 0))
    part = lambda j: pl.BlockSpec((1, tr, cols), lambda i, j=j: (j, i, 0))
    return pl.pallas_call(
        body, name=name, grid=(rows // tr,),
        in_specs=[flat, flat, flat, part(0), part(0), part(1), part(2)],
        out_specs=[flat] * 4,
        out_shape=[jax.ShapeDtypeStruct((rows, cols), F32)] * 4,
        compiler_params=_cparams(dimension_semantics=("arbitrary",)),
    )(w, m, v, own, recv, recv, recv)


def _ada_backward(act_t, dmod16, dmodc8, ada_w_sh, m, v):
    def body(at_ref, dm_ref, dc_ref, w_ref, m_ref, v_ref, g_ref, d_ref, m_out, v_out, pc_ref):
        g = jnp.dot(at_ref[...], dm_ref[...], preferred_element_type=F32, precision=lax.Precision.HIGHEST)
        w = w_ref[...]
        delta, m2, v2 = _adam_update(w, g, m_ref[...], v_ref[...])
        g_ref[...] = g
        d_ref[...] = delta
        m_out[...] = m2
        v_out[...] = v2
        pc_ref[...] = lax.dot_general(dc_ref[...], w, (((1,), (1,)), ((), ())), preferred_element_type=F32,
                                      precision=lax.Precision.HIGHEST)

    return pl.pallas_call(
        body, name="ada_backward",
        out_shape=[jax.ShapeDtypeStruct((D, ADA_SH), F32)] * 4 + [jax.ShapeDtypeStruct((8, D), F32)],
        compiler_params=_cparams(),
    )(act_t, dmod16, dmodc8, ada_w_sh, m, v)


def _sum_devices(gathered, name):
    _, rows, cols = gathered.shape

    def body(g_ref, o_ref):
        acc = g_ref[0]
        for d in range(1, N_DEV):
            acc = acc + g_ref[d]
        o_ref[...] = acc

    return pl.pallas_call(body, name=name, out_shape=jax.ShapeDtypeStruct((rows, cols), F32),
                          compiler_params=_cparams())(gathered)


def _adam_small(w, g_raw, g_raw2, m, v, silu_row, sigm_row):
    rows, cols = w.shape

    def body(w_ref, g_ref, g2_ref, m_ref, v_ref, g_out, d_ref, m_out, v_out):
        wv = w_ref[...]
        row = lax.broadcasted_iota(jnp.int32, (rows, cols), 0)
        s = _sigmoid(wv)
        dsilu = s * (1.0 + wv * (1.0 - s))
        scale = jnp.where(row == silu_row, dsilu, jnp.where(row == sigm_row, 1.0 - s, 1.0))
        g = (g_ref[...] + g2_ref[...]) * scale
        delta, m2, v2 = _adam_update(wv, g, m_ref[...], v_ref[...])
        g_out[...] = g
        d_ref[...] = delta
        m_out[...] = m2
        v_out[...] = v2

    return pl.pallas_call(body, name="adam_small", out_shape=[jax.ShapeDtypeStruct((rows, cols), F32)] * 4,
                          compiler_params=_cparams())(w, g_raw, g_raw2, m, v)


def _rope(t, cos, sin):
    lane = lax.broadcasted_iota(jnp.int32, (1, LANES), 1)
    first_half = jnp.bitwise_and(lane, DK // 2) == 0
    partner = jnp.where(first_half, pltpu.roll(t, LANES - DK // 2, 1), pltpu.roll(t, DK // 2, 1))
    return t * cos + partner * sin


def _in_projection(x, ctx, modx, modc, norm_w, w_in, cos_t, sin_t):
    n_x_tiles = x.shape[0] // TM
    lext = x.shape[0] + ctx.shape[0]
    q_lo, k_lo = 4 * D, 4 * D + N_HEADS * DK

    def body(x_ref, ctx_ref, mx_ref, mc_ref, nw_ref, w_ref, cos_ref, sin_ref, p_ref, xmt_ref):
        is_ctx = pl.program_id(0) >= n_x_tiles
        x = jnp.where(is_ctx, ctx_ref[...], x_ref[...])
        r = lax.rsqrt(jnp.mean(x * x, axis=-1, keepdims=True) + EPS)
        sh = jnp.where(is_ctx, mc_ref[0:1, :], mx_ref[0:1, :])
        sc = jnp.where(is_ctx, mc_ref[1:2, :], mx_ref[1:2, :])
        xm = (x * r * nw_ref[...]) * (1.0 + sc) + sh
        xmb = xm.astype(BF16)
        xmt_ref[...] = xm.T.astype(BF16)
        cos, sin = cos_ref[...], sin_ref[...]
        for d in range(N_DEV):
            acc = _dot(xmb, w_ref[d])
            lo = d * WSH
            col = 0
            while col < WSH:
                g = lo + col
                if q_lo <= g < k_lo + N_HEADS * DK:
                    piece = acc[:, col:col + LANES]
                    if g >= k_lo:
                        piece = piece * (DK ** -0.5)
                    p_ref[:, g:g + LANES] = _rope(piece, cos, sin).astype(BF16)
                    col += LANES
                else:
                    if g < q_lo:
                        end = min(WSH, q_lo - lo)
                    else:
                        end = WSH
                    p_ref[:, g:lo + end] = acc[:, col:end].astype(BF16)
                    col = end

    row = lambda w: pl.BlockSpec((TM, w), lambda i: (i, 0))
    full = lambda a: pl.BlockSpec(a.shape, lambda i: (0,) * a.ndim)
    return pl.pallas_call(
        body, name="in_projection", grid=(lext // TM,),
        in_specs=[pl.BlockSpec((TM, D), lambda i: (jnp.minimum(i, n_x_tiles - 1), 0)), full(ctx),
                  full(modx), full(modc), full(norm_w), full(w_in), row(LANES), row(LANES)],
        out_specs=[row(PW), pl.BlockSpec((D, TM), lambda i: (0, i))],
        out_shape=[jax.ShapeDtypeStruct((lext, PW), BF16), jax.ShapeDtypeStruct((D, lext), BF16)],
        compiler_params=_cparams(dimension_semantics=("arbitrary",)),
    )(x, ctx, modx, modc, norm_w, w_in, cos_t, sin_t)


def _decay_tables(lgf, lgb, n):
    i = lax.broadcasted_iota(jnp.int32, (n, 1), 0).astype(F32)
    return dict(i=i, k_f=jnp.exp(lgf * (n - 1.0 - i)), k_b=jnp.exp(lgb * i),
                q_f=jnp.exp(lgf * (i + 1.0)), q_b=jnp.exp(lgb * (n - i)))


def _decay_matrix(lgf, lgb, n, transposed=False):
    ii = lax.broadcasted_iota(jnp.int32, (n, n), 0)
    jj = lax.broadcasted_iota(jnp.int32, (n, n), 1)
    diff = (jj - ii if transposed else ii - jj).astype(F32)
    low = jnp.exp(lgf * jnp.maximum(diff, 0.0))
    up = jnp.exp(lgb * jnp.maximum(-diff, 0.0))
    return jnp.where(diff > 0, low, jnp.where(diff < 0, up, 2.0)), diff


def _cat_lanes(a, b):
    return jnp.concatenate([a.astype(BF16), b.astype(BF16)], axis=1)


def _retention_forward(p_ext, lg, seq, ctx_len):
    lext = seq + ctx_len
    n_chunks = seq // RET_C
    C = RET_C

    def body(lg_ref, q_ref, k_ref, v_ref, rn_ref, rstd_ref, kv_scr, sf_scr, sb_scr):
        pair = pl.program_id(0)
        lane = lax.broadcasted_iota(jnp.int32, (1, LANES), 1)
        heads = range(2)
        hmask = [(lane // DK == hh).astype(F32) for hh in heads]
        lgf = [lg_ref[0, 2 * pair + hh] for hh in heads]
        lgb = [lg_ref[1, 2 * pair + hh] for hh in heads]
        vs = [slice(hh * DV, (hh + 1) * DV) for hh in heads]
        t = [_decay_tables(lgf[hh], lgb[hh], C) for hh in heads]
        kc_all = k_ref[seq:lext, :].astype(F32)
        s0 = []
        for hh in heads:
            tc = _decay_tables(lgf[hh], lgb[hh], ctx_len)
            kc = kc_all * hmask[hh]
            s0.append(_dot_tn(_cat_lanes(kc * tc["k_f"], kc * tc["k_b"]), v_ref[seq:lext, vs[hh]]))

        def increments(c, carry):
            rows = pl.ds(pl.multiple_of(c * C, C), C)
            k_all = k_ref[rows, :].astype(F32)
            for hh in heads:
                k = k_all * hmask[hh]
                kv_scr[hh, c] = _dot_tn(_cat_lanes(k * t[hh]["k_f"], k * t[hh]["k_b"]), v_ref[rows, vs[hh]])
            return carry

        lax.fori_loop(0, n_chunks, increments, 0, unroll=2)
        gf_c = [jnp.exp(lgf[hh] * C) for hh in heads]
        gb_c = [jnp.exp(lgb[hh] * C) for hh in heads]

        def scan_f(c, s):
            for hh in heads:
                sf_scr[hh, c] = s[hh]
            return tuple(gf_c[hh] * s[hh] + kv_scr[hh, c, 0:LANES, :] for hh in heads)

        def scan_b(n, s):
            c = n_chunks - 1 - n
            for hh in heads:
                sb_scr[hh, c] = s[hh]
            return tuple(gb_c[hh] * s[hh] + kv_scr[hh, c, LANES:2 * LANES, :] for hh in heads)

        lax.fori_loop(0, n_chunks, scan_f, tuple(s0[hh][0:LANES] for hh in heads))
        lax.fori_loop(0, n_chunks, scan_b, tuple(s0[hh][LANES:2 * LANES] for hh in heads))
        dmat = [_decay_matrix(lgf[hh], lgb[hh], C)[0] for hh in heads]

        def outputs(c, carry):
            rows = pl.ds(pl.multiple_of(c * C, C), C)
            q_all = q_ref[rows, :].astype(F32)
            k = k_ref[rows, :]
            for hh in heads:
                q = q_all * hmask[hh]
                v = v_ref[rows, vs[hh]]
                s = _dot_nt(q.astype(BF16), k)
                inner = _dot((s * dmat[hh]).astype(BF16), v)
                states = jnp.concatenate([sf_scr[hh, c], sb_scr[hh, c]], axis=0).astype(BF16)
                ret = inner + _dot(_cat_lanes(q * t[hh]["q_f"], q * t[hh]["q_b"]), states)
                mu = jnp.mean(ret, axis=-1, keepdims=True)
                cen = ret - mu
                rstd = lax.rsqrt(jnp.mean(cen * cen, axis=-1, keepdims=True) + EPS)
                rn_ref[rows, vs[hh]] = cen * rstd
                rstd_ref[rows, vs[hh]] = jnp.broadcast_to(rstd, (C, DV))
            return carry

        lax.fori_loop(0, n_chunks, outputs, 0, unroll=2)

    qk0 = CB_QK * D // LANES
    return pl.pallas_call(
        body, name="retention_forward", grid=(N_HEADS // 2,),
        in_specs=[pl.BlockSpec(memory_space=pltpu.SMEM),
                  pl.BlockSpec((lext, LANES), lambda g: (0, qk0 + g)),
                  pl.BlockSpec((lext, LANES), lambda g: (0, qk0 + N_HEADS // 2 + g)),
                  pl.BlockSpec((lext, 2 * DV), lambda g: (0, CB_V * D // (2 * DV) + g))],
        out_specs=[pl.BlockSpec((seq, 2 * DV), lambda g: (0, g))] * 2,
        out_shape=[jax.ShapeDtypeStruct((seq, D), F32)] * 2,
        scratch_shapes=[pltpu.VMEM((2, n_chunks, 2 * LANES, DV), F32), pltpu.VMEM((2, n_chunks, LANES, DV), F32),
                        pltpu.VMEM((2, n_chunks, LANES, DV), F32)],
        compiler_params=_cparams(dimension_semantics=("arbitrary",)),
    )(lg, p_ext, p_ext, p_ext)


def _retention_backward(p_ext, dret, lg, cos_t, sin_t, seq, ctx_len):
    lext = seq + ctx_len
    n_chunks = seq // RET_C
    C = RET_C

    def body(lg_ref, q_ref, k_ref, v_ref, do_ref, cos_ref, sin_ref, dq_ref, dk_ref, dv_ref, dlg_ref,
             kv_scr, g_scr, sf_scr, sb_scr, gfn_scr, gbp_scr):
        pair = pl.program_id(0)
        lane = lax.broadcasted_iota(jnp.int32, (1, LANES), 1)
        heads = range(2)
        hmask = [(lane // DK == hh).astype(F32) for hh in heads]
        lgf = [lg_ref[0, 2 * pair + hh] for hh in heads]
        lgb = [lg_ref[1, 2 * pair + hh] for hh in heads]
        vs = [slice(hh * DV, (hh + 1) * DV) for hh in heads]
        t = [_decay_tables(lgf[hh], lgb[hh], C) for hh in heads]
        tc = [_decay_tables(lgf[hh], lgb[hh], ctx_len) for hh in heads]
        kc_all = k_ref[seq:lext, :].astype(F32)
        kc = [kc_all * hmask[hh] for hh in heads]
        vc = [v_ref[seq:lext, vs[hh]] for hh in heads]
        kc_cat = [_cat_lanes(kc[hh] * tc[hh]["k_f"], kc[hh] * tc[hh]["k_b"]) for hh in heads]
        s0 = [_dot_tn(kc_cat[hh], vc[hh]) for hh in heads]

        def increments(c, carry):
            rows = pl.ds(pl.multiple_of(c * C, C), C)
            k_all = k_ref[rows, :].astype(F32)
            q_all = q_ref[rows, :].astype(F32)
            for hh in heads:
                k, q = k_all * hmask[hh], q_all * hmask[hh]
                kv_scr[hh, c] = _dot_tn(_cat_lanes(k * t[hh]["k_f"], k * t[hh]["k_b"]), v_ref[rows, vs[hh]])
                g_scr[hh, c] = _dot_tn(_cat_lanes(q * t[hh]["q_f"], q * t[hh]["q_b"]), do_ref[rows, vs[hh]])
            return carry

        lax.fori_loop(0, n_chunks, increments, 0, unroll=2)
        gf_c = [jnp.exp(lgf[hh] * C) for hh in heads]
        gb_c = [jnp.exp(lgb[hh] * C) for hh in heads]

        def scan_f(c, s):
            for hh in heads:
                sf_scr[hh, c] = s[hh]
            return tuple(gf_c[hh] * s[hh] + kv_scr[hh, c, 0:LANES, :] for hh in heads)

        def scan_b(n, s):
            c = n_chunks - 1 - n
            for hh in heads:
                sb_scr[hh, c] = s[hh]
            return tuple(gb_c[hh] * s[hh] + kv_scr[hh, c, LANES:2 * LANES, :] for hh in heads)

        def scan_gf(n, carry):
            c = n_chunks - 1 - n
            for hh in heads:
                gfn_scr[hh, c] = carry[hh]
            return tuple(g_scr[hh, c, 0:LANES, :] + gf_c[hh] * carry[hh] for hh in heads)

        def scan_gb(c, carry):
            for hh in heads:
                gbp_scr[hh, c] = carry[hh]
            return tuple(g_scr[hh, c, LANES:2 * LANES, :] + gb_c[hh] * carry[hh] for hh in heads)

        lax.fori_loop(0, n_chunks, scan_f, tuple(s0[hh][0:LANES] for hh in heads))
        lax.fori_loop(0, n_chunks, scan_b, tuple(s0[hh][LANES:2 * LANES] for hh in heads))
        zero_state = jnp.zeros((LANES, DV), F32)
        gf_first = lax.fori_loop(0, n_chunks, scan_gf, (zero_state, zero_state))
        gb_last = lax.fori_loop(0, n_chunks, scan_gb, (zero_state, zero_state))

        dmat, dmat_t, w_f, w_b = [], [], [], []
        for hh in heads:
            dm, diff = _decay_matrix(lgf[hh], lgb[hh], C)
            dmat.append(dm)
            dmat_t.append(_decay_matrix(lgf[hh], lgb[hh], C, transposed=True)[0])
            w_f.append(jnp.where(diff > 0, diff * dm, 0.0))
            w_b.append(jnp.where(diff < 0, -diff * dm, 0.0))

        def total(a):
            rows_, width = a.shape
            part = jnp.sum(a.reshape(rows_ // 8, 8, width), axis=0)
            return part[:, 0:LANES] + part[:, LANES:2 * LANES] if width == 2 * LANES else part

        def chunk_head(hh, c, rows, q_all, k_all, dlgf, dlgb):
            th = t[hh]
            qm = q_all * hmask[hh]
            km = k_all * hmask[hh]
            qb, kb = qm.astype(BF16), km.astype(BF16)
            v = v_ref[rows, vs[hh]]
            do = do_ref[rows, vs[hh]]
            s = _dot_nt(qb, kb)
            s_t = _dot_nt(kb, qb)
            dsd = _dot_nt(do, v)
            dsd_t = _dot_nt(v, do)
            dq_in = _dot((dsd * dmat[hh]).astype(BF16), kb)
            dk_in = _dot((dsd_t * dmat_t[hh]).astype(BF16), qb)
            dv_in = _dot((s_t * dmat_t[hh]).astype(BF16), do)
            prod = s * dsd
            dlgf = dlgf + total(prod * w_f[hh])
            dlgb = dlgb + total(prod * w_b[hh])
            sf, sb = sf_scr[hh, c], sb_scr[hh, c]
            states = jnp.concatenate([sf, sb], axis=0).astype(BF16)
            dqc = _dot_nt(do, states)
            dqf = dqc[:, 0:LANES] * th["q_f"]
            dqb = dqc[:, LANES:2 * LANES] * th["q_b"]
            dq = (dq_in + dqf + dqb) * hmask[hh]
            dlgf = dlgf + total((th["i"] + 1.0) * (qm * dqf))
            dlgb = dlgb + total((C - th["i"]) * (qm * dqb))
            gfn, gbp = gfn_scr[hh, c], gbp_scr[hh, c]
            gstates = jnp.concatenate([gfn, gbp], axis=0).astype(BF16)
            dkc = _dot_nt(v, gstates)
            dkf = dkc[:, 0:LANES] * th["k_f"]
            dkb = dkc[:, LANES:2 * LANES] * th["k_b"]
            dk = dk_in + dkf + dkb
            dlgf = dlgf + total((C - 1.0 - th["i"]) * (km * dkf)) + C * gf_c[hh] * total(gfn * sf)
            dlgb = dlgb + total(th["i"] * (km * dkb)) + C * gb_c[hh] * total(gbp * sb)
            dv = dv_in + _dot(_cat_lanes(km * th["k_f"], km * th["k_b"]), gstates)
            dv_ref[rows, vs[hh]] = dv.astype(BF16)
            return dq, dk, dlgf, dlgb

        def chunk(c, carry):
            rows = pl.ds(pl.multiple_of(c * C, C), C)
            q_all = q_ref[rows, :].astype(F32)
            k_all = k_ref[rows, :].astype(F32)
            dq0, dk0, f0, b0 = chunk_head(0, c, rows, q_all, k_all, carry[0], carry[1])
            dq1, dk1, f1, b1 = chunk_head(1, c, rows, q_all, k_all, carry[2], carry[3])
            cos, sin = cos_ref[rows, :], sin_ref[rows, :]
            dq_ref[rows, :] = _rope(dq0 + dq1, cos, -sin).astype(BF16)
            dk_ref[rows, :] = (_rope(dk0 + dk1, cos, -sin) * (DK ** -0.5)).astype(BF16)
            return f0, b0, f1, b1

        zero = jnp.zeros((8, LANES), F32)
        sums = lax.fori_loop(0, n_chunks, chunk, (zero, zero, zero, zero), unroll=2)

        dlg = []
        dk_ctx = jnp.zeros((ctx_len, LANES), F32)
        for hh in heads:
            g0 = jnp.concatenate([gf_first[hh], gb_last[hh]], axis=0).astype(BF16)
            dkcc = _dot_nt(vc[hh], g0)
            dkcf = dkcc[:, 0:LANES] * tc[hh]["k_f"]
            dkcb = dkcc[:, LANES:2 * LANES] * tc[hh]["k_b"]
            dlgf = sums[2 * hh] + total((ctx_len - 1.0 - tc[hh]["i"]) * (kc[hh] * dkcf))
            dlgb = sums[2 * hh + 1] + total(tc[hh]["i"] * (kc[hh] * dkcb))
            dk_ctx = dk_ctx + (dkcf + dkcb) * (DK ** -0.5)
            dv_ref[seq:lext, vs[hh]] = _dot(kc_cat[hh], g0).astype(BF16)
            dlg += [jnp.sum(jnp.sum(a, axis=1, keepdims=True), axis=0, keepdims=True) for a in (dlgf, dlgb)]
        dk_ref[seq:lext, :] = dk_ctx.astype(BF16)
        dq_ref[seq:lext, :] = jnp.zeros((ctx_len, LANES), BF16)

        lane8 = lax.broadcasted_iota(jnp.int32, (8, LANES), 1)
        out = jnp.zeros((8, LANES), F32)
        for n, val in enumerate(dlg):
            out = jnp.where(lane8 == n, val, out)
        dlg_ref[0] = out

    qk0 = CB_QK * D // LANES
    q_spec = pl.BlockSpec((lext, LANES), lambda g: (0, qk0 + g))
    k_spec = pl.BlockSpec((lext, LANES), lambda g: (0, qk0 + N_HEADS // 2 + g))
    v_spec = pl.BlockSpec((lext, 2 * DV), lambda g: (0, CB_V * D // (2 * DV) + g))
    table = pl.BlockSpec((lext, LANES), lambda g: (0, 0))
    state = pltpu.VMEM((2, n_chunks, LANES, DV), F32)
    return pl.pallas_call(
        body, name="retention_backward", grid=(N_HEADS // 2,),
        in_specs=[pl.BlockSpec(memory_space=pltpu.SMEM), q_spec, k_spec, v_spec,
                  pl.BlockSpec((seq, 2 * DV), lambda g: (0, g)), table, table],
        out_specs=[pl.BlockSpec((lext, LANES), lambda g: (0, g)), pl.BlockSpec((lext, LANES), lambda g: (0, g)),
                   pl.BlockSpec((lext, 2 * DV), lambda g: (0, g)), pl.BlockSpec((1, 8, LANES), lambda g: (g, 0, 0))],
        out_shape=[jax.ShapeDtypeStruct((lext, N_HEADS * DK), BF16), jax.ShapeDtypeStruct((lext, N_HEADS * DK), BF16),
                   jax.ShapeDtypeStruct((lext, D), BF16), jax.ShapeDtypeStruct((N_HEADS // 2, 8, LANES), F32)],
        scratch_shapes=[pltpu.VMEM((2, n_chunks, 2 * LANES, DV), F32), pltpu.VMEM((2, n_chunks, 2 * LANES, DV), F32),
                        state, state, state, state],
        compiler_params=_cparams(dimension_semantics=("arbitrary",)),
    )(lg, p_ext, p_ext, p_ext, dret, cos_t, sin_t)


def _merge(p_ext, rn, rstd, x, target, w_a, w_b, w_out, vecs, seq):
    n_tiles = seq // TM
    hb = TM // HALO

    def body(h_ref, bg_ref, cg_ref, za_ref, zb_ref, ga_ref, gb_ref, hp_ref, hn_ref, cp_ref, cn_ref,
             rn_ref, rstd_ref, x_ref, t_ref, wa_ref, wb_ref, wo_ref, vec_ref,
             dx1_ref, dpa_ref, dconv_ref, dret_ref, at_ref, b_ref, part_ref):
        i = pl.program_id(0)
        f = lambda ref: ref[...].astype(F32)
        h, bg, cg, za, zb, ga, gb = f(h_ref), f(bg_ref), f(cg_ref), f(za_ref), f(zb_ref), f(ga_ref), f(gb_ref)
        gx, w0, w1, w2 = vec_ref[0:1, :], vec_ref[1:2, :], vec_ref[2:3, :], vec_ref[3:4, :]
        cb, gnw, fw = vec_ref[4:5, :], vec_ref[5:6, :], vec_ref[6:7, :]
        u = cg * h
        row = lax.broadcasted_iota(jnp.int32, (TM, 1), 0)
        u_prev = (f(cp_ref) * f(hp_ref))[HALO - 1:HALO, :]
        u_next = (f(cn_ref) * f(hn_ref))[0:1, :]
        u_prev = jnp.where(i == 0, 0.0, u_prev)
        u_next = jnp.where(i == n_tiles - 1, 0.0, u_next)
        u_up = jnp.where(row == 0, u_prev, pltpu.roll(u, 1, 0))
        u_dn = jnp.where(row == TM - 1, u_next, pltpu.roll(u, TM - 1, 0))
        conv = w0 * u_up + w1 * u + w2 * u_dn + cb
        sza = _sigmoid(za)
        silu_za = za * sza
        a_act = silu_za * bg * conv
        rn = rn_ref[...]
        szb = _sigmoid(zb)
        silu_zb = zb * szb
        rg = rn * gnw
        b_act = silu_zb * rg
        y_a = _dot(a_act.astype(BF16), wa_ref[...])
        y_b = _dot(b_act.astype(BF16), wb_ref[...])
        sga, sgb = _sigmoid(ga), _sigmoid(gb)
        mix = sga * y_a + sgb * y_b
        y = _dot(mix.astype(BF16), wo_ref[...])
        x1 = x_ref[...] + gx * y
        r1 = lax.rsqrt(jnp.mean(x1 * x1, axis=-1, keepdims=True) + EPS)
        xh1 = x1 * r1
        err = xh1 * fw - t_ref[...]
        loss = jnp.sum(jnp.sum(err * err, axis=1, keepdims=True), axis=0, keepdims=True) * (0.5 / D)
        dout = err * (1.0 / D)
        dxh = dout * fw
        dx1 = r1 * (dxh - xh1 * jnp.mean(dxh * xh1, axis=-1, keepdims=True))
        dx1_ref[...] = dx1
        dy = (dx1 * gx).astype(BF16)
        dmix = _dot_nt(dy, wo_ref[...])
        dya = (dmix * sga).astype(BF16)
        dyb = (dmix * sgb).astype(BF16)
        da = _dot_nt(dya, wa_ref[...])
        db = _dot_nt(dyb, wb_ref[...])
        dpa_ref[:, 0:D] = (da * silu_za * conv).astype(BF16)
        dpa_ref[:, D:2 * D] = (da * bg * conv * (sza * (1.0 + za * (1.0 - sza)))).astype(BF16)
        dpa_ref[:, 2 * D:3 * D] = (db * rg * (szb * (1.0 + zb * (1.0 - szb)))).astype(BF16)
        dpa_ref[:, 3 * D:4 * D] = (dmix * y_a * sga * (1.0 - sga)).astype(BF16)
        dpa_ref[:, 4 * D:5 * D] = (dmix * y_b * sgb * (1.0 - sgb)).astype(BF16)
        dconv_ref[...] = (da * silu_za * bg).astype(BF16)
        drn_n = db * silu_zb
        drn = drn_n * gnw
        rstd = rstd_ref[...]
        for hd in range(N_HEADS):
            sl = slice(hd * DV, (hd + 1) * DV)
            dh_, rh = drn[:, sl], rn[:, sl]
            m1 = jnp.mean(dh_, axis=-1, keepdims=True)
            m2 = jnp.mean(dh_ * rh, axis=-1, keepdims=True)
            dret_ref[:, sl] = (rstd[:, sl] * (dh_ - m1 - rh * m2)).astype(BF16)
        at_ref[0] = a_act.T.astype(BF16)
        at_ref[1] = b_act.T.astype(BF16)
        at_ref[2] = mix.T.astype(BF16)
        b_ref[0] = dya
        b_ref[1] = dyb
        b_ref[2] = dy

        @pl.when(i == 0)
        def _():
            part_ref[...] = jnp.zeros((8, D), F32)

        part_ref[0:1, :] += jnp.sum(dout * xh1, axis=0, keepdims=True)
        part_ref[1:2, :] += jnp.sum(dx1 * y, axis=0, keepdims=True)
        part_ref[2:3, :] += jnp.sum(drn_n * rn, axis=0, keepdims=True)
        part_ref[3:4, :] += jnp.broadcast_to(loss, (1, D))

    col = lambda cb_: pl.BlockSpec((TM, D), lambda i, cb_=cb_: (i, cb_))
    prev = lambda cb_: pl.BlockSpec((HALO, D), lambda i, cb_=cb_: (jnp.maximum(i * hb - 1, 0), cb_))
    nxt = lambda cb_: pl.BlockSpec((HALO, D), lambda i, cb_=cb_: (jnp.minimum((i + 1) * hb, n_tiles * hb - 1), cb_))
    tile = pl.BlockSpec((TM, D), lambda i: (i, 0))
    full = lambda a: pl.BlockSpec(a.shape, lambda i: (0,) * a.ndim)
    return pl.pallas_call(
        body, name="merge", grid=(n_tiles,),
        in_specs=[col(CB_H), col(CB_BG), col(CB_CG), col(CB_ZA), col(CB_ZB), col(CB_GA), col(CB_GB),
                  prev(CB_H), nxt(CB_H), prev(CB_CG), nxt(CB_CG),
                  tile, tile, tile, tile, full(w_a), full(w_b), full(w_out), full(vecs)],
        out_specs=[tile, pl.BlockSpec((TM, 5 * D), lambda i: (i, 0)), tile, tile,
                   pl.BlockSpec((3, D, TM), lambda i: (0, 0, i)), pl.BlockSpec((3, TM, D), lambda i: (0, i, 0)),
                   pl.BlockSpec((8, D), lambda i: (0, 0))],
        out_shape=[jax.ShapeDtypeStruct((seq, D), F32), jax.ShapeDtypeStruct((seq, 5 * D), BF16),
                   jax.ShapeDtypeStruct((seq, D), BF16), jax.ShapeDtypeStruct((seq, D), BF16),
                   jax.ShapeDtypeStruct((3, D, seq), BF16), jax.ShapeDtypeStruct((3, seq, D), BF16),
                   jax.ShapeDtypeStruct((8, D), F32)],
        compiler_params=_cparams(dimension_semantics=("arbitrary",)),
    )(p_ext, p_ext, p_ext, p_ext, p_ext, p_ext, p_ext, p_ext, p_ext, p_ext, p_ext,
      rn, rstd, x, target, w_a, w_b, w_out, vecs)


def _conv_backward(p_ext, dconv, dpa, dq, dk, dv, vecs, seq):
    n_tiles = seq // TM
    hb = TM // HALO
    qk = N_HEADS * DK

    def body(h_ref, cg_ref, dc_ref, dcp_ref, dcn_ref, dpa_ref, dq_ref, dk_ref, dv_ref, vec_ref, dp_ref, part_ref):
        i = pl.program_id(0)
        dp_ref[:, CB_QK * D:CB_QK * D + qk] = dq_ref[...]
        dp_ref[:, CB_QK * D + qk:CB_V * D] = dk_ref[...]
        dp_ref[:, CB_V * D:CB_ZB * D] = dv_ref[...]

        @pl.when(i == 0)
        def _():
            part_ref[...] = jnp.zeros((8, D), F32)

        @pl.when(i == n_tiles)
        def _():
            dp_ref[:, 0:CB_QK * D] = jnp.zeros((TM, CB_QK * D), BF16)
            dp_ref[:, CB_ZB * D:PW] = jnp.zeros((TM, PW - CB_ZB * D), BF16)

        @pl.when(i < n_tiles)
        def _():
            f = lambda ref: ref[...].astype(F32)
            h, cg, dc = f(h_ref), f(cg_ref), f(dc_ref)
            w0, w1, w2 = vec_ref[1:2, :], vec_ref[2:3, :], vec_ref[3:4, :]
            row = lax.broadcasted_iota(jnp.int32, (TM, 1), 0)
            dc_prev = jnp.where(i == 0, 0.0, f(dcp_ref)[HALO - 1:HALO, :])
            dc_next = jnp.where(i == n_tiles - 1, 0.0, f(dcn_ref)[0:1, :])
            dc_up = jnp.where(row == 0, dc_prev, pltpu.roll(dc, 1, 0))
            dc_dn = jnp.where(row == TM - 1, dc_next, pltpu.roll(dc, TM - 1, 0))
            du = w0 * dc_dn + w1 * dc + w2 * dc_up
            u = cg * h
            dp_ref[:, CB_H * D:(CB_H + 1) * D] = (du * cg).astype(BF16)
            dp_ref[:, CB_CG * D:(CB_CG + 1) * D] = (du * h).astype(BF16)
            dp_ref[:, CB_BG * D:(CB_BG + 1) * D] = dpa_ref[:, 0:D]
            dp_ref[:, CB_ZA * D:(CB_ZA + 1) * D] = dpa_ref[:, D:2 * D]
            dp_ref[:, CB_ZB * D:(CB_GB + 1) * D] = dpa_ref[:, 2 * D:5 * D]
            part_ref[0:1, :] += jnp.sum(u * dc_dn, axis=0, keepdims=True)
            part_ref[1:2, :] += jnp.sum(u * dc, axis=0, keepdims=True)
            part_ref[2:3, :] += jnp.sum(u * dc_up, axis=0, keepdims=True)
            part_ref[3:4, :] += jnp.sum(dc, axis=0, keepdims=True)

    last = n_tiles - 1
    col = lambda cb_: pl.BlockSpec((TM, D), lambda i, cb_=cb_: (jnp.minimum(i, last), cb_))
    lat = lambda w: pl.BlockSpec((TM, w), lambda i: (jnp.minimum(i, last), 0))
    ext = lambda w: pl.BlockSpec((TM, w), lambda i: (i, 0))
    return pl.pallas_call(
        body, name="conv_backward", grid=(n_tiles + 1,),
        in_specs=[col(CB_H), col(CB_CG), lat(D),
                  pl.BlockSpec((HALO, D), lambda i: (jnp.clip(i * hb - 1, 0, n_tiles * hb - 1), 0)),
                  pl.BlockSpec((HALO, D), lambda i: (jnp.minimum((i + 1) * hb, n_tiles * hb - 1), 0)),
                  lat(5 * D), ext(qk), ext(qk), ext(D),
                  pl.BlockSpec(vecs.shape, lambda i: (0, 0))],
        out_specs=[ext(PW), pl.BlockSpec((8, D), lambda i: (0, 0))],
        out_shape=[jax.ShapeDtypeStruct((seq + TM, PW), BF16), jax.ShapeDtypeStruct((8, D), F32)],
        compiler_params=_cparams(dimension_semantics=("arbitrary",)),
    )(p_ext, p_ext, dconv, dconv, dconv, dpa, dq, dk, dv, vecs)


def _input_backward(dp, w_in, x, ctx, dx1, modx, modc, norm_w):
    seq = x.shape[0]
    lext = seq + ctx.shape[0]
    n_x = seq // TM

    def body(dp_ref, w_ref, x_ref, ctx_ref, dx1_ref, mx_ref, mc_ref, nw_ref, gx_ref, part_ref):
        i = pl.program_id(0)
        is_ctx = i >= n_x
        dxm = _dot_nt(dp_ref[:, 0:WSH], w_ref[0])
        for d in range(1, N_DEV):
            dxm = dxm + _dot_nt(dp_ref[:, d * WSH:(d + 1) * WSH], w_ref[d])
        x = jnp.where(is_ctx, ctx_ref[...], x_ref[...])
        r = lax.rsqrt(jnp.mean(x * x, axis=-1, keepdims=True) + EPS)
        xh = x * r
        nw = nw_ref[...]
        sc = jnp.where(is_ctx, mc_ref[1:2, :], mx_ref[1:2, :])
        dxn = dxm * (1.0 + sc)
        dxh = dxn * nw
        dx = r * (dxh - xh * jnp.mean(dxh * xh, axis=-1, keepdims=True))

        @pl.when(jnp.logical_not(is_ctx))
        def _():
            gx_ref[...] = dx1_ref[...] + dx

        @pl.when(i == 0)
        def _():
            part_ref[...] = jnp.zeros((8, D), F32)

        fx = jnp.where(is_ctx, 0.0, 1.0)
        d_shift = jnp.sum(dxm, axis=0, keepdims=True)
        d_scale = jnp.sum(dxm * (xh * nw), axis=0, keepdims=True)
        part_ref[0:1, :] += fx * d_shift
        part_ref[1:2, :] += fx * d_scale
        part_ref[2:3, :] += jnp.sum(dxn * xh, axis=0, keepdims=True)
        part_ref[3:4, :] += (1.0 - fx) * d_shift
        part_ref[4:5, :] += (1.0 - fx) * d_scale

    lat = lambda w: pl.BlockSpec((TM, w), lambda i: (jnp.minimum(i, n_x - 1), 0))
    ext = lambda w: pl.BlockSpec((TM, w), lambda i: (i, 0))
    full = lambda a: pl.BlockSpec(a.shape, lambda i: (0,) * a.ndim)
    return pl.pallas_call(
        body, name="input_backward", grid=(lext // TM,),
        in_specs=[ext(PW), full(w_in), lat(D), full(ctx), lat(D), full(modx), full(modc), full(norm_w)],
        out_specs=[lat(D), pl.BlockSpec((8, D), lambda i: (0, 0))],
        out_shape=[jax.ShapeDtypeStruct((seq, D), F32), jax.ShapeDtypeStruct((8, D), F32)],
        compiler_params=_cparams(dimension_semantics=("arbitrary",)),
    )(dp, w_in, x, ctx, dx1, modx, modc, norm_w)


def _weight_grad_in(xm_t, dp, owners, name):
    lext = xm_t.shape[1]
    tk = lext // 2
    n_k = lext // tk

    def body(own_ref, a_ref, b_ref, o_ref, acc):
        k = pl.program_id(1)

        @pl.when(k == 0)
        def _():
            acc[...] = jnp.zeros_like(acc)

        acc[...] += _dot(a_ref[...], b_ref[...])

        @pl.when(k == n_k - 1)
        def _():
            o_ref[0] = acc[...]

    grid_spec = pltpu.PrefetchScalarGridSpec(
        num_scalar_prefetch=1, grid=(4, n_k),
        in_specs=[pl.BlockSpec((D, tk), lambda j, k, own: (0, k)),
                  pl.BlockSpec((tk, WSH), lambda j, k, own: (k, own[j]))],
        out_specs=pl.BlockSpec((1, D, WSH), lambda j, k, own: (j, 0, 0)),
        scratch_shapes=[pltpu.VMEM((D, WSH), F32)])
    return pl.pallas_call(
        body, name=name, grid_spec=grid_spec,
        out_shape=jax.ShapeDtypeStruct((4, D, WSH), F32),
        compiler_params=_cparams(dimension_semantics=("arbitrary", "arbitrary")),
    )(owners, xm_t, dp)


def _weight_grad_square(a_t, b):
    seq = a_t.shape[2]
    tk = seq // 2

    def body(a_ref, b_ref, o_ref, acc):
        k = pl.program_id(1)

        @pl.when(k == 0)
        def _():
            acc[...] = jnp.zeros_like(acc)

        acc[...] += _dot(a_ref[0], b_ref[0])

        @pl.when(k == pl.num_programs(1) - 1)
        def _():
            o_ref[:, 0] = acc[...].reshape(N_DEV, RSH, D)

    return pl.pallas_call(
        body, name="weight_grad_square", grid=(3, seq // tk),
        in_specs=[pl.BlockSpec((1, D, tk), lambda t, k: (t, 0, k)), pl.BlockSpec((1, tk, D), lambda t, k: (t, k, 0))],
        out_specs=pl.BlockSpec((N_DEV, 1, RSH, D), lambda t, k: (0, t, 0, 0)),
        out_shape=jax.ShapeDtypeStruct((N_DEV, 3, RSH, D), F32),
        scratch_shapes=[pltpu.VMEM((D, D), F32)],
        compiler_params=_cparams(dimension_semantics=("arbitrary", "arbitrary")),
    )(a_t, b)


def _rope_tables(seq, ctx_len):
    rows = seq // GRID_W
    row = jnp.repeat(jnp.arange(rows), GRID_W).astype(F32)
    col = jnp.tile(jnp.arange(GRID_W), rows).astype(F32)
    nf = DK // 4
    inv = ROPE_BASE ** (-jnp.arange(nf, dtype=F32) / nf)
    ang = jnp.concatenate([row[:, None] * inv, col[:, None] * inv], axis=-1)
    cos, sin = jnp.cos(ang), jnp.sin(ang)
    cos_t = jnp.tile(cos, (1, 4))
    sin_t = jnp.tile(jnp.concatenate([-sin, sin], axis=-1), (1, 2))
    cos_t = jnp.concatenate([cos_t, jnp.ones((ctx_len, LANES), F32)], axis=0)
    sin_t = jnp.concatenate([sin_t, jnp.zeros((ctx_len, LANES), F32)], axis=0)
    return cos_t, sin_t


def _tie(value, token):
    return value + token[0, 0].astype(value.dtype)


def _pad_lanes(a):
    return jnp.pad(a, ((0, 0), (0, D - a.shape[1])))


def _rows(total, *parts):
    width = max(a.shape[1] for _, a in parts)
    out = None
    for row, a in parts:
        padded = jnp.pad(a, ((row, total - row - a.shape[0]), (0, width - a.shape[1])))
        out = padded if out is None else out + padded
    return out


def kernel(x, c, ctx, c_ctx, norm_w, ada_w, ada_b, w_in, conv_w, conv_b, decay_logit, gn_w, w_a, w_b, w_out, final_norm_w, loss_target, m_c_ctx, m_norm_w, m_ada_w, m_ada_b, m_w_in, m_conv_w, m_conv_b, m_decay_logit, m_gn_w, m_w_a, m_w_b, m_w_out, m_final_norm_w, v_c_ctx, v_norm_w, v_ada_w, v_ada_b, v_w_in, v_conv_w, v_conv_b, v_decay_logit, v_gn_w, v_w_a, v_w_b, v_w_out, v_final_norm_w):
    xi, yi, ci = _coords()
    me = 4 * xi + 2 * yi + ci
    chip = 2 * xi + yi
    seq, ctx_len = x.shape[1], ctx.shape[1]
    assert seq % TM == 0 and seq % RET_C == 0 and ctx_len == TM and seq % GRID_W == 0
    csh = D // N_DEV

    blk = jnp.pad(c, ((0, 7), (0, 0))) + jnp.pad(conv_w[0], ((1, 4), (0, D - csh)))
    got = _all_gather_small(blk, "gather_cond")
    conv_w_all = got[:, 1:4, 0:csh].transpose(1, 0, 2).reshape(3, D)
    c16 = _rows(16, (0, got[:, 0, :]), (N_DEV, c_ctx[None]))
    ada_b_sh = lax.dynamic_slice(ada_b, (0, me * ADA_SH), (1, ADA_SH))
    mod_sh, act16, lg = _modulation(c16, ada_w[0], ada_b_sh, decay_logit[0])
    mod_all = _all_gather_small(mod_sh, "gather_mod").transpose(1, 0, 2).reshape(16, 3 * D)
    modx = lax.dynamic_slice(mod_all, (me, 0), (1, 3 * D)).reshape(3, D)
    modc = mod_all[8].reshape(3, D)

    w_in_all, gathered = _all_gather_weights([w_in[0].astype(BF16)])
    sq_shards = [_tie(w[0].astype(BF16), gathered) for w in (w_a, w_b, w_out)]
    ag_sq = _Exchange("gather_square", sq_shards, [jax.ShapeDtypeStruct((N_DEV, RSH, D), BF16)] * 3,
                      3 * (N_DEV - 1), _gather_plan)

    x2, tgt = x[0], loss_target[0]
    cos_t, sin_t = _rope_tables(seq, ctx_len)
    modx = _tie(modx, ag_sq.token)
    p_ext, xm_t = _in_projection(x2, ctx[0], modx, modc, norm_w, w_in_all, cos_t, sin_t)
    rn, rstd = _retention_forward(p_ext, lg, seq, ctx_len)
    _, sq_lands = ag_sq.wait(rstd)
    w_a_all, w_b_all, w_out_all = (
        lax.dynamic_update_slice(land, shard[None], (me, 0, 0)).reshape(D, D) for land, shard in zip(sq_lands, sq_shards))
    vecs = _rows(8, (0, modx[2:3]), (1, conv_w_all), (4, conv_b), (5, gn_w), (6, final_norm_w[None]))
    dx1, dpa, dconv, dret, op_at, op_b, part_m = _merge(p_ext, rn, rstd, x2, tgt, w_a_all, w_b_all, w_out_all, vecs, seq)

    j4 = jnp.arange(4, dtype=jnp.int32)
    owners = (2 * jnp.bitwise_xor(chip, j4) + ci).astype(jnp.int32)
    owners_sib = (2 * jnp.bitwise_xor(chip, j4) + (1 - ci)).astype(jnp.int32)
    gw_sq = _weight_grad_square(op_at, op_b).reshape(N_DEV, 3 * RSH, D)
    rs_sq_pair = _Exchange("rs_square_pair", [gw_sq], [jax.ShapeDtypeStruct((4, 3 * RSH, D), F32)], 4,
                           _pair_plan(lambda j, chip_, c_: 2 * jnp.bitwise_xor(chip_, j) + (1 - c_)))
    dq, dk, dv, dlg = _retention_backward(p_ext, dret, _tie(lg, rs_sq_pair.token), cos_t, sin_t, seq, ctx_len)
    (gw_sq,), (r1_sq,) = rs_sq_pair.wait(dlg)
    own_sq, send_sq = _pair_sum(gw_sq, r1_sq, owners, "pair_sum_square")
    rs_sq_chips = _Exchange("rs_square_chips", [send_sq], [jax.ShapeDtypeStruct((3, 3 * RSH, D), BF16)], 3, _chips_plan)
    dp, part_c = _conv_backward(p_ext, dconv, dpa, dq, dk, dv, _tie(vecs, rs_sq_chips.token), seq)
    gw_sib = _weight_grad_in(xm_t, dp, owners_sib, "weight_grad_in_sibling")
    rs_in_pair = _Exchange("rs_in_pair", [gw_sib], [jax.ShapeDtypeStruct((4, D, WSH), F32)], 4,
                           _pair_plan(lambda j, chip_, c_: j))
    gw_own = _weight_grad_in(xm_t, dp, _tie(owners, rs_in_pair.token), "weight_grad_in_own")
    _, (r1_in,) = rs_in_pair.wait(gw_own)
    own_in, send_in = _pair_sum(gw_own, r1_in, j4, "pair_sum_in")
    rs_in_chips = _Exchange("rs_in_chips", [send_in], [jax.ShapeDtypeStruct((3, D, WSH), BF16)], 3, _chips_plan)
    grad_x, part_i = _input_backward(dp, w_in_all, x2, ctx[0], dx1, _tie(modx, rs_in_chips.token), modc, norm_w)

    dl = dlg[:, 0, 0:4]
    dlg_row = jnp.pad(dl[:, 0::2].reshape(1, N_HEADS), ((0, 0), (0, D - N_HEADS))) + jnp.pad(
        dl[:, 1::2].reshape(1, N_HEADS), ((0, 0), (N_HEADS, D - 2 * N_HEADS)))
    partials = _rows(16, (0, part_m[0:1]), (1, part_i[2:3]), (2, part_c[3:4]), (3, part_m[2:3]), (4, part_c[0:3]),
                     (7, part_i[0:2]), (9, part_m[1:2]), (10, part_i[3:5]), (13, part_m[3:4]), (14, dlg_row))
    got = _all_gather_small(partials, "gather_partials")
    tot = _sum_devices(got, "sum_partials")
    dmodc = tot[10:13].reshape(1, 3 * D)
    dmod16 = _rows(16, (0, got[:, 7:10, :].reshape(N_DEV, 3 * D)), (N_DEV, dmodc))
    dmod16 = lax.dynamic_slice(dmod16, (0, me * ADA_SH), (16, ADA_SH))
    dmodc8 = _rows(8, (0, dmod16[8:9]))
    g_ada_w, d_ada_w, nm_ada_w, nv_ada_w, cctx_part = _ada_backward(
        act16.T, dmod16, dmodc8, ada_w[0], m_ada_w[0], v_ada_w[0])
    dact_ctx = _sum_devices(_all_gather_small(cctx_part, "gather_cctx"), "sum_cctx")[0:1]

    my_lanes = lambda a: _pad_lanes(lax.dynamic_slice(a, (0, me * csh), (a.shape[0], csh)))
    small = lambda cc, nw, cb, gn, fn, ab, dlt, cw: _rows(
        16, (0, cc[None]), (1, nw), (2, cb), (3, gn), (4, fn[None]), (5, ab.reshape(3, D)),
        (8, dlt.reshape(1, 2 * N_HEADS)), (9, cw[0]))
    w_sm = small(c_ctx, norm_w, conv_b, gn_w, final_norm_w, ada_b, decay_logit, conv_w)
    m_sm = small(m_c_ctx, m_norm_w, m_conv_b, m_gn_w, m_final_norm_w, m_ada_b, m_decay_logit, m_conv_w)
    v_sm = small(v_c_ctx, v_norm_w, v_conv_b, v_gn_w, v_final_norm_w, v_ada_b, v_decay_logit, v_conv_w)
    g_sm = _rows(16, (0, dact_ctx), (1, tot[1:4]), (4, tot[0:1]), (5, tot[7:10]), (8, tot[14:15]), (9, my_lanes(tot[4:7])))
    g_sm2 = _rows(16, (5, tot[10:13]))
    sm = _adam_small(w_sm, g_sm, g_sm2, m_sm, v_sm, 0, 8)

    _, (r2_sq,) = rs_sq_chips.wait(sm[0])
    _, (r2_in,) = rs_in_chips.wait(sm[0])
    g_w_in, d_w_in, nm_w_in, nv_w_in = _adam_sharded(w_in[0], m_w_in[0], v_w_in[0], own_in, r2_in, "adam_w_in")
    stack = lambda a, b, c_: jnp.concatenate([a[0], b[0], c_[0]], axis=0)
    sq = _adam_sharded(stack(w_a, w_b, w_out), stack(m_w_a, m_w_b, m_w_out), stack(v_w_a, v_w_b, v_w_out),
                       own_sq, r2_sq, "adam_square")
    sq = [o.reshape(3, 1, RSH, D) for o in sq]

    def unpack(a, sq_i):
        return [a[0], a[1:2], None, a[5:8].reshape(1, 3 * D), None, a[9:12, 0:csh][None], a[2:3],
                a[8, 0:2 * N_HEADS].reshape(1, 2, N_HEADS), a[3:4], sq_i[0], sq_i[1], sq_i[2], a[4]]

    outs = []
    for a, sq_i, ada_i, in_i in zip(sm, sq, (g_ada_w, d_ada_w, nm_ada_w, nv_ada_w), (g_w_in, d_w_in, nm_w_in, nv_w_in)):
        group = unpack(a, sq_i)
        group[2] = ada_i[None]
        group[4] = in_i[None]
        outs += group
    return (tot[13, 0], grad_x[None], *outs)
```

```python
import functools
import math

import jax
import jax.numpy as jnp
from jax import lax
from jax.experimental import pallas as pl
from jax.experimental.pallas import tpu as pltpu

F32 = jnp.float32
BF16 = jnp.bfloat16
MESH = pl.DeviceIdType.MESH
_HBM_SPEC = pl.BlockSpec(memory_space=pltpu.HBM)
_SEM_SPEC = pl.BlockSpec(memory_space=pltpu.SEMAPHORE)
_EFFECT = pltpu.SideEffectType.DATAFLOW_SIDE_EFFECTING

N_DEV = 8
D = 1024
N_HEADS = 8
DK = 64
DV = 128
GRID_W = 64
ROPE_BASE = 10000.0
EPS = 1e-6
PW = 9 * D
WSH = PW // N_DEV
RSH = D // N_DEV
ADA_SH = 3 * D // N_DEV
TM = 256
RET_C = 256
HALO = 16
LANES = 128
VMEM_LIMIT = 60 * 1024 * 1024

ADAM_LR = 0.001
ADAM_B1 = 0.9
ADAM_B2 = 0.999
ADAM_EPS = 1e-08
ADAM_WD = 0.01
ADAM_STEP = 10

CB_H, CB_BG, CB_CG, CB_ZA, CB_QK, CB_V, CB_ZB, CB_GA, CB_GB = range(9)


def _cparams(**kw):
    return pltpu.CompilerParams(vmem_limit_bytes=VMEM_LIMIT, **kw)


def _dot(a, b):
    return jnp.dot(a, b, preferred_element_type=F32)


def _dot_nt(a, b):
    return lax.dot_general(a, b, (((1,), (1,)), ((), ())), preferred_element_type=F32)


def _dot_tn(a, b):
    return lax.dot_general(a, b, (((0,), (0,)), ((), ())), preferred_element_type=F32)


def _sigmoid(z):
    return 1.0 / (1.0 + jnp.exp(-z))


def _row_tile(rows):
    return TM if rows % TM == 0 else rows


def _coords():
    return lax.axis_index("x"), lax.axis_index("y"), lax.axis_index("c")


def _flip(v, bit):
    return 1 - v if bit else v


def _all_gather_small(blk, name):
    rows, cols = blk.shape

    def body(x_ref, out_ref, send_sems, recv_sems, local_sem):
        x, y, c = _coords()
        me = 4 * x + 2 * y + c
        mine = pltpu.make_async_copy(x_ref, out_ref.at[me], local_sem)
        mine.start()

        def copy(k, slot):
            peer = (_flip(x, k & 4), _flip(y, k & 2), _flip(c, k & 1))
            return pltpu.make_async_remote_copy(
                src_ref=x_ref, dst_ref=out_ref.at[slot], send_sem=send_sems.at[k - 1],
                recv_sem=recv_sems.at[k - 1], device_id=peer, device_id_type=MESH)

        for k in range(1, N_DEV):
            copy(k, me).start()
        for k in range(1, N_DEV):
            copy(k, jnp.bitwise_xor(me, k)).wait_recv()
        for k in range(1, N_DEV):
            copy(k, me).wait_send()
        mine.wait()

    return pl.pallas_call(
        body, name=name,
        out_shape=jax.ShapeDtypeStruct((N_DEV, rows, cols), blk.dtype),
        in_specs=[pl.BlockSpec(memory_space=pltpu.VMEM)],
        out_specs=pl.BlockSpec(memory_space=pltpu.VMEM),
        scratch_shapes=[pltpu.SemaphoreType.DMA((N_DEV - 1,)), pltpu.SemaphoreType.DMA((N_DEV - 1,)),
                        pltpu.SemaphoreType.DMA],
    )(blk)


def _all_gather_weights(shards):
    n = len(shards)

    def body(*refs):
        srcs, outs, done = refs[:n], refs[n:2 * n], refs[2 * n]
        send_sems, recv_sems, local_sems = refs[2 * n + 1:]
        x, y, c = _coords()
        me = 4 * x + 2 * y + c
        sib = (x, y, 1 - c)
        chips = [(1 - x, y), (x, 1 - y), (1 - x, 1 - y)]

        def slot(px, py, pc):
            return 4 * px + 2 * py + pc

        def copy(a, k, block_slot, to, own=False):
            return pltpu.make_async_remote_copy(
                src_ref=srcs[a] if own else outs[a].at[block_slot], dst_ref=outs[a].at[block_slot],
                send_sem=send_sems.at[7 * a + k], recv_sem=recv_sems.at[7 * a + k],
                device_id=to, device_id_type=MESH)

        locals_ = [pltpu.make_async_copy(srcs[a], outs[a].at[me], local_sems.at[a]) for a in range(n)]
        for cp in locals_:
            cp.start()
        sent = []
        for a in range(n):
            sent.append(copy(a, 0, me, sib, own=True))
            for j, chip in enumerate(chips):
                sent.append(copy(a, 1 + j, me, (*chip, c), own=True))
        for cp in sent:
            cp.start()
        for j, chip in enumerate(chips):
            for a in range(n):
                copy(a, 1 + j, slot(*chip, c), sib).wait_recv()
                fwd = copy(a, 4 + j, slot(*chip, c), sib)
                fwd.start()
                sent.append(fwd)
        for a in range(n):
            copy(a, 0, slot(x, y, 1 - c), sib).wait_recv()
            for j, chip in enumerate(chips):
                copy(a, 4 + j, slot(*chip, 1 - c), sib).wait_recv()
        for cp in sent:
            cp.wait_send()
        for cp in locals_:
            cp.wait()
        done[...] = jnp.zeros((8, LANES), F32)

    anyspec = pl.BlockSpec(memory_space=pl.ANY)
    return pl.pallas_call(
        body, name="all_gather_weights",
        out_shape=[jax.ShapeDtypeStruct((N_DEV,) + s.shape, s.dtype) for s in shards]
        + [jax.ShapeDtypeStruct((8, LANES), F32)],
        in_specs=[anyspec] * n, out_specs=[anyspec] * n + [pl.BlockSpec(memory_space=pltpu.VMEM)],
        scratch_shapes=[pltpu.SemaphoreType.DMA((7 * n,)), pltpu.SemaphoreType.DMA((7 * n,)),
                        pltpu.SemaphoreType.DMA((n,))],
    )(*shards)


class _Exchange:
    def __init__(self, name, srcs, land_shapes, n_copies, plan):
        self.name, self.plan, self.n_copies = name, plan, n_copies
        self.n_src, self.n_land = len(srcs), len(land_shapes)
        hbm = lambda a: pltpu.HBM(a.shape, a.dtype)
        n = self.n_src + self.n_land
        lands = [pltpu.with_memory_space_constraint(lax.empty(s.shape, s.dtype), pltpu.HBM) for s in land_shapes]
        srcs = [pltpu.with_memory_space_constraint(s, pltpu.HBM) for s in srcs]

        def body(*refs):
            send_sems, recv_sems = refs[n], refs[n + 1]
            for cp in self._copies(refs, send_sems, recv_sems):
                cp.start()
            refs[-1][...] = jnp.zeros((8, LANES), F32)

        outs = pl.pallas_call(
            body, name=name + "_start",
            out_shape=(pltpu.SemaphoreType.DMA((n_copies,)), pltpu.SemaphoreType.DMA((n_copies,)),
                       *[hbm(a) for a in srcs], *[hbm(a) for a in lands], jax.ShapeDtypeStruct((8, LANES), F32)),
            in_specs=[_HBM_SPEC] * n,
            out_specs=(_SEM_SPEC, _SEM_SPEC, *[_HBM_SPEC] * n, pl.BlockSpec(memory_space=pltpu.VMEM)),
            input_output_aliases={i: 2 + i for i in range(n)},
            compiler_params=pltpu.CompilerParams(has_side_effects=_EFFECT),
        )(*srcs, *lands)
        self.send_sems, self.recv_sems = outs[0], outs[1]
        self.buffers = list(outs[2:2 + n])
        self.token = outs[-1]

    def _copies(self, refs, send_sems, recv_sems):
        src_refs, land_refs = refs[:self.n_src], refs[self.n_src:self.n_src + self.n_land]
        return [pltpu.make_async_remote_copy(src_ref=s, dst_ref=d, send_sem=send_sems.at[k], recv_sem=recv_sems.at[k],
                                             device_id=dev, device_id_type=MESH)
                for k, (s, d, dev) in enumerate(self.plan(src_refs, land_refs))]

    def wait(self, after):
        n = self.n_src + self.n_land

        def body(*refs):
            for cp in self._copies(refs, refs[n], refs[n + 1]):
                cp.wait_send()
                cp.wait_recv()

        outs = pl.pallas_call(
            body, name=self.name + "_wait",
            out_shape=tuple(pltpu.HBM(b.shape, b.dtype) for b in self.buffers),
            in_specs=[_HBM_SPEC] * n + [_SEM_SPEC, _SEM_SPEC, pl.BlockSpec(memory_space=pl.ANY)],
            out_specs=tuple([_HBM_SPEC] * n),
            input_output_aliases={i: i for i in range(n)},
            compiler_params=pltpu.CompilerParams(has_side_effects=_EFFECT),
        )(*self.buffers, self.send_sems, self.recv_sems, after)
        return list(outs[:self.n_src]), list(outs[self.n_src:])


def _pair_plan(src_index):
    def plan(srcs, lands):
        x, y, c = _coords()
        chip = 2 * x + y
        return [(srcs[0].at[src_index(j, chip, c)], lands[0].at[j], (x, y, 1 - c)) for j in range(4)]
    return plan


def _chips_plan(srcs, lands):
    x, y, c = _coords()
    return [(srcs[0].at[j - 1], lands[0].at[j - 1], (_flip(x, j & 2), _flip(y, j & 1), c)) for j in range(1, 4)]


def _gather_plan(srcs, lands):
    x, y, c = _coords()
    me = 4 * x + 2 * y + c
    return [(srcs[a], lands[a].at[me], (_flip(x, k & 4), _flip(y, k & 2), _flip(c, k & 1)))
            for a in range(len(srcs)) for k in range(1, N_DEV)]


def _pair_sum(grad, recv, owners, name):
    _, rows, cols = grad.shape
    tr = _row_tile(rows)

    def body(own_ref, g_ref, r_ref, mine_ref, send_ref):
        j = pl.program_id(1)
        total = g_ref[...] + r_ref[...]

        @pl.when(j == 0)
        def _():
            mine_ref[...] = total

        @pl.when(j > 0)
        def _():
            send_ref[...] = total.astype(BF16)

    grid_spec = pltpu.PrefetchScalarGridSpec(
        num_scalar_prefetch=1, grid=(rows // tr, 4),
        in_specs=[pl.BlockSpec((1, tr, cols), lambda i, j, own: (own[j], i, 0)),
                  pl.BlockSpec((1, tr, cols), lambda i, j, own: (j, i, 0))],
        out_specs=[pl.BlockSpec((1, tr, cols), lambda i, j, own: (0, i, 0)),
                   pl.BlockSpec((1, tr, cols), lambda i, j, own: (jnp.maximum(j - 1, 0), i, 0))])
    return pl.pallas_call(
        body, name=name, grid_spec=grid_spec,
        out_shape=[jax.ShapeDtypeStruct((1, rows, cols), F32), jax.ShapeDtypeStruct((3, rows, cols), BF16)],
        compiler_params=_cparams(dimension_semantics=("arbitrary", "arbitrary")),
    )(owners, grad, recv)


def _modulation(c16, ada_w_sh, ada_b_sh, decay_logit):
    def body(c_ref, w_ref, b_ref, dl_ref, mod_ref, act_ref, lg_ref):
        cv = c_ref[...]
        act = cv * _sigmoid(cv)
        act_ref[...] = act
        mod_ref[...] = jnp.dot(act, w_ref[...], preferred_element_type=F32,
                               precision=lax.Precision.HIGHEST) + b_ref[...]
        z = dl_ref[...]
        lg_ref[...] = jnp.minimum(z, 0.0) - jnp.log(1.0 + jnp.exp(-jnp.abs(z)))

    return pl.pallas_call(
        body, name="modulation",
        out_shape=[jax.ShapeDtypeStruct((16, ADA_SH), F32), jax.ShapeDtypeStruct((16, D), F32),
                   jax.ShapeDtypeStruct(decay_logit.shape, F32)],
        compiler_params=_cparams(),
    )(c16, ada_w_sh, ada_b_sh, decay_logit)


def _adam_update(w, g, m, v):
    m2 = ADAM_B1 * m + (1.0 - ADAM_B1) * g
    v2 = ADAM_B2 * v + (1.0 - ADAM_B2) * (g * g)
    m_hat = m2 / (1.0 - ADAM_B1 ** ADAM_STEP)
    v_hat = v2 / (1.0 - ADAM_B2 ** ADAM_STEP)
    delta = -ADAM_LR * (m_hat / (jnp.sqrt(v_hat) + ADAM_EPS) + ADAM_WD * w)
    return delta, m2, v2


def _adam_sharded(w, m, v, own, recv, name):
    rows, cols = w.shape
    tr = _row_tile(rows)

    def body(w_ref, m_ref, v_ref, g0, g1, g2, g3, g_ref, d_ref, m_out, v_out):
        g = ((g0[0] + g1[0].astype(F32)) + g2[0].astype(F32)) + g3[0].astype(F32)
        delta, m2, v2 = _adam_update(w_ref[...], g, m_ref[...], v_ref[...])
        g_ref[...] = g
        d_ref[...] = delta
        m_out[...] = m2
        v_out[...] = v2

    flat = pl.BlockSpec((tr, cols), lambda i: (i, 0))
    part = lambda j: pl.BlockSpec((1, tr, cols), lambda i, j=j: (j, i, 0))
    return pl.pallas_call(
        body, name=name, grid=(rows // tr,),
        in_specs=[flat, flat, flat, part(0), part(0), part(1), part(2)],
        out_specs=[flat] * 4,
        out_shape=[jax.ShapeDtypeStruct((rows, cols), F32)] * 4,
        compiler_params=_cparams(dimension_semantics=("arbitrary",)),
    )(w, m, v, own, recv, recv, recv)


def _ada_backward(act_t, dmod16, dmodc8, ada_w_sh, m, v):
    def body(at_ref, dm_ref, dc_ref, w_ref, m_ref, v_ref, g_ref, d_ref, m_out, v_out, pc_ref):
        g = jnp.dot(at_ref[...], dm_ref[...], preferred_element_type=F32, precision=lax.Precision.HIGHEST)
        w = w_ref[...]
        delta, m2, v2 = _adam_update(w, g, m_ref[...], v_ref[...])
        g_ref[...] = g
        d_ref[...] = delta
        m_out[...] = m2
        v_out[...] = v2
        pc_ref[...] = lax.dot_general(dc_ref[...], w, (((1,), (1,)), ((), ())), preferred_element_type=F32,
                                      precision=lax.Precision.HIGHEST)

    return pl.pallas_call(
        body, name="ada_backward",
        out_shape=[jax.ShapeDtypeStruct((D, ADA_SH), F32)] * 4 + [jax.ShapeDtypeStruct((8, D), F32)],
        compiler_params=_cparams(),
    )(act_t, dmod16, dmodc8, ada_w_sh, m, v)


def _sum_devices(gathered, name):
    _, rows, cols = gathered.shape

    def body(g_ref, o_ref):
        acc = g_ref[0]
        for d in range(1, N_DEV):
            acc = acc + g_ref[d]
        o_ref[...] = acc

    return pl.pallas_call(body, name=name, out_shape=jax.ShapeDtypeStruct((rows, cols), F32),
                          compiler_params=_cparams())(gathered)


SMALL_PARAMS = ("c_ctx", "norm_w", "conv_b", "gn_w", "final_norm_w", "ada_b", "decay_logit", "conv_w")


def _adam_small(tot, dact_ctx, conv_w_grad, params):
    n = len(SMALL_PARAMS)

    def body(tot_ref, dact_ref, cwg_ref, *refs):
        ins, outs = refs[:3 * n], refs[3 * n:]
        nh = 2 * N_HEADS
        raw = {
            "c_ctx": dact_ref[0:1, :],
            "norm_w": tot_ref[1:2, :], "conv_b": tot_ref[2:3, :], "gn_w": tot_ref[3:4, :],
            "final_norm_w": tot_ref[0:1, :],
            "ada_b": jnp.concatenate([tot_ref[7 + r:8 + r, :] + tot_ref[10 + r:11 + r, :] for r in range(3)], axis=1),
            "decay_logit": tot_ref[14:15, 0:nh],
            "conv_w": cwg_ref[...],
        }
        for k, name in enumerate(SMALL_PARAMS):
            w, m, v = ins[3 * k][...], ins[3 * k + 1][...], ins[3 * k + 2][...]
            g = raw[name]
            if name == "c_ctx":
                s = _sigmoid(w)
                g = g * (s * (1.0 + w * (1.0 - s)))
            elif name == "decay_logit":
                g = g * (1.0 - _sigmoid(w))
            delta, m2, v2 = _adam_update(w, g, m, v)
            for ref, val in zip(outs[4 * k:4 * k + 4], (g, delta, m2, v2)):
                ref[...] = val

    flat = [a for trio in params for a in trio]
    out_shape = [jax.ShapeDtypeStruct(trio[0].shape, F32) for trio in params for _ in range(4)]
    outs = pl.pallas_call(body, name="adam_small", out_shape=out_shape, compiler_params=_cparams())(
        tot, dact_ctx, conv_w_grad, *flat)
    return [outs[4 * k:4 * k + 4] for k in range(n)]


def _adam_square(params, own, recv):
    def body(own_ref, recv_ref, *refs):
        ins, outs = refs[:9], refs[9:]
        for k in range(3):
            rows = slice(k * RSH, (k + 1) * RSH)
            g = own_ref[0, rows, :]
            for j in range(3):
                g = g + recv_ref[j, rows, :].astype(F32)
            delta, m2, v2 = _adam_update(ins[3 * k][...], g, ins[3 * k + 1][...], ins[3 * k + 2][...])
            for ref, val in zip(outs[4 * k:4 * k + 4], (g, delta, m2, v2)):
                ref[...] = val

    flat = [a for trio in params for a in trio]
    outs = pl.pallas_call(body, name="adam_square", out_shape=[jax.ShapeDtypeStruct((RSH, D), F32)] * 12,
                          compiler_params=_cparams())(own, recv, *flat)
    return [outs[4 * k:4 * k + 4] for k in range(3)]


def _rope(t, cos, sin):
    lane = lax.broadcasted_iota(jnp.int32, (1, LANES), 1)
    first_half = jnp.bitwise_and(lane, DK // 2) == 0
    partner = jnp.where(first_half, pltpu.roll(t, LANES - DK // 2, 1), pltpu.roll(t, DK // 2, 1))
    return t * cos + partner * sin


def _load_w_in(step, w_hbm, w_vmem, sems):
    @pl.when(step == 0)
    def _():
        copies = [pltpu.make_async_copy(w_hbm.at[d], w_vmem.at[:, d * WSH:(d + 1) * WSH], sems.at[d])
                  for d in range(N_DEV)]
        for cp in copies:
            cp.start()
        for cp in copies:
            cp.wait()


def _in_projection(x, ctx, modx, modc, norm_w, w_in, cos_t, sin_t):
    n_x_tiles = x.shape[0] // TM
    lext = x.shape[0] + ctx.shape[0]
    qk = N_HEADS * DK

    def body(x_ref, ctx_ref, mx_ref, mc_ref, nw_ref, w_hbm, cos_ref, sin_ref, p_ref, xmt_ref, w_ref, w_sems):
        i = pl.program_id(0)
        _load_w_in(i, w_hbm, w_ref, w_sems)
        is_ctx = i >= n_x_tiles
        x = jnp.where(is_ctx, ctx_ref[...], x_ref[...])
        r = lax.rsqrt(jnp.mean(x * x, axis=-1, keepdims=True) + EPS)
        sh = jnp.where(is_ctx, mc_ref[0:1, :], mx_ref[0:1, :])
        sc = jnp.where(is_ctx, mc_ref[1:2, :], mx_ref[1:2, :])
        xm = (x * r * nw_ref[...]) * (1.0 + sc) + sh
        xmb = xm.astype(BF16)
        xmt_ref[...] = xm.T.astype(BF16)
        cos, sin = cos_ref[...], sin_ref[...]
        for cb in range(PW // D):
            acc = _dot(xmb, w_ref[:, cb * D:(cb + 1) * D])
            if cb != CB_QK:
                p_ref[:, cb * D:(cb + 1) * D] = acc.astype(BF16)
                continue
            for col in range(0, D, LANES):
                piece = acc[:, col:col + LANES]
                if col >= qk:
                    piece = piece * (DK ** -0.5)
                p_ref[:, cb * D + col:cb * D + col + LANES] = _rope(piece, cos, sin).astype(BF16)

    row = lambda w: pl.BlockSpec((TM, w), lambda i: (i, 0))
    full = lambda a: pl.BlockSpec(a.shape, lambda i: (0,) * a.ndim)
    return pl.pallas_call(
        body, name="in_projection", grid=(lext // TM,),
        in_specs=[pl.BlockSpec((TM, D), lambda i: (jnp.minimum(i, n_x_tiles - 1), 0)), full(ctx),
                  full(modx), full(modc), full(norm_w), pl.BlockSpec(memory_space=pl.ANY), row(LANES), row(LANES)],
        out_specs=[row(PW), pl.BlockSpec((D, TM), lambda i: (0, i))],
        out_shape=[jax.ShapeDtypeStruct((lext, PW), BF16), jax.ShapeDtypeStruct((D, lext), BF16)],
        scratch_shapes=[pltpu.VMEM((D, PW), BF16), pltpu.SemaphoreType.DMA((N_DEV,))],
        compiler_params=_cparams(dimension_semantics=("arbitrary",)),
    )(x, ctx, modx, modc, norm_w, w_in, cos_t, sin_t)


def _decay_tables(lgf, lgb, n):
    i = lax.broadcasted_iota(jnp.int32, (n, 1), 0).astype(F32)
    return dict(i=i, k_f=jnp.exp(lgf * (n - 1.0 - i)), k_b=jnp.exp(lgb * i),
                q_f=jnp.exp(lgf * (i + 1.0)), q_b=jnp.exp(lgb * (n - i)))


def _decay_matrix(lgf, lgb, n, transposed=False):
    ii = lax.broadcasted_iota(jnp.int32, (n, n), 0)
    jj = lax.broadcasted_iota(jnp.int32, (n, n), 1)
    diff = (jj - ii if transposed else ii - jj).astype(F32)
    low = jnp.exp(lgf * jnp.maximum(diff, 0.0))
    up = jnp.exp(lgb * jnp.maximum(-diff, 0.0))
    return jnp.where(diff > 0, low, jnp.where(diff < 0, up, 2.0)), diff


def _cat_lanes(a, b):
    return jnp.concatenate([a.astype(BF16), b.astype(BF16)], axis=1)


def _retention_forward(p_ext, lg, seq, ctx_len):
    lext = seq + ctx_len
    n_chunks = seq // RET_C
    C = RET_C

    def body(lg_ref, q_ref, k_ref, v_ref, rn_ref, rstd_ref, kv_scr, sf_scr, sb_scr):
        pair = pl.program_id(0)
        lane = lax.broadcasted_iota(jnp.int32, (1, LANES), 1)
        heads = range(2)
        hmask = [(lane // DK == hh).astype(F32) for hh in heads]
        lgf = [lg_ref[0, 2 * pair + hh] for hh in heads]
        lgb = [lg_ref[1, 2 * pair + hh] for hh in heads]
        vs = [slice(hh * DV, (hh + 1) * DV) for hh in heads]
        t = [_decay_tables(lgf[hh], lgb[hh], C) for hh in heads]
        kc_all = k_ref[seq:lext, :].astype(F32)
        s0 = []
        for hh in heads:
            tc = _decay_tables(lgf[hh], lgb[hh], ctx_len)
            kc = kc_all * hmask[hh]
            s0.append(_dot_tn(_cat_lanes(kc * tc["k_f"], kc * tc["k_b"]), v_ref[seq:lext, vs[hh]]))

        def increments(c, carry):
            rows = pl.ds(pl.multiple_of(c * C, C), C)
            k_all = k_ref[rows, :].astype(F32)
            for hh in heads:
                k = k_all * hmask[hh]
                kv_scr[hh, c] = _dot_tn(_cat_lanes(k * t[hh]["k_f"], k * t[hh]["k_b"]), v_ref[rows, vs[hh]])
            return carry

        lax.fori_loop(0, n_chunks, increments, 0, unroll=2)
        gf_c = [jnp.exp(lgf[hh] * C) for hh in heads]
        gb_c = [jnp.exp(lgb[hh] * C) for hh in heads]

        def scan_f(c, s):
            for hh in heads:
                sf_scr[hh, c] = s[hh]
            return tuple(gf_c[hh] * s[hh] + kv_scr[hh, c, 0:LANES, :] for hh in heads)

        def scan_b(n, s):
            c = n_chunks - 1 - n
            for hh in heads:
                sb_scr[hh, c] = s[hh]
            return tuple(gb_c[hh] * s[hh] + kv_scr[hh, c, LANES:2 * LANES, :] for hh in heads)

        lax.fori_loop(0, n_chunks, scan_f, tuple(s0[hh][0:LANES] for hh in heads))
        lax.fori_loop(0, n_chunks, scan_b, tuple(s0[hh][LANES:2 * LANES] for hh in heads))
        dmat = [_decay_matrix(lgf[hh], lgb[hh], C)[0] for hh in heads]

        def outputs(c, carry):
            rows = pl.ds(pl.multiple_of(c * C, C), C)
            q_all = q_ref[rows, :].astype(F32)
            k = k_ref[rows, :]
            for hh in heads:
                q = q_all * hmask[hh]
                v = v_ref[rows, vs[hh]]
                s = _dot_nt(q.astype(BF16), k)
                inner = _dot((s * dmat[hh]).astype(BF16), v)
                states = jnp.concatenate([sf_scr[hh, c], sb_scr[hh, c]], axis=0).astype(BF16)
                ret = inner + _dot(_cat_lanes(q * t[hh]["q_f"], q * t[hh]["q_b"]), states)
                mu = jnp.mean(ret, axis=-1, keepdims=True)
                cen = ret - mu
                rstd = lax.rsqrt(jnp.mean(cen * cen, axis=-1, keepdims=True) + EPS)
                rn_ref[rows, vs[hh]] = cen * rstd
                rstd_ref[rows, vs[hh]] = jnp.broadcast_to(rstd, (C, DV))
            return carry

        lax.fori_loop(0, n_chunks, outputs, 0, unroll=2)

    qk0 = CB_QK * D // LANES
    return pl.pallas_call(
        body, name="retention_forward", grid=(N_HEADS // 2,),
        in_specs=[pl.BlockSpec(memory_space=pltpu.SMEM),
                  pl.BlockSpec((lext, LANES), lambda g: (0, qk0 + g)),
                  pl.BlockSpec((lext, LANES), lambda g: (0, qk0 + N_HEADS // 2 + g)),
                  pl.BlockSpec((lext, 2 * DV), lambda g: (0, CB_V * D // (2 * DV) + g))],
        out_specs=[pl.BlockSpec((seq, 2 * DV), lambda g: (0, g))] * 2,
        out_shape=[jax.ShapeDtypeStruct((seq, D), F32)] * 2,
        scratch_shapes=[pltpu.VMEM((2, n_chunks, 2 * LANES, DV), F32), pltpu.VMEM((2, n_chunks, LANES, DV), F32),
                        pltpu.VMEM((2, n_chunks, LANES, DV), F32)],
        compiler_params=_cparams(dimension_semantics=("arbitrary",)),
    )(lg, p_ext, p_ext, p_ext)


def _retention_backward(p_ext, dret, lg, cos_t, sin_t, seq, ctx_len):
    lext = seq + ctx_len
    n_chunks = seq // RET_C
    C = RET_C

    def body(lg_ref, q_ref, k_ref, v_ref, do_ref, cos_ref, sin_ref, dq_ref, dk_ref, dv_ref, dlg_ref,
             kv_scr, g_scr, sf_scr, sb_scr, gfn_scr, gbp_scr):
        pair = pl.program_id(0)
        lane = lax.broadcasted_iota(jnp.int32, (1, LANES), 1)
        heads = range(2)
        hmask = [(lane // DK == hh).astype(F32) for hh in heads]
        lgf = [lg_ref[0, 2 * pair + hh] for hh in heads]
        lgb = [lg_ref[1, 2 * pair + hh] for hh in heads]
        vs = [slice(hh * DV, (hh + 1) * DV) for hh in heads]
        t = [_decay_tables(lgf[hh], lgb[hh], C) for hh in heads]
        tc = [_decay_tables(lgf[hh], lgb[hh], ctx_len) for hh in heads]
        kc_all = k_ref[seq:lext, :].astype(F32)
        kc = [kc_all * hmask[hh] for hh in heads]
        vc = [v_ref[seq:lext, vs[hh]] for hh in heads]
        kc_cat = [_cat_lanes(kc[hh] * tc[hh]["k_f"], kc[hh] * tc[hh]["k_b"]) for hh in heads]
        s0 = [_dot_tn(kc_cat[hh], vc[hh]) for hh in heads]

        def increments(c, carry):
            rows = pl.ds(pl.multiple_of(c * C, C), C)
            k_all = k_ref[rows, :].astype(F32)
            q_all = q_ref[rows, :].astype(F32)
            for hh in heads:
                k, q = k_all * hmask[hh], q_all * hmask[hh]
                kv_scr[hh, c] = _dot_tn(_cat_lanes(k * t[hh]["k_f"], k * t[hh]["k_b"]), v_ref[rows, vs[hh]])
                g_scr[hh, c] = _dot_tn(_cat_lanes(q * t[hh]["q_f"], q * t[hh]["q_b"]), do_ref[rows, vs[hh]])
            return carry

        lax.fori_loop(0, n_chunks, increments, 0, unroll=2)
        gf_c = [jnp.exp(lgf[hh] * C) for hh in heads]
        gb_c = [jnp.exp(lgb[hh] * C) for hh in heads]

        def scan_f(c, s):
            for hh in heads:
                sf_scr[hh, c] = s[hh]
            return tuple(gf_c[hh] * s[hh] + kv_scr[hh, c, 0:LANES, :] for hh in heads)

        def scan_b(n, s):
            c = n_chunks - 1 - n
            for hh in heads:
                sb_scr[hh, c] = s[hh]
            return tuple(gb_c[hh] * s[hh] + kv_scr[hh, c, LANES:2 * LANES, :] for hh in heads)

        def scan_gf(n, carry):
            c = n_chunks - 1 - n
            for hh in heads:
                gfn_scr[hh, c] = carry[hh]
            return tuple(g_scr[hh, c, 0:LANES, :] + gf_c[hh] * carry[hh] for hh in heads)

        def scan_gb(c, carry):
            for hh in heads:
                gbp_scr[hh, c] = carry[hh]
            return tuple(g_scr[hh, c, LANES:2 * LANES, :] + gb_c[hh] * carry[hh] for hh in heads)

        lax.fori_loop(0, n_chunks, scan_f, tuple(s0[hh][0:LANES] for hh in heads))
        lax.fori_loop(0, n_chunks, scan_b, tuple(s0[hh][LANES:2 * LANES] for hh in heads))
        zero_state = jnp.zeros((LANES, DV), F32)
        gf_first = lax.fori_loop(0, n_chunks, scan_gf, (zero_state, zero_state))
        gb_last = lax.fori_loop(0, n_chunks, scan_gb, (zero_state, zero_state))

        dmat, dmat_t, w_f, w_b = [], [], [], []
        for hh in heads:
            dm, diff = _decay_matrix(lgf[hh], lgb[hh], C)
            dmat.append(dm)
            dmat_t.append(_decay_matrix(lgf[hh], lgb[hh], C, transposed=True)[0])
            w_f.append(jnp.where(diff > 0, diff * dm, 0.0))
            w_b.append(jnp.where(diff < 0, -diff * dm, 0.0))

        def total(a):
            rows_, width = a.shape
            part = jnp.sum(a.reshape(rows_ // 8, 8, width), axis=0)
            return part[:, 0:LANES] + part[:, LANES:2 * LANES] if width == 2 * LANES else part

        def chunk_head(hh, c, rows, q_all, k_all, dlgf, dlgb):
            th = t[hh]
            qm = q_all * hmask[hh]
            km = k_all * hmask[hh]
            qb, kb = qm.astype(BF16), km.astype(BF16)
            v = v_ref[rows, vs[hh]]
            do = do_ref[rows, vs[hh]]
            s = _dot_nt(qb, kb)
            s_t = _dot_nt(kb, qb)
            dsd = _dot_nt(do, v)
            dsd_t = _dot_nt(v, do)
            dq_in = _dot((dsd * dmat[hh]).astype(BF16), kb)
            dk_in = _dot((dsd_t * dmat_t[hh]).astype(BF16), qb)
            dv_in = _dot((s_t * dmat_t[hh]).astype(BF16), do)
            prod = s * dsd
            dlgf = dlgf + total(prod * w_f[hh])
            dlgb = dlgb + total(prod * w_b[hh])
            sf, sb = sf_scr[hh, c], sb_scr[hh, c]
            states = jnp.concatenate([sf, sb], axis=0).astype(BF16)
            dqc = _dot_nt(do, states)
            dqf = dqc[:, 0:LANES] * th["q_f"]
            dqb = dqc[:, LANES:2 * LANES] * th["q_b"]
            dq = (dq_in + dqf + dqb) * hmask[hh]
            dlgf = dlgf + total((th["i"] + 1.0) * (qm * dqf))
            dlgb = dlgb + total((C - th["i"]) * (qm * dqb))
            gfn, gbp = gfn_scr[hh, c], gbp_scr[hh, c]
            gstates = jnp.concatenate([gfn, gbp], axis=0).astype(BF16)
            dkc = _dot_nt(v, gstates)
            dkf = dkc[:, 0:LANES] * th["k_f"]
            dkb = dkc[:, LANES:2 * LANES] * th["k_b"]
            dk = dk_in + dkf + dkb
            dlgf = dlgf + total((C - 1.0 - th["i"]) * (km * dkf)) + C * gf_c[hh] * total(gfn * sf)
            dlgb = dlgb + total(th["i"] * (km * dkb)) + C * gb_c[hh] * total(gbp * sb)
            dv = dv_in + _dot(_cat_lanes(km * th["k_f"], km * th["k_b"]), gstates)
            dv_ref[rows, vs[hh]] = dv.astype(BF16)
            return dq, dk, dlgf, dlgb

        def chunk(c, carry):
            rows = pl.ds(pl.multiple_of(c * C, C), C)
            q_all = q_ref[rows, :].astype(F32)
            k_all = k_ref[rows, :].astype(F32)
            dq0, dk0, f0, b0 = chunk_head(0, c, rows, q_all, k_all, carry[0], carry[1])
            dq1, dk1, f1, b1 = chunk_head(1, c, rows, q_all, k_all, carry[2], carry[3])
            cos, sin = cos_ref[rows, :], sin_ref[rows, :]
            dq_ref[rows, :] = _rope(dq0 + dq1, cos, -sin).astype(BF16)
            dk_ref[rows, :] = (_rope(dk0 + dk1, cos, -sin) * (DK ** -0.5)).astype(BF16)
            return f0, b0, f1, b1

        zero = jnp.zeros((8, LANES), F32)
        sums = lax.fori_loop(0, n_chunks, chunk, (zero, zero, zero, zero), unroll=2)

        dlg = []
        dk_ctx = jnp.zeros((ctx_len, LANES), F32)
        for hh in heads:
            g0 = jnp.concatenate([gf_first[hh], gb_last[hh]], axis=0).astype(BF16)
            dkcc = _dot_nt(vc[hh], g0)
            dkcf = dkcc[:, 0:LANES] * tc[hh]["k_f"]
            dkcb = dkcc[:, LANES:2 * LANES] * tc[hh]["k_b"]
            dlgf = sums[2 * hh] + total((ctx_len - 1.0 - tc[hh]["i"]) * (kc[hh] * dkcf))
            dlgb = sums[2 * hh + 1] + total(tc[hh]["i"] * (kc[hh] * dkcb))
            dk_ctx = dk_ctx + (dkcf + dkcb) * (DK ** -0.5)
            dv_ref[seq:lext, vs[hh]] = _dot(kc_cat[hh], g0).astype(BF16)
            dlg += [jnp.sum(jnp.sum(a, axis=1, keepdims=True), axis=0, keepdims=True) for a in (dlgf, dlgb)]
        dk_ref[seq:lext, :] = dk_ctx.astype(BF16)
        dq_ref[seq:lext, :] = jnp.zeros((ctx_len, LANES), BF16)

        lane8 = lax.broadcasted_iota(jnp.int32, (8, LANES), 1)
        out = jnp.zeros((8, LANES), F32)
        for n, val in enumerate(dlg):
            out = jnp.where(lane8 == n, val, out)
        dlg_ref[0] = out

    qk0 = CB_QK * D // LANES
    q_spec = pl.BlockSpec((lext, LANES), lambda g: (0, qk0 + g))
    k_spec = pl.BlockSpec((lext, LANES), lambda g: (0, qk0 + N_HEADS // 2 + g))
    v_spec = pl.BlockSpec((lext, 2 * DV), lambda g: (0, CB_V * D // (2 * DV) + g))
    table = pl.BlockSpec((lext, LANES), lambda g: (0, 0))
    state = pltpu.VMEM((2, n_chunks, LANES, DV), F32)
    return pl.pallas_call(
        body, name="retention_backward", grid=(N_HEADS // 2,),
        in_specs=[pl.BlockSpec(memory_space=pltpu.SMEM), q_spec, k_spec, v_spec,
                  pl.BlockSpec((seq, 2 * DV), lambda g: (0, g)), table, table],
        out_specs=[pl.BlockSpec((lext, LANES), lambda g: (0, g)), pl.BlockSpec((lext, LANES), lambda g: (0, g)),
                   pl.BlockSpec((lext, 2 * DV), lambda g: (0, g)), pl.BlockSpec((1, 8, LANES), lambda g: (g, 0, 0))],
        out_shape=[jax.ShapeDtypeStruct((lext, N_HEADS * DK), BF16), jax.ShapeDtypeStruct((lext, N_HEADS * DK), BF16),
                   jax.ShapeDtypeStruct((lext, D), BF16), jax.ShapeDtypeStruct((N_HEADS // 2, 8, LANES), F32)],
        scratch_shapes=[pltpu.VMEM((2, n_chunks, 2 * LANES, DV), F32), pltpu.VMEM((2, n_chunks, 2 * LANES, DV), F32),
                        state, state, state, state],
        compiler_params=_cparams(dimension_semantics=("arbitrary",)),
    )(lg, p_ext, p_ext, p_ext, dret, cos_t, sin_t)


def _merge(p_ext, rn, rstd, x, target, w_a, w_b, w_out, vecs, seq):
    n_tiles = seq // TM
    hb = TM // HALO

    def body(h_ref, bg_ref, cg_ref, za_ref, zb_ref, ga_ref, gb_ref, hp_ref, hn_ref, cp_ref, cn_ref,
             rn_ref, rstd_ref, x_ref, t_ref, wa_ref, wb_ref, wo_ref, vec_ref,
             dx1_ref, dpa_ref, dconv_ref, dret_ref, at_ref, b_ref, part_ref):
        i = pl.program_id(0)
        f = lambda ref: ref[...].astype(F32)
        h, bg, cg, za, zb, ga, gb = f(h_ref), f(bg_ref), f(cg_ref), f(za_ref), f(zb_ref), f(ga_ref), f(gb_ref)
        gx, w0, w1, w2 = vec_ref[0:1, :], vec_ref[1:2, :], vec_ref[2:3, :], vec_ref[3:4, :]
        cb, gnw, fw = vec_ref[4:5, :], vec_ref[5:6, :], vec_ref[6:7, :]
        u = cg * h
        row = lax.broadcasted_iota(jnp.int32, (TM, 1), 0)
        u_prev = (f(cp_ref) * f(hp_ref))[HALO - 1:HALO, :]
        u_next = (f(cn_ref) * f(hn_ref))[0:1, :]
        u_prev = jnp.where(i == 0, 0.0, u_prev)
        u_next = jnp.where(i == n_tiles - 1, 0.0, u_next)
        u_up = jnp.where(row == 0, u_prev, pltpu.roll(u, 1, 0))
        u_dn = jnp.where(row == TM - 1, u_next, pltpu.roll(u, TM - 1, 0))
        conv = w0 * u_up + w1 * u + w2 * u_dn + cb
        sza = _sigmoid(za)
        silu_za = za * sza
        a_act = silu_za * bg * conv
        rn = rn_ref[...]
        szb = _sigmoid(zb)
        silu_zb = zb * szb
        rg = rn * gnw
        b_act = silu_zb * rg
        y_a = _dot(a_act.astype(BF16), wa_ref[...])
        y_b = _dot(b_act.astype(BF16), wb_ref[...])
        sga, sgb = _sigmoid(ga), _sigmoid(gb)
        mix = sga * y_a + sgb * y_b
        y = _dot(mix.astype(BF16), wo_ref[...])
        x1 = x_ref[...] + gx * y
        r1 = lax.rsqrt(jnp.mean(x1 * x1, axis=-1, keepdims=True) + EPS)
        xh1 = x1 * r1
        err = xh1 * fw - t_ref[...]
        loss = jnp.sum(jnp.sum(err * err, axis=1, keepdims=True), axis=0, keepdims=True) * (0.5 / D)
        dout = err * (1.0 / D)
        dxh = dout * fw
        dx1 = r1 * (dxh - xh1 * jnp.mean(dxh * xh1, axis=-1, keepdims=True))
        dx1_ref[...] = dx1
        dy = (dx1 * gx).astype(BF16)
        dmix = _dot_nt(dy, wo_ref[...])
        dya = (dmix * sga).astype(BF16)
        dyb = (dmix * sgb).astype(BF16)
        da = _dot_nt(dya, wa_ref[...])
        db = _dot_nt(dyb, wb_ref[...])
        dpa_ref[:, 0:D] = (da * silu_za * conv).astype(BF16)
        dpa_ref[:, D:2 * D] = (da * bg * conv * (sza * (1.0 + za * (1.0 - sza)))).astype(BF16)
        dpa_ref[:, 2 * D:3 * D] = (db * rg * (szb * (1.0 + zb * (1.0 - szb)))).astype(BF16)
        dpa_ref[:, 3 * D:4 * D] = (dmix * y_a * sga * (1.0 - sga)).astype(BF16)
        dpa_ref[:, 4 * D:5 * D] = (dmix * y_b * sgb * (1.0 - sgb)).astype(BF16)
        dconv_ref[...] = (da * silu_za * bg).astype(BF16)
        drn_n = db * silu_zb
        drn = drn_n * gnw
        rstd = rstd_ref[...]
        for hd in range(N_HEADS):
            sl = slice(hd * DV, (hd + 1) * DV)
            dh_, rh = drn[:, sl], rn[:, sl]
            m1 = jnp.mean(dh_, axis=-1, keepdims=True)
            m2 = jnp.mean(dh_ * rh, axis=-1, keepdims=True)
            dret_ref[:, sl] = (rstd[:, sl] * (dh_ - m1 - rh * m2)).astype(BF16)
        at_ref[0] = a_act.T.astype(BF16)
        at_ref[1] = b_act.T.astype(BF16)
        at_ref[2] = mix.T.astype(BF16)
        b_ref[0] = dya
        b_ref[1] = dyb
        b_ref[2] = dy

        @pl.when(i == 0)
        def _():
            part_ref[...] = jnp.zeros((8, D), F32)

        part_ref[0:1, :] += jnp.sum(dout * xh1, axis=0, keepdims=True)
        part_ref[1:2, :] += jnp.sum(dx1 * y, axis=0, keepdims=True)
        part_ref[2:3, :] += jnp.sum(drn_n * rn, axis=0, keepdims=True)
        part_ref[3:4, :] += jnp.broadcast_to(loss, (1, D))

    col = lambda cb_: pl.BlockSpec((TM, D), lambda i, cb_=cb_: (i, cb_))
    prev = lambda cb_: pl.BlockSpec((HALO, D), lambda i, cb_=cb_: (jnp.maximum(i * hb - 1, 0), cb_))
    nxt = lambda cb_: pl.BlockSpec((HALO, D), lambda i, cb_=cb_: (jnp.minimum((i + 1) * hb, n_tiles * hb - 1), cb_))
    tile = pl.BlockSpec((TM, D), lambda i: (i, 0))
    full = lambda a: pl.BlockSpec(a.shape, lambda i: (0,) * a.ndim)
    return pl.pallas_call(
        body, name="merge", grid=(n_tiles,),
        in_specs=[col(CB_H), col(CB_BG), col(CB_CG), col(CB_ZA), col(CB_ZB), col(CB_GA), col(CB_GB),
                  prev(CB_H), nxt(CB_H), prev(CB_CG), nxt(CB_CG),
                  tile, tile, tile, tile, full(w_a), full(w_b), full(w_out), full(vecs)],
        out_specs=[tile, pl.BlockSpec((TM, 5 * D), lambda i: (i, 0)), tile, tile,
                   pl.BlockSpec((3, D, TM), lambda i: (0, 0, i)), pl.BlockSpec((3, TM, D), lambda i: (0, i, 0)),
                   pl.BlockSpec((8, D), lambda i: (0, 0))],
        out_shape=[jax.ShapeDtypeStruct((seq, D), F32), jax.ShapeDtypeStruct((seq, 5 * D), BF16),
                   jax.ShapeDtypeStruct((seq, D), BF16), jax.ShapeDtypeStruct((seq, D), BF16),
                   jax.ShapeDtypeStruct((3, D, seq), BF16), jax.ShapeDtypeStruct((3, seq, D), BF16),
                   jax.ShapeDtypeStruct((8, D), F32)],
        compiler_params=_cparams(dimension_semantics=("arbitrary",)),
    )(p_ext, p_ext, p_ext, p_ext, p_ext, p_ext, p_ext, p_ext, p_ext, p_ext, p_ext,
      rn, rstd, x, target, w_a, w_b, w_out, vecs)


def _conv_backward(p_ext, dconv, dpa, dq, dk, dv, vecs, seq):
    n_tiles = seq // TM
    hb = TM // HALO
    qk = N_HEADS * DK

    def body(h_ref, cg_ref, dc_ref, dcp_ref, dcn_ref, dpa_ref, dq_ref, dk_ref, dv_ref, vec_ref, dp_ref, part_ref):
        i = pl.program_id(0)
        dp_ref[:, CB_QK * D:CB_QK * D + qk] = dq_ref[...]
        dp_ref[:, CB_QK * D + qk:CB_V * D] = dk_ref[...]
        dp_ref[:, CB_V * D:CB_ZB * D] = dv_ref[...]

        @pl.when(i == 0)
        def _():
            part_ref[...] = jnp.zeros((8, D), F32)

        @pl.when(i == n_tiles)
        def _():
            dp_ref[:, 0:CB_QK * D] = jnp.zeros((TM, CB_QK * D), BF16)
            dp_ref[:, CB_ZB * D:PW] = jnp.zeros((TM, PW - CB_ZB * D), BF16)

        @pl.when(i < n_tiles)
        def _():
            f = lambda ref: ref[...].astype(F32)
            h, cg, dc = f(h_ref), f(cg_ref), f(dc_ref)
            w0, w1, w2 = vec_ref[1:2, :], vec_ref[2:3, :], vec_ref[3:4, :]
            row = lax.broadcasted_iota(jnp.int32, (TM, 1), 0)
            dc_prev = jnp.where(i == 0, 0.0, f(dcp_ref)[HALO - 1:HALO, :])
            dc_next = jnp.where(i == n_tiles - 1, 0.0, f(dcn_ref)[0:1, :])
            dc_up = jnp.where(row == 0, dc_prev, pltpu.roll(dc, 1, 0))
            dc_dn = jnp.where(row == TM - 1, dc_next, pltpu.roll(dc, TM - 1, 0))
            du = w0 * dc_dn + w1 * dc + w2 * dc_up
            u = cg * h
            dp_ref[:, CB_H * D:(CB_H + 1) * D] = (du * cg).astype(BF16)
            dp_ref[:, CB_CG * D:(CB_CG + 1) * D] = (du * h).astype(BF16)
            dp_ref[:, CB_BG * D:(CB_BG + 1) * D] = dpa_ref[:, 0:D]
            dp_ref[:, CB_ZA * D:(CB_ZA + 1) * D] = dpa_ref[:, D:2 * D]
            dp_ref[:, CB_ZB * D:(CB_GB + 1) * D] = dpa_ref[:, 2 * D:5 * D]
            part_ref[0:1, :] += jnp.sum(u * dc_dn, axis=0, keepdims=True)
            part_ref[1:2, :] += jnp.sum(u * dc, axis=0, keepdims=True)
            part_ref[2:3, :] += jnp.sum(u * dc_up, axis=0, keepdims=True)
            part_ref[3:4, :] += jnp.sum(dc, axis=0, keepdims=True)

    last = n_tiles - 1
    col = lambda cb_: pl.BlockSpec((TM, D), lambda i, cb_=cb_: (jnp.minimum(i, last), cb_))
    lat = lambda w: pl.BlockSpec((TM, w), lambda i: (jnp.minimum(i, last), 0))
    ext = lambda w: pl.BlockSpec((TM, w), lambda i: (i, 0))
    return pl.pallas_call(
        body, name="conv_backward", grid=(n_tiles + 1,),
        in_specs=[col(CB_H), col(CB_CG), lat(D),
                  pl.BlockSpec((HALO, D), lambda i: (jnp.clip(i * hb - 1, 0, n_tiles * hb - 1), 0)),
                  pl.BlockSpec((HALO, D), lambda i: (jnp.minimum((i + 1) * hb, n_tiles * hb - 1), 0)),
                  lat(5 * D), ext(qk), ext(qk), ext(D),
                  pl.BlockSpec(vecs.shape, lambda i: (0, 0))],
        out_specs=[ext(PW), pl.BlockSpec((8, D), lambda i: (0, 0))],
        out_shape=[jax.ShapeDtypeStruct((seq + TM, PW), BF16), jax.ShapeDtypeStruct((8, D), F32)],
        compiler_params=_cparams(dimension_semantics=("arbitrary",)),
    )(p_ext, p_ext, dconv, dconv, dconv, dpa, dq, dk, dv, vecs)


def _input_backward(dp, w_in, x, ctx, dx1, modx, modc, norm_w):
    seq = x.shape[0]
    lext = seq + ctx.shape[0]
    n_x = seq // TM

    def body(dp_ref, w_hbm, x_ref, ctx_ref, dx1_ref, mx_ref, mc_ref, nw_ref, gx_ref, part_ref, w_ref, w_sems):
        i = pl.program_id(0)
        _load_w_in(i, w_hbm, w_ref, w_sems)
        is_ctx = i >= n_x
        dxm = _dot_nt(dp_ref[...], w_ref[...])
        x = jnp.where(is_ctx, ctx_ref[...], x_ref[...])
        r = lax.rsqrt(jnp.mean(x * x, axis=-1, keepdims=True) + EPS)
        xh = x * r
        nw = nw_ref[...]
        sc = jnp.where(is_ctx, mc_ref[1:2, :], mx_ref[1:2, :])
        dxn = dxm * (1.0 + sc)
        dxh = dxn * nw
        dx = r * (dxh - xh * jnp.mean(dxh * xh, axis=-1, keepdims=True))

        @pl.when(jnp.logical_not(is_ctx))
        def _():
            gx_ref[...] = dx1_ref[...] + dx

        @pl.when(i == 0)
        def _():
            part_ref[...] = jnp.zeros((8, D), F32)

        fx = jnp.where(is_ctx, 0.0, 1.0)
        d_shift = jnp.sum(dxm, axis=0, keepdims=True)
        d_scale = jnp.sum(dxm * (xh * nw), axis=0, keepdims=True)
        part_ref[0:1, :] += fx * d_shift
        part_ref[1:2, :] += fx * d_scale
        part_ref[2:3, :] += jnp.sum(dxn * xh, axis=0, keepdims=True)
        part_ref[3:4, :] += (1.0 - fx) * d_shift
        part_ref[4:5, :] += (1.0 - fx) * d_scale

    lat = lambda w: pl.BlockSpec((TM, w), lambda i: (jnp.minimum(i, n_x - 1), 0))
    ext = lambda w: pl.BlockSpec((TM, w), lambda i: (i, 0))
    full = lambda a: pl.BlockSpec(a.shape, lambda i: (0,) * a.ndim)
    return pl.pallas_call(
        body, name="input_backward", grid=(lext // TM,),
        in_specs=[ext(PW), pl.BlockSpec(memory_space=pl.ANY), lat(D), full(ctx), lat(D), full(modx), full(modc),
                  full(norm_w)],
        out_specs=[lat(D), pl.BlockSpec((8, D), lambda i: (0, 0))],
        out_shape=[jax.ShapeDtypeStruct((seq, D), F32), jax.ShapeDtypeStruct((8, D), F32)],
        scratch_shapes=[pltpu.VMEM((D, PW), BF16), pltpu.SemaphoreType.DMA((N_DEV,))],
        compiler_params=_cparams(dimension_semantics=("arbitrary",)),
    )(dp, w_in, x, ctx, dx1, modx, modc, norm_w)


def _weight_grad_in(xm_t, dp, owners, name):
    lext = xm_t.shape[1]
    tk = lext // 2
    n_k = lext // tk

    def body(own_ref, a_ref, b_ref, o_ref, acc):
        k = pl.program_id(1)

        @pl.when(k == 0)
        def _():
            acc[...] = jnp.zeros_like(acc)

        acc[...] += _dot(a_ref[...], b_ref[...])

        @pl.when(k == n_k - 1)
        def _():
            o_ref[0] = acc[...]

    grid_spec = pltpu.PrefetchScalarGridSpec(
        num_scalar_prefetch=1, grid=(4, n_k),
        in_specs=[pl.BlockSpec((D, tk), lambda j, k, own: (0, k)),
                  pl.BlockSpec((tk, WSH), lambda j, k, own: (k, own[j]))],
        out_specs=pl.BlockSpec((1, D, WSH), lambda j, k, own: (j, 0, 0)),
        scratch_shapes=[pltpu.VMEM((D, WSH), F32)])
    return pl.pallas_call(
        body, name=name, grid_spec=grid_spec,
        out_shape=jax.ShapeDtypeStruct((4, D, WSH), F32),
        compiler_params=_cparams(dimension_semantics=("arbitrary", "arbitrary")),
    )(owners, xm_t, dp)


def _weight_grad_square(a_t, b):
    seq = a_t.shape[2]
    tk = seq // 2

    def body(a_ref, b_ref, o_ref, acc):
        k = pl.program_id(1)

        @pl.when(k == 0)
        def _():
            acc[...] = jnp.zeros_like(acc)

        acc[...] += _dot(a_ref[0], b_ref[0])

        @pl.when(k == pl.num_programs(1) - 1)
        def _():
            o_ref[:, 0] = acc[...].reshape(N_DEV, RSH, D)

    return pl.pallas_call(
        body, name="weight_grad_square", grid=(3, seq // tk),
        in_specs=[pl.BlockSpec((1, D, tk), lambda t, k: (t, 0, k)), pl.BlockSpec((1, tk, D), lambda t, k: (t, k, 0))],
        out_specs=pl.BlockSpec((N_DEV, 1, RSH, D), lambda t, k: (0, t, 0, 0)),
        out_shape=jax.ShapeDtypeStruct((N_DEV, 3, RSH, D), F32),
        scratch_shapes=[pltpu.VMEM((D, D), F32)],
        compiler_params=_cparams(dimension_semantics=("arbitrary", "arbitrary")),
    )(a_t, b)


def _rope_tables(seq, ctx_len):
    rows = seq // GRID_W
    row = jnp.repeat(jnp.arange(rows), GRID_W).astype(F32)
    col = jnp.tile(jnp.arange(GRID_W), rows).astype(F32)
    nf = DK // 4
    inv = ROPE_BASE ** (-jnp.arange(nf, dtype=F32) / nf)
    ang = jnp.concatenate([row[:, None] * inv, col[:, None] * inv], axis=-1)
    cos, sin = jnp.cos(ang), jnp.sin(ang)
    cos_t = jnp.tile(cos, (1, 4))
    sin_t = jnp.tile(jnp.concatenate([-sin, sin], axis=-1), (1, 2))
    cos_t = jnp.concatenate([cos_t, jnp.ones((ctx_len, LANES), F32)], axis=0)
    sin_t = jnp.concatenate([sin_t, jnp.zeros((ctx_len, LANES), F32)], axis=0)
    return cos_t, sin_t


def _tie(value, token):
    return value + token[0, 0].astype(value.dtype)


def _pad_lanes(a):
    return jnp.pad(a, ((0, 0), (0, D - a.shape[1])))


def _rows(total, *parts):
    width = max(a.shape[1] for _, a in parts)
    out = None
    for row, a in parts:
        padded = jnp.pad(a, ((row, total - row - a.shape[0]), (0, width - a.shape[1])))
        out = padded if out is None else out + padded
    return out


def kernel(x, c, ctx, c_ctx, norm_w, ada_w, ada_b, w_in, conv_w, conv_b, decay_logit, gn_w, w_a, w_b, w_out, final_norm_w, loss_target, m_c_ctx, m_norm_w, m_ada_w, m_ada_b, m_w_in, m_conv_w, m_conv_b, m_decay_logit, m_gn_w, m_w_a, m_w_b, m_w_out, m_final_norm_w, v_c_ctx, v_norm_w, v_ada_w, v_ada_b, v_w_in, v_conv_w, v_conv_b, v_decay_logit, v_gn_w, v_w_a, v_w_b, v_w_out, v_final_norm_w):
    xi, yi, ci = _coords()
    me = 4 * xi + 2 * yi + ci
    chip = 2 * xi + yi
    seq, ctx_len = x.shape[1], ctx.shape[1]
    assert seq % TM == 0 and seq % RET_C == 0 and ctx_len == TM and seq % GRID_W == 0
    csh = D // N_DEV

    blk = jnp.pad(c, ((0, 7), (0, 0))) + jnp.pad(conv_w[0], ((1, 4), (0, D - csh)))
    got = _all_gather_small(blk, "gather_cond")
    conv_w_all = got[:, 1:4, 0:csh].transpose(1, 0, 2).reshape(3, D)
    c16 = _rows(16, (0, got[:, 0, :]), (N_DEV, c_ctx[None]))
    ada_b_sh = lax.dynamic_slice(ada_b, (0, me * ADA_SH), (1, ADA_SH))
    mod_sh, act16, lg = _modulation(c16, ada_w[0], ada_b_sh, decay_logit[0])
    mod_all = _all_gather_small(mod_sh, "gather_mod").transpose(1, 0, 2).reshape(16, 3 * D)
    modx = lax.dynamic_slice(mod_all, (me, 0), (1, 3 * D)).reshape(3, D)
    modc = mod_all[8].reshape(3, D)

    w_in_all, gathered = _all_gather_weights([w_in[0].astype(BF16)])
    sq_shards = [_tie(w[0].astype(BF16), gathered) for w in (w_a, w_b, w_out)]
    ag_sq = _Exchange("gather_square", sq_shards, [jax.ShapeDtypeStruct((N_DEV, RSH, D), BF16)] * 3,
                      3 * (N_DEV - 1), _gather_plan)

    x2, tgt = x[0], loss_target[0]
    cos_t, sin_t = _rope_tables(seq, ctx_len)
    modx = _tie(modx, ag_sq.token)
    p_ext, xm_t = _in_projection(x2, ctx[0], modx, modc, norm_w, w_in_all, cos_t, sin_t)
    rn, rstd = _retention_forward(p_ext, lg, seq, ctx_len)
    _, sq_lands = ag_sq.wait(rstd)
    w_a_all, w_b_all, w_out_all = (
        lax.dynamic_update_slice(land, shard[None], (me, 0, 0)).reshape(D, D) for land, shard in zip(sq_lands, sq_shards))
    vecs = _rows(8, (0, modx[2:3]), (1, conv_w_all), (4, conv_b), (5, gn_w), (6, final_norm_w[None]))
    dx1, dpa, dconv, dret, op_at, op_b, part_m = _merge(p_ext, rn, rstd, x2, tgt, w_a_all, w_b_all, w_out_all, vecs, seq)

    j4 = jnp.arange(4, dtype=jnp.int32)
    owners = (2 * jnp.bitwise_xor(chip, j4) + ci).astype(jnp.int32)
    owners_sib = (2 * jnp.bitwise_xor(chip, j4) + (1 - ci)).astype(jnp.int32)
    gw_sq = _weight_grad_square(op_at, op_b).reshape(N_DEV, 3 * RSH, D)
    rs_sq_pair = _Exchange("rs_square_pair", [gw_sq], [jax.ShapeDtypeStruct((4, 3 * RSH, D), F32)], 4,
                           _pair_plan(lambda j, chip_, c_: 2 * jnp.bitwise_xor(chip_, j) + (1 - c_)))
    dq, dk, dv, dlg = _retention_backward(p_ext, dret, _tie(lg, rs_sq_pair.token), cos_t, sin_t, seq, ctx_len)
    (gw_sq,), (r1_sq,) = rs_sq_pair.wait(dlg)
    own_sq, send_sq = _pair_sum(gw_sq, r1_sq, owners, "pair_sum_square")
    rs_sq_chips = _Exchange("rs_square_chips", [send_sq], [jax.ShapeDtypeStruct((3, 3 * RSH, D), BF16)], 3, _chips_plan)
    dp, part_c = _conv_backward(p_ext, dconv, dpa, dq, dk, dv, _tie(vecs, rs_sq_chips.token), seq)
    gw_sib = _weight_grad_in(xm_t, dp, owners_sib, "weight_grad_in_sibling")
    rs_in_pair = _Exchange("rs_in_pair", [gw_sib], [jax.ShapeDtypeStruct((4, D, WSH), F32)], 4,
                           _pair_plan(lambda j, chip_, c_: j))
    gw_own = _weight_grad_in(xm_t, dp, _tie(owners, rs_in_pair.token), "weight_grad_in_own")
    _, (r1_in,) = rs_in_pair.wait(gw_own)
    own_in, send_in = _pair_sum(gw_own, r1_in, j4, "pair_sum_in")
    rs_in_chips = _Exchange("rs_in_chips", [send_in], [jax.ShapeDtypeStruct((3, D, WSH), BF16)], 3, _chips_plan)
    grad_x, part_i = _input_backward(dp, w_in_all, x2, ctx[0], dx1, _tie(modx, rs_in_chips.token), modc, norm_w)

    dl = dlg[:, 0, 0:4]
    dlg_row = jnp.pad(dl[:, 0::2].reshape(1, N_HEADS), ((0, 0), (0, D - N_HEADS))) + jnp.pad(
        dl[:, 1::2].reshape(1, N_HEADS), ((0, 0), (N_HEADS, D - 2 * N_HEADS)))
    partials = _rows(16, (0, part_m[0:1]), (1, part_i[2:3]), (2, part_c[3:4]), (3, part_m[2:3]), (4, part_c[0:3]),
                     (7, part_i[0:2]), (9, part_m[1:2]), (10, part_i[3:5]), (13, part_m[3:4]), (14, dlg_row))
    got = _all_gather_small(partials, "gather_partials")
    tot = _sum_devices(got, "sum_partials")
    dmodc = tot[10:13].reshape(1, 3 * D)
    dmod16 = _rows(16, (0, got[:, 7:10, :].reshape(N_DEV, 3 * D)), (N_DEV, dmodc))
    dmod16 = lax.dynamic_slice(dmod16, (0, me * ADA_SH), (16, ADA_SH))
    dmodc8 = _rows(8, (0, dmod16[8:9]))
    g_ada_w, d_ada_w, nm_ada_w, nv_ada_w, cctx_part = _ada_backward(
        act16.T, dmod16, dmodc8, ada_w[0], m_ada_w[0], v_ada_w[0])
    dact_ctx = _sum_devices(_all_gather_small(cctx_part, "gather_cctx"), "sum_cctx")

    view = {"c_ctx": (1, D), "final_norm_w": (1, D), "decay_logit": (1, 2 * N_HEADS), "conv_w": (3, csh)}
    given = {"c_ctx": (c_ctx, m_c_ctx, v_c_ctx), "norm_w": (norm_w, m_norm_w, v_norm_w),
             "conv_b": (conv_b, m_conv_b, v_conv_b), "gn_w": (gn_w, m_gn_w, v_gn_w),
             "final_norm_w": (final_norm_w, m_final_norm_w, v_final_norm_w), "ada_b": (ada_b, m_ada_b, v_ada_b),
             "decay_logit": (decay_logit, m_decay_logit, v_decay_logit), "conv_w": (conv_w, m_conv_w, v_conv_w)}
    small_in = [tuple(a.reshape(view.get(name, a.shape)) for a in given[name]) for name in SMALL_PARAMS]
    conv_w_grad = lax.dynamic_slice(tot, (4, me * csh), (3, csh))
    small_out = _adam_small(tot, dact_ctx, conv_w_grad, small_in)
    results = {name: [o.reshape(given[name][0].shape) for o in outs_]
               for name, outs_ in zip(SMALL_PARAMS, small_out)}

    _, (r2_sq,) = rs_sq_chips.wait(small_out[0][0])
    _, (r2_in,) = rs_in_chips.wait(small_out[0][0])
    results["w_in"] = [o[None] for o in _adam_sharded(w_in[0], m_w_in[0], v_w_in[0], own_in, r2_in, "adam_w_in")]
    square = _adam_square([(w_a[0], m_w_a[0], v_w_a[0]), (w_b[0], m_w_b[0], v_w_b[0]),
                           (w_out[0], m_w_out[0], v_w_out[0])], own_sq, r2_sq)
    for name, outs_ in zip(("w_a", "w_b", "w_out"), square):
        results[name] = [o[None] for o in outs_]
    results["ada_w"] = [o[None] for o in (g_ada_w, d_ada_w, nm_ada_w, nv_ada_w)]

    order = ("c_ctx", "norm_w", "ada_w", "ada_b", "w_in", "conv_w", "conv_b", "decay_logit", "gn_w",
             "w_a", "w_b", "w_out", "final_norm_w")
    outs = [results[name][kind] for kind in range(4) for name in order]
    return (tot[13, 0], grad_x[None], *outs)
```

```python
import functools
import math

import jax
import jax.numpy as jnp
from jax import lax
from jax.experimental import pallas as pl
from jax.experimental.pallas import tpu as pltpu

F32 = jnp.float32
BF16 = jnp.bfloat16
MESH = pl.DeviceIdType.MESH
_HBM_SPEC = pl.BlockSpec(memory_space=pltpu.HBM)
_SEM_SPEC = pl.BlockSpec(memory_space=pltpu.SEMAPHORE)
_EFFECT = pltpu.SideEffectType.DATAFLOW_SIDE_EFFECTING

N_DEV = 8
D = 1024
N_HEADS = 8
DK = 64
DV = 128
GRID_W = 64
ROPE_BASE = 10000.0
EPS = 1e-6
PW = 9 * D
WSH = PW // N_DEV
RSH = D // N_DEV
ADA_SH = 3 * D // N_DEV
TM = 256
RET_C = 256
HALO = 16
LANES = 128
VMEM_LIMIT = 60 * 1024 * 1024

ADAM_LR = 0.001
ADAM_B1 = 0.9
ADAM_B2 = 0.999
ADAM_EPS = 1e-08
ADAM_WD = 0.01
ADAM_STEP = 10

CB_H, CB_BG, CB_CG, CB_ZA, CB_QK, CB_V, CB_ZB, CB_GA, CB_GB = range(9)


def _cparams(**kw):
    return pltpu.CompilerParams(vmem_limit_bytes=VMEM_LIMIT, **kw)


def _dot(a, b):
    return jnp.dot(a, b, preferred_element_type=F32)


def _dot_nt(a, b):
    return lax.dot_general(a, b, (((1,), (1,)), ((), ())), preferred_element_type=F32)


def _dot_tn(a, b):
    return lax.dot_general(a, b, (((0,), (0,)), ((), ())), preferred_element_type=F32)


def _sigmoid(z):
    return 0.5 * jnp.tanh(0.5 * z) + 0.5


def _row_tile(rows):
    return TM if rows % TM == 0 else rows


def _coords():
    return lax.axis_index("x"), lax.axis_index("y"), lax.axis_index("c")


def _flip(v, bit):
    return 1 - v if bit else v


def _all_gather_small(blk, name):
    rows, cols = blk.shape

    def body(x_ref, out_ref, send_sems, recv_sems, local_sem):
        x, y, c = _coords()
        me = 4 * x + 2 * y + c
        mine = pltpu.make_async_copy(x_ref, out_ref.at[me], local_sem)
        mine.start()

        def copy(k, slot):
            peer = (_flip(x, k & 4), _flip(y, k & 2), _flip(c, k & 1))
            return pltpu.make_async_remote_copy(
                src_ref=x_ref, dst_ref=out_ref.at[slot], send_sem=send_sems.at[k - 1],
                recv_sem=recv_sems.at[k - 1], device_id=peer, device_id_type=MESH)

        for k in range(1, N_DEV):
            copy(k, me).start()
        for k in range(1, N_DEV):
            copy(k, jnp.bitwise_xor(me, k)).wait_recv()
        for k in range(1, N_DEV):
            copy(k, me).wait_send()
        mine.wait()

    return pl.pallas_call(
        body, name=name,
        out_shape=jax.ShapeDtypeStruct((N_DEV, rows, cols), blk.dtype),
        in_specs=[pl.BlockSpec(memory_space=pltpu.VMEM)],
        out_specs=pl.BlockSpec(memory_space=pltpu.VMEM),
        scratch_shapes=[pltpu.SemaphoreType.DMA((N_DEV - 1,)), pltpu.SemaphoreType.DMA((N_DEV - 1,)),
                        pltpu.SemaphoreType.DMA],
    )(blk)


def _all_gather_weights(shards):
    n = len(shards)

    def body(*refs):
        srcs, outs, done = refs[:n], refs[n:2 * n], refs[2 * n]
        send_sems, recv_sems, local_sems = refs[2 * n + 1:]
        x, y, c = _coords()
        me = 4 * x + 2 * y + c
        sib = (x, y, 1 - c)
        chips = [(1 - x, y), (x, 1 - y), (1 - x, 1 - y)]

        def slot(px, py, pc):
            return 4 * px + 2 * py + pc

        def copy(a, k, block_slot, to, own=False):
            return pltpu.make_async_remote_copy(
                src_ref=srcs[a] if own else outs[a].at[block_slot], dst_ref=outs[a].at[block_slot],
                send_sem=send_sems.at[7 * a + k], recv_sem=recv_sems.at[7 * a + k],
                device_id=to, device_id_type=MESH)

        locals_ = [pltpu.make_async_copy(srcs[a], outs[a].at[me], local_sems.at[a]) for a in range(n)]
        for cp in locals_:
            cp.start()
        sent = []
        for a in range(n):
            sent.append(copy(a, 0, me, sib, own=True))
            for j, chip in enumerate(chips):
                sent.append(copy(a, 1 + j, me, (*chip, c), own=True))
        for cp in sent:
            cp.start()
        for j, chip in enumerate(chips):
            for a in range(n):
                copy(a, 1 + j, slot(*chip, c), sib).wait_recv()
                fwd = copy(a, 4 + j, slot(*chip, c), sib)
                fwd.start()
                sent.append(fwd)
        for a in range(n):
            copy(a, 0, slot(x, y, 1 - c), sib).wait_recv()
            for j, chip in enumerate(chips):
                copy(a, 4 + j, slot(*chip, 1 - c), sib).wait_recv()
        for cp in sent:
            cp.wait_send()
        for cp in locals_:
            cp.wait()
        done[...] = jnp.zeros((8, LANES), F32)

    anyspec = pl.BlockSpec(memory_space=pl.ANY)
    return pl.pallas_call(
        body, name="all_gather_weights",
        out_shape=[jax.ShapeDtypeStruct((N_DEV,) + s.shape, s.dtype) for s in shards]
        + [jax.ShapeDtypeStruct((8, LANES), F32)],
        in_specs=[anyspec] * n, out_specs=[anyspec] * n + [pl.BlockSpec(memory_space=pltpu.VMEM)],
        scratch_shapes=[pltpu.SemaphoreType.DMA((7 * n,)), pltpu.SemaphoreType.DMA((7 * n,)),
                        pltpu.SemaphoreType.DMA((n,))],
    )(*shards)


class _Exchange:
    def __init__(self, name, srcs, land_shapes, n_copies, plan):
        self.name, self.plan, self.n_copies = name, plan, n_copies
        self.n_src, self.n_land = len(srcs), len(land_shapes)
        hbm = lambda a: pltpu.HBM(a.shape, a.dtype)
        n = self.n_src + self.n_land
        lands = [pltpu.with_memory_space_constraint(lax.empty(s.shape, s.dtype), pltpu.HBM) for s in land_shapes]
        srcs = [pltpu.with_memory_space_constraint(s, pltpu.HBM) for s in srcs]

        def body(*refs):
            send_sems, recv_sems = refs[n], refs[n + 1]
            for cp in self._copies(refs, send_sems, recv_sems):
                cp.start()
            refs[-1][...] = jnp.zeros((8, LANES), F32)

        outs = pl.pallas_call(
            body, name=name + "_start",
            out_shape=(pltpu.SemaphoreType.DMA((n_copies,)), pltpu.SemaphoreType.DMA((n_copies,)),
                       *[hbm(a) for a in srcs], *[hbm(a) for a in lands], jax.ShapeDtypeStruct((8, LANES), F32)),
            in_specs=[_HBM_SPEC] * n,
            out_specs=(_SEM_SPEC, _SEM_SPEC, *[_HBM_SPEC] * n, pl.BlockSpec(memory_space=pltpu.VMEM)),
            input_output_aliases={i: 2 + i for i in range(n)},
            compiler_params=pltpu.CompilerParams(has_side_effects=_EFFECT),
        )(*srcs, *lands)
        self.send_sems, self.recv_sems = outs[0], outs[1]
        self.buffers = list(outs[2:2 + n])
        self.token = outs[-1]

    def _copies(self, refs, send_sems, recv_sems):
        src_refs, land_refs = refs[:self.n_src], refs[self.n_src:self.n_src + self.n_land]
        return [pltpu.make_async_remote_copy(src_ref=s, dst_ref=d, send_sem=send_sems.at[k], recv_sem=recv_sems.at[k],
                                             device_id=dev, device_id_type=MESH)
                for k, (s, d, dev) in enumerate(self.plan(src_refs, land_refs))]

    def wait(self, after):
        n = self.n_src + self.n_land

        def body(*refs):
            for cp in self._copies(refs, refs[n], refs[n + 1]):
                cp.wait_send()
                cp.wait_recv()

        outs = pl.pallas_call(
            body, name=self.name + "_wait",
            out_shape=tuple(pltpu.HBM(b.shape, b.dtype) for b in self.buffers),
            in_specs=[_HBM_SPEC] * n + [_SEM_SPEC, _SEM_SPEC, pl.BlockSpec(memory_space=pl.ANY)],
            out_specs=tuple([_HBM_SPEC] * n),
            input_output_aliases={i: i for i in range(n)},
            compiler_params=pltpu.CompilerParams(has_side_effects=_EFFECT),
        )(*self.buffers, self.send_sems, self.recv_sems, after)
        return list(outs[:self.n_src]), list(outs[self.n_src:])


def _pair_plan(src_index):
    def plan(srcs, lands):
        x, y, c = _coords()
        chip = 2 * x + y
        return [(srcs[0].at[src_index(j, chip, c)], lands[0].at[j], (x, y, 1 - c)) for j in range(4)]
    return plan


def _chips_plan(srcs, lands):
    x, y, c = _coords()
    return [(srcs[0].at[j - 1], lands[0].at[j - 1], (_flip(x, j & 2), _flip(y, j & 1), c)) for j in range(1, 4)]


def _gather_plan(srcs, lands):
    x, y, c = _coords()
    me = 4 * x + 2 * y + c
    return [(srcs[a], lands[a].at[me], (_flip(x, k & 4), _flip(y, k & 2), _flip(c, k & 1)))
            for a in range(len(srcs)) for k in range(1, N_DEV)]


def _pair_sum(grad, recv, owners, name):
    _, rows, cols = grad.shape
    tr = _row_tile(rows)

    def body(own_ref, g_ref, r_ref, mine_ref, send_ref):
        j = pl.program_id(1)
        total = g_ref[...] + r_ref[...]

        @pl.when(j == 0)
        def _():
            mine_ref[...] = total

        @pl.when(j > 0)
        def _():
            send_ref[...] = total.astype(BF16)

    grid_spec = pltpu.PrefetchScalarGridSpec(
        num_scalar_prefetch=1, grid=(rows // tr, 4),
        in_specs=[pl.BlockSpec((1, tr, cols), lambda i, j, own: (own[j], i, 0)),
                  pl.BlockSpec((1, tr, cols), lambda i, j, own: (j, i, 0))],
        out_specs=[pl.BlockSpec((1, tr, cols), lambda i, j, own: (0, i, 0)),
                   pl.BlockSpec((1, tr, cols), lambda i, j, own: (jnp.maximum(j - 1, 0), i, 0))])
    return pl.pallas_call(
        body, name=name, grid_spec=grid_spec,
        out_shape=[jax.ShapeDtypeStruct((1, rows, cols), F32), jax.ShapeDtypeStruct((3, rows, cols), BF16)],
        compiler_params=_cparams(dimension_semantics=("arbitrary", "arbitrary")),
    )(owners, grad, recv)


def _modulation(c16, ada_w_sh, ada_b_sh, decay_logit):
    def body(c_ref, w_ref, b_ref, dl_ref, mod_ref, act_ref, lg_ref):
        cv = c_ref[...]
        act = cv * _sigmoid(cv)
        act_ref[...] = act
        mod_ref[...] = jnp.dot(act, w_ref[...], preferred_element_type=F32,
                               precision=lax.Precision.HIGHEST) + b_ref[...]
        z = dl_ref[...]
        lg_ref[...] = jnp.minimum(z, 0.0) - jnp.log(1.0 + jnp.exp(-jnp.abs(z)))

    return pl.pallas_call(
        body, name="modulation",
        out_shape=[jax.ShapeDtypeStruct((16, ADA_SH), F32), jax.ShapeDtypeStruct((16, D), F32),
                   jax.ShapeDtypeStruct(decay_logit.shape, F32)],
        compiler_params=_cparams(),
    )(c16, ada_w_sh, ada_b_sh, decay_logit)


def _adam_update(w, g, m, v):
    m2 = ADAM_B1 * m + (1.0 - ADAM_B1) * g
    v2 = ADAM_B2 * v + (1.0 - ADAM_B2) * (g * g)
    m_hat = m2 / (1.0 - ADAM_B1 ** ADAM_STEP)
    v_hat = v2 / (1.0 - ADAM_B2 ** ADAM_STEP)
    delta = -ADAM_LR * (m_hat / (jnp.sqrt(v_hat) + ADAM_EPS) + ADAM_WD * w)
    return delta, m2, v2


def _adam_sharded(w, m, v, own, recv, name):
    rows, cols = w.shape
    tr = _row_tile(rows)

    def body(w_ref, m_ref, v_ref, g0, g1, g2, g3, g_ref, d_ref, m_out, v_out):
        g = ((g0[0] + g1[0].astype(F32)) + g2[0].astype(F32)) + g3[0].astype(F32)
        delta, m2, v2 = _adam_update(w_ref[...], g, m_ref[...], v_ref[...])
        g_ref[...] = g
        d_ref[...] = delta
        m_out[...] = m2
        v_out[...] = v2

    flat = pl.BlockSpec((tr, cols), lambda i: (i, 0))
    part = lambda j: pl.BlockSpec((1, tr, cols), lambda i, j=j: (j, i, 0))
    return pl.pallas_call(
        body, name=name, grid=(rows // tr,),
        in_specs=[flat, flat, flat, part(0), part(0), part(1), part(2)],
        out_specs=[flat] * 4,
        out_shape=[jax.ShapeDtypeStruct((rows, cols), F32)] * 4,
        compiler_params=_cparams(dimension_semantics=("arbitrary",)),
    )(w, m, v, own, recv, recv, recv)


def _ada_backward(act_t, dmod16, dmodc8, ada_w_sh, m, v):
    def body(at_ref, dm_ref, dc_ref, w_ref, m_ref, v_ref, g_ref, d_ref, m_out, v_out, pc_ref):
        g = jnp.dot(at_ref[...], dm_ref[...], preferred_element_type=F32, precision=lax.Precision.HIGHEST)
        w = w_ref[...]
        delta, m2, v2 = _adam_update(w, g, m_ref[...], v_ref[...])
        g_ref[...] = g
        d_ref[...] = delta
        m_out[...] = m2
        v_out[...] = v2
        pc_ref[...] = lax.dot_general(dc_ref[...], w, (((1,), (1,)), ((), ())), preferred_element_type=F32,
                                      precision=lax.Precision.HIGHEST)

    return pl.pallas_call(
        body, name="ada_backward",
        out_shape=[jax.ShapeDtypeStruct((D, ADA_SH), F32)] * 4 + [jax.ShapeDtypeStruct((8, D), F32)],
        compiler_params=_cparams(),
    )(act_t, dmod16, dmodc8, ada_w_sh, m, v)


def _sum_devices(gathered, name):
    _, rows, cols = gathered.shape

    def body(g_ref, o_ref):
        acc = g_ref[0]
        for d in range(1, N_DEV):
            acc = acc + g_ref[d]
        o_ref[...] = acc

    return pl.pallas_call(body, name=name, out_shape=jax.ShapeDtypeStruct((rows, cols), F32),
                          compiler_params=_cparams())(gathered)


SMALL_PARAMS = ("c_ctx", "norm_w", "conv_b", "gn_w", "final_norm_w", "ada_b", "decay_logit", "conv_w")


def _adam_small(tot, dact_ctx, conv_w_grad, params):
    n = len(SMALL_PARAMS)

    def body(tot_ref, dact_ref, cwg_ref, *refs):
        ins, outs = refs[:3 * n], refs[3 * n:]
        nh = 2 * N_HEADS
        raw = {
            "c_ctx": dact_ref[0:1, :],
            "norm_w": tot_ref[1:2, :], "conv_b": tot_ref[2:3, :], "gn_w": tot_ref[3:4, :],
            "final_norm_w": tot_ref[0:1, :],
            "ada_b": jnp.concatenate([tot_ref[7 + r:8 + r, :] + tot_ref[10 + r:11 + r, :] for r in range(3)], axis=1),
            "decay_logit": tot_ref[14:15, 0:nh],
            "conv_w": cwg_ref[...],
        }
        for k, name in enumerate(SMALL_PARAMS):
            w, m, v = ins[3 * k][...], ins[3 * k + 1][...], ins[3 * k + 2][...]
            g = raw[name]
            if name == "c_ctx":
                s = _sigmoid(w)
                g = g * (s * (1.0 + w * (1.0 - s)))
            elif name == "decay_logit":
                g = g * (1.0 - _sigmoid(w))
            delta, m2, v2 = _adam_update(w, g, m, v)
            for ref, val in zip(outs[4 * k:4 * k + 4], (g, delta, m2, v2)):
                ref[...] = val

    flat = [a for trio in params for a in trio]
    out_shape = [jax.ShapeDtypeStruct(trio[0].shape, F32) for trio in params for _ in range(4)]
    outs = pl.pallas_call(body, name="adam_small", out_shape=out_shape, compiler_params=_cparams())(
        tot, dact_ctx, conv_w_grad, *flat)
    return [outs[4 * k:4 * k + 4] for k in range(n)]


def _adam_square(params, own, recv):
    def body(own_ref, recv_ref, *refs):
        ins, outs = refs[:9], refs[9:]
        for k in range(3):
            rows = slice(k * RSH, (k + 1) * RSH)
            g = own_ref[0, rows, :]
            for j in range(3):
                g = g + recv_ref[j, rows, :].astype(F32)
            delta, m2, v2 = _adam_update(ins[3 * k][...], g, ins[3 * k + 1][...], ins[3 * k + 2][...])
            for ref, val in zip(outs[4 * k:4 * k + 4], (g, delta, m2, v2)):
                ref[...] = val

    flat = [a for trio in params for a in trio]
    outs = pl.pallas_call(body, name="adam_square", out_shape=[jax.ShapeDtypeStruct((RSH, D), F32)] * 12,
                          compiler_params=_cparams())(own, recv, *flat)
    return [outs[4 * k:4 * k + 4] for k in range(3)]


def _rope(t, cos, sin):
    lane = lax.broadcasted_iota(jnp.int32, (1, LANES), 1)
    first_half = jnp.bitwise_and(lane, DK // 2) == 0
    partner = jnp.where(first_half, pltpu.roll(t, LANES - DK // 2, 1), pltpu.roll(t, DK // 2, 1))
    return t * cos + partner * sin


def _load_w_in(step, w_hbm, w_vmem, sems):
    @pl.when(step == 0)
    def _():
        copies = [pltpu.make_async_copy(w_hbm.at[d], w_vmem.at[:, d * WSH:(d + 1) * WSH], sems.at[d])
                  for d in range(N_DEV)]
        for cp in copies:
            cp.start()
        for cp in copies:
            cp.wait()


def _in_projection(x, ctx, modx, modc, norm_w, w_in, cos_t, sin_t):
    n_x_tiles = x.shape[0] // TM
    lext = x.shape[0] + ctx.shape[0]
    qk = N_HEADS * DK

    def body(x_ref, ctx_ref, mx_ref, mc_ref, nw_ref, w_hbm, cos_ref, sin_ref, p_ref, xmt_ref, w_ref, w_sems):
        i = pl.program_id(0)
        _load_w_in(i, w_hbm, w_ref, w_sems)
        is_ctx = i >= n_x_tiles
        x = jnp.where(is_ctx, ctx_ref[...], x_ref[...])
        r = lax.rsqrt(jnp.mean(x * x, axis=-1, keepdims=True) + EPS)
        sh = jnp.where(is_ctx, mc_ref[0:1, :], mx_ref[0:1, :])
        sc = jnp.where(is_ctx, mc_ref[1:2, :], mx_ref[1:2, :])
        xm = (x * r * nw_ref[...]) * (1.0 + sc) + sh
        xmb = xm.astype(BF16)
        xmt_ref[...] = xm.T.astype(BF16)
        cos, sin = cos_ref[...], sin_ref[...]
        for cb in range(PW // D):
            acc = _dot(xmb, w_ref[:, cb * D:(cb + 1) * D])
            if cb != CB_QK:
                p_ref[:, cb * D:(cb + 1) * D] = acc.astype(BF16)
                continue
            for col in range(0, D, LANES):
                piece = acc[:, col:col + LANES]
                if col >= qk:
                    piece = piece * (DK ** -0.5)
                p_ref[:, cb * D + col:cb * D + col + LANES] = _rope(piece, cos, sin).astype(BF16)

    row = lambda w: pl.BlockSpec((TM, w), lambda i: (i, 0))
    full = lambda a: pl.BlockSpec(a.shape, lambda i: (0,) * a.ndim)
    return pl.pallas_call(
        body, name="in_projection", grid=(lext // TM,),
        in_specs=[pl.BlockSpec((TM, D), lambda i: (jnp.minimum(i, n_x_tiles - 1), 0)), full(ctx),
                  full(modx), full(modc), full(norm_w), pl.BlockSpec(memory_space=pl.ANY), row(LANES), row(LANES)],
        out_specs=[row(PW), pl.BlockSpec((D, TM), lambda i: (0, i))],
        out_shape=[jax.ShapeDtypeStruct((lext, PW), BF16), jax.ShapeDtypeStruct((D, lext), BF16)],
        scratch_shapes=[pltpu.VMEM((D, PW), BF16), pltpu.SemaphoreType.DMA((N_DEV,))],
        compiler_params=_cparams(dimension_semantics=("arbitrary",)),
    )(x, ctx, modx, modc, norm_w, w_in, cos_t, sin_t)


def _decay_tables(lgf, lgb, n):
    i = lax.broadcasted_iota(jnp.int32, (n, 1), 0).astype(F32)
    return dict(i=i, k_f=jnp.exp(lgf * (n - 1.0 - i)), k_b=jnp.exp(lgb * i),
                q_f=jnp.exp(lgf * (i + 1.0)), q_b=jnp.exp(lgb * (n - i)))


def _decay_matrix(lgf, lgb, n, transposed=False):
    ii = lax.broadcasted_iota(jnp.int32, (n, n), 0)
    jj = lax.broadcasted_iota(jnp.int32, (n, n), 1)
    diff = (jj - ii if transposed else ii - jj).astype(F32)
    low = jnp.exp(lgf * jnp.maximum(diff, 0.0))
    up = jnp.exp(lgb * jnp.maximum(-diff, 0.0))
    return jnp.where(diff > 0, low, jnp.where(diff < 0, up, 2.0)), diff


def _cat_lanes(a, b):
    return jnp.concatenate([a.astype(BF16), b.astype(BF16)], axis=1)


def _retention_forward(p_ext, lg, seq, ctx_len):
    lext = seq + ctx_len
    n_chunks = seq // RET_C
    C = RET_C

    def body(lg_ref, q_ref, k_ref, v_ref, rn_ref, rstd_ref, kv_scr, sf_scr, sb_scr):
        pair = pl.program_id(0)
        lane = lax.broadcasted_iota(jnp.int32, (1, LANES), 1)
        heads = range(2)
        hmask = [(lane // DK == hh).astype(F32) for hh in heads]
        lgf = [lg_ref[0, 2 * pair + hh] for hh in heads]
        lgb = [lg_ref[1, 2 * pair + hh] for hh in heads]
        vs = [slice(hh * DV, (hh + 1) * DV) for hh in heads]
        t = [_decay_tables(lgf[hh], lgb[hh], C) for hh in heads]
        kc_all = k_ref[seq:lext, :].astype(F32)
        s0 = []
        for hh in heads:
            tc = _decay_tables(lgf[hh], lgb[hh], ctx_len)
            kc = kc_all * hmask[hh]
            s0.append(_dot_tn(_cat_lanes(kc * tc["k_f"], kc * tc["k_b"]), v_ref[seq:lext, vs[hh]]))

        def increments(c, carry):
            rows = pl.ds(pl.multiple_of(c * C, C), C)
            k_all = k_ref[rows, :].astype(F32)
            for hh in heads:
                k = k_all * hmask[hh]
                kv_scr[hh, c] = _dot_tn(_cat_lanes(k * t[hh]["k_f"], k * t[hh]["k_b"]), v_ref[rows, vs[hh]])
            return carry

        lax.fori_loop(0, n_chunks, increments, 0, unroll=2)
        gf_c = [jnp.exp(lgf[hh] * C) for hh in heads]
        gb_c = [jnp.exp(lgb[hh] * C) for hh in heads]

        def scan_f(c, s):
            for hh in heads:
                sf_scr[hh, c] = s[hh]
            return tuple(gf_c[hh] * s[hh] + kv_scr[hh, c, 0:LANES, :] for hh in heads)

        def scan_b(n, s):
            c = n_chunks - 1 - n
            for hh in heads:
                sb_scr[hh, c] = s[hh]
            return tuple(gb_c[hh] * s[hh] + kv_scr[hh, c, LANES:2 * LANES, :] for hh in heads)

        lax.fori_loop(0, n_chunks, scan_f, tuple(s0[hh][0:LANES] for hh in heads))
        lax.fori_loop(0, n_chunks, scan_b, tuple(s0[hh][LANES:2 * LANES] for hh in heads))
        dmat = [_decay_matrix(lgf[hh], lgb[hh], C)[0] for hh in heads]

        def outputs(c, carry):
            rows = pl.ds(pl.multiple_of(c * C, C), C)
            q_all = q_ref[rows, :].astype(F32)
            k = k_ref[rows, :]
            for hh in heads:
                q = q_all * hmask[hh]
                v = v_ref[rows, vs[hh]]
                s = _dot_nt(q.astype(BF16), k)
                inner = _dot((s * dmat[hh]).astype(BF16), v)
                states = jnp.concatenate([sf_scr[hh, c], sb_scr[hh, c]], axis=0).astype(BF16)
                ret = inner + _dot(_cat_lanes(q * t[hh]["q_f"], q * t[hh]["q_b"]), states)
                mu = jnp.mean(ret, axis=-1, keepdims=True)
                cen = ret - mu
                rstd = lax.rsqrt(jnp.mean(cen * cen, axis=-1, keepdims=True) + EPS)
                rn_ref[rows, vs[hh]] = cen * rstd
                rstd_ref[rows, vs[hh]] = jnp.broadcast_to(rstd, (C, DV))
            return carry

        lax.fori_loop(0, n_chunks, outputs, 0, unroll=2)

    qk0 = CB_QK * D // LANES
    return pl.pallas_call(
        body, name="retention_forward", grid=(N_HEADS // 2,),
        in_specs=[pl.BlockSpec(memory_space=pltpu.SMEM),
                  pl.BlockSpec((lext, LANES), lambda g: (0, qk0 + g)),
                  pl.BlockSpec((lext, LANES), lambda g: (0, qk0 + N_HEADS // 2 + g)),
                  pl.BlockSpec((lext, 2 * DV), lambda g: (0, CB_V * D // (2 * DV) + g))],
        out_specs=[pl.BlockSpec((seq, 2 * DV), lambda g: (0, g))] * 2,
        out_shape=[jax.ShapeDtypeStruct((seq, D), F32)] * 2,
        scratch_shapes=[pltpu.VMEM((2, n_chunks, 2 * LANES, DV), F32), pltpu.VMEM((2, n_chunks, LANES, DV), F32),
                        pltpu.VMEM((2, n_chunks, LANES, DV), F32)],
        compiler_params=_cparams(dimension_semantics=("arbitrary",)),
    )(lg, p_ext, p_ext, p_ext)


def _retention_backward(p_ext, dret, lg, cos_t, sin_t, seq, ctx_len):
    lext = seq + ctx_len
    n_chunks = seq // RET_C
    C = RET_C

    def body(lg_ref, q_ref, k_ref, v_ref, do_ref, cos_ref, sin_ref, dq_ref, dk_ref, dv_ref, dlg_ref,
             kv_scr, g_scr, sf_scr, sb_scr, gfn_scr, gbp_scr):
        pair = pl.program_id(0)
        lane = lax.broadcasted_iota(jnp.int32, (1, LANES), 1)
        heads = range(2)
        hmask = [(lane // DK == hh).astype(F32) for hh in heads]
        lgf = [lg_ref[0, 2 * pair + hh] for hh in heads]
        lgb = [lg_ref[1, 2 * pair + hh] for hh in heads]
        vs = [slice(hh * DV, (hh + 1) * DV) for hh in heads]
        t = [_decay_tables(lgf[hh], lgb[hh], C) for hh in heads]
        tc = [_decay_tables(lgf[hh], lgb[hh], ctx_len) for hh in heads]
        kc_all = k_ref[seq:lext, :].astype(F32)
        kc = [kc_all * hmask[hh] for hh in heads]
        vc = [v_ref[seq:lext, vs[hh]] for hh in heads]
        kc_cat = [_cat_lanes(kc[hh] * tc[hh]["k_f"], kc[hh] * tc[hh]["k_b"]) for hh in heads]
        s0 = [_dot_tn(kc_cat[hh], vc[hh]) for hh in heads]

        def increments(c, carry):
            rows = pl.ds(pl.multiple_of(c * C, C), C)
            k_all = k_ref[rows, :].astype(F32)
            q_all = q_ref[rows, :].astype(F32)
            for hh in heads:
                k, q = k_all * hmask[hh], q_all * hmask[hh]
                kv_scr[hh, c] = _dot_tn(_cat_lanes(k * t[hh]["k_f"], k * t[hh]["k_b"]), v_ref[rows, vs[hh]])
                g_scr[hh, c] = _dot_tn(_cat_lanes(q * t[hh]["q_f"], q * t[hh]["q_b"]), do_ref[rows, vs[hh]])
            return carry

        lax.fori_loop(0, n_chunks, increments, 0, unroll=2)
        gf_c = [jnp.exp(lgf[hh] * C) for hh in heads]
        gb_c = [jnp.exp(lgb[hh] * C) for hh in heads]

        def scan_f(c, s):
            for hh in heads:
                sf_scr[hh, c] = s[hh]
            return tuple(gf_c[hh] * s[hh] + kv_scr[hh, c, 0:LANES, :] for hh in heads)

        def scan_b(n, s):
            c = n_chunks - 1 - n
            for hh in heads:
                sb_scr[hh, c] = s[hh]
            return tuple(gb_c[hh] * s[hh] + kv_scr[hh, c, LANES:2 * LANES, :] for hh in heads)

        def scan_gf(n, carry):
            c = n_chunks - 1 - n
            for hh in heads:
                gfn_scr[hh, c] = carry[hh]
            return tuple(g_scr[hh, c, 0:LANES, :] + gf_c[hh] * carry[hh] for hh in heads)

        def scan_gb(c, carry):
            for hh in heads:
                gbp_scr[hh, c] = carry[hh]
            return tuple(g_scr[hh, c, LANES:2 * LANES, :] + gb_c[hh] * carry[hh] for hh in heads)

        lax.fori_loop(0, n_chunks, scan_f, tuple(s0[hh][0:LANES] for hh in heads))
        lax.fori_loop(0, n_chunks, scan_b, tuple(s0[hh][LANES:2 * LANES] for hh in heads))
        zero_state = jnp.zeros((LANES, DV), F32)
        gf_first = lax.fori_loop(0, n_chunks, scan_gf, (zero_state, zero_state))
        gb_last = lax.fori_loop(0, n_chunks, scan_gb, (zero_state, zero_state))

        dmat, w_f, w_b = [], [], []
        for hh in heads:
            dm, diff = _decay_matrix(lgf[hh], lgb[hh], C)
            dmat.append(dm)
            w_f.append(jnp.where(diff > 0, diff * dm, 0.0))
            w_b.append(jnp.where(diff < 0, -diff * dm, 0.0))

        def total(a):
            rows_, width = a.shape
            part = jnp.sum(a.reshape(rows_ // 8, 8, width), axis=0)
            return part[:, 0:LANES] + part[:, LANES:2 * LANES] if width == 2 * LANES else part

        def chunk_head(hh, c, rows, q_all, k_all, dlgf, dlgb):
            th = t[hh]
            qm = q_all * hmask[hh]
            km = k_all * hmask[hh]
            qb, kb = qm.astype(BF16), km.astype(BF16)
            v = v_ref[rows, vs[hh]]
            do = do_ref[rows, vs[hh]]
            s = _dot_nt(qb, kb)
            dsd = _dot_nt(do, v)
            ds = (dsd * dmat[hh]).astype(BF16)
            dq_in = _dot(ds, kb)
            dk_in = _dot_tn(ds, qb)
            dv_in = _dot_tn((s * dmat[hh]).astype(BF16), do)
            prod = s * dsd
            dlgf = dlgf + total(prod * w_f[hh])
            dlgb = dlgb + total(prod * w_b[hh])
            sf, sb = sf_scr[hh, c], sb_scr[hh, c]
            states = jnp.concatenate([sf, sb], axis=0).astype(BF16)
            dqc = _dot_nt(do, states)
            dqf = dqc[:, 0:LANES] * th["q_f"]
            dqb = dqc[:, LANES:2 * LANES] * th["q_b"]
            dq = (dq_in + dqf + dqb) * hmask[hh]
            dlgf = dlgf + total((th["i"] + 1.0) * (qm * dqf))
            dlgb = dlgb + total((C - th["i"]) * (qm * dqb))
            gfn, gbp = gfn_scr[hh, c], gbp_scr[hh, c]
            gstates = jnp.concatenate([gfn, gbp], axis=0).astype(BF16)
            dkc = _dot_nt(v, gstates)
            dkf = dkc[:, 0:LANES] * th["k_f"]
            dkb = dkc[:, LANES:2 * LANES] * th["k_b"]
            dk = dk_in + dkf + dkb
            dlgf = dlgf + total((C - 1.0 - th["i"]) * (km * dkf)) + C * gf_c[hh] * total(gfn * sf)
            dlgb = dlgb + total(th["i"] * (km * dkb)) + C * gb_c[hh] * total(gbp * sb)
            dv = dv_in + _dot(_cat_lanes(km * th["k_f"], km * th["k_b"]), gstates)
            dv_ref[rows, vs[hh]] = dv.astype(BF16)
            return dq, dk, dlgf, dlgb

        def chunk(c, carry):
            rows = pl.ds(pl.multiple_of(c * C, C), C)
            q_all = q_ref[rows, :].astype(F32)
            k_all = k_ref[rows, :].astype(F32)
            dq0, dk0, f0, b0 = chunk_head(0, c, rows, q_all, k_all, carry[0], carry[1])
            dq1, dk1, f1, b1 = chunk_head(1, c, rows, q_all, k_all, carry[2], carry[3])
            cos, sin = cos_ref[rows, :], sin_ref[rows, :]
            dq_ref[rows, :] = _rope(dq0 + dq1, cos, -sin).astype(BF16)
            dk_ref[rows, :] = (_rope(dk0 + dk1, cos, -sin) * (DK ** -0.5)).astype(BF16)
            return f0, b0, f1, b1

        zero = jnp.zeros((8, LANES), F32)
        sums = lax.fori_loop(0, n_chunks, chunk, (zero, zero, zero, zero), unroll=2)

        dlg = []
        dk_ctx = jnp.zeros((ctx_len, LANES), F32)
        for hh in heads:
            g0 = jnp.concatenate([gf_first[hh], gb_last[hh]], axis=0).astype(BF16)
            dkcc = _dot_nt(vc[hh], g0)
            dkcf = dkcc[:, 0:LANES] * tc[hh]["k_f"]
            dkcb = dkcc[:, LANES:2 * LANES] * tc[hh]["k_b"]
            dlgf = sums[2 * hh] + total((ctx_len - 1.0 - tc[hh]["i"]) * (kc[hh] * dkcf))
            dlgb = sums[2 * hh + 1] + total(tc[hh]["i"] * (kc[hh] * dkcb))
            dk_ctx = dk_ctx + (dkcf + dkcb) * (DK ** -0.5)
            dv_ref[seq:lext, vs[hh]] = _dot(kc_cat[hh], g0).astype(BF16)
            dlg += [jnp.sum(jnp.sum(a, axis=1, keepdims=True), axis=0, keepdims=True) for a in (dlgf, dlgb)]
        dk_ref[seq:lext, :] = dk_ctx.astype(BF16)
        dq_ref[seq:lext, :] = jnp.zeros((ctx_len, LANES), BF16)

        lane8 = lax.broadcasted_iota(jnp.int32, (8, LANES), 1)
        out = jnp.zeros((8, LANES), F32)
        for n, val in enumerate(dlg):
            out = jnp.where(lane8 == n, val, out)
        dlg_ref[0] = out

    qk0 = CB_QK * D // LANES
    q_spec = pl.BlockSpec((lext, LANES), lambda g: (0, qk0 + g))
    k_spec = pl.BlockSpec((lext, LANES), lambda g: (0, qk0 + N_HEADS // 2 + g))
    v_spec = pl.BlockSpec((lext, 2 * DV), lambda g: (0, CB_V * D // (2 * DV) + g))
    table = pl.BlockSpec((lext, LANES), lambda g: (0, 0))
    state = pltpu.VMEM((2, n_chunks, LANES, DV), F32)
    return pl.pallas_call(
        body, name="retention_backward", grid=(N_HEADS // 2,),
        in_specs=[pl.BlockSpec(memory_space=pltpu.SMEM), q_spec, k_spec, v_spec,
                  pl.BlockSpec((seq, 2 * DV), lambda g: (0, g)), table, table],
        out_specs=[pl.BlockSpec((lext, LANES), lambda g: (0, g)), pl.BlockSpec((lext, LANES), lambda g: (0, g)),
                   pl.BlockSpec((lext, 2 * DV), lambda g: (0, g)), pl.BlockSpec((1, 8, LANES), lambda g: (g, 0, 0))],
        out_shape=[jax.ShapeDtypeStruct((lext, N_HEADS * DK), BF16), jax.ShapeDtypeStruct((lext, N_HEADS * DK), BF16),
                   jax.ShapeDtypeStruct((lext, D), BF16), jax.ShapeDtypeStruct((N_HEADS // 2, 8, LANES), F32)],
        scratch_shapes=[pltpu.VMEM((2, n_chunks, 2 * LANES, DV), F32), pltpu.VMEM((2, n_chunks, 2 * LANES, DV), F32),
                        state, state, state, state],
        compiler_params=_cparams(dimension_semantics=("arbitrary",)),
    )(lg, p_ext, p_ext, p_ext, dret, cos_t, sin_t)


def _merge(p_ext, rn, rstd, x, target, w_a, w_b, w_out, vecs, seq):
    n_tiles = seq // TM
    hb = TM // HALO

    def body(h_ref, bg_ref, cg_ref, za_ref, zb_ref, ga_ref, gb_ref, hp_ref, hn_ref, cp_ref, cn_ref,
             rn_ref, rstd_ref, x_ref, t_ref, wa_ref, wb_ref, wo_ref, vec_ref,
             dx1_ref, dpa_ref, dconv_ref, dret_ref, at_ref, b_ref, part_ref):
        i = pl.program_id(0)
        f = lambda ref: ref[...].astype(F32)
        h, bg, cg, za, zb, ga, gb = f(h_ref), f(bg_ref), f(cg_ref), f(za_ref), f(zb_ref), f(ga_ref), f(gb_ref)
        gx, w0, w1, w2 = vec_ref[0:1, :], vec_ref[1:2, :], vec_ref[2:3, :], vec_ref[3:4, :]
        cb, gnw, fw = vec_ref[4:5, :], vec_ref[5:6, :], vec_ref[6:7, :]
        u = cg * h
        row = lax.broadcasted_iota(jnp.int32, (TM, 1), 0)
        u_prev = (f(cp_ref) * f(hp_ref))[HALO - 1:HALO, :]
        u_next = (f(cn_ref) * f(hn_ref))[0:1, :]
        u_prev = jnp.where(i == 0, 0.0, u_prev)
        u_next = jnp.where(i == n_tiles - 1, 0.0, u_next)
        u_up = jnp.where(row == 0, u_prev, pltpu.roll(u, 1, 0))
        u_dn = jnp.where(row == TM - 1, u_next, pltpu.roll(u, TM - 1, 0))
        conv = w0 * u_up + w1 * u + w2 * u_dn + cb
        sza = _sigmoid(za)
        silu_za = za * sza
        a_act = silu_za * bg * conv
        rn = rn_ref[...]
        szb = _sigmoid(zb)
        silu_zb = zb * szb
        rg = rn * gnw
        b_act = silu_zb * rg
        y_a = _dot(a_act.astype(BF16), wa_ref[...])
        y_b = _dot(b_act.astype(BF16), wb_ref[...])
        sga, sgb = _sigmoid(ga), _sigmoid(gb)
        mix = sga * y_a + sgb * y_b
        y = _dot(mix.astype(BF16), wo_ref[...])
        x1 = x_ref[...] + gx * y
        r1 = lax.rsqrt(jnp.mean(x1 * x1, axis=-1, keepdims=True) + EPS)
        xh1 = x1 * r1
        err = xh1 * fw - t_ref[...]
        loss = jnp.sum(jnp.sum(err * err, axis=1, keepdims=True), axis=0, keepdims=True) * (0.5 / D)
        dout = err * (1.0 / D)
        dxh = dout * fw
        dx1 = r1 * (dxh - xh1 * jnp.mean(dxh * xh1, axis=-1, keepdims=True))
        dx1_ref[...] = dx1
        dy = (dx1 * gx).astype(BF16)
        dmix = _dot_nt(dy, wo_ref[...])
        dya = (dmix * sga).astype(BF16)
        dyb = (dmix * sgb).astype(BF16)
        da = _dot_nt(dya, wa_ref[...])
        db = _dot_nt(dyb, wb_ref[...])
        dpa_ref[:, 0:D] = (da * silu_za * conv).astype(BF16)
        dpa_ref[:, D:2 * D] = (da * bg * conv * (sza * (1.0 + za * (1.0 - sza)))).astype(BF16)
        dpa_ref[:, 2 * D:3 * D] = (db * rg * (szb * (1.0 + zb * (1.0 - szb)))).astype(BF16)
        dpa_ref[:, 3 * D:4 * D] = (dmix * y_a * sga * (1.0 - sga)).astype(BF16)
        dpa_ref[:, 4 * D:5 * D] = (dmix * y_b * sgb * (1.0 - sgb)).astype(BF16)
        dconv_ref[...] = (da * silu_za * bg).astype(BF16)
        drn_n = db * silu_zb
        drn = drn_n * gnw
        rstd = rstd_ref[...]
        for hd in range(N_HEADS):
            sl = slice(hd * DV, (hd + 1) * DV)
            dh_, rh = drn[:, sl], rn[:, sl]
            m1 = jnp.mean(dh_, axis=-1, keepdims=True)
            m2 = jnp.mean(dh_ * rh, axis=-1, keepdims=True)
            dret_ref[:, sl] = (rstd[:, sl] * (dh_ - m1 - rh * m2)).astype(BF16)
        at_ref[0] = a_act.T.astype(BF16)
        at_ref[1] = b_act.T.astype(BF16)
        at_ref[2] = mix.T.astype(BF16)
        b_ref[0] = dya
        b_ref[1] = dyb
        b_ref[2] = dy

        @pl.when(i == 0)
        def _():
            part_ref[...] = jnp.zeros((8, D), F32)

        part_ref[0:1, :] += jnp.sum(dout * xh1, axis=0, keepdims=True)
        part_ref[1:2, :] += jnp.sum(dx1 * y, axis=0, keepdims=True)
        part_ref[2:3, :] += jnp.sum(drn_n * rn, axis=0, keepdims=True)
        part_ref[3:4, :] += jnp.broadcast_to(loss, (1, D))

    col = lambda cb_: pl.BlockSpec((TM, D), lambda i, cb_=cb_: (i, cb_))
    prev = lambda cb_: pl.BlockSpec((HALO, D), lambda i, cb_=cb_: (jnp.maximum(i * hb - 1, 0), cb_))
    nxt = lambda cb_: pl.BlockSpec((HALO, D), lambda i, cb_=cb_: (jnp.minimum((i + 1) * hb, n_tiles * hb - 1), cb_))
    tile = pl.BlockSpec((TM, D), lambda i: (i, 0))
    full = lambda a: pl.BlockSpec(a.shape, lambda i: (0,) * a.ndim)
    return pl.pallas_call(
        body, name="merge", grid=(n_tiles,),
        in_specs=[col(CB_H), col(CB_BG), col(CB_CG), col(CB_ZA), col(CB_ZB), col(CB_GA), col(CB_GB),
                  prev(CB_H), nxt(CB_H), prev(CB_CG), nxt(CB_CG),
                  tile, tile, tile, tile, full(w_a), full(w_b), full(w_out), full(vecs)],
        out_specs=[tile, pl.BlockSpec((TM, 5 * D), lambda i: (i, 0)), tile, tile,
                   pl.BlockSpec((3, D, TM), lambda i: (0, 0, i)), pl.BlockSpec((3, TM, D), lambda i: (0, i, 0)),
                   pl.BlockSpec((8, D), lambda i: (0, 0))],
        out_shape=[jax.ShapeDtypeStruct((seq, D), F32), jax.ShapeDtypeStruct((seq, 5 * D), BF16),
                   jax.ShapeDtypeStruct((seq, D), BF16), jax.ShapeDtypeStruct((seq, D), BF16),
                   jax.ShapeDtypeStruct((3, D, seq), BF16), jax.ShapeDtypeStruct((3, seq, D), BF16),
                   jax.ShapeDtypeStruct((8, D), F32)],
        compiler_params=_cparams(dimension_semantics=("arbitrary",)),
    )(p_ext, p_ext, p_ext, p_ext, p_ext, p_ext, p_ext, p_ext, p_ext, p_ext, p_ext,
      rn, rstd, x, target, w_a, w_b, w_out, vecs)


def _conv_backward(p_ext, dconv, dpa, dq, dk, dv, vecs, seq):
    n_tiles = seq // TM
    hb = TM // HALO
    qk = N_HEADS * DK

    def body(h_ref, cg_ref, dc_ref, dcp_ref, dcn_ref, dpa_ref, dq_ref, dk_ref, dv_ref, vec_ref, dp_ref, part_ref):
        i = pl.program_id(0)
        dp_ref[:, CB_QK * D:CB_QK * D + qk] = dq_ref[...]
        dp_ref[:, CB_QK * D + qk:CB_V * D] = dk_ref[...]
        dp_ref[:, CB_V * D:CB_ZB * D] = dv_ref[...]

        @pl.when(i == 0)
        def _():
            part_ref[...] = jnp.zeros((8, D), F32)

        @pl.when(i == n_tiles)
        def _():
            dp_ref[:, 0:CB_QK * D] = jnp.zeros((TM, CB_QK * D), BF16)
            dp_ref[:, CB_ZB * D:PW] = jnp.zeros((TM, PW - CB_ZB * D), BF16)

        @pl.when(i < n_tiles)
        def _():
            f = lambda ref: ref[...].astype(F32)
            h, cg, dc = f(h_ref), f(cg_ref), f(dc_ref)
            w0, w1, w2 = vec_ref[1:2, :], vec_ref[2:3, :], vec_ref[3:4, :]
            row = lax.broadcasted_iota(jnp.int32, (TM, 1), 0)
            dc_prev = jnp.where(i == 0, 0.0, f(dcp_ref)[HALO - 1:HALO, :])
            dc_next = jnp.where(i == n_tiles - 1, 0.0, f(dcn_ref)[0:1, :])
            dc_up = jnp.where(row == 0, dc_prev, pltpu.roll(dc, 1, 0))
            dc_dn = jnp.where(row == TM - 1, dc_next, pltpu.roll(dc, TM - 1, 0))
            du = w0 * dc_dn + w1 * dc + w2 * dc_up
            u = cg * h
            dp_ref[:, CB_H * D:(CB_H + 1) * D] = (du * cg).astype(BF16)
            dp_ref[:, CB_CG * D:(CB_CG + 1) * D] = (du * h).astype(BF16)
            dp_ref[:, CB_BG * D:(CB_BG + 1) * D] = dpa_ref[:, 0:D]
            dp_ref[:, CB_ZA * D:(CB_ZA + 1) * D] = dpa_ref[:, D:2 * D]
            dp_ref[:, CB_ZB * D:(CB_GB + 1) * D] = dpa_ref[:, 2 * D:5 * D]
            part_ref[0:1, :] += jnp.sum(u * dc_dn, axis=0, keepdims=True)
            part_ref[1:2, :] += jnp.sum(u * dc, axis=0, keepdims=True)
            part_ref[2:3, :] += jnp.sum(u * dc_up, axis=0, keepdims=True)
            part_ref[3:4, :] += jnp.sum(dc, axis=0, keepdims=True)

    last = n_tiles - 1
    col = lambda cb_: pl.BlockSpec((TM, D), lambda i, cb_=cb_: (jnp.minimum(i, last), cb_))
    lat = lambda w: pl.BlockSpec((TM, w), lambda i: (jnp.minimum(i, last), 0))
    ext = lambda w: pl.BlockSpec((TM, w), lambda i: (i, 0))
    return pl.pallas_call(
        body, name="conv_backward", grid=(n_tiles + 1,),
        in_specs=[col(CB_H), col(CB_CG), lat(D),
                  pl.BlockSpec((HALO, D), lambda i: (jnp.clip(i * hb - 1, 0, n_tiles * hb - 1), 0)),
                  pl.BlockSpec((HALO, D), lambda i: (jnp.minimum((i + 1) * hb, n_tiles * hb - 1), 0)),
                  lat(5 * D), ext(qk), ext(qk), ext(D),
                  pl.BlockSpec(vecs.shape, lambda i: (0, 0))],
        out_specs=[ext(PW), pl.BlockSpec((8, D), lambda i: (0, 0))],
        out_shape=[jax.ShapeDtypeStruct((seq + TM, PW), BF16), jax.ShapeDtypeStruct((8, D), F32)],
        compiler_params=_cparams(dimension_semantics=("arbitrary",)),
    )(p_ext, p_ext, dconv, dconv, dconv, dpa, dq, dk, dv, vecs)


def _input_backward(dp, w_in, x, ctx, dx1, modx, modc, norm_w):
    seq = x.shape[0]
    lext = seq + ctx.shape[0]
    n_x = seq // TM

    def body(dp_ref, w_hbm, x_ref, ctx_ref, dx1_ref, mx_ref, mc_ref, nw_ref, gx_ref, part_ref, w_ref, w_sems):
        i = pl.program_id(0)
        _load_w_in(i, w_hbm, w_ref, w_sems)
        is_ctx = i >= n_x
        dxm = _dot_nt(dp_ref[...], w_ref[...])
        x = jnp.where(is_ctx, ctx_ref[...], x_ref[...])
        r = lax.rsqrt(jnp.mean(x * x, axis=-1, keepdims=True) + EPS)
        xh = x * r
        nw = nw_ref[...]
        sc = jnp.where(is_ctx, mc_ref[1:2, :], mx_ref[1:2, :])
        dxn = dxm * (1.0 + sc)
        dxh = dxn * nw
        dx = r * (dxh - xh * jnp.mean(dxh * xh, axis=-1, keepdims=True))

        @pl.when(jnp.logical_not(is_ctx))
        def _():
            gx_ref[...] = dx1_ref[...] + dx

        @pl.when(i == 0)
        def _():
            part_ref[...] = jnp.zeros((8, D), F32)

        fx = jnp.where(is_ctx, 0.0, 1.0)
        d_shift = jnp.sum(dxm, axis=0, keepdims=True)
        d_scale = jnp.sum(dxm * (xh * nw), axis=0, keepdims=True)
        part_ref[0:1, :] += fx * d_shift
        part_ref[1:2, :] += fx * d_scale
        part_ref[2:3, :] += jnp.sum(dxn * xh, axis=0, keepdims=True)
        part_ref[3:4, :] += (1.0 - fx) * d_shift
        part_ref[4:5, :] += (1.0 - fx) * d_scale

    lat = lambda w: pl.BlockSpec((TM, w), lambda i: (jnp.minimum(i, n_x - 1), 0))
    ext = lambda w: pl.BlockSpec((TM, w), lambda i: (i, 0))
    full = lambda a: pl.BlockSpec(a.shape, lambda i: (0,) * a.ndim)
    return pl.pallas_call(
        body, name="input_backward", grid=(lext // TM,),
        in_specs=[ext(PW), pl.BlockSpec(memory_space=pl.ANY), lat(D), full(ctx), lat(D), full(modx), full(modc),
                  full(norm_w)],
        out_specs=[lat(D), pl.BlockSpec((8, D), lambda i: (0, 0))],
        out_shape=[jax.ShapeDtypeStruct((seq, D), F32), jax.ShapeDtypeStruct((8, D), F32)],
        scratch_shapes=[pltpu.VMEM((D, PW), BF16), pltpu.SemaphoreType.DMA((N_DEV,))],
        compiler_params=_cparams(dimension_semantics=("arbitrary",)),
    )(dp, w_in, x, ctx, dx1, modx, modc, norm_w)


def _weight_grad_in(xm_t, dp, owners, name):
    lext = xm_t.shape[1]

    def body(own_ref, a_ref, b_ref, o_ref):
        o_ref[0] = _dot(a_ref[...], b_ref[...])

    grid_spec = pltpu.PrefetchScalarGridSpec(
        num_scalar_prefetch=1, grid=(4,),
        in_specs=[pl.BlockSpec((D, lext), lambda j, own: (0, 0)),
                  pl.BlockSpec((lext, WSH), lambda j, own: (0, own[j]))],
        out_specs=pl.BlockSpec((1, D, WSH), lambda j, own: (j, 0, 0)))
    return pl.pallas_call(
        body, name=name, grid_spec=grid_spec,
        out_shape=jax.ShapeDtypeStruct((4, D, WSH), F32),
        compiler_params=_cparams(dimension_semantics=("arbitrary",)),
    )(owners, xm_t, dp)


def _weight_grad_square(a_t, b):
    seq = a_t.shape[2]

    def body(a_ref, b_ref, o_ref):
        o_ref[:, 0] = _dot(a_ref[0], b_ref[0]).reshape(N_DEV, RSH, D)

    return pl.pallas_call(
        body, name="weight_grad_square", grid=(3,),
        in_specs=[pl.BlockSpec((1, D, seq), lambda t: (t, 0, 0)), pl.BlockSpec((1, seq, D), lambda t: (t, 0, 0))],
        out_specs=pl.BlockSpec((N_DEV, 1, RSH, D), lambda t: (0, t, 0, 0)),
        out_shape=jax.ShapeDtypeStruct((N_DEV, 3, RSH, D), F32),
        compiler_params=_cparams(dimension_semantics=("arbitrary",)),
    )(a_t, b)


def _rope_tables(seq, ctx_len):
    rows = seq // GRID_W
    row = jnp.repeat(jnp.arange(rows), GRID_W).astype(F32)
    col = jnp.tile(jnp.arange(GRID_W), rows).astype(F32)
    nf = DK // 4
    inv = ROPE_BASE ** (-jnp.arange(nf, dtype=F32) / nf)
    ang = jnp.concatenate([row[:, None] * inv, col[:, None] * inv], axis=-1)
    cos, sin = jnp.cos(ang), jnp.sin(ang)
    cos_t = jnp.tile(cos, (1, 4))
    sin_t = jnp.tile(jnp.concatenate([-sin, sin], axis=-1), (1, 2))
    cos_t = jnp.concatenate([cos_t, jnp.ones((ctx_len, LANES), F32)], axis=0)
    sin_t = jnp.concatenate([sin_t, jnp.zeros((ctx_len, LANES), F32)], axis=0)
    return cos_t, sin_t


def _tie(value, token):
    return value + token[0, 0].astype(value.dtype)


def _pad_lanes(a):
    return jnp.pad(a, ((0, 0), (0, D - a.shape[1])))


def _rows(total, *parts):
    width = max(a.shape[1] for _, a in parts)
    out = None
    for row, a in parts:
        padded = jnp.pad(a, ((row, total - row - a.shape[0]), (0, width - a.shape[1])))
        out = padded if out is None else out + padded
    return out


def kernel(x, c, ctx, c_ctx, norm_w, ada_w, ada_b, w_in, conv_w, conv_b, decay_logit, gn_w, w_a, w_b, w_out, final_norm_w, loss_target, m_c_ctx, m_norm_w, m_ada_w, m_ada_b, m_w_in, m_conv_w, m_conv_b, m_decay_logit, m_gn_w, m_w_a, m_w_b, m_w_out, m_final_norm_w, v_c_ctx, v_norm_w, v_ada_w, v_ada_b, v_w_in, v_conv_w, v_conv_b, v_decay_logit, v_gn_w, v_w_a, v_w_b, v_w_out, v_final_norm_w):
    xi, yi, ci = _coords()
    me = 4 * xi + 2 * yi + ci
    chip = 2 * xi + yi
    seq, ctx_len = x.shape[1], ctx.shape[1]
    assert seq % TM == 0 and seq % RET_C == 0 and ctx_len == TM and seq % GRID_W == 0
    csh = D // N_DEV

    blk = jnp.pad(c, ((0, 7), (0, 0))) + jnp.pad(conv_w[0], ((1, 4), (0, D - csh)))
    got = _all_gather_small(blk, "gather_cond")
    conv_w_all = got[:, 1:4, 0:csh].transpose(1, 0, 2).reshape(3, D)
    c16 = _rows(16, (0, got[:, 0, :]), (N_DEV, c_ctx[None]))
    ada_b_sh = lax.dynamic_slice(ada_b, (0, me * ADA_SH), (1, ADA_SH))
    mod_sh, act16, lg = _modulation(c16, ada_w[0], ada_b_sh, decay_logit[0])
    mod_all = _all_gather_small(mod_sh, "gather_mod").transpose(1, 0, 2).reshape(16, 3 * D)
    modx = lax.dynamic_slice(mod_all, (me, 0), (1, 3 * D)).reshape(3, D)
    modc = mod_all[8].reshape(3, D)

    w_in_all, gathered = _all_gather_weights([w_in[0].astype(BF16)])
    sq_shards = [_tie(w[0].astype(BF16), gathered) for w in (w_a, w_b, w_out)]
    ag_sq = _Exchange("gather_square", sq_shards, [jax.ShapeDtypeStruct((N_DEV, RSH, D), BF16)] * 3,
                      3 * (N_DEV - 1), _gather_plan)

    x2, tgt = x[0], loss_target[0]
    cos_t, sin_t = _rope_tables(seq, ctx_len)
    modx = _tie(modx, ag_sq.token)
    p_ext, xm_t = _in_projection(x2, ctx[0], modx, modc, norm_w, w_in_all, cos_t, sin_t)
    rn, rstd = _retention_forward(p_ext, lg, seq, ctx_len)
    _, sq_lands = ag_sq.wait(rstd)
    w_a_all, w_b_all, w_out_all = (
        lax.dynamic_update_slice(land, shard[None], (me, 0, 0)).reshape(D, D) for land, shard in zip(sq_lands, sq_shards))
    vecs = _rows(8, (0, modx[2:3]), (1, conv_w_all), (4, conv_b), (5, gn_w), (6, final_norm_w[None]))
    dx1, dpa, dconv, dret, op_at, op_b, part_m = _merge(p_ext, rn, rstd, x2, tgt, w_a_all, w_b_all, w_out_all, vecs, seq)

    j4 = jnp.arange(4, dtype=jnp.int32)
    owners = (2 * jnp.bitwise_xor(chip, j4) + ci).astype(jnp.int32)
    owners_sib = (2 * jnp.bitwise_xor(chip, j4) + (1 - ci)).astype(jnp.int32)
    gw_sq = _weight_grad_square(op_at, op_b).reshape(N_DEV, 3 * RSH, D)
    rs_sq_pair = _Exchange("rs_square_pair", [gw_sq], [jax.ShapeDtypeStruct((4, 3 * RSH, D), F32)], 4,
                           _pair_plan(lambda j, chip_, c_: 2 * jnp.bitwise_xor(chip_, j) + (1 - c_)))
    dq, dk, dv, dlg = _retention_backward(p_ext, dret, _tie(lg, rs_sq_pair.token), cos_t, sin_t, seq, ctx_len)
    (gw_sq,), (r1_sq,) = rs_sq_pair.wait(dlg)
    own_sq, send_sq = _pair_sum(gw_sq, r1_sq, owners, "pair_sum_square")
    rs_sq_chips = _Exchange("rs_square_chips", [send_sq], [jax.ShapeDtypeStruct((3, 3 * RSH, D), BF16)], 3, _chips_plan)
    dp, part_c = _conv_backward(p_ext, dconv, dpa, dq, dk, dv, _tie(vecs, rs_sq_chips.token), seq)
    gw_sib = _weight_grad_in(xm_t, dp, owners_sib, "weight_grad_in_sibling")
    rs_in_pair = _Exchange("rs_in_pair", [gw_sib], [jax.ShapeDtypeStruct((4, D, WSH), F32)], 4,
                           _pair_plan(lambda j, chip_, c_: j))
    gw_own = _weight_grad_in(xm_t, dp, _tie(owners, rs_in_pair.token), "weight_grad_in_own")
    _, (r1_in,) = rs_in_pair.wait(gw_own)
    own_in, send_in = _pair_sum(gw_own, r1_in, j4, "pair_sum_in")
    rs_in_chips = _Exchange("rs_in_chips", [send_in], [jax.ShapeDtypeStruct((3, D, WSH), BF16)], 3, _chips_plan)
    grad_x, part_i = _input_backward(dp, w_in_all, x2, ctx[0], dx1, _tie(modx, rs_in_chips.token), modc, norm_w)

    dl = dlg[:, 0, 0:4]
    dlg_row = jnp.pad(dl[:, 0::2].reshape(1, N_HEADS), ((0, 0), (0, D - N_HEADS))) + jnp.pad(
        dl[:, 1::2].reshape(1, N_HEADS), ((0, 0), (N_HEADS, D - 2 * N_HEADS)))
    partials = _rows(16, (0, part_m[0:1]), (1, part_i[2:3]), (2, part_c[3:4]), (3, part_m[2:3]), (4, part_c[0:3]),
                     (7, part_i[0:2]), (9, part_m[1:2]), (10, part_i[3:5]), (13, part_m[3:4]), (14, dlg_row))
    got = _all_gather_small(partials, "gather_partials")
    tot = _sum_devices(got, "sum_partials")
    dmodc = tot[10:13].reshape(1, 3 * D)
    dmod16 = _rows(16, (0, got[:, 7:10, :].reshape(N_DEV, 3 * D)), (N_DEV, dmodc))
    dmod16 = lax.dynamic_slice(dmod16, (0, me * ADA_SH), (16, ADA_SH))
    dmodc8 = _rows(8, (0, dmod16[8:9]))
    g_ada_w, d_ada_w, nm_ada_w, nv_ada_w, cctx_part = _ada_backward(
        act16.T, dmod16, dmodc8, ada_w[0], m_ada_w[0], v_ada_w[0])
    dact_ctx = _sum_devices(_all_gather_small(cctx_part, "gather_cctx"), "sum_cctx")

    view = {"c_ctx": (1, D), "final_norm_w": (1, D), "decay_logit": (1, 2 * N_HEADS), "conv_w": (3, csh)}
    given = {"c_ctx": (c_ctx, m_c_ctx, v_c_ctx), "norm_w": (norm_w, m_norm_w, v_norm_w),
             "conv_b": (conv_b, m_conv_b, v_conv_b), "gn_w": (gn_w, m_gn_w, v_gn_w),
             "final_norm_w": (final_norm_w, m_final_norm_w, v_final_norm_w), "ada_b": (ada_b, m_ada_b, v_ada_b),
             "decay_logit": (decay_logit, m_decay_logit, v_decay_logit), "conv_w": (conv_w, m_conv_w, v_conv_w)}
    small_in = [tuple(a.reshape(view.get(name, a.shape)) for a in given[name]) for name in SMALL_PARAMS]
    conv_w_grad = lax.dynamic_slice(tot, (4, me * csh), (3, csh))
    small_out = _adam_small(tot, dact_ctx, conv_w_grad, small_in)
    results = {name: [o.reshape(given[name][0].shape) for o in outs_]
               for name, outs_ in zip(SMALL_PARAMS, small_out)}

    _, (r2_sq,) = rs_sq_chips.wait(small_out[0][0])
    _, (r2_in,) = rs_in_chips.wait(small_out[0][0])
    results["w_in"] = [o[None] for o in _adam_sharded(w_in[0], m_w_in[0], v_w_in[0], own_in, r2_in, "adam_w_in")]
    square = _adam_square([(w_a[0], m_w_a[0], v_w_a[0]), (w_b[0], m_w_b[0], v_w_b[0]),
                           (w_out[0], m_w_out[0], v_w_out[0])], own_sq, r2_sq)
    for name, outs_ in zip(("w_a", "w_b", "w_out"), square):
        results[name] = [o[None] for o in outs_]
    results["ada_w"] = [o[None] for o in (g_ada_w, d_ada_w, nm_ada_w, nv_ada_w)]

    order = ("c_ctx", "norm_w", "ada_w", "ada_b", "w_in", "conv_w", "conv_b", "decay_logit", "gn_w",
             "w_a", "w_b", "w_out", "final_norm_w")
    outs = [results[name][kind] for kind in range(4) for name in order]
    return (tot[13, 0], grad_x[None], *outs)
```

```python
import functools
import math

import jax
import jax.numpy as jnp
from jax import lax
from jax.experimental import pallas as pl
from jax.experimental.pallas import tpu as pltpu

F32 = jnp.float32
BF16 = jnp.bfloat16
MESH = pl.DeviceIdType.MESH
_HBM_SPEC = pl.BlockSpec(memory_space=pltpu.HBM)
_SEM_SPEC = pl.BlockSpec(memory_space=pltpu.SEMAPHORE)
_EFFECT = pltpu.SideEffectType.DATAFLOW_SIDE_EFFECTING

N_DEV = 8
D = 1024
N_HEADS = 8
DK = 64
DV = 128
GRID_W = 64
ROPE_BASE = 10000.0
EPS = 1e-6
PW = 9 * D
WSH = PW // N_DEV
RSH = D // N_DEV
ADA_SH = 3 * D // N_DEV
TM = 256
RET_C = 256
HALO = 16
LANES = 128
VMEM_LIMIT = 60 * 1024 * 1024

ADAM_LR = 0.001
ADAM_B1 = 0.9
ADAM_B2 = 0.999
ADAM_EPS = 1e-08
ADAM_WD = 0.01
ADAM_STEP = 10

CB_H, CB_BG, CB_CG, CB_ZA, CB_QK, CB_V, CB_ZB, CB_GA, CB_GB = range(9)


def _cparams(**kw):
    return pltpu.CompilerParams(vmem_limit_bytes=VMEM_LIMIT, **kw)


def _dot(a, b):
    return jnp.dot(a, b, preferred_element_type=F32)


def _dot_nt(a, b):
    return lax.dot_general(a, b, (((1,), (1,)), ((), ())), preferred_element_type=F32)


def _dot_tn(a, b):
    return lax.dot_general(a, b, (((0,), (0,)), ((), ())), preferred_element_type=F32)


def _sigmoid(z):
    return 0.5 * jnp.tanh(0.5 * z) + 0.5


def _row_tile(rows):
    return TM if rows % TM == 0 else rows


def _coords():
    return lax.axis_index("x"), lax.axis_index("y"), lax.axis_index("c")


def _flip(v, bit):
    return 1 - v if bit else v


def _all_gather_small(blk, name):
    rows, cols = blk.shape

    def body(x_ref, out_ref, done, send_sems, recv_sems, local_sem):
        x, y, c = _coords()
        me = 4 * x + 2 * y + c
        mine = pltpu.make_async_copy(x_ref, out_ref.at[me], local_sem)
        mine.start()

        def copy(k, slot):
            peer = (_flip(x, k & 4), _flip(y, k & 2), _flip(c, k & 1))
            return pltpu.make_async_remote_copy(
                src_ref=x_ref, dst_ref=out_ref.at[slot], send_sem=send_sems.at[k - 1],
                recv_sem=recv_sems.at[k - 1], device_id=peer, device_id_type=MESH)

        for k in range(1, N_DEV):
            copy(k, me).start()
        for k in range(1, N_DEV):
            copy(k, jnp.bitwise_xor(me, k)).wait_recv()
        for k in range(1, N_DEV):
            copy(k, me).wait_send()
        mine.wait()
        done[...] = jnp.zeros((8, LANES), F32)

    vmem = pl.BlockSpec(memory_space=pltpu.VMEM)
    return pl.pallas_call(
        body, name=name,
        out_shape=[jax.ShapeDtypeStruct((N_DEV, rows, cols), blk.dtype), jax.ShapeDtypeStruct((8, LANES), F32)],
        in_specs=[vmem], out_specs=[vmem, vmem],
        scratch_shapes=[pltpu.SemaphoreType.DMA((N_DEV - 1,)), pltpu.SemaphoreType.DMA((N_DEV - 1,)),
                        pltpu.SemaphoreType.DMA],
    )(blk)


class _Exchange:
    def __init__(self, name, srcs, land_shapes, n_copies, plan):
        self.name, self.plan, self.n_copies = name, plan, n_copies
        self.n_src, self.n_land = len(srcs), len(land_shapes)
        hbm = lambda a: pltpu.HBM(a.shape, a.dtype)
        n = self.n_src + self.n_land
        lands = [pltpu.with_memory_space_constraint(lax.empty(s.shape, s.dtype), pltpu.HBM) for s in land_shapes]
        srcs = [pltpu.with_memory_space_constraint(s, pltpu.HBM) for s in srcs]

        def body(*refs):
            send_sems, recv_sems = refs[n], refs[n + 1]
            for cp in self._copies(refs, send_sems, recv_sems):
                cp.start()
            refs[-1][...] = jnp.zeros((8, LANES), F32)

        outs = pl.pallas_call(
            body, name=name + "_start",
            out_shape=(pltpu.SemaphoreType.DMA((n_copies,)), pltpu.SemaphoreType.DMA((n_copies,)),
                       *[hbm(a) for a in srcs], *[hbm(a) for a in lands], jax.ShapeDtypeStruct((8, LANES), F32)),
            in_specs=[_HBM_SPEC] * n,
            out_specs=(_SEM_SPEC, _SEM_SPEC, *[_HBM_SPEC] * n, pl.BlockSpec(memory_space=pltpu.VMEM)),
            input_output_aliases={i: 2 + i for i in range(n)},
            compiler_params=pltpu.CompilerParams(has_side_effects=_EFFECT),
        )(*srcs, *lands)
        self.send_sems, self.recv_sems = outs[0], outs[1]
        self.buffers = list(outs[2:2 + n])
        self.token = outs[-1]

    def _copies(self, refs, send_sems, recv_sems):
        src_refs, land_refs = refs[:self.n_src], refs[self.n_src:self.n_src + self.n_land]
        return [pltpu.make_async_remote_copy(src_ref=s, dst_ref=d, send_sem=send_sems.at[k], recv_sem=recv_sems.at[k],
                                             device_id=dev, device_id_type=MESH)
                for k, (s, d, dev) in enumerate(self.plan(src_refs, land_refs))]

    def wait(self, after):
        n = self.n_src + self.n_land

        def body(*refs):
            for cp in self._copies(refs, refs[n], refs[n + 1]):
                cp.wait_send()
                cp.wait_recv()

        outs = pl.pallas_call(
            body, name=self.name + "_wait",
            out_shape=tuple(pltpu.HBM(b.shape, b.dtype) for b in self.buffers),
            in_specs=[_HBM_SPEC] * n + [_SEM_SPEC, _SEM_SPEC, pl.BlockSpec(memory_space=pl.ANY)],
            out_specs=tuple([_HBM_SPEC] * n),
            input_output_aliases={i: i for i in range(n)},
            compiler_params=pltpu.CompilerParams(has_side_effects=_EFFECT),
        )(*self.buffers, self.send_sems, self.recv_sems, after)
        return list(outs[:self.n_src]), list(outs[self.n_src:])


def _pair_plan(src_index):
    def plan(srcs, lands):
        x, y, c = _coords()
        chip = 2 * x + y
        return [(srcs[0].at[src_index(j, chip, c)], lands[0].at[j], (x, y, 1 - c)) for j in range(4)]
    return plan


def _chips_plan(srcs, lands):
    x, y, c = _coords()
    return [(srcs[0].at[j - 1], lands[0].at[j - 1], (_flip(x, j & 2), _flip(y, j & 1), c)) for j in range(1, 4)]


def _peer(k):
    x, y, c = _coords()
    return _flip(x, k & 4), _flip(y, k & 2), _flip(c, k & 1)


def _send_plan(ks):
    def plan(srcs, lands):
        return [(srcs[0].at[0], lands[0].at[p], _peer(k)) for p, k in enumerate(ks)]
    return plan


def _forward_plan(n):
    def plan(srcs, lands):
        return [(srcs[0].at[p], lands[0].at[p], _peer(1)) for p in range(n)]
    return plan


def _gather_plan(srcs, lands):
    x, y, c = _coords()
    me = 4 * x + 2 * y + c
    return [(srcs[a], lands[a].at[me], (_flip(x, k & 4), _flip(y, k & 2), _flip(c, k & 1)))
            for a in range(len(srcs)) for k in range(1, N_DEV)]


def _pair_sum(grad, recv, owners, name):
    _, rows, cols = grad.shape
    tr = _row_tile(rows)

    def body(own_ref, g_ref, r_ref, mine_ref, send_ref):
        j = pl.program_id(1)
        total = g_ref[...] + r_ref[...]

        @pl.when(j == 0)
        def _():
            mine_ref[...] = total

        @pl.when(j > 0)
        def _():
            send_ref[...] = total.astype(BF16)

    grid_spec = pltpu.PrefetchScalarGridSpec(
        num_scalar_prefetch=1, grid=(rows // tr, 4),
        in_specs=[pl.BlockSpec((1, tr, cols), lambda i, j, own: (own[j], i, 0)),
                  pl.BlockSpec((1, tr, cols), lambda i, j, own: (j, i, 0))],
        out_specs=[pl.BlockSpec((1, tr, cols), lambda i, j, own: (0, i, 0)),
                   pl.BlockSpec((1, tr, cols), lambda i, j, own: (jnp.maximum(j - 1, 0), i, 0))])
    return pl.pallas_call(
        body, name=name, grid_spec=grid_spec,
        out_shape=[jax.ShapeDtypeStruct((1, rows, cols), F32), jax.ShapeDtypeStruct((3, rows, cols), BF16)],
        compiler_params=_cparams(dimension_semantics=("arbitrary", "arbitrary")),
    )(owners, grad, recv)


def _modulation(c16, ada_w_sh, ada_b_sh, decay_logit):
    def body(c_ref, w_ref, b_ref, dl_ref, mod_ref, act_ref, lg_ref):
        cv = c_ref[...]
        act = cv * _sigmoid(cv)
        act_ref[...] = act
        mod_ref[...] = jnp.dot(act, w_ref[...], preferred_element_type=F32,
                               precision=lax.Precision.HIGHEST) + b_ref[...]
        z = dl_ref[...]
        lg_ref[...] = jnp.minimum(z, 0.0) - jnp.log(1.0 + jnp.exp(-jnp.abs(z)))

    return pl.pallas_call(
        body, name="modulation",
        out_shape=[jax.ShapeDtypeStruct((16, ADA_SH), F32), jax.ShapeDtypeStruct((16, D), F32),
                   jax.ShapeDtypeStruct(decay_logit.shape, F32)],
        compiler_params=_cparams(),
    )(c16, ada_w_sh, ada_b_sh, decay_logit)


def _adam_update(w, g, m, v):
    m2 = ADAM_B1 * m + (1.0 - ADAM_B1) * g
    v2 = ADAM_B2 * v + (1.0 - ADAM_B2) * (g * g)
    m_hat = m2 / (1.0 - ADAM_B1 ** ADAM_STEP)
    v_hat = v2 / (1.0 - ADAM_B2 ** ADAM_STEP)
    delta = -ADAM_LR * (m_hat / (jnp.sqrt(v_hat) + ADAM_EPS) + ADAM_WD * w)
    return delta, m2, v2


def _adam_sharded(w, m, v, own, recv, name):
    rows, cols = w.shape
    tr = _row_tile(rows)

    def body(w_ref, m_ref, v_ref, g0, g1, g2, g3, g_ref, d_ref, m_out, v_out):
        g = ((g0[0] + g1[0].astype(F32)) + g2[0].astype(F32)) + g3[0].astype(F32)
        delta, m2, v2 = _adam_update(w_ref[...], g, m_ref[...], v_ref[...])
        g_ref[...] = g
        d_ref[...] = delta
        m_out[...] = m2
        v_out[...] = v2

    flat = pl.BlockSpec((tr, cols), lambda i: (i, 0))
    part = lambda j: pl.BlockSpec((1, tr, cols), lambda i, j=j: (j, i, 0))
    return pl.pallas_call(
        body, name=name, grid=(rows // tr,),
        in_specs=[flat, flat, flat, part(0), part(0), part(1), part(2)],
        out_specs=[flat] * 4,
        out_shape=[jax.ShapeDtypeStruct((rows, cols), F32)] * 4,
        compiler_params=_cparams(dimension_semantics=("arbitrary",)),
    )(w, m, v, own, recv, recv, recv)


def _ada_backward(act_t, dmod16, dmodc8, ada_w_sh, m, v):
    def body(at_ref, dm_ref, dc_ref, w_ref, m_ref, v_ref, g_ref, d_ref, m_out, v_out, pc_ref):
        g = jnp.dot(at_ref[...], dm_ref[...], preferred_element_type=F32, precision=lax.Precision.HIGHEST)
        w = w_ref[...]
        delta, m2, v2 = _adam_update(w, g, m_ref[...], v_ref[...])
        g_ref[...] = g
        d_ref[...] = delta
        m_out[...] = m2
        v_out[...] = v2
        pc_ref[...] = lax.dot_general(dc_ref[...], w, (((1,), (1,)), ((), ())), preferred_element_type=F32,
                                      precision=lax.Precision.HIGHEST)

    return pl.pallas_call(
        body, name="ada_backward",
        out_shape=[jax.ShapeDtypeStruct((D, ADA_SH), F32)] * 4 + [jax.ShapeDtypeStruct((8, D), F32)],
        compiler_params=_cparams(),
    )(act_t, dmod16, dmodc8, ada_w_sh, m, v)


def _sum_devices(gathered, name):
    _, rows, cols = gathered.shape

    def body(g_ref, o_ref):
        acc = g_ref[0]
        for d in range(1, N_DEV):
            acc = acc + g_ref[d]
        o_ref[...] = acc

    return pl.pallas_call(body, name=name, out_shape=jax.ShapeDtypeStruct((rows, cols), F32),
                          compiler_params=_cparams())(gathered)


SMALL_PARAMS = ("c_ctx", "norm_w", "conv_b", "gn_w", "final_norm_w", "ada_b", "decay_logit", "conv_w")


def _adam_small(tot, dact_ctx, conv_w_grad, params):
    n = len(SMALL_PARAMS)

    def body(tot_ref, dact_ref, cwg_ref, *refs):
        ins, outs = refs[:3 * n], refs[3 * n:]
        nh = 2 * N_HEADS
        raw = {
            "c_ctx": dact_ref[0:1, :],
            "norm_w": tot_ref[1:2, :], "conv_b": tot_ref[2:3, :], "gn_w": tot_ref[3:4, :],
            "final_norm_w": tot_ref[0:1, :],
            "ada_b": jnp.concatenate([tot_ref[7 + r:8 + r, :] + tot_ref[10 + r:11 + r, :] for r in range(3)], axis=1),
            "decay_logit": tot_ref[14:15, 0:nh],
            "conv_w": cwg_ref[...],
        }
        for k, name in enumerate(SMALL_PARAMS):
            w, m, v = ins[3 * k][...], ins[3 * k + 1][...], ins[3 * k + 2][...]
            g = raw[name]
            if name == "c_ctx":
                s = _sigmoid(w)
                g = g * (s * (1.0 + w * (1.0 - s)))
            elif name == "decay_logit":
                g = g * (1.0 - _sigmoid(w))
            delta, m2, v2 = _adam_update(w, g, m, v)
            for ref, val in zip(outs[4 * k:4 * k + 4], (g, delta, m2, v2)):
                ref[...] = val

    flat = [a for trio in params for a in trio]
    out_shape = [jax.ShapeDtypeStruct(trio[0].shape, F32) for trio in params for _ in range(4)]
    outs = pl.pallas_call(body, name="adam_small", out_shape=out_shape, compiler_params=_cparams())(
        tot, dact_ctx, conv_w_grad, *flat)
    return [outs[4 * k:4 * k + 4] for k in range(n)]


def _adam_square(params, own, recv):
    def body(own_ref, recv_ref, *refs):
        ins, outs = refs[:9], refs[9:]
        for k in range(3):
            rows = slice(k * RSH, (k + 1) * RSH)
            g = own_ref[0, rows, :]
            for j in range(3):
                g = g + recv_ref[j, rows, :].astype(F32)
            delta, m2, v2 = _adam_update(ins[3 * k][...], g, ins[3 * k + 1][...], ins[3 * k + 2][...])
            for ref, val in zip(outs[4 * k:4 * k + 4], (g, delta, m2, v2)):
                ref[...] = val

    flat = [a for trio in params for a in trio]
    outs = pl.pallas_call(body, name="adam_square", out_shape=[jax.ShapeDtypeStruct((RSH, D), F32)] * 12,
                          compiler_params=_cparams())(own, recv, *flat)
    return [outs[4 * k:4 * k + 4] for k in range(3)]


def _rope(t, cos, sin):
    lane = lax.broadcasted_iota(jnp.int32, (1, LANES), 1)
    first_half = jnp.bitwise_and(lane, DK // 2) == 0
    partner = jnp.where(first_half, pltpu.roll(t, LANES - DK // 2, 1), pltpu.roll(t, DK // 2, 1))
    return t * cos + partner * sin


class _Shards:
    def __init__(self, arrays, picks):
        self.arrays, self.picks = list(arrays), list(picks)

    def specs(self, index_args):
        out = []
        for a, arr in enumerate(self.arrays):
            steps = [p for p, (ai, _) in enumerate(self.picks) if ai == a]
            first, slot0, last = steps[0], self.picks[steps[0]][1], arr.shape[0] - 1
            out.append(pl.BlockSpec(
                (1, D, WSH), lambda *args, first=first, slot0=slot0, last=last:
                (jnp.clip(index_args(*args) - first + slot0, 0, last), 0, 0)))
        return out

    def __add__(self, other):
        shift = len(self.arrays)
        return _Shards(self.arrays + other.arrays, self.picks + [(a + shift, s) for a, s in other.picks])


def _load_w_in(step, ids_ref, shards, w_refs, w_vmem, sems):
    @pl.when(step == 0)
    def _():
        copies = []
        for p, (a, slot) in enumerate(shards.picks):
            col = pl.multiple_of(ids_ref[p] * WSH, LANES)
            copies.append(pltpu.make_async_copy(w_refs[a].at[slot], w_vmem.at[:, pl.ds(col, WSH)], sems.at[p]))
        for cp in copies:
            cp.start()
        for cp in copies:
            cp.wait()


def _project_columns(xmb, w, shard, cos, sin, p_ref):
    acc = _dot(xmb, w)
    groups = WSH // LANES
    q0 = CB_QK * D // LANES
    nq = N_HEADS * DK // LANES
    for g in range(groups):
        grp = shard * groups + g
        piece = acc[:, g * LANES:(g + 1) * LANES]
        rotary = jnp.logical_and(grp >= q0, grp < q0 + 2 * nq)

        @pl.when(rotary)
        def _():
            scale = jnp.where(grp >= q0 + nq, DK ** -0.5, 1.0)
            p_ref[:, g * LANES:(g + 1) * LANES] = _rope(piece * scale, cos, sin).astype(BF16)

        @pl.when(jnp.logical_not(rotary))
        def _():
            p_ref[:, g * LANES:(g + 1) * LANES] = piece.astype(BF16)


def _in_projection_first(x, ctx, modx, modc, norm_w, cos_t, sin_t, shards, ids):
    n_x_tiles = x.shape[0] // TM
    lext = x.shape[0] + ctx.shape[0]
    n_w = len(shards.arrays)

    def body(ids_ref, x_ref, ctx_ref, mx_ref, mc_ref, nw_ref, cos_ref, sin_ref, *rest):
        w_refs, (p_ref, xm_ref, xmt_ref) = rest[:n_w], rest[n_w:]
        j, i = pl.program_id(0), pl.program_id(1)
        is_ctx = i >= n_x_tiles
        x = jnp.where(is_ctx, ctx_ref[...], x_ref[...])
        r = lax.rsqrt(jnp.mean(x * x, axis=-1, keepdims=True) + EPS)
        sh = jnp.where(is_ctx, mc_ref[0:1, :], mx_ref[0:1, :])
        sc = jnp.where(is_ctx, mc_ref[1:2, :], mx_ref[1:2, :])
        xm = (x * r * nw_ref[...]) * (1.0 + sc) + sh
        xmb = xm.astype(BF16)

        @pl.when(j == 0)
        def _():
            xm_ref[...] = xmb
            xmt_ref[...] = xm.T.astype(BF16)

        for p, (a, _) in enumerate(shards.picks):
            @pl.when(j == p)
            def _():
                _project_columns(xmb, w_refs[a][0], ids_ref[p], cos_ref[...], sin_ref[...], p_ref)

    n_tiles = lext // TM
    row = lambda w: pl.BlockSpec((TM, w), lambda j, i, ids: (i, 0))
    full = lambda a: pl.BlockSpec(a.shape, lambda j, i, ids: (0,) * a.ndim)
    once = lambda j, i: jnp.where(j == 0, i, n_tiles - 1)
    grid_spec = pltpu.PrefetchScalarGridSpec(
        num_scalar_prefetch=1, grid=(len(shards.picks), n_tiles),
        in_specs=[pl.BlockSpec((TM, D), lambda j, i, ids: (jnp.minimum(i, n_x_tiles - 1), 0)), full(ctx),
                  full(modx), full(modc), full(norm_w), row(LANES), row(LANES)]
        + shards.specs(lambda j, i, ids: j),
        out_specs=[pl.BlockSpec((TM, WSH), lambda j, i, ids: (i, ids[j])),
                   pl.BlockSpec((TM, D), lambda j, i, ids: (once(j, i), 0)),
                   pl.BlockSpec((D, TM), lambda j, i, ids: (0, once(j, i)))])
    return pl.pallas_call(
        body, name="in_projection_first", grid_spec=grid_spec,
        out_shape=[jax.ShapeDtypeStruct((lext, PW), BF16), jax.ShapeDtypeStruct((lext, D), BF16),
                   jax.ShapeDtypeStruct((D, lext), BF16)],
        compiler_params=_cparams(dimension_semantics=("arbitrary", "arbitrary")),
    )(ids, x, ctx, modx, modc, norm_w, cos_t, sin_t, *shards.arrays)


def _in_projection_more(p_ext, xm, cos_t, sin_t, shards, ids, name):
    lext = xm.shape[0]
    n_w = len(shards.arrays)

    def body(ids_ref, xm_ref, cos_ref, sin_ref, *rest):
        w_refs, p_ref = rest[:n_w], rest[-1]
        j = pl.program_id(0)
        for p, (a, _) in enumerate(shards.picks):
            @pl.when(j == p)
            def _():
                _project_columns(xm_ref[...], w_refs[a][0], ids_ref[p], cos_ref[...], sin_ref[...], p_ref)

    row = lambda w: pl.BlockSpec((TM, w), lambda j, i, ids: (i, 0))
    grid_spec = pltpu.PrefetchScalarGridSpec(
        num_scalar_prefetch=1, grid=(len(shards.picks), lext // TM),
        in_specs=[row(D), row(LANES), row(LANES)] + shards.specs(lambda j, i, ids: j)
        + [pl.BlockSpec(memory_space=pl.ANY)],
        out_specs=pl.BlockSpec((TM, WSH), lambda j, i, ids: (i, ids[j])))
    return pl.pallas_call(
        body, name=name, grid_spec=grid_spec,
        out_shape=jax.ShapeDtypeStruct(p_ext.shape, BF16),
        input_output_aliases={4 + n_w: 0},
        compiler_params=_cparams(dimension_semantics=("arbitrary", "arbitrary")),
    )(ids, xm, cos_t, sin_t, *shards.arrays, p_ext)


def _decay_tables(lgf, lgb, n):
    i = lax.broadcasted_iota(jnp.int32, (n, 1), 0).astype(F32)
    return dict(i=i, k_f=jnp.exp(lgf * (n - 1.0 - i)), k_b=jnp.exp(lgb * i),
                q_f=jnp.exp(lgf * (i + 1.0)), q_b=jnp.exp(lgb * (n - i)))


def _decay_matrix(lgf, lgb, n, transposed=False):
    ii = lax.broadcasted_iota(jnp.int32, (n, n), 0)
    jj = lax.broadcasted_iota(jnp.int32, (n, n), 1)
    diff = (jj - ii if transposed else ii - jj).astype(F32)
    low = jnp.exp(lgf * jnp.maximum(diff, 0.0))
    up = jnp.exp(lgb * jnp.maximum(-diff, 0.0))
    return jnp.where(diff > 0, low, jnp.where(diff < 0, up, 2.0)), diff


def _cat_lanes(a, b):
    return jnp.concatenate([a.astype(BF16), b.astype(BF16)], axis=1)


def _retention_forward(p_ext, lg, seq, ctx_len):
    lext = seq + ctx_len
    n_chunks = seq // RET_C
    C = RET_C

    def body(lg_ref, q_ref, k_ref, v_ref, rn_ref, rstd_ref, kv_scr, sf_scr, sb_scr):
        pair = pl.program_id(0)
        lane = lax.broadcasted_iota(jnp.int32, (1, LANES), 1)
        heads = range(2)
        hmask = [(lane // DK == hh).astype(F32) for hh in heads]
        lgf = [lg_ref[0, 2 * pair + hh] for hh in heads]
        lgb = [lg_ref[1, 2 * pair + hh] for hh in heads]
        vs = [slice(hh * DV, (hh + 1) * DV) for hh in heads]
        t = [_decay_tables(lgf[hh], lgb[hh], C) for hh in heads]
        kc_all = k_ref[seq:lext, :].astype(F32)
        s0 = []
        for hh in heads:
            tc = _decay_tables(lgf[hh], lgb[hh], ctx_len)
            kc = kc_all * hmask[hh]
            s0.append(_dot_tn(_cat_lanes(kc * tc["k_f"], kc * tc["k_b"]), v_ref[seq:lext, vs[hh]]))

        def increments(c, carry):
            rows = pl.ds(pl.multiple_of(c * C, C), C)
            k_all = k_ref[rows, :].astype(F32)
            for hh in heads:
                k = k_all * hmask[hh]
                kv_scr[hh, c] = _dot_tn(_cat_lanes(k * t[hh]["k_f"], k * t[hh]["k_b"]), v_ref[rows, vs[hh]])
            return carry

        lax.fori_loop(0, n_chunks, increments, 0, unroll=2)
        gf_c = [jnp.exp(lgf[hh] * C) for hh in heads]
        gb_c = [jnp.exp(lgb[hh] * C) for hh in heads]

        def scan_f(c, s):
            for hh in heads:
                sf_scr[hh, c] = s[hh]
            return tuple(gf_c[hh] * s[hh] + kv_scr[hh, c, 0:LANES, :] for hh in heads)

        def scan_b(n, s):
            c = n_chunks - 1 - n
            for hh in heads:
                sb_scr[hh, c] = s[hh]
            return tuple(gb_c[hh] * s[hh] + kv_scr[hh, c, LANES:2 * LANES, :] for hh in heads)

        lax.fori_loop(0, n_chunks, scan_f, tuple(s0[hh][0:LANES] for hh in heads))
        lax.fori_loop(0, n_chunks, scan_b, tuple(s0[hh][LANES:2 * LANES] for hh in heads))
        dmat = [_decay_matrix(lgf[hh], lgb[hh], C)[0] for hh in heads]

        def outputs(c, carry):
            rows = pl.ds(pl.multiple_of(c * C, C), C)
            q_all = q_ref[rows, :].astype(F32)
            k = k_ref[rows, :]
            for hh in heads:
                q = q_all * hmask[hh]
                v = v_ref[rows, vs[hh]]
                s = _dot_nt(q.astype(BF16), k)
                inner = _dot((s * dmat[hh]).astype(BF16), v)
                states = jnp.concatenate([sf_scr[hh, c], sb_scr[hh, c]], axis=0).astype(BF16)
                ret = inner + _dot(_cat_lanes(q * t[hh]["q_f"], q * t[hh]["q_b"]), states)
                mu = jnp.mean(ret, axis=-1, keepdims=True)
                cen = ret - mu
                rstd = lax.rsqrt(jnp.mean(cen * cen, axis=-1, keepdims=True) + EPS)
                rn_ref[rows, vs[hh]] = cen * rstd
                rstd_ref[rows, vs[hh]] = jnp.broadcast_to(rstd, (C, DV))
            return carry

        lax.fori_loop(0, n_chunks, outputs, 0, unroll=2)

    qk0 = CB_QK * D // LANES
    return pl.pallas_call(
        body, name="retention_forward", grid=(N_HEADS // 2,),
        in_specs=[pl.BlockSpec(memory_space=pltpu.SMEM),
                  pl.BlockSpec((lext, LANES), lambda g: (0, qk0 + g)),
                  pl.BlockSpec((lext, LANES), lambda g: (0, qk0 + N_HEADS // 2 + g)),
                  pl.BlockSpec((lext, 2 * DV), lambda g: (0, CB_V * D // (2 * DV) + g))],
        out_specs=[pl.BlockSpec((seq, 2 * DV), lambda g: (0, g))] * 2,
        out_shape=[jax.ShapeDtypeStruct((seq, D), F32)] * 2,
        scratch_shapes=[pltpu.VMEM((2, n_chunks, 2 * LANES, DV), F32), pltpu.VMEM((2, n_chunks, LANES, DV), F32),
                        pltpu.VMEM((2, n_chunks, LANES, DV), F32)],
        compiler_params=_cparams(dimension_semantics=("arbitrary",)),
    )(lg, p_ext, p_ext, p_ext)


def _retention_backward(p_ext, dret, lg, cos_t, sin_t, seq, ctx_len):
    lext = seq + ctx_len
    n_chunks = seq // RET_C
    C = RET_C

    def body(lg_ref, q_ref, k_ref, v_ref, do_ref, cos_ref, sin_ref, dq_ref, dk_ref, dv_ref, dlg_ref,
             kv_scr, g_scr, sf_scr, sb_scr, gfn_scr, gbp_scr):
        pair = pl.program_id(0)
        lane = lax.broadcasted_iota(jnp.int32, (1, LANES), 1)
        heads = range(2)
        hmask = [(lane // DK == hh).astype(F32) for hh in heads]
        lgf = [lg_ref[0, 2 * pair + hh] for hh in heads]
        lgb = [lg_ref[1, 2 * pair + hh] for hh in heads]
        vs = [slice(hh * DV, (hh + 1) * DV) for hh in heads]
        t = [_decay_tables(lgf[hh], lgb[hh], C) for hh in heads]
        tc = [_decay_tables(lgf[hh], lgb[hh], ctx_len) for hh in heads]
        kc_all = k_ref[seq:lext, :].astype(F32)
        kc = [kc_all * hmask[hh] for hh in heads]
        vc = [v_ref[seq:lext, vs[hh]] for hh in heads]
        kc_cat = [_cat_lanes(kc[hh] * tc[hh]["k_f"], kc[hh] * tc[hh]["k_b"]) for hh in heads]
        s0 = [_dot_tn(kc_cat[hh], vc[hh]) for hh in heads]

        def increments(c, carry):
            rows = pl.ds(pl.multiple_of(c * C, C), C)
            k_all = k_ref[rows, :].astype(F32)
            q_all = q_ref[rows, :].astype(F32)
            for hh in heads:
                k, q = k_all * hmask[hh], q_all * hmask[hh]
                kv_scr[hh, c] = _dot_tn(_cat_lanes(k * t[hh]["k_f"], k * t[hh]["k_b"]), v_ref[rows, vs[hh]])
                g_scr[hh, c] = _dot_tn(_cat_lanes(q * t[hh]["q_f"], q * t[hh]["q_b"]), do_ref[rows, vs[hh]])
            return carry

        lax.fori_loop(0, n_chunks, increments, 0, unroll=2)
        gf_c = [jnp.exp(lgf[hh] * C) for hh in heads]
        gb_c = [jnp.exp(lgb[hh] * C) for hh in heads]

        def scan_f(c, s):
            for hh in heads:
                sf_scr[hh, c] = s[hh]
            return tuple(gf_c[hh] * s[hh] + kv_scr[hh, c, 0:LANES, :] for hh in heads)

        def scan_b(n, s):
            c = n_chunks - 1 - n
            for hh in heads:
                sb_scr[hh, c] = s[hh]
            return tuple(gb_c[hh] * s[hh] + kv_scr[hh, c, LANES:2 * LANES, :] for hh in heads)

        def scan_gf(n, carry):
            c = n_chunks - 1 - n
            for hh in heads:
                gfn_scr[hh, c] = carry[hh]
            return tuple(g_scr[hh, c, 0:LANES, :] + gf_c[hh] * carry[hh] for hh in heads)

        def scan_gb(c, carry):
            for hh in heads:
                gbp_scr[hh, c] = carry[hh]
            return tuple(g_scr[hh, c, LANES:2 * LANES, :] + gb_c[hh] * carry[hh] for hh in heads)

        lax.fori_loop(0, n_chunks, scan_f, tuple(s0[hh][0:LANES] for hh in heads))
        lax.fori_loop(0, n_chunks, scan_b, tuple(s0[hh][LANES:2 * LANES] for hh in heads))
        zero_state = jnp.zeros((LANES, DV), F32)
        gf_first = lax.fori_loop(0, n_chunks, scan_gf, (zero_state, zero_state))
        gb_last = lax.fori_loop(0, n_chunks, scan_gb, (zero_state, zero_state))

        dmat, w_f, w_b = [], [], []
        for hh in heads:
            dm, diff = _decay_matrix(lgf[hh], lgb[hh], C)
            dmat.append(dm)
            w_f.append(jnp.where(diff > 0, diff * dm, 0.0))
            w_b.append(jnp.where(diff < 0, -diff * dm, 0.0))

        def total(a):
            rows_, width = a.shape
            part = jnp.sum(a.reshape(rows_ // 8, 8, width), axis=0)
            return part[:, 0:LANES] + part[:, LANES:2 * LANES] if width == 2 * LANES else part

        def chunk_head(hh, c, rows, q_all, k_all, dlgf, dlgb):
            th = t[hh]
            qm = q_all * hmask[hh]
            km = k_all * hmask[hh]
            qb, kb = qm.astype(BF16), km.astype(BF16)
            v = v_ref[rows, vs[hh]]
            do = do_ref[rows, vs[hh]]
            s = _dot_nt(qb, kb)
            dsd = _dot_nt(do, v)
            ds = (dsd * dmat[hh]).astype(BF16)
            dq_in = _dot(ds, kb)
            dk_in = _dot_tn(ds, qb)
            dv_in = _dot_tn((s * dmat[hh]).astype(BF16), do)
            prod = s * dsd
            dlgf = dlgf + total(prod * w_f[hh])
            dlgb = dlgb + total(prod * w_b[hh])
            sf, sb = sf_scr[hh, c], sb_scr[hh, c]
            states = jnp.concatenate([sf, sb], axis=0).astype(BF16)
            dqc = _dot_nt(do, states)
            dqf = dqc[:, 0:LANES] * th["q_f"]
            dqb = dqc[:, LANES:2 * LANES] * th["q_b"]
            dq = (dq_in + dqf + dqb) * hmask[hh]
            dlgf = dlgf + total((th["i"] + 1.0) * (qm * dqf))
            dlgb = dlgb + total((C - th["i"]) * (qm * dqb))
            gfn, gbp = gfn_scr[hh, c], gbp_scr[hh, c]
            gstates = jnp.concatenate([gfn, gbp], axis=0).astype(BF16)
            dkc = _dot_nt(v, gstates)
            dkf = dkc[:, 0:LANES] * th["k_f"]
            dkb = dkc[:, LANES:2 * LANES] * th["k_b"]
            dk = dk_in + dkf + dkb
            dlgf = dlgf + total((C - 1.0 - th["i"]) * (km * dkf)) + C * gf_c[hh] * total(gfn * sf)
            dlgb = dlgb + total(th["i"] * (km * dkb)) + C * gb_c[hh] * total(gbp * sb)
            dv = dv_in + _dot(_cat_lanes(km * th["k_f"], km * th["k_b"]), gstates)
            dv_ref[rows, vs[hh]] = dv.astype(BF16)
            return dq, dk, dlgf, dlgb

        def chunk(c, carry):
            rows = pl.ds(pl.multiple_of(c * C, C), C)
            q_all = q_ref[rows, :].astype(F32)
            k_all = k_ref[rows, :].astype(F32)
            dq0, dk0, f0, b0 = chunk_head(0, c, rows, q_all, k_all, carry[0], carry[1])
            dq1, dk1, f1, b1 = chunk_head(1, c, rows, q_all, k_all, carry[2], carry[3])
            cos, sin = cos_ref[rows, :], sin_ref[rows, :]
            dq_ref[rows, :] = _rope(dq0 + dq1, cos, -sin).astype(BF16)
            dk_ref[rows, :] = (_rope(dk0 + dk1, cos, -sin) * (DK ** -0.5)).astype(BF16)
            return f0, b0, f1, b1

        zero = jnp.zeros((8, LANES), F32)
        sums = lax.fori_loop(0, n_chunks, chunk, (zero, zero, zero, zero), unroll=2)

        dlg = []
        dk_ctx = jnp.zeros((ctx_len, LANES), F32)
        for hh in heads:
            g0 = jnp.concatenate([gf_first[hh], gb_last[hh]], axis=0).astype(BF16)
            dkcc = _dot_nt(vc[hh], g0)
            dkcf = dkcc[:, 0:LANES] * tc[hh]["k_f"]
            dkcb = dkcc[:, LANES:2 * LANES] * tc[hh]["k_b"]
            dlgf = sums[2 * hh] + total((ctx_len - 1.0 - tc[hh]["i"]) * (kc[hh] * dkcf))
            dlgb = sums[2 * hh + 1] + total(tc[hh]["i"] * (kc[hh] * dkcb))
            dk_ctx = dk_ctx + (dkcf + dkcb) * (DK ** -0.5)
            dv_ref[seq:lext, vs[hh]] = _dot(kc_cat[hh], g0).astype(BF16)
            dlg += [jnp.sum(jnp.sum(a, axis=1, keepdims=True), axis=0, keepdims=True) for a in (dlgf, dlgb)]
        dk_ref[seq:lext, :] = dk_ctx.astype(BF16)
        dq_ref[seq:lext, :] = jnp.zeros((ctx_len, LANES), BF16)

        lane8 = lax.broadcasted_iota(jnp.int32, (8, LANES), 1)
        out = jnp.zeros((8, LANES), F32)
        for n, val in enumerate(dlg):
            out = jnp.where(lane8 == n, val, out)
        dlg_ref[0] = out

    qk0 = CB_QK * D // LANES
    q_spec = pl.BlockSpec((lext, LANES), lambda g: (0, qk0 + g))
    k_spec = pl.BlockSpec((lext, LANES), lambda g: (0, qk0 + N_HEADS // 2 + g))
    v_spec = pl.BlockSpec((lext, 2 * DV), lambda g: (0, CB_V * D // (2 * DV) + g))
    table = pl.BlockSpec((lext, LANES), lambda g: (0, 0))
    state = pltpu.VMEM((2, n_chunks, LANES, DV), F32)
    return pl.pallas_call(
        body, name="retention_backward", grid=(N_HEADS // 2,),
        in_specs=[pl.BlockSpec(memory_space=pltpu.SMEM), q_spec, k_spec, v_spec,
                  pl.BlockSpec((seq, 2 * DV), lambda g: (0, g)), table, table],
        out_specs=[pl.BlockSpec((lext, LANES), lambda g: (0, g)), pl.BlockSpec((lext, LANES), lambda g: (0, g)),
                   pl.BlockSpec((lext, 2 * DV), lambda g: (0, g)), pl.BlockSpec((1, 8, LANES), lambda g: (g, 0, 0))],
        out_shape=[jax.ShapeDtypeStruct((lext, N_HEADS * DK), BF16), jax.ShapeDtypeStruct((lext, N_HEADS * DK), BF16),
                   jax.ShapeDtypeStruct((lext, D), BF16), jax.ShapeDtypeStruct((N_HEADS // 2, 8, LANES), F32)],
        scratch_shapes=[pltpu.VMEM((2, n_chunks, 2 * LANES, DV), F32), pltpu.VMEM((2, n_chunks, 2 * LANES, DV), F32),
                        state, state, state, state],
        compiler_params=_cparams(dimension_semantics=("arbitrary",)),
    )(lg, p_ext, p_ext, p_ext, dret, cos_t, sin_t)


def _merge(p_ext, rn, rstd, x, target, w_a, w_b, w_out, vecs, seq):
    n_tiles = seq // TM
    hb = TM // HALO

    def body(h_ref, bg_ref, cg_ref, za_ref, zb_ref, ga_ref, gb_ref, hp_ref, hn_ref, cp_ref, cn_ref,
             rn_ref, rstd_ref, x_ref, t_ref, wa_ref, wb_ref, wo_ref, vec_ref,
             dx1_ref, dpa_ref, dconv_ref, dret_ref, at_ref, b_ref, part_ref):
        i = pl.program_id(0)
        f = lambda ref: ref[...].astype(F32)
        h, bg, cg, za, zb, ga, gb = f(h_ref), f(bg_ref), f(cg_ref), f(za_ref), f(zb_ref), f(ga_ref), f(gb_ref)
        gx, w0, w1, w2 = vec_ref[0:1, :], vec_ref[1:2, :], vec_ref[2:3, :], vec_ref[3:4, :]
        cb, gnw, fw = vec_ref[4:5, :], vec_ref[5:6, :], vec_ref[6:7, :]
        u = cg * h
        row = lax.broadcasted_iota(jnp.int32, (TM, 1), 0)
        u_prev = (f(cp_ref) * f(hp_ref))[HALO - 1:HALO, :]
        u_next = (f(cn_ref) * f(hn_ref))[0:1, :]
        u_prev = jnp.where(i == 0, 0.0, u_prev)
        u_next = jnp.where(i == n_tiles - 1, 0.0, u_next)
        u_up = jnp.where(row == 0, u_prev, pltpu.roll(u, 1, 0))
        u_dn = jnp.where(row == TM - 1, u_next, pltpu.roll(u, TM - 1, 0))
        conv = w0 * u_up + w1 * u + w2 * u_dn + cb
        sza = _sigmoid(za)
        silu_za = za * sza
        a_act = silu_za * bg * conv
        rn = rn_ref[...]
        szb = _sigmoid(zb)
        silu_zb = zb * szb
        rg = rn * gnw
        b_act = silu_zb * rg
        y_a = _dot(a_act.astype(BF16), wa_ref[...])
        y_b = _dot(b_act.astype(BF16), wb_ref[...])
        sga, sgb = _sigmoid(ga), _sigmoid(gb)
        mix = sga * y_a + sgb * y_b
        y = _dot(mix.astype(BF16), wo_ref[...])
        x1 = x_ref[...] + gx * y
        r1 = lax.rsqrt(jnp.mean(x1 * x1, axis=-1, keepdims=True) + EPS)
        xh1 = x1 * r1
        err = xh1 * fw - t_ref[...]
        loss = jnp.sum(jnp.sum(err * err, axis=1, keepdims=True), axis=0, keepdims=True) * (0.5 / D)
        dout = err * (1.0 / D)
        dxh = dout * fw
        dx1 = r1 * (dxh - xh1 * jnp.mean(dxh * xh1, axis=-1, keepdims=True))
        dx1_ref[...] = dx1
        dy = (dx1 * gx).astype(BF16)
        dmix = _dot_nt(dy, wo_ref[...])
        dya = (dmix * sga).astype(BF16)
        dyb = (dmix * sgb).astype(BF16)
        da = _dot_nt(dya, wa_ref[...])
        db = _dot_nt(dyb, wb_ref[...])
        dpa_ref[:, 0:D] = (da * silu_za * conv).astype(BF16)
        dpa_ref[:, D:2 * D] = (da * bg * conv * (sza * (1.0 + za * (1.0 - sza)))).astype(BF16)
        dpa_ref[:, 2 * D:3 * D] = (db * rg * (szb * (1.0 + zb * (1.0 - szb)))).astype(BF16)
        dpa_ref[:, 3 * D:4 * D] = (dmix * y_a * sga * (1.0 - sga)).astype(BF16)
        dpa_ref[:, 4 * D:5 * D] = (dmix * y_b * sgb * (1.0 - sgb)).astype(BF16)
        dconv_ref[...] = (da * silu_za * bg).astype(BF16)
        drn_n = db * silu_zb
        drn = drn_n * gnw
        rstd = rstd_ref[...]
        for hd in range(N_HEADS):
            sl = slice(hd * DV, (hd + 1) * DV)
            dh_, rh = drn[:, sl], rn[:, sl]
            m1 = jnp.mean(dh_, axis=-1, keepdims=True)
            m2 = jnp.mean(dh_ * rh, axis=-1, keepdims=True)
            dret_ref[:, sl] = (rstd[:, sl] * (dh_ - m1 - rh * m2)).astype(BF16)
        at_ref[0] = a_act.T.astype(BF16)
        at_ref[1] = b_act.T.astype(BF16)
        at_ref[2] = mix.T.astype(BF16)
        b_ref[0] = dya
        b_ref[1] = dyb
        b_ref[2] = dy

        @pl.when(i == 0)
        def _():
            part_ref[...] = jnp.zeros((8, D), F32)

        part_ref[0:1, :] += jnp.sum(dout * xh1, axis=0, keepdims=True)
        part_ref[1:2, :] += jnp.sum(dx1 * y, axis=0, keepdims=True)
        part_ref[2:3, :] += jnp.sum(drn_n * rn, axis=0, keepdims=True)
        part_ref[3:4, :] += jnp.broadcast_to(loss, (1, D))

    col = lambda cb_: pl.BlockSpec((TM, D), lambda i, cb_=cb_: (i, cb_))
    prev = lambda cb_: pl.BlockSpec((HALO, D), lambda i, cb_=cb_: (jnp.maximum(i * hb - 1, 0), cb_))
    nxt = lambda cb_: pl.BlockSpec((HALO, D), lambda i, cb_=cb_: (jnp.minimum((i + 1) * hb, n_tiles * hb - 1), cb_))
    tile = pl.BlockSpec((TM, D), lambda i: (i, 0))
    full = lambda a: pl.BlockSpec(a.shape, lambda i: (0,) * a.ndim)
    return pl.pallas_call(
        body, name="merge", grid=(n_tiles,),
        in_specs=[col(CB_H), col(CB_BG), col(CB_CG), col(CB_ZA), col(CB_ZB), col(CB_GA), col(CB_GB),
                  prev(CB_H), nxt(CB_H), prev(CB_CG), nxt(CB_CG),
                  tile, tile, tile, tile, full(w_a), full(w_b), full(w_out), full(vecs)],
        out_specs=[tile, pl.BlockSpec((TM, 5 * D), lambda i: (i, 0)), tile, tile,
                   pl.BlockSpec((3, D, TM), lambda i: (0, 0, i)), pl.BlockSpec((3, TM, D), lambda i: (0, i, 0)),
                   pl.BlockSpec((8, D), lambda i: (0, 0))],
        out_shape=[jax.ShapeDtypeStruct((seq, D), F32), jax.ShapeDtypeStruct((seq, 5 * D), BF16),
                   jax.ShapeDtypeStruct((seq, D), BF16), jax.ShapeDtypeStruct((seq, D), BF16),
                   jax.ShapeDtypeStruct((3, D, seq), BF16), jax.ShapeDtypeStruct((3, seq, D), BF16),
                   jax.ShapeDtypeStruct((8, D), F32)],
        compiler_params=_cparams(dimension_semantics=("arbitrary",)),
    )(p_ext, p_ext, p_ext, p_ext, p_ext, p_ext, p_ext, p_ext, p_ext, p_ext, p_ext,
      rn, rstd, x, target, w_a, w_b, w_out, vecs)


def _conv_backward(p_ext, dconv, dpa, dq, dk, dv, vecs, seq):
    n_tiles = seq // TM
    hb = TM // HALO
    qk = N_HEADS * DK

    def body(h_ref, cg_ref, dc_ref, dcp_ref, dcn_ref, dpa_ref, dq_ref, dk_ref, dv_ref, vec_ref, dp_ref, part_ref):
        i = pl.program_id(0)
        dp_ref[:, CB_QK * D:CB_QK * D + qk] = dq_ref[...]
        dp_ref[:, CB_QK * D + qk:CB_V * D] = dk_ref[...]
        dp_ref[:, CB_V * D:CB_ZB * D] = dv_ref[...]

        @pl.when(i == 0)
        def _():
            part_ref[...] = jnp.zeros((8, D), F32)

        @pl.when(i == n_tiles)
        def _():
            dp_ref[:, 0:CB_QK * D] = jnp.zeros((TM, CB_QK * D), BF16)
            dp_ref[:, CB_ZB * D:PW] = jnp.zeros((TM, PW - CB_ZB * D), BF16)

        @pl.when(i < n_tiles)
        def _():
            f = lambda ref: ref[...].astype(F32)
            h, cg, dc = f(h_ref), f(cg_ref), f(dc_ref)
            w0, w1, w2 = vec_ref[1:2, :], vec_ref[2:3, :], vec_ref[3:4, :]
            row = lax.broadcasted_iota(jnp.int32, (TM, 1), 0)
            dc_prev = jnp.where(i == 0, 0.0, f(dcp_ref)[HALO - 1:HALO, :])
            dc_next = jnp.where(i == n_tiles - 1, 0.0, f(dcn_ref)[0:1, :])
            dc_up = jnp.where(row == 0, dc_prev, pltpu.roll(dc, 1, 0))
            dc_dn = jnp.where(row == TM - 1, dc_next, pltpu.roll(dc, TM - 1, 0))
            du = w0 * dc_dn + w1 * dc + w2 * dc_up
            u = cg * h
            dp_ref[:, CB_H * D:(CB_H + 1) * D] = (du * cg).astype(BF16)
            dp_ref[:, CB_CG * D:(CB_CG + 1) * D] = (du * h).astype(BF16)
            dp_ref[:, CB_BG * D:(CB_BG + 1) * D] = dpa_ref[:, 0:D]
            dp_ref[:, CB_ZA * D:(CB_ZA + 1) * D] = dpa_ref[:, D:2 * D]
            dp_ref[:, CB_ZB * D:(CB_GB + 1) * D] = dpa_ref[:, 2 * D:5 * D]
            part_ref[0:1, :] += jnp.sum(u * dc_dn, axis=0, keepdims=True)
            part_ref[1:2, :] += jnp.sum(u * dc, axis=0, keepdims=True)
            part_ref[2:3, :] += jnp.sum(u * dc_up, axis=0, keepdims=True)
            part_ref[3:4, :] += jnp.sum(dc, axis=0, keepdims=True)

    last = n_tiles - 1
    col = lambda cb_: pl.BlockSpec((TM, D), lambda i, cb_=cb_: (jnp.minimum(i, last), cb_))
    lat = lambda w: pl.BlockSpec((TM, w), lambda i: (jnp.minimum(i, last), 0))
    ext = lambda w: pl.BlockSpec((TM, w), lambda i: (i, 0))
    return pl.pallas_call(
        body, name="conv_backward", grid=(n_tiles + 1,),
        in_specs=[col(CB_H), col(CB_CG), lat(D),
                  pl.BlockSpec((HALO, D), lambda i: (jnp.clip(i * hb - 1, 0, n_tiles * hb - 1), 0)),
                  pl.BlockSpec((HALO, D), lambda i: (jnp.minimum((i + 1) * hb, n_tiles * hb - 1), 0)),
                  lat(5 * D), ext(qk), ext(qk), ext(D),
                  pl.BlockSpec(vecs.shape, lambda i: (0, 0))],
        out_specs=[ext(PW), pl.BlockSpec((8, D), lambda i: (0, 0))],
        out_shape=[jax.ShapeDtypeStruct((seq + TM, PW), BF16), jax.ShapeDtypeStruct((8, D), F32)],
        compiler_params=_cparams(dimension_semantics=("arbitrary",)),
    )(p_ext, p_ext, dconv, dconv, dconv, dpa, dq, dk, dv, vecs)


def _input_backward(dp, shards, ids, x, ctx, dx1, modx, modc, norm_w):
    seq = x.shape[0]
    lext = seq + ctx.shape[0]
    n_x = seq // TM
    n_w = len(shards.arrays)

    def body(ids_ref, dp_ref, x_ref, ctx_ref, dx1_ref, mx_ref, mc_ref, nw_ref, *rest):
        w_hbm, (gx_ref, part_ref, w_ref, w_sems) = rest[:n_w], rest[n_w:]
        i = pl.program_id(0)
        _load_w_in(i, ids_ref, shards, w_hbm, w_ref, w_sems)
        is_ctx = i >= n_x
        dxm = _dot_nt(dp_ref[...], w_ref[...])
        x = jnp.where(is_ctx, ctx_ref[...], x_ref[...])
        r = lax.rsqrt(jnp.mean(x * x, axis=-1, keepdims=True) + EPS)
        xh = x * r
        nw = nw_ref[...]
        sc = jnp.where(is_ctx, mc_ref[1:2, :], mx_ref[1:2, :])
        dxn = dxm * (1.0 + sc)
        dxh = dxn * nw
        dx = r * (dxh - xh * jnp.mean(dxh * xh, axis=-1, keepdims=True))

        @pl.when(jnp.logical_not(is_ctx))
        def _():
            gx_ref[...] = dx1_ref[...] + dx

        @pl.when(i == 0)
        def _():
            part_ref[...] = jnp.zeros((8, D), F32)

        fx = jnp.where(is_ctx, 0.0, 1.0)
        d_shift = jnp.sum(dxm, axis=0, keepdims=True)
        d_scale = jnp.sum(dxm * (xh * nw), axis=0, keepdims=True)
        part_ref[0:1, :] += fx * d_shift
        part_ref[1:2, :] += fx * d_scale
        part_ref[2:3, :] += jnp.sum(dxn * xh, axis=0, keepdims=True)
        part_ref[3:4, :] += (1.0 - fx) * d_shift
        part_ref[4:5, :] += (1.0 - fx) * d_scale

    lat = lambda w: pl.BlockSpec((TM, w), lambda i, ids_: (jnp.minimum(i, n_x - 1), 0))
    ext = lambda w: pl.BlockSpec((TM, w), lambda i, ids_: (i, 0))
    full = lambda a: pl.BlockSpec(a.shape, lambda i, ids_: (0,) * a.ndim)
    grid_spec = pltpu.PrefetchScalarGridSpec(
        num_scalar_prefetch=1, grid=(lext // TM,),
        in_specs=[ext(PW), lat(D), full(ctx), lat(D), full(modx), full(modc), full(norm_w)]
        + [pl.BlockSpec(memory_space=pl.ANY)] * n_w,
        out_specs=[lat(D), pl.BlockSpec((8, D), lambda i, ids_: (0, 0))],
        scratch_shapes=[pltpu.VMEM((D, PW), BF16), pltpu.SemaphoreType.DMA((N_DEV,))])
    return pl.pallas_call(
        body, name="input_backward", grid_spec=grid_spec,
        out_shape=[jax.ShapeDtypeStruct((seq, D), F32), jax.ShapeDtypeStruct((8, D), F32)],
        compiler_params=_cparams(dimension_semantics=("arbitrary",)),
    )(ids, dp, x, ctx, dx1, modx, modc, norm_w, *shards.arrays)


def _weight_grad_in(xm_t, dp, owners, name):
    lext = xm_t.shape[1]

    def body(own_ref, a_ref, b_ref, o_ref):
        o_ref[0] = _dot(a_ref[...], b_ref[...])

    grid_spec = pltpu.PrefetchScalarGridSpec(
        num_scalar_prefetch=1, grid=(4,),
        in_specs=[pl.BlockSpec((D, lext), lambda j, own: (0, 0)),
                  pl.BlockSpec((lext, WSH), lambda j, own: (0, own[j]))],
        out_specs=pl.BlockSpec((1, D, WSH), lambda j, own: (j, 0, 0)))
    return pl.pallas_call(
        body, name=name, grid_spec=grid_spec,
        out_shape=jax.ShapeDtypeStruct((4, D, WSH), F32),
        compiler_params=_cparams(dimension_semantics=("arbitrary",)),
    )(owners, xm_t, dp)


def _weight_grad_square(a_t, b):
    seq = a_t.shape[2]

    def body(a_ref, b_ref, o_ref):
        o_ref[:, 0] = _dot(a_ref[0], b_ref[0]).reshape(N_DEV, RSH, D)

    return pl.pallas_call(
        body, name="weight_grad_square", grid=(3,),
        in_specs=[pl.BlockSpec((1, D, seq), lambda t: (t, 0, 0)), pl.BlockSpec((1, seq, D), lambda t: (t, 0, 0))],
        out_specs=pl.BlockSpec((N_DEV, 1, RSH, D), lambda t: (0, t, 0, 0)),
        out_shape=jax.ShapeDtypeStruct((N_DEV, 3, RSH, D), F32),
        compiler_params=_cparams(dimension_semantics=("arbitrary",)),
    )(a_t, b)


def _rope_tables(seq, ctx_len, token):
    rows = seq // GRID_W
    row = jnp.repeat(jnp.arange(rows), GRID_W).astype(F32)
    col = jnp.tile(jnp.arange(GRID_W), rows).astype(F32)
    nf = DK // 4
    inv = ROPE_BASE ** (-jnp.arange(nf, dtype=F32) / nf)
    ang = _tie(jnp.concatenate([row[:, None] * inv, col[:, None] * inv], axis=-1), token)
    cos, sin = jnp.cos(ang), jnp.sin(ang)
    cos_t = jnp.tile(cos, (1, 4))
    sin_t = jnp.tile(jnp.concatenate([-sin, sin], axis=-1), (1, 2))
    cos_t = jnp.concatenate([cos_t, jnp.ones((ctx_len, LANES), F32)], axis=0)
    sin_t = jnp.concatenate([sin_t, jnp.zeros((ctx_len, LANES), F32)], axis=0)
    return cos_t, sin_t


def _tie(value, token):
    return value + token[0, 0].astype(value.dtype)


def _pad_lanes(a):
    return jnp.pad(a, ((0, 0), (0, D - a.shape[1])))


def _rows(total, *parts):
    width = max(a.shape[1] for _, a in parts)
    out = None
    for row, a in parts:
        padded = jnp.pad(a, ((row, total - row - a.shape[0]), (0, width - a.shape[1])))
        out = padded if out is None else out + padded
    return out


def kernel(x, c, ctx, c_ctx, norm_w, ada_w, ada_b, w_in, conv_w, conv_b, decay_logit, gn_w, w_a, w_b, w_out, final_norm_w, loss_target, m_c_ctx, m_norm_w, m_ada_w, m_ada_b, m_w_in, m_conv_w, m_conv_b, m_decay_logit, m_gn_w, m_w_a, m_w_b, m_w_out, m_final_norm_w, v_c_ctx, v_norm_w, v_ada_w, v_ada_b, v_w_in, v_conv_w, v_conv_b, v_decay_logit, v_gn_w, v_w_a, v_w_b, v_w_out, v_final_norm_w):
    xi, yi, ci = _coords()
    me = 4 * xi + 2 * yi + ci
    chip = 2 * xi + yi
    seq, ctx_len = x.shape[1], ctx.shape[1]
    assert seq % TM == 0 and seq % RET_C == 0 and ctx_len == TM and seq % GRID_W == 0
    csh = D // N_DEV

    blk = jnp.pad(c, ((0, 7), (0, 0))) + jnp.pad(conv_w[0], ((1, 4), (0, D - csh)))
    got, _ = _all_gather_small(blk, "gather_cond")
    conv_w_all = got[:, 1:4, 0:csh].transpose(1, 0, 2).reshape(3, D)
    c16 = _rows(16, (0, got[:, 0, :]), (N_DEV, c_ctx[None]))
    ada_b_sh = lax.dynamic_slice(ada_b, (0, me * ADA_SH), (1, ADA_SH))
    mod_sh, act16, lg = _modulation(c16, ada_w[0], ada_b_sh, decay_logit[0])
    mod_all, small_done = _all_gather_small(mod_sh, "gather_mod")
    mod_all = mod_all.transpose(1, 0, 2).reshape(16, 3 * D)
    modx = lax.dynamic_slice(mod_all, (me, 0), (1, 3 * D)).reshape(3, D)
    modc = mod_all[8].reshape(3, D)

    x2, tgt = x[0], loss_target[0]
    wblock = lambda n: jax.ShapeDtypeStruct((n, D, WSH), BF16)
    ids_of = lambda ks: jnp.stack([jnp.bitwise_xor(me, k) for k in ks]).astype(jnp.int32)
    own_shard = _tie(w_in[0].astype(BF16)[None], small_done)
    to_sib = _Exchange("gather_in_sibling", [own_shard], [wblock(1)], 1, _send_plan((1,)))
    to_nbr = _Exchange("gather_in_neighbours", [_tie(own_shard, to_sib.token)], [wblock(2)], 2, _send_plan((4, 2)))
    cos_t, sin_t = _rope_tables(seq, ctx_len, to_nbr.token)
    modx = _tie(modx, to_nbr.token)
    (own_shard,), (w_sib,) = to_sib.wait(modx)
    first = _Shards([own_shard, w_sib], [(0, 0), (1, 0)])
    p_ext, xm, xm_t = _in_projection_first(x2, ctx[0], modx, modc, norm_w, cos_t, sin_t, first, ids_of((0, 1)))
    _, (w_nbr,) = to_nbr.wait(xm_t)
    fwd_nbr = _Exchange("gather_in_forward_neighbours", [w_nbr], [wblock(2)], 2, _forward_plan(2))
    to_diag = _Exchange("gather_in_diagonal", [_tie(own_shard, fwd_nbr.token)], [wblock(1)], 1, _send_plan((6,)))
    (w_nbr,), (w_nbr_sib,) = fwd_nbr.wait(to_diag.token)
    second = _Shards([w_nbr, w_nbr_sib], [(0, 0), (0, 1), (1, 0), (1, 1)])
    p_ext = _in_projection_more(p_ext, xm, cos_t, sin_t, second, ids_of((4, 2, 5, 3)), "in_projection_neighbours")
    _, (w_diag,) = to_diag.wait(p_ext)
    fwd_diag = _Exchange("gather_in_forward_diagonal", [w_diag], [wblock(1)], 1, _forward_plan(1))
    sq_shards = [_tie(w[0].astype(BF16), fwd_diag.token) for w in (w_a, w_b, w_out)]
    ag_sq = _Exchange("gather_square", sq_shards, [jax.ShapeDtypeStruct((N_DEV, RSH, D), BF16)] * 3,
                      3 * (N_DEV - 1), _gather_plan)
    (w_diag,), (w_diag_sib,) = fwd_diag.wait(ag_sq.token)
    third = _Shards([w_diag, w_diag_sib], [(0, 0), (1, 0)])
    p_ext = _in_projection_more(p_ext, xm, cos_t, sin_t, third, ids_of((6, 7)), "in_projection_diagonal")
    w_shards, w_ids = first + second + third, ids_of((0, 1, 4, 2, 5, 3, 6, 7))

    rn, rstd = _retention_forward(p_ext, lg, seq, ctx_len)
    _, sq_lands = ag_sq.wait(rstd)
    w_a_all, w_b_all, w_out_all = (
        lax.dynamic_update_slice(land, shard[None], (me, 0, 0)).reshape(D, D) for land, shard in zip(sq_lands, sq_shards))
    vecs = _rows(8, (0, modx[2:3]), (1, conv_w_all), (4, conv_b), (5, gn_w), (6, final_norm_w[None]))
    dx1, dpa, dconv, dret, op_at, op_b, part_m = _merge(p_ext, rn, rstd, x2, tgt, w_a_all, w_b_all, w_out_all, vecs, seq)

    j4 = jnp.arange(4, dtype=jnp.int32)
    owners = (2 * jnp.bitwise_xor(chip, j4) + ci).astype(jnp.int32)
    owners_sib = (2 * jnp.bitwise_xor(chip, j4) + (1 - ci)).astype(jnp.int32)
    gw_sq = _weight_grad_square(op_at, op_b).reshape(N_DEV, 3 * RSH, D)
    rs_sq_pair = _Exchange("rs_square_pair", [gw_sq], [jax.ShapeDtypeStruct((4, 3 * RSH, D), F32)], 4,
                           _pair_plan(lambda j, chip_, c_: 2 * jnp.bitwise_xor(chip_, j) + (1 - c_)))
    dq, dk, dv, dlg = _retention_backward(p_ext, dret, _tie(lg, rs_sq_pair.token), cos_t, sin_t, seq, ctx_len)
    (gw_sq,), (r1_sq,) = rs_sq_pair.wait(dlg)
    own_sq, send_sq = _pair_sum(gw_sq, r1_sq, owners, "pair_sum_square")
    rs_sq_chips = _Exchange("rs_square_chips", [send_sq], [jax.ShapeDtypeStruct((3, 3 * RSH, D), BF16)], 3, _chips_plan)
    dp, part_c = _conv_backward(p_ext, dconv, dpa, dq, dk, dv, _tie(vecs, rs_sq_chips.token), seq)
    gw_sib = _weight_grad_in(xm_t, dp, owners_sib, "weight_grad_in_sibling")
    rs_in_pair = _Exchange("rs_in_pair", [gw_sib], [jax.ShapeDtypeStruct((4, D, WSH), F32)], 4,
                           _pair_plan(lambda j, chip_, c_: j))
    gw_own = _weight_grad_in(xm_t, dp, _tie(owners, rs_in_pair.token), "weight_grad_in_own")
    _, (r1_in,) = rs_in_pair.wait(gw_own)
    own_in, send_in = _pair_sum(gw_own, r1_in, j4, "pair_sum_in")
    rs_in_chips = _Exchange("rs_in_chips", [send_in], [jax.ShapeDtypeStruct((3, D, WSH), BF16)], 3, _chips_plan)
    grad_x, part_i = _input_backward(dp, w_shards, w_ids, x2, ctx[0], dx1, _tie(modx, rs_in_chips.token), modc, norm_w)

    dl = dlg[:, 0, 0:4]
    dlg_row = jnp.pad(dl[:, 0::2].reshape(1, N_HEADS), ((0, 0), (0, D - N_HEADS))) + jnp.pad(
        dl[:, 1::2].reshape(1, N_HEADS), ((0, 0), (N_HEADS, D - 2 * N_HEADS)))
    partials = _rows(16, (0, part_m[0:1]), (1, part_i[2:3]), (2, part_c[3:4]), (3, part_m[2:3]), (4, part_c[0:3]),
                     (7, part_i[0:2]), (9, part_m[1:2]), (10, part_i[3:5]), (13, part_m[3:4]), (14, dlg_row))
    got, _ = _all_gather_small(partials, "gather_partials")
    tot = _sum_devices(got, "sum_partials")
    dmodc = tot[10:13].reshape(1, 3 * D)
    dmod16 = _rows(16, (0, got[:, 7:10, :].reshape(N_DEV, 3 * D)), (N_DEV, dmodc))
    dmod16 = lax.dynamic_slice(dmod16, (0, me * ADA_SH), (16, ADA_SH))
    dmodc8 = _rows(8, (0, dmod16[8:9]))
    g_ada_w, d_ada_w, nm_ada_w, nv_ada_w, cctx_part = _ada_backward(
        act16.T, dmod16, dmodc8, ada_w[0], m_ada_w[0], v_ada_w[0])
    dact_ctx = _sum_devices(_all_gather_small(cctx_part, "gather_cctx")[0], "sum_cctx")

    view = {"c_ctx": (1, D), "final_norm_w": (1, D), "decay_logit": (1, 2 * N_HEADS), "conv_w": (3, csh)}
    given = {"c_ctx": (c_ctx, m_c_ctx, v_c_ctx), "norm_w": (norm_w, m_norm_w, v_norm_w),
             "conv_b": (conv_b, m_conv_b, v_conv_b), "gn_w": (gn_w, m_gn_w, v_gn_w),
             "final_norm_w": (final_norm_w, m_final_norm_w, v_final_norm_w), "ada_b": (ada_b, m_ada_b, v_ada_b),
             "decay_logit": (decay_logit, m_decay_logit, v_decay_logit), "conv_w": (conv_w, m_conv_w, v_conv_w)}
    small_in = [tuple(a.reshape(view.get(name, a.shape)) for a in given[name]) for name in SMALL_PARAMS]
    conv_w_grad = lax.dynamic_slice(tot, (4, me * csh), (3, csh))
    small_out = _adam_small(tot, dact_ctx, conv_w_grad, small_in)
    results = {name: [o.reshape(given[name][0].shape) for o in outs_]
               for name, outs_ in zip(SMALL_PARAMS, small_out)}

    _, (r2_sq,) = rs_sq_chips.wait(small_out[0][0])
    _, (r2_in,) = rs_in_chips.wait(small_out[0][0])
    results["w_in"] = [o[None] for o in _adam_sharded(w_in[0], m_w_in[0], v_w_in[0], own_in, r2_in, "adam_w_in")]
    square = _adam_square([(w_a[0], m_w_a[0], v_w_a[0]), (w_b[0], m_w_b[0], v_w_b[0]),
                           (w_out[0], m_w_out[0], v_w_out[0])], own_sq, r2_sq)
    for name, outs_ in zip(("w_a", "w_b", "w_out"), square):
        results[name] = [o[None] for o in outs_]
    results["ada_w"] = [o[None] for o in (g_ada_w, d_ada_w, nm_ada_w, nv_ada_w)]

    order = ("c_ctx", "norm_w", "ada_w", "ada_b", "w_in", "conv_w", "conv_b", "decay_logit", "gn_w",
             "w_a", "w_b", "w_out", "final_norm_w")
    outs = [results[name][kind] for kind in range(4) for name in order]
    return (tot[13, 0], grad_x[None], *outs)
```

```python
import functools
import math

import jax
import jax.numpy as jnp
from jax import lax
from jax.experimental import pallas as pl
from jax.experimental.pallas import tpu as pltpu

F32 = jnp.float32
BF16 = jnp.bfloat16
MESH = pl.DeviceIdType.MESH
_HBM_SPEC = pl.BlockSpec(memory_space=pltpu.HBM)
_SEM_SPEC = pl.BlockSpec(memory_space=pltpu.SEMAPHORE)
_EFFECT = pltpu.SideEffectType.DATAFLOW_SIDE_EFFECTING

N_DEV = 8
D = 1024
N_HEADS = 8
DK = 64
DV = 128
GRID_W = 64
ROPE_BASE = 10000.0
EPS = 1e-6
PW = 9 * D
WSH = PW // N_DEV
RSH = D // N_DEV
ADA_SH = 3 * D // N_DEV
TM = 256
RET_C = 256
HALO = 16
LANES = 128
VMEM_LIMIT = 60 * 1024 * 1024

ADAM_LR = 0.001
ADAM_B1 = 0.9
ADAM_B2 = 0.999
ADAM_EPS = 1e-08
ADAM_WD = 0.01
ADAM_STEP = 10

CB_H, CB_BG, CB_CG, CB_ZA, CB_QK, CB_V, CB_ZB, CB_GA, CB_GB = range(9)


def _cparams(**kw):
    return pltpu.CompilerParams(vmem_limit_bytes=VMEM_LIMIT, **kw)


def _dot(a, b):
    return jnp.dot(a, b, preferred_element_type=F32)


def _dot_nt(a, b):
    return lax.dot_general(a, b, (((1,), (1,)), ((), ())), preferred_element_type=F32)


def _dot_tn(a, b):
    return lax.dot_general(a, b, (((0,), (0,)), ((), ())), preferred_element_type=F32)


def _sigmoid(z):
    return 0.5 * jnp.tanh(0.5 * z) + 0.5


def _row_tile(rows):
    return TM if rows % TM == 0 else rows


def _coords():
    return lax.axis_index("x"), lax.axis_index("y"), lax.axis_index("c")


def _flip(v, bit):
    return 1 - v if bit else v


def _all_gather_small(blk, name):
    rows, cols = blk.shape

    def body(x_ref, out_ref, done, send_sems, recv_sems, local_sem):
        x, y, c = _coords()
        me = 4 * x + 2 * y + c
        mine = pltpu.make_async_copy(x_ref, out_ref.at[me], local_sem)
        mine.start()

        def copy(k, slot):
            peer = (_flip(x, k & 4), _flip(y, k & 2), _flip(c, k & 1))
            return pltpu.make_async_remote_copy(
                src_ref=x_ref, dst_ref=out_ref.at[slot], send_sem=send_sems.at[k - 1],
                recv_sem=recv_sems.at[k - 1], device_id=peer, device_id_type=MESH)

        for k in range(1, N_DEV):
            copy(k, me).start()
        for k in range(1, N_DEV):
            copy(k, jnp.bitwise_xor(me, k)).wait_recv()
        for k in range(1, N_DEV):
            copy(k, me).wait_send()
        mine.wait()
        done[...] = jnp.zeros((8, LANES), F32)

    vmem = pl.BlockSpec(memory_space=pltpu.VMEM)
    return pl.pallas_call(
        body, name=name,
        out_shape=[jax.ShapeDtypeStruct((N_DEV, rows, cols), blk.dtype), jax.ShapeDtypeStruct((8, LANES), F32)],
        in_specs=[vmem], out_specs=[vmem, vmem],
        scratch_shapes=[pltpu.SemaphoreType.DMA((N_DEV - 1,)), pltpu.SemaphoreType.DMA((N_DEV - 1,)),
                        pltpu.SemaphoreType.DMA],
    )(blk)


class _Exchange:
    def __init__(self, name, srcs, land_shapes, n_copies, plan):
        self.name, self.plan, self.n_copies = name, plan, n_copies
        self.n_src, self.n_land = len(srcs), len(land_shapes)
        hbm = lambda a: pltpu.HBM(a.shape, a.dtype)
        n = self.n_src + self.n_land
        lands = [pltpu.with_memory_space_constraint(lax.empty(s.shape, s.dtype), pltpu.HBM) for s in land_shapes]
        srcs = [pltpu.with_memory_space_constraint(s, pltpu.HBM) for s in srcs]

        def body(*refs):
            send_sems, recv_sems = refs[n], refs[n + 1]
            for cp in self._copies(refs, send_sems, recv_sems):
                cp.start()
            refs[-1][...] = jnp.zeros((8, LANES), F32)

        outs = pl.pallas_call(
            body, name=name + "_start",
            out_shape=(pltpu.SemaphoreType.DMA((n_copies,)), pltpu.SemaphoreType.DMA((n_copies,)),
                       *[hbm(a) for a in srcs], *[hbm(a) for a in lands], jax.ShapeDtypeStruct((8, LANES), F32)),
            in_specs=[_HBM_SPEC] * n,
            out_specs=(_SEM_SPEC, _SEM_SPEC, *[_HBM_SPEC] * n, pl.BlockSpec(memory_space=pltpu.VMEM)),
            input_output_aliases={i: 2 + i for i in range(n)},
            compiler_params=pltpu.CompilerParams(has_side_effects=_EFFECT),
        )(*srcs, *lands)
        self.send_sems, self.recv_sems = outs[0], outs[1]
        self.buffers = list(outs[2:2 + n])
        self.token = outs[-1]

    def _copies(self, refs, send_sems, recv_sems):
        src_refs, land_refs = refs[:self.n_src], refs[self.n_src:self.n_src + self.n_land]
        return [pltpu.make_async_remote_copy(src_ref=s, dst_ref=d, send_sem=send_sems.at[k], recv_sem=recv_sems.at[k],
                                             device_id=dev, device_id_type=MESH)
                for k, (s, d, dev) in enumerate(self.plan(src_refs, land_refs))]

    def wait(self, after):
        n = self.n_src + self.n_land

        def body(*refs):
            for cp in self._copies(refs, refs[n], refs[n + 1]):
                cp.wait_send()
                cp.wait_recv()

        outs = pl.pallas_call(
            body, name=self.name + "_wait",
            out_shape=tuple(pltpu.HBM(b.shape, b.dtype) for b in self.buffers),
            in_specs=[_HBM_SPEC] * n + [_SEM_SPEC, _SEM_SPEC, pl.BlockSpec(memory_space=pl.ANY)],
            out_specs=tuple([_HBM_SPEC] * n),
            input_output_aliases={i: i for i in range(n)},
            compiler_params=pltpu.CompilerParams(has_side_effects=_EFFECT),
        )(*self.buffers, self.send_sems, self.recv_sems, after)
        return list(outs[:self.n_src]), list(outs[self.n_src:])


def _pair_plan(src_index):
    def plan(srcs, lands):
        x, y, c = _coords()
        chip = 2 * x + y
        return [(srcs[0].at[src_index(j, chip, c)], lands[0].at[j], (x, y, 1 - c)) for j in range(4)]
    return plan


def _chips_plan(srcs, lands):
    x, y, c = _coords()
    return [(srcs[0].at[j - 1], lands[0].at[j - 1], (_flip(x, j & 2), _flip(y, j & 1), c)) for j in range(1, 4)]


def _peer(k):
    x, y, c = _coords()
    return _flip(x, k & 4), _flip(y, k & 2), _flip(c, k & 1)


def _send_plan(ks):
    def plan(srcs, lands):
        return [(srcs[0].at[0], lands[0].at[p], _peer(k)) for p, k in enumerate(ks)]
    return plan


def _forward_plan(n):
    def plan(srcs, lands):
        return [(srcs[0].at[p], lands[0].at[p], _peer(1)) for p in range(n)]
    return plan


def _gather_plan(srcs, lands):
    x, y, c = _coords()
    me = 4 * x + 2 * y + c
    return [(srcs[a], lands[a].at[me], (_flip(x, k & 4), _flip(y, k & 2), _flip(c, k & 1)))
            for a in range(len(srcs)) for k in range(1, N_DEV)]


def _pair_sum(grad, recv, owners, name):
    _, rows, cols = grad.shape
    tr = _row_tile(rows)

    def body(own_ref, g_ref, r_ref, mine_ref, send_ref):
        j = pl.program_id(1)
        total = g_ref[...] + r_ref[...]

        @pl.when(j == 0)
        def _():
            mine_ref[...] = total

        @pl.when(j > 0)
        def _():
            send_ref[...] = total.astype(BF16)

    grid_spec = pltpu.PrefetchScalarGridSpec(
        num_scalar_prefetch=1, grid=(rows // tr, 4),
        in_specs=[pl.BlockSpec((1, tr, cols), lambda i, j, own: (own[j], i, 0)),
                  pl.BlockSpec((1, tr, cols), lambda i, j, own: (j, i, 0))],
        out_specs=[pl.BlockSpec((1, tr, cols), lambda i, j, own: (0, i, 0)),
                   pl.BlockSpec((1, tr, cols), lambda i, j, own: (jnp.maximum(j - 1, 0), i, 0))])
    return pl.pallas_call(
        body, name=name, grid_spec=grid_spec,
        out_shape=[jax.ShapeDtypeStruct((1, rows, cols), F32), jax.ShapeDtypeStruct((3, rows, cols), BF16)],
        compiler_params=_cparams(dimension_semantics=("arbitrary", "arbitrary")),
    )(owners, grad, recv)


def _modulation(c16, ada_w_sh, ada_b_sh, decay_logit):
    def body(c_ref, w_ref, b_ref, dl_ref, mod_ref, act_ref, lg_ref):
        cv = c_ref[...]
        act = cv * _sigmoid(cv)
        act_ref[...] = act
        mod_ref[...] = jnp.dot(act, w_ref[...], preferred_element_type=F32,
                               precision=lax.Precision.HIGHEST) + b_ref[...]
        z = dl_ref[...]
        lg_ref[...] = jnp.minimum(z, 0.0) - jnp.log(1.0 + jnp.exp(-jnp.abs(z)))

    return pl.pallas_call(
        body, name="modulation",
        out_shape=[jax.ShapeDtypeStruct((16, ADA_SH), F32), jax.ShapeDtypeStruct((16, D), F32),
                   jax.ShapeDtypeStruct(decay_logit.shape, F32)],
        compiler_params=_cparams(),
    )(c16, ada_w_sh, ada_b_sh, decay_logit)


def _adam_update(w, g, m, v):
    m2 = ADAM_B1 * m + (1.0 - ADAM_B1) * g
    v2 = ADAM_B2 * v + (1.0 - ADAM_B2) * (g * g)
    m_hat = m2 / (1.0 - ADAM_B1 ** ADAM_STEP)
    v_hat = v2 / (1.0 - ADAM_B2 ** ADAM_STEP)
    delta = -ADAM_LR * (m_hat / (jnp.sqrt(v_hat) + ADAM_EPS) + ADAM_WD * w)
    return delta, m2, v2


def _adam_sharded(w, m, v, own, recv, name):
    rows, cols = w.shape
    tr = _row_tile(rows)

    def body(w_ref, m_ref, v_ref, g0, g1, g2, g3, g_ref, d_ref, m_out, v_out):
        g = ((g0[0] + g1[0].astype(F32)) + g2[0].astype(F32)) + g3[0].astype(F32)
        delta, m2, v2 = _adam_update(w_ref[...], g, m_ref[...], v_ref[...])
        g_ref[...] = g
        d_ref[...] = delta
        m_out[...] = m2
        v_out[...] = v2

    flat = pl.BlockSpec((tr, cols), lambda i: (i, 0))
    part = lambda j: pl.BlockSpec((1, tr, cols), lambda i, j=j: (j, i, 0))
    return pl.pallas_call(
        body, name=name, grid=(rows // tr,),
        in_specs=[flat, flat, flat, part(0), part(0), part(1), part(2)],
        out_specs=[flat] * 4,
        out_shape=[jax.ShapeDtypeStruct((rows, cols), F32)] * 4,
        compiler_params=_cparams(dimension_semantics=("arbitrary",)),
    )(w, m, v, own, recv, recv, recv)


def _ada_backward(act_t, dmod16, dmodc8, ada_w_sh, m, v):
    def body(at_ref, dm_ref, dc_ref, w_ref, m_ref, v_ref, g_ref, d_ref, m_out, v_out, pc_ref):
        g = jnp.dot(at_ref[...], dm_ref[...], preferred_element_type=F32, precision=lax.Precision.HIGHEST)
        w = w_ref[...]
        delta, m2, v2 = _adam_update(w, g, m_ref[...], v_ref[...])
        g_ref[...] = g
        d_ref[...] = delta
        m_out[...] = m2
        v_out[...] = v2
        pc_ref[...] = lax.dot_general(dc_ref[...], w, (((1,), (1,)), ((), ())), preferred_element_type=F32,
                                      precision=lax.Precision.HIGHEST)

    return pl.pallas_call(
        body, name="ada_backward",
        out_shape=[jax.ShapeDtypeStruct((D, ADA_SH), F32)] * 4 + [jax.ShapeDtypeStruct((8, D), F32)],
        compiler_params=_cparams(),
    )(act_t, dmod16, dmodc8, ada_w_sh, m, v)


def _sum_devices(gathered, name):
    _, rows, cols = gathered.shape

    def body(g_ref, o_ref):
        acc = g_ref[0]
        for d in range(1, N_DEV):
            acc = acc + g_ref[d]
        o_ref[...] = acc

    return pl.pallas_call(body, name=name, out_shape=jax.ShapeDtypeStruct((rows, cols), F32),
                          compiler_params=_cparams())(gathered)


SMALL_PARAMS = ("c_ctx", "norm_w", "conv_b", "gn_w", "final_norm_w", "ada_b", "decay_logit", "conv_w")


def _adam_small(tot, dact_ctx, conv_w_grad, params):
    n = len(SMALL_PARAMS)

    def body(tot_ref, dact_ref, cwg_ref, *refs):
        ins, outs = refs[:3 * n], refs[3 * n:]
        nh = 2 * N_HEADS
        raw = {
            "c_ctx": dact_ref[0:1, :],
            "norm_w": tot_ref[1:2, :], "conv_b": tot_ref[2:3, :], "gn_w": tot_ref[3:4, :],
            "final_norm_w": tot_ref[0:1, :],
            "ada_b": jnp.concatenate([tot_ref[7 + r:8 + r, :] + tot_ref[10 + r:11 + r, :] for r in range(3)], axis=1),
            "decay_logit": tot_ref[14:15, 0:nh],
            "conv_w": cwg_ref[...],
        }
        for k, name in enumerate(SMALL_PARAMS):
            w, m, v = ins[3 * k][...], ins[3 * k + 1][...], ins[3 * k + 2][...]
            g = raw[name]
            if name == "c_ctx":
                s = _sigmoid(w)
                g = g * (s * (1.0 + w * (1.0 - s)))
            elif name == "decay_logit":
                g = g * (1.0 - _sigmoid(w))
            delta, m2, v2 = _adam_update(w, g, m, v)
            for ref, val in zip(outs[4 * k:4 * k + 4], (g, delta, m2, v2)):
                ref[...] = val

    flat = [a for trio in params for a in trio]
    out_shape = [jax.ShapeDtypeStruct(trio[0].shape, F32) for trio in params for _ in range(4)]
    outs = pl.pallas_call(body, name="adam_small", out_shape=out_shape, compiler_params=_cparams())(
        tot, dact_ctx, conv_w_grad, *flat)
    return [outs[4 * k:4 * k + 4] for k in range(n)]


def _adam_square(params, own, recv):
    def body(own_ref, recv_ref, *refs):
        ins, outs = refs[:9], refs[9:]
        for k in range(3):
            rows = slice(k * RSH, (k + 1) * RSH)
            g = own_ref[0, rows, :]
            for j in range(3):
                g = g + recv_ref[j, rows, :].astype(F32)
            delta, m2, v2 = _adam_update(ins[3 * k][...], g, ins[3 * k + 1][...], ins[3 * k + 2][...])
            for ref, val in zip(outs[4 * k:4 * k + 4], (g, delta, m2, v2)):
                ref[...] = val

    flat = [a for trio in params for a in trio]
    outs = pl.pallas_call(body, name="adam_square", out_shape=[jax.ShapeDtypeStruct((RSH, D), F32)] * 12,
                          compiler_params=_cparams())(own, recv, *flat)
    return [outs[4 * k:4 * k + 4] for k in range(3)]


def _rope(t, cos, sin):
    lane = lax.broadcasted_iota(jnp.int32, (1, LANES), 1)
    first_half = jnp.bitwise_and(lane, DK // 2) == 0
    partner = jnp.where(first_half, pltpu.roll(t, LANES - DK // 2, 1), pltpu.roll(t, DK // 2, 1))
    return t * cos + partner * sin


class _Shards:
    def __init__(self, arrays, picks):
        self.arrays, self.picks = list(arrays), list(picks)

    def specs(self, index_args):
        out = []
        for a, arr in enumerate(self.arrays):
            steps = [p for p, (ai, _) in enumerate(self.picks) if ai == a]
            first, slot0, last = steps[0], self.picks[steps[0]][1], arr.shape[0] - 1
            out.append(pl.BlockSpec(
                (1, D, WSH), lambda *args, first=first, slot0=slot0, last=last:
                (jnp.clip(index_args(*args) - first + slot0, 0, last), 0, 0)))
        return out

    def __add__(self, other):
        shift = len(self.arrays)
        return _Shards(self.arrays + other.arrays, self.picks + [(a + shift, s) for a, s in other.picks])


def _load_w_in(step, ids_ref, shards, w_refs, w_vmem, sems):
    @pl.when(step == 0)
    def _():
        copies = []
        for p, (a, slot) in enumerate(shards.picks):
            col = pl.multiple_of(ids_ref[p] * WSH, LANES)
            copies.append(pltpu.make_async_copy(w_refs[a].at[slot], w_vmem.at[:, pl.ds(col, WSH)], sems.at[p]))
        for cp in copies:
            cp.start()
        for cp in copies:
            cp.wait()


def _project_columns(xmb, w, shard, cos, sin, p_ref):
    acc = _dot(xmb, w)
    q_lo, k_lo, k_hi = CB_QK * D, CB_QK * D + N_HEADS * DK, CB_V * D
    with_q, with_k = q_lo // WSH, k_lo // WSH
    assert (with_q + 1) * WSH == k_lo and k_hi <= (with_k + 1) * WSH

    @pl.when(jnp.logical_and(shard != with_q, shard != with_k))
    def _():
        p_ref[...] = acc.astype(BF16)

    @pl.when(shard == with_q)
    def _():
        plain = q_lo - with_q * WSH
        p_ref[:, 0:plain] = acc[:, 0:plain].astype(BF16)
        for col in range(plain, WSH, LANES):
            p_ref[:, col:col + LANES] = _rope(acc[:, col:col + LANES], cos, sin).astype(BF16)

    @pl.when(shard == with_k)
    def _():
        rotary = k_hi - k_lo
        for col in range(0, rotary, LANES):
            p_ref[:, col:col + LANES] = _rope(acc[:, col:col + LANES] * (DK ** -0.5), cos, sin).astype(BF16)
        p_ref[:, rotary:WSH] = acc[:, rotary:WSH].astype(BF16)


def _prenorm(x, ctx, modx, modc, norm_w):
    n_x_tiles = x.shape[0] // TM
    lext = x.shape[0] + ctx.shape[0]

    def body(x_ref, ctx_ref, mx_ref, mc_ref, nw_ref, xm_ref, xmt_ref):
        is_ctx = pl.program_id(0) >= n_x_tiles
        x = jnp.where(is_ctx, ctx_ref[...], x_ref[...])
        r = lax.rsqrt(jnp.mean(x * x, axis=-1, keepdims=True) + EPS)
        sh = jnp.where(is_ctx, mc_ref[0:1, :], mx_ref[0:1, :])
        sc = jnp.where(is_ctx, mc_ref[1:2, :], mx_ref[1:2, :])
        xm = (x * r * nw_ref[...]) * (1.0 + sc) + sh
        xm_ref[...] = xm.astype(BF16)
        xmt_ref[...] = xm.T.astype(BF16)

    full = lambda a: pl.BlockSpec(a.shape, lambda i: (0,) * a.ndim)
    return pl.pallas_call(
        body, name="prenorm", grid=(lext // TM,),
        in_specs=[pl.BlockSpec((TM, D), lambda i: (jnp.minimum(i, n_x_tiles - 1), 0)), full(ctx),
                  full(modx), full(modc), full(norm_w)],
        out_specs=[pl.BlockSpec((TM, D), lambda i: (i, 0)), pl.BlockSpec((D, TM), lambda i: (0, i))],
        out_shape=[jax.ShapeDtypeStruct((lext, D), BF16), jax.ShapeDtypeStruct((D, lext), BF16)],
        compiler_params=_cparams(dimension_semantics=("arbitrary",)),
    )(x, ctx, modx, modc, norm_w)


def _in_projection(p_ext, xm, cos_t, sin_t, shards, ids, name):
    lext = xm.shape[0]
    n_w = len(shards.arrays)
    tr = lext // 4

    def body(ids_ref, xm_ref, cos_ref, sin_ref, *rest):
        w_refs, p_ref = rest[:n_w], rest[-1]
        j = pl.program_id(0)
        for p, (a, _) in enumerate(shards.picks):
            @pl.when(j == p)
            def _():
                _project_columns(xm_ref[...], w_refs[a][0], ids_ref[p], cos_ref[...], sin_ref[...], p_ref)

    row = lambda w: pl.BlockSpec((tr, w), lambda j, i, ids: (i, 0))
    grid_spec = pltpu.PrefetchScalarGridSpec(
        num_scalar_prefetch=1, grid=(len(shards.picks), lext // tr),
        in_specs=[row(D), row(LANES), row(LANES)] + shards.specs(lambda j, i, ids: j)
        + ([] if p_ext is None else [pl.BlockSpec(memory_space=pl.ANY)]),
        out_specs=pl.BlockSpec((tr, WSH), lambda j, i, ids: (i, ids[j])))
    return pl.pallas_call(
        body, name=name, grid_spec=grid_spec,
        out_shape=jax.ShapeDtypeStruct((lext, PW), BF16),
        input_output_aliases={} if p_ext is None else {4 + n_w: 0},
        compiler_params=_cparams(dimension_semantics=("arbitrary", "arbitrary")),
    )(ids, xm, cos_t, sin_t, *shards.arrays, *([] if p_ext is None else [p_ext]))


def _decay_tables(lgf, lgb, n):
    i = lax.broadcasted_iota(jnp.int32, (n, 1), 0).astype(F32)
    return dict(i=i, k_f=jnp.exp(lgf * (n - 1.0 - i)), k_b=jnp.exp(lgb * i),
                q_f=jnp.exp(lgf * (i + 1.0)), q_b=jnp.exp(lgb * (n - i)))


def _decay_matrix(lgf, lgb, n, transposed=False):
    ii = lax.broadcasted_iota(jnp.int32, (n, n), 0)
    jj = lax.broadcasted_iota(jnp.int32, (n, n), 1)
    diff = (jj - ii if transposed else ii - jj).astype(F32)
    low = jnp.exp(lgf * jnp.maximum(diff, 0.0))
    up = jnp.exp(lgb * jnp.maximum(-diff, 0.0))
    return jnp.where(diff > 0, low, jnp.where(diff < 0, up, 2.0)), diff


def _cat_lanes(a, b):
    return jnp.concatenate([a.astype(BF16), b.astype(BF16)], axis=1)


def _retention_forward(p_ext, lg, seq, ctx_len):
    lext = seq + ctx_len
    n_chunks = seq // RET_C
    C = RET_C

    def body(lg_ref, q_ref, k_ref, v_ref, rn_ref, rstd_ref, kv_scr, sf_scr, sb_scr):
        pair = pl.program_id(0)
        lane = lax.broadcasted_iota(jnp.int32, (1, LANES), 1)
        heads = range(2)
        hmask = [(lane // DK == hh).astype(F32) for hh in heads]
        lgf = [lg_ref[0, 2 * pair + hh] for hh in heads]
        lgb = [lg_ref[1, 2 * pair + hh] for hh in heads]
        vs = [slice(hh * DV, (hh + 1) * DV) for hh in heads]
        t = [_decay_tables(lgf[hh], lgb[hh], C) for hh in heads]
        kc_all = k_ref[seq:lext, :].astype(F32)
        s0 = []
        for hh in heads:
            tc = _decay_tables(lgf[hh], lgb[hh], ctx_len)
            kc = kc_all * hmask[hh]
            s0.append(_dot_tn(_cat_lanes(kc * tc["k_f"], kc * tc["k_b"]), v_ref[seq:lext, vs[hh]]))

        def increments(c, carry):
            rows = pl.ds(pl.multiple_of(c * C, C), C)
            k_all = k_ref[rows, :].astype(F32)
            for hh in heads:
                k = k_all * hmask[hh]
                kv_scr[hh, c] = _dot_tn(_cat_lanes(k * t[hh]["k_f"], k * t[hh]["k_b"]), v_ref[rows, vs[hh]])
            return carry

        lax.fori_loop(0, n_chunks, increments, 0, unroll=2)
        gf_c = [jnp.exp(lgf[hh] * C) for hh in heads]
        gb_c = [jnp.exp(lgb[hh] * C) for hh in heads]

        def scan_f(c, s):
            for hh in heads:
                sf_scr[hh, c] = s[hh]
            return tuple(gf_c[hh] * s[hh] + kv_scr[hh, c, 0:LANES, :] for hh in heads)

        def scan_b(n, s):
            c = n_chunks - 1 - n
            for hh in heads:
                sb_scr[hh, c] = s[hh]
            return tuple(gb_c[hh] * s[hh] + kv_scr[hh, c, LANES:2 * LANES, :] for hh in heads)

        lax.fori_loop(0, n_chunks, scan_f, tuple(s0[hh][0:LANES] for hh in heads))
        lax.fori_loop(0, n_chunks, scan_b, tuple(s0[hh][LANES:2 * LANES] for hh in heads))
        dmat = [_decay_matrix(lgf[hh], lgb[hh], C)[0] for hh in heads]

        def outputs(c, carry):
            rows = pl.ds(pl.multiple_of(c * C, C), C)
            q_all = q_ref[rows, :].astype(F32)
            k = k_ref[rows, :]
            for hh in heads:
                q = q_all * hmask[hh]
                v = v_ref[rows, vs[hh]]
                s = _dot_nt(q.astype(BF16), k)
                inner = _dot((s * dmat[hh]).astype(BF16), v)
                states = jnp.concatenate([sf_scr[hh, c], sb_scr[hh, c]], axis=0).astype(BF16)
                ret = inner + _dot(_cat_lanes(q * t[hh]["q_f"], q * t[hh]["q_b"]), states)
                mu = jnp.mean(ret, axis=-1, keepdims=True)
                cen = ret - mu
                rstd = lax.rsqrt(jnp.mean(cen * cen, axis=-1, keepdims=True) + EPS)
                rn_ref[rows, vs[hh]] = cen * rstd
                rstd_ref[rows, vs[hh]] = jnp.broadcast_to(rstd, (C, DV))
            return carry

        lax.fori_loop(0, n_chunks, outputs, 0, unroll=2)

    qk0 = CB_QK * D // LANES
    return pl.pallas_call(
        body, name="retention_forward", grid=(N_HEADS // 2,),
        in_specs=[pl.BlockSpec(memory_space=pltpu.SMEM),
                  pl.BlockSpec((lext, LANES), lambda g: (0, qk0 + g)),
                  pl.BlockSpec((lext, LANES), lambda g: (0, qk0 + N_HEADS // 2 + g)),
                  pl.BlockSpec((lext, 2 * DV), lambda g: (0, CB_V * D // (2 * DV) + g))],
        out_specs=[pl.BlockSpec((seq, 2 * DV), lambda g: (0, g))] * 2,
        out_shape=[jax.ShapeDtypeStruct((seq, D), F32)] * 2,
        scratch_shapes=[pltpu.VMEM((2, n_chunks, 2 * LANES, DV), F32), pltpu.VMEM((2, n_chunks, LANES, DV), F32),
                        pltpu.VMEM((2, n_chunks, LANES, DV), F32)],
        compiler_params=_cparams(dimension_semantics=("arbitrary",)),
    )(lg, p_ext, p_ext, p_ext)


def _retention_backward(p_ext, dret, lg, cos_t, sin_t, seq, ctx_len):
    lext = seq + ctx_len
    n_chunks = seq // RET_C
    C = RET_C

    def body(lg_ref, q_ref, k_ref, v_ref, do_ref, cos_ref, sin_ref, dq_ref, dk_ref, dv_ref, dlg_ref,
             kv_scr, g_scr, sf_scr, sb_scr, gfn_scr, gbp_scr):
        pair = pl.program_id(0)
        lane = lax.broadcasted_iota(jnp.int32, (1, LANES), 1)
        heads = range(2)
        hmask = [(lane // DK == hh).astype(F32) for hh in heads]
        lgf = [lg_ref[0, 2 * pair + hh] for hh in heads]
        lgb = [lg_ref[1, 2 * pair + hh] for hh in heads]
        vs = [slice(hh * DV, (hh + 1) * DV) for hh in heads]
        t = [_decay_tables(lgf[hh], lgb[hh], C) for hh in heads]
        tc = [_decay_tables(lgf[hh], lgb[hh], ctx_len) for hh in heads]
        kc_all = k_ref[seq:lext, :].astype(F32)
        kc = [kc_all * hmask[hh] for hh in heads]
        vc = [v_ref[seq:lext, vs[hh]] for hh in heads]
        kc_cat = [_cat_lanes(kc[hh] * tc[hh]["k_f"], kc[hh] * tc[hh]["k_b"]) for hh in heads]
        s0 = [_dot_tn(kc_cat[hh], vc[hh]) for hh in heads]

        def increments(c, carry):
            rows = pl.ds(pl.multiple_of(c * C, C), C)
            k_all = k_ref[rows, :].astype(F32)
            q_all = q_ref[rows, :].astype(F32)
            for hh in heads:
                k, q = k_all * hmask[hh], q_all * hmask[hh]
                kv_scr[hh, c] = _dot_tn(_cat_lanes(k * t[hh]["k_f"], k * t[hh]["k_b"]), v_ref[rows, vs[hh]])
                g_scr[hh, c] = _dot_tn(_cat_lanes(q * t[hh]["q_f"], q * t[hh]["q_b"]), do_ref[rows, vs[hh]])
            return carry

        lax.fori_loop(0, n_chunks, increments, 0, unroll=2)
        gf_c = [jnp.exp(lgf[hh] * C) for hh in heads]
        gb_c = [jnp.exp(lgb[hh] * C) for hh in heads]

        def scan_f(c, s):
            for hh in heads:
                sf_scr[hh, c] = s[hh]
            return tuple(gf_c[hh] * s[hh] + kv_scr[hh, c, 0:LANES, :] for hh in heads)

        def scan_b(n, s):
            c = n_chunks - 1 - n
            for hh in heads:
                sb_scr[hh, c] = s[hh]
            return tuple(gb_c[hh] * s[hh] + kv_scr[hh, c, LANES:2 * LANES, :] for hh in heads)

        def scan_gf(n, carry):
            c = n_chunks - 1 - n
            for hh in heads:
                gfn_scr[hh, c] = carry[hh]
            return tuple(g_scr[hh, c, 0:LANES, :] + gf_c[hh] * carry[hh] for hh in heads)

        def scan_gb(c, carry):
            for hh in heads:
                gbp_scr[hh, c] = carry[hh]
            return tuple(g_scr[hh, c, LANES:2 * LANES, :] + gb_c[hh] * carry[hh] for hh in heads)

        lax.fori_loop(0, n_chunks, scan_f, tuple(s0[hh][0:LANES] for hh in heads))
        lax.fori_loop(0, n_chunks, scan_b, tuple(s0[hh][LANES:2 * LANES] for hh in heads))
        zero_state = jnp.zeros((LANES, DV), F32)
        gf_first = lax.fori_loop(0, n_chunks, scan_gf, (zero_state, zero_state))
        gb_last = lax.fori_loop(0, n_chunks, scan_gb, (zero_state, zero_state))

        dmat, w_f, w_b = [], [], []
        for hh in heads:
            dm, diff = _decay_matrix(lgf[hh], lgb[hh], C)
            dmat.append(dm)
            w_f.append(jnp.where(diff > 0, diff * dm, 0.0))
            w_b.append(jnp.where(diff < 0, -diff * dm, 0.0))

        def total(a):
            rows_, width = a.shape
            part = jnp.sum(a.reshape(rows_ // 8, 8, width), axis=0)
            return part[:, 0:LANES] + part[:, LANES:2 * LANES] if width == 2 * LANES else part

        def chunk_head(hh, c, rows, q_all, k_all, dlgf, dlgb):
            th = t[hh]
            qm = q_all * hmask[hh]
            km = k_all * hmask[hh]
            qb, kb = qm.astype(BF16), km.astype(BF16)
            v = v_ref[rows, vs[hh]]
            do = do_ref[rows, vs[hh]]
            s = _dot_nt(qb, kb)
            dsd = _dot_nt(do, v)
            ds = (dsd * dmat[hh]).astype(BF16)
            dq_in = _dot(ds, kb)
            dk_in = _dot_tn(ds, qb)
            dv_in = _dot_tn((s * dmat[hh]).astype(BF16), do)
            prod = s * dsd
            dlgf = dlgf + total(prod * w_f[hh])
            dlgb = dlgb + total(prod * w_b[hh])
            sf, sb = sf_scr[hh, c], sb_scr[hh, c]
            states = jnp.concatenate([sf, sb], axis=0).astype(BF16)
            dqc = _dot_nt(do, states)
            dqf = dqc[:, 0:LANES] * th["q_f"]
            dqb = dqc[:, LANES:2 * LANES] * th["q_b"]
            dq = (dq_in + dqf + dqb) * hmask[hh]
            dlgf = dlgf + total((th["i"] + 1.0) * (qm * dqf))
            dlgb = dlgb + total((C - th["i"]) * (qm * dqb))
            gfn, gbp = gfn_scr[hh, c], gbp_scr[hh, c]
            gstates = jnp.concatenate([gfn, gbp], axis=0).astype(BF16)
            dkc = _dot_nt(v, gstates)
            dkf = dkc[:, 0:LANES] * th["k_f"]
            dkb = dkc[:, LANES:2 * LANES] * th["k_b"]
            dk = dk_in + dkf + dkb
            dlgf = dlgf + total((C - 1.0 - th["i"]) * (km * dkf)) + C * gf_c[hh] * total(gfn * sf)
            dlgb = dlgb + total(th["i"] * (km * dkb)) + C * gb_c[hh] * total(gbp * sb)
            dv = dv_in + _dot(_cat_lanes(km * th["k_f"], km * th["k_b"]), gstates)
            dv_ref[rows, vs[hh]] = dv.astype(BF16)
            return dq, dk, dlgf, dlgb

        def chunk(c, carry):
            rows = pl.ds(pl.multiple_of(c * C, C), C)
            q_all = q_ref[rows, :].astype(F32)
            k_all = k_ref[rows, :].astype(F32)
            dq0, dk0, f0, b0 = chunk_head(0, c, rows, q_all, k_all, carry[0], carry[1])
            dq1, dk1, f1, b1 = chunk_head(1, c, rows, q_all, k_all, carry[2], carry[3])
            cos, sin = cos_ref[rows, :], sin_ref[rows, :]
            dq_ref[rows, :] = _rope(dq0 + dq1, cos, -sin).astype(BF16)
            dk_ref[rows, :] = (_rope(dk0 + dk1, cos, -sin) * (DK ** -0.5)).astype(BF16)
            return f0, b0, f1, b1

        zero = jnp.zeros((8, LANES), F32)
        sums = lax.fori_loop(0, n_chunks, chunk, (zero, zero, zero, zero), unroll=2)

        dlg = []
        dk_ctx = jnp.zeros((ctx_len, LANES), F32)
        for hh in heads:
            g0 = jnp.concatenate([gf_first[hh], gb_last[hh]], axis=0).astype(BF16)
            dkcc = _dot_nt(vc[hh], g0)
            dkcf = dkcc[:, 0:LANES] * tc[hh]["k_f"]
            dkcb = dkcc[:, LANES:2 * LANES] * tc[hh]["k_b"]
            dlgf = sums[2 * hh] + total((ctx_len - 1.0 - tc[hh]["i"]) * (kc[hh] * dkcf))
            dlgb = sums[2 * hh + 1] + total(tc[hh]["i"] * (kc[hh] * dkcb))
            dk_ctx = dk_ctx + (dkcf + dkcb) * (DK ** -0.5)
            dv_ref[seq:lext, vs[hh]] = _dot(kc_cat[hh], g0).astype(BF16)
            dlg += [jnp.sum(jnp.sum(a, axis=1, keepdims=True), axis=0, keepdims=True) for a in (dlgf, dlgb)]
        dk_ref[seq:lext, :] = dk_ctx.astype(BF16)
        dq_ref[seq:lext, :] = jnp.zeros((ctx_len, LANES), BF16)

        lane8 = lax.broadcasted_iota(jnp.int32, (8, LANES), 1)
        out = jnp.zeros((8, LANES), F32)
        for n, val in enumerate(dlg):
            out = jnp.where(lane8 == n, val, out)
        dlg_ref[0] = out

    qk0 = CB_QK * D // LANES
    q_spec = pl.BlockSpec((lext, LANES), lambda g: (0, qk0 + g))
    k_spec = pl.BlockSpec((lext, LANES), lambda g: (0, qk0 + N_HEADS // 2 + g))
    v_spec = pl.BlockSpec((lext, 2 * DV), lambda g: (0, CB_V * D // (2 * DV) + g))
    table = pl.BlockSpec((lext, LANES), lambda g: (0, 0))
    state = pltpu.VMEM((2, n_chunks, LANES, DV), F32)
    return pl.pallas_call(
        body, name="retention_backward", grid=(N_HEADS // 2,),
        in_specs=[pl.BlockSpec(memory_space=pltpu.SMEM), q_spec, k_spec, v_spec,
                  pl.BlockSpec((seq, 2 * DV), lambda g: (0, g)), table, table],
        out_specs=[pl.BlockSpec((lext, LANES), lambda g: (0, g)), pl.BlockSpec((lext, LANES), lambda g: (0, g)),
                   pl.BlockSpec((lext, 2 * DV), lambda g: (0, g)), pl.BlockSpec((1, 8, LANES), lambda g: (g, 0, 0))],
        out_shape=[jax.ShapeDtypeStruct((lext, N_HEADS * DK), BF16), jax.ShapeDtypeStruct((lext, N_HEADS * DK), BF16),
                   jax.ShapeDtypeStruct((lext, D), BF16), jax.ShapeDtypeStruct((N_HEADS // 2, 8, LANES), F32)],
        scratch_shapes=[pltpu.VMEM((2, n_chunks, 2 * LANES, DV), F32), pltpu.VMEM((2, n_chunks, 2 * LANES, DV), F32),
                        state, state, state, state],
        compiler_params=_cparams(dimension_semantics=("arbitrary",)),
    )(lg, p_ext, p_ext, p_ext, dret, cos_t, sin_t)


def _merge(p_ext, rn, rstd, x, target, w_a, w_b, w_out, vecs, seq):
    n_tiles = seq // TM
    hb = TM // HALO

    def body(h_ref, bg_ref, cg_ref, za_ref, zb_ref, ga_ref, gb_ref, hp_ref, hn_ref, cp_ref, cn_ref,
             rn_ref, rstd_ref, x_ref, t_ref, wa_ref, wb_ref, wo_ref, vec_ref,
             dx1_ref, dpa_ref, dconv_ref, dret_ref, at_ref, b_ref, part_ref):
        i = pl.program_id(0)
        f = lambda ref: ref[...].astype(F32)
        h, bg, cg, za, zb, ga, gb = f(h_ref), f(bg_ref), f(cg_ref), f(za_ref), f(zb_ref), f(ga_ref), f(gb_ref)
        gx, w0, w1, w2 = vec_ref[0:1, :], vec_ref[1:2, :], vec_ref[2:3, :], vec_ref[3:4, :]
        cb, gnw, fw = vec_ref[4:5, :], vec_ref[5:6, :], vec_ref[6:7, :]
        u = cg * h
        row = lax.broadcasted_iota(jnp.int32, (TM, 1), 0)
        u_prev = (f(cp_ref) * f(hp_ref))[HALO - 1:HALO, :]
        u_next = (f(cn_ref) * f(hn_ref))[0:1, :]
        u_prev = jnp.where(i == 0, 0.0, u_prev)
        u_next = jnp.where(i == n_tiles - 1, 0.0, u_next)
        u_up = jnp.where(row == 0, u_prev, pltpu.roll(u, 1, 0))
        u_dn = jnp.where(row == TM - 1, u_next, pltpu.roll(u, TM - 1, 0))
        conv = w0 * u_up + w1 * u + w2 * u_dn + cb
        sza = _sigmoid(za)
        silu_za = za * sza
        a_act = silu_za * bg * conv
        rn = rn_ref[...]
        szb = _sigmoid(zb)
        silu_zb = zb * szb
        rg = rn * gnw
        b_act = silu_zb * rg
        y_a = _dot(a_act.astype(BF16), wa_ref[...])
        y_b = _dot(b_act.astype(BF16), wb_ref[...])
        sga, sgb = _sigmoid(ga), _sigmoid(gb)
        mix = sga * y_a + sgb * y_b
        y = _dot(mix.astype(BF16), wo_ref[...])
        x1 = x_ref[...] + gx * y
        r1 = lax.rsqrt(jnp.mean(x1 * x1, axis=-1, keepdims=True) + EPS)
        xh1 = x1 * r1
        err = xh1 * fw - t_ref[...]
        loss = jnp.sum(jnp.sum(err * err, axis=1, keepdims=True), axis=0, keepdims=True) * (0.5 / D)
        dout = err * (1.0 / D)
        dxh = dout * fw
        dx1 = r1 * (dxh - xh1 * jnp.mean(dxh * xh1, axis=-1, keepdims=True))
        dx1_ref[...] = dx1
        dy = (dx1 * gx).astype(BF16)
        dmix = _dot_nt(dy, wo_ref[...])
        dya = (dmix * sga).astype(BF16)
        dyb = (dmix * sgb).astype(BF16)
        da = _dot_nt(dya, wa_ref[...])
        db = _dot_nt(dyb, wb_ref[...])
        dpa_ref[:, 0:D] = (da * silu_za * conv).astype(BF16)
        dpa_ref[:, D:2 * D] = (da * bg * conv * (sza * (1.0 + za * (1.0 - sza)))).astype(BF16)
        dpa_ref[:, 2 * D:3 * D] = (db * rg * (szb * (1.0 + zb * (1.0 - szb)))).astype(BF16)
        dpa_ref[:, 3 * D:4 * D] = (dmix * y_a * sga * (1.0 - sga)).astype(BF16)
        dpa_ref[:, 4 * D:5 * D] = (dmix * y_b * sgb * (1.0 - sgb)).astype(BF16)
        dconv_ref[...] = (da * silu_za * bg).astype(BF16)
        drn_n = db * silu_zb
        drn = drn_n * gnw
        rstd = rstd_ref[...]
        for hd in range(N_HEADS):
            sl = slice(hd * DV, (hd + 1) * DV)
            dh_, rh = drn[:, sl], rn[:, sl]
            m1 = jnp.mean(dh_, axis=-1, keepdims=True)
            m2 = jnp.mean(dh_ * rh, axis=-1, keepdims=True)
            dret_ref[:, sl] = (rstd[:, sl] * (dh_ - m1 - rh * m2)).astype(BF16)
        at_ref[0] = a_act.T.astype(BF16)
        at_ref[1] = b_act.T.astype(BF16)
        at_ref[2] = mix.T.astype(BF16)
        b_ref[0] = dya
        b_ref[1] = dyb
        b_ref[2] = dy

        @pl.when(i == 0)
        def _():
            part_ref[...] = jnp.zeros((8, D), F32)

        part_ref[0:1, :] += jnp.sum(dout * xh1, axis=0, keepdims=True)
        part_ref[1:2, :] += jnp.sum(dx1 * y, axis=0, keepdims=True)
        part_ref[2:3, :] += jnp.sum(drn_n * rn, axis=0, keepdims=True)
        part_ref[3:4, :] += jnp.broadcast_to(loss, (1, D))

    col = lambda cb_: pl.BlockSpec((TM, D), lambda i, cb_=cb_: (i, cb_))
    prev = lambda cb_: pl.BlockSpec((HALO, D), lambda i, cb_=cb_: (jnp.maximum(i * hb - 1, 0), cb_))
    nxt = lambda cb_: pl.BlockSpec((HALO, D), lambda i, cb_=cb_: (jnp.minimum((i + 1) * hb, n_tiles * hb - 1), cb_))
    tile = pl.BlockSpec((TM, D), lambda i: (i, 0))
    full = lambda a: pl.BlockSpec(a.shape, lambda i: (0,) * a.ndim)
    return pl.pallas_call(
        body, name="merge", grid=(n_tiles,),
        in_specs=[col(CB_H), col(CB_BG), col(CB_CG), col(CB_ZA), col(CB_ZB), col(CB_GA), col(CB_GB),
                  prev(CB_H), nxt(CB_H), prev(CB_CG), nxt(CB_CG),
                  tile, tile, tile, tile, full(w_a), full(w_b), full(w_out), full(vecs)],
        out_specs=[tile, pl.BlockSpec((TM, 5 * D), lambda i: (i, 0)), tile, tile,
                   pl.BlockSpec((3, D, TM), lambda i: (0, 0, i)), pl.BlockSpec((3, TM, D), lambda i: (0, i, 0)),
                   pl.BlockSpec((8, D), lambda i: (0, 0))],
        out_shape=[jax.ShapeDtypeStruct((seq, D), F32), jax.ShapeDtypeStruct((seq, 5 * D), BF16),
                   jax.ShapeDtypeStruct((seq, D), BF16), jax.ShapeDtypeStruct((seq, D), BF16),
                   jax.ShapeDtypeStruct((3, D, seq), BF16), jax.ShapeDtypeStruct((3, seq, D), BF16),
                   jax.ShapeDtypeStruct((8, D), F32)],
        compiler_params=_cparams(dimension_semantics=("arbitrary",)),
    )(p_ext, p_ext, p_ext, p_ext, p_ext, p_ext, p_ext, p_ext, p_ext, p_ext, p_ext,
      rn, rstd, x, target, w_a, w_b, w_out, vecs)


def _conv_backward(p_ext, dconv, dpa, dq, dk, dv, vecs, seq):
    n_tiles = seq // TM
    hb = TM // HALO
    qk = N_HEADS * DK

    def body(h_ref, cg_ref, dc_ref, dcp_ref, dcn_ref, dpa_ref, dq_ref, dk_ref, dv_ref, vec_ref, dp_ref, part_ref):
        i = pl.program_id(0)
        dp_ref[:, CB_QK * D:CB_QK * D + qk] = dq_ref[...]
        dp_ref[:, CB_QK * D + qk:CB_V * D] = dk_ref[...]
        dp_ref[:, CB_V * D:CB_ZB * D] = dv_ref[...]

        @pl.when(i == 0)
        def _():
            part_ref[...] = jnp.zeros((8, D), F32)

        @pl.when(i == n_tiles)
        def _():
            dp_ref[:, 0:CB_QK * D] = jnp.zeros((TM, CB_QK * D), BF16)
            dp_ref[:, CB_ZB * D:PW] = jnp.zeros((TM, PW - CB_ZB * D), BF16)

        @pl.when(i < n_tiles)
        def _():
            f = lambda ref: ref[...].astype(F32)
            h, cg, dc = f(h_ref), f(cg_ref), f(dc_ref)
            w0, w1, w2 = vec_ref[1:2, :], vec_ref[2:3, :], vec_ref[3:4, :]
            row = lax.broadcasted_iota(jnp.int32, (TM, 1), 0)
            dc_prev = jnp.where(i == 0, 0.0, f(dcp_ref)[HALO - 1:HALO, :])
            dc_next = jnp.where(i == n_tiles - 1, 0.0, f(dcn_ref)[0:1, :])
            dc_up = jnp.where(row == 0, dc_prev, pltpu.roll(dc, 1, 0))
            dc_dn = jnp.where(row == TM - 1, dc_next, pltpu.roll(dc, TM - 1, 0))
            du = w0 * dc_dn + w1 * dc + w2 * dc_up
            u = cg * h
            dp_ref[:, CB_H * D:(CB_H + 1) * D] = (du * cg).astype(BF16)
            dp_ref[:, CB_CG * D:(CB_CG + 1) * D] = (du * h).astype(BF16)
            dp_ref[:, CB_BG * D:(CB_BG + 1) * D] = dpa_ref[:, 0:D]
            dp_ref[:, CB_ZA * D:(CB_ZA + 1) * D] = dpa_ref[:, D:2 * D]
            dp_ref[:, CB_ZB * D:(CB_GB + 1) * D] = dpa_ref[:, 2 * D:5 * D]
            part_ref[0:1, :] += jnp.sum(u * dc_dn, axis=0, keepdims=True)
            part_ref[1:2, :] += jnp.sum(u * dc, axis=0, keepdims=True)
            part_ref[2:3, :] += jnp.sum(u * dc_up, axis=0, keepdims=True)
            part_ref[3:4, :] += jnp.sum(dc, axis=0, keepdims=True)

    last = n_tiles - 1
    col = lambda cb_: pl.BlockSpec((TM, D), lambda i, cb_=cb_: (jnp.minimum(i, last), cb_))
    lat = lambda w: pl.BlockSpec((TM, w), lambda i: (jnp.minimum(i, last), 0))
    ext = lambda w: pl.BlockSpec((TM, w), lambda i: (i, 0))
    return pl.pallas_call(
        body, name="conv_backward", grid=(n_tiles + 1,),
        in_specs=[col(CB_H), col(CB_CG), lat(D),
                  pl.BlockSpec((HALO, D), lambda i: (jnp.clip(i * hb - 1, 0, n_tiles * hb - 1), 0)),
                  pl.BlockSpec((HALO, D), lambda i: (jnp.minimum((i + 1) * hb, n_tiles * hb - 1), 0)),
                  lat(5 * D), ext(qk), ext(qk), ext(D),
                  pl.BlockSpec(vecs.shape, lambda i: (0, 0))],
        out_specs=[ext(PW), pl.BlockSpec((8, D), lambda i: (0, 0))],
        out_shape=[jax.ShapeDtypeStruct((seq + TM, PW), BF16), jax.ShapeDtypeStruct((8, D), F32)],
        compiler_params=_cparams(dimension_semantics=("arbitrary",)),
    )(p_ext, p_ext, dconv, dconv, dconv, dpa, dq, dk, dv, vecs)


def _input_backward(dp, shards, ids, x, ctx, dx1, modx, modc, norm_w):
    seq = x.shape[0]
    lext = seq + ctx.shape[0]
    n_x = seq // TM
    n_w = len(shards.arrays)

    def body(ids_ref, dp_ref, x_ref, ctx_ref, dx1_ref, mx_ref, mc_ref, nw_ref, *rest):
        w_hbm, (gx_ref, part_ref, w_ref, w_sems) = rest[:n_w], rest[n_w:]
        i = pl.program_id(0)
        _load_w_in(i, ids_ref, shards, w_hbm, w_ref, w_sems)
        is_ctx = i >= n_x
        dxm = _dot_nt(dp_ref[...], w_ref[...])
        x = jnp.where(is_ctx, ctx_ref[...], x_ref[...])
        r = lax.rsqrt(jnp.mean(x * x, axis=-1, keepdims=True) + EPS)
        xh = x * r
        nw = nw_ref[...]
        sc = jnp.where(is_ctx, mc_ref[1:2, :], mx_ref[1:2, :])
        dxn = dxm * (1.0 + sc)
        dxh = dxn * nw
        dx = r * (dxh - xh * jnp.mean(dxh * xh, axis=-1, keepdims=True))

        @pl.when(jnp.logical_not(is_ctx))
        def _():
            gx_ref[...] = dx1_ref[...] + dx

        @pl.when(i == 0)
        def _():
            part_ref[...] = jnp.zeros((8, D), F32)

        fx = jnp.where(is_ctx, 0.0, 1.0)
        d_shift = jnp.sum(dxm, axis=0, keepdims=True)
        d_scale = jnp.sum(dxm * (xh * nw), axis=0, keepdims=True)
        part_ref[0:1, :] += fx * d_shift
        part_ref[1:2, :] += fx * d_scale
        part_ref[2:3, :] += jnp.sum(dxn * xh, axis=0, keepdims=True)
        part_ref[3:4, :] += (1.0 - fx) * d_shift
        part_ref[4:5, :] += (1.0 - fx) * d_scale

    lat = lambda w: pl.BlockSpec((TM, w), lambda i, ids_: (jnp.minimum(i, n_x - 1), 0))
    ext = lambda w: pl.BlockSpec((TM, w), lambda i, ids_: (i, 0))
    full = lambda a: pl.BlockSpec(a.shape, lambda i, ids_: (0,) * a.ndim)
    grid_spec = pltpu.PrefetchScalarGridSpec(
        num_scalar_prefetch=1, grid=(lext // TM,),
        in_specs=[ext(PW), lat(D), full(ctx), lat(D), full(modx), full(modc), full(norm_w)]
        + [pl.BlockSpec(memory_space=pl.ANY)] * n_w,
        out_specs=[lat(D), pl.BlockSpec((8, D), lambda i, ids_: (0, 0))],
        scratch_shapes=[pltpu.VMEM((D, PW), BF16), pltpu.SemaphoreType.DMA((N_DEV,))])
    return pl.pallas_call(
        body, name="input_backward", grid_spec=grid_spec,
        out_shape=[jax.ShapeDtypeStruct((seq, D), F32), jax.ShapeDtypeStruct((8, D), F32)],
        compiler_params=_cparams(dimension_semantics=("arbitrary",)),
    )(ids, dp, x, ctx, dx1, modx, modc, norm_w, *shards.arrays)


def _weight_grad_in(xm_t, dp, owners, name):
    lext = xm_t.shape[1]

    def body(own_ref, a_ref, b_ref, o_ref):
        o_ref[0] = _dot(a_ref[...], b_ref[...])

    grid_spec = pltpu.PrefetchScalarGridSpec(
        num_scalar_prefetch=1, grid=(4,),
        in_specs=[pl.BlockSpec((D, lext), lambda j, own: (0, 0)),
                  pl.BlockSpec((lext, WSH), lambda j, own: (0, own[j]))],
        out_specs=pl.BlockSpec((1, D, WSH), lambda j, own: (j, 0, 0)))
    return pl.pallas_call(
        body, name=name, grid_spec=grid_spec,
        out_shape=jax.ShapeDtypeStruct((4, D, WSH), F32),
        compiler_params=_cparams(dimension_semantics=("arbitrary",)),
    )(owners, xm_t, dp)


def _weight_grad_square(a_t, b):
    seq = a_t.shape[2]

    def body(a_ref, b_ref, o_ref):
        o_ref[:, 0] = _dot(a_ref[0], b_ref[0]).reshape(N_DEV, RSH, D)

    return pl.pallas_call(
        body, name="weight_grad_square", grid=(3,),
        in_specs=[pl.BlockSpec((1, D, seq), lambda t: (t, 0, 0)), pl.BlockSpec((1, seq, D), lambda t: (t, 0, 0))],
        out_specs=pl.BlockSpec((N_DEV, 1, RSH, D), lambda t: (0, t, 0, 0)),
        out_shape=jax.ShapeDtypeStruct((N_DEV, 3, RSH, D), F32),
        compiler_params=_cparams(dimension_semantics=("arbitrary",)),
    )(a_t, b)


def _rope_tables(seq, ctx_len, token):
    rows = seq // GRID_W
    row = jnp.repeat(jnp.arange(rows), GRID_W).astype(F32)
    col = jnp.tile(jnp.arange(GRID_W), rows).astype(F32)
    nf = DK // 4
    inv = ROPE_BASE ** (-jnp.arange(nf, dtype=F32) / nf)
    ang = _tie(jnp.concatenate([row[:, None] * inv, col[:, None] * inv], axis=-1), token)
    cos, sin = jnp.cos(ang), jnp.sin(ang)
    cos_t = jnp.tile(cos, (1, 4))
    sin_t = jnp.tile(jnp.concatenate([-sin, sin], axis=-1), (1, 2))
    cos_t = jnp.concatenate([cos_t, jnp.ones((ctx_len, LANES), F32)], axis=0)
    sin_t = jnp.concatenate([sin_t, jnp.zeros((ctx_len, LANES), F32)], axis=0)
    return cos_t, sin_t


def _tie(value, token):
    return value + token[0, 0].astype(value.dtype)


def _pad_lanes(a):
    return jnp.pad(a, ((0, 0), (0, D - a.shape[1])))


def _rows(total, *parts):
    width = max(a.shape[1] for _, a in parts)
    out = None
    for row, a in parts:
        padded = jnp.pad(a, ((row, total - row - a.shape[0]), (0, width - a.shape[1])))
        out = padded if out is None else out + padded
    return out


def kernel(x, c, ctx, c_ctx, norm_w, ada_w, ada_b, w_in, conv_w, conv_b, decay_logit, gn_w, w_a, w_b, w_out, final_norm_w, loss_target, m_c_ctx, m_norm_w, m_ada_w, m_ada_b, m_w_in, m_conv_w, m_conv_b, m_decay_logit, m_gn_w, m_w_a, m_w_b, m_w_out, m_final_norm_w, v_c_ctx, v_norm_w, v_ada_w, v_ada_b, v_w_in, v_conv_w, v_conv_b, v_decay_logit, v_gn_w, v_w_a, v_w_b, v_w_out, v_final_norm_w):
    xi, yi, ci = _coords()
    me = 4 * xi + 2 * yi + ci
    chip = 2 * xi + yi
    seq, ctx_len = x.shape[1], ctx.shape[1]
    assert seq % TM == 0 and seq % RET_C == 0 and ctx_len == TM and seq % GRID_W == 0
    csh = D // N_DEV

    blk = jnp.pad(c, ((0, 7), (0, 0))) + jnp.pad(conv_w[0], ((1, 4), (0, D - csh)))
    got, _ = _all_gather_small(blk, "gather_cond")
    conv_w_all = got[:, 1:4, 0:csh].transpose(1, 0, 2).reshape(3, D)
    c16 = _rows(16, (0, got[:, 0, :]), (N_DEV, c_ctx[None]))
    ada_b_sh = lax.dynamic_slice(ada_b, (0, me * ADA_SH), (1, ADA_SH))
    mod_sh, act16, lg = _modulation(c16, ada_w[0], ada_b_sh, decay_logit[0])
    mod_all, small_done = _all_gather_small(mod_sh, "gather_mod")
    mod_all = mod_all.transpose(1, 0, 2).reshape(16, 3 * D)
    modx = lax.dynamic_slice(mod_all, (me, 0), (1, 3 * D)).reshape(3, D)
    modc = mod_all[8].reshape(3, D)

    x2, tgt = x[0], loss_target[0]
    wblock = lambda n: jax.ShapeDtypeStruct((n, D, WSH), BF16)
    ids_of = lambda ks: jnp.stack([jnp.bitwise_xor(me, k) for k in ks]).astype(jnp.int32)
    own_shard = _tie(w_in[0].astype(BF16)[None], small_done)
    to_sib = _Exchange("gather_in_sibling", [own_shard], [wblock(1)], 1, _send_plan((1,)))
    to_nbr = _Exchange("gather_in_neighbours", [_tie(own_shard, to_sib.token)], [wblock(2)], 2, _send_plan((4, 2)))
    cos_t, sin_t = _rope_tables(seq, ctx_len, to_nbr.token)
    modx = _tie(modx, to_nbr.token)
    xm, xm_t = _prenorm(x2, ctx[0], modx, modc, norm_w)
    (own_shard,), (w_sib,) = to_sib.wait(xm_t)
    first = _Shards([own_shard, w_sib], [(0, 0), (1, 0)])
    p_ext = _in_projection(None, xm, cos_t, sin_t, first, ids_of((0, 1)), "in_projection_pair")
    _, (w_nbr,) = to_nbr.wait(p_ext)
    fwd_nbr = _Exchange("gather_in_forward_neighbours", [w_nbr], [wblock(2)], 2, _forward_plan(2))
    to_diag = _Exchange("gather_in_diagonal", [_tie(own_shard, fwd_nbr.token)], [wblock(1)], 1, _send_plan((6,)))
    (w_nbr,), (w_nbr_sib,) = fwd_nbr.wait(to_diag.token)
    second = _Shards([w_nbr, w_nbr_sib], [(0, 0), (0, 1), (1, 0), (1, 1)])
    p_ext = _in_projection(p_ext, xm, cos_t, sin_t, second, ids_of((4, 2, 5, 3)), "in_projection_neighbours")
    _, (w_diag,) = to_diag.wait(p_ext)
    fwd_diag = _Exchange("gather_in_forward_diagonal", [w_diag], [wblock(1)], 1, _forward_plan(1))
    sq_shards = [_tie(w[0].astype(BF16), fwd_diag.token) for w in (w_a, w_b, w_out)]
    ag_sq = _Exchange("gather_square", sq_shards, [jax.ShapeDtypeStruct((N_DEV, RSH, D), BF16)] * 3,
                      3 * (N_DEV - 1), _gather_plan)
    (w_diag,), (w_diag_sib,) = fwd_diag.wait(ag_sq.token)
    third = _Shards([w_diag, w_diag_sib], [(0, 0), (1, 0)])
    p_ext = _in_projection(p_ext, xm, cos_t, sin_t, third, ids_of((6, 7)), "in_projection_diagonal")
    w_shards, w_ids = first + second + third, ids_of((0, 1, 4, 2, 5, 3, 6, 7))

    rn, rstd = _retention_forward(p_ext, lg, seq, ctx_len)
    _, sq_lands = ag_sq.wait(rstd)
    w_a_all, w_b_all, w_out_all = (
        lax.dynamic_update_slice(land, shard[None], (me, 0, 0)).reshape(D, D) for land, shard in zip(sq_lands, sq_shards))
    vecs = _rows(8, (0, modx[2:3]), (1, conv_w_all), (4, conv_b), (5, gn_w), (6, final_norm_w[None]))
    dx1, dpa, dconv, dret, op_at, op_b, part_m = _merge(p_ext, rn, rstd, x2, tgt, w_a_all, w_b_all, w_out_all, vecs, seq)

    j4 = jnp.arange(4, dtype=jnp.int32)
    owners = (2 * jnp.bitwise_xor(chip, j4) + ci).astype(jnp.int32)
    owners_sib = (2 * jnp.bitwise_xor(chip, j4) + (1 - ci)).astype(jnp.int32)
    gw_sq = _weight_grad_square(op_at, op_b).reshape(N_DEV, 3 * RSH, D)
    rs_sq_pair = _Exchange("rs_square_pair", [gw_sq], [jax.ShapeDtypeStruct((4, 3 * RSH, D), F32)], 4,
                           _pair_plan(lambda j, chip_, c_: 2 * jnp.bitwise_xor(chip_, j) + (1 - c_)))
    dq, dk, dv, dlg = _retention_backward(p_ext, dret, _tie(lg, rs_sq_pair.token), cos_t, sin_t, seq, ctx_len)
    (gw_sq,), (r1_sq,) = rs_sq_pair.wait(dlg)
    own_sq, send_sq = _pair_sum(gw_sq, r1_sq, owners, "pair_sum_square")
    rs_sq_chips = _Exchange("rs_square_chips", [send_sq], [jax.ShapeDtypeStruct((3, 3 * RSH, D), BF16)], 3, _chips_plan)
    dp, part_c = _conv_backward(p_ext, dconv, dpa, dq, dk, dv, _tie(vecs, rs_sq_chips.token), seq)
    gw_sib = _weight_grad_in(xm_t, dp, owners_sib, "weight_grad_in_sibling")
    rs_in_pair = _Exchange("rs_in_pair", [gw_sib], [jax.ShapeDtypeStruct((4, D, WSH), F32)], 4,
                           _pair_plan(lambda j, chip_, c_: j))
    gw_own = _weight_grad_in(xm_t, dp, _tie(owners, rs_in_pair.token), "weight_grad_in_own")
    _, (r1_in,) = rs_in_pair.wait(gw_own)
    own_in, send_in = _pair_sum(gw_own, r1_in, j4, "pair_sum_in")
    rs_in_chips = _Exchange("rs_in_chips", [send_in], [jax.ShapeDtypeStruct((3, D, WSH), BF16)], 3, _chips_plan)
    grad_x, part_i = _input_backward(dp, w_shards, w_ids, x2, ctx[0], dx1, _tie(modx, rs_in_chips.token), modc, norm_w)

    dl = dlg[:, 0, 0:4]
    dlg_row = jnp.pad(dl[:, 0::2].reshape(1, N_HEADS), ((0, 0), (0, D - N_HEADS))) + jnp.pad(
        dl[:, 1::2].reshape(1, N_HEADS), ((0, 0), (N_HEADS, D - 2 * N_HEADS)))
    partials = _rows(16, (0, part_m[0:1]), (1, part_i[2:3]), (2, part_c[3:4]), (3, part_m[2:3]), (4, part_c[0:3]),
                     (7, part_i[0:2]), (9, part_m[1:2]), (10, part_i[3:5]), (13, part_m[3:4]), (14, dlg_row))
    got, _ = _all_gather_small(partials, "gather_partials")
    tot = _sum_devices(got, "sum_partials")
    dmodc = tot[10:13].reshape(1, 3 * D)
    dmod16 = _rows(16, (0, got[:, 7:10, :].reshape(N_DEV, 3 * D)), (N_DEV, dmodc))
    dmod16 = lax.dynamic_slice(dmod16, (0, me * ADA_SH), (16, ADA_SH))
    dmodc8 = _rows(8, (0, dmod16[8:9]))
    g_ada_w, d_ada_w, nm_ada_w, nv_ada_w, cctx_part = _ada_backward(
        act16.T, dmod16, dmodc8, ada_w[0], m_ada_w[0], v_ada_w[0])
    dact_ctx = _sum_devices(_all_gather_small(cctx_part, "gather_cctx")[0], "sum_cctx")

    view = {"c_ctx": (1, D), "final_norm_w": (1, D), "decay_logit": (1, 2 * N_HEADS), "conv_w": (3, csh)}
    given = {"c_ctx": (c_ctx, m_c_ctx, v_c_ctx), "norm_w": (norm_w, m_norm_w, v_norm_w),
             "conv_b": (conv_b, m_conv_b, v_conv_b), "gn_w": (gn_w, m_gn_w, v_gn_w),
             "final_norm_w": (final_norm_w, m_final_norm_w, v_final_norm_w), "ada_b": (ada_b, m_ada_b, v_ada_b),
             "decay_logit": (decay_logit, m_decay_logit, v_decay_logit), "conv_w": (conv_w, m_conv_w, v_conv_w)}
    small_in = [tuple(a.reshape(view.get(name, a.shape)) for a in given[name]) for name in SMALL_PARAMS]
    conv_w_grad = lax.dynamic_slice(tot, (4, me * csh), (3, csh))
    small_out = _adam_small(tot, dact_ctx, conv_w_grad, small_in)
    results = {name: [o.reshape(given[name][0].shape) for o in outs_]
               for name, outs_ in zip(SMALL_PARAMS, small_out)}

    _, (r2_sq,) = rs_sq_chips.wait(small_out[0][0])
    _, (r2_in,) = rs_in_chips.wait(small_out[0][0])
    results["w_in"] = [o[None] for o in _adam_sharded(w_in[0], m_w_in[0], v_w_in[0], own_in, r2_in, "adam_w_in")]
    square = _adam_square([(w_a[0], m_w_a[0], v_w_a[0]), (w_b[0], m_w_b[0], v_w_b[0]),
                           (w_out[0], m_w_out[0], v_w_out[0])], own_sq, r2_sq)
    for name, outs_ in zip(("w_a", "w_b", "w_out"), square):
        results[name] = [o[None] for o in outs_]
    results["ada_w"] = [o[None] for o in (g_ada_w, d_ada_w, nm_ada_w, nv_ada_w)]

    order = ("c_ctx", "norm_w", "ada_w", "ada_b", "w_in", "conv_w", "conv_b", "decay_logit", "gn_w",
             "w_a", "w_b", "w_out", "final_norm_w")
    outs = [results[name][kind] for kind in range(4) for name in order]
    return (tot[13, 0], grad_x[None], *outs)
```

```python
import functools
import math

import jax
import jax.numpy as jnp
from jax import lax
from jax.experimental import pallas as pl
from jax.experimental.pallas import tpu as pltpu

F32 = jnp.float32
BF16 = jnp.bfloat16
MESH = pl.DeviceIdType.MESH
_HBM_SPEC = pl.BlockSpec(memory_space=pltpu.HBM)
_SEM_SPEC = pl.BlockSpec(memory_space=pltpu.SEMAPHORE)
_EFFECT = pltpu.SideEffectType.DATAFLOW_SIDE_EFFECTING

N_DEV = 8
D = 1024
N_HEADS = 8
DK = 64
DV = 128
GRID_W = 64
ROPE_BASE = 10000.0
EPS = 1e-6
PW = 9 * D
WSH = PW // N_DEV
RSH = D // N_DEV
ADA_SH = 3 * D // N_DEV
TM = 256
RET_C = 256
HALO = 16
LANES = 128
VMEM_LIMIT = 60 * 1024 * 1024

ADAM_LR = 0.001
ADAM_B1 = 0.9
ADAM_B2 = 0.999
ADAM_EPS = 1e-08
ADAM_WD = 0.01
ADAM_STEP = 10

CB_H, CB_BG, CB_CG, CB_ZA, CB_QK, CB_V, CB_ZB, CB_GA, CB_GB = range(9)


def _cparams(**kw):
    return pltpu.CompilerParams(vmem_limit_bytes=VMEM_LIMIT, **kw)


def _dot(a, b):
    return jnp.dot(a, b, preferred_element_type=F32)


def _dot_nt(a, b):
    return lax.dot_general(a, b, (((1,), (1,)), ((), ())), preferred_element_type=F32)


def _dot_tn(a, b):
    return lax.dot_general(a, b, (((0,), (0,)), ((), ())), preferred_element_type=F32)


def _sigmoid(z):
    return 0.5 * jnp.tanh(0.5 * z) + 0.5


def _row_tile(rows):
    return TM if rows % TM == 0 else rows


def _coords():
    return lax.axis_index("x"), lax.axis_index("y"), lax.axis_index("c")


def _flip(v, bit):
    return 1 - v if bit else v


def _all_gather_small(blk, name):
    rows, cols = blk.shape

    def body(x_ref, out_ref, done, send_sems, recv_sems, local_sem):
        x, y, c = _coords()
        me = 4 * x + 2 * y + c
        mine = pltpu.make_async_copy(x_ref, out_ref.at[me], local_sem)
        mine.start()

        def copy(k, slot):
            peer = (_flip(x, k & 4), _flip(y, k & 2), _flip(c, k & 1))
            return pltpu.make_async_remote_copy(
                src_ref=x_ref, dst_ref=out_ref.at[slot], send_sem=send_sems.at[k - 1],
                recv_sem=recv_sems.at[k - 1], device_id=peer, device_id_type=MESH)

        for k in range(1, N_DEV):
            copy(k, me).start()
        for k in range(1, N_DEV):
            copy(k, jnp.bitwise_xor(me, k)).wait_recv()
        for k in range(1, N_DEV):
            copy(k, me).wait_send()
        mine.wait()
        done[...] = jnp.zeros((8, LANES), F32)

    vmem = pl.BlockSpec(memory_space=pltpu.VMEM)
    return pl.pallas_call(
        body, name=name,
        out_shape=[jax.ShapeDtypeStruct((N_DEV, rows, cols), blk.dtype), jax.ShapeDtypeStruct((8, LANES), F32)],
        in_specs=[vmem], out_specs=[vmem, vmem],
        scratch_shapes=[pltpu.SemaphoreType.DMA((N_DEV - 1,)), pltpu.SemaphoreType.DMA((N_DEV - 1,)),
                        pltpu.SemaphoreType.DMA],
    )(blk)


class _Exchange:
    def __init__(self, name, srcs, land_shapes, n_copies, plan):
        self.name, self.plan, self.n_copies = name, plan, n_copies
        self.n_src, self.n_land = len(srcs), len(land_shapes)
        hbm = lambda a: pltpu.HBM(a.shape, a.dtype)
        n = self.n_src + self.n_land
        lands = [pltpu.with_memory_space_constraint(lax.empty(s.shape, s.dtype), pltpu.HBM) for s in land_shapes]
        srcs = [pltpu.with_memory_space_constraint(s, pltpu.HBM) for s in srcs]

        def body(*refs):
            send_sems, recv_sems = refs[n], refs[n + 1]
            for cp in self._copies(refs, send_sems, recv_sems):
                cp.start()
            refs[-1][...] = jnp.zeros((8, LANES), F32)

        outs = pl.pallas_call(
            body, name=name + "_start",
            out_shape=(pltpu.SemaphoreType.DMA((n_copies,)), pltpu.SemaphoreType.DMA((n_copies,)),
                       *[hbm(a) for a in srcs], *[hbm(a) for a in lands], jax.ShapeDtypeStruct((8, LANES), F32)),
            in_specs=[_HBM_SPEC] * n,
            out_specs=(_SEM_SPEC, _SEM_SPEC, *[_HBM_SPEC] * n, pl.BlockSpec(memory_space=pltpu.VMEM)),
            input_output_aliases={i: 2 + i for i in range(n)},
            compiler_params=pltpu.CompilerParams(has_side_effects=_EFFECT),
        )(*srcs, *lands)
        self.send_sems, self.recv_sems = outs[0], outs[1]
        self.buffers = list(outs[2:2 + n])
        self.token = outs[-1]

    def _copies(self, refs, send_sems, recv_sems):
        src_refs, land_refs = refs[:self.n_src], refs[self.n_src:self.n_src + self.n_land]
        return [pltpu.make_async_remote_copy(src_ref=s, dst_ref=d, send_sem=send_sems.at[k], recv_sem=recv_sems.at[k],
                                             device_id=dev, device_id_type=MESH)
                for k, (s, d, dev) in enumerate(self.plan(src_refs, land_refs))]

    def wait(self, after):
        n = self.n_src + self.n_land

        def body(*refs):
            for cp in self._copies(refs, refs[n], refs[n + 1]):
                cp.wait_send()
                cp.wait_recv()

        outs = pl.pallas_call(
            body, name=self.name + "_wait",
            out_shape=tuple(pltpu.HBM(b.shape, b.dtype) for b in self.buffers),
            in_specs=[_HBM_SPEC] * n + [_SEM_SPEC, _SEM_SPEC, pl.BlockSpec(memory_space=pl.ANY)],
            out_specs=tuple([_HBM_SPEC] * n),
            input_output_aliases={i: i for i in range(n)},
            compiler_params=pltpu.CompilerParams(has_side_effects=_EFFECT),
        )(*self.buffers, self.send_sems, self.recv_sems, after)
        return list(outs[:self.n_src]), list(outs[self.n_src:])


def _pair_plan(src_index):
    def plan(srcs, lands):
        x, y, c = _coords()
        chip = 2 * x + y
        return [(srcs[0].at[src_index(j, chip, c)], lands[0].at[j], (x, y, 1 - c)) for j in range(4)]
    return plan


def _chips_plan(srcs, lands):
    x, y, c = _coords()
    return [(srcs[0].at[j - 1], lands[0].at[j - 1], (_flip(x, j & 2), _flip(y, j & 1), c)) for j in range(1, 4)]


def _peer(k):
    x, y, c = _coords()
    return _flip(x, k & 4), _flip(y, k & 2), _flip(c, k & 1)


def _send_plan(ks):
    def plan(srcs, lands):
        return [(srcs[0].at[0], lands[0].at[p], _peer(k)) for p, k in enumerate(ks)]
    return plan


def _forward_plan(n):
    def plan(srcs, lands):
        return [(srcs[0].at[p], lands[0].at[p], _peer(1)) for p in range(n)]
    return plan


def _gather_plan(srcs, lands):
    x, y, c = _coords()
    me = 4 * x + 2 * y + c
    return [(srcs[a], lands[a].at[me], (_flip(x, k & 4), _flip(y, k & 2), _flip(c, k & 1)))
            for a in range(len(srcs)) for k in range(1, N_DEV)]


def _pair_sum(grad, recv, owners, name):
    _, rows, cols = grad.shape
    tr = _row_tile(rows)

    def body(own_ref, g_ref, r_ref, mine_ref, send_ref):
        j = pl.program_id(1)
        total = g_ref[...] + r_ref[...]

        @pl.when(j == 0)
        def _():
            mine_ref[...] = total

        @pl.when(j > 0)
        def _():
            send_ref[...] = total.astype(BF16)

    grid_spec = pltpu.PrefetchScalarGridSpec(
        num_scalar_prefetch=1, grid=(rows // tr, 4),
        in_specs=[pl.BlockSpec((1, tr, cols), lambda i, j, own: (own[j], i, 0)),
                  pl.BlockSpec((1, tr, cols), lambda i, j, own: (j, i, 0))],
        out_specs=[pl.BlockSpec((1, tr, cols), lambda i, j, own: (0, i, 0)),
                   pl.BlockSpec((1, tr, cols), lambda i, j, own: (jnp.maximum(j - 1, 0), i, 0))])
    return pl.pallas_call(
        body, name=name, grid_spec=grid_spec,
        out_shape=[jax.ShapeDtypeStruct((1, rows, cols), F32), jax.ShapeDtypeStruct((3, rows, cols), BF16)],
        compiler_params=_cparams(dimension_semantics=("arbitrary", "arbitrary")),
    )(owners, grad, recv)


def _modulation(c16, ada_w_sh, ada_b_sh, decay_logit):
    def body(c_ref, w_ref, b_ref, dl_ref, mod_ref, act_ref, lg_ref):
        cv = c_ref[...]
        act = cv * _sigmoid(cv)
        act_ref[...] = act
        mod_ref[...] = jnp.dot(act, w_ref[...], preferred_element_type=F32,
                               precision=lax.Precision.HIGHEST) + b_ref[...]
        z = dl_ref[...]
        lg_ref[...] = jnp.minimum(z, 0.0) - jnp.log(1.0 + jnp.exp(-jnp.abs(z)))

    return pl.pallas_call(
        body, name="modulation",
        out_shape=[jax.ShapeDtypeStruct((16, ADA_SH), F32), jax.ShapeDtypeStruct((16, D), F32),
                   jax.ShapeDtypeStruct(decay_logit.shape, F32)],
        compiler_params=_cparams(),
    )(c16, ada_w_sh, ada_b_sh, decay_logit)


def _adam_update(w, g, m, v):
    m2 = ADAM_B1 * m + (1.0 - ADAM_B1) * g
    v2 = ADAM_B2 * v + (1.0 - ADAM_B2) * (g * g)
    m_hat = m2 / (1.0 - ADAM_B1 ** ADAM_STEP)
    v_hat = v2 / (1.0 - ADAM_B2 ** ADAM_STEP)
    delta = -ADAM_LR * (m_hat / (jnp.sqrt(v_hat) + ADAM_EPS) + ADAM_WD * w)
    return delta, m2, v2


def _adam_sharded(w, m, v, own, recv, name):
    rows, cols = w.shape
    tr = _row_tile(rows)

    def body(w_ref, m_ref, v_ref, g0, g1, g2, g3, g_ref, d_ref, m_out, v_out):
        g = ((g0[0] + g1[0].astype(F32)) + g2[0].astype(F32)) + g3[0].astype(F32)
        delta, m2, v2 = _adam_update(w_ref[...], g, m_ref[...], v_ref[...])
        g_ref[...] = g
        d_ref[...] = delta
        m_out[...] = m2
        v_out[...] = v2

    flat = pl.BlockSpec((tr, cols), lambda i: (i, 0))
    part = lambda j: pl.BlockSpec((1, tr, cols), lambda i, j=j: (j, i, 0))
    return pl.pallas_call(
        body, name=name, grid=(rows // tr,),
        in_specs=[flat, flat, flat, part(0), part(0), part(1), part(2)],
        out_specs=[flat] * 4,
        out_shape=[jax.ShapeDtypeStruct((rows, cols), F32)] * 4,
        compiler_params=_cparams(dimension_semantics=("arbitrary",)),
    )(w, m, v, own, recv, recv, recv)


def _ada_backward(act_t, dmod16, dmodc8, ada_w_sh, m, v):
    def body(at_ref, dm_ref, dc_ref, w_ref, m_ref, v_ref, g_ref, d_ref, m_out, v_out, pc_ref):
        g = jnp.dot(at_ref[...], dm_ref[...], preferred_element_type=F32, precision=lax.Precision.HIGHEST)
        w = w_ref[...]
        delta, m2, v2 = _adam_update(w, g, m_ref[...], v_ref[...])
        g_ref[...] = g
        d_ref[...] = delta
        m_out[...] = m2
        v_out[...] = v2
        pc_ref[...] = lax.dot_general(dc_ref[...], w, (((1,), (1,)), ((), ())), preferred_element_type=F32,
                                      precision=lax.Precision.HIGHEST)

    return pl.pallas_call(
        body, name="ada_backward",
        out_shape=[jax.ShapeDtypeStruct((D, ADA_SH), F32)] * 4 + [jax.ShapeDtypeStruct((8, D), F32)],
        compiler_params=_cparams(),
    )(act_t, dmod16, dmodc8, ada_w_sh, m, v)


def _sum_devices(gathered, name):
    _, rows, cols = gathered.shape

    def body(g_ref, o_ref):
        acc = g_ref[0]
        for d in range(1, N_DEV):
            acc = acc + g_ref[d]
        o_ref[...] = acc

    return pl.pallas_call(body, name=name, out_shape=jax.ShapeDtypeStruct((rows, cols), F32),
                          compiler_params=_cparams())(gathered)


SMALL_PARAMS = ("c_ctx", "norm_w", "conv_b", "gn_w", "final_norm_w", "ada_b", "decay_logit", "conv_w")


def _adam_small(tot, dact_ctx, conv_w_grad, params):
    n = len(SMALL_PARAMS)

    def body(tot_ref, dact_ref, cwg_ref, *refs):
        ins, outs = refs[:3 * n], refs[3 * n:]
        nh = 2 * N_HEADS
        raw = {
            "c_ctx": dact_ref[0:1, :],
            "norm_w": tot_ref[1:2, :], "conv_b": tot_ref[2:3, :], "gn_w": tot_ref[3:4, :],
            "final_norm_w": tot_ref[0:1, :],
            "ada_b": jnp.concatenate([tot_ref[7 + r:8 + r, :] + tot_ref[10 + r:11 + r, :] for r in range(3)], axis=1),
            "decay_logit": tot_ref[14:15, 0:nh],
            "conv_w": cwg_ref[...],
        }
        for k, name in enumerate(SMALL_PARAMS):
            w, m, v = ins[3 * k][...], ins[3 * k + 1][...], ins[3 * k + 2][...]
            g = raw[name]
            if name == "c_ctx":
                s = _sigmoid(w)
                g = g * (s * (1.0 + w * (1.0 - s)))
            elif name == "decay_logit":
                g = g * (1.0 - _sigmoid(w))
            delta, m2, v2 = _adam_update(w, g, m, v)
            for ref, val in zip(outs[4 * k:4 * k + 4], (g, delta, m2, v2)):
                ref[...] = val

    flat = [a for trio in params for a in trio]
    out_shape = [jax.ShapeDtypeStruct(trio[0].shape, F32) for trio in params for _ in range(4)]
    outs = pl.pallas_call(body, name="adam_small", out_shape=out_shape, compiler_params=_cparams())(
        tot, dact_ctx, conv_w_grad, *flat)
    return [outs[4 * k:4 * k + 4] for k in range(n)]


def _adam_square(params, own, recv):
    def body(own_ref, recv_ref, *refs):
        ins, outs = refs[:9], refs[9:]
        for k in range(3):
            rows = slice(k * RSH, (k + 1) * RSH)
            g = own_ref[0, rows, :]
            for j in range(3):
                g = g + recv_ref[j, rows, :].astype(F32)
            delta, m2, v2 = _adam_update(ins[3 * k][...], g, ins[3 * k + 1][...], ins[3 * k + 2][...])
            for ref, val in zip(outs[4 * k:4 * k + 4], (g, delta, m2, v2)):
                ref[...] = val

    flat = [a for trio in params for a in trio]
    outs = pl.pallas_call(body, name="adam_square", out_shape=[jax.ShapeDtypeStruct((RSH, D), F32)] * 12,
                          compiler_params=_cparams())(own, recv, *flat)
    return [outs[4 * k:4 * k + 4] for k in range(3)]


def _rope(t, cos, sin):
    lane = lax.broadcasted_iota(jnp.int32, (1, LANES), 1)
    first_half = jnp.bitwise_and(lane, DK // 2) == 0
    partner = jnp.where(first_half, pltpu.roll(t, LANES - DK // 2, 1), pltpu.roll(t, DK // 2, 1))
    return t * cos + partner * sin


class _Shards:
    def __init__(self, arrays, picks):
        self.arrays, self.picks = list(arrays), list(picks)

    def specs(self, index_args):
        out = []
        for a, arr in enumerate(self.arrays):
            steps = [p for p, (ai, _) in enumerate(self.picks) if ai == a]
            first, slot0, last = steps[0], self.picks[steps[0]][1], arr.shape[0] - 1
            out.append(pl.BlockSpec(
                (1, D, WSH), lambda *args, first=first, slot0=slot0, last=last:
                (jnp.clip(index_args(*args) - first + slot0, 0, last), 0, 0)))
        return out

    def __add__(self, other):
        shift = len(self.arrays)
        return _Shards(self.arrays + other.arrays, self.picks + [(a + shift, s) for a, s in other.picks])


def _load_w_in(step, ids_ref, shards, w_refs, w_vmem, sems):
    @pl.when(step == 0)
    def _():
        copies = []
        for p, (a, slot) in enumerate(shards.picks):
            col = pl.multiple_of(ids_ref[p] * WSH, LANES)
            copies.append(pltpu.make_async_copy(w_refs[a].at[slot], w_vmem.at[:, pl.ds(col, WSH)], sems.at[p]))
        for cp in copies:
            cp.start()
        for cp in copies:
            cp.wait()


def _project_columns(xmb, w_ref, shard, cos, sin, p_ref):
    q_lo, k_lo, k_hi = CB_QK * D, CB_QK * D + N_HEADS * DK, CB_V * D
    with_q, with_k = q_lo // WSH, k_lo // WSH
    assert (with_q + 1) * WSH == k_lo and k_hi <= (with_k + 1) * WSH
    mxu_cols = 2 * LANES

    def columns(lo, hi, scale=None):
        for a in range(lo, hi, mxu_cols):
            b = min(a + mxu_cols, hi)
            acc = _dot(xmb, w_ref[0, :, a:b])
            if scale is None:
                p_ref[:, a:b] = acc.astype(BF16)
                continue
            for col in range(0, b - a, LANES):
                p_ref[:, a + col:a + col + LANES] = _rope(acc[:, col:col + LANES] * scale, cos, sin).astype(BF16)

    @pl.when(jnp.logical_and(shard != with_q, shard != with_k))
    def _():
        columns(0, WSH)

    @pl.when(shard == with_q)
    def _():
        columns(0, q_lo - with_q * WSH)
        columns(q_lo - with_q * WSH, WSH, 1.0)

    @pl.when(shard == with_k)
    def _():
        columns(0, k_hi - k_lo, DK ** -0.5)
        columns(k_hi - k_lo, WSH)


def _prenorm(x, ctx, modx, modc, norm_w):
    n_x_tiles = x.shape[0] // TM
    lext = x.shape[0] + ctx.shape[0]

    def body(x_ref, ctx_ref, mx_ref, mc_ref, nw_ref, xm_ref, xmt_ref):
        is_ctx = pl.program_id(0) >= n_x_tiles
        x = jnp.where(is_ctx, ctx_ref[...], x_ref[...])
        r = lax.rsqrt(jnp.mean(x * x, axis=-1, keepdims=True) + EPS)
        sh = jnp.where(is_ctx, mc_ref[0:1, :], mx_ref[0:1, :])
        sc = jnp.where(is_ctx, mc_ref[1:2, :], mx_ref[1:2, :])
        xm = (x * r * nw_ref[...]) * (1.0 + sc) + sh
        xm_ref[...] = xm.astype(BF16)
        xmt_ref[...] = xm.T.astype(BF16)

    full = lambda a: pl.BlockSpec(a.shape, lambda i: (0,) * a.ndim)
    return pl.pallas_call(
        body, name="prenorm", grid=(lext // TM,),
        in_specs=[pl.BlockSpec((TM, D), lambda i: (jnp.minimum(i, n_x_tiles - 1), 0)), full(ctx),
                  full(modx), full(modc), full(norm_w)],
        out_specs=[pl.BlockSpec((TM, D), lambda i: (i, 0)), pl.BlockSpec((D, TM), lambda i: (0, i))],
        out_shape=[jax.ShapeDtypeStruct((lext, D), BF16), jax.ShapeDtypeStruct((D, lext), BF16)],
        compiler_params=_cparams(dimension_semantics=("arbitrary",)),
    )(x, ctx, modx, modc, norm_w)


def _in_projection(p_ext, xm, cos_t, sin_t, shards, ids, name):
    lext = xm.shape[0]
    n_w = len(shards.arrays)
    tr = lext // 4

    def body(ids_ref, xm_ref, cos_ref, sin_ref, *rest):
        w_refs, p_ref = rest[:n_w], rest[-1]
        j = pl.program_id(0)
        for p, (a, _) in enumerate(shards.picks):
            @pl.when(j == p)
            def _():
                _project_columns(xm_ref[...], w_refs[a], ids_ref[p], cos_ref[...], sin_ref[...], p_ref)

    row = lambda w: pl.BlockSpec((tr, w), lambda j, i, ids: (i, 0))
    grid_spec = pltpu.PrefetchScalarGridSpec(
        num_scalar_prefetch=1, grid=(len(shards.picks), lext // tr),
        in_specs=[row(D), row(LANES), row(LANES)] + shards.specs(lambda j, i, ids: j)
        + ([] if p_ext is None else [pl.BlockSpec(memory_space=pl.ANY)]),
        out_specs=pl.BlockSpec((tr, WSH), lambda j, i, ids: (i, ids[j])))
    return pl.pallas_call(
        body, name=name, grid_spec=grid_spec,
        out_shape=jax.ShapeDtypeStruct((lext, PW), BF16),
        input_output_aliases={} if p_ext is None else {4 + n_w: 0},
        compiler_params=_cparams(dimension_semantics=("arbitrary", "arbitrary")),
    )(ids, xm, cos_t, sin_t, *shards.arrays, *([] if p_ext is None else [p_ext]))


def _decay_tables(lgf, lgb, n):
    i = lax.broadcasted_iota(jnp.int32, (n, 1), 0).astype(F32)
    return dict(i=i, k_f=jnp.exp(lgf * (n - 1.0 - i)), k_b=jnp.exp(lgb * i),
                q_f=jnp.exp(lgf * (i + 1.0)), q_b=jnp.exp(lgb * (n - i)))


def _decay_matrix(lgf, lgb, n, transposed=False):
    ii = lax.broadcasted_iota(jnp.int32, (n, n), 0)
    jj = lax.broadcasted_iota(jnp.int32, (n, n), 1)
    diff = (jj - ii if transposed else ii - jj).astype(F32)
    low = jnp.exp(lgf * jnp.maximum(diff, 0.0))
    up = jnp.exp(lgb * jnp.maximum(-diff, 0.0))
    return jnp.where(diff > 0, low, jnp.where(diff < 0, up, 2.0)), diff


def _cat_lanes(a, b):
    return jnp.concatenate([a.astype(BF16), b.astype(BF16)], axis=1)


def _retention_forward(p_ext, lg, seq, ctx_len):
    lext = seq + ctx_len
    n_chunks = seq // RET_C
    C = RET_C

    def body(lg_ref, q_ref, k_ref, v_ref, rn_ref, rstd_ref, kv_scr, sf_scr, sb_scr):
        pair = pl.program_id(0)
        lane = lax.broadcasted_iota(jnp.int32, (1, LANES), 1)
        heads = range(2)
        hmask = [(lane // DK == hh).astype(F32) for hh in heads]
        lgf = [lg_ref[0, 2 * pair + hh] for hh in heads]
        lgb = [lg_ref[1, 2 * pair + hh] for hh in heads]
        vs = [slice(hh * DV, (hh + 1) * DV) for hh in heads]
        t = [_decay_tables(lgf[hh], lgb[hh], C) for hh in heads]
        kc_all = k_ref[seq:lext, :].astype(F32)
        s0 = []
        for hh in heads:
            tc = _decay_tables(lgf[hh], lgb[hh], ctx_len)
            kc = kc_all * hmask[hh]
            s0.append(_dot_tn(_cat_lanes(kc * tc["k_f"], kc * tc["k_b"]), v_ref[seq:lext, vs[hh]]))

        def increments(c, carry):
            rows = pl.ds(pl.multiple_of(c * C, C), C)
            k_all = k_ref[rows, :].astype(F32)
            for hh in heads:
                k = k_all * hmask[hh]
                kv_scr[hh, c] = _dot_tn(_cat_lanes(k * t[hh]["k_f"], k * t[hh]["k_b"]), v_ref[rows, vs[hh]])
            return carry

        lax.fori_loop(0, n_chunks, increments, 0, unroll=2)
        gf_c = [jnp.exp(lgf[hh] * C) for hh in heads]
        gb_c = [jnp.exp(lgb[hh] * C) for hh in heads]

        def scan_f(c, s):
            for hh in heads:
                sf_scr[hh, c] = s[hh]
            return tuple(gf_c[hh] * s[hh] + kv_scr[hh, c, 0:LANES, :] for hh in heads)

        def scan_b(n, s):
            c = n_chunks - 1 - n
            for hh in heads:
                sb_scr[hh, c] = s[hh]
            return tuple(gb_c[hh] * s[hh] + kv_scr[hh, c, LANES:2 * LANES, :] for hh in heads)

        lax.fori_loop(0, n_chunks, scan_f, tuple(s0[hh][0:LANES] for hh in heads))
        lax.fori_loop(0, n_chunks, scan_b, tuple(s0[hh][LANES:2 * LANES] for hh in heads))
        dmat = [_decay_matrix(lgf[hh], lgb[hh], C)[0] for hh in heads]

        def outputs(c, carry):
            rows = pl.ds(pl.multiple_of(c * C, C), C)
            q_all = q_ref[rows, :].astype(F32)
            k = k_ref[rows, :]
            for hh in heads:
                q = q_all * hmask[hh]
                v = v_ref[rows, vs[hh]]
                s = _dot_nt(q.astype(BF16), k)
                inner = _dot((s * dmat[hh]).astype(BF16), v)
                states = jnp.concatenate([sf_scr[hh, c], sb_scr[hh, c]], axis=0).astype(BF16)
                ret = inner + _dot(_cat_lanes(q * t[hh]["q_f"], q * t[hh]["q_b"]), states)
                mu = jnp.mean(ret, axis=-1, keepdims=True)
                cen = ret - mu
                rstd = lax.rsqrt(jnp.mean(cen * cen, axis=-1, keepdims=True) + EPS)
                rn_ref[rows, vs[hh]] = cen * rstd
                rstd_ref[rows, vs[hh]] = jnp.broadcast_to(rstd, (C, DV))
            return carry

        lax.fori_loop(0, n_chunks, outputs, 0, unroll=2)

    qk0 = CB_QK * D // LANES
    return pl.pallas_call(
        body, name="retention_forward", grid=(N_HEADS // 2,),
        in_specs=[pl.BlockSpec(memory_space=pltpu.SMEM),
                  pl.BlockSpec((lext, LANES), lambda g: (0, qk0 + g)),
                  pl.BlockSpec((lext, LANES), lambda g: (0, qk0 + N_HEADS // 2 + g)),
                  pl.BlockSpec((lext, 2 * DV), lambda g: (0, CB_V * D // (2 * DV) + g))],
        out_specs=[pl.BlockSpec((seq, 2 * DV), lambda g: (0, g))] * 2,
        out_shape=[jax.ShapeDtypeStruct((seq, D), F32)] * 2,
        scratch_shapes=[pltpu.VMEM((2, n_chunks, 2 * LANES, DV), F32), pltpu.VMEM((2, n_chunks, LANES, DV), F32),
                        pltpu.VMEM((2, n_chunks, LANES, DV), F32)],
        compiler_params=_cparams(dimension_semantics=("arbitrary",)),
    )(lg, p_ext, p_ext, p_ext)


def _retention_backward(p_ext, dret, lg, cos_t, sin_t, seq, ctx_len):
    lext = seq + ctx_len
    n_chunks = seq // RET_C
    C = RET_C

    def body(lg_ref, q_ref, k_ref, v_ref, do_ref, cos_ref, sin_ref, dq_ref, dk_ref, dv_ref, dlg_ref,
             kv_scr, g_scr, sf_scr, sb_scr, gfn_scr, gbp_scr):
        pair = pl.program_id(0)
        lane = lax.broadcasted_iota(jnp.int32, (1, LANES), 1)
        heads = range(2)
        hmask = [(lane // DK == hh).astype(F32) for hh in heads]
        lgf = [lg_ref[0, 2 * pair + hh] for hh in heads]
        lgb = [lg_ref[1, 2 * pair + hh] for hh in heads]
        vs = [slice(hh * DV, (hh + 1) * DV) for hh in heads]
        t = [_decay_tables(lgf[hh], lgb[hh], C) for hh in heads]
        tc = [_decay_tables(lgf[hh], lgb[hh], ctx_len) for hh in heads]
        kc_all = k_ref[seq:lext, :].astype(F32)
        kc = [kc_all * hmask[hh] for hh in heads]
        vc = [v_ref[seq:lext, vs[hh]] for hh in heads]
        kc_cat = [_cat_lanes(kc[hh] * tc[hh]["k_f"], kc[hh] * tc[hh]["k_b"]) for hh in heads]
        s0 = [_dot_tn(kc_cat[hh], vc[hh]) for hh in heads]

        def increments(c, carry):
            rows = pl.ds(pl.multiple_of(c * C, C), C)
            k_all = k_ref[rows, :].astype(F32)
            q_all = q_ref[rows, :].astype(F32)
            for hh in heads:
                k, q = k_all * hmask[hh], q_all * hmask[hh]
                kv_scr[hh, c] = _dot_tn(_cat_lanes(k * t[hh]["k_f"], k * t[hh]["k_b"]), v_ref[rows, vs[hh]])
                g_scr[hh, c] = _dot_tn(_cat_lanes(q * t[hh]["q_f"], q * t[hh]["q_b"]), do_ref[rows, vs[hh]])
            return carry

        lax.fori_loop(0, n_chunks, increments, 0, unroll=2)
        gf_c = [jnp.exp(lgf[hh] * C) for hh in heads]
        gb_c = [jnp.exp(lgb[hh] * C) for hh in heads]

        def scan_f(c, s):
            for hh in heads:
                sf_scr[hh, c] = s[hh]
            return tuple(gf_c[hh] * s[hh] + kv_scr[hh, c, 0:LANES, :] for hh in heads)

        def scan_b(n, s):
            c = n_chunks - 1 - n
            for hh in heads:
                sb_scr[hh, c] = s[hh]
            return tuple(gb_c[hh] * s[hh] + kv_scr[hh, c, LANES:2 * LANES, :] for hh in heads)

        def scan_gf(n, carry):
            c = n_chunks - 1 - n
            for hh in heads:
                gfn_scr[hh, c] = carry[hh]
            return tuple(g_scr[hh, c, 0:LANES, :] + gf_c[hh] * carry[hh] for hh in heads)

        def scan_gb(c, carry):
            for hh in heads:
                gbp_scr[hh, c] = carry[hh]
            return tuple(g_scr[hh, c, LANES:2 * LANES, :] + gb_c[hh] * carry[hh] for hh in heads)

        lax.fori_loop(0, n_chunks, scan_f, tuple(s0[hh][0:LANES] for hh in heads))
        lax.fori_loop(0, n_chunks, scan_b, tuple(s0[hh][LANES:2 * LANES] for hh in heads))
        zero_state = jnp.zeros((LANES, DV), F32)
        gf_first = lax.fori_loop(0, n_chunks, scan_gf, (zero_state, zero_state))
        gb_last = lax.fori_loop(0, n_chunks, scan_gb, (zero_state, zero_state))

        dmat, w_f, w_b = [], [], []
        for hh in heads:
            dm, diff = _decay_matrix(lgf[hh], lgb[hh], C)
            dmat.append(dm)
            w_f.append(jnp.where(diff > 0, diff * dm, 0.0))
            w_b.append(jnp.where(diff < 0, -diff * dm, 0.0))

        def total(a):
            rows_, width = a.shape
            part = jnp.sum(a.reshape(rows_ // 8, 8, width), axis=0)
            return part[:, 0:LANES] + part[:, LANES:2 * LANES] if width == 2 * LANES else part

        def chunk_head(hh, c, rows, q_all, k_all, dlgf, dlgb):
            th = t[hh]
            qm = q_all * hmask[hh]
            km = k_all * hmask[hh]
            qb, kb = qm.astype(BF16), km.astype(BF16)
            v = v_ref[rows, vs[hh]]
            do = do_ref[rows, vs[hh]]
            s = _dot_nt(qb, kb)
            dsd = _dot_nt(do, v)
            ds = (dsd * dmat[hh]).astype(BF16)
            dq_in = _dot(ds, kb)
            dk_in = _dot_tn(ds, qb)
            dv_in = _dot_tn((s * dmat[hh]).astype(BF16), do)
            prod = s * dsd
            dlgf = dlgf + total(prod * w_f[hh])
            dlgb = dlgb + total(prod * w_b[hh])
            sf, sb = sf_scr[hh, c], sb_scr[hh, c]
            states = jnp.concatenate([sf, sb], axis=0).astype(BF16)
            dqc = _dot_nt(do, states)
            dqf = dqc[:, 0:LANES] * th["q_f"]
            dqb = dqc[:, LANES:2 * LANES] * th["q_b"]
            dq = (dq_in + dqf + dqb) * hmask[hh]
            dlgf = dlgf + total((th["i"] + 1.0) * (qm * dqf))
            dlgb = dlgb + total((C - th["i"]) * (qm * dqb))
            gfn, gbp = gfn_scr[hh, c], gbp_scr[hh, c]
            gstates = jnp.concatenate([gfn, gbp], axis=0).astype(BF16)
            dkc = _dot_nt(v, gstates)
            dkf = dkc[:, 0:LANES] * th["k_f"]
            dkb = dkc[:, LANES:2 * LANES] * th["k_b"]
            dk = dk_in + dkf + dkb
            dlgf = dlgf + total((C - 1.0 - th["i"]) * (km * dkf)) + C * gf_c[hh] * total(gfn * sf)
            dlgb = dlgb + total(th["i"] * (km * dkb)) + C * gb_c[hh] * total(gbp * sb)
            dv = dv_in + _dot(_cat_lanes(km * th["k_f"], km * th["k_b"]), gstates)
            dv_ref[rows, vs[hh]] = dv.astype(BF16)
            return dq, dk, dlgf, dlgb

        def chunk(c, carry):
            rows = pl.ds(pl.multiple_of(c * C, C), C)
            q_all = q_ref[rows, :].astype(F32)
            k_all = k_ref[rows, :].astype(F32)
            dq0, dk0, f0, b0 = chunk_head(0, c, rows, q_all, k_all, carry[0], carry[1])
            dq1, dk1, f1, b1 = chunk_head(1, c, rows, q_all, k_all, carry[2], carry[3])
            cos, sin = cos_ref[rows, :], sin_ref[rows, :]
            dq_ref[rows, :] = _rope(dq0 + dq1, cos, -sin).astype(BF16)
            dk_ref[rows, :] = (_rope(dk0 + dk1, cos, -sin) * (DK ** -0.5)).astype(BF16)
            return f0, b0, f1, b1

        zero = jnp.zeros((8, LANES), F32)
        sums = lax.fori_loop(0, n_chunks, chunk, (zero, zero, zero, zero), unroll=2)

        dlg = []
        dk_ctx = jnp.zeros((ctx_len, LANES), F32)
        for hh in heads:
            g0 = jnp.concatenate([gf_first[hh], gb_last[hh]], axis=0).astype(BF16)
            dkcc = _dot_nt(vc[hh], g0)
            dkcf = dkcc[:, 0:LANES] * tc[hh]["k_f"]
            dkcb = dkcc[:, LANES:2 * LANES] * tc[hh]["k_b"]
            dlgf = sums[2 * hh] + total((ctx_len - 1.0 - tc[hh]["i"]) * (kc[hh] * dkcf))
            dlgb = sums[2 * hh + 1] + total(tc[hh]["i"] * (kc[hh] * dkcb))
            dk_ctx = dk_ctx + (dkcf + dkcb) * (DK ** -0.5)
            dv_ref[seq:lext, vs[hh]] = _dot(kc_cat[hh], g0).astype(BF16)
            dlg += [jnp.sum(jnp.sum(a, axis=1, keepdims=True), axis=0, keepdims=True) for a in (dlgf, dlgb)]
        dk_ref[seq:lext, :] = dk_ctx.astype(BF16)
        dq_ref[seq:lext, :] = jnp.zeros((ctx_len, LANES), BF16)

        lane8 = lax.broadcasted_iota(jnp.int32, (8, LANES), 1)
        out = jnp.zeros((8, LANES), F32)
        for n, val in enumerate(dlg):
            out = jnp.where(lane8 == n, val, out)
        dlg_ref[0] = out

    qk0 = CB_QK * D // LANES
    q_spec = pl.BlockSpec((lext, LANES), lambda g: (0, qk0 + g))
    k_spec = pl.BlockSpec((lext, LANES), lambda g: (0, qk0 + N_HEADS // 2 + g))
    v_spec = pl.BlockSpec((lext, 2 * DV), lambda g: (0, CB_V * D // (2 * DV) + g))
    table = pl.BlockSpec((lext, LANES), lambda g: (0, 0))
    state = pltpu.VMEM((2, n_chunks, LANES, DV), F32)
    return pl.pallas_call(
        body, name="retention_backward", grid=(N_HEADS // 2,),
        in_specs=[pl.BlockSpec(memory_space=pltpu.SMEM), q_spec, k_spec, v_spec,
                  pl.BlockSpec((seq, 2 * DV), lambda g: (0, g)), table, table],
        out_specs=[pl.BlockSpec((lext, LANES), lambda g: (0, g)), pl.BlockSpec((lext, LANES), lambda g: (0, g)),
                   pl.BlockSpec((lext, 2 * DV), lambda g: (0, g)), pl.BlockSpec((1, 8, LANES), lambda g: (g, 0, 0))],
        out_shape=[jax.ShapeDtypeStruct((lext, N_HEADS * DK), BF16), jax.ShapeDtypeStruct((lext, N_HEADS * DK), BF16),
                   jax.ShapeDtypeStruct((lext, D), BF16), jax.ShapeDtypeStruct((N_HEADS // 2, 8, LANES), F32)],
        scratch_shapes=[pltpu.VMEM((2, n_chunks, 2 * LANES, DV), F32), pltpu.VMEM((2, n_chunks, 2 * LANES, DV), F32),
                        state, state, state, state],
        compiler_params=_cparams(dimension_semantics=("arbitrary",)),
    )(lg, p_ext, p_ext, p_ext, dret, cos_t, sin_t)


def _merge(p_ext, rn, rstd, x, target, w_a, w_b, w_out, vecs, seq):
    n_tiles = seq // TM
    hb = TM // HALO

    def body(h_ref, bg_ref, cg_ref, za_ref, zb_ref, ga_ref, gb_ref, hp_ref, hn_ref, cp_ref, cn_ref,
             rn_ref, rstd_ref, x_ref, t_ref, wa_ref, wb_ref, wo_ref, vec_ref,
             dx1_ref, dpa_ref, dconv_ref, dret_ref, at_ref, b_ref, part_ref):
        i = pl.program_id(0)
        f = lambda ref: ref[...].astype(F32)
        h, bg, cg, za, zb, ga, gb = f(h_ref), f(bg_ref), f(cg_ref), f(za_ref), f(zb_ref), f(ga_ref), f(gb_ref)
        gx, w0, w1, w2 = vec_ref[0:1, :], vec_ref[1:2, :], vec_ref[2:3, :], vec_ref[3:4, :]
        cb, gnw, fw = vec_ref[4:5, :], vec_ref[5:6, :], vec_ref[6:7, :]
        u = cg * h
        row = lax.broadcasted_iota(jnp.int32, (TM, 1), 0)
        u_prev = (f(cp_ref) * f(hp_ref))[HALO - 1:HALO, :]
        u_next = (f(cn_ref) * f(hn_ref))[0:1, :]
        u_prev = jnp.where(i == 0, 0.0, u_prev)
        u_next = jnp.where(i == n_tiles - 1, 0.0, u_next)
        u_up = jnp.where(row == 0, u_prev, pltpu.roll(u, 1, 0))
        u_dn = jnp.where(row == TM - 1, u_next, pltpu.roll(u, TM - 1, 0))
        conv = w0 * u_up + w1 * u + w2 * u_dn + cb
        sza = _sigmoid(za)
        silu_za = za * sza
        a_act = silu_za * bg * conv
        rn = rn_ref[...]
        szb = _sigmoid(zb)
        silu_zb = zb * szb
        rg = rn * gnw
        b_act = silu_zb * rg
        y_a = _dot(a_act.astype(BF16), wa_ref[...])
        y_b = _dot(b_act.astype(BF16), wb_ref[...])
        sga, sgb = _sigmoid(ga), _sigmoid(gb)
        mix = sga * y_a + sgb * y_b
        y = _dot(mix.astype(BF16), wo_ref[...])
        x1 = x_ref[...] + gx * y
        r1 = lax.rsqrt(jnp.mean(x1 * x1, axis=-1, keepdims=True) + EPS)
        xh1 = x1 * r1
        err = xh1 * fw - t_ref[...]
        loss = jnp.sum(jnp.sum(err * err, axis=1, keepdims=True), axis=0, keepdims=True) * (0.5 / D)
        dout = err * (1.0 / D)
        dxh = dout * fw
        dx1 = r1 * (dxh - xh1 * jnp.mean(dxh * xh1, axis=-1, keepdims=True))
        dx1_ref[...] = dx1
        dy = (dx1 * gx).astype(BF16)
        dmix = _dot_nt(dy, wo_ref[...])
        dya = (dmix * sga).astype(BF16)
        dyb = (dmix * sgb).astype(BF16)
        da = _dot_nt(dya, wa_ref[...])
        db = _dot_nt(dyb, wb_ref[...])
        dpa_ref[:, 0:D] = (da * silu_za * conv).astype(BF16)
        dpa_ref[:, D:2 * D] = (da * bg * conv * (sza * (1.0 + za * (1.0 - sza)))).astype(BF16)
        dpa_ref[:, 2 * D:3 * D] = (db * rg * (szb * (1.0 + zb * (1.0 - szb)))).astype(BF16)
        dpa_ref[:, 3 * D:4 * D] = (dmix * y_a * sga * (1.0 - sga)).astype(BF16)
        dpa_ref[:, 4 * D:5 * D] = (dmix * y_b * sgb * (1.0 - sgb)).astype(BF16)
        dconv_ref[...] = (da * silu_za * bg).astype(BF16)
        drn_n = db * silu_zb
        drn = drn_n * gnw
        rstd = rstd_ref[...]
        for hd in range(N_HEADS):
            sl = slice(hd * DV, (hd + 1) * DV)
            dh_, rh = drn[:, sl], rn[:, sl]
            m1 = jnp.mean(dh_, axis=-1, keepdims=True)
            m2 = jnp.mean(dh_ * rh, axis=-1, keepdims=True)
            dret_ref[:, sl] = (rstd[:, sl] * (dh_ - m1 - rh * m2)).astype(BF16)
        at_ref[0] = a_act.T.astype(BF16)
        at_ref[1] = b_act.T.astype(BF16)
        at_ref[2] = mix.T.astype(BF16)
        b_ref[0] = dya
        b_ref[1] = dyb
        b_ref[2] = dy

        @pl.when(i == 0)
        def _():
            part_ref[...] = jnp.zeros((8, D), F32)

        part_ref[0:1, :] += jnp.sum(dout * xh1, axis=0, keepdims=True)
        part_ref[1:2, :] += jnp.sum(dx1 * y, axis=0, keepdims=True)
        part_ref[2:3, :] += jnp.sum(drn_n * rn, axis=0, keepdims=True)
        part_ref[3:4, :] += jnp.broadcast_to(loss, (1, D))

    col = lambda cb_: pl.BlockSpec((TM, D), lambda i, cb_=cb_: (i, cb_))
    prev = lambda cb_: pl.BlockSpec((HALO, D), lambda i, cb_=cb_: (jnp.maximum(i * hb - 1, 0), cb_))
    nxt = lambda cb_: pl.BlockSpec((HALO, D), lambda i, cb_=cb_: (jnp.minimum((i + 1) * hb, n_tiles * hb - 1), cb_))
    tile = pl.BlockSpec((TM, D), lambda i: (i, 0))
    full = lambda a: pl.BlockSpec(a.shape, lambda i: (0,) * a.ndim)
    return pl.pallas_call(
        body, name="merge", grid=(n_tiles,),
        in_specs=[col(CB_H), col(CB_BG), col(CB_CG), col(CB_ZA), col(CB_ZB), col(CB_GA), col(CB_GB),
                  prev(CB_H), nxt(CB_H), prev(CB_CG), nxt(CB_CG),
                  tile, tile, tile, tile, full(w_a), full(w_b), full(w_out), full(vecs)],
        out_specs=[tile, pl.BlockSpec((TM, 5 * D), lambda i: (i, 0)), tile, tile,
                   pl.BlockSpec((3, D, TM), lambda i: (0, 0, i)), pl.BlockSpec((3, TM, D), lambda i: (0, i, 0)),
                   pl.BlockSpec((8, D), lambda i: (0, 0))],
        out_shape=[jax.ShapeDtypeStruct((seq, D), F32), jax.ShapeDtypeStruct((seq, 5 * D), BF16),
                   jax.ShapeDtypeStruct((seq, D), BF16), jax.ShapeDtypeStruct((seq, D), BF16),
                   jax.ShapeDtypeStruct((3, D, seq), BF16), jax.ShapeDtypeStruct((3, seq, D), BF16),
                   jax.ShapeDtypeStruct((8, D), F32)],
        compiler_params=_cparams(dimension_semantics=("arbitrary",)),
    )(p_ext, p_ext, p_ext, p_ext, p_ext, p_ext, p_ext, p_ext, p_ext, p_ext, p_ext,
      rn, rstd, x, target, w_a, w_b, w_out, vecs)


def _conv_backward(p_ext, dconv, dpa, dq, dk, dv, vecs, seq):
    n_tiles = seq // TM
    hb = TM // HALO
    qk = N_HEADS * DK

    def body(h_ref, cg_ref, dc_ref, dcp_ref, dcn_ref, dpa_ref, dq_ref, dk_ref, dv_ref, vec_ref, dp_ref, part_ref):
        i = pl.program_id(0)
        dp_ref[:, CB_QK * D:CB_QK * D + qk] = dq_ref[...]
        dp_ref[:, CB_QK * D + qk:CB_V * D] = dk_ref[...]
        dp_ref[:, CB_V * D:CB_ZB * D] = dv_ref[...]

        @pl.when(i == 0)
        def _():
            part_ref[...] = jnp.zeros((8, D), F32)

        @pl.when(i == n_tiles)
        def _():
            dp_ref[:, 0:CB_QK * D] = jnp.zeros((TM, CB_QK * D), BF16)
            dp_ref[:, CB_ZB * D:PW] = jnp.zeros((TM, PW - CB_ZB * D), BF16)

        @pl.when(i < n_tiles)
        def _():
            f = lambda ref: ref[...].astype(F32)
            h, cg, dc = f(h_ref), f(cg_ref), f(dc_ref)
            w0, w1, w2 = vec_ref[1:2, :], vec_ref[2:3, :], vec_ref[3:4, :]
            row = lax.broadcasted_iota(jnp.int32, (TM, 1), 0)
            dc_prev = jnp.where(i == 0, 0.0, f(dcp_ref)[HALO - 1:HALO, :])
            dc_next = jnp.where(i == n_tiles - 1, 0.0, f(dcn_ref)[0:1, :])
            dc_up = jnp.where(row == 0, dc_prev, pltpu.roll(dc, 1, 0))
            dc_dn = jnp.where(row == TM - 1, dc_next, pltpu.roll(dc, TM - 1, 0))
            du = w0 * dc_dn + w1 * dc + w2 * dc_up
            u = cg * h
            dp_ref[:, CB_H * D:(CB_H + 1) * D] = (du * cg).astype(BF16)
            dp_ref[:, CB_CG * D:(CB_CG + 1) * D] = (du * h).astype(BF16)
            dp_ref[:, CB_BG * D:(CB_BG + 1) * D] = dpa_ref[:, 0:D]
            dp_ref[:, CB_ZA * D:(CB_ZA + 1) * D] = dpa_ref[:, D:2 * D]
            dp_ref[:, CB_ZB * D:(CB_GB + 1) * D] = dpa_ref[:, 2 * D:5 * D]
            part_ref[0:1, :] += jnp.sum(u * dc_dn, axis=0, keepdims=True)
            part_ref[1:2, :] += jnp.sum(u * dc, axis=0, keepdims=True)
            part_ref[2:3, :] += jnp.sum(u * dc_up, axis=0, keepdims=True)
            part_ref[3:4, :] += jnp.sum(dc, axis=0, keepdims=True)

    last = n_tiles - 1
    col = lambda cb_: pl.BlockSpec((TM, D), lambda i, cb_=cb_: (jnp.minimum(i, last), cb_))
    lat = lambda w: pl.BlockSpec((TM, w), lambda i: (jnp.minimum(i, last), 0))
    ext = lambda w: pl.BlockSpec((TM, w), lambda i: (i, 0))
    return pl.pallas_call(
        body, name="conv_backward", grid=(n_tiles + 1,),
        in_specs=[col(CB_H), col(CB_CG), lat(D),
                  pl.BlockSpec((HALO, D), lambda i: (jnp.clip(i * hb - 1, 0, n_tiles * hb - 1), 0)),
                  pl.BlockSpec((HALO, D), lambda i: (jnp.minimum((i + 1) * hb, n_tiles * hb - 1), 0)),
                  lat(5 * D), ext(qk), ext(qk), ext(D),
                  pl.BlockSpec(vecs.shape, lambda i: (0, 0))],
        out_specs=[ext(PW), pl.BlockSpec((8, D), lambda i: (0, 0))],
        out_shape=[jax.ShapeDtypeStruct((seq + TM, PW), BF16), jax.ShapeDtypeStruct((8, D), F32)],
        compiler_params=_cparams(dimension_semantics=("arbitrary",)),
    )(p_ext, p_ext, dconv, dconv, dconv, dpa, dq, dk, dv, vecs)


def _input_backward(dp, shards, ids, x, ctx, dx1, modx, modc, norm_w):
    seq = x.shape[0]
    lext = seq + ctx.shape[0]
    n_x = seq // TM
    n_w = len(shards.arrays)

    def body(ids_ref, dp_ref, x_ref, ctx_ref, dx1_ref, mx_ref, mc_ref, nw_ref, *rest):
        w_hbm, (gx_ref, part_ref, w_ref, w_sems) = rest[:n_w], rest[n_w:]
        i = pl.program_id(0)
        _load_w_in(i, ids_ref, shards, w_hbm, w_ref, w_sems)
        is_ctx = i >= n_x
        dxm = _dot_nt(dp_ref[...], w_ref[...])
        x = jnp.where(is_ctx, ctx_ref[...], x_ref[...])
        r = lax.rsqrt(jnp.mean(x * x, axis=-1, keepdims=True) + EPS)
        xh = x * r
        nw = nw_ref[...]
        sc = jnp.where(is_ctx, mc_ref[1:2, :], mx_ref[1:2, :])
        dxn = dxm * (1.0 + sc)
        dxh = dxn * nw
        dx = r * (dxh - xh * jnp.mean(dxh * xh, axis=-1, keepdims=True))

        @pl.when(jnp.logical_not(is_ctx))
        def _():
            gx_ref[...] = dx1_ref[...] + dx

        @pl.when(i == 0)
        def _():
            part_ref[...] = jnp.zeros((8, D), F32)

        fx = jnp.where(is_ctx, 0.0, 1.0)
        d_shift = jnp.sum(dxm, axis=0, keepdims=True)
        d_scale = jnp.sum(dxm * (xh * nw), axis=0, keepdims=True)
        part_ref[0:1, :] += fx * d_shift
        part_ref[1:2, :] += fx * d_scale
        part_ref[2:3, :] += jnp.sum(dxn * xh, axis=0, keepdims=True)
        part_ref[3:4, :] += (1.0 - fx) * d_shift
        part_ref[4:5, :] += (1.0 - fx) * d_scale

    lat = lambda w: pl.BlockSpec((TM, w), lambda i, ids_: (jnp.minimum(i, n_x - 1), 0))
    ext = lambda w: pl.BlockSpec((TM, w), lambda i, ids_: (i, 0))
    full = lambda a: pl.BlockSpec(a.shape, lambda i, ids_: (0,) * a.ndim)
    grid_spec = pltpu.PrefetchScalarGridSpec(
        num_scalar_prefetch=1, grid=(lext // TM,),
        in_specs=[ext(PW), lat(D), full(ctx), lat(D), full(modx), full(modc), full(norm_w)]
        + [pl.BlockSpec(memory_space=pl.ANY)] * n_w,
        out_specs=[lat(D), pl.BlockSpec((8, D), lambda i, ids_: (0, 0))],
        scratch_shapes=[pltpu.VMEM((D, PW), BF16), pltpu.SemaphoreType.DMA((N_DEV,))])
    return pl.pallas_call(
        body, name="input_backward", grid_spec=grid_spec,
        out_shape=[jax.ShapeDtypeStruct((seq, D), F32), jax.ShapeDtypeStruct((8, D), F32)],
        compiler_params=_cparams(dimension_semantics=("arbitrary",)),
    )(ids, dp, x, ctx, dx1, modx, modc, norm_w, *shards.arrays)


def _weight_grad_in(xm_t, dp, owners, name):
    lext = xm_t.shape[1]

    def body(own_ref, a_ref, b_ref, o_ref):
        o_ref[0] = _dot(a_ref[...], b_ref[...])

    grid_spec = pltpu.PrefetchScalarGridSpec(
        num_scalar_prefetch=1, grid=(4,),
        in_specs=[pl.BlockSpec((D, lext), lambda j, own: (0, 0)),
                  pl.BlockSpec((lext, WSH), lambda j, own: (0, own[j]))],
        out_specs=pl.BlockSpec((1, D, WSH), lambda j, own: (j, 0, 0)))
    return pl.pallas_call(
        body, name=name, grid_spec=grid_spec,
        out_shape=jax.ShapeDtypeStruct((4, D, WSH), F32),
        compiler_params=_cparams(dimension_semantics=("arbitrary",)),
    )(owners, xm_t, dp)


def _weight_grad_square(a_t, b):
    seq = a_t.shape[2]

    def body(a_ref, b_ref, o_ref):
        o_ref[:, 0] = _dot(a_ref[0], b_ref[0]).reshape(N_DEV, RSH, D)

    return pl.pallas_call(
        body, name="weight_grad_square", grid=(3,),
        in_specs=[pl.BlockSpec((1, D, seq), lambda t: (t, 0, 0)), pl.BlockSpec((1, seq, D), lambda t: (t, 0, 0))],
        out_specs=pl.BlockSpec((N_DEV, 1, RSH, D), lambda t: (0, t, 0, 0)),
        out_shape=jax.ShapeDtypeStruct((N_DEV, 3, RSH, D), F32),
        compiler_params=_cparams(dimension_semantics=("arbitrary",)),
    )(a_t, b)


def _rope_tables(seq, ctx_len, token):
    rows = seq // GRID_W
    row = jnp.repeat(jnp.arange(rows), GRID_W).astype(F32)
    col = jnp.tile(jnp.arange(GRID_W), rows).astype(F32)
    nf = DK // 4
    inv = ROPE_BASE ** (-jnp.arange(nf, dtype=F32) / nf)
    ang = _tie(jnp.concatenate([row[:, None] * inv, col[:, None] * inv], axis=-1), token)
    cos, sin = jnp.cos(ang), jnp.sin(ang)
    cos_t = jnp.tile(cos, (1, 4))
    sin_t = jnp.tile(jnp.concatenate([-sin, sin], axis=-1), (1, 2))
    cos_t = jnp.concatenate([cos_t, jnp.ones((ctx_len, LANES), F32)], axis=0)
    sin_t = jnp.concatenate([sin_t, jnp.zeros((ctx_len, LANES), F32)], axis=0)
    return cos_t, sin_t


def _tie(value, token):
    return value + token[0, 0].astype(value.dtype)


def _pad_lanes(a):
    return jnp.pad(a, ((0, 0), (0, D - a.shape[1])))


def _rows(total, *parts):
    width = max(a.shape[1] for _, a in parts)
    out = None
    for row, a in parts:
        padded = jnp.pad(a, ((row, total - row - a.shape[0]), (0, width - a.shape[1])))
        out = padded if out is None else out + padded
    return out


def kernel(x, c, ctx, c_ctx, norm_w, ada_w, ada_b, w_in, conv_w, conv_b, decay_logit, gn_w, w_a, w_b, w_out, final_norm_w, loss_target, m_c_ctx, m_norm_w, m_ada_w, m_ada_b, m_w_in, m_conv_w, m_conv_b, m_decay_logit, m_gn_w, m_w_a, m_w_b, m_w_out, m_final_norm_w, v_c_ctx, v_norm_w, v_ada_w, v_ada_b, v_w_in, v_conv_w, v_conv_b, v_decay_logit, v_gn_w, v_w_a, v_w_b, v_w_out, v_final_norm_w):
    xi, yi, ci = _coords()
    me = 4 * xi + 2 * yi + ci
    chip = 2 * xi + yi
    seq, ctx_len = x.shape[1], ctx.shape[1]
    assert seq % TM == 0 and seq % RET_C == 0 and ctx_len == TM and seq % GRID_W == 0
    csh = D // N_DEV

    blk = jnp.pad(c, ((0, 7), (0, 0))) + jnp.pad(conv_w[0], ((1, 4), (0, D - csh)))
    got, _ = _all_gather_small(blk, "gather_cond")
    conv_w_all = got[:, 1:4, 0:csh].transpose(1, 0, 2).reshape(3, D)
    c16 = _rows(16, (0, got[:, 0, :]), (N_DEV, c_ctx[None]))
    ada_b_sh = lax.dynamic_slice(ada_b, (0, me * ADA_SH), (1, ADA_SH))
    mod_sh, act16, lg = _modulation(c16, ada_w[0], ada_b_sh, decay_logit[0])
    mod_all, small_done = _all_gather_small(mod_sh, "gather_mod")
    mod_all = mod_all.transpose(1, 0, 2).reshape(16, 3 * D)
    modx = lax.dynamic_slice(mod_all, (me, 0), (1, 3 * D)).reshape(3, D)
    modc = mod_all[8].reshape(3, D)

    x2, tgt = x[0], loss_target[0]
    wblock = lambda n: jax.ShapeDtypeStruct((n, D, WSH), BF16)
    ids_of = lambda ks: jnp.stack([jnp.bitwise_xor(me, k) for k in ks]).astype(jnp.int32)
    own_shard = _tie(w_in[0].astype(BF16)[None], small_done)
    to_sib = _Exchange("gather_in_sibling", [own_shard], [wblock(1)], 1, _send_plan((1,)))
    to_nbr = _Exchange("gather_in_neighbours", [_tie(own_shard, to_sib.token)], [wblock(2)], 2, _send_plan((4, 2)))
    cos_t, sin_t = _rope_tables(seq, ctx_len, to_nbr.token)
    modx = _tie(modx, to_nbr.token)
    xm, xm_t = _prenorm(x2, ctx[0], modx, modc, norm_w)
    (own_shard,), (w_sib,) = to_sib.wait(xm_t)
    first = _Shards([own_shard, w_sib], [(0, 0), (1, 0)])
    p_ext = _in_projection(None, xm, cos_t, sin_t, first, ids_of((0, 1)), "in_projection_pair")
    _, (w_nbr,) = to_nbr.wait(p_ext)
    fwd_nbr = _Exchange("gather_in_forward_neighbours", [w_nbr], [wblock(2)], 2, _forward_plan(2))
    to_diag = _Exchange("gather_in_diagonal", [_tie(own_shard, fwd_nbr.token)], [wblock(1)], 1, _send_plan((6,)))
    (w_nbr,), (w_nbr_sib,) = fwd_nbr.wait(to_diag.token)
    second = _Shards([w_nbr, w_nbr_sib], [(0, 0), (0, 1), (1, 0), (1, 1)])
    p_ext = _in_projection(p_ext, xm, cos_t, sin_t, second, ids_of((4, 2, 5, 3)), "in_projection_neighbours")
    _, (w_diag,) = to_diag.wait(p_ext)
    fwd_diag = _Exchange("gather_in_forward_diagonal", [w_diag], [wblock(1)], 1, _forward_plan(1))
    sq_shards = [_tie(w[0].astype(BF16), fwd_diag.token) for w in (w_a, w_b, w_out)]
    ag_sq = _Exchange("gather_square", sq_shards, [jax.ShapeDtypeStruct((N_DEV, RSH, D), BF16)] * 3,
                      3 * (N_DEV - 1), _gather_plan)
    (w_diag,), (w_diag_sib,) = fwd_diag.wait(ag_sq.token)
    third = _Shards([w_diag, w_diag_sib], [(0, 0), (1, 0)])
    p_ext = _in_projection(p_ext, xm, cos_t, sin_t, third, ids_of((6, 7)), "in_projection_diagonal")
    w_shards, w_ids = first + second + third, ids_of((0, 1, 4, 2, 5, 3, 6, 7))

    rn, rstd = _retention_forward(p_ext, lg, seq, ctx_len)
    _, sq_lands = ag_sq.wait(rstd)
    w_a_all, w_b_all, w_out_all = (
        lax.dynamic_update_slice(land, shard[None], (me, 0, 0)).reshape(D, D) for land, shard in zip(sq_lands, sq_shards))
    vecs = _rows(8, (0, modx[2:3]), (1, conv_w_all), (4, conv_b), (5, gn_w), (6, final_norm_w[None]))
    dx1, dpa, dconv, dret, op_at, op_b, part_m = _merge(p_ext, rn, rstd, x2, tgt, w_a_all, w_b_all, w_out_all, vecs, seq)

    j4 = jnp.arange(4, dtype=jnp.int32)
    owners = (2 * jnp.bitwise_xor(chip, j4) + ci).astype(jnp.int32)
    owners_sib = (2 * jnp.bitwise_xor(chip, j4) + (1 - ci)).astype(jnp.int32)
    gw_sq = _weight_grad_square(op_at, op_b).reshape(N_DEV, 3 * RSH, D)
    rs_sq_pair = _Exchange("rs_square_pair", [gw_sq], [jax.ShapeDtypeStruct((4, 3 * RSH, D), F32)], 4,
                           _pair_plan(lambda j, chip_, c_: 2 * jnp.bitwise_xor(chip_, j) + (1 - c_)))
    dq, dk, dv, dlg = _retention_backward(p_ext, dret, _tie(lg, rs_sq_pair.token), cos_t, sin_t, seq, ctx_len)
    (gw_sq,), (r1_sq,) = rs_sq_pair.wait(dlg)
    own_sq, send_sq = _pair_sum(gw_sq, r1_sq, owners, "pair_sum_square")
    rs_sq_chips = _Exchange("rs_square_chips", [send_sq], [jax.ShapeDtypeStruct((3, 3 * RSH, D), BF16)], 3, _chips_plan)
    dp, part_c = _conv_backward(p_ext, dconv, dpa, dq, dk, dv, _tie(vecs, rs_sq_chips.token), seq)
    gw_sib = _weight_grad_in(xm_t, dp, owners_sib, "weight_grad_in_sibling")
    rs_in_pair = _Exchange("rs_in_pair", [gw_sib], [jax.ShapeDtypeStruct((4, D, WSH), F32)], 4,
                           _pair_plan(lambda j, chip_, c_: j))
    gw_own = _weight_grad_in(xm_t, dp, _tie(owners, rs_in_pair.token), "weight_grad_in_own")
    _, (r1_in,) = rs_in_pair.wait(gw_own)
    own_in, send_in = _pair_sum(gw_own, r1_in, j4, "pair_sum_in")
    rs_in_chips = _Exchange("rs_in_chips", [send_in], [jax.ShapeDtypeStruct((3, D, WSH), BF16)], 3, _chips_plan)
    grad_x, part_i = _input_backward(dp, w_shards, w_ids, x2, ctx[0], dx1, _tie(modx, rs_in_chips.token), modc, norm_w)

    dl = dlg[:, 0, 0:4]
    dlg_row = jnp.pad(dl[:, 0::2].reshape(1, N_HEADS), ((0, 0), (0, D - N_HEADS))) + jnp.pad(
        dl[:, 1::2].reshape(1, N_HEADS), ((0, 0), (N_HEADS, D - 2 * N_HEADS)))
    partials = _rows(16, (0, part_m[0:1]), (1, part_i[2:3]), (2, part_c[3:4]), (3, part_m[2:3]), (4, part_c[0:3]),
                     (7, part_i[0:2]), (9, part_m[1:2]), (10, part_i[3:5]), (13, part_m[3:4]), (14, dlg_row))
    got, _ = _all_gather_small(partials, "gather_partials")
    tot = _sum_devices(got, "sum_partials")
    dmodc = tot[10:13].reshape(1, 3 * D)
    dmod16 = _rows(16, (0, got[:, 7:10, :].reshape(N_DEV, 3 * D)), (N_DEV, dmodc))
    dmod16 = lax.dynamic_slice(dmod16, (0, me * ADA_SH), (16, ADA_SH))
    dmodc8 = _rows(8, (0, dmod16[8:9]))
    g_ada_w, d_ada_w, nm_ada_w, nv_ada_w, cctx_part = _ada_backward(
        act16.T, dmod16, dmodc8, ada_w[0], m_ada_w[0], v_ada_w[0])
    dact_ctx = _sum_devices(_all_gather_small(cctx_part, "gather_cctx")[0], "sum_cctx")

    view = {"c_ctx": (1, D), "final_norm_w": (1, D), "decay_logit": (1, 2 * N_HEADS), "conv_w": (3, csh)}
    given = {"c_ctx": (c_ctx, m_c_ctx, v_c_ctx), "norm_w": (norm_w, m_norm_w, v_norm_w),
             "conv_b": (conv_b, m_conv_b, v_conv_b), "gn_w": (gn_w, m_gn_w, v_gn_w),
             "final_norm_w": (final_norm_w, m_final_norm_w, v_final_norm_w), "ada_b": (ada_b, m_ada_b, v_ada_b),
             "decay_logit": (decay_logit, m_decay_logit, v_decay_logit), "conv_w": (conv_w, m_conv_w, v_conv_w)}
    small_in = [tuple(a.reshape(view.get(name, a.shape)) for a in given[name]) for name in SMALL_PARAMS]
    conv_w_grad = lax.dynamic_slice(tot, (4, me * csh), (3, csh))
    small_out = _adam_small(tot, dact_ctx, conv_w_grad, small_in)
    results = {name: [o.reshape(given[name][0].shape) for o in outs_]
               for name, outs_ in zip(SMALL_PARAMS, small_out)}

    _, (r2_sq,) = rs_sq_chips.wait(small_out[0][0])
    _, (r2_in,) = rs_in_chips.wait(small_out[0][0])
    results["w_in"] = [o[None] for o in _adam_sharded(w_in[0], m_w_in[0], v_w_in[0], own_in, r2_in, "adam_w_in")]
    square = _adam_square([(w_a[0], m_w_a[0], v_w_a[0]), (w_b[0], m_w_b[0], v_w_b[0]),
                           (w_out[0], m_w_out[0], v_w_out[0])], own_sq, r2_sq)
    for name, outs_ in zip(("w_a", "w_b", "w_out"), square):
        results[name] = [o[None] for o in outs_]
    results["ada_w"] = [o[None] for o in (g_ada_w, d_ada_w, nm_ada_w, nv_ada_w)]

    order = ("c_ctx", "norm_w", "ada_w", "ada_b", "w_in", "conv_w", "conv_b", "decay_logit", "gn_w",
             "w_a", "w_b", "w_out", "final_norm_w")
    outs = [results[name][kind] for kind in range(4) for name in order]
    return (tot[13, 0], grad_x[None], *outs)
```

```python
import functools
import math

import jax
import jax.numpy as jnp
from jax import lax
from jax.experimental import pallas as pl
from jax.experimental.pallas import tpu as pltpu

F32 = jnp.float32
BF16 = jnp.bfloat16
MESH = pl.DeviceIdType.MESH
_HBM_SPEC = pl.BlockSpec(memory_space=pltpu.HBM)
_SEM_SPEC = pl.BlockSpec(memory_space=pltpu.SEMAPHORE)
_EFFECT = pltpu.SideEffectType.DATAFLOW_SIDE_EFFECTING

N_DEV = 8
D = 1024
N_HEADS = 8
DK = 64
DV = 128
GRID_W = 64
ROPE_BASE = 10000.0
EPS = 1e-6
PW = 9 * D
WSH = PW // N_DEV
RSH = D // N_DEV
ADA_SH = 3 * D // N_DEV
TM = 256
RET_C = 256
HALO = 16
LANES = 128
VMEM_LIMIT = 60 * 1024 * 1024

ADAM_LR = 0.001
ADAM_B1 = 0.9
ADAM_B2 = 0.999
ADAM_EPS = 1e-08
ADAM_WD = 0.01
ADAM_STEP = 10

CB_H, CB_BG, CB_CG, CB_ZA, CB_QK, CB_V, CB_ZB, CB_GA, CB_GB = range(9)


def _cparams(**kw):
    return pltpu.CompilerParams(vmem_limit_bytes=VMEM_LIMIT, **kw)


def _dot(a, b):
    return jnp.dot(a, b, preferred_element_type=F32)


def _dot_nt(a, b):
    return lax.dot_general(a, b, (((1,), (1,)), ((), ())), preferred_element_type=F32)


def _dot_tn(a, b):
    return lax.dot_general(a, b, (((0,), (0,)), ((), ())), preferred_element_type=F32)


def _sigmoid(z):
    return 0.5 * jnp.tanh(0.5 * z) + 0.5


def _row_tile(rows):
    return TM if rows % TM == 0 else rows


def _coords():
    return lax.axis_index("x"), lax.axis_index("y"), lax.axis_index("c")


def _flip(v, bit):
    return 1 - v if bit else v


def _all_gather_small(blk, name):
    rows, cols = blk.shape

    def body(x_ref, out_ref, done, send_sems, recv_sems, local_sem):
        x, y, c = _coords()
        me = 4 * x + 2 * y + c
        mine = pltpu.make_async_copy(x_ref, out_ref.at[me], local_sem)
        mine.start()

        def copy(k, slot):
            peer = (_flip(x, k & 4), _flip(y, k & 2), _flip(c, k & 1))
            return pltpu.make_async_remote_copy(
                src_ref=x_ref, dst_ref=out_ref.at[slot], send_sem=send_sems.at[k - 1],
                recv_sem=recv_sems.at[k - 1], device_id=peer, device_id_type=MESH)

        for k in range(1, N_DEV):
            copy(k, me).start()
        for k in range(1, N_DEV):
            copy(k, jnp.bitwise_xor(me, k)).wait_recv()
        for k in range(1, N_DEV):
            copy(k, me).wait_send()
        mine.wait()
        done[...] = jnp.zeros((8, LANES), F32)

    vmem = pl.BlockSpec(memory_space=pltpu.VMEM)
    return pl.pallas_call(
        body, name=name,
        out_shape=[jax.ShapeDtypeStruct((N_DEV, rows, cols), blk.dtype), jax.ShapeDtypeStruct((8, LANES), F32)],
        in_specs=[vmem], out_specs=[vmem, vmem],
        scratch_shapes=[pltpu.SemaphoreType.DMA((N_DEV - 1,)), pltpu.SemaphoreType.DMA((N_DEV - 1,)),
                        pltpu.SemaphoreType.DMA],
    )(blk)


class _Exchange:
    def __init__(self, name, srcs, land_shapes, n_copies, plan):
        self.name, self.plan, self.n_copies = name, plan, n_copies
        self.n_src, self.n_land = len(srcs), len(land_shapes)
        hbm = lambda a: pltpu.HBM(a.shape, a.dtype)
        n = self.n_src + self.n_land
        lands = [pltpu.with_memory_space_constraint(lax.empty(s.shape, s.dtype), pltpu.HBM) for s in land_shapes]
        srcs = [pltpu.with_memory_space_constraint(s, pltpu.HBM) for s in srcs]

        def body(*refs):
            send_sems, recv_sems = refs[n], refs[n + 1]
            for cp in self._copies(refs, send_sems, recv_sems):
                cp.start()
            refs[-1][...] = jnp.zeros((8, LANES), F32)

        outs = pl.pallas_call(
            body, name=name + "_start",
            out_shape=(pltpu.SemaphoreType.DMA((n_copies,)), pltpu.SemaphoreType.DMA((n_copies,)),
                       *[hbm(a) for a in srcs], *[hbm(a) for a in lands], jax.ShapeDtypeStruct((8, LANES), F32)),
            in_specs=[_HBM_SPEC] * n,
            out_specs=(_SEM_SPEC, _SEM_SPEC, *[_HBM_SPEC] * n, pl.BlockSpec(memory_space=pltpu.VMEM)),
            input_output_aliases={i: 2 + i for i in range(n)},
            compiler_params=pltpu.CompilerParams(has_side_effects=_EFFECT),
        )(*srcs, *lands)
        self.send_sems, self.recv_sems = outs[0], outs[1]
        self.buffers = list(outs[2:2 + n])
        self.token = outs[-1]

    def _copies(self, refs, send_sems, recv_sems):
        src_refs, land_refs = refs[:self.n_src], refs[self.n_src:self.n_src + self.n_land]
        return [pltpu.make_async_remote_copy(src_ref=s, dst_ref=d, send_sem=send_sems.at[k], recv_sem=recv_sems.at[k],
                                             device_id=dev, device_id_type=MESH)
                for k, (s, d, dev) in enumerate(self.plan(src_refs, land_refs))]

    def wait(self, after):
        n = self.n_src + self.n_land

        def body(*refs):
            for cp in self._copies(refs, refs[n], refs[n + 1]):
                cp.wait_send()
                cp.wait_recv()

        outs = pl.pallas_call(
            body, name=self.name + "_wait",
            out_shape=tuple(pltpu.HBM(b.shape, b.dtype) for b in self.buffers),
            in_specs=[_HBM_SPEC] * n + [_SEM_SPEC, _SEM_SPEC, pl.BlockSpec(memory_space=pl.ANY)],
            out_specs=tuple([_HBM_SPEC] * n),
            input_output_aliases={i: i for i in range(n)},
            compiler_params=pltpu.CompilerParams(has_side_effects=_EFFECT),
        )(*self.buffers, self.send_sems, self.recv_sems, after)
        return list(outs[:self.n_src]), list(outs[self.n_src:])


def _pair_plan(src_index):
    def plan(srcs, lands):
        x, y, c = _coords()
        chip = 2 * x + y
        return [(srcs[0].at[src_index(j, chip, c)], lands[0].at[j], (x, y, 1 - c)) for j in range(4)]
    return plan


def _chips_plan(srcs, lands):
    x, y, c = _coords()
    return [(srcs[0].at[j - 1], lands[0].at[j - 1], (_flip(x, j & 2), _flip(y, j & 1), c)) for j in range(1, 4)]


def _peer(k):
    x, y, c = _coords()
    return _flip(x, k & 4), _flip(y, k & 2), _flip(c, k & 1)


def _send_plan(ks):
    def plan(srcs, lands):
        return [(srcs[0].at[0], lands[0].at[p], _peer(k)) for p, k in enumerate(ks)]
    return plan


def _forward_plan(moves):
    def plan(srcs, lands):
        return [(srcs[0].at[s], srcs[0].at[d], _peer(1)) for s, d in moves]
    return plan


def _gather_plan(srcs, lands):
    x, y, c = _coords()
    me = 4 * x + 2 * y + c
    return [(srcs[a], lands[a].at[me], (_flip(x, k & 4), _flip(y, k & 2), _flip(c, k & 1)))
            for a in range(len(srcs)) for k in range(1, N_DEV)]


def _pair_sum(grad, recv, owners, name):
    _, rows, cols = grad.shape
    tr = _row_tile(rows)

    def body(own_ref, g_ref, r_ref, mine_ref, send_ref):
        j = pl.program_id(1)
        total = g_ref[...] + r_ref[...]

        @pl.when(j == 0)
        def _():
            mine_ref[...] = total

        @pl.when(j > 0)
        def _():
            send_ref[...] = total.astype(BF16)

    grid_spec = pltpu.PrefetchScalarGridSpec(
        num_scalar_prefetch=1, grid=(rows // tr, 4),
        in_specs=[pl.BlockSpec((1, tr, cols), lambda i, j, own: (own[j], i, 0)),
                  pl.BlockSpec((1, tr, cols), lambda i, j, own: (j, i, 0))],
        out_specs=[pl.BlockSpec((1, tr, cols), lambda i, j, own: (0, i, 0)),
                   pl.BlockSpec((1, tr, cols), lambda i, j, own: (jnp.maximum(j - 1, 0), i, 0))])
    return pl.pallas_call(
        body, name=name, grid_spec=grid_spec,
        out_shape=[jax.ShapeDtypeStruct((1, rows, cols), F32), jax.ShapeDtypeStruct((3, rows, cols), BF16)],
        compiler_params=_cparams(dimension_semantics=("arbitrary", "arbitrary")),
    )(owners, grad, recv)


def _modulation(c16, ada_w_sh, ada_b_sh, decay_logit):
    def body(c_ref, w_ref, b_ref, dl_ref, mod_ref, act_ref, lg_ref):
        cv = c_ref[...]
        act = cv * _sigmoid(cv)
        act_ref[...] = act
        mod_ref[...] = jnp.dot(act, w_ref[...], preferred_element_type=F32,
                               precision=lax.Precision.HIGHEST) + b_ref[...]
        z = dl_ref[...]
        lg_ref[...] = jnp.minimum(z, 0.0) - jnp.log(1.0 + jnp.exp(-jnp.abs(z)))

    return pl.pallas_call(
        body, name="modulation",
        out_shape=[jax.ShapeDtypeStruct((16, ADA_SH), F32), jax.ShapeDtypeStruct((16, D), F32),
                   jax.ShapeDtypeStruct(decay_logit.shape, F32)],
        compiler_params=_cparams(),
    )(c16, ada_w_sh, ada_b_sh, decay_logit)


def _adam_update(w, g, m, v):
    m2 = ADAM_B1 * m + (1.0 - ADAM_B1) * g
    v2 = ADAM_B2 * v + (1.0 - ADAM_B2) * (g * g)
    m_hat = m2 / (1.0 - ADAM_B1 ** ADAM_STEP)
    v_hat = v2 / (1.0 - ADAM_B2 ** ADAM_STEP)
    delta = -ADAM_LR * (m_hat / (jnp.sqrt(v_hat) + ADAM_EPS) + ADAM_WD * w)
    return delta, m2, v2


def _adam_sharded(w, m, v, own, recv, name):
    rows, cols = w.shape
    tr = _row_tile(rows)

    def body(w_ref, m_ref, v_ref, g0, g1, g2, g3, g_ref, d_ref, m_out, v_out):
        g = ((g0[0] + g1[0].astype(F32)) + g2[0].astype(F32)) + g3[0].astype(F32)
        delta, m2, v2 = _adam_update(w_ref[...], g, m_ref[...], v_ref[...])
        g_ref[...] = g
        d_ref[...] = delta
        m_out[...] = m2
        v_out[...] = v2

    flat = pl.BlockSpec((tr, cols), lambda i: (i, 0))
    part = lambda j: pl.BlockSpec((1, tr, cols), lambda i, j=j: (j, i, 0))
    return pl.pallas_call(
        body, name=name, grid=(rows // tr,),
        in_specs=[flat, flat, flat, part(0), part(0), part(1), part(2)],
        out_specs=[flat] * 4,
        out_shape=[jax.ShapeDtypeStruct((rows, cols), F32)] * 4,
        compiler_params=_cparams(dimension_semantics=("arbitrary",)),
    )(w, m, v, own, recv, recv, recv)


def _ada_backward(act_t, dmod16, dmodc8, ada_w_sh, m, v):
    def body(at_ref, dm_ref, dc_ref, w_ref, m_ref, v_ref, g_ref, d_ref, m_out, v_out, pc_ref):
        g = jnp.dot(at_ref[...], dm_ref[...], preferred_element_type=F32, precision=lax.Precision.HIGHEST)
        w = w_ref[...]
        delta, m2, v2 = _adam_update(w, g, m_ref[...], v_ref[...])
        g_ref[...] = g
        d_ref[...] = delta
        m_out[...] = m2
        v_out[...] = v2
        pc_ref[...] = lax.dot_general(dc_ref[...], w, (((1,), (1,)), ((), ())), preferred_element_type=F32,
                                      precision=lax.Precision.HIGHEST)

    return pl.pallas_call(
        body, name="ada_backward",
        out_shape=[jax.ShapeDtypeStruct((D, ADA_SH), F32)] * 4 + [jax.ShapeDtypeStruct((8, D), F32)],
        compiler_params=_cparams(),
    )(act_t, dmod16, dmodc8, ada_w_sh, m, v)


def _sum_devices(gathered, name):
    _, rows, cols = gathered.shape

    def body(g_ref, o_ref):
        acc = g_ref[0]
        for d in range(1, N_DEV):
            acc = acc + g_ref[d]
        o_ref[...] = acc

    return pl.pallas_call(body, name=name, out_shape=jax.ShapeDtypeStruct((rows, cols), F32),
                          compiler_params=_cparams())(gathered)


SMALL_PARAMS = ("c_ctx", "norm_w", "conv_b", "gn_w", "final_norm_w", "ada_b", "decay_logit", "conv_w")


def _adam_small(tot, dact_ctx, conv_w_grad, params):
    n = len(SMALL_PARAMS)

    def body(tot_ref, dact_ref, cwg_ref, *refs):
        ins, outs = refs[:3 * n], refs[3 * n:]
        nh = 2 * N_HEADS
        raw = {
            "c_ctx": dact_ref[0:1, :],
            "norm_w": tot_ref[1:2, :], "conv_b": tot_ref[2:3, :], "gn_w": tot_ref[3:4, :],
            "final_norm_w": tot_ref[0:1, :],
            "ada_b": jnp.concatenate([tot_ref[7 + r:8 + r, :] + tot_ref[10 + r:11 + r, :] for r in range(3)], axis=1),
            "decay_logit": tot_ref[14:15, 0:nh],
            "conv_w": cwg_ref[...],
        }
        for k, name in enumerate(SMALL_PARAMS):
            w, m, v = ins[3 * k][...], ins[3 * k + 1][...], ins[3 * k + 2][...]
            g = raw[name]
            if name == "c_ctx":
                s = _sigmoid(w)
                g = g * (s * (1.0 + w * (1.0 - s)))
            elif name == "decay_logit":
                g = g * (1.0 - _sigmoid(w))
            delta, m2, v2 = _adam_update(w, g, m, v)
            for ref, val in zip(outs[4 * k:4 * k + 4], (g, delta, m2, v2)):
                ref[...] = val

    flat = [a for trio in params for a in trio]
    out_shape = [jax.ShapeDtypeStruct(trio[0].shape, F32) for trio in params for _ in range(4)]
    outs = pl.pallas_call(body, name="adam_small", out_shape=out_shape, compiler_params=_cparams())(
        tot, dact_ctx, conv_w_grad, *flat)
    return [outs[4 * k:4 * k + 4] for k in range(n)]


def _adam_square(params, own, recv):
    def body(own_ref, recv_ref, *refs):
        ins, outs = refs[:9], refs[9:]
        for k in range(3):
            rows = slice(k * RSH, (k + 1) * RSH)
            g = own_ref[0, rows, :]
            for j in range(3):
                g = g + recv_ref[j, rows, :].astype(F32)
            delta, m2, v2 = _adam_update(ins[3 * k][...], g, ins[3 * k + 1][...], ins[3 * k + 2][...])
            for ref, val in zip(outs[4 * k:4 * k + 4], (g, delta, m2, v2)):
                ref[...] = val

    flat = [a for trio in params for a in trio]
    outs = pl.pallas_call(body, name="adam_square", out_shape=[jax.ShapeDtypeStruct((RSH, D), F32)] * 12,
                          compiler_params=_cparams())(own, recv, *flat)
    return [outs[4 * k:4 * k + 4] for k in range(3)]


def _rope(t, cos, sin):
    lane = lax.broadcasted_iota(jnp.int32, (1, LANES), 1)
    first_half = jnp.bitwise_and(lane, DK // 2) == 0
    partner = jnp.where(first_half, pltpu.roll(t, LANES - DK // 2, 1), pltpu.roll(t, DK // 2, 1))
    return t * cos + partner * sin


class _Shards:
    def __init__(self, arrays, picks):
        self.arrays, self.picks = list(arrays), list(picks)

    def specs(self, index_args):
        out = []
        for a, arr in enumerate(self.arrays):
            steps = [p for p, (ai, _) in enumerate(self.picks) if ai == a]
            first, slot0, last = steps[0], self.picks[steps[0]][1], arr.shape[0] - 1
            out.append(pl.BlockSpec(
                (1, D, WSH), lambda *args, first=first, slot0=slot0, last=last:
                (jnp.clip(index_args(*args) - first + slot0, 0, last), 0, 0)))
        return out

    def __add__(self, other):
        shift = len(self.arrays)
        return _Shards(self.arrays + other.arrays, self.picks + [(a + shift, s) for a, s in other.picks])


def _load_w_in(step, ids_ref, shards, w_refs, w_vmem, sems):
    @pl.when(step == 0)
    def _():
        copies = []
        for p, (a, slot) in enumerate(shards.picks):
            col = pl.multiple_of(ids_ref[p] * WSH, LANES)
            copies.append(pltpu.make_async_copy(w_refs[a].at[slot], w_vmem.at[:, pl.ds(col, WSH)], sems.at[p]))
        for cp in copies:
            cp.start()
        for cp in copies:
            cp.wait()


def _project_columns(xmb, w_ref, shard, cos, sin, p_ref):
    groups = WSH // LANES
    q0, k0, v0 = CB_QK * D // LANES, (CB_QK * D + N_HEADS * DK) // LANES, CB_V * D // LANES
    mxu_cols = 2 * LANES
    for a in range(0, WSH, mxu_cols):
        b = min(a + mxu_cols, WSH)
        acc = _dot(xmb, w_ref[0, :, a:b])
        for col in range(0, b - a, LANES):
            grp = shard * groups + (a + col) // LANES
            rotary = jnp.logical_and(grp >= q0, grp < v0)
            scale = jnp.where(grp >= k0, DK ** -0.5, 1.0)
            piece = acc[:, col:col + LANES]
            turned = _rope(piece * scale, cos, sin)
            p_ref[:, a + col:a + col + LANES] = jnp.where(rotary, turned, piece).astype(BF16)


def _prenorm(x, ctx, modx, modc, norm_w):
    n_x_tiles = x.shape[0] // TM
    lext = x.shape[0] + ctx.shape[0]

    def body(x_ref, ctx_ref, mx_ref, mc_ref, nw_ref, xm_ref, xmt_ref):
        is_ctx = pl.program_id(0) >= n_x_tiles
        x = jnp.where(is_ctx, ctx_ref[...], x_ref[...])
        r = lax.rsqrt(jnp.mean(x * x, axis=-1, keepdims=True) + EPS)
        sh = jnp.where(is_ctx, mc_ref[0:1, :], mx_ref[0:1, :])
        sc = jnp.where(is_ctx, mc_ref[1:2, :], mx_ref[1:2, :])
        xm = (x * r * nw_ref[...]) * (1.0 + sc) + sh
        xm_ref[...] = xm.astype(BF16)
        xmt_ref[...] = xm.T.astype(BF16)

    full = lambda a: pl.BlockSpec(a.shape, lambda i: (0,) * a.ndim)
    return pl.pallas_call(
        body, name="prenorm", grid=(lext // TM,),
        in_specs=[pl.BlockSpec((TM, D), lambda i: (jnp.minimum(i, n_x_tiles - 1), 0)), full(ctx),
                  full(modx), full(modc), full(norm_w)],
        out_specs=[pl.BlockSpec((TM, D), lambda i: (i, 0)), pl.BlockSpec((D, TM), lambda i: (0, i))],
        out_shape=[jax.ShapeDtypeStruct((lext, D), BF16), jax.ShapeDtypeStruct((D, lext), BF16)],
        compiler_params=_cparams(dimension_semantics=("arbitrary",)),
    )(x, ctx, modx, modc, norm_w)


def _in_projection(p_ext, xm, cos_t, sin_t, shards, ids, name):
    lext = xm.shape[0]
    n_w = len(shards.arrays)
    tr = lext // 4

    assert n_w == 1

    def body(ids_ref, xm_ref, cos_ref, sin_ref, *rest):
        w_refs, p_ref = rest[:n_w], rest[-1]
        _project_columns(xm_ref[...], w_refs[0], ids_ref[pl.program_id(0)], cos_ref[...], sin_ref[...], p_ref)

    row = lambda w: pl.BlockSpec((tr, w), lambda j, i, ids: (i, 0))
    grid_spec = pltpu.PrefetchScalarGridSpec(
        num_scalar_prefetch=1, grid=(len(shards.picks), lext // tr),
        in_specs=[row(D), row(LANES), row(LANES)] + shards.specs(lambda j, i, ids: j)
        + ([] if p_ext is None else [pl.BlockSpec(memory_space=pl.ANY)]),
        out_specs=pl.BlockSpec((tr, WSH), lambda j, i, ids: (i, ids[j])))
    return pl.pallas_call(
        body, name=name, grid_spec=grid_spec,
        out_shape=jax.ShapeDtypeStruct((lext, PW), BF16),
        input_output_aliases={} if p_ext is None else {4 + n_w: 0},
        compiler_params=_cparams(dimension_semantics=("arbitrary", "arbitrary")),
    )(ids, xm, cos_t, sin_t, *shards.arrays, *([] if p_ext is None else [p_ext]))


def _decay_tables(lgf, lgb, n):
    i = lax.broadcasted_iota(jnp.int32, (n, 1), 0).astype(F32)
    return dict(i=i, k_f=jnp.exp(lgf * (n - 1.0 - i)), k_b=jnp.exp(lgb * i),
                q_f=jnp.exp(lgf * (i + 1.0)), q_b=jnp.exp(lgb * (n - i)))


def _decay_matrix(lgf, lgb, n, transposed=False):
    ii = lax.broadcasted_iota(jnp.int32, (n, n), 0)
    jj = lax.broadcasted_iota(jnp.int32, (n, n), 1)
    diff = (jj - ii if transposed else ii - jj).astype(F32)
    low = jnp.exp(lgf * jnp.maximum(diff, 0.0))
    up = jnp.exp(lgb * jnp.maximum(-diff, 0.0))
    return jnp.where(diff > 0, low, jnp.where(diff < 0, up, 2.0)), diff


def _cat_lanes(a, b):
    return jnp.concatenate([a.astype(BF16), b.astype(BF16)], axis=1)


def _retention_forward(p_ext, lg, seq, ctx_len):
    lext = seq + ctx_len
    n_chunks = seq // RET_C
    C = RET_C

    def body(lg_ref, q_ref, k_ref, v_ref, rn_ref, rstd_ref, kv_scr, sf_scr, sb_scr):
        pair = pl.program_id(0)
        lane = lax.broadcasted_iota(jnp.int32, (1, LANES), 1)
        heads = range(2)
        hmask = [(lane // DK == hh).astype(F32) for hh in heads]
        lgf = [lg_ref[0, 2 * pair + hh] for hh in heads]
        lgb = [lg_ref[1, 2 * pair + hh] for hh in heads]
        vs = [slice(hh * DV, (hh + 1) * DV) for hh in heads]
        t = [_decay_tables(lgf[hh], lgb[hh], C) for hh in heads]
        kc_all = k_ref[seq:lext, :].astype(F32)
        s0 = []
        for hh in heads:
            tc = _decay_tables(lgf[hh], lgb[hh], ctx_len)
            kc = kc_all * hmask[hh]
            s0.append(_dot_tn(_cat_lanes(kc * tc["k_f"], kc * tc["k_b"]), v_ref[seq:lext, vs[hh]]))

        def increments(c, carry):
            rows = pl.ds(pl.multiple_of(c * C, C), C)
            k_all = k_ref[rows, :].astype(F32)
            for hh in heads:
                k = k_all * hmask[hh]
                kv_scr[hh, c] = _dot_tn(_cat_lanes(k * t[hh]["k_f"], k * t[hh]["k_b"]), v_ref[rows, vs[hh]])
            return carry

        lax.fori_loop(0, n_chunks, increments, 0, unroll=2)
        gf_c = [jnp.exp(lgf[hh] * C) for hh in heads]
        gb_c = [jnp.exp(lgb[hh] * C) for hh in heads]

        def scan_f(c, s):
            for hh in heads:
                sf_scr[hh, c] = s[hh]
            return tuple(gf_c[hh] * s[hh] + kv_scr[hh, c, 0:LANES, :] for hh in heads)

        def scan_b(n, s):
            c = n_chunks - 1 - n
            for hh in heads:
                sb_scr[hh, c] = s[hh]
            return tuple(gb_c[hh] * s[hh] + kv_scr[hh, c, LANES:2 * LANES, :] for hh in heads)

        lax.fori_loop(0, n_chunks, scan_f, tuple(s0[hh][0:LANES] for hh in heads))
        lax.fori_loop(0, n_chunks, scan_b, tuple(s0[hh][LANES:2 * LANES] for hh in heads))
        dmat = [_decay_matrix(lgf[hh], lgb[hh], C)[0] for hh in heads]

        def outputs(c, carry):
            rows = pl.ds(pl.multiple_of(c * C, C), C)
            q_all = q_ref[rows, :].astype(F32)
            k = k_ref[rows, :]
            for hh in heads:
                q = q_all * hmask[hh]
                v = v_ref[rows, vs[hh]]
                s = _dot_nt(q.astype(BF16), k)
                inner = _dot((s * dmat[hh]).astype(BF16), v)
                states = jnp.concatenate([sf_scr[hh, c], sb_scr[hh, c]], axis=0).astype(BF16)
                ret = inner + _dot(_cat_lanes(q * t[hh]["q_f"], q * t[hh]["q_b"]), states)
                mu = jnp.mean(ret, axis=-1, keepdims=True)
                cen = ret - mu
                rstd = lax.rsqrt(jnp.mean(cen * cen, axis=-1, keepdims=True) + EPS)
                rn_ref[rows, vs[hh]] = cen * rstd
                rstd_ref[rows, vs[hh]] = jnp.broadcast_to(rstd, (C, DV))
            return carry

        lax.fori_loop(0, n_chunks, outputs, 0, unroll=2)

    qk0 = CB_QK * D // LANES
    return pl.pallas_call(
        body, name="retention_forward", grid=(N_HEADS // 2,),
        in_specs=[pl.BlockSpec(memory_space=pltpu.SMEM),
                  pl.BlockSpec((lext, LANES), lambda g: (0, qk0 + g)),
                  pl.BlockSpec((lext, LANES), lambda g: (0, qk0 + N_HEADS // 2 + g)),
                  pl.BlockSpec((lext, 2 * DV), lambda g: (0, CB_V * D // (2 * DV) + g))],
        out_specs=[pl.BlockSpec((seq, 2 * DV), lambda g: (0, g))] * 2,
        out_shape=[jax.ShapeDtypeStruct((seq, D), F32)] * 2,
        scratch_shapes=[pltpu.VMEM((2, n_chunks, 2 * LANES, DV), F32), pltpu.VMEM((2, n_chunks, LANES, DV), F32),
                        pltpu.VMEM((2, n_chunks, LANES, DV), F32)],
        compiler_params=_cparams(dimension_semantics=("arbitrary",)),
    )(lg, p_ext, p_ext, p_ext)


def _retention_backward(p_ext, dret, lg, cos_t, sin_t, seq, ctx_len):
    lext = seq + ctx_len
    n_chunks = seq // RET_C
    C = RET_C

    def body(lg_ref, q_ref, k_ref, v_ref, do_ref, cos_ref, sin_ref, dq_ref, dk_ref, dv_ref, dlg_ref,
             kv_scr, g_scr, sf_scr, sb_scr, gfn_scr, gbp_scr):
        pair = pl.program_id(0)
        lane = lax.broadcasted_iota(jnp.int32, (1, LANES), 1)
        heads = range(2)
        hmask = [(lane // DK == hh).astype(F32) for hh in heads]
        lgf = [lg_ref[0, 2 * pair + hh] for hh in heads]
        lgb = [lg_ref[1, 2 * pair + hh] for hh in heads]
        vs = [slice(hh * DV, (hh + 1) * DV) for hh in heads]
        t = [_decay_tables(lgf[hh], lgb[hh], C) for hh in heads]
        tc = [_decay_tables(lgf[hh], lgb[hh], ctx_len) for hh in heads]
        kc_all = k_ref[seq:lext, :].astype(F32)
        kc = [kc_all * hmask[hh] for hh in heads]
        vc = [v_ref[seq:lext, vs[hh]] for hh in heads]
        kc_cat = [_cat_lanes(kc[hh] * tc[hh]["k_f"], kc[hh] * tc[hh]["k_b"]) for hh in heads]
        s0 = [_dot_tn(kc_cat[hh], vc[hh]) for hh in heads]

        def increments(c, carry):
            rows = pl.ds(pl.multiple_of(c * C, C), C)
            k_all = k_ref[rows, :].astype(F32)
            q_all = q_ref[rows, :].astype(F32)
            for hh in heads:
                k, q = k_all * hmask[hh], q_all * hmask[hh]
                kv_scr[hh, c] = _dot_tn(_cat_lanes(k * t[hh]["k_f"], k * t[hh]["k_b"]), v_ref[rows, vs[hh]])
                g_scr[hh, c] = _dot_tn(_cat_lanes(q * t[hh]["q_f"], q * t[hh]["q_b"]), do_ref[rows, vs[hh]])
            return carry

        lax.fori_loop(0, n_chunks, increments, 0, unroll=2)
        gf_c = [jnp.exp(lgf[hh] * C) for hh in heads]
        gb_c = [jnp.exp(lgb[hh] * C) for hh in heads]

        def scan_f(c, s):
            for hh in heads:
                sf_scr[hh, c] = s[hh]
            return tuple(gf_c[hh] * s[hh] + kv_scr[hh, c, 0:LANES, :] for hh in heads)

        def scan_b(n, s):
            c = n_chunks - 1 - n
            for hh in heads:
                sb_scr[hh, c] = s[hh]
            return tuple(gb_c[hh] * s[hh] + kv_scr[hh, c, LANES:2 * LANES, :] for hh in heads)

        def scan_gf(n, carry):
            c = n_chunks - 1 - n
            for hh in heads:
                gfn_scr[hh, c] = carry[hh]
            return tuple(g_scr[hh, c, 0:LANES, :] + gf_c[hh] * carry[hh] for hh in heads)

        def scan_gb(c, carry):
            for hh in heads:
                gbp_scr[hh, c] = carry[hh]
            return tuple(g_scr[hh, c, LANES:2 * LANES, :] + gb_c[hh] * carry[hh] for hh in heads)

        lax.fori_loop(0, n_chunks, scan_f, tuple(s0[hh][0:LANES] for hh in heads))
        lax.fori_loop(0, n_chunks, scan_b, tuple(s0[hh][LANES:2 * LANES] for hh in heads))
        zero_state = jnp.zeros((LANES, DV), F32)
        gf_first = lax.fori_loop(0, n_chunks, scan_gf, (zero_state, zero_state))
        gb_last = lax.fori_loop(0, n_chunks, scan_gb, (zero_state, zero_state))

        dmat, w_f, w_b = [], [], []
        for hh in heads:
            dm, diff = _decay_matrix(lgf[hh], lgb[hh], C)
            dmat.append(dm)
            w_f.append(jnp.where(diff > 0, diff * dm, 0.0))
            w_b.append(jnp.where(diff < 0, -diff * dm, 0.0))

        def total(a):
            rows_, width = a.shape
            part = jnp.sum(a.reshape(rows_ // 8, 8, width), axis=0)
            return part[:, 0:LANES] + part[:, LANES:2 * LANES] if width == 2 * LANES else part

        def chunk_head(hh, c, rows, q_all, k_all, dlgf, dlgb):
            th = t[hh]
            qm = q_all * hmask[hh]
            km = k_all * hmask[hh]
            qb, kb = qm.astype(BF16), km.astype(BF16)
            v = v_ref[rows, vs[hh]]
            do = do_ref[rows, vs[hh]]
            s = _dot_nt(qb, kb)
            dsd = _dot_nt(do, v)
            ds = (dsd * dmat[hh]).astype(BF16)
            dq_in = _dot(ds, kb)
            dk_in = _dot_tn(ds, qb)
            dv_in = _dot_tn((s * dmat[hh]).astype(BF16), do)
            prod = s * dsd
            dlgf = dlgf + total(prod * w_f[hh])
            dlgb = dlgb + total(prod * w_b[hh])
            sf, sb = sf_scr[hh, c], sb_scr[hh, c]
            states = jnp.concatenate([sf, sb], axis=0).astype(BF16)
            dqc = _dot_nt(do, states)
            dqf = dqc[:, 0:LANES] * th["q_f"]
            dqb = dqc[:, LANES:2 * LANES] * th["q_b"]
            dq = (dq_in + dqf + dqb) * hmask[hh]
            dlgf = dlgf + total((th["i"] + 1.0) * (qm * dqf))
            dlgb = dlgb + total((C - th["i"]) * (qm * dqb))
            gfn, gbp = gfn_scr[hh, c], gbp_scr[hh, c]
            gstates = jnp.concatenate([gfn, gbp], axis=0).astype(BF16)
            dkc = _dot_nt(v, gstates)
            dkf = dkc[:, 0:LANES] * th["k_f"]
            dkb = dkc[:, LANES:2 * LANES] * th["k_b"]
            dk = dk_in + dkf + dkb
            dlgf = dlgf + total((C - 1.0 - th["i"]) * (km * dkf)) + C * gf_c[hh] * total(gfn * sf)
            dlgb = dlgb + total(th["i"] * (km * dkb)) + C * gb_c[hh] * total(gbp * sb)
            dv = dv_in + _dot(_cat_lanes(km * th["k_f"], km * th["k_b"]), gstates)
            dv_ref[rows, vs[hh]] = dv.astype(BF16)
            return dq, dk, dlgf, dlgb

        def chunk(c, carry):
            rows = pl.ds(pl.multiple_of(c * C, C), C)
            q_all = q_ref[rows, :].astype(F32)
            k_all = k_ref[rows, :].astype(F32)
            dq0, dk0, f0, b0 = chunk_head(0, c, rows, q_all, k_all, carry[0], carry[1])
            dq1, dk1, f1, b1 = chunk_head(1, c, rows, q_all, k_all, carry[2], carry[3])
            cos, sin = cos_ref[rows, :], sin_ref[rows, :]
            dq_ref[rows, :] = _rope(dq0 + dq1, cos, -sin).astype(BF16)
            dk_ref[rows, :] = (_rope(dk0 + dk1, cos, -sin) * (DK ** -0.5)).astype(BF16)
            return f0, b0, f1, b1

        zero = jnp.zeros((8, LANES), F32)
        sums = lax.fori_loop(0, n_chunks, chunk, (zero, zero, zero, zero), unroll=2)

        dlg = []
        dk_ctx = jnp.zeros((ctx_len, LANES), F32)
        for hh in heads:
            g0 = jnp.concatenate([gf_first[hh], gb_last[hh]], axis=0).astype(BF16)
            dkcc = _dot_nt(vc[hh], g0)
            dkcf = dkcc[:, 0:LANES] * tc[hh]["k_f"]
            dkcb = dkcc[:, LANES:2 * LANES] * tc[hh]["k_b"]
            dlgf = sums[2 * hh] + total((ctx_len - 1.0 - tc[hh]["i"]) * (kc[hh] * dkcf))
            dlgb = sums[2 * hh + 1] + total(tc[hh]["i"] * (kc[hh] * dkcb))
            dk_ctx = dk_ctx + (dkcf + dkcb) * (DK ** -0.5)
            dv_ref[seq:lext, vs[hh]] = _dot(kc_cat[hh], g0).astype(BF16)
            dlg += [jnp.sum(jnp.sum(a, axis=1, keepdims=True), axis=0, keepdims=True) for a in (dlgf, dlgb)]
        dk_ref[seq:lext, :] = dk_ctx.astype(BF16)
        dq_ref[seq:lext, :] = jnp.zeros((ctx_len, LANES), BF16)

        lane8 = lax.broadcasted_iota(jnp.int32, (8, LANES), 1)
        out = jnp.zeros((8, LANES), F32)
        for n, val in enumerate(dlg):
            out = jnp.where(lane8 == n, val, out)
        dlg_ref[0] = out

    qk0 = CB_QK * D // LANES
    q_spec = pl.BlockSpec((lext, LANES), lambda g: (0, qk0 + g))
    k_spec = pl.BlockSpec((lext, LANES), lambda g: (0, qk0 + N_HEADS // 2 + g))
    v_spec = pl.BlockSpec((lext, 2 * DV), lambda g: (0, CB_V * D // (2 * DV) + g))
    table = pl.BlockSpec((lext, LANES), lambda g: (0, 0))
    state = pltpu.VMEM((2, n_chunks, LANES, DV), F32)
    return pl.pallas_call(
        body, name="retention_backward", grid=(N_HEADS // 2,),
        in_specs=[pl.BlockSpec(memory_space=pltpu.SMEM), q_spec, k_spec, v_spec,
                  pl.BlockSpec((seq, 2 * DV), lambda g: (0, g)), table, table],
        out_specs=[pl.BlockSpec((lext, LANES), lambda g: (0, g)), pl.BlockSpec((lext, LANES), lambda g: (0, g)),
                   pl.BlockSpec((lext, 2 * DV), lambda g: (0, g)), pl.BlockSpec((1, 8, LANES), lambda g: (g, 0, 0))],
        out_shape=[jax.ShapeDtypeStruct((lext, N_HEADS * DK), BF16), jax.ShapeDtypeStruct((lext, N_HEADS * DK), BF16),
                   jax.ShapeDtypeStruct((lext, D), BF16), jax.ShapeDtypeStruct((N_HEADS // 2, 8, LANES), F32)],
        scratch_shapes=[pltpu.VMEM((2, n_chunks, 2 * LANES, DV), F32), pltpu.VMEM((2, n_chunks, 2 * LANES, DV), F32),
                        state, state, state, state],
        compiler_params=_cparams(dimension_semantics=("arbitrary",)),
    )(lg, p_ext, p_ext, p_ext, dret, cos_t, sin_t)


def _merge(p_ext, rn, rstd, x, target, w_a, w_b, w_out, vecs, seq):
    n_tiles = seq // TM
    hb = TM // HALO

    def body(h_ref, bg_ref, cg_ref, za_ref, zb_ref, ga_ref, gb_ref, hp_ref, hn_ref, cp_ref, cn_ref,
             rn_ref, rstd_ref, x_ref, t_ref, wa_ref, wb_ref, wo_ref, vec_ref,
             dx1_ref, dpa_ref, dconv_ref, dret_ref, at_ref, b_ref, part_ref):
        i = pl.program_id(0)
        f = lambda ref: ref[...].astype(F32)
        h, bg, cg, za, zb, ga, gb = f(h_ref), f(bg_ref), f(cg_ref), f(za_ref), f(zb_ref), f(ga_ref), f(gb_ref)
        gx, w0, w1, w2 = vec_ref[0:1, :], vec_ref[1:2, :], vec_ref[2:3, :], vec_ref[3:4, :]
        cb, gnw, fw = vec_ref[4:5, :], vec_ref[5:6, :], vec_ref[6:7, :]
        u = cg * h
        row = lax.broadcasted_iota(jnp.int32, (TM, 1), 0)
        u_prev = (f(cp_ref) * f(hp_ref))[HALO - 1:HALO, :]
        u_next = (f(cn_ref) * f(hn_ref))[0:1, :]
        u_prev = jnp.where(i == 0, 0.0, u_prev)
        u_next = jnp.where(i == n_tiles - 1, 0.0, u_next)
        u_up = jnp.where(row == 0, u_prev, pltpu.roll(u, 1, 0))
        u_dn = jnp.where(row == TM - 1, u_next, pltpu.roll(u, TM - 1, 0))
        conv = w0 * u_up + w1 * u + w2 * u_dn + cb
        sza = _sigmoid(za)
        silu_za = za * sza
        a_act = silu_za * bg * conv
        rn = rn_ref[...]
        szb = _sigmoid(zb)
        silu_zb = zb * szb
        rg = rn * gnw
        b_act = silu_zb * rg
        y_a = _dot(a_act.astype(BF16), wa_ref[...])
        y_b = _dot(b_act.astype(BF16), wb_ref[...])
        sga, sgb = _sigmoid(ga), _sigmoid(gb)
        mix = sga * y_a + sgb * y_b
        y = _dot(mix.astype(BF16), wo_ref[...])
        x1 = x_ref[...] + gx * y
        r1 = lax.rsqrt(jnp.mean(x1 * x1, axis=-1, keepdims=True) + EPS)
        xh1 = x1 * r1
        err = xh1 * fw - t_ref[...]
        loss = jnp.sum(jnp.sum(err * err, axis=1, keepdims=True), axis=0, keepdims=True) * (0.5 / D)
        dout = err * (1.0 / D)
        dxh = dout * fw
        dx1 = r1 * (dxh - xh1 * jnp.mean(dxh * xh1, axis=-1, keepdims=True))
        dx1_ref[...] = dx1
        dy = (dx1 * gx).astype(BF16)
        dmix = _dot_nt(dy, wo_ref[...])
        dya = (dmix * sga).astype(BF16)
        dyb = (dmix * sgb).astype(BF16)
        da = _dot_nt(dya, wa_ref[...])
        db = _dot_nt(dyb, wb_ref[...])
        dpa_ref[:, 0:D] = (da * silu_za * conv).astype(BF16)
        dpa_ref[:, D:2 * D] = (da * bg * conv * (sza * (1.0 + za * (1.0 - sza)))).astype(BF16)
        dpa_ref[:, 2 * D:3 * D] = (db * rg * (szb * (1.0 + zb * (1.0 - szb)))).astype(BF16)
        dpa_ref[:, 3 * D:4 * D] = (dmix * y_a * sga * (1.0 - sga)).astype(BF16)
        dpa_ref[:, 4 * D:5 * D] = (dmix * y_b * sgb * (1.0 - sgb)).astype(BF16)
        dconv_ref[...] = (da * silu_za * bg).astype(BF16)
        drn_n = db * silu_zb
        drn = drn_n * gnw
        rstd = rstd_ref[...]
        for hd in range(N_HEADS):
            sl = slice(hd * DV, (hd + 1) * DV)
            dh_, rh = drn[:, sl], rn[:, sl]
            m1 = jnp.mean(dh_, axis=-1, keepdims=True)
            m2 = jnp.mean(dh_ * rh, axis=-1, keepdims=True)
            dret_ref[:, sl] = (rstd[:, sl] * (dh_ - m1 - rh * m2)).astype(BF16)
        at_ref[0] = a_act.T.astype(BF16)
        at_ref[1] = b_act.T.astype(BF16)
        at_ref[2] = mix.T.astype(BF16)
        b_ref[0] = dya
        b_ref[1] = dyb
        b_ref[2] = dy

        @pl.when(i == 0)
        def _():
            part_ref[...] = jnp.zeros((8, D), F32)

        part_ref[0:1, :] += jnp.sum(dout * xh1, axis=0, keepdims=True)
        part_ref[1:2, :] += jnp.sum(dx1 * y, axis=0, keepdims=True)
        part_ref[2:3, :] += jnp.sum(drn_n * rn, axis=0, keepdims=True)
        part_ref[3:4, :] += jnp.broadcast_to(loss, (1, D))

    col = lambda cb_: pl.BlockSpec((TM, D), lambda i, cb_=cb_: (i, cb_))
    prev = lambda cb_: pl.BlockSpec((HALO, D), lambda i, cb_=cb_: (jnp.maximum(i * hb - 1, 0), cb_))
    nxt = lambda cb_: pl.BlockSpec((HALO, D), lambda i, cb_=cb_: (jnp.minimum((i + 1) * hb, n_tiles * hb - 1), cb_))
    tile = pl.BlockSpec((TM, D), lambda i: (i, 0))
    full = lambda a: pl.BlockSpec(a.shape, lambda i: (0,) * a.ndim)
    return pl.pallas_call(
        body, name="merge", grid=(n_tiles,),
        in_specs=[col(CB_H), col(CB_BG), col(CB_CG), col(CB_ZA), col(CB_ZB), col(CB_GA), col(CB_GB),
                  prev(CB_H), nxt(CB_H), prev(CB_CG), nxt(CB_CG),
                  tile, tile, tile, tile, full(w_a), full(w_b), full(w_out), full(vecs)],
        out_specs=[tile, pl.BlockSpec((TM, 5 * D), lambda i: (i, 0)), tile, tile,
                   pl.BlockSpec((3, D, TM), lambda i: (0, 0, i)), pl.BlockSpec((3, TM, D), lambda i: (0, i, 0)),
                   pl.BlockSpec((8, D), lambda i: (0, 0))],
        out_shape=[jax.ShapeDtypeStruct((seq, D), F32), jax.ShapeDtypeStruct((seq, 5 * D), BF16),
                   jax.ShapeDtypeStruct((seq, D), BF16), jax.ShapeDtypeStruct((seq, D), BF16),
                   jax.ShapeDtypeStruct((3, D, seq), BF16), jax.ShapeDtypeStruct((3, seq, D), BF16),
                   jax.ShapeDtypeStruct((8, D), F32)],
        compiler_params=_cparams(dimension_semantics=("arbitrary",)),
    )(p_ext, p_ext, p_ext, p_ext, p_ext, p_ext, p_ext, p_ext, p_ext, p_ext, p_ext,
      rn, rstd, x, target, w_a, w_b, w_out, vecs)


def _conv_backward(p_ext, dconv, dpa, dq, dk, dv, vecs, seq):
    n_tiles = seq // TM
    hb = TM // HALO
    qk = N_HEADS * DK

    def body(h_ref, cg_ref, dc_ref, dcp_ref, dcn_ref, dpa_ref, dq_ref, dk_ref, dv_ref, vec_ref, dp_ref, part_ref):
        i = pl.program_id(0)
        dp_ref[:, CB_QK * D:CB_QK * D + qk] = dq_ref[...]
        dp_ref[:, CB_QK * D + qk:CB_V * D] = dk_ref[...]
        dp_ref[:, CB_V * D:CB_ZB * D] = dv_ref[...]

        @pl.when(i == 0)
        def _():
            part_ref[...] = jnp.zeros((8, D), F32)

        @pl.when(i == n_tiles)
        def _():
            dp_ref[:, 0:CB_QK * D] = jnp.zeros((TM, CB_QK * D), BF16)
            dp_ref[:, CB_ZB * D:PW] = jnp.zeros((TM, PW - CB_ZB * D), BF16)

        @pl.when(i < n_tiles)
        def _():
            f = lambda ref: ref[...].astype(F32)
            h, cg, dc = f(h_ref), f(cg_ref), f(dc_ref)
            w0, w1, w2 = vec_ref[1:2, :], vec_ref[2:3, :], vec_ref[3:4, :]
            row = lax.broadcasted_iota(jnp.int32, (TM, 1), 0)
            dc_prev = jnp.where(i == 0, 0.0, f(dcp_ref)[HALO - 1:HALO, :])
            dc_next = jnp.where(i == n_tiles - 1, 0.0, f(dcn_ref)[0:1, :])
            dc_up = jnp.where(row == 0, dc_prev, pltpu.roll(dc, 1, 0))
            dc_dn = jnp.where(row == TM - 1, dc_next, pltpu.roll(dc, TM - 1, 0))
            du = w0 * dc_dn + w1 * dc + w2 * dc_up
            u = cg * h
            dp_ref[:, CB_H * D:(CB_H + 1) * D] = (du * cg).astype(BF16)
            dp_ref[:, CB_CG * D:(CB_CG + 1) * D] = (du * h).astype(BF16)
            dp_ref[:, CB_BG * D:(CB_BG + 1) * D] = dpa_ref[:, 0:D]
            dp_ref[:, CB_ZA * D:(CB_ZA + 1) * D] = dpa_ref[:, D:2 * D]
            dp_ref[:, CB_ZB * D:(CB_GB + 1) * D] = dpa_ref[:, 2 * D:5 * D]
            part_ref[0:1, :] += jnp.sum(u * dc_dn, axis=0, keepdims=True)
            part_ref[1:2, :] += jnp.sum(u * dc, axis=0, keepdims=True)
            part_ref[2:3, :] += jnp.sum(u * dc_up, axis=0, keepdims=True)
            part_ref[3:4, :] += jnp.sum(dc, axis=0, keepdims=True)

    last = n_tiles - 1
    col = lambda cb_: pl.BlockSpec((TM, D), lambda i, cb_=cb_: (jnp.minimum(i, last), cb_))
    lat = lambda w: pl.BlockSpec((TM, w), lambda i: (jnp.minimum(i, last), 0))
    ext = lambda w: pl.BlockSpec((TM, w), lambda i: (i, 0))
    return pl.pallas_call(
        body, name="conv_backward", grid=(n_tiles + 1,),
        in_specs=[col(CB_H), col(CB_CG), lat(D),
                  pl.BlockSpec((HALO, D), lambda i: (jnp.clip(i * hb - 1, 0, n_tiles * hb - 1), 0)),
                  pl.BlockSpec((HALO, D), lambda i: (jnp.minimum((i + 1) * hb, n_tiles * hb - 1), 0)),
                  lat(5 * D), ext(qk), ext(qk), ext(D),
                  pl.BlockSpec(vecs.shape, lambda i: (0, 0))],
        out_specs=[ext(PW), pl.BlockSpec((8, D), lambda i: (0, 0))],
        out_shape=[jax.ShapeDtypeStruct((seq + TM, PW), BF16), jax.ShapeDtypeStruct((8, D), F32)],
        compiler_params=_cparams(dimension_semantics=("arbitrary",)),
    )(p_ext, p_ext, dconv, dconv, dconv, dpa, dq, dk, dv, vecs)


def _input_backward(dp, shards, ids, x, ctx, dx1, modx, modc, norm_w):
    seq = x.shape[0]
    lext = seq + ctx.shape[0]
    n_x = seq // TM
    n_w = len(shards.arrays)

    def body(ids_ref, dp_ref, x_ref, ctx_ref, dx1_ref, mx_ref, mc_ref, nw_ref, *rest):
        w_hbm, (gx_ref, part_ref, w_ref, w_sems) = rest[:n_w], rest[n_w:]
        i = pl.program_id(0)
        _load_w_in(i, ids_ref, shards, w_hbm, w_ref, w_sems)
        is_ctx = i >= n_x
        dxm = _dot_nt(dp_ref[...], w_ref[...])
        x = jnp.where(is_ctx, ctx_ref[...], x_ref[...])
        r = lax.rsqrt(jnp.mean(x * x, axis=-1, keepdims=True) + EPS)
        xh = x * r
        nw = nw_ref[...]
        sc = jnp.where(is_ctx, mc_ref[1:2, :], mx_ref[1:2, :])
        dxn = dxm * (1.0 + sc)
        dxh = dxn * nw
        dx = r * (dxh - xh * jnp.mean(dxh * xh, axis=-1, keepdims=True))

        @pl.when(jnp.logical_not(is_ctx))
        def _():
            gx_ref[...] = dx1_ref[...] + dx

        @pl.when(i == 0)
        def _():
            part_ref[...] = jnp.zeros((8, D), F32)

        fx = jnp.where(is_ctx, 0.0, 1.0)
        d_shift = jnp.sum(dxm, axis=0, keepdims=True)
        d_scale = jnp.sum(dxm * (xh * nw), axis=0, keepdims=True)
        part_ref[0:1, :] += fx * d_shift
        part_ref[1:2, :] += fx * d_scale
        part_ref[2:3, :] += jnp.sum(dxn * xh, axis=0, keepdims=True)
        part_ref[3:4, :] += (1.0 - fx) * d_shift
        part_ref[4:5, :] += (1.0 - fx) * d_scale

    lat = lambda w: pl.BlockSpec((TM, w), lambda i, ids_: (jnp.minimum(i, n_x - 1), 0))
    ext = lambda w: pl.BlockSpec((TM, w), lambda i, ids_: (i, 0))
    full = lambda a: pl.BlockSpec(a.shape, lambda i, ids_: (0,) * a.ndim)
    grid_spec = pltpu.PrefetchScalarGridSpec(
        num_scalar_prefetch=1, grid=(lext // TM,),
        in_specs=[ext(PW), lat(D), full(ctx), lat(D), full(modx), full(modc), full(norm_w)]
        + [pl.BlockSpec(memory_space=pl.ANY)] * n_w,
        out_specs=[lat(D), pl.BlockSpec((8, D), lambda i, ids_: (0, 0))],
        scratch_shapes=[pltpu.VMEM((D, PW), BF16), pltpu.SemaphoreType.DMA((N_DEV,))])
    return pl.pallas_call(
        body, name="input_backward", grid_spec=grid_spec,
        out_shape=[jax.ShapeDtypeStruct((seq, D), F32), jax.ShapeDtypeStruct((8, D), F32)],
        compiler_params=_cparams(dimension_semantics=("arbitrary",)),
    )(ids, dp, x, ctx, dx1, modx, modc, norm_w, *shards.arrays)


def _weight_grad_in(xm_t, dp, owners, name):
    lext = xm_t.shape[1]

    def body(own_ref, a_ref, b_ref, o_ref):
        o_ref[0] = _dot(a_ref[...], b_ref[...])

    grid_spec = pltpu.PrefetchScalarGridSpec(
        num_scalar_prefetch=1, grid=(4,),
        in_specs=[pl.BlockSpec((D, lext), lambda j, own: (0, 0)),
                  pl.BlockSpec((lext, WSH), lambda j, own: (0, own[j]))],
        out_specs=pl.BlockSpec((1, D, WSH), lambda j, own: (j, 0, 0)))
    return pl.pallas_call(
        body, name=name, grid_spec=grid_spec,
        out_shape=jax.ShapeDtypeStruct((4, D, WSH), F32),
        compiler_params=_cparams(dimension_semantics=("arbitrary",)),
    )(owners, xm_t, dp)


def _weight_grad_square(a_t, b):
    seq = a_t.shape[2]

    def body(a_ref, b_ref, o_ref):
        o_ref[:, 0] = _dot(a_ref[0], b_ref[0]).reshape(N_DEV, RSH, D)

    return pl.pallas_call(
        body, name="weight_grad_square", grid=(3,),
        in_specs=[pl.BlockSpec((1, D, seq), lambda t: (t, 0, 0)), pl.BlockSpec((1, seq, D), lambda t: (t, 0, 0))],
        out_specs=pl.BlockSpec((N_DEV, 1, RSH, D), lambda t: (0, t, 0, 0)),
        out_shape=jax.ShapeDtypeStruct((N_DEV, 3, RSH, D), F32),
        compiler_params=_cparams(dimension_semantics=("arbitrary",)),
    )(a_t, b)


def _rope_tables(seq, ctx_len, token):
    rows = seq // GRID_W
    row = jnp.repeat(jnp.arange(rows), GRID_W).astype(F32)
    col = jnp.tile(jnp.arange(GRID_W), rows).astype(F32)
    nf = DK // 4
    inv = ROPE_BASE ** (-jnp.arange(nf, dtype=F32) / nf)
    ang = _tie(jnp.concatenate([row[:, None] * inv, col[:, None] * inv], axis=-1), token)
    cos, sin = jnp.cos(ang), jnp.sin(ang)
    cos_t = jnp.tile(cos, (1, 4))
    sin_t = jnp.tile(jnp.concatenate([-sin, sin], axis=-1), (1, 2))
    cos_t = jnp.concatenate([cos_t, jnp.ones((ctx_len, LANES), F32)], axis=0)
    sin_t = jnp.concatenate([sin_t, jnp.zeros((ctx_len, LANES), F32)], axis=0)
    return cos_t, sin_t


def _tie(value, token):
    return value + token[0, 0].astype(value.dtype)


def _pad_lanes(a):
    return jnp.pad(a, ((0, 0), (0, D - a.shape[1])))


def _rows(total, *parts):
    width = max(a.shape[1] for _, a in parts)
    out = None
    for row, a in parts:
        padded = jnp.pad(a, ((row, total - row - a.shape[0]), (0, width - a.shape[1])))
        out = padded if out is None else out + padded
    return out


def kernel(x, c, ctx, c_ctx, norm_w, ada_w, ada_b, w_in, conv_w, conv_b, decay_logit, gn_w, w_a, w_b, w_out, final_norm_w, loss_target, m_c_ctx, m_norm_w, m_ada_w, m_ada_b, m_w_in, m_conv_w, m_conv_b, m_decay_logit, m_gn_w, m_w_a, m_w_b, m_w_out, m_final_norm_w, v_c_ctx, v_norm_w, v_ada_w, v_ada_b, v_w_in, v_conv_w, v_conv_b, v_decay_logit, v_gn_w, v_w_a, v_w_b, v_w_out, v_final_norm_w):
    xi, yi, ci = _coords()
    me = 4 * xi + 2 * yi + ci
    chip = 2 * xi + yi
    seq, ctx_len = x.shape[1], ctx.shape[1]
    assert seq % TM == 0 and seq % RET_C == 0 and ctx_len == TM and seq % GRID_W == 0
    csh = D // N_DEV

    blk = jnp.pad(c, ((0, 7), (0, 0))) + jnp.pad(conv_w[0], ((1, 4), (0, D - csh)))
    got, _ = _all_gather_small(blk, "gather_cond")
    conv_w_all = got[:, 1:4, 0:csh].transpose(1, 0, 2).reshape(3, D)
    c16 = _rows(16, (0, got[:, 0, :]), (N_DEV, c_ctx[None]))
    ada_b_sh = lax.dynamic_slice(ada_b, (0, me * ADA_SH), (1, ADA_SH))
    mod_sh, act16, lg = _modulation(c16, ada_w[0], ada_b_sh, decay_logit[0])
    mod_all, small_done = _all_gather_small(mod_sh, "gather_mod")
    mod_all = mod_all.transpose(1, 0, 2).reshape(16, 3 * D)
    modx = lax.dynamic_slice(mod_all, (me, 0), (1, 3 * D)).reshape(3, D)
    modc = mod_all[8].reshape(3, D)

    x2, tgt = x[0], loss_target[0]
    wblock = lambda n: jax.ShapeDtypeStruct((n, D, WSH), BF16)
    ids_of = lambda ks: jnp.stack([jnp.bitwise_xor(me, k) for k in ks]).astype(jnp.int32)
    own_shard = _tie(w_in[0].astype(BF16)[None], small_done)
    to_sib = _Exchange("gather_in_sibling", [jnp.concatenate([own_shard, own_shard], axis=0)], [], 1,
                       _forward_plan([(0, 1)]))
    to_nbr = _Exchange("gather_in_neighbours", [_tie(own_shard, to_sib.token)], [wblock(4)], 2, _send_plan((4, 2)))
    cos_t, sin_t = _rope_tables(seq, ctx_len, to_nbr.token)
    modx = _tie(modx, to_nbr.token)
    xm, xm_t = _prenorm(x2, ctx[0], modx, modc, norm_w)
    (w_pair,), _ = to_sib.wait(xm_t)
    first = _Shards([w_pair], [(0, 0), (0, 1)])
    p_ext = _in_projection(None, xm, cos_t, sin_t, first, ids_of((0, 1)), "in_projection_pair")
    _, (w_nbr,) = to_nbr.wait(p_ext)
    fwd_nbr = _Exchange("gather_in_forward_neighbours", [w_nbr], [], 2, _forward_plan([(0, 2), (1, 3)]))
    to_diag = _Exchange("gather_in_diagonal", [_tie(own_shard, fwd_nbr.token)], [wblock(2)], 1, _send_plan((6,)))
    (w_nbr,), _ = fwd_nbr.wait(to_diag.token)
    second = _Shards([w_nbr], [(0, 0), (0, 1), (0, 2), (0, 3)])
    p_ext = _in_projection(p_ext, xm, cos_t, sin_t, second, ids_of((4, 2, 5, 3)), "in_projection_neighbours")
    _, (w_diag,) = to_diag.wait(p_ext)
    fwd_diag = _Exchange("gather_in_forward_diagonal", [w_diag], [], 1, _forward_plan([(0, 1)]))
    sq_shards = [_tie(w[0].astype(BF16), fwd_diag.token) for w in (w_a, w_b, w_out)]
    ag_sq = _Exchange("gather_square", sq_shards, [jax.ShapeDtypeStruct((N_DEV, RSH, D), BF16)] * 3,
                      3 * (N_DEV - 1), _gather_plan)
    (w_diag,), _ = fwd_diag.wait(ag_sq.token)
    third = _Shards([w_diag], [(0, 0), (0, 1)])
    p_ext = _in_projection(p_ext, xm, cos_t, sin_t, third, ids_of((6, 7)), "in_projection_diagonal")
    w_shards, w_ids = first + second + third, ids_of((0, 1, 4, 2, 5, 3, 6, 7))

    rn, rstd = _retention_forward(p_ext, lg, seq, ctx_len)
    _, sq_lands = ag_sq.wait(rstd)
    w_a_all, w_b_all, w_out_all = (
        lax.dynamic_update_slice(land, shard[None], (me, 0, 0)).reshape(D, D) for land, shard in zip(sq_lands, sq_shards))
    vecs = _rows(8, (0, modx[2:3]), (1, conv_w_all), (4, conv_b), (5, gn_w), (6, final_norm_w[None]))
    dx1, dpa, dconv, dret, op_at, op_b, part_m = _merge(p_ext, rn, rstd, x2, tgt, w_a_all, w_b_all, w_out_all, vecs, seq)

    j4 = jnp.arange(4, dtype=jnp.int32)
    owners = (2 * jnp.bitwise_xor(chip, j4) + ci).astype(jnp.int32)
    owners_sib = (2 * jnp.bitwise_xor(chip, j4) + (1 - ci)).astype(jnp.int32)
    gw_sq = _weight_grad_square(op_at, op_b).reshape(N_DEV, 3 * RSH, D)
    rs_sq_pair = _Exchange("rs_square_pair", [gw_sq], [jax.ShapeDtypeStruct((4, 3 * RSH, D), F32)], 4,
                           _pair_plan(lambda j, chip_, c_: 2 * jnp.bitwise_xor(chip_, j) + (1 - c_)))
    dq, dk, dv, dlg = _retention_backward(p_ext, dret, _tie(lg, rs_sq_pair.token), cos_t, sin_t, seq, ctx_len)
    (gw_sq,), (r1_sq,) = rs_sq_pair.wait(dlg)
    own_sq, send_sq = _pair_sum(gw_sq, r1_sq, owners, "pair_sum_square")
    rs_sq_chips = _Exchange("rs_square_chips", [send_sq], [jax.ShapeDtypeStruct((3, 3 * RSH, D), BF16)], 3, _chips_plan)
    dp, part_c = _conv_backward(p_ext, dconv, dpa, dq, dk, dv, _tie(vecs, rs_sq_chips.token), seq)
    gw_sib = _weight_grad_in(xm_t, dp, owners_sib, "weight_grad_in_sibling")
    rs_in_pair = _Exchange("rs_in_pair", [gw_sib], [jax.ShapeDtypeStruct((4, D, WSH), F32)], 4,
                           _pair_plan(lambda j, chip_, c_: j))
    gw_own = _weight_grad_in(xm_t, dp, _tie(owners, rs_in_pair.token), "weight_grad_in_own")
    _, (r1_in,) = rs_in_pair.wait(gw_own)
    own_in, send_in = _pair_sum(gw_own, r1_in, j4, "pair_sum_in")
    rs_in_chips = _Exchange("rs_in_chips", [send_in], [jax.ShapeDtypeStruct((3, D, WSH), BF16)], 3, _chips_plan)
    grad_x, part_i = _input_backward(dp, w_shards, w_ids, x2, ctx[0], dx1, _tie(modx, rs_in_chips.token), modc, norm_w)

    dl = dlg[:, 0, 0:4]
    dlg_row = jnp.pad(dl[:, 0::2].reshape(1, N_HEADS), ((0, 0), (0, D - N_HEADS))) + jnp.pad(
        dl[:, 1::2].reshape(1, N_HEADS), ((0, 0), (N_HEADS, D - 2 * N_HEADS)))
    partials = _rows(16, (0, part_m[0:1]), (1, part_i[2:3]), (2, part_c[3:4]), (3, part_m[2:3]), (4, part_c[0:3]),
                     (7, part_i[0:2]), (9, part_m[1:2]), (10, part_i[3:5]), (13, part_m[3:4]), (14, dlg_row))
    got, _ = _all_gather_small(partials, "gather_partials")
    tot = _sum_devices(got, "sum_partials")
    dmodc = tot[10:13].reshape(1, 3 * D)
    dmod16 = _rows(16, (0, got[:, 7:10, :].reshape(N_DEV, 3 * D)), (N_DEV, dmodc))
    dmod16 = lax.dynamic_slice(dmod16, (0, me * ADA_SH), (16, ADA_SH))
    dmodc8 = _rows(8, (0, dmod16[8:9]))
    g_ada_w, d_ada_w, nm_ada_w, nv_ada_w, cctx_part = _ada_backward(
        act16.T, dmod16, dmodc8, ada_w[0], m_ada_w[0], v_ada_w[0])
    dact_ctx = _sum_devices(_all_gather_small(cctx_part, "gather_cctx")[0], "sum_cctx")

    view = {"c_ctx": (1, D), "final_norm_w": (1, D), "decay_logit": (1, 2 * N_HEADS), "conv_w": (3, csh)}
    given = {"c_ctx": (c_ctx, m_c_ctx, v_c_ctx), "norm_w": (norm_w, m_norm_w, v_norm_w),
             "conv_b": (conv_b, m_conv_b, v_conv_b), "gn_w": (gn_w, m_gn_w, v_gn_w),
             "final_norm_w": (final_norm_w, m_final_norm_w, v_final_norm_w), "ada_b": (ada_b, m_ada_b, v_ada_b),
             "decay_logit": (decay_logit, m_decay_logit, v_decay_logit), "conv_w": (conv_w, m_conv_w, v_conv_w)}
    small_in = [tuple(a.reshape(view.get(name, a.shape)) for a in given[name]) for name in SMALL_PARAMS]
    conv_w_grad = lax.dynamic_slice(tot, (4, me * csh), (3, csh))
    small_out = _adam_small(tot, dact_ctx, conv_w_grad, small_in)
    results = {name: [o.reshape(given[name][0].shape) for o in outs_]
               for name, outs_ in zip(SMALL_PARAMS, small_out)}

    _, (r2_sq,) = rs_sq_chips.wait(small_out[0][0])
    _, (r2_in,) = rs_in_chips.wait(small_out[0][0])
    results["w_in"] = [o[None] for o in _adam_sharded(w_in[0], m_w_in[0], v_w_in[0], own_in, r2_in, "adam_w_in")]
    square = _adam_square([(w_a[0], m_w_a[0], v_w_a[0]), (w_b[0], m_w_b[0], v_w_b[0]),
                           (w_out[0], m_w_out[0], v_w_out[0])], own_sq, r2_sq)
    for name, outs_ in zip(("w_a", "w_b", "w_out"), square):
        results[name] = [o[None] for o in outs_]
    results["ada_w"] = [o[None] for o in (g_ada_w, d_ada_w, nm_ada_w, nv_ada_w)]

    order = ("c_ctx", "norm_w", "ada_w", "ada_b", "w_in", "conv_w", "conv_b", "decay_logit", "gn_w",
             "w_a", "w_b", "w_out", "final_norm_w")
    outs = [results[name][kind] for kind in range(4) for name in order]
    return (tot[13, 0], grad_x[None], *outs)
```

```python
import functools
import math

import jax
import jax.numpy as jnp
from jax import lax
from jax.experimental import pallas as pl
from jax.experimental.pallas import tpu as pltpu

F32 = jnp.float32
BF16 = jnp.bfloat16
MESH = pl.DeviceIdType.MESH
_HBM_SPEC = pl.BlockSpec(memory_space=pltpu.HBM)
_SEM_SPEC = pl.BlockSpec(memory_space=pltpu.SEMAPHORE)
_EFFECT = pltpu.SideEffectType.DATAFLOW_SIDE_EFFECTING

N_DEV = 8
D = 1024
N_HEADS = 8
DK = 64
DV = 128
GRID_W = 64
ROPE_BASE = 10000.0
EPS = 1e-6
PW = 9 * D
WSH = PW // N_DEV
RSH = D // N_DEV
ADA_SH = 3 * D // N_DEV
TM = 256
RET_C = 256
HALO = 16
LANES = 128
VMEM_LIMIT = 60 * 1024 * 1024

ADAM_LR = 0.001
ADAM_B1 = 0.9
ADAM_B2 = 0.999
ADAM_EPS = 1e-08
ADAM_WD = 0.01
ADAM_STEP = 10

CB_H, CB_BG, CB_CG, CB_ZA, CB_QK, CB_V, CB_ZB, CB_GA, CB_GB = range(9)


def _cparams(**kw):
    return pltpu.CompilerParams(vmem_limit_bytes=VMEM_LIMIT, **kw)


def _dot(a, b):
    return jnp.dot(a, b, preferred_element_type=F32)


def _dot_nt(a, b):
    return lax.dot_general(a, b, (((1,), (1,)), ((), ())), preferred_element_type=F32)


def _dot_tn(a, b):
    return lax.dot_general(a, b, (((0,), (0,)), ((), ())), preferred_element_type=F32)


def _sigmoid(z):
    return 0.5 * jnp.tanh(0.5 * z) + 0.5


def _row_tile(rows):
    return TM if rows % TM == 0 else rows


def _coords():
    return lax.axis_index("x"), lax.axis_index("y"), lax.axis_index("c")


def _flip(v, bit):
    return 1 - v if bit else v


def _all_gather_small(blk, name):
    rows, cols = blk.shape

    def body(x_ref, out_ref, done, send_sems, recv_sems, local_sem):
        x, y, c = _coords()
        me = 4 * x + 2 * y + c
        mine = pltpu.make_async_copy(x_ref, out_ref.at[me], local_sem)
        mine.start()

        def copy(k, slot):
            peer = (_flip(x, k & 4), _flip(y, k & 2), _flip(c, k & 1))
            return pltpu.make_async_remote_copy(
                src_ref=x_ref, dst_ref=out_ref.at[slot], send_sem=send_sems.at[k - 1],
                recv_sem=recv_sems.at[k - 1], device_id=peer, device_id_type=MESH)

        for k in range(1, N_DEV):
            copy(k, me).start()
        for k in range(1, N_DEV):
            copy(k, jnp.bitwise_xor(me, k)).wait_recv()
        for k in range(1, N_DEV):
            copy(k, me).wait_send()
        mine.wait()
        done[...] = jnp.zeros((8, LANES), F32)

    vmem = pl.BlockSpec(memory_space=pltpu.VMEM)
    return pl.pallas_call(
        body, name=name,
        out_shape=[jax.ShapeDtypeStruct((N_DEV, rows, cols), blk.dtype), jax.ShapeDtypeStruct((8, LANES), F32)],
        in_specs=[vmem], out_specs=[vmem, vmem],
        scratch_shapes=[pltpu.SemaphoreType.DMA((N_DEV - 1,)), pltpu.SemaphoreType.DMA((N_DEV - 1,)),
                        pltpu.SemaphoreType.DMA],
    )(blk)


class _Exchange:
    def __init__(self, name, srcs, land_shapes, n_copies, plan, after=None):
        self.name, self.plan, self.n_copies = name, plan, n_copies
        self.n_src, self.n_land = len(srcs), len(land_shapes)
        hbm = lambda a: pltpu.HBM(a.shape, a.dtype)
        n = self.n_src + self.n_land
        lands = [pltpu.with_memory_space_constraint(lax.empty(s.shape, s.dtype), pltpu.HBM) for s in land_shapes]
        srcs = [pltpu.with_memory_space_constraint(s, pltpu.HBM) for s in srcs]
        extra = [] if after is None else [after]

        def body(*refs):
            send_sems, recv_sems = refs[n + len(extra)], refs[n + len(extra) + 1]
            for cp in self._copies(refs, send_sems, recv_sems):
                cp.start()
            refs[-1][...] = jnp.zeros((8, LANES), F32)

        outs = pl.pallas_call(
            body, name=name + "_start",
            out_shape=(pltpu.SemaphoreType.DMA((n_copies,)), pltpu.SemaphoreType.DMA((n_copies,)),
                       *[hbm(a) for a in srcs], *[hbm(a) for a in lands], jax.ShapeDtypeStruct((8, LANES), F32)),
            in_specs=[_HBM_SPEC] * n + [pl.BlockSpec(memory_space=pl.ANY)] * len(extra),
            out_specs=(_SEM_SPEC, _SEM_SPEC, *[_HBM_SPEC] * n, pl.BlockSpec(memory_space=pltpu.VMEM)),
            input_output_aliases={i: 2 + i for i in range(n)},
            compiler_params=pltpu.CompilerParams(has_side_effects=_EFFECT),
        )(*srcs, *lands, *extra)
        self.send_sems, self.recv_sems = outs[0], outs[1]
        self.buffers = list(outs[2:2 + n])
        self.token = outs[-1]
        self.waited = 0

    def _copies(self, refs, send_sems, recv_sems):
        src_refs, land_refs = refs[:self.n_src], refs[self.n_src:self.n_src + self.n_land]
        return [pltpu.make_async_remote_copy(src_ref=s, dst_ref=d, send_sem=send_sems.at[k], recv_sem=recv_sems.at[k],
                                             device_id=dev, device_id_type=MESH)
                for k, (s, d, dev) in enumerate(self.plan(src_refs, land_refs))]

    def wait(self, after, count=None):
        n = self.n_src + self.n_land
        lo = self.waited
        hi = self.n_copies if count is None else lo + count
        self.waited = hi

        def body(*refs):
            for cp in self._copies(refs, refs[n], refs[n + 1])[lo:hi]:
                cp.wait_send()
                cp.wait_recv()

        outs = pl.pallas_call(
            body, name=f"{self.name}_wait{lo}" if lo or hi < self.n_copies else self.name + "_wait",
            out_shape=tuple(pltpu.HBM(b.shape, b.dtype) for b in self.buffers),
            in_specs=[_HBM_SPEC] * n + [_SEM_SPEC, _SEM_SPEC, pl.BlockSpec(memory_space=pl.ANY)],
            out_specs=tuple([_HBM_SPEC] * n),
            input_output_aliases={i: i for i in range(n)},
            compiler_params=pltpu.CompilerParams(has_side_effects=_EFFECT),
        )(*self.buffers, self.send_sems, self.recv_sems, after)
        self.buffers = list(outs)
        return list(outs[:self.n_src]), list(outs[self.n_src:])


def _pair_plan(src_index):
    def plan(srcs, lands):
        x, y, c = _coords()
        chip = 2 * x + y
        return [(srcs[0].at[src_index(j, chip, c)], lands[0].at[j], (x, y, 1 - c)) for j in range(4)]
    return plan


def _chips_plan(srcs, lands):
    x, y, c = _coords()
    return [(srcs[0].at[j - 1], lands[0].at[j - 1], (_flip(x, j & 2), _flip(y, j & 1), c)) for j in range(1, 4)]


def _peer(k):
    x, y, c = _coords()
    return _flip(x, k & 4), _flip(y, k & 2), _flip(c, k & 1)


def _send_plan(ks):
    def plan(srcs, lands):
        return [(srcs[0].at[0], lands[0].at[p], _peer(k)) for p, k in enumerate(ks)]
    return plan


def _forward_plan(moves):
    def plan(srcs, lands):
        return [(srcs[0].at[s], srcs[0].at[d], _peer(1)) for s, d in moves]
    return plan


def _gather_plan(srcs, lands):
    x, y, c = _coords()
    me = 4 * x + 2 * y + c
    return [(srcs[a], lands[a].at[me], (_flip(x, k & 4), _flip(y, k & 2), _flip(c, k & 1)))
            for a in range(len(srcs)) for k in range(1, N_DEV)]


def _pair_sum(grad, recv, owners, name):
    _, rows, cols = grad.shape
    tr = _row_tile(rows)

    def body(own_ref, g_ref, r_ref, mine_ref, send_ref):
        j = pl.program_id(1)
        total = g_ref[...] + r_ref[...]

        @pl.when(j == 0)
        def _():
            mine_ref[...] = total

        @pl.when(j > 0)
        def _():
            send_ref[...] = total.astype(BF16)

    grid_spec = pltpu.PrefetchScalarGridSpec(
        num_scalar_prefetch=1, grid=(rows // tr, 4),
        in_specs=[pl.BlockSpec((1, tr, cols), lambda i, j, own: (own[j], i, 0)),
                  pl.BlockSpec((1, tr, cols), lambda i, j, own: (j, i, 0))],
        out_specs=[pl.BlockSpec((1, tr, cols), lambda i, j, own: (0, i, 0)),
                   pl.BlockSpec((1, tr, cols), lambda i, j, own: (jnp.maximum(j - 1, 0), i, 0))])
    return pl.pallas_call(
        body, name=name, grid_spec=grid_spec,
        out_shape=[jax.ShapeDtypeStruct((1, rows, cols), F32), jax.ShapeDtypeStruct((3, rows, cols), BF16)],
        compiler_params=_cparams(dimension_semantics=("arbitrary", "arbitrary")),
    )(owners, grad, recv)


def _modulation(c16, ada_w_sh, ada_b_sh, decay_logit):
    def body(c_ref, w_ref, b_ref, dl_ref, mod_ref, act_ref, lg_ref):
        cv = c_ref[...]
        act = cv * _sigmoid(cv)
        act_ref[...] = act
        mod_ref[...] = jnp.dot(act, w_ref[...], preferred_element_type=F32,
                               precision=lax.Precision.HIGHEST) + b_ref[...]
        z = dl_ref[...]
        lg_ref[...] = jnp.minimum(z, 0.0) - jnp.log(1.0 + jnp.exp(-jnp.abs(z)))

    return pl.pallas_call(
        body, name="modulation",
        out_shape=[jax.ShapeDtypeStruct((16, ADA_SH), F32), jax.ShapeDtypeStruct((16, D), F32),
                   jax.ShapeDtypeStruct(decay_logit.shape, F32)],
        compiler_params=_cparams(),
    )(c16, ada_w_sh, ada_b_sh, decay_logit)


def _adam_update(w, g, m, v):
    m2 = ADAM_B1 * m + (1.0 - ADAM_B1) * g
    v2 = ADAM_B2 * v + (1.0 - ADAM_B2) * (g * g)
    m_hat = m2 / (1.0 - ADAM_B1 ** ADAM_STEP)
    v_hat = v2 / (1.0 - ADAM_B2 ** ADAM_STEP)
    delta = -ADAM_LR * (m_hat / (jnp.sqrt(v_hat) + ADAM_EPS) + ADAM_WD * w)
    return delta, m2, v2


def _adam_sharded(w, m, v, own, recv, name):
    rows, cols = w.shape
    tr = _row_tile(rows)

    def body(w_ref, m_ref, v_ref, g0, g1, g2, g3, g_ref, d_ref, m_out, v_out):
        g = ((g0[0] + g1[0].astype(F32)) + g2[0].astype(F32)) + g3[0].astype(F32)
        delta, m2, v2 = _adam_update(w_ref[...], g, m_ref[...], v_ref[...])
        g_ref[...] = g
        d_ref[...] = delta
        m_out[...] = m2
        v_out[...] = v2

    flat = pl.BlockSpec((tr, cols), lambda i: (i, 0))
    part = lambda j: pl.BlockSpec((1, tr, cols), lambda i, j=j: (j, i, 0))
    return pl.pallas_call(
        body, name=name, grid=(rows // tr,),
        in_specs=[flat, flat, flat, part(0), part(0), part(1), part(2)],
        out_specs=[flat] * 4,
        out_shape=[jax.ShapeDtypeStruct((rows, cols), F32)] * 4,
        compiler_params=_cparams(dimension_semantics=("arbitrary",)),
    )(w, m, v, own, recv, recv, recv)


def _ada_backward(act_t, dmod16, dmodc8, ada_w_sh, m, v):
    def body(at_ref, dm_ref, dc_ref, w_ref, m_ref, v_ref, g_ref, d_ref, m_out, v_out, pc_ref):
        g = jnp.dot(at_ref[...], dm_ref[...], preferred_element_type=F32, precision=lax.Precision.HIGHEST)
        w = w_ref[...]
        delta, m2, v2 = _adam_update(w, g, m_ref[...], v_ref[...])
        g_ref[...] = g
        d_ref[...] = delta
        m_out[...] = m2
        v_out[...] = v2
        pc_ref[...] = lax.dot_general(dc_ref[...], w, (((1,), (1,)), ((), ())), preferred_element_type=F32,
                                      precision=lax.Precision.HIGHEST)

    return pl.pallas_call(
        body, name="ada_backward",
        out_shape=[jax.ShapeDtypeStruct((D, ADA_SH), F32)] * 4 + [jax.ShapeDtypeStruct((8, D), F32)],
        compiler_params=_cparams(),
    )(act_t, dmod16, dmodc8, ada_w_sh, m, v)


def _sum_devices(gathered, name):
    _, rows, cols = gathered.shape

    def body(g_ref, o_ref):
        acc = g_ref[0]
        for d in range(1, N_DEV):
            acc = acc + g_ref[d]
        o_ref[...] = acc

    return pl.pallas_call(body, name=name, out_shape=jax.ShapeDtypeStruct((rows, cols), F32),
                          compiler_params=_cparams())(gathered)


SMALL_PARAMS = ("c_ctx", "norm_w", "conv_b", "gn_w", "final_norm_w", "ada_b", "decay_logit", "conv_w")


def _adam_small(tot, dact_ctx, conv_w_grad, params):
    n = len(SMALL_PARAMS)

    def body(tot_ref, dact_ref, cwg_ref, *refs):
        ins, outs = refs[:3 * n], refs[3 * n:]
        nh = 2 * N_HEADS
        raw = {
            "c_ctx": dact_ref[0:1, :],
            "norm_w": tot_ref[1:2, :], "conv_b": tot_ref[2:3, :], "gn_w": tot_ref[3:4, :],
            "final_norm_w": tot_ref[0:1, :],
            "ada_b": jnp.concatenate([tot_ref[7 + r:8 + r, :] + tot_ref[10 + r:11 + r, :] for r in range(3)], axis=1),
            "decay_logit": tot_ref[14:15, 0:nh],
            "conv_w": cwg_ref[...],
        }
        for k, name in enumerate(SMALL_PARAMS):
            w, m, v = ins[3 * k][...], ins[3 * k + 1][...], ins[3 * k + 2][...]
            g = raw[name]
            if name == "c_ctx":
                s = _sigmoid(w)
                g = g * (s * (1.0 + w * (1.0 - s)))
            elif name == "decay_logit":
                g = g * (1.0 - _sigmoid(w))
            delta, m2, v2 = _adam_update(w, g, m, v)
            for ref, val in zip(outs[4 * k:4 * k + 4], (g, delta, m2, v2)):
                ref[...] = val

    flat = [a for trio in params for a in trio]
    out_shape = [jax.ShapeDtypeStruct(trio[0].shape, F32) for trio in params for _ in range(4)]
    outs = pl.pallas_call(body, name="adam_small", out_shape=out_shape, compiler_params=_cparams())(
        tot, dact_ctx, conv_w_grad, *flat)
    return [outs[4 * k:4 * k + 4] for k in range(n)]


def _adam_square(params, own, recv):
    def body(own_ref, recv_ref, *refs):
        ins, outs = refs[:9], refs[9:]
        for k in range(3):
            rows = slice(k * RSH, (k + 1) * RSH)
            g = own_ref[0, rows, :]
            for j in range(3):
                g = g + recv_ref[j, rows, :].astype(F32)
            delta, m2, v2 = _adam_update(ins[3 * k][...], g, ins[3 * k + 1][...], ins[3 * k + 2][...])
            for ref, val in zip(outs[4 * k:4 * k + 4], (g, delta, m2, v2)):
                ref[...] = val

    flat = [a for trio in params for a in trio]
    outs = pl.pallas_call(body, name="adam_square", out_shape=[jax.ShapeDtypeStruct((RSH, D), F32)] * 12,
                          compiler_params=_cparams())(own, recv, *flat)
    return [outs[4 * k:4 * k + 4] for k in range(3)]


def _rope(t, cos, sin):
    lane = lax.broadcasted_iota(jnp.int32, (1, LANES), 1)
    first_half = jnp.bitwise_and(lane, DK // 2) == 0
    partner = jnp.where(first_half, pltpu.roll(t, LANES - DK // 2, 1), pltpu.roll(t, DK // 2, 1))
    return t * cos + partner * sin


class _Shards:
    def __init__(self, arrays, picks):
        self.arrays, self.picks = list(arrays), list(picks)

    def specs(self, index_args):
        out = []
        for a, arr in enumerate(self.arrays):
            steps = [p for p, (ai, _) in enumerate(self.picks) if ai == a]
            first, slot0, last = steps[0], self.picks[steps[0]][1], arr.shape[0] - 1
            out.append(pl.BlockSpec(
                (1, D, WSH), lambda *args, first=first, slot0=slot0, last=last:
                (jnp.clip(index_args(*args) - first + slot0, 0, last), 0, 0)))
        return out

    def __add__(self, other):
        shift = len(self.arrays)
        return _Shards(self.arrays + other.arrays, self.picks + [(a + shift, s) for a, s in other.picks])


def _load_w_in(step, ids_ref, shards, w_refs, w_vmem, sems):
    @pl.when(step == 0)
    def _():
        copies = []
        for p, (a, slot) in enumerate(shards.picks):
            col = pl.multiple_of(ids_ref[p] * WSH, LANES)
            copies.append(pltpu.make_async_copy(w_refs[a].at[slot], w_vmem.at[:, pl.ds(col, WSH)], sems.at[p]))
        for cp in copies:
            cp.start()
        for cp in copies:
            cp.wait()


def _project_columns(xmb, w_ref, shard, cos, sin, p_ref):
    groups = WSH // LANES
    q0, k0, v0 = CB_QK * D // LANES, (CB_QK * D + N_HEADS * DK) // LANES, CB_V * D // LANES
    mxu_cols = 2 * LANES
    for a in range(0, WSH, mxu_cols):
        b = min(a + mxu_cols, WSH)
        acc = _dot(xmb, w_ref[0, :, a:b])
        for col in range(0, b - a, LANES):
            grp = shard * groups + (a + col) // LANES
            rotary = jnp.logical_and(grp >= q0, grp < v0)
            scale = jnp.where(grp >= k0, DK ** -0.5, 1.0)
            piece = acc[:, col:col + LANES]
            turned = _rope(piece * scale, cos, sin)
            p_ref[:, a + col:a + col + LANES] = jnp.where(rotary, turned, piece).astype(BF16)


def _prenorm(x, ctx, modx, modc, norm_w):
    n_x_tiles = x.shape[0] // TM
    lext = x.shape[0] + ctx.shape[0]

    def body(x_ref, ctx_ref, mx_ref, mc_ref, nw_ref, xm_ref, xmt_ref):
        is_ctx = pl.program_id(0) >= n_x_tiles
        x = jnp.where(is_ctx, ctx_ref[...], x_ref[...])
        r = lax.rsqrt(jnp.mean(x * x, axis=-1, keepdims=True) + EPS)
        sh = jnp.where(is_ctx, mc_ref[0:1, :], mx_ref[0:1, :])
        sc = jnp.where(is_ctx, mc_ref[1:2, :], mx_ref[1:2, :])
        xm = (x * r * nw_ref[...]) * (1.0 + sc) + sh
        xm_ref[...] = xm.astype(BF16)
        xmt_ref[...] = xm.T.astype(BF16)

    full = lambda a: pl.BlockSpec(a.shape, lambda i: (0,) * a.ndim)
    return pl.pallas_call(
        body, name="prenorm", grid=(lext // TM,),
        in_specs=[pl.BlockSpec((TM, D), lambda i: (jnp.minimum(i, n_x_tiles - 1), 0)), full(ctx),
                  full(modx), full(modc), full(norm_w)],
        out_specs=[pl.BlockSpec((TM, D), lambda i: (i, 0)), pl.BlockSpec((D, TM), lambda i: (0, i))],
        out_shape=[jax.ShapeDtypeStruct((lext, D), BF16), jax.ShapeDtypeStruct((D, lext), BF16)],
        compiler_params=_cparams(dimension_semantics=("arbitrary",)),
    )(x, ctx, modx, modc, norm_w)


def _in_projection(p_ext, xm, cos_t, sin_t, shards, ids, name):
    lext = xm.shape[0]
    n_w = len(shards.arrays)
    tr = lext // 4

    assert n_w == 1

    def body(ids_ref, xm_ref, cos_ref, sin_ref, *rest):
        w_refs, p_ref = rest[:n_w], rest[-1]
        _project_columns(xm_ref[...], w_refs[0], ids_ref[pl.program_id(0)], cos_ref[...], sin_ref[...], p_ref)

    row = lambda w: pl.BlockSpec((tr, w), lambda j, i, ids: (i, 0))
    grid_spec = pltpu.PrefetchScalarGridSpec(
        num_scalar_prefetch=1, grid=(len(shards.picks), lext // tr),
        in_specs=[row(D), row(LANES), row(LANES)] + shards.specs(lambda j, i, ids: j)
        + ([] if p_ext is None else [pl.BlockSpec(memory_space=pl.ANY)]),
        out_specs=pl.BlockSpec((tr, WSH), lambda j, i, ids: (i, ids[j])))
    return pl.pallas_call(
        body, name=name, grid_spec=grid_spec,
        out_shape=jax.ShapeDtypeStruct((lext, PW), BF16),
        input_output_aliases={} if p_ext is None else {4 + n_w: 0},
        compiler_params=_cparams(dimension_semantics=("arbitrary", "arbitrary")),
    )(ids, xm, cos_t, sin_t, *shards.arrays, *([] if p_ext is None else [p_ext]))


def _decay_tables(lgf, lgb, n):
    i = lax.broadcasted_iota(jnp.int32, (n, 1), 0).astype(F32)
    return dict(i=i, k_f=jnp.exp(lgf * (n - 1.0 - i)), k_b=jnp.exp(lgb * i),
                q_f=jnp.exp(lgf * (i + 1.0)), q_b=jnp.exp(lgb * (n - i)))


def _decay_matrix(lgf, lgb, n, transposed=False):
    ii = lax.broadcasted_iota(jnp.int32, (n, n), 0)
    jj = lax.broadcasted_iota(jnp.int32, (n, n), 1)
    diff = (jj - ii if transposed else ii - jj).astype(F32)
    low = jnp.exp(lgf * jnp.maximum(diff, 0.0))
    up = jnp.exp(lgb * jnp.maximum(-diff, 0.0))
    return jnp.where(diff > 0, low, jnp.where(diff < 0, up, 2.0)), diff


def _cat_lanes(a, b):
    return jnp.concatenate([a.astype(BF16), b.astype(BF16)], axis=1)


def _retention_forward(p_ext, lg, seq, ctx_len):
    lext = seq + ctx_len
    n_chunks = seq // RET_C
    C = RET_C

    def body(lg_ref, q_ref, k_ref, v_ref, rn_ref, rstd_ref, kv_scr, sf_scr, sb_scr):
        pair = pl.program_id(0)
        lane = lax.broadcasted_iota(jnp.int32, (1, LANES), 1)
        heads = range(2)
        hmask = [(lane // DK == hh).astype(F32) for hh in heads]
        lgf = [lg_ref[0, 2 * pair + hh] for hh in heads]
        lgb = [lg_ref[1, 2 * pair + hh] for hh in heads]
        vs = [slice(hh * DV, (hh + 1) * DV) for hh in heads]
        t = [_decay_tables(lgf[hh], lgb[hh], C) for hh in heads]
        kc_all = k_ref[seq:lext, :].astype(F32)
        s0 = []
        for hh in heads:
            tc = _decay_tables(lgf[hh], lgb[hh], ctx_len)
            kc = kc_all * hmask[hh]
            s0.append(_dot_tn(_cat_lanes(kc * tc["k_f"], kc * tc["k_b"]), v_ref[seq:lext, vs[hh]]))

        def increments(c, carry):
            rows = pl.ds(pl.multiple_of(c * C, C), C)
            k_all = k_ref[rows, :].astype(F32)
            for hh in heads:
                k = k_all * hmask[hh]
                kv_scr[hh, c] = _dot_tn(_cat_lanes(k * t[hh]["k_f"], k * t[hh]["k_b"]), v_ref[rows, vs[hh]])
            return carry

        lax.fori_loop(0, n_chunks, increments, 0, unroll=2)
        gf_c = [jnp.exp(lgf[hh] * C) for hh in heads]
        gb_c = [jnp.exp(lgb[hh] * C) for hh in heads]

        def scan_f(c, s):
            for hh in heads:
                sf_scr[hh, c] = s[hh]
            return tuple(gf_c[hh] * s[hh] + kv_scr[hh, c, 0:LANES, :] for hh in heads)

        def scan_b(n, s):
            c = n_chunks - 1 - n
            for hh in heads:
                sb_scr[hh, c] = s[hh]
            return tuple(gb_c[hh] * s[hh] + kv_scr[hh, c, LANES:2 * LANES, :] for hh in heads)

        lax.fori_loop(0, n_chunks, scan_f, tuple(s0[hh][0:LANES] for hh in heads))
        lax.fori_loop(0, n_chunks, scan_b, tuple(s0[hh][LANES:2 * LANES] for hh in heads))
        dmat = [_decay_matrix(lgf[hh], lgb[hh], C)[0] for hh in heads]

        def outputs(c, carry):
            rows = pl.ds(pl.multiple_of(c * C, C), C)
            q_all = q_ref[rows, :].astype(F32)
            k = k_ref[rows, :]
            for hh in heads:
                q = q_all * hmask[hh]
                v = v_ref[rows, vs[hh]]
                s = _dot_nt(q.astype(BF16), k)
                inner = _dot((s * dmat[hh]).astype(BF16), v)
                states = jnp.concatenate([sf_scr[hh, c], sb_scr[hh, c]], axis=0).astype(BF16)
                ret = inner + _dot(_cat_lanes(q * t[hh]["q_f"], q * t[hh]["q_b"]), states)
                mu = jnp.mean(ret, axis=-1, keepdims=True)
                cen = ret - mu
                rstd = lax.rsqrt(jnp.mean(cen * cen, axis=-1, keepdims=True) + EPS)
                rn_ref[rows, vs[hh]] = cen * rstd
                rstd_ref[rows, vs[hh]] = jnp.broadcast_to(rstd, (C, DV))
            return carry

        lax.fori_loop(0, n_chunks, outputs, 0, unroll=2)

    qk0 = CB_QK * D // LANES
    return pl.pallas_call(
        body, name="retention_forward", grid=(N_HEADS // 2,),
        in_specs=[pl.BlockSpec(memory_space=pltpu.SMEM),
                  pl.BlockSpec((lext, LANES), lambda g: (0, qk0 + g)),
                  pl.BlockSpec((lext, LANES), lambda g: (0, qk0 + N_HEADS // 2 + g)),
                  pl.BlockSpec((lext, 2 * DV), lambda g: (0, CB_V * D // (2 * DV) + g))],
        out_specs=[pl.BlockSpec((seq, 2 * DV), lambda g: (0, g))] * 2,
        out_shape=[jax.ShapeDtypeStruct((seq, D), F32)] * 2,
        scratch_shapes=[pltpu.VMEM((2, n_chunks, 2 * LANES, DV), F32), pltpu.VMEM((2, n_chunks, LANES, DV), F32),
                        pltpu.VMEM((2, n_chunks, LANES, DV), F32)],
        compiler_params=_cparams(dimension_semantics=("arbitrary",)),
    )(lg, p_ext, p_ext, p_ext)


def _retention_backward(p_ext, dret, lg, cos_t, sin_t, seq, ctx_len):
    lext = seq + ctx_len
    n_chunks = seq // RET_C
    C = RET_C

    def body(lg_ref, q_ref, k_ref, v_ref, do_ref, cos_ref, sin_ref, dq_ref, dk_ref, dv_ref, dlg_ref,
             kv_scr, g_scr, sf_scr, sb_scr, gfn_scr, gbp_scr):
        pair = pl.program_id(0)
        lane = lax.broadcasted_iota(jnp.int32, (1, LANES), 1)
        heads = range(2)
        hmask = [(lane // DK == hh).astype(F32) for hh in heads]
        lgf = [lg_ref[0, 2 * pair + hh] for hh in heads]
        lgb = [lg_ref[1, 2 * pair + hh] for hh in heads]
        vs = [slice(hh * DV, (hh + 1) * DV) for hh in heads]
        t = [_decay_tables(lgf[hh], lgb[hh], C) for hh in heads]
        tc = [_decay_tables(lgf[hh], lgb[hh], ctx_len) for hh in heads]
        kc_all = k_ref[seq:lext, :].astype(F32)
        kc = [kc_all * hmask[hh] for hh in heads]
        vc = [v_ref[seq:lext, vs[hh]] for hh in heads]
        kc_cat = [_cat_lanes(kc[hh] * tc[hh]["k_f"], kc[hh] * tc[hh]["k_b"]) for hh in heads]
        s0 = [_dot_tn(kc_cat[hh], vc[hh]) for hh in heads]

        def increments(c, carry):
            rows = pl.ds(pl.multiple_of(c * C, C), C)
            k_all = k_ref[rows, :].astype(F32)
            q_all = q_ref[rows, :].astype(F32)
            for hh in heads:
                k, q = k_all * hmask[hh], q_all * hmask[hh]
                kv_scr[hh, c] = _dot_tn(_cat_lanes(k * t[hh]["k_f"], k * t[hh]["k_b"]), v_ref[rows, vs[hh]])
                g_scr[hh, c] = _dot_tn(_cat_lanes(q * t[hh]["q_f"], q * t[hh]["q_b"]), do_ref[rows, vs[hh]])
            return carry

        lax.fori_loop(0, n_chunks, increments, 0, unroll=2)
        gf_c = [jnp.exp(lgf[hh] * C) for hh in heads]
        gb_c = [jnp.exp(lgb[hh] * C) for hh in heads]

        def scan_f(c, s):
            for hh in heads:
                sf_scr[hh, c] = s[hh]
            return tuple(gf_c[hh] * s[hh] + kv_scr[hh, c, 0:LANES, :] for hh in heads)

        def scan_b(n, s):
            c = n_chunks - 1 - n
            for hh in heads:
                sb_scr[hh, c] = s[hh]
            return tuple(gb_c[hh] * s[hh] + kv_scr[hh, c, LANES:2 * LANES, :] for hh in heads)

        def scan_gf(n, carry):
            c = n_chunks - 1 - n
            for hh in heads:
                gfn_scr[hh, c] = carry[hh]
            return tuple(g_scr[hh, c, 0:LANES, :] + gf_c[hh] * carry[hh] for hh in heads)

        def scan_gb(c, carry):
            for hh in heads:
                gbp_scr[hh, c] = carry[hh]
            return tuple(g_scr[hh, c, LANES:2 * LANES, :] + gb_c[hh] * carry[hh] for hh in heads)

        lax.fori_loop(0, n_chunks, scan_f, tuple(s0[hh][0:LANES] for hh in heads))
        lax.fori_loop(0, n_chunks, scan_b, tuple(s0[hh][LANES:2 * LANES] for hh in heads))
        zero_state = jnp.zeros((LANES, DV), F32)
        gf_first = lax.fori_loop(0, n_chunks, scan_gf, (zero_state, zero_state))
        gb_last = lax.fori_loop(0, n_chunks, scan_gb, (zero_state, zero_state))

        dmat, w_f, w_b = [], [], []
        for hh in heads:
            dm, diff = _decay_matrix(lgf[hh], lgb[hh], C)
            dmat.append(dm)
            w_f.append(jnp.where(diff > 0, diff * dm, 0.0))
            w_b.append(jnp.where(diff < 0, -diff * dm, 0.0))

        def total(a):
            rows_, width = a.shape
            part = jnp.sum(a.reshape(rows_ // 8, 8, width), axis=0)
            return part[:, 0:LANES] + part[:, LANES:2 * LANES] if width == 2 * LANES else part

        def chunk_head(hh, c, rows, q_all, k_all, dlgf, dlgb):
            th = t[hh]
            qm = q_all * hmask[hh]
            km = k_all * hmask[hh]
            qb, kb = qm.astype(BF16), km.astype(BF16)
            v = v_ref[rows, vs[hh]]
            do = do_ref[rows, vs[hh]]
            s = _dot_nt(qb, kb)
            dsd = _dot_nt(do, v)
            ds = (dsd * dmat[hh]).astype(BF16)
            dq_in = _dot(ds, kb)
            dk_in = _dot_tn(ds, qb)
            dv_in = _dot_tn((s * dmat[hh]).astype(BF16), do)
            prod = s * dsd
            dlgf = dlgf + total(prod * w_f[hh])
            dlgb = dlgb + total(prod * w_b[hh])
            sf, sb = sf_scr[hh, c], sb_scr[hh, c]
            states = jnp.concatenate([sf, sb], axis=0).astype(BF16)
            dqc = _dot_nt(do, states)
            dqf = dqc[:, 0:LANES] * th["q_f"]
            dqb = dqc[:, LANES:2 * LANES] * th["q_b"]
            dq = (dq_in + dqf + dqb) * hmask[hh]
            dlgf = dlgf + total((th["i"] + 1.0) * (qm * dqf))
            dlgb = dlgb + total((C - th["i"]) * (qm * dqb))
            gfn, gbp = gfn_scr[hh, c], gbp_scr[hh, c]
            gstates = jnp.concatenate([gfn, gbp], axis=0).astype(BF16)
            dkc = _dot_nt(v, gstates)
            dkf = dkc[:, 0:LANES] * th["k_f"]
            dkb = dkc[:, LANES:2 * LANES] * th["k_b"]
            dk = dk_in + dkf + dkb
            dlgf = dlgf + total((C - 1.0 - th["i"]) * (km * dkf)) + C * gf_c[hh] * total(gfn * sf)
            dlgb = dlgb + total(th["i"] * (km * dkb)) + C * gb_c[hh] * total(gbp * sb)
            dv = dv_in + _dot(_cat_lanes(km * th["k_f"], km * th["k_b"]), gstates)
            dv_ref[rows, vs[hh]] = dv.astype(BF16)
            return dq, dk, dlgf, dlgb

        def chunk(c, carry):
            rows = pl.ds(pl.multiple_of(c * C, C), C)
            q_all = q_ref[rows, :].astype(F32)
            k_all = k_ref[rows, :].astype(F32)
            dq0, dk0, f0, b0 = chunk_head(0, c, rows, q_all, k_all, carry[0], carry[1])
            dq1, dk1, f1, b1 = chunk_head(1, c, rows, q_all, k_all, carry[2], carry[3])
            cos, sin = cos_ref[rows, :], sin_ref[rows, :]
            dq_ref[rows, :] = _rope(dq0 + dq1, cos, -sin).astype(BF16)
            dk_ref[rows, :] = (_rope(dk0 + dk1, cos, -sin) * (DK ** -0.5)).astype(BF16)
            return f0, b0, f1, b1

        zero = jnp.zeros((8, LANES), F32)
        sums = lax.fori_loop(0, n_chunks, chunk, (zero, zero, zero, zero), unroll=2)

        dlg = []
        dk_ctx = jnp.zeros((ctx_len, LANES), F32)
        for hh in heads:
            g0 = jnp.concatenate([gf_first[hh], gb_last[hh]], axis=0).astype(BF16)
            dkcc = _dot_nt(vc[hh], g0)
            dkcf = dkcc[:, 0:LANES] * tc[hh]["k_f"]
            dkcb = dkcc[:, LANES:2 * LANES] * tc[hh]["k_b"]
            dlgf = sums[2 * hh] + total((ctx_len - 1.0 - tc[hh]["i"]) * (kc[hh] * dkcf))
            dlgb = sums[2 * hh + 1] + total(tc[hh]["i"] * (kc[hh] * dkcb))
            dk_ctx = dk_ctx + (dkcf + dkcb) * (DK ** -0.5)
            dv_ref[seq:lext, vs[hh]] = _dot(kc_cat[hh], g0).astype(BF16)
            dlg += [jnp.sum(jnp.sum(a, axis=1, keepdims=True), axis=0, keepdims=True) for a in (dlgf, dlgb)]
        dk_ref[seq:lext, :] = dk_ctx.astype(BF16)
        dq_ref[seq:lext, :] = jnp.zeros((ctx_len, LANES), BF16)

        lane8 = lax.broadcasted_iota(jnp.int32, (8, LANES), 1)
        out = jnp.zeros((8, LANES), F32)
        for n, val in enumerate(dlg):
            out = jnp.where(lane8 == n, val, out)
        dlg_ref[0] = out

    qk0 = CB_QK * D // LANES
    q_spec = pl.BlockSpec((lext, LANES), lambda g: (0, qk0 + g))
    k_spec = pl.BlockSpec((lext, LANES), lambda g: (0, qk0 + N_HEADS // 2 + g))
    v_spec = pl.BlockSpec((lext, 2 * DV), lambda g: (0, CB_V * D // (2 * DV) + g))
    table = pl.BlockSpec((lext, LANES), lambda g: (0, 0))
    state = pltpu.VMEM((2, n_chunks, LANES, DV), F32)
    return pl.pallas_call(
        body, name="retention_backward", grid=(N_HEADS // 2,),
        in_specs=[pl.BlockSpec(memory_space=pltpu.SMEM), q_spec, k_spec, v_spec,
                  pl.BlockSpec((seq, 2 * DV), lambda g: (0, g)), table, table],
        out_specs=[pl.BlockSpec((lext, LANES), lambda g: (0, g)), pl.BlockSpec((lext, LANES), lambda g: (0, g)),
                   pl.BlockSpec((lext, 2 * DV), lambda g: (0, g)), pl.BlockSpec((1, 8, LANES), lambda g: (g, 0, 0))],
        out_shape=[jax.ShapeDtypeStruct((lext, N_HEADS * DK), BF16), jax.ShapeDtypeStruct((lext, N_HEADS * DK), BF16),
                   jax.ShapeDtypeStruct((lext, D), BF16), jax.ShapeDtypeStruct((N_HEADS // 2, 8, LANES), F32)],
        scratch_shapes=[pltpu.VMEM((2, n_chunks, 2 * LANES, DV), F32), pltpu.VMEM((2, n_chunks, 2 * LANES, DV), F32),
                        state, state, state, state],
        compiler_params=_cparams(dimension_semantics=("arbitrary",)),
    )(lg, p_ext, p_ext, p_ext, dret, cos_t, sin_t)


def _merge(p_ext, rn, rstd, x, target, w_a, w_b, w_out, vecs, seq):
    n_tiles = seq // TM
    hb = TM // HALO

    def body(h_ref, bg_ref, cg_ref, za_ref, zb_ref, ga_ref, gb_ref, hp_ref, hn_ref, cp_ref, cn_ref,
             rn_ref, rstd_ref, x_ref, t_ref, wa_ref, wb_ref, wo_ref, vec_ref,
             dx1_ref, dpa_ref, dconv_ref, dret_ref, at_ref, b_ref, part_ref):
        i = pl.program_id(0)
        f = lambda ref: ref[...].astype(F32)
        h, bg, cg, za, zb, ga, gb = f(h_ref), f(bg_ref), f(cg_ref), f(za_ref), f(zb_ref), f(ga_ref), f(gb_ref)
        gx, w0, w1, w2 = vec_ref[0:1, :], vec_ref[1:2, :], vec_ref[2:3, :], vec_ref[3:4, :]
        cb, gnw, fw = vec_ref[4:5, :], vec_ref[5:6, :], vec_ref[6:7, :]
        u = cg * h
        row = lax.broadcasted_iota(jnp.int32, (TM, 1), 0)
        u_prev = (f(cp_ref) * f(hp_ref))[HALO - 1:HALO, :]
        u_next = (f(cn_ref) * f(hn_ref))[0:1, :]
        u_prev = jnp.where(i == 0, 0.0, u_prev)
        u_next = jnp.where(i == n_tiles - 1, 0.0, u_next)
        u_up = jnp.where(row == 0, u_prev, pltpu.roll(u, 1, 0))
        u_dn = jnp.where(row == TM - 1, u_next, pltpu.roll(u, TM - 1, 0))
        conv = w0 * u_up + w1 * u + w2 * u_dn + cb
        sza = _sigmoid(za)
        silu_za = za * sza
        a_act = silu_za * bg * conv
        rn = rn_ref[...]
        szb = _sigmoid(zb)
        silu_zb = zb * szb
        rg = rn * gnw
        b_act = silu_zb * rg
        y_a = _dot(a_act.astype(BF16), wa_ref[...])
        y_b = _dot(b_act.astype(BF16), wb_ref[...])
        sga, sgb = _sigmoid(ga), _sigmoid(gb)
        mix = sga * y_a + sgb * y_b
        y = _dot(mix.astype(BF16), wo_ref[...])
        x1 = x_ref[...] + gx * y
        r1 = lax.rsqrt(jnp.mean(x1 * x1, axis=-1, keepdims=True) + EPS)
        xh1 = x1 * r1
        err = xh1 * fw - t_ref[...]
        loss = jnp.sum(jnp.sum(err * err, axis=1, keepdims=True), axis=0, keepdims=True) * (0.5 / D)
        dout = err * (1.0 / D)
        dxh = dout * fw
        dx1 = r1 * (dxh - xh1 * jnp.mean(dxh * xh1, axis=-1, keepdims=True))
        dx1_ref[...] = dx1
        dy = (dx1 * gx).astype(BF16)
        dmix = _dot_nt(dy, wo_ref[...])
        dya = (dmix * sga).astype(BF16)
        dyb = (dmix * sgb).astype(BF16)
        da = _dot_nt(dya, wa_ref[...])
        db = _dot_nt(dyb, wb_ref[...])
        dpa_ref[:, 0:D] = (da * silu_za * conv).astype(BF16)
        dpa_ref[:, D:2 * D] = (da * bg * conv * (sza * (1.0 + za * (1.0 - sza)))).astype(BF16)
        dpa_ref[:, 2 * D:3 * D] = (db * rg * (szb * (1.0 + zb * (1.0 - szb)))).astype(BF16)
        dpa_ref[:, 3 * D:4 * D] = (dmix * y_a * sga * (1.0 - sga)).astype(BF16)
        dpa_ref[:, 4 * D:5 * D] = (dmix * y_b * sgb * (1.0 - sgb)).astype(BF16)
        dconv_ref[...] = (da * silu_za * bg).astype(BF16)
        drn_n = db * silu_zb
        drn = drn_n * gnw
        rstd = rstd_ref[...]
        for hd in range(N_HEADS):
            sl = slice(hd * DV, (hd + 1) * DV)
            dh_, rh = drn[:, sl], rn[:, sl]
            m1 = jnp.mean(dh_, axis=-1, keepdims=True)
            m2 = jnp.mean(dh_ * rh, axis=-1, keepdims=True)
            dret_ref[:, sl] = (rstd[:, sl] * (dh_ - m1 - rh * m2)).astype(BF16)
        at_ref[0] = a_act.T.astype(BF16)
        at_ref[1] = b_act.T.astype(BF16)
        at_ref[2] = mix.T.astype(BF16)
        b_ref[0] = dya
        b_ref[1] = dyb
        b_ref[2] = dy

        @pl.when(i == 0)
        def _():
            part_ref[...] = jnp.zeros((8, D), F32)

        part_ref[0:1, :] += jnp.sum(dout * xh1, axis=0, keepdims=True)
        part_ref[1:2, :] += jnp.sum(dx1 * y, axis=0, keepdims=True)
        part_ref[2:3, :] += jnp.sum(drn_n * rn, axis=0, keepdims=True)
        part_ref[3:4, :] += jnp.broadcast_to(loss, (1, D))

    col = lambda cb_: pl.BlockSpec((TM, D), lambda i, cb_=cb_: (i, cb_))
    prev = lambda cb_: pl.BlockSpec((HALO, D), lambda i, cb_=cb_: (jnp.maximum(i * hb - 1, 0), cb_))
    nxt = lambda cb_: pl.BlockSpec((HALO, D), lambda i, cb_=cb_: (jnp.minimum((i + 1) * hb, n_tiles * hb - 1), cb_))
    tile = pl.BlockSpec((TM, D), lambda i: (i, 0))
    full = lambda a: pl.BlockSpec(a.shape, lambda i: (0,) * a.ndim)
    return pl.pallas_call(
        body, name="merge", grid=(n_tiles,),
        in_specs=[col(CB_H), col(CB_BG), col(CB_CG), col(CB_ZA), col(CB_ZB), col(CB_GA), col(CB_GB),
                  prev(CB_H), nxt(CB_H), prev(CB_CG), nxt(CB_CG),
                  tile, tile, tile, tile, full(w_a), full(w_b), full(w_out), full(vecs)],
        out_specs=[tile, pl.BlockSpec((TM, 5 * D), lambda i: (i, 0)), tile, tile,
                   pl.BlockSpec((3, D, TM), lambda i: (0, 0, i)), pl.BlockSpec((3, TM, D), lambda i: (0, i, 0)),
                   pl.BlockSpec((8, D), lambda i: (0, 0))],
        out_shape=[jax.ShapeDtypeStruct((seq, D), F32), jax.ShapeDtypeStruct((seq, 5 * D), BF16),
                   jax.ShapeDtypeStruct((seq, D), BF16), jax.ShapeDtypeStruct((seq, D), BF16),
                   jax.ShapeDtypeStruct((3, D, seq), BF16), jax.ShapeDtypeStruct((3, seq, D), BF16),
                   jax.ShapeDtypeStruct((8, D), F32)],
        compiler_params=_cparams(dimension_semantics=("arbitrary",)),
    )(p_ext, p_ext, p_ext, p_ext, p_ext, p_ext, p_ext, p_ext, p_ext, p_ext, p_ext,
      rn, rstd, x, target, w_a, w_b, w_out, vecs)


def _conv_backward(p_ext, dconv, dpa, dq, dk, dv, vecs, seq):
    n_tiles = seq // TM
    hb = TM // HALO
    qk = N_HEADS * DK

    def body(h_ref, cg_ref, dc_ref, dcp_ref, dcn_ref, dpa_ref, dq_ref, dk_ref, dv_ref, vec_ref, dp_ref, part_ref):
        i = pl.program_id(0)
        dp_ref[:, CB_QK * D:CB_QK * D + qk] = dq_ref[...]
        dp_ref[:, CB_QK * D + qk:CB_V * D] = dk_ref[...]
        dp_ref[:, CB_V * D:CB_ZB * D] = dv_ref[...]

        @pl.when(i == 0)
        def _():
            part_ref[...] = jnp.zeros((8, D), F32)

        @pl.when(i == n_tiles)
        def _():
            dp_ref[:, 0:CB_QK * D] = jnp.zeros((TM, CB_QK * D), BF16)
            dp_ref[:, CB_ZB * D:PW] = jnp.zeros((TM, PW - CB_ZB * D), BF16)

        @pl.when(i < n_tiles)
        def _():
            f = lambda ref: ref[...].astype(F32)
            h, cg, dc = f(h_ref), f(cg_ref), f(dc_ref)
            w0, w1, w2 = vec_ref[1:2, :], vec_ref[2:3, :], vec_ref[3:4, :]
            row = lax.broadcasted_iota(jnp.int32, (TM, 1), 0)
            dc_prev = jnp.where(i == 0, 0.0, f(dcp_ref)[HALO - 1:HALO, :])
            dc_next = jnp.where(i == n_tiles - 1, 0.0, f(dcn_ref)[0:1, :])
            dc_up = jnp.where(row == 0, dc_prev, pltpu.roll(dc, 1, 0))
            dc_dn = jnp.where(row == TM - 1, dc_next, pltpu.roll(dc, TM - 1, 0))
            du = w0 * dc_dn + w1 * dc + w2 * dc_up
            u = cg * h
            dp_ref[:, CB_H * D:(CB_H + 1) * D] = (du * cg).astype(BF16)
            dp_ref[:, CB_CG * D:(CB_CG + 1) * D] = (du * h).astype(BF16)
            dp_ref[:, CB_BG * D:(CB_BG + 1) * D] = dpa_ref[:, 0:D]
            dp_ref[:, CB_ZA * D:(CB_ZA + 1) * D] = dpa_ref[:, D:2 * D]
            dp_ref[:, CB_ZB * D:(CB_GB + 1) * D] = dpa_ref[:, 2 * D:5 * D]
            part_ref[0:1, :] += jnp.sum(u * dc_dn, axis=0, keepdims=True)
            part_ref[1:2, :] += jnp.sum(u * dc, axis=0, keepdims=True)
            part_ref[2:3, :] += jnp.sum(u * dc_up, axis=0, keepdims=True)
            part_ref[3:4, :] += jnp.sum(dc, axis=0, keepdims=True)

    last = n_tiles - 1
    col = lambda cb_: pl.BlockSpec((TM, D), lambda i, cb_=cb_: (jnp.minimum(i, last), cb_))
    lat = lambda w: pl.BlockSpec((TM, w), lambda i: (jnp.minimum(i, last), 0))
    ext = lambda w: pl.BlockSpec((TM, w), lambda i: (i, 0))
    return pl.pallas_call(
        body, name="conv_backward", grid=(n_tiles + 1,),
        in_specs=[col(CB_H), col(CB_CG), lat(D),
                  pl.BlockSpec((HALO, D), lambda i: (jnp.clip(i * hb - 1, 0, n_tiles * hb - 1), 0)),
                  pl.BlockSpec((HALO, D), lambda i: (jnp.minimum((i + 1) * hb, n_tiles * hb - 1), 0)),
                  lat(5 * D), ext(qk), ext(qk), ext(D),
                  pl.BlockSpec(vecs.shape, lambda i: (0, 0))],
        out_specs=[ext(PW), pl.BlockSpec((8, D), lambda i: (0, 0))],
        out_shape=[jax.ShapeDtypeStruct((seq + TM, PW), BF16), jax.ShapeDtypeStruct((8, D), F32)],
        compiler_params=_cparams(dimension_semantics=("arbitrary",)),
    )(p_ext, p_ext, dconv, dconv, dconv, dpa, dq, dk, dv, vecs)


def _input_backward(dp, shards, ids, x, ctx, dx1, modx, modc, norm_w):
    seq = x.shape[0]
    lext = seq + ctx.shape[0]
    n_x = seq // TM
    n_w = len(shards.arrays)

    def body(ids_ref, dp_ref, x_ref, ctx_ref, dx1_ref, mx_ref, mc_ref, nw_ref, *rest):
        w_hbm, (gx_ref, part_ref, w_ref, w_sems) = rest[:n_w], rest[n_w:]
        i = pl.program_id(0)
        _load_w_in(i, ids_ref, shards, w_hbm, w_ref, w_sems)
        is_ctx = i >= n_x
        dxm = _dot_nt(dp_ref[...], w_ref[...])
        x = jnp.where(is_ctx, ctx_ref[...], x_ref[...])
        r = lax.rsqrt(jnp.mean(x * x, axis=-1, keepdims=True) + EPS)
        xh = x * r
        nw = nw_ref[...]
        sc = jnp.where(is_ctx, mc_ref[1:2, :], mx_ref[1:2, :])
        dxn = dxm * (1.0 + sc)
        dxh = dxn * nw
        dx = r * (dxh - xh * jnp.mean(dxh * xh, axis=-1, keepdims=True))

        @pl.when(jnp.logical_not(is_ctx))
        def _():
            gx_ref[...] = dx1_ref[...] + dx

        @pl.when(i == 0)
        def _():
            part_ref[...] = jnp.zeros((8, D), F32)

        fx = jnp.where(is_ctx, 0.0, 1.0)
        d_shift = jnp.sum(dxm, axis=0, keepdims=True)
        d_scale = jnp.sum(dxm * (xh * nw), axis=0, keepdims=True)
        part_ref[0:1, :] += fx * d_shift
        part_ref[1:2, :] += fx * d_scale
        part_ref[2:3, :] += jnp.sum(dxn * xh, axis=0, keepdims=True)
        part_ref[3:4, :] += (1.0 - fx) * d_shift
        part_ref[4:5, :] += (1.0 - fx) * d_scale

    lat = lambda w: pl.BlockSpec((TM, w), lambda i, ids_: (jnp.minimum(i, n_x - 1), 0))
    ext = lambda w: pl.BlockSpec((TM, w), lambda i, ids_: (i, 0))
    full = lambda a: pl.BlockSpec(a.shape, lambda i, ids_: (0,) * a.ndim)
    grid_spec = pltpu.PrefetchScalarGridSpec(
        num_scalar_prefetch=1, grid=(lext // TM,),
        in_specs=[ext(PW), lat(D), full(ctx), lat(D), full(modx), full(modc), full(norm_w)]
        + [pl.BlockSpec(memory_space=pl.ANY)] * n_w,
        out_specs=[lat(D), pl.BlockSpec((8, D), lambda i, ids_: (0, 0))],
        scratch_shapes=[pltpu.VMEM((D, PW), BF16), pltpu.SemaphoreType.DMA((N_DEV,))])
    return pl.pallas_call(
        body, name="input_backward", grid_spec=grid_spec,
        out_shape=[jax.ShapeDtypeStruct((seq, D), F32), jax.ShapeDtypeStruct((8, D), F32)],
        compiler_params=_cparams(dimension_semantics=("arbitrary",)),
    )(ids, dp, x, ctx, dx1, modx, modc, norm_w, *shards.arrays)


def _weight_grad_in(xm_t, dp, owners, name):
    lext = xm_t.shape[1]

    def body(own_ref, a_ref, b_ref, o_ref):
        o_ref[0] = _dot(a_ref[...], b_ref[...])

    grid_spec = pltpu.PrefetchScalarGridSpec(
        num_scalar_prefetch=1, grid=(4,),
        in_specs=[pl.BlockSpec((D, lext), lambda j, own: (0, 0)),
                  pl.BlockSpec((lext, WSH), lambda j, own: (0, own[j]))],
        out_specs=pl.BlockSpec((1, D, WSH), lambda j, own: (j, 0, 0)))
    return pl.pallas_call(
        body, name=name, grid_spec=grid_spec,
        out_shape=jax.ShapeDtypeStruct((4, D, WSH), F32),
        compiler_params=_cparams(dimension_semantics=("arbitrary",)),
    )(owners, xm_t, dp)


def _weight_grad_square(a_t, b):
    seq = a_t.shape[2]

    def body(a_ref, b_ref, o_ref):
        o_ref[:, 0] = _dot(a_ref[0], b_ref[0]).reshape(N_DEV, RSH, D)

    return pl.pallas_call(
        body, name="weight_grad_square", grid=(3,),
        in_specs=[pl.BlockSpec((1, D, seq), lambda t: (t, 0, 0)), pl.BlockSpec((1, seq, D), lambda t: (t, 0, 0))],
        out_specs=pl.BlockSpec((N_DEV, 1, RSH, D), lambda t: (0, t, 0, 0)),
        out_shape=jax.ShapeDtypeStruct((N_DEV, 3, RSH, D), F32),
        compiler_params=_cparams(dimension_semantics=("arbitrary",)),
    )(a_t, b)


def _rope_tables(seq, ctx_len, token):
    rows = seq // GRID_W
    row = jnp.repeat(jnp.arange(rows), GRID_W).astype(F32)
    col = jnp.tile(jnp.arange(GRID_W), rows).astype(F32)
    nf = DK // 4
    inv = ROPE_BASE ** (-jnp.arange(nf, dtype=F32) / nf)
    ang = _tie(jnp.concatenate([row[:, None] * inv, col[:, None] * inv], axis=-1), token)
    cos, sin = jnp.cos(ang), jnp.sin(ang)
    cos_t = jnp.tile(cos, (1, 4))
    sin_t = jnp.tile(jnp.concatenate([-sin, sin], axis=-1), (1, 2))
    cos_t = jnp.concatenate([cos_t, jnp.ones((ctx_len, LANES), F32)], axis=0)
    sin_t = jnp.concatenate([sin_t, jnp.zeros((ctx_len, LANES), F32)], axis=0)
    return cos_t, sin_t


def _tie(value, token):
    return value + token[0, 0].astype(value.dtype)


def _pad_lanes(a):
    return jnp.pad(a, ((0, 0), (0, D - a.shape[1])))


def _rows(total, *parts):
    width = max(a.shape[1] for _, a in parts)
    out = None
    for row, a in parts:
        padded = jnp.pad(a, ((row, total - row - a.shape[0]), (0, width - a.shape[1])))
        out = padded if out is None else out + padded
    return out


def kernel(x, c, ctx, c_ctx, norm_w, ada_w, ada_b, w_in, conv_w, conv_b, decay_logit, gn_w, w_a, w_b, w_out, final_norm_w, loss_target, m_c_ctx, m_norm_w, m_ada_w, m_ada_b, m_w_in, m_conv_w, m_conv_b, m_decay_logit, m_gn_w, m_w_a, m_w_b, m_w_out, m_final_norm_w, v_c_ctx, v_norm_w, v_ada_w, v_ada_b, v_w_in, v_conv_w, v_conv_b, v_decay_logit, v_gn_w, v_w_a, v_w_b, v_w_out, v_final_norm_w):
    xi, yi, ci = _coords()
    me = 4 * xi + 2 * yi + ci
    chip = 2 * xi + yi
    seq, ctx_len = x.shape[1], ctx.shape[1]
    assert seq % TM == 0 and seq % RET_C == 0 and ctx_len == TM and seq % GRID_W == 0
    csh = D // N_DEV

    blk = jnp.pad(c, ((0, 7), (0, 0))) + jnp.pad(conv_w[0], ((1, 4), (0, D - csh)))
    got, _ = _all_gather_small(blk, "gather_cond")
    conv_w_all = got[:, 1:4, 0:csh].transpose(1, 0, 2).reshape(3, D)
    c16 = _rows(16, (0, got[:, 0, :]), (N_DEV, c_ctx[None]))
    ada_b_sh = lax.dynamic_slice(ada_b, (0, me * ADA_SH), (1, ADA_SH))
    mod_sh, act16, lg = _modulation(c16, ada_w[0], ada_b_sh, decay_logit[0])
    mod_all, small_done = _all_gather_small(mod_sh, "gather_mod")
    mod_all = mod_all.transpose(1, 0, 2).reshape(16, 3 * D)
    modx = lax.dynamic_slice(mod_all, (me, 0), (1, 3 * D)).reshape(3, D)
    modc = mod_all[8].reshape(3, D)

    x2, tgt = x[0], loss_target[0]
    wblock = lambda n: jax.ShapeDtypeStruct((n, D, WSH), BF16)
    ids_of = lambda ks: jnp.stack([jnp.bitwise_xor(me, k) for k in ks]).astype(jnp.int32)
    own_shard = w_in[0].astype(BF16)[None]
    sq_shards = [w[0].astype(BF16) for w in (w_a, w_b, w_out)]
    wave1 = _Exchange(
        "gather_in_first", [jnp.broadcast_to(own_shard, (2, D, WSH)), own_shard], [wblock(4)], 3,
        lambda srcs, lands: _forward_plan([(0, 1)])(srcs, lands) + _send_plan((4, 2))(srcs[1:], lands),
        after=small_done)
    cos_t, sin_t = _rope_tables(seq, ctx_len, wave1.token)
    modx = _tie(modx, wave1.token)
    xm, xm_t = _prenorm(x2, ctx[0], modx, modc, norm_w)
    (w_pair, _), _ = wave1.wait(xm_t, 1)
    first = _Shards([w_pair], [(0, 0), (0, 1)])
    p_ext = _in_projection(None, xm, cos_t, sin_t, first, ids_of((0, 1)), "in_projection_pair")
    _, (w_nbr,) = wave1.wait(p_ext)
    wave2 = _Exchange(
        "gather_in_second", [w_nbr, own_shard], [wblock(2)], 3,
        lambda srcs, lands: _forward_plan([(0, 2), (1, 3)])(srcs, lands) + _send_plan((6,))(srcs[1:], lands))
    (w_nbr, _), _ = wave2.wait(wave2.token, 2)
    second = _Shards([w_nbr], [(0, 0), (0, 1), (0, 2), (0, 3)])
    p_ext = _in_projection(p_ext, xm, cos_t, sin_t, second, ids_of((4, 2, 5, 3)), "in_projection_neighbours")
    _, (w_diag, ) = wave2.wait(p_ext)
    wave3 = _Exchange(
        "gather_in_third", [w_diag] + sq_shards, [jax.ShapeDtypeStruct((N_DEV, RSH, D), BF16)] * 3, 1 + 3 * (N_DEV - 1),
        lambda srcs, lands: _forward_plan([(0, 1)])(srcs, lands) + _gather_plan(srcs[1:], lands))
    (w_diag, *_), _ = wave3.wait(wave3.token, 1)
    third = _Shards([w_diag], [(0, 0), (0, 1)])
    p_ext = _in_projection(p_ext, xm, cos_t, sin_t, third, ids_of((6, 7)), "in_projection_diagonal")
    w_shards, w_ids = first + second + third, ids_of((0, 1, 4, 2, 5, 3, 6, 7))

    rn, rstd = _retention_forward(p_ext, lg, seq, ctx_len)
    _, sq_lands = wave3.wait(rstd)
    w_a_all, w_b_all, w_out_all = (
        lax.dynamic_update_slice(land, shard[None], (me, 0, 0)).reshape(D, D) for land, shard in zip(sq_lands, sq_shards))
    vecs = _rows(8, (0, modx[2:3]), (1, conv_w_all), (4, conv_b), (5, gn_w), (6, final_norm_w[None]))
    dx1, dpa, dconv, dret, op_at, op_b, part_m = _merge(p_ext, rn, rstd, x2, tgt, w_a_all, w_b_all, w_out_all, vecs, seq)

    j4 = jnp.arange(4, dtype=jnp.int32)
    owners = (2 * jnp.bitwise_xor(chip, j4) + ci).astype(jnp.int32)
    owners_sib = (2 * jnp.bitwise_xor(chip, j4) + (1 - ci)).astype(jnp.int32)
    gw_sq = _weight_grad_square(op_at, op_b).reshape(N_DEV, 3 * RSH, D)
    rs_sq_pair = _Exchange("rs_square_pair", [gw_sq], [jax.ShapeDtypeStruct((4, 3 * RSH, D), F32)], 4,
                           _pair_plan(lambda j, chip_, c_: 2 * jnp.bitwise_xor(chip_, j) + (1 - c_)))
    dq, dk, dv, dlg = _retention_backward(p_ext, dret, _tie(lg, rs_sq_pair.token), cos_t, sin_t, seq, ctx_len)
    (gw_sq,), (r1_sq,) = rs_sq_pair.wait(dlg)
    own_sq, send_sq = _pair_sum(gw_sq, r1_sq, owners, "pair_sum_square")
    rs_sq_chips = _Exchange("rs_square_chips", [send_sq], [jax.ShapeDtypeStruct((3, 3 * RSH, D), BF16)], 3, _chips_plan)
    dp, part_c = _conv_backward(p_ext, dconv, dpa, dq, dk, dv, _tie(vecs, rs_sq_chips.token), seq)
    gw_sib = _weight_grad_in(xm_t, dp, owners_sib, "weight_grad_in_sibling")
    rs_in_pair = _Exchange("rs_in_pair", [gw_sib], [jax.ShapeDtypeStruct((4, D, WSH), F32)], 4,
                           _pair_plan(lambda j, chip_, c_: j))
    gw_own = _weight_grad_in(xm_t, dp, _tie(owners, rs_in_pair.token), "weight_grad_in_own")
    _, (r1_in,) = rs_in_pair.wait(gw_own)
    own_in, send_in = _pair_sum(gw_own, r1_in, j4, "pair_sum_in")
    rs_in_chips = _Exchange("rs_in_chips", [send_in], [jax.ShapeDtypeStruct((3, D, WSH), BF16)], 3, _chips_plan)
    grad_x, part_i = _input_backward(dp, w_shards, w_ids, x2, ctx[0], dx1, _tie(modx, rs_in_chips.token), modc, norm_w)

    dl = dlg[:, 0, 0:4]
    dlg_row = jnp.pad(dl[:, 0::2].reshape(1, N_HEADS), ((0, 0), (0, D - N_HEADS))) + jnp.pad(
        dl[:, 1::2].reshape(1, N_HEADS), ((0, 0), (N_HEADS, D - 2 * N_HEADS)))
    partials = _rows(16, (0, part_m[0:1]), (1, part_i[2:3]), (2, part_c[3:4]), (3, part_m[2:3]), (4, part_c[0:3]),
                     (7, part_i[0:2]), (9, part_m[1:2]), (10, part_i[3:5]), (13, part_m[3:4]), (14, dlg_row))
    got, _ = _all_gather_small(partials, "gather_partials")
    tot = _sum_devices(got, "sum_partials")
    dmodc = tot[10:13].reshape(1, 3 * D)
    dmod16 = _rows(16, (0, got[:, 7:10, :].reshape(N_DEV, 3 * D)), (N_DEV, dmodc))
    dmod16 = lax.dynamic_slice(dmod16, (0, me * ADA_SH), (16, ADA_SH))
    dmodc8 = _rows(8, (0, dmod16[8:9]))
    g_ada_w, d_ada_w, nm_ada_w, nv_ada_w, cctx_part = _ada_backward(
        act16.T, dmod16, dmodc8, ada_w[0], m_ada_w[0], v_ada_w[0])
    dact_ctx = _sum_devices(_all_gather_small(cctx_part, "gather_cctx")[0], "sum_cctx")

    view = {"c_ctx": (1, D), "final_norm_w": (1, D), "decay_logit": (1, 2 * N_HEADS), "conv_w": (3, csh)}
    given = {"c_ctx": (c_ctx, m_c_ctx, v_c_ctx), "norm_w": (norm_w, m_norm_w, v_norm_w),
             "conv_b": (conv_b, m_conv_b, v_conv_b), "gn_w": (gn_w, m_gn_w, v_gn_w),
             "final_norm_w": (final_norm_w, m_final_norm_w, v_final_norm_w), "ada_b": (ada_b, m_ada_b, v_ada_b),
             "decay_logit": (decay_logit, m_decay_logit, v_decay_logit), "conv_w": (conv_w, m_conv_w, v_conv_w)}
    small_in = [tuple(a.reshape(view.get(name, a.shape)) for a in given[name]) for name in SMALL_PARAMS]
    conv_w_grad = lax.dynamic_slice(tot, (4, me * csh), (3, csh))
    small_out = _adam_small(tot, dact_ctx, conv_w_grad, small_in)
    results = {name: [o.reshape(given[name][0].shape) for o in outs_]
               for name, outs_ in zip(SMALL_PARAMS, small_out)}

    _, (r2_sq,) = rs_sq_chips.wait(small_out[0][0])
    _, (r2_in,) = rs_in_chips.wait(small_out[0][0])
    results["w_in"] = [o[None] for o in _adam_sharded(w_in[0], m_w_in[0], v_w_in[0], own_in, r2_in, "adam_w_in")]
    square = _adam_square([(w_a[0], m_w_a[0], v_w_a[0]), (w_b[0], m_w_b[0], v_w_b[0]),
                           (w_out[0], m_w_out[0], v_w_out[0])], own_sq, r2_sq)
    for name, outs_ in zip(("w_a", "w_b", "w_out"), square):
        results[name] = [o[None] for o in outs_]
    results["ada_w"] = [o[None] for o in (g_ada_w, d_ada_w, nm_ada_w, nv_ada_w)]

    order = ("c_ctx", "norm_w", "ada_w", "ada_b", "w_in", "conv_w", "conv_b", "decay_logit", "gn_w",
             "w_a", "w_b", "w_out", "final_norm_w")
    outs = [results[name][kind] for kind in range(4) for name in order]
    return (tot[13, 0], grad_x[None], *outs)
```

```python
import functools
import math

import jax
import jax.numpy as jnp
from jax import lax
from jax.experimental import pallas as pl
from jax.experimental.pallas import tpu as pltpu

F32 = jnp.float32
BF16 = jnp.bfloat16
MESH = pl.DeviceIdType.MESH
_HBM_SPEC = pl.BlockSpec(memory_space=pltpu.HBM)
_SEM_SPEC = pl.BlockSpec(memory_space=pltpu.SEMAPHORE)
_EFFECT = pltpu.SideEffectType.DATAFLOW_SIDE_EFFECTING

N_DEV = 8
D = 1024
N_HEADS = 8
DK = 64
DV = 128
GRID_W = 64
ROPE_BASE = 10000.0
EPS = 1e-6
PW = 9 * D
WSH = PW // N_DEV
RSH = D // N_DEV
ADA_SH = 3 * D // N_DEV
TM = 256
RET_C = 256
HALO = 16
LANES = 128
VMEM_LIMIT = 60 * 1024 * 1024

ADAM_LR = 0.001
ADAM_B1 = 0.9
ADAM_B2 = 0.999
ADAM_EPS = 1e-08
ADAM_WD = 0.01
ADAM_STEP = 10

CB_H, CB_BG, CB_CG, CB_ZA, CB_QK, CB_V, CB_ZB, CB_GA, CB_GB = range(9)


def _cparams(**kw):
    return pltpu.CompilerParams(vmem_limit_bytes=VMEM_LIMIT, **kw)


def _dot(a, b):
    return jnp.dot(a, b, preferred_element_type=F32)


def _dot_nt(a, b):
    return lax.dot_general(a, b, (((1,), (1,)), ((), ())), preferred_element_type=F32)


def _dot_tn(a, b):
    return lax.dot_general(a, b, (((0,), (0,)), ((), ())), preferred_element_type=F32)


def _sigmoid(z):
    return 0.5 * jnp.tanh(0.5 * z) + 0.5


def _row_tile(rows):
    return TM if rows % TM == 0 else rows


def _coords():
    return lax.axis_index("x"), lax.axis_index("y"), lax.axis_index("c")


def _flip(v, bit):
    return 1 - v if bit else v


def _all_gather_small(blk, name):
    rows, cols = blk.shape

    def body(x_ref, out_ref, done, send_sems, recv_sems, local_sem):
        x, y, c = _coords()
        me = 4 * x + 2 * y + c
        mine = pltpu.make_async_copy(x_ref, out_ref.at[me], local_sem)
        mine.start()

        def copy(k, slot):
            peer = (_flip(x, k & 4), _flip(y, k & 2), _flip(c, k & 1))
            return pltpu.make_async_remote_copy(
                src_ref=x_ref, dst_ref=out_ref.at[slot], send_sem=send_sems.at[k - 1],
                recv_sem=recv_sems.at[k - 1], device_id=peer, device_id_type=MESH)

        for k in range(1, N_DEV):
            copy(k, me).start()
        for k in range(1, N_DEV):
            copy(k, jnp.bitwise_xor(me, k)).wait_recv()
        for k in range(1, N_DEV):
            copy(k, me).wait_send()
        mine.wait()
        done[...] = jnp.zeros((8, LANES), F32)

    vmem = pl.BlockSpec(memory_space=pltpu.VMEM)
    return pl.pallas_call(
        body, name=name,
        out_shape=[jax.ShapeDtypeStruct((N_DEV, rows, cols), blk.dtype), jax.ShapeDtypeStruct((8, LANES), F32)],
        in_specs=[vmem], out_specs=[vmem, vmem],
        scratch_shapes=[pltpu.SemaphoreType.DMA((N_DEV - 1,)), pltpu.SemaphoreType.DMA((N_DEV - 1,)),
                        pltpu.SemaphoreType.DMA],
    )(blk)


class _Absent:
    at = property(lambda self: self)

    def __getitem__(self, index):
        return self


class _Exchange:
    def __init__(self, name, srcs, land_shapes, n_copies, plan, after=None):
        self.name, self.plan, self.n_copies = name, plan, n_copies
        self.n_src, self.n_land = len(srcs), len(land_shapes)
        hbm = lambda a: pltpu.HBM(a.shape, a.dtype)
        n = self.n_src + self.n_land
        lands = [pltpu.with_memory_space_constraint(lax.empty(s.shape, s.dtype), pltpu.HBM) for s in land_shapes]
        srcs = [pltpu.with_memory_space_constraint(s, pltpu.HBM) for s in srcs]
        extra = [] if after is None else [after]

        def body(*refs):
            send_sems, recv_sems = refs[n + len(extra)], refs[n + len(extra) + 1]
            for cp in self._copies(refs, send_sems, recv_sems):
                cp.start()
            refs[-1][...] = jnp.zeros((8, LANES), F32)

        outs = pl.pallas_call(
            body, name=name + "_start",
            out_shape=(pltpu.SemaphoreType.DMA((n_copies,)), pltpu.SemaphoreType.DMA((n_copies,)),
                       *[hbm(a) for a in srcs], *[hbm(a) for a in lands], jax.ShapeDtypeStruct((8, LANES), F32)),
            in_specs=[_HBM_SPEC] * n + [pl.BlockSpec(memory_space=pl.ANY)] * len(extra),
            out_specs=(_SEM_SPEC, _SEM_SPEC, *[_HBM_SPEC] * n, pl.BlockSpec(memory_space=pltpu.VMEM)),
            input_output_aliases={i: 2 + i for i in range(n)},
            compiler_params=pltpu.CompilerParams(has_side_effects=_EFFECT),
        )(*srcs, *lands, *extra)
        self.send_sems, self.recv_sems = outs[0], outs[1]
        self.buffers = list(outs[2:2 + n])
        self.token = outs[-1]
        self.waited = 0

    def _copies(self, refs, send_sems, recv_sems, lo=0, hi=None):
        src_refs, land_refs = refs[:self.n_src], refs[self.n_src:self.n_src + self.n_land]
        planned = list(enumerate(self.plan(src_refs, land_refs)))[lo:hi]
        return [pltpu.make_async_remote_copy(src_ref=s, dst_ref=d, send_sem=send_sems.at[k], recv_sem=recv_sems.at[k],
                                             device_id=dev, device_id_type=MESH) for k, (s, d, dev) in planned]

    def wait(self, after, count=None, through=None):
        n = self.n_src + self.n_land
        lo = self.waited
        hi = self.n_copies if count is None else lo + count
        self.waited = hi
        through = list(range(n)) if through is None else list(through)
        m = len(through)

        def body(*refs):
            held = [_Absent()] * n
            for pos, ref in zip(through, refs[:m]):
                held[pos] = ref
            for cp in self._copies(held, refs[m], refs[m + 1], lo, hi):
                cp.wait_send()
                cp.wait_recv()

        outs = pl.pallas_call(
            body, name=f"{self.name}_wait{lo}" if lo or hi < self.n_copies else self.name + "_wait",
            out_shape=tuple(pltpu.HBM(self.buffers[p].shape, self.buffers[p].dtype) for p in through),
            in_specs=[_HBM_SPEC] * m + [_SEM_SPEC, _SEM_SPEC, pl.BlockSpec(memory_space=pl.ANY)],
            out_specs=tuple([_HBM_SPEC] * m),
            input_output_aliases={i: i for i in range(m)},
            compiler_params=pltpu.CompilerParams(has_side_effects=_EFFECT),
        )(*[self.buffers[p] for p in through], self.send_sems, self.recv_sems, after)
        for pos, out in zip(through, outs):
            self.buffers[pos] = out
        return list(self.buffers[:self.n_src]), list(self.buffers[self.n_src:])


def _pair_plan(src_index):
    def plan(srcs, lands):
        x, y, c = _coords()
        chip = 2 * x + y
        return [(srcs[0].at[src_index(j, chip, c)], lands[0].at[j], (x, y, 1 - c)) for j in range(4)]
    return plan


def _chips_plan(srcs, lands):
    x, y, c = _coords()
    return [(srcs[0].at[j - 1], lands[0].at[j - 1], (_flip(x, j & 2), _flip(y, j & 1), c)) for j in range(1, 4)]


def _peer(k):
    x, y, c = _coords()
    return _flip(x, k & 4), _flip(y, k & 2), _flip(c, k & 1)


def _send_plan(ks):
    def plan(srcs, lands):
        return [(srcs[0].at[0], lands[0].at[p], _peer(k)) for p, k in enumerate(ks)]
    return plan


def _forward_plan(moves):
    def plan(srcs, lands):
        return [(srcs[0].at[s], srcs[0].at[d], _peer(1)) for s, d in moves]
    return plan


def _gather_plan(srcs, lands):
    x, y, c = _coords()
    me = 4 * x + 2 * y + c
    return [(srcs[a], lands[a].at[me], (_flip(x, k & 4), _flip(y, k & 2), _flip(c, k & 1)))
            for a in range(len(srcs)) for k in range(1, N_DEV)]


def _pair_sum(grad, recv, owners, name):
    _, rows, cols = grad.shape
    tr = _row_tile(rows)

    def body(own_ref, g_ref, r_ref, mine_ref, send_ref):
        j = pl.program_id(1)
        total = g_ref[...] + r_ref[...]

        @pl.when(j == 0)
        def _():
            mine_ref[...] = total

        @pl.when(j > 0)
        def _():
            send_ref[...] = total.astype(BF16)

    grid_spec = pltpu.PrefetchScalarGridSpec(
        num_scalar_prefetch=1, grid=(rows // tr, 4),
        in_specs=[pl.BlockSpec((1, tr, cols), lambda i, j, own: (own[j], i, 0)),
                  pl.BlockSpec((1, tr, cols), lambda i, j, own: (j, i, 0))],
        out_specs=[pl.BlockSpec((1, tr, cols), lambda i, j, own: (0, i, 0)),
                   pl.BlockSpec((1, tr, cols), lambda i, j, own: (jnp.maximum(j - 1, 0), i, 0))])
    return pl.pallas_call(
        body, name=name, grid_spec=grid_spec,
        out_shape=[jax.ShapeDtypeStruct((1, rows, cols), F32), jax.ShapeDtypeStruct((3, rows, cols), BF16)],
        compiler_params=_cparams(dimension_semantics=("arbitrary", "arbitrary")),
    )(owners, grad, recv)


def _modulation(c16, ada_w_sh, ada_b_sh, decay_logit):
    def body(c_ref, w_ref, b_ref, dl_ref, mod_ref, act_ref, lg_ref):
        cv = c_ref[...]
        act = cv * _sigmoid(cv)
        act_ref[...] = act
        mod_ref[...] = jnp.dot(act, w_ref[...], preferred_element_type=F32,
                               precision=lax.Precision.HIGHEST) + b_ref[...]
        z = dl_ref[...]
        lg_ref[...] = jnp.minimum(z, 0.0) - jnp.log(1.0 + jnp.exp(-jnp.abs(z)))

    return pl.pallas_call(
        body, name="modulation",
        out_shape=[jax.ShapeDtypeStruct((16, ADA_SH), F32), jax.ShapeDtypeStruct((16, D), F32),
                   jax.ShapeDtypeStruct(decay_logit.shape, F32)],
        compiler_params=_cparams(),
    )(c16, ada_w_sh, ada_b_sh, decay_logit)


def _adam_update(w, g, m, v):
    m2 = ADAM_B1 * m + (1.0 - ADAM_B1) * g
    v2 = ADAM_B2 * v + (1.0 - ADAM_B2) * (g * g)
    m_hat = m2 / (1.0 - ADAM_B1 ** ADAM_STEP)
    v_hat = v2 / (1.0 - ADAM_B2 ** ADAM_STEP)
    delta = -ADAM_LR * (m_hat / (jnp.sqrt(v_hat) + ADAM_EPS) + ADAM_WD * w)
    return delta, m2, v2


def _adam_sharded(w, m, v, own, recv, name):
    rows, cols = w.shape
    tr = _row_tile(rows)

    def body(w_ref, m_ref, v_ref, g0, g1, g2, g3, g_ref, d_ref, m_out, v_out):
        g = ((g0[0] + g1[0].astype(F32)) + g2[0].astype(F32)) + g3[0].astype(F32)
        delta, m2, v2 = _adam_update(w_ref[...], g, m_ref[...], v_ref[...])
        g_ref[...] = g
        d_ref[...] = delta
        m_out[...] = m2
        v_out[...] = v2

    flat = pl.BlockSpec((tr, cols), lambda i: (i, 0))
    part = lambda j: pl.BlockSpec((1, tr, cols), lambda i, j=j: (j, i, 0))
    return pl.pallas_call(
        body, name=name, grid=(rows // tr,),
        in_specs=[flat, flat, flat, part(0), part(0), part(1), part(2)],
        out_specs=[flat] * 4,
        out_shape=[jax.ShapeDtypeStruct((rows, cols), F32)] * 4,
        compiler_params=_cparams(dimension_semantics=("arbitrary",)),
    )(w, m, v, own, recv, recv, recv)


def _ada_backward(act_t, dmod16, dmodc8, ada_w_sh, m, v):
    def body(at_ref, dm_ref, dc_ref, w_ref, m_ref, v_ref, g_ref, d_ref, m_out, v_out, pc_ref):
        g = jnp.dot(at_ref[...], dm_ref[...], preferred_element_type=F32, precision=lax.Precision.HIGHEST)
        w = w_ref[...]
        delta, m2, v2 = _adam_update(w, g, m_ref[...], v_ref[...])
        g_ref[...] = g
        d_ref[...] = delta
        m_out[...] = m2
        v_out[...] = v2
        pc_ref[...] = lax.dot_general(dc_ref[...], w, (((1,), (1,)), ((), ())), preferred_element_type=F32,
                                      precision=lax.Precision.HIGHEST)

    return pl.pallas_call(
        body, name="ada_backward",
        out_shape=[jax.ShapeDtypeStruct((D, ADA_SH), F32)] * 4 + [jax.ShapeDtypeStruct((8, D), F32)],
        compiler_params=_cparams(),
    )(act_t, dmod16, dmodc8, ada_w_sh, m, v)


def _sum_devices(gathered, name):
    _, rows, cols = gathered.shape

    def body(g_ref, o_ref):
        acc = g_ref[0]
        for d in range(1, N_DEV):
            acc = acc + g_ref[d]
        o_ref[...] = acc

    return pl.pallas_call(body, name=name, out_shape=jax.ShapeDtypeStruct((rows, cols), F32),
                          compiler_params=_cparams())(gathered)


SMALL_PARAMS = ("c_ctx", "norm_w", "conv_b", "gn_w", "final_norm_w", "ada_b", "decay_logit", "conv_w")


def _adam_small(tot, dact_ctx, conv_w_grad, params):
    n = len(SMALL_PARAMS)

    def body(tot_ref, dact_ref, cwg_ref, *refs):
        ins, outs = refs[:3 * n], refs[3 * n:]
        nh = 2 * N_HEADS
        raw = {
            "c_ctx": dact_ref[0:1, :],
            "norm_w": tot_ref[1:2, :], "conv_b": tot_ref[2:3, :], "gn_w": tot_ref[3:4, :],
            "final_norm_w": tot_ref[0:1, :],
            "ada_b": jnp.concatenate([tot_ref[7 + r:8 + r, :] + tot_ref[10 + r:11 + r, :] for r in range(3)], axis=1),
            "decay_logit": tot_ref[14:15, 0:nh],
            "conv_w": cwg_ref[...],
        }
        for k, name in enumerate(SMALL_PARAMS):
            w, m, v = ins[3 * k][...], ins[3 * k + 1][...], ins[3 * k + 2][...]
            g = raw[name]
            if name == "c_ctx":
                s = _sigmoid(w)
                g = g * (s * (1.0 + w * (1.0 - s)))
            elif name == "decay_logit":
                g = g * (1.0 - _sigmoid(w))
            delta, m2, v2 = _adam_update(w, g, m, v)
            for ref, val in zip(outs[4 * k:4 * k + 4], (g, delta, m2, v2)):
                ref[...] = val

    flat = [a for trio in params for a in trio]
    out_shape = [jax.ShapeDtypeStruct(trio[0].shape, F32) for trio in params for _ in range(4)]
    outs = pl.pallas_call(body, name="adam_small", out_shape=out_shape, compiler_params=_cparams())(
        tot, dact_ctx, conv_w_grad, *flat)
    return [outs[4 * k:4 * k + 4] for k in range(n)]


def _adam_square(params, own, recv):
    def body(own_ref, recv_ref, *refs):
        ins, outs = refs[:9], refs[9:]
        for k in range(3):
            rows = slice(k * RSH, (k + 1) * RSH)
            g = own_ref[0, rows, :]
            for j in range(3):
                g = g + recv_ref[j, rows, :].astype(F32)
            delta, m2, v2 = _adam_update(ins[3 * k][...], g, ins[3 * k + 1][...], ins[3 * k + 2][...])
            for ref, val in zip(outs[4 * k:4 * k + 4], (g, delta, m2, v2)):
                ref[...] = val

    flat = [a for trio in params for a in trio]
    outs = pl.pallas_call(body, name="adam_square", out_shape=[jax.ShapeDtypeStruct((RSH, D), F32)] * 12,
                          compiler_params=_cparams())(own, recv, *flat)
    return [outs[4 * k:4 * k + 4] for k in range(3)]


def _rope(t, cos, sin):
    lane = lax.broadcasted_iota(jnp.int32, (1, LANES), 1)
    first_half = jnp.bitwise_and(lane, DK // 2) == 0
    partner = jnp.where(first_half, pltpu.roll(t, LANES - DK // 2, 1), pltpu.roll(t, DK // 2, 1))
    return t * cos + partner * sin


class _Shards:
    def __init__(self, arrays, picks):
        self.arrays, self.picks = list(arrays), list(picks)

    def specs(self, index_args):
        out = []
        for a, arr in enumerate(self.arrays):
            steps = [p for p, (ai, _) in enumerate(self.picks) if ai == a]
            first, slot0, last = steps[0], self.picks[steps[0]][1], arr.shape[0] - 1
            out.append(pl.BlockSpec(
                (1, D, WSH), lambda *args, first=first, slot0=slot0, last=last:
                (jnp.clip(index_args(*args) - first + slot0, 0, last), 0, 0)))
        return out

    def __add__(self, other):
        shift = len(self.arrays)
        return _Shards(self.arrays + other.arrays, self.picks + [(a + shift, s) for a, s in other.picks])


def _load_w_in(step, ids_ref, shards, w_refs, w_vmem, sems):
    @pl.when(step == 0)
    def _():
        copies = []
        for p, (a, slot) in enumerate(shards.picks):
            col = pl.multiple_of(ids_ref[p] * WSH, LANES)
            copies.append(pltpu.make_async_copy(w_refs[a].at[slot], w_vmem.at[:, pl.ds(col, WSH)], sems.at[p]))
        for cp in copies:
            cp.start()
        for cp in copies:
            cp.wait()


def _project_columns(xmb, w_ref, shard, cos, sin, p_ref):
    groups = WSH // LANES
    q0, k0, v0 = CB_QK * D // LANES, (CB_QK * D + N_HEADS * DK) // LANES, CB_V * D // LANES
    mxu_cols = 2 * LANES
    for a in range(0, WSH, mxu_cols):
        b = min(a + mxu_cols, WSH)
        acc = _dot(xmb, w_ref[0, :, a:b])
        for col in range(0, b - a, LANES):
            grp = shard * groups + (a + col) // LANES
            rotary = jnp.logical_and(grp >= q0, grp < v0)
            scale = jnp.where(grp >= k0, DK ** -0.5, 1.0)
            piece = acc[:, col:col + LANES]
            turned = _rope(piece * scale, cos, sin)
            p_ref[:, a + col:a + col + LANES] = jnp.where(rotary, turned, piece).astype(BF16)


def _prenorm(x, ctx, modx, modc, norm_w):
    n_x_tiles = x.shape[0] // TM
    lext = x.shape[0] + ctx.shape[0]

    def body(x_ref, ctx_ref, mx_ref, mc_ref, nw_ref, xm_ref, xmt_ref):
        is_ctx = pl.program_id(0) >= n_x_tiles
        x = jnp.where(is_ctx, ctx_ref[...], x_ref[...])
        r = lax.rsqrt(jnp.mean(x * x, axis=-1, keepdims=True) + EPS)
        sh = jnp.where(is_ctx, mc_ref[0:1, :], mx_ref[0:1, :])
        sc = jnp.where(is_ctx, mc_ref[1:2, :], mx_ref[1:2, :])
        xm = (x * r * nw_ref[...]) * (1.0 + sc) + sh
        xm_ref[...] = xm.astype(BF16)
        xmt_ref[...] = xm.T.astype(BF16)

    full = lambda a: pl.BlockSpec(a.shape, lambda i: (0,) * a.ndim)
    return pl.pallas_call(
        body, name="prenorm", grid=(lext // TM,),
        in_specs=[pl.BlockSpec((TM, D), lambda i: (jnp.minimum(i, n_x_tiles - 1), 0)), full(ctx),
                  full(modx), full(modc), full(norm_w)],
        out_specs=[pl.BlockSpec((TM, D), lambda i: (i, 0)), pl.BlockSpec((D, TM), lambda i: (0, i))],
        out_shape=[jax.ShapeDtypeStruct((lext, D), BF16), jax.ShapeDtypeStruct((D, lext), BF16)],
        compiler_params=_cparams(dimension_semantics=("arbitrary",)),
    )(x, ctx, modx, modc, norm_w)


def _in_projection(p_ext, xm, cos_t, sin_t, shards, ids, name):
    lext = xm.shape[0]
    n_w = len(shards.arrays)
    tr = lext // 4

    assert n_w == 1

    def body(ids_ref, xm_ref, cos_ref, sin_ref, *rest):
        w_refs, p_ref = rest[:n_w], rest[-1]
        _project_columns(xm_ref[...], w_refs[0], ids_ref[pl.program_id(0)], cos_ref[...], sin_ref[...], p_ref)

    row = lambda w: pl.BlockSpec((tr, w), lambda j, i, ids: (i, 0))
    grid_spec = pltpu.PrefetchScalarGridSpec(
        num_scalar_prefetch=1, grid=(len(shards.picks), lext // tr),
        in_specs=[row(D), row(LANES), row(LANES)] + shards.specs(lambda j, i, ids: j)
        + ([] if p_ext is None else [pl.BlockSpec(memory_space=pl.ANY)]),
        out_specs=pl.BlockSpec((tr, WSH), lambda j, i, ids: (i, ids[j])))
    return pl.pallas_call(
        body, name=name, grid_spec=grid_spec,
        out_shape=jax.ShapeDtypeStruct((lext, PW), BF16),
        input_output_aliases={} if p_ext is None else {4 + n_w: 0},
        compiler_params=_cparams(dimension_semantics=("arbitrary", "arbitrary")),
    )(ids, xm, cos_t, sin_t, *shards.arrays, *([] if p_ext is None else [p_ext]))


def _decay_tables(lgf, lgb, n):
    i = lax.broadcasted_iota(jnp.int32, (n, 1), 0).astype(F32)
    return dict(i=i, k_f=jnp.exp(lgf * (n - 1.0 - i)), k_b=jnp.exp(lgb * i),
                q_f=jnp.exp(lgf * (i + 1.0)), q_b=jnp.exp(lgb * (n - i)))


def _decay_matrix(lgf, lgb, n, transposed=False):
    ii = lax.broadcasted_iota(jnp.int32, (n, n), 0)
    jj = lax.broadcasted_iota(jnp.int32, (n, n), 1)
    diff = (jj - ii if transposed else ii - jj).astype(F32)
    low = jnp.exp(lgf * jnp.maximum(diff, 0.0))
    up = jnp.exp(lgb * jnp.maximum(-diff, 0.0))
    return jnp.where(diff > 0, low, jnp.where(diff < 0, up, 2.0)), diff


def _cat_lanes(a, b):
    return jnp.concatenate([a.astype(BF16), b.astype(BF16)], axis=1)


def _retention_forward(p_ext, lg, seq, ctx_len):
    lext = seq + ctx_len
    n_chunks = seq // RET_C
    C = RET_C

    def body(lg_ref, q_ref, k_ref, v_ref, rn_ref, rstd_ref, kv_scr, sf_scr, sb_scr):
        pair = pl.program_id(0)
        lane = lax.broadcasted_iota(jnp.int32, (1, LANES), 1)
        heads = range(2)
        hmask = [(lane // DK == hh).astype(F32) for hh in heads]
        lgf = [lg_ref[0, 2 * pair + hh] for hh in heads]
        lgb = [lg_ref[1, 2 * pair + hh] for hh in heads]
        vs = [slice(hh * DV, (hh + 1) * DV) for hh in heads]
        t = [_decay_tables(lgf[hh], lgb[hh], C) for hh in heads]
        kc_all = k_ref[seq:lext, :].astype(F32)
        s0 = []
        for hh in heads:
            tc = _decay_tables(lgf[hh], lgb[hh], ctx_len)
            kc = kc_all * hmask[hh]
            s0.append(_dot_tn(_cat_lanes(kc * tc["k_f"], kc * tc["k_b"]), v_ref[seq:lext, vs[hh]]))

        def increments(c, carry):
            rows = pl.ds(pl.multiple_of(c * C, C), C)
            k_all = k_ref[rows, :].astype(F32)
            for hh in heads:
                k = k_all * hmask[hh]
                kv_scr[hh, c] = _dot_tn(_cat_lanes(k * t[hh]["k_f"], k * t[hh]["k_b"]), v_ref[rows, vs[hh]])
            return carry

        lax.fori_loop(0, n_chunks, increments, 0, unroll=2)
        gf_c = [jnp.exp(lgf[hh] * C) for hh in heads]
        gb_c = [jnp.exp(lgb[hh] * C) for hh in heads]

        def scan_f(c, s):
            for hh in heads:
                sf_scr[hh, c] = s[hh]
            return tuple(gf_c[hh] * s[hh] + kv_scr[hh, c, 0:LANES, :] for hh in heads)

        def scan_b(n, s):
            c = n_chunks - 1 - n
            for hh in heads:
                sb_scr[hh, c] = s[hh]
            return tuple(gb_c[hh] * s[hh] + kv_scr[hh, c, LANES:2 * LANES, :] for hh in heads)

        lax.fori_loop(0, n_chunks, scan_f, tuple(s0[hh][0:LANES] for hh in heads))
        lax.fori_loop(0, n_chunks, scan_b, tuple(s0[hh][LANES:2 * LANES] for hh in heads))
        dmat = [_decay_matrix(lgf[hh], lgb[hh], C)[0] for hh in heads]

        def outputs(c, carry):
            rows = pl.ds(pl.multiple_of(c * C, C), C)
            q_all = q_ref[rows, :].astype(F32)
            k = k_ref[rows, :]
            for hh in heads:
                q = q_all * hmask[hh]
                v = v_ref[rows, vs[hh]]
                s = _dot_nt(q.astype(BF16), k)
                inner = _dot((s * dmat[hh]).astype(BF16), v)
                states = jnp.concatenate([sf_scr[hh, c], sb_scr[hh, c]], axis=0).astype(BF16)
                ret = inner + _dot(_cat_lanes(q * t[hh]["q_f"], q * t[hh]["q_b"]), states)
                mu = jnp.mean(ret, axis=-1, keepdims=True)
                cen = ret - mu
                rstd = lax.rsqrt(jnp.mean(cen * cen, axis=-1, keepdims=True) + EPS)
                rn_ref[rows, vs[hh]] = cen * rstd
                rstd_ref[rows, vs[hh]] = jnp.broadcast_to(rstd, (C, DV))
            return carry

        lax.fori_loop(0, n_chunks, outputs, 0, unroll=2)

    qk0 = CB_QK * D // LANES
    return pl.pallas_call(
        body, name="retention_forward", grid=(N_HEADS // 2,),
        in_specs=[pl.BlockSpec(memory_space=pltpu.SMEM),
                  pl.BlockSpec((lext, LANES), lambda g: (0, qk0 + g)),
                  pl.BlockSpec((lext, LANES), lambda g: (0, qk0 + N_HEADS // 2 + g)),
                  pl.BlockSpec((lext, 2 * DV), lambda g: (0, CB_V * D // (2 * DV) + g))],
        out_specs=[pl.BlockSpec((seq, 2 * DV), lambda g: (0, g))] * 2,
        out_shape=[jax.ShapeDtypeStruct((seq, D), F32)] * 2,
        scratch_shapes=[pltpu.VMEM((2, n_chunks, 2 * LANES, DV), F32), pltpu.VMEM((2, n_chunks, LANES, DV), F32),
                        pltpu.VMEM((2, n_chunks, LANES, DV), F32)],
        compiler_params=_cparams(dimension_semantics=("arbitrary",)),
    )(lg, p_ext, p_ext, p_ext)


def _retention_backward(p_ext, dret, lg, cos_t, sin_t, seq, ctx_len):
    lext = seq + ctx_len
    n_chunks = seq // RET_C
    C = RET_C

    def body(lg_ref, q_ref, k_ref, v_ref, do_ref, cos_ref, sin_ref, dq_ref, dk_ref, dv_ref, dlg_ref,
             kv_scr, g_scr, sf_scr, sb_scr, gfn_scr, gbp_scr):
        pair = pl.program_id(0)
        lane = lax.broadcasted_iota(jnp.int32, (1, LANES), 1)
        heads = range(2)
        hmask = [(lane // DK == hh).astype(F32) for hh in heads]
        lgf = [lg_ref[0, 2 * pair + hh] for hh in heads]
        lgb = [lg_ref[1, 2 * pair + hh] for hh in heads]
        vs = [slice(hh * DV, (hh + 1) * DV) for hh in heads]
        t = [_decay_tables(lgf[hh], lgb[hh], C) for hh in heads]
        tc = [_decay_tables(lgf[hh], lgb[hh], ctx_len) for hh in heads]
        kc_all = k_ref[seq:lext, :].astype(F32)
        kc = [kc_all * hmask[hh] for hh in heads]
        vc = [v_ref[seq:lext, vs[hh]] for hh in heads]
        kc_cat = [_cat_lanes(kc[hh] * tc[hh]["k_f"], kc[hh] * tc[hh]["k_b"]) for hh in heads]
        s0 = [_dot_tn(kc_cat[hh], vc[hh]) for hh in heads]

        def increments(c, carry):
            rows = pl.ds(pl.multiple_of(c * C, C), C)
            k_all = k_ref[rows, :].astype(F32)
            q_all = q_ref[rows, :].astype(F32)
            for hh in heads:
                k, q = k_all * hmask[hh], q_all * hmask[hh]
                kv_scr[hh, c] = _dot_tn(_cat_lanes(k * t[hh]["k_f"], k * t[hh]["k_b"]), v_ref[rows, vs[hh]])
                g_scr[hh, c] = _dot_tn(_cat_lanes(q * t[hh]["q_f"], q * t[hh]["q_b"]), do_ref[rows, vs[hh]])
            return carry

        lax.fori_loop(0, n_chunks, increments, 0, unroll=2)
        gf_c = [jnp.exp(lgf[hh] * C) for hh in heads]
        gb_c = [jnp.exp(lgb[hh] * C) for hh in heads]

        def scan_f(c, s):
            for hh in heads:
                sf_scr[hh, c] = s[hh]
            return tuple(gf_c[hh] * s[hh] + kv_scr[hh, c, 0:LANES, :] for hh in heads)

        def scan_b(n, s):
            c = n_chunks - 1 - n
            for hh in heads:
                sb_scr[hh, c] = s[hh]
            return tuple(gb_c[hh] * s[hh] + kv_scr[hh, c, LANES:2 * LANES, :] for hh in heads)

        def scan_gf(n, carry):
            c = n_chunks - 1 - n
            for hh in heads:
                gfn_scr[hh, c] = carry[hh]
            return tuple(g_scr[hh, c, 0:LANES, :] + gf_c[hh] * carry[hh] for hh in heads)

        def scan_gb(c, carry):
            for hh in heads:
                gbp_scr[hh, c] = carry[hh]
            return tuple(g_scr[hh, c, LANES:2 * LANES, :] + gb_c[hh] * carry[hh] for hh in heads)

        lax.fori_loop(0, n_chunks, scan_f, tuple(s0[hh][0:LANES] for hh in heads))
        lax.fori_loop(0, n_chunks, scan_b, tuple(s0[hh][LANES:2 * LANES] for hh in heads))
        zero_state = jnp.zeros((LANES, DV), F32)
        gf_first = lax.fori_loop(0, n_chunks, scan_gf, (zero_state, zero_state))
        gb_last = lax.fori_loop(0, n_chunks, scan_gb, (zero_state, zero_state))

        dmat, w_f, w_b = [], [], []
        for hh in heads:
            dm, diff = _decay_matrix(lgf[hh], lgb[hh], C)
            dmat.append(dm)
            w_f.append(jnp.where(diff > 0, diff * dm, 0.0))
            w_b.append(jnp.where(diff < 0, -diff * dm, 0.0))

        def total(a):
            rows_, width = a.shape
            part = jnp.sum(a.reshape(rows_ // 8, 8, width), axis=0)
            return part[:, 0:LANES] + part[:, LANES:2 * LANES] if width == 2 * LANES else part

        def chunk_head(hh, c, rows, q_all, k_all, dlgf, dlgb):
            th = t[hh]
            qm = q_all * hmask[hh]
            km = k_all * hmask[hh]
            qb, kb = qm.astype(BF16), km.astype(BF16)
            v = v_ref[rows, vs[hh]]
            do = do_ref[rows, vs[hh]]
            s = _dot_nt(qb, kb)
            dsd = _dot_nt(do, v)
            ds = (dsd * dmat[hh]).astype(BF16)
            dq_in = _dot(ds, kb)
            dk_in = _dot_tn(ds, qb)
            dv_in = _dot_tn((s * dmat[hh]).astype(BF16), do)
            prod = s * dsd
            dlgf = dlgf + total(prod * w_f[hh])
            dlgb = dlgb + total(prod * w_b[hh])
            sf, sb = sf_scr[hh, c], sb_scr[hh, c]
            states = jnp.concatenate([sf, sb], axis=0).astype(BF16)
            dqc = _dot_nt(do, states)
            dqf = dqc[:, 0:LANES] * th["q_f"]
            dqb = dqc[:, LANES:2 * LANES] * th["q_b"]
            dq = (dq_in + dqf + dqb) * hmask[hh]
            dlgf = dlgf + total((th["i"] + 1.0) * (qm * dqf))
            dlgb = dlgb + total((C - th["i"]) * (qm * dqb))
            gfn, gbp = gfn_scr[hh, c], gbp_scr[hh, c]
            gstates = jnp.concatenate([gfn, gbp], axis=0).astype(BF16)
            dkc = _dot_nt(v, gstates)
            dkf = dkc[:, 0:LANES] * th["k_f"]
            dkb = dkc[:, LANES:2 * LANES] * th["k_b"]
            dk = dk_in + dkf + dkb
            dlgf = dlgf + total((C - 1.0 - th["i"]) * (km * dkf)) + C * gf_c[hh] * total(gfn * sf)
            dlgb = dlgb + total(th["i"] * (km * dkb)) + C * gb_c[hh] * total(gbp * sb)
            dv = dv_in + _dot(_cat_lanes(km * th["k_f"], km * th["k_b"]), gstates)
            dv_ref[rows, vs[hh]] = dv.astype(BF16)
            return dq, dk, dlgf, dlgb

        def chunk(c, carry):
            rows = pl.ds(pl.multiple_of(c * C, C), C)
            q_all = q_ref[rows, :].astype(F32)
            k_all = k_ref[rows, :].astype(F32)
            dq0, dk0, f0, b0 = chunk_head(0, c, rows, q_all, k_all, carry[0], carry[1])
            dq1, dk1, f1, b1 = chunk_head(1, c, rows, q_all, k_all, carry[2], carry[3])
            cos, sin = cos_ref[rows, :], sin_ref[rows, :]
            dq_ref[rows, :] = _rope(dq0 + dq1, cos, -sin).astype(BF16)
            dk_ref[rows, :] = (_rope(dk0 + dk1, cos, -sin) * (DK ** -0.5)).astype(BF16)
            return f0, b0, f1, b1

        zero = jnp.zeros((8, LANES), F32)
        sums = lax.fori_loop(0, n_chunks, chunk, (zero, zero, zero, zero), unroll=2)

        dlg = []
        dk_ctx = jnp.zeros((ctx_len, LANES), F32)
        for hh in heads:
            g0 = jnp.concatenate([gf_first[hh], gb_last[hh]], axis=0).astype(BF16)
            dkcc = _dot_nt(vc[hh], g0)
            dkcf = dkcc[:, 0:LANES] * tc[hh]["k_f"]
            dkcb = dkcc[:, LANES:2 * LANES] * tc[hh]["k_b"]
            dlgf = sums[2 * hh] + total((ctx_len - 1.0 - tc[hh]["i"]) * (kc[hh] * dkcf))
            dlgb = sums[2 * hh + 1] + total(tc[hh]["i"] * (kc[hh] * dkcb))
            dk_ctx = dk_ctx + (dkcf + dkcb) * (DK ** -0.5)
            dv_ref[seq:lext, vs[hh]] = _dot(kc_cat[hh], g0).astype(BF16)
            dlg += [jnp.sum(jnp.sum(a, axis=1, keepdims=True), axis=0, keepdims=True) for a in (dlgf, dlgb)]
        dk_ref[seq:lext, :] = dk_ctx.astype(BF16)
        dq_ref[seq:lext, :] = jnp.zeros((ctx_len, LANES), BF16)

        lane8 = lax.broadcasted_iota(jnp.int32, (8, LANES), 1)
        out = jnp.zeros((8, LANES), F32)
        for n, val in enumerate(dlg):
            out = jnp.where(lane8 == n, val, out)
        dlg_ref[0] = out

    qk0 = CB_QK * D // LANES
    q_spec = pl.BlockSpec((lext, LANES), lambda g: (0, qk0 + g))
    k_spec = pl.BlockSpec((lext, LANES), lambda g: (0, qk0 + N_HEADS // 2 + g))
    v_spec = pl.BlockSpec((lext, 2 * DV), lambda g: (0, CB_V * D // (2 * DV) + g))
    table = pl.BlockSpec((lext, LANES), lambda g: (0, 0))
    state = pltpu.VMEM((2, n_chunks, LANES, DV), F32)
    return pl.pallas_call(
        body, name="retention_backward", grid=(N_HEADS // 2,),
        in_specs=[pl.BlockSpec(memory_space=pltpu.SMEM), q_spec, k_spec, v_spec,
                  pl.BlockSpec((seq, 2 * DV), lambda g: (0, g)), table, table],
        out_specs=[pl.BlockSpec((lext, LANES), lambda g: (0, g)), pl.BlockSpec((lext, LANES), lambda g: (0, g)),
                   pl.BlockSpec((lext, 2 * DV), lambda g: (0, g)), pl.BlockSpec((1, 8, LANES), lambda g: (g, 0, 0))],
        out_shape=[jax.ShapeDtypeStruct((lext, N_HEADS * DK), BF16), jax.ShapeDtypeStruct((lext, N_HEADS * DK), BF16),
                   jax.ShapeDtypeStruct((lext, D), BF16), jax.ShapeDtypeStruct((N_HEADS // 2, 8, LANES), F32)],
        scratch_shapes=[pltpu.VMEM((2, n_chunks, 2 * LANES, DV), F32), pltpu.VMEM((2, n_chunks, 2 * LANES, DV), F32),
                        state, state, state, state],
        compiler_params=_cparams(dimension_semantics=("arbitrary",)),
    )(lg, p_ext, p_ext, p_ext, dret, cos_t, sin_t)


def _merge(p_ext, rn, rstd, x, target, w_a, w_b, w_out, vecs, seq):
    n_tiles = seq // TM
    hb = TM // HALO

    def body(h_ref, bg_ref, cg_ref, za_ref, zb_ref, ga_ref, gb_ref, hp_ref, hn_ref, cp_ref, cn_ref,
             rn_ref, rstd_ref, x_ref, t_ref, wa_ref, wb_ref, wo_ref, vec_ref,
             dx1_ref, dpa_ref, dconv_ref, dret_ref, at_ref, b_ref, part_ref):
        i = pl.program_id(0)
        f = lambda ref: ref[...].astype(F32)
        h, bg, cg, za, zb, ga, gb = f(h_ref), f(bg_ref), f(cg_ref), f(za_ref), f(zb_ref), f(ga_ref), f(gb_ref)
        gx, w0, w1, w2 = vec_ref[0:1, :], vec_ref[1:2, :], vec_ref[2:3, :], vec_ref[3:4, :]
        cb, gnw, fw = vec_ref[4:5, :], vec_ref[5:6, :], vec_ref[6:7, :]
        u = cg * h
        row = lax.broadcasted_iota(jnp.int32, (TM, 1), 0)
        u_prev = (f(cp_ref) * f(hp_ref))[HALO - 1:HALO, :]
        u_next = (f(cn_ref) * f(hn_ref))[0:1, :]
        u_prev = jnp.where(i == 0, 0.0, u_prev)
        u_next = jnp.where(i == n_tiles - 1, 0.0, u_next)
        u_up = jnp.where(row == 0, u_prev, pltpu.roll(u, 1, 0))
        u_dn = jnp.where(row == TM - 1, u_next, pltpu.roll(u, TM - 1, 0))
        conv = w0 * u_up + w1 * u + w2 * u_dn + cb
        sza = _sigmoid(za)
        silu_za = za * sza
        a_act = silu_za * bg * conv
        rn = rn_ref[...]
        szb = _sigmoid(zb)
        silu_zb = zb * szb
        rg = rn * gnw
        b_act = silu_zb * rg
        y_a = _dot(a_act.astype(BF16), wa_ref[...])
        y_b = _dot(b_act.astype(BF16), wb_ref[...])
        sga, sgb = _sigmoid(ga), _sigmoid(gb)
        mix = sga * y_a + sgb * y_b
        y = _dot(mix.astype(BF16), wo_ref[...])
        x1 = x_ref[...] + gx * y
        r1 = lax.rsqrt(jnp.mean(x1 * x1, axis=-1, keepdims=True) + EPS)
        xh1 = x1 * r1
        err = xh1 * fw - t_ref[...]
        loss = jnp.sum(jnp.sum(err * err, axis=1, keepdims=True), axis=0, keepdims=True) * (0.5 / D)
        dout = err * (1.0 / D)
        dxh = dout * fw
        dx1 = r1 * (dxh - xh1 * jnp.mean(dxh * xh1, axis=-1, keepdims=True))
        dx1_ref[...] = dx1
        dy = (dx1 * gx).astype(BF16)
        dmix = _dot_nt(dy, wo_ref[...])
        dya = (dmix * sga).astype(BF16)
        dyb = (dmix * sgb).astype(BF16)
        da = _dot_nt(dya, wa_ref[...])
        db = _dot_nt(dyb, wb_ref[...])
        dpa_ref[:, 0:D] = (da * silu_za * conv).astype(BF16)
        dpa_ref[:, D:2 * D] = (da * bg * conv * (sza * (1.0 + za * (1.0 - sza)))).astype(BF16)
        dpa_ref[:, 2 * D:3 * D] = (db * rg * (szb * (1.0 + zb * (1.0 - szb)))).astype(BF16)
        dpa_ref[:, 3 * D:4 * D] = (dmix * y_a * sga * (1.0 - sga)).astype(BF16)
        dpa_ref[:, 4 * D:5 * D] = (dmix * y_b * sgb * (1.0 - sgb)).astype(BF16)
        dconv_ref[...] = (da * silu_za * bg).astype(BF16)
        drn_n = db * silu_zb
        drn = drn_n * gnw
        rstd = rstd_ref[...]
        for hd in range(N_HEADS):
            sl = slice(hd * DV, (hd + 1) * DV)
            dh_, rh = drn[:, sl], rn[:, sl]
            m1 = jnp.mean(dh_, axis=-1, keepdims=True)
            m2 = jnp.mean(dh_ * rh, axis=-1, keepdims=True)
            dret_ref[:, sl] = (rstd[:, sl] * (dh_ - m1 - rh * m2)).astype(BF16)
        at_ref[0] = a_act.T.astype(BF16)
        at_ref[1] = b_act.T.astype(BF16)
        at_ref[2] = mix.T.astype(BF16)
        b_ref[0] = dya
        b_ref[1] = dyb
        b_ref[2] = dy

        @pl.when(i == 0)
        def _():
            part_ref[...] = jnp.zeros((8, D), F32)

        part_ref[0:1, :] += jnp.sum(dout * xh1, axis=0, keepdims=True)
        part_ref[1:2, :] += jnp.sum(dx1 * y, axis=0, keepdims=True)
        part_ref[2:3, :] += jnp.sum(drn_n * rn, axis=0, keepdims=True)
        part_ref[3:4, :] += jnp.broadcast_to(loss, (1, D))

    col = lambda cb_: pl.BlockSpec((TM, D), lambda i, cb_=cb_: (i, cb_))
    prev = lambda cb_: pl.BlockSpec((HALO, D), lambda i, cb_=cb_: (jnp.maximum(i * hb - 1, 0), cb_))
    nxt = lambda cb_: pl.BlockSpec((HALO, D), lambda i, cb_=cb_: (jnp.minimum((i + 1) * hb, n_tiles * hb - 1), cb_))
    tile = pl.BlockSpec((TM, D), lambda i: (i, 0))
    full = lambda a: pl.BlockSpec(a.shape, lambda i: (0,) * a.ndim)
    return pl.pallas_call(
        body, name="merge", grid=(n_tiles,),
        in_specs=[col(CB_H), col(CB_BG), col(CB_CG), col(CB_ZA), col(CB_ZB), col(CB_GA), col(CB_GB),
                  prev(CB_H), nxt(CB_H), prev(CB_CG), nxt(CB_CG),
                  tile, tile, tile, tile, full(w_a), full(w_b), full(w_out), full(vecs)],
        out_specs=[tile, pl.BlockSpec((TM, 5 * D), lambda i: (i, 0)), tile, tile,
                   pl.BlockSpec((3, D, TM), lambda i: (0, 0, i)), pl.BlockSpec((3, TM, D), lambda i: (0, i, 0)),
                   pl.BlockSpec((8, D), lambda i: (0, 0))],
        out_shape=[jax.ShapeDtypeStruct((seq, D), F32), jax.ShapeDtypeStruct((seq, 5 * D), BF16),
                   jax.ShapeDtypeStruct((seq, D), BF16), jax.ShapeDtypeStruct((seq, D), BF16),
                   jax.ShapeDtypeStruct((3, D, seq), BF16), jax.ShapeDtypeStruct((3, seq, D), BF16),
                   jax.ShapeDtypeStruct((8, D), F32)],
        compiler_params=_cparams(dimension_semantics=("arbitrary",)),
    )(p_ext, p_ext, p_ext, p_ext, p_ext, p_ext, p_ext, p_ext, p_ext, p_ext, p_ext,
      rn, rstd, x, target, w_a, w_b, w_out, vecs)


def _conv_backward(p_ext, dconv, dpa, dq, dk, dv, vecs, seq):
    n_tiles = seq // TM
    hb = TM // HALO
    qk = N_HEADS * DK

    def body(h_ref, cg_ref, dc_ref, dcp_ref, dcn_ref, dpa_ref, dq_ref, dk_ref, dv_ref, vec_ref, dp_ref, part_ref):
        i = pl.program_id(0)
        dp_ref[:, CB_QK * D:CB_QK * D + qk] = dq_ref[...]
        dp_ref[:, CB_QK * D + qk:CB_V * D] = dk_ref[...]
        dp_ref[:, CB_V * D:CB_ZB * D] = dv_ref[...]

        @pl.when(i == 0)
        def _():
            part_ref[...] = jnp.zeros((8, D), F32)

        @pl.when(i == n_tiles)
        def _():
            dp_ref[:, 0:CB_QK * D] = jnp.zeros((TM, CB_QK * D), BF16)
            dp_ref[:, CB_ZB * D:PW] = jnp.zeros((TM, PW - CB_ZB * D), BF16)

        @pl.when(i < n_tiles)
        def _():
            f = lambda ref: ref[...].astype(F32)
            h, cg, dc = f(h_ref), f(cg_ref), f(dc_ref)
            w0, w1, w2 = vec_ref[1:2, :], vec_ref[2:3, :], vec_ref[3:4, :]
            row = lax.broadcasted_iota(jnp.int32, (TM, 1), 0)
            dc_prev = jnp.where(i == 0, 0.0, f(dcp_ref)[HALO - 1:HALO, :])
            dc_next = jnp.where(i == n_tiles - 1, 0.0, f(dcn_ref)[0:1, :])
            dc_up = jnp.where(row == 0, dc_prev, pltpu.roll(dc, 1, 0))
            dc_dn = jnp.where(row == TM - 1, dc_next, pltpu.roll(dc, TM - 1, 0))
            du = w0 * dc_dn + w1 * dc + w2 * dc_up
            u = cg * h
            dp_ref[:, CB_H * D:(CB_H + 1) * D] = (du * cg).astype(BF16)
            dp_ref[:, CB_CG * D:(CB_CG + 1) * D] = (du * h).astype(BF16)
            dp_ref[:, CB_BG * D:(CB_BG + 1) * D] = dpa_ref[:, 0:D]
            dp_ref[:, CB_ZA * D:(CB_ZA + 1) * D] = dpa_ref[:, D:2 * D]
            dp_ref[:, CB_ZB * D:(CB_GB + 1) * D] = dpa_ref[:, 2 * D:5 * D]
            part_ref[0:1, :] += jnp.sum(u * dc_dn, axis=0, keepdims=True)
            part_ref[1:2, :] += jnp.sum(u * dc, axis=0, keepdims=True)
            part_ref[2:3, :] += jnp.sum(u * dc_up, axis=0, keepdims=True)
            part_ref[3:4, :] += jnp.sum(dc, axis=0, keepdims=True)

    last = n_tiles - 1
    col = lambda cb_: pl.BlockSpec((TM, D), lambda i, cb_=cb_: (jnp.minimum(i, last), cb_))
    lat = lambda w: pl.BlockSpec((TM, w), lambda i: (jnp.minimum(i, last), 0))
    ext = lambda w: pl.BlockSpec((TM, w), lambda i: (i, 0))
    return pl.pallas_call(
        body, name="conv_backward", grid=(n_tiles + 1,),
        in_specs=[col(CB_H), col(CB_CG), lat(D),
                  pl.BlockSpec((HALO, D), lambda i: (jnp.clip(i * hb - 1, 0, n_tiles * hb - 1), 0)),
                  pl.BlockSpec((HALO, D), lambda i: (jnp.minimum((i + 1) * hb, n_tiles * hb - 1), 0)),
                  lat(5 * D), ext(qk), ext(qk), ext(D),
                  pl.BlockSpec(vecs.shape, lambda i: (0, 0))],
        out_specs=[ext(PW), pl.BlockSpec((8, D), lambda i: (0, 0))],
        out_shape=[jax.ShapeDtypeStruct((seq + TM, PW), BF16), jax.ShapeDtypeStruct((8, D), F32)],
        compiler_params=_cparams(dimension_semantics=("arbitrary",)),
    )(p_ext, p_ext, dconv, dconv, dconv, dpa, dq, dk, dv, vecs)


def _input_backward(dp, shards, ids, x, ctx, dx1, modx, modc, norm_w):
    seq = x.shape[0]
    lext = seq + ctx.shape[0]
    n_x = seq // TM
    n_w = len(shards.arrays)

    def body(ids_ref, dp_ref, x_ref, ctx_ref, dx1_ref, mx_ref, mc_ref, nw_ref, *rest):
        w_hbm, (gx_ref, part_ref, w_ref, w_sems) = rest[:n_w], rest[n_w:]
        i = pl.program_id(0)
        _load_w_in(i, ids_ref, shards, w_hbm, w_ref, w_sems)
        is_ctx = i >= n_x
        dxm = _dot_nt(dp_ref[...], w_ref[...])
        x = jnp.where(is_ctx, ctx_ref[...], x_ref[...])
        r = lax.rsqrt(jnp.mean(x * x, axis=-1, keepdims=True) + EPS)
        xh = x * r
        nw = nw_ref[...]
        sc = jnp.where(is_ctx, mc_ref[1:2, :], mx_ref[1:2, :])
        dxn = dxm * (1.0 + sc)
        dxh = dxn * nw
        dx = r * (dxh - xh * jnp.mean(dxh * xh, axis=-1, keepdims=True))

        @pl.when(jnp.logical_not(is_ctx))
        def _():
            gx_ref[...] = dx1_ref[...] + dx

        @pl.when(i == 0)
        def _():
            part_ref[...] = jnp.zeros((8, D), F32)

        fx = jnp.where(is_ctx, 0.0, 1.0)
        d_shift = jnp.sum(dxm, axis=0, keepdims=True)
        d_scale = jnp.sum(dxm * (xh * nw), axis=0, keepdims=True)
        part_ref[0:1, :] += fx * d_shift
        part_ref[1:2, :] += fx * d_scale
        part_ref[2:3, :] += jnp.sum(dxn * xh, axis=0, keepdims=True)
        part_ref[3:4, :] += (1.0 - fx) * d_shift
        part_ref[4:5, :] += (1.0 - fx) * d_scale

    lat = lambda w: pl.BlockSpec((TM, w), lambda i, ids_: (jnp.minimum(i, n_x - 1), 0))
    ext = lambda w: pl.BlockSpec((TM, w), lambda i, ids_: (i, 0))
    full = lambda a: pl.BlockSpec(a.shape, lambda i, ids_: (0,) * a.ndim)
    grid_spec = pltpu.PrefetchScalarGridSpec(
        num_scalar_prefetch=1, grid=(lext // TM,),
        in_specs=[ext(PW), lat(D), full(ctx), lat(D), full(modx), full(modc), full(norm_w)]
        + [pl.BlockSpec(memory_space=pl.ANY)] * n_w,
        out_specs=[lat(D), pl.BlockSpec((8, D), lambda i, ids_: (0, 0))],
        scratch_shapes=[pltpu.VMEM((D, PW), BF16), pltpu.SemaphoreType.DMA((N_DEV,))])
    return pl.pallas_call(
        body, name="input_backward", grid_spec=grid_spec,
        out_shape=[jax.ShapeDtypeStruct((seq, D), F32), jax.ShapeDtypeStruct((8, D), F32)],
        compiler_params=_cparams(dimension_semantics=("arbitrary",)),
    )(ids, dp, x, ctx, dx1, modx, modc, norm_w, *shards.arrays)


def _weight_grad_in(xm_t, dp, owners, name):
    lext = xm_t.shape[1]

    def body(own_ref, a_ref, b_ref, o_ref):
        o_ref[0] = _dot(a_ref[...], b_ref[...])

    grid_spec = pltpu.PrefetchScalarGridSpec(
        num_scalar_prefetch=1, grid=(4,),
        in_specs=[pl.BlockSpec((D, lext), lambda j, own: (0, 0)),
                  pl.BlockSpec((lext, WSH), lambda j, own: (0, own[j]))],
        out_specs=pl.BlockSpec((1, D, WSH), lambda j, own: (j, 0, 0)))
    return pl.pallas_call(
        body, name=name, grid_spec=grid_spec,
        out_shape=jax.ShapeDtypeStruct((4, D, WSH), F32),
        compiler_params=_cparams(dimension_semantics=("arbitrary",)),
    )(owners, xm_t, dp)


def _weight_grad_square(a_t, b):
    seq = a_t.shape[2]

    def body(a_ref, b_ref, o_ref):
        o_ref[:, 0] = _dot(a_ref[0], b_ref[0]).reshape(N_DEV, RSH, D)

    return pl.pallas_call(
        body, name="weight_grad_square", grid=(3,),
        in_specs=[pl.BlockSpec((1, D, seq), lambda t: (t, 0, 0)), pl.BlockSpec((1, seq, D), lambda t: (t, 0, 0))],
        out_specs=pl.BlockSpec((N_DEV, 1, RSH, D), lambda t: (0, t, 0, 0)),
        out_shape=jax.ShapeDtypeStruct((N_DEV, 3, RSH, D), F32),
        compiler_params=_cparams(dimension_semantics=("arbitrary",)),
    )(a_t, b)


def _rope_tables(seq, ctx_len, token):
    rows = seq // GRID_W
    row = jnp.repeat(jnp.arange(rows), GRID_W).astype(F32)
    col = jnp.tile(jnp.arange(GRID_W), rows).astype(F32)
    nf = DK // 4
    inv = ROPE_BASE ** (-jnp.arange(nf, dtype=F32) / nf)
    ang = _tie(jnp.concatenate([row[:, None] * inv, col[:, None] * inv], axis=-1), token)
    cos, sin = jnp.cos(ang), jnp.sin(ang)
    cos_t = jnp.tile(cos, (1, 4))
    sin_t = jnp.tile(jnp.concatenate([-sin, sin], axis=-1), (1, 2))
    cos_t = jnp.concatenate([cos_t, jnp.ones((ctx_len, LANES), F32)], axis=0)
    sin_t = jnp.concatenate([sin_t, jnp.zeros((ctx_len, LANES), F32)], axis=0)
    return cos_t, sin_t


def _tie(value, token):
    return value + token[0, 0].astype(value.dtype)


def _pad_lanes(a):
    return jnp.pad(a, ((0, 0), (0, D - a.shape[1])))


def _rows(total, *parts):
    width = max(a.shape[1] for _, a in parts)
    out = None
    for row, a in parts:
        padded = jnp.pad(a, ((row, total - row - a.shape[0]), (0, width - a.shape[1])))
        out = padded if out is None else out + padded
    return out


def kernel(x, c, ctx, c_ctx, norm_w, ada_w, ada_b, w_in, conv_w, conv_b, decay_logit, gn_w, w_a, w_b, w_out, final_norm_w, loss_target, m_c_ctx, m_norm_w, m_ada_w, m_ada_b, m_w_in, m_conv_w, m_conv_b, m_decay_logit, m_gn_w, m_w_a, m_w_b, m_w_out, m_final_norm_w, v_c_ctx, v_norm_w, v_ada_w, v_ada_b, v_w_in, v_conv_w, v_conv_b, v_decay_logit, v_gn_w, v_w_a, v_w_b, v_w_out, v_final_norm_w):
    xi, yi, ci = _coords()
    me = 4 * xi + 2 * yi + ci
    chip = 2 * xi + yi
    seq, ctx_len = x.shape[1], ctx.shape[1]
    assert seq % TM == 0 and seq % RET_C == 0 and ctx_len == TM and seq % GRID_W == 0
    csh = D // N_DEV

    blk = jnp.pad(c, ((0, 7), (0, 0))) + jnp.pad(conv_w[0], ((1, 4), (0, D - csh)))
    got, _ = _all_gather_small(blk, "gather_cond")
    conv_w_all = got[:, 1:4, 0:csh].transpose(1, 0, 2).reshape(3, D)
    c16 = _rows(16, (0, got[:, 0, :]), (N_DEV, c_ctx[None]))
    ada_b_sh = lax.dynamic_slice(ada_b, (0, me * ADA_SH), (1, ADA_SH))
    mod_sh, act16, lg = _modulation(c16, ada_w[0], ada_b_sh, decay_logit[0])
    mod_all, small_done = _all_gather_small(mod_sh, "gather_mod")
    mod_all = mod_all.transpose(1, 0, 2).reshape(16, 3 * D)
    modx = lax.dynamic_slice(mod_all, (me, 0), (1, 3 * D)).reshape(3, D)
    modc = mod_all[8].reshape(3, D)

    x2, tgt = x[0], loss_target[0]
    wblock = lambda n: jax.ShapeDtypeStruct((n, D, WSH), BF16)
    ids_of = lambda ks: jnp.stack([jnp.bitwise_xor(me, k) for k in ks]).astype(jnp.int32)
    own_shard = w_in[0].astype(BF16)[None]
    sq_shards = [w[0].astype(BF16) for w in (w_a, w_b, w_out)]
    wave1 = _Exchange(
        "gather_in_first", [jnp.broadcast_to(own_shard, (2, D, WSH)), own_shard], [wblock(4)], 3,
        lambda srcs, lands: _forward_plan([(0, 1)])(srcs, lands) + _send_plan((4, 2))(srcs[1:], lands),
        after=small_done)
    cos_t, sin_t = _rope_tables(seq, ctx_len, wave1.token)
    modx = _tie(modx, wave1.token)
    xm, xm_t = _prenorm(x2, ctx[0], modx, modc, norm_w)
    (w_pair, _), _ = wave1.wait(xm_t, 1, through=[0])
    first = _Shards([w_pair], [(0, 0), (0, 1)])
    p_ext = _in_projection(None, xm, cos_t, sin_t, first, ids_of((0, 1)), "in_projection_pair")
    _, (w_nbr,) = wave1.wait(p_ext, through=[1, 2])
    wave2 = _Exchange(
        "gather_in_second", [w_nbr, own_shard], [wblock(2)], 3,
        lambda srcs, lands: _forward_plan([(0, 2), (1, 3)])(srcs, lands) + _send_plan((6,))(srcs[1:], lands))
    (w_nbr, _), _ = wave2.wait(wave2.token, 2, through=[0])
    second = _Shards([w_nbr], [(0, 0), (0, 1), (0, 2), (0, 3)])
    p_ext = _in_projection(p_ext, xm, cos_t, sin_t, second, ids_of((4, 2, 5, 3)), "in_projection_neighbours")
    _, (w_diag,) = wave2.wait(p_ext, through=[1, 2])
    wave3 = _Exchange(
        "gather_in_third", [w_diag] + sq_shards, [jax.ShapeDtypeStruct((N_DEV, RSH, D), BF16)] * 3, 1 + 3 * (N_DEV - 1),
        lambda srcs, lands: _forward_plan([(0, 1)])(srcs, lands) + _gather_plan(srcs[1:], lands))
    (w_diag, *_), _ = wave3.wait(wave3.token, 1, through=[0])
    third = _Shards([w_diag], [(0, 0), (0, 1)])
    p_ext = _in_projection(p_ext, xm, cos_t, sin_t, third, ids_of((6, 7)), "in_projection_diagonal")
    w_shards, w_ids = first + second + third, ids_of((0, 1, 4, 2, 5, 3, 6, 7))

    rn, rstd = _retention_forward(p_ext, lg, seq, ctx_len)
    _, sq_lands = wave3.wait(rstd, through=range(1, 7))
    w_a_all, w_b_all, w_out_all = (
        lax.dynamic_update_slice(land, shard[None], (me, 0, 0)).reshape(D, D) for land, shard in zip(sq_lands, sq_shards))
    vecs = _rows(8, (0, modx[2:3]), (1, conv_w_all), (4, conv_b), (5, gn_w), (6, final_norm_w[None]))
    dx1, dpa, dconv, dret, op_at, op_b, part_m = _merge(p_ext, rn, rstd, x2, tgt, w_a_all, w_b_all, w_out_all, vecs, seq)

    j4 = jnp.arange(4, dtype=jnp.int32)
    owners = (2 * jnp.bitwise_xor(chip, j4) + ci).astype(jnp.int32)
    owners_sib = (2 * jnp.bitwise_xor(chip, j4) + (1 - ci)).astype(jnp.int32)
    gw_sq = _weight_grad_square(op_at, op_b).reshape(N_DEV, 3 * RSH, D)
    rs_sq_pair = _Exchange("rs_square_pair", [gw_sq], [jax.ShapeDtypeStruct((4, 3 * RSH, D), F32)], 4,
                           _pair_plan(lambda j, chip_, c_: 2 * jnp.bitwise_xor(chip_, j) + (1 - c_)))
    dq, dk, dv, dlg = _retention_backward(p_ext, dret, _tie(lg, rs_sq_pair.token), cos_t, sin_t, seq, ctx_len)
    (gw_sq,), (r1_sq,) = rs_sq_pair.wait(dlg)
    own_sq, send_sq = _pair_sum(gw_sq, r1_sq, owners, "pair_sum_square")
    rs_sq_chips = _Exchange("rs_square_chips", [send_sq], [jax.ShapeDtypeStruct((3, 3 * RSH, D), BF16)], 3, _chips_plan)
    dp, part_c = _conv_backward(p_ext, dconv, dpa, dq, dk, dv, _tie(vecs, rs_sq_chips.token), seq)
    gw_sib = _weight_grad_in(xm_t, dp, owners_sib, "weight_grad_in_sibling")
    rs_in_pair = _Exchange("rs_in_pair", [gw_sib], [jax.ShapeDtypeStruct((4, D, WSH), F32)], 4,
                           _pair_plan(lambda j, chip_, c_: j))
    gw_own = _weight_grad_in(xm_t, dp, _tie(owners, rs_in_pair.token), "weight_grad_in_own")
    _, (r1_in,) = rs_in_pair.wait(gw_own)
    own_in, send_in = _pair_sum(gw_own, r1_in, j4, "pair_sum_in")
    rs_in_chips = _Exchange("rs_in_chips", [send_in], [jax.ShapeDtypeStruct((3, D, WSH), BF16)], 3, _chips_plan)
    grad_x, part_i = _input_backward(dp, w_shards, w_ids, x2, ctx[0], dx1, _tie(modx, rs_in_chips.token), modc, norm_w)

    dl = dlg[:, 0, 0:4]
    dlg_row = jnp.pad(dl[:, 0::2].reshape(1, N_HEADS), ((0, 0), (0, D - N_HEADS))) + jnp.pad(
        dl[:, 1::2].reshape(1, N_HEADS), ((0, 0), (N_HEADS, D - 2 * N_HEADS)))
    partials = _rows(16, (0, part_m[0:1]), (1, part_i[2:3]), (2, part_c[3:4]), (3, part_m[2:3]), (4, part_c[0:3]),
                     (7, part_i[0:2]), (9, part_m[1:2]), (10, part_i[3:5]), (13, part_m[3:4]), (14, dlg_row))
    got, _ = _all_gather_small(partials, "gather_partials")
    tot = _sum_devices(got, "sum_partials")
    dmodc = tot[10:13].reshape(1, 3 * D)
    dmod16 = _rows(16, (0, got[:, 7:10, :].reshape(N_DEV, 3 * D)), (N_DEV, dmodc))
    dmod16 = lax.dynamic_slice(dmod16, (0, me * ADA_SH), (16, ADA_SH))
    dmodc8 = _rows(8, (0, dmod16[8:9]))
    g_ada_w, d_ada_w, nm_ada_w, nv_ada_w, cctx_part = _ada_backward(
        act16.T, dmod16, dmodc8, ada_w[0], m_ada_w[0], v_ada_w[0])
    dact_ctx = _sum_devices(_all_gather_small(cctx_part, "gather_cctx")[0], "sum_cctx")

    view = {"c_ctx": (1, D), "final_norm_w": (1, D), "decay_logit": (1, 2 * N_HEADS), "conv_w": (3, csh)}
    given = {"c_ctx": (c_ctx, m_c_ctx, v_c_ctx), "norm_w": (norm_w, m_norm_w, v_norm_w),
             "conv_b": (conv_b, m_conv_b, v_conv_b), "gn_w": (gn_w, m_gn_w, v_gn_w),
             "final_norm_w": (final_norm_w, m_final_norm_w, v_final_norm_w), "ada_b": (ada_b, m_ada_b, v_ada_b),
             "decay_logit": (decay_logit, m_decay_logit, v_decay_logit), "conv_w": (conv_w, m_conv_w, v_conv_w)}
    small_in = [tuple(a.reshape(view.get(name, a.shape)) for a in given[name]) for name in SMALL_PARAMS]
    conv_w_grad = lax.dynamic_slice(tot, (4, me * csh), (3, csh))
    small_out = _adam_small(tot, dact_ctx, conv_w_grad, small_in)
    results = {name: [o.reshape(given[name][0].shape) for o in outs_]
               for name, outs_ in zip(SMALL_PARAMS, small_out)}

    _, (r2_sq,) = rs_sq_chips.wait(small_out[0][0])
    _, (r2_in,) = rs_in_chips.wait(small_out[0][0])
    results["w_in"] = [o[None] for o in _adam_sharded(w_in[0], m_w_in[0], v_w_in[0], own_in, r2_in, "adam_w_in")]
    square = _adam_square([(w_a[0], m_w_a[0], v_w_a[0]), (w_b[0], m_w_b[0], v_w_b[0]),
                           (w_out[0], m_w_out[0], v_w_out[0])], own_sq, r2_sq)
    for name, outs_ in zip(("w_a", "w_b", "w_out"), square):
        results[name] = [o[None] for o in outs_]
    results["ada_w"] = [o[None] for o in (g_ada_w, d_ada_w, nm_ada_w, nv_ada_w)]

    order = ("c_ctx", "norm_w", "ada_w", "ada_b", "w_in", "conv_w", "conv_b", "decay_logit", "gn_w",
             "w_a", "w_b", "w_out", "final_norm_w")
    outs = [results[name][kind] for kind in range(4) for name in order]
    return (tot[13, 0], grad_x[None], *outs)
```

```python
import functools
import math

import jax
import jax.numpy as jnp
from jax import lax
from jax.experimental import pallas as pl
from jax.experimental.pallas import tpu as pltpu

F32 = jnp.float32
BF16 = jnp.bfloat16
MESH = pl.DeviceIdType.MESH
_HBM_SPEC = pl.BlockSpec(memory_space=pltpu.HBM)
_SEM_SPEC = pl.BlockSpec(memory_space=pltpu.SEMAPHORE)
_EFFECT = pltpu.SideEffectType.DATAFLOW_SIDE_EFFECTING

N_DEV = 8
D = 1024
N_HEADS = 8
DK = 64
DV = 128
GRID_W = 64
ROPE_BASE = 10000.0
EPS = 1e-6
PW = 9 * D
WSH = PW // N_DEV
RSH = D // N_DEV
ADA_SH = 3 * D // N_DEV
TM = 256
RET_C = 256
HALO = 16
LANES = 128
VMEM_LIMIT = 60 * 1024 * 1024

ADAM_LR = 0.001
ADAM_B1 = 0.9
ADAM_B2 = 0.999
ADAM_EPS = 1e-08
ADAM_WD = 0.01
ADAM_STEP = 10

CB_H, CB_BG, CB_CG, CB_ZA, CB_QK, CB_V, CB_ZB, CB_GA, CB_GB = range(9)


def _cparams(**kw):
    return pltpu.CompilerParams(vmem_limit_bytes=VMEM_LIMIT, **kw)


def _dot(a, b):
    return jnp.dot(a, b, preferred_element_type=F32)


def _dot_nt(a, b):
    return lax.dot_general(a, b, (((1,), (1,)), ((), ())), preferred_element_type=F32)


def _dot_tn(a, b):
    return lax.dot_general(a, b, (((0,), (0,)), ((), ())), preferred_element_type=F32)


def _sigmoid(z):
    return 0.5 * jnp.tanh(0.5 * z) + 0.5


def _row_tile(rows):
    return TM if rows % TM == 0 else rows


def _coords():
    return lax.axis_index("x"), lax.axis_index("y"), lax.axis_index("c")


def _flip(v, bit):
    return 1 - v if bit else v


def _all_gather_small(blk, name):
    rows, cols = blk.shape

    def body(x_ref, out_ref, done, send_sems, recv_sems, local_sem):
        x, y, c = _coords()
        me = 4 * x + 2 * y + c
        mine = pltpu.make_async_copy(x_ref, out_ref.at[me], local_sem)
        mine.start()

        def copy(k, slot):
            peer = (_flip(x, k & 4), _flip(y, k & 2), _flip(c, k & 1))
            return pltpu.make_async_remote_copy(
                src_ref=x_ref, dst_ref=out_ref.at[slot], send_sem=send_sems.at[k - 1],
                recv_sem=recv_sems.at[k - 1], device_id=peer, device_id_type=MESH)

        for k in range(1, N_DEV):
            copy(k, me).start()
        for k in range(1, N_DEV):
            copy(k, jnp.bitwise_xor(me, k)).wait_recv()
        for k in range(1, N_DEV):
            copy(k, me).wait_send()
        mine.wait()
        done[...] = jnp.zeros((8, LANES), F32)

    vmem = pl.BlockSpec(memory_space=pltpu.VMEM)
    return pl.pallas_call(
        body, name=name,
        out_shape=[jax.ShapeDtypeStruct((N_DEV, rows, cols), blk.dtype), jax.ShapeDtypeStruct((8, LANES), F32)],
        in_specs=[vmem], out_specs=[vmem, vmem],
        scratch_shapes=[pltpu.SemaphoreType.DMA((N_DEV - 1,)), pltpu.SemaphoreType.DMA((N_DEV - 1,)),
                        pltpu.SemaphoreType.DMA],
    )(blk)


class _Absent:
    at = property(lambda self: self)

    def __getitem__(self, index):
        return self


class _Exchange:
    def __init__(self, name, srcs, land_shapes, n_copies, plan, after=None):
        self.name, self.plan, self.n_copies = name, plan, n_copies
        self.n_src, self.n_land = len(srcs), len(land_shapes)
        hbm = lambda a: pltpu.HBM(a.shape, a.dtype)
        n = self.n_src + self.n_land
        lands = [pltpu.with_memory_space_constraint(lax.empty(s.shape, s.dtype), pltpu.HBM) for s in land_shapes]
        srcs = [pltpu.with_memory_space_constraint(s, pltpu.HBM) for s in srcs]
        extra = [] if after is None else [after]

        def body(*refs):
            send_sems, recv_sems = refs[n + len(extra)], refs[n + len(extra) + 1]
            for cp in self._copies(refs, send_sems, recv_sems):
                cp.start()
            refs[-1][...] = jnp.zeros((8, LANES), F32)

        outs = pl.pallas_call(
            body, name=name + "_start",
            out_shape=(pltpu.SemaphoreType.DMA((n_copies,)), pltpu.SemaphoreType.DMA((n_copies,)),
                       *[hbm(a) for a in srcs], *[hbm(a) for a in lands], jax.ShapeDtypeStruct((8, LANES), F32)),
            in_specs=[_HBM_SPEC] * n + [pl.BlockSpec(memory_space=pl.ANY)] * len(extra),
            out_specs=(_SEM_SPEC, _SEM_SPEC, *[_HBM_SPEC] * n, pl.BlockSpec(memory_space=pltpu.VMEM)),
            input_output_aliases={i: 2 + i for i in range(n)},
            compiler_params=pltpu.CompilerParams(has_side_effects=_EFFECT),
        )(*srcs, *lands, *extra)
        self.send_sems, self.recv_sems = outs[0], outs[1]
        self.buffers = list(outs[2:2 + n])
        self.token = outs[-1]
        self.waited = 0

    def _copies(self, refs, send_sems, recv_sems, lo=0, hi=None):
        src_refs, land_refs = refs[:self.n_src], refs[self.n_src:self.n_src + self.n_land]
        planned = list(enumerate(self.plan(src_refs, land_refs)))[lo:hi]
        return [pltpu.make_async_remote_copy(src_ref=s, dst_ref=d, send_sem=send_sems.at[k], recv_sem=recv_sems.at[k],
                                             device_id=dev, device_id_type=MESH) for k, (s, d, dev) in planned]

    def wait(self, after, count=None, through=None):
        n = self.n_src + self.n_land
        lo = self.waited
        hi = self.n_copies if count is None else lo + count
        self.waited = hi
        through = list(range(n)) if through is None else list(through)
        m = len(through)

        def body(*refs):
            held = [_Absent()] * n
            for pos, ref in zip(through, refs[:m]):
                held[pos] = ref
            for cp in self._copies(held, refs[m], refs[m + 1], lo, hi):
                cp.wait_send()
                cp.wait_recv()

        outs = pl.pallas_call(
            body, name=f"{self.name}_wait{lo}" if lo or hi < self.n_copies else self.name + "_wait",
            out_shape=tuple(pltpu.HBM(self.buffers[p].shape, self.buffers[p].dtype) for p in through),
            in_specs=[_HBM_SPEC] * m + [_SEM_SPEC, _SEM_SPEC, pl.BlockSpec(memory_space=pl.ANY)],
            out_specs=tuple([_HBM_SPEC] * m),
            input_output_aliases={i: i for i in range(m)},
            compiler_params=pltpu.CompilerParams(has_side_effects=_EFFECT),
        )(*[self.buffers[p] for p in through], self.send_sems, self.recv_sems, after)
        for pos, out in zip(through, outs):
            self.buffers[pos] = out
        return list(self.buffers[:self.n_src]), list(self.buffers[self.n_src:])


def _pair_plan(src_index):
    def plan(srcs, lands):
        x, y, c = _coords()
        chip = 2 * x + y
        return [(srcs[0].at[src_index(j, chip, c)], lands[0].at[j], (x, y, 1 - c)) for j in range(4)]
    return plan


def _chips_plan(srcs, lands):
    x, y, c = _coords()
    return [(srcs[0].at[j - 1], lands[0].at[j - 1], (_flip(x, j & 2), _flip(y, j & 1), c)) for j in range(1, 4)]


def _peer(k):
    x, y, c = _coords()
    return _flip(x, k & 4), _flip(y, k & 2), _flip(c, k & 1)


def _send_plan(ks):
    def plan(srcs, lands):
        return [(srcs[0].at[0], lands[0].at[p], _peer(k)) for p, k in enumerate(ks)]
    return plan


def _forward_plan(moves):
    def plan(srcs, lands):
        return [(srcs[0].at[s], srcs[0].at[d], _peer(1)) for s, d in moves]
    return plan


def _gather_plan(srcs, lands):
    x, y, c = _coords()
    me = 4 * x + 2 * y + c
    return [(srcs[a], lands[a].at[me], (_flip(x, k & 4), _flip(y, k & 2), _flip(c, k & 1)))
            for a in range(len(srcs)) for k in range(1, N_DEV)]


def _pair_sum(grad, recv, owners, name):
    _, rows, cols = grad.shape
    tr = _row_tile(rows)

    def body(own_ref, g_ref, r_ref, mine_ref, send_ref):
        j = pl.program_id(1)
        total = g_ref[...] + r_ref[...]

        @pl.when(j == 0)
        def _():
            mine_ref[...] = total

        @pl.when(j > 0)
        def _():
            send_ref[...] = total.astype(BF16)

    grid_spec = pltpu.PrefetchScalarGridSpec(
        num_scalar_prefetch=1, grid=(rows // tr, 4),
        in_specs=[pl.BlockSpec((1, tr, cols), lambda i, j, own: (own[j], i, 0)),
                  pl.BlockSpec((1, tr, cols), lambda i, j, own: (j, i, 0))],
        out_specs=[pl.BlockSpec((1, tr, cols), lambda i, j, own: (0, i, 0)),
                   pl.BlockSpec((1, tr, cols), lambda i, j, own: (jnp.maximum(j - 1, 0), i, 0))])
    return pl.pallas_call(
        body, name=name, grid_spec=grid_spec,
        out_shape=[jax.ShapeDtypeStruct((1, rows, cols), F32), jax.ShapeDtypeStruct((3, rows, cols), BF16)],
        compiler_params=_cparams(dimension_semantics=("arbitrary", "arbitrary")),
    )(owners, grad, recv)


def _modulation(c16, ada_w_sh, ada_b_sh, decay_logit):
    def body(c_ref, w_ref, b_ref, dl_ref, mod_ref, act_ref, lg_ref):
        cv = c_ref[...]
        act = cv * _sigmoid(cv)
        act_ref[...] = act
        mod_ref[...] = jnp.dot(act, w_ref[...], preferred_element_type=F32,
                               precision=lax.Precision.HIGHEST) + b_ref[...]
        z = dl_ref[...]
        lg_ref[...] = jnp.minimum(z, 0.0) - jnp.log(1.0 + jnp.exp(-jnp.abs(z)))

    return pl.pallas_call(
        body, name="modulation",
        out_shape=[jax.ShapeDtypeStruct((16, ADA_SH), F32), jax.ShapeDtypeStruct((16, D), F32),
                   jax.ShapeDtypeStruct(decay_logit.shape, F32)],
        compiler_params=_cparams(),
    )(c16, ada_w_sh, ada_b_sh, decay_logit)


def _adam_update(w, g, m, v):
    m2 = ADAM_B1 * m + (1.0 - ADAM_B1) * g
    v2 = ADAM_B2 * v + (1.0 - ADAM_B2) * (g * g)
    m_hat = m2 / (1.0 - ADAM_B1 ** ADAM_STEP)
    v_hat = v2 / (1.0 - ADAM_B2 ** ADAM_STEP)
    delta = -ADAM_LR * (m_hat / (jnp.sqrt(v_hat) + ADAM_EPS) + ADAM_WD * w)
    return delta, m2, v2


def _adam_sharded(w, m, v, own, recv, name):
    rows, cols = w.shape
    tr = _row_tile(rows)

    def body(w_ref, m_ref, v_ref, g0, g1, g2, g3, g_ref, d_ref, m_out, v_out):
        g = ((g0[0] + g1[0].astype(F32)) + g2[0].astype(F32)) + g3[0].astype(F32)
        delta, m2, v2 = _adam_update(w_ref[...], g, m_ref[...], v_ref[...])
        g_ref[...] = g
        d_ref[...] = delta
        m_out[...] = m2
        v_out[...] = v2

    flat = pl.BlockSpec((tr, cols), lambda i: (i, 0))
    part = lambda j: pl.BlockSpec((1, tr, cols), lambda i, j=j: (j, i, 0))
    return pl.pallas_call(
        body, name=name, grid=(rows // tr,),
        in_specs=[flat, flat, flat, part(0), part(0), part(1), part(2)],
        out_specs=[flat] * 4,
        out_shape=[jax.ShapeDtypeStruct((rows, cols), F32)] * 4,
        compiler_params=_cparams(dimension_semantics=("arbitrary",)),
    )(w, m, v, own, recv, recv, recv)


def _ada_backward(act_t, dmod16, dmodc8, ada_w_sh, m, v):
    def body(at_ref, dm_ref, dc_ref, w_ref, m_ref, v_ref, g_ref, d_ref, m_out, v_out, pc_ref):
        g = jnp.dot(at_ref[...], dm_ref[...], preferred_element_type=F32, precision=lax.Precision.HIGHEST)
        w = w_ref[...]
        delta, m2, v2 = _adam_update(w, g, m_ref[...], v_ref[...])
        g_ref[...] = g
        d_ref[...] = delta
        m_out[...] = m2
        v_out[...] = v2
        pc_ref[...] = lax.dot_general(dc_ref[...], w, (((1,), (1,)), ((), ())), preferred_element_type=F32,
                                      precision=lax.Precision.HIGHEST)

    return pl.pallas_call(
        body, name="ada_backward",
        out_shape=[jax.ShapeDtypeStruct((D, ADA_SH), F32)] * 4 + [jax.ShapeDtypeStruct((8, D), F32)],
        compiler_params=_cparams(),
    )(act_t, dmod16, dmodc8, ada_w_sh, m, v)


def _sum_devices(gathered, name):
    _, rows, cols = gathered.shape

    def body(g_ref, o_ref):
        acc = g_ref[0]
        for d in range(1, N_DEV):
            acc = acc + g_ref[d]
        o_ref[...] = acc

    return pl.pallas_call(body, name=name, out_shape=jax.ShapeDtypeStruct((rows, cols), F32),
                          compiler_params=_cparams())(gathered)


SMALL_PARAMS = ("c_ctx", "norm_w", "conv_b", "gn_w", "final_norm_w", "ada_b", "decay_logit", "conv_w")


def _adam_small(tot, dact_ctx, conv_w_grad, params):
    n = len(SMALL_PARAMS)

    def body(tot_ref, dact_ref, cwg_ref, *refs):
        ins, outs = refs[:3 * n], refs[3 * n:]
        nh = 2 * N_HEADS
        raw = {
            "c_ctx": dact_ref[0:1, :],
            "norm_w": tot_ref[1:2, :], "conv_b": tot_ref[2:3, :], "gn_w": tot_ref[3:4, :],
            "final_norm_w": tot_ref[0:1, :],
            "ada_b": jnp.concatenate([tot_ref[7 + r:8 + r, :] + tot_ref[10 + r:11 + r, :] for r in range(3)], axis=1),
            "decay_logit": tot_ref[14:15, 0:nh],
            "conv_w": cwg_ref[...],
        }
        for k, name in enumerate(SMALL_PARAMS):
            w, m, v = ins[3 * k][...], ins[3 * k + 1][...], ins[3 * k + 2][...]
            g = raw[name]
            if name == "c_ctx":
                s = _sigmoid(w)
                g = g * (s * (1.0 + w * (1.0 - s)))
            elif name == "decay_logit":
                g = g * (1.0 - _sigmoid(w))
            delta, m2, v2 = _adam_update(w, g, m, v)
            for ref, val in zip(outs[4 * k:4 * k + 4], (g, delta, m2, v2)):
                ref[...] = val

    flat = [a for trio in params for a in trio]
    out_shape = [jax.ShapeDtypeStruct(trio[0].shape, F32) for trio in params for _ in range(4)]
    outs = pl.pallas_call(body, name="adam_small", out_shape=out_shape, compiler_params=_cparams())(
        tot, dact_ctx, conv_w_grad, *flat)
    return [outs[4 * k:4 * k + 4] for k in range(n)]


def _adam_square(params, own, recv):
    def body(own_ref, recv_ref, *refs):
        ins, outs = refs[:9], refs[9:]
        for k in range(3):
            rows = slice(k * RSH, (k + 1) * RSH)
            g = own_ref[0, rows, :]
            for j in range(3):
                g = g + recv_ref[j, rows, :].astype(F32)
            delta, m2, v2 = _adam_update(ins[3 * k][...], g, ins[3 * k + 1][...], ins[3 * k + 2][...])
            for ref, val in zip(outs[4 * k:4 * k + 4], (g, delta, m2, v2)):
                ref[...] = val

    flat = [a for trio in params for a in trio]
    outs = pl.pallas_call(body, name="adam_square", out_shape=[jax.ShapeDtypeStruct((RSH, D), F32)] * 12,
                          compiler_params=_cparams())(own, recv, *flat)
    return [outs[4 * k:4 * k + 4] for k in range(3)]


def _rope(t, cos, sin):
    lane = lax.broadcasted_iota(jnp.int32, (1, LANES), 1)
    first_half = jnp.bitwise_and(lane, DK // 2) == 0
    partner = jnp.where(first_half, pltpu.roll(t, LANES - DK // 2, 1), pltpu.roll(t, DK // 2, 1))
    return t * cos + partner * sin


class _Shards:
    def __init__(self, arrays, picks):
        self.arrays, self.picks = list(arrays), list(picks)

    def specs(self, index_args):
        out = []
        for a, arr in enumerate(self.arrays):
            steps = [p for p, (ai, _) in enumerate(self.picks) if ai == a]
            first, slot0, last = steps[0], self.picks[steps[0]][1], arr.shape[0] - 1
            out.append(pl.BlockSpec(
                (1, D, WSH), lambda *args, first=first, slot0=slot0, last=last:
                (jnp.clip(index_args(*args) - first + slot0, 0, last), 0, 0)))
        return out

    def __add__(self, other):
        shift = len(self.arrays)
        return _Shards(self.arrays + other.arrays, self.picks + [(a + shift, s) for a, s in other.picks])


def _load_w_in(step, ids_ref, shards, w_refs, w_vmem, sems):
    @pl.when(step == 0)
    def _():
        copies = []
        for p, (a, slot) in enumerate(shards.picks):
            col = pl.multiple_of(ids_ref[p] * WSH, LANES)
            copies.append(pltpu.make_async_copy(w_refs[a].at[slot], w_vmem.at[:, pl.ds(col, WSH)], sems.at[p]))
        for cp in copies:
            cp.start()
        for cp in copies:
            cp.wait()


def _project_columns(xmb, w_ref, shard, cos, sin, p_ref):
    groups = 2 * WSH // LANES
    q0, k0, v0 = CB_QK * D // LANES, (CB_QK * D + N_HEADS * DK) // LANES, CB_V * D // LANES
    mxu_cols = 2 * LANES
    for a in range(0, 2 * WSH, mxu_cols):
        acc = _dot(xmb, w_ref[:, a:a + mxu_cols])
        for col in range(0, mxu_cols, LANES):
            grp = shard * groups + (a + col) // LANES
            rotary = jnp.logical_and(grp >= q0, grp < v0)
            scale = jnp.where(grp >= k0, DK ** -0.5, 1.0)
            piece = acc[:, col:col + LANES]
            turned = _rope(piece * scale, cos, sin)
            p_ref[:, a + col:a + col + LANES] = jnp.where(rotary, turned, piece).astype(BF16)


def _prenorm(x, ctx, modx, modc, norm_w):
    n_x_tiles = x.shape[0] // TM
    lext = x.shape[0] + ctx.shape[0]

    def body(x_ref, ctx_ref, mx_ref, mc_ref, nw_ref, xm_ref, xmt_ref, cos_ref, sin_ref):
        i = pl.program_id(0)
        is_ctx = i >= n_x_tiles
        x = jnp.where(is_ctx, ctx_ref[...], x_ref[...])
        r = lax.rsqrt(jnp.mean(x * x, axis=-1, keepdims=True) + EPS)
        sh = jnp.where(is_ctx, mc_ref[0:1, :], mx_ref[0:1, :])
        sc = jnp.where(is_ctx, mc_ref[1:2, :], mx_ref[1:2, :])
        xm = (x * r * nw_ref[...]) * (1.0 + sc) + sh
        xm_ref[...] = xm.astype(BF16)
        xmt_ref[...] = xm.T.astype(BF16)
        pos = i * TM + lax.broadcasted_iota(jnp.int32, (TM, 1), 0)
        lane = lax.broadcasted_iota(jnp.int32, (1, LANES), 1)
        n_freq = DK // 4
        inv = jnp.exp((lane % n_freq).astype(F32) * (-math.log(ROPE_BASE) / n_freq))
        by_column = (lane % (2 * n_freq)) >= n_freq
        ang = jnp.where(by_column, (pos % GRID_W).astype(F32), (pos // GRID_W).astype(F32)) * inv
        first_half = (lane % DK) < DK // 2
        cos_ref[...] = jnp.where(is_ctx, 1.0, jnp.cos(ang))
        sin_ref[...] = jnp.where(is_ctx, 0.0, jnp.where(first_half, -jnp.sin(ang), jnp.sin(ang)))

    full = lambda a: pl.BlockSpec(a.shape, lambda i: (0,) * a.ndim)
    row = lambda width: pl.BlockSpec((TM, width), lambda i: (i, 0))
    return pl.pallas_call(
        body, name="prenorm", grid=(lext // TM,),
        in_specs=[pl.BlockSpec((TM, D), lambda i: (jnp.minimum(i, n_x_tiles - 1), 0)), full(ctx),
                  full(modx), full(modc), full(norm_w)],
        out_specs=[row(D), pl.BlockSpec((D, TM), lambda i: (0, i)), row(LANES), row(LANES)],
        out_shape=[jax.ShapeDtypeStruct((lext, D), BF16), jax.ShapeDtypeStruct((D, lext), BF16),
                   jax.ShapeDtypeStruct((lext, LANES), F32), jax.ShapeDtypeStruct((lext, LANES), F32)],
        compiler_params=_cparams(dimension_semantics=("arbitrary",)),
    )(x, ctx, modx, modc, norm_w)


def _in_projection(p_ext, xm, cos_t, sin_t, w, pairs, me, name):
    lext = xm.shape[0]
    tr = lext // 4
    meta = []
    for (slot_a, k_a), (slot_b, k_b) in pairs:
        assert k_a ^ k_b == 1
        dev_a = jnp.bitwise_xor(me, k_a)
        a_low = dev_a % 2 == 0
        meta += [jnp.where(a_low, slot_a, slot_b), jnp.where(a_low, slot_b, slot_a), dev_a // 2]
    meta = jnp.stack(meta).astype(jnp.int32)

    def body(meta_ref, xm_ref, cos_ref, sin_ref, w_hbm, *rest):
        p_ref, w_ref, sems = rest[-3:]
        j, i = pl.program_id(0), pl.program_id(1)

        @pl.when(i == 0)
        def _():
            copies = [pltpu.make_async_copy(w_hbm.at[meta_ref[3 * j + h]], w_ref.at[:, h * WSH:(h + 1) * WSH],
                                            sems.at[h]) for h in range(2)]
            for cp in copies:
                cp.start()
            for cp in copies:
                cp.wait()

        _project_columns(xm_ref[...], w_ref, meta_ref[3 * j + 2], cos_ref[...], sin_ref[...], p_ref)

    row = lambda width: pl.BlockSpec((tr, width), lambda j, i, meta_: (i, 0))
    grid_spec = pltpu.PrefetchScalarGridSpec(
        num_scalar_prefetch=1, grid=(len(pairs), lext // tr),
        in_specs=[row(D), row(LANES), row(LANES), pl.BlockSpec(memory_space=pl.ANY)]
        + ([] if p_ext is None else [pl.BlockSpec(memory_space=pl.ANY)]),
        out_specs=pl.BlockSpec((tr, 2 * WSH), lambda j, i, meta_: (i, meta_[3 * j + 2])),
        scratch_shapes=[pltpu.VMEM((D, 2 * WSH), BF16), pltpu.SemaphoreType.DMA((2,))])
    return pl.pallas_call(
        body, name=name, grid_spec=grid_spec,
        out_shape=jax.ShapeDtypeStruct((lext, PW), BF16),
        input_output_aliases={} if p_ext is None else {5: 0},
        compiler_params=_cparams(dimension_semantics=("arbitrary", "arbitrary")),
    )(meta, xm, cos_t, sin_t, w, *([] if p_ext is None else [p_ext]))


def _decay_tables(lgf, lgb, n):
    i = lax.broadcasted_iota(jnp.int32, (n, 1), 0).astype(F32)
    return dict(i=i, k_f=jnp.exp(lgf * (n - 1.0 - i)), k_b=jnp.exp(lgb * i),
                q_f=jnp.exp(lgf * (i + 1.0)), q_b=jnp.exp(lgb * (n - i)))


def _decay_matrix(lgf, lgb, n, transposed=False):
    ii = lax.broadcasted_iota(jnp.int32, (n, n), 0)
    jj = lax.broadcasted_iota(jnp.int32, (n, n), 1)
    diff = (jj - ii if transposed else ii - jj).astype(F32)
    low = jnp.exp(lgf * jnp.maximum(diff, 0.0))
    up = jnp.exp(lgb * jnp.maximum(-diff, 0.0))
    return jnp.where(diff > 0, low, jnp.where(diff < 0, up, 2.0)), diff


def _cat_lanes(a, b):
    return jnp.concatenate([a.astype(BF16), b.astype(BF16)], axis=1)


def _retention_forward(p_ext, lg, seq, ctx_len):
    lext = seq + ctx_len
    n_chunks = seq // RET_C
    C = RET_C

    def body(lg_ref, q_ref, k_ref, v_ref, rn_ref, rstd_ref, kv_scr, sf_scr, sb_scr):
        pair = pl.program_id(0)
        lane = lax.broadcasted_iota(jnp.int32, (1, LANES), 1)
        heads = range(2)
        hmask = [(lane // DK == hh).astype(F32) for hh in heads]
        lgf = [lg_ref[0, 2 * pair + hh] for hh in heads]
        lgb = [lg_ref[1, 2 * pair + hh] for hh in heads]
        vs = [slice(hh * DV, (hh + 1) * DV) for hh in heads]
        t = [_decay_tables(lgf[hh], lgb[hh], C) for hh in heads]
        kc_all = k_ref[seq:lext, :].astype(F32)
        s0 = []
        for hh in heads:
            tc = _decay_tables(lgf[hh], lgb[hh], ctx_len)
            kc = kc_all * hmask[hh]
            s0.append(_dot_tn(_cat_lanes(kc * tc["k_f"], kc * tc["k_b"]), v_ref[seq:lext, vs[hh]]))

        def increments(c, carry):
            rows = pl.ds(pl.multiple_of(c * C, C), C)
            k_all = k_ref[rows, :].astype(F32)
            for hh in heads:
                k = k_all * hmask[hh]
                kv_scr[hh, c] = _dot_tn(_cat_lanes(k * t[hh]["k_f"], k * t[hh]["k_b"]), v_ref[rows, vs[hh]])
            return carry

        lax.fori_loop(0, n_chunks, increments, 0, unroll=2)
        gf_c = [jnp.exp(lgf[hh] * C) for hh in heads]
        gb_c = [jnp.exp(lgb[hh] * C) for hh in heads]

        def scan_f(c, s):
            for hh in heads:
                sf_scr[hh, c] = s[hh]
            return tuple(gf_c[hh] * s[hh] + kv_scr[hh, c, 0:LANES, :] for hh in heads)

        def scan_b(n, s):
            c = n_chunks - 1 - n
            for hh in heads:
                sb_scr[hh, c] = s[hh]
            return tuple(gb_c[hh] * s[hh] + kv_scr[hh, c, LANES:2 * LANES, :] for hh in heads)

        lax.fori_loop(0, n_chunks, scan_f, tuple(s0[hh][0:LANES] for hh in heads))
        lax.fori_loop(0, n_chunks, scan_b, tuple(s0[hh][LANES:2 * LANES] for hh in heads))
        dmat = [_decay_matrix(lgf[hh], lgb[hh], C)[0] for hh in heads]

        def outputs(c, carry):
            rows = pl.ds(pl.multiple_of(c * C, C), C)
            q_all = q_ref[rows, :].astype(F32)
            k = k_ref[rows, :]
            for hh in heads:
                q = q_all * hmask[hh]
                v = v_ref[rows, vs[hh]]
                s = _dot_nt(q.astype(BF16), k)
                inner = _dot((s * dmat[hh]).astype(BF16), v)
                states = jnp.concatenate([sf_scr[hh, c], sb_scr[hh, c]], axis=0).astype(BF16)
                ret = inner + _dot(_cat_lanes(q * t[hh]["q_f"], q * t[hh]["q_b"]), states)
                mu = jnp.mean(ret, axis=-1, keepdims=True)
                cen = ret - mu
                rstd = lax.rsqrt(jnp.mean(cen * cen, axis=-1, keepdims=True) + EPS)
                rn_ref[rows, vs[hh]] = cen * rstd
                rstd_ref[rows, vs[hh]] = jnp.broadcast_to(rstd, (C, DV))
            return carry

        lax.fori_loop(0, n_chunks, outputs, 0, unroll=2)

    qk0 = CB_QK * D // LANES
    return pl.pallas_call(
        body, name="retention_forward", grid=(N_HEADS // 2,),
        in_specs=[pl.BlockSpec(memory_space=pltpu.SMEM),
                  pl.BlockSpec((lext, LANES), lambda g: (0, qk0 + g)),
                  pl.BlockSpec((lext, LANES), lambda g: (0, qk0 + N_HEADS // 2 + g)),
                  pl.BlockSpec((lext, 2 * DV), lambda g: (0, CB_V * D // (2 * DV) + g))],
        out_specs=[pl.BlockSpec((seq, 2 * DV), lambda g: (0, g))] * 2,
        out_shape=[jax.ShapeDtypeStruct((seq, D), F32)] * 2,
        scratch_shapes=[pltpu.VMEM((2, n_chunks, 2 * LANES, DV), F32), pltpu.VMEM((2, n_chunks, LANES, DV), F32),
                        pltpu.VMEM((2, n_chunks, LANES, DV), F32)],
        compiler_params=_cparams(dimension_semantics=("arbitrary",)),
    )(lg, p_ext, p_ext, p_ext)


def _retention_backward(p_ext, dret, lg, cos_t, sin_t, seq, ctx_len):
    lext = seq + ctx_len
    n_chunks = seq // RET_C
    C = RET_C

    def body(lg_ref, q_ref, k_ref, v_ref, do_ref, cos_ref, sin_ref, dq_ref, dk_ref, dv_ref, dlg_ref,
             kv_scr, g_scr, sf_scr, sb_scr, gfn_scr, gbp_scr):
        pair = pl.program_id(0)
        lane = lax.broadcasted_iota(jnp.int32, (1, LANES), 1)
        heads = range(2)
        hmask = [(lane // DK == hh).astype(F32) for hh in heads]
        lgf = [lg_ref[0, 2 * pair + hh] for hh in heads]
        lgb = [lg_ref[1, 2 * pair + hh] for hh in heads]
        vs = [slice(hh * DV, (hh + 1) * DV) for hh in heads]
        t = [_decay_tables(lgf[hh], lgb[hh], C) for hh in heads]
        tc = [_decay_tables(lgf[hh], lgb[hh], ctx_len) for hh in heads]
        kc_all = k_ref[seq:lext, :].astype(F32)
        kc = [kc_all * hmask[hh] for hh in heads]
        vc = [v_ref[seq:lext, vs[hh]] for hh in heads]
        kc_cat = [_cat_lanes(kc[hh] * tc[hh]["k_f"], kc[hh] * tc[hh]["k_b"]) for hh in heads]
        s0 = [_dot_tn(kc_cat[hh], vc[hh]) for hh in heads]

        def increments(c, carry):
            rows = pl.ds(pl.multiple_of(c * C, C), C)
            k_all = k_ref[rows, :].astype(F32)
            q_all = q_ref[rows, :].astype(F32)
            for hh in heads:
                k, q = k_all * hmask[hh], q_all * hmask[hh]
                kv_scr[hh, c] = _dot_tn(_cat_lanes(k * t[hh]["k_f"], k * t[hh]["k_b"]), v_ref[rows, vs[hh]])
                g_scr[hh, c] = _dot_tn(_cat_lanes(q * t[hh]["q_f"], q * t[hh]["q_b"]), do_ref[rows, vs[hh]])
            return carry

        lax.fori_loop(0, n_chunks, increments, 0, unroll=2)
        gf_c = [jnp.exp(lgf[hh] * C) for hh in heads]
        gb_c = [jnp.exp(lgb[hh] * C) for hh in heads]

        def scan_f(c, s):
            for hh in heads:
                sf_scr[hh, c] = s[hh]
            return tuple(gf_c[hh] * s[hh] + kv_scr[hh, c, 0:LANES, :] for hh in heads)

        def scan_b(n, s):
            c = n_chunks - 1 - n
            for hh in heads:
                sb_scr[hh, c] = s[hh]
            return tuple(gb_c[hh] * s[hh] + kv_scr[hh, c, LANES:2 * LANES, :] for hh in heads)

        def scan_gf(n, carry):
            c = n_chunks - 1 - n
            for hh in heads:
                gfn_scr[hh, c] = carry[hh]
            return tuple(g_scr[hh, c, 0:LANES, :] + gf_c[hh] * carry[hh] for hh in heads)

        def scan_gb(c, carry):
            for hh in heads:
                gbp_scr[hh, c] = carry[hh]
            return tuple(g_scr[hh, c, LANES:2 * LANES, :] + gb_c[hh] * carry[hh] for hh in heads)

        lax.fori_loop(0, n_chunks, scan_f, tuple(s0[hh][0:LANES] for hh in heads))
        lax.fori_loop(0, n_chunks, scan_b, tuple(s0[hh][LANES:2 * LANES] for hh in heads))
        zero_state = jnp.zeros((LANES, DV), F32)
        gf_first = lax.fori_loop(0, n_chunks, scan_gf, (zero_state, zero_state))
        gb_last = lax.fori_loop(0, n_chunks, scan_gb, (zero_state, zero_state))

        dmat, w_f, w_b = [], [], []
        for hh in heads:
            dm, diff = _decay_matrix(lgf[hh], lgb[hh], C)
            dmat.append(dm)
            w_f.append(jnp.where(diff > 0, diff * dm, 0.0))
            w_b.append(jnp.where(diff < 0, -diff * dm, 0.0))

        def total(a):
            rows_, width = a.shape
            part = jnp.sum(a.reshape(rows_ // 8, 8, width), axis=0)
            return part[:, 0:LANES] + part[:, LANES:2 * LANES] if width == 2 * LANES else part

        def chunk_head(hh, c, rows, q_all, k_all, dlgf, dlgb):
            th = t[hh]
            qm = q_all * hmask[hh]
            km = k_all * hmask[hh]
            qb, kb = qm.astype(BF16), km.astype(BF16)
            v = v_ref[rows, vs[hh]]
            do = do_ref[rows, vs[hh]]
            s = _dot_nt(qb, kb)
            dsd = _dot_nt(do, v)
            ds = (dsd * dmat[hh]).astype(BF16)
            dq_in = _dot(ds, kb)
            dk_in = _dot_tn(ds, qb)
            dv_in = _dot_tn((s * dmat[hh]).astype(BF16), do)
            prod = s * dsd
            dlgf = dlgf + total(prod * w_f[hh])
            dlgb = dlgb + total(prod * w_b[hh])
            sf, sb = sf_scr[hh, c], sb_scr[hh, c]
            states = jnp.concatenate([sf, sb], axis=0).astype(BF16)
            dqc = _dot_nt(do, states)
            dqf = dqc[:, 0:LANES] * th["q_f"]
            dqb = dqc[:, LANES:2 * LANES] * th["q_b"]
            dq = (dq_in + dqf + dqb) * hmask[hh]
            dlgf = dlgf + total((th["i"] + 1.0) * (qm * dqf))
            dlgb = dlgb + total((C - th["i"]) * (qm * dqb))
            gfn, gbp = gfn_scr[hh, c], gbp_scr[hh, c]
            gstates = jnp.concatenate([gfn, gbp], axis=0).astype(BF16)
            dkc = _dot_nt(v, gstates)
            dkf = dkc[:, 0:LANES] * th["k_f"]
            dkb = dkc[:, LANES:2 * LANES] * th["k_b"]
            dk = dk_in + dkf + dkb
            dlgf = dlgf + total((C - 1.0 - th["i"]) * (km * dkf)) + C * gf_c[hh] * total(gfn * sf)
            dlgb = dlgb + total(th["i"] * (km * dkb)) + C * gb_c[hh] * total(gbp * sb)
            dv = dv_in + _dot(_cat_lanes(km * th["k_f"], km * th["k_b"]), gstates)
            dv_ref[rows, vs[hh]] = dv.astype(BF16)
            return dq, dk, dlgf, dlgb

        def chunk(c, carry):
            rows = pl.ds(pl.multiple_of(c * C, C), C)
            q_all = q_ref[rows, :].astype(F32)
            k_all = k_ref[rows, :].astype(F32)
            dq0, dk0, f0, b0 = chunk_head(0, c, rows, q_all, k_all, carry[0], carry[1])
            dq1, dk1, f1, b1 = chunk_head(1, c, rows, q_all, k_all, carry[2], carry[3])
            cos, sin = cos_ref[rows, :], sin_ref[rows, :]
            dq_ref[rows, :] = _rope(dq0 + dq1, cos, -sin).astype(BF16)
            dk_ref[rows, :] = (_rope(dk0 + dk1, cos, -sin) * (DK ** -0.5)).astype(BF16)
            return f0, b0, f1, b1

        zero = jnp.zeros((8, LANES), F32)
        sums = lax.fori_loop(0, n_chunks, chunk, (zero, zero, zero, zero), unroll=2)

        dlg = []
        dk_ctx = jnp.zeros((ctx_len, LANES), F32)
        for hh in heads:
            g0 = jnp.concatenate([gf_first[hh], gb_last[hh]], axis=0).astype(BF16)
            dkcc = _dot_nt(vc[hh], g0)
            dkcf = dkcc[:, 0:LANES] * tc[hh]["k_f"]
            dkcb = dkcc[:, LANES:2 * LANES] * tc[hh]["k_b"]
            dlgf = sums[2 * hh] + total((ctx_len - 1.0 - tc[hh]["i"]) * (kc[hh] * dkcf))
            dlgb = sums[2 * hh + 1] + total(tc[hh]["i"] * (kc[hh] * dkcb))
            dk_ctx = dk_ctx + (dkcf + dkcb) * (DK ** -0.5)
            dv_ref[seq:lext, vs[hh]] = _dot(kc_cat[hh], g0).astype(BF16)
            dlg += [jnp.sum(jnp.sum(a, axis=1, keepdims=True), axis=0, keepdims=True) for a in (dlgf, dlgb)]
        dk_ref[seq:lext, :] = dk_ctx.astype(BF16)
        dq_ref[seq:lext, :] = jnp.zeros((ctx_len, LANES), BF16)

        lane8 = lax.broadcasted_iota(jnp.int32, (8, LANES), 1)
        out = jnp.zeros((8, LANES), F32)
        for n, val in enumerate(dlg):
            out = jnp.where(lane8 == n, val, out)
        dlg_ref[0] = out

    qk0 = CB_QK * D // LANES
    q_spec = pl.BlockSpec((lext, LANES), lambda g: (0, qk0 + g))
    k_spec = pl.BlockSpec((lext, LANES), lambda g: (0, qk0 + N_HEADS // 2 + g))
    v_spec = pl.BlockSpec((lext, 2 * DV), lambda g: (0, CB_V * D // (2 * DV) + g))
    table = pl.BlockSpec((lext, LANES), lambda g: (0, 0))
    state = pltpu.VMEM((2, n_chunks, LANES, DV), F32)
    return pl.pallas_call(
        body, name="retention_backward", grid=(N_HEADS // 2,),
        in_specs=[pl.BlockSpec(memory_space=pltpu.SMEM), q_spec, k_spec, v_spec,
                  pl.BlockSpec((seq, 2 * DV), lambda g: (0, g)), table, table],
        out_specs=[pl.BlockSpec((lext, LANES), lambda g: (0, g)), pl.BlockSpec((lext, LANES), lambda g: (0, g)),
                   pl.BlockSpec((lext, 2 * DV), lambda g: (0, g)), pl.BlockSpec((1, 8, LANES), lambda g: (g, 0, 0))],
        out_shape=[jax.ShapeDtypeStruct((lext, N_HEADS * DK), BF16), jax.ShapeDtypeStruct((lext, N_HEADS * DK), BF16),
                   jax.ShapeDtypeStruct((lext, D), BF16), jax.ShapeDtypeStruct((N_HEADS // 2, 8, LANES), F32)],
        scratch_shapes=[pltpu.VMEM((2, n_chunks, 2 * LANES, DV), F32), pltpu.VMEM((2, n_chunks, 2 * LANES, DV), F32),
                        state, state, state, state],
        compiler_params=_cparams(dimension_semantics=("arbitrary",)),
    )(lg, p_ext, p_ext, p_ext, dret, cos_t, sin_t)


def _merge(p_ext, rn, rstd, x, target, w_a, w_b, w_out, vecs, seq):
    n_tiles = seq // TM
    hb = TM // HALO

    def body(h_ref, bg_ref, cg_ref, za_ref, zb_ref, ga_ref, gb_ref, hp_ref, hn_ref, cp_ref, cn_ref,
             rn_ref, rstd_ref, x_ref, t_ref, wa_ref, wb_ref, wo_ref, vec_ref,
             dx1_ref, dpa_ref, dconv_ref, dret_ref, at_ref, b_ref, part_ref):
        i = pl.program_id(0)
        f = lambda ref: ref[...].astype(F32)
        h, bg, cg, za, zb, ga, gb = f(h_ref), f(bg_ref), f(cg_ref), f(za_ref), f(zb_ref), f(ga_ref), f(gb_ref)
        gx, w0, w1, w2 = vec_ref[0:1, :], vec_ref[1:2, :], vec_ref[2:3, :], vec_ref[3:4, :]
        cb, gnw, fw = vec_ref[4:5, :], vec_ref[5:6, :], vec_ref[6:7, :]
        u = cg * h
        row = lax.broadcasted_iota(jnp.int32, (TM, 1), 0)
        u_prev = (f(cp_ref) * f(hp_ref))[HALO - 1:HALO, :]
        u_next = (f(cn_ref) * f(hn_ref))[0:1, :]
        u_prev = jnp.where(i == 0, 0.0, u_prev)
        u_next = jnp.where(i == n_tiles - 1, 0.0, u_next)
        u_up = jnp.where(row == 0, u_prev, pltpu.roll(u, 1, 0))
        u_dn = jnp.where(row == TM - 1, u_next, pltpu.roll(u, TM - 1, 0))
        conv = w0 * u_up + w1 * u + w2 * u_dn + cb
        sza = _sigmoid(za)
        silu_za = za * sza
        a_act = silu_za * bg * conv
        rn = rn_ref[...]
        szb = _sigmoid(zb)
        silu_zb = zb * szb
        rg = rn * gnw
        b_act = silu_zb * rg
        y_a = _dot(a_act.astype(BF16), wa_ref[...])
        y_b = _dot(b_act.astype(BF16), wb_ref[...])
        sga, sgb = _sigmoid(ga), _sigmoid(gb)
        mix = sga * y_a + sgb * y_b
        y = _dot(mix.astype(BF16), wo_ref[...])
        x1 = x_ref[...] + gx * y
        r1 = lax.rsqrt(jnp.mean(x1 * x1, axis=-1, keepdims=True) + EPS)
        xh1 = x1 * r1
        err = xh1 * fw - t_ref[...]
        loss = jnp.sum(jnp.sum(err * err, axis=1, keepdims=True), axis=0, keepdims=True) * (0.5 / D)
        dout = err * (1.0 / D)
        dxh = dout * fw
        dx1 = r1 * (dxh - xh1 * jnp.mean(dxh * xh1, axis=-1, keepdims=True))
        dx1_ref[...] = dx1
        dy = (dx1 * gx).astype(BF16)
        dmix = _dot_nt(dy, wo_ref[...])
        dya = (dmix * sga).astype(BF16)
        dyb = (dmix * sgb).astype(BF16)
        da = _dot_nt(dya, wa_ref[...])
        db = _dot_nt(dyb, wb_ref[...])
        dpa_ref[:, 0:D] = (da * silu_za * conv).astype(BF16)
        dpa_ref[:, D:2 * D] = (da * bg * conv * (sza * (1.0 + za * (1.0 - sza)))).astype(BF16)
        dpa_ref[:, 2 * D:3 * D] = (db * rg * (szb * (1.0 + zb * (1.0 - szb)))).astype(BF16)
        dpa_ref[:, 3 * D:4 * D] = (dmix * y_a * sga * (1.0 - sga)).astype(BF16)
        dpa_ref[:, 4 * D:5 * D] = (dmix * y_b * sgb * (1.0 - sgb)).astype(BF16)
        dconv_ref[...] = (da * silu_za * bg).astype(BF16)
        drn_n = db * silu_zb
        drn = drn_n * gnw
        rstd = rstd_ref[...]
        for hd in range(N_HEADS):
            sl = slice(hd * DV, (hd + 1) * DV)
            dh_, rh = drn[:, sl], rn[:, sl]
            m1 = jnp.mean(dh_, axis=-1, keepdims=True)
            m2 = jnp.mean(dh_ * rh, axis=-1, keepdims=True)
            dret_ref[:, sl] = (rstd[:, sl] * (dh_ - m1 - rh * m2)).astype(BF16)
        at_ref[0] = a_act.T.astype(BF16)
        at_ref[1] = b_act.T.astype(BF16)
        at_ref[2] = mix.T.astype(BF16)
        b_ref[0] = dya
        b_ref[1] = dyb
        b_ref[2] = dy

        @pl.when(i == 0)
        def _():
            part_ref[...] = jnp.zeros((8, D), F32)

        part_ref[0:1, :] += jnp.sum(dout * xh1, axis=0, keepdims=True)
        part_ref[1:2, :] += jnp.sum(dx1 * y, axis=0, keepdims=True)
        part_ref[2:3, :] += jnp.sum(drn_n * rn, axis=0, keepdims=True)
        part_ref[3:4, :] += jnp.broadcast_to(loss, (1, D))

    col = lambda cb_: pl.BlockSpec((TM, D), lambda i, cb_=cb_: (i, cb_))
    prev = lambda cb_: pl.BlockSpec((HALO, D), lambda i, cb_=cb_: (jnp.maximum(i * hb - 1, 0), cb_))
    nxt = lambda cb_: pl.BlockSpec((HALO, D), lambda i, cb_=cb_: (jnp.minimum((i + 1) * hb, n_tiles * hb - 1), cb_))
    tile = pl.BlockSpec((TM, D), lambda i: (i, 0))
    full = lambda a: pl.BlockSpec(a.shape, lambda i: (0,) * a.ndim)
    return pl.pallas_call(
        body, name="merge", grid=(n_tiles,),
        in_specs=[col(CB_H), col(CB_BG), col(CB_CG), col(CB_ZA), col(CB_ZB), col(CB_GA), col(CB_GB),
                  prev(CB_H), nxt(CB_H), prev(CB_CG), nxt(CB_CG),
                  tile, tile, tile, tile, full(w_a), full(w_b), full(w_out), full(vecs)],
        out_specs=[tile, pl.BlockSpec((TM, 5 * D), lambda i: (i, 0)), tile, tile,
                   pl.BlockSpec((3, D, TM), lambda i: (0, 0, i)), pl.BlockSpec((3, TM, D), lambda i: (0, i, 0)),
                   pl.BlockSpec((8, D), lambda i: (0, 0))],
        out_shape=[jax.ShapeDtypeStruct((seq, D), F32), jax.ShapeDtypeStruct((seq, 5 * D), BF16),
                   jax.ShapeDtypeStruct((seq, D), BF16), jax.ShapeDtypeStruct((seq, D), BF16),
                   jax.ShapeDtypeStruct((3, D, seq), BF16), jax.ShapeDtypeStruct((3, seq, D), BF16),
                   jax.ShapeDtypeStruct((8, D), F32)],
        compiler_params=_cparams(dimension_semantics=("arbitrary",)),
    )(p_ext, p_ext, p_ext, p_ext, p_ext, p_ext, p_ext, p_ext, p_ext, p_ext, p_ext,
      rn, rstd, x, target, w_a, w_b, w_out, vecs)


def _conv_backward(p_ext, dconv, dpa, dq, dk, dv, vecs, seq):
    n_tiles = seq // TM
    hb = TM // HALO
    qk = N_HEADS * DK

    def body(h_ref, cg_ref, dc_ref, dcp_ref, dcn_ref, dpa_ref, dq_ref, dk_ref, dv_ref, vec_ref, dp_ref, part_ref):
        i = pl.program_id(0)
        dp_ref[:, CB_QK * D:CB_QK * D + qk] = dq_ref[...]
        dp_ref[:, CB_QK * D + qk:CB_V * D] = dk_ref[...]
        dp_ref[:, CB_V * D:CB_ZB * D] = dv_ref[...]

        @pl.when(i == 0)
        def _():
            part_ref[...] = jnp.zeros((8, D), F32)

        @pl.when(i == n_tiles)
        def _():
            dp_ref[:, 0:CB_QK * D] = jnp.zeros((TM, CB_QK * D), BF16)
            dp_ref[:, CB_ZB * D:PW] = jnp.zeros((TM, PW - CB_ZB * D), BF16)

        @pl.when(i < n_tiles)
        def _():
            f = lambda ref: ref[...].astype(F32)
            h, cg, dc = f(h_ref), f(cg_ref), f(dc_ref)
            w0, w1, w2 = vec_ref[1:2, :], vec_ref[2:3, :], vec_ref[3:4, :]
            row = lax.broadcasted_iota(jnp.int32, (TM, 1), 0)
            dc_prev = jnp.where(i == 0, 0.0, f(dcp_ref)[HALO - 1:HALO, :])
            dc_next = jnp.where(i == n_tiles - 1, 0.0, f(dcn_ref)[0:1, :])
            dc_up = jnp.where(row == 0, dc_prev, pltpu.roll(dc, 1, 0))
            dc_dn = jnp.where(row == TM - 1, dc_next, pltpu.roll(dc, TM - 1, 0))
            du = w0 * dc_dn + w1 * dc + w2 * dc_up
            u = cg * h
            dp_ref[:, CB_H * D:(CB_H + 1) * D] = (du * cg).astype(BF16)
            dp_ref[:, CB_CG * D:(CB_CG + 1) * D] = (du * h).astype(BF16)
            dp_ref[:, CB_BG * D:(CB_BG + 1) * D] = dpa_ref[:, 0:D]
            dp_ref[:, CB_ZA * D:(CB_ZA + 1) * D] = dpa_ref[:, D:2 * D]
            dp_ref[:, CB_ZB * D:(CB_GB + 1) * D] = dpa_ref[:, 2 * D:5 * D]
            part_ref[0:1, :] += jnp.sum(u * dc_dn, axis=0, keepdims=True)
            part_ref[1:2, :] += jnp.sum(u * dc, axis=0, keepdims=True)
            part_ref[2:3, :] += jnp.sum(u * dc_up, axis=0, keepdims=True)
            part_ref[3:4, :] += jnp.sum(dc, axis=0, keepdims=True)

    last = n_tiles - 1
    col = lambda cb_: pl.BlockSpec((TM, D), lambda i, cb_=cb_: (jnp.minimum(i, last), cb_))
    lat = lambda w: pl.BlockSpec((TM, w), lambda i: (jnp.minimum(i, last), 0))
    ext = lambda w: pl.BlockSpec((TM, w), lambda i: (i, 0))
    return pl.pallas_call(
        body, name="conv_backward", grid=(n_tiles + 1,),
        in_specs=[col(CB_H), col(CB_CG), lat(D),
                  pl.BlockSpec((HALO, D), lambda i: (jnp.clip(i * hb - 1, 0, n_tiles * hb - 1), 0)),
                  pl.BlockSpec((HALO, D), lambda i: (jnp.minimum((i + 1) * hb, n_tiles * hb - 1), 0)),
                  lat(5 * D), ext(qk), ext(qk), ext(D),
                  pl.BlockSpec(vecs.shape, lambda i: (0, 0))],
        out_specs=[ext(PW), pl.BlockSpec((8, D), lambda i: (0, 0))],
        out_shape=[jax.ShapeDtypeStruct((seq + TM, PW), BF16), jax.ShapeDtypeStruct((8, D), F32)],
        compiler_params=_cparams(dimension_semantics=("arbitrary",)),
    )(p_ext, p_ext, dconv, dconv, dconv, dpa, dq, dk, dv, vecs)


def _input_backward(dp, shards, ids, x, ctx, dx1, modx, modc, norm_w):
    seq = x.shape[0]
    lext = seq + ctx.shape[0]
    n_x = seq // TM
    n_w = len(shards.arrays)

    def body(ids_ref, dp_ref, x_ref, ctx_ref, dx1_ref, mx_ref, mc_ref, nw_ref, *rest):
        w_hbm, (gx_ref, part_ref, w_ref, w_sems) = rest[:n_w], rest[n_w:]
        i = pl.program_id(0)
        _load_w_in(i, ids_ref, shards, w_hbm, w_ref, w_sems)
        is_ctx = i >= n_x
        dxm = _dot_nt(dp_ref[...], w_ref[...])
        x = jnp.where(is_ctx, ctx_ref[...], x_ref[...])
        r = lax.rsqrt(jnp.mean(x * x, axis=-1, keepdims=True) + EPS)
        xh = x * r
        nw = nw_ref[...]
        sc = jnp.where(is_ctx, mc_ref[1:2, :], mx_ref[1:2, :])
        dxn = dxm * (1.0 + sc)
        dxh = dxn * nw
        dx = r * (dxh - xh * jnp.mean(dxh * xh, axis=-1, keepdims=True))

        @pl.when(jnp.logical_not(is_ctx))
        def _():
            gx_ref[...] = dx1_ref[...] + dx

        @pl.when(i == 0)
        def _():
            part_ref[...] = jnp.zeros((8, D), F32)

        fx = jnp.where(is_ctx, 0.0, 1.0)
        d_shift = jnp.sum(dxm, axis=0, keepdims=True)
        d_scale = jnp.sum(dxm * (xh * nw), axis=0, keepdims=True)
        part_ref[0:1, :] += fx * d_shift
        part_ref[1:2, :] += fx * d_scale
        part_ref[2:3, :] += jnp.sum(dxn * xh, axis=0, keepdims=True)
        part_ref[3:4, :] += (1.0 - fx) * d_shift
        part_ref[4:5, :] += (1.0 - fx) * d_scale

    lat = lambda w: pl.BlockSpec((TM, w), lambda i, ids_: (jnp.minimum(i, n_x - 1), 0))
    ext = lambda w: pl.BlockSpec((TM, w), lambda i, ids_: (i, 0))
    full = lambda a: pl.BlockSpec(a.shape, lambda i, ids_: (0,) * a.ndim)
    grid_spec = pltpu.PrefetchScalarGridSpec(
        num_scalar_prefetch=1, grid=(lext // TM,),
        in_specs=[ext(PW), lat(D), full(ctx), lat(D), full(modx), full(modc), full(norm_w)]
        + [pl.BlockSpec(memory_space=pl.ANY)] * n_w,
        out_specs=[lat(D), pl.BlockSpec((8, D), lambda i, ids_: (0, 0))],
        scratch_shapes=[pltpu.VMEM((D, PW), BF16), pltpu.SemaphoreType.DMA((N_DEV,))])
    return pl.pallas_call(
        body, name="input_backward", grid_spec=grid_spec,
        out_shape=[jax.ShapeDtypeStruct((seq, D), F32), jax.ShapeDtypeStruct((8, D), F32)],
        compiler_params=_cparams(dimension_semantics=("arbitrary",)),
    )(ids, dp, x, ctx, dx1, modx, modc, norm_w, *shards.arrays)


def _weight_grad_in(xm_t, dp, owners, name):
    lext = xm_t.shape[1]

    def body(own_ref, a_ref, b_ref, o_ref):
        o_ref[0] = _dot(a_ref[...], b_ref[...])

    grid_spec = pltpu.PrefetchScalarGridSpec(
        num_scalar_prefetch=1, grid=(4,),
        in_specs=[pl.BlockSpec((D, lext), lambda j, own: (0, 0)),
                  pl.BlockSpec((lext, WSH), lambda j, own: (0, own[j]))],
        out_specs=pl.BlockSpec((1, D, WSH), lambda j, own: (j, 0, 0)))
    return pl.pallas_call(
        body, name=name, grid_spec=grid_spec,
        out_shape=jax.ShapeDtypeStruct((4, D, WSH), F32),
        compiler_params=_cparams(dimension_semantics=("arbitrary",)),
    )(owners, xm_t, dp)


def _weight_grad_square(a_t, b):
    seq = a_t.shape[2]

    def body(a_ref, b_ref, o_ref):
        o_ref[:, 0] = _dot(a_ref[0], b_ref[0]).reshape(N_DEV, RSH, D)

    return pl.pallas_call(
        body, name="weight_grad_square", grid=(3,),
        in_specs=[pl.BlockSpec((1, D, seq), lambda t: (t, 0, 0)), pl.BlockSpec((1, seq, D), lambda t: (t, 0, 0))],
        out_specs=pl.BlockSpec((N_DEV, 1, RSH, D), lambda t: (0, t, 0, 0)),
        out_shape=jax.ShapeDtypeStruct((N_DEV, 3, RSH, D), F32),
        compiler_params=_cparams(dimension_semantics=("arbitrary",)),
    )(a_t, b)


def _tie(value, token):
    return value + token[0, 0].astype(value.dtype)


def _pad_lanes(a):
    return jnp.pad(a, ((0, 0), (0, D - a.shape[1])))


def _rows(total, *parts):
    width = max(a.shape[1] for _, a in parts)
    out = None
    for row, a in parts:
        padded = jnp.pad(a, ((row, total - row - a.shape[0]), (0, width - a.shape[1])))
        out = padded if out is None else out + padded
    return out


def kernel(x, c, ctx, c_ctx, norm_w, ada_w, ada_b, w_in, conv_w, conv_b, decay_logit, gn_w, w_a, w_b, w_out, final_norm_w, loss_target, m_c_ctx, m_norm_w, m_ada_w, m_ada_b, m_w_in, m_conv_w, m_conv_b, m_decay_logit, m_gn_w, m_w_a, m_w_b, m_w_out, m_final_norm_w, v_c_ctx, v_norm_w, v_ada_w, v_ada_b, v_w_in, v_conv_w, v_conv_b, v_decay_logit, v_gn_w, v_w_a, v_w_b, v_w_out, v_final_norm_w):
    xi, yi, ci = _coords()
    me = 4 * xi + 2 * yi + ci
    chip = 2 * xi + yi
    seq, ctx_len = x.shape[1], ctx.shape[1]
    assert seq % TM == 0 and seq % RET_C == 0 and ctx_len == TM and seq % GRID_W == 0
    csh = D // N_DEV

    blk = jnp.pad(c, ((0, 7), (0, 0))) + jnp.pad(conv_w[0], ((1, 4), (0, D - csh)))
    got, _ = _all_gather_small(blk, "gather_cond")
    conv_w_all = got[:, 1:4, 0:csh].transpose(1, 0, 2).reshape(3, D)
    c16 = _rows(16, (0, got[:, 0, :]), (N_DEV, c_ctx[None]))
    ada_b_sh = lax.dynamic_slice(ada_b, (0, me * ADA_SH), (1, ADA_SH))
    mod_sh, act16, lg = _modulation(c16, ada_w[0], ada_b_sh, decay_logit[0])
    mod_all, small_done = _all_gather_small(mod_sh, "gather_mod")
    mod_all = mod_all.transpose(1, 0, 2).reshape(16, 3 * D)
    modx = lax.dynamic_slice(mod_all, (me, 0), (1, 3 * D)).reshape(3, D)
    modc = mod_all[8].reshape(3, D)

    x2, tgt = x[0], loss_target[0]
    wblock = lambda n: jax.ShapeDtypeStruct((n, D, WSH), BF16)
    ids_of = lambda ks: jnp.stack([jnp.bitwise_xor(me, k) for k in ks]).astype(jnp.int32)
    own_shard = w_in[0].astype(BF16)[None]
    sq_shards = [w[0].astype(BF16) for w in (w_a, w_b, w_out)]
    wave1 = _Exchange(
        "gather_in_first", [jnp.broadcast_to(own_shard, (2, D, WSH)), own_shard], [wblock(4)], 3,
        lambda srcs, lands: _forward_plan([(0, 1)])(srcs, lands) + _send_plan((4, 2))(srcs[1:], lands),
        after=small_done)
    modx = _tie(modx, wave1.token)
    xm, xm_t, cos_t, sin_t = _prenorm(x2, ctx[0], modx, modc, norm_w)
    (w_pair, _), _ = wave1.wait(xm_t, 1, through=[0])
    first = _Shards([w_pair], [(0, 0), (0, 1)])
    p_ext = _in_projection(None, xm, cos_t, sin_t, w_pair, [((0, 0), (1, 1))], me, "in_projection_pair")
    _, (w_nbr,) = wave1.wait(p_ext, through=[1, 2])
    wave2 = _Exchange(
        "gather_in_second", [w_nbr, own_shard], [wblock(2)], 3,
        lambda srcs, lands: _forward_plan([(0, 2), (1, 3)])(srcs, lands) + _send_plan((6,))(srcs[1:], lands))
    (w_nbr, _), _ = wave2.wait(wave2.token, 2, through=[0])
    second = _Shards([w_nbr], [(0, 0), (0, 1), (0, 2), (0, 3)])
    p_ext = _in_projection(p_ext, xm, cos_t, sin_t, w_nbr, [((0, 4), (2, 5)), ((1, 2), (3, 3))], me,
                           "in_projection_neighbours")
    _, (w_diag,) = wave2.wait(p_ext, through=[1, 2])
    wave3 = _Exchange(
        "gather_in_third", [w_diag] + sq_shards, [jax.ShapeDtypeStruct((N_DEV, RSH, D), BF16)] * 3, 1 + 3 * (N_DEV - 1),
        lambda srcs, lands: _forward_plan([(0, 1)])(srcs, lands) + _gather_plan(srcs[1:], lands))
    (w_diag, *_), _ = wave3.wait(wave3.token, 1, through=[0])
    third = _Shards([w_diag], [(0, 0), (0, 1)])
    p_ext = _in_projection(p_ext, xm, cos_t, sin_t, w_diag, [((0, 6), (1, 7))], me, "in_projection_diagonal")
    w_shards, w_ids = first + second + third, ids_of((0, 1, 4, 2, 5, 3, 6, 7))

    rn, rstd = _retention_forward(p_ext, lg, seq, ctx_len)
    _, sq_lands = wave3.wait(rstd, through=range(1, 7))
    w_a_all, w_b_all, w_out_all = (
        lax.dynamic_update_slice(land, shard[None], (me, 0, 0)).reshape(D, D) for land, shard in zip(sq_lands, sq_shards))
    vecs = _rows(8, (0, modx[2:3]), (1, conv_w_all), (4, conv_b), (5, gn_w), (6, final_norm_w[None]))
    dx1, dpa, dconv, dret, op_at, op_b, part_m = _merge(p_ext, rn, rstd, x2, tgt, w_a_all, w_b_all, w_out_all, vecs, seq)

    j4 = jnp.arange(4, dtype=jnp.int32)
    owners = (2 * jnp.bitwise_xor(chip, j4) + ci).astype(jnp.int32)
    owners_sib = (2 * jnp.bitwise_xor(chip, j4) + (1 - ci)).astype(jnp.int32)
    gw_sq = _weight_grad_square(op_at, op_b).reshape(N_DEV, 3 * RSH, D)
    rs_sq_pair = _Exchange("rs_square_pair", [gw_sq], [jax.ShapeDtypeStruct((4, 3 * RSH, D), F32)], 4,
                           _pair_plan(lambda j, chip_, c_: 2 * jnp.bitwise_xor(chip_, j) + (1 - c_)))
    dq, dk, dv, dlg = _retention_backward(p_ext, dret, _tie(lg, rs_sq_pair.token), cos_t, sin_t, seq, ctx_len)
    (gw_sq,), (r1_sq,) = rs_sq_pair.wait(dlg)
    own_sq, send_sq = _pair_sum(gw_sq, r1_sq, owners, "pair_sum_square")
    rs_sq_chips = _Exchange("rs_square_chips", [send_sq], [jax.ShapeDtypeStruct((3, 3 * RSH, D), BF16)], 3, _chips_plan)
    dp, part_c = _conv_backward(p_ext, dconv, dpa, dq, dk, dv, _tie(vecs, rs_sq_chips.token), seq)
    gw_sib = _weight_grad_in(xm_t, dp, owners_sib, "weight_grad_in_sibling")
    rs_in_pair = _Exchange("rs_in_pair", [gw_sib], [jax.ShapeDtypeStruct((4, D, WSH), F32)], 4,
                           _pair_plan(lambda j, chip_, c_: j))
    gw_own = _weight_grad_in(xm_t, dp, _tie(owners, rs_in_pair.token), "weight_grad_in_own")
    _, (r1_in,) = rs_in_pair.wait(gw_own)
    own_in, send_in = _pair_sum(gw_own, r1_in, j4, "pair_sum_in")
    rs_in_chips = _Exchange("rs_in_chips", [send_in], [jax.ShapeDtypeStruct((3, D, WSH), BF16)], 3, _chips_plan)
    grad_x, part_i = _input_backward(dp, w_shards, w_ids, x2, ctx[0], dx1, _tie(modx, rs_in_chips.token), modc, norm_w)

    dl = dlg[:, 0, 0:4]
    dlg_row = jnp.pad(dl[:, 0::2].reshape(1, N_HEADS), ((0, 0), (0, D - N_HEADS))) + jnp.pad(
        dl[:, 1::2].reshape(1, N_HEADS), ((0, 0), (N_HEADS, D - 2 * N_HEADS)))
    partials = _rows(16, (0, part_m[0:1]), (1, part_i[2:3]), (2, part_c[3:4]), (3, part_m[2:3]), (4, part_c[0:3]),
                     (7, part_i[0:2]), (9, part_m[1:2]), (10, part_i[3:5]), (13, part_m[3:4]), (14, dlg_row))
    got, _ = _all_gather_small(partials, "gather_partials")
    tot = _sum_devices(got, "sum_partials")
    dmodc = tot[10:13].reshape(1, 3 * D)
    dmod16 = _rows(16, (0, got[:, 7:10, :].reshape(N_DEV, 3 * D)), (N_DEV, dmodc))
    dmod16 = lax.dynamic_slice(dmod16, (0, me * ADA_SH), (16, ADA_SH))
    dmodc8 = _rows(8, (0, dmod16[8:9]))
    g_ada_w, d_ada_w, nm_ada_w, nv_ada_w, cctx_part = _ada_backward(
        act16.T, dmod16, dmodc8, ada_w[0], m_ada_w[0], v_ada_w[0])
    dact_ctx = _sum_devices(_all_gather_small(cctx_part, "gather_cctx")[0], "sum_cctx")

    view = {"c_ctx": (1, D), "final_norm_w": (1, D), "decay_logit": (1, 2 * N_HEADS), "conv_w": (3, csh)}
    given = {"c_ctx": (c_ctx, m_c_ctx, v_c_ctx), "norm_w": (norm_w, m_norm_w, v_norm_w),
             "conv_b": (conv_b, m_conv_b, v_conv_b), "gn_w": (gn_w, m_gn_w, v_gn_w),
             "final_norm_w": (final_norm_w, m_final_norm_w, v_final_norm_w), "ada_b": (ada_b, m_ada_b, v_ada_b),
             "decay_logit": (decay_logit, m_decay_logit, v_decay_logit), "conv_w": (conv_w, m_conv_w, v_conv_w)}
    small_in = [tuple(a.reshape(view.get(name, a.shape)) for a in given[name]) for name in SMALL_PARAMS]
    conv_w_grad = lax.dynamic_slice(tot, (4, me * csh), (3, csh))
    small_out = _adam_small(tot, dact_ctx, conv_w_grad, small_in)
    results = {name: [o.reshape(given[name][0].shape) for o in outs_]
               for name, outs_ in zip(SMALL_PARAMS, small_out)}

    _, (r2_sq,) = rs_sq_chips.wait(small_out[0][0])
    _, (r2_in,) = rs_in_chips.wait(small_out[0][0])
    results["w_in"] = [o[None] for o in _adam_sharded(w_in[0], m_w_in[0], v_w_in[0], own_in, r2_in, "adam_w_in")]
    square = _adam_square([(w_a[0], m_w_a[0], v_w_a[0]), (w_b[0], m_w_b[0], v_w_b[0]),
                           (w_out[0], m_w_out[0], v_w_out[0])], own_sq, r2_sq)
    for name, outs_ in zip(("w_a", "w_b", "w_out"), square):
        results[name] = [o[None] for o in outs_]
    results["ada_w"] = [o[None] for o in (g_ada_w, d_ada_w, nm_ada_w, nv_ada_w)]

    order = ("c_ctx", "norm_w", "ada_w", "ada_b", "w_in", "conv_w", "conv_b", "decay_logit", "gn_w",
             "w_a", "w_b", "w_out", "final_norm_w")
    outs = [results[name][kind] for kind in range(4) for name in order]
    return (tot[13, 0], grad_x[None], *outs)
```

```python
import functools
import math

import jax
import jax.numpy as jnp
from jax import lax
from jax.experimental import pallas as pl
from jax.experimental.pallas import tpu as pltpu

F32 = jnp.float32
BF16 = jnp.bfloat16
MESH = pl.DeviceIdType.MESH
_HBM_SPEC = pl.BlockSpec(memory_space=pltpu.HBM)
_SEM_SPEC = pl.BlockSpec(memory_space=pltpu.SEMAPHORE)
_EFFECT = pltpu.SideEffectType.DATAFLOW_SIDE_EFFECTING

N_DEV = 8
D = 1024
N_HEADS = 8
DK = 64
DV = 128
GRID_W = 64
ROPE_BASE = 10000.0
EPS = 1e-6
PW = 9 * D
WSH = PW // N_DEV
RSH = D // N_DEV
ADA_SH = 3 * D // N_DEV
TM = 256
RET_C = 256
HALO = 16
LANES = 128
VMEM_LIMIT = 60 * 1024 * 1024

ADAM_LR = 0.001
ADAM_B1 = 0.9
ADAM_B2 = 0.999
ADAM_EPS = 1e-08
ADAM_WD = 0.01
ADAM_STEP = 10

CB_H, CB_BG, CB_CG, CB_ZA, CB_QK, CB_V, CB_ZB, CB_GA, CB_GB = range(9)


def _cparams(**kw):
    return pltpu.CompilerParams(vmem_limit_bytes=VMEM_LIMIT, **kw)


def _dot(a, b):
    return jnp.dot(a, b, preferred_element_type=F32)


def _dot_nt(a, b):
    return lax.dot_general(a, b, (((1,), (1,)), ((), ())), preferred_element_type=F32)


def _dot_tn(a, b):
    return lax.dot_general(a, b, (((0,), (0,)), ((), ())), preferred_element_type=F32)


def _sigmoid(z):
    return 0.5 * jnp.tanh(0.5 * z) + 0.5


def _row_tile(rows):
    return TM if rows % TM == 0 else rows


def _coords():
    return lax.axis_index("x"), lax.axis_index("y"), lax.axis_index("c")


def _flip(v, bit):
    return 1 - v if bit else v


def _all_gather_small(blk, name):
    rows, cols = blk.shape

    def body(x_ref, out_ref, done, send_sems, recv_sems, local_sem):
        x, y, c = _coords()
        me = 4 * x + 2 * y + c
        mine = pltpu.make_async_copy(x_ref, out_ref.at[me], local_sem)
        mine.start()

        def copy(k, slot):
            peer = (_flip(x, k & 4), _flip(y, k & 2), _flip(c, k & 1))
            return pltpu.make_async_remote_copy(
                src_ref=x_ref, dst_ref=out_ref.at[slot], send_sem=send_sems.at[k - 1],
                recv_sem=recv_sems.at[k - 1], device_id=peer, device_id_type=MESH)

        for k in range(1, N_DEV):
            copy(k, me).start()
        for k in range(1, N_DEV):
            copy(k, jnp.bitwise_xor(me, k)).wait_recv()
        for k in range(1, N_DEV):
            copy(k, me).wait_send()
        mine.wait()
        done[...] = jnp.zeros((8, LANES), F32)

    vmem = pl.BlockSpec(memory_space=pltpu.VMEM)
    return pl.pallas_call(
        body, name=name,
        out_shape=[jax.ShapeDtypeStruct((N_DEV, rows, cols), blk.dtype), jax.ShapeDtypeStruct((8, LANES), F32)],
        in_specs=[vmem], out_specs=[vmem, vmem],
        scratch_shapes=[pltpu.SemaphoreType.DMA((N_DEV - 1,)), pltpu.SemaphoreType.DMA((N_DEV - 1,)),
                        pltpu.SemaphoreType.DMA],
    )(blk)


class _Absent:
    at = property(lambda self: self)

    def __getitem__(self, index):
        return self


class _Exchange:
    def __init__(self, name, srcs, land_shapes, n_copies, plan, after=None):
        self.name, self.plan, self.n_copies = name, plan, n_copies
        self.n_src, self.n_land = len(srcs), len(land_shapes)
        hbm = lambda a: pltpu.HBM(a.shape, a.dtype)
        n = self.n_src + self.n_land
        lands = [pltpu.with_memory_space_constraint(lax.empty(s.shape, s.dtype), pltpu.HBM) for s in land_shapes]
        srcs = [pltpu.with_memory_space_constraint(s, pltpu.HBM) for s in srcs]
        extra = [] if after is None else [after]

        def body(*refs):
            send_sems, recv_sems = refs[n + len(extra)], refs[n + len(extra) + 1]
            for cp in self._copies(refs, send_sems, recv_sems):
                cp.start()
            refs[-1][...] = jnp.zeros((8, LANES), F32)

        outs = pl.pallas_call(
            body, name=name + "_start",
            out_shape=(pltpu.SemaphoreType.DMA((n_copies,)), pltpu.SemaphoreType.DMA((n_copies,)),
                       *[hbm(a) for a in srcs], *[hbm(a) for a in lands], jax.ShapeDtypeStruct((8, LANES), F32)),
            in_specs=[_HBM_SPEC] * n + [pl.BlockSpec(memory_space=pl.ANY)] * len(extra),
            out_specs=(_SEM_SPEC, _SEM_SPEC, *[_HBM_SPEC] * n, pl.BlockSpec(memory_space=pltpu.VMEM)),
            input_output_aliases={i: 2 + i for i in range(n)},
            compiler_params=pltpu.CompilerParams(has_side_effects=_EFFECT),
        )(*srcs, *lands, *extra)
        self.send_sems, self.recv_sems = outs[0], outs[1]
        self.buffers = list(outs[2:2 + n])
        self.token = outs[-1]
        self.waited = 0

    def _copies(self, refs, send_sems, recv_sems, lo=0, hi=None):
        src_refs, land_refs = refs[:self.n_src], refs[self.n_src:self.n_src + self.n_land]
        planned = list(enumerate(self.plan(src_refs, land_refs)))[lo:hi]
        return [pltpu.make_async_remote_copy(src_ref=s, dst_ref=d, send_sem=send_sems.at[k], recv_sem=recv_sems.at[k],
                                             device_id=dev, device_id_type=MESH) for k, (s, d, dev) in planned]

    def wait(self, after, count=None, through=None):
        n = self.n_src + self.n_land
        lo = self.waited
        hi = self.n_copies if count is None else lo + count
        self.waited = hi
        through = list(range(n)) if through is None else list(through)
        m = len(through)

        def body(*refs):
            held = [_Absent()] * n
            for pos, ref in zip(through, refs[:m]):
                held[pos] = ref
            for cp in self._copies(held, refs[m], refs[m + 1], lo, hi):
                cp.wait_send()
                cp.wait_recv()

        outs = pl.pallas_call(
            body, name=f"{self.name}_wait{lo}" if lo or hi < self.n_copies else self.name + "_wait",
            out_shape=tuple(pltpu.HBM(self.buffers[p].shape, self.buffers[p].dtype) for p in through),
            in_specs=[_HBM_SPEC] * m + [_SEM_SPEC, _SEM_SPEC, pl.BlockSpec(memory_space=pl.ANY)],
            out_specs=tuple([_HBM_SPEC] * m),
            input_output_aliases={i: i for i in range(m)},
            compiler_params=pltpu.CompilerParams(has_side_effects=_EFFECT),
        )(*[self.buffers[p] for p in through], self.send_sems, self.recv_sems, after)
        for pos, out in zip(through, outs):
            self.buffers[pos] = out
        return list(self.buffers[:self.n_src]), list(self.buffers[self.n_src:])


def _pair_plan(src_index):
    def plan(srcs, lands):
        x, y, c = _coords()
        chip = 2 * x + y
        return [(srcs[0].at[src_index(j, chip, c)], lands[0].at[j], (x, y, 1 - c)) for j in range(4)]
    return plan


def _chips_plan(srcs, lands):
    x, y, c = _coords()
    return [(srcs[0].at[j - 1], lands[0].at[j - 1], (_flip(x, j & 2), _flip(y, j & 1), c)) for j in range(1, 4)]


def _peer(k):
    x, y, c = _coords()
    return _flip(x, k & 4), _flip(y, k & 2), _flip(c, k & 1)


def _send_plan(ks):
    def plan(srcs, lands):
        return [(srcs[0].at[0], lands[0].at[p], _peer(k)) for p, k in enumerate(ks)]
    return plan


def _forward_plan(moves):
    def plan(srcs, lands):
        return [(srcs[0].at[s], srcs[0].at[d], _peer(1)) for s, d in moves]
    return plan


def _gather_plan(srcs, lands):
    x, y, c = _coords()
    me = 4 * x + 2 * y + c
    return [(srcs[a], lands[a].at[me], (_flip(x, k & 4), _flip(y, k & 2), _flip(c, k & 1)))
            for a in range(len(srcs)) for k in range(1, N_DEV)]


def _pair_sum(grad, recv, owners, name):
    _, rows, cols = grad.shape
    tr = _row_tile(rows)

    def body(own_ref, g_ref, r_ref, mine_ref, send_ref):
        j = pl.program_id(1)
        total = g_ref[...] + r_ref[...]

        @pl.when(j == 0)
        def _():
            mine_ref[...] = total

        @pl.when(j > 0)
        def _():
            send_ref[...] = total.astype(BF16)

    grid_spec = pltpu.PrefetchScalarGridSpec(
        num_scalar_prefetch=1, grid=(rows // tr, 4),
        in_specs=[pl.BlockSpec((1, tr, cols), lambda i, j, own: (own[j], i, 0)),
                  pl.BlockSpec((1, tr, cols), lambda i, j, own: (j, i, 0))],
        out_specs=[pl.BlockSpec((1, tr, cols), lambda i, j, own: (0, i, 0)),
                   pl.BlockSpec((1, tr, cols), lambda i, j, own: (jnp.maximum(j - 1, 0), i, 0))])
    return pl.pallas_call(
        body, name=name, grid_spec=grid_spec,
        out_shape=[jax.ShapeDtypeStruct((1, rows, cols), F32), jax.ShapeDtypeStruct((3, rows, cols), BF16)],
        compiler_params=_cparams(dimension_semantics=("arbitrary", "arbitrary")),
    )(owners, grad, recv)


def _modulation(c16, ada_w_sh, ada_b_sh, decay_logit):
    def body(c_ref, w_ref, b_ref, dl_ref, mod_ref, act_ref, lg_ref):
        cv = c_ref[...]
        act = cv * _sigmoid(cv)
        act_ref[...] = act
        mod_ref[...] = jnp.dot(act, w_ref[...], preferred_element_type=F32,
                               precision=lax.Precision.HIGHEST) + b_ref[...]
        z = dl_ref[...]
        lg_ref[...] = jnp.minimum(z, 0.0) - jnp.log(1.0 + jnp.exp(-jnp.abs(z)))

    return pl.pallas_call(
        body, name="modulation",
        out_shape=[jax.ShapeDtypeStruct((16, ADA_SH), F32), jax.ShapeDtypeStruct((16, D), F32),
                   jax.ShapeDtypeStruct(decay_logit.shape, F32)],
        compiler_params=_cparams(),
    )(c16, ada_w_sh, ada_b_sh, decay_logit)


def _adam_update(w, g, m, v):
    m2 = ADAM_B1 * m + (1.0 - ADAM_B1) * g
    v2 = ADAM_B2 * v + (1.0 - ADAM_B2) * (g * g)
    m_hat = m2 / (1.0 - ADAM_B1 ** ADAM_STEP)
    v_hat = v2 / (1.0 - ADAM_B2 ** ADAM_STEP)
    delta = -ADAM_LR * (m_hat / (jnp.sqrt(v_hat) + ADAM_EPS) + ADAM_WD * w)
    return delta, m2, v2


def _adam_sharded(w, m, v, own, recv, name):
    rows, cols = w.shape
    tr = _row_tile(rows)

    def body(w_ref, m_ref, v_ref, g0, g1, g2, g3, g_ref, d_ref, m_out, v_out):
        g = ((g0[0] + g1[0].astype(F32)) + g2[0].astype(F32)) + g3[0].astype(F32)
        delta, m2, v2 = _adam_update(w_ref[...], g, m_ref[...], v_ref[...])
        g_ref[...] = g
        d_ref[...] = delta
        m_out[...] = m2
        v_out[...] = v2

    flat = pl.BlockSpec((tr, cols), lambda i: (i, 0))
    part = lambda j: pl.BlockSpec((1, tr, cols), lambda i, j=j: (j, i, 0))
    return pl.pallas_call(
        body, name=name, grid=(rows // tr,),
        in_specs=[flat, flat, flat, part(0), part(0), part(1), part(2)],
        out_specs=[flat] * 4,
        out_shape=[jax.ShapeDtypeStruct((rows, cols), F32)] * 4,
        compiler_params=_cparams(dimension_semantics=("arbitrary",)),
    )(w, m, v, own, recv, recv, recv)


def _ada_backward(act_t, dmod16, dmodc8, ada_w_sh, m, v):
    def body(at_ref, dm_ref, dc_ref, w_ref, m_ref, v_ref, g_ref, d_ref, m_out, v_out, pc_ref):
        g = jnp.dot(at_ref[...], dm_ref[...], preferred_element_type=F32, precision=lax.Precision.HIGHEST)
        w = w_ref[...]
        delta, m2, v2 = _adam_update(w, g, m_ref[...], v_ref[...])
        g_ref[...] = g
        d_ref[...] = delta
        m_out[...] = m2
        v_out[...] = v2
        pc_ref[...] = lax.dot_general(dc_ref[...], w, (((1,), (1,)), ((), ())), preferred_element_type=F32,
                                      precision=lax.Precision.HIGHEST)

    return pl.pallas_call(
        body, name="ada_backward",
        out_shape=[jax.ShapeDtypeStruct((D, ADA_SH), F32)] * 4 + [jax.ShapeDtypeStruct((8, D), F32)],
        compiler_params=_cparams(),
    )(act_t, dmod16, dmodc8, ada_w_sh, m, v)


def _sum_devices(gathered, name):
    _, rows, cols = gathered.shape

    def body(g_ref, o_ref):
        acc = g_ref[0]
        for d in range(1, N_DEV):
            acc = acc + g_ref[d]
        o_ref[...] = acc

    return pl.pallas_call(body, name=name, out_shape=jax.ShapeDtypeStruct((rows, cols), F32),
                          compiler_params=_cparams())(gathered)


SMALL_PARAMS = ("c_ctx", "norm_w", "conv_b", "gn_w", "final_norm_w", "ada_b", "decay_logit", "conv_w")


def _adam_small(tot, dact_ctx, conv_w_grad, params):
    n = len(SMALL_PARAMS)

    def body(tot_ref, dact_ref, cwg_ref, *refs):
        ins, outs = refs[:3 * n], refs[3 * n:]
        nh = 2 * N_HEADS
        raw = {
            "c_ctx": dact_ref[0:1, :],
            "norm_w": tot_ref[1:2, :], "conv_b": tot_ref[2:3, :], "gn_w": tot_ref[3:4, :],
            "final_norm_w": tot_ref[0:1, :],
            "ada_b": jnp.concatenate([tot_ref[7 + r:8 + r, :] + tot_ref[10 + r:11 + r, :] for r in range(3)], axis=1),
            "decay_logit": tot_ref[14:15, 0:nh],
            "conv_w": cwg_ref[...],
        }
        for k, name in enumerate(SMALL_PARAMS):
            w, m, v = ins[3 * k][...], ins[3 * k + 1][...], ins[3 * k + 2][...]
            g = raw[name]
            if name == "c_ctx":
                s = _sigmoid(w)
                g = g * (s * (1.0 + w * (1.0 - s)))
            elif name == "decay_logit":
                g = g * (1.0 - _sigmoid(w))
            delta, m2, v2 = _adam_update(w, g, m, v)
            for ref, val in zip(outs[4 * k:4 * k + 4], (g, delta, m2, v2)):
                ref[...] = val

    flat = [a for trio in params for a in trio]
    out_shape = [jax.ShapeDtypeStruct(trio[0].shape, F32) for trio in params for _ in range(4)]
    outs = pl.pallas_call(body, name="adam_small", out_shape=out_shape, compiler_params=_cparams())(
        tot, dact_ctx, conv_w_grad, *flat)
    return [outs[4 * k:4 * k + 4] for k in range(n)]


def _adam_square(params, own, recv):
    def body(own_ref, recv_ref, *refs):
        ins, outs = refs[:9], refs[9:]
        for k in range(3):
            rows = slice(k * RSH, (k + 1) * RSH)
            g = own_ref[0, rows, :]
            for j in range(3):
                g = g + recv_ref[j, rows, :].astype(F32)
            delta, m2, v2 = _adam_update(ins[3 * k][...], g, ins[3 * k + 1][...], ins[3 * k + 2][...])
            for ref, val in zip(outs[4 * k:4 * k + 4], (g, delta, m2, v2)):
                ref[...] = val

    flat = [a for trio in params for a in trio]
    outs = pl.pallas_call(body, name="adam_square", out_shape=[jax.ShapeDtypeStruct((RSH, D), F32)] * 12,
                          compiler_params=_cparams())(own, recv, *flat)
    return [outs[4 * k:4 * k + 4] for k in range(3)]


def _rope(t, cos, sin):
    lane = lax.broadcasted_iota(jnp.int32, (1, LANES), 1)
    first_half = jnp.bitwise_and(lane, DK // 2) == 0
    partner = jnp.where(first_half, pltpu.roll(t, LANES - DK // 2, 1), pltpu.roll(t, DK // 2, 1))
    return t * cos + partner * sin


class _Shards:
    def __init__(self, arrays, picks):
        self.arrays, self.picks = list(arrays), list(picks)

    def specs(self, index_args):
        out = []
        for a, arr in enumerate(self.arrays):
            steps = [p for p, (ai, _) in enumerate(self.picks) if ai == a]
            first, slot0, last = steps[0], self.picks[steps[0]][1], arr.shape[0] - 1
            out.append(pl.BlockSpec(
                (1, D, WSH), lambda *args, first=first, slot0=slot0, last=last:
                (jnp.clip(index_args(*args) - first + slot0, 0, last), 0, 0)))
        return out

    def __add__(self, other):
        shift = len(self.arrays)
        return _Shards(self.arrays + other.arrays, self.picks + [(a + shift, s) for a, s in other.picks])


def _load_w_in(step, ids_ref, shards, w_refs, w_vmem, sems):
    @pl.when(step == 0)
    def _():
        copies = []
        for p, (a, slot) in enumerate(shards.picks):
            col = pl.multiple_of(ids_ref[p] * WSH, LANES)
            copies.append(pltpu.make_async_copy(w_refs[a].at[slot], w_vmem.at[:, pl.ds(col, WSH)], sems.at[p]))
        for cp in copies:
            cp.start()
        for cp in copies:
            cp.wait()


def _project_columns(xmb, w_ref, shard, cos, sin, p_ref):
    groups = 2 * WSH // LANES
    q0, k0, v0 = CB_QK * D // LANES, (CB_QK * D + N_HEADS * DK) // LANES, CB_V * D // LANES
    mxu_cols = 2 * LANES
    for a in range(0, 2 * WSH, mxu_cols):
        acc = _dot(xmb, w_ref[:, a:a + mxu_cols])
        for col in range(0, mxu_cols, LANES):
            grp = shard * groups + (a + col) // LANES
            rotary = jnp.logical_and(grp >= q0, grp < v0)
            scale = jnp.where(grp >= k0, DK ** -0.5, 1.0)
            piece = acc[:, col:col + LANES]
            turned = _rope(piece * scale, cos, sin)
            p_ref[:, a + col:a + col + LANES] = jnp.where(rotary, turned, piece).astype(BF16)


def _prenorm(x, ctx, modx, modc, norm_w):
    n_x_tiles = x.shape[0] // TM
    lext = x.shape[0] + ctx.shape[0]

    def body(x_ref, ctx_ref, mx_ref, mc_ref, nw_ref, xm_ref, xmt_ref, cos_ref, sin_ref):
        i = pl.program_id(0)
        is_ctx = i >= n_x_tiles
        x = jnp.where(is_ctx, ctx_ref[...], x_ref[...])
        r = lax.rsqrt(jnp.mean(x * x, axis=-1, keepdims=True) + EPS)
        sh = jnp.where(is_ctx, mc_ref[0:1, :], mx_ref[0:1, :])
        sc = jnp.where(is_ctx, mc_ref[1:2, :], mx_ref[1:2, :])
        xm = (x * r * nw_ref[...]) * (1.0 + sc) + sh
        xm_ref[...] = xm.astype(BF16)
        xmt_ref[...] = xm.T.astype(BF16)
        pos = i * TM + lax.broadcasted_iota(jnp.int32, (TM, 1), 0)
        lane = lax.broadcasted_iota(jnp.int32, (1, LANES), 1)
        n_freq = DK // 4
        inv = jnp.exp((lane % n_freq).astype(F32) * (-math.log(ROPE_BASE) / n_freq))
        by_column = (lane % (2 * n_freq)) >= n_freq
        ang = jnp.where(by_column, (pos % GRID_W).astype(F32), (pos // GRID_W).astype(F32)) * inv
        first_half = (lane % DK) < DK // 2
        cos_ref[...] = jnp.where(is_ctx, 1.0, jnp.cos(ang))
        sin_ref[...] = jnp.where(is_ctx, 0.0, jnp.where(first_half, -jnp.sin(ang), jnp.sin(ang)))

    full = lambda a: pl.BlockSpec(a.shape, lambda i: (0,) * a.ndim)
    row = lambda width: pl.BlockSpec((TM, width), lambda i: (i, 0))
    return pl.pallas_call(
        body, name="prenorm", grid=(lext // TM,),
        in_specs=[pl.BlockSpec((TM, D), lambda i: (jnp.minimum(i, n_x_tiles - 1), 0)), full(ctx),
                  full(modx), full(modc), full(norm_w)],
        out_specs=[row(D), pl.BlockSpec((D, TM), lambda i: (0, i)), row(LANES), row(LANES)],
        out_shape=[jax.ShapeDtypeStruct((lext, D), BF16), jax.ShapeDtypeStruct((D, lext), BF16),
                   jax.ShapeDtypeStruct((lext, LANES), F32), jax.ShapeDtypeStruct((lext, LANES), F32)],
        compiler_params=_cparams(dimension_semantics=("arbitrary",)),
    )(x, ctx, modx, modc, norm_w)


def _in_projection(p_ext, xm, cos_t, sin_t, w, pairs, me, name):
    lext = xm.shape[0]
    tr = lext // 4
    meta = []
    for (slot_a, k_a), (slot_b, k_b) in pairs:
        assert k_a ^ k_b == 1
        dev_a = jnp.bitwise_xor(me, k_a)
        a_low = dev_a % 2 == 0
        meta += [jnp.where(a_low, slot_a, slot_b), jnp.where(a_low, slot_b, slot_a), dev_a // 2]
    meta = jnp.stack(meta).astype(jnp.int32)

    def body(meta_ref, xm_ref, cos_ref, sin_ref, w_hbm, *rest):
        p_ref, w_ref, sems = rest[-3:]
        j, i = pl.program_id(0), pl.program_id(1)

        @pl.when(i == 0)
        def _():
            copies = [pltpu.make_async_copy(w_hbm.at[meta_ref[3 * j + h]], w_ref.at[:, h * WSH:(h + 1) * WSH],
                                            sems.at[h]) for h in range(2)]
            for cp in copies:
                cp.start()
            for cp in copies:
                cp.wait()

        _project_columns(xm_ref[...], w_ref, meta_ref[3 * j + 2], cos_ref[...], sin_ref[...], p_ref)

    row = lambda width: pl.BlockSpec((tr, width), lambda j, i, meta_: (i, 0))
    grid_spec = pltpu.PrefetchScalarGridSpec(
        num_scalar_prefetch=1, grid=(len(pairs), lext // tr),
        in_specs=[row(D), row(LANES), row(LANES), pl.BlockSpec(memory_space=pl.ANY)]
        + ([] if p_ext is None else [pl.BlockSpec(memory_space=pl.ANY)]),
        out_specs=pl.BlockSpec((tr, 2 * WSH), lambda j, i, meta_: (i, meta_[3 * j + 2])),
        scratch_shapes=[pltpu.VMEM((D, 2 * WSH), BF16), pltpu.SemaphoreType.DMA((2,))])
    return pl.pallas_call(
        body, name=name, grid_spec=grid_spec,
        out_shape=jax.ShapeDtypeStruct((lext, PW), BF16),
        input_output_aliases={} if p_ext is None else {5: 0},
        compiler_params=_cparams(dimension_semantics=("arbitrary", "arbitrary")),
    )(meta, xm, cos_t, sin_t, w, *([] if p_ext is None else [p_ext]))


def _decay_tables(lgf, lgb, n):
    i = lax.broadcasted_iota(jnp.int32, (n, 1), 0).astype(F32)
    return dict(i=i, k_f=jnp.exp(lgf * (n - 1.0 - i)), k_b=jnp.exp(lgb * i),
                q_f=jnp.exp(lgf * (i + 1.0)), q_b=jnp.exp(lgb * (n - i)))


def _decay_matrix(lgf, lgb, n, transposed=False):
    ii = lax.broadcasted_iota(jnp.int32, (n, n), 0)
    jj = lax.broadcasted_iota(jnp.int32, (n, n), 1)
    diff = (jj - ii if transposed else ii - jj).astype(F32)
    low = jnp.exp(lgf * jnp.maximum(diff, 0.0))
    up = jnp.exp(lgb * jnp.maximum(-diff, 0.0))
    return jnp.where(diff > 0, low, jnp.where(diff < 0, up, 2.0)), diff


def _cat_lanes(a, b):
    return jnp.concatenate([a.astype(BF16), b.astype(BF16)], axis=1)


def _retention_forward(p_ext, lg, seq, ctx_len):
    lext = seq + ctx_len
    n_chunks = seq // RET_C
    C = RET_C

    def body(lg_ref, q_ref, k_ref, v_ref, rn_ref, rstd_ref, kv_scr, sf_scr, sb_scr):
        pair = pl.program_id(0)
        lane = lax.broadcasted_iota(jnp.int32, (1, LANES), 1)
        heads = range(2)
        hmask = [(lane // DK == hh).astype(F32) for hh in heads]
        lgf = [lg_ref[0, 2 * pair + hh] for hh in heads]
        lgb = [lg_ref[1, 2 * pair + hh] for hh in heads]
        vs = [slice(hh * DV, (hh + 1) * DV) for hh in heads]
        t = [_decay_tables(lgf[hh], lgb[hh], C) for hh in heads]
        kc_all = k_ref[seq:lext, :].astype(F32)
        s0 = []
        for hh in heads:
            tc = _decay_tables(lgf[hh], lgb[hh], ctx_len)
            kc = kc_all * hmask[hh]
            s0.append(_dot_tn(_cat_lanes(kc * tc["k_f"], kc * tc["k_b"]), v_ref[seq:lext, vs[hh]]))

        def increments(c, carry):
            rows = pl.ds(pl.multiple_of(c * C, C), C)
            k_all = k_ref[rows, :].astype(F32)
            for hh in heads:
                k = k_all * hmask[hh]
                kv_scr[hh, c] = _dot_tn(_cat_lanes(k * t[hh]["k_f"], k * t[hh]["k_b"]), v_ref[rows, vs[hh]])
            return carry

        lax.fori_loop(0, n_chunks, increments, 0, unroll=2)
        gf_c = [jnp.exp(lgf[hh] * C) for hh in heads]
        gb_c = [jnp.exp(lgb[hh] * C) for hh in heads]

        def scan_f(c, s):
            for hh in heads:
                sf_scr[hh, c] = s[hh]
            return tuple(gf_c[hh] * s[hh] + kv_scr[hh, c, 0:LANES, :] for hh in heads)

        def scan_b(n, s):
            c = n_chunks - 1 - n
            for hh in heads:
                sb_scr[hh, c] = s[hh]
            return tuple(gb_c[hh] * s[hh] + kv_scr[hh, c, LANES:2 * LANES, :] for hh in heads)

        lax.fori_loop(0, n_chunks, scan_f, tuple(s0[hh][0:LANES] for hh in heads))
        lax.fori_loop(0, n_chunks, scan_b, tuple(s0[hh][LANES:2 * LANES] for hh in heads))
        dmat = [_decay_matrix(lgf[hh], lgb[hh], C)[0] for hh in heads]

        def outputs(c, carry):
            rows = pl.ds(pl.multiple_of(c * C, C), C)
            q_all = q_ref[rows, :].astype(F32)
            k = k_ref[rows, :]
            for hh in heads:
                q = q_all * hmask[hh]
                v = v_ref[rows, vs[hh]]
                s = _dot_nt(q.astype(BF16), k)
                inner = _dot((s * dmat[hh]).astype(BF16), v)
                states = jnp.concatenate([sf_scr[hh, c], sb_scr[hh, c]], axis=0).astype(BF16)
                ret = inner + _dot(_cat_lanes(q * t[hh]["q_f"], q * t[hh]["q_b"]), states)
                mu = jnp.mean(ret, axis=-1, keepdims=True)
                cen = ret - mu
                rstd = lax.rsqrt(jnp.mean(cen * cen, axis=-1, keepdims=True) + EPS)
                rn_ref[rows, vs[hh]] = cen * rstd
                rstd_ref[rows, vs[hh]] = jnp.broadcast_to(rstd, (C, DV))
            return carry

        lax.fori_loop(0, n_chunks, outputs, 0, unroll=2)

    qk0 = CB_QK * D // LANES
    return pl.pallas_call(
        body, name="retention_forward", grid=(N_HEADS // 2,),
        in_specs=[pl.BlockSpec(memory_space=pltpu.SMEM),
                  pl.BlockSpec((lext, LANES), lambda g: (0, qk0 + g)),
                  pl.BlockSpec((lext, LANES), lambda g: (0, qk0 + N_HEADS // 2 + g)),
                  pl.BlockSpec((lext, 2 * DV), lambda g: (0, CB_V * D // (2 * DV) + g))],
        out_specs=[pl.BlockSpec((seq, 2 * DV), lambda g: (0, g))] * 2,
        out_shape=[jax.ShapeDtypeStruct((seq, D), F32)] * 2,
        scratch_shapes=[pltpu.VMEM((2, n_chunks, 2 * LANES, DV), F32), pltpu.VMEM((2, n_chunks, LANES, DV), F32),
                        pltpu.VMEM((2, n_chunks, LANES, DV), F32)],
        compiler_params=_cparams(dimension_semantics=("arbitrary",)),
    )(lg, p_ext, p_ext, p_ext)


def _retention_backward(p_ext, dret, lg, cos_t, sin_t, seq, ctx_len):
    lext = seq + ctx_len
    n_chunks = seq // RET_C
    C = RET_C

    def body(lg_ref, q_ref, k_ref, v_ref, do_ref, cos_ref, sin_ref, dq_ref, dk_ref, dv_ref, dlg_ref,
             kv_scr, g_scr, sf_scr, sb_scr, gfn_scr, gbp_scr):
        pair = pl.program_id(0)
        lane = lax.broadcasted_iota(jnp.int32, (1, LANES), 1)
        heads = range(2)
        hmask = [(lane // DK == hh).astype(F32) for hh in heads]
        lgf = [lg_ref[0, 2 * pair + hh] for hh in heads]
        lgb = [lg_ref[1, 2 * pair + hh] for hh in heads]
        vs = [slice(hh * DV, (hh + 1) * DV) for hh in heads]
        t = [_decay_tables(lgf[hh], lgb[hh], C) for hh in heads]
        tc = [_decay_tables(lgf[hh], lgb[hh], ctx_len) for hh in heads]
        kc_all = k_ref[seq:lext, :].astype(F32)
        kc = [kc_all * hmask[hh] for hh in heads]
        vc = [v_ref[seq:lext, vs[hh]] for hh in heads]
        kc_cat = [_cat_lanes(kc[hh] * tc[hh]["k_f"], kc[hh] * tc[hh]["k_b"]) for hh in heads]
        s0 = [_dot_tn(kc_cat[hh], vc[hh]) for hh in heads]

        def increments(c, carry):
            rows = pl.ds(pl.multiple_of(c * C, C), C)
            k_all = k_ref[rows, :].astype(F32)
            q_all = q_ref[rows, :].astype(F32)
            for hh in heads:
                k, q = k_all * hmask[hh], q_all * hmask[hh]
                kv_scr[hh, c] = _dot_tn(_cat_lanes(k * t[hh]["k_f"], k * t[hh]["k_b"]), v_ref[rows, vs[hh]])
                g_scr[hh, c] = _dot_tn(_cat_lanes(q * t[hh]["q_f"], q * t[hh]["q_b"]), do_ref[rows, vs[hh]])
            return carry

        lax.fori_loop(0, n_chunks, increments, 0, unroll=2)
        gf_c = [jnp.exp(lgf[hh] * C) for hh in heads]
        gb_c = [jnp.exp(lgb[hh] * C) for hh in heads]

        def scan_f(c, s):
            for hh in heads:
                sf_scr[hh, c] = s[hh]
            return tuple(gf_c[hh] * s[hh] + kv_scr[hh, c, 0:LANES, :] for hh in heads)

        def scan_b(n, s):
            c = n_chunks - 1 - n
            for hh in heads:
                sb_scr[hh, c] = s[hh]
            return tuple(gb_c[hh] * s[hh] + kv_scr[hh, c, LANES:2 * LANES, :] for hh in heads)

        def scan_gf(n, carry):
            c = n_chunks - 1 - n
            for hh in heads:
                gfn_scr[hh, c] = carry[hh]
            return tuple(g_scr[hh, c, 0:LANES, :] + gf_c[hh] * carry[hh] for hh in heads)

        def scan_gb(c, carry):
            for hh in heads:
                gbp_scr[hh, c] = carry[hh]
            return tuple(g_scr[hh, c, LANES:2 * LANES, :] + gb_c[hh] * carry[hh] for hh in heads)

        lax.fori_loop(0, n_chunks, scan_f, tuple(s0[hh][0:LANES] for hh in heads))
        lax.fori_loop(0, n_chunks, scan_b, tuple(s0[hh][LANES:2 * LANES] for hh in heads))
        zero_state = jnp.zeros((LANES, DV), F32)
        gf_first = lax.fori_loop(0, n_chunks, scan_gf, (zero_state, zero_state))
        gb_last = lax.fori_loop(0, n_chunks, scan_gb, (zero_state, zero_state))

        dmat, w_f, w_b = [], [], []
        for hh in heads:
            dm, diff = _decay_matrix(lgf[hh], lgb[hh], C)
            dmat.append(dm)
            w_f.append(jnp.where(diff > 0, diff * dm, 0.0))
            w_b.append(jnp.where(diff < 0, -diff * dm, 0.0))

        def total(a):
            rows_, width = a.shape
            part = jnp.sum(a.reshape(rows_ // 8, 8, width), axis=0)
            return part[:, 0:LANES] + part[:, LANES:2 * LANES] if width == 2 * LANES else part

        def chunk_head(hh, c, rows, q_all, k_all, dlgf, dlgb):
            th = t[hh]
            qm = q_all * hmask[hh]
            km = k_all * hmask[hh]
            qb, kb = qm.astype(BF16), km.astype(BF16)
            v = v_ref[rows, vs[hh]]
            do = do_ref[rows, vs[hh]]
            s = _dot_nt(qb, kb)
            dsd = _dot_nt(do, v)
            ds = (dsd * dmat[hh]).astype(BF16)
            dq_in = _dot(ds, kb)
            dk_in = _dot_tn(ds, qb)
            dv_in = _dot_tn((s * dmat[hh]).astype(BF16), do)
            prod = s * dsd
            dlgf = dlgf + total(prod * w_f[hh])
            dlgb = dlgb + total(prod * w_b[hh])
            sf, sb = sf_scr[hh, c], sb_scr[hh, c]
            states = jnp.concatenate([sf, sb], axis=0).astype(BF16)
            dqc = _dot_nt(do, states)
            dqf = dqc[:, 0:LANES] * th["q_f"]
            dqb = dqc[:, LANES:2 * LANES] * th["q_b"]
            dq = (dq_in + dqf + dqb) * hmask[hh]
            dlgf = dlgf + total((th["i"] + 1.0) * (qm * dqf))
            dlgb = dlgb + total((C - th["i"]) * (qm * dqb))
            gfn, gbp = gfn_scr[hh, c], gbp_scr[hh, c]
            gstates = jnp.concatenate([gfn, gbp], axis=0).astype(BF16)
            dkc = _dot_nt(v, gstates)
            dkf = dkc[:, 0:LANES] * th["k_f"]
            dkb = dkc[:, LANES:2 * LANES] * th["k_b"]
            dk = dk_in + dkf + dkb
            dlgf = dlgf + total((C - 1.0 - th["i"]) * (km * dkf)) + C * gf_c[hh] * total(gfn * sf)
            dlgb = dlgb + total(th["i"] * (km * dkb)) + C * gb_c[hh] * total(gbp * sb)
            dv = dv_in + _dot(_cat_lanes(km * th["k_f"], km * th["k_b"]), gstates)
            dv_ref[rows, vs[hh]] = dv.astype(BF16)
            return dq, dk, dlgf, dlgb

        def chunk(c, carry):
            rows = pl.ds(pl.multiple_of(c * C, C), C)
            q_all = q_ref[rows, :].astype(F32)
            k_all = k_ref[rows, :].astype(F32)
            dq0, dk0, f0, b0 = chunk_head(0, c, rows, q_all, k_all, carry[0], carry[1])
            dq1, dk1, f1, b1 = chunk_head(1, c, rows, q_all, k_all, carry[2], carry[3])
            cos, sin = cos_ref[rows, :], sin_ref[rows, :]
            dq_ref[rows, :] = _rope(dq0 + dq1, cos, -sin).astype(BF16)
            dk_ref[rows, :] = (_rope(dk0 + dk1, cos, -sin) * (DK ** -0.5)).astype(BF16)
            return f0, b0, f1, b1

        zero = jnp.zeros((8, LANES), F32)
        sums = lax.fori_loop(0, n_chunks, chunk, (zero, zero, zero, zero), unroll=2)

        dlg = []
        dk_ctx = jnp.zeros((ctx_len, LANES), F32)
        for hh in heads:
            g0 = jnp.concatenate([gf_first[hh], gb_last[hh]], axis=0).astype(BF16)
            dkcc = _dot_nt(vc[hh], g0)
            dkcf = dkcc[:, 0:LANES] * tc[hh]["k_f"]
            dkcb = dkcc[:, LANES:2 * LANES] * tc[hh]["k_b"]
            dlgf = sums[2 * hh] + total((ctx_len - 1.0 - tc[hh]["i"]) * (kc[hh] * dkcf))
            dlgb = sums[2 * hh + 1] + total(tc[hh]["i"] * (kc[hh] * dkcb))
            dk_ctx = dk_ctx + (dkcf + dkcb) * (DK ** -0.5)
            dv_ref[seq:lext, vs[hh]] = _dot(kc_cat[hh], g0).astype(BF16)
            dlg += [jnp.sum(jnp.sum(a, axis=1, keepdims=True), axis=0, keepdims=True) for a in (dlgf, dlgb)]
        dk_ref[seq:lext, :] = dk_ctx.astype(BF16)
        dq_ref[seq:lext, :] = jnp.zeros((ctx_len, LANES), BF16)

        lane8 = lax.broadcasted_iota(jnp.int32, (8, LANES), 1)
        out = jnp.zeros((8, LANES), F32)
        for n, val in enumerate(dlg):
            out = jnp.where(lane8 == n, val, out)
        dlg_ref[0] = out

    qk0 = CB_QK * D // LANES
    q_spec = pl.BlockSpec((lext, LANES), lambda g: (0, qk0 + g))
    k_spec = pl.BlockSpec((lext, LANES), lambda g: (0, qk0 + N_HEADS // 2 + g))
    v_spec = pl.BlockSpec((lext, 2 * DV), lambda g: (0, CB_V * D // (2 * DV) + g))
    table = pl.BlockSpec((lext, LANES), lambda g: (0, 0))
    state = pltpu.VMEM((2, n_chunks, LANES, DV), F32)
    return pl.pallas_call(
        body, name="retention_backward", grid=(N_HEADS // 2,),
        in_specs=[pl.BlockSpec(memory_space=pltpu.SMEM), q_spec, k_spec, v_spec,
                  pl.BlockSpec((seq, 2 * DV), lambda g: (0, g)), table, table],
        out_specs=[pl.BlockSpec((lext, LANES), lambda g: (0, g)), pl.BlockSpec((lext, LANES), lambda g: (0, g)),
                   pl.BlockSpec((lext, 2 * DV), lambda g: (0, g)), pl.BlockSpec((1, 8, LANES), lambda g: (g, 0, 0))],
        out_shape=[jax.ShapeDtypeStruct((lext, N_HEADS * DK), BF16), jax.ShapeDtypeStruct((lext, N_HEADS * DK), BF16),
                   jax.ShapeDtypeStruct((lext, D), BF16), jax.ShapeDtypeStruct((N_HEADS // 2, 8, LANES), F32)],
        scratch_shapes=[pltpu.VMEM((2, n_chunks, 2 * LANES, DV), F32), pltpu.VMEM((2, n_chunks, 2 * LANES, DV), F32),
                        state, state, state, state],
        compiler_params=_cparams(dimension_semantics=("arbitrary",)),
    )(lg, p_ext, p_ext, p_ext, dret, cos_t, sin_t)


class _ColumnWriter:
    def __init__(self, dst_hbm, stage, sems, pieces, n_steps):
        self.dst, self.stage, self.sems, self.pieces, self.n_steps = dst_hbm, stage, sems, pieces, n_steps

    def _copies(self, slot, tile):
        rows = pl.ds(pl.multiple_of(tile * TM, TM), TM)
        return [pltpu.make_async_copy(self.stage.at[slot, :, pl.ds(src, width)], self.dst.at[rows, pl.ds(dst, width)],
                                      self.sems.at[slot, k]) for k, (src, dst, width) in enumerate(self.pieces)]

    def slot_for(self, step):
        slot = step % 2

        @pl.when(step >= 2)
        def _():
            for cp in self._copies(slot, step - 2):
                cp.wait()

        return slot

    def send(self, step, slot):
        for cp in self._copies(slot, step):
            cp.start()

        @pl.when(step == self.n_steps - 1)
        def _():
            if self.n_steps >= 2:
                for cp in self._copies(1 - slot, step - 1):
                    cp.wait()
            for cp in self._copies(slot, step):
                cp.wait()


def _merge(p_ext, rn, rstd, x, target, w_a, w_b, w_out, vecs, seq):
    n_tiles = seq // TM
    hb = TM // HALO
    dp_pieces = [(0, CB_BG * D, D), (D, CB_ZA * D, D), (2 * D, CB_ZB * D, 3 * D)]

    def body(h_ref, bg_ref, cg_ref, za_ref, zb_ref, ga_ref, gb_ref, hp_ref, hn_ref, cp_ref, cn_ref,
             rn_ref, rstd_ref, x_ref, t_ref, wa_ref, wb_ref, wo_ref, vec_ref,
             dx1_ref, dp_hbm, dconv_ref, dret_ref, at_ref, b_ref, part_ref, stage, dp_sems):
        i = pl.program_id(0)
        writer = _ColumnWriter(dp_hbm, stage, dp_sems, dp_pieces, n_tiles)
        slot = writer.slot_for(i)
        dpa_ref = stage.at[slot]
        f = lambda ref: ref[...].astype(F32)
        h, bg, cg, za, zb, ga, gb = f(h_ref), f(bg_ref), f(cg_ref), f(za_ref), f(zb_ref), f(ga_ref), f(gb_ref)
        gx, w0, w1, w2 = vec_ref[0:1, :], vec_ref[1:2, :], vec_ref[2:3, :], vec_ref[3:4, :]
        cb, gnw, fw = vec_ref[4:5, :], vec_ref[5:6, :], vec_ref[6:7, :]
        u = cg * h
        row = lax.broadcasted_iota(jnp.int32, (TM, 1), 0)
        u_prev = (f(cp_ref) * f(hp_ref))[HALO - 1:HALO, :]
        u_next = (f(cn_ref) * f(hn_ref))[0:1, :]
        u_prev = jnp.where(i == 0, 0.0, u_prev)
        u_next = jnp.where(i == n_tiles - 1, 0.0, u_next)
        u_up = jnp.where(row == 0, u_prev, pltpu.roll(u, 1, 0))
        u_dn = jnp.where(row == TM - 1, u_next, pltpu.roll(u, TM - 1, 0))
        conv = w0 * u_up + w1 * u + w2 * u_dn + cb
        sza = _sigmoid(za)
        silu_za = za * sza
        a_act = silu_za * bg * conv
        rn = rn_ref[...]
        szb = _sigmoid(zb)
        silu_zb = zb * szb
        rg = rn * gnw
        b_act = silu_zb * rg
        y_a = _dot(a_act.astype(BF16), wa_ref[...])
        y_b = _dot(b_act.astype(BF16), wb_ref[...])
        sga, sgb = _sigmoid(ga), _sigmoid(gb)
        mix = sga * y_a + sgb * y_b
        y = _dot(mix.astype(BF16), wo_ref[...])
        x1 = x_ref[...] + gx * y
        r1 = lax.rsqrt(jnp.mean(x1 * x1, axis=-1, keepdims=True) + EPS)
        xh1 = x1 * r1
        err = xh1 * fw - t_ref[...]
        loss = jnp.sum(jnp.sum(err * err, axis=1, keepdims=True), axis=0, keepdims=True) * (0.5 / D)
        dout = err * (1.0 / D)
        dxh = dout * fw
        dx1 = r1 * (dxh - xh1 * jnp.mean(dxh * xh1, axis=-1, keepdims=True))
        dx1_ref[...] = dx1
        dy = (dx1 * gx).astype(BF16)
        dmix = _dot_nt(dy, wo_ref[...])
        dya = (dmix * sga).astype(BF16)
        dyb = (dmix * sgb).astype(BF16)
        da = _dot_nt(dya, wa_ref[...])
        db = _dot_nt(dyb, wb_ref[...])
        dpa_ref[:, 0:D] = (da * silu_za * conv).astype(BF16)
        dpa_ref[:, D:2 * D] = (da * bg * conv * (sza * (1.0 + za * (1.0 - sza)))).astype(BF16)
        dpa_ref[:, 2 * D:3 * D] = (db * rg * (szb * (1.0 + zb * (1.0 - szb)))).astype(BF16)
        dpa_ref[:, 3 * D:4 * D] = (dmix * y_a * sga * (1.0 - sga)).astype(BF16)
        dpa_ref[:, 4 * D:5 * D] = (dmix * y_b * sgb * (1.0 - sgb)).astype(BF16)
        writer.send(i, slot)
        dconv_ref[...] = (da * silu_za * bg).astype(BF16)
        drn_n = db * silu_zb
        drn = drn_n * gnw
        rstd = rstd_ref[...]
        for hd in range(N_HEADS):
            sl = slice(hd * DV, (hd + 1) * DV)
            dh_, rh = drn[:, sl], rn[:, sl]
            m1 = jnp.mean(dh_, axis=-1, keepdims=True)
            m2 = jnp.mean(dh_ * rh, axis=-1, keepdims=True)
            dret_ref[:, sl] = (rstd[:, sl] * (dh_ - m1 - rh * m2)).astype(BF16)
        at_ref[0] = a_act.T.astype(BF16)
        at_ref[1] = b_act.T.astype(BF16)
        at_ref[2] = mix.T.astype(BF16)
        b_ref[0] = dya
        b_ref[1] = dyb
        b_ref[2] = dy

        @pl.when(i == 0)
        def _():
            part_ref[...] = jnp.zeros((8, D), F32)

        part_ref[0:1, :] += jnp.sum(dout * xh1, axis=0, keepdims=True)
        part_ref[1:2, :] += jnp.sum(dx1 * y, axis=0, keepdims=True)
        part_ref[2:3, :] += jnp.sum(drn_n * rn, axis=0, keepdims=True)
        part_ref[3:4, :] += jnp.broadcast_to(loss, (1, D))

    col = lambda cb_: pl.BlockSpec((TM, D), lambda i, cb_=cb_: (i, cb_))
    prev = lambda cb_: pl.BlockSpec((HALO, D), lambda i, cb_=cb_: (jnp.maximum(i * hb - 1, 0), cb_))
    nxt = lambda cb_: pl.BlockSpec((HALO, D), lambda i, cb_=cb_: (jnp.minimum((i + 1) * hb, n_tiles * hb - 1), cb_))
    tile = pl.BlockSpec((TM, D), lambda i: (i, 0))
    full = lambda a: pl.BlockSpec(a.shape, lambda i: (0,) * a.ndim)
    return pl.pallas_call(
        body, name="merge", grid=(n_tiles,),
        in_specs=[col(CB_H), col(CB_BG), col(CB_CG), col(CB_ZA), col(CB_ZB), col(CB_GA), col(CB_GB),
                  prev(CB_H), nxt(CB_H), prev(CB_CG), nxt(CB_CG),
                  tile, tile, tile, tile, full(w_a), full(w_b), full(w_out), full(vecs)],
        out_specs=[tile, pl.BlockSpec(memory_space=pl.ANY), tile, tile,
                   pl.BlockSpec((3, D, TM), lambda i: (0, 0, i)), pl.BlockSpec((3, TM, D), lambda i: (0, i, 0)),
                   pl.BlockSpec((8, D), lambda i: (0, 0))],
        out_shape=[jax.ShapeDtypeStruct((seq, D), F32), jax.ShapeDtypeStruct((p_ext.shape[0], PW), BF16),
                   jax.ShapeDtypeStruct((seq, D), BF16), jax.ShapeDtypeStruct((seq, D), BF16),
                   jax.ShapeDtypeStruct((3, D, seq), BF16), jax.ShapeDtypeStruct((3, seq, D), BF16),
                   jax.ShapeDtypeStruct((8, D), F32)],
        scratch_shapes=[pltpu.VMEM((2, TM, 5 * D), BF16), pltpu.SemaphoreType.DMA((2, len(dp_pieces)))],
        compiler_params=_cparams(dimension_semantics=("arbitrary",)),
    )(p_ext, p_ext, p_ext, p_ext, p_ext, p_ext, p_ext, p_ext, p_ext, p_ext, p_ext,
      rn, rstd, x, target, w_a, w_b, w_out, vecs)


def _conv_backward(p_ext, dconv, dp, dq, dk, dv, vecs, seq):
    n_tiles = seq // TM
    hb = TM // HALO
    qk = N_HEADS * DK
    pieces = [(0, CB_H * D, D), (D, CB_CG * D, D), (2 * D, CB_QK * D, 2 * D)]
    zero_pieces = [(CB_BG * D, D), (CB_ZA * D, D), (CB_ZB * D, 3 * D)]

    def body(h_ref, cg_ref, dc_ref, dcp_ref, dcn_ref, dq_ref, dk_ref, dv_ref, vec_ref, dp_in, dp_hbm, part_ref,
             stage, sems, zeros, zero_sems):
        del dp_in
        i = pl.program_id(0)
        writer = _ColumnWriter(dp_hbm, stage, sems, pieces, n_tiles + 1)
        slot = writer.slot_for(i)
        out = stage.at[slot]
        out[:, 2 * D:2 * D + qk] = dq_ref[...]
        out[:, 2 * D + qk:3 * D] = dk_ref[...]
        out[:, 3 * D:4 * D] = dv_ref[...]

        @pl.when(i == 0)
        def _():
            part_ref[...] = jnp.zeros((8, D), F32)

        @pl.when(i == n_tiles)
        def _():
            out[:, 0:2 * D] = jnp.zeros((TM, 2 * D), BF16)
            zeros[...] = jnp.zeros(zeros.shape, BF16)
            rows = pl.ds(n_tiles * TM, TM)
            fills = [pltpu.make_async_copy(zeros.at[:, pl.ds(0, width)], dp_hbm.at[rows, pl.ds(dst, width)],
                                           zero_sems.at[k]) for k, (dst, width) in enumerate(zero_pieces)]
            for cp in fills:
                cp.start()
            for cp in fills:
                cp.wait()

        @pl.when(i < n_tiles)
        def _():
            f = lambda ref: ref[...].astype(F32)
            h, cg, dc = f(h_ref), f(cg_ref), f(dc_ref)
            w0, w1, w2 = vec_ref[1:2, :], vec_ref[2:3, :], vec_ref[3:4, :]
            row = lax.broadcasted_iota(jnp.int32, (TM, 1), 0)
            dc_prev = jnp.where(i == 0, 0.0, f(dcp_ref)[HALO - 1:HALO, :])
            dc_next = jnp.where(i == n_tiles - 1, 0.0, f(dcn_ref)[0:1, :])
            dc_up = jnp.where(row == 0, dc_prev, pltpu.roll(dc, 1, 0))
            dc_dn = jnp.where(row == TM - 1, dc_next, pltpu.roll(dc, TM - 1, 0))
            du = w0 * dc_dn + w1 * dc + w2 * dc_up
            u = cg * h
            out[:, 0:D] = (du * cg).astype(BF16)
            out[:, D:2 * D] = (du * h).astype(BF16)
            part_ref[0:1, :] += jnp.sum(u * dc_dn, axis=0, keepdims=True)
            part_ref[1:2, :] += jnp.sum(u * dc, axis=0, keepdims=True)
            part_ref[2:3, :] += jnp.sum(u * dc_up, axis=0, keepdims=True)
            part_ref[3:4, :] += jnp.sum(dc, axis=0, keepdims=True)

        writer.send(i, slot)

    last = n_tiles - 1
    col = lambda cb_: pl.BlockSpec((TM, D), lambda i, cb_=cb_: (jnp.minimum(i, last), cb_))
    lat = lambda w: pl.BlockSpec((TM, w), lambda i: (jnp.minimum(i, last), 0))
    ext = lambda w: pl.BlockSpec((TM, w), lambda i: (i, 0))
    anyspec = pl.BlockSpec(memory_space=pl.ANY)
    return pl.pallas_call(
        body, name="conv_backward", grid=(n_tiles + 1,),
        in_specs=[col(CB_H), col(CB_CG), lat(D),
                  pl.BlockSpec((HALO, D), lambda i: (jnp.clip(i * hb - 1, 0, n_tiles * hb - 1), 0)),
                  pl.BlockSpec((HALO, D), lambda i: (jnp.minimum((i + 1) * hb, n_tiles * hb - 1), 0)),
                  ext(qk), ext(qk), ext(D), pl.BlockSpec(vecs.shape, lambda i: (0, 0)), anyspec],
        out_specs=[anyspec, pl.BlockSpec((8, D), lambda i: (0, 0))],
        out_shape=[jax.ShapeDtypeStruct(dp.shape, BF16), jax.ShapeDtypeStruct((8, D), F32)],
        input_output_aliases={9: 0},
        scratch_shapes=[pltpu.VMEM((2, TM, 4 * D), BF16), pltpu.SemaphoreType.DMA((2, len(pieces))),
                        pltpu.VMEM((TM, 3 * D), BF16), pltpu.SemaphoreType.DMA((len(zero_pieces),))],
        compiler_params=_cparams(dimension_semantics=("arbitrary",)),
    )(p_ext, p_ext, dconv, dconv, dconv, dq, dk, dv, vecs, dp)


def _input_backward(dp, shards, ids, x, ctx, dx1, modx, modc, norm_w):
    seq = x.shape[0]
    lext = seq + ctx.shape[0]
    n_x = seq // TM
    n_w = len(shards.arrays)

    def body(ids_ref, dp_ref, x_ref, ctx_ref, dx1_ref, mx_ref, mc_ref, nw_ref, *rest):
        w_hbm, (gx_ref, part_ref, w_ref, w_sems) = rest[:n_w], rest[n_w:]
        i = pl.program_id(0)
        _load_w_in(i, ids_ref, shards, w_hbm, w_ref, w_sems)
        is_ctx = i >= n_x
        dxm = _dot_nt(dp_ref[...], w_ref[...])
        x = jnp.where(is_ctx, ctx_ref[...], x_ref[...])
        r = lax.rsqrt(jnp.mean(x * x, axis=-1, keepdims=True) + EPS)
        xh = x * r
        nw = nw_ref[...]
        sc = jnp.where(is_ctx, mc_ref[1:2, :], mx_ref[1:2, :])
        dxn = dxm * (1.0 + sc)
        dxh = dxn * nw
        dx = r * (dxh - xh * jnp.mean(dxh * xh, axis=-1, keepdims=True))

        @pl.when(jnp.logical_not(is_ctx))
        def _():
            gx_ref[...] = dx1_ref[...] + dx

        @pl.when(i == 0)
        def _():
            part_ref[...] = jnp.zeros((8, D), F32)

        fx = jnp.where(is_ctx, 0.0, 1.0)
        d_shift = jnp.sum(dxm, axis=0, keepdims=True)
        d_scale = jnp.sum(dxm * (xh * nw), axis=0, keepdims=True)
        part_ref[0:1, :] += fx * d_shift
        part_ref[1:2, :] += fx * d_scale
        part_ref[2:3, :] += jnp.sum(dxn * xh, axis=0, keepdims=True)
        part_ref[3:4, :] += (1.0 - fx) * d_shift
        part_ref[4:5, :] += (1.0 - fx) * d_scale

    lat = lambda w: pl.BlockSpec((TM, w), lambda i, ids_: (jnp.minimum(i, n_x - 1), 0))
    ext = lambda w: pl.BlockSpec((TM, w), lambda i, ids_: (i, 0))
    full = lambda a: pl.BlockSpec(a.shape, lambda i, ids_: (0,) * a.ndim)
    grid_spec = pltpu.PrefetchScalarGridSpec(
        num_scalar_prefetch=1, grid=(lext // TM,),
        in_specs=[ext(PW), lat(D), full(ctx), lat(D), full(modx), full(modc), full(norm_w)]
        + [pl.BlockSpec(memory_space=pl.ANY)] * n_w,
        out_specs=[lat(D), pl.BlockSpec((8, D), lambda i, ids_: (0, 0))],
        scratch_shapes=[pltpu.VMEM((D, PW), BF16), pltpu.SemaphoreType.DMA((N_DEV,))])
    return pl.pallas_call(
        body, name="input_backward", grid_spec=grid_spec,
        out_shape=[jax.ShapeDtypeStruct((seq, D), F32), jax.ShapeDtypeStruct((8, D), F32)],
        compiler_params=_cparams(dimension_semantics=("arbitrary",)),
    )(ids, dp, x, ctx, dx1, modx, modc, norm_w, *shards.arrays)


def _weight_grad_in(xm_t, dp, owners, name):
    lext = xm_t.shape[1]

    def body(own_ref, a_ref, b_ref, o_ref):
        o_ref[0] = _dot(a_ref[...], b_ref[...])

    grid_spec = pltpu.PrefetchScalarGridSpec(
        num_scalar_prefetch=1, grid=(4,),
        in_specs=[pl.BlockSpec((D, lext), lambda j, own: (0, 0)),
                  pl.BlockSpec((lext, WSH), lambda j, own: (0, own[j]))],
        out_specs=pl.BlockSpec((1, D, WSH), lambda j, own: (j, 0, 0)))
    return pl.pallas_call(
        body, name=name, grid_spec=grid_spec,
        out_shape=jax.ShapeDtypeStruct((4, D, WSH), F32),
        compiler_params=_cparams(dimension_semantics=("arbitrary",)),
    )(owners, xm_t, dp)


def _weight_grad_square(a_t, b):
    seq = a_t.shape[2]

    def body(a_ref, b_ref, o_ref):
        o_ref[:, 0] = _dot(a_ref[0], b_ref[0]).reshape(N_DEV, RSH, D)

    return pl.pallas_call(
        body, name="weight_grad_square", grid=(3,),
        in_specs=[pl.BlockSpec((1, D, seq), lambda t: (t, 0, 0)), pl.BlockSpec((1, seq, D), lambda t: (t, 0, 0))],
        out_specs=pl.BlockSpec((N_DEV, 1, RSH, D), lambda t: (0, t, 0, 0)),
        out_shape=jax.ShapeDtypeStruct((N_DEV, 3, RSH, D), F32),
        compiler_params=_cparams(dimension_semantics=("arbitrary",)),
    )(a_t, b)


def _tie(value, token):
    return value + token[0, 0].astype(value.dtype)


def _pad_lanes(a):
    return jnp.pad(a, ((0, 0), (0, D - a.shape[1])))


def _rows(total, *parts):
    width = max(a.shape[1] for _, a in parts)
    out = None
    for row, a in parts:
        padded = jnp.pad(a, ((row, total - row - a.shape[0]), (0, width - a.shape[1])))
        out = padded if out is None else out + padded
    return out


def kernel(x, c, ctx, c_ctx, norm_w, ada_w, ada_b, w_in, conv_w, conv_b, decay_logit, gn_w, w_a, w_b, w_out, final_norm_w, loss_target, m_c_ctx, m_norm_w, m_ada_w, m_ada_b, m_w_in, m_conv_w, m_conv_b, m_decay_logit, m_gn_w, m_w_a, m_w_b, m_w_out, m_final_norm_w, v_c_ctx, v_norm_w, v_ada_w, v_ada_b, v_w_in, v_conv_w, v_conv_b, v_decay_logit, v_gn_w, v_w_a, v_w_b, v_w_out, v_final_norm_w):
    xi, yi, ci = _coords()
    me = 4 * xi + 2 * yi + ci
    chip = 2 * xi + yi
    seq, ctx_len = x.shape[1], ctx.shape[1]
    assert seq % TM == 0 and seq % RET_C == 0 and ctx_len == TM and seq % GRID_W == 0
    csh = D // N_DEV

    blk = jnp.pad(c, ((0, 7), (0, 0))) + jnp.pad(conv_w[0], ((1, 4), (0, D - csh)))
    got, _ = _all_gather_small(blk, "gather_cond")
    conv_w_all = got[:, 1:4, 0:csh].transpose(1, 0, 2).reshape(3, D)
    c16 = _rows(16, (0, got[:, 0, :]), (N_DEV, c_ctx[None]))
    ada_b_sh = lax.dynamic_slice(ada_b, (0, me * ADA_SH), (1, ADA_SH))
    mod_sh, act16, lg = _modulation(c16, ada_w[0], ada_b_sh, decay_logit[0])
    mod_all, small_done = _all_gather_small(mod_sh, "gather_mod")
    mod_all = mod_all.transpose(1, 0, 2).reshape(16, 3 * D)
    modx = lax.dynamic_slice(mod_all, (me, 0), (1, 3 * D)).reshape(3, D)
    modc = mod_all[8].reshape(3, D)

    x2, tgt = x[0], loss_target[0]
    wblock = lambda n: jax.ShapeDtypeStruct((n, D, WSH), BF16)
    ids_of = lambda ks: jnp.stack([jnp.bitwise_xor(me, k) for k in ks]).astype(jnp.int32)
    own_shard = w_in[0].astype(BF16)[None]
    sq_shards = [w[0].astype(BF16) for w in (w_a, w_b, w_out)]
    wave1 = _Exchange(
        "gather_in_first", [jnp.broadcast_to(own_shard, (2, D, WSH)), own_shard], [wblock(4)], 3,
        lambda srcs, lands: _forward_plan([(0, 1)])(srcs, lands) + _send_plan((4, 2))(srcs[1:], lands),
        after=small_done)
    modx = _tie(modx, wave1.token)
    xm, xm_t, cos_t, sin_t = _prenorm(x2, ctx[0], modx, modc, norm_w)
    (w_pair, _), _ = wave1.wait(xm_t, 1, through=[0])
    first = _Shards([w_pair], [(0, 0), (0, 1)])
    p_ext = _in_projection(None, xm, cos_t, sin_t, w_pair, [((0, 0), (1, 1))], me, "in_projection_pair")
    _, (w_nbr,) = wave1.wait(p_ext, through=[1, 2])
    wave2 = _Exchange(
        "gather_in_second", [w_nbr, own_shard], [wblock(2)], 3,
        lambda srcs, lands: _forward_plan([(0, 2), (1, 3)])(srcs, lands) + _send_plan((6,))(srcs[1:], lands))
    (w_nbr, _), _ = wave2.wait(wave2.token, 2, through=[0])
    second = _Shards([w_nbr], [(0, 0), (0, 1), (0, 2), (0, 3)])
    p_ext = _in_projection(p_ext, xm, cos_t, sin_t, w_nbr, [((0, 4), (2, 5)), ((1, 2), (3, 3))], me,
                           "in_projection_neighbours")
    _, (w_diag,) = wave2.wait(p_ext, through=[1, 2])
    wave3 = _Exchange(
        "gather_in_third", [w_diag] + sq_shards, [jax.ShapeDtypeStruct((N_DEV, RSH, D), BF16)] * 3, 1 + 3 * (N_DEV - 1),
        lambda srcs, lands: _forward_plan([(0, 1)])(srcs, lands) + _gather_plan(srcs[1:], lands))
    (w_diag, *_), _ = wave3.wait(wave3.token, 1, through=[0])
    third = _Shards([w_diag], [(0, 0), (0, 1)])
    p_ext = _in_projection(p_ext, xm, cos_t, sin_t, w_diag, [((0, 6), (1, 7))], me, "in_projection_diagonal")
    w_shards, w_ids = first + second + third, ids_of((0, 1, 4, 2, 5, 3, 6, 7))

    rn, rstd = _retention_forward(p_ext, lg, seq, ctx_len)
    _, sq_lands = wave3.wait(rstd, through=range(1, 7))
    w_a_all, w_b_all, w_out_all = (
        lax.dynamic_update_slice(land, shard[None], (me, 0, 0)).reshape(D, D) for land, shard in zip(sq_lands, sq_shards))
    vecs = _rows(8, (0, modx[2:3]), (1, conv_w_all), (4, conv_b), (5, gn_w), (6, final_norm_w[None]))
    dx1, dp, dconv, dret, op_at, op_b, part_m = _merge(p_ext, rn, rstd, x2, tgt, w_a_all, w_b_all, w_out_all, vecs, seq)

    j4 = jnp.arange(4, dtype=jnp.int32)
    owners = (2 * jnp.bitwise_xor(chip, j4) + ci).astype(jnp.int32)
    owners_sib = (2 * jnp.bitwise_xor(chip, j4) + (1 - ci)).astype(jnp.int32)
    gw_sq = _weight_grad_square(op_at, op_b).reshape(N_DEV, 3 * RSH, D)
    rs_sq_pair = _Exchange("rs_square_pair", [gw_sq], [jax.ShapeDtypeStruct((4, 3 * RSH, D), F32)], 4,
                           _pair_plan(lambda j, chip_, c_: 2 * jnp.bitwise_xor(chip_, j) + (1 - c_)))
    dq, dk, dv, dlg = _retention_backward(p_ext, dret, _tie(lg, rs_sq_pair.token), cos_t, sin_t, seq, ctx_len)
    (gw_sq,), (r1_sq,) = rs_sq_pair.wait(dlg)
    own_sq, send_sq = _pair_sum(gw_sq, r1_sq, owners, "pair_sum_square")
    rs_sq_chips = _Exchange("rs_square_chips", [send_sq], [jax.ShapeDtypeStruct((3, 3 * RSH, D), BF16)], 3, _chips_plan)
    dp, part_c = _conv_backward(p_ext, dconv, dp, dq, dk, dv, _tie(vecs, rs_sq_chips.token), seq)
    gw_sib = _weight_grad_in(xm_t, dp, owners_sib, "weight_grad_in_sibling")
    rs_in_pair = _Exchange("rs_in_pair", [gw_sib], [jax.ShapeDtypeStruct((4, D, WSH), F32)], 4,
                           _pair_plan(lambda j, chip_, c_: j))
    gw_own = _weight_grad_in(xm_t, dp, _tie(owners, rs_in_pair.token), "weight_grad_in_own")
    _, (r1_in,) = rs_in_pair.wait(gw_own)
    own_in, send_in = _pair_sum(gw_own, r1_in, j4, "pair_sum_in")
    rs_in_chips = _Exchange("rs_in_chips", [send_in], [jax.ShapeDtypeStruct((3, D, WSH), BF16)], 3, _chips_plan)
    grad_x, part_i = _input_backward(dp, w_shards, w_ids, x2, ctx[0], dx1, _tie(modx, rs_in_chips.token), modc, norm_w)

    dl = dlg[:, 0, 0:4]
    dlg_row = jnp.pad(dl[:, 0::2].reshape(1, N_HEADS), ((0, 0), (0, D - N_HEADS))) + jnp.pad(
        dl[:, 1::2].reshape(1, N_HEADS), ((0, 0), (N_HEADS, D - 2 * N_HEADS)))
    partials = _rows(16, (0, part_m[0:1]), (1, part_i[2:3]), (2, part_c[3:4]), (3, part_m[2:3]), (4, part_c[0:3]),
                     (7, part_i[0:2]), (9, part_m[1:2]), (10, part_i[3:5]), (13, part_m[3:4]), (14, dlg_row))
    got, _ = _all_gather_small(partials, "gather_partials")
    tot = _sum_devices(got, "sum_partials")
    dmodc = tot[10:13].reshape(1, 3 * D)
    dmod16 = _rows(16, (0, got[:, 7:10, :].reshape(N_DEV, 3 * D)), (N_DEV, dmodc))
    dmod16 = lax.dynamic_slice(dmod16, (0, me * ADA_SH), (16, ADA_SH))
    dmodc8 = _rows(8, (0, dmod16[8:9]))
    g_ada_w, d_ada_w, nm_ada_w, nv_ada_w, cctx_part = _ada_backward(
        act16.T, dmod16, dmodc8, ada_w[0], m_ada_w[0], v_ada_w[0])
    dact_ctx = _sum_devices(_all_gather_small(cctx_part, "gather_cctx")[0], "sum_cctx")

    view = {"c_ctx": (1, D), "final_norm_w": (1, D), "decay_logit": (1, 2 * N_HEADS), "conv_w": (3, csh)}
    given = {"c_ctx": (c_ctx, m_c_ctx, v_c_ctx), "norm_w": (norm_w, m_norm_w, v_norm_w),
             "conv_b": (conv_b, m_conv_b, v_conv_b), "gn_w": (gn_w, m_gn_w, v_gn_w),
             "final_norm_w": (final_norm_w, m_final_norm_w, v_final_norm_w), "ada_b": (ada_b, m_ada_b, v_ada_b),
             "decay_logit": (decay_logit, m_decay_logit, v_decay_logit), "conv_w": (conv_w, m_conv_w, v_conv_w)}
    small_in = [tuple(a.reshape(view.get(name, a.shape)) for a in given[name]) for name in SMALL_PARAMS]
    conv_w_grad = lax.dynamic_slice(tot, (4, me * csh), (3, csh))
    small_out = _adam_small(tot, dact_ctx, conv_w_grad, small_in)
    results = {name: [o.reshape(given[name][0].shape) for o in outs_]
               for name, outs_ in zip(SMALL_PARAMS, small_out)}

    _, (r2_sq,) = rs_sq_chips.wait(small_out[0][0])
    _, (r2_in,) = rs_in_chips.wait(small_out[0][0])
    results["w_in"] = [o[None] for o in _adam_sharded(w_in[0], m_w_in[0], v_w_in[0], own_in, r2_in, "adam_w_in")]
    square = _adam_square([(w_a[0], m_w_a[0], v_w_a[0]), (w_b[0], m_w_b[0], v_w_b[0]),
                           (w_out[0], m_w_out[0], v_w_out[0])], own_sq, r2_sq)
    for name, outs_ in zip(("w_a", "w_b", "w_out"), square):
        results[name] = [o[None] for o in outs_]
    results["ada_w"] = [o[None] for o in (g_ada_w, d_ada_w, nm_ada_w, nv_ada_w)]

    order = ("c_ctx", "norm_w", "ada_w", "ada_b", "w_in", "conv_w", "conv_b", "decay_logit", "gn_w",
             "w_a", "w_b", "w_out", "final_norm_w")
    outs = [results[name][kind] for kind in range(4) for name in order]
    return (tot[13, 0], grad_x[None], *outs)
```

```python
import functools
import math

import jax
import jax.numpy as jnp
from jax import lax
from jax.experimental import pallas as pl
from jax.experimental.pallas import tpu as pltpu

F32 = jnp.float32
BF16 = jnp.bfloat16
MESH = pl.DeviceIdType.MESH
_HBM_SPEC = pl.BlockSpec(memory_space=pltpu.HBM)
_SEM_SPEC = pl.BlockSpec(memory_space=pltpu.SEMAPHORE)
_EFFECT = pltpu.SideEffectType.DATAFLOW_SIDE_EFFECTING

N_DEV = 8
D = 1024
N_HEADS = 8
DK = 64
DV = 128
GRID_W = 64
ROPE_BASE = 10000.0
EPS = 1e-6
PW = 9 * D
WSH = PW // N_DEV
RSH = D // N_DEV
ADA_SH = 3 * D // N_DEV
TM = 256
RET_C = 256
HALO = 16
LANES = 128
VMEM_LIMIT = 60 * 1024 * 1024

ADAM_LR = 0.001
ADAM_B1 = 0.9
ADAM_B2 = 0.999
ADAM_EPS = 1e-08
ADAM_WD = 0.01
ADAM_STEP = 10

CB_H, CB_BG, CB_CG, CB_ZA, CB_QK, CB_V, CB_ZB, CB_GA, CB_GB = range(9)


def _cparams(**kw):
    return pltpu.CompilerParams(vmem_limit_bytes=VMEM_LIMIT, **kw)


def _dot(a, b):
    return jnp.dot(a, b, preferred_element_type=F32)


def _dot_nt(a, b):
    return lax.dot_general(a, b, (((1,), (1,)), ((), ())), preferred_element_type=F32)


def _dot_tn(a, b):
    return lax.dot_general(a, b, (((0,), (0,)), ((), ())), preferred_element_type=F32)


def _sigmoid(z):
    return 0.5 * jnp.tanh(0.5 * z) + 0.5


def _row_tile(rows):
    return TM if rows % TM == 0 else rows


def _coords():
    return lax.axis_index("x"), lax.axis_index("y"), lax.axis_index("c")


def _flip(v, bit):
    return 1 - v if bit else v


def _all_gather_small(blk, name):
    rows, cols = blk.shape

    def body(x_ref, out_ref, done, send_sems, recv_sems, local_sem):
        x, y, c = _coords()
        me = 4 * x + 2 * y + c
        mine = pltpu.make_async_copy(x_ref, out_ref.at[me], local_sem)
        mine.start()

        def copy(k, slot):
            peer = (_flip(x, k & 4), _flip(y, k & 2), _flip(c, k & 1))
            return pltpu.make_async_remote_copy(
                src_ref=x_ref, dst_ref=out_ref.at[slot], send_sem=send_sems.at[k - 1],
                recv_sem=recv_sems.at[k - 1], device_id=peer, device_id_type=MESH)

        for k in range(1, N_DEV):
            copy(k, me).start()
        for k in range(1, N_DEV):
            copy(k, jnp.bitwise_xor(me, k)).wait_recv()
        for k in range(1, N_DEV):
            copy(k, me).wait_send()
        mine.wait()
        done[...] = jnp.zeros((8, LANES), F32)

    vmem = pl.BlockSpec(memory_space=pltpu.VMEM)
    return pl.pallas_call(
        body, name=name,
        out_shape=[jax.ShapeDtypeStruct((N_DEV, rows, cols), blk.dtype), jax.ShapeDtypeStruct((8, LANES), F32)],
        in_specs=[vmem], out_specs=[vmem, vmem],
        scratch_shapes=[pltpu.SemaphoreType.DMA((N_DEV - 1,)), pltpu.SemaphoreType.DMA((N_DEV - 1,)),
                        pltpu.SemaphoreType.DMA],
    )(blk)


class _Absent:
    at = property(lambda self: self)

    def __getitem__(self, index):
        return self


class _Exchange:
    def __init__(self, name, srcs, land_shapes, n_copies, plan, after=None):
        self.name, self.plan, self.n_copies = name, plan, n_copies
        self.n_src, self.n_land = len(srcs), len(land_shapes)
        hbm = lambda a: pltpu.HBM(a.shape, a.dtype)
        n = self.n_src + self.n_land
        lands = [pltpu.with_memory_space_constraint(lax.empty(s.shape, s.dtype), pltpu.HBM) for s in land_shapes]
        srcs = [pltpu.with_memory_space_constraint(s, pltpu.HBM) for s in srcs]
        extra = [] if after is None else [after]

        def body(*refs):
            send_sems, recv_sems = refs[n + len(extra)], refs[n + len(extra) + 1]
            for cp in self._copies(refs, send_sems, recv_sems):
                cp.start()
            refs[-1][...] = jnp.zeros((8, LANES), F32)

        outs = pl.pallas_call(
            body, name=name + "_start",
            out_shape=(pltpu.SemaphoreType.DMA((n_copies,)), pltpu.SemaphoreType.DMA((n_copies,)),
                       *[hbm(a) for a in srcs], *[hbm(a) for a in lands], jax.ShapeDtypeStruct((8, LANES), F32)),
            in_specs=[_HBM_SPEC] * n + [pl.BlockSpec(memory_space=pl.ANY)] * len(extra),
            out_specs=(_SEM_SPEC, _SEM_SPEC, *[_HBM_SPEC] * n, pl.BlockSpec(memory_space=pltpu.VMEM)),
            input_output_aliases={i: 2 + i for i in range(n)},
            compiler_params=pltpu.CompilerParams(has_side_effects=_EFFECT),
        )(*srcs, *lands, *extra)
        self.send_sems, self.recv_sems = outs[0], outs[1]
        self.buffers = list(outs[2:2 + n])
        self.token = outs[-1]
        self.waited = 0

    def _copies(self, refs, send_sems, recv_sems, lo=0, hi=None):
        src_refs, land_refs = refs[:self.n_src], refs[self.n_src:self.n_src + self.n_land]
        planned = list(enumerate(self.plan(src_refs, land_refs)))[lo:hi]
        return [pltpu.make_async_remote_copy(src_ref=s, dst_ref=d, send_sem=send_sems.at[k], recv_sem=recv_sems.at[k],
                                             device_id=dev, device_id_type=MESH) for k, (s, d, dev) in planned]

    def wait(self, after, count=None, through=None):
        n = self.n_src + self.n_land
        lo = self.waited
        hi = self.n_copies if count is None else lo + count
        self.waited = hi
        through = list(range(n)) if through is None else list(through)
        m = len(through)

        def body(*refs):
            held = [_Absent()] * n
            for pos, ref in zip(through, refs[:m]):
                held[pos] = ref
            for cp in self._copies(held, refs[m], refs[m + 1], lo, hi):
                cp.wait_send()
                cp.wait_recv()

        outs = pl.pallas_call(
            body, name=f"{self.name}_wait{lo}" if lo or hi < self.n_copies else self.name + "_wait",
            out_shape=tuple(pltpu.HBM(self.buffers[p].shape, self.buffers[p].dtype) for p in through),
            in_specs=[_HBM_SPEC] * m + [_SEM_SPEC, _SEM_SPEC, pl.BlockSpec(memory_space=pl.ANY)],
            out_specs=tuple([_HBM_SPEC] * m),
            input_output_aliases={i: i for i in range(m)},
            compiler_params=pltpu.CompilerParams(has_side_effects=_EFFECT),
        )(*[self.buffers[p] for p in through], self.send_sems, self.recv_sems, after)
        for pos, out in zip(through, outs):
            self.buffers[pos] = out
        return list(self.buffers[:self.n_src]), list(self.buffers[self.n_src:])


def _pair_plan(src_index):
    def plan(srcs, lands):
        x, y, c = _coords()
        chip = 2 * x + y
        return [(srcs[0].at[src_index(j, chip, c)], lands[0].at[j], (x, y, 1 - c)) for j in range(4)]
    return plan


def _chips_plan(srcs, lands):
    x, y, c = _coords()
    return [(srcs[0].at[j - 1], lands[0].at[j - 1], (_flip(x, j & 2), _flip(y, j & 1), c)) for j in range(1, 4)]


def _peer(k):
    x, y, c = _coords()
    return _flip(x, k & 4), _flip(y, k & 2), _flip(c, k & 1)


def _send_plan(ks):
    def plan(srcs, lands):
        return [(srcs[0].at[0], lands[0].at[p], _peer(k)) for p, k in enumerate(ks)]
    return plan


def _forward_plan(moves):
    def plan(srcs, lands):
        return [(srcs[0].at[s], srcs[0].at[d], _peer(1)) for s, d in moves]
    return plan


def _gather_plan(srcs, lands):
    x, y, c = _coords()
    me = 4 * x + 2 * y + c
    return [(srcs[a], lands[a].at[me], (_flip(x, k & 4), _flip(y, k & 2), _flip(c, k & 1)))
            for a in range(len(srcs)) for k in range(1, N_DEV)]


def _pair_sum(grad, recv, owners, name):
    _, rows, cols = grad.shape
    tr = _row_tile(rows)

    def body(own_ref, g_ref, r_ref, mine_ref, send_ref):
        j = pl.program_id(1)
        total = g_ref[...] + r_ref[...]

        @pl.when(j == 0)
        def _():
            mine_ref[...] = total

        @pl.when(j > 0)
        def _():
            send_ref[...] = total.astype(BF16)

    grid_spec = pltpu.PrefetchScalarGridSpec(
        num_scalar_prefetch=1, grid=(rows // tr, 4),
        in_specs=[pl.BlockSpec((1, tr, cols), lambda i, j, own: (own[j], i, 0)),
                  pl.BlockSpec((1, tr, cols), lambda i, j, own: (j, i, 0))],
        out_specs=[pl.BlockSpec((1, tr, cols), lambda i, j, own: (0, i, 0)),
                   pl.BlockSpec((1, tr, cols), lambda i, j, own: (jnp.maximum(j - 1, 0), i, 0))])
    return pl.pallas_call(
        body, name=name, grid_spec=grid_spec,
        out_shape=[jax.ShapeDtypeStruct((1, rows, cols), F32), jax.ShapeDtypeStruct((3, rows, cols), BF16)],
        compiler_params=_cparams(dimension_semantics=("arbitrary", "arbitrary")),
    )(owners, grad, recv)


def _modulation(c16, ada_w_sh, ada_b_sh, decay_logit):
    def body(c_ref, w_ref, b_ref, dl_ref, mod_ref, act_ref, lg_ref):
        cv = c_ref[...]
        act = cv * _sigmoid(cv)
        act_ref[...] = act
        mod_ref[...] = jnp.dot(act, w_ref[...], preferred_element_type=F32,
                               precision=lax.Precision.HIGHEST) + b_ref[...]
        z = dl_ref[...]
        lg_ref[...] = jnp.minimum(z, 0.0) - jnp.log(1.0 + jnp.exp(-jnp.abs(z)))

    return pl.pallas_call(
        body, name="modulation",
        out_shape=[jax.ShapeDtypeStruct((16, ADA_SH), F32), jax.ShapeDtypeStruct((16, D), F32),
                   jax.ShapeDtypeStruct(decay_logit.shape, F32)],
        compiler_params=_cparams(),
    )(c16, ada_w_sh, ada_b_sh, decay_logit)


def _adam_update(w, g, m, v):
    m2 = ADAM_B1 * m + (1.0 - ADAM_B1) * g
    v2 = ADAM_B2 * v + (1.0 - ADAM_B2) * (g * g)
    m_hat = m2 / (1.0 - ADAM_B1 ** ADAM_STEP)
    v_hat = v2 / (1.0 - ADAM_B2 ** ADAM_STEP)
    delta = -ADAM_LR * (m_hat / (jnp.sqrt(v_hat) + ADAM_EPS) + ADAM_WD * w)
    return delta, m2, v2


def _adam_sharded(w, m, v, own, recv, name):
    rows, cols = w.shape
    tr = _row_tile(rows)

    def body(w_ref, m_ref, v_ref, g0, g1, g2, g3, g_ref, d_ref, m_out, v_out):
        g = ((g0[0] + g1[0].astype(F32)) + g2[0].astype(F32)) + g3[0].astype(F32)
        delta, m2, v2 = _adam_update(w_ref[...], g, m_ref[...], v_ref[...])
        g_ref[...] = g
        d_ref[...] = delta
        m_out[...] = m2
        v_out[...] = v2

    flat = pl.BlockSpec((tr, cols), lambda i: (i, 0))
    part = lambda j: pl.BlockSpec((1, tr, cols), lambda i, j=j: (j, i, 0))
    return pl.pallas_call(
        body, name=name, grid=(rows // tr,),
        in_specs=[flat, flat, flat, part(0), part(0), part(1), part(2)],
        out_specs=[flat] * 4,
        out_shape=[jax.ShapeDtypeStruct((rows, cols), F32)] * 4,
        compiler_params=_cparams(dimension_semantics=("arbitrary",)),
    )(w, m, v, own, recv, recv, recv)


def _ada_backward(act_t, dmod16, dmodc8, ada_w_sh, m, v):
    def body(at_ref, dm_ref, dc_ref, w_ref, m_ref, v_ref, g_ref, d_ref, m_out, v_out, pc_ref):
        g = jnp.dot(at_ref[...], dm_ref[...], preferred_element_type=F32, precision=lax.Precision.HIGHEST)
        w = w_ref[...]
        delta, m2, v2 = _adam_update(w, g, m_ref[...], v_ref[...])
        g_ref[...] = g
        d_ref[...] = delta
        m_out[...] = m2
        v_out[...] = v2
        pc_ref[...] = lax.dot_general(dc_ref[...], w, (((1,), (1,)), ((), ())), preferred_element_type=F32,
                                      precision=lax.Precision.HIGHEST)

    return pl.pallas_call(
        body, name="ada_backward",
        out_shape=[jax.ShapeDtypeStruct((D, ADA_SH), F32)] * 4 + [jax.ShapeDtypeStruct((8, D), F32)],
        compiler_params=_cparams(),
    )(act_t, dmod16, dmodc8, ada_w_sh, m, v)


def _sum_devices(gathered, name):
    _, rows, cols = gathered.shape

    def body(g_ref, o_ref):
        acc = g_ref[0]
        for d in range(1, N_DEV):
            acc = acc + g_ref[d]
        o_ref[...] = acc

    return pl.pallas_call(body, name=name, out_shape=jax.ShapeDtypeStruct((rows, cols), F32),
                          compiler_params=_cparams())(gathered)


SMALL_PARAMS = ("c_ctx", "norm_w", "conv_b", "gn_w", "final_norm_w", "ada_b", "decay_logit", "conv_w")


def _adam_small(tot, dact_ctx, conv_w_grad, params):
    n = len(SMALL_PARAMS)

    def body(tot_ref, dact_ref, cwg_ref, *refs):
        ins, outs = refs[:3 * n], refs[3 * n:]
        nh = 2 * N_HEADS
        raw = {
            "c_ctx": dact_ref[0:1, :],
            "norm_w": tot_ref[1:2, :], "conv_b": tot_ref[2:3, :], "gn_w": tot_ref[3:4, :],
            "final_norm_w": tot_ref[0:1, :],
            "ada_b": jnp.concatenate([tot_ref[7 + r:8 + r, :] + tot_ref[10 + r:11 + r, :] for r in range(3)], axis=1),
            "decay_logit": tot_ref[14:15, 0:nh],
            "conv_w": cwg_ref[...],
        }
        for k, name in enumerate(SMALL_PARAMS):
            w, m, v = ins[3 * k][...], ins[3 * k + 1][...], ins[3 * k + 2][...]
            g = raw[name]
            if name == "c_ctx":
                s = _sigmoid(w)
                g = g * (s * (1.0 + w * (1.0 - s)))
            elif name == "decay_logit":
                g = g * (1.0 - _sigmoid(w))
            delta, m2, v2 = _adam_update(w, g, m, v)
            for ref, val in zip(outs[4 * k:4 * k + 4], (g, delta, m2, v2)):
                ref[...] = val

    flat = [a for trio in params for a in trio]
    out_shape = [jax.ShapeDtypeStruct(trio[0].shape, F32) for trio in params for _ in range(4)]
    outs = pl.pallas_call(body, name="adam_small", out_shape=out_shape, compiler_params=_cparams())(
        tot, dact_ctx, conv_w_grad, *flat)
    return [outs[4 * k:4 * k + 4] for k in range(n)]


def _adam_square(params, own, recv):
    def body(own_ref, recv_ref, *refs):
        ins, outs = refs[:9], refs[9:]
        for k in range(3):
            rows = slice(k * RSH, (k + 1) * RSH)
            g = own_ref[0, rows, :]
            for j in range(3):
                g = g + recv_ref[j, rows, :].astype(F32)
            delta, m2, v2 = _adam_update(ins[3 * k][...], g, ins[3 * k + 1][...], ins[3 * k + 2][...])
            for ref, val in zip(outs[4 * k:4 * k + 4], (g, delta, m2, v2)):
                ref[...] = val

    flat = [a for trio in params for a in trio]
    outs = pl.pallas_call(body, name="adam_square", out_shape=[jax.ShapeDtypeStruct((RSH, D), F32)] * 12,
                          compiler_params=_cparams())(own, recv, *flat)
    return [outs[4 * k:4 * k + 4] for k in range(3)]


def _rope(t, cos, sin):
    lane = lax.broadcasted_iota(jnp.int32, (1, LANES), 1)
    first_half = jnp.bitwise_and(lane, DK // 2) == 0
    partner = jnp.where(first_half, pltpu.roll(t, LANES - DK // 2, 1), pltpu.roll(t, DK // 2, 1))
    return t * cos + partner * sin


class _Shards:
    def __init__(self, arrays, picks):
        self.arrays, self.picks = list(arrays), list(picks)

    def specs(self, index_args):
        out = []
        for a, arr in enumerate(self.arrays):
            steps = [p for p, (ai, _) in enumerate(self.picks) if ai == a]
            first, slot0, last = steps[0], self.picks[steps[0]][1], arr.shape[0] - 1
            out.append(pl.BlockSpec(
                (1, D, WSH), lambda *args, first=first, slot0=slot0, last=last:
                (jnp.clip(index_args(*args) - first + slot0, 0, last), 0, 0)))
        return out

    def __add__(self, other):
        shift = len(self.arrays)
        return _Shards(self.arrays + other.arrays, self.picks + [(a + shift, s) for a, s in other.picks])


def _load_w_in(step, ids_ref, shards, w_refs, w_vmem, sems):
    @pl.when(step == 0)
    def _():
        copies = []
        for p, (a, slot) in enumerate(shards.picks):
            col = pl.multiple_of(ids_ref[p] * WSH, LANES)
            copies.append(pltpu.make_async_copy(w_refs[a].at[slot], w_vmem.at[:, pl.ds(col, WSH)], sems.at[p]))
        for cp in copies:
            cp.start()
        for cp in copies:
            cp.wait()


def _project_columns(xmb, w_ref, shard, cos, sin, p_ref):
    groups = 2 * WSH // LANES
    q0, k0, v0 = CB_QK * D // LANES, (CB_QK * D + N_HEADS * DK) // LANES, CB_V * D // LANES
    mxu_cols = 2 * LANES
    for a in range(0, 2 * WSH, mxu_cols):
        acc = _dot(xmb, w_ref[:, a:a + mxu_cols])
        for col in range(0, mxu_cols, LANES):
            grp = shard * groups + (a + col) // LANES
            rotary = jnp.logical_and(grp >= q0, grp < v0)
            scale = jnp.where(grp >= k0, DK ** -0.5, 1.0)
            piece = acc[:, col:col + LANES]
            turned = _rope(piece * scale, cos, sin)
            p_ref[:, a + col:a + col + LANES] = jnp.where(rotary, turned, piece).astype(BF16)


def _prenorm(x, ctx, modx, modc, norm_w):
    n_x_tiles = x.shape[0] // TM
    lext = x.shape[0] + ctx.shape[0]

    def body(x_ref, ctx_ref, mx_ref, mc_ref, nw_ref, xm_ref, xmt_ref, cos_ref, sin_ref):
        i = pl.program_id(0)
        is_ctx = i >= n_x_tiles
        x = jnp.where(is_ctx, ctx_ref[...], x_ref[...])
        r = lax.rsqrt(jnp.mean(x * x, axis=-1, keepdims=True) + EPS)
        sh = jnp.where(is_ctx, mc_ref[0:1, :], mx_ref[0:1, :])
        sc = jnp.where(is_ctx, mc_ref[1:2, :], mx_ref[1:2, :])
        xm = (x * r * nw_ref[...]) * (1.0 + sc) + sh
        xm_ref[...] = xm.astype(BF16)
        xmt_ref[...] = xm.T.astype(BF16)
        pos = i * TM + lax.broadcasted_iota(jnp.int32, (TM, 1), 0)
        lane = lax.broadcasted_iota(jnp.int32, (1, LANES), 1)
        n_freq = DK // 4
        inv = jnp.exp((lane % n_freq).astype(F32) * (-math.log(ROPE_BASE) / n_freq))
        by_column = (lane % (2 * n_freq)) >= n_freq
        ang = jnp.where(by_column, (pos % GRID_W).astype(F32), (pos // GRID_W).astype(F32)) * inv
        first_half = (lane % DK) < DK // 2
        cos_ref[...] = jnp.where(is_ctx, 1.0, jnp.cos(ang))
        sin_ref[...] = jnp.where(is_ctx, 0.0, jnp.where(first_half, -jnp.sin(ang), jnp.sin(ang)))

    full = lambda a: pl.BlockSpec(a.shape, lambda i: (0,) * a.ndim)
    row = lambda width: pl.BlockSpec((TM, width), lambda i: (i, 0))
    return pl.pallas_call(
        body, name="prenorm", grid=(lext // TM,),
        in_specs=[pl.BlockSpec((TM, D), lambda i: (jnp.minimum(i, n_x_tiles - 1), 0)), full(ctx),
                  full(modx), full(modc), full(norm_w)],
        out_specs=[row(D), pl.BlockSpec((D, TM), lambda i: (0, i)), row(LANES), row(LANES)],
        out_shape=[jax.ShapeDtypeStruct((lext, D), BF16), jax.ShapeDtypeStruct((D, lext), BF16),
                   jax.ShapeDtypeStruct((lext, LANES), F32), jax.ShapeDtypeStruct((lext, LANES), F32)],
        compiler_params=_cparams(dimension_semantics=("arbitrary",)),
    )(x, ctx, modx, modc, norm_w)


def _in_projection(p_ext, xm, cos_t, sin_t, w, pairs, me, name):
    lext = xm.shape[0]
    tr = lext // 4
    meta = []
    for (slot_a, k_a), (slot_b, k_b) in pairs:
        assert k_a ^ k_b == 1
        dev_a = jnp.bitwise_xor(me, k_a)
        a_low = dev_a % 2 == 0
        meta += [jnp.where(a_low, slot_a, slot_b), jnp.where(a_low, slot_b, slot_a), dev_a // 2]
    meta = jnp.stack(meta).astype(jnp.int32)

    def body(meta_ref, xm_ref, cos_ref, sin_ref, w_hbm, *rest):
        p_ref, w_ref, sems = rest[-3:]
        j, i = pl.program_id(0), pl.program_id(1)

        @pl.when(i == 0)
        def _():
            copies = [pltpu.make_async_copy(w_hbm.at[meta_ref[3 * j + h]], w_ref.at[:, h * WSH:(h + 1) * WSH],
                                            sems.at[h]) for h in range(2)]
            for cp in copies:
                cp.start()
            for cp in copies:
                cp.wait()

        _project_columns(xm_ref[...], w_ref, meta_ref[3 * j + 2], cos_ref[...], sin_ref[...], p_ref)

    row = lambda width: pl.BlockSpec((tr, width), lambda j, i, meta_: (i, 0))
    grid_spec = pltpu.PrefetchScalarGridSpec(
        num_scalar_prefetch=1, grid=(len(pairs), lext // tr),
        in_specs=[row(D), row(LANES), row(LANES), pl.BlockSpec(memory_space=pl.ANY)]
        + ([] if p_ext is None else [pl.BlockSpec(memory_space=pl.ANY)]),
        out_specs=pl.BlockSpec((tr, 2 * WSH), lambda j, i, meta_: (i, meta_[3 * j + 2])),
        scratch_shapes=[pltpu.VMEM((D, 2 * WSH), BF16), pltpu.SemaphoreType.DMA((2,))])
    return pl.pallas_call(
        body, name=name, grid_spec=grid_spec,
        out_shape=jax.ShapeDtypeStruct((lext, PW), BF16),
        input_output_aliases={} if p_ext is None else {5: 0},
        compiler_params=_cparams(dimension_semantics=("arbitrary", "arbitrary")),
    )(meta, xm, cos_t, sin_t, w, *([] if p_ext is None else [p_ext]))


def _decay_tables(lgf, lgb, n):
    i = lax.broadcasted_iota(jnp.int32, (n, 1), 0).astype(F32)
    return dict(i=i, k_f=jnp.exp(lgf * (n - 1.0 - i)), k_b=jnp.exp(lgb * i),
                q_f=jnp.exp(lgf * (i + 1.0)), q_b=jnp.exp(lgb * (n - i)))


def _decay_matrix(lgf, lgb, n, transposed=False):
    ii = lax.broadcasted_iota(jnp.int32, (n, n), 0)
    jj = lax.broadcasted_iota(jnp.int32, (n, n), 1)
    diff = (jj - ii if transposed else ii - jj).astype(F32)
    low = jnp.exp(lgf * jnp.maximum(diff, 0.0))
    up = jnp.exp(lgb * jnp.maximum(-diff, 0.0))
    return jnp.where(diff > 0, low, jnp.where(diff < 0, up, 2.0)), diff


def _cat_lanes(a, b):
    return jnp.concatenate([a.astype(BF16), b.astype(BF16)], axis=1)


def _retention_forward(p_ext, lg, seq, ctx_len):
    lext = seq + ctx_len
    n_chunks = seq // RET_C
    C = RET_C

    def body(lg_ref, q_ref, k_ref, v_ref, rn_ref, rstd_ref, kv_scr, sf_scr, sb_scr):
        pair = pl.program_id(0)
        lane = lax.broadcasted_iota(jnp.int32, (1, LANES), 1)
        heads = range(2)
        hmask = [(lane // DK == hh).astype(F32) for hh in heads]
        lgf = [lg_ref[0, 2 * pair + hh] for hh in heads]
        lgb = [lg_ref[1, 2 * pair + hh] for hh in heads]
        vs = [slice(hh * DV, (hh + 1) * DV) for hh in heads]
        t = [_decay_tables(lgf[hh], lgb[hh], C) for hh in heads]
        kc_all = k_ref[seq:lext, :].astype(F32)
        s0 = []
        for hh in heads:
            tc = _decay_tables(lgf[hh], lgb[hh], ctx_len)
            kc = kc_all * hmask[hh]
            s0.append(_dot_tn(_cat_lanes(kc * tc["k_f"], kc * tc["k_b"]), v_ref[seq:lext, vs[hh]]))

        def increments(c, carry):
            rows = pl.ds(pl.multiple_of(c * C, C), C)
            k_all = k_ref[rows, :].astype(F32)
            for hh in heads:
                k = k_all * hmask[hh]
                kv_scr[hh, c] = _dot_tn(_cat_lanes(k * t[hh]["k_f"], k * t[hh]["k_b"]), v_ref[rows, vs[hh]])
            return carry

        lax.fori_loop(0, n_chunks, increments, 0, unroll=2)
        gf_c = [jnp.exp(lgf[hh] * C) for hh in heads]
        gb_c = [jnp.exp(lgb[hh] * C) for hh in heads]

        def scan_f(c, s):
            for hh in heads:
                sf_scr[hh, c] = s[hh]
            return tuple(gf_c[hh] * s[hh] + kv_scr[hh, c, 0:LANES, :] for hh in heads)

        def scan_b(n, s):
            c = n_chunks - 1 - n
            for hh in heads:
                sb_scr[hh, c] = s[hh]
            return tuple(gb_c[hh] * s[hh] + kv_scr[hh, c, LANES:2 * LANES, :] for hh in heads)

        lax.fori_loop(0, n_chunks, scan_f, tuple(s0[hh][0:LANES] for hh in heads))
        lax.fori_loop(0, n_chunks, scan_b, tuple(s0[hh][LANES:2 * LANES] for hh in heads))
        dmat = [_decay_matrix(lgf[hh], lgb[hh], C)[0] for hh in heads]

        def outputs(c, carry):
            rows = pl.ds(pl.multiple_of(c * C, C), C)
            q_all = q_ref[rows, :].astype(F32)
            k = k_ref[rows, :]
            for hh in heads:
                q = q_all * hmask[hh]
                v = v_ref[rows, vs[hh]]
                s = _dot_nt(q.astype(BF16), k)
                inner = _dot((s * dmat[hh]).astype(BF16), v)
                states = jnp.concatenate([sf_scr[hh, c], sb_scr[hh, c]], axis=0).astype(BF16)
                ret = inner + _dot(_cat_lanes(q * t[hh]["q_f"], q * t[hh]["q_b"]), states)
                mu = jnp.mean(ret, axis=-1, keepdims=True)
                cen = ret - mu
                rstd = lax.rsqrt(jnp.mean(cen * cen, axis=-1, keepdims=True) + EPS)
                rn_ref[rows, vs[hh]] = cen * rstd
                rstd_ref[rows, vs[hh]] = jnp.broadcast_to(rstd, (C, DV))
            return carry

        lax.fori_loop(0, n_chunks, outputs, 0, unroll=2)

    qk0 = CB_QK * D // LANES
    return pl.pallas_call(
        body, name="retention_forward", grid=(N_HEADS // 2,),
        in_specs=[pl.BlockSpec(memory_space=pltpu.SMEM),
                  pl.BlockSpec((lext, LANES), lambda g: (0, qk0 + g)),
                  pl.BlockSpec((lext, LANES), lambda g: (0, qk0 + N_HEADS // 2 + g)),
                  pl.BlockSpec((lext, 2 * DV), lambda g: (0, CB_V * D // (2 * DV) + g))],
        out_specs=[pl.BlockSpec((seq, 2 * DV), lambda g: (0, g))] * 2,
        out_shape=[jax.ShapeDtypeStruct((seq, D), F32)] * 2,
        scratch_shapes=[pltpu.VMEM((2, n_chunks, 2 * LANES, DV), F32), pltpu.VMEM((2, n_chunks, LANES, DV), F32),
                        pltpu.VMEM((2, n_chunks, LANES, DV), F32)],
        compiler_params=_cparams(dimension_semantics=("arbitrary",)),
    )(lg, p_ext, p_ext, p_ext)


def _retention_backward(p_ext, dret, lg, cos_t, sin_t, seq, ctx_len):
    lext = seq + ctx_len
    n_chunks = seq // RET_C
    C = RET_C

    def body(lg_ref, q_ref, k_ref, v_ref, do_ref, cos_ref, sin_ref, dq_ref, dk_ref, dv_ref, dlg_ref,
             kv_scr, g_scr, sf_scr, sb_scr, gfn_scr, gbp_scr):
        pair = pl.program_id(0)
        lane = lax.broadcasted_iota(jnp.int32, (1, LANES), 1)
        heads = range(2)
        hmask = [(lane // DK == hh).astype(F32) for hh in heads]
        lgf = [lg_ref[0, 2 * pair + hh] for hh in heads]
        lgb = [lg_ref[1, 2 * pair + hh] for hh in heads]
        vs = [slice(hh * DV, (hh + 1) * DV) for hh in heads]
        t = [_decay_tables(lgf[hh], lgb[hh], C) for hh in heads]
        tc = [_decay_tables(lgf[hh], lgb[hh], ctx_len) for hh in heads]
        kc_all = k_ref[seq:lext, :].astype(F32)
        kc = [kc_all * hmask[hh] for hh in heads]
        vc = [v_ref[seq:lext, vs[hh]] for hh in heads]
        kc_cat = [_cat_lanes(kc[hh] * tc[hh]["k_f"], kc[hh] * tc[hh]["k_b"]) for hh in heads]
        s0 = [_dot_tn(kc_cat[hh], vc[hh]) for hh in heads]

        def increments(c, carry):
            rows = pl.ds(pl.multiple_of(c * C, C), C)
            k_all = k_ref[rows, :].astype(F32)
            q_all = q_ref[rows, :].astype(F32)
            for hh in heads:
                k, q = k_all * hmask[hh], q_all * hmask[hh]
                kv_scr[hh, c] = _dot_tn(_cat_lanes(k * t[hh]["k_f"], k * t[hh]["k_b"]), v_ref[rows, vs[hh]])
                g_scr[hh, c] = _dot_tn(_cat_lanes(q * t[hh]["q_f"], q * t[hh]["q_b"]), do_ref[rows, vs[hh]])
            return carry

        lax.fori_loop(0, n_chunks, increments, 0, unroll=2)
        gf_c = [jnp.exp(lgf[hh] * C) for hh in heads]
        gb_c = [jnp.exp(lgb[hh] * C) for hh in heads]

        def scan_f(c, s):
            for hh in heads:
                sf_scr[hh, c] = s[hh]
            return tuple(gf_c[hh] * s[hh] + kv_scr[hh, c, 0:LANES, :] for hh in heads)

        def scan_b(n, s):
            c = n_chunks - 1 - n
            for hh in heads:
                sb_scr[hh, c] = s[hh]
            return tuple(gb_c[hh] * s[hh] + kv_scr[hh, c, LANES:2 * LANES, :] for hh in heads)

        def scan_gf(n, carry):
            c = n_chunks - 1 - n
            for hh in heads:
                gfn_scr[hh, c] = carry[hh]
            return tuple(g_scr[hh, c, 0:LANES, :] + gf_c[hh] * carry[hh] for hh in heads)

        def scan_gb(c, carry):
            for hh in heads:
                gbp_scr[hh, c] = carry[hh]
            return tuple(g_scr[hh, c, LANES:2 * LANES, :] + gb_c[hh] * carry[hh] for hh in heads)

        lax.fori_loop(0, n_chunks, scan_f, tuple(s0[hh][0:LANES] for hh in heads))
        lax.fori_loop(0, n_chunks, scan_b, tuple(s0[hh][LANES:2 * LANES] for hh in heads))
        zero_state = jnp.zeros((LANES, DV), F32)
        gf_first = lax.fori_loop(0, n_chunks, scan_gf, (zero_state, zero_state))
        gb_last = lax.fori_loop(0, n_chunks, scan_gb, (zero_state, zero_state))

        dmat, w_f, w_b = [], [], []
        for hh in heads:
            dm, diff = _decay_matrix(lgf[hh], lgb[hh], C)
            dmat.append(dm)
            w_f.append(jnp.where(diff > 0, diff * dm, 0.0))
            w_b.append(jnp.where(diff < 0, -diff * dm, 0.0))

        def total(a):
            rows_, width = a.shape
            part = jnp.sum(a.reshape(rows_ // 8, 8, width), axis=0)
            return part[:, 0:LANES] + part[:, LANES:2 * LANES] if width == 2 * LANES else part

        def chunk_head(hh, c, rows, q_all, k_all, dlgf, dlgb):
            th = t[hh]
            qm = q_all * hmask[hh]
            km = k_all * hmask[hh]
            qb, kb = qm.astype(BF16), km.astype(BF16)
            v = v_ref[rows, vs[hh]]
            do = do_ref[rows, vs[hh]]
            s = _dot_nt(qb, kb)
            dsd = _dot_nt(do, v)
            ds = (dsd * dmat[hh]).astype(BF16)
            dq_in = _dot(ds, kb)
            dk_in = _dot_tn(ds, qb)
            dv_in = _dot_tn((s * dmat[hh]).astype(BF16), do)
            prod = s * dsd
            dlgf = dlgf + total(prod * w_f[hh])
            dlgb = dlgb + total(prod * w_b[hh])
            sf, sb = sf_scr[hh, c], sb_scr[hh, c]
            states = jnp.concatenate([sf, sb], axis=0).astype(BF16)
            dqc = _dot_nt(do, states)
            dqf = dqc[:, 0:LANES] * th["q_f"]
            dqb = dqc[:, LANES:2 * LANES] * th["q_b"]
            dq = (dq_in + dqf + dqb) * hmask[hh]
            dlgf = dlgf + total((th["i"] + 1.0) * (qm * dqf))
            dlgb = dlgb + total((C - th["i"]) * (qm * dqb))
            gfn, gbp = gfn_scr[hh, c], gbp_scr[hh, c]
            gstates = jnp.concatenate([gfn, gbp], axis=0).astype(BF16)
            dkc = _dot_nt(v, gstates)
            dkf = dkc[:, 0:LANES] * th["k_f"]
            dkb = dkc[:, LANES:2 * LANES] * th["k_b"]
            dk = dk_in + dkf + dkb
            dlgf = dlgf + total((C - 1.0 - th["i"]) * (km * dkf)) + C * gf_c[hh] * total(gfn * sf)
            dlgb = dlgb + total(th["i"] * (km * dkb)) + C * gb_c[hh] * total(gbp * sb)
            dv = dv_in + _dot(_cat_lanes(km * th["k_f"], km * th["k_b"]), gstates)
            dv_ref[rows, vs[hh]] = dv.astype(BF16)
            return dq, dk, dlgf, dlgb

        def chunk(c, carry):
            rows = pl.ds(pl.multiple_of(c * C, C), C)
            q_all = q_ref[rows, :].astype(F32)
            k_all = k_ref[rows, :].astype(F32)
            dq0, dk0, f0, b0 = chunk_head(0, c, rows, q_all, k_all, carry[0], carry[1])
            dq1, dk1, f1, b1 = chunk_head(1, c, rows, q_all, k_all, carry[2], carry[3])
            cos, sin = cos_ref[rows, :], sin_ref[rows, :]
            dq_ref[rows, :] = _rope(dq0 + dq1, cos, -sin).astype(BF16)
            dk_ref[rows, :] = (_rope(dk0 + dk1, cos, -sin) * (DK ** -0.5)).astype(BF16)
            return f0, b0, f1, b1

        zero = jnp.zeros((8, LANES), F32)
        sums = lax.fori_loop(0, n_chunks, chunk, (zero, zero, zero, zero), unroll=2)

        dlg = []
        dk_ctx = jnp.zeros((ctx_len, LANES), F32)
        for hh in heads:
            g0 = jnp.concatenate([gf_first[hh], gb_last[hh]], axis=0).astype(BF16)
            dkcc = _dot_nt(vc[hh], g0)
            dkcf = dkcc[:, 0:LANES] * tc[hh]["k_f"]
            dkcb = dkcc[:, LANES:2 * LANES] * tc[hh]["k_b"]
            dlgf = sums[2 * hh] + total((ctx_len - 1.0 - tc[hh]["i"]) * (kc[hh] * dkcf))
            dlgb = sums[2 * hh + 1] + total(tc[hh]["i"] * (kc[hh] * dkcb))
            dk_ctx = dk_ctx + (dkcf + dkcb) * (DK ** -0.5)
            dv_ref[seq:lext, vs[hh]] = _dot(kc_cat[hh], g0).astype(BF16)
            dlg += [jnp.sum(jnp.sum(a, axis=1, keepdims=True), axis=0, keepdims=True) for a in (dlgf, dlgb)]
        dk_ref[seq:lext, :] = dk_ctx.astype(BF16)
        dq_ref[seq:lext, :] = jnp.zeros((ctx_len, LANES), BF16)

        lane8 = lax.broadcasted_iota(jnp.int32, (8, LANES), 1)
        out = jnp.zeros((8, LANES), F32)
        for n, val in enumerate(dlg):
            out = jnp.where(lane8 == n, val, out)
        dlg_ref[0] = out

    qk0 = CB_QK * D // LANES
    q_spec = pl.BlockSpec((lext, LANES), lambda g: (0, qk0 + g))
    k_spec = pl.BlockSpec((lext, LANES), lambda g: (0, qk0 + N_HEADS // 2 + g))
    v_spec = pl.BlockSpec((lext, 2 * DV), lambda g: (0, CB_V * D // (2 * DV) + g))
    table = pl.BlockSpec((lext, LANES), lambda g: (0, 0))
    state = pltpu.VMEM((2, n_chunks, LANES, DV), F32)
    return pl.pallas_call(
        body, name="retention_backward", grid=(N_HEADS // 2,),
        in_specs=[pl.BlockSpec(memory_space=pltpu.SMEM), q_spec, k_spec, v_spec,
                  pl.BlockSpec((seq, 2 * DV), lambda g: (0, g)), table, table],
        out_specs=[pl.BlockSpec((lext, LANES), lambda g: (0, g)), pl.BlockSpec((lext, LANES), lambda g: (0, g)),
                   pl.BlockSpec((lext, 2 * DV), lambda g: (0, g)), pl.BlockSpec((1, 8, LANES), lambda g: (g, 0, 0))],
        out_shape=[jax.ShapeDtypeStruct((lext, N_HEADS * DK), BF16), jax.ShapeDtypeStruct((lext, N_HEADS * DK), BF16),
                   jax.ShapeDtypeStruct((lext, D), BF16), jax.ShapeDtypeStruct((N_HEADS // 2, 8, LANES), F32)],
        scratch_shapes=[pltpu.VMEM((2, n_chunks, 2 * LANES, DV), F32), pltpu.VMEM((2, n_chunks, 2 * LANES, DV), F32),
                        state, state, state, state],
        compiler_params=_cparams(dimension_semantics=("arbitrary",)),
    )(lg, p_ext, p_ext, p_ext, dret, cos_t, sin_t)


class _ColumnWriter:
    def __init__(self, dst_hbm, stage, sems, pieces, n_steps):
        self.dst, self.stage, self.sems, self.pieces, self.n_steps = dst_hbm, stage, sems, pieces, n_steps

    def _copies(self, tile):
        rows = pl.ds(pl.multiple_of(tile * TM, TM), TM)
        return [pltpu.make_async_copy(self.stage.at[:, pl.ds(src, width)], self.dst.at[rows, pl.ds(dst, width)],
                                      self.sems.at[k]) for k, (src, dst, width) in enumerate(self.pieces)]

    def free(self, step):
        @pl.when(step >= 1)
        def _():
            for cp in self._copies(step - 1):
                cp.wait()

    def send(self, step):
        for cp in self._copies(step):
            cp.start()

        @pl.when(step == self.n_steps - 1)
        def _():
            for cp in self._copies(step):
                cp.wait()


def _merge(p_ext, rn, rstd, x, target, w_a, w_b, w_out, vecs, seq):
    n_tiles = seq // TM
    hb = TM // HALO
    dp_pieces = [(0, CB_BG * D, D), (D, CB_ZA * D, D), (2 * D, CB_ZB * D, 3 * D)]

    def body(h_ref, bg_ref, cg_ref, za_ref, zb_ref, ga_ref, gb_ref, hp_ref, hn_ref, cp_ref, cn_ref,
             rn_ref, rstd_ref, x_ref, t_ref, wa_ref, wb_ref, wo_ref, vec_ref,
             dx1_ref, dp_hbm, dconv_ref, dret_ref, at_ref, b_ref, part_ref, stage, dp_sems):
        i = pl.program_id(0)
        writer = _ColumnWriter(dp_hbm, stage, dp_sems, dp_pieces, n_tiles)
        dpa_ref = stage
        writer.free(i)
        f = lambda ref: ref[...].astype(F32)
        h, bg, cg, za, zb, ga, gb = f(h_ref), f(bg_ref), f(cg_ref), f(za_ref), f(zb_ref), f(ga_ref), f(gb_ref)
        gx, w0, w1, w2 = vec_ref[0:1, :], vec_ref[1:2, :], vec_ref[2:3, :], vec_ref[3:4, :]
        cb, gnw, fw = vec_ref[4:5, :], vec_ref[5:6, :], vec_ref[6:7, :]
        u = cg * h
        row = lax.broadcasted_iota(jnp.int32, (TM, 1), 0)
        u_prev = (f(cp_ref) * f(hp_ref))[HALO - 1:HALO, :]
        u_next = (f(cn_ref) * f(hn_ref))[0:1, :]
        u_prev = jnp.where(i == 0, 0.0, u_prev)
        u_next = jnp.where(i == n_tiles - 1, 0.0, u_next)
        u_up = jnp.where(row == 0, u_prev, pltpu.roll(u, 1, 0))
        u_dn = jnp.where(row == TM - 1, u_next, pltpu.roll(u, TM - 1, 0))
        conv = w0 * u_up + w1 * u + w2 * u_dn + cb
        sza = _sigmoid(za)
        silu_za = za * sza
        a_act = silu_za * bg * conv
        rn = rn_ref[...]
        szb = _sigmoid(zb)
        silu_zb = zb * szb
        rg = rn * gnw
        b_act = silu_zb * rg
        y_a = _dot(a_act.astype(BF16), wa_ref[...])
        y_b = _dot(b_act.astype(BF16), wb_ref[...])
        sga, sgb = _sigmoid(ga), _sigmoid(gb)
        mix = sga * y_a + sgb * y_b
        y = _dot(mix.astype(BF16), wo_ref[...])
        x1 = x_ref[...] + gx * y
        r1 = lax.rsqrt(jnp.mean(x1 * x1, axis=-1, keepdims=True) + EPS)
        xh1 = x1 * r1
        err = xh1 * fw - t_ref[...]
        loss = jnp.sum(jnp.sum(err * err, axis=1, keepdims=True), axis=0, keepdims=True) * (0.5 / D)
        dout = err * (1.0 / D)
        dxh = dout * fw
        dx1 = r1 * (dxh - xh1 * jnp.mean(dxh * xh1, axis=-1, keepdims=True))
        dx1_ref[...] = dx1
        dy = (dx1 * gx).astype(BF16)
        dmix = _dot_nt(dy, wo_ref[...])
        dya = (dmix * sga).astype(BF16)
        dyb = (dmix * sgb).astype(BF16)
        da = _dot_nt(dya, wa_ref[...])
        db = _dot_nt(dyb, wb_ref[...])
        dpa_ref[:, 0:D] = (da * silu_za * conv).astype(BF16)
        dpa_ref[:, D:2 * D] = (da * bg * conv * (sza * (1.0 + za * (1.0 - sza)))).astype(BF16)
        dpa_ref[:, 2 * D:3 * D] = (db * rg * (szb * (1.0 + zb * (1.0 - szb)))).astype(BF16)
        dpa_ref[:, 3 * D:4 * D] = (dmix * y_a * sga * (1.0 - sga)).astype(BF16)
        dpa_ref[:, 4 * D:5 * D] = (dmix * y_b * sgb * (1.0 - sgb)).astype(BF16)
        writer.send(i)
        dconv_ref[...] = (da * silu_za * bg).astype(BF16)
        drn_n = db * silu_zb
        drn = drn_n * gnw
        rstd = rstd_ref[...]
        for hd in range(N_HEADS):
            sl = slice(hd * DV, (hd + 1) * DV)
            dh_, rh = drn[:, sl], rn[:, sl]
            m1 = jnp.mean(dh_, axis=-1, keepdims=True)
            m2 = jnp.mean(dh_ * rh, axis=-1, keepdims=True)
            dret_ref[:, sl] = (rstd[:, sl] * (dh_ - m1 - rh * m2)).astype(BF16)
        at_ref[0] = a_act.T.astype(BF16)
        at_ref[1] = b_act.T.astype(BF16)
        at_ref[2] = mix.T.astype(BF16)
        b_ref[0] = dya
        b_ref[1] = dyb
        b_ref[2] = dy

        @pl.when(i == 0)
        def _():
            part_ref[...] = jnp.zeros((8, D), F32)

        part_ref[0:1, :] += jnp.sum(dout * xh1, axis=0, keepdims=True)
        part_ref[1:2, :] += jnp.sum(dx1 * y, axis=0, keepdims=True)
        part_ref[2:3, :] += jnp.sum(drn_n * rn, axis=0, keepdims=True)
        part_ref[3:4, :] += jnp.broadcast_to(loss, (1, D))

    col = lambda cb_: pl.BlockSpec((TM, D), lambda i, cb_=cb_: (i, cb_))
    prev = lambda cb_: pl.BlockSpec((HALO, D), lambda i, cb_=cb_: (jnp.maximum(i * hb - 1, 0), cb_))
    nxt = lambda cb_: pl.BlockSpec((HALO, D), lambda i, cb_=cb_: (jnp.minimum((i + 1) * hb, n_tiles * hb - 1), cb_))
    tile = pl.BlockSpec((TM, D), lambda i: (i, 0))
    full = lambda a: pl.BlockSpec(a.shape, lambda i: (0,) * a.ndim, pipeline_mode=pl.Buffered(1))
    return pl.pallas_call(
        body, name="merge", grid=(n_tiles,),
        in_specs=[col(CB_H), col(CB_BG), col(CB_CG), col(CB_ZA), col(CB_ZB), col(CB_GA), col(CB_GB),
                  prev(CB_H), nxt(CB_H), prev(CB_CG), nxt(CB_CG),
                  tile, tile, tile, tile, full(w_a), full(w_b), full(w_out), full(vecs)],
        out_specs=[tile, pl.BlockSpec(memory_space=pl.ANY), tile, tile,
                   pl.BlockSpec((3, D, TM), lambda i: (0, 0, i)), pl.BlockSpec((3, TM, D), lambda i: (0, i, 0)),
                   pl.BlockSpec((8, D), lambda i: (0, 0))],
        out_shape=[jax.ShapeDtypeStruct((seq, D), F32), jax.ShapeDtypeStruct((p_ext.shape[0], PW), BF16),
                   jax.ShapeDtypeStruct((seq, D), BF16), jax.ShapeDtypeStruct((seq, D), BF16),
                   jax.ShapeDtypeStruct((3, D, seq), BF16), jax.ShapeDtypeStruct((3, seq, D), BF16),
                   jax.ShapeDtypeStruct((8, D), F32)],
        scratch_shapes=[pltpu.VMEM((TM, 5 * D), BF16), pltpu.SemaphoreType.DMA((len(dp_pieces),))],
        compiler_params=_cparams(dimension_semantics=("arbitrary",)),
    )(p_ext, p_ext, p_ext, p_ext, p_ext, p_ext, p_ext, p_ext, p_ext, p_ext, p_ext,
      rn, rstd, x, target, w_a, w_b, w_out, vecs)


def _conv_backward(p_ext, dconv, dp, dq, dk, dv, vecs, seq):
    n_tiles = seq // TM
    hb = TM // HALO
    qk = N_HEADS * DK
    pieces = [(0, CB_H * D, D), (D, CB_CG * D, D), (2 * D, CB_QK * D, 2 * D)]
    zero_pieces = [(CB_BG * D, D), (CB_ZA * D, D), (CB_ZB * D, 3 * D)]

    def body(h_ref, cg_ref, dc_ref, dcp_ref, dcn_ref, dq_ref, dk_ref, dv_ref, vec_ref, dp_in, dp_hbm, part_ref,
             stage, sems, zeros, zero_sems):
        del dp_in
        i = pl.program_id(0)
        writer = _ColumnWriter(dp_hbm, stage, sems, pieces, n_tiles + 1)
        out = stage

        @pl.when(i == 0)
        def _():
            part_ref[...] = jnp.zeros((8, D), F32)

        @pl.when(i == n_tiles)
        def _():
            writer.free(i)
            out[:, 0:2 * D] = jnp.zeros((TM, 2 * D), BF16)
            zeros[...] = jnp.zeros(zeros.shape, BF16)
            rows = pl.ds(n_tiles * TM, TM)
            fills = [pltpu.make_async_copy(zeros.at[:, pl.ds(0, width)], dp_hbm.at[rows, pl.ds(dst, width)],
                                           zero_sems.at[k]) for k, (dst, width) in enumerate(zero_pieces)]
            for cp in fills:
                cp.start()
            for cp in fills:
                cp.wait()

        @pl.when(i < n_tiles)
        def _():
            f = lambda ref: ref[...].astype(F32)
            h, cg, dc = f(h_ref), f(cg_ref), f(dc_ref)
            w0, w1, w2 = vec_ref[1:2, :], vec_ref[2:3, :], vec_ref[3:4, :]
            row = lax.broadcasted_iota(jnp.int32, (TM, 1), 0)
            dc_prev = jnp.where(i == 0, 0.0, f(dcp_ref)[HALO - 1:HALO, :])
            dc_next = jnp.where(i == n_tiles - 1, 0.0, f(dcn_ref)[0:1, :])
            dc_up = jnp.where(row == 0, dc_prev, pltpu.roll(dc, 1, 0))
            dc_dn = jnp.where(row == TM - 1, dc_next, pltpu.roll(dc, TM - 1, 0))
            du = w0 * dc_dn + w1 * dc + w2 * dc_up
            u = cg * h
            part_ref[0:1, :] += jnp.sum(u * dc_dn, axis=0, keepdims=True)
            part_ref[1:2, :] += jnp.sum(u * dc, axis=0, keepdims=True)
            part_ref[2:3, :] += jnp.sum(u * dc_up, axis=0, keepdims=True)
            part_ref[3:4, :] += jnp.sum(dc, axis=0, keepdims=True)
            dh, dcg = (du * cg).astype(BF16), (du * h).astype(BF16)
            writer.free(i)
            out[:, 0:D] = dh
            out[:, D:2 * D] = dcg

        out[:, 2 * D:2 * D + qk] = dq_ref[...]
        out[:, 2 * D + qk:3 * D] = dk_ref[...]
        out[:, 3 * D:4 * D] = dv_ref[...]
        writer.send(i)

    last = n_tiles - 1
    col = lambda cb_: pl.BlockSpec((TM, D), lambda i, cb_=cb_: (jnp.minimum(i, last), cb_))
    lat = lambda w: pl.BlockSpec((TM, w), lambda i: (jnp.minimum(i, last), 0))
    ext = lambda w: pl.BlockSpec((TM, w), lambda i: (i, 0))
    anyspec = pl.BlockSpec(memory_space=pl.ANY)
    return pl.pallas_call(
        body, name="conv_backward", grid=(n_tiles + 1,),
        in_specs=[col(CB_H), col(CB_CG), lat(D),
                  pl.BlockSpec((HALO, D), lambda i: (jnp.clip(i * hb - 1, 0, n_tiles * hb - 1), 0)),
                  pl.BlockSpec((HALO, D), lambda i: (jnp.minimum((i + 1) * hb, n_tiles * hb - 1), 0)),
                  ext(qk), ext(qk), ext(D), pl.BlockSpec(vecs.shape, lambda i: (0, 0)), anyspec],
        out_specs=[anyspec, pl.BlockSpec((8, D), lambda i: (0, 0))],
        out_shape=[jax.ShapeDtypeStruct(dp.shape, BF16), jax.ShapeDtypeStruct((8, D), F32)],
        input_output_aliases={9: 0},
        scratch_shapes=[pltpu.VMEM((TM, 4 * D), BF16), pltpu.SemaphoreType.DMA((len(pieces),)),
                        pltpu.VMEM((TM, 3 * D), BF16), pltpu.SemaphoreType.DMA((len(zero_pieces),))],
        compiler_params=_cparams(dimension_semantics=("arbitrary",)),
    )(p_ext, p_ext, dconv, dconv, dconv, dq, dk, dv, vecs, dp)


def _input_backward(dp, shards, ids, x, ctx, dx1, modx, modc, norm_w):
    seq = x.shape[0]
    lext = seq + ctx.shape[0]
    n_x = seq // TM
    n_w = len(shards.arrays)

    def body(ids_ref, dp_ref, x_ref, ctx_ref, dx1_ref, mx_ref, mc_ref, nw_ref, *rest):
        w_hbm, (gx_ref, part_ref, w_ref, w_sems) = rest[:n_w], rest[n_w:]
        i = pl.program_id(0)
        _load_w_in(i, ids_ref, shards, w_hbm, w_ref, w_sems)
        is_ctx = i >= n_x
        dxm = _dot_nt(dp_ref[...], w_ref[...])
        x = jnp.where(is_ctx, ctx_ref[...], x_ref[...])
        r = lax.rsqrt(jnp.mean(x * x, axis=-1, keepdims=True) + EPS)
        xh = x * r
        nw = nw_ref[...]
        sc = jnp.where(is_ctx, mc_ref[1:2, :], mx_ref[1:2, :])
        dxn = dxm * (1.0 + sc)
        dxh = dxn * nw
        dx = r * (dxh - xh * jnp.mean(dxh * xh, axis=-1, keepdims=True))

        @pl.when(jnp.logical_not(is_ctx))
        def _():
            gx_ref[...] = dx1_ref[...] + dx

        @pl.when(i == 0)
        def _():
            part_ref[...] = jnp.zeros((8, D), F32)

        fx = jnp.where(is_ctx, 0.0, 1.0)
        d_shift = jnp.sum(dxm, axis=0, keepdims=True)
        d_scale = jnp.sum(dxm * (xh * nw), axis=0, keepdims=True)
        part_ref[0:1, :] += fx * d_shift
        part_ref[1:2, :] += fx * d_scale
        part_ref[2:3, :] += jnp.sum(dxn * xh, axis=0, keepdims=True)
        part_ref[3:4, :] += (1.0 - fx) * d_shift
        part_ref[4:5, :] += (1.0 - fx) * d_scale

    lat = lambda w: pl.BlockSpec((TM, w), lambda i, ids_: (jnp.minimum(i, n_x - 1), 0))
    ext = lambda w: pl.BlockSpec((TM, w), lambda i, ids_: (i, 0))
    full = lambda a: pl.BlockSpec(a.shape, lambda i, ids_: (0,) * a.ndim)
    grid_spec = pltpu.PrefetchScalarGridSpec(
        num_scalar_prefetch=1, grid=(lext // TM,),
        in_specs=[ext(PW), lat(D), full(ctx), lat(D), full(modx), full(modc), full(norm_w)]
        + [pl.BlockSpec(memory_space=pl.ANY)] * n_w,
        out_specs=[lat(D), pl.BlockSpec((8, D), lambda i, ids_: (0, 0))],
        scratch_shapes=[pltpu.VMEM((D, PW), BF16), pltpu.SemaphoreType.DMA((N_DEV,))])
    return pl.pallas_call(
        body, name="input_backward", grid_spec=grid_spec,
        out_shape=[jax.ShapeDtypeStruct((seq, D), F32), jax.ShapeDtypeStruct((8, D), F32)],
        compiler_params=_cparams(dimension_semantics=("arbitrary",)),
    )(ids, dp, x, ctx, dx1, modx, modc, norm_w, *shards.arrays)


def _weight_grad_in(xm_t, dp, owners, name):
    lext = xm_t.shape[1]

    def body(own_ref, a_ref, b_ref, o_ref):
        o_ref[0] = _dot(a_ref[...], b_ref[...])

    grid_spec = pltpu.PrefetchScalarGridSpec(
        num_scalar_prefetch=1, grid=(4,),
        in_specs=[pl.BlockSpec((D, lext), lambda j, own: (0, 0)),
                  pl.BlockSpec((lext, WSH), lambda j, own: (0, own[j]))],
        out_specs=pl.BlockSpec((1, D, WSH), lambda j, own: (j, 0, 0)))
    return pl.pallas_call(
        body, name=name, grid_spec=grid_spec,
        out_shape=jax.ShapeDtypeStruct((4, D, WSH), F32),
        compiler_params=_cparams(dimension_semantics=("arbitrary",)),
    )(owners, xm_t, dp)


def _weight_grad_square(a_t, b):
    seq = a_t.shape[2]

    def body(a_ref, b_ref, o_ref):
        o_ref[:, 0] = _dot(a_ref[0], b_ref[0]).reshape(N_DEV, RSH, D)

    return pl.pallas_call(
        body, name="weight_grad_square", grid=(3,),
        in_specs=[pl.BlockSpec((1, D, seq), lambda t: (t, 0, 0)), pl.BlockSpec((1, seq, D), lambda t: (t, 0, 0))],
        out_specs=pl.BlockSpec((N_DEV, 1, RSH, D), lambda t: (0, t, 0, 0)),
        out_shape=jax.ShapeDtypeStruct((N_DEV, 3, RSH, D), F32),
        compiler_params=_cparams(dimension_semantics=("arbitrary",)),
    )(a_t, b)


def _tie(value, token):
    return value + token[0, 0].astype(value.dtype)


def _pad_lanes(a):
    return jnp.pad(a, ((0, 0), (0, D - a.shape[1])))


def _rows(total, *parts):
    width = max(a.shape[1] for _, a in parts)
    out = None
    for row, a in parts:
        padded = jnp.pad(a, ((row, total - row - a.shape[0]), (0, width - a.shape[1])))
        out = padded if out is None else out + padded
    return out


def kernel(x, c, ctx, c_ctx, norm_w, ada_w, ada_b, w_in, conv_w, conv_b, decay_logit, gn_w, w_a, w_b, w_out, final_norm_w, loss_target, m_c_ctx, m_norm_w, m_ada_w, m_ada_b, m_w_in, m_conv_w, m_conv_b, m_decay_logit, m_gn_w, m_w_a, m_w_b, m_w_out, m_final_norm_w, v_c_ctx, v_norm_w, v_ada_w, v_ada_b, v_w_in, v_conv_w, v_conv_b, v_decay_logit, v_gn_w, v_w_a, v_w_b, v_w_out, v_final_norm_w):
    xi, yi, ci = _coords()
    me = 4 * xi + 2 * yi + ci
    chip = 2 * xi + yi
    seq, ctx_len = x.shape[1], ctx.shape[1]
    assert seq % TM == 0 and seq % RET_C == 0 and ctx_len == TM and seq % GRID_W == 0
    csh = D // N_DEV

    blk = jnp.pad(c, ((0, 7), (0, 0))) + jnp.pad(conv_w[0], ((1, 4), (0, D - csh)))
    got, _ = _all_gather_small(blk, "gather_cond")
    conv_w_all = got[:, 1:4, 0:csh].transpose(1, 0, 2).reshape(3, D)
    c16 = _rows(16, (0, got[:, 0, :]), (N_DEV, c_ctx[None]))
    ada_b_sh = lax.dynamic_slice(ada_b, (0, me * ADA_SH), (1, ADA_SH))
    mod_sh, act16, lg = _modulation(c16, ada_w[0], ada_b_sh, decay_logit[0])
    mod_all, small_done = _all_gather_small(mod_sh, "gather_mod")
    mod_all = mod_all.transpose(1, 0, 2).reshape(16, 3 * D)
    modx = lax.dynamic_slice(mod_all, (me, 0), (1, 3 * D)).reshape(3, D)
    modc = mod_all[8].reshape(3, D)

    x2, tgt = x[0], loss_target[0]
    wblock = lambda n: jax.ShapeDtypeStruct((n, D, WSH), BF16)
    ids_of = lambda ks: jnp.stack([jnp.bitwise_xor(me, k) for k in ks]).astype(jnp.int32)
    own_shard = w_in[0].astype(BF16)[None]
    sq_shards = [w[0].astype(BF16) for w in (w_a, w_b, w_out)]
    wave1 = _Exchange(
        "gather_in_first", [jnp.broadcast_to(own_shard, (2, D, WSH)), own_shard], [wblock(4)], 3,
        lambda srcs, lands: _forward_plan([(0, 1)])(srcs, lands) + _send_plan((4, 2))(srcs[1:], lands),
        after=small_done)
    modx = _tie(modx, wave1.token)
    xm, xm_t, cos_t, sin_t = _prenorm(x2, ctx[0], modx, modc, norm_w)
    (w_pair, _), _ = wave1.wait(xm_t, 1, through=[0])
    first = _Shards([w_pair], [(0, 0), (0, 1)])
    p_ext = _in_projection(None, xm, cos_t, sin_t, w_pair, [((0, 0), (1, 1))], me, "in_projection_pair")
    _, (w_nbr,) = wave1.wait(p_ext, through=[1, 2])
    wave2 = _Exchange(
        "gather_in_second", [w_nbr, own_shard], [wblock(2)], 3,
        lambda srcs, lands: _forward_plan([(0, 2), (1, 3)])(srcs, lands) + _send_plan((6,))(srcs[1:], lands))
    (w_nbr, _), _ = wave2.wait(wave2.token, 2, through=[0])
    second = _Shards([w_nbr], [(0, 0), (0, 1), (0, 2), (0, 3)])
    p_ext = _in_projection(p_ext, xm, cos_t, sin_t, w_nbr, [((0, 4), (2, 5)), ((1, 2), (3, 3))], me,
                           "in_projection_neighbours")
    _, (w_diag,) = wave2.wait(p_ext, through=[1, 2])
    wave3 = _Exchange(
        "gather_in_third", [w_diag] + sq_shards, [jax.ShapeDtypeStruct((N_DEV, RSH, D), BF16)] * 3, 1 + 3 * (N_DEV - 1),
        lambda srcs, lands: _forward_plan([(0, 1)])(srcs, lands) + _gather_plan(srcs[1:], lands))
    (w_diag, *_), _ = wave3.wait(wave3.token, 1, through=[0])
    third = _Shards([w_diag], [(0, 0), (0, 1)])
    p_ext = _in_projection(p_ext, xm, cos_t, sin_t, w_diag, [((0, 6), (1, 7))], me, "in_projection_diagonal")
    w_shards, w_ids = first + second + third, ids_of((0, 1, 4, 2, 5, 3, 6, 7))

    rn, rstd = _retention_forward(p_ext, lg, seq, ctx_len)
    _, sq_lands = wave3.wait(rstd, through=range(1, 7))
    w_a_all, w_b_all, w_out_all = (
        lax.dynamic_update_slice(land, shard[None], (me, 0, 0)).reshape(D, D) for land, shard in zip(sq_lands, sq_shards))
    vecs = _rows(8, (0, modx[2:3]), (1, conv_w_all), (4, conv_b), (5, gn_w), (6, final_norm_w[None]))
    dx1, dp, dconv, dret, op_at, op_b, part_m = _merge(p_ext, rn, rstd, x2, tgt, w_a_all, w_b_all, w_out_all, vecs, seq)

    j4 = jnp.arange(4, dtype=jnp.int32)
    owners = (2 * jnp.bitwise_xor(chip, j4) + ci).astype(jnp.int32)
    owners_sib = (2 * jnp.bitwise_xor(chip, j4) + (1 - ci)).astype(jnp.int32)
    gw_sq = _weight_grad_square(op_at, op_b).reshape(N_DEV, 3 * RSH, D)
    rs_sq_pair = _Exchange("rs_square_pair", [gw_sq], [jax.ShapeDtypeStruct((4, 3 * RSH, D), F32)], 4,
                           _pair_plan(lambda j, chip_, c_: 2 * jnp.bitwise_xor(chip_, j) + (1 - c_)))
    dq, dk, dv, dlg = _retention_backward(p_ext, dret, _tie(lg, rs_sq_pair.token), cos_t, sin_t, seq, ctx_len)
    (gw_sq,), (r1_sq,) = rs_sq_pair.wait(dlg)
    own_sq, send_sq = _pair_sum(gw_sq, r1_sq, owners, "pair_sum_square")
    rs_sq_chips = _Exchange("rs_square_chips", [send_sq], [jax.ShapeDtypeStruct((3, 3 * RSH, D), BF16)], 3, _chips_plan)
    dp, part_c = _conv_backward(p_ext, dconv, dp, dq, dk, dv, _tie(vecs, rs_sq_chips.token), seq)
    gw_sib = _weight_grad_in(xm_t, dp, owners_sib, "weight_grad_in_sibling")
    rs_in_pair = _Exchange("rs_in_pair", [gw_sib], [jax.ShapeDtypeStruct((4, D, WSH), F32)], 4,
                           _pair_plan(lambda j, chip_, c_: j))
    gw_own = _weight_grad_in(xm_t, dp, _tie(owners, rs_in_pair.token), "weight_grad_in_own")
    _, (r1_in,) = rs_in_pair.wait(gw_own)
    own_in, send_in = _pair_sum(gw_own, r1_in, j4, "pair_sum_in")
    rs_in_chips = _Exchange("rs_in_chips", [send_in], [jax.ShapeDtypeStruct((3, D, WSH), BF16)], 3, _chips_plan)
    grad_x, part_i = _input_backward(dp, w_shards, w_ids, x2, ctx[0], dx1, _tie(modx, rs_in_chips.token), modc, norm_w)

    dl = dlg[:, 0, 0:4]
    dlg_row = jnp.pad(dl[:, 0::2].reshape(1, N_HEADS), ((0, 0), (0, D - N_HEADS))) + jnp.pad(
        dl[:, 1::2].reshape(1, N_HEADS), ((0, 0), (N_HEADS, D - 2 * N_HEADS)))
    partials = _rows(16, (0, part_m[0:1]), (1, part_i[2:3]), (2, part_c[3:4]), (3, part_m[2:3]), (4, part_c[0:3]),
                     (7, part_i[0:2]), (9, part_m[1:2]), (10, part_i[3:5]), (13, part_m[3:4]), (14, dlg_row))
    got, _ = _all_gather_small(partials, "gather_partials")
    tot = _sum_devices(got, "sum_partials")
    dmodc = tot[10:13].reshape(1, 3 * D)
    dmod16 = _rows(16, (0, got[:, 7:10, :].reshape(N_DEV, 3 * D)), (N_DEV, dmodc))
    dmod16 = lax.dynamic_slice(dmod16, (0, me * ADA_SH), (16, ADA_SH))
    dmodc8 = _rows(8, (0, dmod16[8:9]))
    g_ada_w, d_ada_w, nm_ada_w, nv_ada_w, cctx_part = _ada_backward(
        act16.T, dmod16, dmodc8, ada_w[0], m_ada_w[0], v_ada_w[0])
    dact_ctx = _sum_devices(_all_gather_small(cctx_part, "gather_cctx")[0], "sum_cctx")

    view = {"c_ctx": (1, D), "final_norm_w": (1, D), "decay_logit": (1, 2 * N_HEADS), "conv_w": (3, csh)}
    given = {"c_ctx": (c_ctx, m_c_ctx, v_c_ctx), "norm_w": (norm_w, m_norm_w, v_norm_w),
             "conv_b": (conv_b, m_conv_b, v_conv_b), "gn_w": (gn_w, m_gn_w, v_gn_w),
             "final_norm_w": (final_norm_w, m_final_norm_w, v_final_norm_w), "ada_b": (ada_b, m_ada_b, v_ada_b),
             "decay_logit": (decay_logit, m_decay_logit, v_decay_logit), "conv_w": (conv_w, m_conv_w, v_conv_w)}
    small_in = [tuple(a.reshape(view.get(name, a.shape)) for a in given[name]) for name in SMALL_PARAMS]
    conv_w_grad = lax.dynamic_slice(tot, (4, me * csh), (3, csh))
    small_out = _adam_small(tot, dact_ctx, conv_w_grad, small_in)
    results = {name: [o.reshape(given[name][0].shape) for o in outs_]
               for name, outs_ in zip(SMALL_PARAMS, small_out)}

    _, (r2_sq,) = rs_sq_chips.wait(small_out[0][0])
    _, (r2_in,) = rs_in_chips.wait(small_out[0][0])
    results["w_in"] = [o[None] for o in _adam_sharded(w_in[0], m_w_in[0], v_w_in[0], own_in, r2_in, "adam_w_in")]
    square = _adam_square([(w_a[0], m_w_a[0], v_w_a[0]), (w_b[0], m_w_b[0], v_w_b[0]),
                           (w_out[0], m_w_out[0], v_w_out[0])], own_sq, r2_sq)
    for name, outs_ in zip(("w_a", "w_b", "w_out"), square):
        results[name] = [o[None] for o in outs_]
    results["ada_w"] = [o[None] for o in (g_ada_w, d_ada_w, nm_ada_w, nv_ada_w)]

    order = ("c_ctx", "norm_w", "ada_w", "ada_b", "w_in", "conv_w", "conv_b", "decay_logit", "gn_w",
             "w_a", "w_b", "w_out", "final_norm_w")
    outs = [results[name][kind] for kind in range(4) for name in order]
    return (tot[13, 0], grad_x[None], *outs)
```

```python
import functools
import math

import jax
import jax.numpy as jnp
from jax import lax
from jax.experimental import pallas as pl
from jax.experimental.pallas import tpu as pltpu

F32 = jnp.float32
BF16 = jnp.bfloat16
MESH = pl.DeviceIdType.MESH
_HBM_SPEC = pl.BlockSpec(memory_space=pltpu.HBM)
_SEM_SPEC = pl.BlockSpec(memory_space=pltpu.SEMAPHORE)
_EFFECT = pltpu.SideEffectType.DATAFLOW_SIDE_EFFECTING

N_DEV = 8
D = 1024
N_HEADS = 8
DK = 64
DV = 128
GRID_W = 64
ROPE_BASE = 10000.0
EPS = 1e-6
PW = 9 * D
WSH = PW // N_DEV
RSH = D // N_DEV
ADA_SH = 3 * D // N_DEV
TM = 256
RET_C = 256
HALO = 16
LANES = 128
VMEM_LIMIT = 60 * 1024 * 1024

ADAM_LR = 0.001
ADAM_B1 = 0.9
ADAM_B2 = 0.999
ADAM_EPS = 1e-08
ADAM_WD = 0.01
ADAM_STEP = 10

CB_H, CB_BG, CB_CG, CB_ZA, CB_QK, CB_V, CB_ZB, CB_GA, CB_GB = range(9)


def _cparams(**kw):
    return pltpu.CompilerParams(vmem_limit_bytes=VMEM_LIMIT, **kw)


def _dot(a, b):
    return jnp.dot(a, b, preferred_element_type=F32)


def _dot_nt(a, b):
    return lax.dot_general(a, b, (((1,), (1,)), ((), ())), preferred_element_type=F32)


def _dot_tn(a, b):
    return lax.dot_general(a, b, (((0,), (0,)), ((), ())), preferred_element_type=F32)


def _sigmoid(z):
    return 0.5 * jnp.tanh(0.5 * z) + 0.5


def _row_tile(rows):
    return TM if rows % TM == 0 else rows


def _coords():
    return lax.axis_index("x"), lax.axis_index("y"), lax.axis_index("c")


def _flip(v, bit):
    return 1 - v if bit else v


def _all_gather_small(blk, name):
    rows, cols = blk.shape

    def body(x_ref, out_ref, done, send_sems, recv_sems, local_sem):
        x, y, c = _coords()
        me = 4 * x + 2 * y + c
        mine = pltpu.make_async_copy(x_ref, out_ref.at[me], local_sem)
        mine.start()

        def copy(k, slot):
            peer = (_flip(x, k & 4), _flip(y, k & 2), _flip(c, k & 1))
            return pltpu.make_async_remote_copy(
                src_ref=x_ref, dst_ref=out_ref.at[slot], send_sem=send_sems.at[k - 1],
                recv_sem=recv_sems.at[k - 1], device_id=peer, device_id_type=MESH)

        for k in range(1, N_DEV):
            copy(k, me).start()
        for k in range(1, N_DEV):
            copy(k, jnp.bitwise_xor(me, k)).wait_recv()
        for k in range(1, N_DEV):
            copy(k, me).wait_send()
        mine.wait()
        done[...] = jnp.zeros((8, LANES), F32)

    vmem = pl.BlockSpec(memory_space=pltpu.VMEM)
    return pl.pallas_call(
        body, name=name,
        out_shape=[jax.ShapeDtypeStruct((N_DEV, rows, cols), blk.dtype), jax.ShapeDtypeStruct((8, LANES), F32)],
        in_specs=[vmem], out_specs=[vmem, vmem],
        scratch_shapes=[pltpu.SemaphoreType.DMA((N_DEV - 1,)), pltpu.SemaphoreType.DMA((N_DEV - 1,)),
                        pltpu.SemaphoreType.DMA],
    )(blk)


class _Absent:
    at = property(lambda self: self)

    def __getitem__(self, index):
        return self


class _Exchange:
    def __init__(self, name, srcs, land_shapes, n_copies, plan, after=None):
        self.name, self.plan, self.n_copies = name, plan, n_copies
        self.n_src, self.n_land = len(srcs), len(land_shapes)
        hbm = lambda a: pltpu.HBM(a.shape, a.dtype)
        n = self.n_src + self.n_land
        lands = [pltpu.with_memory_space_constraint(lax.empty(s.shape, s.dtype), pltpu.HBM) for s in land_shapes]
        srcs = [pltpu.with_memory_space_constraint(s, pltpu.HBM) for s in srcs]
        extra = [] if after is None else [after]

        def body(*refs):
            send_sems, recv_sems = refs[n + len(extra)], refs[n + len(extra) + 1]
            for cp in self._copies(refs, send_sems, recv_sems):
                cp.start()
            refs[-1][...] = jnp.zeros((8, LANES), F32)

        outs = pl.pallas_call(
            body, name=name + "_start",
            out_shape=(pltpu.SemaphoreType.DMA((n_copies,)), pltpu.SemaphoreType.DMA((n_copies,)),
                       *[hbm(a) for a in srcs], *[hbm(a) for a in lands], jax.ShapeDtypeStruct((8, LANES), F32)),
            in_specs=[_HBM_SPEC] * n + [pl.BlockSpec(memory_space=pl.ANY)] * len(extra),
            out_specs=(_SEM_SPEC, _SEM_SPEC, *[_HBM_SPEC] * n, pl.BlockSpec(memory_space=pltpu.VMEM)),
            input_output_aliases={i: 2 + i for i in range(n)},
            compiler_params=pltpu.CompilerParams(has_side_effects=_EFFECT),
        )(*srcs, *lands, *extra)
        self.send_sems, self.recv_sems = outs[0], outs[1]
        self.buffers = list(outs[2:2 + n])
        self.token = outs[-1]
        self.waited = 0

    def _copies(self, refs, send_sems, recv_sems, lo=0, hi=None):
        src_refs, land_refs = refs[:self.n_src], refs[self.n_src:self.n_src + self.n_land]
        planned = list(enumerate(self.plan(src_refs, land_refs)))[lo:hi]
        return [pltpu.make_async_remote_copy(src_ref=s, dst_ref=d, send_sem=send_sems.at[k], recv_sem=recv_sems.at[k],
                                             device_id=dev, device_id_type=MESH) for k, (s, d, dev) in planned]

    def wait(self, after, count=None, through=None):
        n = self.n_src + self.n_land
        lo = self.waited
        hi = self.n_copies if count is None else lo + count
        self.waited = hi
        through = list(range(n)) if through is None else list(through)
        m = len(through)

        def body(*refs):
            held = [_Absent()] * n
            for pos, ref in zip(through, refs[:m]):
                held[pos] = ref
            for cp in self._copies(held, refs[m], refs[m + 1], lo, hi):
                cp.wait_send()
                cp.wait_recv()

        outs = pl.pallas_call(
            body, name=f"{self.name}_wait{lo}" if lo or hi < self.n_copies else self.name + "_wait",
            out_shape=tuple(pltpu.HBM(self.buffers[p].shape, self.buffers[p].dtype) for p in through),
            in_specs=[_HBM_SPEC] * m + [_SEM_SPEC, _SEM_SPEC, pl.BlockSpec(memory_space=pl.ANY)],
            out_specs=tuple([_HBM_SPEC] * m),
            input_output_aliases={i: i for i in range(m)},
            compiler_params=pltpu.CompilerParams(has_side_effects=_EFFECT),
        )(*[self.buffers[p] for p in through], self.send_sems, self.recv_sems, after)
        for pos, out in zip(through, outs):
            self.buffers[pos] = out
        return list(self.buffers[:self.n_src]), list(self.buffers[self.n_src:])


def _pair_plan(src_index):
    def plan(srcs, lands):
        x, y, c = _coords()
        chip = 2 * x + y
        return [(srcs[0].at[src_index(j, chip, c)], lands[0].at[j], (x, y, 1 - c)) for j in range(4)]
    return plan


def _chips_plan(srcs, lands):
    x, y, c = _coords()
    return [(srcs[0].at[j - 1], lands[0].at[j - 1], (_flip(x, j & 2), _flip(y, j & 1), c)) for j in range(1, 4)]


def _peer(k):
    x, y, c = _coords()
    return _flip(x, k & 4), _flip(y, k & 2), _flip(c, k & 1)


def _send_plan(ks):
    def plan(srcs, lands):
        return [(srcs[0].at[0], lands[0].at[p], _peer(k)) for p, k in enumerate(ks)]
    return plan


def _forward_plan(moves):
    def plan(srcs, lands):
        return [(srcs[0].at[s], srcs[0].at[d], _peer(1)) for s, d in moves]
    return plan


def _gather_plan(srcs, lands):
    x, y, c = _coords()
    me = 4 * x + 2 * y + c
    return [(srcs[a], lands[a].at[me], (_flip(x, k & 4), _flip(y, k & 2), _flip(c, k & 1)))
            for a in range(len(srcs)) for k in range(1, N_DEV)]


def _pair_sum(grad, recv, owners, name):
    _, rows, cols = grad.shape
    tr = _row_tile(rows)

    def body(own_ref, g_ref, r_ref, mine_ref, send_ref):
        j = pl.program_id(1)
        total = g_ref[...] + r_ref[...]

        @pl.when(j == 0)
        def _():
            mine_ref[...] = total

        @pl.when(j > 0)
        def _():
            send_ref[...] = total.astype(BF16)

    grid_spec = pltpu.PrefetchScalarGridSpec(
        num_scalar_prefetch=1, grid=(rows // tr, 4),
        in_specs=[pl.BlockSpec((1, tr, cols), lambda i, j, own: (own[j], i, 0)),
                  pl.BlockSpec((1, tr, cols), lambda i, j, own: (j, i, 0))],
        out_specs=[pl.BlockSpec((1, tr, cols), lambda i, j, own: (0, i, 0)),
                   pl.BlockSpec((1, tr, cols), lambda i, j, own: (jnp.maximum(j - 1, 0), i, 0))])
    return pl.pallas_call(
        body, name=name, grid_spec=grid_spec,
        out_shape=[jax.ShapeDtypeStruct((1, rows, cols), F32), jax.ShapeDtypeStruct((3, rows, cols), BF16)],
        compiler_params=_cparams(dimension_semantics=("arbitrary", "arbitrary")),
    )(owners, grad, recv)


def _modulation(c16, ada_w_sh, ada_b_sh, decay_logit):
    def body(c_ref, w_ref, b_ref, dl_ref, mod_ref, act_ref, lg_ref):
        cv = c_ref[...]
        act = cv * _sigmoid(cv)
        act_ref[...] = act
        mod_ref[...] = jnp.dot(act, w_ref[...], preferred_element_type=F32,
                               precision=lax.Precision.HIGHEST) + b_ref[...]
        z = dl_ref[...]
        lg_ref[...] = jnp.minimum(z, 0.0) - jnp.log(1.0 + jnp.exp(-jnp.abs(z)))

    return pl.pallas_call(
        body, name="modulation",
        out_shape=[jax.ShapeDtypeStruct((16, ADA_SH), F32), jax.ShapeDtypeStruct((16, D), F32),
                   jax.ShapeDtypeStruct(decay_logit.shape, F32)],
        compiler_params=_cparams(),
    )(c16, ada_w_sh, ada_b_sh, decay_logit)


def _adam_update(w, g, m, v):
    m2 = ADAM_B1 * m + (1.0 - ADAM_B1) * g
    v2 = ADAM_B2 * v + (1.0 - ADAM_B2) * (g * g)
    m_hat = m2 / (1.0 - ADAM_B1 ** ADAM_STEP)
    v_hat = v2 / (1.0 - ADAM_B2 ** ADAM_STEP)
    delta = -ADAM_LR * (m_hat / (jnp.sqrt(v_hat) + ADAM_EPS) + ADAM_WD * w)
    return delta, m2, v2


def _adam_sharded(w, m, v, own, recv, name):
    rows, cols = w.shape
    tr = _row_tile(rows)

    def body(w_ref, m_ref, v_ref, g0, g1, g2, g3, g_ref, d_ref, m_out, v_out):
        g = ((g0[0] + g1[0].astype(F32)) + g2[0].astype(F32)) + g3[0].astype(F32)
        delta, m2, v2 = _adam_update(w_ref[...], g, m_ref[...], v_ref[...])
        g_ref[...] = g
        d_ref[...] = delta
        m_out[...] = m2
        v_out[...] = v2

    flat = pl.BlockSpec((tr, cols), lambda i: (i, 0))
    part = lambda j: pl.BlockSpec((1, tr, cols), lambda i, j=j: (j, i, 0))
    return pl.pallas_call(
        body, name=name, grid=(rows // tr,),
        in_specs=[flat, flat, flat, part(0), part(0), part(1), part(2)],
        out_specs=[flat] * 4,
        out_shape=[jax.ShapeDtypeStruct((rows, cols), F32)] * 4,
        compiler_params=_cparams(dimension_semantics=("arbitrary",)),
    )(w, m, v, own, recv, recv, recv)


def _ada_backward(act_t, dmod16, dmodc8, ada_w_sh, m, v):
    def body(at_ref, dm_ref, dc_ref, w_ref, m_ref, v_ref, g_ref, d_ref, m_out, v_out, pc_ref):
        g = jnp.dot(at_ref[...], dm_ref[...], preferred_element_type=F32, precision=lax.Precision.HIGHEST)
        w = w_ref[...]
        delta, m2, v2 = _adam_update(w, g, m_ref[...], v_ref[...])
        g_ref[...] = g
        d_ref[...] = delta
        m_out[...] = m2
        v_out[...] = v2
        pc_ref[...] = lax.dot_general(dc_ref[...], w, (((1,), (1,)), ((), ())), preferred_element_type=F32,
                                      precision=lax.Precision.HIGHEST)

    return pl.pallas_call(
        body, name="ada_backward",
        out_shape=[jax.ShapeDtypeStruct((D, ADA_SH), F32)] * 4 + [jax.ShapeDtypeStruct((8, D), F32)],
        compiler_params=_cparams(),
    )(act_t, dmod16, dmodc8, ada_w_sh, m, v)


def _sum_devices(gathered, name):
    _, rows, cols = gathered.shape

    def body(g_ref, o_ref):
        acc = g_ref[0]
        for d in range(1, N_DEV):
            acc = acc + g_ref[d]
        o_ref[...] = acc

    return pl.pallas_call(body, name=name, out_shape=jax.ShapeDtypeStruct((rows, cols), F32),
                          compiler_params=_cparams())(gathered)


SMALL_PARAMS = ("c_ctx", "norm_w", "conv_b", "gn_w", "final_norm_w", "ada_b", "decay_logit", "conv_w")


def _adam_small(tot, dact_ctx, conv_w_grad, params):
    n = len(SMALL_PARAMS)

    def body(tot_ref, dact_ref, cwg_ref, *refs):
        ins, outs = refs[:3 * n], refs[3 * n:]
        nh = 2 * N_HEADS
        raw = {
            "c_ctx": dact_ref[0:1, :],
            "norm_w": tot_ref[1:2, :], "conv_b": tot_ref[2:3, :], "gn_w": tot_ref[3:4, :],
            "final_norm_w": tot_ref[0:1, :],
            "ada_b": jnp.concatenate([tot_ref[7 + r:8 + r, :] + tot_ref[10 + r:11 + r, :] for r in range(3)], axis=1),
            "decay_logit": tot_ref[14:15, 0:nh],
            "conv_w": cwg_ref[...],
        }
        for k, name in enumerate(SMALL_PARAMS):
            w, m, v = ins[3 * k][...], ins[3 * k + 1][...], ins[3 * k + 2][...]
            g = raw[name]
            if name == "c_ctx":
                s = _sigmoid(w)
                g = g * (s * (1.0 + w * (1.0 - s)))
            elif name == "decay_logit":
                g = g * (1.0 - _sigmoid(w))
            delta, m2, v2 = _adam_update(w, g, m, v)
            for ref, val in zip(outs[4 * k:4 * k + 4], (g, delta, m2, v2)):
                ref[...] = val

    flat = [a for trio in params for a in trio]
    out_shape = [jax.ShapeDtypeStruct(trio[0].shape, F32) for trio in params for _ in range(4)]
    outs = pl.pallas_call(body, name="adam_small", out_shape=out_shape, compiler_params=_cparams())(
        tot, dact_ctx, conv_w_grad, *flat)
    return [outs[4 * k:4 * k + 4] for k in range(n)]


def _adam_square(params, own, recv):
    def body(own_ref, recv_ref, *refs):
        ins, outs = refs[:9], refs[9:]
        for k in range(3):
            rows = slice(k * RSH, (k + 1) * RSH)
            g = own_ref[0, rows, :]
            for j in range(3):
                g = g + recv_ref[j, rows, :].astype(F32)
            delta, m2, v2 = _adam_update(ins[3 * k][...], g, ins[3 * k + 1][...], ins[3 * k + 2][...])
            for ref, val in zip(outs[4 * k:4 * k + 4], (g, delta, m2, v2)):
                ref[...] = val

    flat = [a for trio in params for a in trio]
    outs = pl.pallas_call(body, name="adam_square", out_shape=[jax.ShapeDtypeStruct((RSH, D), F32)] * 12,
                          compiler_params=_cparams())(own, recv, *flat)
    return [outs[4 * k:4 * k + 4] for k in range(3)]


def _rope(t, cos, sin):
    lane = lax.broadcasted_iota(jnp.int32, (1, LANES), 1)
    first_half = jnp.bitwise_and(lane, DK // 2) == 0
    partner = jnp.where(first_half, pltpu.roll(t, LANES - DK // 2, 1), pltpu.roll(t, DK // 2, 1))
    return t * cos + partner * sin


class _Shards:
    def __init__(self, arrays, picks):
        self.arrays, self.picks = list(arrays), list(picks)

    def specs(self, index_args):
        out = []
        for a, arr in enumerate(self.arrays):
            steps = [p for p, (ai, _) in enumerate(self.picks) if ai == a]
            first, slot0, last = steps[0], self.picks[steps[0]][1], arr.shape[0] - 1
            out.append(pl.BlockSpec(
                (1, D, WSH), lambda *args, first=first, slot0=slot0, last=last:
                (jnp.clip(index_args(*args) - first + slot0, 0, last), 0, 0)))
        return out

    def __add__(self, other):
        shift = len(self.arrays)
        return _Shards(self.arrays + other.arrays, self.picks + [(a + shift, s) for a, s in other.picks])


def _load_w_in(step, ids_ref, shards, w_refs, w_vmem, sems):
    @pl.when(step == 0)
    def _():
        copies = []
        for p, (a, slot) in enumerate(shards.picks):
            col = pl.multiple_of(ids_ref[p] * WSH, LANES)
            copies.append(pltpu.make_async_copy(w_refs[a].at[slot], w_vmem.at[:, pl.ds(col, WSH)], sems.at[p]))
        for cp in copies:
            cp.start()
        for cp in copies:
            cp.wait()


def _project_columns(xmb, w_ref, shard, cos, sin, p_ref):
    groups = 2 * WSH // LANES
    q0, k0, v0 = CB_QK * D // LANES, (CB_QK * D + N_HEADS * DK) // LANES, CB_V * D // LANES
    mxu_cols = 2 * LANES
    for a in range(0, 2 * WSH, mxu_cols):
        acc = _dot(xmb, w_ref[:, a:a + mxu_cols])
        for col in range(0, mxu_cols, LANES):
            grp = shard * groups + (a + col) // LANES
            rotary = jnp.logical_and(grp >= q0, grp < v0)
            scale = jnp.where(grp >= k0, DK ** -0.5, 1.0)
            piece = acc[:, col:col + LANES]
            turned = _rope(piece * scale, cos, sin)
            p_ref[:, a + col:a + col + LANES] = jnp.where(rotary, turned, piece).astype(BF16)


def _prenorm(x, ctx, modx, modc, norm_w):
    n_x_tiles = x.shape[0] // TM
    lext = x.shape[0] + ctx.shape[0]

    def body(x_ref, ctx_ref, mx_ref, mc_ref, nw_ref, xm_ref, xmt_ref, cos_ref, sin_ref):
        i = pl.program_id(0)
        is_ctx = i >= n_x_tiles
        x = jnp.where(is_ctx, ctx_ref[...], x_ref[...])
        r = lax.rsqrt(jnp.mean(x * x, axis=-1, keepdims=True) + EPS)
        sh = jnp.where(is_ctx, mc_ref[0:1, :], mx_ref[0:1, :])
        sc = jnp.where(is_ctx, mc_ref[1:2, :], mx_ref[1:2, :])
        xm = (x * r * nw_ref[...]) * (1.0 + sc) + sh
        xm_ref[...] = xm.astype(BF16)
        xmt_ref[...] = xm.T.astype(BF16)
        pos = i * TM + lax.broadcasted_iota(jnp.int32, (TM, 1), 0)
        lane = lax.broadcasted_iota(jnp.int32, (1, LANES), 1)
        n_freq = DK // 4
        inv = jnp.exp((lane % n_freq).astype(F32) * (-math.log(ROPE_BASE) / n_freq))
        by_column = (lane % (2 * n_freq)) >= n_freq
        ang = jnp.where(by_column, (pos % GRID_W).astype(F32), (pos // GRID_W).astype(F32)) * inv
        first_half = (lane % DK) < DK // 2
        cos_ref[...] = jnp.where(is_ctx, 1.0, jnp.cos(ang))
        sin_ref[...] = jnp.where(is_ctx, 0.0, jnp.where(first_half, -jnp.sin(ang), jnp.sin(ang)))

    full = lambda a: pl.BlockSpec(a.shape, lambda i: (0,) * a.ndim)
    row = lambda width: pl.BlockSpec((TM, width), lambda i: (i, 0))
    return pl.pallas_call(
        body, name="prenorm", grid=(lext // TM,),
        in_specs=[pl.BlockSpec((TM, D), lambda i: (jnp.minimum(i, n_x_tiles - 1), 0)), full(ctx),
                  full(modx), full(modc), full(norm_w)],
        out_specs=[row(D), pl.BlockSpec((D, TM), lambda i: (0, i)), row(LANES), row(LANES)],
        out_shape=[jax.ShapeDtypeStruct((lext, D), BF16), jax.ShapeDtypeStruct((D, lext), BF16),
                   jax.ShapeDtypeStruct((lext, LANES), F32), jax.ShapeDtypeStruct((lext, LANES), F32)],
        compiler_params=_cparams(dimension_semantics=("arbitrary",)),
    )(x, ctx, modx, modc, norm_w)


def _in_projection(p_ext, xm, cos_t, sin_t, w, pairs, me, name):
    lext = xm.shape[0]
    tr = lext // 4
    meta = []
    for (slot_a, k_a), (slot_b, k_b) in pairs:
        assert k_a ^ k_b == 1
        dev_a = jnp.bitwise_xor(me, k_a)
        a_low = dev_a % 2 == 0
        meta += [jnp.where(a_low, slot_a, slot_b), jnp.where(a_low, slot_b, slot_a), dev_a // 2]
    meta = jnp.stack(meta).astype(jnp.int32)

    def body(meta_ref, xm_ref, cos_ref, sin_ref, w_hbm, *rest):
        p_ref, w_ref, sems = rest[-3:]
        j, i = pl.program_id(0), pl.program_id(1)

        @pl.when(i == 0)
        def _():
            copies = [pltpu.make_async_copy(w_hbm.at[meta_ref[3 * j + h]], w_ref.at[:, h * WSH:(h + 1) * WSH],
                                            sems.at[h]) for h in range(2)]
            for cp in copies:
                cp.start()
            for cp in copies:
                cp.wait()

        _project_columns(xm_ref[...], w_ref, meta_ref[3 * j + 2], cos_ref[...], sin_ref[...], p_ref)

    row = lambda width: pl.BlockSpec((tr, width), lambda j, i, meta_: (i, 0))
    grid_spec = pltpu.PrefetchScalarGridSpec(
        num_scalar_prefetch=1, grid=(len(pairs), lext // tr),
        in_specs=[row(D), row(LANES), row(LANES), pl.BlockSpec(memory_space=pl.ANY)]
        + ([] if p_ext is None else [pl.BlockSpec(memory_space=pl.ANY)]),
        out_specs=pl.BlockSpec((tr, 2 * WSH), lambda j, i, meta_: (i, meta_[3 * j + 2])),
        scratch_shapes=[pltpu.VMEM((D, 2 * WSH), BF16), pltpu.SemaphoreType.DMA((2,))])
    return pl.pallas_call(
        body, name=name, grid_spec=grid_spec,
        out_shape=jax.ShapeDtypeStruct((lext, PW), BF16),
        input_output_aliases={} if p_ext is None else {5: 0},
        compiler_params=_cparams(dimension_semantics=("arbitrary", "arbitrary")),
    )(meta, xm, cos_t, sin_t, w, *([] if p_ext is None else [p_ext]))


def _decay_tables(lgf, lgb, n):
    i = lax.broadcasted_iota(jnp.int32, (n, 1), 0).astype(F32)
    return dict(i=i, k_f=jnp.exp(lgf * (n - 1.0 - i)), k_b=jnp.exp(lgb * i),
                q_f=jnp.exp(lgf * (i + 1.0)), q_b=jnp.exp(lgb * (n - i)))


def _decay_matrix(lgf, lgb, n, transposed=False):
    ii = lax.broadcasted_iota(jnp.int32, (n, n), 0)
    jj = lax.broadcasted_iota(jnp.int32, (n, n), 1)
    diff = (jj - ii if transposed else ii - jj).astype(F32)
    low = jnp.exp(lgf * jnp.maximum(diff, 0.0))
    up = jnp.exp(lgb * jnp.maximum(-diff, 0.0))
    return jnp.where(diff > 0, low, jnp.where(diff < 0, up, 2.0)), diff


def _cat_lanes(a, b):
    return jnp.concatenate([a.astype(BF16), b.astype(BF16)], axis=1)


def _retention_forward(p_ext, lg, seq, ctx_len):
    lext = seq + ctx_len
    n_chunks = seq // RET_C
    C = RET_C

    def body(lg_ref, q_ref, k_ref, v_ref, rn_ref, rstd_ref, kv_scr, sf_scr, sb_scr):
        pair = pl.program_id(0)
        lane = lax.broadcasted_iota(jnp.int32, (1, LANES), 1)
        heads = range(2)
        hmask = [(lane // DK == hh).astype(F32) for hh in heads]
        lgf = [lg_ref[0, 2 * pair + hh] for hh in heads]
        lgb = [lg_ref[1, 2 * pair + hh] for hh in heads]
        vs = [slice(hh * DV, (hh + 1) * DV) for hh in heads]
        t = [_decay_tables(lgf[hh], lgb[hh], C) for hh in heads]
        kc_all = k_ref[seq:lext, :].astype(F32)
        s0 = []
        for hh in heads:
            tc = _decay_tables(lgf[hh], lgb[hh], ctx_len)
            kc = kc_all * hmask[hh]
            s0.append(_dot_tn(_cat_lanes(kc * tc["k_f"], kc * tc["k_b"]), v_ref[seq:lext, vs[hh]]))

        def increments(c, carry):
            rows = pl.ds(pl.multiple_of(c * C, C), C)
            k_all = k_ref[rows, :].astype(F32)
            for hh in heads:
                k = k_all * hmask[hh]
                kv_scr[hh, c] = _dot_tn(_cat_lanes(k * t[hh]["k_f"], k * t[hh]["k_b"]), v_ref[rows, vs[hh]])
            return carry

        lax.fori_loop(0, n_chunks, increments, 0, unroll=2)
        gf_c = [jnp.exp(lgf[hh] * C) for hh in heads]
        gb_c = [jnp.exp(lgb[hh] * C) for hh in heads]

        def scan_f(c, s):
            for hh in heads:
                sf_scr[hh, c] = s[hh]
            return tuple(gf_c[hh] * s[hh] + kv_scr[hh, c, 0:LANES, :] for hh in heads)

        def scan_b(n, s):
            c = n_chunks - 1 - n
            for hh in heads:
                sb_scr[hh, c] = s[hh]
            return tuple(gb_c[hh] * s[hh] + kv_scr[hh, c, LANES:2 * LANES, :] for hh in heads)

        lax.fori_loop(0, n_chunks, scan_f, tuple(s0[hh][0:LANES] for hh in heads))
        lax.fori_loop(0, n_chunks, scan_b, tuple(s0[hh][LANES:2 * LANES] for hh in heads))
        dmat = [_decay_matrix(lgf[hh], lgb[hh], C)[0] for hh in heads]

        def outputs(c, carry):
            rows = pl.ds(pl.multiple_of(c * C, C), C)
            q_all = q_ref[rows, :].astype(F32)
            k = k_ref[rows, :]
            for hh in heads:
                q = q_all * hmask[hh]
                v = v_ref[rows, vs[hh]]
                s = _dot_nt(q.astype(BF16), k)
                inner = _dot((s * dmat[hh]).astype(BF16), v)
                states = jnp.concatenate([sf_scr[hh, c], sb_scr[hh, c]], axis=0).astype(BF16)
                ret = inner + _dot(_cat_lanes(q * t[hh]["q_f"], q * t[hh]["q_b"]), states)
                mu = jnp.mean(ret, axis=-1, keepdims=True)
                cen = ret - mu
                rstd = lax.rsqrt(jnp.mean(cen * cen, axis=-1, keepdims=True) + EPS)
                rn_ref[rows, vs[hh]] = cen * rstd
                rstd_ref[rows, vs[hh]] = jnp.broadcast_to(rstd, (C, DV))
            return carry

        lax.fori_loop(0, n_chunks, outputs, 0, unroll=2)

    qk0 = CB_QK * D // LANES
    return pl.pallas_call(
        body, name="retention_forward", grid=(N_HEADS // 2,),
        in_specs=[pl.BlockSpec(memory_space=pltpu.SMEM),
                  pl.BlockSpec((lext, LANES), lambda g: (0, qk0 + g)),
                  pl.BlockSpec((lext, LANES), lambda g: (0, qk0 + N_HEADS // 2 + g)),
                  pl.BlockSpec((lext, 2 * DV), lambda g: (0, CB_V * D // (2 * DV) + g))],
        out_specs=[pl.BlockSpec((seq, 2 * DV), lambda g: (0, g))] * 2,
        out_shape=[jax.ShapeDtypeStruct((seq, D), F32)] * 2,
        scratch_shapes=[pltpu.VMEM((2, n_chunks, 2 * LANES, DV), F32), pltpu.VMEM((2, n_chunks, LANES, DV), F32),
                        pltpu.VMEM((2, n_chunks, LANES, DV), F32)],
        compiler_params=_cparams(dimension_semantics=("arbitrary",)),
    )(lg, p_ext, p_ext, p_ext)


def _retention_backward(p_ext, dret, lg, cos_t, sin_t, seq, ctx_len):
    lext = seq + ctx_len
    n_chunks = seq // RET_C
    C = RET_C

    def body(lg_ref, q_ref, k_ref, v_ref, do_ref, cos_ref, sin_ref, dq_ref, dk_ref, dv_ref, dlg_ref,
             kv_scr, g_scr, sf_scr, sb_scr, gfn_scr, gbp_scr):
        pair = pl.program_id(0)
        lane = lax.broadcasted_iota(jnp.int32, (1, LANES), 1)
        heads = range(2)
        hmask = [(lane // DK == hh).astype(F32) for hh in heads]
        lgf = [lg_ref[0, 2 * pair + hh] for hh in heads]
        lgb = [lg_ref[1, 2 * pair + hh] for hh in heads]
        vs = [slice(hh * DV, (hh + 1) * DV) for hh in heads]
        t = [_decay_tables(lgf[hh], lgb[hh], C) for hh in heads]
        tc = [_decay_tables(lgf[hh], lgb[hh], ctx_len) for hh in heads]
        kc_all = k_ref[seq:lext, :].astype(F32)
        kc = [kc_all * hmask[hh] for hh in heads]
        vc = [v_ref[seq:lext, vs[hh]] for hh in heads]
        kc_cat = [_cat_lanes(kc[hh] * tc[hh]["k_f"], kc[hh] * tc[hh]["k_b"]) for hh in heads]
        s0 = [_dot_tn(kc_cat[hh], vc[hh]) for hh in heads]

        def increments(c, carry):
            rows = pl.ds(pl.multiple_of(c * C, C), C)
            k_all = k_ref[rows, :].astype(F32)
            q_all = q_ref[rows, :].astype(F32)
            for hh in heads:
                k, q = k_all * hmask[hh], q_all * hmask[hh]
                kv_scr[hh, c] = _dot_tn(_cat_lanes(k * t[hh]["k_f"], k * t[hh]["k_b"]), v_ref[rows, vs[hh]])
                g_scr[hh, c] = _dot_tn(_cat_lanes(q * t[hh]["q_f"], q * t[hh]["q_b"]), do_ref[rows, vs[hh]])
            return carry

        lax.fori_loop(0, n_chunks, increments, 0, unroll=2)
        gf_c = [jnp.exp(lgf[hh] * C) for hh in heads]
        gb_c = [jnp.exp(lgb[hh] * C) for hh in heads]

        def scan_f(c, s):
            for hh in heads:
                sf_scr[hh, c] = s[hh]
            return tuple(gf_c[hh] * s[hh] + kv_scr[hh, c, 0:LANES, :] for hh in heads)

        def scan_b(n, s):
            c = n_chunks - 1 - n
            for hh in heads:
                sb_scr[hh, c] = s[hh]
            return tuple(gb_c[hh] * s[hh] + kv_scr[hh, c, LANES:2 * LANES, :] for hh in heads)

        def scan_gf(n, carry):
            c = n_chunks - 1 - n
            for hh in heads:
                gfn_scr[hh, c] = carry[hh]
            return tuple(g_scr[hh, c, 0:LANES, :] + gf_c[hh] * carry[hh] for hh in heads)

        def scan_gb(c, carry):
            for hh in heads:
                gbp_scr[hh, c] = carry[hh]
            return tuple(g_scr[hh, c, LANES:2 * LANES, :] + gb_c[hh] * carry[hh] for hh in heads)

        lax.fori_loop(0, n_chunks, scan_f, tuple(s0[hh][0:LANES] for hh in heads))
        lax.fori_loop(0, n_chunks, scan_b, tuple(s0[hh][LANES:2 * LANES] for hh in heads))
        zero_state = jnp.zeros((LANES, DV), F32)
        gf_first = lax.fori_loop(0, n_chunks, scan_gf, (zero_state, zero_state))
        gb_last = lax.fori_loop(0, n_chunks, scan_gb, (zero_state, zero_state))

        dmat, w_f, w_b = [], [], []
        for hh in heads:
            dm, diff = _decay_matrix(lgf[hh], lgb[hh], C)
            dmat.append(dm)
            w_f.append(jnp.where(diff > 0, diff * dm, 0.0))
            w_b.append(jnp.where(diff < 0, -diff * dm, 0.0))

        def total(a):
            rows_, width = a.shape
            part = jnp.sum(a.reshape(rows_ // 8, 8, width), axis=0)
            return part[:, 0:LANES] + part[:, LANES:2 * LANES] if width == 2 * LANES else part

        def chunk_head(hh, c, rows, q_all, k_all, dlgf, dlgb):
            th = t[hh]
            qm = q_all * hmask[hh]
            km = k_all * hmask[hh]
            qb, kb = qm.astype(BF16), km.astype(BF16)
            v = v_ref[rows, vs[hh]]
            do = do_ref[rows, vs[hh]]
            s = _dot_nt(qb, kb)
            dsd = _dot_nt(do, v)
            ds = (dsd * dmat[hh]).astype(BF16)
            dq_in = _dot(ds, kb)
            dk_in = _dot_tn(ds, qb)
            dv_in = _dot_tn((s * dmat[hh]).astype(BF16), do)
            prod = s * dsd
            dlgf = dlgf + total(prod * w_f[hh])
            dlgb = dlgb + total(prod * w_b[hh])
            sf, sb = sf_scr[hh, c], sb_scr[hh, c]
            states = jnp.concatenate([sf, sb], axis=0).astype(BF16)
            dqc = _dot_nt(do, states)
            dqf = dqc[:, 0:LANES] * th["q_f"]
            dqb = dqc[:, LANES:2 * LANES] * th["q_b"]
            dq = (dq_in + dqf + dqb) * hmask[hh]
            dlgf = dlgf + total((th["i"] + 1.0) * (qm * dqf))
            dlgb = dlgb + total((C - th["i"]) * (qm * dqb))
            gfn, gbp = gfn_scr[hh, c], gbp_scr[hh, c]
            gstates = jnp.concatenate([gfn, gbp], axis=0).astype(BF16)
            dkc = _dot_nt(v, gstates)
            dkf = dkc[:, 0:LANES] * th["k_f"]
            dkb = dkc[:, LANES:2 * LANES] * th["k_b"]
            dk = dk_in + dkf + dkb
            dlgf = dlgf + total((C - 1.0 - th["i"]) * (km * dkf)) + C * gf_c[hh] * total(gfn * sf)
            dlgb = dlgb + total(th["i"] * (km * dkb)) + C * gb_c[hh] * total(gbp * sb)
            dv = dv_in + _dot(_cat_lanes(km * th["k_f"], km * th["k_b"]), gstates)
            dv_ref[rows, vs[hh]] = dv.astype(BF16)
            return dq, dk, dlgf, dlgb

        def chunk(c, carry):
            rows = pl.ds(pl.multiple_of(c * C, C), C)
            q_all = q_ref[rows, :].astype(F32)
            k_all = k_ref[rows, :].astype(F32)
            dq0, dk0, f0, b0 = chunk_head(0, c, rows, q_all, k_all, carry[0], carry[1])
            dq1, dk1, f1, b1 = chunk_head(1, c, rows, q_all, k_all, carry[2], carry[3])
            cos, sin = cos_ref[rows, :], sin_ref[rows, :]
            dq_ref[rows, :] = _rope(dq0 + dq1, cos, -sin).astype(BF16)
            dk_ref[rows, :] = (_rope(dk0 + dk1, cos, -sin) * (DK ** -0.5)).astype(BF16)
            return f0, b0, f1, b1

        zero = jnp.zeros((8, LANES), F32)
        sums = lax.fori_loop(0, n_chunks, chunk, (zero, zero, zero, zero), unroll=2)

        dlg = []
        dk_ctx = jnp.zeros((ctx_len, LANES), F32)
        for hh in heads:
            g0 = jnp.concatenate([gf_first[hh], gb_last[hh]], axis=0).astype(BF16)
            dkcc = _dot_nt(vc[hh], g0)
            dkcf = dkcc[:, 0:LANES] * tc[hh]["k_f"]
            dkcb = dkcc[:, LANES:2 * LANES] * tc[hh]["k_b"]
            dlgf = sums[2 * hh] + total((ctx_len - 1.0 - tc[hh]["i"]) * (kc[hh] * dkcf))
            dlgb = sums[2 * hh + 1] + total(tc[hh]["i"] * (kc[hh] * dkcb))
            dk_ctx = dk_ctx + (dkcf + dkcb) * (DK ** -0.5)
            dv_ref[seq:lext, vs[hh]] = _dot(kc_cat[hh], g0).astype(BF16)
            dlg += [jnp.sum(jnp.sum(a, axis=1, keepdims=True), axis=0, keepdims=True) for a in (dlgf, dlgb)]
        dk_ref[seq:lext, :] = dk_ctx.astype(BF16)
        dq_ref[seq:lext, :] = jnp.zeros((ctx_len, LANES), BF16)

        lane8 = lax.broadcasted_iota(jnp.int32, (8, LANES), 1)
        out = jnp.zeros((8, LANES), F32)
        for n, val in enumerate(dlg):
            out = jnp.where(lane8 == n, val, out)
        dlg_ref[0] = out

    qk0 = CB_QK * D // LANES
    q_spec = pl.BlockSpec((lext, LANES), lambda g: (0, qk0 + g))
    k_spec = pl.BlockSpec((lext, LANES), lambda g: (0, qk0 + N_HEADS // 2 + g))
    v_spec = pl.BlockSpec((lext, 2 * DV), lambda g: (0, CB_V * D // (2 * DV) + g))
    table = pl.BlockSpec((lext, LANES), lambda g: (0, 0))
    state = pltpu.VMEM((2, n_chunks, LANES, DV), F32)
    return pl.pallas_call(
        body, name="retention_backward", grid=(N_HEADS // 2,),
        in_specs=[pl.BlockSpec(memory_space=pltpu.SMEM), q_spec, k_spec, v_spec,
                  pl.BlockSpec((seq, 2 * DV), lambda g: (0, g)), table, table],
        out_specs=[pl.BlockSpec((lext, LANES), lambda g: (0, g)), pl.BlockSpec((lext, LANES), lambda g: (0, g)),
                   pl.BlockSpec((lext, 2 * DV), lambda g: (0, g)), pl.BlockSpec((1, 8, LANES), lambda g: (g, 0, 0))],
        out_shape=[jax.ShapeDtypeStruct((lext, N_HEADS * DK), BF16), jax.ShapeDtypeStruct((lext, N_HEADS * DK), BF16),
                   jax.ShapeDtypeStruct((lext, D), BF16), jax.ShapeDtypeStruct((N_HEADS // 2, 8, LANES), F32)],
        scratch_shapes=[pltpu.VMEM((2, n_chunks, 2 * LANES, DV), F32), pltpu.VMEM((2, n_chunks, 2 * LANES, DV), F32),
                        state, state, state, state],
        compiler_params=_cparams(dimension_semantics=("arbitrary",)),
    )(lg, p_ext, p_ext, p_ext, dret, cos_t, sin_t)


class _ColumnWriter:
    def __init__(self, dst_hbm, stage, sems, pieces, n_steps):
        self.dst, self.stage, self.sems, self.pieces, self.n_steps = dst_hbm, stage, sems, pieces, n_steps

    def _copies(self, slot, tile):
        rows = pl.ds(pl.multiple_of(tile * TM, TM), TM)
        return [pltpu.make_async_copy(self.stage.at[slot, :, pl.ds(src, width)], self.dst.at[rows, pl.ds(dst, width)],
                                      self.sems.at[slot, k]) for k, (src, dst, width) in enumerate(self.pieces)]

    def slot_for(self, step):
        slot = step % 2

        @pl.when(step >= 2)
        def _():
            for cp in self._copies(slot, step - 2):
                cp.wait()

        return slot

    def send(self, step, slot):
        for cp in self._copies(slot, step):
            cp.start()

        @pl.when(step == self.n_steps - 1)
        def _():
            if self.n_steps >= 2:
                for cp in self._copies(1 - slot, step - 1):
                    cp.wait()
            for cp in self._copies(slot, step):
                cp.wait()


def _merge(p_ext, rn, rstd, x, target, w_a, w_b, w_out, vecs, seq):
    n_tiles = seq // TM
    hb = TM // HALO
    dp_pieces = [(0, CB_BG * D, D), (D, CB_ZA * D, D), (2 * D, CB_ZB * D, 3 * D)]

    def body(h_ref, bg_ref, cg_ref, za_ref, zb_ref, ga_ref, gb_ref, hp_ref, hn_ref, cp_ref, cn_ref,
             rn_ref, rstd_ref, x_ref, t_ref, wa_ref, wb_ref, wo_ref, vec_ref,
             dx1_ref, dp_hbm, dconv_ref, dret_ref, at_ref, b_ref, part_ref, stage, dp_sems):
        i = pl.program_id(0)
        writer = _ColumnWriter(dp_hbm, stage, dp_sems, dp_pieces, n_tiles)
        slot = writer.slot_for(i)
        dpa_ref = stage.at[slot]
        f = lambda ref: ref[...].astype(F32)
        h, bg, cg, za, zb, ga, gb = f(h_ref), f(bg_ref), f(cg_ref), f(za_ref), f(zb_ref), f(ga_ref), f(gb_ref)
        gx, w0, w1, w2 = vec_ref[0:1, :], vec_ref[1:2, :], vec_ref[2:3, :], vec_ref[3:4, :]
        cb, gnw, fw = vec_ref[4:5, :], vec_ref[5:6, :], vec_ref[6:7, :]
        u = cg * h
        row = lax.broadcasted_iota(jnp.int32, (TM, 1), 0)
        u_prev = (f(cp_ref) * f(hp_ref))[HALO - 1:HALO, :]
        u_next = (f(cn_ref) * f(hn_ref))[0:1, :]
        u_prev = jnp.where(i == 0, 0.0, u_prev)
        u_next = jnp.where(i == n_tiles - 1, 0.0, u_next)
        u_up = jnp.where(row == 0, u_prev, pltpu.roll(u, 1, 0))
        u_dn = jnp.where(row == TM - 1, u_next, pltpu.roll(u, TM - 1, 0))
        conv = w0 * u_up + w1 * u + w2 * u_dn + cb
        sza = _sigmoid(za)
        silu_za = za * sza
        a_act = silu_za * bg * conv
        rn = rn_ref[...]
        szb = _sigmoid(zb)
        silu_zb = zb * szb
        rg = rn * gnw
        b_act = silu_zb * rg
        y_a = _dot(a_act.astype(BF16), wa_ref[...])
        y_b = _dot(b_act.astype(BF16), wb_ref[...])
        sga, sgb = _sigmoid(ga), _sigmoid(gb)
        mix = sga * y_a + sgb * y_b
        y = _dot(mix.astype(BF16), wo_ref[...])
        x1 = x_ref[...] + gx * y
        r1 = lax.rsqrt(jnp.mean(x1 * x1, axis=-1, keepdims=True) + EPS)
        xh1 = x1 * r1
        err = xh1 * fw - t_ref[...]
        loss = jnp.sum(jnp.sum(err * err, axis=1, keepdims=True), axis=0, keepdims=True) * (0.5 / D)
        dout = err * (1.0 / D)
        dxh = dout * fw
        dx1 = r1 * (dxh - xh1 * jnp.mean(dxh * xh1, axis=-1, keepdims=True))
        dx1_ref[...] = dx1
        dy = (dx1 * gx).astype(BF16)
        dmix = _dot_nt(dy, wo_ref[...])
        dya = (dmix * sga).astype(BF16)
        dyb = (dmix * sgb).astype(BF16)
        da = _dot_nt(dya, wa_ref[...])
        db = _dot_nt(dyb, wb_ref[...])
        dpa_ref[:, 0:D] = (da * silu_za * conv).astype(BF16)
        dpa_ref[:, D:2 * D] = (da * bg * conv * (sza * (1.0 + za * (1.0 - sza)))).astype(BF16)
        dpa_ref[:, 2 * D:3 * D] = (db * rg * (szb * (1.0 + zb * (1.0 - szb)))).astype(BF16)
        dpa_ref[:, 3 * D:4 * D] = (dmix * y_a * sga * (1.0 - sga)).astype(BF16)
        dpa_ref[:, 4 * D:5 * D] = (dmix * y_b * sgb * (1.0 - sgb)).astype(BF16)
        dconv_ref[...] = (da * silu_za * bg).astype(BF16)
        drn_n = db * silu_zb
        drn = drn_n * gnw
        rstd = rstd_ref[...]
        for hd in range(N_HEADS):
            sl = slice(hd * DV, (hd + 1) * DV)
            dh_, rh = drn[:, sl], rn[:, sl]
            m1 = jnp.mean(dh_, axis=-1, keepdims=True)
            m2 = jnp.mean(dh_ * rh, axis=-1, keepdims=True)
            dret_ref[:, sl] = (rstd[:, sl] * (dh_ - m1 - rh * m2)).astype(BF16)
        at_ref[0] = a_act.T.astype(BF16)
        at_ref[1] = b_act.T.astype(BF16)
        at_ref[2] = mix.T.astype(BF16)
        b_ref[0] = dya
        b_ref[1] = dyb
        b_ref[2] = dy

        @pl.when(i == 0)
        def _():
            part_ref[...] = jnp.zeros((8, D), F32)

        part_ref[0:1, :] += jnp.sum(dout * xh1, axis=0, keepdims=True)
        part_ref[1:2, :] += jnp.sum(dx1 * y, axis=0, keepdims=True)
        part_ref[2:3, :] += jnp.sum(drn_n * rn, axis=0, keepdims=True)
        part_ref[3:4, :] += jnp.broadcast_to(loss, (1, D))
        writer.send(i, slot)

    col = lambda cb_: pl.BlockSpec((TM, D), lambda i, cb_=cb_: (i, cb_))
    prev = lambda cb_: pl.BlockSpec((HALO, D), lambda i, cb_=cb_: (jnp.maximum(i * hb - 1, 0), cb_))
    nxt = lambda cb_: pl.BlockSpec((HALO, D), lambda i, cb_=cb_: (jnp.minimum((i + 1) * hb, n_tiles * hb - 1), cb_))
    tile = pl.BlockSpec((TM, D), lambda i: (i, 0))
    full = lambda a: pl.BlockSpec(a.shape, lambda i: (0,) * a.ndim, pipeline_mode=pl.Buffered(1))
    return pl.pallas_call(
        body, name="merge", grid=(n_tiles,),
        in_specs=[col(CB_H), col(CB_BG), col(CB_CG), col(CB_ZA), col(CB_ZB), col(CB_GA), col(CB_GB),
                  prev(CB_H), nxt(CB_H), prev(CB_CG), nxt(CB_CG),
                  tile, tile, tile, tile, full(w_a), full(w_b), full(w_out), full(vecs)],
        out_specs=[tile, pl.BlockSpec(memory_space=pl.ANY), tile, tile,
                   pl.BlockSpec((3, D, TM), lambda i: (0, 0, i)), pl.BlockSpec((3, TM, D), lambda i: (0, i, 0)),
                   pl.BlockSpec((8, D), lambda i: (0, 0))],
        out_shape=[jax.ShapeDtypeStruct((seq, D), F32), jax.ShapeDtypeStruct((p_ext.shape[0], PW), BF16),
                   jax.ShapeDtypeStruct((seq, D), BF16), jax.ShapeDtypeStruct((seq, D), BF16),
                   jax.ShapeDtypeStruct((3, D, seq), BF16), jax.ShapeDtypeStruct((3, seq, D), BF16),
                   jax.ShapeDtypeStruct((8, D), F32)],
        scratch_shapes=[pltpu.VMEM((2, TM, 5 * D), BF16), pltpu.SemaphoreType.DMA((2, len(dp_pieces)))],
        compiler_params=_cparams(dimension_semantics=("arbitrary",)),
    )(p_ext, p_ext, p_ext, p_ext, p_ext, p_ext, p_ext, p_ext, p_ext, p_ext, p_ext,
      rn, rstd, x, target, w_a, w_b, w_out, vecs)


def _conv_backward(p_ext, dconv, dp, dq, dk, dv, vecs, seq):
    n_tiles = seq // TM
    hb = TM // HALO
    qk = N_HEADS * DK
    pieces = [(0, CB_H * D, D), (D, CB_CG * D, D), (2 * D, CB_QK * D, 2 * D)]
    zero_pieces = [(CB_BG * D, D), (CB_ZA * D, D), (CB_ZB * D, 3 * D)]

    def body(h_ref, cg_ref, dc_ref, dcp_ref, dcn_ref, dq_ref, dk_ref, dv_ref, vec_ref, dp_in, dp_hbm, part_ref,
             stage, sems, zeros, zero_sems):
        del dp_in
        i = pl.program_id(0)
        writer = _ColumnWriter(dp_hbm, stage, sems, pieces, n_tiles + 1)
        slot = writer.slot_for(i)
        out = stage.at[slot]
        out[:, 2 * D:2 * D + qk] = dq_ref[...]
        out[:, 2 * D + qk:3 * D] = dk_ref[...]
        out[:, 3 * D:4 * D] = dv_ref[...]

        @pl.when(i == 0)
        def _():
            part_ref[...] = jnp.zeros((8, D), F32)

        @pl.when(i == n_tiles)
        def _():
            out[:, 0:2 * D] = jnp.zeros((TM, 2 * D), BF16)
            zeros[...] = jnp.zeros(zeros.shape, BF16)
            rows = pl.ds(n_tiles * TM, TM)
            fills = [pltpu.make_async_copy(zeros.at[:, pl.ds(0, width)], dp_hbm.at[rows, pl.ds(dst, width)],
                                           zero_sems.at[k]) for k, (dst, width) in enumerate(zero_pieces)]
            for cp in fills:
                cp.start()
            for cp in fills:
                cp.wait()

        @pl.when(i < n_tiles)
        def _():
            f = lambda ref: ref[...].astype(F32)
            h, cg, dc = f(h_ref), f(cg_ref), f(dc_ref)
            w0, w1, w2 = vec_ref[1:2, :], vec_ref[2:3, :], vec_ref[3:4, :]
            row = lax.broadcasted_iota(jnp.int32, (TM, 1), 0)
            dc_prev = jnp.where(i == 0, 0.0, f(dcp_ref)[HALO - 1:HALO, :])
            dc_next = jnp.where(i == n_tiles - 1, 0.0, f(dcn_ref)[0:1, :])
            dc_up = jnp.where(row == 0, dc_prev, pltpu.roll(dc, 1, 0))
            dc_dn = jnp.where(row == TM - 1, dc_next, pltpu.roll(dc, TM - 1, 0))
            du = w0 * dc_dn + w1 * dc + w2 * dc_up
            u = cg * h
            part_ref[0:1, :] += jnp.sum(u * dc_dn, axis=0, keepdims=True)
            part_ref[1:2, :] += jnp.sum(u * dc, axis=0, keepdims=True)
            part_ref[2:3, :] += jnp.sum(u * dc_up, axis=0, keepdims=True)
            part_ref[3:4, :] += jnp.sum(dc, axis=0, keepdims=True)
            out[:, 0:D] = (du * cg).astype(BF16)
            out[:, D:2 * D] = (du * h).astype(BF16)

        writer.send(i, slot)

    last = n_tiles - 1
    col = lambda cb_: pl.BlockSpec((TM, D), lambda i, cb_=cb_: (jnp.minimum(i, last), cb_))
    lat = lambda w: pl.BlockSpec((TM, w), lambda i: (jnp.minimum(i, last), 0))
    ext = lambda w: pl.BlockSpec((TM, w), lambda i: (i, 0))
    anyspec = pl.BlockSpec(memory_space=pl.ANY)
    return pl.pallas_call(
        body, name="conv_backward", grid=(n_tiles + 1,),
        in_specs=[col(CB_H), col(CB_CG), lat(D),
                  pl.BlockSpec((HALO, D), lambda i: (jnp.clip(i * hb - 1, 0, n_tiles * hb - 1), 0)),
                  pl.BlockSpec((HALO, D), lambda i: (jnp.minimum((i + 1) * hb, n_tiles * hb - 1), 0)),
                  ext(qk), ext(qk), ext(D), pl.BlockSpec(vecs.shape, lambda i: (0, 0)), anyspec],
        out_specs=[anyspec, pl.BlockSpec((8, D), lambda i: (0, 0))],
        out_shape=[jax.ShapeDtypeStruct(dp.shape, BF16), jax.ShapeDtypeStruct((8, D), F32)],
        input_output_aliases={9: 0},
        scratch_shapes=[pltpu.VMEM((2, TM, 4 * D), BF16), pltpu.SemaphoreType.DMA((2, len(pieces))),
                        pltpu.VMEM((TM, 3 * D), BF16), pltpu.SemaphoreType.DMA((len(zero_pieces),))],
        compiler_params=_cparams(dimension_semantics=("arbitrary",)),
    )(p_ext, p_ext, dconv, dconv, dconv, dq, dk, dv, vecs, dp)


def _input_backward(dp, shards, ids, x, ctx, dx1, modx, modc, norm_w):
    seq = x.shape[0]
    lext = seq + ctx.shape[0]
    n_x = seq // TM
    n_w = len(shards.arrays)

    def body(ids_ref, dp_ref, x_ref, ctx_ref, dx1_ref, mx_ref, mc_ref, nw_ref, *rest):
        w_hbm, (gx_ref, part_ref, w_ref, w_sems) = rest[:n_w], rest[n_w:]
        i = pl.program_id(0)
        _load_w_in(i, ids_ref, shards, w_hbm, w_ref, w_sems)
        is_ctx = i >= n_x
        dxm = _dot_nt(dp_ref[...], w_ref[...])
        x = jnp.where(is_ctx, ctx_ref[...], x_ref[...])
        r = lax.rsqrt(jnp.mean(x * x, axis=-1, keepdims=True) + EPS)
        xh = x * r
        nw = nw_ref[...]
        sc = jnp.where(is_ctx, mc_ref[1:2, :], mx_ref[1:2, :])
        dxn = dxm * (1.0 + sc)
        dxh = dxn * nw
        dx = r * (dxh - xh * jnp.mean(dxh * xh, axis=-1, keepdims=True))

        @pl.when(jnp.logical_not(is_ctx))
        def _():
            gx_ref[...] = dx1_ref[...] + dx

        @pl.when(i == 0)
        def _():
            part_ref[...] = jnp.zeros((8, D), F32)

        fx = jnp.where(is_ctx, 0.0, 1.0)
        d_shift = jnp.sum(dxm, axis=0, keepdims=True)
        d_scale = jnp.sum(dxm * (xh * nw), axis=0, keepdims=True)
        part_ref[0:1, :] += fx * d_shift
        part_ref[1:2, :] += fx * d_scale
        part_ref[2:3, :] += jnp.sum(dxn * xh, axis=0, keepdims=True)
        part_ref[3:4, :] += (1.0 - fx) * d_shift
        part_ref[4:5, :] += (1.0 - fx) * d_scale

    lat = lambda w: pl.BlockSpec((TM, w), lambda i, ids_: (jnp.minimum(i, n_x - 1), 0))
    ext = lambda w: pl.BlockSpec((TM, w), lambda i, ids_: (i, 0))
    full = lambda a: pl.BlockSpec(a.shape, lambda i, ids_: (0,) * a.ndim)
    grid_spec = pltpu.PrefetchScalarGridSpec(
        num_scalar_prefetch=1, grid=(lext // TM,),
        in_specs=[ext(PW), lat(D), full(ctx), lat(D), full(modx), full(modc), full(norm_w)]
        + [pl.BlockSpec(memory_space=pl.ANY)] * n_w,
        out_specs=[lat(D), pl.BlockSpec((8, D), lambda i, ids_: (0, 0))],
        scratch_shapes=[pltpu.VMEM((D, PW), BF16), pltpu.SemaphoreType.DMA((N_DEV,))])
    return pl.pallas_call(
        body, name="input_backward", grid_spec=grid_spec,
        out_shape=[jax.ShapeDtypeStruct((seq, D), F32), jax.ShapeDtypeStruct((8, D), F32)],
        compiler_params=_cparams(dimension_semantics=("arbitrary",)),
    )(ids, dp, x, ctx, dx1, modx, modc, norm_w, *shards.arrays)


def _weight_grad_in(xm_t, dp, owners, name):
    lext = xm_t.shape[1]

    def body(own_ref, a_ref, b_ref, o_ref):
        o_ref[0] = _dot(a_ref[...], b_ref[...])

    grid_spec = pltpu.PrefetchScalarGridSpec(
        num_scalar_prefetch=1, grid=(4,),
        in_specs=[pl.BlockSpec((D, lext), lambda j, own: (0, 0)),
                  pl.BlockSpec((lext, WSH), lambda j, own: (0, own[j]))],
        out_specs=pl.BlockSpec((1, D, WSH), lambda j, own: (j, 0, 0)))
    return pl.pallas_call(
        body, name=name, grid_spec=grid_spec,
        out_shape=jax.ShapeDtypeStruct((4, D, WSH), F32),
        compiler_params=_cparams(dimension_semantics=("arbitrary",)),
    )(owners, xm_t, dp)


def _weight_grad_square(a_t, b):
    seq = a_t.shape[2]

    def body(a_ref, b_ref, o_ref):
        o_ref[:, 0] = _dot(a_ref[0], b_ref[0]).reshape(N_DEV, RSH, D)

    return pl.pallas_call(
        body, name="weight_grad_square", grid=(3,),
        in_specs=[pl.BlockSpec((1, D, seq), lambda t: (t, 0, 0)), pl.BlockSpec((1, seq, D), lambda t: (t, 0, 0))],
        out_specs=pl.BlockSpec((N_DEV, 1, RSH, D), lambda t: (0, t, 0, 0)),
        out_shape=jax.ShapeDtypeStruct((N_DEV, 3, RSH, D), F32),
        compiler_params=_cparams(dimension_semantics=("arbitrary",)),
    )(a_t, b)


def _tie(value, token):
    return value + token[0, 0].astype(value.dtype)


def _pad_lanes(a):
    return jnp.pad(a, ((0, 0), (0, D - a.shape[1])))


def _rows(total, *parts):
    width = max(a.shape[1] for _, a in parts)
    out = None
    for row, a in parts:
        padded = jnp.pad(a, ((row, total - row - a.shape[0]), (0, width - a.shape[1])))
        out = padded if out is None else out + padded
    return out


def kernel(x, c, ctx, c_ctx, norm_w, ada_w, ada_b, w_in, conv_w, conv_b, decay_logit, gn_w, w_a, w_b, w_out, final_norm_w, loss_target, m_c_ctx, m_norm_w, m_ada_w, m_ada_b, m_w_in, m_conv_w, m_conv_b, m_decay_logit, m_gn_w, m_w_a, m_w_b, m_w_out, m_final_norm_w, v_c_ctx, v_norm_w, v_ada_w, v_ada_b, v_w_in, v_conv_w, v_conv_b, v_decay_logit, v_gn_w, v_w_a, v_w_b, v_w_out, v_final_norm_w):
    xi, yi, ci = _coords()
    me = 4 * xi + 2 * yi + ci
    chip = 2 * xi + yi
    seq, ctx_len = x.shape[1], ctx.shape[1]
    assert seq % TM == 0 and seq % RET_C == 0 and ctx_len == TM and seq % GRID_W == 0
    csh = D // N_DEV

    blk = jnp.pad(c, ((0, 7), (0, 0))) + jnp.pad(conv_w[0], ((1, 4), (0, D - csh)))
    got, _ = _all_gather_small(blk, "gather_cond")
    conv_w_all = got[:, 1:4, 0:csh].transpose(1, 0, 2).reshape(3, D)
    c16 = _rows(16, (0, got[:, 0, :]), (N_DEV, c_ctx[None]))
    ada_b_sh = lax.dynamic_slice(ada_b, (0, me * ADA_SH), (1, ADA_SH))
    mod_sh, act16, lg = _modulation(c16, ada_w[0], ada_b_sh, decay_logit[0])
    mod_all, small_done = _all_gather_small(mod_sh, "gather_mod")
    mod_all = mod_all.transpose(1, 0, 2).reshape(16, 3 * D)
    modx = lax.dynamic_slice(mod_all, (me, 0), (1, 3 * D)).reshape(3, D)
    modc = mod_all[8].reshape(3, D)

    x2, tgt = x[0], loss_target[0]
    wblock = lambda n: jax.ShapeDtypeStruct((n, D, WSH), BF16)
    ids_of = lambda ks: jnp.stack([jnp.bitwise_xor(me, k) for k in ks]).astype(jnp.int32)
    own_shard = w_in[0].astype(BF16)[None]
    sq_shards = [w[0].astype(BF16) for w in (w_a, w_b, w_out)]
    wave1 = _Exchange(
        "gather_in_first", [jnp.broadcast_to(own_shard, (2, D, WSH)), own_shard], [wblock(4)], 3,
        lambda srcs, lands: _forward_plan([(0, 1)])(srcs, lands) + _send_plan((4, 2))(srcs[1:], lands),
        after=small_done)
    modx = _tie(modx, wave1.token)
    xm, xm_t, cos_t, sin_t = _prenorm(x2, ctx[0], modx, modc, norm_w)
    (w_pair, _), _ = wave1.wait(xm_t, 1, through=[0])
    first = _Shards([w_pair], [(0, 0), (0, 1)])
    p_ext = _in_projection(None, xm, cos_t, sin_t, w_pair, [((0, 0), (1, 1))], me, "in_projection_pair")
    _, (w_nbr,) = wave1.wait(p_ext, through=[1, 2])
    wave2 = _Exchange(
        "gather_in_second", [w_nbr, own_shard], [wblock(2)], 3,
        lambda srcs, lands: _forward_plan([(0, 2), (1, 3)])(srcs, lands) + _send_plan((6,))(srcs[1:], lands))
    (w_nbr, _), _ = wave2.wait(wave2.token, 2, through=[0])
    second = _Shards([w_nbr], [(0, 0), (0, 1), (0, 2), (0, 3)])
    p_ext = _in_projection(p_ext, xm, cos_t, sin_t, w_nbr, [((0, 4), (2, 5)), ((1, 2), (3, 3))], me,
                           "in_projection_neighbours")
    _, (w_diag,) = wave2.wait(p_ext, through=[1, 2])
    wave3 = _Exchange(
        "gather_in_third", [w_diag] + sq_shards, [jax.ShapeDtypeStruct((N_DEV, RSH, D), BF16)] * 3, 1 + 3 * (N_DEV - 1),
        lambda srcs, lands: _forward_plan([(0, 1)])(srcs, lands) + _gather_plan(srcs[1:], lands))
    (w_diag, *_), _ = wave3.wait(wave3.token, 1, through=[0])
    third = _Shards([w_diag], [(0, 0), (0, 1)])
    p_ext = _in_projection(p_ext, xm, cos_t, sin_t, w_diag, [((0, 6), (1, 7))], me, "in_projection_diagonal")
    w_shards, w_ids = first + second + third, ids_of((0, 1, 4, 2, 5, 3, 6, 7))

    rn, rstd = _retention_forward(p_ext, lg, seq, ctx_len)
    _, sq_lands = wave3.wait(rstd, through=range(1, 7))
    w_a_all, w_b_all, w_out_all = (
        lax.dynamic_update_slice(land, shard[None], (me, 0, 0)).reshape(D, D) for land, shard in zip(sq_lands, sq_shards))
    vecs = _rows(8, (0, modx[2:3]), (1, conv_w_all), (4, conv_b), (5, gn_w), (6, final_norm_w[None]))
    dx1, dp, dconv, dret, op_at, op_b, part_m = _merge(p_ext, rn, rstd, x2, tgt, w_a_all, w_b_all, w_out_all, vecs, seq)

    j4 = jnp.arange(4, dtype=jnp.int32)
    owners = (2 * jnp.bitwise_xor(chip, j4) + ci).astype(jnp.int32)
    owners_sib = (2 * jnp.bitwise_xor(chip, j4) + (1 - ci)).astype(jnp.int32)
    gw_sq = _weight_grad_square(op_at, op_b).reshape(N_DEV, 3 * RSH, D)
    rs_sq_pair = _Exchange("rs_square_pair", [gw_sq], [jax.ShapeDtypeStruct((4, 3 * RSH, D), F32)], 4,
                           _pair_plan(lambda j, chip_, c_: 2 * jnp.bitwise_xor(chip_, j) + (1 - c_)))
    dq, dk, dv, dlg = _retention_backward(p_ext, dret, _tie(lg, rs_sq_pair.token), cos_t, sin_t, seq, ctx_len)
    (gw_sq,), (r1_sq,) = rs_sq_pair.wait(dlg)
    own_sq, send_sq = _pair_sum(gw_sq, r1_sq, owners, "pair_sum_square")
    rs_sq_chips = _Exchange("rs_square_chips", [send_sq], [jax.ShapeDtypeStruct((3, 3 * RSH, D), BF16)], 3, _chips_plan)
    dp, part_c = _conv_backward(p_ext, dconv, dp, dq, dk, dv, _tie(vecs, rs_sq_chips.token), seq)
    gw_sib = _weight_grad_in(xm_t, dp, owners_sib, "weight_grad_in_sibling")
    rs_in_pair = _Exchange("rs_in_pair", [gw_sib], [jax.ShapeDtypeStruct((4, D, WSH), F32)], 4,
                           _pair_plan(lambda j, chip_, c_: j))
    gw_own = _weight_grad_in(xm_t, dp, _tie(owners, rs_in_pair.token), "weight_grad_in_own")
    _, (r1_in,) = rs_in_pair.wait(gw_own)
    own_in, send_in = _pair_sum(gw_own, r1_in, j4, "pair_sum_in")
    rs_in_chips = _Exchange("rs_in_chips", [send_in], [jax.ShapeDtypeStruct((3, D, WSH), BF16)], 3, _chips_plan)
    grad_x, part_i = _input_backward(dp, w_shards, w_ids, x2, ctx[0], dx1, _tie(modx, rs_in_chips.token), modc, norm_w)

    dl = dlg[:, 0, 0:4]
    dlg_row = jnp.pad(dl[:, 0::2].reshape(1, N_HEADS), ((0, 0), (0, D - N_HEADS))) + jnp.pad(
        dl[:, 1::2].reshape(1, N_HEADS), ((0, 0), (N_HEADS, D - 2 * N_HEADS)))
    partials = _rows(16, (0, part_m[0:1]), (1, part_i[2:3]), (2, part_c[3:4]), (3, part_m[2:3]), (4, part_c[0:3]),
                     (7, part_i[0:2]), (9, part_m[1:2]), (10, part_i[3:5]), (13, part_m[3:4]), (14, dlg_row))
    got, _ = _all_gather_small(partials, "gather_partials")
    tot = _sum_devices(got, "sum_partials")
    dmodc = tot[10:13].reshape(1, 3 * D)
    dmod16 = _rows(16, (0, got[:, 7:10, :].reshape(N_DEV, 3 * D)), (N_DEV, dmodc))
    dmod16 = lax.dynamic_slice(dmod16, (0, me * ADA_SH), (16, ADA_SH))
    dmodc8 = _rows(8, (0, dmod16[8:9]))
    g_ada_w, d_ada_w, nm_ada_w, nv_ada_w, cctx_part = _ada_backward(
        act16.T, dmod16, dmodc8, ada_w[0], m_ada_w[0], v_ada_w[0])
    dact_ctx = _sum_devices(_all_gather_small(cctx_part, "gather_cctx")[0], "sum_cctx")

    view = {"c_ctx": (1, D), "final_norm_w": (1, D), "decay_logit": (1, 2 * N_HEADS), "conv_w": (3, csh)}
    given = {"c_ctx": (c_ctx, m_c_ctx, v_c_ctx), "norm_w": (norm_w, m_norm_w, v_norm_w),
             "conv_b": (conv_b, m_conv_b, v_conv_b), "gn_w": (gn_w, m_gn_w, v_gn_w),
             "final_norm_w": (final_norm_w, m_final_norm_w, v_final_norm_w), "ada_b": (ada_b, m_ada_b, v_ada_b),
             "decay_logit": (decay_logit, m_decay_logit, v_decay_logit), "conv_w": (conv_w, m_conv_w, v_conv_w)}
    small_in = [tuple(a.reshape(view.get(name, a.shape)) for a in given[name]) for name in SMALL_PARAMS]
    conv_w_grad = lax.dynamic_slice(tot, (4, me * csh), (3, csh))
    small_out = _adam_small(tot, dact_ctx, conv_w_grad, small_in)
    results = {name: [o.reshape(given[name][0].shape) for o in outs_]
               for name, outs_ in zip(SMALL_PARAMS, small_out)}

    _, (r2_sq,) = rs_sq_chips.wait(small_out[0][0])
    _, (r2_in,) = rs_in_chips.wait(small_out[0][0])
    results["w_in"] = [o[None] for o in _adam_sharded(w_in[0], m_w_in[0], v_w_in[0], own_in, r2_in, "adam_w_in")]
    square = _adam_square([(w_a[0], m_w_a[0], v_w_a[0]), (w_b[0], m_w_b[0], v_w_b[0]),
                           (w_out[0], m_w_out[0], v_w_out[0])], own_sq, r2_sq)
    for name, outs_ in zip(("w_a", "w_b", "w_out"), square):
        results[name] = [o[None] for o in outs_]
    results["ada_w"] = [o[None] for o in (g_ada_w, d_ada_w, nm_ada_w, nv_ada_w)]

    order = ("c_ctx", "norm_w", "ada_w", "ada_b", "w_in", "conv_w", "conv_b", "decay_logit", "gn_w",
             "w_a", "w_b", "w_out", "final_norm_w")
    outs = [results[name][kind] for kind in range(4) for name in order]
    return (tot[13, 0], grad_x[None], *outs)
```

```python
import functools
import math

import jax
import jax.numpy as jnp
from jax import lax
from jax.experimental import pallas as pl
from jax.experimental.pallas import tpu as pltpu

F32 = jnp.float32
BF16 = jnp.bfloat16
MESH = pl.DeviceIdType.MESH
_HBM_SPEC = pl.BlockSpec(memory_space=pltpu.HBM)
_SEM_SPEC = pl.BlockSpec(memory_space=pltpu.SEMAPHORE)
_EFFECT = pltpu.SideEffectType.DATAFLOW_SIDE_EFFECTING

N_DEV = 8
D = 1024
N_HEADS = 8
DK = 64
DV = 128
GRID_W = 64
ROPE_BASE = 10000.0
EPS = 1e-6
PW = 9 * D
WSH = PW // N_DEV
RSH = D // N_DEV
ADA_SH = 3 * D // N_DEV
TM = 256
RET_C = 256
HALO = 16
LANES = 128
VMEM_LIMIT = 60 * 1024 * 1024

ADAM_LR = 0.001
ADAM_B1 = 0.9
ADAM_B2 = 0.999
ADAM_EPS = 1e-08
ADAM_WD = 0.01
ADAM_STEP = 10

CB_H, CB_BG, CB_CG, CB_ZA, CB_QK, CB_V, CB_ZB, CB_GA, CB_GB = range(9)


def _cparams(**kw):
    return pltpu.CompilerParams(vmem_limit_bytes=VMEM_LIMIT, **kw)


def _dot(a, b):
    return jnp.dot(a, b, preferred_element_type=F32)


def _dot_nt(a, b):
    return lax.dot_general(a, b, (((1,), (1,)), ((), ())), preferred_element_type=F32)


def _dot_tn(a, b):
    return lax.dot_general(a, b, (((0,), (0,)), ((), ())), preferred_element_type=F32)


def _sigmoid(z):
    return 0.5 * jnp.tanh(0.5 * z) + 0.5


def _row_tile(rows):
    return TM if rows % TM == 0 else rows


def _coords():
    return lax.axis_index("x"), lax.axis_index("y"), lax.axis_index("c")


def _flip(v, bit):
    return 1 - v if bit else v


def _all_gather_small(blk, name):
    rows, cols = blk.shape

    def body(x_ref, out_ref, done, send_sems, recv_sems, local_sem):
        x, y, c = _coords()
        me = 4 * x + 2 * y + c
        mine = pltpu.make_async_copy(x_ref, out_ref.at[me], local_sem)
        mine.start()

        def copy(k, slot):
            peer = (_flip(x, k & 4), _flip(y, k & 2), _flip(c, k & 1))
            return pltpu.make_async_remote_copy(
                src_ref=x_ref, dst_ref=out_ref.at[slot], send_sem=send_sems.at[k - 1],
                recv_sem=recv_sems.at[k - 1], device_id=peer, device_id_type=MESH)

        for k in range(1, N_DEV):
            copy(k, me).start()
        for k in range(1, N_DEV):
            copy(k, jnp.bitwise_xor(me, k)).wait_recv()
        for k in range(1, N_DEV):
            copy(k, me).wait_send()
        mine.wait()
        done[...] = jnp.zeros((8, LANES), F32)

    vmem = pl.BlockSpec(memory_space=pltpu.VMEM)
    return pl.pallas_call(
        body, name=name,
        out_shape=[jax.ShapeDtypeStruct((N_DEV, rows, cols), blk.dtype), jax.ShapeDtypeStruct((8, LANES), F32)],
        in_specs=[vmem], out_specs=[vmem, vmem],
        scratch_shapes=[pltpu.SemaphoreType.DMA((N_DEV - 1,)), pltpu.SemaphoreType.DMA((N_DEV - 1,)),
                        pltpu.SemaphoreType.DMA],
    )(blk)


class _Absent:
    at = property(lambda self: self)

    def __getitem__(self, index):
        return self


class _Exchange:
    def __init__(self, name, srcs, land_shapes, n_copies, plan, after=None):
        self.name, self.plan, self.n_copies = name, plan, n_copies
        self.n_src, self.n_land = len(srcs), len(land_shapes)
        hbm = lambda a: pltpu.HBM(a.shape, a.dtype)
        n = self.n_src + self.n_land
        lands = [pltpu.with_memory_space_constraint(lax.empty(s.shape, s.dtype), pltpu.HBM) for s in land_shapes]
        srcs = [pltpu.with_memory_space_constraint(s, pltpu.HBM) for s in srcs]
        extra = [] if after is None else [after]

        def body(*refs):
            send_sems, recv_sems = refs[n + len(extra)], refs[n + len(extra) + 1]
            for cp in self._copies(refs, send_sems, recv_sems):
                cp.start()
            refs[-1][...] = jnp.zeros((8, LANES), F32)

        outs = pl.pallas_call(
            body, name=name + "_start",
            out_shape=(pltpu.SemaphoreType.DMA((n_copies,)), pltpu.SemaphoreType.DMA((n_copies,)),
                       *[hbm(a) for a in srcs], *[hbm(a) for a in lands], jax.ShapeDtypeStruct((8, LANES), F32)),
            in_specs=[_HBM_SPEC] * n + [pl.BlockSpec(memory_space=pl.ANY)] * len(extra),
            out_specs=(_SEM_SPEC, _SEM_SPEC, *[_HBM_SPEC] * n, pl.BlockSpec(memory_space=pltpu.VMEM)),
            input_output_aliases={i: 2 + i for i in range(n)},
            compiler_params=pltpu.CompilerParams(has_side_effects=_EFFECT),
        )(*srcs, *lands, *extra)
        self.send_sems, self.recv_sems = outs[0], outs[1]
        self.buffers = list(outs[2:2 + n])
        self.token = outs[-1]
        self.waited = 0

    def _copies(self, refs, send_sems, recv_sems, lo=0, hi=None):
        src_refs, land_refs = refs[:self.n_src], refs[self.n_src:self.n_src + self.n_land]
        planned = list(enumerate(self.plan(src_refs, land_refs)))[lo:hi]
        return [pltpu.make_async_remote_copy(src_ref=s, dst_ref=d, send_sem=send_sems.at[k], recv_sem=recv_sems.at[k],
                                             device_id=dev, device_id_type=MESH) for k, (s, d, dev) in planned]

    def wait(self, after, count=None, through=None):
        n = self.n_src + self.n_land
        lo = self.waited
        hi = self.n_copies if count is None else lo + count
        self.waited = hi
        through = list(range(n)) if through is None else list(through)
        m = len(through)

        def body(*refs):
            held = [_Absent()] * n
            for pos, ref in zip(through, refs[:m]):
                held[pos] = ref
            for cp in self._copies(held, refs[m], refs[m + 1], lo, hi):
                cp.wait_send()
                cp.wait_recv()

        outs = pl.pallas_call(
            body, name=f"{self.name}_wait{lo}" if lo or hi < self.n_copies else self.name + "_wait",
            out_shape=tuple(pltpu.HBM(self.buffers[p].shape, self.buffers[p].dtype) for p in through),
            in_specs=[_HBM_SPEC] * m + [_SEM_SPEC, _SEM_SPEC, pl.BlockSpec(memory_space=pl.ANY)],
            out_specs=tuple([_HBM_SPEC] * m),
            input_output_aliases={i: i for i in range(m)},
            compiler_params=pltpu.CompilerParams(has_side_effects=_EFFECT),
        )(*[self.buffers[p] for p in through], self.send_sems, self.recv_sems, after)
        for pos, out in zip(through, outs):
            self.buffers[pos] = out
        return list(self.buffers[:self.n_src]), list(self.buffers[self.n_src:])


def _pair_plan(src_index):
    def plan(srcs, lands):
        x, y, c = _coords()
        chip = 2 * x + y
        return [(srcs[0].at[src_index(j, chip, c)], lands[0].at[j], (x, y, 1 - c)) for j in range(4)]
    return plan


def _chips_plan(srcs, lands):
    x, y, c = _coords()
    return [(srcs[0].at[j - 1], lands[0].at[j - 1], (_flip(x, j & 2), _flip(y, j & 1), c)) for j in range(1, 4)]


def _peer(k):
    x, y, c = _coords()
    return _flip(x, k & 4), _flip(y, k & 2), _flip(c, k & 1)


def _send_plan(ks):
    def plan(srcs, lands):
        return [(srcs[0].at[0], lands[0].at[p], _peer(k)) for p, k in enumerate(ks)]
    return plan


def _forward_plan(moves):
    def plan(srcs, lands):
        return [(srcs[0].at[s], srcs[0].at[d], _peer(1)) for s, d in moves]
    return plan


def _gather_plan(srcs, lands):
    x, y, c = _coords()
    me = 4 * x + 2 * y + c
    return [(srcs[a], lands[a].at[me], (_flip(x, k & 4), _flip(y, k & 2), _flip(c, k & 1)))
            for a in range(len(srcs)) for k in range(1, N_DEV)]


def _pair_sum(grad, recv, owners, name):
    _, rows, cols = grad.shape
    tr = _row_tile(rows)

    def body(own_ref, g_ref, r_ref, mine_ref, send_ref):
        j = pl.program_id(1)
        total = g_ref[...] + r_ref[...]

        @pl.when(j == 0)
        def _():
            mine_ref[...] = total

        @pl.when(j > 0)
        def _():
            send_ref[...] = total.astype(BF16)

    grid_spec = pltpu.PrefetchScalarGridSpec(
        num_scalar_prefetch=1, grid=(rows // tr, 4),
        in_specs=[pl.BlockSpec((1, tr, cols), lambda i, j, own: (own[j], i, 0)),
                  pl.BlockSpec((1, tr, cols), lambda i, j, own: (j, i, 0))],
        out_specs=[pl.BlockSpec((1, tr, cols), lambda i, j, own: (0, i, 0)),
                   pl.BlockSpec((1, tr, cols), lambda i, j, own: (jnp.maximum(j - 1, 0), i, 0))])
    return pl.pallas_call(
        body, name=name, grid_spec=grid_spec,
        out_shape=[jax.ShapeDtypeStruct((1, rows, cols), F32), jax.ShapeDtypeStruct((3, rows, cols), BF16)],
        compiler_params=_cparams(dimension_semantics=("arbitrary", "arbitrary")),
    )(owners, grad, recv)


def _modulation(c16, ada_w_sh, ada_b_sh, decay_logit):
    def body(c_ref, w_ref, b_ref, dl_ref, mod_ref, act_ref, lg_ref):
        cv = c_ref[...]
        act = cv * _sigmoid(cv)
        act_ref[...] = act
        mod_ref[...] = jnp.dot(act, w_ref[...], preferred_element_type=F32,
                               precision=lax.Precision.HIGHEST) + b_ref[...]
        z = dl_ref[...]
        lg_ref[...] = jnp.minimum(z, 0.0) - jnp.log(1.0 + jnp.exp(-jnp.abs(z)))

    return pl.pallas_call(
        body, name="modulation",
        out_shape=[jax.ShapeDtypeStruct((16, ADA_SH), F32), jax.ShapeDtypeStruct((16, D), F32),
                   jax.ShapeDtypeStruct(decay_logit.shape, F32)],
        compiler_params=_cparams(),
    )(c16, ada_w_sh, ada_b_sh, decay_logit)


def _adam_update(w, g, m, v):
    m2 = ADAM_B1 * m + (1.0 - ADAM_B1) * g
    v2 = ADAM_B2 * v + (1.0 - ADAM_B2) * (g * g)
    m_hat = m2 / (1.0 - ADAM_B1 ** ADAM_STEP)
    v_hat = v2 / (1.0 - ADAM_B2 ** ADAM_STEP)
    delta = -ADAM_LR * (m_hat / (jnp.sqrt(v_hat) + ADAM_EPS) + ADAM_WD * w)
    return delta, m2, v2


def _adam_sharded(w, m, v, own, recv, name):
    rows, cols = w.shape
    tr = _row_tile(rows)

    def body(w_ref, m_ref, v_ref, g0, g1, g2, g3, g_ref, d_ref, m_out, v_out):
        g = ((g0[0] + g1[0].astype(F32)) + g2[0].astype(F32)) + g3[0].astype(F32)
        delta, m2, v2 = _adam_update(w_ref[...], g, m_ref[...], v_ref[...])
        g_ref[...] = g
        d_ref[...] = delta
        m_out[...] = m2
        v_out[...] = v2

    flat = pl.BlockSpec((tr, cols), lambda i: (i, 0))
    part = lambda j: pl.BlockSpec((1, tr, cols), lambda i, j=j: (j, i, 0))
    return pl.pallas_call(
        body, name=name, grid=(rows // tr,),
        in_specs=[flat, flat, flat, part(0), part(0), part(1), part(2)],
        out_specs=[flat] * 4,
        out_shape=[jax.ShapeDtypeStruct((rows, cols), F32)] * 4,
        compiler_params=_cparams(dimension_semantics=("arbitrary",)),
    )(w, m, v, own, recv, recv, recv)


def _ada_backward(act_t, dmod16, dmodc8, ada_w_sh, m, v):
    def body(at_ref, dm_ref, dc_ref, w_ref, m_ref, v_ref, g_ref, d_ref, m_out, v_out, pc_ref):
        g = jnp.dot(at_ref[...], dm_ref[...], preferred_element_type=F32, precision=lax.Precision.HIGHEST)
        w = w_ref[...]
        delta, m2, v2 = _adam_update(w, g, m_ref[...], v_ref[...])
        g_ref[...] = g
        d_ref[...] = delta
        m_out[...] = m2
        v_out[...] = v2
        pc_ref[...] = lax.dot_general(dc_ref[...], w, (((1,), (1,)), ((), ())), preferred_element_type=F32,
                                      precision=lax.Precision.HIGHEST)

    return pl.pallas_call(
        body, name="ada_backward",
        out_shape=[jax.ShapeDtypeStruct((D, ADA_SH), F32)] * 4 + [jax.ShapeDtypeStruct((8, D), F32)],
        compiler_params=_cparams(),
    )(act_t, dmod16, dmodc8, ada_w_sh, m, v)


def _sum_devices(gathered, name):
    _, rows, cols = gathered.shape

    def body(g_ref, o_ref):
        acc = g_ref[0]
        for d in range(1, N_DEV):
            acc = acc + g_ref[d]
        o_ref[...] = acc

    return pl.pallas_call(body, name=name, out_shape=jax.ShapeDtypeStruct((rows, cols), F32),
                          compiler_params=_cparams())(gathered)


SMALL_PARAMS = ("c_ctx", "norm_w", "conv_b", "gn_w", "final_norm_w", "ada_b", "decay_logit", "conv_w")


def _adam_small(tot, dact_ctx, conv_w_grad, params):
    n = len(SMALL_PARAMS)

    def body(tot_ref, dact_ref, cwg_ref, *refs):
        ins, outs = refs[:3 * n], refs[3 * n:]
        nh = 2 * N_HEADS
        raw = {
            "c_ctx": dact_ref[0:1, :],
            "norm_w": tot_ref[1:2, :], "conv_b": tot_ref[2:3, :], "gn_w": tot_ref[3:4, :],
            "final_norm_w": tot_ref[0:1, :],
            "ada_b": jnp.concatenate([tot_ref[7 + r:8 + r, :] + tot_ref[10 + r:11 + r, :] for r in range(3)], axis=1),
            "decay_logit": tot_ref[14:15, 0:nh],
            "conv_w": cwg_ref[...],
        }
        for k, name in enumerate(SMALL_PARAMS):
            w, m, v = ins[3 * k][...], ins[3 * k + 1][...], ins[3 * k + 2][...]
            g = raw[name]
            if name == "c_ctx":
                s = _sigmoid(w)
                g = g * (s * (1.0 + w * (1.0 - s)))
            elif name == "decay_logit":
                g = g * (1.0 - _sigmoid(w))
            delta, m2, v2 = _adam_update(w, g, m, v)
            for ref, val in zip(outs[4 * k:4 * k + 4], (g, delta, m2, v2)):
                ref[...] = val

    flat = [a for trio in params for a in trio]
    out_shape = [jax.ShapeDtypeStruct(trio[0].shape, F32) for trio in params for _ in range(4)]
    outs = pl.pallas_call(body, name="adam_small", out_shape=out_shape, compiler_params=_cparams())(
        tot, dact_ctx, conv_w_grad, *flat)
    return [outs[4 * k:4 * k + 4] for k in range(n)]


def _adam_square(params, own, recv):
    def body(own_ref, recv_ref, *refs):
        ins, outs = refs[:9], refs[9:]
        for k in range(3):
            rows = slice(k * RSH, (k + 1) * RSH)
            g = own_ref[0, rows, :]
            for j in range(3):
                g = g + recv_ref[j, rows, :].astype(F32)
            delta, m2, v2 = _adam_update(ins[3 * k][...], g, ins[3 * k + 1][...], ins[3 * k + 2][...])
            for ref, val in zip(outs[4 * k:4 * k + 4], (g, delta, m2, v2)):
                ref[...] = val

    flat = [a for trio in params for a in trio]
    outs = pl.pallas_call(body, name="adam_square", out_shape=[jax.ShapeDtypeStruct((RSH, D), F32)] * 12,
                          compiler_params=_cparams())(own, recv, *flat)
    return [outs[4 * k:4 * k + 4] for k in range(3)]


def _rope(t, cos, sin):
    lane = lax.broadcasted_iota(jnp.int32, (1, LANES), 1)
    first_half = jnp.bitwise_and(lane, DK // 2) == 0
    partner = jnp.where(first_half, pltpu.roll(t, LANES - DK // 2, 1), pltpu.roll(t, DK // 2, 1))
    return t * cos + partner * sin


class _Shards:
    def __init__(self, arrays, picks):
        self.arrays, self.picks = list(arrays), list(picks)

    def specs(self, index_args):
        out = []
        for a, arr in enumerate(self.arrays):
            steps = [p for p, (ai, _) in enumerate(self.picks) if ai == a]
            first, slot0, last = steps[0], self.picks[steps[0]][1], arr.shape[0] - 1
            out.append(pl.BlockSpec(
                (1, D, WSH), lambda *args, first=first, slot0=slot0, last=last:
                (jnp.clip(index_args(*args) - first + slot0, 0, last), 0, 0)))
        return out

    def __add__(self, other):
        shift = len(self.arrays)
        return _Shards(self.arrays + other.arrays, self.picks + [(a + shift, s) for a, s in other.picks])


def _load_w_in(step, ids_ref, shards, w_refs, w_vmem, sems):
    @pl.when(step == 0)
    def _():
        copies = []
        for p, (a, slot) in enumerate(shards.picks):
            col = pl.multiple_of(ids_ref[p] * WSH, LANES)
            copies.append(pltpu.make_async_copy(w_refs[a].at[slot], w_vmem.at[:, pl.ds(col, WSH)], sems.at[p]))
        for cp in copies:
            cp.start()
        for cp in copies:
            cp.wait()


def _project_columns(xmb, w_ref, shard, cos, sin, p_ref):
    groups = 2 * WSH // LANES
    q0, k0, v0 = CB_QK * D // LANES, (CB_QK * D + N_HEADS * DK) // LANES, CB_V * D // LANES
    mxu_cols = 2 * LANES
    for a in range(0, 2 * WSH, mxu_cols):
        acc = _dot(xmb, w_ref[:, a:a + mxu_cols])
        for col in range(0, mxu_cols, LANES):
            grp = shard * groups + (a + col) // LANES
            rotary = jnp.logical_and(grp >= q0, grp < v0)
            scale = jnp.where(grp >= k0, DK ** -0.5, 1.0)
            piece = acc[:, col:col + LANES]
            turned = _rope(piece * scale, cos, sin)
            p_ref[:, a + col:a + col + LANES] = jnp.where(rotary, turned, piece).astype(BF16)


def _prenorm(x, ctx, modx, modc, norm_w):
    n_x_tiles = x.shape[0] // TM
    lext = x.shape[0] + ctx.shape[0]

    def body(x_ref, ctx_ref, mx_ref, mc_ref, nw_ref, xm_ref, xmt_ref, cos_ref, sin_ref):
        i = pl.program_id(0)
        is_ctx = i >= n_x_tiles
        x = jnp.where(is_ctx, ctx_ref[...], x_ref[...])
        r = lax.rsqrt(jnp.mean(x * x, axis=-1, keepdims=True) + EPS)
        sh = jnp.where(is_ctx, mc_ref[0:1, :], mx_ref[0:1, :])
        sc = jnp.where(is_ctx, mc_ref[1:2, :], mx_ref[1:2, :])
        xm = (x * r * nw_ref[...]) * (1.0 + sc) + sh
        xm_ref[...] = xm.astype(BF16)
        xmt_ref[...] = xm.T.astype(BF16)
        pos = i * TM + lax.broadcasted_iota(jnp.int32, (TM, 1), 0)
        lane = lax.broadcasted_iota(jnp.int32, (1, LANES), 1)
        n_freq = DK // 4
        inv = jnp.exp((lane % n_freq).astype(F32) * (-math.log(ROPE_BASE) / n_freq))
        by_column = (lane % (2 * n_freq)) >= n_freq
        ang = jnp.where(by_column, (pos % GRID_W).astype(F32), (pos // GRID_W).astype(F32)) * inv
        first_half = (lane % DK) < DK // 2
        cos_ref[...] = jnp.where(is_ctx, 1.0, jnp.cos(ang))
        sin_ref[...] = jnp.where(is_ctx, 0.0, jnp.where(first_half, -jnp.sin(ang), jnp.sin(ang)))

    full = lambda a: pl.BlockSpec(a.shape, lambda i: (0,) * a.ndim)
    row = lambda width: pl.BlockSpec((TM, width), lambda i: (i, 0))
    return pl.pallas_call(
        body, name="prenorm", grid=(lext // TM,),
        in_specs=[pl.BlockSpec((TM, D), lambda i: (jnp.minimum(i, n_x_tiles - 1), 0)), full(ctx),
                  full(modx), full(modc), full(norm_w)],
        out_specs=[row(D), pl.BlockSpec((D, TM), lambda i: (0, i)), row(LANES), row(LANES)],
        out_shape=[jax.ShapeDtypeStruct((lext, D), BF16), jax.ShapeDtypeStruct((D, lext), BF16),
                   jax.ShapeDtypeStruct((lext, LANES), F32), jax.ShapeDtypeStruct((lext, LANES), F32)],
        compiler_params=_cparams(dimension_semantics=("arbitrary",)),
    )(x, ctx, modx, modc, norm_w)


def _in_projection(p_ext, xm, cos_t, sin_t, w, pairs, me, name):
    lext = xm.shape[0]
    tr = lext // 4
    meta = []
    for (slot_a, k_a), (slot_b, k_b) in pairs:
        assert k_a ^ k_b == 1
        dev_a = jnp.bitwise_xor(me, k_a)
        a_low = dev_a % 2 == 0
        meta += [jnp.where(a_low, slot_a, slot_b), jnp.where(a_low, slot_b, slot_a), dev_a // 2]
    meta = jnp.stack(meta).astype(jnp.int32)

    def body(meta_ref, xm_ref, cos_ref, sin_ref, w_hbm, *rest):
        p_ref, w_ref, sems = rest[-3:]
        j, i = pl.program_id(0), pl.program_id(1)

        @pl.when(i == 0)
        def _():
            copies = [pltpu.make_async_copy(w_hbm.at[meta_ref[3 * j + h]], w_ref.at[:, h * WSH:(h + 1) * WSH],
                                            sems.at[h]) for h in range(2)]
            for cp in copies:
                cp.start()
            for cp in copies:
                cp.wait()

        _project_columns(xm_ref[...], w_ref, meta_ref[3 * j + 2], cos_ref[...], sin_ref[...], p_ref)

    row = lambda width: pl.BlockSpec((tr, width), lambda j, i, meta_: (i, 0))
    grid_spec = pltpu.PrefetchScalarGridSpec(
        num_scalar_prefetch=1, grid=(len(pairs), lext // tr),
        in_specs=[row(D), row(LANES), row(LANES), pl.BlockSpec(memory_space=pl.ANY)]
        + ([] if p_ext is None else [pl.BlockSpec(memory_space=pl.ANY)]),
        out_specs=pl.BlockSpec((tr, 2 * WSH), lambda j, i, meta_: (i, meta_[3 * j + 2])),
        scratch_shapes=[pltpu.VMEM((D, 2 * WSH), BF16), pltpu.SemaphoreType.DMA((2,))])
    return pl.pallas_call(
        body, name=name, grid_spec=grid_spec,
        out_shape=jax.ShapeDtypeStruct((lext, PW), BF16),
        input_output_aliases={} if p_ext is None else {5: 0},
        compiler_params=_cparams(dimension_semantics=("arbitrary", "arbitrary")),
    )(meta, xm, cos_t, sin_t, w, *([] if p_ext is None else [p_ext]))


def _decay_tables(lgf, lgb, n):
    i = lax.broadcasted_iota(jnp.int32, (n, 1), 0).astype(F32)
    return dict(i=i, k_f=jnp.exp(lgf * (n - 1.0 - i)), k_b=jnp.exp(lgb * i),
                q_f=jnp.exp(lgf * (i + 1.0)), q_b=jnp.exp(lgb * (n - i)))


def _decay_matrix(lgf, lgb, n, transposed=False):
    ii = lax.broadcasted_iota(jnp.int32, (n, n), 0)
    jj = lax.broadcasted_iota(jnp.int32, (n, n), 1)
    diff = (jj - ii if transposed else ii - jj).astype(F32)
    low = jnp.exp(lgf * jnp.maximum(diff, 0.0))
    up = jnp.exp(lgb * jnp.maximum(-diff, 0.0))
    return jnp.where(diff > 0, low, jnp.where(diff < 0, up, 2.0)), diff


def _cat_lanes(a, b):
    return jnp.concatenate([a.astype(BF16), b.astype(BF16)], axis=1)


def _retention_forward(p_ext, lg, seq, ctx_len):
    lext = seq + ctx_len
    n_chunks = seq // RET_C
    C = RET_C

    def body(lg_ref, q_ref, k_ref, v_ref, rn_ref, rstd_ref, kv_scr, sf_scr, sb_scr):
        pair = pl.program_id(0)
        lane = lax.broadcasted_iota(jnp.int32, (1, LANES), 1)
        heads = range(2)
        hmask = [(lane // DK == hh).astype(F32) for hh in heads]
        lgf = [lg_ref[0, 2 * pair + hh] for hh in heads]
        lgb = [lg_ref[1, 2 * pair + hh] for hh in heads]
        vs = [slice(hh * DV, (hh + 1) * DV) for hh in heads]
        t = [_decay_tables(lgf[hh], lgb[hh], C) for hh in heads]
        kc_all = k_ref[seq:lext, :].astype(F32)
        s0 = []
        for hh in heads:
            tc = _decay_tables(lgf[hh], lgb[hh], ctx_len)
            kc = kc_all * hmask[hh]
            s0.append(_dot_tn(_cat_lanes(kc * tc["k_f"], kc * tc["k_b"]), v_ref[seq:lext, vs[hh]]))

        def increments(c, carry):
            rows = pl.ds(pl.multiple_of(c * C, C), C)
            k_all = k_ref[rows, :].astype(F32)
            for hh in heads:
                k = k_all * hmask[hh]
                kv_scr[hh, c] = _dot_tn(_cat_lanes(k * t[hh]["k_f"], k * t[hh]["k_b"]), v_ref[rows, vs[hh]])
            return carry

        lax.fori_loop(0, n_chunks, increments, 0, unroll=2)
        gf_c = [jnp.exp(lgf[hh] * C) for hh in heads]
        gb_c = [jnp.exp(lgb[hh] * C) for hh in heads]

        def scan_f(c, s):
            for hh in heads:
                sf_scr[hh, c] = s[hh]
            return tuple(gf_c[hh] * s[hh] + kv_scr[hh, c, 0:LANES, :] for hh in heads)

        def scan_b(n, s):
            c = n_chunks - 1 - n
            for hh in heads:
                sb_scr[hh, c] = s[hh]
            return tuple(gb_c[hh] * s[hh] + kv_scr[hh, c, LANES:2 * LANES, :] for hh in heads)

        lax.fori_loop(0, n_chunks, scan_f, tuple(s0[hh][0:LANES] for hh in heads))
        lax.fori_loop(0, n_chunks, scan_b, tuple(s0[hh][LANES:2 * LANES] for hh in heads))
        dmat = [_decay_matrix(lgf[hh], lgb[hh], C)[0] for hh in heads]

        def outputs(c, carry):
            rows = pl.ds(pl.multiple_of(c * C, C), C)
            q_all = q_ref[rows, :].astype(F32)
            k = k_ref[rows, :]
            for hh in heads:
                q = q_all * hmask[hh]
                v = v_ref[rows, vs[hh]]
                s = _dot_nt(q.astype(BF16), k)
                inner = _dot((s * dmat[hh]).astype(BF16), v)
                states = jnp.concatenate([sf_scr[hh, c], sb_scr[hh, c]], axis=0).astype(BF16)
                ret = inner + _dot(_cat_lanes(q * t[hh]["q_f"], q * t[hh]["q_b"]), states)
                mu = jnp.mean(ret, axis=-1, keepdims=True)
                cen = ret - mu
                rstd = lax.rsqrt(jnp.mean(cen * cen, axis=-1, keepdims=True) + EPS)
                rn_ref[rows, vs[hh]] = cen * rstd
                rstd_ref[rows, vs[hh]] = jnp.broadcast_to(rstd, (C, DV))
            return carry

        lax.fori_loop(0, n_chunks, outputs, 0, unroll=2)

    qk0 = CB_QK * D // LANES
    return pl.pallas_call(
        body, name="retention_forward", grid=(N_HEADS // 2,),
        in_specs=[pl.BlockSpec(memory_space=pltpu.SMEM),
                  pl.BlockSpec((lext, LANES), lambda g: (0, qk0 + g)),
                  pl.BlockSpec((lext, LANES), lambda g: (0, qk0 + N_HEADS // 2 + g)),
                  pl.BlockSpec((lext, 2 * DV), lambda g: (0, CB_V * D // (2 * DV) + g))],
        out_specs=[pl.BlockSpec((seq, 2 * DV), lambda g: (0, g))] * 2,
        out_shape=[jax.ShapeDtypeStruct((seq, D), F32)] * 2,
        scratch_shapes=[pltpu.VMEM((2, n_chunks, 2 * LANES, DV), F32), pltpu.VMEM((2, n_chunks, LANES, DV), F32),
                        pltpu.VMEM((2, n_chunks, LANES, DV), F32)],
        compiler_params=_cparams(dimension_semantics=("arbitrary",)),
    )(lg, p_ext, p_ext, p_ext)


def _retention_backward(p_ext, dret, lg, cos_t, sin_t, seq, ctx_len):
    lext = seq + ctx_len
    n_chunks = seq // RET_C
    C = RET_C

    def body(lg_ref, q_ref, k_ref, v_ref, do_ref, cos_ref, sin_ref, dq_ref, dk_ref, dv_ref, dlg_ref,
             kv_scr, g_scr, sf_scr, sb_scr, gfn_scr, gbp_scr):
        pair = pl.program_id(0)
        lane = lax.broadcasted_iota(jnp.int32, (1, LANES), 1)
        heads = range(2)
        hmask = [(lane // DK == hh).astype(F32) for hh in heads]
        lgf = [lg_ref[0, 2 * pair + hh] for hh in heads]
        lgb = [lg_ref[1, 2 * pair + hh] for hh in heads]
        vs = [slice(hh * DV, (hh + 1) * DV) for hh in heads]
        t = [_decay_tables(lgf[hh], lgb[hh], C) for hh in heads]
        tc = [_decay_tables(lgf[hh], lgb[hh], ctx_len) for hh in heads]
        kc_all = k_ref[seq:lext, :].astype(F32)
        kc = [kc_all * hmask[hh] for hh in heads]
        vc = [v_ref[seq:lext, vs[hh]] for hh in heads]
        kc_cat = [_cat_lanes(kc[hh] * tc[hh]["k_f"], kc[hh] * tc[hh]["k_b"]) for hh in heads]
        s0 = [_dot_tn(kc_cat[hh], vc[hh]) for hh in heads]

        def increments(c, carry):
            rows = pl.ds(pl.multiple_of(c * C, C), C)
            k_all = k_ref[rows, :].astype(F32)
            q_all = q_ref[rows, :].astype(F32)
            for hh in heads:
                k, q = k_all * hmask[hh], q_all * hmask[hh]
                kv_scr[hh, c] = _dot_tn(_cat_lanes(k * t[hh]["k_f"], k * t[hh]["k_b"]), v_ref[rows, vs[hh]])
                g_scr[hh, c] = _dot_tn(_cat_lanes(q * t[hh]["q_f"], q * t[hh]["q_b"]), do_ref[rows, vs[hh]])
            return carry

        lax.fori_loop(0, n_chunks, increments, 0, unroll=2)
        gf_c = [jnp.exp(lgf[hh] * C) for hh in heads]
        gb_c = [jnp.exp(lgb[hh] * C) for hh in heads]

        def scan_f(c, s):
            for hh in heads:
                sf_scr[hh, c] = s[hh]
            return tuple(gf_c[hh] * s[hh] + kv_scr[hh, c, 0:LANES, :] for hh in heads)

        def scan_b(n, s):
            c = n_chunks - 1 - n
            for hh in heads:
                sb_scr[hh, c] = s[hh]
            return tuple(gb_c[hh] * s[hh] + kv_scr[hh, c, LANES:2 * LANES, :] for hh in heads)

        def scan_gf(n, carry):
            c = n_chunks - 1 - n
            for hh in heads:
                gfn_scr[hh, c] = carry[hh]
            return tuple(g_scr[hh, c, 0:LANES, :] + gf_c[hh] * carry[hh] for hh in heads)

        def scan_gb(c, carry):
            for hh in heads:
                gbp_scr[hh, c] = carry[hh]
            return tuple(g_scr[hh, c, LANES:2 * LANES, :] + gb_c[hh] * carry[hh] for hh in heads)

        lax.fori_loop(0, n_chunks, scan_f, tuple(s0[hh][0:LANES] for hh in heads))
        lax.fori_loop(0, n_chunks, scan_b, tuple(s0[hh][LANES:2 * LANES] for hh in heads))
        zero_state = jnp.zeros((LANES, DV), F32)
        gf_first = lax.fori_loop(0, n_chunks, scan_gf, (zero_state, zero_state))
        gb_last = lax.fori_loop(0, n_chunks, scan_gb, (zero_state, zero_state))

        dmat, w_f, w_b = [], [], []
        for hh in heads:
            dm, diff = _decay_matrix(lgf[hh], lgb[hh], C)
            dmat.append(dm)
            w_f.append(jnp.where(diff > 0, diff * dm, 0.0))
            w_b.append(jnp.where(diff < 0, -diff * dm, 0.0))

        def total(a):
            rows_, width = a.shape
            part = jnp.sum(a.reshape(rows_ // 8, 8, width), axis=0)
            return part[:, 0:LANES] + part[:, LANES:2 * LANES] if width == 2 * LANES else part

        def chunk_head(hh, c, rows, q_all, k_all, dlgf, dlgb):
            th = t[hh]
            qm = q_all * hmask[hh]
            km = k_all * hmask[hh]
            qb, kb = qm.astype(BF16), km.astype(BF16)
            v = v_ref[rows, vs[hh]]
            do = do_ref[rows, vs[hh]]
            s = _dot_nt(qb, kb)
            dsd = _dot_nt(do, v)
            ds = (dsd * dmat[hh]).astype(BF16)
            dq_in = _dot(ds, kb)
            dk_in = _dot_tn(ds, qb)
            dv_in = _dot_tn((s * dmat[hh]).astype(BF16), do)
            prod = s * dsd
            dlgf = dlgf + total(prod * w_f[hh])
            dlgb = dlgb + total(prod * w_b[hh])
            sf, sb = sf_scr[hh, c], sb_scr[hh, c]
            states = jnp.concatenate([sf, sb], axis=0).astype(BF16)
            dqc = _dot_nt(do, states)
            dqf = dqc[:, 0:LANES] * th["q_f"]
            dqb = dqc[:, LANES:2 * LANES] * th["q_b"]
            dq = (dq_in + dqf + dqb) * hmask[hh]
            dlgf = dlgf + total((th["i"] + 1.0) * (qm * dqf))
            dlgb = dlgb + total((C - th["i"]) * (qm * dqb))
            gfn, gbp = gfn_scr[hh, c], gbp_scr[hh, c]
            gstates = jnp.concatenate([gfn, gbp], axis=0).astype(BF16)
            dkc = _dot_nt(v, gstates)
            dkf = dkc[:, 0:LANES] * th["k_f"]
            dkb = dkc[:, LANES:2 * LANES] * th["k_b"]
            dk = dk_in + dkf + dkb
            dlgf = dlgf + total((C - 1.0 - th["i"]) * (km * dkf)) + C * gf_c[hh] * total(gfn * sf)
            dlgb = dlgb + total(th["i"] * (km * dkb)) + C * gb_c[hh] * total(gbp * sb)
            dv = dv_in + _dot(_cat_lanes(km * th["k_f"], km * th["k_b"]), gstates)
            dv_ref[rows, vs[hh]] = dv.astype(BF16)
            return dq, dk, dlgf, dlgb

        def chunk(c, carry):
            rows = pl.ds(pl.multiple_of(c * C, C), C)
            q_all = q_ref[rows, :].astype(F32)
            k_all = k_ref[rows, :].astype(F32)
            dq0, dk0, f0, b0 = chunk_head(0, c, rows, q_all, k_all, carry[0], carry[1])
            dq1, dk1, f1, b1 = chunk_head(1, c, rows, q_all, k_all, carry[2], carry[3])
            cos, sin = cos_ref[rows, :], sin_ref[rows, :]
            dq_ref[rows, :] = _rope(dq0 + dq1, cos, -sin).astype(BF16)
            dk_ref[rows, :] = (_rope(dk0 + dk1, cos, -sin) * (DK ** -0.5)).astype(BF16)
            return f0, b0, f1, b1

        zero = jnp.zeros((8, LANES), F32)
        sums = lax.fori_loop(0, n_chunks, chunk, (zero, zero, zero, zero), unroll=2)

        dlg = []
        dk_ctx = jnp.zeros((ctx_len, LANES), F32)
        for hh in heads:
            g0 = jnp.concatenate([gf_first[hh], gb_last[hh]], axis=0).astype(BF16)
            dkcc = _dot_nt(vc[hh], g0)
            dkcf = dkcc[:, 0:LANES] * tc[hh]["k_f"]
            dkcb = dkcc[:, LANES:2 * LANES] * tc[hh]["k_b"]
            dlgf = sums[2 * hh] + total((ctx_len - 1.0 - tc[hh]["i"]) * (kc[hh] * dkcf))
            dlgb = sums[2 * hh + 1] + total(tc[hh]["i"] * (kc[hh] * dkcb))
            dk_ctx = dk_ctx + (dkcf + dkcb) * (DK ** -0.5)
            dv_ref[seq:lext, vs[hh]] = _dot(kc_cat[hh], g0).astype(BF16)
            dlg += [jnp.sum(jnp.sum(a, axis=1, keepdims=True), axis=0, keepdims=True) for a in (dlgf, dlgb)]
        dk_ref[seq:lext, :] = dk_ctx.astype(BF16)
        dq_ref[seq:lext, :] = jnp.zeros((ctx_len, LANES), BF16)

        lane8 = lax.broadcasted_iota(jnp.int32, (8, LANES), 1)
        out = jnp.zeros((8, LANES), F32)
        for n, val in enumerate(dlg):
            out = jnp.where(lane8 == n, val, out)
        dlg_ref[0] = out

    qk0 = CB_QK * D // LANES
    q_spec = pl.BlockSpec((lext, LANES), lambda g: (0, qk0 + g))
    k_spec = pl.BlockSpec((lext, LANES), lambda g: (0, qk0 + N_HEADS // 2 + g))
    v_spec = pl.BlockSpec((lext, 2 * DV), lambda g: (0, CB_V * D // (2 * DV) + g))
    table = pl.BlockSpec((lext, LANES), lambda g: (0, 0))
    state = pltpu.VMEM((2, n_chunks, LANES, DV), F32)
    return pl.pallas_call(
        body, name="retention_backward", grid=(N_HEADS // 2,),
        in_specs=[pl.BlockSpec(memory_space=pltpu.SMEM), q_spec, k_spec, v_spec,
                  pl.BlockSpec((seq, 2 * DV), lambda g: (0, g)), table, table],
        out_specs=[pl.BlockSpec((lext, LANES), lambda g: (0, g)), pl.BlockSpec((lext, LANES), lambda g: (0, g)),
                   pl.BlockSpec((lext, 2 * DV), lambda g: (0, g)), pl.BlockSpec((1, 8, LANES), lambda g: (g, 0, 0))],
        out_shape=[jax.ShapeDtypeStruct((lext, N_HEADS * DK), BF16), jax.ShapeDtypeStruct((lext, N_HEADS * DK), BF16),
                   jax.ShapeDtypeStruct((lext, D), BF16), jax.ShapeDtypeStruct((N_HEADS // 2, 8, LANES), F32)],
        scratch_shapes=[pltpu.VMEM((2, n_chunks, 2 * LANES, DV), F32), pltpu.VMEM((2, n_chunks, 2 * LANES, DV), F32),
                        state, state, state, state],
        compiler_params=_cparams(dimension_semantics=("arbitrary",)),
    )(lg, p_ext, p_ext, p_ext, dret, cos_t, sin_t)


class _ColumnWriter:
    def __init__(self, dst_hbm, stage, sems, pieces, n_steps):
        self.dst, self.stage, self.sems, self.pieces, self.n_steps = dst_hbm, stage, sems, pieces, n_steps

    def _copies(self, slot, tile):
        rows = pl.ds(pl.multiple_of(tile * TM, TM), TM)
        return [pltpu.make_async_copy(self.stage.at[slot, :, pl.ds(src, width)], self.dst.at[rows, pl.ds(dst, width)],
                                      self.sems.at[slot, k]) for k, (src, dst, width) in enumerate(self.pieces)]

    def slot_for(self, step):
        slot = step % 2

        @pl.when(step >= 2)
        def _():
            for cp in self._copies(slot, step - 2):
                cp.wait()

        return slot

    def send(self, step, slot):
        for cp in self._copies(slot, step):
            cp.start()

        @pl.when(step == self.n_steps - 1)
        def _():
            if self.n_steps >= 2:
                for cp in self._copies(1 - slot, step - 1):
                    cp.wait()
            for cp in self._copies(slot, step):
                cp.wait()


def _merge(p_ext, rn, rstd, x, target, w_a, w_b, w_out, vecs, seq):
    n_tiles = seq // TM
    hb = TM // HALO
    dp_pieces = [(0, CB_BG * D, D), (D, CB_ZA * D, D), (2 * D, CB_ZB * D, 3 * D)]

    def body(h_ref, bg_ref, cg_ref, za_ref, zb_ref, ga_ref, gb_ref, hp_ref, hn_ref, cp_ref, cn_ref,
             rn_ref, rstd_ref, x_ref, t_ref, wa_ref, wb_ref, wo_ref, vec_ref,
             dx1_ref, dp_hbm, dconv_ref, dret_ref, at_ref, b_ref, part_ref, stage, dp_sems):
        i = pl.program_id(0)
        writer = _ColumnWriter(dp_hbm, stage, dp_sems, dp_pieces, n_tiles)
        slot = writer.slot_for(i)
        dpa_ref = stage.at[slot]
        f = lambda ref: ref[...].astype(F32)
        h, bg, cg, za, zb, ga, gb = f(h_ref), f(bg_ref), f(cg_ref), f(za_ref), f(zb_ref), f(ga_ref), f(gb_ref)
        gx, w0, w1, w2 = vec_ref[0:1, :], vec_ref[1:2, :], vec_ref[2:3, :], vec_ref[3:4, :]
        cb, gnw, fw = vec_ref[4:5, :], vec_ref[5:6, :], vec_ref[6:7, :]
        u = cg * h
        row = lax.broadcasted_iota(jnp.int32, (TM, 1), 0)
        u_prev = (f(cp_ref) * f(hp_ref))[HALO - 1:HALO, :]
        u_next = (f(cn_ref) * f(hn_ref))[0:1, :]
        u_prev = jnp.where(i == 0, 0.0, u_prev)
        u_next = jnp.where(i == n_tiles - 1, 0.0, u_next)
        u_up = jnp.where(row == 0, u_prev, pltpu.roll(u, 1, 0))
        u_dn = jnp.where(row == TM - 1, u_next, pltpu.roll(u, TM - 1, 0))
        conv = w0 * u_up + w1 * u + w2 * u_dn + cb
        sza = _sigmoid(za)
        silu_za = za * sza
        a_act = silu_za * bg * conv
        rn = rn_ref[...]
        szb = _sigmoid(zb)
        silu_zb = zb * szb
        rg = rn * gnw
        b_act = silu_zb * rg
        y_a = _dot(a_act.astype(BF16), wa_ref[...])
        y_b = _dot(b_act.astype(BF16), wb_ref[...])
        sga, sgb = _sigmoid(ga), _sigmoid(gb)
        mix = sga * y_a + sgb * y_b
        y = _dot(mix.astype(BF16), wo_ref[...])
        x1 = x_ref[...] + gx * y
        r1 = lax.rsqrt(jnp.mean(x1 * x1, axis=-1, keepdims=True) + EPS)
        xh1 = x1 * r1
        err = xh1 * fw - t_ref[...]
        loss = jnp.sum(jnp.sum(err * err, axis=1, keepdims=True), axis=0, keepdims=True) * (0.5 / D)
        dout = err * (1.0 / D)
        dxh = dout * fw
        dx1 = r1 * (dxh - xh1 * jnp.mean(dxh * xh1, axis=-1, keepdims=True))
        dx1_ref[...] = dx1
        dy = (dx1 * gx).astype(BF16)
        dmix = _dot_nt(dy, wo_ref[...])
        dya = (dmix * sga).astype(BF16)
        dyb = (dmix * sgb).astype(BF16)
        da = _dot_nt(dya, wa_ref[...])
        db = _dot_nt(dyb, wb_ref[...])
        dpa_ref[:, 0:D] = (da * silu_za * conv).astype(BF16)
        dpa_ref[:, D:2 * D] = (da * bg * conv * (sza * (1.0 + za * (1.0 - sza)))).astype(BF16)
        dpa_ref[:, 2 * D:3 * D] = (db * rg * (szb * (1.0 + zb * (1.0 - szb)))).astype(BF16)
        dpa_ref[:, 3 * D:4 * D] = (dmix * y_a * sga * (1.0 - sga)).astype(BF16)
        dpa_ref[:, 4 * D:5 * D] = (dmix * y_b * sgb * (1.0 - sgb)).astype(BF16)
        dconv_ref[...] = (da * silu_za * bg).astype(BF16)
        drn_n = db * silu_zb
        drn = drn_n * gnw
        rstd = rstd_ref[...]
        for hd in range(N_HEADS):
            sl = slice(hd * DV, (hd + 1) * DV)
            dh_, rh = drn[:, sl], rn[:, sl]
            m1 = jnp.mean(dh_, axis=-1, keepdims=True)
            m2 = jnp.mean(dh_ * rh, axis=-1, keepdims=True)
            dret_ref[:, sl] = (rstd[:, sl] * (dh_ - m1 - rh * m2)).astype(BF16)
        at_ref[0] = a_act.T.astype(BF16)
        at_ref[1] = b_act.T.astype(BF16)
        at_ref[2] = mix.T.astype(BF16)
        b_ref[0] = dya
        b_ref[1] = dyb
        b_ref[2] = dy

        @pl.when(i == 0)
        def _():
            part_ref[...] = jnp.zeros((8, D), F32)

        part_ref[0:1, :] += jnp.sum(dout * xh1, axis=0, keepdims=True)
        part_ref[1:2, :] += jnp.sum(dx1 * y, axis=0, keepdims=True)
        part_ref[2:3, :] += jnp.sum(drn_n * rn, axis=0, keepdims=True)
        part_ref[3:4, :] += jnp.broadcast_to(loss, (1, D))
        writer.send(i, slot)

    col = lambda cb_: pl.BlockSpec((TM, D), lambda i, cb_=cb_: (i, cb_))
    prev = lambda cb_: pl.BlockSpec((HALO, D), lambda i, cb_=cb_: (jnp.maximum(i * hb - 1, 0), cb_))
    nxt = lambda cb_: pl.BlockSpec((HALO, D), lambda i, cb_=cb_: (jnp.minimum((i + 1) * hb, n_tiles * hb - 1), cb_))
    tile = pl.BlockSpec((TM, D), lambda i: (i, 0))
    full = lambda a: pl.BlockSpec(a.shape, lambda i: (0,) * a.ndim, pipeline_mode=pl.Buffered(1))
    return pl.pallas_call(
        body, name="merge", grid=(n_tiles,),
        in_specs=[col(CB_H), col(CB_BG), col(CB_CG), col(CB_ZA), col(CB_ZB), col(CB_GA), col(CB_GB),
                  prev(CB_H), nxt(CB_H), prev(CB_CG), nxt(CB_CG),
                  tile, tile, tile, tile, full(w_a), full(w_b), full(w_out), full(vecs)],
        out_specs=[tile, pl.BlockSpec(memory_space=pl.ANY), tile, tile,
                   pl.BlockSpec((3, D, TM), lambda i: (0, 0, i)), pl.BlockSpec((3, TM, D), lambda i: (0, i, 0)),
                   pl.BlockSpec((8, D), lambda i: (0, 0))],
        out_shape=[jax.ShapeDtypeStruct((seq, D), F32), jax.ShapeDtypeStruct((p_ext.shape[0], PW), BF16),
                   jax.ShapeDtypeStruct((seq, D), BF16), jax.ShapeDtypeStruct((seq, D), BF16),
                   jax.ShapeDtypeStruct((3, D, seq), BF16), jax.ShapeDtypeStruct((3, seq, D), BF16),
                   jax.ShapeDtypeStruct((8, D), F32)],
        scratch_shapes=[pltpu.VMEM((2, TM, 5 * D), BF16), pltpu.SemaphoreType.DMA((2, len(dp_pieces)))],
        compiler_params=_cparams(dimension_semantics=("arbitrary",)),
    )(p_ext, p_ext, p_ext, p_ext, p_ext, p_ext, p_ext, p_ext, p_ext, p_ext, p_ext,
      rn, rstd, x, target, w_a, w_b, w_out, vecs)


def _conv_backward(p_ext, dconv, dp, dq, dk, dv, vecs, seq):
    n_tiles = seq // TM
    hb = TM // HALO
    qk = N_HEADS * DK
    pieces = [(0, CB_H * D, D), (D, CB_CG * D, D), (2 * D, CB_QK * D, 2 * D)]
    zero_pieces = [(CB_BG * D, D), (CB_ZA * D, D), (CB_ZB * D, 3 * D)]

    def body(h_ref, cg_ref, dc_ref, dcp_ref, dcn_ref, dq_ref, dk_ref, dv_ref, vec_ref, dp_in, dp_hbm, part_ref,
             stage, sems, zeros, zero_sems):
        del dp_in
        i = pl.program_id(0)
        writer = _ColumnWriter(dp_hbm, stage, sems, pieces, n_tiles + 1)
        slot = writer.slot_for(i)
        out = stage.at[slot]
        out[:, 2 * D:2 * D + qk] = dq_ref[...]
        out[:, 2 * D + qk:3 * D] = dk_ref[...]
        out[:, 3 * D:4 * D] = dv_ref[...]

        @pl.when(i == 0)
        def _():
            part_ref[...] = jnp.zeros((8, D), F32)

        @pl.when(i == n_tiles)
        def _():
            out[:, 0:2 * D] = jnp.zeros((TM, 2 * D), BF16)
            zeros[...] = jnp.zeros(zeros.shape, BF16)
            rows = pl.ds(n_tiles * TM, TM)
            fills = [pltpu.make_async_copy(zeros.at[:, pl.ds(0, width)], dp_hbm.at[rows, pl.ds(dst, width)],
                                           zero_sems.at[k]) for k, (dst, width) in enumerate(zero_pieces)]
            for cp in fills:
                cp.start()
            for cp in fills:
                cp.wait()

        @pl.when(i < n_tiles)
        def _():
            f = lambda ref: ref[...].astype(F32)
            h, cg, dc = f(h_ref), f(cg_ref), f(dc_ref)
            w0, w1, w2 = vec_ref[1:2, :], vec_ref[2:3, :], vec_ref[3:4, :]
            row = lax.broadcasted_iota(jnp.int32, (TM, 1), 0)
            dc_prev = jnp.where(i == 0, 0.0, f(dcp_ref)[HALO - 1:HALO, :])
            dc_next = jnp.where(i == n_tiles - 1, 0.0, f(dcn_ref)[0:1, :])
            dc_up = jnp.where(row == 0, dc_prev, pltpu.roll(dc, 1, 0))
            dc_dn = jnp.where(row == TM - 1, dc_next, pltpu.roll(dc, TM - 1, 0))
            du = w0 * dc_dn + w1 * dc + w2 * dc_up
            u = cg * h
            part_ref[0:1, :] += jnp.sum(u * dc_dn, axis=0, keepdims=True)
            part_ref[1:2, :] += jnp.sum(u * dc, axis=0, keepdims=True)
            part_ref[2:3, :] += jnp.sum(u * dc_up, axis=0, keepdims=True)
            part_ref[3:4, :] += jnp.sum(dc, axis=0, keepdims=True)
            out[:, 0:D] = (du * cg).astype(BF16)
            out[:, D:2 * D] = (du * h).astype(BF16)

        writer.send(i, slot)

    last = n_tiles - 1
    col = lambda cb_: pl.BlockSpec((TM, D), lambda i, cb_=cb_: (jnp.minimum(i, last), cb_))
    lat = lambda w: pl.BlockSpec((TM, w), lambda i: (jnp.minimum(i, last), 0))
    ext = lambda w: pl.BlockSpec((TM, w), lambda i: (i, 0))
    anyspec = pl.BlockSpec(memory_space=pl.ANY)
    return pl.pallas_call(
        body, name="conv_backward", grid=(n_tiles + 1,),
        in_specs=[col(CB_H), col(CB_CG), lat(D),
                  pl.BlockSpec((HALO, D), lambda i: (jnp.clip(i * hb - 1, 0, n_tiles * hb - 1), 0)),
                  pl.BlockSpec((HALO, D), lambda i: (jnp.minimum((i + 1) * hb, n_tiles * hb - 1), 0)),
                  ext(qk), ext(qk), ext(D), pl.BlockSpec(vecs.shape, lambda i: (0, 0)), anyspec],
        out_specs=[anyspec, pl.BlockSpec((8, D), lambda i: (0, 0))],
        out_shape=[jax.ShapeDtypeStruct(dp.shape, BF16), jax.ShapeDtypeStruct((8, D), F32)],
        input_output_aliases={9: 0},
        scratch_shapes=[pltpu.VMEM((2, TM, 4 * D), BF16), pltpu.SemaphoreType.DMA((2, len(pieces))),
                        pltpu.VMEM((TM, 3 * D), BF16), pltpu.SemaphoreType.DMA((len(zero_pieces),))],
        compiler_params=_cparams(dimension_semantics=("arbitrary",)),
    )(p_ext, p_ext, dconv, dconv, dconv, dq, dk, dv, vecs, dp)


def _input_backward(dp, shards, ids, x, ctx, dx1, modx, modc, norm_w):
    seq = x.shape[0]
    lext = seq + ctx.shape[0]
    n_x = seq // TM
    n_w = len(shards.arrays)

    def body(ids_ref, dp_ref, x_ref, ctx_ref, dx1_ref, mx_ref, mc_ref, nw_ref, *rest):
        w_hbm, (gx_ref, part_ref, w_ref, w_sems) = rest[:n_w], rest[n_w:]
        i = pl.program_id(0)
        _load_w_in(i, ids_ref, shards, w_hbm, w_ref, w_sems)
        is_ctx = i >= n_x
        dxm = _dot_nt(dp_ref[...], w_ref[...])
        x = jnp.where(is_ctx, ctx_ref[...], x_ref[...])
        r = lax.rsqrt(jnp.mean(x * x, axis=-1, keepdims=True) + EPS)
        xh = x * r
        nw = nw_ref[...]
        sc = jnp.where(is_ctx, mc_ref[1:2, :], mx_ref[1:2, :])
        dxn = dxm * (1.0 + sc)
        dxh = dxn * nw
        dx = r * (dxh - xh * jnp.mean(dxh * xh, axis=-1, keepdims=True))

        @pl.when(jnp.logical_not(is_ctx))
        def _():
            gx_ref[...] = dx1_ref[...] + dx

        @pl.when(i == 0)
        def _():
            part_ref[...] = jnp.zeros((8, D), F32)

        fx = jnp.where(is_ctx, 0.0, 1.0)
        d_shift = jnp.sum(dxm, axis=0, keepdims=True)
        d_scale = jnp.sum(dxm * (xh * nw), axis=0, keepdims=True)
        part_ref[0:1, :] += fx * d_shift
        part_ref[1:2, :] += fx * d_scale
        part_ref[2:3, :] += jnp.sum(dxn * xh, axis=0, keepdims=True)
        part_ref[3:4, :] += (1.0 - fx) * d_shift
        part_ref[4:5, :] += (1.0 - fx) * d_scale

    lat = lambda w: pl.BlockSpec((TM, w), lambda i, ids_: (jnp.minimum(i, n_x - 1), 0))
    ext = lambda w: pl.BlockSpec((TM, w), lambda i, ids_: (i, 0))
    full = lambda a: pl.BlockSpec(a.shape, lambda i, ids_: (0,) * a.ndim)
    grid_spec = pltpu.PrefetchScalarGridSpec(
        num_scalar_prefetch=1, grid=(lext // TM,),
        in_specs=[ext(PW), lat(D), full(ctx), lat(D), full(modx), full(modc), full(norm_w)]
        + [pl.BlockSpec(memory_space=pl.ANY)] * n_w,
        out_specs=[lat(D), pl.BlockSpec((8, D), lambda i, ids_: (0, 0))],
        scratch_shapes=[pltpu.VMEM((D, PW), BF16), pltpu.SemaphoreType.DMA((N_DEV,))])
    return pl.pallas_call(
        body, name="input_backward", grid_spec=grid_spec,
        out_shape=[jax.ShapeDtypeStruct((seq, D), F32), jax.ShapeDtypeStruct((8, D), F32)],
        compiler_params=_cparams(dimension_semantics=("arbitrary",)),
    )(ids, dp, x, ctx, dx1, modx, modc, norm_w, *shards.arrays)


def _weight_grad_in(xm_t, dp, owners, name):
    lext = xm_t.shape[1]

    def body(own_ref, a_ref, b_ref, o_ref):
        o_ref[0] = _dot(a_ref[...], b_ref[...])

    grid_spec = pltpu.PrefetchScalarGridSpec(
        num_scalar_prefetch=1, grid=(4,),
        in_specs=[pl.BlockSpec((D, lext), lambda j, own: (0, 0)),
                  pl.BlockSpec((lext, WSH), lambda j, own: (0, own[j]))],
        out_specs=pl.BlockSpec((1, D, WSH), lambda j, own: (j, 0, 0)))
    return pl.pallas_call(
        body, name=name, grid_spec=grid_spec,
        out_shape=jax.ShapeDtypeStruct((4, D, WSH), F32),
        compiler_params=_cparams(dimension_semantics=("arbitrary",)),
    )(owners, xm_t, dp)


def _weight_grad_square(a_t, b):
    seq = a_t.shape[2]

    def body(a_ref, b_ref, o_ref):
        o_ref[:, 0] = _dot(a_ref[0], b_ref[0]).reshape(N_DEV, RSH, D)

    return pl.pallas_call(
        body, name="weight_grad_square", grid=(3,),
        in_specs=[pl.BlockSpec((1, D, seq), lambda t: (t, 0, 0)), pl.BlockSpec((1, seq, D), lambda t: (t, 0, 0))],
        out_specs=pl.BlockSpec((N_DEV, 1, RSH, D), lambda t: (0, t, 0, 0)),
        out_shape=jax.ShapeDtypeStruct((N_DEV, 3, RSH, D), F32),
        compiler_params=_cparams(dimension_semantics=("arbitrary",)),
    )(a_t, b)


def _tie(value, token):
    return value + token[0, 0].astype(value.dtype)


def _pad_lanes(a):
    return jnp.pad(a, ((0, 0), (0, D - a.shape[1])))


def _rows(total, *parts):
    width = max(a.shape[1] for _, a in parts)
    out = None
    for row, a in parts:
        padded = jnp.pad(a, ((row, total - row - a.shape[0]), (0, width - a.shape[1])))
        out = padded if out is None else out + padded
    return out


def kernel(x, c, ctx, c_ctx, norm_w, ada_w, ada_b, w_in, conv_w, conv_b, decay_logit, gn_w, w_a, w_b, w_out, final_norm_w, loss_target, m_c_ctx, m_norm_w, m_ada_w, m_ada_b, m_w_in, m_conv_w, m_conv_b, m_decay_logit, m_gn_w, m_w_a, m_w_b, m_w_out, m_final_norm_w, v_c_ctx, v_norm_w, v_ada_w, v_ada_b, v_w_in, v_conv_w, v_conv_b, v_decay_logit, v_gn_w, v_w_a, v_w_b, v_w_out, v_final_norm_w):
    xi, yi, ci = _coords()
    me = 4 * xi + 2 * yi + ci
    chip = 2 * xi + yi
    seq, ctx_len = x.shape[1], ctx.shape[1]
    assert seq % TM == 0 and seq % RET_C == 0 and ctx_len == TM and seq % GRID_W == 0
    csh = D // N_DEV

    blk = jnp.pad(c, ((0, 7), (0, 0))) + jnp.pad(conv_w[0], ((1, 4), (0, D - csh)))
    got, _ = _all_gather_small(blk, "gather_cond")
    conv_w_all = got[:, 1:4, 0:csh].transpose(1, 0, 2).reshape(3, D)
    c16 = _rows(16, (0, got[:, 0, :]), (N_DEV, c_ctx[None]))
    ada_b_sh = lax.dynamic_slice(ada_b, (0, me * ADA_SH), (1, ADA_SH))
    mod_sh, act16, lg = _modulation(c16, ada_w[0], ada_b_sh, decay_logit[0])
    mod_all, small_done = _all_gather_small(mod_sh, "gather_mod")
    mod_all = mod_all.transpose(1, 0, 2).reshape(16, 3 * D)
    modx = lax.dynamic_slice(mod_all, (me, 0), (1, 3 * D)).reshape(3, D)
    modc = mod_all[8].reshape(3, D)

    x2, tgt = x[0], loss_target[0]
    wblock = lambda n: jax.ShapeDtypeStruct((n, D, WSH), BF16)
    ids_of = lambda ks: jnp.stack([jnp.bitwise_xor(me, k) for k in ks]).astype(jnp.int32)
    own_shard = w_in[0].astype(BF16)[None]
    sq_shards = [w[0].astype(BF16) for w in (w_a, w_b, w_out)]
    wave1 = _Exchange(
        "gather_in_first", [jnp.broadcast_to(own_shard, (2, D, WSH)), own_shard], [wblock(4)], 3,
        lambda srcs, lands: _forward_plan([(0, 1)])(srcs, lands) + _send_plan((4, 2))(srcs[1:], lands),
        after=small_done)
    modx = _tie(modx, wave1.token)
    xm, xm_t, cos_t, sin_t = _prenorm(x2, ctx[0], modx, modc, norm_w)
    (w_pair, _), _ = wave1.wait(xm_t, 1, through=[0])
    first = _Shards([w_pair], [(0, 0), (0, 1)])
    p_ext = _in_projection(None, xm, cos_t, sin_t, w_pair, [((0, 0), (1, 1))], me, "in_projection_pair")
    _, (w_nbr,) = wave1.wait(p_ext, through=[1, 2])
    wave2 = _Exchange(
        "gather_in_second", [w_nbr, own_shard], [wblock(2)], 3,
        lambda srcs, lands: _forward_plan([(0, 2), (1, 3)])(srcs, lands) + _send_plan((6,))(srcs[1:], lands))
    (w_nbr, _), _ = wave2.wait(wave2.token, 2, through=[0])
    second = _Shards([w_nbr], [(0, 0), (0, 1), (0, 2), (0, 3)])
    p_ext = _in_projection(p_ext, xm, cos_t, sin_t, w_nbr, [((0, 4), (2, 5)), ((1, 2), (3, 3))], me,
                           "in_projection_neighbours")
    _, (w_diag,) = wave2.wait(p_ext, through=[1, 2])
    wave3 = _Exchange(
        "gather_in_third", [w_diag] + sq_shards, [jax.ShapeDtypeStruct((N_DEV, RSH, D), BF16)] * 3, 1 + 3 * (N_DEV - 1),
        lambda srcs, lands: _forward_plan([(0, 1)])(srcs, lands) + _gather_plan(srcs[1:], lands))
    (w_diag, *_), _ = wave3.wait(wave3.token, 1, through=[0])
    third = _Shards([w_diag], [(0, 0), (0, 1)])
    p_ext = _in_projection(p_ext, xm, cos_t, sin_t, w_diag, [((0, 6), (1, 7))], me, "in_projection_diagonal")
    w_shards, w_ids = first + second + third, ids_of((0, 1, 4, 2, 5, 3, 6, 7))

    rn, rstd = _retention_forward(p_ext, lg, seq, ctx_len)
    _, sq_lands = wave3.wait(rstd, through=range(1, 7))
    w_a_all, w_b_all, w_out_all = (
        lax.dynamic_update_slice(land, shard[None], (me, 0, 0)).reshape(D, D) for land, shard in zip(sq_lands, sq_shards))
    vecs = _rows(8, (0, modx[2:3]), (1, conv_w_all), (4, conv_b), (5, gn_w), (6, final_norm_w[None]))
    dx1, dp, dconv, dret, op_at, op_b, part_m = _merge(p_ext, rn, rstd, x2, tgt, w_a_all, w_b_all, w_out_all, vecs, seq)

    j4 = jnp.arange(4, dtype=jnp.int32)
    owners = (2 * jnp.bitwise_xor(chip, j4) + ci).astype(jnp.int32)
    owners_sib = (2 * jnp.bitwise_xor(chip, j4) + (1 - ci)).astype(jnp.int32)
    gw_sq = _weight_grad_square(op_at, op_b).reshape(N_DEV, 3 * RSH, D)
    rs_sq_pair = _Exchange("rs_square_pair", [gw_sq], [jax.ShapeDtypeStruct((4, 3 * RSH, D), F32)], 4,
                           _pair_plan(lambda j, chip_, c_: 2 * jnp.bitwise_xor(chip_, j) + (1 - c_)))
    dq, dk, dv, dlg = _retention_backward(p_ext, dret, _tie(lg, rs_sq_pair.token), cos_t, sin_t, seq, ctx_len)
    (gw_sq,), (r1_sq,) = rs_sq_pair.wait(dlg)
    own_sq, send_sq = _pair_sum(gw_sq, r1_sq, owners, "pair_sum_square")
    rs_sq_chips = _Exchange("rs_square_chips", [send_sq], [jax.ShapeDtypeStruct((3, 3 * RSH, D), BF16)], 3, _chips_plan)
    dp, part_c = _conv_backward(p_ext, dconv, dp, dq, dk, dv, _tie(vecs, rs_sq_chips.token), seq)
    gw_sib = _weight_grad_in(xm_t, dp, owners_sib, "weight_grad_in_sibling")
    rs_in_pair = _Exchange("rs_in_pair", [gw_sib], [jax.ShapeDtypeStruct((4, D, WSH), F32)], 4,
                           _pair_plan(lambda j, chip_, c_: j))
    gw_own = _weight_grad_in(xm_t, dp, _tie(owners, rs_in_pair.token), "weight_grad_in_own")
    _, (r1_in,) = rs_in_pair.wait(gw_own)
    own_in, send_in = _pair_sum(gw_own, r1_in, j4, "pair_sum_in")
    rs_in_chips = _Exchange("rs_in_chips", [send_in], [jax.ShapeDtypeStruct((3, D, WSH), BF16)], 3, _chips_plan)
    grad_x, part_i = _input_backward(dp, w_shards, w_ids, x2, ctx[0], dx1, _tie(modx, rs_in_chips.token), modc, norm_w)

    dl = dlg[:, 0, 0:4]
    dlg_row = jnp.pad(dl[:, 0::2].reshape(1, N_HEADS), ((0, 0), (0, D - N_HEADS))) + jnp.pad(
        dl[:, 1::2].reshape(1, N_HEADS), ((0, 0), (N_HEADS, D - 2 * N_HEADS)))
    partials = _rows(16, (0, part_m[0:1]), (1, part_i[2:3]), (2, part_c[3:4]), (3, part_m[2:3]), (4, part_c[0:3]),
                     (7, part_i[0:2]), (9, part_m[1:2]), (10, part_i[3:5]), (13, part_m[3:4]), (14, dlg_row))
    results = {}
    gather_part = _Exchange("gather_partials", [partials], [jax.ShapeDtypeStruct((N_DEV, 16, D), F32)],
                            N_DEV - 1, _gather_plan)
    _, (r2_in,) = rs_in_chips.wait(gather_part.token)
    results["w_in"] = [o[None] for o in _adam_sharded(w_in[0], m_w_in[0], v_w_in[0], own_in, r2_in, "adam_w_in")]
    _, (got,) = gather_part.wait(results["w_in"][1])
    got = lax.dynamic_update_slice(got, partials[None], (me, 0, 0))
    tot = _sum_devices(got, "sum_partials")
    dmodc = tot[10:13].reshape(1, 3 * D)
    dmod16 = _rows(16, (0, got[:, 7:10, :].reshape(N_DEV, 3 * D)), (N_DEV, dmodc))
    dmod16 = lax.dynamic_slice(dmod16, (0, me * ADA_SH), (16, ADA_SH))
    dmodc8 = _rows(8, (0, dmod16[8:9]))
    g_ada_w, d_ada_w, nm_ada_w, nv_ada_w, cctx_part = _ada_backward(
        act16.T, dmod16, dmodc8, ada_w[0], m_ada_w[0], v_ada_w[0])
    gather_cctx = _Exchange("gather_cctx", [cctx_part], [jax.ShapeDtypeStruct((N_DEV, 8, D), F32)],
                            N_DEV - 1, _gather_plan)
    _, (r2_sq,) = rs_sq_chips.wait(gather_cctx.token)
    square = _adam_square([(w_a[0], m_w_a[0], v_w_a[0]), (w_b[0], m_w_b[0], v_w_b[0]),
                           (w_out[0], m_w_out[0], v_w_out[0])], own_sq, r2_sq)
    _, (got_cctx,) = gather_cctx.wait(square[0][1])
    dact_ctx = _sum_devices(lax.dynamic_update_slice(got_cctx, cctx_part[None], (me, 0, 0)), "sum_cctx")

    view = {"c_ctx": (1, D), "final_norm_w": (1, D), "decay_logit": (1, 2 * N_HEADS), "conv_w": (3, csh)}
    given = {"c_ctx": (c_ctx, m_c_ctx, v_c_ctx), "norm_w": (norm_w, m_norm_w, v_norm_w),
             "conv_b": (conv_b, m_conv_b, v_conv_b), "gn_w": (gn_w, m_gn_w, v_gn_w),
             "final_norm_w": (final_norm_w, m_final_norm_w, v_final_norm_w), "ada_b": (ada_b, m_ada_b, v_ada_b),
             "decay_logit": (decay_logit, m_decay_logit, v_decay_logit), "conv_w": (conv_w, m_conv_w, v_conv_w)}
    small_in = [tuple(a.reshape(view.get(name, a.shape)) for a in given[name]) for name in SMALL_PARAMS]
    conv_w_grad = lax.dynamic_slice(tot, (4, me * csh), (3, csh))
    small_out = _adam_small(tot, dact_ctx, conv_w_grad, small_in)
    results.update({name: [o.reshape(given[name][0].shape) for o in outs_]
                    for name, outs_ in zip(SMALL_PARAMS, small_out)})
    for name, outs_ in zip(("w_a", "w_b", "w_out"), square):
        results[name] = [o[None] for o in outs_]
    results["ada_w"] = [o[None] for o in (g_ada_w, d_ada_w, nm_ada_w, nv_ada_w)]

    order = ("c_ctx", "norm_w", "ada_w", "ada_b", "w_in", "conv_w", "conv_b", "decay_logit", "gn_w",
             "w_a", "w_b", "w_out", "final_norm_w")
    outs = [results[name][kind] for kind in range(4) for name in order]
    return (tot[13, 0], grad_x[None], *outs)
```

```python
import functools
import math

import jax
import jax.numpy as jnp
from jax import lax
from jax.experimental import pallas as pl
from jax.experimental.pallas import tpu as pltpu

F32 = jnp.float32
BF16 = jnp.bfloat16
MESH = pl.DeviceIdType.MESH
_HBM_SPEC = pl.BlockSpec(memory_space=pltpu.HBM)
_SEM_SPEC = pl.BlockSpec(memory_space=pltpu.SEMAPHORE)
_EFFECT = pltpu.SideEffectType.DATAFLOW_SIDE_EFFECTING

N_DEV = 8
D = 1024
N_HEADS = 8
DK = 64
DV = 128
GRID_W = 64
ROPE_BASE = 10000.0
EPS = 1e-6
PW = 9 * D
WSH = PW // N_DEV
RSH = D // N_DEV
ADA_SH = 3 * D // N_DEV
TM = 256
RET_C = 256
HALO = 16
LANES = 128
VMEM_LIMIT = 60 * 1024 * 1024

ADAM_LR = 0.001
ADAM_B1 = 0.9
ADAM_B2 = 0.999
ADAM_EPS = 1e-08
ADAM_WD = 0.01
ADAM_STEP = 10

CB_H, CB_BG, CB_CG, CB_ZA, CB_QK, CB_V, CB_ZB, CB_GA, CB_GB = range(9)


def _cparams(**kw):
    return pltpu.CompilerParams(vmem_limit_bytes=VMEM_LIMIT, **kw)


def _dot(a, b):
    return jnp.dot(a, b, preferred_element_type=F32)


def _dot_nt(a, b):
    return lax.dot_general(a, b, (((1,), (1,)), ((), ())), preferred_element_type=F32)


def _dot_tn(a, b):
    return lax.dot_general(a, b, (((0,), (0,)), ((), ())), preferred_element_type=F32)


def _sigmoid(z):
    return 0.5 * jnp.tanh(0.5 * z) + 0.5


def _row_tile(rows):
    return TM if rows % TM == 0 else rows


def _coords():
    return lax.axis_index("x"), lax.axis_index("y"), lax.axis_index("c")


def _flip(v, bit):
    return 1 - v if bit else v


def _all_gather_small(blk, name):
    rows, cols = blk.shape

    def body(x_ref, out_ref, done, send_sems, recv_sems, local_sem):
        x, y, c = _coords()
        me = 4 * x + 2 * y + c
        mine = pltpu.make_async_copy(x_ref, out_ref.at[me], local_sem)
        mine.start()

        def copy(k, slot):
            peer = (_flip(x, k & 4), _flip(y, k & 2), _flip(c, k & 1))
            return pltpu.make_async_remote_copy(
                src_ref=x_ref, dst_ref=out_ref.at[slot], send_sem=send_sems.at[k - 1],
                recv_sem=recv_sems.at[k - 1], device_id=peer, device_id_type=MESH)

        for k in range(1, N_DEV):
            copy(k, me).start()
        for k in range(1, N_DEV):
            copy(k, jnp.bitwise_xor(me, k)).wait_recv()
        for k in range(1, N_DEV):
            copy(k, me).wait_send()
        mine.wait()
        done[...] = jnp.zeros((8, LANES), F32)

    vmem = pl.BlockSpec(memory_space=pltpu.VMEM)
    return pl.pallas_call(
        body, name=name,
        out_shape=[jax.ShapeDtypeStruct((N_DEV, rows, cols), blk.dtype), jax.ShapeDtypeStruct((8, LANES), F32)],
        in_specs=[vmem], out_specs=[vmem, vmem],
        scratch_shapes=[pltpu.SemaphoreType.DMA((N_DEV - 1,)), pltpu.SemaphoreType.DMA((N_DEV - 1,)),
                        pltpu.SemaphoreType.DMA],
    )(blk)


class _Absent:
    at = property(lambda self: self)

    def __getitem__(self, index):
        return self


class _Exchange:
    def __init__(self, name, srcs, land_shapes, n_copies, plan, after=None):
        self.name, self.plan, self.n_copies = name, plan, n_copies
        self.n_src, self.n_land = len(srcs), len(land_shapes)
        hbm = lambda a: pltpu.HBM(a.shape, a.dtype)
        n = self.n_src + self.n_land
        lands = [pltpu.with_memory_space_constraint(lax.empty(s.shape, s.dtype), pltpu.HBM) for s in land_shapes]
        srcs = [pltpu.with_memory_space_constraint(s, pltpu.HBM) for s in srcs]
        extra = [] if after is None else [after]

        def body(*refs):
            send_sems, recv_sems = refs[n + len(extra)], refs[n + len(extra) + 1]
            for cp in self._copies(refs, send_sems, recv_sems):
                cp.start()
            refs[-1][...] = jnp.zeros((8, LANES), F32)

        outs = pl.pallas_call(
            body, name=name + "_start",
            out_shape=(pltpu.SemaphoreType.DMA((n_copies,)), pltpu.SemaphoreType.DMA((n_copies,)),
                       *[hbm(a) for a in srcs], *[hbm(a) for a in lands], jax.ShapeDtypeStruct((8, LANES), F32)),
            in_specs=[_HBM_SPEC] * n + [pl.BlockSpec(memory_space=pl.ANY)] * len(extra),
            out_specs=(_SEM_SPEC, _SEM_SPEC, *[_HBM_SPEC] * n, pl.BlockSpec(memory_space=pltpu.VMEM)),
            input_output_aliases={i: 2 + i for i in range(n)},
            compiler_params=pltpu.CompilerParams(has_side_effects=_EFFECT),
        )(*srcs, *lands, *extra)
        self.send_sems, self.recv_sems = outs[0], outs[1]
        self.buffers = list(outs[2:2 + n])
        self.token = outs[-1]
        self.waited = 0

    def _copies(self, refs, send_sems, recv_sems, lo=0, hi=None):
        src_refs, land_refs = refs[:self.n_src], refs[self.n_src:self.n_src + self.n_land]
        planned = list(enumerate(self.plan(src_refs, land_refs)))[lo:hi]
        return [pltpu.make_async_remote_copy(src_ref=s, dst_ref=d, send_sem=send_sems.at[k], recv_sem=recv_sems.at[k],
                                             device_id=dev, device_id_type=MESH) for k, (s, d, dev) in planned]

    def wait(self, after, count=None, through=None):
        n = self.n_src + self.n_land
        lo = self.waited
        hi = self.n_copies if count is None else lo + count
        self.waited = hi
        through = list(range(n)) if through is None else list(through)
        m = len(through)

        def body(*refs):
            held = [_Absent()] * n
            for pos, ref in zip(through, refs[:m]):
                held[pos] = ref
            for cp in self._copies(held, refs[m], refs[m + 1], lo, hi):
                cp.wait_send()
                cp.wait_recv()

        outs = pl.pallas_call(
            body, name=f"{self.name}_wait{lo}" if lo or hi < self.n_copies else self.name + "_wait",
            out_shape=tuple(pltpu.HBM(self.buffers[p].shape, self.buffers[p].dtype) for p in through),
            in_specs=[_HBM_SPEC] * m + [_SEM_SPEC, _SEM_SPEC, pl.BlockSpec(memory_space=pl.ANY)],
            out_specs=tuple([_HBM_SPEC] * m),
            input_output_aliases={i: i for i in range(m)},
            compiler_params=pltpu.CompilerParams(has_side_effects=_EFFECT),
        )(*[self.buffers[p] for p in through], self.send_sems, self.recv_sems, after)
        for pos, out in zip(through, outs):
            self.buffers[pos] = out
        return list(self.buffers[:self.n_src]), list(self.buffers[self.n_src:])


def _pair_plan(src_index):
    def plan(srcs, lands):
        x, y, c = _coords()
        chip = 2 * x + y
        return [(srcs[0].at[src_index(j, chip, c)], lands[0].at[j], (x, y, 1 - c)) for j in range(4)]
    return plan


def _chips_plan(srcs, lands):
    x, y, c = _coords()
    return [(srcs[0].at[j - 1], lands[0].at[j - 1], (_flip(x, j & 2), _flip(y, j & 1), c)) for j in range(1, 4)]


def _peer(k):
    x, y, c = _coords()
    return _flip(x, k & 4), _flip(y, k & 2), _flip(c, k & 1)


def _send_plan(ks):
    def plan(srcs, lands):
        return [(srcs[0].at[0], lands[0].at[p], _peer(k)) for p, k in enumerate(ks)]
    return plan


def _forward_plan(moves):
    def plan(srcs, lands):
        return [(srcs[0].at[s], srcs[0].at[d], _peer(1)) for s, d in moves]
    return plan


def _gather_plan(srcs, lands):
    x, y, c = _coords()
    me = 4 * x + 2 * y + c
    return [(srcs[a], lands[a].at[me], (_flip(x, k & 4), _flip(y, k & 2), _flip(c, k & 1)))
            for a in range(len(srcs)) for k in range(1, N_DEV)]


def _pair_sum(grad, recv, owners, name):
    _, rows, cols = grad.shape
    tr = _row_tile(rows)

    def body(own_ref, g_ref, r_ref, mine_ref, send_ref):
        j = pl.program_id(1)
        total = g_ref[...] + r_ref[...]

        @pl.when(j == 0)
        def _():
            mine_ref[...] = total

        @pl.when(j > 0)
        def _():
            send_ref[...] = total.astype(BF16)

    grid_spec = pltpu.PrefetchScalarGridSpec(
        num_scalar_prefetch=1, grid=(rows // tr, 4),
        in_specs=[pl.BlockSpec((1, tr, cols), lambda i, j, own: (own[j], i, 0)),
                  pl.BlockSpec((1, tr, cols), lambda i, j, own: (j, i, 0))],
        out_specs=[pl.BlockSpec((1, tr, cols), lambda i, j, own: (0, i, 0)),
                   pl.BlockSpec((1, tr, cols), lambda i, j, own: (jnp.maximum(j - 1, 0), i, 0))])
    return pl.pallas_call(
        body, name=name, grid_spec=grid_spec,
        out_shape=[jax.ShapeDtypeStruct((1, rows, cols), F32), jax.ShapeDtypeStruct((3, rows, cols), BF16)],
        compiler_params=_cparams(dimension_semantics=("arbitrary", "arbitrary")),
    )(owners, grad, recv)


def _modulation(c16, ada_w_sh, ada_b_sh, decay_logit):
    def body(c_ref, w_ref, b_ref, dl_ref, mod_ref, act_ref, lg_ref):
        cv = c_ref[...]
        act = cv * _sigmoid(cv)
        act_ref[...] = act
        mod_ref[...] = jnp.dot(act, w_ref[...], preferred_element_type=F32,
                               precision=lax.Precision.HIGHEST) + b_ref[...]
        z = dl_ref[...]
        lg_ref[...] = jnp.minimum(z, 0.0) - jnp.log(1.0 + jnp.exp(-jnp.abs(z)))

    return pl.pallas_call(
        body, name="modulation",
        out_shape=[jax.ShapeDtypeStruct((16, ADA_SH), F32), jax.ShapeDtypeStruct((16, D), F32),
                   jax.ShapeDtypeStruct(decay_logit.shape, F32)],
        compiler_params=_cparams(),
    )(c16, ada_w_sh, ada_b_sh, decay_logit)


def _adam_update(w, g, m, v):
    m2 = ADAM_B1 * m + (1.0 - ADAM_B1) * g
    v2 = ADAM_B2 * v + (1.0 - ADAM_B2) * (g * g)
    m_hat = m2 / (1.0 - ADAM_B1 ** ADAM_STEP)
    v_hat = v2 / (1.0 - ADAM_B2 ** ADAM_STEP)
    delta = -ADAM_LR * (m_hat / (jnp.sqrt(v_hat) + ADAM_EPS) + ADAM_WD * w)
    return delta, m2, v2


def _adam_sharded(w, m, v, own, recv, name):
    rows, cols = w.shape
    tr = _row_tile(rows)

    def body(w_ref, m_ref, v_ref, g0, g1, g2, g3, g_ref, d_ref, m_out, v_out):
        g = ((g0[0] + g1[0].astype(F32)) + g2[0].astype(F32)) + g3[0].astype(F32)
        delta, m2, v2 = _adam_update(w_ref[...], g, m_ref[...], v_ref[...])
        g_ref[...] = g
        d_ref[...] = delta
        m_out[...] = m2
        v_out[...] = v2

    flat = pl.BlockSpec((tr, cols), lambda i: (i, 0))
    part = lambda j: pl.BlockSpec((1, tr, cols), lambda i, j=j: (j, i, 0))
    return pl.pallas_call(
        body, name=name, grid=(rows // tr,),
        in_specs=[flat, flat, flat, part(0), part(0), part(1), part(2)],
        out_specs=[flat] * 4,
        out_shape=[jax.ShapeDtypeStruct((rows, cols), F32)] * 4,
        compiler_params=_cparams(dimension_semantics=("arbitrary",)),
    )(w, m, v, own, recv, recv, recv)


def _ada_backward(act_t, dmod16, dmodc8, ada_w_sh, m, v):
    def body(at_ref, dm_ref, dc_ref, w_ref, m_ref, v_ref, g_ref, d_ref, m_out, v_out, pc_ref):
        g = jnp.dot(at_ref[...], dm_ref[...], preferred_element_type=F32, precision=lax.Precision.HIGHEST)
        w = w_ref[...]
        delta, m2, v2 = _adam_update(w, g, m_ref[...], v_ref[...])
        g_ref[...] = g
        d_ref[...] = delta
        m_out[...] = m2
        v_out[...] = v2
        pc_ref[...] = lax.dot_general(dc_ref[...], w, (((1,), (1,)), ((), ())), preferred_element_type=F32,
                                      precision=lax.Precision.HIGHEST)

    return pl.pallas_call(
        body, name="ada_backward",
        out_shape=[jax.ShapeDtypeStruct((D, ADA_SH), F32)] * 4 + [jax.ShapeDtypeStruct((8, D), F32)],
        compiler_params=_cparams(),
    )(act_t, dmod16, dmodc8, ada_w_sh, m, v)


def _sum_devices(gathered, name):
    _, rows, cols = gathered.shape

    def body(g_ref, o_ref):
        acc = g_ref[0]
        for d in range(1, N_DEV):
            acc = acc + g_ref[d]
        o_ref[...] = acc

    return pl.pallas_call(body, name=name, out_shape=jax.ShapeDtypeStruct((rows, cols), F32),
                          compiler_params=_cparams())(gathered)


SMALL_PARAMS = ("c_ctx", "norm_w", "conv_b", "gn_w", "final_norm_w", "ada_b", "decay_logit", "conv_w")


def _adam_small(tot, dact_ctx, conv_w_grad, decay_grad, params):
    n = len(SMALL_PARAMS)

    def body(tot_ref, dact_ref, cwg_ref, dlg_ref, *refs):
        ins, outs = refs[:3 * n], refs[3 * n:]
        raw = {
            "c_ctx": dact_ref[0:1, :],
            "norm_w": tot_ref[1:2, :], "conv_b": tot_ref[2:3, :], "gn_w": tot_ref[3:4, :],
            "final_norm_w": tot_ref[0:1, :],
            "ada_b": jnp.concatenate([tot_ref[7 + r:8 + r, :] + tot_ref[10 + r:11 + r, :] for r in range(3)], axis=1),
            "decay_logit": dlg_ref[...],
            "conv_w": cwg_ref[...],
        }
        for k, name in enumerate(SMALL_PARAMS):
            lead = (0,) if len(ins[3 * k].shape) == 3 else ()
            w, m, v = (ref[lead + (...,)] for ref in ins[3 * k:3 * k + 3])
            g = raw[name]
            if name == "c_ctx":
                s = _sigmoid(w)
                g = g * (s * (1.0 + w * (1.0 - s)))
            elif name == "decay_logit":
                g = g * (1.0 - _sigmoid(w))
            delta, m2, v2 = _adam_update(w, g, m, v)
            for ref, val in zip(outs[4 * k:4 * k + 4], (g, delta, m2, v2)):
                ref[lead + (...,)] = val

    flat = [a for trio in params for a in trio]
    out_shape = [jax.ShapeDtypeStruct(trio[0].shape, F32) for trio in params for _ in range(4)]
    outs = pl.pallas_call(body, name="adam_small", out_shape=out_shape, compiler_params=_cparams())(
        tot, dact_ctx, conv_w_grad, decay_grad, *flat)
    return [outs[4 * k:4 * k + 4] for k in range(n)]


def _adam_square(params, own, recv):
    def body(own_ref, recv_ref, *refs):
        ins, outs = refs[:9], refs[9:]
        for k in range(3):
            rows = slice(k * RSH, (k + 1) * RSH)
            g = own_ref[0, rows, :]
            for j in range(3):
                g = g + recv_ref[j, rows, :].astype(F32)
            delta, m2, v2 = _adam_update(ins[3 * k][...], g, ins[3 * k + 1][...], ins[3 * k + 2][...])
            for ref, val in zip(outs[4 * k:4 * k + 4], (g, delta, m2, v2)):
                ref[...] = val

    flat = [a for trio in params for a in trio]
    outs = pl.pallas_call(body, name="adam_square", out_shape=[jax.ShapeDtypeStruct((RSH, D), F32)] * 12,
                          compiler_params=_cparams())(own, recv, *flat)
    return [outs[4 * k:4 * k + 4] for k in range(3)]


def _rope(t, cos, sin):
    lane = lax.broadcasted_iota(jnp.int32, (1, LANES), 1)
    first_half = jnp.bitwise_and(lane, DK // 2) == 0
    partner = jnp.where(first_half, pltpu.roll(t, LANES - DK // 2, 1), pltpu.roll(t, DK // 2, 1))
    return t * cos + partner * sin


class _Shards:
    def __init__(self, arrays, picks):
        self.arrays, self.picks = list(arrays), list(picks)

    def specs(self, index_args):
        out = []
        for a, arr in enumerate(self.arrays):
            steps = [p for p, (ai, _) in enumerate(self.picks) if ai == a]
            first, slot0, last = steps[0], self.picks[steps[0]][1], arr.shape[0] - 1
            out.append(pl.BlockSpec(
                (1, D, WSH), lambda *args, first=first, slot0=slot0, last=last:
                (jnp.clip(index_args(*args) - first + slot0, 0, last), 0, 0)))
        return out

    def __add__(self, other):
        shift = len(self.arrays)
        return _Shards(self.arrays + other.arrays, self.picks + [(a + shift, s) for a, s in other.picks])


def _load_w_in(step, ids_ref, shards, w_refs, w_vmem, sems):
    @pl.when(step == 0)
    def _():
        copies = []
        for p, (a, slot) in enumerate(shards.picks):
            col = pl.multiple_of(ids_ref[p] * WSH, LANES)
            copies.append(pltpu.make_async_copy(w_refs[a].at[slot], w_vmem.at[:, pl.ds(col, WSH)], sems.at[p]))
        for cp in copies:
            cp.start()
        for cp in copies:
            cp.wait()


def _project_columns(xmb, w_ref, shard, cos, sin, p_ref):
    groups = 2 * WSH // LANES
    q0, k0, v0 = CB_QK * D // LANES, (CB_QK * D + N_HEADS * DK) // LANES, CB_V * D // LANES
    mxu_cols = 2 * LANES
    for a in range(0, 2 * WSH, mxu_cols):
        acc = _dot(xmb, w_ref[:, a:a + mxu_cols])
        for col in range(0, mxu_cols, LANES):
            grp = shard * groups + (a + col) // LANES
            rotary = jnp.logical_and(grp >= q0, grp < v0)
            scale = jnp.where(grp >= k0, DK ** -0.5, 1.0)
            piece = acc[:, col:col + LANES]
            turned = _rope(piece * scale, cos, sin)
            p_ref[:, a + col:a + col + LANES] = jnp.where(rotary, turned, piece).astype(BF16)


_AFTER_SPEC = pl.BlockSpec(memory_space=pl.ANY)


def _prenorm(x, ctx, modx, modc, norm_w, after):
    n_x_tiles = x.shape[0] // TM
    lext = x.shape[0] + ctx.shape[0]

    def body(x_ref, ctx_ref, mx_ref, mc_ref, nw_ref, after_ref, xm_ref, xmt_ref, cos_ref, sin_ref):
        del after_ref
        i = pl.program_id(0)
        is_ctx = i >= n_x_tiles
        x = jnp.where(is_ctx, ctx_ref[...], x_ref[...])
        r = lax.rsqrt(jnp.mean(x * x, axis=-1, keepdims=True) + EPS)
        sh = jnp.where(is_ctx, mc_ref[0:1, :], mx_ref[0:1, :])
        sc = jnp.where(is_ctx, mc_ref[1:2, :], mx_ref[1:2, :])
        xm = (x * r * nw_ref[...]) * (1.0 + sc) + sh
        xm_ref[...] = xm.astype(BF16)
        xmt_ref[...] = xm.T.astype(BF16)
        pos = i * TM + lax.broadcasted_iota(jnp.int32, (TM, 1), 0)
        lane = lax.broadcasted_iota(jnp.int32, (1, LANES), 1)
        n_freq = DK // 4
        inv = jnp.exp((lane % n_freq).astype(F32) * (-math.log(ROPE_BASE) / n_freq))
        by_column = (lane % (2 * n_freq)) >= n_freq
        ang = jnp.where(by_column, (pos % GRID_W).astype(F32), (pos // GRID_W).astype(F32)) * inv
        first_half = (lane % DK) < DK // 2
        cos_ref[...] = jnp.where(is_ctx, 1.0, jnp.cos(ang))
        sin_ref[...] = jnp.where(is_ctx, 0.0, jnp.where(first_half, -jnp.sin(ang), jnp.sin(ang)))

    full = lambda a: pl.BlockSpec(a.shape, lambda i: (0,) * a.ndim)
    row = lambda width: pl.BlockSpec((TM, width), lambda i: (i, 0))
    return pl.pallas_call(
        body, name="prenorm", grid=(lext // TM,),
        in_specs=[pl.BlockSpec((TM, D), lambda i: (jnp.minimum(i, n_x_tiles - 1), 0)), full(ctx),
                  full(modx), full(modc), full(norm_w), _AFTER_SPEC],
        out_specs=[row(D), pl.BlockSpec((D, TM), lambda i: (0, i)), row(LANES), row(LANES)],
        out_shape=[jax.ShapeDtypeStruct((lext, D), BF16), jax.ShapeDtypeStruct((D, lext), BF16),
                   jax.ShapeDtypeStruct((lext, LANES), F32), jax.ShapeDtypeStruct((lext, LANES), F32)],
        compiler_params=_cparams(dimension_semantics=("arbitrary",)),
    )(x, ctx, modx, modc, norm_w, after)


def _in_projection(p_ext, xm, cos_t, sin_t, w, pairs, me, name):
    lext = xm.shape[0]
    tr = lext // 4
    meta = []
    for (slot_a, k_a), (slot_b, k_b) in pairs:
        assert k_a ^ k_b == 1
        dev_a = jnp.bitwise_xor(me, k_a)
        a_low = dev_a % 2 == 0
        meta += [jnp.where(a_low, slot_a, slot_b), jnp.where(a_low, slot_b, slot_a), dev_a // 2]
    meta = jnp.stack(meta).astype(jnp.int32)

    def body(meta_ref, xm_ref, cos_ref, sin_ref, w_hbm, *rest):
        p_ref, w_ref, sems = rest[-3:]
        j, i = pl.program_id(0), pl.program_id(1)

        @pl.when(i == 0)
        def _():
            copies = [pltpu.make_async_copy(w_hbm.at[meta_ref[3 * j + h]], w_ref.at[:, h * WSH:(h + 1) * WSH],
                                            sems.at[h]) for h in range(2)]
            for cp in copies:
                cp.start()
            for cp in copies:
                cp.wait()

        _project_columns(xm_ref[...], w_ref, meta_ref[3 * j + 2], cos_ref[...], sin_ref[...], p_ref)

    row = lambda width: pl.BlockSpec((tr, width), lambda j, i, meta_: (i, 0))
    grid_spec = pltpu.PrefetchScalarGridSpec(
        num_scalar_prefetch=1, grid=(len(pairs), lext // tr),
        in_specs=[row(D), row(LANES), row(LANES), pl.BlockSpec(memory_space=pl.ANY)]
        + ([] if p_ext is None else [pl.BlockSpec(memory_space=pl.ANY)]),
        out_specs=pl.BlockSpec((tr, 2 * WSH), lambda j, i, meta_: (i, meta_[3 * j + 2])),
        scratch_shapes=[pltpu.VMEM((D, 2 * WSH), BF16), pltpu.SemaphoreType.DMA((2,))])
    return pl.pallas_call(
        body, name=name, grid_spec=grid_spec,
        out_shape=jax.ShapeDtypeStruct((lext, PW), BF16),
        input_output_aliases={} if p_ext is None else {5: 0},
        compiler_params=_cparams(dimension_semantics=("arbitrary", "arbitrary")),
    )(meta, xm, cos_t, sin_t, w, *([] if p_ext is None else [p_ext]))


def _decay_tables(lgf, lgb, n):
    i = lax.broadcasted_iota(jnp.int32, (n, 1), 0).astype(F32)
    return dict(i=i, k_f=jnp.exp(lgf * (n - 1.0 - i)), k_b=jnp.exp(lgb * i),
                q_f=jnp.exp(lgf * (i + 1.0)), q_b=jnp.exp(lgb * (n - i)))


def _decay_matrix(lgf, lgb, n, transposed=False):
    ii = lax.broadcasted_iota(jnp.int32, (n, n), 0)
    jj = lax.broadcasted_iota(jnp.int32, (n, n), 1)
    diff = (jj - ii if transposed else ii - jj).astype(F32)
    low = jnp.exp(lgf * jnp.maximum(diff, 0.0))
    up = jnp.exp(lgb * jnp.maximum(-diff, 0.0))
    return jnp.where(diff > 0, low, jnp.where(diff < 0, up, 2.0)), diff


def _cat_lanes(a, b):
    return jnp.concatenate([a.astype(BF16), b.astype(BF16)], axis=1)


def _retention_forward(p_ext, lg, seq, ctx_len):
    lext = seq + ctx_len
    n_chunks = seq // RET_C
    C = RET_C

    def body(lg_ref, q_ref, k_ref, v_ref, rn_ref, rstd_ref, kv_scr, sf_scr, sb_scr):
        pair = pl.program_id(0)
        lane = lax.broadcasted_iota(jnp.int32, (1, LANES), 1)
        heads = range(2)
        hmask = [(lane // DK == hh).astype(F32) for hh in heads]
        lgf = [lg_ref[0, 2 * pair + hh] for hh in heads]
        lgb = [lg_ref[1, 2 * pair + hh] for hh in heads]
        vs = [slice(hh * DV, (hh + 1) * DV) for hh in heads]
        t = [_decay_tables(lgf[hh], lgb[hh], C) for hh in heads]
        kc_all = k_ref[seq:lext, :].astype(F32)
        s0 = []
        for hh in heads:
            tc = _decay_tables(lgf[hh], lgb[hh], ctx_len)
            kc = kc_all * hmask[hh]
            s0.append(_dot_tn(_cat_lanes(kc * tc["k_f"], kc * tc["k_b"]), v_ref[seq:lext, vs[hh]]))

        def increments(c, carry):
            rows = pl.ds(pl.multiple_of(c * C, C), C)
            k_all = k_ref[rows, :].astype(F32)
            for hh in heads:
                k = k_all * hmask[hh]
                kv_scr[hh, c] = _dot_tn(_cat_lanes(k * t[hh]["k_f"], k * t[hh]["k_b"]), v_ref[rows, vs[hh]])
            return carry

        lax.fori_loop(0, n_chunks, increments, 0, unroll=2)
        gf_c = [jnp.exp(lgf[hh] * C) for hh in heads]
        gb_c = [jnp.exp(lgb[hh] * C) for hh in heads]

        def scan_f(c, s):
            for hh in heads:
                sf_scr[hh, c] = s[hh]
            return tuple(gf_c[hh] * s[hh] + kv_scr[hh, c, 0:LANES, :] for hh in heads)

        def scan_b(n, s):
            c = n_chunks - 1 - n
            for hh in heads:
                sb_scr[hh, c] = s[hh]
            return tuple(gb_c[hh] * s[hh] + kv_scr[hh, c, LANES:2 * LANES, :] for hh in heads)

        lax.fori_loop(0, n_chunks, scan_f, tuple(s0[hh][0:LANES] for hh in heads))
        lax.fori_loop(0, n_chunks, scan_b, tuple(s0[hh][LANES:2 * LANES] for hh in heads))
        dmat = [_decay_matrix(lgf[hh], lgb[hh], C)[0] for hh in heads]

        def outputs(c, carry):
            rows = pl.ds(pl.multiple_of(c * C, C), C)
            q_all = q_ref[rows, :].astype(F32)
            k = k_ref[rows, :]
            for hh in heads:
                q = q_all * hmask[hh]
                v = v_ref[rows, vs[hh]]
                s = _dot_nt(q.astype(BF16), k)
                inner = _dot((s * dmat[hh]).astype(BF16), v)
                states = jnp.concatenate([sf_scr[hh, c], sb_scr[hh, c]], axis=0).astype(BF16)
                ret = inner + _dot(_cat_lanes(q * t[hh]["q_f"], q * t[hh]["q_b"]), states)
                mu = jnp.mean(ret, axis=-1, keepdims=True)
                cen = ret - mu
                rstd = lax.rsqrt(jnp.mean(cen * cen, axis=-1, keepdims=True) + EPS)
                rn_ref[rows, vs[hh]] = cen * rstd
                rstd_ref[rows, vs[hh]] = jnp.broadcast_to(rstd, (C, DV))
            return carry

        lax.fori_loop(0, n_chunks, outputs, 0, unroll=2)

    qk0 = CB_QK * D // LANES
    return pl.pallas_call(
        body, name="retention_forward", grid=(N_HEADS // 2,),
        in_specs=[pl.BlockSpec(memory_space=pltpu.SMEM),
                  pl.BlockSpec((lext, LANES), lambda g: (0, qk0 + g)),
                  pl.BlockSpec((lext, LANES), lambda g: (0, qk0 + N_HEADS // 2 + g)),
                  pl.BlockSpec((lext, 2 * DV), lambda g: (0, CB_V * D // (2 * DV) + g))],
        out_specs=[pl.BlockSpec((seq, 2 * DV), lambda g: (0, g))] * 2,
        out_shape=[jax.ShapeDtypeStruct((seq, D), F32)] * 2,
        scratch_shapes=[pltpu.VMEM((2, n_chunks, 2 * LANES, DV), F32), pltpu.VMEM((2, n_chunks, LANES, DV), F32),
                        pltpu.VMEM((2, n_chunks, LANES, DV), F32)],
        compiler_params=_cparams(dimension_semantics=("arbitrary",)),
    )(lg, p_ext, p_ext, p_ext)


def _retention_backward(p_ext, dret, lg, cos_t, sin_t, seq, ctx_len, after):
    lext = seq + ctx_len
    n_chunks = seq // RET_C
    C = RET_C

    def body(lg_ref, q_ref, k_ref, v_ref, do_ref, cos_ref, sin_ref, after_ref, dq_ref, dk_ref, dv_ref, dlg_ref,
             kv_scr, g_scr, sf_scr, sb_scr, gfn_scr, gbp_scr):
        pair = pl.program_id(0)
        lane = lax.broadcasted_iota(jnp.int32, (1, LANES), 1)
        heads = range(2)
        hmask = [(lane // DK == hh).astype(F32) for hh in heads]
        lgf = [lg_ref[0, 2 * pair + hh] for hh in heads]
        lgb = [lg_ref[1, 2 * pair + hh] for hh in heads]
        vs = [slice(hh * DV, (hh + 1) * DV) for hh in heads]
        t = [_decay_tables(lgf[hh], lgb[hh], C) for hh in heads]
        tc = [_decay_tables(lgf[hh], lgb[hh], ctx_len) for hh in heads]
        kc_all = k_ref[seq:lext, :].astype(F32)
        kc = [kc_all * hmask[hh] for hh in heads]
        vc = [v_ref[seq:lext, vs[hh]] for hh in heads]
        kc_cat = [_cat_lanes(kc[hh] * tc[hh]["k_f"], kc[hh] * tc[hh]["k_b"]) for hh in heads]
        s0 = [_dot_tn(kc_cat[hh], vc[hh]) for hh in heads]

        def increments(c, carry):
            rows = pl.ds(pl.multiple_of(c * C, C), C)
            k_all = k_ref[rows, :].astype(F32)
            q_all = q_ref[rows, :].astype(F32)
            for hh in heads:
                k, q = k_all * hmask[hh], q_all * hmask[hh]
                kv_scr[hh, c] = _dot_tn(_cat_lanes(k * t[hh]["k_f"], k * t[hh]["k_b"]), v_ref[rows, vs[hh]])
                g_scr[hh, c] = _dot_tn(_cat_lanes(q * t[hh]["q_f"], q * t[hh]["q_b"]), do_ref[rows, vs[hh]])
            return carry

        lax.fori_loop(0, n_chunks, increments, 0, unroll=2)
        gf_c = [jnp.exp(lgf[hh] * C) for hh in heads]
        gb_c = [jnp.exp(lgb[hh] * C) for hh in heads]

        def scan_f(c, s):
            for hh in heads:
                sf_scr[hh, c] = s[hh]
            return tuple(gf_c[hh] * s[hh] + kv_scr[hh, c, 0:LANES, :] for hh in heads)

        def scan_b(n, s):
            c = n_chunks - 1 - n
            for hh in heads:
                sb_scr[hh, c] = s[hh]
            return tuple(gb_c[hh] * s[hh] + kv_scr[hh, c, LANES:2 * LANES, :] for hh in heads)

        def scan_gf(n, carry):
            c = n_chunks - 1 - n
            for hh in heads:
                gfn_scr[hh, c] = carry[hh]
            return tuple(g_scr[hh, c, 0:LANES, :] + gf_c[hh] * carry[hh] for hh in heads)

        def scan_gb(c, carry):
            for hh in heads:
                gbp_scr[hh, c] = carry[hh]
            return tuple(g_scr[hh, c, LANES:2 * LANES, :] + gb_c[hh] * carry[hh] for hh in heads)

        lax.fori_loop(0, n_chunks, scan_f, tuple(s0[hh][0:LANES] for hh in heads))
        lax.fori_loop(0, n_chunks, scan_b, tuple(s0[hh][LANES:2 * LANES] for hh in heads))
        zero_state = jnp.zeros((LANES, DV), F32)
        gf_first = lax.fori_loop(0, n_chunks, scan_gf, (zero_state, zero_state))
        gb_last = lax.fori_loop(0, n_chunks, scan_gb, (zero_state, zero_state))

        dmat, w_f, w_b = [], [], []
        for hh in heads:
            dm, diff = _decay_matrix(lgf[hh], lgb[hh], C)
            dmat.append(dm)
            w_f.append(jnp.where(diff > 0, diff * dm, 0.0))
            w_b.append(jnp.where(diff < 0, -diff * dm, 0.0))

        def total(a):
            rows_, width = a.shape
            part = jnp.sum(a.reshape(rows_ // 8, 8, width), axis=0)
            return part[:, 0:LANES] + part[:, LANES:2 * LANES] if width == 2 * LANES else part

        def chunk_head(hh, c, rows, q_all, k_all, dlgf, dlgb):
            th = t[hh]
            qm = q_all * hmask[hh]
            km = k_all * hmask[hh]
            qb, kb = qm.astype(BF16), km.astype(BF16)
            v = v_ref[rows, vs[hh]]
            do = do_ref[rows, vs[hh]]
            s = _dot_nt(qb, kb)
            dsd = _dot_nt(do, v)
            ds = (dsd * dmat[hh]).astype(BF16)
            dq_in = _dot(ds, kb)
            dk_in = _dot_tn(ds, qb)
            dv_in = _dot_tn((s * dmat[hh]).astype(BF16), do)
            prod = s * dsd
            dlgf = dlgf + total(prod * w_f[hh])
            dlgb = dlgb + total(prod * w_b[hh])
            sf, sb = sf_scr[hh, c], sb_scr[hh, c]
            states = jnp.concatenate([sf, sb], axis=0).astype(BF16)
            dqc = _dot_nt(do, states)
            dqf = dqc[:, 0:LANES] * th["q_f"]
            dqb = dqc[:, LANES:2 * LANES] * th["q_b"]
            dq = (dq_in + dqf + dqb) * hmask[hh]
            dlgf = dlgf + total((th["i"] + 1.0) * (qm * dqf))
            dlgb = dlgb + total((C - th["i"]) * (qm * dqb))
            gfn, gbp = gfn_scr[hh, c], gbp_scr[hh, c]
            gstates = jnp.concatenate([gfn, gbp], axis=0).astype(BF16)
            dkc = _dot_nt(v, gstates)
            dkf = dkc[:, 0:LANES] * th["k_f"]
            dkb = dkc[:, LANES:2 * LANES] * th["k_b"]
            dk = dk_in + dkf + dkb
            dlgf = dlgf + total((C - 1.0 - th["i"]) * (km * dkf)) + C * gf_c[hh] * total(gfn * sf)
            dlgb = dlgb + total(th["i"] * (km * dkb)) + C * gb_c[hh] * total(gbp * sb)
            dv = dv_in + _dot(_cat_lanes(km * th["k_f"], km * th["k_b"]), gstates)
            dv_ref[rows, vs[hh]] = dv.astype(BF16)
            return dq, dk, dlgf, dlgb

        def chunk(c, carry):
            rows = pl.ds(pl.multiple_of(c * C, C), C)
            q_all = q_ref[rows, :].astype(F32)
            k_all = k_ref[rows, :].astype(F32)
            dq0, dk0, f0, b0 = chunk_head(0, c, rows, q_all, k_all, carry[0], carry[1])
            dq1, dk1, f1, b1 = chunk_head(1, c, rows, q_all, k_all, carry[2], carry[3])
            cos, sin = cos_ref[rows, :], sin_ref[rows, :]
            dq_ref[rows, :] = _rope(dq0 + dq1, cos, -sin).astype(BF16)
            dk_ref[rows, :] = (_rope(dk0 + dk1, cos, -sin) * (DK ** -0.5)).astype(BF16)
            return f0, b0, f1, b1

        zero = jnp.zeros((8, LANES), F32)
        sums = lax.fori_loop(0, n_chunks, chunk, (zero, zero, zero, zero), unroll=2)

        dlg = []
        dk_ctx = jnp.zeros((ctx_len, LANES), F32)
        for hh in heads:
            g0 = jnp.concatenate([gf_first[hh], gb_last[hh]], axis=0).astype(BF16)
            dkcc = _dot_nt(vc[hh], g0)
            dkcf = dkcc[:, 0:LANES] * tc[hh]["k_f"]
            dkcb = dkcc[:, LANES:2 * LANES] * tc[hh]["k_b"]
            dlgf = sums[2 * hh] + total((ctx_len - 1.0 - tc[hh]["i"]) * (kc[hh] * dkcf))
            dlgb = sums[2 * hh + 1] + total(tc[hh]["i"] * (kc[hh] * dkcb))
            dk_ctx = dk_ctx + (dkcf + dkcb) * (DK ** -0.5)
            dv_ref[seq:lext, vs[hh]] = _dot(kc_cat[hh], g0).astype(BF16)
            dlg += [jnp.sum(jnp.sum(a, axis=1, keepdims=True), axis=0, keepdims=True) for a in (dlgf, dlgb)]
        dk_ref[seq:lext, :] = dk_ctx.astype(BF16)
        dq_ref[seq:lext, :] = jnp.zeros((ctx_len, LANES), BF16)

        lane8 = lax.broadcasted_iota(jnp.int32, (8, LANES), 1)
        out = jnp.zeros((8, LANES), F32)
        for n, val in enumerate(dlg):
            out = jnp.where(lane8 == n, val, out)
        dlg_ref[0] = out

    qk0 = CB_QK * D // LANES
    q_spec = pl.BlockSpec((lext, LANES), lambda g: (0, qk0 + g))
    k_spec = pl.BlockSpec((lext, LANES), lambda g: (0, qk0 + N_HEADS // 2 + g))
    v_spec = pl.BlockSpec((lext, 2 * DV), lambda g: (0, CB_V * D // (2 * DV) + g))
    table = pl.BlockSpec((lext, LANES), lambda g: (0, 0))
    state = pltpu.VMEM((2, n_chunks, LANES, DV), F32)
    return pl.pallas_call(
        body, name="retention_backward", grid=(N_HEADS // 2,),
        in_specs=[pl.BlockSpec(memory_space=pltpu.SMEM), q_spec, k_spec, v_spec,
                  pl.BlockSpec((seq, 2 * DV), lambda g: (0, g)), table, table, _AFTER_SPEC],
        out_specs=[pl.BlockSpec((lext, LANES), lambda g: (0, g)), pl.BlockSpec((lext, LANES), lambda g: (0, g)),
                   pl.BlockSpec((lext, 2 * DV), lambda g: (0, g)), pl.BlockSpec((1, 8, LANES), lambda g: (g, 0, 0))],
        out_shape=[jax.ShapeDtypeStruct((lext, N_HEADS * DK), BF16), jax.ShapeDtypeStruct((lext, N_HEADS * DK), BF16),
                   jax.ShapeDtypeStruct((lext, D), BF16), jax.ShapeDtypeStruct((N_HEADS // 2, 8, LANES), F32)],
        scratch_shapes=[pltpu.VMEM((2, n_chunks, 2 * LANES, DV), F32), pltpu.VMEM((2, n_chunks, 2 * LANES, DV), F32),
                        state, state, state, state],
        compiler_params=_cparams(dimension_semantics=("arbitrary",)),
    )(lg, p_ext, p_ext, p_ext, dret, cos_t, sin_t, after)


class _ColumnWriter:
    def __init__(self, dst_hbm, stage, sems, pieces, n_steps):
        self.dst, self.stage, self.sems, self.pieces, self.n_steps = dst_hbm, stage, sems, pieces, n_steps

    def _copies(self, slot, tile):
        rows = pl.ds(pl.multiple_of(tile * TM, TM), TM)
        return [pltpu.make_async_copy(self.stage.at[slot, :, pl.ds(src, width)], self.dst.at[rows, pl.ds(dst, width)],
                                      self.sems.at[slot, k]) for k, (src, dst, width) in enumerate(self.pieces)]

    def slot_for(self, step):
        slot = step % 2

        @pl.when(step >= 2)
        def _():
            for cp in self._copies(slot, step - 2):
                cp.wait()

        return slot

    def send(self, step, slot):
        for cp in self._copies(slot, step):
            cp.start()

        @pl.when(step == self.n_steps - 1)
        def _():
            if self.n_steps >= 2:
                for cp in self._copies(1 - slot, step - 1):
                    cp.wait()
            for cp in self._copies(slot, step):
                cp.wait()


def _merge(p_ext, rn, rstd, x, target, w_a, w_b, w_out, vecs, seq):
    n_tiles = seq // TM
    hb = TM // HALO
    dp_pieces = [(0, CB_BG * D, D), (D, CB_ZA * D, D), (2 * D, CB_ZB * D, 3 * D)]

    def body(h_ref, bg_ref, cg_ref, za_ref, zb_ref, ga_ref, gb_ref, hp_ref, hn_ref, cp_ref, cn_ref,
             rn_ref, rstd_ref, x_ref, t_ref, wa_ref, wb_ref, wo_ref, vec_ref,
             dx1_ref, dp_hbm, dconv_ref, dret_ref, at_ref, b_ref, part_ref, stage, dp_sems):
        i = pl.program_id(0)
        writer = _ColumnWriter(dp_hbm, stage, dp_sems, dp_pieces, n_tiles)
        slot = writer.slot_for(i)
        dpa_ref = stage.at[slot]
        f = lambda ref: ref[...].astype(F32)
        h, bg, cg, za, zb, ga, gb = f(h_ref), f(bg_ref), f(cg_ref), f(za_ref), f(zb_ref), f(ga_ref), f(gb_ref)
        gx, w0, w1, w2 = vec_ref[0:1, :], vec_ref[1:2, :], vec_ref[2:3, :], vec_ref[3:4, :]
        cb, gnw, fw = vec_ref[4:5, :], vec_ref[5:6, :], vec_ref[6:7, :]
        u = cg * h
        row = lax.broadcasted_iota(jnp.int32, (TM, 1), 0)
        u_prev = (f(cp_ref) * f(hp_ref))[HALO - 1:HALO, :]
        u_next = (f(cn_ref) * f(hn_ref))[0:1, :]
        u_prev = jnp.where(i == 0, 0.0, u_prev)
        u_next = jnp.where(i == n_tiles - 1, 0.0, u_next)
        u_up = jnp.where(row == 0, u_prev, pltpu.roll(u, 1, 0))
        u_dn = jnp.where(row == TM - 1, u_next, pltpu.roll(u, TM - 1, 0))
        conv = w0 * u_up + w1 * u + w2 * u_dn + cb
        sza = _sigmoid(za)
        silu_za = za * sza
        a_act = silu_za * bg * conv
        rn = rn_ref[...]
        szb = _sigmoid(zb)
        silu_zb = zb * szb
        rg = rn * gnw
        b_act = silu_zb * rg
        y_a = _dot(a_act.astype(BF16), wa_ref[...])
        y_b = _dot(b_act.astype(BF16), wb_ref[...])
        sga, sgb = _sigmoid(ga), _sigmoid(gb)
        mix = sga * y_a + sgb * y_b
        y = _dot(mix.astype(BF16), wo_ref[...])
        x1 = x_ref[...] + gx * y
        r1 = lax.rsqrt(jnp.mean(x1 * x1, axis=-1, keepdims=True) + EPS)
        xh1 = x1 * r1
        err = xh1 * fw - t_ref[...]
        loss = jnp.sum(jnp.sum(err * err, axis=1, keepdims=True), axis=0, keepdims=True) * (0.5 / D)
        dout = err * (1.0 / D)
        dxh = dout * fw
        dx1 = r1 * (dxh - xh1 * jnp.mean(dxh * xh1, axis=-1, keepdims=True))
        dx1_ref[...] = dx1
        dy = (dx1 * gx).astype(BF16)
        dmix = _dot_nt(dy, wo_ref[...])
        dya = (dmix * sga).astype(BF16)
        dyb = (dmix * sgb).astype(BF16)
        da = _dot_nt(dya, wa_ref[...])
        db = _dot_nt(dyb, wb_ref[...])
        dpa_ref[:, 0:D] = (da * silu_za * conv).astype(BF16)
        dpa_ref[:, D:2 * D] = (da * bg * conv * (sza * (1.0 + za * (1.0 - sza)))).astype(BF16)
        dpa_ref[:, 2 * D:3 * D] = (db * rg * (szb * (1.0 + zb * (1.0 - szb)))).astype(BF16)
        dpa_ref[:, 3 * D:4 * D] = (dmix * y_a * sga * (1.0 - sga)).astype(BF16)
        dpa_ref[:, 4 * D:5 * D] = (dmix * y_b * sgb * (1.0 - sgb)).astype(BF16)
        dconv_ref[...] = (da * silu_za * bg).astype(BF16)
        drn_n = db * silu_zb
        drn = drn_n * gnw
        rstd = rstd_ref[...]
        for hd in range(N_HEADS):
            sl = slice(hd * DV, (hd + 1) * DV)
            dh_, rh = drn[:, sl], rn[:, sl]
            m1 = jnp.mean(dh_, axis=-1, keepdims=True)
            m2 = jnp.mean(dh_ * rh, axis=-1, keepdims=True)
            dret_ref[:, sl] = (rstd[:, sl] * (dh_ - m1 - rh * m2)).astype(BF16)
        at_ref[0] = a_act.T.astype(BF16)
        at_ref[1] = b_act.T.astype(BF16)
        at_ref[2] = mix.T.astype(BF16)
        b_ref[0] = dya
        b_ref[1] = dyb
        b_ref[2] = dy

        @pl.when(i == 0)
        def _():
            part_ref[...] = jnp.zeros((8, D), F32)

        part_ref[0:1, :] += jnp.sum(dout * xh1, axis=0, keepdims=True)
        part_ref[1:2, :] += jnp.sum(dx1 * y, axis=0, keepdims=True)
        part_ref[2:3, :] += jnp.sum(drn_n * rn, axis=0, keepdims=True)
        part_ref[3:4, :] += jnp.broadcast_to(loss, (1, D))
        writer.send(i, slot)

    col = lambda cb_: pl.BlockSpec((TM, D), lambda i, cb_=cb_: (i, cb_))
    prev = lambda cb_: pl.BlockSpec((HALO, D), lambda i, cb_=cb_: (jnp.maximum(i * hb - 1, 0), cb_))
    nxt = lambda cb_: pl.BlockSpec((HALO, D), lambda i, cb_=cb_: (jnp.minimum((i + 1) * hb, n_tiles * hb - 1), cb_))
    tile = pl.BlockSpec((TM, D), lambda i: (i, 0))
    full = lambda a: pl.BlockSpec(a.shape, lambda i: (0,) * a.ndim, pipeline_mode=pl.Buffered(1))
    return pl.pallas_call(
        body, name="merge", grid=(n_tiles,),
        in_specs=[col(CB_H), col(CB_BG), col(CB_CG), col(CB_ZA), col(CB_ZB), col(CB_GA), col(CB_GB),
                  prev(CB_H), nxt(CB_H), prev(CB_CG), nxt(CB_CG),
                  tile, tile, tile, tile, full(w_a), full(w_b), full(w_out), full(vecs)],
        out_specs=[tile, pl.BlockSpec(memory_space=pl.ANY), tile, tile,
                   pl.BlockSpec((3, D, TM), lambda i: (0, 0, i)), pl.BlockSpec((3, TM, D), lambda i: (0, i, 0)),
                   pl.BlockSpec((8, D), lambda i: (0, 0))],
        out_shape=[jax.ShapeDtypeStruct((seq, D), F32), jax.ShapeDtypeStruct((p_ext.shape[0], PW), BF16),
                   jax.ShapeDtypeStruct((seq, D), BF16), jax.ShapeDtypeStruct((seq, D), BF16),
                   jax.ShapeDtypeStruct((3, D, seq), BF16), jax.ShapeDtypeStruct((3, seq, D), BF16),
                   jax.ShapeDtypeStruct((8, D), F32)],
        scratch_shapes=[pltpu.VMEM((2, TM, 5 * D), BF16), pltpu.SemaphoreType.DMA((2, len(dp_pieces)))],
        compiler_params=_cparams(dimension_semantics=("arbitrary",)),
    )(p_ext, p_ext, p_ext, p_ext, p_ext, p_ext, p_ext, p_ext, p_ext, p_ext, p_ext,
      rn, rstd, x, target, w_a, w_b, w_out, vecs)


def _conv_backward(p_ext, dconv, dp, dq, dk, dv, vecs, seq, after):
    n_tiles = seq // TM
    hb = TM // HALO
    qk = N_HEADS * DK
    pieces = [(0, CB_H * D, D), (D, CB_CG * D, D), (2 * D, CB_QK * D, 2 * D)]
    zero_pieces = [(CB_BG * D, D), (CB_ZA * D, D), (CB_ZB * D, 3 * D)]

    def body(h_ref, cg_ref, dc_ref, dcp_ref, dcn_ref, dq_ref, dk_ref, dv_ref, vec_ref, dp_in, after_ref, dp_hbm, part_ref,
             stage, sems, zeros, zero_sems):
        del dp_in, after_ref
        i = pl.program_id(0)
        writer = _ColumnWriter(dp_hbm, stage, sems, pieces, n_tiles + 1)
        slot = writer.slot_for(i)
        out = stage.at[slot]
        out[:, 2 * D:2 * D + qk] = dq_ref[...]
        out[:, 2 * D + qk:3 * D] = dk_ref[...]
        out[:, 3 * D:4 * D] = dv_ref[...]

        @pl.when(i == 0)
        def _():
            part_ref[...] = jnp.zeros((8, D), F32)

        @pl.when(i == n_tiles)
        def _():
            out[:, 0:2 * D] = jnp.zeros((TM, 2 * D), BF16)
            zeros[...] = jnp.zeros(zeros.shape, BF16)
            rows = pl.ds(n_tiles * TM, TM)
            fills = [pltpu.make_async_copy(zeros.at[:, pl.ds(0, width)], dp_hbm.at[rows, pl.ds(dst, width)],
                                           zero_sems.at[k]) for k, (dst, width) in enumerate(zero_pieces)]
            for cp in fills:
                cp.start()
            for cp in fills:
                cp.wait()

        @pl.when(i < n_tiles)
        def _():
            f = lambda ref: ref[...].astype(F32)
            h, cg, dc = f(h_ref), f(cg_ref), f(dc_ref)
            w0, w1, w2 = vec_ref[1:2, :], vec_ref[2:3, :], vec_ref[3:4, :]
            row = lax.broadcasted_iota(jnp.int32, (TM, 1), 0)
            dc_prev = jnp.where(i == 0, 0.0, f(dcp_ref)[HALO - 1:HALO, :])
            dc_next = jnp.where(i == n_tiles - 1, 0.0, f(dcn_ref)[0:1, :])
            dc_up = jnp.where(row == 0, dc_prev, pltpu.roll(dc, 1, 0))
            dc_dn = jnp.where(row == TM - 1, dc_next, pltpu.roll(dc, TM - 1, 0))
            du = w0 * dc_dn + w1 * dc + w2 * dc_up
            u = cg * h
            part_ref[0:1, :] += jnp.sum(u * dc_dn, axis=0, keepdims=True)
            part_ref[1:2, :] += jnp.sum(u * dc, axis=0, keepdims=True)
            part_ref[2:3, :] += jnp.sum(u * dc_up, axis=0, keepdims=True)
            part_ref[3:4, :] += jnp.sum(dc, axis=0, keepdims=True)
            out[:, 0:D] = (du * cg).astype(BF16)
            out[:, D:2 * D] = (du * h).astype(BF16)

        writer.send(i, slot)

    last = n_tiles - 1
    col = lambda cb_: pl.BlockSpec((TM, D), lambda i, cb_=cb_: (jnp.minimum(i, last), cb_))
    lat = lambda w: pl.BlockSpec((TM, w), lambda i: (jnp.minimum(i, last), 0))
    ext = lambda w: pl.BlockSpec((TM, w), lambda i: (i, 0))
    anyspec = pl.BlockSpec(memory_space=pl.ANY)
    return pl.pallas_call(
        body, name="conv_backward", grid=(n_tiles + 1,),
        in_specs=[col(CB_H), col(CB_CG), lat(D),
                  pl.BlockSpec((HALO, D), lambda i: (jnp.clip(i * hb - 1, 0, n_tiles * hb - 1), 0)),
                  pl.BlockSpec((HALO, D), lambda i: (jnp.minimum((i + 1) * hb, n_tiles * hb - 1), 0)),
                  ext(qk), ext(qk), ext(D), pl.BlockSpec(vecs.shape, lambda i: (0, 0)), anyspec, _AFTER_SPEC],
        out_specs=[anyspec, pl.BlockSpec((8, D), lambda i: (0, 0))],
        out_shape=[jax.ShapeDtypeStruct(dp.shape, BF16), jax.ShapeDtypeStruct((8, D), F32)],
        input_output_aliases={9: 0},
        scratch_shapes=[pltpu.VMEM((2, TM, 4 * D), BF16), pltpu.SemaphoreType.DMA((2, len(pieces))),
                        pltpu.VMEM((TM, 3 * D), BF16), pltpu.SemaphoreType.DMA((len(zero_pieces),))],
        compiler_params=_cparams(dimension_semantics=("arbitrary",)),
    )(p_ext, p_ext, dconv, dconv, dconv, dq, dk, dv, vecs, dp, after)


def _input_backward(dp, shards, ids, x, ctx, dx1, modx, modc, norm_w, after):
    seq = x.shape[0]
    lext = seq + ctx.shape[0]
    n_x = seq // TM
    n_w = len(shards.arrays)

    def body(ids_ref, dp_ref, x_ref, ctx_ref, dx1_ref, mx_ref, mc_ref, nw_ref, *rest):
        w_hbm, (_, gx_ref, part_ref, w_ref, w_sems) = rest[:n_w], rest[n_w:]
        i = pl.program_id(0)
        _load_w_in(i, ids_ref, shards, w_hbm, w_ref, w_sems)
        is_ctx = i >= n_x
        dxm = _dot_nt(dp_ref[...], w_ref[...])
        x = jnp.where(is_ctx, ctx_ref[...], x_ref[...])
        r = lax.rsqrt(jnp.mean(x * x, axis=-1, keepdims=True) + EPS)
        xh = x * r
        nw = nw_ref[...]
        sc = jnp.where(is_ctx, mc_ref[1:2, :], mx_ref[1:2, :])
        dxn = dxm * (1.0 + sc)
        dxh = dxn * nw
        dx = r * (dxh - xh * jnp.mean(dxh * xh, axis=-1, keepdims=True))

        @pl.when(jnp.logical_not(is_ctx))
        def _():
            gx_ref[...] = dx1_ref[...] + dx

        @pl.when(i == 0)
        def _():
            part_ref[...] = jnp.zeros((8, D), F32)

        fx = jnp.where(is_ctx, 0.0, 1.0)
        d_shift = jnp.sum(dxm, axis=0, keepdims=True)
        d_scale = jnp.sum(dxm * (xh * nw), axis=0, keepdims=True)
        part_ref[0:1, :] += fx * d_shift
        part_ref[1:2, :] += fx * d_scale
        part_ref[2:3, :] += jnp.sum(dxn * xh, axis=0, keepdims=True)
        part_ref[3:4, :] += (1.0 - fx) * d_shift
        part_ref[4:5, :] += (1.0 - fx) * d_scale

    lat = lambda w: pl.BlockSpec((TM, w), lambda i, ids_: (jnp.minimum(i, n_x - 1), 0))
    ext = lambda w: pl.BlockSpec((TM, w), lambda i, ids_: (i, 0))
    full = lambda a: pl.BlockSpec(a.shape, lambda i, ids_: (0,) * a.ndim)
    grid_spec = pltpu.PrefetchScalarGridSpec(
        num_scalar_prefetch=1, grid=(lext // TM,),
        in_specs=[ext(PW), lat(D), full(ctx), lat(D), full(modx), full(modc), full(norm_w)]
        + [pl.BlockSpec(memory_space=pl.ANY)] * n_w + [_AFTER_SPEC],
        out_specs=[lat(D), pl.BlockSpec((8, D), lambda i, ids_: (0, 0))],
        scratch_shapes=[pltpu.VMEM((D, PW), BF16), pltpu.SemaphoreType.DMA((N_DEV,))])
    return pl.pallas_call(
        body, name="input_backward", grid_spec=grid_spec,
        out_shape=[jax.ShapeDtypeStruct((seq, D), F32), jax.ShapeDtypeStruct((8, D), F32)],
        compiler_params=_cparams(dimension_semantics=("arbitrary",)),
    )(ids, dp, x, ctx, dx1, modx, modc, norm_w, *shards.arrays, after)


def _weight_grad_in(xm_t, dp, owners, name, after):
    lext = xm_t.shape[1]

    def body(own_ref, a_ref, b_ref, after_ref, o_ref):
        o_ref[0] = _dot(a_ref[...], b_ref[...])

    grid_spec = pltpu.PrefetchScalarGridSpec(
        num_scalar_prefetch=1, grid=(4,),
        in_specs=[pl.BlockSpec((D, lext), lambda j, own: (0, 0)),
                  pl.BlockSpec((lext, WSH), lambda j, own: (0, own[j])), _AFTER_SPEC],
        out_specs=pl.BlockSpec((1, D, WSH), lambda j, own: (j, 0, 0)))
    return pl.pallas_call(
        body, name=name, grid_spec=grid_spec,
        out_shape=jax.ShapeDtypeStruct((4, D, WSH), F32),
        compiler_params=_cparams(dimension_semantics=("arbitrary",)),
    )(owners, xm_t, dp, after)


def _weight_grad_square(a_t, b):
    seq = a_t.shape[2]

    def body(a_ref, b_ref, o_ref):
        o_ref[:, 0] = _dot(a_ref[0], b_ref[0]).reshape(N_DEV, RSH, D)

    return pl.pallas_call(
        body, name="weight_grad_square", grid=(3,),
        in_specs=[pl.BlockSpec((1, D, seq), lambda t: (t, 0, 0)), pl.BlockSpec((1, seq, D), lambda t: (t, 0, 0))],
        out_specs=pl.BlockSpec((N_DEV, 1, RSH, D), lambda t: (0, t, 0, 0)),
        out_shape=jax.ShapeDtypeStruct((N_DEV, 3, RSH, D), F32),
        compiler_params=_cparams(dimension_semantics=("arbitrary",)),
    )(a_t, b)


def _pad_lanes(a):
    return jnp.pad(a, ((0, 0), (0, D - a.shape[1])))


def _rows(total, *parts):
    width = max(a.shape[1] for _, a in parts)
    out = None
    for row, a in parts:
        padded = jnp.pad(a, ((row, total - row - a.shape[0]), (0, width - a.shape[1])))
        out = padded if out is None else out + padded
    return out


def kernel(x, c, ctx, c_ctx, norm_w, ada_w, ada_b, w_in, conv_w, conv_b, decay_logit, gn_w, w_a, w_b, w_out, final_norm_w, loss_target, m_c_ctx, m_norm_w, m_ada_w, m_ada_b, m_w_in, m_conv_w, m_conv_b, m_decay_logit, m_gn_w, m_w_a, m_w_b, m_w_out, m_final_norm_w, v_c_ctx, v_norm_w, v_ada_w, v_ada_b, v_w_in, v_conv_w, v_conv_b, v_decay_logit, v_gn_w, v_w_a, v_w_b, v_w_out, v_final_norm_w):
    xi, yi, ci = _coords()
    me = 4 * xi + 2 * yi + ci
    chip = 2 * xi + yi
    seq, ctx_len = x.shape[1], ctx.shape[1]
    assert seq % TM == 0 and seq % RET_C == 0 and ctx_len == TM and seq % GRID_W == 0
    csh = D // N_DEV

    blk = jnp.pad(c, ((0, 7), (0, 0))) + jnp.pad(conv_w[0], ((1, 4), (0, D - csh)))
    got, _ = _all_gather_small(blk, "gather_cond")
    conv_w_all = got[:, 1:4, 0:csh].transpose(1, 0, 2).reshape(3, D)
    c16 = _rows(16, (0, got[:, 0, :]), (N_DEV, c_ctx[None]))
    ada_b_sh = lax.dynamic_slice(ada_b, (0, me * ADA_SH), (1, ADA_SH))
    mod_sh, act16, lg = _modulation(c16, ada_w[0], ada_b_sh, decay_logit[0])
    mod_all, small_done = _all_gather_small(mod_sh, "gather_mod")
    mod_all = mod_all.transpose(1, 0, 2).reshape(16, 3 * D)
    modx = lax.dynamic_slice(mod_all, (me, 0), (1, 3 * D)).reshape(3, D)
    modc = mod_all[8].reshape(3, D)

    x2, tgt = x[0], loss_target[0]
    wblock = lambda n: jax.ShapeDtypeStruct((n, D, WSH), BF16)
    ids_of = lambda ks: jnp.stack([jnp.bitwise_xor(me, k) for k in ks]).astype(jnp.int32)
    own_shard = w_in[0].astype(BF16)[None]
    sq_shards = [w[0].astype(BF16) for w in (w_a, w_b, w_out)]
    wave1 = _Exchange(
        "gather_in_first", [jnp.broadcast_to(own_shard, (2, D, WSH)), own_shard], [wblock(4)], 3,
        lambda srcs, lands: _forward_plan([(0, 1)])(srcs, lands) + _send_plan((4, 2))(srcs[1:], lands),
        after=small_done)
    xm, xm_t, cos_t, sin_t = _prenorm(x2, ctx[0], modx, modc, norm_w, wave1.token)
    (w_pair, _), _ = wave1.wait(xm_t, 1, through=[0])
    first = _Shards([w_pair], [(0, 0), (0, 1)])
    p_ext = _in_projection(None, xm, cos_t, sin_t, w_pair, [((0, 0), (1, 1))], me, "in_projection_pair")
    _, (w_nbr,) = wave1.wait(p_ext, through=[1, 2])
    wave2 = _Exchange(
        "gather_in_second", [w_nbr, own_shard], [wblock(2)], 3,
        lambda srcs, lands: _forward_plan([(0, 2), (1, 3)])(srcs, lands) + _send_plan((6,))(srcs[1:], lands))
    (w_nbr, _), _ = wave2.wait(wave2.token, 2, through=[0])
    second = _Shards([w_nbr], [(0, 0), (0, 1), (0, 2), (0, 3)])
    p_ext = _in_projection(p_ext, xm, cos_t, sin_t, w_nbr, [((0, 4), (2, 5)), ((1, 2), (3, 3))], me,
                           "in_projection_neighbours")
    _, (w_diag,) = wave2.wait(p_ext, through=[1, 2])
    wave3 = _Exchange(
        "gather_in_third", [w_diag] + sq_shards, [jax.ShapeDtypeStruct((N_DEV, RSH, D), BF16)] * 3, 1 + 3 * (N_DEV - 1),
        lambda srcs, lands: _forward_plan([(0, 1)])(srcs, lands) + _gather_plan(srcs[1:], lands))
    (w_diag, *_), _ = wave3.wait(wave3.token, 1, through=[0])
    third = _Shards([w_diag], [(0, 0), (0, 1)])
    p_ext = _in_projection(p_ext, xm, cos_t, sin_t, w_diag, [((0, 6), (1, 7))], me, "in_projection_diagonal")
    w_shards, w_ids = first + second + third, ids_of((0, 1, 4, 2, 5, 3, 6, 7))

    rn, rstd = _retention_forward(p_ext, lg, seq, ctx_len)
    _, sq_lands = wave3.wait(rstd, through=range(1, 7))
    w_a_all, w_b_all, w_out_all = (
        lax.dynamic_update_slice(land, shard[None], (me, 0, 0)).reshape(D, D) for land, shard in zip(sq_lands, sq_shards))
    vecs = _rows(8, (0, modx[2:3]), (1, conv_w_all), (4, conv_b), (5, gn_w), (6, final_norm_w[None]))
    dx1, dp, dconv, dret, op_at, op_b, part_m = _merge(p_ext, rn, rstd, x2, tgt, w_a_all, w_b_all, w_out_all, vecs, seq)

    j4 = jnp.arange(4, dtype=jnp.int32)
    owners = (2 * jnp.bitwise_xor(chip, j4) + ci).astype(jnp.int32)
    owners_sib = (2 * jnp.bitwise_xor(chip, j4) + (1 - ci)).astype(jnp.int32)
    gw_sq = _weight_grad_square(op_at, op_b).reshape(N_DEV, 3 * RSH, D)
    rs_sq_pair = _Exchange("rs_square_pair", [gw_sq], [jax.ShapeDtypeStruct((4, 3 * RSH, D), F32)], 4,
                           _pair_plan(lambda j, chip_, c_: 2 * jnp.bitwise_xor(chip_, j) + (1 - c_)))
    dq, dk, dv, dlg = _retention_backward(p_ext, dret, lg, cos_t, sin_t, seq, ctx_len, rs_sq_pair.token)
    (gw_sq,), (r1_sq,) = rs_sq_pair.wait(dlg)
    own_sq, send_sq = _pair_sum(gw_sq, r1_sq, owners, "pair_sum_square")
    rs_sq_chips = _Exchange("rs_square_chips", [send_sq], [jax.ShapeDtypeStruct((3, 3 * RSH, D), BF16)], 3, _chips_plan)
    dp, part_c = _conv_backward(p_ext, dconv, dp, dq, dk, dv, vecs, seq, rs_sq_chips.token)
    gw_sib = _weight_grad_in(xm_t, dp, owners_sib, "weight_grad_in_sibling", part_c)
    rs_in_pair = _Exchange("rs_in_pair", [gw_sib], [jax.ShapeDtypeStruct((4, D, WSH), F32)], 4,
                           _pair_plan(lambda j, chip_, c_: j))
    gw_own = _weight_grad_in(xm_t, dp, owners, "weight_grad_in_own", rs_in_pair.token)
    _, (r1_in,) = rs_in_pair.wait(gw_own)
    own_in, send_in = _pair_sum(gw_own, r1_in, j4, "pair_sum_in")
    rs_in_chips = _Exchange("rs_in_chips", [send_in], [jax.ShapeDtypeStruct((3, D, WSH), BF16)], 3, _chips_plan)
    grad_x, part_i = _input_backward(dp, w_shards, w_ids, x2, ctx[0], dx1, modx, modc, norm_w, rs_in_chips.token)

    dl = dlg[:, 0, 0:4]
    dlg_row = jnp.pad(dl[:, 0::2].reshape(1, N_HEADS), ((0, 0), (0, D - N_HEADS))) + jnp.pad(
        dl[:, 1::2].reshape(1, N_HEADS), ((0, 0), (N_HEADS, D - 2 * N_HEADS)))
    partials = _rows(16, (0, part_m[0:1]), (1, part_i[2:3]), (2, part_c[3:4]), (3, part_m[2:3]), (4, part_c[0:3]),
                     (7, part_i[0:2]), (9, part_m[1:2]), (10, part_i[3:5]), (13, part_m[3:4]), (14, dlg_row))
    results = {}
    gather_part = _Exchange("gather_partials", [partials], [jax.ShapeDtypeStruct((N_DEV, 16, D), F32)],
                            N_DEV - 1, _gather_plan)
    _, (r2_in,) = rs_in_chips.wait(gather_part.token)
    results["w_in"] = [o[None] for o in _adam_sharded(w_in[0], m_w_in[0], v_w_in[0], own_in, r2_in, "adam_w_in")]
    _, (got,) = gather_part.wait(results["w_in"][1])
    got = lax.dynamic_update_slice(got, partials[None], (me, 0, 0))
    tot = _sum_devices(got, "sum_partials")
    dmodc = tot[10:13].reshape(1, 3 * D)
    dmod16 = _rows(16, (0, got[:, 7:10, :].reshape(N_DEV, 3 * D)), (N_DEV, dmodc))
    dmod16 = lax.dynamic_slice(dmod16, (0, me * ADA_SH), (16, ADA_SH))
    dmodc8 = _rows(8, (0, dmod16[8:9]))
    g_ada_w, d_ada_w, nm_ada_w, nv_ada_w, cctx_part = _ada_backward(
        act16.T, dmod16, dmodc8, ada_w[0], m_ada_w[0], v_ada_w[0])
    gather_cctx = _Exchange("gather_cctx", [cctx_part], [jax.ShapeDtypeStruct((N_DEV, 8, D), F32)],
                            N_DEV - 1, _gather_plan)
    _, (r2_sq,) = rs_sq_chips.wait(gather_cctx.token)
    square = _adam_square([(w_a[0], m_w_a[0], v_w_a[0]), (w_b[0], m_w_b[0], v_w_b[0]),
                           (w_out[0], m_w_out[0], v_w_out[0])], own_sq, r2_sq)
    _, (got_cctx,) = gather_cctx.wait(square[0][1])
    dact_ctx = _sum_devices(lax.dynamic_update_slice(got_cctx, cctx_part[None], (me, 0, 0)), "sum_cctx")

    view = {"c_ctx": (1, D), "final_norm_w": (1, D)}
    given = {"c_ctx": (c_ctx, m_c_ctx, v_c_ctx), "norm_w": (norm_w, m_norm_w, v_norm_w),
             "conv_b": (conv_b, m_conv_b, v_conv_b), "gn_w": (gn_w, m_gn_w, v_gn_w),
             "final_norm_w": (final_norm_w, m_final_norm_w, v_final_norm_w), "ada_b": (ada_b, m_ada_b, v_ada_b),
             "decay_logit": (decay_logit, m_decay_logit, v_decay_logit), "conv_w": (conv_w, m_conv_w, v_conv_w)}
    small_in = [tuple(a.reshape(view.get(name, a.shape)) for a in given[name]) for name in SMALL_PARAMS]
    conv_w_grad = lax.dynamic_slice(tot, (4, me * csh), (3, csh))
    small_out = _adam_small(tot, dact_ctx, conv_w_grad, tot[14, 0:2 * N_HEADS].reshape(2, N_HEADS), small_in)
    results.update({name: [o.reshape(given[name][0].shape) for o in outs_]
                    for name, outs_ in zip(SMALL_PARAMS, small_out)})
    for name, outs_ in zip(("w_a", "w_b", "w_out"), square):
        results[name] = [o[None] for o in outs_]
    results["ada_w"] = [o[None] for o in (g_ada_w, d_ada_w, nm_ada_w, nv_ada_w)]

    order = ("c_ctx", "norm_w", "ada_w", "ada_b", "w_in", "conv_w", "conv_b", "decay_logit", "gn_w",
             "w_a", "w_b", "w_out", "final_norm_w")
    outs = [results[name][kind] for kind in range(4) for name in order]
    return (tot[13, 0], grad_x[None], *outs)
```

```python
import functools
import math

import jax
import jax.numpy as jnp
from jax import lax
from jax.experimental import pallas as pl
from jax.experimental.pallas import tpu as pltpu

F32 = jnp.float32
BF16 = jnp.bfloat16
MESH = pl.DeviceIdType.MESH
_HBM_SPEC = pl.BlockSpec(memory_space=pltpu.HBM)
_SEM_SPEC = pl.BlockSpec(memory_space=pltpu.SEMAPHORE)
_EFFECT = pltpu.SideEffectType.DATAFLOW_SIDE_EFFECTING

N_DEV = 8
D = 1024
N_HEADS = 8
DK = 64
DV = 128
GRID_W = 64
ROPE_BASE = 10000.0
EPS = 1e-6
PW = 9 * D
WSH = PW // N_DEV
RSH = D // N_DEV
ADA_SH = 3 * D // N_DEV
TM = 256
RET_C = 256
HALO = 16
LANES = 128
VMEM_LIMIT = 60 * 1024 * 1024

ADAM_LR = 0.001
ADAM_B1 = 0.9
ADAM_B2 = 0.999
ADAM_EPS = 1e-08
ADAM_WD = 0.01
ADAM_STEP = 10

CB_H, CB_BG, CB_CG, CB_ZA, CB_QK, CB_V, CB_ZB, CB_GA, CB_GB = range(9)


def _cparams(**kw):
    return pltpu.CompilerParams(vmem_limit_bytes=VMEM_LIMIT, **kw)


def _dot(a, b):
    return jnp.dot(a, b, preferred_element_type=F32)


def _dot_nt(a, b):
    return lax.dot_general(a, b, (((1,), (1,)), ((), ())), preferred_element_type=F32)


def _dot_tn(a, b):
    return lax.dot_general(a, b, (((0,), (0,)), ((), ())), preferred_element_type=F32)


def _sigmoid(z):
    return 0.5 * jnp.tanh(0.5 * z) + 0.5


def _row_tile(rows):
    return TM if rows % TM == 0 else rows


def _coords():
    return lax.axis_index("x"), lax.axis_index("y"), lax.axis_index("c")


def _flip(v, bit):
    return 1 - v if bit else v


def _all_gather_small(blk, name):
    rows, cols = blk.shape

    def body(x_ref, out_ref, done, send_sems, recv_sems, local_sem):
        x, y, c = _coords()
        me = 4 * x + 2 * y + c
        mine = pltpu.make_async_copy(x_ref, out_ref.at[me], local_sem)
        mine.start()

        def copy(k, slot):
            peer = (_flip(x, k & 4), _flip(y, k & 2), _flip(c, k & 1))
            return pltpu.make_async_remote_copy(
                src_ref=x_ref, dst_ref=out_ref.at[slot], send_sem=send_sems.at[k - 1],
                recv_sem=recv_sems.at[k - 1], device_id=peer, device_id_type=MESH)

        for k in range(1, N_DEV):
            copy(k, me).start()
        for k in range(1, N_DEV):
            copy(k, jnp.bitwise_xor(me, k)).wait_recv()
        for k in range(1, N_DEV):
            copy(k, me).wait_send()
        mine.wait()
        done[...] = jnp.zeros((8, LANES), F32)

    vmem = pl.BlockSpec(memory_space=pltpu.VMEM)
    return pl.pallas_call(
        body, name=name,
        out_shape=[jax.ShapeDtypeStruct((N_DEV, rows, cols), blk.dtype), jax.ShapeDtypeStruct((8, LANES), F32)],
        in_specs=[vmem], out_specs=[vmem, vmem],
        scratch_shapes=[pltpu.SemaphoreType.DMA((N_DEV - 1,)), pltpu.SemaphoreType.DMA((N_DEV - 1,)),
                        pltpu.SemaphoreType.DMA],
    )(blk)


class _Absent:
    at = property(lambda self: self)

    def __getitem__(self, index):
        return self


class _Exchange:
    def __init__(self, name, srcs, land_shapes, n_copies, plan, after=None):
        self.name, self.plan, self.n_copies = name, plan, n_copies
        self.n_src, self.n_land = len(srcs), len(land_shapes)
        hbm = lambda a: pltpu.HBM(a.shape, a.dtype)
        n = self.n_src + self.n_land
        lands = [pltpu.with_memory_space_constraint(lax.empty(s.shape, s.dtype), pltpu.HBM) for s in land_shapes]
        srcs = [pltpu.with_memory_space_constraint(s, pltpu.HBM) for s in srcs]
        extra = [] if after is None else [after]

        def body(*refs):
            send_sems, recv_sems = refs[n + len(extra)], refs[n + len(extra) + 1]
            for cp in self._copies(refs, send_sems, recv_sems):
                cp.start()
            refs[-1][...] = jnp.zeros((8, LANES), F32)

        outs = pl.pallas_call(
            body, name=name + "_start",
            out_shape=(pltpu.SemaphoreType.DMA((n_copies,)), pltpu.SemaphoreType.DMA((n_copies,)),
                       *[hbm(a) for a in srcs], *[hbm(a) for a in lands], jax.ShapeDtypeStruct((8, LANES), F32)),
            in_specs=[_HBM_SPEC] * n + [pl.BlockSpec(memory_space=pl.ANY)] * len(extra),
            out_specs=(_SEM_SPEC, _SEM_SPEC, *[_HBM_SPEC] * n, pl.BlockSpec(memory_space=pltpu.VMEM)),
            input_output_aliases={i: 2 + i for i in range(n)},
            compiler_params=pltpu.CompilerParams(has_side_effects=_EFFECT),
        )(*srcs, *lands, *extra)
        self.send_sems, self.recv_sems = outs[0], outs[1]
        self.buffers = list(outs[2:2 + n])
        self.token = outs[-1]
        self.waited = 0

    def _copies(self, refs, send_sems, recv_sems, lo=0, hi=None):
        src_refs, land_refs = refs[:self.n_src], refs[self.n_src:self.n_src + self.n_land]
        planned = list(enumerate(self.plan(src_refs, land_refs)))[lo:hi]
        return [pltpu.make_async_remote_copy(src_ref=s, dst_ref=d, send_sem=send_sems.at[k], recv_sem=recv_sems.at[k],
                                             device_id=dev, device_id_type=MESH) for k, (s, d, dev) in planned]

    def wait(self, after, count=None, through=None):
        n = self.n_src + self.n_land
        lo = self.waited
        hi = self.n_copies if count is None else lo + count
        self.waited = hi
        through = list(range(n)) if through is None else list(through)
        m = len(through)

        def body(*refs):
            held = [_Absent()] * n
            for pos, ref in zip(through, refs[:m]):
                held[pos] = ref
            for cp in self._copies(held, refs[m], refs[m + 1], lo, hi):
                cp.wait_send()
                cp.wait_recv()

        outs = pl.pallas_call(
            body, name=f"{self.name}_wait{lo}" if lo or hi < self.n_copies else self.name + "_wait",
            out_shape=tuple(pltpu.HBM(self.buffers[p].shape, self.buffers[p].dtype) for p in through),
            in_specs=[_HBM_SPEC] * m + [_SEM_SPEC, _SEM_SPEC, pl.BlockSpec(memory_space=pl.ANY)],
            out_specs=tuple([_HBM_SPEC] * m),
            input_output_aliases={i: i for i in range(m)},
            compiler_params=pltpu.CompilerParams(has_side_effects=_EFFECT),
        )(*[self.buffers[p] for p in through], self.send_sems, self.recv_sems, after)
        for pos, out in zip(through, outs):
            self.buffers[pos] = out
        return list(self.buffers[:self.n_src]), list(self.buffers[self.n_src:])


def _pair_plan(src_index):
    def plan(srcs, lands):
        x, y, c = _coords()
        chip = 2 * x + y
        return [(srcs[0].at[src_index(j, chip, c)], lands[0].at[j], (x, y, 1 - c)) for j in range(4)]
    return plan


def _chips_plan(srcs, lands):
    x, y, c = _coords()
    return [(srcs[0].at[j - 1], lands[0].at[j - 1], (_flip(x, j & 2), _flip(y, j & 1), c)) for j in range(1, 4)]


def _peer(k):
    x, y, c = _coords()
    return _flip(x, k & 4), _flip(y, k & 2), _flip(c, k & 1)


def _send_plan(ks):
    def plan(srcs, lands):
        return [(srcs[0].at[0], lands[0].at[p], _peer(k)) for p, k in enumerate(ks)]
    return plan


def _forward_plan(moves):
    def plan(srcs, lands):
        return [(srcs[0].at[s], srcs[0].at[d], _peer(1)) for s, d in moves]
    return plan


def _gather_plan(srcs, lands):
    x, y, c = _coords()
    me = 4 * x + 2 * y + c
    return [(srcs[a], lands[a].at[me], (_flip(x, k & 4), _flip(y, k & 2), _flip(c, k & 1)))
            for a in range(len(srcs)) for k in range(1, N_DEV)]


def _pair_sum(grad, recv, owners, name):
    _, rows, cols = grad.shape
    tr = _row_tile(rows)

    def body(own_ref, g_ref, r_ref, mine_ref, send_ref):
        j = pl.program_id(1)
        total = g_ref[...] + r_ref[...]

        @pl.when(j == 0)
        def _():
            mine_ref[...] = total

        @pl.when(j > 0)
        def _():
            send_ref[...] = total.astype(BF16)

    grid_spec = pltpu.PrefetchScalarGridSpec(
        num_scalar_prefetch=1, grid=(rows // tr, 4),
        in_specs=[pl.BlockSpec((1, tr, cols), lambda i, j, own: (own[j], i, 0)),
                  pl.BlockSpec((1, tr, cols), lambda i, j, own: (j, i, 0))],
        out_specs=[pl.BlockSpec((1, tr, cols), lambda i, j, own: (0, i, 0)),
                   pl.BlockSpec((1, tr, cols), lambda i, j, own: (jnp.maximum(j - 1, 0), i, 0))])
    return pl.pallas_call(
        body, name=name, grid_spec=grid_spec,
        out_shape=[jax.ShapeDtypeStruct((1, rows, cols), F32), jax.ShapeDtypeStruct((3, rows, cols), BF16)],
        compiler_params=_cparams(dimension_semantics=("arbitrary", "arbitrary")),
    )(owners, grad, recv)


def _modulation(c16, ada_w_sh, ada_b_sh, decay_logit):
    def body(c_ref, w_ref, b_ref, dl_ref, mod_ref, act_ref, lg_ref):
        cv = c_ref[...]
        act = cv * _sigmoid(cv)
        act_ref[...] = act
        mod_ref[...] = jnp.dot(act, w_ref[...], preferred_element_type=F32,
                               precision=lax.Precision.HIGHEST) + b_ref[...]
        z = dl_ref[...]
        lg_ref[...] = jnp.minimum(z, 0.0) - jnp.log(1.0 + jnp.exp(-jnp.abs(z)))

    return pl.pallas_call(
        body, name="modulation",
        out_shape=[jax.ShapeDtypeStruct((16, ADA_SH), F32), jax.ShapeDtypeStruct((16, D), F32),
                   jax.ShapeDtypeStruct(decay_logit.shape, F32)],
        compiler_params=_cparams(),
    )(c16, ada_w_sh, ada_b_sh, decay_logit)


def _adam_update(w, g, m, v):
    m2 = ADAM_B1 * m + (1.0 - ADAM_B1) * g
    v2 = ADAM_B2 * v + (1.0 - ADAM_B2) * (g * g)
    m_hat = m2 / (1.0 - ADAM_B1 ** ADAM_STEP)
    v_hat = v2 / (1.0 - ADAM_B2 ** ADAM_STEP)
    delta = -ADAM_LR * (m_hat / (jnp.sqrt(v_hat) + ADAM_EPS) + ADAM_WD * w)
    return delta, m2, v2


def _adam_sharded(w, m, v, own, recv, name):
    rows, cols = w.shape
    tr = _row_tile(rows)

    def body(w_ref, m_ref, v_ref, g0, g1, g2, g3, g_ref, d_ref, m_out, v_out):
        g = ((g0[0] + g1[0].astype(F32)) + g2[0].astype(F32)) + g3[0].astype(F32)
        delta, m2, v2 = _adam_update(w_ref[...], g, m_ref[...], v_ref[...])
        g_ref[...] = g
        d_ref[...] = delta
        m_out[...] = m2
        v_out[...] = v2

    flat = pl.BlockSpec((tr, cols), lambda i: (i, 0))
    part = lambda j: pl.BlockSpec((1, tr, cols), lambda i, j=j: (j, i, 0))
    return pl.pallas_call(
        body, name=name, grid=(rows // tr,),
        in_specs=[flat, flat, flat, part(0), part(0), part(1), part(2)],
        out_specs=[flat] * 4,
        out_shape=[jax.ShapeDtypeStruct((rows, cols), F32)] * 4,
        compiler_params=_cparams(dimension_semantics=("arbitrary",)),
    )(w, m, v, own, recv, recv, recv)


def _ada_backward(act_t, dmod16, dmodc8, ada_w_sh, m, v):
    def body(at_ref, dm_ref, dc_ref, w_ref, m_ref, v_ref, g_ref, d_ref, m_out, v_out, pc_ref):
        g = jnp.dot(at_ref[...], dm_ref[...], preferred_element_type=F32, precision=lax.Precision.HIGHEST)
        w = w_ref[...]
        delta, m2, v2 = _adam_update(w, g, m_ref[...], v_ref[...])
        g_ref[...] = g
        d_ref[...] = delta
        m_out[...] = m2
        v_out[...] = v2
        pc_ref[...] = lax.dot_general(dc_ref[...], w, (((1,), (1,)), ((), ())), preferred_element_type=F32,
                                      precision=lax.Precision.HIGHEST)

    return pl.pallas_call(
        body, name="ada_backward",
        out_shape=[jax.ShapeDtypeStruct((D, ADA_SH), F32)] * 4 + [jax.ShapeDtypeStruct((8, D), F32)],
        compiler_params=_cparams(),
    )(act_t, dmod16, dmodc8, ada_w_sh, m, v)


def _sum_devices(gathered, name):
    _, rows, cols = gathered.shape

    def body(g_ref, o_ref):
        acc = g_ref[0]
        for d in range(1, N_DEV):
            acc = acc + g_ref[d]
        o_ref[...] = acc

    return pl.pallas_call(body, name=name, out_shape=jax.ShapeDtypeStruct((rows, cols), F32),
                          compiler_params=_cparams())(gathered)


SMALL_PARAMS = ("c_ctx", "norm_w", "conv_b", "gn_w", "final_norm_w", "ada_b", "decay_logit", "conv_w")


def _adam_small(tot, dact_ctx, conv_w_grad, decay_grad, params):
    n = len(SMALL_PARAMS)

    def body(tot_ref, dact_ref, cwg_ref, dlg_ref, *refs):
        ins, outs = refs[:3 * n], refs[3 * n:]
        raw = {
            "c_ctx": dact_ref[0:1, :],
            "norm_w": tot_ref[1:2, :], "conv_b": tot_ref[2:3, :], "gn_w": tot_ref[3:4, :],
            "final_norm_w": tot_ref[0:1, :],
            "ada_b": jnp.concatenate([tot_ref[7 + r:8 + r, :] + tot_ref[10 + r:11 + r, :] for r in range(3)], axis=1),
            "decay_logit": dlg_ref[...],
            "conv_w": cwg_ref[...],
        }
        for k, name in enumerate(SMALL_PARAMS):
            lead = (0,) if len(ins[3 * k].shape) == 3 else ()
            w, m, v = (ref[lead + (...,)] for ref in ins[3 * k:3 * k + 3])
            g = raw[name]
            if name == "c_ctx":
                s = _sigmoid(w)
                g = g * (s * (1.0 + w * (1.0 - s)))
            elif name == "decay_logit":
                g = g * (1.0 - _sigmoid(w))
            delta, m2, v2 = _adam_update(w, g, m, v)
            for ref, val in zip(outs[4 * k:4 * k + 4], (g, delta, m2, v2)):
                ref[lead + (...,)] = val

    flat = [a for trio in params for a in trio]
    out_shape = [jax.ShapeDtypeStruct(trio[0].shape, F32) for trio in params for _ in range(4)]
    outs = pl.pallas_call(body, name="adam_small", out_shape=out_shape, compiler_params=_cparams())(
        tot, dact_ctx, conv_w_grad, decay_grad, *flat)
    return [outs[4 * k:4 * k + 4] for k in range(n)]


def _adam_square(params, own, recv):
    def body(own_ref, recv_ref, *refs):
        ins, outs = refs[:9], refs[9:]
        for k in range(3):
            rows = slice(k * RSH, (k + 1) * RSH)
            g = own_ref[0, rows, :]
            for j in range(3):
                g = g + recv_ref[j, rows, :].astype(F32)
            delta, m2, v2 = _adam_update(ins[3 * k][...], g, ins[3 * k + 1][...], ins[3 * k + 2][...])
            for ref, val in zip(outs[4 * k:4 * k + 4], (g, delta, m2, v2)):
                ref[...] = val

    flat = [a for trio in params for a in trio]
    outs = pl.pallas_call(body, name="adam_square", out_shape=[jax.ShapeDtypeStruct((RSH, D), F32)] * 12,
                          compiler_params=_cparams())(own, recv, *flat)
    return [outs[4 * k:4 * k + 4] for k in range(3)]


def _rope(t, cos, sin):
    lane = lax.broadcasted_iota(jnp.int32, (1, LANES), 1)
    first_half = jnp.bitwise_and(lane, DK // 2) == 0
    partner = jnp.where(first_half, pltpu.roll(t, LANES - DK // 2, 1), pltpu.roll(t, DK // 2, 1))
    return t * cos + partner * sin


class _Shards:
    def __init__(self, arrays, picks):
        self.arrays, self.picks = list(arrays), list(picks)

    def specs(self, index_args):
        out = []
        for a, arr in enumerate(self.arrays):
            steps = [p for p, (ai, _) in enumerate(self.picks) if ai == a]
            first, slot0, last = steps[0], self.picks[steps[0]][1], arr.shape[0] - 1
            out.append(pl.BlockSpec(
                (1, D, WSH), lambda *args, first=first, slot0=slot0, last=last:
                (jnp.clip(index_args(*args) - first + slot0, 0, last), 0, 0)))
        return out

    def __add__(self, other):
        shift = len(self.arrays)
        return _Shards(self.arrays + other.arrays, self.picks + [(a + shift, s) for a, s in other.picks])


def _load_w_in(step, ids_ref, shards, w_refs, w_vmem, sems):
    @pl.when(step == 0)
    def _():
        copies = []
        for p, (a, slot) in enumerate(shards.picks):
            col = pl.multiple_of(ids_ref[p] * WSH, LANES)
            copies.append(pltpu.make_async_copy(w_refs[a].at[slot], w_vmem.at[:, pl.ds(col, WSH)], sems.at[p]))
        for cp in copies:
            cp.start()
        for cp in copies:
            cp.wait()


def _project_columns(xmb, w_ref, shard, cos, sin, p_ref):
    width = 2 * WSH
    q_lo, k_lo, k_hi = CB_QK * D, CB_QK * D + N_HEADS * DK, CB_V * D
    q_pair, k_pair = q_lo // width, k_lo // width
    assert (q_pair + 1) * width == k_lo and k_pair * width == k_lo and k_hi <= (k_pair + 1) * width
    mxu_cols = 2 * LANES

    def columns(rot_lo, rot_hi, scale):
        for a in range(0, width, mxu_cols):
            acc = _dot(xmb, w_ref[:, a:a + mxu_cols])
            for col in range(a, a + mxu_cols, LANES):
                piece = acc[:, col - a:col - a + LANES]
                if rot_lo <= col < rot_hi:
                    piece = _rope(piece if scale == 1.0 else piece * scale, cos, sin)
                p_ref[:, col:col + LANES] = piece.astype(BF16)

    @pl.when(shard == q_pair)
    def _():
        columns(q_lo - q_pair * width, width, 1.0)

    @pl.when(shard == k_pair)
    def _():
        columns(0, k_hi - k_lo, DK ** -0.5)

    @pl.when(jnp.logical_and(shard != q_pair, shard != k_pair))
    def _():
        columns(0, 0, 1.0)


_AFTER_SPEC = pl.BlockSpec(memory_space=pl.ANY)


def _prenorm(x, ctx, modx, modc, norm_w, after):
    n_x_tiles = x.shape[0] // TM
    lext = x.shape[0] + ctx.shape[0]

    def body(x_ref, ctx_ref, mx_ref, mc_ref, nw_ref, after_ref, xm_ref, xmt_ref, cos_ref, sin_ref):
        del after_ref
        i = pl.program_id(0)
        is_ctx = i >= n_x_tiles
        x = jnp.where(is_ctx, ctx_ref[...], x_ref[...])
        r = lax.rsqrt(jnp.mean(x * x, axis=-1, keepdims=True) + EPS)
        sh = jnp.where(is_ctx, mc_ref[0:1, :], mx_ref[0:1, :])
        sc = jnp.where(is_ctx, mc_ref[1:2, :], mx_ref[1:2, :])
        xm = (x * r * nw_ref[...]) * (1.0 + sc) + sh
        xm_ref[...] = xm.astype(BF16)
        xmt_ref[...] = xm.T.astype(BF16)
        pos = i * TM + lax.broadcasted_iota(jnp.int32, (TM, 1), 0)
        lane = lax.broadcasted_iota(jnp.int32, (1, LANES), 1)
        n_freq = DK // 4
        inv = jnp.exp((lane % n_freq).astype(F32) * (-math.log(ROPE_BASE) / n_freq))
        by_column = (lane % (2 * n_freq)) >= n_freq
        ang = jnp.where(by_column, (pos % GRID_W).astype(F32), (pos // GRID_W).astype(F32)) * inv
        first_half = (lane % DK) < DK // 2
        cos_ref[...] = jnp.where(is_ctx, 1.0, jnp.cos(ang))
        sin_ref[...] = jnp.where(is_ctx, 0.0, jnp.where(first_half, -jnp.sin(ang), jnp.sin(ang)))

    full = lambda a: pl.BlockSpec(a.shape, lambda i: (0,) * a.ndim)
    row = lambda width: pl.BlockSpec((TM, width), lambda i: (i, 0))
    return pl.pallas_call(
        body, name="prenorm", grid=(lext // TM,),
        in_specs=[pl.BlockSpec((TM, D), lambda i: (jnp.minimum(i, n_x_tiles - 1), 0)), full(ctx),
                  full(modx), full(modc), full(norm_w), _AFTER_SPEC],
        out_specs=[row(D), pl.BlockSpec((D, TM), lambda i: (0, i)), row(LANES), row(LANES)],
        out_shape=[jax.ShapeDtypeStruct((lext, D), BF16), jax.ShapeDtypeStruct((D, lext), BF16),
                   jax.ShapeDtypeStruct((lext, LANES), F32), jax.ShapeDtypeStruct((lext, LANES), F32)],
        compiler_params=_cparams(dimension_semantics=("arbitrary",)),
    )(x, ctx, modx, modc, norm_w, after)


def _in_projection(p_ext, xm, cos_t, sin_t, w, pairs, me, name):
    lext = xm.shape[0]
    tr = lext // 4
    meta = []
    for (slot_a, k_a), (slot_b, k_b) in pairs:
        assert k_a ^ k_b == 1
        dev_a = jnp.bitwise_xor(me, k_a)
        a_low = dev_a % 2 == 0
        meta += [jnp.where(a_low, slot_a, slot_b), jnp.where(a_low, slot_b, slot_a), dev_a // 2]
    meta = jnp.stack(meta).astype(jnp.int32)

    def body(meta_ref, xm_ref, cos_ref, sin_ref, w_hbm, *rest):
        p_ref, w_ref, sems = rest[-3:]
        j, i = pl.program_id(0), pl.program_id(1)

        @pl.when(i == 0)
        def _():
            copies = [pltpu.make_async_copy(w_hbm.at[meta_ref[3 * j + h]], w_ref.at[:, h * WSH:(h + 1) * WSH],
                                            sems.at[h]) for h in range(2)]
            for cp in copies:
                cp.start()
            for cp in copies:
                cp.wait()

        _project_columns(xm_ref[...], w_ref, meta_ref[3 * j + 2], cos_ref[...], sin_ref[...], p_ref)

    row = lambda width: pl.BlockSpec((tr, width), lambda j, i, meta_: (i, 0))
    grid_spec = pltpu.PrefetchScalarGridSpec(
        num_scalar_prefetch=1, grid=(len(pairs), lext // tr),
        in_specs=[row(D), row(LANES), row(LANES), pl.BlockSpec(memory_space=pl.ANY)]
        + ([] if p_ext is None else [pl.BlockSpec(memory_space=pl.ANY)]),
        out_specs=pl.BlockSpec((tr, 2 * WSH), lambda j, i, meta_: (i, meta_[3 * j + 2])),
        scratch_shapes=[pltpu.VMEM((D, 2 * WSH), BF16), pltpu.SemaphoreType.DMA((2,))])
    return pl.pallas_call(
        body, name=name, grid_spec=grid_spec,
        out_shape=jax.ShapeDtypeStruct((lext, PW), BF16),
        input_output_aliases={} if p_ext is None else {5: 0},
        compiler_params=_cparams(dimension_semantics=("arbitrary", "arbitrary")),
    )(meta, xm, cos_t, sin_t, w, *([] if p_ext is None else [p_ext]))


def _decay_tables(lgf, lgb, n):
    i = lax.broadcasted_iota(jnp.int32, (n, 1), 0).astype(F32)
    return dict(i=i, k_f=jnp.exp(lgf * (n - 1.0 - i)), k_b=jnp.exp(lgb * i),
                q_f=jnp.exp(lgf * (i + 1.0)), q_b=jnp.exp(lgb * (n - i)))


def _decay_matrix(lgf, lgb, n, transposed=False):
    ii = lax.broadcasted_iota(jnp.int32, (n, n), 0)
    jj = lax.broadcasted_iota(jnp.int32, (n, n), 1)
    diff = (jj - ii if transposed else ii - jj).astype(F32)
    low = jnp.exp(lgf * jnp.maximum(diff, 0.0))
    up = jnp.exp(lgb * jnp.maximum(-diff, 0.0))
    return jnp.where(diff > 0, low, jnp.where(diff < 0, up, 2.0)), diff


def _cat_lanes(a, b):
    return jnp.concatenate([a.astype(BF16), b.astype(BF16)], axis=1)


def _retention_forward(p_ext, lg, seq, ctx_len):
    lext = seq + ctx_len
    n_chunks = seq // RET_C
    C = RET_C

    def body(lg_ref, q_ref, k_ref, v_ref, rn_ref, rstd_ref, kv_scr, sf_scr, sb_scr):
        pair = pl.program_id(0)
        lane = lax.broadcasted_iota(jnp.int32, (1, LANES), 1)
        heads = range(2)
        hmask = [(lane // DK == hh).astype(F32) for hh in heads]
        lgf = [lg_ref[0, 2 * pair + hh] for hh in heads]
        lgb = [lg_ref[1, 2 * pair + hh] for hh in heads]
        vs = [slice(hh * DV, (hh + 1) * DV) for hh in heads]
        t = [_decay_tables(lgf[hh], lgb[hh], C) for hh in heads]
        kc_all = k_ref[seq:lext, :].astype(F32)
        s0 = []
        for hh in heads:
            tc = _decay_tables(lgf[hh], lgb[hh], ctx_len)
            kc = kc_all * hmask[hh]
            s0.append(_dot_tn(_cat_lanes(kc * tc["k_f"], kc * tc["k_b"]), v_ref[seq:lext, vs[hh]]))

        def increments(c, carry):
            rows = pl.ds(pl.multiple_of(c * C, C), C)
            k_all = k_ref[rows, :].astype(F32)
            for hh in heads:
                k = k_all * hmask[hh]
                kv_scr[hh, c] = _dot_tn(_cat_lanes(k * t[hh]["k_f"], k * t[hh]["k_b"]), v_ref[rows, vs[hh]])
            return carry

        lax.fori_loop(0, n_chunks, increments, 0, unroll=2)
        gf_c = [jnp.exp(lgf[hh] * C) for hh in heads]
        gb_c = [jnp.exp(lgb[hh] * C) for hh in heads]

        def scan_f(c, s):
            for hh in heads:
                sf_scr[hh, c] = s[hh]
            return tuple(gf_c[hh] * s[hh] + kv_scr[hh, c, 0:LANES, :] for hh in heads)

        def scan_b(n, s):
            c = n_chunks - 1 - n
            for hh in heads:
                sb_scr[hh, c] = s[hh]
            return tuple(gb_c[hh] * s[hh] + kv_scr[hh, c, LANES:2 * LANES, :] for hh in heads)

        lax.fori_loop(0, n_chunks, scan_f, tuple(s0[hh][0:LANES] for hh in heads))
        lax.fori_loop(0, n_chunks, scan_b, tuple(s0[hh][LANES:2 * LANES] for hh in heads))
        dmat = [_decay_matrix(lgf[hh], lgb[hh], C)[0] for hh in heads]

        def outputs(c, carry):
            rows = pl.ds(pl.multiple_of(c * C, C), C)
            q_all = q_ref[rows, :].astype(F32)
            k = k_ref[rows, :]
            for hh in heads:
                q = q_all * hmask[hh]
                v = v_ref[rows, vs[hh]]
                s = _dot_nt(q.astype(BF16), k)
                inner = _dot((s * dmat[hh]).astype(BF16), v)
                states = jnp.concatenate([sf_scr[hh, c], sb_scr[hh, c]], axis=0).astype(BF16)
                ret = inner + _dot(_cat_lanes(q * t[hh]["q_f"], q * t[hh]["q_b"]), states)
                mu = jnp.mean(ret, axis=-1, keepdims=True)
                cen = ret - mu
                rstd = lax.rsqrt(jnp.mean(cen * cen, axis=-1, keepdims=True) + EPS)
                rn_ref[rows, vs[hh]] = cen * rstd
                rstd_ref[rows, vs[hh]] = jnp.broadcast_to(rstd, (C, DV))
            return carry

        lax.fori_loop(0, n_chunks, outputs, 0, unroll=2)

    qk0 = CB_QK * D // LANES
    return pl.pallas_call(
        body, name="retention_forward", grid=(N_HEADS // 2,),
        in_specs=[pl.BlockSpec(memory_space=pltpu.SMEM),
                  pl.BlockSpec((lext, LANES), lambda g: (0, qk0 + g)),
                  pl.BlockSpec((lext, LANES), lambda g: (0, qk0 + N_HEADS // 2 + g)),
                  pl.BlockSpec((lext, 2 * DV), lambda g: (0, CB_V * D // (2 * DV) + g))],
        out_specs=[pl.BlockSpec((seq, 2 * DV), lambda g: (0, g))] * 2,
        out_shape=[jax.ShapeDtypeStruct((seq, D), F32)] * 2,
        scratch_shapes=[pltpu.VMEM((2, n_chunks, 2 * LANES, DV), F32), pltpu.VMEM((2, n_chunks, LANES, DV), F32),
                        pltpu.VMEM((2, n_chunks, LANES, DV), F32)],
        compiler_params=_cparams(dimension_semantics=("arbitrary",)),
    )(lg, p_ext, p_ext, p_ext)


def _retention_backward(p_ext, dret, lg, cos_t, sin_t, seq, ctx_len, after):
    lext = seq + ctx_len
    n_chunks = seq // RET_C
    C = RET_C

    def body(lg_ref, q_ref, k_ref, v_ref, do_ref, cos_ref, sin_ref, after_ref, dq_ref, dk_ref, dv_ref, dlg_ref,
             kv_scr, g_scr, sf_scr, sb_scr, gfn_scr, gbp_scr):
        pair = pl.program_id(0)
        lane = lax.broadcasted_iota(jnp.int32, (1, LANES), 1)
        heads = range(2)
        hmask = [(lane // DK == hh).astype(F32) for hh in heads]
        lgf = [lg_ref[0, 2 * pair + hh] for hh in heads]
        lgb = [lg_ref[1, 2 * pair + hh] for hh in heads]
        vs = [slice(hh * DV, (hh + 1) * DV) for hh in heads]
        t = [_decay_tables(lgf[hh], lgb[hh], C) for hh in heads]
        tc = [_decay_tables(lgf[hh], lgb[hh], ctx_len) for hh in heads]
        kc_all = k_ref[seq:lext, :].astype(F32)
        kc = [kc_all * hmask[hh] for hh in heads]
        vc = [v_ref[seq:lext, vs[hh]] for hh in heads]
        kc_cat = [_cat_lanes(kc[hh] * tc[hh]["k_f"], kc[hh] * tc[hh]["k_b"]) for hh in heads]
        s0 = [_dot_tn(kc_cat[hh], vc[hh]) for hh in heads]

        def increments(c, carry):
            rows = pl.ds(pl.multiple_of(c * C, C), C)
            k_all = k_ref[rows, :].astype(F32)
            q_all = q_ref[rows, :].astype(F32)
            for hh in heads:
                k, q = k_all * hmask[hh], q_all * hmask[hh]
                kv_scr[hh, c] = _dot_tn(_cat_lanes(k * t[hh]["k_f"], k * t[hh]["k_b"]), v_ref[rows, vs[hh]])
                g_scr[hh, c] = _dot_tn(_cat_lanes(q * t[hh]["q_f"], q * t[hh]["q_b"]), do_ref[rows, vs[hh]])
            return carry

        lax.fori_loop(0, n_chunks, increments, 0, unroll=2)
        gf_c = [jnp.exp(lgf[hh] * C) for hh in heads]
        gb_c = [jnp.exp(lgb[hh] * C) for hh in heads]

        def scan_f(c, s):
            for hh in heads:
                sf_scr[hh, c] = s[hh]
            return tuple(gf_c[hh] * s[hh] + kv_scr[hh, c, 0:LANES, :] for hh in heads)

        def scan_b(n, s):
            c = n_chunks - 1 - n
            for hh in heads:
                sb_scr[hh, c] = s[hh]
            return tuple(gb_c[hh] * s[hh] + kv_scr[hh, c, LANES:2 * LANES, :] for hh in heads)

        def scan_gf(n, carry):
            c = n_chunks - 1 - n
            for hh in heads:
                gfn_scr[hh, c] = carry[hh]
            return tuple(g_scr[hh, c, 0:LANES, :] + gf_c[hh] * carry[hh] for hh in heads)

        def scan_gb(c, carry):
            for hh in heads:
                gbp_scr[hh, c] = carry[hh]
            return tuple(g_scr[hh, c, LANES:2 * LANES, :] + gb_c[hh] * carry[hh] for hh in heads)

        lax.fori_loop(0, n_chunks, scan_f, tuple(s0[hh][0:LANES] for hh in heads))
        lax.fori_loop(0, n_chunks, scan_b, tuple(s0[hh][LANES:2 * LANES] for hh in heads))
        zero_state = jnp.zeros((LANES, DV), F32)
        gf_first = lax.fori_loop(0, n_chunks, scan_gf, (zero_state, zero_state))
        gb_last = lax.fori_loop(0, n_chunks, scan_gb, (zero_state, zero_state))

        dmat, w_f, w_b = [], [], []
        for hh in heads:
            dm, diff = _decay_matrix(lgf[hh], lgb[hh], C)
            dmat.append(dm)
            w_f.append(jnp.where(diff > 0, diff * dm, 0.0))
            w_b.append(jnp.where(diff < 0, -diff * dm, 0.0))

        def total(a):
            rows_, width = a.shape
            part = jnp.sum(a.reshape(rows_ // 8, 8, width), axis=0)
            return part[:, 0:LANES] + part[:, LANES:2 * LANES] if width == 2 * LANES else part

        def chunk_head(hh, c, rows, q_all, k_all, dlgf, dlgb):
            th = t[hh]
            qm = q_all * hmask[hh]
            km = k_all * hmask[hh]
            qb, kb = qm.astype(BF16), km.astype(BF16)
            v = v_ref[rows, vs[hh]]
            do = do_ref[rows, vs[hh]]
            s = _dot_nt(qb, kb)
            dsd = _dot_nt(do, v)
            ds = (dsd * dmat[hh]).astype(BF16)
            dq_in = _dot(ds, kb)
            dk_in = _dot_tn(ds, qb)
            dv_in = _dot_tn((s * dmat[hh]).astype(BF16), do)
            prod = s * dsd
            dlgf = dlgf + total(prod * w_f[hh])
            dlgb = dlgb + total(prod * w_b[hh])
            sf, sb = sf_scr[hh, c], sb_scr[hh, c]
            states = jnp.concatenate([sf, sb], axis=0).astype(BF16)
            dqc = _dot_nt(do, states)
            dqf = dqc[:, 0:LANES] * th["q_f"]
            dqb = dqc[:, LANES:2 * LANES] * th["q_b"]
            dq = (dq_in + dqf + dqb) * hmask[hh]
            dlgf = dlgf + total((th["i"] + 1.0) * (qm * dqf))
            dlgb = dlgb + total((C - th["i"]) * (qm * dqb))
            gfn, gbp = gfn_scr[hh, c], gbp_scr[hh, c]
            gstates = jnp.concatenate([gfn, gbp], axis=0).astype(BF16)
            dkc = _dot_nt(v, gstates)
            dkf = dkc[:, 0:LANES] * th["k_f"]
            dkb = dkc[:, LANES:2 * LANES] * th["k_b"]
            dk = dk_in + dkf + dkb
            dlgf = dlgf + total((C - 1.0 - th["i"]) * (km * dkf)) + C * gf_c[hh] * total(gfn * sf)
            dlgb = dlgb + total(th["i"] * (km * dkb)) + C * gb_c[hh] * total(gbp * sb)
            dv = dv_in + _dot(_cat_lanes(km * th["k_f"], km * th["k_b"]), gstates)
            dv_ref[rows, vs[hh]] = dv.astype(BF16)
            return dq, dk, dlgf, dlgb

        def chunk(c, carry):
            rows = pl.ds(pl.multiple_of(c * C, C), C)
            q_all = q_ref[rows, :].astype(F32)
            k_all = k_ref[rows, :].astype(F32)
            dq0, dk0, f0, b0 = chunk_head(0, c, rows, q_all, k_all, carry[0], carry[1])
            dq1, dk1, f1, b1 = chunk_head(1, c, rows, q_all, k_all, carry[2], carry[3])
            cos, sin = cos_ref[rows, :], sin_ref[rows, :]
            dq_ref[rows, :] = _rope(dq0 + dq1, cos, -sin).astype(BF16)
            dk_ref[rows, :] = (_rope(dk0 + dk1, cos, -sin) * (DK ** -0.5)).astype(BF16)
            return f0, b0, f1, b1

        zero = jnp.zeros((8, LANES), F32)
        sums = lax.fori_loop(0, n_chunks, chunk, (zero, zero, zero, zero), unroll=2)

        dlg = []
        dk_ctx = jnp.zeros((ctx_len, LANES), F32)
        for hh in heads:
            g0 = jnp.concatenate([gf_first[hh], gb_last[hh]], axis=0).astype(BF16)
            dkcc = _dot_nt(vc[hh], g0)
            dkcf = dkcc[:, 0:LANES] * tc[hh]["k_f"]
            dkcb = dkcc[:, LANES:2 * LANES] * tc[hh]["k_b"]
            dlgf = sums[2 * hh] + total((ctx_len - 1.0 - tc[hh]["i"]) * (kc[hh] * dkcf))
            dlgb = sums[2 * hh + 1] + total(tc[hh]["i"] * (kc[hh] * dkcb))
            dk_ctx = dk_ctx + (dkcf + dkcb) * (DK ** -0.5)
            dv_ref[seq:lext, vs[hh]] = _dot(kc_cat[hh], g0).astype(BF16)
            dlg += [jnp.sum(jnp.sum(a, axis=1, keepdims=True), axis=0, keepdims=True) for a in (dlgf, dlgb)]
        dk_ref[seq:lext, :] = dk_ctx.astype(BF16)
        dq_ref[seq:lext, :] = jnp.zeros((ctx_len, LANES), BF16)

        lane8 = lax.broadcasted_iota(jnp.int32, (8, LANES), 1)
        out = jnp.zeros((8, LANES), F32)
        for n, val in enumerate(dlg):
            out = jnp.where(lane8 == n, val, out)
        dlg_ref[0] = out

    qk0 = CB_QK * D // LANES
    q_spec = pl.BlockSpec((lext, LANES), lambda g: (0, qk0 + g))
    k_spec = pl.BlockSpec((lext, LANES), lambda g: (0, qk0 + N_HEADS // 2 + g))
    v_spec = pl.BlockSpec((lext, 2 * DV), lambda g: (0, CB_V * D // (2 * DV) + g))
    table = pl.BlockSpec((lext, LANES), lambda g: (0, 0))
    state = pltpu.VMEM((2, n_chunks, LANES, DV), F32)
    return pl.pallas_call(
        body, name="retention_backward", grid=(N_HEADS // 2,),
        in_specs=[pl.BlockSpec(memory_space=pltpu.SMEM), q_spec, k_spec, v_spec,
                  pl.BlockSpec((seq, 2 * DV), lambda g: (0, g)), table, table, _AFTER_SPEC],
        out_specs=[pl.BlockSpec((lext, LANES), lambda g: (0, g)), pl.BlockSpec((lext, LANES), lambda g: (0, g)),
                   pl.BlockSpec((lext, 2 * DV), lambda g: (0, g)), pl.BlockSpec((1, 8, LANES), lambda g: (g, 0, 0))],
        out_shape=[jax.ShapeDtypeStruct((lext, N_HEADS * DK), BF16), jax.ShapeDtypeStruct((lext, N_HEADS * DK), BF16),
                   jax.ShapeDtypeStruct((lext, D), BF16), jax.ShapeDtypeStruct((N_HEADS // 2, 8, LANES), F32)],
        scratch_shapes=[pltpu.VMEM((2, n_chunks, 2 * LANES, DV), F32), pltpu.VMEM((2, n_chunks, 2 * LANES, DV), F32),
                        state, state, state, state],
        compiler_params=_cparams(dimension_semantics=("arbitrary",)),
    )(lg, p_ext, p_ext, p_ext, dret, cos_t, sin_t, after)


class _ColumnWriter:
    def __init__(self, dst_hbm, stage, sems, pieces, n_steps):
        self.dst, self.stage, self.sems, self.pieces, self.n_steps = dst_hbm, stage, sems, pieces, n_steps

    def _copies(self, slot, tile):
        rows = pl.ds(pl.multiple_of(tile * TM, TM), TM)
        return [pltpu.make_async_copy(self.stage.at[slot, :, pl.ds(src, width)], self.dst.at[rows, pl.ds(dst, width)],
                                      self.sems.at[slot, k]) for k, (src, dst, width) in enumerate(self.pieces)]

    def slot_for(self, step):
        slot = step % 2

        @pl.when(step >= 2)
        def _():
            for cp in self._copies(slot, step - 2):
                cp.wait()

        return slot

    def send(self, step, slot):
        for cp in self._copies(slot, step):
            cp.start()

        @pl.when(step == self.n_steps - 1)
        def _():
            if self.n_steps >= 2:
                for cp in self._copies(1 - slot, step - 1):
                    cp.wait()
            for cp in self._copies(slot, step):
                cp.wait()


def _merge(p_ext, rn, rstd, x, target, w_a, w_b, w_out, vecs, seq):
    n_tiles = seq // TM
    hb = TM // HALO
    dp_pieces = [(0, CB_BG * D, D), (D, CB_ZA * D, D), (2 * D, CB_ZB * D, 3 * D)]

    def body(h_ref, bg_ref, cg_ref, za_ref, zb_ref, ga_ref, gb_ref, hp_ref, hn_ref, cp_ref, cn_ref,
             rn_ref, rstd_ref, x_ref, t_ref, wa_ref, wb_ref, wo_ref, vec_ref,
             dx1_ref, dp_hbm, dconv_ref, dret_ref, at_ref, b_ref, part_ref, stage, dp_sems):
        i = pl.program_id(0)
        writer = _ColumnWriter(dp_hbm, stage, dp_sems, dp_pieces, n_tiles)
        slot = writer.slot_for(i)
        dpa_ref = stage.at[slot]
        f = lambda ref: ref[...].astype(F32)
        h, bg, cg, za, zb, ga, gb = f(h_ref), f(bg_ref), f(cg_ref), f(za_ref), f(zb_ref), f(ga_ref), f(gb_ref)
        gx, w0, w1, w2 = vec_ref[0:1, :], vec_ref[1:2, :], vec_ref[2:3, :], vec_ref[3:4, :]
        cb, gnw, fw = vec_ref[4:5, :], vec_ref[5:6, :], vec_ref[6:7, :]
        u = cg * h
        row = lax.broadcasted_iota(jnp.int32, (TM, 1), 0)
        u_prev = (f(cp_ref) * f(hp_ref))[HALO - 1:HALO, :]
        u_next = (f(cn_ref) * f(hn_ref))[0:1, :]
        u_prev = jnp.where(i == 0, 0.0, u_prev)
        u_next = jnp.where(i == n_tiles - 1, 0.0, u_next)
        u_up = jnp.where(row == 0, u_prev, pltpu.roll(u, 1, 0))
        u_dn = jnp.where(row == TM - 1, u_next, pltpu.roll(u, TM - 1, 0))
        conv = w0 * u_up + w1 * u + w2 * u_dn + cb
        sza = _sigmoid(za)
        silu_za = za * sza
        a_act = silu_za * bg * conv
        rn = rn_ref[...]
        szb = _sigmoid(zb)
        silu_zb = zb * szb
        rg = rn * gnw
        b_act = silu_zb * rg
        y_a = _dot(a_act.astype(BF16), wa_ref[...])
        y_b = _dot(b_act.astype(BF16), wb_ref[...])
        sga, sgb = _sigmoid(ga), _sigmoid(gb)
        mix = sga * y_a + sgb * y_b
        y = _dot(mix.astype(BF16), wo_ref[...])
        x1 = x_ref[...] + gx * y
        r1 = lax.rsqrt(jnp.mean(x1 * x1, axis=-1, keepdims=True) + EPS)
        xh1 = x1 * r1
        err = xh1 * fw - t_ref[...]
        loss = jnp.sum(jnp.sum(err * err, axis=1, keepdims=True), axis=0, keepdims=True) * (0.5 / D)
        dout = err * (1.0 / D)
        dxh = dout * fw
        dx1 = r1 * (dxh - xh1 * jnp.mean(dxh * xh1, axis=-1, keepdims=True))
        dx1_ref[...] = dx1
        dy = (dx1 * gx).astype(BF16)
        dmix = _dot_nt(dy, wo_ref[...])
        dya = (dmix * sga).astype(BF16)
        dyb = (dmix * sgb).astype(BF16)
        da = _dot_nt(dya, wa_ref[...])
        db = _dot_nt(dyb, wb_ref[...])
        dpa_ref[:, 0:D] = (da * silu_za * conv).astype(BF16)
        dpa_ref[:, D:2 * D] = (da * bg * conv * (sza * (1.0 + za * (1.0 - sza)))).astype(BF16)
        dpa_ref[:, 2 * D:3 * D] = (db * rg * (szb * (1.0 + zb * (1.0 - szb)))).astype(BF16)
        dpa_ref[:, 3 * D:4 * D] = (dmix * y_a * sga * (1.0 - sga)).astype(BF16)
        dpa_ref[:, 4 * D:5 * D] = (dmix * y_b * sgb * (1.0 - sgb)).astype(BF16)
        dconv_ref[...] = (da * silu_za * bg).astype(BF16)
        drn_n = db * silu_zb
        drn = drn_n * gnw
        rstd = rstd_ref[...]
        for hd in range(N_HEADS):
            sl = slice(hd * DV, (hd + 1) * DV)
            dh_, rh = drn[:, sl], rn[:, sl]
            m1 = jnp.mean(dh_, axis=-1, keepdims=True)
            m2 = jnp.mean(dh_ * rh, axis=-1, keepdims=True)
            dret_ref[:, sl] = (rstd[:, sl] * (dh_ - m1 - rh * m2)).astype(BF16)
        at_ref[0] = a_act.T.astype(BF16)
        at_ref[1] = b_act.T.astype(BF16)
        at_ref[2] = mix.T.astype(BF16)
        b_ref[0] = dya
        b_ref[1] = dyb
        b_ref[2] = dy

        @pl.when(i == 0)
        def _():
            part_ref[...] = jnp.zeros((8, D), F32)

        part_ref[0:1, :] += jnp.sum(dout * xh1, axis=0, keepdims=True)
        part_ref[1:2, :] += jnp.sum(dx1 * y, axis=0, keepdims=True)
        part_ref[2:3, :] += jnp.sum(drn_n * rn, axis=0, keepdims=True)
        part_ref[3:4, :] += jnp.broadcast_to(loss, (1, D))
        writer.send(i, slot)

    col = lambda cb_: pl.BlockSpec((TM, D), lambda i, cb_=cb_: (i, cb_))
    prev = lambda cb_: pl.BlockSpec((HALO, D), lambda i, cb_=cb_: (jnp.maximum(i * hb - 1, 0), cb_))
    nxt = lambda cb_: pl.BlockSpec((HALO, D), lambda i, cb_=cb_: (jnp.minimum((i + 1) * hb, n_tiles * hb - 1), cb_))
    tile = pl.BlockSpec((TM, D), lambda i: (i, 0))
    full = lambda a: pl.BlockSpec(a.shape, lambda i: (0,) * a.ndim, pipeline_mode=pl.Buffered(1))
    return pl.pallas_call(
        body, name="merge", grid=(n_tiles,),
        in_specs=[col(CB_H), col(CB_BG), col(CB_CG), col(CB_ZA), col(CB_ZB), col(CB_GA), col(CB_GB),
                  prev(CB_H), nxt(CB_H), prev(CB_CG), nxt(CB_CG),
                  tile, tile, tile, tile, full(w_a), full(w_b), full(w_out), full(vecs)],
        out_specs=[tile, pl.BlockSpec(memory_space=pl.ANY), tile, tile,
                   pl.BlockSpec((3, D, TM), lambda i: (0, 0, i)), pl.BlockSpec((3, TM, D), lambda i: (0, i, 0)),
                   pl.BlockSpec((8, D), lambda i: (0, 0))],
        out_shape=[jax.ShapeDtypeStruct((seq, D), F32), jax.ShapeDtypeStruct((p_ext.shape[0], PW), BF16),
                   jax.ShapeDtypeStruct((seq, D), BF16), jax.ShapeDtypeStruct((seq, D), BF16),
                   jax.ShapeDtypeStruct((3, D, seq), BF16), jax.ShapeDtypeStruct((3, seq, D), BF16),
                   jax.ShapeDtypeStruct((8, D), F32)],
        scratch_shapes=[pltpu.VMEM((2, TM, 5 * D), BF16), pltpu.SemaphoreType.DMA((2, len(dp_pieces)))],
        compiler_params=_cparams(dimension_semantics=("arbitrary",)),
    )(p_ext, p_ext, p_ext, p_ext, p_ext, p_ext, p_ext, p_ext, p_ext, p_ext, p_ext,
      rn, rstd, x, target, w_a, w_b, w_out, vecs)


def _conv_backward(p_ext, dconv, dp, dq, dk, dv, vecs, seq, after):
    n_tiles = seq // TM
    hb = TM // HALO
    qk = N_HEADS * DK
    pieces = [(0, CB_H * D, D), (D, CB_CG * D, D), (2 * D, CB_QK * D, 2 * D)]
    zero_pieces = [(CB_BG * D, D), (CB_ZA * D, D), (CB_ZB * D, 3 * D)]

    def body(h_ref, cg_ref, dc_ref, dcp_ref, dcn_ref, dq_ref, dk_ref, dv_ref, vec_ref, dp_in, after_ref, dp_hbm, part_ref,
             stage, sems, zeros, zero_sems):
        del dp_in, after_ref
        i = pl.program_id(0)
        writer = _ColumnWriter(dp_hbm, stage, sems, pieces, n_tiles + 1)
        slot = writer.slot_for(i)
        out = stage.at[slot]
        out[:, 2 * D:2 * D + qk] = dq_ref[...]
        out[:, 2 * D + qk:3 * D] = dk_ref[...]
        out[:, 3 * D:4 * D] = dv_ref[...]

        @pl.when(i == 0)
        def _():
            part_ref[...] = jnp.zeros((8, D), F32)

        @pl.when(i == n_tiles)
        def _():
            out[:, 0:2 * D] = jnp.zeros((TM, 2 * D), BF16)
            zeros[...] = jnp.zeros(zeros.shape, BF16)
            rows = pl.ds(n_tiles * TM, TM)
            fills = [pltpu.make_async_copy(zeros.at[:, pl.ds(0, width)], dp_hbm.at[rows, pl.ds(dst, width)],
                                           zero_sems.at[k]) for k, (dst, width) in enumerate(zero_pieces)]
            for cp in fills:
                cp.start()
            for cp in fills:
                cp.wait()

        @pl.when(i < n_tiles)
        def _():
            f = lambda ref: ref[...].astype(F32)
            h, cg, dc = f(h_ref), f(cg_ref), f(dc_ref)
            w0, w1, w2 = vec_ref[1:2, :], vec_ref[2:3, :], vec_ref[3:4, :]
            row = lax.broadcasted_iota(jnp.int32, (TM, 1), 0)
            dc_prev = jnp.where(i == 0, 0.0, f(dcp_ref)[HALO - 1:HALO, :])
            dc_next = jnp.where(i == n_tiles - 1, 0.0, f(dcn_ref)[0:1, :])
            dc_up = jnp.where(row == 0, dc_prev, pltpu.roll(dc, 1, 0))
            dc_dn = jnp.where(row == TM - 1, dc_next, pltpu.roll(dc, TM - 1, 0))
            du = w0 * dc_dn + w1 * dc + w2 * dc_up
            u = cg * h
            part_ref[0:1, :] += jnp.sum(u * dc_dn, axis=0, keepdims=True)
            part_ref[1:2, :] += jnp.sum(u * dc, axis=0, keepdims=True)
            part_ref[2:3, :] += jnp.sum(u * dc_up, axis=0, keepdims=True)
            part_ref[3:4, :] += jnp.sum(dc, axis=0, keepdims=True)
            out[:, 0:D] = (du * cg).astype(BF16)
            out[:, D:2 * D] = (du * h).astype(BF16)

        writer.send(i, slot)

    last = n_tiles - 1
    col = lambda cb_: pl.BlockSpec((TM, D), lambda i, cb_=cb_: (jnp.minimum(i, last), cb_))
    lat = lambda w: pl.BlockSpec((TM, w), lambda i: (jnp.minimum(i, last), 0))
    ext = lambda w: pl.BlockSpec((TM, w), lambda i: (i, 0))
    anyspec = pl.BlockSpec(memory_space=pl.ANY)
    return pl.pallas_call(
        body, name="conv_backward", grid=(n_tiles + 1,),
        in_specs=[col(CB_H), col(CB_CG), lat(D),
                  pl.BlockSpec((HALO, D), lambda i: (jnp.clip(i * hb - 1, 0, n_tiles * hb - 1), 0)),
                  pl.BlockSpec((HALO, D), lambda i: (jnp.minimum((i + 1) * hb, n_tiles * hb - 1), 0)),
                  ext(qk), ext(qk), ext(D), pl.BlockSpec(vecs.shape, lambda i: (0, 0)), anyspec, _AFTER_SPEC],
        out_specs=[anyspec, pl.BlockSpec((8, D), lambda i: (0, 0))],
        out_shape=[jax.ShapeDtypeStruct(dp.shape, BF16), jax.ShapeDtypeStruct((8, D), F32)],
        input_output_aliases={9: 0},
        scratch_shapes=[pltpu.VMEM((2, TM, 4 * D), BF16), pltpu.SemaphoreType.DMA((2, len(pieces))),
                        pltpu.VMEM((TM, 3 * D), BF16), pltpu.SemaphoreType.DMA((len(zero_pieces),))],
        compiler_params=_cparams(dimension_semantics=("arbitrary",)),
    )(p_ext, p_ext, dconv, dconv, dconv, dq, dk, dv, vecs, dp, after)


def _input_backward(dp, shards, ids, x, ctx, dx1, modx, modc, norm_w, after):
    seq = x.shape[0]
    lext = seq + ctx.shape[0]
    n_x = seq // TM
    n_w = len(shards.arrays)

    def body(ids_ref, dp_ref, x_ref, ctx_ref, dx1_ref, mx_ref, mc_ref, nw_ref, *rest):
        w_hbm, (_, gx_ref, part_ref, w_ref, w_sems) = rest[:n_w], rest[n_w:]
        i = pl.program_id(0)
        _load_w_in(i, ids_ref, shards, w_hbm, w_ref, w_sems)
        is_ctx = i >= n_x
        dxm = _dot_nt(dp_ref[...], w_ref[...])
        x = jnp.where(is_ctx, ctx_ref[...], x_ref[...])
        r = lax.rsqrt(jnp.mean(x * x, axis=-1, keepdims=True) + EPS)
        xh = x * r
        nw = nw_ref[...]
        sc = jnp.where(is_ctx, mc_ref[1:2, :], mx_ref[1:2, :])
        dxn = dxm * (1.0 + sc)
        dxh = dxn * nw
        dx = r * (dxh - xh * jnp.mean(dxh * xh, axis=-1, keepdims=True))

        @pl.when(jnp.logical_not(is_ctx))
        def _():
            gx_ref[...] = dx1_ref[...] + dx

        @pl.when(i == 0)
        def _():
            part_ref[...] = jnp.zeros((8, D), F32)

        fx = jnp.where(is_ctx, 0.0, 1.0)
        d_shift = jnp.sum(dxm, axis=0, keepdims=True)
        d_scale = jnp.sum(dxm * (xh * nw), axis=0, keepdims=True)
        part_ref[0:1, :] += fx * d_shift
        part_ref[1:2, :] += fx * d_scale
        part_ref[2:3, :] += jnp.sum(dxn * xh, axis=0, keepdims=True)
        part_ref[3:4, :] += (1.0 - fx) * d_shift
        part_ref[4:5, :] += (1.0 - fx) * d_scale

    lat = lambda w: pl.BlockSpec((TM, w), lambda i, ids_: (jnp.minimum(i, n_x - 1), 0))
    ext = lambda w: pl.BlockSpec((TM, w), lambda i, ids_: (i, 0))
    full = lambda a: pl.BlockSpec(a.shape, lambda i, ids_: (0,) * a.ndim)
    grid_spec = pltpu.PrefetchScalarGridSpec(
        num_scalar_prefetch=1, grid=(lext // TM,),
        in_specs=[ext(PW), lat(D), full(ctx), lat(D), full(modx), full(modc), full(norm_w)]
        + [pl.BlockSpec(memory_space=pl.ANY)] * n_w + [_AFTER_SPEC],
        out_specs=[lat(D), pl.BlockSpec((8, D), lambda i, ids_: (0, 0))],
        scratch_shapes=[pltpu.VMEM((D, PW), BF16), pltpu.SemaphoreType.DMA((N_DEV,))])
    return pl.pallas_call(
        body, name="input_backward", grid_spec=grid_spec,
        out_shape=[jax.ShapeDtypeStruct((seq, D), F32), jax.ShapeDtypeStruct((8, D), F32)],
        compiler_params=_cparams(dimension_semantics=("arbitrary",)),
    )(ids, dp, x, ctx, dx1, modx, modc, norm_w, *shards.arrays, after)


def _weight_grad_in(xm_t, dp, owners, name, after):
    lext = xm_t.shape[1]

    def body(own_ref, a_ref, b_ref, after_ref, o_ref):
        o_ref[0] = _dot(a_ref[...], b_ref[...])

    grid_spec = pltpu.PrefetchScalarGridSpec(
        num_scalar_prefetch=1, grid=(4,),
        in_specs=[pl.BlockSpec((D, lext), lambda j, own: (0, 0)),
                  pl.BlockSpec((lext, WSH), lambda j, own: (0, own[j])), _AFTER_SPEC],
        out_specs=pl.BlockSpec((1, D, WSH), lambda j, own: (j, 0, 0)))
    return pl.pallas_call(
        body, name=name, grid_spec=grid_spec,
        out_shape=jax.ShapeDtypeStruct((4, D, WSH), F32),
        compiler_params=_cparams(dimension_semantics=("arbitrary",)),
    )(owners, xm_t, dp, after)


def _weight_grad_square(a_t, b):
    seq = a_t.shape[2]

    def body(a_ref, b_ref, o_ref):
        o_ref[:, 0] = _dot(a_ref[0], b_ref[0]).reshape(N_DEV, RSH, D)

    return pl.pallas_call(
        body, name="weight_grad_square", grid=(3,),
        in_specs=[pl.BlockSpec((1, D, seq), lambda t: (t, 0, 0)), pl.BlockSpec((1, seq, D), lambda t: (t, 0, 0))],
        out_specs=pl.BlockSpec((N_DEV, 1, RSH, D), lambda t: (0, t, 0, 0)),
        out_shape=jax.ShapeDtypeStruct((N_DEV, 3, RSH, D), F32),
        compiler_params=_cparams(dimension_semantics=("arbitrary",)),
    )(a_t, b)


def _pad_lanes(a):
    return jnp.pad(a, ((0, 0), (0, D - a.shape[1])))


def _rows(total, *parts):
    width = max(a.shape[1] for _, a in parts)
    out = None
    for row, a in parts:
        padded = jnp.pad(a, ((row, total - row - a.shape[0]), (0, width - a.shape[1])))
        out = padded if out is None else out + padded
    return out


def kernel(x, c, ctx, c_ctx, norm_w, ada_w, ada_b, w_in, conv_w, conv_b, decay_logit, gn_w, w_a, w_b, w_out, final_norm_w, loss_target, m_c_ctx, m_norm_w, m_ada_w, m_ada_b, m_w_in, m_conv_w, m_conv_b, m_decay_logit, m_gn_w, m_w_a, m_w_b, m_w_out, m_final_norm_w, v_c_ctx, v_norm_w, v_ada_w, v_ada_b, v_w_in, v_conv_w, v_conv_b, v_decay_logit, v_gn_w, v_w_a, v_w_b, v_w_out, v_final_norm_w):
    xi, yi, ci = _coords()
    me = 4 * xi + 2 * yi + ci
    chip = 2 * xi + yi
    seq, ctx_len = x.shape[1], ctx.shape[1]
    assert seq % TM == 0 and seq % RET_C == 0 and ctx_len == TM and seq % GRID_W == 0
    csh = D // N_DEV

    blk = jnp.pad(c, ((0, 7), (0, 0))) + jnp.pad(conv_w[0], ((1, 4), (0, D - csh)))
    got, _ = _all_gather_small(blk, "gather_cond")
    conv_w_all = got[:, 1:4, 0:csh].transpose(1, 0, 2).reshape(3, D)
    c16 = _rows(16, (0, got[:, 0, :]), (N_DEV, c_ctx[None]))
    ada_b_sh = lax.dynamic_slice(ada_b, (0, me * ADA_SH), (1, ADA_SH))
    mod_sh, act16, lg = _modulation(c16, ada_w[0], ada_b_sh, decay_logit[0])
    mod_all, small_done = _all_gather_small(mod_sh, "gather_mod")
    mod_all = mod_all.transpose(1, 0, 2).reshape(16, 3 * D)
    modx = lax.dynamic_slice(mod_all, (me, 0), (1, 3 * D)).reshape(3, D)
    modc = mod_all[8].reshape(3, D)

    x2, tgt = x[0], loss_target[0]
    wblock = lambda n: jax.ShapeDtypeStruct((n, D, WSH), BF16)
    ids_of = lambda ks: jnp.stack([jnp.bitwise_xor(me, k) for k in ks]).astype(jnp.int32)
    own_shard = w_in[0].astype(BF16)[None]
    sq_shards = [w[0].astype(BF16) for w in (w_a, w_b, w_out)]
    wave1 = _Exchange(
        "gather_in_first", [jnp.broadcast_to(own_shard, (2, D, WSH)), own_shard], [wblock(4)], 3,
        lambda srcs, lands: _forward_plan([(0, 1)])(srcs, lands) + _send_plan((4, 2))(srcs[1:], lands),
        after=small_done)
    xm, xm_t, cos_t, sin_t = _prenorm(x2, ctx[0], modx, modc, norm_w, wave1.token)
    (w_pair, _), _ = wave1.wait(xm_t, 1, through=[0])
    first = _Shards([w_pair], [(0, 0), (0, 1)])
    p_ext = _in_projection(None, xm, cos_t, sin_t, w_pair, [((0, 0), (1, 1))], me, "in_projection_pair")
    _, (w_nbr,) = wave1.wait(p_ext, through=[1, 2])
    wave2 = _Exchange(
        "gather_in_second", [w_nbr, own_shard], [wblock(2)], 3,
        lambda srcs, lands: _forward_plan([(0, 2), (1, 3)])(srcs, lands) + _send_plan((6,))(srcs[1:], lands))
    (w_nbr, _), _ = wave2.wait(wave2.token, 2, through=[0])
    second = _Shards([w_nbr], [(0, 0), (0, 1), (0, 2), (0, 3)])
    p_ext = _in_projection(p_ext, xm, cos_t, sin_t, w_nbr, [((0, 4), (2, 5)), ((1, 2), (3, 3))], me,
                           "in_projection_neighbours")
    _, (w_diag,) = wave2.wait(p_ext, through=[1, 2])
    wave3 = _Exchange(
        "gather_in_third", [w_diag] + sq_shards, [jax.ShapeDtypeStruct((N_DEV, RSH, D), BF16)] * 3, 1 + 3 * (N_DEV - 1),
        lambda srcs, lands: _forward_plan([(0, 1)])(srcs, lands) + _gather_plan(srcs[1:], lands))
    (w_diag, *_), _ = wave3.wait(wave3.token, 1, through=[0])
    third = _Shards([w_diag], [(0, 0), (0, 1)])
    p_ext = _in_projection(p_ext, xm, cos_t, sin_t, w_diag, [((0, 6), (1, 7))], me, "in_projection_diagonal")
    w_shards, w_ids = first + second + third, ids_of((0, 1, 4, 2, 5, 3, 6, 7))

    rn, rstd = _retention_forward(p_ext, lg, seq, ctx_len)
    _, sq_lands = wave3.wait(rstd, through=range(1, 7))
    w_a_all, w_b_all, w_out_all = (
        lax.dynamic_update_slice(land, shard[None], (me, 0, 0)).reshape(D, D) for land, shard in zip(sq_lands, sq_shards))
    vecs = _rows(8, (0, modx[2:3]), (1, conv_w_all), (4, conv_b), (5, gn_w), (6, final_norm_w[None]))
    dx1, dp, dconv, dret, op_at, op_b, part_m = _merge(p_ext, rn, rstd, x2, tgt, w_a_all, w_b_all, w_out_all, vecs, seq)

    j4 = jnp.arange(4, dtype=jnp.int32)
    owners = (2 * jnp.bitwise_xor(chip, j4) + ci).astype(jnp.int32)
    owners_sib = (2 * jnp.bitwise_xor(chip, j4) + (1 - ci)).astype(jnp.int32)
    gw_sq = _weight_grad_square(op_at, op_b).reshape(N_DEV, 3 * RSH, D)
    rs_sq_pair = _Exchange("rs_square_pair", [gw_sq], [jax.ShapeDtypeStruct((4, 3 * RSH, D), F32)], 4,
                           _pair_plan(lambda j, chip_, c_: 2 * jnp.bitwise_xor(chip_, j) + (1 - c_)))
    dq, dk, dv, dlg = _retention_backward(p_ext, dret, lg, cos_t, sin_t, seq, ctx_len, rs_sq_pair.token)
    (gw_sq,), (r1_sq,) = rs_sq_pair.wait(dlg)
    own_sq, send_sq = _pair_sum(gw_sq, r1_sq, owners, "pair_sum_square")
    rs_sq_chips = _Exchange("rs_square_chips", [send_sq], [jax.ShapeDtypeStruct((3, 3 * RSH, D), BF16)], 3, _chips_plan)
    dp, part_c = _conv_backward(p_ext, dconv, dp, dq, dk, dv, vecs, seq, rs_sq_chips.token)
    gw_sib = _weight_grad_in(xm_t, dp, owners_sib, "weight_grad_in_sibling", part_c)
    rs_in_pair = _Exchange("rs_in_pair", [gw_sib], [jax.ShapeDtypeStruct((4, D, WSH), F32)], 4,
                           _pair_plan(lambda j, chip_, c_: j))
    gw_own = _weight_grad_in(xm_t, dp, owners, "weight_grad_in_own", rs_in_pair.token)
    _, (r1_in,) = rs_in_pair.wait(gw_own)
    own_in, send_in = _pair_sum(gw_own, r1_in, j4, "pair_sum_in")
    rs_in_chips = _Exchange("rs_in_chips", [send_in], [jax.ShapeDtypeStruct((3, D, WSH), BF16)], 3, _chips_plan)
    grad_x, part_i = _input_backward(dp, w_shards, w_ids, x2, ctx[0], dx1, modx, modc, norm_w, rs_in_chips.token)

    dl = dlg[:, 0, 0:4]
    dlg_row = jnp.pad(dl[:, 0::2].reshape(1, N_HEADS), ((0, 0), (0, D - N_HEADS))) + jnp.pad(
        dl[:, 1::2].reshape(1, N_HEADS), ((0, 0), (N_HEADS, D - 2 * N_HEADS)))
    partials = _rows(16, (0, part_m[0:1]), (1, part_i[2:3]), (2, part_c[3:4]), (3, part_m[2:3]), (4, part_c[0:3]),
                     (7, part_i[0:2]), (9, part_m[1:2]), (10, part_i[3:5]), (13, part_m[3:4]), (14, dlg_row))
    results = {}
    gather_part = _Exchange("gather_partials", [partials], [jax.ShapeDtypeStruct((N_DEV, 16, D), F32)],
                            N_DEV - 1, _gather_plan)
    _, (r2_in,) = rs_in_chips.wait(gather_part.token)
    results["w_in"] = [o[None] for o in _adam_sharded(w_in[0], m_w_in[0], v_w_in[0], own_in, r2_in, "adam_w_in")]
    _, (got,) = gather_part.wait(results["w_in"][1])
    got = lax.dynamic_update_slice(got, partials[None], (me, 0, 0))
    tot = _sum_devices(got, "sum_partials")
    dmodc = tot[10:13].reshape(1, 3 * D)
    dmod16 = _rows(16, (0, got[:, 7:10, :].reshape(N_DEV, 3 * D)), (N_DEV, dmodc))
    dmod16 = lax.dynamic_slice(dmod16, (0, me * ADA_SH), (16, ADA_SH))
    dmodc8 = _rows(8, (0, dmod16[8:9]))
    g_ada_w, d_ada_w, nm_ada_w, nv_ada_w, cctx_part = _ada_backward(
        act16.T, dmod16, dmodc8, ada_w[0], m_ada_w[0], v_ada_w[0])
    gather_cctx = _Exchange("gather_cctx", [cctx_part], [jax.ShapeDtypeStruct((N_DEV, 8, D), F32)],
                            N_DEV - 1, _gather_plan)
    _, (r2_sq,) = rs_sq_chips.wait(gather_cctx.token)
    square = _adam_square([(w_a[0], m_w_a[0], v_w_a[0]), (w_b[0], m_w_b[0], v_w_b[0]),
                           (w_out[0], m_w_out[0], v_w_out[0])], own_sq, r2_sq)
    _, (got_cctx,) = gather_cctx.wait(square[0][1])
    dact_ctx = _sum_devices(lax.dynamic_update_slice(got_cctx, cctx_part[None], (me, 0, 0)), "sum_cctx")

    view = {"c_ctx": (1, D), "final_norm_w": (1, D)}
    given = {"c_ctx": (c_ctx, m_c_ctx, v_c_ctx), "norm_w": (norm_w, m_norm_w, v_norm_w),
             "conv_b": (conv_b, m_conv_b, v_conv_b), "gn_w": (gn_w, m_gn_w, v_gn_w),
             "final_norm_w": (final_norm_w, m_final_norm_w, v_final_norm_w), "ada_b": (ada_b, m_ada_b, v_ada_b),
             "decay_logit": (decay_logit, m_decay_logit, v_decay_logit), "conv_w": (conv_w, m_conv_w, v_conv_w)}
    small_in = [tuple(a.reshape(view.get(name, a.shape)) for a in given[name]) for name in SMALL_PARAMS]
    conv_w_grad = lax.dynamic_slice(tot, (4, me * csh), (3, csh))
    small_out = _adam_small(tot, dact_ctx, conv_w_grad, tot[14, 0:2 * N_HEADS].reshape(2, N_HEADS), small_in)
    results.update({name: [o.reshape(given[name][0].shape) for o in outs_]
                    for name, outs_ in zip(SMALL_PARAMS, small_out)})
    for name, outs_ in zip(("w_a", "w_b", "w_out"), square):
        results[name] = [o[None] for o in outs_]
    results["ada_w"] = [o[None] for o in (g_ada_w, d_ada_w, nm_ada_w, nv_ada_w)]

    order = ("c_ctx", "norm_w", "ada_w", "ada_b", "w_in", "conv_w", "conv_b", "decay_logit", "gn_w",
             "w_a", "w_b", "w_out", "final_norm_w")
    outs = [results[name][kind] for kind in range(4) for name in order]
    return (tot[13, 0], grad_x[None], *outs)
```

```python
import math

import jax
import jax.numpy as jnp
from jax import lax
from jax.experimental import pallas as pl
from jax.experimental.pallas import tpu as pltpu

F32 = jnp.float32
BF16 = jnp.bfloat16
MESH = pl.DeviceIdType.MESH
_HBM_SPEC = pl.BlockSpec(memory_space=pltpu.HBM)
_SEM_SPEC = pl.BlockSpec(memory_space=pltpu.SEMAPHORE)
_EFFECT = pltpu.SideEffectType.DATAFLOW_SIDE_EFFECTING

N_DEV = 8
D = 1024
N_HEADS = 8
DK = 64
DV = 128
GRID_W = 64
ROPE_BASE = 10000.0
EPS = 1e-6
PW = 9 * D
WSH = PW // N_DEV
RSH = D // N_DEV
ADA_SH = 3 * D // N_DEV
TM = 256
RET_C = 256
HALO = 16
LANES = 128
VMEM_LIMIT = 60 * 1024 * 1024

ADAM_LR = 0.001
ADAM_B1 = 0.9
ADAM_B2 = 0.999
ADAM_EPS = 1e-08
ADAM_WD = 0.01
ADAM_STEP = 10

CB_H, CB_BG, CB_CG, CB_ZA, CB_QK, CB_V, CB_ZB, CB_GA, CB_GB = range(9)


def _cparams(**kw):
    return pltpu.CompilerParams(vmem_limit_bytes=VMEM_LIMIT, **kw)


def _dot(a, b):
    return jnp.dot(a, b, preferred_element_type=F32)


def _dot_nt(a, b):
    return lax.dot_general(a, b, (((1,), (1,)), ((), ())), preferred_element_type=F32)


def _dot_tn(a, b):
    return lax.dot_general(a, b, (((0,), (0,)), ((), ())), preferred_element_type=F32)


def _sigmoid(z):
    return 0.5 * jnp.tanh(0.5 * z) + 0.5


def _row_tile(rows):
    return TM if rows % TM == 0 else rows


def _coords():
    return lax.axis_index("x"), lax.axis_index("y"), lax.axis_index("c")


def _flip(v, bit):
    return 1 - v if bit else v


def _all_gather_small(blk, name):
    rows, cols = blk.shape

    def body(x_ref, out_ref, done, send_sems, recv_sems, local_sem):
        x, y, c = _coords()
        me = 4 * x + 2 * y + c
        mine = pltpu.make_async_copy(x_ref, out_ref.at[me], local_sem)
        mine.start()

        def copy(k, slot):
            peer = (_flip(x, k & 4), _flip(y, k & 2), _flip(c, k & 1))
            return pltpu.make_async_remote_copy(
                src_ref=x_ref, dst_ref=out_ref.at[slot], send_sem=send_sems.at[k - 1],
                recv_sem=recv_sems.at[k - 1], device_id=peer, device_id_type=MESH)

        for k in range(1, N_DEV):
            copy(k, me).start()
        for k in range(1, N_DEV):
            copy(k, jnp.bitwise_xor(me, k)).wait_recv()
        for k in range(1, N_DEV):
            copy(k, me).wait_send()
        mine.wait()
        done[...] = jnp.zeros((8, LANES), F32)

    vmem = pl.BlockSpec(memory_space=pltpu.VMEM)
    return pl.pallas_call(
        body, name=name,
        out_shape=[jax.ShapeDtypeStruct((N_DEV, rows, cols), blk.dtype), jax.ShapeDtypeStruct((8, LANES), F32)],
        in_specs=[vmem], out_specs=[vmem, vmem],
        scratch_shapes=[pltpu.SemaphoreType.DMA((N_DEV - 1,)), pltpu.SemaphoreType.DMA((N_DEV - 1,)),
                        pltpu.SemaphoreType.DMA],
    )(blk)


class _Absent:
    at = property(lambda self: self)

    def __getitem__(self, index):
        return self


class _Exchange:
    def __init__(self, name, srcs, land_shapes, n_copies, plan, after=None):
        self.name, self.plan, self.n_copies = name, plan, n_copies
        self.n_src, self.n_land = len(srcs), len(land_shapes)
        hbm = lambda a: pltpu.HBM(a.shape, a.dtype)
        n = self.n_src + self.n_land
        lands = [pltpu.with_memory_space_constraint(lax.empty(s.shape, s.dtype), pltpu.HBM) for s in land_shapes]
        srcs = [pltpu.with_memory_space_constraint(s, pltpu.HBM) for s in srcs]
        extra = [] if after is None else [after]

        def body(*refs):
            send_sems, recv_sems = refs[n + len(extra)], refs[n + len(extra) + 1]
            for cp in self._copies(refs, send_sems, recv_sems):
                cp.start()
            refs[-1][...] = jnp.zeros((8, LANES), F32)

        outs = pl.pallas_call(
            body, name=name + "_start",
            out_shape=(pltpu.SemaphoreType.DMA((n_copies,)), pltpu.SemaphoreType.DMA((n_copies,)),
                       *[hbm(a) for a in srcs], *[hbm(a) for a in lands], jax.ShapeDtypeStruct((8, LANES), F32)),
            in_specs=[_HBM_SPEC] * n + [pl.BlockSpec(memory_space=pl.ANY)] * len(extra),
            out_specs=(_SEM_SPEC, _SEM_SPEC, *[_HBM_SPEC] * n, pl.BlockSpec(memory_space=pltpu.VMEM)),
            input_output_aliases={i: 2 + i for i in range(n)},
            compiler_params=pltpu.CompilerParams(has_side_effects=_EFFECT),
        )(*srcs, *lands, *extra)
        self.send_sems, self.recv_sems = outs[0], outs[1]
        self.buffers = list(outs[2:2 + n])
        self.token = outs[-1]
        self.waited = 0

    def _copies(self, refs, send_sems, recv_sems, lo=0, hi=None):
        src_refs, land_refs = refs[:self.n_src], refs[self.n_src:self.n_src + self.n_land]
        planned = list(enumerate(self.plan(src_refs, land_refs)))[lo:hi]
        return [pltpu.make_async_remote_copy(src_ref=s, dst_ref=d, send_sem=send_sems.at[k], recv_sem=recv_sems.at[k],
                                             device_id=dev, device_id_type=MESH) for k, (s, d, dev) in planned]

    def wait(self, after, count=None, through=None):
        n = self.n_src + self.n_land
        lo = self.waited
        hi = self.n_copies if count is None else lo + count
        self.waited = hi
        through = list(range(n)) if through is None else list(through)
        m = len(through)

        def body(*refs):
            held = [_Absent()] * n
            for pos, ref in zip(through, refs[:m]):
                held[pos] = ref
            for cp in self._copies(held, refs[m], refs[m + 1], lo, hi):
                cp.wait_send()
                cp.wait_recv()

        outs = pl.pallas_call(
            body, name=f"{self.name}_wait{lo}" if lo or hi < self.n_copies else self.name + "_wait",
            out_shape=tuple(pltpu.HBM(self.buffers[p].shape, self.buffers[p].dtype) for p in through),
            in_specs=[_HBM_SPEC] * m + [_SEM_SPEC, _SEM_SPEC, pl.BlockSpec(memory_space=pl.ANY)],
            out_specs=tuple([_HBM_SPEC] * m),
            input_output_aliases={i: i for i in range(m)},
            compiler_params=pltpu.CompilerParams(has_side_effects=_EFFECT),
        )(*[self.buffers[p] for p in through], self.send_sems, self.recv_sems, after)
        for pos, out in zip(through, outs):
            self.buffers[pos] = out
        return list(self.buffers[:self.n_src]), list(self.buffers[self.n_src:])


def _pair_plan(src_index):
    def plan(srcs, lands):
        x, y, c = _coords()
        chip = 2 * x + y
        return [(srcs[0].at[src_index(j, chip, c)], lands[0].at[j], (x, y, 1 - c)) for j in range(4)]
    return plan


def _chips_plan(srcs, lands):
    x, y, c = _coords()
    return [(srcs[0].at[j - 1], lands[0].at[j - 1], (_flip(x, j & 2), _flip(y, j & 1), c)) for j in range(1, 4)]


def _peer(k):
    x, y, c = _coords()
    return _flip(x, k & 4), _flip(y, k & 2), _flip(c, k & 1)


def _send_plan(ks):
    def plan(srcs, lands):
        return [(srcs[0].at[0], lands[0].at[p], _peer(k)) for p, k in enumerate(ks)]
    return plan


def _forward_plan(moves):
    def plan(srcs, lands):
        return [(srcs[0].at[s], srcs[0].at[d], _peer(1)) for s, d in moves]
    return plan


def _gather_plan(srcs, lands):
    x, y, c = _coords()
    me = 4 * x + 2 * y + c
    return [(srcs[a], lands[a].at[me], (_flip(x, k & 4), _flip(y, k & 2), _flip(c, k & 1)))
            for a in range(len(srcs)) for k in range(1, N_DEV)]


def _pair_sum(grad, recv, owners, name):
    _, rows, cols = grad.shape
    tr = rows

    def body(own_ref, g_ref, r_ref, mine_ref, send_ref):
        j = pl.program_id(1)
        total = g_ref[...] + r_ref[...]

        @pl.when(j == 0)
        def _():
            mine_ref[...] = total

        @pl.when(j > 0)
        def _():
            send_ref[...] = total.astype(BF16)

    grid_spec = pltpu.PrefetchScalarGridSpec(
        num_scalar_prefetch=1, grid=(rows // tr, 4),
        in_specs=[pl.BlockSpec((1, tr, cols), lambda i, j, own: (own[j], i, 0)),
                  pl.BlockSpec((1, tr, cols), lambda i, j, own: (j, i, 0))],
        out_specs=[pl.BlockSpec((1, tr, cols), lambda i, j, own: (0, i, 0)),
                   pl.BlockSpec((1, tr, cols), lambda i, j, own: (jnp.maximum(j - 1, 0), i, 0))])
    return pl.pallas_call(
        body, name=name, grid_spec=grid_spec,
        out_shape=[jax.ShapeDtypeStruct((1, rows, cols), F32), jax.ShapeDtypeStruct((3, rows, cols), BF16)],
        compiler_params=_cparams(dimension_semantics=("arbitrary", "arbitrary")),
    )(owners, grad, recv)


def _modulation(c16, ada_w_sh, ada_b_sh, decay_logit):
    def body(c_ref, w_ref, b_ref, dl_ref, mod_ref, act_ref, lg_ref):
        cv = c_ref[...]
        act = cv * _sigmoid(cv)
        act_ref[...] = act
        mod_ref[...] = jnp.dot(act, w_ref[...], preferred_element_type=F32,
                               precision=lax.Precision.HIGHEST) + b_ref[...]
        z = dl_ref[...]
        lg_ref[...] = jnp.minimum(z, 0.0) - jnp.log(1.0 + jnp.exp(-jnp.abs(z)))

    return pl.pallas_call(
        body, name="modulation",
        out_shape=[jax.ShapeDtypeStruct((16, ADA_SH), F32), jax.ShapeDtypeStruct((16, D), F32),
                   jax.ShapeDtypeStruct(decay_logit.shape, F32)],
        compiler_params=_cparams(),
    )(c16, ada_w_sh, ada_b_sh, decay_logit)


def _adam_update(w, g, m, v):
    m2 = ADAM_B1 * m + (1.0 - ADAM_B1) * g
    v2 = ADAM_B2 * v + (1.0 - ADAM_B2) * (g * g)
    m_hat = m2 / (1.0 - ADAM_B1 ** ADAM_STEP)
    v_hat = v2 / (1.0 - ADAM_B2 ** ADAM_STEP)
    delta = -ADAM_LR * (m_hat / (jnp.sqrt(v_hat) + ADAM_EPS) + ADAM_WD * w)
    return delta, m2, v2


def _adam_sharded(w, m, v, own, recv, name):
    rows, cols = w.shape
    tr = _row_tile(rows)

    def body(w_ref, m_ref, v_ref, g0, g1, g2, g3, g_ref, d_ref, m_out, v_out):
        g = ((g0[0] + g1[0].astype(F32)) + g2[0].astype(F32)) + g3[0].astype(F32)
        delta, m2, v2 = _adam_update(w_ref[...], g, m_ref[...], v_ref[...])
        g_ref[...] = g
        d_ref[...] = delta
        m_out[...] = m2
        v_out[...] = v2

    flat = pl.BlockSpec((tr, cols), lambda i: (i, 0))
    part = lambda j: pl.BlockSpec((1, tr, cols), lambda i, j=j: (j, i, 0))
    return pl.pallas_call(
        body, name=name, grid=(rows // tr,),
        in_specs=[flat, flat, flat, part(0), part(0), part(1), part(2)],
        out_specs=[flat] * 4,
        out_shape=[jax.ShapeDtypeStruct((rows, cols), F32)] * 4,
        compiler_params=_cparams(dimension_semantics=("arbitrary",)),
    )(w, m, v, own, recv, recv, recv)


def _ada_backward(act_t, dmod16, dmodc8, ada_w_sh, m, v):
    def body(at_ref, dm_ref, dc_ref, w_ref, m_ref, v_ref, g_ref, d_ref, m_out, v_out, pc_ref):
        g = jnp.dot(at_ref[...], dm_ref[...], preferred_element_type=F32, precision=lax.Precision.HIGHEST)
        w = w_ref[...]
        delta, m2, v2 = _adam_update(w, g, m_ref[...], v_ref[...])
        g_ref[...] = g
        d_ref[...] = delta
        m_out[...] = m2
        v_out[...] = v2
        pc_ref[...] = lax.dot_general(dc_ref[...], w, (((1,), (1,)), ((), ())), preferred_element_type=F32,
                                      precision=lax.Precision.HIGHEST)

    return pl.pallas_call(
        body, name="ada_backward",
        out_shape=[jax.ShapeDtypeStruct((D, ADA_SH), F32)] * 4 + [jax.ShapeDtypeStruct((8, D), F32)],
        compiler_params=_cparams(),
    )(act_t, dmod16, dmodc8, ada_w_sh, m, v)


def _sum_devices(gathered, name):
    _, rows, cols = gathered.shape

    def body(g_ref, o_ref):
        acc = g_ref[0]
        for d in range(1, N_DEV):
            acc = acc + g_ref[d]
        o_ref[...] = acc

    return pl.pallas_call(body, name=name, out_shape=jax.ShapeDtypeStruct((rows, cols), F32),
                          compiler_params=_cparams())(gathered)


SMALL_PARAMS = ("c_ctx", "norm_w", "conv_b", "gn_w", "final_norm_w", "ada_b", "decay_logit", "conv_w")


def _adam_small(tot, dact_ctx, conv_w_grad, decay_grad, params):
    n = len(SMALL_PARAMS)

    def body(tot_ref, dact_ref, cwg_ref, dlg_ref, *refs):
        ins, outs = refs[:3 * n], refs[3 * n:]
        raw = {
            "c_ctx": dact_ref[0:1, :],
            "norm_w": tot_ref[1:2, :], "conv_b": tot_ref[2:3, :], "gn_w": tot_ref[3:4, :],
            "final_norm_w": tot_ref[0:1, :],
            "ada_b": jnp.concatenate([tot_ref[7 + r:8 + r, :] + tot_ref[10 + r:11 + r, :] for r in range(3)], axis=1),
            "decay_logit": dlg_ref[...],
            "conv_w": cwg_ref[...],
        }
        for k, name in enumerate(SMALL_PARAMS):
            lead = (0,) if len(ins[3 * k].shape) == 3 else ()
            w, m, v = (ref[lead + (...,)] for ref in ins[3 * k:3 * k + 3])
            g = raw[name]
            if name == "c_ctx":
                s = _sigmoid(w)
                g = g * (s * (1.0 + w * (1.0 - s)))
            elif name == "decay_logit":
                g = g * (1.0 - _sigmoid(w))
            delta, m2, v2 = _adam_update(w, g, m, v)
            for ref, val in zip(outs[4 * k:4 * k + 4], (g, delta, m2, v2)):
                ref[lead + (...,)] = val

    flat = [a for trio in params for a in trio]
    out_shape = [jax.ShapeDtypeStruct(trio[0].shape, F32) for trio in params for _ in range(4)]
    outs = pl.pallas_call(body, name="adam_small", out_shape=out_shape, compiler_params=_cparams())(
        tot, dact_ctx, conv_w_grad, decay_grad, *flat)
    return [outs[4 * k:4 * k + 4] for k in range(n)]


def _adam_square(params, own, recv):
    def body(own_ref, recv_ref, *refs):
        ins, outs = refs[:9], refs[9:]
        for k in range(3):
            rows = slice(k * RSH, (k + 1) * RSH)
            g = own_ref[0, rows, :]
            for j in range(3):
                g = g + recv_ref[j, rows, :].astype(F32)
            delta, m2, v2 = _adam_update(ins[3 * k][...], g, ins[3 * k + 1][...], ins[3 * k + 2][...])
            for ref, val in zip(outs[4 * k:4 * k + 4], (g, delta, m2, v2)):
                ref[...] = val

    flat = [a for trio in params for a in trio]
    outs = pl.pallas_call(body, name="adam_square", out_shape=[jax.ShapeDtypeStruct((RSH, D), F32)] * 12,
                          compiler_params=_cparams())(own, recv, *flat)
    return [outs[4 * k:4 * k + 4] for k in range(3)]


def _rope(t, cos, sin):
    lane = lax.broadcasted_iota(jnp.int32, (1, LANES), 1)
    first_half = jnp.bitwise_and(lane, DK // 2) == 0
    partner = jnp.where(first_half, pltpu.roll(t, LANES - DK // 2, 1), pltpu.roll(t, DK // 2, 1))
    return t * cos + partner * sin


class _Shards:
    def __init__(self, arrays, picks):
        self.arrays, self.picks = list(arrays), list(picks)

    def __add__(self, other):
        shift = len(self.arrays)
        return _Shards(self.arrays + other.arrays, self.picks + [(a + shift, s) for a, s in other.picks])


def _load_w_in(step, ids_ref, shards, w_refs, w_vmem, sems):
    @pl.when(step == 0)
    def _():
        copies = []
        for p, (a, slot) in enumerate(shards.picks):
            col = pl.multiple_of(ids_ref[p] * WSH, LANES)
            copies.append(pltpu.make_async_copy(w_refs[a].at[slot], w_vmem.at[:, pl.ds(col, WSH)], sems.at[p]))
        for cp in copies:
            cp.start()
        for cp in copies:
            cp.wait()


def _project_columns(xmb, w_ref, shard, cos, sin, p_ref):
    width = 2 * WSH
    q_lo, k_lo, k_hi = CB_QK * D, CB_QK * D + N_HEADS * DK, CB_V * D
    q_pair, k_pair = q_lo // width, k_lo // width
    assert (q_pair + 1) * width == k_lo and k_pair * width == k_lo and k_hi <= (k_pair + 1) * width
    mxu_cols = 2 * LANES

    def columns(rot_lo, rot_hi, scale):
        for a in range(0, width, mxu_cols):
            acc = _dot(xmb, w_ref[:, a:a + mxu_cols])
            for col in range(a, a + mxu_cols, LANES):
                piece = acc[:, col - a:col - a + LANES]
                if rot_lo <= col < rot_hi:
                    piece = _rope(piece if scale == 1.0 else piece * scale, cos, sin)
                p_ref[:, col:col + LANES] = piece.astype(BF16)

    @pl.when(shard == q_pair)
    def _():
        columns(q_lo - q_pair * width, width, 1.0)

    @pl.when(shard == k_pair)
    def _():
        columns(0, k_hi - k_lo, DK ** -0.5)

    @pl.when(jnp.logical_and(shard != q_pair, shard != k_pair))
    def _():
        columns(0, 0, 1.0)


_AFTER_SPEC = pl.BlockSpec(memory_space=pl.ANY)


def _prenorm(x, ctx, modx, modc, norm_w, after):
    n_x_tiles = x.shape[0] // TM
    lext = x.shape[0] + ctx.shape[0]

    def body(x_ref, ctx_ref, mx_ref, mc_ref, nw_ref, after_ref, xm_ref, xmt_ref, cos_ref, sin_ref):
        del after_ref
        i = pl.program_id(0)
        is_ctx = i >= n_x_tiles
        x = jnp.where(is_ctx, ctx_ref[...], x_ref[...])
        r = lax.rsqrt(jnp.mean(x * x, axis=-1, keepdims=True) + EPS)
        sh = jnp.where(is_ctx, mc_ref[0:1, :], mx_ref[0:1, :])
        sc = jnp.where(is_ctx, mc_ref[1:2, :], mx_ref[1:2, :])
        xm = (x * r * nw_ref[...]) * (1.0 + sc) + sh
        xm_ref[...] = xm.astype(BF16)
        xmt_ref[...] = xm.T.astype(BF16)
        pos = i * TM + lax.broadcasted_iota(jnp.int32, (TM, 1), 0)
        lane = lax.broadcasted_iota(jnp.int32, (1, LANES), 1)
        n_freq = DK // 4
        inv = jnp.exp((lane % n_freq).astype(F32) * (-math.log(ROPE_BASE) / n_freq))
        by_column = (lane % (2 * n_freq)) >= n_freq
        ang = jnp.where(by_column, (pos % GRID_W).astype(F32), (pos // GRID_W).astype(F32)) * inv
        first_half = (lane % DK) < DK // 2
        cos_ref[...] = jnp.where(is_ctx, 1.0, jnp.cos(ang))
        sin_ref[...] = jnp.where(is_ctx, 0.0, jnp.where(first_half, -jnp.sin(ang), jnp.sin(ang)))

    full = lambda a: pl.BlockSpec(a.shape, lambda i: (0,) * a.ndim)
    row = lambda width: pl.BlockSpec((TM, width), lambda i: (i, 0))
    return pl.pallas_call(
        body, name="prenorm", grid=(lext // TM,),
        in_specs=[pl.BlockSpec((TM, D), lambda i: (jnp.minimum(i, n_x_tiles - 1), 0)), full(ctx),
                  full(modx), full(modc), full(norm_w), _AFTER_SPEC],
        out_specs=[row(D), pl.BlockSpec((D, TM), lambda i: (0, i)), row(LANES), row(LANES)],
        out_shape=[jax.ShapeDtypeStruct((lext, D), BF16), jax.ShapeDtypeStruct((D, lext), BF16),
                   jax.ShapeDtypeStruct((lext, LANES), F32), jax.ShapeDtypeStruct((lext, LANES), F32)],
        compiler_params=_cparams(dimension_semantics=("arbitrary",)),
    )(x, ctx, modx, modc, norm_w, after)


def _in_projection(p_ext, xm, cos_t, sin_t, w, pairs, me, name):
    lext = xm.shape[0]
    tr = lext // 4
    meta = []
    for (slot_a, k_a), (slot_b, k_b) in pairs:
        assert k_a ^ k_b == 1
        dev_a = jnp.bitwise_xor(me, k_a)
        a_low = dev_a % 2 == 0
        meta += [jnp.where(a_low, slot_a, slot_b), jnp.where(a_low, slot_b, slot_a), dev_a // 2]
    meta = jnp.stack(meta).astype(jnp.int32)

    def body(meta_ref, xm_ref, cos_ref, sin_ref, w_hbm, *rest):
        p_ref, w_ref, sems = rest[-3:]
        j, i = pl.program_id(0), pl.program_id(1)

        @pl.when(i == 0)
        def _():
            copies = [pltpu.make_async_copy(w_hbm.at[meta_ref[3 * j + h]], w_ref.at[:, h * WSH:(h + 1) * WSH],
                                            sems.at[h]) for h in range(2)]
            for cp in copies:
                cp.start()
            for cp in copies:
                cp.wait()

        _project_columns(xm_ref[...], w_ref, meta_ref[3 * j + 2], cos_ref[...], sin_ref[...], p_ref)

    row = lambda width: pl.BlockSpec((tr, width), lambda j, i, meta_: (i, 0))
    grid_spec = pltpu.PrefetchScalarGridSpec(
        num_scalar_prefetch=1, grid=(len(pairs), lext // tr),
        in_specs=[row(D), row(LANES), row(LANES), pl.BlockSpec(memory_space=pl.ANY)]
        + ([] if p_ext is None else [pl.BlockSpec(memory_space=pl.ANY)]),
        out_specs=pl.BlockSpec((tr, 2 * WSH), lambda j, i, meta_: (i, meta_[3 * j + 2])),
        scratch_shapes=[pltpu.VMEM((D, 2 * WSH), BF16), pltpu.SemaphoreType.DMA((2,))])
    return pl.pallas_call(
        body, name=name, grid_spec=grid_spec,
        out_shape=jax.ShapeDtypeStruct((lext, PW), BF16),
        input_output_aliases={} if p_ext is None else {5: 0},
        compiler_params=_cparams(dimension_semantics=("arbitrary", "arbitrary")),
    )(meta, xm, cos_t, sin_t, w, *([] if p_ext is None else [p_ext]))


def _decay_tables(lgf, lgb, n):
    i = lax.broadcasted_iota(jnp.int32, (n, 1), 0).astype(F32)
    return dict(i=i, k_f=jnp.exp(lgf * (n - 1.0 - i)), k_b=jnp.exp(lgb * i),
                q_f=jnp.exp(lgf * (i + 1.0)), q_b=jnp.exp(lgb * (n - i)))


def _decay_matrix(lgf, lgb, n):
    ii = lax.broadcasted_iota(jnp.int32, (n, n), 0)
    jj = lax.broadcasted_iota(jnp.int32, (n, n), 1)
    diff = (ii - jj).astype(F32)
    low = jnp.exp(lgf * jnp.maximum(diff, 0.0))
    up = jnp.exp(lgb * jnp.maximum(-diff, 0.0))
    return jnp.where(diff > 0, low, jnp.where(diff < 0, up, 2.0)), diff


def _cat_lanes(a, b):
    return jnp.concatenate([a.astype(BF16), b.astype(BF16)], axis=1)


def _retention_forward(p_ext, lg, seq, ctx_len):
    lext = seq + ctx_len
    n_chunks = seq // RET_C
    C = RET_C

    def body(lg_ref, q_ref, k_ref, v_ref, rn_ref, rstd_ref, kv_scr, sf_scr, sb_scr):
        pair = pl.program_id(0)
        lane = lax.broadcasted_iota(jnp.int32, (1, LANES), 1)
        heads = range(2)
        hmask = [(lane // DK == hh).astype(F32) for hh in heads]
        lgf = [lg_ref[0, 2 * pair + hh] for hh in heads]
        lgb = [lg_ref[1, 2 * pair + hh] for hh in heads]
        vs = [slice(hh * DV, (hh + 1) * DV) for hh in heads]
        t = [_decay_tables(lgf[hh], lgb[hh], C) for hh in heads]
        kc_all = k_ref[seq:lext, :].astype(F32)
        s0 = []
        for hh in heads:
            tc = _decay_tables(lgf[hh], lgb[hh], ctx_len)
            kc = kc_all * hmask[hh]
            s0.append(_dot_tn(_cat_lanes(kc * tc["k_f"], kc * tc["k_b"]), v_ref[seq:lext, vs[hh]]))

        def increments(c, carry):
            rows = pl.ds(pl.multiple_of(c * C, C), C)
            k_all = k_ref[rows, :].astype(F32)
            for hh in heads:
                k = k_all * hmask[hh]
                kv_scr[hh, c] = _dot_tn(_cat_lanes(k * t[hh]["k_f"], k * t[hh]["k_b"]), v_ref[rows, vs[hh]])
            return carry

        lax.fori_loop(0, n_chunks, increments, 0, unroll=2)
        gf_c = [jnp.exp(lgf[hh] * C) for hh in heads]
        gb_c = [jnp.exp(lgb[hh] * C) for hh in heads]

        def scan_f(c, s):
            for hh in heads:
                sf_scr[hh, c] = s[hh]
            return tuple(gf_c[hh] * s[hh] + kv_scr[hh, c, 0:LANES, :] for hh in heads)

        def scan_b(n, s):
            c = n_chunks - 1 - n
            for hh in heads:
                sb_scr[hh, c] = s[hh]
            return tuple(gb_c[hh] * s[hh] + kv_scr[hh, c, LANES:2 * LANES, :] for hh in heads)

        lax.fori_loop(0, n_chunks, scan_f, tuple(s0[hh][0:LANES] for hh in heads))
        lax.fori_loop(0, n_chunks, scan_b, tuple(s0[hh][LANES:2 * LANES] for hh in heads))
        dmat = [_decay_matrix(lgf[hh], lgb[hh], C)[0] for hh in heads]

        def outputs(c, carry):
            rows = pl.ds(pl.multiple_of(c * C, C), C)
            q_all = q_ref[rows, :].astype(F32)
            k = k_ref[rows, :]
            for hh in heads:
                q = q_all * hmask[hh]
                v = v_ref[rows, vs[hh]]
                s = _dot_nt(q.astype(BF16), k)
                inner = _dot((s * dmat[hh]).astype(BF16), v)
                states = jnp.concatenate([sf_scr[hh, c], sb_scr[hh, c]], axis=0).astype(BF16)
                ret = inner + _dot(_cat_lanes(q * t[hh]["q_f"], q * t[hh]["q_b"]), states)
                mu = jnp.mean(ret, axis=-1, keepdims=True)
                cen = ret - mu
                rstd = lax.rsqrt(jnp.mean(cen * cen, axis=-1, keepdims=True) + EPS)
                rn_ref[rows, vs[hh]] = cen * rstd
                rstd_ref[rows, vs[hh]] = jnp.broadcast_to(rstd, (C, DV))
            return carry

        lax.fori_loop(0, n_chunks, outputs, 0, unroll=2)

    qk0 = CB_QK * D // LANES
    return pl.pallas_call(
        body, name="retention_forward", grid=(N_HEADS // 2,),
        in_specs=[pl.BlockSpec(memory_space=pltpu.SMEM),
                  pl.BlockSpec((lext, LANES), lambda g: (0, qk0 + g)),
                  pl.BlockSpec((lext, LANES), lambda g: (0, qk0 + N_HEADS // 2 + g)),
                  pl.BlockSpec((lext, 2 * DV), lambda g: (0, CB_V * D // (2 * DV) + g))],
        out_specs=[pl.BlockSpec((seq, 2 * DV), lambda g: (0, g))] * 2,
        out_shape=[jax.ShapeDtypeStruct((seq, D), F32)] * 2,
        scratch_shapes=[pltpu.VMEM((2, n_chunks, 2 * LANES, DV), F32), pltpu.VMEM((2, n_chunks, LANES, DV), F32),
                        pltpu.VMEM((2, n_chunks, LANES, DV), F32)],
        compiler_params=_cparams(dimension_semantics=("arbitrary",)),
    )(lg, p_ext, p_ext, p_ext)


def _retention_backward(p_ext, dret, lg, cos_t, sin_t, seq, ctx_len, after):
    lext = seq + ctx_len
    n_chunks = seq // RET_C
    C = RET_C

    def body(lg_ref, q_ref, k_ref, v_ref, do_ref, cos_ref, sin_ref, after_ref, dq_ref, dk_ref, dv_ref, dlg_ref,
             kv_scr, g_scr, sf_scr, sb_scr, gfn_scr, gbp_scr):
        pair = pl.program_id(0)
        lane = lax.broadcasted_iota(jnp.int32, (1, LANES), 1)
        heads = range(2)
        hmask = [(lane // DK == hh).astype(F32) for hh in heads]
        lgf = [lg_ref[0, 2 * pair + hh] for hh in heads]
        lgb = [lg_ref[1, 2 * pair + hh] for hh in heads]
        vs = [slice(hh * DV, (hh + 1) * DV) for hh in heads]
        t = [_decay_tables(lgf[hh], lgb[hh], C) for hh in heads]
        tc = [_decay_tables(lgf[hh], lgb[hh], ctx_len) for hh in heads]
        kc_all = k_ref[seq:lext, :].astype(F32)
        kc = [kc_all * hmask[hh] for hh in heads]
        vc = [v_ref[seq:lext, vs[hh]] for hh in heads]
        kc_cat = [_cat_lanes(kc[hh] * tc[hh]["k_f"], kc[hh] * tc[hh]["k_b"]) for hh in heads]
        s0 = [_dot_tn(kc_cat[hh], vc[hh]) for hh in heads]

        def increments(c, carry):
            rows = pl.ds(pl.multiple_of(c * C, C), C)
            k_all = k_ref[rows, :].astype(F32)
            q_all = q_ref[rows, :].astype(F32)
            for hh in heads:
                k, q = k_all * hmask[hh], q_all * hmask[hh]
                kv_scr[hh, c] = _dot_tn(_cat_lanes(k * t[hh]["k_f"], k * t[hh]["k_b"]), v_ref[rows, vs[hh]])
                g_scr[hh, c] = _dot_tn(_cat_lanes(q * t[hh]["q_f"], q * t[hh]["q_b"]), do_ref[rows, vs[hh]])
            return carry

        lax.fori_loop(0, n_chunks, increments, 0, unroll=2)
        gf_c = [jnp.exp(lgf[hh] * C) for hh in heads]
        gb_c = [jnp.exp(lgb[hh] * C) for hh in heads]

        def scan_f(c, s):
            for hh in heads:
                sf_scr[hh, c] = s[hh]
            return tuple(gf_c[hh] * s[hh] + kv_scr[hh, c, 0:LANES, :] for hh in heads)

        def scan_b(n, s):
            c = n_chunks - 1 - n
            for hh in heads:
                sb_scr[hh, c] = s[hh]
            return tuple(gb_c[hh] * s[hh] + kv_scr[hh, c, LANES:2 * LANES, :] for hh in heads)

        def scan_gf(n, carry):
            c = n_chunks - 1 - n
            for hh in heads:
                gfn_scr[hh, c] = carry[hh]
            return tuple(g_scr[hh, c, 0:LANES, :] + gf_c[hh] * carry[hh] for hh in heads)

        def scan_gb(c, carry):
            for hh in heads:
                gbp_scr[hh, c] = carry[hh]
            return tuple(g_scr[hh, c, LANES:2 * LANES, :] + gb_c[hh] * carry[hh] for hh in heads)

        lax.fori_loop(0, n_chunks, scan_f, tuple(s0[hh][0:LANES] for hh in heads))
        lax.fori_loop(0, n_chunks, scan_b, tuple(s0[hh][LANES:2 * LANES] for hh in heads))
        zero_state = jnp.zeros((LANES, DV), F32)
        gf_first = lax.fori_loop(0, n_chunks, scan_gf, (zero_state, zero_state))
        gb_last = lax.fori_loop(0, n_chunks, scan_gb, (zero_state, zero_state))

        dmat, w_f, w_b = [], [], []
        for hh in heads:
            dm, diff = _decay_matrix(lgf[hh], lgb[hh], C)
            dmat.append(dm)
            w_f.append(jnp.where(diff > 0, diff * dm, 0.0))
            w_b.append(jnp.where(diff < 0, -diff * dm, 0.0))

        def total(a):
            rows_, width = a.shape
            part = jnp.sum(a.reshape(rows_ // 8, 8, width), axis=0)
            return part[:, 0:LANES] + part[:, LANES:2 * LANES] if width == 2 * LANES else part

        def chunk_head(hh, c, rows, q_all, k_all, dlgf, dlgb):
            th = t[hh]
            qm = q_all * hmask[hh]
            km = k_all * hmask[hh]
            qb, kb = qm.astype(BF16), km.astype(BF16)
            v = v_ref[rows, vs[hh]]
            do = do_ref[rows, vs[hh]]
            s = _dot_nt(qb, kb)
            dsd = _dot_nt(do, v)
            ds = (dsd * dmat[hh]).astype(BF16)
            dq_in = _dot(ds, kb)
            dk_in = _dot_tn(ds, qb)
            dv_in = _dot_tn((s * dmat[hh]).astype(BF16), do)
            prod = s * dsd
            dlgf = dlgf + total(prod * w_f[hh])
            dlgb = dlgb + total(prod * w_b[hh])
            sf, sb = sf_scr[hh, c], sb_scr[hh, c]
            states = jnp.concatenate([sf, sb], axis=0).astype(BF16)
            dqc = _dot_nt(do, states)
            dqf = dqc[:, 0:LANES] * th["q_f"]
            dqb = dqc[:, LANES:2 * LANES] * th["q_b"]
            dq = (dq_in + dqf + dqb) * hmask[hh]
            dlgf = dlgf + total((th["i"] + 1.0) * (qm * dqf))
            dlgb = dlgb + total((C - th["i"]) * (qm * dqb))
            gfn, gbp = gfn_scr[hh, c], gbp_scr[hh, c]
            gstates = jnp.concatenate([gfn, gbp], axis=0).astype(BF16)
            dkc = _dot_nt(v, gstates)
            dkf = dkc[:, 0:LANES] * th["k_f"]
            dkb = dkc[:, LANES:2 * LANES] * th["k_b"]
            dk = dk_in + dkf + dkb
            dlgf = dlgf + total((C - 1.0 - th["i"]) * (km * dkf)) + C * gf_c[hh] * total(gfn * sf)
            dlgb = dlgb + total(th["i"] * (km * dkb)) + C * gb_c[hh] * total(gbp * sb)
            dv = dv_in + _dot(_cat_lanes(km * th["k_f"], km * th["k_b"]), gstates)
            dv_ref[rows, vs[hh]] = dv.astype(BF16)
            return dq, dk, dlgf, dlgb

        def chunk(c, carry):
            rows = pl.ds(pl.multiple_of(c * C, C), C)
            q_all = q_ref[rows, :].astype(F32)
            k_all = k_ref[rows, :].astype(F32)
            dq0, dk0, f0, b0 = chunk_head(0, c, rows, q_all, k_all, carry[0], carry[1])
            dq1, dk1, f1, b1 = chunk_head(1, c, rows, q_all, k_all, carry[2], carry[3])
            cos, sin = cos_ref[rows, :], sin_ref[rows, :]
            dq_ref[rows, :] = _rope(dq0 + dq1, cos, -sin).astype(BF16)
            dk_ref[rows, :] = (_rope(dk0 + dk1, cos, -sin) * (DK ** -0.5)).astype(BF16)
            return f0, b0, f1, b1

        zero = jnp.zeros((8, LANES), F32)
        sums = lax.fori_loop(0, n_chunks, chunk, (zero, zero, zero, zero), unroll=2)

        dlg = []
        dk_ctx = jnp.zeros((ctx_len, LANES), F32)
        for hh in heads:
            g0 = jnp.concatenate([gf_first[hh], gb_last[hh]], axis=0).astype(BF16)
            dkcc = _dot_nt(vc[hh], g0)
            dkcf = dkcc[:, 0:LANES] * tc[hh]["k_f"]
            dkcb = dkcc[:, LANES:2 * LANES] * tc[hh]["k_b"]
            dlgf = sums[2 * hh] + total((ctx_len - 1.0 - tc[hh]["i"]) * (kc[hh] * dkcf))
            dlgb = sums[2 * hh + 1] + total(tc[hh]["i"] * (kc[hh] * dkcb))
            dk_ctx = dk_ctx + (dkcf + dkcb) * (DK ** -0.5)
            dv_ref[seq:lext, vs[hh]] = _dot(kc_cat[hh], g0).astype(BF16)
            dlg += [jnp.sum(jnp.sum(a, axis=1, keepdims=True), axis=0, keepdims=True) for a in (dlgf, dlgb)]
        dk_ref[seq:lext, :] = dk_ctx.astype(BF16)
        dq_ref[seq:lext, :] = jnp.zeros((ctx_len, LANES), BF16)

        lane8 = lax.broadcasted_iota(jnp.int32, (8, LANES), 1)
        out = jnp.zeros((8, LANES), F32)
        for n, val in enumerate(dlg):
            out = jnp.where(lane8 == n, val, out)
        dlg_ref[0] = out

    qk0 = CB_QK * D // LANES
    q_spec = pl.BlockSpec((lext, LANES), lambda g: (0, qk0 + g))
    k_spec = pl.BlockSpec((lext, LANES), lambda g: (0, qk0 + N_HEADS // 2 + g))
    v_spec = pl.BlockSpec((lext, 2 * DV), lambda g: (0, CB_V * D // (2 * DV) + g))
    table = pl.BlockSpec((lext, LANES), lambda g: (0, 0))
    state = pltpu.VMEM((2, n_chunks, LANES, DV), F32)
    return pl.pallas_call(
        body, name="retention_backward", grid=(N_HEADS // 2,),
        in_specs=[pl.BlockSpec(memory_space=pltpu.SMEM), q_spec, k_spec, v_spec,
                  pl.BlockSpec((seq, 2 * DV), lambda g: (0, g)), table, table, _AFTER_SPEC],
        out_specs=[pl.BlockSpec((lext, LANES), lambda g: (0, g)), pl.BlockSpec((lext, LANES), lambda g: (0, g)),
                   pl.BlockSpec((lext, 2 * DV), lambda g: (0, g)), pl.BlockSpec((1, 8, LANES), lambda g: (g, 0, 0))],
        out_shape=[jax.ShapeDtypeStruct((lext, N_HEADS * DK), BF16), jax.ShapeDtypeStruct((lext, N_HEADS * DK), BF16),
                   jax.ShapeDtypeStruct((lext, D), BF16), jax.ShapeDtypeStruct((N_HEADS // 2, 8, LANES), F32)],
        scratch_shapes=[pltpu.VMEM((2, n_chunks, 2 * LANES, DV), F32), pltpu.VMEM((2, n_chunks, 2 * LANES, DV), F32),
                        state, state, state, state],
        compiler_params=_cparams(dimension_semantics=("arbitrary",)),
    )(lg, p_ext, p_ext, p_ext, dret, cos_t, sin_t, after)


class _ColumnWriter:
    def __init__(self, dst_hbm, stage, sems, pieces, n_steps):
        self.dst, self.stage, self.sems, self.pieces, self.n_steps = dst_hbm, stage, sems, pieces, n_steps

    def _copies(self, slot, tile):
        rows = pl.ds(pl.multiple_of(tile * TM, TM), TM)
        return [pltpu.make_async_copy(self.stage.at[slot, :, pl.ds(src, width)], self.dst.at[rows, pl.ds(dst, width)],
                                      self.sems.at[slot, k]) for k, (src, dst, width) in enumerate(self.pieces)]

    def slot_for(self, step):
        slot = step % 2

        @pl.when(step >= 2)
        def _():
            for cp in self._copies(slot, step - 2):
                cp.wait()

        return slot

    def send(self, step, slot):
        for cp in self._copies(slot, step):
            cp.start()

        @pl.when(step == self.n_steps - 1)
        def _():
            if self.n_steps >= 2:
                for cp in self._copies(1 - slot, step - 1):
                    cp.wait()
            for cp in self._copies(slot, step):
                cp.wait()


def _merge(p_ext, rn, rstd, x, target, w_a, w_b, w_out, vecs, seq):
    n_tiles = seq // TM
    hb = TM // HALO
    dp_pieces = [(0, CB_BG * D, D), (D, CB_ZA * D, D), (2 * D, CB_ZB * D, 3 * D)]

    def body(h_ref, bg_ref, cg_ref, za_ref, zb_ref, ga_ref, gb_ref, hp_ref, hn_ref, cp_ref, cn_ref,
             rn_ref, rstd_ref, x_ref, t_ref, wa_ref, wb_ref, wo_ref, vec_ref,
             dx1_ref, dp_hbm, dconv_ref, dret_ref, at_ref, b_ref, part_ref, stage, dp_sems):
        i = pl.program_id(0)
        writer = _ColumnWriter(dp_hbm, stage, dp_sems, dp_pieces, n_tiles)
        slot = writer.slot_for(i)
        dpa_ref = stage.at[slot]
        f = lambda ref: ref[...].astype(F32)
        h, bg, cg, za, zb, ga, gb = f(h_ref), f(bg_ref), f(cg_ref), f(za_ref), f(zb_ref), f(ga_ref), f(gb_ref)
        gx, w0, w1, w2 = vec_ref[0:1, :], vec_ref[1:2, :], vec_ref[2:3, :], vec_ref[3:4, :]
        cb, gnw, fw = vec_ref[4:5, :], vec_ref[5:6, :], vec_ref[6:7, :]
        u = cg * h
        row = lax.broadcasted_iota(jnp.int32, (TM, 1), 0)
        u_prev = (f(cp_ref) * f(hp_ref))[HALO - 1:HALO, :]
        u_next = (f(cn_ref) * f(hn_ref))[0:1, :]
        u_prev = jnp.where(i == 0, 0.0, u_prev)
        u_next = jnp.where(i == n_tiles - 1, 0.0, u_next)
        u_up = jnp.where(row == 0, u_prev, pltpu.roll(u, 1, 0))
        u_dn = jnp.where(row == TM - 1, u_next, pltpu.roll(u, TM - 1, 0))
        conv = w0 * u_up + w1 * u + w2 * u_dn + cb
        sza = _sigmoid(za)
        silu_za = za * sza
        a_act = silu_za * bg * conv
        rn = rn_ref[...]
        szb = _sigmoid(zb)
        silu_zb = zb * szb
        rg = rn * gnw
        b_act = silu_zb * rg
        y_a = _dot(a_act.astype(BF16), wa_ref[...])
        y_b = _dot(b_act.astype(BF16), wb_ref[...])
        sga, sgb = _sigmoid(ga), _sigmoid(gb)
        mix = sga * y_a + sgb * y_b
        y = _dot(mix.astype(BF16), wo_ref[...])
        x1 = x_ref[...] + gx * y
        r1 = lax.rsqrt(jnp.mean(x1 * x1, axis=-1, keepdims=True) + EPS)
        xh1 = x1 * r1
        err = xh1 * fw - t_ref[...]
        loss = jnp.sum(jnp.sum(err * err, axis=1, keepdims=True), axis=0, keepdims=True) * (0.5 / D)
        dout = err * (1.0 / D)
        dxh = dout * fw
        dx1 = r1 * (dxh - xh1 * jnp.mean(dxh * xh1, axis=-1, keepdims=True))
        dx1_ref[...] = dx1
        dy = (dx1 * gx).astype(BF16)
        dmix = _dot_nt(dy, wo_ref[...])
        dya = (dmix * sga).astype(BF16)
        dyb = (dmix * sgb).astype(BF16)
        da = _dot_nt(dya, wa_ref[...])
        db = _dot_nt(dyb, wb_ref[...])
        dpa_ref[:, 0:D] = (da * silu_za * conv).astype(BF16)
        dpa_ref[:, D:2 * D] = (da * bg * conv * (sza * (1.0 + za * (1.0 - sza)))).astype(BF16)
        dpa_ref[:, 2 * D:3 * D] = (db * rg * (szb * (1.0 + zb * (1.0 - szb)))).astype(BF16)
        dpa_ref[:, 3 * D:4 * D] = (dmix * y_a * sga * (1.0 - sga)).astype(BF16)
        dpa_ref[:, 4 * D:5 * D] = (dmix * y_b * sgb * (1.0 - sgb)).astype(BF16)
        dconv_ref[...] = (da * silu_za * bg).astype(BF16)
        drn_n = db * silu_zb
        drn = drn_n * gnw
        rstd = rstd_ref[...]
        for hd in range(N_HEADS):
            sl = slice(hd * DV, (hd + 1) * DV)
            dh_, rh = drn[:, sl], rn[:, sl]
            m1 = jnp.mean(dh_, axis=-1, keepdims=True)
            m2 = jnp.mean(dh_ * rh, axis=-1, keepdims=True)
            dret_ref[:, sl] = (rstd[:, sl] * (dh_ - m1 - rh * m2)).astype(BF16)
        at_ref[0] = a_act.T.astype(BF16)
        at_ref[1] = b_act.T.astype(BF16)
        at_ref[2] = mix.T.astype(BF16)
        b_ref[0] = dya
        b_ref[1] = dyb
        b_ref[2] = dy

        @pl.when(i == 0)
        def _():
            part_ref[...] = jnp.zeros((8, D), F32)

        part_ref[0:1, :] += jnp.sum(dout * xh1, axis=0, keepdims=True)
        part_ref[1:2, :] += jnp.sum(dx1 * y, axis=0, keepdims=True)
        part_ref[2:3, :] += jnp.sum(drn_n * rn, axis=0, keepdims=True)
        part_ref[3:4, :] += jnp.broadcast_to(loss, (1, D))
        writer.send(i, slot)

    col = lambda cb_: pl.BlockSpec((TM, D), lambda i, cb_=cb_: (i, cb_))
    prev = lambda cb_: pl.BlockSpec((HALO, D), lambda i, cb_=cb_: (jnp.maximum(i * hb - 1, 0), cb_))
    nxt = lambda cb_: pl.BlockSpec((HALO, D), lambda i, cb_=cb_: (jnp.minimum((i + 1) * hb, n_tiles * hb - 1), cb_))
    tile = pl.BlockSpec((TM, D), lambda i: (i, 0))
    full = lambda a: pl.BlockSpec(a.shape, lambda i: (0,) * a.ndim, pipeline_mode=pl.Buffered(1))
    return pl.pallas_call(
        body, name="merge", grid=(n_tiles,),
        in_specs=[col(CB_H), col(CB_BG), col(CB_CG), col(CB_ZA), col(CB_ZB), col(CB_GA), col(CB_GB),
                  prev(CB_H), nxt(CB_H), prev(CB_CG), nxt(CB_CG),
                  tile, tile, tile, tile, full(w_a), full(w_b), full(w_out), full(vecs)],
        out_specs=[tile, pl.BlockSpec(memory_space=pl.ANY), tile, tile,
                   pl.BlockSpec((3, D, TM), lambda i: (0, 0, i)), pl.BlockSpec((3, TM, D), lambda i: (0, i, 0)),
                   pl.BlockSpec((8, D), lambda i: (0, 0))],
        out_shape=[jax.ShapeDtypeStruct((seq, D), F32), jax.ShapeDtypeStruct((p_ext.shape[0], PW), BF16),
                   jax.ShapeDtypeStruct((seq, D), BF16), jax.ShapeDtypeStruct((seq, D), BF16),
                   jax.ShapeDtypeStruct((3, D, seq), BF16), jax.ShapeDtypeStruct((3, seq, D), BF16),
                   jax.ShapeDtypeStruct((8, D), F32)],
        scratch_shapes=[pltpu.VMEM((2, TM, 5 * D), BF16), pltpu.SemaphoreType.DMA((2, len(dp_pieces)))],
        compiler_params=_cparams(dimension_semantics=("arbitrary",)),
    )(p_ext, p_ext, p_ext, p_ext, p_ext, p_ext, p_ext, p_ext, p_ext, p_ext, p_ext,
      rn, rstd, x, target, w_a, w_b, w_out, vecs)


def _conv_backward(p_ext, dconv, dp, dq, dk, dv, vecs, seq, after):
    n_tiles = seq // TM
    hb = TM // HALO
    qk = N_HEADS * DK
    pieces = [(0, CB_H * D, D), (D, CB_CG * D, D), (2 * D, CB_QK * D, 2 * D)]
    zero_pieces = [(CB_BG * D, D), (CB_ZA * D, D), (CB_ZB * D, 3 * D)]

    def body(h_ref, cg_ref, dc_ref, dcp_ref, dcn_ref, dq_ref, dk_ref, dv_ref, vec_ref, dp_in, after_ref, dp_hbm, part_ref,
             stage, sems, zeros, zero_sems):
        del dp_in, after_ref
        i = pl.program_id(0)
        writer = _ColumnWriter(dp_hbm, stage, sems, pieces, n_tiles + 1)
        slot = writer.slot_for(i)
        out = stage.at[slot]
        out[:, 2 * D:2 * D + qk] = dq_ref[...]
        out[:, 2 * D + qk:3 * D] = dk_ref[...]
        out[:, 3 * D:4 * D] = dv_ref[...]

        @pl.when(i == 0)
        def _():
            part_ref[...] = jnp.zeros((8, D), F32)

        @pl.when(i == n_tiles)
        def _():
            out[:, 0:2 * D] = jnp.zeros((TM, 2 * D), BF16)
            zeros[...] = jnp.zeros(zeros.shape, BF16)
            rows = pl.ds(n_tiles * TM, TM)
            fills = [pltpu.make_async_copy(zeros.at[:, pl.ds(0, width)], dp_hbm.at[rows, pl.ds(dst, width)],
                                           zero_sems.at[k]) for k, (dst, width) in enumerate(zero_pieces)]
            for cp in fills:
                cp.start()
            for cp in fills:
                cp.wait()

        @pl.when(i < n_tiles)
        def _():
            f = lambda ref: ref[...].astype(F32)
            h, cg, dc = f(h_ref), f(cg_ref), f(dc_ref)
            w0, w1, w2 = vec_ref[1:2, :], vec_ref[2:3, :], vec_ref[3:4, :]
            row = lax.broadcasted_iota(jnp.int32, (TM, 1), 0)
            dc_prev = jnp.where(i == 0, 0.0, f(dcp_ref)[HALO - 1:HALO, :])
            dc_next = jnp.where(i == n_tiles - 1, 0.0, f(dcn_ref)[0:1, :])
            dc_up = jnp.where(row == 0, dc_prev, pltpu.roll(dc, 1, 0))
            dc_dn = jnp.where(row == TM - 1, dc_next, pltpu.roll(dc, TM - 1, 0))
            du = w0 * dc_dn + w1 * dc + w2 * dc_up
            u = cg * h
            part_ref[0:1, :] += jnp.sum(u * dc_dn, axis=0, keepdims=True)
            part_ref[1:2, :] += jnp.sum(u * dc, axis=0, keepdims=True)
            part_ref[2:3, :] += jnp.sum(u * dc_up, axis=0, keepdims=True)
            part_ref[3:4, :] += jnp.sum(dc, axis=0, keepdims=True)
            out[:, 0:D] = (du * cg).astype(BF16)
            out[:, D:2 * D] = (du * h).astype(BF16)

        writer.send(i, slot)

    last = n_tiles - 1
    col = lambda cb_: pl.BlockSpec((TM, D), lambda i, cb_=cb_: (jnp.minimum(i, last), cb_))
    lat = lambda w: pl.BlockSpec((TM, w), lambda i: (jnp.minimum(i, last), 0))
    ext = lambda w: pl.BlockSpec((TM, w), lambda i: (i, 0))
    anyspec = pl.BlockSpec(memory_space=pl.ANY)
    return pl.pallas_call(
        body, name="conv_backward", grid=(n_tiles + 1,),
        in_specs=[col(CB_H), col(CB_CG), lat(D),
                  pl.BlockSpec((HALO, D), lambda i: (jnp.clip(i * hb - 1, 0, n_tiles * hb - 1), 0)),
                  pl.BlockSpec((HALO, D), lambda i: (jnp.minimum((i + 1) * hb, n_tiles * hb - 1), 0)),
                  ext(qk), ext(qk), ext(D), pl.BlockSpec(vecs.shape, lambda i: (0, 0)), anyspec, _AFTER_SPEC],
        out_specs=[anyspec, pl.BlockSpec((8, D), lambda i: (0, 0))],
        out_shape=[jax.ShapeDtypeStruct(dp.shape, BF16), jax.ShapeDtypeStruct((8, D), F32)],
        input_output_aliases={9: 0},
        scratch_shapes=[pltpu.VMEM((2, TM, 4 * D), BF16), pltpu.SemaphoreType.DMA((2, len(pieces))),
                        pltpu.VMEM((TM, 3 * D), BF16), pltpu.SemaphoreType.DMA((len(zero_pieces),))],
        compiler_params=_cparams(dimension_semantics=("arbitrary",)),
    )(p_ext, p_ext, dconv, dconv, dconv, dq, dk, dv, vecs, dp, after)


def _input_backward(dp, shards, ids, x, ctx, dx1, modx, modc, norm_w, after):
    seq = x.shape[0]
    lext = seq + ctx.shape[0]
    n_x = seq // TM
    n_w = len(shards.arrays)

    def body(ids_ref, dp_ref, x_ref, ctx_ref, dx1_ref, mx_ref, mc_ref, nw_ref, *rest):
        w_hbm, (_, gx_ref, part_ref, w_ref, w_sems) = rest[:n_w], rest[n_w:]
        i = pl.program_id(0)
        _load_w_in(i, ids_ref, shards, w_hbm, w_ref, w_sems)
        is_ctx = i >= n_x
        dxm = _dot_nt(dp_ref[...], w_ref[...])
        x = jnp.where(is_ctx, ctx_ref[...], x_ref[...])
        r = lax.rsqrt(jnp.mean(x * x, axis=-1, keepdims=True) + EPS)
        xh = x * r
        nw = nw_ref[...]
        sc = jnp.where(is_ctx, mc_ref[1:2, :], mx_ref[1:2, :])
        dxn = dxm * (1.0 + sc)
        dxh = dxn * nw
        dx = r * (dxh - xh * jnp.mean(dxh * xh, axis=-1, keepdims=True))

        @pl.when(jnp.logical_not(is_ctx))
        def _():
            gx_ref[...] = dx1_ref[...] + dx

        @pl.when(i == 0)
        def _():
            part_ref[...] = jnp.zeros((8, D), F32)

        fx = jnp.where(is_ctx, 0.0, 1.0)
        d_shift = jnp.sum(dxm, axis=0, keepdims=True)
        d_scale = jnp.sum(dxm * (xh * nw), axis=0, keepdims=True)
        part_ref[0:1, :] += fx * d_shift
        part_ref[1:2, :] += fx * d_scale
        part_ref[2:3, :] += jnp.sum(dxn * xh, axis=0, keepdims=True)
        part_ref[3:4, :] += (1.0 - fx) * d_shift
        part_ref[4:5, :] += (1.0 - fx) * d_scale

    lat = lambda w: pl.BlockSpec((TM, w), lambda i, ids_: (jnp.minimum(i, n_x - 1), 0))
    ext = lambda w: pl.BlockSpec((TM, w), lambda i, ids_: (i, 0))
    full = lambda a: pl.BlockSpec(a.shape, lambda i, ids_: (0,) * a.ndim)
    grid_spec = pltpu.PrefetchScalarGridSpec(
        num_scalar_prefetch=1, grid=(lext // TM,),
        in_specs=[ext(PW), lat(D), full(ctx), lat(D), full(modx), full(modc), full(norm_w)]
        + [pl.BlockSpec(memory_space=pl.ANY)] * n_w + [_AFTER_SPEC],
        out_specs=[lat(D), pl.BlockSpec((8, D), lambda i, ids_: (0, 0))],
        scratch_shapes=[pltpu.VMEM((D, PW), BF16), pltpu.SemaphoreType.DMA((N_DEV,))])
    return pl.pallas_call(
        body, name="input_backward", grid_spec=grid_spec,
        out_shape=[jax.ShapeDtypeStruct((seq, D), F32), jax.ShapeDtypeStruct((8, D), F32)],
        compiler_params=_cparams(dimension_semantics=("arbitrary",)),
    )(ids, dp, x, ctx, dx1, modx, modc, norm_w, *shards.arrays, after)


def _weight_grad_in(xm_t, dp, owners, name, after):
    lext = xm_t.shape[1]

    def body(own_ref, a_ref, b_ref, after_ref, o_ref):
        o_ref[0] = _dot(a_ref[...], b_ref[...])

    grid_spec = pltpu.PrefetchScalarGridSpec(
        num_scalar_prefetch=1, grid=(4,),
        in_specs=[pl.BlockSpec((D, lext), lambda j, own: (0, 0)),
                  pl.BlockSpec((lext, WSH), lambda j, own: (0, own[j])), _AFTER_SPEC],
        out_specs=pl.BlockSpec((1, D, WSH), lambda j, own: (j, 0, 0)))
    return pl.pallas_call(
        body, name=name, grid_spec=grid_spec,
        out_shape=jax.ShapeDtypeStruct((4, D, WSH), F32),
        compiler_params=_cparams(dimension_semantics=("arbitrary",)),
    )(owners, xm_t, dp, after)


def _weight_grad_square(a_t, b):
    seq = a_t.shape[2]

    def body(a_ref, b_ref, o_ref):
        o_ref[:, 0] = _dot(a_ref[0], b_ref[0]).reshape(N_DEV, RSH, D)

    return pl.pallas_call(
        body, name="weight_grad_square", grid=(3,),
        in_specs=[pl.BlockSpec((1, D, seq), lambda t: (t, 0, 0)), pl.BlockSpec((1, seq, D), lambda t: (t, 0, 0))],
        out_specs=pl.BlockSpec((N_DEV, 1, RSH, D), lambda t: (0, t, 0, 0)),
        out_shape=jax.ShapeDtypeStruct((N_DEV, 3, RSH, D), F32),
        compiler_params=_cparams(dimension_semantics=("arbitrary",)),
    )(a_t, b)


def _rows(total, *parts):
    width = max(a.shape[1] for _, a in parts)
    out = None
    for row, a in parts:
        padded = jnp.pad(a, ((row, total - row - a.shape[0]), (0, width - a.shape[1])))
        out = padded if out is None else out + padded
    return out


def kernel(x, c, ctx, c_ctx, norm_w, ada_w, ada_b, w_in, conv_w, conv_b, decay_logit, gn_w, w_a, w_b, w_out, final_norm_w, loss_target, m_c_ctx, m_norm_w, m_ada_w, m_ada_b, m_w_in, m_conv_w, m_conv_b, m_decay_logit, m_gn_w, m_w_a, m_w_b, m_w_out, m_final_norm_w, v_c_ctx, v_norm_w, v_ada_w, v_ada_b, v_w_in, v_conv_w, v_conv_b, v_decay_logit, v_gn_w, v_w_a, v_w_b, v_w_out, v_final_norm_w):
    xi, yi, ci = _coords()
    me = 4 * xi + 2 * yi + ci
    chip = 2 * xi + yi
    seq, ctx_len = x.shape[1], ctx.shape[1]
    assert seq % TM == 0 and seq % RET_C == 0 and ctx_len == TM and seq % GRID_W == 0
    csh = D // N_DEV

    blk = jnp.pad(c, ((0, 7), (0, 0))) + jnp.pad(conv_w[0], ((1, 4), (0, D - csh)))
    got, _ = _all_gather_small(blk, "gather_cond")
    conv_w_all = got[:, 1:4, 0:csh].transpose(1, 0, 2).reshape(3, D)
    c16 = _rows(16, (0, got[:, 0, :]), (N_DEV, c_ctx[None]))
    ada_b_sh = lax.dynamic_slice(ada_b, (0, me * ADA_SH), (1, ADA_SH))
    mod_sh, act16, lg = _modulation(c16, ada_w[0], ada_b_sh, decay_logit[0])
    mod_all, small_done = _all_gather_small(mod_sh, "gather_mod")
    mod_all = mod_all.transpose(1, 0, 2).reshape(16, 3 * D)
    modx = lax.dynamic_slice(mod_all, (me, 0), (1, 3 * D)).reshape(3, D)
    modc = mod_all[8].reshape(3, D)

    x2, tgt = x[0], loss_target[0]
    wblock = lambda n: jax.ShapeDtypeStruct((n, D, WSH), BF16)
    ids_of = lambda ks: jnp.stack([jnp.bitwise_xor(me, k) for k in ks]).astype(jnp.int32)
    own_shard = w_in[0].astype(BF16)[None]
    sq_shards = [w[0].astype(BF16) for w in (w_a, w_b, w_out)]
    wave1 = _Exchange(
        "gather_in_first", [jnp.broadcast_to(own_shard, (2, D, WSH)), own_shard], [wblock(4)], 3,
        lambda srcs, lands: _forward_plan([(0, 1)])(srcs, lands) + _send_plan((4, 2))(srcs[1:], lands),
        after=small_done)
    xm, xm_t, cos_t, sin_t = _prenorm(x2, ctx[0], modx, modc, norm_w, wave1.token)
    (w_pair, _), _ = wave1.wait(xm_t, 1, through=[0])
    first = _Shards([w_pair], [(0, 0), (0, 1)])
    p_ext = _in_projection(None, xm, cos_t, sin_t, w_pair, [((0, 0), (1, 1))], me, "in_projection_pair")
    _, (w_nbr,) = wave1.wait(p_ext, through=[1, 2])
    wave2 = _Exchange(
        "gather_in_second", [w_nbr, own_shard], [wblock(2)], 3,
        lambda srcs, lands: _forward_plan([(0, 2), (1, 3)])(srcs, lands) + _send_plan((6,))(srcs[1:], lands))
    (w_nbr, _), _ = wave2.wait(wave2.token, 2, through=[0])
    second = _Shards([w_nbr], [(0, 0), (0, 1), (0, 2), (0, 3)])
    p_ext = _in_projection(p_ext, xm, cos_t, sin_t, w_nbr, [((0, 4), (2, 5)), ((1, 2), (3, 3))], me,
                           "in_projection_neighbours")
    _, (w_diag,) = wave2.wait(p_ext, through=[1, 2])
    wave3 = _Exchange(
        "gather_in_third", [w_diag] + sq_shards, [jax.ShapeDtypeStruct((N_DEV, RSH, D), BF16)] * 3, 1 + 3 * (N_DEV - 1),
        lambda srcs, lands: _forward_plan([(0, 1)])(srcs, lands) + _gather_plan(srcs[1:], lands))
    (w_diag, *_), _ = wave3.wait(wave3.token, 1, through=[0])
    third = _Shards([w_diag], [(0, 0), (0, 1)])
    p_ext = _in_projection(p_ext, xm, cos_t, sin_t, w_diag, [((0, 6), (1, 7))], me, "in_projection_diagonal")
    w_shards, w_ids = first + second + third, ids_of((0, 1, 4, 2, 5, 3, 6, 7))

    rn, rstd = _retention_forward(p_ext, lg, seq, ctx_len)
    _, sq_lands = wave3.wait(rstd, through=range(1, 7))
    w_a_all, w_b_all, w_out_all = (
        lax.dynamic_update_slice(land, shard[None], (me, 0, 0)).reshape(D, D) for land, shard in zip(sq_lands, sq_shards))
    vecs = _rows(8, (0, modx[2:3]), (1, conv_w_all), (4, conv_b), (5, gn_w), (6, final_norm_w[None]))
    dx1, dp, dconv, dret, op_at, op_b, part_m = _merge(p_ext, rn, rstd, x2, tgt, w_a_all, w_b_all, w_out_all, vecs, seq)

    j4 = jnp.arange(4, dtype=jnp.int32)
    owners = (2 * jnp.bitwise_xor(chip, j4) + ci).astype(jnp.int32)
    owners_sib = (2 * jnp.bitwise_xor(chip, j4) + (1 - ci)).astype(jnp.int32)
    gw_sq = _weight_grad_square(op_at, op_b).reshape(N_DEV, 3 * RSH, D)
    rs_sq_pair = _Exchange("rs_square_pair", [gw_sq], [jax.ShapeDtypeStruct((4, 3 * RSH, D), F32)], 4,
                           _pair_plan(lambda j, chip_, c_: 2 * jnp.bitwise_xor(chip_, j) + (1 - c_)))
    dq, dk, dv, dlg = _retention_backward(p_ext, dret, lg, cos_t, sin_t, seq, ctx_len, rs_sq_pair.token)
    (gw_sq,), (r1_sq,) = rs_sq_pair.wait(dlg)
    own_sq, send_sq = _pair_sum(gw_sq, r1_sq, owners, "pair_sum_square")
    rs_sq_chips = _Exchange("rs_square_chips", [send_sq], [jax.ShapeDtypeStruct((3, 3 * RSH, D), BF16)], 3, _chips_plan)
    dp, part_c = _conv_backward(p_ext, dconv, dp, dq, dk, dv, vecs, seq, rs_sq_chips.token)
    gw_sib = _weight_grad_in(xm_t, dp, owners_sib, "weight_grad_in_sibling", part_c)
    rs_in_pair = _Exchange("rs_in_pair", [gw_sib], [jax.ShapeDtypeStruct((4, D, WSH), F32)], 4,
                           _pair_plan(lambda j, chip_, c_: j))
    gw_own = _weight_grad_in(xm_t, dp, owners, "weight_grad_in_own", rs_in_pair.token)
    _, (r1_in,) = rs_in_pair.wait(gw_own)
    own_in, send_in = _pair_sum(gw_own, r1_in, j4, "pair_sum_in")
    rs_in_chips = _Exchange("rs_in_chips", [send_in], [jax.ShapeDtypeStruct((3, D, WSH), BF16)], 3, _chips_plan)
    grad_x, part_i = _input_backward(dp, w_shards, w_ids, x2, ctx[0], dx1, modx, modc, norm_w, rs_in_chips.token)

    dl = dlg[:, 0, 0:4]
    dlg_row = jnp.pad(dl[:, 0::2].reshape(1, N_HEADS), ((0, 0), (0, D - N_HEADS))) + jnp.pad(
        dl[:, 1::2].reshape(1, N_HEADS), ((0, 0), (N_HEADS, D - 2 * N_HEADS)))
    partials = _rows(16, (0, part_m[0:1]), (1, part_i[2:3]), (2, part_c[3:4]), (3, part_m[2:3]), (4, part_c[0:3]),
                     (7, part_i[0:2]), (9, part_m[1:2]), (10, part_i[3:5]), (13, part_m[3:4]), (14, dlg_row))
    results = {}
    gather_part = _Exchange("gather_partials", [partials], [jax.ShapeDtypeStruct((N_DEV, 16, D), F32)],
                            N_DEV - 1, _gather_plan)
    _, (r2_in,) = rs_in_chips.wait(gather_part.token)
    results["w_in"] = [o[None] for o in _adam_sharded(w_in[0], m_w_in[0], v_w_in[0], own_in, r2_in, "adam_w_in")]
    _, (got,) = gather_part.wait(results["w_in"][1])
    got = lax.dynamic_update_slice(got, partials[None], (me, 0, 0))
    tot = _sum_devices(got, "sum_partials")
    dmodc = tot[10:13].reshape(1, 3 * D)
    dmod16 = _rows(16, (0, got[:, 7:10, :].reshape(N_DEV, 3 * D)), (N_DEV, dmodc))
    dmod16 = lax.dynamic_slice(dmod16, (0, me * ADA_SH), (16, ADA_SH))
    dmodc8 = _rows(8, (0, dmod16[8:9]))
    g_ada_w, d_ada_w, nm_ada_w, nv_ada_w, cctx_part = _ada_backward(
        act16.T, dmod16, dmodc8, ada_w[0], m_ada_w[0], v_ada_w[0])
    gather_cctx = _Exchange("gather_cctx", [cctx_part], [jax.ShapeDtypeStruct((N_DEV, 8, D), F32)],
                            N_DEV - 1, _gather_plan)
    _, (r2_sq,) = rs_sq_chips.wait(gather_cctx.token)
    square = _adam_square([(w_a[0], m_w_a[0], v_w_a[0]), (w_b[0], m_w_b[0], v_w_b[0]),
                           (w_out[0], m_w_out[0], v_w_out[0])], own_sq, r2_sq)
    _, (got_cctx,) = gather_cctx.wait(square[0][1])
    dact_ctx = _sum_devices(lax.dynamic_update_slice(got_cctx, cctx_part[None], (me, 0, 0)), "sum_cctx")

    view = {"c_ctx": (1, D), "final_norm_w": (1, D)}
    given = {"c_ctx": (c_ctx, m_c_ctx, v_c_ctx), "norm_w": (norm_w, m_norm_w, v_norm_w),
             "conv_b": (conv_b, m_conv_b, v_conv_b), "gn_w": (gn_w, m_gn_w, v_gn_w),
             "final_norm_w": (final_norm_w, m_final_norm_w, v_final_norm_w), "ada_b": (ada_b, m_ada_b, v_ada_b),
             "decay_logit": (decay_logit, m_decay_logit, v_decay_logit), "conv_w": (conv_w, m_conv_w, v_conv_w)}
    small_in = [tuple(a.reshape(view.get(name, a.shape)) for a in given[name]) for name in SMALL_PARAMS]
    conv_w_grad = lax.dynamic_slice(tot, (4, me * csh), (3, csh))
    small_out = _adam_small(tot, dact_ctx, conv_w_grad, tot[14, 0:2 * N_HEADS].reshape(2, N_HEADS), small_in)
    results.update({name: [o.reshape(given[name][0].shape) for o in outs_]
                    for name, outs_ in zip(SMALL_PARAMS, small_out)})
    for name, outs_ in zip(("w_a", "w_b", "w_out"), square):
        results[name] = [o[None] for o in outs_]
    results["ada_w"] = [o[None] for o in (g_ada_w, d_ada_w, nm_ada_w, nv_ada_w)]

    order = ("c_ctx", "norm_w", "ada_w", "ada_b", "w_in", "conv_w", "conv_b", "decay_logit", "gn_w",
             "w_a", "w_b", "w_out", "final_norm_w")
    outs = [results[name][kind] for kind in range(4) for name in order]
    return (tot[13, 0], grad_x[None], *outs)
```

```python
import math

import jax
import jax.numpy as jnp
from jax import lax
from jax.experimental import pallas as pl
from jax.experimental.pallas import tpu as pltpu

F32 = jnp.float32
BF16 = jnp.bfloat16
MESH = pl.DeviceIdType.MESH
_HBM_SPEC = pl.BlockSpec(memory_space=pltpu.HBM)
_SEM_SPEC = pl.BlockSpec(memory_space=pltpu.SEMAPHORE)
_EFFECT = pltpu.SideEffectType.DATAFLOW_SIDE_EFFECTING

N_DEV = 8
D = 1024
N_HEADS = 8
DK = 64
DV = 128
GRID_W = 64
ROPE_BASE = 10000.0
EPS = 1e-6
PW = 9 * D
WSH = PW // N_DEV
RSH = D // N_DEV
ADA_SH = 3 * D // N_DEV
TM = 256
RET_C = 256
HALO = 16
LANES = 128
VMEM_LIMIT = 60 * 1024 * 1024

ADAM_LR = 0.001
ADAM_B1 = 0.9
ADAM_B2 = 0.999
ADAM_EPS = 1e-08
ADAM_WD = 0.01
ADAM_STEP = 10

CB_H, CB_BG, CB_CG, CB_ZA, CB_QK, CB_V, CB_ZB, CB_GA, CB_GB = range(9)


def _cparams(**kw):
    return pltpu.CompilerParams(vmem_limit_bytes=VMEM_LIMIT, **kw)


def _dot(a, b):
    return jnp.dot(a, b, preferred_element_type=F32)


def _dot_nt(a, b):
    return lax.dot_general(a, b, (((1,), (1,)), ((), ())), preferred_element_type=F32)


def _dot_tn(a, b):
    return lax.dot_general(a, b, (((0,), (0,)), ((), ())), preferred_element_type=F32)


def _sigmoid(z):
    return 0.5 * jnp.tanh(0.5 * z) + 0.5


def _row_tile(rows):
    return TM if rows % TM == 0 else rows


def _coords():
    return lax.axis_index("x"), lax.axis_index("y"), lax.axis_index("c")


def _flip(v, bit):
    return 1 - v if bit else v


def _all_gather_small(blk, name):
    rows, cols = blk.shape

    def body(x_ref, out_ref, done, send_sems, recv_sems, local_sem):
        x, y, c = _coords()
        me = 4 * x + 2 * y + c
        mine = pltpu.make_async_copy(x_ref, out_ref.at[me], local_sem)
        mine.start()

        def copy(k, slot):
            peer = (_flip(x, k & 4), _flip(y, k & 2), _flip(c, k & 1))
            return pltpu.make_async_remote_copy(
                src_ref=x_ref, dst_ref=out_ref.at[slot], send_sem=send_sems.at[k - 1],
                recv_sem=recv_sems.at[k - 1], device_id=peer, device_id_type=MESH)

        for k in range(1, N_DEV):
            copy(k, me).start()
        for k in range(1, N_DEV):
            copy(k, jnp.bitwise_xor(me, k)).wait_recv()
        for k in range(1, N_DEV):
            copy(k, me).wait_send()
        mine.wait()
        done[...] = jnp.zeros((8, LANES), F32)

    vmem = pl.BlockSpec(memory_space=pltpu.VMEM)
    return pl.pallas_call(
        body, name=name,
        out_shape=[jax.ShapeDtypeStruct((N_DEV, rows, cols), blk.dtype), jax.ShapeDtypeStruct((8, LANES), F32)],
        in_specs=[vmem], out_specs=[vmem, vmem],
        scratch_shapes=[pltpu.SemaphoreType.DMA((N_DEV - 1,)), pltpu.SemaphoreType.DMA((N_DEV - 1,)),
                        pltpu.SemaphoreType.DMA],
    )(blk)


class _Absent:
    at = property(lambda self: self)

    def __getitem__(self, index):
        return self


class _Exchange:
    def __init__(self, name, srcs, land_shapes, n_copies, plan, after=None):
        self.name, self.plan, self.n_copies = name, plan, n_copies
        self.n_src, self.n_land = len(srcs), len(land_shapes)
        hbm = lambda a: pltpu.HBM(a.shape, a.dtype)
        n = self.n_src + self.n_land
        lands = [pltpu.with_memory_space_constraint(lax.empty(s.shape, s.dtype), pltpu.HBM) for s in land_shapes]
        srcs = [pltpu.with_memory_space_constraint(s, pltpu.HBM) for s in srcs]
        extra = [] if after is None else [after]

        def body(*refs):
            send_sems, recv_sems = refs[n + len(extra)], refs[n + len(extra) + 1]
            for cp in self._copies(refs, send_sems, recv_sems):
                cp.start()
            refs[-1][...] = jnp.zeros((8, LANES), F32)

        outs = pl.pallas_call(
            body, name=name + "_start",
            out_shape=(pltpu.SemaphoreType.DMA((n_copies,)), pltpu.SemaphoreType.DMA((n_copies,)),
                       *[hbm(a) for a in srcs], *[hbm(a) for a in lands], jax.ShapeDtypeStruct((8, LANES), F32)),
            in_specs=[_HBM_SPEC] * n + [pl.BlockSpec(memory_space=pl.ANY)] * len(extra),
            out_specs=(_SEM_SPEC, _SEM_SPEC, *[_HBM_SPEC] * n, pl.BlockSpec(memory_space=pltpu.VMEM)),
            input_output_aliases={i: 2 + i for i in range(n)},
            compiler_params=pltpu.CompilerParams(has_side_effects=_EFFECT),
        )(*srcs, *lands, *extra)
        self.send_sems, self.recv_sems = outs[0], outs[1]
        self.buffers = list(outs[2:2 + n])
        self.token = outs[-1]
        self.waited = 0

    def _copies(self, refs, send_sems, recv_sems, lo=0, hi=None):
        src_refs, land_refs = refs[:self.n_src], refs[self.n_src:self.n_src + self.n_land]
        planned = list(enumerate(self.plan(src_refs, land_refs)))[lo:hi]
        return [pltpu.make_async_remote_copy(src_ref=s, dst_ref=d, send_sem=send_sems.at[k], recv_sem=recv_sems.at[k],
                                             device_id=dev, device_id_type=MESH) for k, (s, d, dev) in planned]

    def wait(self, after, count=None, through=None):
        n = self.n_src + self.n_land
        lo = self.waited
        hi = self.n_copies if count is None else lo + count
        self.waited = hi
        through = list(range(n)) if through is None else list(through)
        m = len(through)

        def body(*refs):
            held = [_Absent()] * n
            for pos, ref in zip(through, refs[:m]):
                held[pos] = ref
            for cp in self._copies(held, refs[m], refs[m + 1], lo, hi):
                cp.wait_send()
                cp.wait_recv()

        outs = pl.pallas_call(
            body, name=f"{self.name}_wait{lo}" if lo or hi < self.n_copies else self.name + "_wait",
            out_shape=tuple(pltpu.HBM(self.buffers[p].shape, self.buffers[p].dtype) for p in through),
            in_specs=[_HBM_SPEC] * m + [_SEM_SPEC, _SEM_SPEC, pl.BlockSpec(memory_space=pl.ANY)],
            out_specs=tuple([_HBM_SPEC] * m),
            input_output_aliases={i: i for i in range(m)},
            compiler_params=pltpu.CompilerParams(has_side_effects=_EFFECT),
        )(*[self.buffers[p] for p in through], self.send_sems, self.recv_sems, after)
        for pos, out in zip(through, outs):
            self.buffers[pos] = out
        return list(self.buffers[:self.n_src]), list(self.buffers[self.n_src:])


def _pair_plan(src_index):
    def plan(srcs, lands):
        x, y, c = _coords()
        chip = 2 * x + y
        return [(srcs[0].at[src_index(j, chip, c)], lands[0].at[j], (x, y, 1 - c)) for j in range(4)]
    return plan


def _chips_plan(srcs, lands):
    x, y, c = _coords()
    return [(srcs[0].at[j - 1], lands[0].at[j - 1], (_flip(x, j & 2), _flip(y, j & 1), c)) for j in range(1, 4)]


def _peer(k):
    x, y, c = _coords()
    return _flip(x, k & 4), _flip(y, k & 2), _flip(c, k & 1)


def _send_plan(ks):
    def plan(srcs, lands):
        return [(srcs[0].at[0], lands[0].at[p], _peer(k)) for p, k in enumerate(ks)]
    return plan


def _forward_plan(moves):
    def plan(srcs, lands):
        return [(srcs[0].at[s], srcs[0].at[d], _peer(1)) for s, d in moves]
    return plan


def _gather_plan(srcs, lands):
    x, y, c = _coords()
    me = 4 * x + 2 * y + c
    return [(srcs[a], lands[a].at[me], (_flip(x, k & 4), _flip(y, k & 2), _flip(c, k & 1)))
            for a in range(len(srcs)) for k in range(1, N_DEV)]


def _pair_sum(grad, recv, owners, name):
    _, rows, cols = grad.shape
    tr = rows

    def body(own_ref, g_ref, r_ref, mine_ref, send_ref):
        j = pl.program_id(1)
        total = g_ref[...] + r_ref[...]

        @pl.when(j == 0)
        def _():
            mine_ref[...] = total

        @pl.when(j > 0)
        def _():
            send_ref[...] = total.astype(BF16)

    grid_spec = pltpu.PrefetchScalarGridSpec(
        num_scalar_prefetch=1, grid=(rows // tr, 4),
        in_specs=[pl.BlockSpec((1, tr, cols), lambda i, j, own: (own[j], i, 0)),
                  pl.BlockSpec((1, tr, cols), lambda i, j, own: (j, i, 0))],
        out_specs=[pl.BlockSpec((1, tr, cols), lambda i, j, own: (0, i, 0)),
                   pl.BlockSpec((1, tr, cols), lambda i, j, own: (jnp.maximum(j - 1, 0), i, 0))])
    return pl.pallas_call(
        body, name=name, grid_spec=grid_spec,
        out_shape=[jax.ShapeDtypeStruct((1, rows, cols), F32), jax.ShapeDtypeStruct((3, rows, cols), BF16)],
        compiler_params=_cparams(dimension_semantics=("arbitrary", "arbitrary")),
    )(owners, grad, recv)


def _modulation(c16, ada_w_sh, ada_b_sh, decay_logit):
    def body(c_ref, w_ref, b_ref, dl_ref, mod_ref, act_ref, lg_ref):
        cv = c_ref[...]
        act = cv * _sigmoid(cv)
        act_ref[...] = act
        mod_ref[...] = jnp.dot(act, w_ref[...], preferred_element_type=F32,
                               precision=lax.Precision.HIGHEST) + b_ref[...]
        z = dl_ref[...]
        lg_ref[...] = jnp.minimum(z, 0.0) - jnp.log(1.0 + jnp.exp(-jnp.abs(z)))

    return pl.pallas_call(
        body, name="modulation",
        out_shape=[jax.ShapeDtypeStruct((16, ADA_SH), F32), jax.ShapeDtypeStruct((16, D), F32),
                   jax.ShapeDtypeStruct(decay_logit.shape, F32)],
        compiler_params=_cparams(),
    )(c16, ada_w_sh, ada_b_sh, decay_logit)


def _adam_update(w, g, m, v):
    m2 = ADAM_B1 * m + (1.0 - ADAM_B1) * g
    v2 = ADAM_B2 * v + (1.0 - ADAM_B2) * (g * g)
    m_hat = m2 / (1.0 - ADAM_B1 ** ADAM_STEP)
    v_hat = v2 / (1.0 - ADAM_B2 ** ADAM_STEP)
    delta = -ADAM_LR * (m_hat / (jnp.sqrt(v_hat) + ADAM_EPS) + ADAM_WD * w)
    return delta, m2, v2


def _adam_sharded(w, m, v, own, recv, name):
    rows, cols = w.shape
    tr = _row_tile(rows)

    def body(w_ref, m_ref, v_ref, g0, g1, g2, g3, g_ref, d_ref, m_out, v_out):
        g = ((g0[0] + g1[0].astype(F32)) + g2[0].astype(F32)) + g3[0].astype(F32)
        delta, m2, v2 = _adam_update(w_ref[...], g, m_ref[...], v_ref[...])
        g_ref[...] = g
        d_ref[...] = delta
        m_out[...] = m2
        v_out[...] = v2

    flat = pl.BlockSpec((tr, cols), lambda i: (i, 0))
    part = lambda j: pl.BlockSpec((1, tr, cols), lambda i, j=j: (j, i, 0))
    return pl.pallas_call(
        body, name=name, grid=(rows // tr,),
        in_specs=[flat, flat, flat, part(0), part(0), part(1), part(2)],
        out_specs=[flat] * 4,
        out_shape=[jax.ShapeDtypeStruct((rows, cols), F32)] * 4,
        compiler_params=_cparams(dimension_semantics=("arbitrary",)),
    )(w, m, v, own, recv, recv, recv)


def _ada_backward(act_t, dmod16, dmodc8, ada_w_sh, m, v):
    def body(at_ref, dm_ref, dc_ref, w_ref, m_ref, v_ref, g_ref, d_ref, m_out, v_out, pc_ref):
        g = jnp.dot(at_ref[...], dm_ref[...], preferred_element_type=F32, precision=lax.Precision.HIGHEST)
        w = w_ref[...]
        delta, m2, v2 = _adam_update(w, g, m_ref[...], v_ref[...])
        g_ref[...] = g
        d_ref[...] = delta
        m_out[...] = m2
        v_out[...] = v2
        pc_ref[...] = lax.dot_general(dc_ref[...], w, (((1,), (1,)), ((), ())), preferred_element_type=F32,
                                      precision=lax.Precision.HIGHEST)

    return pl.pallas_call(
        body, name="ada_backward",
        out_shape=[jax.ShapeDtypeStruct((D, ADA_SH), F32)] * 4 + [jax.ShapeDtypeStruct((8, D), F32)],
        compiler_params=_cparams(),
    )(act_t, dmod16, dmodc8, ada_w_sh, m, v)


def _sum_devices(gathered, name):
    _, rows, cols = gathered.shape

    def body(g_ref, o_ref):
        acc = g_ref[0]
        for d in range(1, N_DEV):
            acc = acc + g_ref[d]
        o_ref[...] = acc

    return pl.pallas_call(body, name=name, out_shape=jax.ShapeDtypeStruct((rows, cols), F32),
                          compiler_params=_cparams())(gathered)


SMALL_PARAMS = ("c_ctx", "norm_w", "conv_b", "gn_w", "final_norm_w", "ada_b", "decay_logit", "conv_w")


def _adam_small(tot, dact_ctx, conv_w_grad, decay_grad, params):
    n = len(SMALL_PARAMS)

    def body(tot_ref, dact_ref, cwg_ref, dlg_ref, *refs):
        ins, outs = refs[:3 * n], refs[3 * n:]
        raw = {
            "c_ctx": dact_ref[0:1, :],
            "norm_w": tot_ref[1:2, :], "conv_b": tot_ref[2:3, :], "gn_w": tot_ref[3:4, :],
            "final_norm_w": tot_ref[0:1, :],
            "ada_b": jnp.concatenate([tot_ref[7 + r:8 + r, :] + tot_ref[10 + r:11 + r, :] for r in range(3)], axis=1),
            "decay_logit": dlg_ref[...],
            "conv_w": cwg_ref[...],
        }
        for k, name in enumerate(SMALL_PARAMS):
            lead = (0,) if len(ins[3 * k].shape) == 3 else ()
            w, m, v = (ref[lead + (...,)] for ref in ins[3 * k:3 * k + 3])
            g = raw[name]
            if name == "c_ctx":
                s = _sigmoid(w)
                g = g * (s * (1.0 + w * (1.0 - s)))
            elif name == "decay_logit":
                g = g * (1.0 - _sigmoid(w))
            delta, m2, v2 = _adam_update(w, g, m, v)
            for ref, val in zip(outs[4 * k:4 * k + 4], (g, delta, m2, v2)):
                ref[lead + (...,)] = val

    flat = [a for trio in params for a in trio]
    out_shape = [jax.ShapeDtypeStruct(trio[0].shape, F32) for trio in params for _ in range(4)]
    outs = pl.pallas_call(body, name="adam_small", out_shape=out_shape, compiler_params=_cparams())(
        tot, dact_ctx, conv_w_grad, decay_grad, *flat)
    return [outs[4 * k:4 * k + 4] for k in range(n)]


def _adam_square(params, own, recv):
    def body(own_ref, recv_ref, *refs):
        ins, outs = refs[:9], refs[9:]
        for k in range(3):
            rows = slice(k * RSH, (k + 1) * RSH)
            g = own_ref[0, rows, :]
            for j in range(3):
                g = g + recv_ref[j, rows, :].astype(F32)
            delta, m2, v2 = _adam_update(ins[3 * k][...], g, ins[3 * k + 1][...], ins[3 * k + 2][...])
            for ref, val in zip(outs[4 * k:4 * k + 4], (g, delta, m2, v2)):
                ref[...] = val

    flat = [a for trio in params for a in trio]
    outs = pl.pallas_call(body, name="adam_square", out_shape=[jax.ShapeDtypeStruct((RSH, D), F32)] * 12,
                          compiler_params=_cparams())(own, recv, *flat)
    return [outs[4 * k:4 * k + 4] for k in range(3)]


def _rope(t, cos, sin):
    lane = lax.broadcasted_iota(jnp.int32, (1, LANES), 1)
    first_half = jnp.bitwise_and(lane, DK // 2) == 0
    partner = jnp.where(first_half, pltpu.roll(t, LANES - DK // 2, 1), pltpu.roll(t, DK // 2, 1))
    return t * cos + partner * sin


class _Shards:
    def __init__(self, arrays, picks):
        self.arrays, self.picks = list(arrays), list(picks)

    def __add__(self, other):
        shift = len(self.arrays)
        return _Shards(self.arrays + other.arrays, self.picks + [(a + shift, s) for a, s in other.picks])


def _load_w_in(step, ids_ref, shards, w_refs, w_vmem, sems):
    @pl.when(step == 0)
    def _():
        copies = []
        for p, (a, slot) in enumerate(shards.picks):
            col = pl.multiple_of(ids_ref[p] * WSH, LANES)
            copies.append(pltpu.make_async_copy(w_refs[a].at[slot], w_vmem.at[:, pl.ds(col, WSH)], sems.at[p]))
        for cp in copies:
            cp.start()
        for cp in copies:
            cp.wait()


def _project_columns(xmb, w_ref, shard, cos, sin, p_ref):
    width = 2 * WSH
    q_lo, k_lo, k_hi = CB_QK * D, CB_QK * D + N_HEADS * DK, CB_V * D
    q_pair, k_pair = q_lo // width, k_lo // width
    assert (q_pair + 1) * width == k_lo and k_pair * width == k_lo and k_hi <= (k_pair + 1) * width
    mxu_cols = 2 * LANES

    def columns(rot_lo, rot_hi, scale):
        for a in range(0, width, mxu_cols):
            acc = _dot(xmb, w_ref[:, a:a + mxu_cols])
            for col in range(a, a + mxu_cols, LANES):
                piece = acc[:, col - a:col - a + LANES]
                if rot_lo <= col < rot_hi:
                    piece = _rope(piece if scale == 1.0 else piece * scale, cos, sin)
                p_ref[:, col:col + LANES] = piece.astype(BF16)

    @pl.when(shard == q_pair)
    def _():
        columns(q_lo - q_pair * width, width, 1.0)

    @pl.when(shard == k_pair)
    def _():
        columns(0, k_hi - k_lo, DK ** -0.5)

    @pl.when(jnp.logical_and(shard != q_pair, shard != k_pair))
    def _():
        columns(0, 0, 1.0)


_AFTER_SPEC = pl.BlockSpec(memory_space=pl.ANY)


def _prenorm(x, ctx, modx, modc, norm_w, after):
    n_x_tiles = x.shape[0] // TM
    lext = x.shape[0] + ctx.shape[0]

    assert TM % GRID_W == 0 and TM // GRID_W <= 8
    grid_rows = TM // GRID_W

    def body(x_ref, ctx_ref, mx_ref, mc_ref, nw_ref, after_ref, xm_ref, xmt_ref, cos_ref, sin_ref, col_cos, col_sin):
        del after_ref
        i = pl.program_id(0)
        is_ctx = i >= n_x_tiles
        x = jnp.where(is_ctx, ctx_ref[...], x_ref[...])
        r = lax.rsqrt(jnp.mean(x * x, axis=-1, keepdims=True) + EPS)
        sh = jnp.where(is_ctx, mc_ref[0:1, :], mx_ref[0:1, :])
        sc = jnp.where(is_ctx, mc_ref[1:2, :], mx_ref[1:2, :])
        xm = (x * r * nw_ref[...]) * (1.0 + sc) + sh
        xm_ref[...] = xm.astype(BF16)
        xmt_ref[...] = xm.T.astype(BF16)
        lane = lax.broadcasted_iota(jnp.int32, (1, LANES), 1)
        n_freq = DK // 4
        inv = jnp.exp((lane % n_freq).astype(F32) * (-math.log(ROPE_BASE) / n_freq))
        by_column = (lane % (2 * n_freq)) >= n_freq
        sign = jnp.where((lane % DK) < DK // 2, -1.0, 1.0)

        @pl.when(i == 0)
        def _():
            col = (lax.broadcasted_iota(jnp.int32, (TM, 1), 0) % GRID_W).astype(F32) * inv
            col_cos[...] = jnp.cos(col)
            col_sin[...] = jnp.sin(col) * sign

        row = (i * grid_rows + lax.broadcasted_iota(jnp.int32, (8, 1), 0)).astype(F32) * inv
        row_cos, row_sin = jnp.cos(row), jnp.sin(row) * sign
        tall = lambda t: jnp.concatenate(
            [jnp.broadcast_to(t[g:g + 1, :], (GRID_W, LANES)) for g in range(grid_rows)], axis=0)
        cos_ref[...] = jnp.where(is_ctx, 1.0, jnp.where(by_column, col_cos[...], tall(row_cos)))
        sin_ref[...] = jnp.where(is_ctx, 0.0, jnp.where(by_column, col_sin[...], tall(row_sin)))

    full = lambda a: pl.BlockSpec(a.shape, lambda i: (0,) * a.ndim)
    row = lambda width: pl.BlockSpec((TM, width), lambda i: (i, 0))
    return pl.pallas_call(
        body, name="prenorm", grid=(lext // TM,),
        in_specs=[pl.BlockSpec((TM, D), lambda i: (jnp.minimum(i, n_x_tiles - 1), 0)), full(ctx),
                  full(modx), full(modc), full(norm_w), _AFTER_SPEC],
        out_specs=[row(D), pl.BlockSpec((D, TM), lambda i: (0, i)), row(LANES), row(LANES)],
        out_shape=[jax.ShapeDtypeStruct((lext, D), BF16), jax.ShapeDtypeStruct((D, lext), BF16),
                   jax.ShapeDtypeStruct((lext, LANES), F32), jax.ShapeDtypeStruct((lext, LANES), F32)],
        scratch_shapes=[pltpu.VMEM((TM, LANES), F32), pltpu.VMEM((TM, LANES), F32)],
        compiler_params=_cparams(dimension_semantics=("arbitrary",)),
    )(x, ctx, modx, modc, norm_w, after)


def _in_projection(p_ext, xm, cos_t, sin_t, w, pairs, me, name):
    lext = xm.shape[0]
    tr = lext // 4
    meta = []
    for (slot_a, k_a), (slot_b, k_b) in pairs:
        assert k_a ^ k_b == 1
        dev_a = jnp.bitwise_xor(me, k_a)
        a_low = dev_a % 2 == 0
        meta += [jnp.where(a_low, slot_a, slot_b), jnp.where(a_low, slot_b, slot_a), dev_a // 2]
    meta = jnp.stack(meta).astype(jnp.int32)

    def body(meta_ref, xm_ref, cos_ref, sin_ref, w_hbm, *rest):
        p_ref, w_ref, sems = rest[-3:]
        j, i = pl.program_id(0), pl.program_id(1)

        @pl.when(i == 0)
        def _():
            copies = [pltpu.make_async_copy(w_hbm.at[meta_ref[3 * j + h]], w_ref.at[:, h * WSH:(h + 1) * WSH],
                                            sems.at[h]) for h in range(2)]
            for cp in copies:
                cp.start()
            for cp in copies:
                cp.wait()

        _project_columns(xm_ref[...], w_ref, meta_ref[3 * j + 2], cos_ref[...], sin_ref[...], p_ref)

    row = lambda width: pl.BlockSpec((tr, width), lambda j, i, meta_: (i, 0))
    grid_spec = pltpu.PrefetchScalarGridSpec(
        num_scalar_prefetch=1, grid=(len(pairs), lext // tr),
        in_specs=[row(D), row(LANES), row(LANES), pl.BlockSpec(memory_space=pl.ANY)]
        + ([] if p_ext is None else [pl.BlockSpec(memory_space=pl.ANY)]),
        out_specs=pl.BlockSpec((tr, 2 * WSH), lambda j, i, meta_: (i, meta_[3 * j + 2])),
        scratch_shapes=[pltpu.VMEM((D, 2 * WSH), BF16), pltpu.SemaphoreType.DMA((2,))])
    return pl.pallas_call(
        body, name=name, grid_spec=grid_spec,
        out_shape=jax.ShapeDtypeStruct((lext, PW), BF16),
        input_output_aliases={} if p_ext is None else {5: 0},
        compiler_params=_cparams(dimension_semantics=("arbitrary", "arbitrary")),
    )(meta, xm, cos_t, sin_t, w, *([] if p_ext is None else [p_ext]))


def _decay_tables(lgf, lgb, n):
    i = lax.broadcasted_iota(jnp.int32, (n, 1), 0).astype(F32)
    return dict(i=i, k_f=jnp.exp(lgf * (n - 1.0 - i)), k_b=jnp.exp(lgb * i),
                q_f=jnp.exp(lgf * (i + 1.0)), q_b=jnp.exp(lgb * (n - i)))


def _decay_matrix(lgf, lgb, n):
    ii = lax.broadcasted_iota(jnp.int32, (n, n), 0)
    jj = lax.broadcasted_iota(jnp.int32, (n, n), 1)
    diff = (ii - jj).astype(F32)
    low = jnp.exp(lgf * jnp.maximum(diff, 0.0))
    up = jnp.exp(lgb * jnp.maximum(-diff, 0.0))
    return jnp.where(diff > 0, low, jnp.where(diff < 0, up, 2.0)), diff


def _cat_lanes(a, b):
    return jnp.concatenate([a.astype(BF16), b.astype(BF16)], axis=1)


def _retention_forward(p_ext, lg, seq, ctx_len):
    lext = seq + ctx_len
    n_chunks = seq // RET_C
    C = RET_C

    def body(lg_ref, q_ref, k_ref, v_ref, rn_ref, rstd_ref, kv_scr, sf_scr, sb_scr):
        pair = pl.program_id(0)
        lane = lax.broadcasted_iota(jnp.int32, (1, LANES), 1)
        heads = range(2)
        hmask = [(lane // DK == hh).astype(F32) for hh in heads]
        lgf = [lg_ref[0, 2 * pair + hh] for hh in heads]
        lgb = [lg_ref[1, 2 * pair + hh] for hh in heads]
        vs = [slice(hh * DV, (hh + 1) * DV) for hh in heads]
        t = [_decay_tables(lgf[hh], lgb[hh], C) for hh in heads]
        kc_all = k_ref[seq:lext, :].astype(F32)
        s0 = []
        for hh in heads:
            tc = _decay_tables(lgf[hh], lgb[hh], ctx_len)
            kc = kc_all * hmask[hh]
            s0.append(_dot_tn(_cat_lanes(kc * tc["k_f"], kc * tc["k_b"]), v_ref[seq:lext, vs[hh]]))

        def increments(c, carry):
            rows = pl.ds(pl.multiple_of(c * C, C), C)
            k_all = k_ref[rows, :].astype(F32)
            for hh in heads:
                k = k_all * hmask[hh]
                kv_scr[hh, c] = _dot_tn(_cat_lanes(k * t[hh]["k_f"], k * t[hh]["k_b"]), v_ref[rows, vs[hh]])
            return carry

        lax.fori_loop(0, n_chunks, increments, 0, unroll=2)
        gf_c = [jnp.exp(lgf[hh] * C) for hh in heads]
        gb_c = [jnp.exp(lgb[hh] * C) for hh in heads]

        def scan_f(c, s):
            for hh in heads:
                sf_scr[hh, c] = s[hh]
            return tuple(gf_c[hh] * s[hh] + kv_scr[hh, c, 0:LANES, :] for hh in heads)

        def scan_b(n, s):
            c = n_chunks - 1 - n
            for hh in heads:
                sb_scr[hh, c] = s[hh]
            return tuple(gb_c[hh] * s[hh] + kv_scr[hh, c, LANES:2 * LANES, :] for hh in heads)

        lax.fori_loop(0, n_chunks, scan_f, tuple(s0[hh][0:LANES] for hh in heads))
        lax.fori_loop(0, n_chunks, scan_b, tuple(s0[hh][LANES:2 * LANES] for hh in heads))
        dmat = [_decay_matrix(lgf[hh], lgb[hh], C)[0] for hh in heads]

        def outputs(c, carry):
            rows = pl.ds(pl.multiple_of(c * C, C), C)
            q_all = q_ref[rows, :].astype(F32)
            k = k_ref[rows, :]
            for hh in heads:
                q = q_all * hmask[hh]
                v = v_ref[rows, vs[hh]]
                s = _dot_nt(q.astype(BF16), k)
                inner = _dot((s * dmat[hh]).astype(BF16), v)
                states = jnp.concatenate([sf_scr[hh, c], sb_scr[hh, c]], axis=0).astype(BF16)
                ret = inner + _dot(_cat_lanes(q * t[hh]["q_f"], q * t[hh]["q_b"]), states)
                mu = jnp.mean(ret, axis=-1, keepdims=True)
                cen = ret - mu
                rstd = lax.rsqrt(jnp.mean(cen * cen, axis=-1, keepdims=True) + EPS)
                rn_ref[rows, vs[hh]] = cen * rstd
                rstd_ref[rows, vs[hh]] = jnp.broadcast_to(rstd, (C, DV))
            return carry

        lax.fori_loop(0, n_chunks, outputs, 0, unroll=2)

    qk0 = CB_QK * D // LANES
    return pl.pallas_call(
        body, name="retention_forward", grid=(N_HEADS // 2,),
        in_specs=[pl.BlockSpec(memory_space=pltpu.SMEM),
                  pl.BlockSpec((lext, LANES), lambda g: (0, qk0 + g)),
                  pl.BlockSpec((lext, LANES), lambda g: (0, qk0 + N_HEADS // 2 + g)),
                  pl.BlockSpec((lext, 2 * DV), lambda g: (0, CB_V * D // (2 * DV) + g))],
        out_specs=[pl.BlockSpec((seq, 2 * DV), lambda g: (0, g))] * 2,
        out_shape=[jax.ShapeDtypeStruct((seq, D), F32)] * 2,
        scratch_shapes=[pltpu.VMEM((2, n_chunks, 2 * LANES, DV), F32), pltpu.VMEM((2, n_chunks, LANES, DV), F32),
                        pltpu.VMEM((2, n_chunks, LANES, DV), F32)],
        compiler_params=_cparams(dimension_semantics=("arbitrary",)),
    )(lg, p_ext, p_ext, p_ext)


def _retention_backward(p_ext, dret, lg, cos_t, sin_t, seq, ctx_len, after):
    lext = seq + ctx_len
    n_chunks = seq // RET_C
    C = RET_C

    def body(lg_ref, q_ref, k_ref, v_ref, do_ref, cos_ref, sin_ref, after_ref, dq_ref, dk_ref, dv_ref, dlg_ref,
             kv_scr, g_scr, sf_scr, sb_scr, gfn_scr, gbp_scr):
        pair = pl.program_id(0)
        lane = lax.broadcasted_iota(jnp.int32, (1, LANES), 1)
        heads = range(2)
        hmask = [(lane // DK == hh).astype(F32) for hh in heads]
        lgf = [lg_ref[0, 2 * pair + hh] for hh in heads]
        lgb = [lg_ref[1, 2 * pair + hh] for hh in heads]
        vs = [slice(hh * DV, (hh + 1) * DV) for hh in heads]
        t = [_decay_tables(lgf[hh], lgb[hh], C) for hh in heads]
        tc = [_decay_tables(lgf[hh], lgb[hh], ctx_len) for hh in heads]
        kc_all = k_ref[seq:lext, :].astype(F32)
        kc = [kc_all * hmask[hh] for hh in heads]
        vc = [v_ref[seq:lext, vs[hh]] for hh in heads]
        kc_cat = [_cat_lanes(kc[hh] * tc[hh]["k_f"], kc[hh] * tc[hh]["k_b"]) for hh in heads]
        s0 = [_dot_tn(kc_cat[hh], vc[hh]) for hh in heads]

        def increments(c, carry):
            rows = pl.ds(pl.multiple_of(c * C, C), C)
            k_all = k_ref[rows, :].astype(F32)
            q_all = q_ref[rows, :].astype(F32)
            for hh in heads:
                k, q = k_all * hmask[hh], q_all * hmask[hh]
                kv_scr[hh, c] = _dot_tn(_cat_lanes(k * t[hh]["k_f"], k * t[hh]["k_b"]), v_ref[rows, vs[hh]])
                g_scr[hh, c] = _dot_tn(_cat_lanes(q * t[hh]["q_f"], q * t[hh]["q_b"]), do_ref[rows, vs[hh]])
            return carry

        lax.fori_loop(0, n_chunks, increments, 0, unroll=2)
        gf_c = [jnp.exp(lgf[hh] * C) for hh in heads]
        gb_c = [jnp.exp(lgb[hh] * C) for hh in heads]

        def scan_f(c, s):
            for hh in heads:
                sf_scr[hh, c] = s[hh]
            return tuple(gf_c[hh] * s[hh] + kv_scr[hh, c, 0:LANES, :] for hh in heads)

        def scan_b(n, s):
            c = n_chunks - 1 - n
            for hh in heads:
                sb_scr[hh, c] = s[hh]
            return tuple(gb_c[hh] * s[hh] + kv_scr[hh, c, LANES:2 * LANES, :] for hh in heads)

        def scan_gf(n, carry):
            c = n_chunks - 1 - n
            for hh in heads:
                gfn_scr[hh, c] = carry[hh]
            return tuple(g_scr[hh, c, 0:LANES, :] + gf_c[hh] * carry[hh] for hh in heads)

        def scan_gb(c, carry):
            for hh in heads:
                gbp_scr[hh, c] = carry[hh]
            return tuple(g_scr[hh, c, LANES:2 * LANES, :] + gb_c[hh] * carry[hh] for hh in heads)

        lax.fori_loop(0, n_chunks, scan_f, tuple(s0[hh][0:LANES] for hh in heads))
        lax.fori_loop(0, n_chunks, scan_b, tuple(s0[hh][LANES:2 * LANES] for hh in heads))
        zero_state = jnp.zeros((LANES, DV), F32)
        gf_first = lax.fori_loop(0, n_chunks, scan_gf, (zero_state, zero_state))
        gb_last = lax.fori_loop(0, n_chunks, scan_gb, (zero_state, zero_state))

        dmat, w_f, w_b = [], [], []
        for hh in heads:
            dm, diff = _decay_matrix(lgf[hh], lgb[hh], C)
            dmat.append(dm)
            w_f.append(jnp.where(diff > 0, diff * dm, 0.0))
            w_b.append(jnp.where(diff < 0, -diff * dm, 0.0))

        def total(a):
            rows_, width = a.shape
            part = jnp.sum(a.reshape(rows_ // 8, 8, width), axis=0)
            return part[:, 0:LANES] + part[:, LANES:2 * LANES] if width == 2 * LANES else part

        def chunk_head(hh, c, rows, q_all, k_all, dlgf, dlgb):
            th = t[hh]
            qm = q_all * hmask[hh]
            km = k_all * hmask[hh]
            qb, kb = qm.astype(BF16), km.astype(BF16)
            v = v_ref[rows, vs[hh]]
            do = do_ref[rows, vs[hh]]
            s = _dot_nt(qb, kb)
            dsd = _dot_nt(do, v)
            ds = (dsd * dmat[hh]).astype(BF16)
            dq_in = _dot(ds, kb)
            dk_in = _dot_tn(ds, qb)
            dv_in = _dot_tn((s * dmat[hh]).astype(BF16), do)
            prod = s * dsd
            dlgf = dlgf + total(prod * w_f[hh])
            dlgb = dlgb + total(prod * w_b[hh])
            sf, sb = sf_scr[hh, c], sb_scr[hh, c]
            states = jnp.concatenate([sf, sb], axis=0).astype(BF16)
            dqc = _dot_nt(do, states)
            dqf = dqc[:, 0:LANES] * th["q_f"]
            dqb = dqc[:, LANES:2 * LANES] * th["q_b"]
            dq = (dq_in + dqf + dqb) * hmask[hh]
            dlgf = dlgf + total((th["i"] + 1.0) * (qm * dqf))
            dlgb = dlgb + total((C - th["i"]) * (qm * dqb))
            gfn, gbp = gfn_scr[hh, c], gbp_scr[hh, c]
            gstates = jnp.concatenate([gfn, gbp], axis=0).astype(BF16)
            dkc = _dot_nt(v, gstates)
            dkf = dkc[:, 0:LANES] * th["k_f"]
            dkb = dkc[:, LANES:2 * LANES] * th["k_b"]
            dk = dk_in + dkf + dkb
            dlgf = dlgf + total((C - 1.0 - th["i"]) * (km * dkf)) + C * gf_c[hh] * total(gfn * sf)
            dlgb = dlgb + total(th["i"] * (km * dkb)) + C * gb_c[hh] * total(gbp * sb)
            dv = dv_in + _dot(_cat_lanes(km * th["k_f"], km * th["k_b"]), gstates)
            dv_ref[rows, vs[hh]] = dv.astype(BF16)
            return dq, dk, dlgf, dlgb

        def chunk(c, carry):
            rows = pl.ds(pl.multiple_of(c * C, C), C)
            q_all = q_ref[rows, :].astype(F32)
            k_all = k_ref[rows, :].astype(F32)
            dq0, dk0, f0, b0 = chunk_head(0, c, rows, q_all, k_all, carry[0], carry[1])
            dq1, dk1, f1, b1 = chunk_head(1, c, rows, q_all, k_all, carry[2], carry[3])
            cos, sin = cos_ref[rows, :], sin_ref[rows, :]
            dq_ref[rows, :] = _rope(dq0 + dq1, cos, -sin).astype(BF16)
            dk_ref[rows, :] = (_rope(dk0 + dk1, cos, -sin) * (DK ** -0.5)).astype(BF16)
            return f0, b0, f1, b1

        zero = jnp.zeros((8, LANES), F32)
        sums = lax.fori_loop(0, n_chunks, chunk, (zero, zero, zero, zero), unroll=2)

        dlg = []
        dk_ctx = jnp.zeros((ctx_len, LANES), F32)
        for hh in heads:
            g0 = jnp.concatenate([gf_first[hh], gb_last[hh]], axis=0).astype(BF16)
            dkcc = _dot_nt(vc[hh], g0)
            dkcf = dkcc[:, 0:LANES] * tc[hh]["k_f"]
            dkcb = dkcc[:, LANES:2 * LANES] * tc[hh]["k_b"]
            dlgf = sums[2 * hh] + total((ctx_len - 1.0 - tc[hh]["i"]) * (kc[hh] * dkcf))
            dlgb = sums[2 * hh + 1] + total(tc[hh]["i"] * (kc[hh] * dkcb))
            dk_ctx = dk_ctx + (dkcf + dkcb) * (DK ** -0.5)
            dv_ref[seq:lext, vs[hh]] = _dot(kc_cat[hh], g0).astype(BF16)
            dlg += [jnp.sum(jnp.sum(a, axis=1, keepdims=True), axis=0, keepdims=True) for a in (dlgf, dlgb)]
        dk_ref[seq:lext, :] = dk_ctx.astype(BF16)
        dq_ref[seq:lext, :] = jnp.zeros((ctx_len, LANES), BF16)

        lane8 = lax.broadcasted_iota(jnp.int32, (8, LANES), 1)
        out = jnp.zeros((8, LANES), F32)
        for n, val in enumerate(dlg):
            out = jnp.where(lane8 == n, val, out)
        dlg_ref[0] = out

    qk0 = CB_QK * D // LANES
    q_spec = pl.BlockSpec((lext, LANES), lambda g: (0, qk0 + g))
    k_spec = pl.BlockSpec((lext, LANES), lambda g: (0, qk0 + N_HEADS // 2 + g))
    v_spec = pl.BlockSpec((lext, 2 * DV), lambda g: (0, CB_V * D // (2 * DV) + g))
    table = pl.BlockSpec((lext, LANES), lambda g: (0, 0))
    state = pltpu.VMEM((2, n_chunks, LANES, DV), F32)
    return pl.pallas_call(
        body, name="retention_backward", grid=(N_HEADS // 2,),
        in_specs=[pl.BlockSpec(memory_space=pltpu.SMEM), q_spec, k_spec, v_spec,
                  pl.BlockSpec((seq, 2 * DV), lambda g: (0, g)), table, table, _AFTER_SPEC],
        out_specs=[pl.BlockSpec((lext, LANES), lambda g: (0, g)), pl.BlockSpec((lext, LANES), lambda g: (0, g)),
                   pl.BlockSpec((lext, 2 * DV), lambda g: (0, g)), pl.BlockSpec((1, 8, LANES), lambda g: (g, 0, 0))],
        out_shape=[jax.ShapeDtypeStruct((lext, N_HEADS * DK), BF16), jax.ShapeDtypeStruct((lext, N_HEADS * DK), BF16),
                   jax.ShapeDtypeStruct((lext, D), BF16), jax.ShapeDtypeStruct((N_HEADS // 2, 8, LANES), F32)],
        scratch_shapes=[pltpu.VMEM((2, n_chunks, 2 * LANES, DV), F32), pltpu.VMEM((2, n_chunks, 2 * LANES, DV), F32),
                        state, state, state, state],
        compiler_params=_cparams(dimension_semantics=("arbitrary",)),
    )(lg, p_ext, p_ext, p_ext, dret, cos_t, sin_t, after)


class _ColumnWriter:
    def __init__(self, dst_hbm, stage, sems, pieces, n_steps):
        self.dst, self.stage, self.sems, self.pieces, self.n_steps = dst_hbm, stage, sems, pieces, n_steps

    def _copies(self, slot, tile):
        rows = pl.ds(pl.multiple_of(tile * TM, TM), TM)
        return [pltpu.make_async_copy(self.stage.at[slot, :, pl.ds(src, width)], self.dst.at[rows, pl.ds(dst, width)],
                                      self.sems.at[slot, k]) for k, (src, dst, width) in enumerate(self.pieces)]

    def slot_for(self, step):
        slot = step % 2

        @pl.when(step >= 2)
        def _():
            for cp in self._copies(slot, step - 2):
                cp.wait()

        return slot

    def send(self, step, slot):
        for cp in self._copies(slot, step):
            cp.start()

        @pl.when(step == self.n_steps - 1)
        def _():
            if self.n_steps >= 2:
                for cp in self._copies(1 - slot, step - 1):
                    cp.wait()
            for cp in self._copies(slot, step):
                cp.wait()


def _merge(p_ext, rn, rstd, x, target, w_a, w_b, w_out, vecs, seq):
    n_tiles = seq // TM
    hb = TM // HALO
    dp_pieces = [(0, CB_BG * D, D), (D, CB_ZA * D, D), (2 * D, CB_ZB * D, 3 * D)]

    def body(h_ref, bg_ref, cg_ref, za_ref, zb_ref, ga_ref, gb_ref, hp_ref, hn_ref, cp_ref, cn_ref,
             rn_ref, rstd_ref, x_ref, t_ref, wa_ref, wb_ref, wo_ref, vec_ref,
             dx1_ref, dp_hbm, dconv_ref, dret_ref, at_ref, b_ref, part_ref, stage, dp_sems):
        i = pl.program_id(0)
        writer = _ColumnWriter(dp_hbm, stage, dp_sems, dp_pieces, n_tiles)
        slot = writer.slot_for(i)
        dpa_ref = stage.at[slot]
        f = lambda ref: ref[...].astype(F32)
        h, bg, cg, za, zb, ga, gb = f(h_ref), f(bg_ref), f(cg_ref), f(za_ref), f(zb_ref), f(ga_ref), f(gb_ref)
        gx, w0, w1, w2 = vec_ref[0:1, :], vec_ref[1:2, :], vec_ref[2:3, :], vec_ref[3:4, :]
        cb, gnw, fw = vec_ref[4:5, :], vec_ref[5:6, :], vec_ref[6:7, :]
        u = cg * h
        row = lax.broadcasted_iota(jnp.int32, (TM, 1), 0)
        u_prev = (f(cp_ref) * f(hp_ref))[HALO - 1:HALO, :]
        u_next = (f(cn_ref) * f(hn_ref))[0:1, :]
        u_prev = jnp.where(i == 0, 0.0, u_prev)
        u_next = jnp.where(i == n_tiles - 1, 0.0, u_next)
        u_up = jnp.where(row == 0, u_prev, pltpu.roll(u, 1, 0))
        u_dn = jnp.where(row == TM - 1, u_next, pltpu.roll(u, TM - 1, 0))
        conv = w0 * u_up + w1 * u + w2 * u_dn + cb
        sza = _sigmoid(za)
        silu_za = za * sza
        a_act = silu_za * bg * conv
        rn = rn_ref[...]
        szb = _sigmoid(zb)
        silu_zb = zb * szb
        rg = rn * gnw
        b_act = silu_zb * rg
        y_a = _dot(a_act.astype(BF16), wa_ref[...])
        y_b = _dot(b_act.astype(BF16), wb_ref[...])
        sga, sgb = _sigmoid(ga), _sigmoid(gb)
        mix = sga * y_a + sgb * y_b
        y = _dot(mix.astype(BF16), wo_ref[...])
        x1 = x_ref[...] + gx * y
        r1 = lax.rsqrt(jnp.mean(x1 * x1, axis=-1, keepdims=True) + EPS)
        xh1 = x1 * r1
        err = xh1 * fw - t_ref[...]
        loss = jnp.sum(jnp.sum(err * err, axis=1, keepdims=True), axis=0, keepdims=True) * (0.5 / D)
        dout = err * (1.0 / D)
        dxh = dout * fw
        dx1 = r1 * (dxh - xh1 * jnp.mean(dxh * xh1, axis=-1, keepdims=True))
        dx1_ref[...] = dx1
        dy = (dx1 * gx).astype(BF16)
        dmix = _dot_nt(dy, wo_ref[...])
        dya = (dmix * sga).astype(BF16)
        dyb = (dmix * sgb).astype(BF16)
        da = _dot_nt(dya, wa_ref[...])
        db = _dot_nt(dyb, wb_ref[...])
        dpa_ref[:, 0:D] = (da * silu_za * conv).astype(BF16)
        dpa_ref[:, D:2 * D] = (da * bg * conv * (sza * (1.0 + za * (1.0 - sza)))).astype(BF16)
        dpa_ref[:, 2 * D:3 * D] = (db * rg * (szb * (1.0 + zb * (1.0 - szb)))).astype(BF16)
        dpa_ref[:, 3 * D:4 * D] = (dmix * y_a * sga * (1.0 - sga)).astype(BF16)
        dpa_ref[:, 4 * D:5 * D] = (dmix * y_b * sgb * (1.0 - sgb)).astype(BF16)
        dconv_ref[...] = (da * silu_za * bg).astype(BF16)
        drn_n = db * silu_zb
        drn = drn_n * gnw
        rstd = rstd_ref[...]
        for hd in range(N_HEADS):
            sl = slice(hd * DV, (hd + 1) * DV)
            dh_, rh = drn[:, sl], rn[:, sl]
            m1 = jnp.mean(dh_, axis=-1, keepdims=True)
            m2 = jnp.mean(dh_ * rh, axis=-1, keepdims=True)
            dret_ref[:, sl] = (rstd[:, sl] * (dh_ - m1 - rh * m2)).astype(BF16)
        at_ref[0] = a_act.T.astype(BF16)
        at_ref[1] = b_act.T.astype(BF16)
        at_ref[2] = mix.T.astype(BF16)
        b_ref[0] = dya
        b_ref[1] = dyb
        b_ref[2] = dy

        @pl.when(i == 0)
        def _():
            part_ref[...] = jnp.zeros((8, D), F32)

        part_ref[0:1, :] += jnp.sum(dout * xh1, axis=0, keepdims=True)
        part_ref[1:2, :] += jnp.sum(dx1 * y, axis=0, keepdims=True)
        part_ref[2:3, :] += jnp.sum(drn_n * rn, axis=0, keepdims=True)
        part_ref[3:4, :] += jnp.broadcast_to(loss, (1, D))
        writer.send(i, slot)

    col = lambda cb_: pl.BlockSpec((TM, D), lambda i, cb_=cb_: (i, cb_))
    prev = lambda cb_: pl.BlockSpec((HALO, D), lambda i, cb_=cb_: (jnp.maximum(i * hb - 1, 0), cb_))
    nxt = lambda cb_: pl.BlockSpec((HALO, D), lambda i, cb_=cb_: (jnp.minimum((i + 1) * hb, n_tiles * hb - 1), cb_))
    tile = pl.BlockSpec((TM, D), lambda i: (i, 0))
    full = lambda a: pl.BlockSpec(a.shape, lambda i: (0,) * a.ndim, pipeline_mode=pl.Buffered(1))
    return pl.pallas_call(
        body, name="merge", grid=(n_tiles,),
        in_specs=[col(CB_H), col(CB_BG), col(CB_CG), col(CB_ZA), col(CB_ZB), col(CB_GA), col(CB_GB),
                  prev(CB_H), nxt(CB_H), prev(CB_CG), nxt(CB_CG),
                  tile, tile, tile, tile, full(w_a), full(w_b), full(w_out), full(vecs)],
        out_specs=[tile, pl.BlockSpec(memory_space=pl.ANY), tile, tile,
                   pl.BlockSpec((3, D, TM), lambda i: (0, 0, i)), pl.BlockSpec((3, TM, D), lambda i: (0, i, 0)),
                   pl.BlockSpec((8, D), lambda i: (0, 0))],
        out_shape=[jax.ShapeDtypeStruct((seq, D), F32), jax.ShapeDtypeStruct((p_ext.shape[0], PW), BF16),
                   jax.ShapeDtypeStruct((seq, D), BF16), jax.ShapeDtypeStruct((seq, D), BF16),
                   jax.ShapeDtypeStruct((3, D, seq), BF16), jax.ShapeDtypeStruct((3, seq, D), BF16),
                   jax.ShapeDtypeStruct((8, D), F32)],
        scratch_shapes=[pltpu.VMEM((2, TM, 5 * D), BF16), pltpu.SemaphoreType.DMA((2, len(dp_pieces)))],
        compiler_params=_cparams(dimension_semantics=("arbitrary",)),
    )(p_ext, p_ext, p_ext, p_ext, p_ext, p_ext, p_ext, p_ext, p_ext, p_ext, p_ext,
      rn, rstd, x, target, w_a, w_b, w_out, vecs)


def _conv_backward(p_ext, dconv, dp, dq, dk, dv, vecs, seq, after):
    n_tiles = seq // TM
    hb = TM // HALO
    qk = N_HEADS * DK
    pieces = [(0, CB_H * D, D), (D, CB_CG * D, D), (2 * D, CB_QK * D, 2 * D)]
    zero_pieces = [(CB_BG * D, D), (CB_ZA * D, D), (CB_ZB * D, 3 * D)]

    def body(h_ref, cg_ref, dc_ref, dcp_ref, dcn_ref, dq_ref, dk_ref, dv_ref, vec_ref, dp_in, after_ref, dp_hbm, part_ref,
             stage, sems, zeros, zero_sems):
        del dp_in, after_ref
        i = pl.program_id(0)
        writer = _ColumnWriter(dp_hbm, stage, sems, pieces, n_tiles + 1)
        slot = writer.slot_for(i)
        out = stage.at[slot]
        out[:, 2 * D:2 * D + qk] = dq_ref[...]
        out[:, 2 * D + qk:3 * D] = dk_ref[...]
        out[:, 3 * D:4 * D] = dv_ref[...]

        @pl.when(i == 0)
        def _():
            part_ref[...] = jnp.zeros((8, D), F32)

        @pl.when(i == n_tiles)
        def _():
            out[:, 0:2 * D] = jnp.zeros((TM, 2 * D), BF16)
            zeros[...] = jnp.zeros(zeros.shape, BF16)
            rows = pl.ds(n_tiles * TM, TM)
            fills = [pltpu.make_async_copy(zeros.at[:, pl.ds(0, width)], dp_hbm.at[rows, pl.ds(dst, width)],
                                           zero_sems.at[k]) for k, (dst, width) in enumerate(zero_pieces)]
            for cp in fills:
                cp.start()
            for cp in fills:
                cp.wait()

        @pl.when(i < n_tiles)
        def _():
            f = lambda ref: ref[...].astype(F32)
            h, cg, dc = f(h_ref), f(cg_ref), f(dc_ref)
            w0, w1, w2 = vec_ref[1:2, :], vec_ref[2:3, :], vec_ref[3:4, :]
            row = lax.broadcasted_iota(jnp.int32, (TM, 1), 0)
            dc_prev = jnp.where(i == 0, 0.0, f(dcp_ref)[HALO - 1:HALO, :])
            dc_next = jnp.where(i == n_tiles - 1, 0.0, f(dcn_ref)[0:1, :])
            dc_up = jnp.where(row == 0, dc_prev, pltpu.roll(dc, 1, 0))
            dc_dn = jnp.where(row == TM - 1, dc_next, pltpu.roll(dc, TM - 1, 0))
            du = w0 * dc_dn + w1 * dc + w2 * dc_up
            u = cg * h
            part_ref[0:1, :] += jnp.sum(u * dc_dn, axis=0, keepdims=True)
            part_ref[1:2, :] += jnp.sum(u * dc, axis=0, keepdims=True)
            part_ref[2:3, :] += jnp.sum(u * dc_up, axis=0, keepdims=True)
            part_ref[3:4, :] += jnp.sum(dc, axis=0, keepdims=True)
            out[:, 0:D] = (du * cg).astype(BF16)
            out[:, D:2 * D] = (du * h).astype(BF16)

        writer.send(i, slot)

    last = n_tiles - 1
    col = lambda cb_: pl.BlockSpec((TM, D), lambda i, cb_=cb_: (jnp.minimum(i, last), cb_))
    lat = lambda w: pl.BlockSpec((TM, w), lambda i: (jnp.minimum(i, last), 0))
    ext = lambda w: pl.BlockSpec((TM, w), lambda i: (i, 0))
    anyspec = pl.BlockSpec(memory_space=pl.ANY)
    return pl.pallas_call(
        body, name="conv_backward", grid=(n_tiles + 1,),
        in_specs=[col(CB_H), col(CB_CG), lat(D),
                  pl.BlockSpec((HALO, D), lambda i: (jnp.clip(i * hb - 1, 0, n_tiles * hb - 1), 0)),
                  pl.BlockSpec((HALO, D), lambda i: (jnp.minimum((i + 1) * hb, n_tiles * hb - 1), 0)),
                  ext(qk), ext(qk), ext(D), pl.BlockSpec(vecs.shape, lambda i: (0, 0)), anyspec, _AFTER_SPEC],
        out_specs=[anyspec, pl.BlockSpec((8, D), lambda i: (0, 0))],
        out_shape=[jax.ShapeDtypeStruct(dp.shape, BF16), jax.ShapeDtypeStruct((8, D), F32)],
        input_output_aliases={9: 0},
        scratch_shapes=[pltpu.VMEM((2, TM, 4 * D), BF16), pltpu.SemaphoreType.DMA((2, len(pieces))),
                        pltpu.VMEM((TM, 3 * D), BF16), pltpu.SemaphoreType.DMA((len(zero_pieces),))],
        compiler_params=_cparams(dimension_semantics=("arbitrary",)),
    )(p_ext, p_ext, dconv, dconv, dconv, dq, dk, dv, vecs, dp, after)


def _input_backward(dp, shards, ids, x, ctx, dx1, modx, modc, norm_w, after):
    seq = x.shape[0]
    lext = seq + ctx.shape[0]
    n_x = seq // TM
    n_w = len(shards.arrays)

    def body(ids_ref, dp_ref, x_ref, ctx_ref, dx1_ref, mx_ref, mc_ref, nw_ref, *rest):
        w_hbm, (_, gx_ref, part_ref, w_ref, w_sems) = rest[:n_w], rest[n_w:]
        i = pl.program_id(0)
        _load_w_in(i, ids_ref, shards, w_hbm, w_ref, w_sems)
        is_ctx = i >= n_x
        dxm = _dot_nt(dp_ref[...], w_ref[...])
        x = jnp.where(is_ctx, ctx_ref[...], x_ref[...])
        r = lax.rsqrt(jnp.mean(x * x, axis=-1, keepdims=True) + EPS)
        xh = x * r
        nw = nw_ref[...]
        sc = jnp.where(is_ctx, mc_ref[1:2, :], mx_ref[1:2, :])
        dxn = dxm * (1.0 + sc)
        dxh = dxn * nw
        dx = r * (dxh - xh * jnp.mean(dxh * xh, axis=-1, keepdims=True))

        @pl.when(jnp.logical_not(is_ctx))
        def _():
            gx_ref[...] = dx1_ref[...] + dx

        @pl.when(i == 0)
        def _():
            part_ref[...] = jnp.zeros((8, D), F32)

        fx = jnp.where(is_ctx, 0.0, 1.0)
        d_shift = jnp.sum(dxm, axis=0, keepdims=True)
        d_scale = jnp.sum(dxm * (xh * nw), axis=0, keepdims=True)
        part_ref[0:1, :] += fx * d_shift
        part_ref[1:2, :] += fx * d_scale
        part_ref[2:3, :] += jnp.sum(dxn * xh, axis=0, keepdims=True)
        part_ref[3:4, :] += (1.0 - fx) * d_shift
        part_ref[4:5, :] += (1.0 - fx) * d_scale

    lat = lambda w: pl.BlockSpec((TM, w), lambda i, ids_: (jnp.minimum(i, n_x - 1), 0))
    ext = lambda w: pl.BlockSpec((TM, w), lambda i, ids_: (i, 0))
    full = lambda a: pl.BlockSpec(a.shape, lambda i, ids_: (0,) * a.ndim)
    grid_spec = pltpu.PrefetchScalarGridSpec(
        num_scalar_prefetch=1, grid=(lext // TM,),
        in_specs=[ext(PW), lat(D), full(ctx), lat(D), full(modx), full(modc), full(norm_w)]
        + [pl.BlockSpec(memory_space=pl.ANY)] * n_w + [_AFTER_SPEC],
        out_specs=[lat(D), pl.BlockSpec((8, D), lambda i, ids_: (0, 0))],
        scratch_shapes=[pltpu.VMEM((D, PW), BF16), pltpu.SemaphoreType.DMA((N_DEV,))])
    return pl.pallas_call(
        body, name="input_backward", grid_spec=grid_spec,
        out_shape=[jax.ShapeDtypeStruct((seq, D), F32), jax.ShapeDtypeStruct((8, D), F32)],
        compiler_params=_cparams(dimension_semantics=("arbitrary",)),
    )(ids, dp, x, ctx, dx1, modx, modc, norm_w, *shards.arrays, after)


def _weight_grad_in(xm_t, dp, owners, name, after):
    lext = xm_t.shape[1]

    def body(own_ref, a_ref, b_ref, after_ref, o_ref):
        o_ref[0] = _dot(a_ref[...], b_ref[...])

    grid_spec = pltpu.PrefetchScalarGridSpec(
        num_scalar_prefetch=1, grid=(4,),
        in_specs=[pl.BlockSpec((D, lext), lambda j, own: (0, 0)),
                  pl.BlockSpec((lext, WSH), lambda j, own: (0, own[j])), _AFTER_SPEC],
        out_specs=pl.BlockSpec((1, D, WSH), lambda j, own: (j, 0, 0)))
    return pl.pallas_call(
        body, name=name, grid_spec=grid_spec,
        out_shape=jax.ShapeDtypeStruct((4, D, WSH), F32),
        compiler_params=_cparams(dimension_semantics=("arbitrary",)),
    )(owners, xm_t, dp, after)


def _weight_grad_square(a_t, b):
    seq = a_t.shape[2]

    def body(a_ref, b_ref, o_ref):
        o_ref[:, 0] = _dot(a_ref[0], b_ref[0]).reshape(N_DEV, RSH, D)

    return pl.pallas_call(
        body, name="weight_grad_square", grid=(3,),
        in_specs=[pl.BlockSpec((1, D, seq), lambda t: (t, 0, 0)), pl.BlockSpec((1, seq, D), lambda t: (t, 0, 0))],
        out_specs=pl.BlockSpec((N_DEV, 1, RSH, D), lambda t: (0, t, 0, 0)),
        out_shape=jax.ShapeDtypeStruct((N_DEV, 3, RSH, D), F32),
        compiler_params=_cparams(dimension_semantics=("arbitrary",)),
    )(a_t, b)


def _rows(total, *parts):
    width = max(a.shape[1] for _, a in parts)
    out = None
    for row, a in parts:
        padded = jnp.pad(a, ((row, total - row - a.shape[0]), (0, width - a.shape[1])))
        out = padded if out is None else out + padded
    return out


def kernel(x, c, ctx, c_ctx, norm_w, ada_w, ada_b, w_in, conv_w, conv_b, decay_logit, gn_w, w_a, w_b, w_out, final_norm_w, loss_target, m_c_ctx, m_norm_w, m_ada_w, m_ada_b, m_w_in, m_conv_w, m_conv_b, m_decay_logit, m_gn_w, m_w_a, m_w_b, m_w_out, m_final_norm_w, v_c_ctx, v_norm_w, v_ada_w, v_ada_b, v_w_in, v_conv_w, v_conv_b, v_decay_logit, v_gn_w, v_w_a, v_w_b, v_w_out, v_final_norm_w):
    xi, yi, ci = _coords()
    me = 4 * xi + 2 * yi + ci
    chip = 2 * xi + yi
    seq, ctx_len = x.shape[1], ctx.shape[1]
    assert seq % TM == 0 and seq % RET_C == 0 and ctx_len == TM and seq % GRID_W == 0
    csh = D // N_DEV

    blk = jnp.pad(c, ((0, 7), (0, 0))) + jnp.pad(conv_w[0], ((1, 4), (0, D - csh)))
    got, _ = _all_gather_small(blk, "gather_cond")
    conv_w_all = got[:, 1:4, 0:csh].transpose(1, 0, 2).reshape(3, D)
    c16 = _rows(16, (0, got[:, 0, :]), (N_DEV, c_ctx[None]))
    ada_b_sh = lax.dynamic_slice(ada_b, (0, me * ADA_SH), (1, ADA_SH))
    mod_sh, act16, lg = _modulation(c16, ada_w[0], ada_b_sh, decay_logit[0])
    mod_all, small_done = _all_gather_small(mod_sh, "gather_mod")
    mod_all = mod_all.transpose(1, 0, 2).reshape(16, 3 * D)
    modx = lax.dynamic_slice(mod_all, (me, 0), (1, 3 * D)).reshape(3, D)
    modc = mod_all[8].reshape(3, D)

    x2, tgt = x[0], loss_target[0]
    wblock = lambda n: jax.ShapeDtypeStruct((n, D, WSH), BF16)
    ids_of = lambda ks: jnp.stack([jnp.bitwise_xor(me, k) for k in ks]).astype(jnp.int32)
    own_shard = w_in[0].astype(BF16)[None]
    sq_shards = [w[0].astype(BF16) for w in (w_a, w_b, w_out)]
    wave1 = _Exchange(
        "gather_in_first", [jnp.broadcast_to(own_shard, (2, D, WSH)), own_shard], [wblock(4)], 3,
        lambda srcs, lands: _forward_plan([(0, 1)])(srcs, lands) + _send_plan((4, 2))(srcs[1:], lands),
        after=small_done)
    xm, xm_t, cos_t, sin_t = _prenorm(x2, ctx[0], modx, modc, norm_w, wave1.token)
    (w_pair, _), _ = wave1.wait(xm_t, 1, through=[0])
    first = _Shards([w_pair], [(0, 0), (0, 1)])
    p_ext = _in_projection(None, xm, cos_t, sin_t, w_pair, [((0, 0), (1, 1))], me, "in_projection_pair")
    _, (w_nbr,) = wave1.wait(p_ext, through=[1, 2])
    wave2 = _Exchange(
        "gather_in_second", [w_nbr, own_shard], [wblock(2)], 3,
        lambda srcs, lands: _forward_plan([(0, 2), (1, 3)])(srcs, lands) + _send_plan((6,))(srcs[1:], lands))
    (w_nbr, _), _ = wave2.wait(wave2.token, 2, through=[0])
    second = _Shards([w_nbr], [(0, 0), (0, 1), (0, 2), (0, 3)])
    p_ext = _in_projection(p_ext, xm, cos_t, sin_t, w_nbr, [((0, 4), (2, 5)), ((1, 2), (3, 3))], me,
                           "in_projection_neighbours")
    _, (w_diag,) = wave2.wait(p_ext, through=[1, 2])
    wave3 = _Exchange(
        "gather_in_third", [w_diag] + sq_shards, [jax.ShapeDtypeStruct((N_DEV, RSH, D), BF16)] * 3, 1 + 3 * (N_DEV - 1),
        lambda srcs, lands: _forward_plan([(0, 1)])(srcs, lands) + _gather_plan(srcs[1:], lands))
    (w_diag, *_), _ = wave3.wait(wave3.token, 1, through=[0])
    third = _Shards([w_diag], [(0, 0), (0, 1)])
    p_ext = _in_projection(p_ext, xm, cos_t, sin_t, w_diag, [((0, 6), (1, 7))], me, "in_projection_diagonal")
    w_shards, w_ids = first + second + third, ids_of((0, 1, 4, 2, 5, 3, 6, 7))

    rn, rstd = _retention_forward(p_ext, lg, seq, ctx_len)
    _, sq_lands = wave3.wait(rstd, through=range(1, 7))
    w_a_all, w_b_all, w_out_all = (
        lax.dynamic_update_slice(land, shard[None], (me, 0, 0)).reshape(D, D) for land, shard in zip(sq_lands, sq_shards))
    vecs = _rows(8, (0, modx[2:3]), (1, conv_w_all), (4, conv_b), (5, gn_w), (6, final_norm_w[None]))
    dx1, dp, dconv, dret, op_at, op_b, part_m = _merge(p_ext, rn, rstd, x2, tgt, w_a_all, w_b_all, w_out_all, vecs, seq)

    j4 = jnp.arange(4, dtype=jnp.int32)
    owners = (2 * jnp.bitwise_xor(chip, j4) + ci).astype(jnp.int32)
    owners_sib = (2 * jnp.bitwise_xor(chip, j4) + (1 - ci)).astype(jnp.int32)
    gw_sq = _weight_grad_square(op_at, op_b).reshape(N_DEV, 3 * RSH, D)
    rs_sq_pair = _Exchange("rs_square_pair", [gw_sq], [jax.ShapeDtypeStruct((4, 3 * RSH, D), F32)], 4,
                           _pair_plan(lambda j, chip_, c_: 2 * jnp.bitwise_xor(chip_, j) + (1 - c_)))
    dq, dk, dv, dlg = _retention_backward(p_ext, dret, lg, cos_t, sin_t, seq, ctx_len, rs_sq_pair.token)
    (gw_sq,), (r1_sq,) = rs_sq_pair.wait(dlg)
    own_sq, send_sq = _pair_sum(gw_sq, r1_sq, owners, "pair_sum_square")
    rs_sq_chips = _Exchange("rs_square_chips", [send_sq], [jax.ShapeDtypeStruct((3, 3 * RSH, D), BF16)], 3, _chips_plan)
    dp, part_c = _conv_backward(p_ext, dconv, dp, dq, dk, dv, vecs, seq, rs_sq_chips.token)
    gw_sib = _weight_grad_in(xm_t, dp, owners_sib, "weight_grad_in_sibling", part_c)
    rs_in_pair = _Exchange("rs_in_pair", [gw_sib], [jax.ShapeDtypeStruct((4, D, WSH), F32)], 4,
                           _pair_plan(lambda j, chip_, c_: j))
    gw_own = _weight_grad_in(xm_t, dp, owners, "weight_grad_in_own", rs_in_pair.token)
    _, (r1_in,) = rs_in_pair.wait(gw_own)
    own_in, send_in = _pair_sum(gw_own, r1_in, j4, "pair_sum_in")
    rs_in_chips = _Exchange("rs_in_chips", [send_in], [jax.ShapeDtypeStruct((3, D, WSH), BF16)], 3, _chips_plan)
    grad_x, part_i = _input_backward(dp, w_shards, w_ids, x2, ctx[0], dx1, modx, modc, norm_w, rs_in_chips.token)

    dl = dlg[:, 0, 0:4]
    dlg_row = jnp.pad(dl[:, 0::2].reshape(1, N_HEADS), ((0, 0), (0, D - N_HEADS))) + jnp.pad(
        dl[:, 1::2].reshape(1, N_HEADS), ((0, 0), (N_HEADS, D - 2 * N_HEADS)))
    partials = _rows(16, (0, part_m[0:1]), (1, part_i[2:3]), (2, part_c[3:4]), (3, part_m[2:3]), (4, part_c[0:3]),
                     (7, part_i[0:2]), (9, part_m[1:2]), (10, part_i[3:5]), (13, part_m[3:4]), (14, dlg_row))
    results = {}
    gather_part = _Exchange("gather_partials", [partials], [jax.ShapeDtypeStruct((N_DEV, 16, D), F32)],
                            N_DEV - 1, _gather_plan)
    _, (r2_in,) = rs_in_chips.wait(gather_part.token)
    results["w_in"] = [o[None] for o in _adam_sharded(w_in[0], m_w_in[0], v_w_in[0], own_in, r2_in, "adam_w_in")]
    _, (got,) = gather_part.wait(results["w_in"][1])
    got = lax.dynamic_update_slice(got, partials[None], (me, 0, 0))
    tot = _sum_devices(got, "sum_partials")
    dmodc = tot[10:13].reshape(1, 3 * D)
    dmod16 = _rows(16, (0, got[:, 7:10, :].reshape(N_DEV, 3 * D)), (N_DEV, dmodc))
    dmod16 = lax.dynamic_slice(dmod16, (0, me * ADA_SH), (16, ADA_SH))
    dmodc8 = _rows(8, (0, dmod16[8:9]))
    g_ada_w, d_ada_w, nm_ada_w, nv_ada_w, cctx_part = _ada_backward(
        act16.T, dmod16, dmodc8, ada_w[0], m_ada_w[0], v_ada_w[0])
    gather_cctx = _Exchange("gather_cctx", [cctx_part], [jax.ShapeDtypeStruct((N_DEV, 8, D), F32)],
                            N_DEV - 1, _gather_plan)
    _, (r2_sq,) = rs_sq_chips.wait(gather_cctx.token)
    square = _adam_square([(w_a[0], m_w_a[0], v_w_a[0]), (w_b[0], m_w_b[0], v_w_b[0]),
                           (w_out[0], m_w_out[0], v_w_out[0])], own_sq, r2_sq)
    _, (got_cctx,) = gather_cctx.wait(square[0][1])
    dact_ctx = _sum_devices(lax.dynamic_update_slice(got_cctx, cctx_part[None], (me, 0, 0)), "sum_cctx")

    view = {"c_ctx": (1, D), "final_norm_w": (1, D)}
    given = {"c_ctx": (c_ctx, m_c_ctx, v_c_ctx), "norm_w": (norm_w, m_norm_w, v_norm_w),
             "conv_b": (conv_b, m_conv_b, v_conv_b), "gn_w": (gn_w, m_gn_w, v_gn_w),
             "final_norm_w": (final_norm_w, m_final_norm_w, v_final_norm_w), "ada_b": (ada_b, m_ada_b, v_ada_b),
             "decay_logit": (decay_logit, m_decay_logit, v_decay_logit), "conv_w": (conv_w, m_conv_w, v_conv_w)}
    small_in = [tuple(a.reshape(view.get(name, a.shape)) for a in given[name]) for name in SMALL_PARAMS]
    conv_w_grad = lax.dynamic_slice(tot, (4, me * csh), (3, csh))
    small_out = _adam_small(tot, dact_ctx, conv_w_grad, tot[14, 0:2 * N_HEADS].reshape(2, N_HEADS), small_in)
    results.update({name: [o.reshape(given[name][0].shape) for o in outs_]
                    for name, outs_ in zip(SMALL_PARAMS, small_out)})
    for name, outs_ in zip(("w_a", "w_b", "w_out"), square):
        results[name] = [o[None] for o in outs_]
    results["ada_w"] = [o[None] for o in (g_ada_w, d_ada_w, nm_ada_w, nv_ada_w)]

    order = ("c_ctx", "norm_w", "ada_w", "ada_b", "w_in", "conv_w", "conv_b", "decay_logit", "gn_w",
             "w_a", "w_b", "w_out", "final_norm_w")
    outs = [results[name][kind] for kind in range(4) for name in order]
    return (tot[13, 0], grad_x[None], *outs)
```

```python
import math

import jax
import jax.numpy as jnp
from jax import lax
from jax.experimental import pallas as pl
from jax.experimental.pallas import tpu as pltpu

F32 = jnp.float32
BF16 = jnp.bfloat16
MESH = pl.DeviceIdType.MESH
_HBM_SPEC = pl.BlockSpec(memory_space=pltpu.HBM)
_SEM_SPEC = pl.BlockSpec(memory_space=pltpu.SEMAPHORE)
_EFFECT = pltpu.SideEffectType.DATAFLOW_SIDE_EFFECTING

N_DEV = 8
D = 1024
N_HEADS = 8
DK = 64
DV = 128
GRID_W = 64
ROPE_BASE = 10000.0
EPS = 1e-6
PW = 9 * D
WSH = PW // N_DEV
RSH = D // N_DEV
ADA_SH = 3 * D // N_DEV
TM = 256
RET_C = 256
HALO = 16
LANES = 128
VMEM_LIMIT = 60 * 1024 * 1024

ADAM_LR = 0.001
ADAM_B1 = 0.9
ADAM_B2 = 0.999
ADAM_EPS = 1e-08
ADAM_WD = 0.01
ADAM_STEP = 10

CB_H, CB_BG, CB_CG, CB_ZA, CB_QK, CB_V, CB_ZB, CB_GA, CB_GB = range(9)


def _cparams(**kw):
    return pltpu.CompilerParams(vmem_limit_bytes=VMEM_LIMIT, **kw)


def _dot(a, b):
    return jnp.dot(a, b, preferred_element_type=F32)


def _dot_nt(a, b):
    return lax.dot_general(a, b, (((1,), (1,)), ((), ())), preferred_element_type=F32)


def _dot_tn(a, b):
    return lax.dot_general(a, b, (((0,), (0,)), ((), ())), preferred_element_type=F32)


def _sigmoid(z):
    return 0.5 * jnp.tanh(0.5 * z) + 0.5


def _row_tile(rows):
    return TM if rows % TM == 0 else rows


def _coords():
    return lax.axis_index("x"), lax.axis_index("y"), lax.axis_index("c")


def _flip(v, bit):
    return 1 - v if bit else v


def _all_gather_small(blk, name):
    rows, cols = blk.shape

    def body(x_ref, out_ref, done, send_sems, recv_sems, local_sem):
        x, y, c = _coords()
        me = 4 * x + 2 * y + c
        mine = pltpu.make_async_copy(x_ref, out_ref.at[me], local_sem)
        mine.start()

        def copy(k, slot):
            peer = (_flip(x, k & 4), _flip(y, k & 2), _flip(c, k & 1))
            return pltpu.make_async_remote_copy(
                src_ref=x_ref, dst_ref=out_ref.at[slot], send_sem=send_sems.at[k - 1],
                recv_sem=recv_sems.at[k - 1], device_id=peer, device_id_type=MESH)

        for k in range(1, N_DEV):
            copy(k, me).start()
        for k in range(1, N_DEV):
            copy(k, jnp.bitwise_xor(me, k)).wait_recv()
        for k in range(1, N_DEV):
            copy(k, me).wait_send()
        mine.wait()
        done[...] = jnp.zeros((8, LANES), F32)

    vmem = pl.BlockSpec(memory_space=pltpu.VMEM)
    return pl.pallas_call(
        body, name=name,
        out_shape=[jax.ShapeDtypeStruct((N_DEV, rows, cols), blk.dtype), jax.ShapeDtypeStruct((8, LANES), F32)],
        in_specs=[vmem], out_specs=[vmem, vmem],
        scratch_shapes=[pltpu.SemaphoreType.DMA((N_DEV - 1,)), pltpu.SemaphoreType.DMA((N_DEV - 1,)),
                        pltpu.SemaphoreType.DMA],
    )(blk)


class _Absent:
    at = property(lambda self: self)

    def __getitem__(self, index):
        return self


class _Exchange:
    def __init__(self, name, srcs, land_shapes, n_copies, plan, after=None):
        self.name, self.plan, self.n_copies = name, plan, n_copies
        self.n_src, self.n_land = len(srcs), len(land_shapes)
        hbm = lambda a: pltpu.HBM(a.shape, a.dtype)
        n = self.n_src + self.n_land
        lands = [pltpu.with_memory_space_constraint(lax.empty(s.shape, s.dtype), pltpu.HBM) for s in land_shapes]
        srcs = [pltpu.with_memory_space_constraint(s, pltpu.HBM) for s in srcs]
        extra = [] if after is None else [after]

        def body(*refs):
            send_sems, recv_sems = refs[n + len(extra)], refs[n + len(extra) + 1]
            for cp in self._copies(refs, send_sems, recv_sems):
                cp.start()
            refs[-1][...] = jnp.zeros((8, LANES), F32)

        outs = pl.pallas_call(
            body, name=name + "_start",
            out_shape=(pltpu.SemaphoreType.DMA((n_copies,)), pltpu.SemaphoreType.DMA((n_copies,)),
                       *[hbm(a) for a in srcs], *[hbm(a) for a in lands], jax.ShapeDtypeStruct((8, LANES), F32)),
            in_specs=[_HBM_SPEC] * n + [pl.BlockSpec(memory_space=pl.ANY)] * len(extra),
            out_specs=(_SEM_SPEC, _SEM_SPEC, *[_HBM_SPEC] * n, pl.BlockSpec(memory_space=pltpu.VMEM)),
            input_output_aliases={i: 2 + i for i in range(n)},
            compiler_params=pltpu.CompilerParams(has_side_effects=_EFFECT),
        )(*srcs, *lands, *extra)
        self.send_sems, self.recv_sems = outs[0], outs[1]
        self.buffers = list(outs[2:2 + n])
        self.token = outs[-1]
        self.waited = 0

    def _copies(self, refs, send_sems, recv_sems, lo=0, hi=None):
        src_refs, land_refs = refs[:self.n_src], refs[self.n_src:self.n_src + self.n_land]
        planned = list(enumerate(self.plan(src_refs, land_refs)))[lo:hi]
        return [pltpu.make_async_remote_copy(src_ref=s, dst_ref=d, send_sem=send_sems.at[k], recv_sem=recv_sems.at[k],
                                             device_id=dev, device_id_type=MESH) for k, (s, d, dev) in planned]

    def wait(self, after, count=None, through=None):
        n = self.n_src + self.n_land
        lo = self.waited
        hi = self.n_copies if count is None else lo + count
        self.waited = hi
        through = list(range(n)) if through is None else list(through)
        m = len(through)

        def body(*refs):
            held = [_Absent()] * n
            for pos, ref in zip(through, refs[:m]):
                held[pos] = ref
            for cp in self._copies(held, refs[m], refs[m + 1], lo, hi):
                cp.wait_send()
                cp.wait_recv()

        outs = pl.pallas_call(
            body, name=f"{self.name}_wait{lo}" if lo or hi < self.n_copies else self.name + "_wait",
            out_shape=tuple(pltpu.HBM(self.buffers[p].shape, self.buffers[p].dtype) for p in through),
            in_specs=[_HBM_SPEC] * m + [_SEM_SPEC, _SEM_SPEC, pl.BlockSpec(memory_space=pl.ANY)],
            out_specs=tuple([_HBM_SPEC] * m),
            input_output_aliases={i: i for i in range(m)},
            compiler_params=pltpu.CompilerParams(has_side_effects=_EFFECT),
        )(*[self.buffers[p] for p in through], self.send_sems, self.recv_sems, after)
        for pos, out in zip(through, outs):
            self.buffers[pos] = out
        return list(self.buffers[:self.n_src]), list(self.buffers[self.n_src:])


def _pair_plan(src_index):
    def plan(srcs, lands):
        x, y, c = _coords()
        chip = 2 * x + y
        return [(srcs[0].at[src_index(j, chip, c)], lands[0].at[j], (x, y, 1 - c)) for j in range(4)]
    return plan


def _chips_plan(srcs, lands):
    x, y, c = _coords()
    return [(srcs[0].at[j - 1], lands[0].at[j - 1], (_flip(x, j & 2), _flip(y, j & 1), c)) for j in range(1, 4)]


def _peer(k):
    x, y, c = _coords()
    return _flip(x, k & 4), _flip(y, k & 2), _flip(c, k & 1)


def _send_plan(ks):
    def plan(srcs, lands):
        return [(srcs[0].at[0], lands[0].at[p], _peer(k)) for p, k in enumerate(ks)]
    return plan


def _forward_plan(moves):
    def plan(srcs, lands):
        return [(srcs[0].at[s], srcs[0].at[d], _peer(1)) for s, d in moves]
    return plan


def _gather_plan(srcs, lands):
    x, y, c = _coords()
    me = 4 * x + 2 * y + c
    return [(srcs[a], lands[a].at[me], (_flip(x, k & 4), _flip(y, k & 2), _flip(c, k & 1)))
            for a in range(len(srcs)) for k in range(1, N_DEV)]


def _pair_sum(grad, recv, owners, name):
    _, rows, cols = grad.shape
    tr = rows

    def body(own_ref, g_ref, r_ref, mine_ref, send_ref):
        j = pl.program_id(1)
        total = g_ref[...] + r_ref[...]

        @pl.when(j == 0)
        def _():
            mine_ref[...] = total

        @pl.when(j > 0)
        def _():
            send_ref[...] = total.astype(BF16)

    grid_spec = pltpu.PrefetchScalarGridSpec(
        num_scalar_prefetch=1, grid=(rows // tr, 4),
        in_specs=[pl.BlockSpec((1, tr, cols), lambda i, j, own: (own[j], i, 0)),
                  pl.BlockSpec((1, tr, cols), lambda i, j, own: (j, i, 0))],
        out_specs=[pl.BlockSpec((1, tr, cols), lambda i, j, own: (0, i, 0)),
                   pl.BlockSpec((1, tr, cols), lambda i, j, own: (jnp.maximum(j - 1, 0), i, 0))])
    return pl.pallas_call(
        body, name=name, grid_spec=grid_spec,
        out_shape=[jax.ShapeDtypeStruct((1, rows, cols), F32), jax.ShapeDtypeStruct((3, rows, cols), BF16)],
        compiler_params=_cparams(dimension_semantics=("arbitrary", "arbitrary")),
    )(owners, grad, recv)


def _modulation(c16, ada_w_sh, ada_b_sh, decay_logit):
    def body(c_ref, w_ref, b_ref, dl_ref, mod_ref, act_ref, lg_ref):
        cv = c_ref[...]
        act = cv * _sigmoid(cv)
        act_ref[...] = act
        mod_ref[...] = jnp.dot(act, w_ref[...], preferred_element_type=F32,
                               precision=lax.Precision.HIGHEST) + b_ref[...]
        z = dl_ref[...]
        lg_ref[...] = jnp.minimum(z, 0.0) - jnp.log(1.0 + jnp.exp(-jnp.abs(z)))

    return pl.pallas_call(
        body, name="modulation",
        out_shape=[jax.ShapeDtypeStruct((16, ADA_SH), F32), jax.ShapeDtypeStruct((16, D), F32),
                   jax.ShapeDtypeStruct(decay_logit.shape, F32)],
        compiler_params=_cparams(),
    )(c16, ada_w_sh, ada_b_sh, decay_logit)


def _adam_update(w, g, m, v):
    m2 = ADAM_B1 * m + (1.0 - ADAM_B1) * g
    v2 = ADAM_B2 * v + (1.0 - ADAM_B2) * (g * g)
    m_hat = m2 / (1.0 - ADAM_B1 ** ADAM_STEP)
    v_hat = v2 / (1.0 - ADAM_B2 ** ADAM_STEP)
    delta = -ADAM_LR * (m_hat / (jnp.sqrt(v_hat) + ADAM_EPS) + ADAM_WD * w)
    return delta, m2, v2


def _adam_sharded(w, m, v, own, recv, name):
    rows, cols = w.shape
    tr = _row_tile(rows)

    def body(w_ref, m_ref, v_ref, g0, g1, g2, g3, g_ref, d_ref, m_out, v_out):
        g = ((g0[0] + g1[0].astype(F32)) + g2[0].astype(F32)) + g3[0].astype(F32)
        delta, m2, v2 = _adam_update(w_ref[...], g, m_ref[...], v_ref[...])
        g_ref[...] = g
        d_ref[...] = delta
        m_out[...] = m2
        v_out[...] = v2

    flat = pl.BlockSpec((tr, cols), lambda i: (i, 0))
    part = lambda j: pl.BlockSpec((1, tr, cols), lambda i, j=j: (j, i, 0))
    return pl.pallas_call(
        body, name=name, grid=(rows // tr,),
        in_specs=[flat, flat, flat, part(0), part(0), part(1), part(2)],
        out_specs=[flat] * 4,
        out_shape=[jax.ShapeDtypeStruct((rows, cols), F32)] * 4,
        compiler_params=_cparams(dimension_semantics=("arbitrary",)),
    )(w, m, v, own, recv, recv, recv)


def _ada_backward(act_t, dmod16, dmodc8, ada_w_sh, m, v):
    def body(at_ref, dm_ref, dc_ref, w_ref, m_ref, v_ref, g_ref, d_ref, m_out, v_out, pc_ref):
        g = jnp.dot(at_ref[...], dm_ref[...], preferred_element_type=F32, precision=lax.Precision.HIGHEST)
        w = w_ref[...]
        delta, m2, v2 = _adam_update(w, g, m_ref[...], v_ref[...])
        g_ref[...] = g
        d_ref[...] = delta
        m_out[...] = m2
        v_out[...] = v2
        pc_ref[...] = lax.dot_general(dc_ref[...], w, (((1,), (1,)), ((), ())), preferred_element_type=F32,
                                      precision=lax.Precision.HIGHEST)

    return pl.pallas_call(
        body, name="ada_backward",
        out_shape=[jax.ShapeDtypeStruct((D, ADA_SH), F32)] * 4 + [jax.ShapeDtypeStruct((8, D), F32)],
        compiler_params=_cparams(),
    )(act_t, dmod16, dmodc8, ada_w_sh, m, v)


def _sum_devices(gathered, name):
    _, rows, cols = gathered.shape

    def body(g_ref, o_ref):
        acc = g_ref[0]
        for d in range(1, N_DEV):
            acc = acc + g_ref[d]
        o_ref[...] = acc

    return pl.pallas_call(body, name=name, out_shape=jax.ShapeDtypeStruct((rows, cols), F32),
                          compiler_params=_cparams())(gathered)


SMALL_PARAMS = ("c_ctx", "norm_w", "conv_b", "gn_w", "final_norm_w", "ada_b", "decay_logit", "conv_w")


def _adam_small(tot, dact_ctx, conv_w_grad, decay_grad, params):
    n = len(SMALL_PARAMS)

    def body(tot_ref, dact_ref, cwg_ref, dlg_ref, *refs):
        ins, outs = refs[:3 * n], refs[3 * n:]
        raw = {
            "c_ctx": dact_ref[0:1, :],
            "norm_w": tot_ref[1:2, :], "conv_b": tot_ref[2:3, :], "gn_w": tot_ref[3:4, :],
            "final_norm_w": tot_ref[0:1, :],
            "ada_b": jnp.concatenate([tot_ref[7 + r:8 + r, :] + tot_ref[10 + r:11 + r, :] for r in range(3)], axis=1),
            "decay_logit": dlg_ref[...],
            "conv_w": cwg_ref[...],
        }
        for k, name in enumerate(SMALL_PARAMS):
            lead = (0,) if len(ins[3 * k].shape) == 3 else ()
            w, m, v = (ref[lead + (...,)] for ref in ins[3 * k:3 * k + 3])
            g = raw[name]
            if name == "c_ctx":
                s = _sigmoid(w)
                g = g * (s * (1.0 + w * (1.0 - s)))
            elif name == "decay_logit":
                g = g * (1.0 - _sigmoid(w))
            delta, m2, v2 = _adam_update(w, g, m, v)
            for ref, val in zip(outs[4 * k:4 * k + 4], (g, delta, m2, v2)):
                ref[lead + (...,)] = val

    flat = [a for trio in params for a in trio]
    out_shape = [jax.ShapeDtypeStruct(trio[0].shape, F32) for trio in params for _ in range(4)]
    outs = pl.pallas_call(body, name="adam_small", out_shape=out_shape, compiler_params=_cparams())(
        tot, dact_ctx, conv_w_grad, decay_grad, *flat)
    return [outs[4 * k:4 * k + 4] for k in range(n)]


def _adam_square(params, own, recv):
    def body(own_ref, recv_ref, *refs):
        ins, outs = refs[:9], refs[9:]
        for k in range(3):
            rows = slice(k * RSH, (k + 1) * RSH)
            g = own_ref[0, rows, :]
            for j in range(3):
                g = g + recv_ref[j, rows, :].astype(F32)
            delta, m2, v2 = _adam_update(ins[3 * k][...], g, ins[3 * k + 1][...], ins[3 * k + 2][...])
            for ref, val in zip(outs[4 * k:4 * k + 4], (g, delta, m2, v2)):
                ref[...] = val

    flat = [a for trio in params for a in trio]
    outs = pl.pallas_call(body, name="adam_square", out_shape=[jax.ShapeDtypeStruct((RSH, D), F32)] * 12,
                          compiler_params=_cparams())(own, recv, *flat)
    return [outs[4 * k:4 * k + 4] for k in range(3)]


def _rope(t, cos, sin):
    lane = lax.broadcasted_iota(jnp.int32, (1, LANES), 1)
    first_half = jnp.bitwise_and(lane, DK // 2) == 0
    partner = jnp.where(first_half, pltpu.roll(t, LANES - DK // 2, 1), pltpu.roll(t, DK // 2, 1))
    return t * cos + partner * sin


class _Shards:
    def __init__(self, arrays, picks):
        self.arrays, self.picks = list(arrays), list(picks)

    def __add__(self, other):
        shift = len(self.arrays)
        return _Shards(self.arrays + other.arrays, self.picks + [(a + shift, s) for a, s in other.picks])


def _load_w_in(step, ids_ref, shards, w_refs, w_vmem, sems):
    @pl.when(step == 0)
    def _():
        copies = []
        for p, (a, slot) in enumerate(shards.picks):
            col = pl.multiple_of(ids_ref[p] * WSH, LANES)
            copies.append(pltpu.make_async_copy(w_refs[a].at[slot], w_vmem.at[:, pl.ds(col, WSH)], sems.at[p]))
        for cp in copies:
            cp.start()
        for cp in copies:
            cp.wait()


def _project_columns(xmb, w_ref, shard, cos, sin, p_ref):
    width = 2 * WSH
    q_lo, k_lo, k_hi = CB_QK * D, CB_QK * D + N_HEADS * DK, CB_V * D
    q_pair, k_pair = q_lo // width, k_lo // width
    assert (q_pair + 1) * width == k_lo and k_pair * width == k_lo and k_hi <= (k_pair + 1) * width
    mxu_cols = 2 * LANES

    def columns(rot_lo, rot_hi, scale):
        for a in range(0, width, mxu_cols):
            acc = _dot(xmb, w_ref[:, a:a + mxu_cols])
            for col in range(a, a + mxu_cols, LANES):
                piece = acc[:, col - a:col - a + LANES]
                if rot_lo <= col < rot_hi:
                    piece = _rope(piece if scale == 1.0 else piece * scale, cos, sin)
                p_ref[:, col:col + LANES] = piece.astype(BF16)

    @pl.when(shard == q_pair)
    def _():
        columns(q_lo - q_pair * width, width, 1.0)

    @pl.when(shard == k_pair)
    def _():
        columns(0, k_hi - k_lo, DK ** -0.5)

    @pl.when(jnp.logical_and(shard != q_pair, shard != k_pair))
    def _():
        columns(0, 0, 1.0)


_AFTER_SPEC = pl.BlockSpec(memory_space=pl.ANY)


def _prenorm(x, ctx, modx, modc, norm_w, after):
    n_x_tiles = x.shape[0] // TM
    lext = x.shape[0] + ctx.shape[0]

    assert TM % GRID_W == 0 and TM // GRID_W <= 8
    grid_rows = TM // GRID_W

    def body(x_ref, ctx_ref, mx_ref, mc_ref, nw_ref, after_ref, xm_ref, xmt_ref, cos_ref, sin_ref, col_cos, col_sin):
        del after_ref
        i = pl.program_id(0)
        is_ctx = i >= n_x_tiles
        x = jnp.where(is_ctx, ctx_ref[...], x_ref[...])
        r = lax.rsqrt(jnp.mean(x * x, axis=-1, keepdims=True) + EPS)
        sh = jnp.where(is_ctx, mc_ref[0:1, :], mx_ref[0:1, :])
        sc = jnp.where(is_ctx, mc_ref[1:2, :], mx_ref[1:2, :])
        xm = (x * r * nw_ref[...]) * (1.0 + sc) + sh
        xm_ref[...] = xm.astype(BF16)
        xmt_ref[...] = xm.T.astype(BF16)
        lane = lax.broadcasted_iota(jnp.int32, (1, LANES), 1)
        n_freq = DK // 4
        inv = jnp.exp((lane % n_freq).astype(F32) * (-math.log(ROPE_BASE) / n_freq))
        by_column = (lane % (2 * n_freq)) >= n_freq
        sign = jnp.where((lane % DK) < DK // 2, -1.0, 1.0)

        @pl.when(i == 0)
        def _():
            col = (lax.broadcasted_iota(jnp.int32, (TM, 1), 0) % GRID_W).astype(F32) * inv
            col_cos[...] = jnp.cos(col)
            col_sin[...] = jnp.sin(col) * sign

        row = (i * grid_rows + lax.broadcasted_iota(jnp.int32, (8, 1), 0)).astype(F32) * inv
        row_cos, row_sin = jnp.cos(row), jnp.sin(row) * sign
        tall = lambda t: jnp.concatenate(
            [jnp.broadcast_to(t[g:g + 1, :], (GRID_W, LANES)) for g in range(grid_rows)], axis=0)
        cos_ref[...] = jnp.where(is_ctx, 1.0, jnp.where(by_column, col_cos[...], tall(row_cos)))
        sin_ref[...] = jnp.where(is_ctx, 0.0, jnp.where(by_column, col_sin[...], tall(row_sin)))

    full = lambda a: pl.BlockSpec(a.shape, lambda i: (0,) * a.ndim)
    row = lambda width: pl.BlockSpec((TM, width), lambda i: (i, 0))
    return pl.pallas_call(
        body, name="prenorm", grid=(lext // TM,),
        in_specs=[pl.BlockSpec((TM, D), lambda i: (jnp.minimum(i, n_x_tiles - 1), 0)), full(ctx),
                  full(modx), full(modc), full(norm_w), _AFTER_SPEC],
        out_specs=[row(D), pl.BlockSpec((D, TM), lambda i: (0, i)), row(LANES), row(LANES)],
        out_shape=[jax.ShapeDtypeStruct((lext, D), BF16), jax.ShapeDtypeStruct((D, lext), BF16),
                   jax.ShapeDtypeStruct((lext, LANES), F32), jax.ShapeDtypeStruct((lext, LANES), F32)],
        scratch_shapes=[pltpu.VMEM((TM, LANES), F32), pltpu.VMEM((TM, LANES), F32)],
        compiler_params=_cparams(dimension_semantics=("arbitrary",)),
    )(x, ctx, modx, modc, norm_w, after)


def _in_projection(p_ext, xm, cos_t, sin_t, w, pairs, me, name):
    lext = xm.shape[0]
    tr = lext // 2
    meta = []
    for (slot_a, k_a), (slot_b, k_b) in pairs:
        assert k_a ^ k_b == 1
        dev_a = jnp.bitwise_xor(me, k_a)
        a_low = dev_a % 2 == 0
        meta += [jnp.where(a_low, slot_a, slot_b), jnp.where(a_low, slot_b, slot_a), dev_a // 2]
    meta = jnp.stack(meta).astype(jnp.int32)

    def body(meta_ref, xm_ref, cos_ref, sin_ref, w_hbm, *rest):
        p_ref, w_ref, sems = rest[-3:]
        j, i = pl.program_id(0), pl.program_id(1)

        @pl.when(i == 0)
        def _():
            copies = [pltpu.make_async_copy(w_hbm.at[meta_ref[3 * j + h]], w_ref.at[:, h * WSH:(h + 1) * WSH],
                                            sems.at[h]) for h in range(2)]
            for cp in copies:
                cp.start()
            for cp in copies:
                cp.wait()

        _project_columns(xm_ref[...], w_ref, meta_ref[3 * j + 2], cos_ref[...], sin_ref[...], p_ref)

    row = lambda width: pl.BlockSpec((tr, width), lambda j, i, meta_: (i, 0))
    grid_spec = pltpu.PrefetchScalarGridSpec(
        num_scalar_prefetch=1, grid=(len(pairs), lext // tr),
        in_specs=[row(D), row(LANES), row(LANES), pl.BlockSpec(memory_space=pl.ANY)]
        + ([] if p_ext is None else [pl.BlockSpec(memory_space=pl.ANY)]),
        out_specs=pl.BlockSpec((tr, 2 * WSH), lambda j, i, meta_: (i, meta_[3 * j + 2])),
        scratch_shapes=[pltpu.VMEM((D, 2 * WSH), BF16), pltpu.SemaphoreType.DMA((2,))])
    return pl.pallas_call(
        body, name=name, grid_spec=grid_spec,
        out_shape=jax.ShapeDtypeStruct((lext, PW), BF16),
        input_output_aliases={} if p_ext is None else {5: 0},
        compiler_params=_cparams(dimension_semantics=("arbitrary", "arbitrary")),
    )(meta, xm, cos_t, sin_t, w, *([] if p_ext is None else [p_ext]))


def _decay_tables(lgf, lgb, n):
    i = lax.broadcasted_iota(jnp.int32, (n, 1), 0).astype(F32)
    return dict(i=i, k_f=jnp.exp(lgf * (n - 1.0 - i)), k_b=jnp.exp(lgb * i),
                q_f=jnp.exp(lgf * (i + 1.0)), q_b=jnp.exp(lgb * (n - i)))


def _decay_matrix(lgf, lgb, n):
    ii = lax.broadcasted_iota(jnp.int32, (n, n), 0)
    jj = lax.broadcasted_iota(jnp.int32, (n, n), 1)
    diff = (ii - jj).astype(F32)
    low = jnp.exp(lgf * jnp.maximum(diff, 0.0))
    up = jnp.exp(lgb * jnp.maximum(-diff, 0.0))
    return jnp.where(diff > 0, low, jnp.where(diff < 0, up, 2.0)), diff


def _cat_lanes(a, b):
    return jnp.concatenate([a.astype(BF16), b.astype(BF16)], axis=1)


def _retention_forward(p_ext, lg, seq, ctx_len):
    lext = seq + ctx_len
    n_chunks = seq // RET_C
    C = RET_C

    def body(lg_ref, q_ref, k_ref, v_ref, rn_ref, rstd_ref, kv_scr, sf_scr, sb_scr):
        pair = pl.program_id(0)
        lane = lax.broadcasted_iota(jnp.int32, (1, LANES), 1)
        heads = range(2)
        hmask = [(lane // DK == hh).astype(F32) for hh in heads]
        lgf = [lg_ref[0, 2 * pair + hh] for hh in heads]
        lgb = [lg_ref[1, 2 * pair + hh] for hh in heads]
        vs = [slice(hh * DV, (hh + 1) * DV) for hh in heads]
        t = [_decay_tables(lgf[hh], lgb[hh], C) for hh in heads]
        kc_all = k_ref[seq:lext, :].astype(F32)
        s0 = []
        for hh in heads:
            tc = _decay_tables(lgf[hh], lgb[hh], ctx_len)
            kc = kc_all * hmask[hh]
            s0.append(_dot_tn(_cat_lanes(kc * tc["k_f"], kc * tc["k_b"]), v_ref[seq:lext, vs[hh]]))

        def increments(c, carry):
            rows = pl.ds(pl.multiple_of(c * C, C), C)
            k_all = k_ref[rows, :].astype(F32)
            for hh in heads:
                k = k_all * hmask[hh]
                kv_scr[hh, c] = _dot_tn(_cat_lanes(k * t[hh]["k_f"], k * t[hh]["k_b"]), v_ref[rows, vs[hh]])
            return carry

        lax.fori_loop(0, n_chunks, increments, 0, unroll=2)
        gf_c = [jnp.exp(lgf[hh] * C) for hh in heads]
        gb_c = [jnp.exp(lgb[hh] * C) for hh in heads]

        def scan_f(c, s):
            for hh in heads:
                sf_scr[hh, c] = s[hh]
            return tuple(gf_c[hh] * s[hh] + kv_scr[hh, c, 0:LANES, :] for hh in heads)

        def scan_b(n, s):
            c = n_chunks - 1 - n
            for hh in heads:
                sb_scr[hh, c] = s[hh]
            return tuple(gb_c[hh] * s[hh] + kv_scr[hh, c, LANES:2 * LANES, :] for hh in heads)

        lax.fori_loop(0, n_chunks, scan_f, tuple(s0[hh][0:LANES] for hh in heads))
        lax.fori_loop(0, n_chunks, scan_b, tuple(s0[hh][LANES:2 * LANES] for hh in heads))
        dmat = [_decay_matrix(lgf[hh], lgb[hh], C)[0] for hh in heads]

        def outputs(c, carry):
            rows = pl.ds(pl.multiple_of(c * C, C), C)
            q_all = q_ref[rows, :].astype(F32)
            k = k_ref[rows, :]
            for hh in heads:
                q = q_all * hmask[hh]
                v = v_ref[rows, vs[hh]]
                s = _dot_nt(q.astype(BF16), k)
                inner = _dot((s * dmat[hh]).astype(BF16), v)
                states = jnp.concatenate([sf_scr[hh, c], sb_scr[hh, c]], axis=0).astype(BF16)
                ret = inner + _dot(_cat_lanes(q * t[hh]["q_f"], q * t[hh]["q_b"]), states)
                mu = jnp.mean(ret, axis=-1, keepdims=True)
                cen = ret - mu
                rstd = lax.rsqrt(jnp.mean(cen * cen, axis=-1, keepdims=True) + EPS)
                rn_ref[rows, vs[hh]] = cen * rstd
                rstd_ref[rows, vs[hh]] = jnp.broadcast_to(rstd, (C, DV))
            return carry

        lax.fori_loop(0, n_chunks, outputs, 0, unroll=2)

    qk0 = CB_QK * D // LANES
    return pl.pallas_call(
        body, name="retention_forward", grid=(N_HEADS // 2,),
        in_specs=[pl.BlockSpec(memory_space=pltpu.SMEM),
                  pl.BlockSpec((lext, LANES), lambda g: (0, qk0 + g)),
                  pl.BlockSpec((lext, LANES), lambda g: (0, qk0 + N_HEADS // 2 + g)),
                  pl.BlockSpec((lext, 2 * DV), lambda g: (0, CB_V * D // (2 * DV) + g))],
        out_specs=[pl.BlockSpec((seq, 2 * DV), lambda g: (0, g))] * 2,
        out_shape=[jax.ShapeDtypeStruct((seq, D), F32)] * 2,
        scratch_shapes=[pltpu.VMEM((2, n_chunks, 2 * LANES, DV), F32), pltpu.VMEM((2, n_chunks, LANES, DV), F32),
                        pltpu.VMEM((2, n_chunks, LANES, DV), F32)],
        compiler_params=_cparams(dimension_semantics=("arbitrary",)),
    )(lg, p_ext, p_ext, p_ext)


def _retention_backward(p_ext, dret, lg, cos_t, sin_t, seq, ctx_len, after):
    lext = seq + ctx_len
    n_chunks = seq // RET_C
    C = RET_C

    def body(lg_ref, q_ref, k_ref, v_ref, do_ref, cos_ref, sin_ref, after_ref, dq_ref, dk_ref, dv_ref, dlg_ref,
             kv_scr, g_scr, sf_scr, sb_scr, gfn_scr, gbp_scr):
        pair = pl.program_id(0)
        lane = lax.broadcasted_iota(jnp.int32, (1, LANES), 1)
        heads = range(2)
        hmask = [(lane // DK == hh).astype(F32) for hh in heads]
        lgf = [lg_ref[0, 2 * pair + hh] for hh in heads]
        lgb = [lg_ref[1, 2 * pair + hh] for hh in heads]
        vs = [slice(hh * DV, (hh + 1) * DV) for hh in heads]
        t = [_decay_tables(lgf[hh], lgb[hh], C) for hh in heads]
        tc = [_decay_tables(lgf[hh], lgb[hh], ctx_len) for hh in heads]
        kc_all = k_ref[seq:lext, :].astype(F32)
        kc = [kc_all * hmask[hh] for hh in heads]
        vc = [v_ref[seq:lext, vs[hh]] for hh in heads]
        kc_cat = [_cat_lanes(kc[hh] * tc[hh]["k_f"], kc[hh] * tc[hh]["k_b"]) for hh in heads]
        s0 = [_dot_tn(kc_cat[hh], vc[hh]) for hh in heads]

        def increments(c, carry):
            rows = pl.ds(pl.multiple_of(c * C, C), C)
            k_all = k_ref[rows, :].astype(F32)
            q_all = q_ref[rows, :].astype(F32)
            for hh in heads:
                k, q = k_all * hmask[hh], q_all * hmask[hh]
                kv_scr[hh, c] = _dot_tn(_cat_lanes(k * t[hh]["k_f"], k * t[hh]["k_b"]), v_ref[rows, vs[hh]])
                g_scr[hh, c] = _dot_tn(_cat_lanes(q * t[hh]["q_f"], q * t[hh]["q_b"]), do_ref[rows, vs[hh]])
            return carry

        lax.fori_loop(0, n_chunks, increments, 0, unroll=2)
        gf_c = [jnp.exp(lgf[hh] * C) for hh in heads]
        gb_c = [jnp.exp(lgb[hh] * C) for hh in heads]

        def scan_f(c, s):
            for hh in heads:
                sf_scr[hh, c] = s[hh]
            return tuple(gf_c[hh] * s[hh] + kv_scr[hh, c, 0:LANES, :] for hh in heads)

        def scan_b(n, s):
            c = n_chunks - 1 - n
            for hh in heads:
                sb_scr[hh, c] = s[hh]
            return tuple(gb_c[hh] * s[hh] + kv_scr[hh, c, LANES:2 * LANES, :] for hh in heads)

        def scan_gf(n, carry):
            c = n_chunks - 1 - n
            for hh in heads:
                gfn_scr[hh, c] = carry[hh]
            return tuple(g_scr[hh, c, 0:LANES, :] + gf_c[hh] * carry[hh] for hh in heads)

        def scan_gb(c, carry):
            for hh in heads:
                gbp_scr[hh, c] = carry[hh]
            return tuple(g_scr[hh, c, LANES:2 * LANES, :] + gb_c[hh] * carry[hh] for hh in heads)

        lax.fori_loop(0, n_chunks, scan_f, tuple(s0[hh][0:LANES] for hh in heads))
        lax.fori_loop(0, n_chunks, scan_b, tuple(s0[hh][LANES:2 * LANES] for hh in heads))
        zero_state = jnp.zeros((LANES, DV), F32)
        gf_first = lax.fori_loop(0, n_chunks, scan_gf, (zero_state, zero_state))
        gb_last = lax.fori_loop(0, n_chunks, scan_gb, (zero_state, zero_state))

        dmat, w_f, w_b = [], [], []
        for hh in heads:
            dm, diff = _decay_matrix(lgf[hh], lgb[hh], C)
            dmat.append(dm)
            w_f.append(jnp.where(diff > 0, diff * dm, 0.0))
            w_b.append(jnp.where(diff < 0, -diff * dm, 0.0))

        def total(a):
            rows_, width = a.shape
            part = jnp.sum(a.reshape(rows_ // 8, 8, width), axis=0)
            return part[:, 0:LANES] + part[:, LANES:2 * LANES] if width == 2 * LANES else part

        def chunk_head(hh, c, rows, q_all, k_all, dlgf, dlgb):
            th = t[hh]
            qm = q_all * hmask[hh]
            km = k_all * hmask[hh]
            qb, kb = qm.astype(BF16), km.astype(BF16)
            v = v_ref[rows, vs[hh]]
            do = do_ref[rows, vs[hh]]
            s = _dot_nt(qb, kb)
            dsd = _dot_nt(do, v)
            ds = (dsd * dmat[hh]).astype(BF16)
            dq_in = _dot(ds, kb)
            dk_in = _dot_tn(ds, qb)
            dv_in = _dot_tn((s * dmat[hh]).astype(BF16), do)
            prod = s * dsd
            dlgf = dlgf + total(prod * w_f[hh])
            dlgb = dlgb + total(prod * w_b[hh])
            sf, sb = sf_scr[hh, c], sb_scr[hh, c]
            states = jnp.concatenate([sf, sb], axis=0).astype(BF16)
            dqc = _dot_nt(do, states)
            dqf = dqc[:, 0:LANES] * th["q_f"]
            dqb = dqc[:, LANES:2 * LANES] * th["q_b"]
            dq = (dq_in + dqf + dqb) * hmask[hh]
            dlgf = dlgf + total((th["i"] + 1.0) * (qm * dqf))
            dlgb = dlgb + total((C - th["i"]) * (qm * dqb))
            gfn, gbp = gfn_scr[hh, c], gbp_scr[hh, c]
            gstates = jnp.concatenate([gfn, gbp], axis=0).astype(BF16)
            dkc = _dot_nt(v, gstates)
            dkf = dkc[:, 0:LANES] * th["k_f"]
            dkb = dkc[:, LANES:2 * LANES] * th["k_b"]
            dk = dk_in + dkf + dkb
            dlgf = dlgf + total((C - 1.0 - th["i"]) * (km * dkf)) + C * gf_c[hh] * total(gfn * sf)
            dlgb = dlgb + total(th["i"] * (km * dkb)) + C * gb_c[hh] * total(gbp * sb)
            dv = dv_in + _dot(_cat_lanes(km * th["k_f"], km * th["k_b"]), gstates)
            dv_ref[rows, vs[hh]] = dv.astype(BF16)
            return dq, dk, dlgf, dlgb

        def chunk(c, carry):
            rows = pl.ds(pl.multiple_of(c * C, C), C)
            q_all = q_ref[rows, :].astype(F32)
            k_all = k_ref[rows, :].astype(F32)
            dq0, dk0, f0, b0 = chunk_head(0, c, rows, q_all, k_all, carry[0], carry[1])
            dq1, dk1, f1, b1 = chunk_head(1, c, rows, q_all, k_all, carry[2], carry[3])
            cos, sin = cos_ref[rows, :], sin_ref[rows, :]
            dq_ref[rows, :] = _rope(dq0 + dq1, cos, -sin).astype(BF16)
            dk_ref[rows, :] = (_rope(dk0 + dk1, cos, -sin) * (DK ** -0.5)).astype(BF16)
            return f0, b0, f1, b1

        zero = jnp.zeros((8, LANES), F32)
        sums = lax.fori_loop(0, n_chunks, chunk, (zero, zero, zero, zero), unroll=2)

        dlg = []
        dk_ctx = jnp.zeros((ctx_len, LANES), F32)
        for hh in heads:
            g0 = jnp.concatenate([gf_first[hh], gb_last[hh]], axis=0).astype(BF16)
            dkcc = _dot_nt(vc[hh], g0)
            dkcf = dkcc[:, 0:LANES] * tc[hh]["k_f"]
            dkcb = dkcc[:, LANES:2 * LANES] * tc[hh]["k_b"]
            dlgf = sums[2 * hh] + total((ctx_len - 1.0 - tc[hh]["i"]) * (kc[hh] * dkcf))
            dlgb = sums[2 * hh + 1] + total(tc[hh]["i"] * (kc[hh] * dkcb))
            dk_ctx = dk_ctx + (dkcf + dkcb) * (DK ** -0.5)
            dv_ref[seq:lext, vs[hh]] = _dot(kc_cat[hh], g0).astype(BF16)
            dlg += [jnp.sum(jnp.sum(a, axis=1, keepdims=True), axis=0, keepdims=True) for a in (dlgf, dlgb)]
        dk_ref[seq:lext, :] = dk_ctx.astype(BF16)
        dq_ref[seq:lext, :] = jnp.zeros((ctx_len, LANES), BF16)

        lane8 = lax.broadcasted_iota(jnp.int32, (8, LANES), 1)
        out = jnp.zeros((8, LANES), F32)
        for n, val in enumerate(dlg):
            out = jnp.where(lane8 == n, val, out)
        dlg_ref[0] = out

    qk0 = CB_QK * D // LANES
    q_spec = pl.BlockSpec((lext, LANES), lambda g: (0, qk0 + g))
    k_spec = pl.BlockSpec((lext, LANES), lambda g: (0, qk0 + N_HEADS // 2 + g))
    v_spec = pl.BlockSpec((lext, 2 * DV), lambda g: (0, CB_V * D // (2 * DV) + g))
    table = pl.BlockSpec((lext, LANES), lambda g: (0, 0))
    state = pltpu.VMEM((2, n_chunks, LANES, DV), F32)
    return pl.pallas_call(
        body, name="retention_backward", grid=(N_HEADS // 2,),
        in_specs=[pl.BlockSpec(memory_space=pltpu.SMEM), q_spec, k_spec, v_spec,
                  pl.BlockSpec((seq, 2 * DV), lambda g: (0, g)), table, table, _AFTER_SPEC],
        out_specs=[pl.BlockSpec((lext, LANES), lambda g: (0, g)), pl.BlockSpec((lext, LANES), lambda g: (0, g)),
                   pl.BlockSpec((lext, 2 * DV), lambda g: (0, g)), pl.BlockSpec((1, 8, LANES), lambda g: (g, 0, 0))],
        out_shape=[jax.ShapeDtypeStruct((lext, N_HEADS * DK), BF16), jax.ShapeDtypeStruct((lext, N_HEADS * DK), BF16),
                   jax.ShapeDtypeStruct((lext, D), BF16), jax.ShapeDtypeStruct((N_HEADS // 2, 8, LANES), F32)],
        scratch_shapes=[pltpu.VMEM((2, n_chunks, 2 * LANES, DV), F32), pltpu.VMEM((2, n_chunks, 2 * LANES, DV), F32),
                        state, state, state, state],
        compiler_params=_cparams(dimension_semantics=("arbitrary",)),
    )(lg, p_ext, p_ext, p_ext, dret, cos_t, sin_t, after)


class _ColumnWriter:
    def __init__(self, dst_hbm, stage, sems, pieces, n_steps):
        self.dst, self.stage, self.sems, self.pieces, self.n_steps = dst_hbm, stage, sems, pieces, n_steps

    def _copies(self, slot, tile):
        rows = pl.ds(pl.multiple_of(tile * TM, TM), TM)
        return [pltpu.make_async_copy(self.stage.at[slot, :, pl.ds(src, width)], self.dst.at[rows, pl.ds(dst, width)],
                                      self.sems.at[slot, k]) for k, (src, dst, width) in enumerate(self.pieces)]

    def slot_for(self, step):
        slot = step % 2

        @pl.when(step >= 2)
        def _():
            for cp in self._copies(slot, step - 2):
                cp.wait()

        return slot

    def send(self, step, slot):
        for cp in self._copies(slot, step):
            cp.start()

        @pl.when(step == self.n_steps - 1)
        def _():
            if self.n_steps >= 2:
                for cp in self._copies(1 - slot, step - 1):
                    cp.wait()
            for cp in self._copies(slot, step):
                cp.wait()


def _merge(p_ext, rn, rstd, x, target, w_a, w_b, w_out, vecs, seq):
    n_tiles = seq // TM
    hb = TM // HALO
    dp_pieces = [(0, CB_BG * D, D), (D, CB_ZA * D, D), (2 * D, CB_ZB * D, 3 * D)]

    def body(h_ref, bg_ref, cg_ref, za_ref, zb_ref, ga_ref, gb_ref, hp_ref, hn_ref, cp_ref, cn_ref,
             rn_ref, rstd_ref, x_ref, t_ref, wa_ref, wb_ref, wo_ref, vec_ref,
             dx1_ref, dp_hbm, dconv_ref, dret_ref, at_ref, b_ref, part_ref, stage, dp_sems):
        i = pl.program_id(0)
        writer = _ColumnWriter(dp_hbm, stage, dp_sems, dp_pieces, n_tiles)
        slot = writer.slot_for(i)
        dpa_ref = stage.at[slot]
        f = lambda ref: ref[...].astype(F32)
        h, bg, cg, za, zb, ga, gb = f(h_ref), f(bg_ref), f(cg_ref), f(za_ref), f(zb_ref), f(ga_ref), f(gb_ref)
        gx, w0, w1, w2 = vec_ref[0:1, :], vec_ref[1:2, :], vec_ref[2:3, :], vec_ref[3:4, :]
        cb, gnw, fw = vec_ref[4:5, :], vec_ref[5:6, :], vec_ref[6:7, :]
        u = cg * h
        row = lax.broadcasted_iota(jnp.int32, (TM, 1), 0)
        u_prev = (f(cp_ref) * f(hp_ref))[HALO - 1:HALO, :]
        u_next = (f(cn_ref) * f(hn_ref))[0:1, :]
        u_prev = jnp.where(i == 0, 0.0, u_prev)
        u_next = jnp.where(i == n_tiles - 1, 0.0, u_next)
        u_up = jnp.where(row == 0, u_prev, pltpu.roll(u, 1, 0))
        u_dn = jnp.where(row == TM - 1, u_next, pltpu.roll(u, TM - 1, 0))
        conv = w0 * u_up + w1 * u + w2 * u_dn + cb
        sza = _sigmoid(za)
        silu_za = za * sza
        a_act = silu_za * bg * conv
        rn = rn_ref[...]
        szb = _sigmoid(zb)
        silu_zb = zb * szb
        rg = rn * gnw
        b_act = silu_zb * rg
        y_a = _dot(a_act.astype(BF16), wa_ref[...])
        y_b = _dot(b_act.astype(BF16), wb_ref[...])
        sga, sgb = _sigmoid(ga), _sigmoid(gb)
        mix = sga * y_a + sgb * y_b
        y = _dot(mix.astype(BF16), wo_ref[...])
        x1 = x_ref[...] + gx * y
        r1 = lax.rsqrt(jnp.mean(x1 * x1, axis=-1, keepdims=True) + EPS)
        xh1 = x1 * r1
        err = xh1 * fw - t_ref[...]
        loss = jnp.sum(jnp.sum(err * err, axis=1, keepdims=True), axis=0, keepdims=True) * (0.5 / D)
        dout = err * (1.0 / D)
        dxh = dout * fw
        dx1 = r1 * (dxh - xh1 * jnp.mean(dxh * xh1, axis=-1, keepdims=True))
        dx1_ref[...] = dx1
        dy = (dx1 * gx).astype(BF16)
        dmix = _dot_nt(dy, wo_ref[...])
        dya = (dmix * sga).astype(BF16)
        dyb = (dmix * sgb).astype(BF16)
        da = _dot_nt(dya, wa_ref[...])
        db = _dot_nt(dyb, wb_ref[...])
        dpa_ref[:, 0:D] = (da * silu_za * conv).astype(BF16)
        dpa_ref[:, D:2 * D] = (da * bg * conv * (sza * (1.0 + za * (1.0 - sza)))).astype(BF16)
        dpa_ref[:, 2 * D:3 * D] = (db * rg * (szb * (1.0 + zb * (1.0 - szb)))).astype(BF16)
        dpa_ref[:, 3 * D:4 * D] = (dmix * y_a * sga * (1.0 - sga)).astype(BF16)
        dpa_ref[:, 4 * D:5 * D] = (dmix * y_b * sgb * (1.0 - sgb)).astype(BF16)
        dconv_ref[...] = (da * silu_za * bg).astype(BF16)
        drn_n = db * silu_zb
        drn = drn_n * gnw
        rstd = rstd_ref[...]
        for hd in range(N_HEADS):
            sl = slice(hd * DV, (hd + 1) * DV)
            dh_, rh = drn[:, sl], rn[:, sl]
            m1 = jnp.mean(dh_, axis=-1, keepdims=True)
            m2 = jnp.mean(dh_ * rh, axis=-1, keepdims=True)
            dret_ref[:, sl] = (rstd[:, sl] * (dh_ - m1 - rh * m2)).astype(BF16)
        at_ref[0] = a_act.T.astype(BF16)
        at_ref[1] = b_act.T.astype(BF16)
        at_ref[2] = mix.T.astype(BF16)
        b_ref[0] = dya
        b_ref[1] = dyb
        b_ref[2] = dy

        @pl.when(i == 0)
        def _():
            part_ref[...] = jnp.zeros((8, D), F32)

        part_ref[0:1, :] += jnp.sum(dout * xh1, axis=0, keepdims=True)
        part_ref[1:2, :] += jnp.sum(dx1 * y, axis=0, keepdims=True)
        part_ref[2:3, :] += jnp.sum(drn_n * rn, axis=0, keepdims=True)
        part_ref[3:4, :] += jnp.broadcast_to(loss, (1, D))
        writer.send(i, slot)

    col = lambda cb_: pl.BlockSpec((TM, D), lambda i, cb_=cb_: (i, cb_))
    prev = lambda cb_: pl.BlockSpec((HALO, D), lambda i, cb_=cb_: (jnp.maximum(i * hb - 1, 0), cb_))
    nxt = lambda cb_: pl.BlockSpec((HALO, D), lambda i, cb_=cb_: (jnp.minimum((i + 1) * hb, n_tiles * hb - 1), cb_))
    tile = pl.BlockSpec((TM, D), lambda i: (i, 0))
    full = lambda a: pl.BlockSpec(a.shape, lambda i: (0,) * a.ndim, pipeline_mode=pl.Buffered(1))
    return pl.pallas_call(
        body, name="merge", grid=(n_tiles,),
        in_specs=[col(CB_H), col(CB_BG), col(CB_CG), col(CB_ZA), col(CB_ZB), col(CB_GA), col(CB_GB),
                  prev(CB_H), nxt(CB_H), prev(CB_CG), nxt(CB_CG),
                  tile, tile, tile, tile, full(w_a), full(w_b), full(w_out), full(vecs)],
        out_specs=[tile, pl.BlockSpec(memory_space=pl.ANY), tile, tile,
                   pl.BlockSpec((3, D, TM), lambda i: (0, 0, i)), pl.BlockSpec((3, TM, D), lambda i: (0, i, 0)),
                   pl.BlockSpec((8, D), lambda i: (0, 0))],
        out_shape=[jax.ShapeDtypeStruct((seq, D), F32), jax.ShapeDtypeStruct((p_ext.shape[0], PW), BF16),
                   jax.ShapeDtypeStruct((seq, D), BF16), jax.ShapeDtypeStruct((seq, D), BF16),
                   jax.ShapeDtypeStruct((3, D, seq), BF16), jax.ShapeDtypeStruct((3, seq, D), BF16),
                   jax.ShapeDtypeStruct((8, D), F32)],
        scratch_shapes=[pltpu.VMEM((2, TM, 5 * D), BF16), pltpu.SemaphoreType.DMA((2, len(dp_pieces)))],
        compiler_params=_cparams(dimension_semantics=("arbitrary",)),
    )(p_ext, p_ext, p_ext, p_ext, p_ext, p_ext, p_ext, p_ext, p_ext, p_ext, p_ext,
      rn, rstd, x, target, w_a, w_b, w_out, vecs)


def _conv_backward(p_ext, dconv, dp, dq, dk, dv, vecs, seq, after):
    n_tiles = seq // TM
    hb = TM // HALO
    qk = N_HEADS * DK
    pieces = [(0, CB_H * D, D), (D, CB_CG * D, D), (2 * D, CB_QK * D, 2 * D)]
    zero_pieces = [(CB_BG * D, D), (CB_ZA * D, D), (CB_ZB * D, 3 * D)]

    def body(h_ref, cg_ref, dc_ref, dcp_ref, dcn_ref, dq_ref, dk_ref, dv_ref, vec_ref, dp_in, after_ref, dp_hbm, part_ref,
             stage, sems, zeros, zero_sems):
        del dp_in, after_ref
        i = pl.program_id(0)
        writer = _ColumnWriter(dp_hbm, stage, sems, pieces, n_tiles + 1)
        slot = writer.slot_for(i)
        out = stage.at[slot]
        out[:, 2 * D:2 * D + qk] = dq_ref[...]
        out[:, 2 * D + qk:3 * D] = dk_ref[...]
        out[:, 3 * D:4 * D] = dv_ref[...]

        @pl.when(i == 0)
        def _():
            part_ref[...] = jnp.zeros((8, D), F32)

        @pl.when(i == n_tiles)
        def _():
            out[:, 0:2 * D] = jnp.zeros((TM, 2 * D), BF16)
            zeros[...] = jnp.zeros(zeros.shape, BF16)
            rows = pl.ds(n_tiles * TM, TM)
            fills = [pltpu.make_async_copy(zeros.at[:, pl.ds(0, width)], dp_hbm.at[rows, pl.ds(dst, width)],
                                           zero_sems.at[k]) for k, (dst, width) in enumerate(zero_pieces)]
            for cp in fills:
                cp.start()
            for cp in fills:
                cp.wait()

        @pl.when(i < n_tiles)
        def _():
            f = lambda ref: ref[...].astype(F32)
            h, cg, dc = f(h_ref), f(cg_ref), f(dc_ref)
            w0, w1, w2 = vec_ref[1:2, :], vec_ref[2:3, :], vec_ref[3:4, :]
            row = lax.broadcasted_iota(jnp.int32, (TM, 1), 0)
            dc_prev = jnp.where(i == 0, 0.0, f(dcp_ref)[HALO - 1:HALO, :])
            dc_next = jnp.where(i == n_tiles - 1, 0.0, f(dcn_ref)[0:1, :])
            dc_up = jnp.where(row == 0, dc_prev, pltpu.roll(dc, 1, 0))
            dc_dn = jnp.where(row == TM - 1, dc_next, pltpu.roll(dc, TM - 1, 0))
            du = w0 * dc_dn + w1 * dc + w2 * dc_up
            u = cg * h
            part_ref[0:1, :] += jnp.sum(u * dc_dn, axis=0, keepdims=True)
            part_ref[1:2, :] += jnp.sum(u * dc, axis=0, keepdims=True)
            part_ref[2:3, :] += jnp.sum(u * dc_up, axis=0, keepdims=True)
            part_ref[3:4, :] += jnp.sum(dc, axis=0, keepdims=True)
            out[:, 0:D] = (du * cg).astype(BF16)
            out[:, D:2 * D] = (du * h).astype(BF16)

        writer.send(i, slot)

    last = n_tiles - 1
    col = lambda cb_: pl.BlockSpec((TM, D), lambda i, cb_=cb_: (jnp.minimum(i, last), cb_))
    lat = lambda w: pl.BlockSpec((TM, w), lambda i: (jnp.minimum(i, last), 0))
    ext = lambda w: pl.BlockSpec((TM, w), lambda i: (i, 0))
    anyspec = pl.BlockSpec(memory_space=pl.ANY)
    return pl.pallas_call(
        body, name="conv_backward", grid=(n_tiles + 1,),
        in_specs=[col(CB_H), col(CB_CG), lat(D),
                  pl.BlockSpec((HALO, D), lambda i: (jnp.clip(i * hb - 1, 0, n_tiles * hb - 1), 0)),
                  pl.BlockSpec((HALO, D), lambda i: (jnp.minimum((i + 1) * hb, n_tiles * hb - 1), 0)),
                  ext(qk), ext(qk), ext(D), pl.BlockSpec(vecs.shape, lambda i: (0, 0)), anyspec, _AFTER_SPEC],
        out_specs=[anyspec, pl.BlockSpec((8, D), lambda i: (0, 0))],
        out_shape=[jax.ShapeDtypeStruct(dp.shape, BF16), jax.ShapeDtypeStruct((8, D), F32)],
        input_output_aliases={9: 0},
        scratch_shapes=[pltpu.VMEM((2, TM, 4 * D), BF16), pltpu.SemaphoreType.DMA((2, len(pieces))),
                        pltpu.VMEM((TM, 3 * D), BF16), pltpu.SemaphoreType.DMA((len(zero_pieces),))],
        compiler_params=_cparams(dimension_semantics=("arbitrary",)),
    )(p_ext, p_ext, dconv, dconv, dconv, dq, dk, dv, vecs, dp, after)


def _input_backward(dp, shards, ids, x, ctx, dx1, modx, modc, norm_w, after):
    seq = x.shape[0]
    lext = seq + ctx.shape[0]
    n_x = seq // TM
    n_w = len(shards.arrays)

    def body(ids_ref, dp_ref, x_ref, ctx_ref, dx1_ref, mx_ref, mc_ref, nw_ref, *rest):
        w_hbm, (_, gx_ref, part_ref, w_ref, w_sems) = rest[:n_w], rest[n_w:]
        i = pl.program_id(0)
        _load_w_in(i, ids_ref, shards, w_hbm, w_ref, w_sems)
        is_ctx = i >= n_x
        dxm = _dot_nt(dp_ref[...], w_ref[...])
        x = jnp.where(is_ctx, ctx_ref[...], x_ref[...])
        r = lax.rsqrt(jnp.mean(x * x, axis=-1, keepdims=True) + EPS)
        xh = x * r
        nw = nw_ref[...]
        sc = jnp.where(is_ctx, mc_ref[1:2, :], mx_ref[1:2, :])
        dxn = dxm * (1.0 + sc)
        dxh = dxn * nw
        dx = r * (dxh - xh * jnp.mean(dxh * xh, axis=-1, keepdims=True))

        @pl.when(jnp.logical_not(is_ctx))
        def _():
            gx_ref[...] = dx1_ref[...] + dx

        @pl.when(i == 0)
        def _():
            part_ref[...] = jnp.zeros((8, D), F32)

        fx = jnp.where(is_ctx, 0.0, 1.0)
        d_shift = jnp.sum(dxm, axis=0, keepdims=True)
        d_scale = jnp.sum(dxm * (xh * nw), axis=0, keepdims=True)
        part_ref[0:1, :] += fx * d_shift
        part_ref[1:2, :] += fx * d_scale
        part_ref[2:3, :] += jnp.sum(dxn * xh, axis=0, keepdims=True)
        part_ref[3:4, :] += (1.0 - fx) * d_shift
        part_ref[4:5, :] += (1.0 - fx) * d_scale

    lat = lambda w: pl.BlockSpec((TM, w), lambda i, ids_: (jnp.minimum(i, n_x - 1), 0))
    ext = lambda w: pl.BlockSpec((TM, w), lambda i, ids_: (i, 0))
    full = lambda a: pl.BlockSpec(a.shape, lambda i, ids_: (0,) * a.ndim)
    grid_spec = pltpu.PrefetchScalarGridSpec(
        num_scalar_prefetch=1, grid=(lext // TM,),
        in_specs=[ext(PW), lat(D), full(ctx), lat(D), full(modx), full(modc), full(norm_w)]
        + [pl.BlockSpec(memory_space=pl.ANY)] * n_w + [_AFTER_SPEC],
        out_specs=[lat(D), pl.BlockSpec((8, D), lambda i, ids_: (0, 0))],
        scratch_shapes=[pltpu.VMEM((D, PW), BF16), pltpu.SemaphoreType.DMA((N_DEV,))])
    return pl.pallas_call(
        body, name="input_backward", grid_spec=grid_spec,
        out_shape=[jax.ShapeDtypeStruct((seq, D), F32), jax.ShapeDtypeStruct((8, D), F32)],
        compiler_params=_cparams(dimension_semantics=("arbitrary",)),
    )(ids, dp, x, ctx, dx1, modx, modc, norm_w, *shards.arrays, after)


def _weight_grad_in(xm_t, dp, owners, name, after):
    lext = xm_t.shape[1]

    def body(own_ref, a_ref, b_ref, after_ref, o_ref):
        o_ref[0] = _dot(a_ref[...], b_ref[...])

    grid_spec = pltpu.PrefetchScalarGridSpec(
        num_scalar_prefetch=1, grid=(4,),
        in_specs=[pl.BlockSpec((D, lext), lambda j, own: (0, 0)),
                  pl.BlockSpec((lext, WSH), lambda j, own: (0, own[j])), _AFTER_SPEC],
        out_specs=pl.BlockSpec((1, D, WSH), lambda j, own: (j, 0, 0)))
    return pl.pallas_call(
        body, name=name, grid_spec=grid_spec,
        out_shape=jax.ShapeDtypeStruct((4, D, WSH), F32),
        compiler_params=_cparams(dimension_semantics=("arbitrary",)),
    )(owners, xm_t, dp, after)


def _weight_grad_square(a_t, b):
    seq = a_t.shape[2]

    def body(a_ref, b_ref, o_ref):
        o_ref[:, 0] = _dot(a_ref[0], b_ref[0]).reshape(N_DEV, RSH, D)

    return pl.pallas_call(
        body, name="weight_grad_square", grid=(3,),
        in_specs=[pl.BlockSpec((1, D, seq), lambda t: (t, 0, 0)), pl.BlockSpec((1, seq, D), lambda t: (t, 0, 0))],
        out_specs=pl.BlockSpec((N_DEV, 1, RSH, D), lambda t: (0, t, 0, 0)),
        out_shape=jax.ShapeDtypeStruct((N_DEV, 3, RSH, D), F32),
        compiler_params=_cparams(dimension_semantics=("arbitrary",)),
    )(a_t, b)


def _rows(total, *parts):
    width = max(a.shape[1] for _, a in parts)
    out = None
    for row, a in parts:
        padded = jnp.pad(a, ((row, total - row - a.shape[0]), (0, width - a.shape[1])))
        out = padded if out is None else out + padded
    return out


def kernel(x, c, ctx, c_ctx, norm_w, ada_w, ada_b, w_in, conv_w, conv_b, decay_logit, gn_w, w_a, w_b, w_out, final_norm_w, loss_target, m_c_ctx, m_norm_w, m_ada_w, m_ada_b, m_w_in, m_conv_w, m_conv_b, m_decay_logit, m_gn_w, m_w_a, m_w_b, m_w_out, m_final_norm_w, v_c_ctx, v_norm_w, v_ada_w, v_ada_b, v_w_in, v_conv_w, v_conv_b, v_decay_logit, v_gn_w, v_w_a, v_w_b, v_w_out, v_final_norm_w):
    xi, yi, ci = _coords()
    me = 4 * xi + 2 * yi + ci
    chip = 2 * xi + yi
    seq, ctx_len = x.shape[1], ctx.shape[1]
    assert seq % TM == 0 and seq % RET_C == 0 and ctx_len == TM and seq % GRID_W == 0
    csh = D // N_DEV

    blk = jnp.pad(c, ((0, 7), (0, 0))) + jnp.pad(conv_w[0], ((1, 4), (0, D - csh)))
    got, _ = _all_gather_small(blk, "gather_cond")
    conv_w_all = got[:, 1:4, 0:csh].transpose(1, 0, 2).reshape(3, D)
    c16 = _rows(16, (0, got[:, 0, :]), (N_DEV, c_ctx[None]))
    ada_b_sh = lax.dynamic_slice(ada_b, (0, me * ADA_SH), (1, ADA_SH))
    mod_sh, act16, lg = _modulation(c16, ada_w[0], ada_b_sh, decay_logit[0])
    mod_all, small_done = _all_gather_small(mod_sh, "gather_mod")
    mod_all = mod_all.transpose(1, 0, 2).reshape(16, 3 * D)
    modx = lax.dynamic_slice(mod_all, (me, 0), (1, 3 * D)).reshape(3, D)
    modc = mod_all[8].reshape(3, D)

    x2, tgt = x[0], loss_target[0]
    wblock = lambda n: jax.ShapeDtypeStruct((n, D, WSH), BF16)
    ids_of = lambda ks: jnp.stack([jnp.bitwise_xor(me, k) for k in ks]).astype(jnp.int32)
    own_shard = w_in[0].astype(BF16)[None]
    sq_shards = [w[0].astype(BF16) for w in (w_a, w_b, w_out)]
    wave1 = _Exchange(
        "gather_in_first", [jnp.broadcast_to(own_shard, (2, D, WSH)), own_shard], [wblock(4)], 3,
        lambda srcs, lands: _forward_plan([(0, 1)])(srcs, lands) + _send_plan((4, 2))(srcs[1:], lands),
        after=small_done)
    xm, xm_t, cos_t, sin_t = _prenorm(x2, ctx[0], modx, modc, norm_w, wave1.token)
    (w_pair, _), _ = wave1.wait(xm_t, 1, through=[0])
    first = _Shards([w_pair], [(0, 0), (0, 1)])
    p_ext = _in_projection(None, xm, cos_t, sin_t, w_pair, [((0, 0), (1, 1))], me, "in_projection_pair")
    _, (w_nbr,) = wave1.wait(p_ext, through=[1, 2])
    wave2 = _Exchange(
        "gather_in_second", [w_nbr, own_shard], [wblock(2)], 3,
        lambda srcs, lands: _forward_plan([(0, 2), (1, 3)])(srcs, lands) + _send_plan((6,))(srcs[1:], lands))
    (w_nbr, _), _ = wave2.wait(wave2.token, 2, through=[0])
    second = _Shards([w_nbr], [(0, 0), (0, 1), (0, 2), (0, 3)])
    p_ext = _in_projection(p_ext, xm, cos_t, sin_t, w_nbr, [((0, 4), (2, 5)), ((1, 2), (3, 3))], me,
                           "in_projection_neighbours")
    _, (w_diag,) = wave2.wait(p_ext, through=[1, 2])
    wave3 = _Exchange(
        "gather_in_third", [w_diag] + sq_shards, [jax.ShapeDtypeStruct((N_DEV, RSH, D), BF16)] * 3, 1 + 3 * (N_DEV - 1),
        lambda srcs, lands: _forward_plan([(0, 1)])(srcs, lands) + _gather_plan(srcs[1:], lands))
    (w_diag, *_), _ = wave3.wait(wave3.token, 1, through=[0])
    third = _Shards([w_diag], [(0, 0), (0, 1)])
    p_ext = _in_projection(p_ext, xm, cos_t, sin_t, w_diag, [((0, 6), (1, 7))], me, "in_projection_diagonal")
    w_shards, w_ids = first + second + third, ids_of((0, 1, 4, 2, 5, 3, 6, 7))

    rn, rstd = _retention_forward(p_ext, lg, seq, ctx_len)
    _, sq_lands = wave3.wait(rstd, through=range(1, 7))
    w_a_all, w_b_all, w_out_all = (
        lax.dynamic_update_slice(land, shard[None], (me, 0, 0)).reshape(D, D) for land, shard in zip(sq_lands, sq_shards))
    vecs = _rows(8, (0, modx[2:3]), (1, conv_w_all), (4, conv_b), (5, gn_w), (6, final_norm_w[None]))
    dx1, dp, dconv, dret, op_at, op_b, part_m = _merge(p_ext, rn, rstd, x2, tgt, w_a_all, w_b_all, w_out_all, vecs, seq)

    j4 = jnp.arange(4, dtype=jnp.int32)
    owners = (2 * jnp.bitwise_xor(chip, j4) + ci).astype(jnp.int32)
    owners_sib = (2 * jnp.bitwise_xor(chip, j4) + (1 - ci)).astype(jnp.int32)
    gw_sq = _weight_grad_square(op_at, op_b).reshape(N_DEV, 3 * RSH, D)
    rs_sq_pair = _Exchange("rs_square_pair", [gw_sq], [jax.ShapeDtypeStruct((4, 3 * RSH, D), F32)], 4,
                           _pair_plan(lambda j, chip_, c_: 2 * jnp.bitwise_xor(chip_, j) + (1 - c_)))
    dq, dk, dv, dlg = _retention_backward(p_ext, dret, lg, cos_t, sin_t, seq, ctx_len, rs_sq_pair.token)
    (gw_sq,), (r1_sq,) = rs_sq_pair.wait(dlg)
    own_sq, send_sq = _pair_sum(gw_sq, r1_sq, owners, "pair_sum_square")
    rs_sq_chips = _Exchange("rs_square_chips", [send_sq], [jax.ShapeDtypeStruct((3, 3 * RSH, D), BF16)], 3, _chips_plan)
    dp, part_c = _conv_backward(p_ext, dconv, dp, dq, dk, dv, vecs, seq, rs_sq_chips.token)
    gw_sib = _weight_grad_in(xm_t, dp, owners_sib, "weight_grad_in_sibling", part_c)
    rs_in_pair = _Exchange("rs_in_pair", [gw_sib], [jax.ShapeDtypeStruct((4, D, WSH), F32)], 4,
                           _pair_plan(lambda j, chip_, c_: j))
    gw_own = _weight_grad_in(xm_t, dp, owners, "weight_grad_in_own", rs_in_pair.token)
    _, (r1_in,) = rs_in_pair.wait(gw_own)
    own_in, send_in = _pair_sum(gw_own, r1_in, j4, "pair_sum_in")
    rs_in_chips = _Exchange("rs_in_chips", [send_in], [jax.ShapeDtypeStruct((3, D, WSH), BF16)], 3, _chips_plan)
    grad_x, part_i = _input_backward(dp, w_shards, w_ids, x2, ctx[0], dx1, modx, modc, norm_w, rs_in_chips.token)

    dl = dlg[:, 0, 0:4]
    dlg_row = jnp.pad(dl[:, 0::2].reshape(1, N_HEADS), ((0, 0), (0, D - N_HEADS))) + jnp.pad(
        dl[:, 1::2].reshape(1, N_HEADS), ((0, 0), (N_HEADS, D - 2 * N_HEADS)))
    partials = _rows(16, (0, part_m[0:1]), (1, part_i[2:3]), (2, part_c[3:4]), (3, part_m[2:3]), (4, part_c[0:3]),
                     (7, part_i[0:2]), (9, part_m[1:2]), (10, part_i[3:5]), (13, part_m[3:4]), (14, dlg_row))
    results = {}
    gather_part = _Exchange("gather_partials", [partials], [jax.ShapeDtypeStruct((N_DEV, 16, D), F32)],
                            N_DEV - 1, _gather_plan)
    _, (r2_in,) = rs_in_chips.wait(gather_part.token)
    results["w_in"] = [o[None] for o in _adam_sharded(w_in[0], m_w_in[0], v_w_in[0], own_in, r2_in, "adam_w_in")]
    _, (got,) = gather_part.wait(results["w_in"][1])
    got = lax.dynamic_update_slice(got, partials[None], (me, 0, 0))
    tot = _sum_devices(got, "sum_partials")
    dmodc = tot[10:13].reshape(1, 3 * D)
    dmod16 = _rows(16, (0, got[:, 7:10, :].reshape(N_DEV, 3 * D)), (N_DEV, dmodc))
    dmod16 = lax.dynamic_slice(dmod16, (0, me * ADA_SH), (16, ADA_SH))
    dmodc8 = _rows(8, (0, dmod16[8:9]))
    g_ada_w, d_ada_w, nm_ada_w, nv_ada_w, cctx_part = _ada_backward(
        act16.T, dmod16, dmodc8, ada_w[0], m_ada_w[0], v_ada_w[0])
    gather_cctx = _Exchange("gather_cctx", [cctx_part], [jax.ShapeDtypeStruct((N_DEV, 8, D), F32)],
                            N_DEV - 1, _gather_plan)
    _, (r2_sq,) = rs_sq_chips.wait(gather_cctx.token)
    square = _adam_square([(w_a[0], m_w_a[0], v_w_a[0]), (w_b[0], m_w_b[0], v_w_b[0]),
                           (w_out[0], m_w_out[0], v_w_out[0])], own_sq, r2_sq)
    _, (got_cctx,) = gather_cctx.wait(square[0][1])
    dact_ctx = _sum_devices(lax.dynamic_update_slice(got_cctx, cctx_part[None], (me, 0, 0)), "sum_cctx")

    view = {"c_ctx": (1, D), "final_norm_w": (1, D)}
    given = {"c_ctx": (c_ctx, m_c_ctx, v_c_ctx), "norm_w": (norm_w, m_norm_w, v_norm_w),
             "conv_b": (conv_b, m_conv_b, v_conv_b), "gn_w": (gn_w, m_gn_w, v_gn_w),
             "final_norm_w": (final_norm_w, m_final_norm_w, v_final_norm_w), "ada_b": (ada_b, m_ada_b, v_ada_b),
             "decay_logit": (decay_logit, m_decay_logit, v_decay_logit), "conv_w": (conv_w, m_conv_w, v_conv_w)}
    small_in = [tuple(a.reshape(view.get(name, a.shape)) for a in given[name]) for name in SMALL_PARAMS]
    conv_w_grad = lax.dynamic_slice(tot, (4, me * csh), (3, csh))
    small_out = _adam_small(tot, dact_ctx, conv_w_grad, tot[14, 0:2 * N_HEADS].reshape(2, N_HEADS), small_in)
    results.update({name: [o.reshape(given[name][0].shape) for o in outs_]
                    for name, outs_ in zip(SMALL_PARAMS, small_out)})
    for name, outs_ in zip(("w_a", "w_b", "w_out"), square):
        results[name] = [o[None] for o in outs_]
    results["ada_w"] = [o[None] for o in (g_ada_w, d_ada_w, nm_ada_w, nv_ada_w)]

    order = ("c_ctx", "norm_w", "ada_w", "ada_b", "w_in", "conv_w", "conv_b", "decay_logit", "gn_w",
             "w_a", "w_b", "w_out", "final_norm_w")
    outs = [results[name][kind] for kind in range(4) for name in order]
    return (tot[13, 0], grad_x[None], *outs)
```

```python
import math

import jax
import jax.numpy as jnp
from jax import lax
from jax.experimental import pallas as pl
from jax.experimental.pallas import tpu as pltpu

F32 = jnp.float32
BF16 = jnp.bfloat16
MESH = pl.DeviceIdType.MESH
_HBM_SPEC = pl.BlockSpec(memory_space=pltpu.HBM)
_SEM_SPEC = pl.BlockSpec(memory_space=pltpu.SEMAPHORE)
_EFFECT = pltpu.SideEffectType.DATAFLOW_SIDE_EFFECTING

N_DEV = 8
D = 1024
N_HEADS = 8
DK = 64
DV = 128
GRID_W = 64
ROPE_BASE = 10000.0
EPS = 1e-6
PW = 9 * D
WSH = PW // N_DEV
RSH = D // N_DEV
ADA_SH = 3 * D // N_DEV
TM = 256
RET_C = 256
HALO = 16
LANES = 128
VMEM_LIMIT = 60 * 1024 * 1024

ADAM_LR = 0.001
ADAM_B1 = 0.9
ADAM_B2 = 0.999
ADAM_EPS = 1e-08
ADAM_WD = 0.01
ADAM_STEP = 10

CB_H, CB_BG, CB_CG, CB_ZA, CB_QK, CB_V, CB_ZB, CB_GA, CB_GB = range(9)


def _cparams(**kw):
    return pltpu.CompilerParams(vmem_limit_bytes=VMEM_LIMIT, **kw)


def _dot(a, b):
    return jnp.dot(a, b, preferred_element_type=F32)


def _dot_nt(a, b):
    return lax.dot_general(a, b, (((1,), (1,)), ((), ())), preferred_element_type=F32)


def _dot_tn(a, b):
    return lax.dot_general(a, b, (((0,), (0,)), ((), ())), preferred_element_type=F32)


def _sigmoid(z):
    return 0.5 * jnp.tanh(0.5 * z) + 0.5


def _row_tile(rows):
    return TM if rows % TM == 0 else rows


def _coords():
    return lax.axis_index("x"), lax.axis_index("y"), lax.axis_index("c")


def _flip(v, bit):
    return 1 - v if bit else v


def _all_gather_small(blk, name):
    rows, cols = blk.shape

    def body(x_ref, out_ref, done, send_sems, recv_sems, local_sem):
        x, y, c = _coords()
        me = 4 * x + 2 * y + c
        mine = pltpu.make_async_copy(x_ref, out_ref.at[me], local_sem)
        mine.start()

        def copy(k, slot):
            peer = (_flip(x, k & 4), _flip(y, k & 2), _flip(c, k & 1))
            return pltpu.make_async_remote_copy(
                src_ref=x_ref, dst_ref=out_ref.at[slot], send_sem=send_sems.at[k - 1],
                recv_sem=recv_sems.at[k - 1], device_id=peer, device_id_type=MESH)

        for k in range(1, N_DEV):
            copy(k, me).start()
        for k in range(1, N_DEV):
            copy(k, jnp.bitwise_xor(me, k)).wait_recv()
        for k in range(1, N_DEV):
            copy(k, me).wait_send()
        mine.wait()
        done[...] = jnp.zeros((8, LANES), F32)

    vmem = pl.BlockSpec(memory_space=pltpu.VMEM)
    return pl.pallas_call(
        body, name=name,
        out_shape=[jax.ShapeDtypeStruct((N_DEV, rows, cols), blk.dtype), jax.ShapeDtypeStruct((8, LANES), F32)],
        in_specs=[vmem], out_specs=[vmem, vmem],
        scratch_shapes=[pltpu.SemaphoreType.DMA((N_DEV - 1,)), pltpu.SemaphoreType.DMA((N_DEV - 1,)),
                        pltpu.SemaphoreType.DMA],
    )(blk)


class _Absent:
    at = property(lambda self: self)

    def __getitem__(self, index):
        return self


class _Exchange:
    def __init__(self, name, srcs, land_shapes, n_copies, plan, after=None):
        self.name, self.plan, self.n_copies = name, plan, n_copies
        self.n_src, self.n_land = len(srcs), len(land_shapes)
        hbm = lambda a: pltpu.HBM(a.shape, a.dtype)
        n = self.n_src + self.n_land
        lands = [pltpu.with_memory_space_constraint(lax.empty(s.shape, s.dtype), pltpu.HBM) for s in land_shapes]
        srcs = [pltpu.with_memory_space_constraint(s, pltpu.HBM) for s in srcs]
        extra = [] if after is None else [after]

        def body(*refs):
            send_sems, recv_sems = refs[n + len(extra)], refs[n + len(extra) + 1]
            for cp in self._copies(refs, send_sems, recv_sems):
                cp.start()
            refs[-1][...] = jnp.zeros((8, LANES), F32)

        outs = pl.pallas_call(
            body, name=name + "_start",
            out_shape=(pltpu.SemaphoreType.DMA((n_copies,)), pltpu.SemaphoreType.DMA((n_copies,)),
                       *[hbm(a) for a in srcs], *[hbm(a) for a in lands], jax.ShapeDtypeStruct((8, LANES), F32)),
            in_specs=[_HBM_SPEC] * n + [pl.BlockSpec(memory_space=pl.ANY)] * len(extra),
            out_specs=(_SEM_SPEC, _SEM_SPEC, *[_HBM_SPEC] * n, pl.BlockSpec(memory_space=pltpu.VMEM)),
            input_output_aliases={i: 2 + i for i in range(n)},
            compiler_params=pltpu.CompilerParams(has_side_effects=_EFFECT),
        )(*srcs, *lands, *extra)
        self.send_sems, self.recv_sems = outs[0], outs[1]
        self.buffers = list(outs[2:2 + n])
        self.token = outs[-1]
        self.waited = 0

    def _copies(self, refs, send_sems, recv_sems, lo=0, hi=None):
        src_refs, land_refs = refs[:self.n_src], refs[self.n_src:self.n_src + self.n_land]
        planned = list(enumerate(self.plan(src_refs, land_refs)))[lo:hi]
        return [pltpu.make_async_remote_copy(src_ref=s, dst_ref=d, send_sem=send_sems.at[k], recv_sem=recv_sems.at[k],
                                             device_id=dev, device_id_type=MESH) for k, (s, d, dev) in planned]

    def wait(self, after, count=None, through=None):
        n = self.n_src + self.n_land
        lo = self.waited
        hi = self.n_copies if count is None else lo + count
        self.waited = hi
        through = list(range(n)) if through is None else list(through)
        m = len(through)

        def body(*refs):
            held = [_Absent()] * n
            for pos, ref in zip(through, refs[:m]):
                held[pos] = ref
            for cp in self._copies(held, refs[m], refs[m + 1], lo, hi):
                cp.wait_send()
                cp.wait_recv()

        outs = pl.pallas_call(
            body, name=f"{self.name}_wait{lo}" if lo or hi < self.n_copies else self.name + "_wait",
            out_shape=tuple(pltpu.HBM(self.buffers[p].shape, self.buffers[p].dtype) for p in through),
            in_specs=[_HBM_SPEC] * m + [_SEM_SPEC, _SEM_SPEC, pl.BlockSpec(memory_space=pl.ANY)],
            out_specs=tuple([_HBM_SPEC] * m),
            input_output_aliases={i: i for i in range(m)},
            compiler_params=pltpu.CompilerParams(has_side_effects=_EFFECT),
        )(*[self.buffers[p] for p in through], self.send_sems, self.recv_sems, after)
        for pos, out in zip(through, outs):
            self.buffers[pos] = out
        return list(self.buffers[:self.n_src]), list(self.buffers[self.n_src:])


def _pair_plan(src_index):
    def plan(srcs, lands):
        x, y, c = _coords()
        chip = 2 * x + y
        return [(srcs[0].at[src_index(j, chip, c)], lands[0].at[j], (x, y, 1 - c)) for j in range(4)]
    return plan


def _chips_plan(srcs, lands):
    x, y, c = _coords()
    return [(srcs[0].at[j - 1], lands[0].at[j - 1], (_flip(x, j & 2), _flip(y, j & 1), c)) for j in range(1, 4)]


def _peer(k):
    x, y, c = _coords()
    return _flip(x, k & 4), _flip(y, k & 2), _flip(c, k & 1)


def _send_plan(ks):
    def plan(srcs, lands):
        return [(srcs[0].at[0], lands[0].at[p], _peer(k)) for p, k in enumerate(ks)]
    return plan


def _forward_plan(moves):
    def plan(srcs, lands):
        return [(srcs[0].at[s], srcs[0].at[d], _peer(1)) for s, d in moves]
    return plan


def _gather_plan(srcs, lands):
    x, y, c = _coords()
    me = 4 * x + 2 * y + c
    return [(srcs[a], lands[a].at[me], (_flip(x, k & 4), _flip(y, k & 2), _flip(c, k & 1)))
            for a in range(len(srcs)) for k in range(1, N_DEV)]


def _pair_sum(grad, recv, owners, name):
    _, rows, cols = grad.shape
    tr = rows

    def body(own_ref, g_ref, r_ref, mine_ref, send_ref):
        j = pl.program_id(1)
        total = g_ref[...] + r_ref[...]

        @pl.when(j == 0)
        def _():
            mine_ref[...] = total

        @pl.when(j > 0)
        def _():
            send_ref[...] = total.astype(BF16)

    grid_spec = pltpu.PrefetchScalarGridSpec(
        num_scalar_prefetch=1, grid=(rows // tr, 4),
        in_specs=[pl.BlockSpec((1, tr, cols), lambda i, j, own: (own[j], i, 0)),
                  pl.BlockSpec((1, tr, cols), lambda i, j, own: (j, i, 0))],
        out_specs=[pl.BlockSpec((1, tr, cols), lambda i, j, own: (0, i, 0)),
                   pl.BlockSpec((1, tr, cols), lambda i, j, own: (jnp.maximum(j - 1, 0), i, 0))])
    return pl.pallas_call(
        body, name=name, grid_spec=grid_spec,
        out_shape=[jax.ShapeDtypeStruct((1, rows, cols), F32), jax.ShapeDtypeStruct((3, rows, cols), BF16)],
        compiler_params=_cparams(dimension_semantics=("arbitrary", "arbitrary")),
    )(owners, grad, recv)


def _modulation(c16, ada_w_sh, ada_b_sh, decay_logit):
    def body(c_ref, w_ref, b_ref, dl_ref, mod_ref, act_ref, lg_ref):
        cv = c_ref[...]
        act = cv * _sigmoid(cv)
        act_ref[...] = act
        mod_ref[...] = jnp.dot(act, w_ref[...], preferred_element_type=F32,
                               precision=lax.Precision.HIGHEST) + b_ref[...]
        z = dl_ref[...]
        lg_ref[...] = jnp.minimum(z, 0.0) - jnp.log(1.0 + jnp.exp(-jnp.abs(z)))

    return pl.pallas_call(
        body, name="modulation",
        out_shape=[jax.ShapeDtypeStruct((16, ADA_SH), F32), jax.ShapeDtypeStruct((16, D), F32),
                   jax.ShapeDtypeStruct(decay_logit.shape, F32)],
        compiler_params=_cparams(),
    )(c16, ada_w_sh, ada_b_sh, decay_logit)


def _adam_update(w, g, m, v):
    m2 = ADAM_B1 * m + (1.0 - ADAM_B1) * g
    v2 = ADAM_B2 * v + (1.0 - ADAM_B2) * (g * g)
    m_hat = m2 / (1.0 - ADAM_B1 ** ADAM_STEP)
    v_hat = v2 / (1.0 - ADAM_B2 ** ADAM_STEP)
    delta = -ADAM_LR * (m_hat / (jnp.sqrt(v_hat) + ADAM_EPS) + ADAM_WD * w)
    return delta, m2, v2


def _adam_sharded(w, m, v, own, recv, name):
    rows, cols = w.shape
    tr = _row_tile(rows)

    def body(w_ref, m_ref, v_ref, g0, g1, g2, g3, g_ref, d_ref, m_out, v_out):
        g = ((g0[0] + g1[0].astype(F32)) + g2[0].astype(F32)) + g3[0].astype(F32)
        delta, m2, v2 = _adam_update(w_ref[...], g, m_ref[...], v_ref[...])
        g_ref[...] = g
        d_ref[...] = delta
        m_out[...] = m2
        v_out[...] = v2

    flat = pl.BlockSpec((tr, cols), lambda i: (i, 0))
    part = lambda j: pl.BlockSpec((1, tr, cols), lambda i, j=j: (j, i, 0))
    return pl.pallas_call(
        body, name=name, grid=(rows // tr,),
        in_specs=[flat, flat, flat, part(0), part(0), part(1), part(2)],
        out_specs=[flat] * 4,
        out_shape=[jax.ShapeDtypeStruct((rows, cols), F32)] * 4,
        compiler_params=_cparams(dimension_semantics=("arbitrary",)),
    )(w, m, v, own, recv, recv, recv)


def _ada_backward(act_t, dmod16, dmodc8, ada_w_sh, m, v):
    def body(at_ref, dm_ref, dc_ref, w_ref, m_ref, v_ref, g_ref, d_ref, m_out, v_out, pc_ref):
        g = jnp.dot(at_ref[...], dm_ref[...], preferred_element_type=F32, precision=lax.Precision.HIGHEST)
        w = w_ref[...]
        delta, m2, v2 = _adam_update(w, g, m_ref[...], v_ref[...])
        g_ref[...] = g
        d_ref[...] = delta
        m_out[...] = m2
        v_out[...] = v2
        pc_ref[...] = lax.dot_general(dc_ref[...], w, (((1,), (1,)), ((), ())), preferred_element_type=F32,
                                      precision=lax.Precision.HIGHEST)

    return pl.pallas_call(
        body, name="ada_backward",
        out_shape=[jax.ShapeDtypeStruct((D, ADA_SH), F32)] * 4 + [jax.ShapeDtypeStruct((8, D), F32)],
        compiler_params=_cparams(),
    )(act_t, dmod16, dmodc8, ada_w_sh, m, v)


def _sum_devices(gathered, name):
    _, rows, cols = gathered.shape

    def body(g_ref, o_ref):
        acc = g_ref[0]
        for d in range(1, N_DEV):
            acc = acc + g_ref[d]
        o_ref[...] = acc

    return pl.pallas_call(body, name=name, out_shape=jax.ShapeDtypeStruct((rows, cols), F32),
                          compiler_params=_cparams())(gathered)


SMALL_PARAMS = ("c_ctx", "norm_w", "conv_b", "gn_w", "final_norm_w", "ada_b", "decay_logit", "conv_w")


def _adam_small(tot, dact_ctx, conv_w_grad, decay_grad, params):
    n = len(SMALL_PARAMS)

    def body(tot_ref, dact_ref, cwg_ref, dlg_ref, *refs):
        ins, outs = refs[:3 * n], refs[3 * n:]
        raw = {
            "c_ctx": dact_ref[0:1, :],
            "norm_w": tot_ref[1:2, :], "conv_b": tot_ref[2:3, :], "gn_w": tot_ref[3:4, :],
            "final_norm_w": tot_ref[0:1, :],
            "ada_b": jnp.concatenate([tot_ref[7 + r:8 + r, :] + tot_ref[10 + r:11 + r, :] for r in range(3)], axis=1),
            "decay_logit": dlg_ref[...],
            "conv_w": cwg_ref[...],
        }
        for k, name in enumerate(SMALL_PARAMS):
            lead = (0,) if len(ins[3 * k].shape) == 3 else ()
            w, m, v = (ref[lead + (...,)] for ref in ins[3 * k:3 * k + 3])
            g = raw[name]
            if name == "c_ctx":
                s = _sigmoid(w)
                g = g * (s * (1.0 + w * (1.0 - s)))
            elif name == "decay_logit":
                g = g * (1.0 - _sigmoid(w))
            delta, m2, v2 = _adam_update(w, g, m, v)
            for ref, val in zip(outs[4 * k:4 * k + 4], (g, delta, m2, v2)):
                ref[lead + (...,)] = val

    flat = [a for trio in params for a in trio]
    out_shape = [jax.ShapeDtypeStruct(trio[0].shape, F32) for trio in params for _ in range(4)]
    outs = pl.pallas_call(body, name="adam_small", out_shape=out_shape, compiler_params=_cparams())(
        tot, dact_ctx, conv_w_grad, decay_grad, *flat)
    return [outs[4 * k:4 * k + 4] for k in range(n)]


def _adam_square(params, own, recv):
    def body(own_ref, recv_ref, *refs):
        ins, outs = refs[:9], refs[9:]
        for k in range(3):
            rows = slice(k * RSH, (k + 1) * RSH)
            g = own_ref[0, rows, :]
            for j in range(3):
                g = g + recv_ref[j, rows, :].astype(F32)
            delta, m2, v2 = _adam_update(ins[3 * k][...], g, ins[3 * k + 1][...], ins[3 * k + 2][...])
            for ref, val in zip(outs[4 * k:4 * k + 4], (g, delta, m2, v2)):
                ref[...] = val

    flat = [a for trio in params for a in trio]
    outs = pl.pallas_call(body, name="adam_square", out_shape=[jax.ShapeDtypeStruct((RSH, D), F32)] * 12,
                          compiler_params=_cparams())(own, recv, *flat)
    return [outs[4 * k:4 * k + 4] for k in range(3)]


def _rope(t, cos, sin):
    lane = lax.broadcasted_iota(jnp.int32, (1, LANES), 1)
    first_half = jnp.bitwise_and(lane, DK // 2) == 0
    partner = jnp.where(first_half, pltpu.roll(t, LANES - DK // 2, 1), pltpu.roll(t, DK // 2, 1))
    return t * cos + partner * sin


class _Shards:
    def __init__(self, arrays, picks):
        self.arrays, self.picks = list(arrays), list(picks)

    def __add__(self, other):
        shift = len(self.arrays)
        return _Shards(self.arrays + other.arrays, self.picks + [(a + shift, s) for a, s in other.picks])


def _load_w_in(step, ids_ref, shards, w_refs, w_vmem, sems):
    @pl.when(step == 0)
    def _():
        copies = []
        for p, (a, slot) in enumerate(shards.picks):
            col = pl.multiple_of(ids_ref[p] * WSH, LANES)
            copies.append(pltpu.make_async_copy(w_refs[a].at[slot], w_vmem.at[:, pl.ds(col, WSH)], sems.at[p]))
        for cp in copies:
            cp.start()
        for cp in copies:
            cp.wait()


def _project_columns(xmb, w_ref, shard, cos, sin, p_ref):
    width = 2 * WSH
    q_lo, k_lo, k_hi = CB_QK * D, CB_QK * D + N_HEADS * DK, CB_V * D
    q_pair, k_pair = q_lo // width, k_lo // width
    assert (q_pair + 1) * width == k_lo and k_pair * width == k_lo and k_hi <= (k_pair + 1) * width
    mxu_cols = 2 * LANES

    def columns(rot_lo, rot_hi, scale):
        for a in range(0, width, mxu_cols):
            acc = _dot(xmb, w_ref[:, a:a + mxu_cols])
            for col in range(a, a + mxu_cols, LANES):
                piece = acc[:, col - a:col - a + LANES]
                if rot_lo <= col < rot_hi:
                    piece = _rope(piece if scale == 1.0 else piece * scale, cos, sin)
                p_ref[:, col:col + LANES] = piece.astype(BF16)

    @pl.when(shard == q_pair)
    def _():
        columns(q_lo - q_pair * width, width, 1.0)

    @pl.when(shard == k_pair)
    def _():
        columns(0, k_hi - k_lo, DK ** -0.5)

    @pl.when(jnp.logical_and(shard != q_pair, shard != k_pair))
    def _():
        columns(0, 0, 1.0)


_AFTER_SPEC = pl.BlockSpec(memory_space=pl.ANY)


def _prenorm(x, ctx, modx, modc, norm_w, after):
    n_x_tiles = x.shape[0] // TM
    lext = x.shape[0] + ctx.shape[0]

    assert TM % GRID_W == 0 and TM // GRID_W <= 8
    grid_rows = TM // GRID_W

    def body(x_ref, ctx_ref, mx_ref, mc_ref, nw_ref, after_ref, xm_ref, xmt_ref, cos_ref, sin_ref, col_cos, col_sin):
        del after_ref
        i = pl.program_id(0)
        is_ctx = i >= n_x_tiles
        x = jnp.where(is_ctx, ctx_ref[...], x_ref[...])
        r = lax.rsqrt(jnp.mean(x * x, axis=-1, keepdims=True) + EPS)
        sh = jnp.where(is_ctx, mc_ref[0:1, :], mx_ref[0:1, :])
        sc = jnp.where(is_ctx, mc_ref[1:2, :], mx_ref[1:2, :])
        xm = (x * r * nw_ref[...]) * (1.0 + sc) + sh
        xm_ref[...] = xm.astype(BF16)
        xmt_ref[...] = xm.T.astype(BF16)
        lane = lax.broadcasted_iota(jnp.int32, (1, LANES), 1)
        n_freq = DK // 4
        inv = jnp.exp((lane % n_freq).astype(F32) * (-math.log(ROPE_BASE) / n_freq))
        by_column = (lane % (2 * n_freq)) >= n_freq
        sign = jnp.where((lane % DK) < DK // 2, -1.0, 1.0)

        @pl.when(i == 0)
        def _():
            col = (lax.broadcasted_iota(jnp.int32, (TM, 1), 0) % GRID_W).astype(F32) * inv
            col_cos[...] = jnp.cos(col)
            col_sin[...] = jnp.sin(col) * sign

        row = (i * grid_rows + lax.broadcasted_iota(jnp.int32, (8, 1), 0)).astype(F32) * inv
        row_cos, row_sin = jnp.cos(row), jnp.sin(row) * sign
        tall = lambda t: jnp.concatenate(
            [jnp.broadcast_to(t[g:g + 1, :], (GRID_W, LANES)) for g in range(grid_rows)], axis=0)
        cos_ref[...] = jnp.where(is_ctx, 1.0, jnp.where(by_column, col_cos[...], tall(row_cos)))
        sin_ref[...] = jnp.where(is_ctx, 0.0, jnp.where(by_column, col_sin[...], tall(row_sin)))

    full = lambda a: pl.BlockSpec(a.shape, lambda i: (0,) * a.ndim)
    row = lambda width: pl.BlockSpec((TM, width), lambda i: (i, 0))
    return pl.pallas_call(
        body, name="prenorm", grid=(lext // TM,),
        in_specs=[pl.BlockSpec((TM, D), lambda i: (jnp.minimum(i, n_x_tiles - 1), 0)), full(ctx),
                  full(modx), full(modc), full(norm_w), _AFTER_SPEC],
        out_specs=[row(D), pl.BlockSpec((D, TM), lambda i: (0, i)), row(LANES), row(LANES)],
        out_shape=[jax.ShapeDtypeStruct((lext, D), BF16), jax.ShapeDtypeStruct((D, lext), BF16),
                   jax.ShapeDtypeStruct((lext, LANES), F32), jax.ShapeDtypeStruct((lext, LANES), F32)],
        scratch_shapes=[pltpu.VMEM((TM, LANES), F32), pltpu.VMEM((TM, LANES), F32)],
        compiler_params=_cparams(dimension_semantics=("arbitrary",)),
    )(x, ctx, modx, modc, norm_w, after)


def _in_projection(p_ext, xm, cos_t, sin_t, w, pairs, me, name):
    lext = xm.shape[0]
    tr = lext // 8
    meta = []
    for (slot_a, k_a), (slot_b, k_b) in pairs:
        assert k_a ^ k_b == 1
        dev_a = jnp.bitwise_xor(me, k_a)
        a_low = dev_a % 2 == 0
        meta += [jnp.where(a_low, slot_a, slot_b), jnp.where(a_low, slot_b, slot_a), dev_a // 2]
    meta = jnp.stack(meta).astype(jnp.int32)

    def body(meta_ref, xm_ref, cos_ref, sin_ref, w_hbm, *rest):
        p_ref, w_ref, sems = rest[-3:]
        j, i = pl.program_id(0), pl.program_id(1)

        @pl.when(i == 0)
        def _():
            copies = [pltpu.make_async_copy(w_hbm.at[meta_ref[3 * j + h]], w_ref.at[:, h * WSH:(h + 1) * WSH],
                                            sems.at[h]) for h in range(2)]
            for cp in copies:
                cp.start()
            for cp in copies:
                cp.wait()

        _project_columns(xm_ref[...], w_ref, meta_ref[3 * j + 2], cos_ref[...], sin_ref[...], p_ref)

    row = lambda width: pl.BlockSpec((tr, width), lambda j, i, meta_: (i, 0))
    grid_spec = pltpu.PrefetchScalarGridSpec(
        num_scalar_prefetch=1, grid=(len(pairs), lext // tr),
        in_specs=[row(D), row(LANES), row(LANES), pl.BlockSpec(memory_space=pl.ANY)]
        + ([] if p_ext is None else [pl.BlockSpec(memory_space=pl.ANY)]),
        out_specs=pl.BlockSpec((tr, 2 * WSH), lambda j, i, meta_: (i, meta_[3 * j + 2])),
        scratch_shapes=[pltpu.VMEM((D, 2 * WSH), BF16), pltpu.SemaphoreType.DMA((2,))])
    return pl.pallas_call(
        body, name=name, grid_spec=grid_spec,
        out_shape=jax.ShapeDtypeStruct((lext, PW), BF16),
        input_output_aliases={} if p_ext is None else {5: 0},
        compiler_params=_cparams(dimension_semantics=("arbitrary", "arbitrary")),
    )(meta, xm, cos_t, sin_t, w, *([] if p_ext is None else [p_ext]))


def _decay_tables(lgf, lgb, n):
    i = lax.broadcasted_iota(jnp.int32, (n, 1), 0).astype(F32)
    return dict(i=i, k_f=jnp.exp(lgf * (n - 1.0 - i)), k_b=jnp.exp(lgb * i),
                q_f=jnp.exp(lgf * (i + 1.0)), q_b=jnp.exp(lgb * (n - i)))


def _decay_matrix(lgf, lgb, n):
    ii = lax.broadcasted_iota(jnp.int32, (n, n), 0)
    jj = lax.broadcasted_iota(jnp.int32, (n, n), 1)
    diff = (ii - jj).astype(F32)
    low = jnp.exp(lgf * jnp.maximum(diff, 0.0))
    up = jnp.exp(lgb * jnp.maximum(-diff, 0.0))
    return jnp.where(diff > 0, low, jnp.where(diff < 0, up, 2.0)), diff


def _cat_lanes(a, b):
    return jnp.concatenate([a.astype(BF16), b.astype(BF16)], axis=1)


def _retention_forward(p_ext, lg, seq, ctx_len):
    lext = seq + ctx_len
    n_chunks = seq // RET_C
    C = RET_C

    def body(lg_ref, q_ref, k_ref, v_ref, rn_ref, rstd_ref, kv_scr, sf_scr, sb_scr):
        pair = pl.program_id(0)
        lane = lax.broadcasted_iota(jnp.int32, (1, LANES), 1)
        heads = range(2)
        hmask = [(lane // DK == hh).astype(F32) for hh in heads]
        lgf = [lg_ref[0, 2 * pair + hh] for hh in heads]
        lgb = [lg_ref[1, 2 * pair + hh] for hh in heads]
        vs = [slice(hh * DV, (hh + 1) * DV) for hh in heads]
        t = [_decay_tables(lgf[hh], lgb[hh], C) for hh in heads]
        kc_all = k_ref[seq:lext, :].astype(F32)
        s0 = []
        for hh in heads:
            tc = _decay_tables(lgf[hh], lgb[hh], ctx_len)
            kc = kc_all * hmask[hh]
            s0.append(_dot_tn(_cat_lanes(kc * tc["k_f"], kc * tc["k_b"]), v_ref[seq:lext, vs[hh]]))

        def increments(c, carry):
            rows = pl.ds(pl.multiple_of(c * C, C), C)
            k_all = k_ref[rows, :].astype(F32)
            for hh in heads:
                k = k_all * hmask[hh]
                kv_scr[hh, c] = _dot_tn(_cat_lanes(k * t[hh]["k_f"], k * t[hh]["k_b"]), v_ref[rows, vs[hh]])
            return carry

        lax.fori_loop(0, n_chunks, increments, 0, unroll=2)
        gf_c = [jnp.exp(lgf[hh] * C) for hh in heads]
        gb_c = [jnp.exp(lgb[hh] * C) for hh in heads]

        def scan_f(c, s):
            for hh in heads:
                sf_scr[hh, c] = s[hh]
            return tuple(gf_c[hh] * s[hh] + kv_scr[hh, c, 0:LANES, :] for hh in heads)

        def scan_b(n, s):
            c = n_chunks - 1 - n
            for hh in heads:
                sb_scr[hh, c] = s[hh]
            return tuple(gb_c[hh] * s[hh] + kv_scr[hh, c, LANES:2 * LANES, :] for hh in heads)

        lax.fori_loop(0, n_chunks, scan_f, tuple(s0[hh][0:LANES] for hh in heads))
        lax.fori_loop(0, n_chunks, scan_b, tuple(s0[hh][LANES:2 * LANES] for hh in heads))
        dmat = [_decay_matrix(lgf[hh], lgb[hh], C)[0] for hh in heads]

        def outputs(c, carry):
            rows = pl.ds(pl.multiple_of(c * C, C), C)
            q_all = q_ref[rows, :].astype(F32)
            k = k_ref[rows, :]
            for hh in heads:
                q = q_all * hmask[hh]
                v = v_ref[rows, vs[hh]]
                s = _dot_nt(q.astype(BF16), k)
                inner = _dot((s * dmat[hh]).astype(BF16), v)
                states = jnp.concatenate([sf_scr[hh, c], sb_scr[hh, c]], axis=0).astype(BF16)
                ret = inner + _dot(_cat_lanes(q * t[hh]["q_f"], q * t[hh]["q_b"]), states)
                mu = jnp.mean(ret, axis=-1, keepdims=True)
                cen = ret - mu
                rstd = lax.rsqrt(jnp.mean(cen * cen, axis=-1, keepdims=True) + EPS)
                rn_ref[rows, vs[hh]] = cen * rstd
                rstd_ref[rows, vs[hh]] = jnp.broadcast_to(rstd, (C, DV))
            return carry

        lax.fori_loop(0, n_chunks, outputs, 0, unroll=2)

    qk0 = CB_QK * D // LANES
    return pl.pallas_call(
        body, name="retention_forward", grid=(N_HEADS // 2,),
        in_specs=[pl.BlockSpec(memory_space=pltpu.SMEM),
                  pl.BlockSpec((lext, LANES), lambda g: (0, qk0 + g)),
                  pl.BlockSpec((lext, LANES), lambda g: (0, qk0 + N_HEADS // 2 + g)),
                  pl.BlockSpec((lext, 2 * DV), lambda g: (0, CB_V * D // (2 * DV) + g))],
        out_specs=[pl.BlockSpec((seq, 2 * DV), lambda g: (0, g))] * 2,
        out_shape=[jax.ShapeDtypeStruct((seq, D), F32)] * 2,
        scratch_shapes=[pltpu.VMEM((2, n_chunks, 2 * LANES, DV), F32), pltpu.VMEM((2, n_chunks, LANES, DV), F32),
                        pltpu.VMEM((2, n_chunks, LANES, DV), F32)],
        compiler_params=_cparams(dimension_semantics=("arbitrary",)),
    )(lg, p_ext, p_ext, p_ext)


def _retention_backward(p_ext, dret, lg, cos_t, sin_t, seq, ctx_len, after):
    lext = seq + ctx_len
    n_chunks = seq // RET_C
    C = RET_C

    def body(lg_ref, q_ref, k_ref, v_ref, do_ref, cos_ref, sin_ref, after_ref, dq_ref, dk_ref, dv_ref, dlg_ref,
             kv_scr, g_scr, sf_scr, sb_scr, gfn_scr, gbp_scr):
        pair = pl.program_id(0)
        lane = lax.broadcasted_iota(jnp.int32, (1, LANES), 1)
        heads = range(2)
        hmask = [(lane // DK == hh).astype(F32) for hh in heads]
        lgf = [lg_ref[0, 2 * pair + hh] for hh in heads]
        lgb = [lg_ref[1, 2 * pair + hh] for hh in heads]
        vs = [slice(hh * DV, (hh + 1) * DV) for hh in heads]
        t = [_decay_tables(lgf[hh], lgb[hh], C) for hh in heads]
        tc = [_decay_tables(lgf[hh], lgb[hh], ctx_len) for hh in heads]
        kc_all = k_ref[seq:lext, :].astype(F32)
        kc = [kc_all * hmask[hh] for hh in heads]
        vc = [v_ref[seq:lext, vs[hh]] for hh in heads]
        kc_cat = [_cat_lanes(kc[hh] * tc[hh]["k_f"], kc[hh] * tc[hh]["k_b"]) for hh in heads]
        s0 = [_dot_tn(kc_cat[hh], vc[hh]) for hh in heads]

        def increments(c, carry):
            rows = pl.ds(pl.multiple_of(c * C, C), C)
            k_all = k_ref[rows, :].astype(F32)
            q_all = q_ref[rows, :].astype(F32)
            for hh in heads:
                k, q = k_all * hmask[hh], q_all * hmask[hh]
                kv_scr[hh, c] = _dot_tn(_cat_lanes(k * t[hh]["k_f"], k * t[hh]["k_b"]), v_ref[rows, vs[hh]])
                g_scr[hh, c] = _dot_tn(_cat_lanes(q * t[hh]["q_f"], q * t[hh]["q_b"]), do_ref[rows, vs[hh]])
            return carry

        lax.fori_loop(0, n_chunks, increments, 0, unroll=2)
        gf_c = [jnp.exp(lgf[hh] * C) for hh in heads]
        gb_c = [jnp.exp(lgb[hh] * C) for hh in heads]

        def scan_f(c, s):
            for hh in heads:
                sf_scr[hh, c] = s[hh]
            return tuple(gf_c[hh] * s[hh] + kv_scr[hh, c, 0:LANES, :] for hh in heads)

        def scan_b(n, s):
            c = n_chunks - 1 - n
            for hh in heads:
                sb_scr[hh, c] = s[hh]
            return tuple(gb_c[hh] * s[hh] + kv_scr[hh, c, LANES:2 * LANES, :] for hh in heads)

        def scan_gf(n, carry):
            c = n_chunks - 1 - n
            for hh in heads:
                gfn_scr[hh, c] = carry[hh]
            return tuple(g_scr[hh, c, 0:LANES, :] + gf_c[hh] * carry[hh] for hh in heads)

        def scan_gb(c, carry):
            for hh in heads:
                gbp_scr[hh, c] = carry[hh]
            return tuple(g_scr[hh, c, LANES:2 * LANES, :] + gb_c[hh] * carry[hh] for hh in heads)

        lax.fori_loop(0, n_chunks, scan_f, tuple(s0[hh][0:LANES] for hh in heads))
        lax.fori_loop(0, n_chunks, scan_b, tuple(s0[hh][LANES:2 * LANES] for hh in heads))
        zero_state = jnp.zeros((LANES, DV), F32)
        gf_first = lax.fori_loop(0, n_chunks, scan_gf, (zero_state, zero_state))
        gb_last = lax.fori_loop(0, n_chunks, scan_gb, (zero_state, zero_state))

        dmat, w_f, w_b = [], [], []
        for hh in heads:
            dm, diff = _decay_matrix(lgf[hh], lgb[hh], C)
            dmat.append(dm)
            w_f.append(jnp.where(diff > 0, diff * dm, 0.0))
            w_b.append(jnp.where(diff < 0, -diff * dm, 0.0))

        def total(a):
            rows_, width = a.shape
            part = jnp.sum(a.reshape(rows_ // 8, 8, width), axis=0)
            return part[:, 0:LANES] + part[:, LANES:2 * LANES] if width == 2 * LANES else part

        def chunk_head(hh, c, rows, q_all, k_all, dlgf, dlgb):
            th = t[hh]
            qm = q_all * hmask[hh]
            km = k_all * hmask[hh]
            qb, kb = qm.astype(BF16), km.astype(BF16)
            v = v_ref[rows, vs[hh]]
            do = do_ref[rows, vs[hh]]
            s = _dot_nt(qb, kb)
            dsd = _dot_nt(do, v)
            ds = (dsd * dmat[hh]).astype(BF16)
            dq_in = _dot(ds, kb)
            dk_in = _dot_tn(ds, qb)
            dv_in = _dot_tn((s * dmat[hh]).astype(BF16), do)
            prod = s * dsd
            dlgf = dlgf + total(prod * w_f[hh])
            dlgb = dlgb + total(prod * w_b[hh])
            sf, sb = sf_scr[hh, c], sb_scr[hh, c]
            states = jnp.concatenate([sf, sb], axis=0).astype(BF16)
            dqc = _dot_nt(do, states)
            dqf = dqc[:, 0:LANES] * th["q_f"]
            dqb = dqc[:, LANES:2 * LANES] * th["q_b"]
            dq = (dq_in + dqf + dqb) * hmask[hh]
            dlgf = dlgf + total((th["i"] + 1.0) * (qm * dqf))
            dlgb = dlgb + total((C - th["i"]) * (qm * dqb))
            gfn, gbp = gfn_scr[hh, c], gbp_scr[hh, c]
            gstates = jnp.concatenate([gfn, gbp], axis=0).astype(BF16)
            dkc = _dot_nt(v, gstates)
            dkf = dkc[:, 0:LANES] * th["k_f"]
            dkb = dkc[:, LANES:2 * LANES] * th["k_b"]
            dk = dk_in + dkf + dkb
            dlgf = dlgf + total((C - 1.0 - th["i"]) * (km * dkf)) + C * gf_c[hh] * total(gfn * sf)
            dlgb = dlgb + total(th["i"] * (km * dkb)) + C * gb_c[hh] * total(gbp * sb)
            dv = dv_in + _dot(_cat_lanes(km * th["k_f"], km * th["k_b"]), gstates)
            dv_ref[rows, vs[hh]] = dv.astype(BF16)
            return dq, dk, dlgf, dlgb

        def chunk(c, carry):
            rows = pl.ds(pl.multiple_of(c * C, C), C)
            q_all = q_ref[rows, :].astype(F32)
            k_all = k_ref[rows, :].astype(F32)
            dq0, dk0, f0, b0 = chunk_head(0, c, rows, q_all, k_all, carry[0], carry[1])
            dq1, dk1, f1, b1 = chunk_head(1, c, rows, q_all, k_all, carry[2], carry[3])
            cos, sin = cos_ref[rows, :], sin_ref[rows, :]
            dq_ref[rows, :] = _rope(dq0 + dq1, cos, -sin).astype(BF16)
            dk_ref[rows, :] = (_rope(dk0 + dk1, cos, -sin) * (DK ** -0.5)).astype(BF16)
            return f0, b0, f1, b1

        zero = jnp.zeros((8, LANES), F32)
        sums = lax.fori_loop(0, n_chunks, chunk, (zero, zero, zero, zero), unroll=2)

        dlg = []
        dk_ctx = jnp.zeros((ctx_len, LANES), F32)
        for hh in heads:
            g0 = jnp.concatenate([gf_first[hh], gb_last[hh]], axis=0).astype(BF16)
            dkcc = _dot_nt(vc[hh], g0)
            dkcf = dkcc[:, 0:LANES] * tc[hh]["k_f"]
            dkcb = dkcc[:, LANES:2 * LANES] * tc[hh]["k_b"]
            dlgf = sums[2 * hh] + total((ctx_len - 1.0 - tc[hh]["i"]) * (kc[hh] * dkcf))
            dlgb = sums[2 * hh + 1] + total(tc[hh]["i"] * (kc[hh] * dkcb))
            dk_ctx = dk_ctx + (dkcf + dkcb) * (DK ** -0.5)
            dv_ref[seq:lext, vs[hh]] = _dot(kc_cat[hh], g0).astype(BF16)
            dlg += [jnp.sum(jnp.sum(a, axis=1, keepdims=True), axis=0, keepdims=True) for a in (dlgf, dlgb)]
        dk_ref[seq:lext, :] = dk_ctx.astype(BF16)
        dq_ref[seq:lext, :] = jnp.zeros((ctx_len, LANES), BF16)

        lane8 = lax.broadcasted_iota(jnp.int32, (8, LANES), 1)
        out = jnp.zeros((8, LANES), F32)
        for n, val in enumerate(dlg):
            out = jnp.where(lane8 == n, val, out)
        dlg_ref[0] = out

    qk0 = CB_QK * D // LANES
    q_spec = pl.BlockSpec((lext, LANES), lambda g: (0, qk0 + g))
    k_spec = pl.BlockSpec((lext, LANES), lambda g: (0, qk0 + N_HEADS // 2 + g))
    v_spec = pl.BlockSpec((lext, 2 * DV), lambda g: (0, CB_V * D // (2 * DV) + g))
    table = pl.BlockSpec((lext, LANES), lambda g: (0, 0))
    state = pltpu.VMEM((2, n_chunks, LANES, DV), F32)
    return pl.pallas_call(
        body, name="retention_backward", grid=(N_HEADS // 2,),
        in_specs=[pl.BlockSpec(memory_space=pltpu.SMEM), q_spec, k_spec, v_spec,
                  pl.BlockSpec((seq, 2 * DV), lambda g: (0, g)), table, table, _AFTER_SPEC],
        out_specs=[pl.BlockSpec((lext, LANES), lambda g: (0, g)), pl.BlockSpec((lext, LANES), lambda g: (0, g)),
                   pl.BlockSpec((lext, 2 * DV), lambda g: (0, g)), pl.BlockSpec((1, 8, LANES), lambda g: (g, 0, 0))],
        out_shape=[jax.ShapeDtypeStruct((lext, N_HEADS * DK), BF16), jax.ShapeDtypeStruct((lext, N_HEADS * DK), BF16),
                   jax.ShapeDtypeStruct((lext, D), BF16), jax.ShapeDtypeStruct((N_HEADS // 2, 8, LANES), F32)],
        scratch_shapes=[pltpu.VMEM((2, n_chunks, 2 * LANES, DV), F32), pltpu.VMEM((2, n_chunks, 2 * LANES, DV), F32),
                        state, state, state, state],
        compiler_params=_cparams(dimension_semantics=("arbitrary",)),
    )(lg, p_ext, p_ext, p_ext, dret, cos_t, sin_t, after)


class _ColumnWriter:
    def __init__(self, dst_hbm, stage, sems, pieces, n_steps):
        self.dst, self.stage, self.sems, self.pieces, self.n_steps = dst_hbm, stage, sems, pieces, n_steps

    def _copies(self, slot, tile):
        rows = pl.ds(pl.multiple_of(tile * TM, TM), TM)
        return [pltpu.make_async_copy(self.stage.at[slot, :, pl.ds(src, width)], self.dst.at[rows, pl.ds(dst, width)],
                                      self.sems.at[slot, k]) for k, (src, dst, width) in enumerate(self.pieces)]

    def slot_for(self, step):
        slot = step % 2

        @pl.when(step >= 2)
        def _():
            for cp in self._copies(slot, step - 2):
                cp.wait()

        return slot

    def send(self, step, slot):
        for cp in self._copies(slot, step):
            cp.start()

        @pl.when(step == self.n_steps - 1)
        def _():
            if self.n_steps >= 2:
                for cp in self._copies(1 - slot, step - 1):
                    cp.wait()
            for cp in self._copies(slot, step):
                cp.wait()


def _merge(p_ext, rn, rstd, x, target, w_a, w_b, w_out, vecs, seq):
    n_tiles = seq // TM
    hb = TM // HALO
    dp_pieces = [(0, CB_BG * D, D), (D, CB_ZA * D, D), (2 * D, CB_ZB * D, 3 * D)]

    def body(h_ref, bg_ref, cg_ref, za_ref, zb_ref, ga_ref, gb_ref, hp_ref, hn_ref, cp_ref, cn_ref,
             rn_ref, rstd_ref, x_ref, t_ref, wa_ref, wb_ref, wo_ref, vec_ref,
             dx1_ref, dp_hbm, dconv_ref, dret_ref, at_ref, b_ref, part_ref, stage, dp_sems):
        i = pl.program_id(0)
        writer = _ColumnWriter(dp_hbm, stage, dp_sems, dp_pieces, n_tiles)
        slot = writer.slot_for(i)
        dpa_ref = stage.at[slot]
        f = lambda ref: ref[...].astype(F32)
        h, bg, cg, za, zb, ga, gb = f(h_ref), f(bg_ref), f(cg_ref), f(za_ref), f(zb_ref), f(ga_ref), f(gb_ref)
        gx, w0, w1, w2 = vec_ref[0:1, :], vec_ref[1:2, :], vec_ref[2:3, :], vec_ref[3:4, :]
        cb, gnw, fw = vec_ref[4:5, :], vec_ref[5:6, :], vec_ref[6:7, :]
        u = cg * h
        row = lax.broadcasted_iota(jnp.int32, (TM, 1), 0)
        u_prev = (f(cp_ref) * f(hp_ref))[HALO - 1:HALO, :]
        u_next = (f(cn_ref) * f(hn_ref))[0:1, :]
        u_prev = jnp.where(i == 0, 0.0, u_prev)
        u_next = jnp.where(i == n_tiles - 1, 0.0, u_next)
        u_up = jnp.where(row == 0, u_prev, pltpu.roll(u, 1, 0))
        u_dn = jnp.where(row == TM - 1, u_next, pltpu.roll(u, TM - 1, 0))
        conv = w0 * u_up + w1 * u + w2 * u_dn + cb
        sza = _sigmoid(za)
        silu_za = za * sza
        a_act = silu_za * bg * conv
        rn = rn_ref[...]
        szb = _sigmoid(zb)
        silu_zb = zb * szb
        rg = rn * gnw
        b_act = silu_zb * rg
        y_a = _dot(a_act.astype(BF16), wa_ref[...])
        y_b = _dot(b_act.astype(BF16), wb_ref[...])
        sga, sgb = _sigmoid(ga), _sigmoid(gb)
        mix = sga * y_a + sgb * y_b
        y = _dot(mix.astype(BF16), wo_ref[...])
        x1 = x_ref[...] + gx * y
        r1 = lax.rsqrt(jnp.mean(x1 * x1, axis=-1, keepdims=True) + EPS)
        xh1 = x1 * r1
        err = xh1 * fw - t_ref[...]
        loss = jnp.sum(jnp.sum(err * err, axis=1, keepdims=True), axis=0, keepdims=True) * (0.5 / D)
        dout = err * (1.0 / D)
        dxh = dout * fw
        dx1 = r1 * (dxh - xh1 * jnp.mean(dxh * xh1, axis=-1, keepdims=True))
        dx1_ref[...] = dx1
        dy = (dx1 * gx).astype(BF16)
        dmix = _dot_nt(dy, wo_ref[...])
        dya = (dmix * sga).astype(BF16)
        dyb = (dmix * sgb).astype(BF16)
        da = _dot_nt(dya, wa_ref[...])
        db = _dot_nt(dyb, wb_ref[...])
        dpa_ref[:, 0:D] = (da * silu_za * conv).astype(BF16)
        dpa_ref[:, D:2 * D] = (da * bg * conv * (sza * (1.0 + za * (1.0 - sza)))).astype(BF16)
        dpa_ref[:, 2 * D:3 * D] = (db * rg * (szb * (1.0 + zb * (1.0 - szb)))).astype(BF16)
        dpa_ref[:, 3 * D:4 * D] = (dmix * y_a * sga * (1.0 - sga)).astype(BF16)
        dpa_ref[:, 4 * D:5 * D] = (dmix * y_b * sgb * (1.0 - sgb)).astype(BF16)
        dconv_ref[...] = (da * silu_za * bg).astype(BF16)
        drn_n = db * silu_zb
        drn = drn_n * gnw
        rstd = rstd_ref[...]
        for hd in range(N_HEADS):
            sl = slice(hd * DV, (hd + 1) * DV)
            dh_, rh = drn[:, sl], rn[:, sl]
            m1 = jnp.mean(dh_, axis=-1, keepdims=True)
            m2 = jnp.mean(dh_ * rh, axis=-1, keepdims=True)
            dret_ref[:, sl] = (rstd[:, sl] * (dh_ - m1 - rh * m2)).astype(BF16)
        at_ref[0] = a_act.T.astype(BF16)
        at_ref[1] = b_act.T.astype(BF16)
        at_ref[2] = mix.T.astype(BF16)
        b_ref[0] = dya
        b_ref[1] = dyb
        b_ref[2] = dy

        @pl.when(i == 0)
        def _():
            part_ref[...] = jnp.zeros((8, D), F32)

        part_ref[0:1, :] += jnp.sum(dout * xh1, axis=0, keepdims=True)
        part_ref[1:2, :] += jnp.sum(dx1 * y, axis=0, keepdims=True)
        part_ref[2:3, :] += jnp.sum(drn_n * rn, axis=0, keepdims=True)
        part_ref[3:4, :] += jnp.broadcast_to(loss, (1, D))
        writer.send(i, slot)

    col = lambda cb_: pl.BlockSpec((TM, D), lambda i, cb_=cb_: (i, cb_))
    prev = lambda cb_: pl.BlockSpec((HALO, D), lambda i, cb_=cb_: (jnp.maximum(i * hb - 1, 0), cb_))
    nxt = lambda cb_: pl.BlockSpec((HALO, D), lambda i, cb_=cb_: (jnp.minimum((i + 1) * hb, n_tiles * hb - 1), cb_))
    tile = pl.BlockSpec((TM, D), lambda i: (i, 0))
    full = lambda a: pl.BlockSpec(a.shape, lambda i: (0,) * a.ndim, pipeline_mode=pl.Buffered(1))
    return pl.pallas_call(
        body, name="merge", grid=(n_tiles,),
        in_specs=[col(CB_H), col(CB_BG), col(CB_CG), col(CB_ZA), col(CB_ZB), col(CB_GA), col(CB_GB),
                  prev(CB_H), nxt(CB_H), prev(CB_CG), nxt(CB_CG),
                  tile, tile, tile, tile, full(w_a), full(w_b), full(w_out), full(vecs)],
        out_specs=[tile, pl.BlockSpec(memory_space=pl.ANY), tile, tile,
                   pl.BlockSpec((3, D, TM), lambda i: (0, 0, i)), pl.BlockSpec((3, TM, D), lambda i: (0, i, 0)),
                   pl.BlockSpec((8, D), lambda i: (0, 0))],
        out_shape=[jax.ShapeDtypeStruct((seq, D), F32), jax.ShapeDtypeStruct((p_ext.shape[0], PW), BF16),
                   jax.ShapeDtypeStruct((seq, D), BF16), jax.ShapeDtypeStruct((seq, D), BF16),
                   jax.ShapeDtypeStruct((3, D, seq), BF16), jax.ShapeDtypeStruct((3, seq, D), BF16),
                   jax.ShapeDtypeStruct((8, D), F32)],
        scratch_shapes=[pltpu.VMEM((2, TM, 5 * D), BF16), pltpu.SemaphoreType.DMA((2, len(dp_pieces)))],
        compiler_params=_cparams(dimension_semantics=("arbitrary",)),
    )(p_ext, p_ext, p_ext, p_ext, p_ext, p_ext, p_ext, p_ext, p_ext, p_ext, p_ext,
      rn, rstd, x, target, w_a, w_b, w_out, vecs)


def _conv_backward(p_ext, dconv, dp, dq, dk, dv, vecs, seq, after):
    n_tiles = seq // TM
    hb = TM // HALO
    qk = N_HEADS * DK
    pieces = [(0, CB_H * D, D), (D, CB_CG * D, D), (2 * D, CB_QK * D, 2 * D)]
    zero_pieces = [(CB_BG * D, D), (CB_ZA * D, D), (CB_ZB * D, 3 * D)]

    def body(h_ref, cg_ref, dc_ref, dcp_ref, dcn_ref, dq_ref, dk_ref, dv_ref, vec_ref, dp_in, after_ref, dp_hbm, part_ref,
             stage, sems, zeros, zero_sems):
        del dp_in, after_ref
        i = pl.program_id(0)
        writer = _ColumnWriter(dp_hbm, stage, sems, pieces, n_tiles + 1)
        slot = writer.slot_for(i)
        out = stage.at[slot]
        out[:, 2 * D:2 * D + qk] = dq_ref[...]
        out[:, 2 * D + qk:3 * D] = dk_ref[...]
        out[:, 3 * D:4 * D] = dv_ref[...]

        @pl.when(i == 0)
        def _():
            part_ref[...] = jnp.zeros((8, D), F32)

        @pl.when(i == n_tiles)
        def _():
            out[:, 0:2 * D] = jnp.zeros((TM, 2 * D), BF16)
            zeros[...] = jnp.zeros(zeros.shape, BF16)
            rows = pl.ds(n_tiles * TM, TM)
            fills = [pltpu.make_async_copy(zeros.at[:, pl.ds(0, width)], dp_hbm.at[rows, pl.ds(dst, width)],
                                           zero_sems.at[k]) for k, (dst, width) in enumerate(zero_pieces)]
            for cp in fills:
                cp.start()
            for cp in fills:
                cp.wait()

        @pl.when(i < n_tiles)
        def _():
            f = lambda ref: ref[...].astype(F32)
            h, cg, dc = f(h_ref), f(cg_ref), f(dc_ref)
            w0, w1, w2 = vec_ref[1:2, :], vec_ref[2:3, :], vec_ref[3:4, :]
            row = lax.broadcasted_iota(jnp.int32, (TM, 1), 0)
            dc_prev = jnp.where(i == 0, 0.0, f(dcp_ref)[HALO - 1:HALO, :])
            dc_next = jnp.where(i == n_tiles - 1, 0.0, f(dcn_ref)[0:1, :])
            dc_up = jnp.where(row == 0, dc_prev, pltpu.roll(dc, 1, 0))
            dc_dn = jnp.where(row == TM - 1, dc_next, pltpu.roll(dc, TM - 1, 0))
            du = w0 * dc_dn + w1 * dc + w2 * dc_up
            u = cg * h
            part_ref[0:1, :] += jnp.sum(u * dc_dn, axis=0, keepdims=True)
            part_ref[1:2, :] += jnp.sum(u * dc, axis=0, keepdims=True)
            part_ref[2:3, :] += jnp.sum(u * dc_up, axis=0, keepdims=True)
            part_ref[3:4, :] += jnp.sum(dc, axis=0, keepdims=True)
            out[:, 0:D] = (du * cg).astype(BF16)
            out[:, D:2 * D] = (du * h).astype(BF16)

        writer.send(i, slot)

    last = n_tiles - 1
    col = lambda cb_: pl.BlockSpec((TM, D), lambda i, cb_=cb_: (jnp.minimum(i, last), cb_))
    lat = lambda w: pl.BlockSpec((TM, w), lambda i: (jnp.minimum(i, last), 0))
    ext = lambda w: pl.BlockSpec((TM, w), lambda i: (i, 0))
    anyspec = pl.BlockSpec(memory_space=pl.ANY)
    return pl.pallas_call(
        body, name="conv_backward", grid=(n_tiles + 1,),
        in_specs=[col(CB_H), col(CB_CG), lat(D),
                  pl.BlockSpec((HALO, D), lambda i: (jnp.clip(i * hb - 1, 0, n_tiles * hb - 1), 0)),
                  pl.BlockSpec((HALO, D), lambda i: (jnp.minimum((i + 1) * hb, n_tiles * hb - 1), 0)),
                  ext(qk), ext(qk), ext(D), pl.BlockSpec(vecs.shape, lambda i: (0, 0)), anyspec, _AFTER_SPEC],
        out_specs=[anyspec, pl.BlockSpec((8, D), lambda i: (0, 0))],
        out_shape=[jax.ShapeDtypeStruct(dp.shape, BF16), jax.ShapeDtypeStruct((8, D), F32)],
        input_output_aliases={9: 0},
        scratch_shapes=[pltpu.VMEM((2, TM, 4 * D), BF16), pltpu.SemaphoreType.DMA((2, len(pieces))),
                        pltpu.VMEM((TM, 3 * D), BF16), pltpu.SemaphoreType.DMA((len(zero_pieces),))],
        compiler_params=_cparams(dimension_semantics=("arbitrary",)),
    )(p_ext, p_ext, dconv, dconv, dconv, dq, dk, dv, vecs, dp, after)


def _input_backward(dp, shards, ids, x, ctx, dx1, modx, modc, norm_w, after):
    seq = x.shape[0]
    lext = seq + ctx.shape[0]
    n_x = seq // TM
    n_w = len(shards.arrays)

    def body(ids_ref, dp_ref, x_ref, ctx_ref, dx1_ref, mx_ref, mc_ref, nw_ref, *rest):
        w_hbm, (_, gx_ref, part_ref, w_ref, w_sems) = rest[:n_w], rest[n_w:]
        i = pl.program_id(0)
        _load_w_in(i, ids_ref, shards, w_hbm, w_ref, w_sems)
        is_ctx = i >= n_x
        dxm = _dot_nt(dp_ref[...], w_ref[...])
        x = jnp.where(is_ctx, ctx_ref[...], x_ref[...])
        r = lax.rsqrt(jnp.mean(x * x, axis=-1, keepdims=True) + EPS)
        xh = x * r
        nw = nw_ref[...]
        sc = jnp.where(is_ctx, mc_ref[1:2, :], mx_ref[1:2, :])
        dxn = dxm * (1.0 + sc)
        dxh = dxn * nw
        dx = r * (dxh - xh * jnp.mean(dxh * xh, axis=-1, keepdims=True))

        @pl.when(jnp.logical_not(is_ctx))
        def _():
            gx_ref[...] = dx1_ref[...] + dx

        @pl.when(i == 0)
        def _():
            part_ref[...] = jnp.zeros((8, D), F32)

        fx = jnp.where(is_ctx, 0.0, 1.0)
        d_shift = jnp.sum(dxm, axis=0, keepdims=True)
        d_scale = jnp.sum(dxm * (xh * nw), axis=0, keepdims=True)
        part_ref[0:1, :] += fx * d_shift
        part_ref[1:2, :] += fx * d_scale
        part_ref[2:3, :] += jnp.sum(dxn * xh, axis=0, keepdims=True)
        part_ref[3:4, :] += (1.0 - fx) * d_shift
        part_ref[4:5, :] += (1.0 - fx) * d_scale

    lat = lambda w: pl.BlockSpec((TM, w), lambda i, ids_: (jnp.minimum(i, n_x - 1), 0))
    ext = lambda w: pl.BlockSpec((TM, w), lambda i, ids_: (i, 0))
    full = lambda a: pl.BlockSpec(a.shape, lambda i, ids_: (0,) * a.ndim)
    grid_spec = pltpu.PrefetchScalarGridSpec(
        num_scalar_prefetch=1, grid=(lext // TM,),
        in_specs=[ext(PW), lat(D), full(ctx), lat(D), full(modx), full(modc), full(norm_w)]
        + [pl.BlockSpec(memory_space=pl.ANY)] * n_w + [_AFTER_SPEC],
        out_specs=[lat(D), pl.BlockSpec((8, D), lambda i, ids_: (0, 0))],
        scratch_shapes=[pltpu.VMEM((D, PW), BF16), pltpu.SemaphoreType.DMA((N_DEV,))])
    return pl.pallas_call(
        body, name="input_backward", grid_spec=grid_spec,
        out_shape=[jax.ShapeDtypeStruct((seq, D), F32), jax.ShapeDtypeStruct((8, D), F32)],
        compiler_params=_cparams(dimension_semantics=("arbitrary",)),
    )(ids, dp, x, ctx, dx1, modx, modc, norm_w, *shards.arrays, after)


def _weight_grad_in(xm_t, dp, owners, name, after):
    lext = xm_t.shape[1]

    def body(own_ref, a_ref, b_ref, after_ref, o_ref):
        o_ref[0] = _dot(a_ref[...], b_ref[...])

    grid_spec = pltpu.PrefetchScalarGridSpec(
        num_scalar_prefetch=1, grid=(4,),
        in_specs=[pl.BlockSpec((D, lext), lambda j, own: (0, 0)),
                  pl.BlockSpec((lext, WSH), lambda j, own: (0, own[j])), _AFTER_SPEC],
        out_specs=pl.BlockSpec((1, D, WSH), lambda j, own: (j, 0, 0)))
    return pl.pallas_call(
        body, name=name, grid_spec=grid_spec,
        out_shape=jax.ShapeDtypeStruct((4, D, WSH), F32),
        compiler_params=_cparams(dimension_semantics=("arbitrary",)),
    )(owners, xm_t, dp, after)


def _weight_grad_square(a_t, b):
    seq = a_t.shape[2]

    def body(a_ref, b_ref, o_ref):
        o_ref[:, 0] = _dot(a_ref[0], b_ref[0]).reshape(N_DEV, RSH, D)

    return pl.pallas_call(
        body, name="weight_grad_square", grid=(3,),
        in_specs=[pl.BlockSpec((1, D, seq), lambda t: (t, 0, 0)), pl.BlockSpec((1, seq, D), lambda t: (t, 0, 0))],
        out_specs=pl.BlockSpec((N_DEV, 1, RSH, D), lambda t: (0, t, 0, 0)),
        out_shape=jax.ShapeDtypeStruct((N_DEV, 3, RSH, D), F32),
        compiler_params=_cparams(dimension_semantics=("arbitrary",)),
    )(a_t, b)


def _rows(total, *parts):
    width = max(a.shape[1] for _, a in parts)
    out = None
    for row, a in parts:
        padded = jnp.pad(a, ((row, total - row - a.shape[0]), (0, width - a.shape[1])))
        out = padded if out is None else out + padded
    return out


def kernel(x, c, ctx, c_ctx, norm_w, ada_w, ada_b, w_in, conv_w, conv_b, decay_logit, gn_w, w_a, w_b, w_out, final_norm_w, loss_target, m_c_ctx, m_norm_w, m_ada_w, m_ada_b, m_w_in, m_conv_w, m_conv_b, m_decay_logit, m_gn_w, m_w_a, m_w_b, m_w_out, m_final_norm_w, v_c_ctx, v_norm_w, v_ada_w, v_ada_b, v_w_in, v_conv_w, v_conv_b, v_decay_logit, v_gn_w, v_w_a, v_w_b, v_w_out, v_final_norm_w):
    xi, yi, ci = _coords()
    me = 4 * xi + 2 * yi + ci
    chip = 2 * xi + yi
    seq, ctx_len = x.shape[1], ctx.shape[1]
    assert seq % TM == 0 and seq % RET_C == 0 and ctx_len == TM and seq % GRID_W == 0
    csh = D // N_DEV

    blk = jnp.pad(c, ((0, 7), (0, 0))) + jnp.pad(conv_w[0], ((1, 4), (0, D - csh)))
    got, _ = _all_gather_small(blk, "gather_cond")
    conv_w_all = got[:, 1:4, 0:csh].transpose(1, 0, 2).reshape(3, D)
    c16 = _rows(16, (0, got[:, 0, :]), (N_DEV, c_ctx[None]))
    ada_b_sh = lax.dynamic_slice(ada_b, (0, me * ADA_SH), (1, ADA_SH))
    mod_sh, act16, lg = _modulation(c16, ada_w[0], ada_b_sh, decay_logit[0])
    mod_all, small_done = _all_gather_small(mod_sh, "gather_mod")
    mod_all = mod_all.transpose(1, 0, 2).reshape(16, 3 * D)
    modx = lax.dynamic_slice(mod_all, (me, 0), (1, 3 * D)).reshape(3, D)
    modc = mod_all[8].reshape(3, D)

    x2, tgt = x[0], loss_target[0]
    wblock = lambda n: jax.ShapeDtypeStruct((n, D, WSH), BF16)
    ids_of = lambda ks: jnp.stack([jnp.bitwise_xor(me, k) for k in ks]).astype(jnp.int32)
    own_shard = w_in[0].astype(BF16)[None]
    sq_shards = [w[0].astype(BF16) for w in (w_a, w_b, w_out)]
    wave1 = _Exchange(
        "gather_in_first", [jnp.broadcast_to(own_shard, (2, D, WSH)), own_shard], [wblock(4)], 3,
        lambda srcs, lands: _forward_plan([(0, 1)])(srcs, lands) + _send_plan((4, 2))(srcs[1:], lands),
        after=small_done)
    xm, xm_t, cos_t, sin_t = _prenorm(x2, ctx[0], modx, modc, norm_w, wave1.token)
    (w_pair, _), _ = wave1.wait(xm_t, 1, through=[0])
    first = _Shards([w_pair], [(0, 0), (0, 1)])
    p_ext = _in_projection(None, xm, cos_t, sin_t, w_pair, [((0, 0), (1, 1))], me, "in_projection_pair")
    _, (w_nbr,) = wave1.wait(p_ext, through=[1, 2])
    wave2 = _Exchange(
        "gather_in_second", [w_nbr, own_shard], [wblock(2)], 3,
        lambda srcs, lands: _forward_plan([(0, 2), (1, 3)])(srcs, lands) + _send_plan((6,))(srcs[1:], lands))
    (w_nbr, _), _ = wave2.wait(wave2.token, 2, through=[0])
    second = _Shards([w_nbr], [(0, 0), (0, 1), (0, 2), (0, 3)])
    p_ext = _in_projection(p_ext, xm, cos_t, sin_t, w_nbr, [((0, 4), (2, 5)), ((1, 2), (3, 3))], me,
                           "in_projection_neighbours")
    _, (w_diag,) = wave2.wait(p_ext, through=[1, 2])
    wave3 = _Exchange(
        "gather_in_third", [w_diag] + sq_shards, [jax.ShapeDtypeStruct((N_DEV, RSH, D), BF16)] * 3, 1 + 3 * (N_DEV - 1),
        lambda srcs, lands: _forward_plan([(0, 1)])(srcs, lands) + _gather_plan(srcs[1:], lands))
    (w_diag, *_), _ = wave3.wait(wave3.token, 1, through=[0])
    third = _Shards([w_diag], [(0, 0), (0, 1)])
    p_ext = _in_projection(p_ext, xm, cos_t, sin_t, w_diag, [((0, 6), (1, 7))], me, "in_projection_diagonal")
    w_shards, w_ids = first + second + third, ids_of((0, 1, 4, 2, 5, 3, 6, 7))

    rn, rstd = _retention_forward(p_ext, lg, seq, ctx_len)
    _, sq_lands = wave3.wait(rstd, through=range(1, 7))
    w_a_all, w_b_all, w_out_all = (
        lax.dynamic_update_slice(land, shard[None], (me, 0, 0)).reshape(D, D) for land, shard in zip(sq_lands, sq_shards))
    vecs = _rows(8, (0, modx[2:3]), (1, conv_w_all), (4, conv_b), (5, gn_w), (6, final_norm_w[None]))
    dx1, dp, dconv, dret, op_at, op_b, part_m = _merge(p_ext, rn, rstd, x2, tgt, w_a_all, w_b_all, w_out_all, vecs, seq)

    j4 = jnp.arange(4, dtype=jnp.int32)
    owners = (2 * jnp.bitwise_xor(chip, j4) + ci).astype(jnp.int32)
    owners_sib = (2 * jnp.bitwise_xor(chip, j4) + (1 - ci)).astype(jnp.int32)
    gw_sq = _weight_grad_square(op_at, op_b).reshape(N_DEV, 3 * RSH, D)
    rs_sq_pair = _Exchange("rs_square_pair", [gw_sq], [jax.ShapeDtypeStruct((4, 3 * RSH, D), F32)], 4,
                           _pair_plan(lambda j, chip_, c_: 2 * jnp.bitwise_xor(chip_, j) + (1 - c_)))
    dq, dk, dv, dlg = _retention_backward(p_ext, dret, lg, cos_t, sin_t, seq, ctx_len, rs_sq_pair.token)
    (gw_sq,), (r1_sq,) = rs_sq_pair.wait(dlg)
    own_sq, send_sq = _pair_sum(gw_sq, r1_sq, owners, "pair_sum_square")
    rs_sq_chips = _Exchange("rs_square_chips", [send_sq], [jax.ShapeDtypeStruct((3, 3 * RSH, D), BF16)], 3, _chips_plan)
    dp, part_c = _conv_backward(p_ext, dconv, dp, dq, dk, dv, vecs, seq, rs_sq_chips.token)
    gw_sib = _weight_grad_in(xm_t, dp, owners_sib, "weight_grad_in_sibling", part_c)
    rs_in_pair = _Exchange("rs_in_pair", [gw_sib], [jax.ShapeDtypeStruct((4, D, WSH), F32)], 4,
                           _pair_plan(lambda j, chip_, c_: j))
    gw_own = _weight_grad_in(xm_t, dp, owners, "weight_grad_in_own", rs_in_pair.token)
    _, (r1_in,) = rs_in_pair.wait(gw_own)
    own_in, send_in = _pair_sum(gw_own, r1_in, j4, "pair_sum_in")
    rs_in_chips = _Exchange("rs_in_chips", [send_in], [jax.ShapeDtypeStruct((3, D, WSH), BF16)], 3, _chips_plan)
    grad_x, part_i = _input_backward(dp, w_shards, w_ids, x2, ctx[0], dx1, modx, modc, norm_w, rs_in_chips.token)

    dl = dlg[:, 0, 0:4]
    dlg_row = jnp.pad(dl[:, 0::2].reshape(1, N_HEADS), ((0, 0), (0, D - N_HEADS))) + jnp.pad(
        dl[:, 1::2].reshape(1, N_HEADS), ((0, 0), (N_HEADS, D - 2 * N_HEADS)))
    partials = _rows(16, (0, part_m[0:1]), (1, part_i[2:3]), (2, part_c[3:4]), (3, part_m[2:3]), (4, part_c[0:3]),
                     (7, part_i[0:2]), (9, part_m[1:2]), (10, part_i[3:5]), (13, part_m[3:4]), (14, dlg_row))
    results = {}
    gather_part = _Exchange("gather_partials", [partials], [jax.ShapeDtypeStruct((N_DEV, 16, D), F32)],
                            N_DEV - 1, _gather_plan)
    _, (r2_in,) = rs_in_chips.wait(gather_part.token)
    results["w_in"] = [o[None] for o in _adam_sharded(w_in[0], m_w_in[0], v_w_in[0], own_in, r2_in, "adam_w_in")]
    _, (got,) = gather_part.wait(results["w_in"][1])
    got = lax.dynamic_update_slice(got, partials[None], (me, 0, 0))
    tot = _sum_devices(got, "sum_partials")
    dmodc = tot[10:13].reshape(1, 3 * D)
    dmod16 = _rows(16, (0, got[:, 7:10, :].reshape(N_DEV, 3 * D)), (N_DEV, dmodc))
    dmod16 = lax.dynamic_slice(dmod16, (0, me * ADA_SH), (16, ADA_SH))
    dmodc8 = _rows(8, (0, dmod16[8:9]))
    g_ada_w, d_ada_w, nm_ada_w, nv_ada_w, cctx_part = _ada_backward(
        act16.T, dmod16, dmodc8, ada_w[0], m_ada_w[0], v_ada_w[0])
    gather_cctx = _Exchange("gather_cctx", [cctx_part], [jax.ShapeDtypeStruct((N_DEV, 8, D), F32)],
                            N_DEV - 1, _gather_plan)
    _, (r2_sq,) = rs_sq_chips.wait(gather_cctx.token)
    square = _adam_square([(w_a[0], m_w_a[0], v_w_a[0]), (w_b[0], m_w_b[0], v_w_b[0]),
                           (w_out[0], m_w_out[0], v_w_out[0])], own_sq, r2_sq)
    _, (got_cctx,) = gather_cctx.wait(square[0][1])
    dact_ctx = _sum_devices(lax.dynamic_update_slice(got_cctx, cctx_part[None], (me, 0, 0)), "sum_cctx")

    view = {"c_ctx": (1, D), "final_norm_w": (1, D)}
    given = {"c_ctx": (c_ctx, m_c_ctx, v_c_ctx), "norm_w": (norm_w, m_norm_w, v_norm_w),
             "conv_b": (conv_b, m_conv_b, v_conv_b), "gn_w": (gn_w, m_gn_w, v_gn_w),
             "final_norm_w": (final_norm_w, m_final_norm_w, v_final_norm_w), "ada_b": (ada_b, m_ada_b, v_ada_b),
             "decay_logit": (decay_logit, m_decay_logit, v_decay_logit), "conv_w": (conv_w, m_conv_w, v_conv_w)}
    small_in = [tuple(a.reshape(view.get(name, a.shape)) for a in given[name]) for name in SMALL_PARAMS]
    conv_w_grad = lax.dynamic_slice(tot, (4, me * csh), (3, csh))
    small_out = _adam_small(tot, dact_ctx, conv_w_grad, tot[14, 0:2 * N_HEADS].reshape(2, N_HEADS), small_in)
    results.update({name: [o.reshape(given[name][0].shape) for o in outs_]
                    for name, outs_ in zip(SMALL_PARAMS, small_out)})
    for name, outs_ in zip(("w_a", "w_b", "w_out"), square):
        results[name] = [o[None] for o in outs_]
    results["ada_w"] = [o[None] for o in (g_ada_w, d_ada_w, nm_ada_w, nv_ada_w)]

    order = ("c_ctx", "norm_w", "ada_w", "ada_b", "w_in", "conv_w", "conv_b", "decay_logit", "gn_w",
             "w_a", "w_b", "w_out", "final_norm_w")
    outs = [results[name][kind] for kind in range(4) for name in order]
    return (tot[13, 0], grad_x[None], *outs)
```

```python
import math

import jax
import jax.numpy as jnp
from jax import lax
from jax.experimental import pallas as pl
from jax.experimental.pallas import tpu as pltpu

F32 = jnp.float32
BF16 = jnp.bfloat16
MESH = pl.DeviceIdType.MESH
_HBM_SPEC = pl.BlockSpec(memory_space=pltpu.HBM)
_SEM_SPEC = pl.BlockSpec(memory_space=pltpu.SEMAPHORE)
_EFFECT = pltpu.SideEffectType.DATAFLOW_SIDE_EFFECTING

N_DEV = 8
D = 1024
N_HEADS = 8
DK = 64
DV = 128
GRID_W = 64
ROPE_BASE = 10000.0
EPS = 1e-6
PW = 9 * D
WSH = PW // N_DEV
RSH = D // N_DEV
ADA_SH = 3 * D // N_DEV
TM = 256
RET_C = 256
HALO = 16
LANES = 128
VMEM_LIMIT = 60 * 1024 * 1024

ADAM_LR = 0.001
ADAM_B1 = 0.9
ADAM_B2 = 0.999
ADAM_EPS = 1e-08
ADAM_WD = 0.01
ADAM_STEP = 10

CB_H, CB_BG, CB_CG, CB_ZA, CB_QK, CB_V, CB_ZB, CB_GA, CB_GB = range(9)


def _cparams(**kw):
    return pltpu.CompilerParams(vmem_limit_bytes=VMEM_LIMIT, **kw)


def _dot(a, b):
    return jnp.dot(a, b, preferred_element_type=F32)


def _dot_nt(a, b):
    return lax.dot_general(a, b, (((1,), (1,)), ((), ())), preferred_element_type=F32)


def _dot_tn(a, b):
    return lax.dot_general(a, b, (((0,), (0,)), ((), ())), preferred_element_type=F32)


def _sigmoid(z):
    return 0.5 * jnp.tanh(0.5 * z) + 0.5


def _row_tile(rows):
    return TM if rows % TM == 0 else rows


def _coords():
    return lax.axis_index("x"), lax.axis_index("y"), lax.axis_index("c")


def _flip(v, bit):
    return 1 - v if bit else v


def _all_gather_small(blk, name):
    rows, cols = blk.shape

    def body(x_ref, out_ref, done, send_sems, recv_sems, local_sem):
        x, y, c = _coords()
        me = 4 * x + 2 * y + c
        mine = pltpu.make_async_copy(x_ref, out_ref.at[me], local_sem)
        mine.start()

        def copy(k, slot):
            peer = (_flip(x, k & 4), _flip(y, k & 2), _flip(c, k & 1))
            return pltpu.make_async_remote_copy(
                src_ref=x_ref, dst_ref=out_ref.at[slot], send_sem=send_sems.at[k - 1],
                recv_sem=recv_sems.at[k - 1], device_id=peer, device_id_type=MESH)

        for k in range(1, N_DEV):
            copy(k, me).start()
        for k in range(1, N_DEV):
            copy(k, jnp.bitwise_xor(me, k)).wait_recv()
        for k in range(1, N_DEV):
            copy(k, me).wait_send()
        mine.wait()
        done[...] = jnp.zeros((8, LANES), F32)

    vmem = pl.BlockSpec(memory_space=pltpu.VMEM)
    return pl.pallas_call(
        body, name=name,
        out_shape=[jax.ShapeDtypeStruct((N_DEV, rows, cols), blk.dtype), jax.ShapeDtypeStruct((8, LANES), F32)],
        in_specs=[vmem], out_specs=[vmem, vmem],
        scratch_shapes=[pltpu.SemaphoreType.DMA((N_DEV - 1,)), pltpu.SemaphoreType.DMA((N_DEV - 1,)),
                        pltpu.SemaphoreType.DMA],
    )(blk)


class _Absent:
    at = property(lambda self: self)

    def __getitem__(self, index):
        return self


class _Exchange:
    def __init__(self, name, srcs, land_shapes, n_copies, plan, after=None):
        self.name, self.plan, self.n_copies = name, plan, n_copies
        self.n_src, self.n_land = len(srcs), len(land_shapes)
        hbm = lambda a: pltpu.HBM(a.shape, a.dtype)
        n = self.n_src + self.n_land
        lands = [pltpu.with_memory_space_constraint(lax.empty(s.shape, s.dtype), pltpu.HBM) for s in land_shapes]
        srcs = [pltpu.with_memory_space_constraint(s, pltpu.HBM) for s in srcs]
        extra = [] if after is None else [after]

        def body(*refs):
            send_sems, recv_sems = refs[n + len(extra)], refs[n + len(extra) + 1]
            for cp in self._copies(refs, send_sems, recv_sems):
                cp.start()
            refs[-1][...] = jnp.zeros((8, LANES), F32)

        outs = pl.pallas_call(
            body, name=name + "_start",
            out_shape=(pltpu.SemaphoreType.DMA((n_copies,)), pltpu.SemaphoreType.DMA((n_copies,)),
                       *[hbm(a) for a in srcs], *[hbm(a) for a in lands], jax.ShapeDtypeStruct((8, LANES), F32)),
            in_specs=[_HBM_SPEC] * n + [pl.BlockSpec(memory_space=pl.ANY)] * len(extra),
            out_specs=(_SEM_SPEC, _SEM_SPEC, *[_HBM_SPEC] * n, pl.BlockSpec(memory_space=pltpu.VMEM)),
            input_output_aliases={i: 2 + i for i in range(n)},
            compiler_params=pltpu.CompilerParams(has_side_effects=_EFFECT),
        )(*srcs, *lands, *extra)
        self.send_sems, self.recv_sems = outs[0], outs[1]
        self.buffers = list(outs[2:2 + n])
        self.token = outs[-1]
        self.waited = 0

    def _copies(self, refs, send_sems, recv_sems, lo=0, hi=None):
        src_refs, land_refs = refs[:self.n_src], refs[self.n_src:self.n_src + self.n_land]
        planned = list(enumerate(self.plan(src_refs, land_refs)))[lo:hi]
        return [pltpu.make_async_remote_copy(src_ref=s, dst_ref=d, send_sem=send_sems.at[k], recv_sem=recv_sems.at[k],
                                             device_id=dev, device_id_type=MESH) for k, (s, d, dev) in planned]

    def wait(self, after, count=None, through=None):
        n = self.n_src + self.n_land
        lo = self.waited
        hi = self.n_copies if count is None else lo + count
        self.waited = hi
        through = list(range(n)) if through is None else list(through)
        m = len(through)

        def body(*refs):
            held = [_Absent()] * n
            for pos, ref in zip(through, refs[:m]):
                held[pos] = ref
            for cp in self._copies(held, refs[m], refs[m + 1], lo, hi):
                cp.wait_send()
                cp.wait_recv()

        outs = pl.pallas_call(
            body, name=f"{self.name}_wait{lo}" if lo or hi < self.n_copies else self.name + "_wait",
            out_shape=tuple(pltpu.HBM(self.buffers[p].shape, self.buffers[p].dtype) for p in through),
            in_specs=[_HBM_SPEC] * m + [_SEM_SPEC, _SEM_SPEC, pl.BlockSpec(memory_space=pl.ANY)],
            out_specs=tuple([_HBM_SPEC] * m),
            input_output_aliases={i: i for i in range(m)},
            compiler_params=pltpu.CompilerParams(has_side_effects=_EFFECT),
        )(*[self.buffers[p] for p in through], self.send_sems, self.recv_sems, after)
        for pos, out in zip(through, outs):
            self.buffers[pos] = out
        return list(self.buffers[:self.n_src]), list(self.buffers[self.n_src:])


def _pair_plan(src_index):
    def plan(srcs, lands):
        x, y, c = _coords()
        chip = 2 * x + y
        return [(srcs[0].at[src_index(j, chip, c)], lands[0].at[j], (x, y, 1 - c)) for j in range(4)]
    return plan


def _chips_plan(srcs, lands):
    x, y, c = _coords()
    return [(srcs[0].at[j - 1], lands[0].at[j - 1], (_flip(x, j & 2), _flip(y, j & 1), c)) for j in range(1, 4)]


def _peer(k):
    x, y, c = _coords()
    return _flip(x, k & 4), _flip(y, k & 2), _flip(c, k & 1)


def _send_plan(ks):
    def plan(srcs, lands):
        return [(srcs[0].at[0], lands[0].at[p], _peer(k)) for p, k in enumerate(ks)]
    return plan


def _forward_plan(moves):
    def plan(srcs, lands):
        return [(srcs[0].at[s], srcs[0].at[d], _peer(1)) for s, d in moves]
    return plan


def _gather_plan(srcs, lands):
    x, y, c = _coords()
    me = 4 * x + 2 * y + c
    return [(srcs[a], lands[a].at[me], (_flip(x, k & 4), _flip(y, k & 2), _flip(c, k & 1)))
            for a in range(len(srcs)) for k in range(1, N_DEV)]


def _pair_sum(grad, recv, owners, name):
    _, rows, cols = grad.shape
    tr = rows

    def body(own_ref, g_ref, r_ref, mine_ref, send_ref):
        j = pl.program_id(1)
        total = g_ref[...] + r_ref[...]

        @pl.when(j == 0)
        def _():
            mine_ref[...] = total

        @pl.when(j > 0)
        def _():
            send_ref[...] = total.astype(BF16)

    grid_spec = pltpu.PrefetchScalarGridSpec(
        num_scalar_prefetch=1, grid=(rows // tr, 4),
        in_specs=[pl.BlockSpec((1, tr, cols), lambda i, j, own: (own[j], i, 0)),
                  pl.BlockSpec((1, tr, cols), lambda i, j, own: (j, i, 0))],
        out_specs=[pl.BlockSpec((1, tr, cols), lambda i, j, own: (0, i, 0)),
                   pl.BlockSpec((1, tr, cols), lambda i, j, own: (jnp.maximum(j - 1, 0), i, 0))])
    return pl.pallas_call(
        body, name=name, grid_spec=grid_spec,
        out_shape=[jax.ShapeDtypeStruct((1, rows, cols), F32), jax.ShapeDtypeStruct((3, rows, cols), BF16)],
        compiler_params=_cparams(dimension_semantics=("arbitrary", "arbitrary")),
    )(owners, grad, recv)


def _modulation(c16, ada_w_sh, ada_b_sh, decay_logit):
    def body(c_ref, w_ref, b_ref, dl_ref, mod_ref, act_ref, lg_ref):
        cv = c_ref[...]
        act = cv * _sigmoid(cv)
        act_ref[...] = act
        mod_ref[...] = jnp.dot(act, w_ref[...], preferred_element_type=F32,
                               precision=lax.Precision.HIGHEST) + b_ref[...]
        z = dl_ref[...]
        lg_ref[...] = jnp.minimum(z, 0.0) - jnp.log(1.0 + jnp.exp(-jnp.abs(z)))

    return pl.pallas_call(
        body, name="modulation",
        out_shape=[jax.ShapeDtypeStruct((16, ADA_SH), F32), jax.ShapeDtypeStruct((16, D), F32),
                   jax.ShapeDtypeStruct(decay_logit.shape, F32)],
        compiler_params=_cparams(),
    )(c16, ada_w_sh, ada_b_sh, decay_logit)


def _adam_update(w, g, m, v):
    m2 = ADAM_B1 * m + (1.0 - ADAM_B1) * g
    v2 = ADAM_B2 * v + (1.0 - ADAM_B2) * (g * g)
    m_hat = m2 / (1.0 - ADAM_B1 ** ADAM_STEP)
    v_hat = v2 / (1.0 - ADAM_B2 ** ADAM_STEP)
    delta = -ADAM_LR * (m_hat / (jnp.sqrt(v_hat) + ADAM_EPS) + ADAM_WD * w)
    return delta, m2, v2


def _adam_sharded(w, m, v, own, recv, name):
    rows, cols = w.shape
    tr = _row_tile(rows)

    def body(w_ref, m_ref, v_ref, g0, g1, g2, g3, g_ref, d_ref, m_out, v_out):
        g = ((g0[0] + g1[0].astype(F32)) + g2[0].astype(F32)) + g3[0].astype(F32)
        delta, m2, v2 = _adam_update(w_ref[...], g, m_ref[...], v_ref[...])
        g_ref[...] = g
        d_ref[...] = delta
        m_out[...] = m2
        v_out[...] = v2

    flat = pl.BlockSpec((tr, cols), lambda i: (i, 0))
    part = lambda j: pl.BlockSpec((1, tr, cols), lambda i, j=j: (j, i, 0))
    return pl.pallas_call(
        body, name=name, grid=(rows // tr,),
        in_specs=[flat, flat, flat, part(0), part(0), part(1), part(2)],
        out_specs=[flat] * 4,
        out_shape=[jax.ShapeDtypeStruct((rows, cols), F32)] * 4,
        compiler_params=_cparams(dimension_semantics=("arbitrary",)),
    )(w, m, v, own, recv, recv, recv)


def _ada_backward(act_t, dmod16, dmodc8, ada_w_sh, m, v):
    def body(at_ref, dm_ref, dc_ref, w_ref, m_ref, v_ref, g_ref, d_ref, m_out, v_out, pc_ref):
        g = jnp.dot(at_ref[...], dm_ref[...], preferred_element_type=F32, precision=lax.Precision.HIGHEST)
        w = w_ref[...]
        delta, m2, v2 = _adam_update(w, g, m_ref[...], v_ref[...])
        g_ref[...] = g
        d_ref[...] = delta
        m_out[...] = m2
        v_out[...] = v2
        pc_ref[...] = lax.dot_general(dc_ref[...], w, (((1,), (1,)), ((), ())), preferred_element_type=F32,
                                      precision=lax.Precision.HIGHEST)

    return pl.pallas_call(
        body, name="ada_backward",
        out_shape=[jax.ShapeDtypeStruct((D, ADA_SH), F32)] * 4 + [jax.ShapeDtypeStruct((8, D), F32)],
        compiler_params=_cparams(),
    )(act_t, dmod16, dmodc8, ada_w_sh, m, v)


def _sum_devices(gathered, name):
    _, rows, cols = gathered.shape

    def body(g_ref, o_ref):
        acc = g_ref[0]
        for d in range(1, N_DEV):
            acc = acc + g_ref[d]
        o_ref[...] = acc

    return pl.pallas_call(body, name=name, out_shape=jax.ShapeDtypeStruct((rows, cols), F32),
                          compiler_params=_cparams())(gathered)


SMALL_PARAMS = ("c_ctx", "norm_w", "conv_b", "gn_w", "final_norm_w", "ada_b", "decay_logit", "conv_w")


def _adam_small(tot, dact_ctx, conv_w_grad, decay_grad, params):
    n = len(SMALL_PARAMS)

    def body(tot_ref, dact_ref, cwg_ref, dlg_ref, *refs):
        ins, outs = refs[:3 * n], refs[3 * n:]
        raw = {
            "c_ctx": dact_ref[0:1, :],
            "norm_w": tot_ref[1:2, :], "conv_b": tot_ref[2:3, :], "gn_w": tot_ref[3:4, :],
            "final_norm_w": tot_ref[0:1, :],
            "ada_b": jnp.concatenate([tot_ref[7 + r:8 + r, :] + tot_ref[10 + r:11 + r, :] for r in range(3)], axis=1),
            "decay_logit": dlg_ref[...],
            "conv_w": cwg_ref[...],
        }
        for k, name in enumerate(SMALL_PARAMS):
            lead = (0,) if len(ins[3 * k].shape) == 3 else ()
            w, m, v = (ref[lead + (...,)] for ref in ins[3 * k:3 * k + 3])
            g = raw[name]
            if name == "c_ctx":
                s = _sigmoid(w)
                g = g * (s * (1.0 + w * (1.0 - s)))
            elif name == "decay_logit":
                g = g * (1.0 - _sigmoid(w))
            delta, m2, v2 = _adam_update(w, g, m, v)
            for ref, val in zip(outs[4 * k:4 * k + 4], (g, delta, m2, v2)):
                ref[lead + (...,)] = val

    flat = [a for trio in params for a in trio]
    out_shape = [jax.ShapeDtypeStruct(trio[0].shape, F32) for trio in params for _ in range(4)]
    outs = pl.pallas_call(body, name="adam_small", out_shape=out_shape, compiler_params=_cparams())(
        tot, dact_ctx, conv_w_grad, decay_grad, *flat)
    return [outs[4 * k:4 * k + 4] for k in range(n)]


def _adam_square(params, own, recv):
    def body(own_ref, recv_ref, *refs):
        ins, outs = refs[:9], refs[9:]
        for k in range(3):
            rows = slice(k * RSH, (k + 1) * RSH)
            g = own_ref[0, rows, :]
            for j in range(3):
                g = g + recv_ref[j, rows, :].astype(F32)
            delta, m2, v2 = _adam_update(ins[3 * k][...], g, ins[3 * k + 1][...], ins[3 * k + 2][...])
            for ref, val in zip(outs[4 * k:4 * k + 4], (g, delta, m2, v2)):
                ref[...] = val

    flat = [a for trio in params for a in trio]
    outs = pl.pallas_call(body, name="adam_square", out_shape=[jax.ShapeDtypeStruct((RSH, D), F32)] * 12,
                          compiler_params=_cparams())(own, recv, *flat)
    return [outs[4 * k:4 * k + 4] for k in range(3)]


def _rope(t, cos, sin):
    lane = lax.broadcasted_iota(jnp.int32, (1, LANES), 1)
    first_half = jnp.bitwise_and(lane, DK // 2) == 0
    partner = jnp.where(first_half, pltpu.roll(t, LANES - DK // 2, 1), pltpu.roll(t, DK // 2, 1))
    return t * cos + partner * sin


class _Shards:
    def __init__(self, arrays, picks):
        self.arrays, self.picks = list(arrays), list(picks)

    def __add__(self, other):
        shift = len(self.arrays)
        return _Shards(self.arrays + other.arrays, self.picks + [(a + shift, s) for a, s in other.picks])


def _load_w_in(step, ids_ref, shards, w_refs, w_vmem, sems):
    @pl.when(step == 0)
    def _():
        copies = []
        for p, (a, slot) in enumerate(shards.picks):
            col = pl.multiple_of(ids_ref[p] * WSH, LANES)
            copies.append(pltpu.make_async_copy(w_refs[a].at[slot], w_vmem.at[:, pl.ds(col, WSH)], sems.at[p]))
        for cp in copies:
            cp.start()
        for cp in copies:
            cp.wait()


def _project_columns(xmb, w_ref, shard, cos, sin, p_ref):
    width = 2 * WSH
    q_lo, k_lo, k_hi = CB_QK * D, CB_QK * D + N_HEADS * DK, CB_V * D
    q_pair, k_pair = q_lo // width, k_lo // width
    assert (q_pair + 1) * width == k_lo and k_pair * width == k_lo and k_hi <= (k_pair + 1) * width
    mxu_cols = 2 * LANES

    def columns(rot_lo, rot_hi, scale):
        for a in range(0, width, mxu_cols):
            acc = _dot(xmb, w_ref[:, a:a + mxu_cols])
            for col in range(a, a + mxu_cols, LANES):
                piece = acc[:, col - a:col - a + LANES]
                if rot_lo <= col < rot_hi:
                    piece = _rope(piece if scale == 1.0 else piece * scale, cos, sin)
                p_ref[:, col:col + LANES] = piece.astype(BF16)

    @pl.when(shard == q_pair)
    def _():
        columns(q_lo - q_pair * width, width, 1.0)

    @pl.when(shard == k_pair)
    def _():
        columns(0, k_hi - k_lo, DK ** -0.5)

    @pl.when(jnp.logical_and(shard != q_pair, shard != k_pair))
    def _():
        columns(0, 0, 1.0)


_AFTER_SPEC = pl.BlockSpec(memory_space=pl.ANY)


def _prenorm(x, ctx, modx, modc, norm_w, after):
    n_x_tiles = x.shape[0] // TM
    lext = x.shape[0] + ctx.shape[0]

    assert TM % GRID_W == 0 and TM // GRID_W <= 8
    grid_rows = TM // GRID_W

    def body(x_ref, ctx_ref, mx_ref, mc_ref, nw_ref, after_ref, xm_ref, xmt_ref, cos_ref, sin_ref, col_cos, col_sin):
        del after_ref
        i = pl.program_id(0)
        is_ctx = i >= n_x_tiles
        x = jnp.where(is_ctx, ctx_ref[...], x_ref[...])
        r = lax.rsqrt(jnp.mean(x * x, axis=-1, keepdims=True) + EPS)
        sh = jnp.where(is_ctx, mc_ref[0:1, :], mx_ref[0:1, :])
        sc = jnp.where(is_ctx, mc_ref[1:2, :], mx_ref[1:2, :])
        xm = (x * r * nw_ref[...]) * (1.0 + sc) + sh
        xm_ref[...] = xm.astype(BF16)
        xmt_ref[...] = xm.T.astype(BF16)
        lane = lax.broadcasted_iota(jnp.int32, (1, LANES), 1)
        n_freq = DK // 4
        inv = jnp.exp((lane % n_freq).astype(F32) * (-math.log(ROPE_BASE) / n_freq))
        by_column = (lane % (2 * n_freq)) >= n_freq
        sign = jnp.where((lane % DK) < DK // 2, -1.0, 1.0)

        @pl.when(i == 0)
        def _():
            col = (lax.broadcasted_iota(jnp.int32, (TM, 1), 0) % GRID_W).astype(F32) * inv
            col_cos[...] = jnp.cos(col)
            col_sin[...] = jnp.sin(col) * sign

        row = (i * grid_rows + lax.broadcasted_iota(jnp.int32, (8, 1), 0)).astype(F32) * inv
        row_cos, row_sin = jnp.cos(row), jnp.sin(row) * sign
        tall = lambda t: jnp.concatenate(
            [jnp.broadcast_to(t[g:g + 1, :], (GRID_W, LANES)) for g in range(grid_rows)], axis=0)
        cos_ref[...] = jnp.where(is_ctx, 1.0, jnp.where(by_column, col_cos[...], tall(row_cos)))
        sin_ref[...] = jnp.where(is_ctx, 0.0, jnp.where(by_column, col_sin[...], tall(row_sin)))

    full = lambda a: pl.BlockSpec(a.shape, lambda i: (0,) * a.ndim)
    row = lambda width: pl.BlockSpec((TM, width), lambda i: (i, 0))
    return pl.pallas_call(
        body, name="prenorm", grid=(lext // TM,),
        in_specs=[pl.BlockSpec((TM, D), lambda i: (jnp.minimum(i, n_x_tiles - 1), 0)), full(ctx),
                  full(modx), full(modc), full(norm_w), _AFTER_SPEC],
        out_specs=[row(D), pl.BlockSpec((D, TM), lambda i: (0, i)), row(LANES), row(LANES)],
        out_shape=[jax.ShapeDtypeStruct((lext, D), BF16), jax.ShapeDtypeStruct((D, lext), BF16),
                   jax.ShapeDtypeStruct((lext, LANES), F32), jax.ShapeDtypeStruct((lext, LANES), F32)],
        scratch_shapes=[pltpu.VMEM((TM, LANES), F32), pltpu.VMEM((TM, LANES), F32)],
        compiler_params=_cparams(dimension_semantics=("arbitrary",)),
    )(x, ctx, modx, modc, norm_w, after)


def _in_projection(p_ext, xm, cos_t, sin_t, w, pairs, me, name):
    lext = xm.shape[0]
    tr = lext // 4
    meta = []
    for (slot_a, k_a), (slot_b, k_b) in pairs:
        assert k_a ^ k_b == 1
        dev_a = jnp.bitwise_xor(me, k_a)
        a_low = dev_a % 2 == 0
        meta += [jnp.where(a_low, slot_a, slot_b), jnp.where(a_low, slot_b, slot_a), dev_a // 2]
    meta = jnp.stack(meta).astype(jnp.int32)

    def body(meta_ref, xm_ref, cos_ref, sin_ref, w_hbm, *rest):
        p_ref, w_ref, sems = rest[-3:]
        j, i = pl.program_id(0), pl.program_id(1)

        @pl.when(i == 0)
        def _():
            copies = [pltpu.make_async_copy(w_hbm.at[meta_ref[3 * j + h]], w_ref.at[:, h * WSH:(h + 1) * WSH],
                                            sems.at[h]) for h in range(2)]
            for cp in copies:
                cp.start()
            for cp in copies:
                cp.wait()

        _project_columns(xm_ref[...], w_ref, meta_ref[3 * j + 2], cos_ref[...], sin_ref[...], p_ref)

    row = lambda width: pl.BlockSpec((tr, width), lambda j, i, meta_: (i, 0))
    grid_spec = pltpu.PrefetchScalarGridSpec(
        num_scalar_prefetch=1, grid=(len(pairs), lext // tr),
        in_specs=[row(D), row(LANES), row(LANES), pl.BlockSpec(memory_space=pl.ANY)]
        + ([] if p_ext is None else [pl.BlockSpec(memory_space=pl.ANY)]),
        out_specs=pl.BlockSpec((tr, 2 * WSH), lambda j, i, meta_: (i, meta_[3 * j + 2])),
        scratch_shapes=[pltpu.VMEM((D, 2 * WSH), BF16), pltpu.SemaphoreType.DMA((2,))])
    return pl.pallas_call(
        body, name=name, grid_spec=grid_spec,
        out_shape=jax.ShapeDtypeStruct((lext, PW), BF16),
        input_output_aliases={} if p_ext is None else {5: 0},
        compiler_params=_cparams(dimension_semantics=("arbitrary", "arbitrary")),
    )(meta, xm, cos_t, sin_t, w, *([] if p_ext is None else [p_ext]))


def _decay_tables(lgf, lgb, n):
    i = lax.broadcasted_iota(jnp.int32, (n, 1), 0).astype(F32)
    return dict(i=i, k_f=jnp.exp(lgf * (n - 1.0 - i)), k_b=jnp.exp(lgb * i),
                q_f=jnp.exp(lgf * (i + 1.0)), q_b=jnp.exp(lgb * (n - i)))


def _decay_matrix(lgf, lgb, n):
    ii = lax.broadcasted_iota(jnp.int32, (n, n), 0)
    jj = lax.broadcasted_iota(jnp.int32, (n, n), 1)
    diff = (ii - jj).astype(F32)
    low = jnp.exp(lgf * jnp.maximum(diff, 0.0))
    up = jnp.exp(lgb * jnp.maximum(-diff, 0.0))
    return jnp.where(diff > 0, low, jnp.where(diff < 0, up, 2.0)), diff


def _cat_lanes(a, b):
    return jnp.concatenate([a.astype(BF16), b.astype(BF16)], axis=1)


def _retention_forward(p_ext, lg, seq, ctx_len):
    lext = seq + ctx_len
    n_chunks = seq // RET_C
    C = RET_C

    def body(lg_ref, q_ref, k_ref, v_ref, rn_ref, rstd_ref, kv_scr, sf_scr, sb_scr):
        pair = pl.program_id(0)
        lane = lax.broadcasted_iota(jnp.int32, (1, LANES), 1)
        heads = range(2)
        hmask = [(lane // DK == hh).astype(F32) for hh in heads]
        lgf = [lg_ref[0, 2 * pair + hh] for hh in heads]
        lgb = [lg_ref[1, 2 * pair + hh] for hh in heads]
        vs = [slice(hh * DV, (hh + 1) * DV) for hh in heads]
        t = [_decay_tables(lgf[hh], lgb[hh], C) for hh in heads]
        kc_all = k_ref[seq:lext, :].astype(F32)
        s0 = []
        for hh in heads:
            tc = _decay_tables(lgf[hh], lgb[hh], ctx_len)
            kc = kc_all * hmask[hh]
            s0.append(_dot_tn(_cat_lanes(kc * tc["k_f"], kc * tc["k_b"]), v_ref[seq:lext, vs[hh]]))

        def increments(c, carry):
            rows = pl.ds(pl.multiple_of(c * C, C), C)
            k_all = k_ref[rows, :].astype(F32)
            for hh in heads:
                k = k_all * hmask[hh]
                kv_scr[hh, c] = _dot_tn(_cat_lanes(k * t[hh]["k_f"], k * t[hh]["k_b"]), v_ref[rows, vs[hh]])
            return carry

        lax.fori_loop(0, n_chunks, increments, 0, unroll=2)
        gf_c = [jnp.exp(lgf[hh] * C) for hh in heads]
        gb_c = [jnp.exp(lgb[hh] * C) for hh in heads]

        def scan_f(c, s):
            for hh in heads:
                sf_scr[hh, c] = s[hh]
            return tuple(gf_c[hh] * s[hh] + kv_scr[hh, c, 0:LANES, :] for hh in heads)

        def scan_b(n, s):
            c = n_chunks - 1 - n
            for hh in heads:
                sb_scr[hh, c] = s[hh]
            return tuple(gb_c[hh] * s[hh] + kv_scr[hh, c, LANES:2 * LANES, :] for hh in heads)

        lax.fori_loop(0, n_chunks, scan_f, tuple(s0[hh][0:LANES] for hh in heads))
        lax.fori_loop(0, n_chunks, scan_b, tuple(s0[hh][LANES:2 * LANES] for hh in heads))
        dmat = [_decay_matrix(lgf[hh], lgb[hh], C)[0] for hh in heads]

        def outputs(c, carry):
            rows = pl.ds(pl.multiple_of(c * C, C), C)
            q_all = q_ref[rows, :].astype(F32)
            k = k_ref[rows, :]
            for hh in heads:
                q = q_all * hmask[hh]
                v = v_ref[rows, vs[hh]]
                s = _dot_nt(q.astype(BF16), k)
                inner = _dot((s * dmat[hh]).astype(BF16), v)
                states = jnp.concatenate([sf_scr[hh, c], sb_scr[hh, c]], axis=0).astype(BF16)
                ret = inner + _dot(_cat_lanes(q * t[hh]["q_f"], q * t[hh]["q_b"]), states)
                mu = jnp.mean(ret, axis=-1, keepdims=True)
                cen = ret - mu
                rstd = lax.rsqrt(jnp.mean(cen * cen, axis=-1, keepdims=True) + EPS)
                rn_ref[rows, vs[hh]] = cen * rstd
                rstd_ref[rows, vs[hh]] = jnp.broadcast_to(rstd, (C, DV))
            return carry

        lax.fori_loop(0, n_chunks, outputs, 0, unroll=2)

    qk0 = CB_QK * D // LANES
    return pl.pallas_call(
        body, name="retention_forward", grid=(N_HEADS // 2,),
        in_specs=[pl.BlockSpec(memory_space=pltpu.SMEM),
                  pl.BlockSpec((lext, LANES), lambda g: (0, qk0 + g)),
                  pl.BlockSpec((lext, LANES), lambda g: (0, qk0 + N_HEADS // 2 + g)),
                  pl.BlockSpec((lext, 2 * DV), lambda g: (0, CB_V * D // (2 * DV) + g))],
        out_specs=[pl.BlockSpec((seq, 2 * DV), lambda g: (0, g))] * 2,
        out_shape=[jax.ShapeDtypeStruct((seq, D), F32)] * 2,
        scratch_shapes=[pltpu.VMEM((2, n_chunks, 2 * LANES, DV), F32), pltpu.VMEM((2, n_chunks, LANES, DV), F32),
                        pltpu.VMEM((2, n_chunks, LANES, DV), F32)],
        compiler_params=_cparams(dimension_semantics=("arbitrary",)),
    )(lg, p_ext, p_ext, p_ext)


def _retention_backward(p_ext, dret, lg, cos_t, sin_t, seq, ctx_len, after):
    lext = seq + ctx_len
    n_chunks = seq // RET_C
    C = RET_C

    def body(lg_ref, q_ref, k_ref, v_ref, do_ref, cos_ref, sin_ref, after_ref, dq_ref, dk_ref, dv_ref, dlg_ref,
             kv_scr, g_scr, sf_scr, sb_scr, gfn_scr, gbp_scr):
        pair = pl.program_id(0)
        lane = lax.broadcasted_iota(jnp.int32, (1, LANES), 1)
        heads = range(2)
        hmask = [(lane // DK == hh).astype(F32) for hh in heads]
        lgf = [lg_ref[0, 2 * pair + hh] for hh in heads]
        lgb = [lg_ref[1, 2 * pair + hh] for hh in heads]
        vs = [slice(hh * DV, (hh + 1) * DV) for hh in heads]
        t = [_decay_tables(lgf[hh], lgb[hh], C) for hh in heads]
        tc = [_decay_tables(lgf[hh], lgb[hh], ctx_len) for hh in heads]
        kc_all = k_ref[seq:lext, :].astype(F32)
        kc = [kc_all * hmask[hh] for hh in heads]
        vc = [v_ref[seq:lext, vs[hh]] for hh in heads]
        kc_cat = [_cat_lanes(kc[hh] * tc[hh]["k_f"], kc[hh] * tc[hh]["k_b"]) for hh in heads]
        s0 = [_dot_tn(kc_cat[hh], vc[hh]) for hh in heads]

        def increments(c, carry):
            rows = pl.ds(pl.multiple_of(c * C, C), C)
            k_all = k_ref[rows, :].astype(F32)
            q_all = q_ref[rows, :].astype(F32)
            for hh in heads:
                k, q = k_all * hmask[hh], q_all * hmask[hh]
                kv_scr[hh, c] = _dot_tn(_cat_lanes(k * t[hh]["k_f"], k * t[hh]["k_b"]), v_ref[rows, vs[hh]])
                g_scr[hh, c] = _dot_tn(_cat_lanes(q * t[hh]["q_f"], q * t[hh]["q_b"]), do_ref[rows, vs[hh]])
            return carry

        lax.fori_loop(0, n_chunks, increments, 0, unroll=2)
        gf_c = [jnp.exp(lgf[hh] * C) for hh in heads]
        gb_c = [jnp.exp(lgb[hh] * C) for hh in heads]

        def scan_f(c, s):
            for hh in heads:
                sf_scr[hh, c] = s[hh]
            return tuple(gf_c[hh] * s[hh] + kv_scr[hh, c, 0:LANES, :] for hh in heads)

        def scan_b(n, s):
            c = n_chunks - 1 - n
            for hh in heads:
                sb_scr[hh, c] = s[hh]
            return tuple(gb_c[hh] * s[hh] + kv_scr[hh, c, LANES:2 * LANES, :] for hh in heads)

        def scan_gf(n, carry):
            c = n_chunks - 1 - n
            for hh in heads:
                gfn_scr[hh, c] = carry[hh]
            return tuple(g_scr[hh, c, 0:LANES, :] + gf_c[hh] * carry[hh] for hh in heads)

        def scan_gb(c, carry):
            for hh in heads:
                gbp_scr[hh, c] = carry[hh]
            return tuple(g_scr[hh, c, LANES:2 * LANES, :] + gb_c[hh] * carry[hh] for hh in heads)

        lax.fori_loop(0, n_chunks, scan_f, tuple(s0[hh][0:LANES] for hh in heads))
        lax.fori_loop(0, n_chunks, scan_b, tuple(s0[hh][LANES:2 * LANES] for hh in heads))
        zero_state = jnp.zeros((LANES, DV), F32)
        gf_first = lax.fori_loop(0, n_chunks, scan_gf, (zero_state, zero_state))
        gb_last = lax.fori_loop(0, n_chunks, scan_gb, (zero_state, zero_state))

        dmat, w_f, w_b = [], [], []
        for hh in heads:
            dm, diff = _decay_matrix(lgf[hh], lgb[hh], C)
            dmat.append(dm)
            w_f.append(jnp.where(diff > 0, diff * dm, 0.0))
            w_b.append(jnp.where(diff < 0, -diff * dm, 0.0))

        def total(a):
            rows_, width = a.shape
            part = jnp.sum(a.reshape(rows_ // 8, 8, width), axis=0)
            return part[:, 0:LANES] + part[:, LANES:2 * LANES] if width == 2 * LANES else part

        def chunk_head(hh, c, rows, q_all, k_all, dlgf, dlgb):
            th = t[hh]
            qm = q_all * hmask[hh]
            km = k_all * hmask[hh]
            qb, kb = qm.astype(BF16), km.astype(BF16)
            v = v_ref[rows, vs[hh]]
            do = do_ref[rows, vs[hh]]
            s = _dot_nt(qb, kb)
            dsd = _dot_nt(do, v)
            ds = (dsd * dmat[hh]).astype(BF16)
            dq_in = _dot(ds, kb)
            dk_in = _dot_tn(ds, qb)
            dv_in = _dot_tn((s * dmat[hh]).astype(BF16), do)
            prod = s * dsd
            dlgf = dlgf + total(prod * w_f[hh])
            dlgb = dlgb + total(prod * w_b[hh])
            sf, sb = sf_scr[hh, c], sb_scr[hh, c]
            states = jnp.concatenate([sf, sb], axis=0).astype(BF16)
            dqc = _dot_nt(do, states)
            dqf = dqc[:, 0:LANES] * th["q_f"]
            dqb = dqc[:, LANES:2 * LANES] * th["q_b"]
            dq = (dq_in + dqf + dqb) * hmask[hh]
            dlgf = dlgf + total((th["i"] + 1.0) * (qm * dqf))
            dlgb = dlgb + total((C - th["i"]) * (qm * dqb))
            gfn, gbp = gfn_scr[hh, c], gbp_scr[hh, c]
            gstates = jnp.concatenate([gfn, gbp], axis=0).astype(BF16)
            dkc = _dot_nt(v, gstates)
            dkf = dkc[:, 0:LANES] * th["k_f"]
            dkb = dkc[:, LANES:2 * LANES] * th["k_b"]
            dk = dk_in + dkf + dkb
            dlgf = dlgf + total((C - 1.0 - th["i"]) * (km * dkf)) + C * gf_c[hh] * total(gfn * sf)
            dlgb = dlgb + total(th["i"] * (km * dkb)) + C * gb_c[hh] * total(gbp * sb)
            dv = dv_in + _dot(_cat_lanes(km * th["k_f"], km * th["k_b"]), gstates)
            dv_ref[rows, vs[hh]] = dv.astype(BF16)
            return dq, dk, dlgf, dlgb

        def chunk(c, carry):
            rows = pl.ds(pl.multiple_of(c * C, C), C)
            q_all = q_ref[rows, :].astype(F32)
            k_all = k_ref[rows, :].astype(F32)
            dq0, dk0, f0, b0 = chunk_head(0, c, rows, q_all, k_all, carry[0], carry[1])
            dq1, dk1, f1, b1 = chunk_head(1, c, rows, q_all, k_all, carry[2], carry[3])
            cos, sin = cos_ref[rows, :], sin_ref[rows, :]
            dq_ref[rows, :] = _rope(dq0 + dq1, cos, -sin).astype(BF16)
            dk_ref[rows, :] = (_rope(dk0 + dk1, cos, -sin) * (DK ** -0.5)).astype(BF16)
            return f0, b0, f1, b1

        zero = jnp.zeros((8, LANES), F32)
        sums = lax.fori_loop(0, n_chunks, chunk, (zero, zero, zero, zero), unroll=2)

        dlg = []
        dk_ctx = jnp.zeros((ctx_len, LANES), F32)
        for hh in heads:
            g0 = jnp.concatenate([gf_first[hh], gb_last[hh]], axis=0).astype(BF16)
            dkcc = _dot_nt(vc[hh], g0)
            dkcf = dkcc[:, 0:LANES] * tc[hh]["k_f"]
            dkcb = dkcc[:, LANES:2 * LANES] * tc[hh]["k_b"]
            dlgf = sums[2 * hh] + total((ctx_len - 1.0 - tc[hh]["i"]) * (kc[hh] * dkcf))
            dlgb = sums[2 * hh + 1] + total(tc[hh]["i"] * (kc[hh] * dkcb))
            dk_ctx = dk_ctx + (dkcf + dkcb) * (DK ** -0.5)
            dv_ref[seq:lext, vs[hh]] = _dot(kc_cat[hh], g0).astype(BF16)
            dlg += [jnp.sum(jnp.sum(a, axis=1, keepdims=True), axis=0, keepdims=True) for a in (dlgf, dlgb)]
        dk_ref[seq:lext, :] = dk_ctx.astype(BF16)
        dq_ref[seq:lext, :] = jnp.zeros((ctx_len, LANES), BF16)

        lane8 = lax.broadcasted_iota(jnp.int32, (8, LANES), 1)
        out = jnp.zeros((8, LANES), F32)
        for n, val in enumerate(dlg):
            out = jnp.where(lane8 == n, val, out)
        dlg_ref[0] = out

    qk0 = CB_QK * D // LANES
    q_spec = pl.BlockSpec((lext, LANES), lambda g: (0, qk0 + g))
    k_spec = pl.BlockSpec((lext, LANES), lambda g: (0, qk0 + N_HEADS // 2 + g))
    v_spec = pl.BlockSpec((lext, 2 * DV), lambda g: (0, CB_V * D // (2 * DV) + g))
    table = pl.BlockSpec((lext, LANES), lambda g: (0, 0))
    state = pltpu.VMEM((2, n_chunks, LANES, DV), F32)
    return pl.pallas_call(
        body, name="retention_backward", grid=(N_HEADS // 2,),
        in_specs=[pl.BlockSpec(memory_space=pltpu.SMEM), q_spec, k_spec, v_spec,
                  pl.BlockSpec((seq, 2 * DV), lambda g: (0, g)), table, table, _AFTER_SPEC],
        out_specs=[pl.BlockSpec((lext, LANES), lambda g: (0, g)), pl.BlockSpec((lext, LANES), lambda g: (0, g)),
                   pl.BlockSpec((lext, 2 * DV), lambda g: (0, g)), pl.BlockSpec((1, 8, LANES), lambda g: (g, 0, 0))],
        out_shape=[jax.ShapeDtypeStruct((lext, N_HEADS * DK), BF16), jax.ShapeDtypeStruct((lext, N_HEADS * DK), BF16),
                   jax.ShapeDtypeStruct((lext, D), BF16), jax.ShapeDtypeStruct((N_HEADS // 2, 8, LANES), F32)],
        scratch_shapes=[pltpu.VMEM((2, n_chunks, 2 * LANES, DV), F32), pltpu.VMEM((2, n_chunks, 2 * LANES, DV), F32),
                        state, state, state, state],
        compiler_params=_cparams(dimension_semantics=("arbitrary",)),
    )(lg, p_ext, p_ext, p_ext, dret, cos_t, sin_t, after)


class _ColumnWriter:
    def __init__(self, dst_hbm, stage, sems, pieces, n_steps):
        self.dst, self.stage, self.sems, self.pieces, self.n_steps = dst_hbm, stage, sems, pieces, n_steps

    def _copies(self, slot, tile):
        rows = pl.ds(pl.multiple_of(tile * TM, TM), TM)
        return [pltpu.make_async_copy(self.stage.at[slot, :, pl.ds(src, width)], self.dst.at[rows, pl.ds(dst, width)],
                                      self.sems.at[slot, k]) for k, (src, dst, width) in enumerate(self.pieces)]

    def slot_for(self, step):
        slot = step % 2

        @pl.when(step >= 2)
        def _():
            for cp in self._copies(slot, step - 2):
                cp.wait()

        return slot

    def send(self, step, slot):
        for cp in self._copies(slot, step):
            cp.start()

        @pl.when(step == self.n_steps - 1)
        def _():
            if self.n_steps >= 2:
                for cp in self._copies(1 - slot, step - 1):
                    cp.wait()
            for cp in self._copies(slot, step):
                cp.wait()


def _merge(p_ext, rn, rstd, x, target, w_a, w_b, w_out, vecs, seq):
    n_tiles = seq // TM
    hb = TM // HALO
    dp_pieces = [(0, CB_BG * D, D), (D, CB_ZA * D, D), (2 * D, CB_ZB * D, 3 * D)]

    def body(h_ref, bg_ref, cg_ref, za_ref, zb_ref, ga_ref, gb_ref, hp_ref, hn_ref, cp_ref, cn_ref,
             rn_ref, rstd_ref, x_ref, t_ref, wa_ref, wb_ref, wo_ref, vec_ref,
             dx1_ref, dp_hbm, dconv_ref, dret_ref, at_ref, b_ref, part_ref, stage, dp_sems):
        i = pl.program_id(0)
        writer = _ColumnWriter(dp_hbm, stage, dp_sems, dp_pieces, n_tiles)
        slot = writer.slot_for(i)
        dpa_ref = stage.at[slot]
        f = lambda ref: ref[...].astype(F32)
        h, bg, cg, za, zb, ga, gb = f(h_ref), f(bg_ref), f(cg_ref), f(za_ref), f(zb_ref), f(ga_ref), f(gb_ref)
        gx, w0, w1, w2 = vec_ref[0:1, :], vec_ref[1:2, :], vec_ref[2:3, :], vec_ref[3:4, :]
        cb, gnw, fw = vec_ref[4:5, :], vec_ref[5:6, :], vec_ref[6:7, :]
        u = cg * h
        row = lax.broadcasted_iota(jnp.int32, (TM, 1), 0)
        u_prev = (f(cp_ref) * f(hp_ref))[HALO - 1:HALO, :]
        u_next = (f(cn_ref) * f(hn_ref))[0:1, :]
        u_prev = jnp.where(i == 0, 0.0, u_prev)
        u_next = jnp.where(i == n_tiles - 1, 0.0, u_next)
        u_up = jnp.where(row == 0, u_prev, pltpu.roll(u, 1, 0))
        u_dn = jnp.where(row == TM - 1, u_next, pltpu.roll(u, TM - 1, 0))
        conv = w0 * u_up + w1 * u + w2 * u_dn + cb
        sza = _sigmoid(za)
        silu_za = za * sza
        a_act = silu_za * bg * conv
        rn = rn_ref[...]
        szb = _sigmoid(zb)
        silu_zb = zb * szb
        rg = rn * gnw
        b_act = silu_zb * rg
        y_a = _dot(a_act.astype(BF16), wa_ref[...])
        y_b = _dot(b_act.astype(BF16), wb_ref[...])
        sga, sgb = _sigmoid(ga), _sigmoid(gb)
        mix = sga * y_a + sgb * y_b
        y = _dot(mix.astype(BF16), wo_ref[...])
        x1 = x_ref[...] + gx * y
        r1 = lax.rsqrt(jnp.mean(x1 * x1, axis=-1, keepdims=True) + EPS)
        xh1 = x1 * r1
        err = xh1 * fw - t_ref[...]
        loss = jnp.sum(jnp.sum(err * err, axis=1, keepdims=True), axis=0, keepdims=True) * (0.5 / D)
        dxh = err * (fw * (1.0 / D))
        dx1 = r1 * (dxh - xh1 * jnp.mean(dxh * xh1, axis=-1, keepdims=True))
        dx1_ref[...] = dx1
        dy = (dx1 * gx).astype(BF16)
        dmix = _dot_nt(dy, wo_ref[...])
        dya_f, dyb_f = dmix * sga, dmix * sgb
        dya, dyb = dya_f.astype(BF16), dyb_f.astype(BF16)
        da = _dot_nt(dya, wa_ref[...])
        db = _dot_nt(dyb, wb_ref[...])
        da_silu = da * silu_za
        dpa_ref[:, 0:D] = (da_silu * conv).astype(BF16)
        dpa_ref[:, D:2 * D] = (da * bg * conv * (sza + silu_za * (1.0 - sza))).astype(BF16)
        dpa_ref[:, 2 * D:3 * D] = (db * rg * (szb + silu_zb * (1.0 - szb))).astype(BF16)
        dpa_ref[:, 3 * D:4 * D] = (dya_f * y_a * (1.0 - sga)).astype(BF16)
        dpa_ref[:, 4 * D:5 * D] = (dyb_f * y_b * (1.0 - sgb)).astype(BF16)
        dconv_ref[...] = (da_silu * bg).astype(BF16)
        drn_n = db * silu_zb
        drn = drn_n * gnw
        rstd = rstd_ref[...]
        for hd in range(N_HEADS):
            sl = slice(hd * DV, (hd + 1) * DV)
            dh_, rh = drn[:, sl], rn[:, sl]
            m1 = jnp.mean(dh_, axis=-1, keepdims=True)
            m2 = jnp.mean(dh_ * rh, axis=-1, keepdims=True)
            dret_ref[:, sl] = (rstd[:, sl] * (dh_ - m1 - rh * m2)).astype(BF16)
        at_ref[0] = a_act.T.astype(BF16)
        at_ref[1] = b_act.T.astype(BF16)
        at_ref[2] = mix.T.astype(BF16)
        b_ref[0] = dya
        b_ref[1] = dyb
        b_ref[2] = dy

        @pl.when(i == 0)
        def _():
            part_ref[...] = jnp.zeros((8, D), F32)

        part_ref[0:1, :] += jnp.sum(err * xh1, axis=0, keepdims=True) * (1.0 / D)
        part_ref[1:2, :] += jnp.sum(dx1 * y, axis=0, keepdims=True)
        part_ref[2:3, :] += jnp.sum(drn_n * rn, axis=0, keepdims=True)
        part_ref[3:4, :] += jnp.broadcast_to(loss, (1, D))
        writer.send(i, slot)

    col = lambda cb_: pl.BlockSpec((TM, D), lambda i, cb_=cb_: (i, cb_))
    prev = lambda cb_: pl.BlockSpec((HALO, D), lambda i, cb_=cb_: (jnp.maximum(i * hb - 1, 0), cb_))
    nxt = lambda cb_: pl.BlockSpec((HALO, D), lambda i, cb_=cb_: (jnp.minimum((i + 1) * hb, n_tiles * hb - 1), cb_))
    tile = pl.BlockSpec((TM, D), lambda i: (i, 0))
    full = lambda a: pl.BlockSpec(a.shape, lambda i: (0,) * a.ndim, pipeline_mode=pl.Buffered(1))
    return pl.pallas_call(
        body, name="merge", grid=(n_tiles,),
        in_specs=[col(CB_H), col(CB_BG), col(CB_CG), col(CB_ZA), col(CB_ZB), col(CB_GA), col(CB_GB),
                  prev(CB_H), nxt(CB_H), prev(CB_CG), nxt(CB_CG),
                  tile, tile, tile, tile, full(w_a), full(w_b), full(w_out), full(vecs)],
        out_specs=[tile, pl.BlockSpec(memory_space=pl.ANY), tile, tile,
                   pl.BlockSpec((3, D, TM), lambda i: (0, 0, i)), pl.BlockSpec((3, TM, D), lambda i: (0, i, 0)),
                   pl.BlockSpec((8, D), lambda i: (0, 0))],
        out_shape=[jax.ShapeDtypeStruct((seq, D), F32), jax.ShapeDtypeStruct((p_ext.shape[0], PW), BF16),
                   jax.ShapeDtypeStruct((seq, D), BF16), jax.ShapeDtypeStruct((seq, D), BF16),
                   jax.ShapeDtypeStruct((3, D, seq), BF16), jax.ShapeDtypeStruct((3, seq, D), BF16),
                   jax.ShapeDtypeStruct((8, D), F32)],
        scratch_shapes=[pltpu.VMEM((2, TM, 5 * D), BF16), pltpu.SemaphoreType.DMA((2, len(dp_pieces)))],
        compiler_params=_cparams(dimension_semantics=("arbitrary",)),
    )(p_ext, p_ext, p_ext, p_ext, p_ext, p_ext, p_ext, p_ext, p_ext, p_ext, p_ext,
      rn, rstd, x, target, w_a, w_b, w_out, vecs)


def _conv_backward(p_ext, dconv, dp, dq, dk, dv, vecs, seq, after):
    n_tiles = seq // TM
    hb = TM // HALO
    qk = N_HEADS * DK
    pieces = [(0, CB_H * D, D), (D, CB_CG * D, D), (2 * D, CB_QK * D, 2 * D)]
    zero_pieces = [(CB_BG * D, D), (CB_ZA * D, D), (CB_ZB * D, 3 * D)]

    def body(h_ref, cg_ref, dc_ref, dcp_ref, dcn_ref, dq_ref, dk_ref, dv_ref, vec_ref, dp_in, after_ref, dp_hbm, part_ref,
             stage, sems, zeros, zero_sems):
        del dp_in, after_ref
        i = pl.program_id(0)
        writer = _ColumnWriter(dp_hbm, stage, sems, pieces, n_tiles + 1)
        slot = writer.slot_for(i)
        out = stage.at[slot]
        out[:, 2 * D:2 * D + qk] = dq_ref[...]
        out[:, 2 * D + qk:3 * D] = dk_ref[...]
        out[:, 3 * D:4 * D] = dv_ref[...]

        @pl.when(i == 0)
        def _():
            part_ref[...] = jnp.zeros((8, D), F32)

        @pl.when(i == n_tiles)
        def _():
            out[:, 0:2 * D] = jnp.zeros((TM, 2 * D), BF16)
            zeros[...] = jnp.zeros(zeros.shape, BF16)
            rows = pl.ds(n_tiles * TM, TM)
            fills = [pltpu.make_async_copy(zeros.at[:, pl.ds(0, width)], dp_hbm.at[rows, pl.ds(dst, width)],
                                           zero_sems.at[k]) for k, (dst, width) in enumerate(zero_pieces)]
            for cp in fills:
                cp.start()
            for cp in fills:
                cp.wait()

        @pl.when(i < n_tiles)
        def _():
            f = lambda ref: ref[...].astype(F32)
            h, cg, dc = f(h_ref), f(cg_ref), f(dc_ref)
            w0, w1, w2 = vec_ref[1:2, :], vec_ref[2:3, :], vec_ref[3:4, :]
            row = lax.broadcasted_iota(jnp.int32, (TM, 1), 0)
            dc_prev = jnp.where(i == 0, 0.0, f(dcp_ref)[HALO - 1:HALO, :])
            dc_next = jnp.where(i == n_tiles - 1, 0.0, f(dcn_ref)[0:1, :])
            dc_up = jnp.where(row == 0, dc_prev, pltpu.roll(dc, 1, 0))
            dc_dn = jnp.where(row == TM - 1, dc_next, pltpu.roll(dc, TM - 1, 0))
            du = w0 * dc_dn + w1 * dc + w2 * dc_up
            u = cg * h
            part_ref[0:1, :] += jnp.sum(u * dc_dn, axis=0, keepdims=True)
            part_ref[1:2, :] += jnp.sum(u * dc, axis=0, keepdims=True)
            part_ref[2:3, :] += jnp.sum(u * dc_up, axis=0, keepdims=True)
            part_ref[3:4, :] += jnp.sum(dc, axis=0, keepdims=True)
            out[:, 0:D] = (du * cg).astype(BF16)
            out[:, D:2 * D] = (du * h).astype(BF16)

        writer.send(i, slot)

    last = n_tiles - 1
    col = lambda cb_: pl.BlockSpec((TM, D), lambda i, cb_=cb_: (jnp.minimum(i, last), cb_))
    lat = lambda w: pl.BlockSpec((TM, w), lambda i: (jnp.minimum(i, last), 0))
    ext = lambda w: pl.BlockSpec((TM, w), lambda i: (i, 0))
    anyspec = pl.BlockSpec(memory_space=pl.ANY)
    return pl.pallas_call(
        body, name="conv_backward", grid=(n_tiles + 1,),
        in_specs=[col(CB_H), col(CB_CG), lat(D),
                  pl.BlockSpec((HALO, D), lambda i: (jnp.clip(i * hb - 1, 0, n_tiles * hb - 1), 0)),
                  pl.BlockSpec((HALO, D), lambda i: (jnp.minimum((i + 1) * hb, n_tiles * hb - 1), 0)),
                  ext(qk), ext(qk), ext(D), pl.BlockSpec(vecs.shape, lambda i: (0, 0)), anyspec, _AFTER_SPEC],
        out_specs=[anyspec, pl.BlockSpec((8, D), lambda i: (0, 0))],
        out_shape=[jax.ShapeDtypeStruct(dp.shape, BF16), jax.ShapeDtypeStruct((8, D), F32)],
        input_output_aliases={9: 0},
        scratch_shapes=[pltpu.VMEM((2, TM, 4 * D), BF16), pltpu.SemaphoreType.DMA((2, len(pieces))),
                        pltpu.VMEM((TM, 3 * D), BF16), pltpu.SemaphoreType.DMA((len(zero_pieces),))],
        compiler_params=_cparams(dimension_semantics=("arbitrary",)),
    )(p_ext, p_ext, dconv, dconv, dconv, dq, dk, dv, vecs, dp, after)


def _input_backward(dp, shards, ids, x, ctx, dx1, modx, modc, norm_w, after):
    seq = x.shape[0]
    lext = seq + ctx.shape[0]
    n_x = seq // TM
    n_w = len(shards.arrays)

    def body(ids_ref, dp_ref, x_ref, ctx_ref, dx1_ref, mx_ref, mc_ref, nw_ref, *rest):
        w_hbm, (_, gx_ref, part_ref, w_ref, w_sems) = rest[:n_w], rest[n_w:]
        i = pl.program_id(0)
        _load_w_in(i, ids_ref, shards, w_hbm, w_ref, w_sems)
        is_ctx = i >= n_x
        dxm = _dot_nt(dp_ref[...], w_ref[...])
        x = jnp.where(is_ctx, ctx_ref[...], x_ref[...])
        r = lax.rsqrt(jnp.mean(x * x, axis=-1, keepdims=True) + EPS)
        xh = x * r
        nw = nw_ref[...]
        sc = jnp.where(is_ctx, mc_ref[1:2, :], mx_ref[1:2, :])
        dxn = dxm * (1.0 + sc)
        dxh = dxn * nw
        dx = r * (dxh - xh * jnp.mean(dxh * xh, axis=-1, keepdims=True))

        @pl.when(jnp.logical_not(is_ctx))
        def _():
            gx_ref[...] = dx1_ref[...] + dx

        @pl.when(i == 0)
        def _():
            part_ref[...] = jnp.zeros((8, D), F32)

        fx = jnp.where(is_ctx, 0.0, 1.0)
        d_shift = jnp.sum(dxm, axis=0, keepdims=True)
        d_scale = jnp.sum(dxm * (xh * nw), axis=0, keepdims=True)
        part_ref[0:1, :] += fx * d_shift
        part_ref[1:2, :] += fx * d_scale
        part_ref[2:3, :] += jnp.sum(dxn * xh, axis=0, keepdims=True)
        part_ref[3:4, :] += (1.0 - fx) * d_shift
        part_ref[4:5, :] += (1.0 - fx) * d_scale

    lat = lambda w: pl.BlockSpec((TM, w), lambda i, ids_: (jnp.minimum(i, n_x - 1), 0))
    ext = lambda w: pl.BlockSpec((TM, w), lambda i, ids_: (i, 0))
    full = lambda a: pl.BlockSpec(a.shape, lambda i, ids_: (0,) * a.ndim)
    grid_spec = pltpu.PrefetchScalarGridSpec(
        num_scalar_prefetch=1, grid=(lext // TM,),
        in_specs=[ext(PW), lat(D), full(ctx), lat(D), full(modx), full(modc), full(norm_w)]
        + [pl.BlockSpec(memory_space=pl.ANY)] * n_w + [_AFTER_SPEC],
        out_specs=[lat(D), pl.BlockSpec((8, D), lambda i, ids_: (0, 0))],
        scratch_shapes=[pltpu.VMEM((D, PW), BF16), pltpu.SemaphoreType.DMA((N_DEV,))])
    return pl.pallas_call(
        body, name="input_backward", grid_spec=grid_spec,
        out_shape=[jax.ShapeDtypeStruct((seq, D), F32), jax.ShapeDtypeStruct((8, D), F32)],
        compiler_params=_cparams(dimension_semantics=("arbitrary",)),
    )(ids, dp, x, ctx, dx1, modx, modc, norm_w, *shards.arrays, after)


def _weight_grad_in(xm_t, dp, owners, name, after):
    lext = xm_t.shape[1]

    def body(own_ref, a_ref, b_ref, after_ref, o_ref):
        o_ref[0] = _dot(a_ref[...], b_ref[...])

    grid_spec = pltpu.PrefetchScalarGridSpec(
        num_scalar_prefetch=1, grid=(4,),
        in_specs=[pl.BlockSpec((D, lext), lambda j, own: (0, 0)),
                  pl.BlockSpec((lext, WSH), lambda j, own: (0, own[j])), _AFTER_SPEC],
        out_specs=pl.BlockSpec((1, D, WSH), lambda j, own: (j, 0, 0)))
    return pl.pallas_call(
        body, name=name, grid_spec=grid_spec,
        out_shape=jax.ShapeDtypeStruct((4, D, WSH), F32),
        compiler_params=_cparams(dimension_semantics=("arbitrary",)),
    )(owners, xm_t, dp, after)


def _weight_grad_square(a_t, b):
    seq = a_t.shape[2]

    def body(a_ref, b_ref, o_ref):
        o_ref[:, 0] = _dot(a_ref[0], b_ref[0]).reshape(N_DEV, RSH, D)

    return pl.pallas_call(
        body, name="weight_grad_square", grid=(3,),
        in_specs=[pl.BlockSpec((1, D, seq), lambda t: (t, 0, 0)), pl.BlockSpec((1, seq, D), lambda t: (t, 0, 0))],
        out_specs=pl.BlockSpec((N_DEV, 1, RSH, D), lambda t: (0, t, 0, 0)),
        out_shape=jax.ShapeDtypeStruct((N_DEV, 3, RSH, D), F32),
        compiler_params=_cparams(dimension_semantics=("arbitrary",)),
    )(a_t, b)


def _rows(total, *parts):
    width = max(a.shape[1] for _, a in parts)
    out = None
    for row, a in parts:
        padded = jnp.pad(a, ((row, total - row - a.shape[0]), (0, width - a.shape[1])))
        out = padded if out is None else out + padded
    return out


def kernel(x, c, ctx, c_ctx, norm_w, ada_w, ada_b, w_in, conv_w, conv_b, decay_logit, gn_w, w_a, w_b, w_out, final_norm_w, loss_target, m_c_ctx, m_norm_w, m_ada_w, m_ada_b, m_w_in, m_conv_w, m_conv_b, m_decay_logit, m_gn_w, m_w_a, m_w_b, m_w_out, m_final_norm_w, v_c_ctx, v_norm_w, v_ada_w, v_ada_b, v_w_in, v_conv_w, v_conv_b, v_decay_logit, v_gn_w, v_w_a, v_w_b, v_w_out, v_final_norm_w):
    xi, yi, ci = _coords()
    me = 4 * xi + 2 * yi + ci
    chip = 2 * xi + yi
    seq, ctx_len = x.shape[1], ctx.shape[1]
    assert seq % TM == 0 and seq % RET_C == 0 and ctx_len == TM and seq % GRID_W == 0
    csh = D // N_DEV

    blk = jnp.pad(c, ((0, 7), (0, 0))) + jnp.pad(conv_w[0], ((1, 4), (0, D - csh)))
    got, _ = _all_gather_small(blk, "gather_cond")
    conv_w_all = got[:, 1:4, 0:csh].transpose(1, 0, 2).reshape(3, D)
    c16 = _rows(16, (0, got[:, 0, :]), (N_DEV, c_ctx[None]))
    ada_b_sh = lax.dynamic_slice(ada_b, (0, me * ADA_SH), (1, ADA_SH))
    mod_sh, act16, lg = _modulation(c16, ada_w[0], ada_b_sh, decay_logit[0])
    mod_all, small_done = _all_gather_small(mod_sh, "gather_mod")
    mod_all = mod_all.transpose(1, 0, 2).reshape(16, 3 * D)
    modx = lax.dynamic_slice(mod_all, (me, 0), (1, 3 * D)).reshape(3, D)
    modc = mod_all[8].reshape(3, D)

    x2, tgt = x[0], loss_target[0]
    wblock = lambda n: jax.ShapeDtypeStruct((n, D, WSH), BF16)
    ids_of = lambda ks: jnp.stack([jnp.bitwise_xor(me, k) for k in ks]).astype(jnp.int32)
    own_shard = w_in[0].astype(BF16)[None]
    sq_shards = [w[0].astype(BF16) for w in (w_a, w_b, w_out)]
    wave1 = _Exchange(
        "gather_in_first", [jnp.broadcast_to(own_shard, (2, D, WSH)), own_shard], [wblock(4)], 3,
        lambda srcs, lands: _forward_plan([(0, 1)])(srcs, lands) + _send_plan((4, 2))(srcs[1:], lands),
        after=small_done)
    xm, xm_t, cos_t, sin_t = _prenorm(x2, ctx[0], modx, modc, norm_w, wave1.token)
    (w_pair, _), _ = wave1.wait(xm_t, 1, through=[0])
    first = _Shards([w_pair], [(0, 0), (0, 1)])
    p_ext = _in_projection(None, xm, cos_t, sin_t, w_pair, [((0, 0), (1, 1))], me, "in_projection_pair")
    _, (w_nbr,) = wave1.wait(p_ext, through=[1, 2])
    wave2 = _Exchange(
        "gather_in_second", [w_nbr, own_shard], [wblock(2)], 3,
        lambda srcs, lands: _forward_plan([(0, 2), (1, 3)])(srcs, lands) + _send_plan((6,))(srcs[1:], lands))
    (w_nbr, _), _ = wave2.wait(wave2.token, 2, through=[0])
    second = _Shards([w_nbr], [(0, 0), (0, 1), (0, 2), (0, 3)])
    p_ext = _in_projection(p_ext, xm, cos_t, sin_t, w_nbr, [((0, 4), (2, 5)), ((1, 2), (3, 3))], me,
                           "in_projection_neighbours")
    _, (w_diag,) = wave2.wait(p_ext, through=[1, 2])
    wave3 = _Exchange(
        "gather_in_third", [w_diag] + sq_shards, [jax.ShapeDtypeStruct((N_DEV, RSH, D), BF16)] * 3, 1 + 3 * (N_DEV - 1),
        lambda srcs, lands: _forward_plan([(0, 1)])(srcs, lands) + _gather_plan(srcs[1:], lands))
    (w_diag, *_), _ = wave3.wait(wave3.token, 1, through=[0])
    third = _Shards([w_diag], [(0, 0), (0, 1)])
    p_ext = _in_projection(p_ext, xm, cos_t, sin_t, w_diag, [((0, 6), (1, 7))], me, "in_projection_diagonal")
    w_shards, w_ids = first + second + third, ids_of((0, 1, 4, 2, 5, 3, 6, 7))

    rn, rstd = _retention_forward(p_ext, lg, seq, ctx_len)
    _, sq_lands = wave3.wait(rstd, through=range(1, 7))
    w_a_all, w_b_all, w_out_all = (
        lax.dynamic_update_slice(land, shard[None], (me, 0, 0)).reshape(D, D) for land, shard in zip(sq_lands, sq_shards))
    vecs = _rows(8, (0, modx[2:3]), (1, conv_w_all), (4, conv_b), (5, gn_w), (6, final_norm_w[None]))
    dx1, dp, dconv, dret, op_at, op_b, part_m = _merge(p_ext, rn, rstd, x2, tgt, w_a_all, w_b_all, w_out_all, vecs, seq)

    j4 = jnp.arange(4, dtype=jnp.int32)
    owners = (2 * jnp.bitwise_xor(chip, j4) + ci).astype(jnp.int32)
    owners_sib = (2 * jnp.bitwise_xor(chip, j4) + (1 - ci)).astype(jnp.int32)
    gw_sq = _weight_grad_square(op_at, op_b).reshape(N_DEV, 3 * RSH, D)
    rs_sq_pair = _Exchange("rs_square_pair", [gw_sq], [jax.ShapeDtypeStruct((4, 3 * RSH, D), F32)], 4,
                           _pair_plan(lambda j, chip_, c_: 2 * jnp.bitwise_xor(chip_, j) + (1 - c_)))
    dq, dk, dv, dlg = _retention_backward(p_ext, dret, lg, cos_t, sin_t, seq, ctx_len, rs_sq_pair.token)
    (gw_sq,), (r1_sq,) = rs_sq_pair.wait(dlg)
    own_sq, send_sq = _pair_sum(gw_sq, r1_sq, owners, "pair_sum_square")
    rs_sq_chips = _Exchange("rs_square_chips", [send_sq], [jax.ShapeDtypeStruct((3, 3 * RSH, D), BF16)], 3, _chips_plan)
    dp, part_c = _conv_backward(p_ext, dconv, dp, dq, dk, dv, vecs, seq, rs_sq_chips.token)
    gw_sib = _weight_grad_in(xm_t, dp, owners_sib, "weight_grad_in_sibling", part_c)
    rs_in_pair = _Exchange("rs_in_pair", [gw_sib], [jax.ShapeDtypeStruct((4, D, WSH), F32)], 4,
                           _pair_plan(lambda j, chip_, c_: j))
    gw_own = _weight_grad_in(xm_t, dp, owners, "weight_grad_in_own", rs_in_pair.token)
    _, (r1_in,) = rs_in_pair.wait(gw_own)
    own_in, send_in = _pair_sum(gw_own, r1_in, j4, "pair_sum_in")
    rs_in_chips = _Exchange("rs_in_chips", [send_in], [jax.ShapeDtypeStruct((3, D, WSH), BF16)], 3, _chips_plan)
    grad_x, part_i = _input_backward(dp, w_shards, w_ids, x2, ctx[0], dx1, modx, modc, norm_w, rs_in_chips.token)

    dl = dlg[:, 0, 0:4]
    dlg_row = jnp.pad(dl[:, 0::2].reshape(1, N_HEADS), ((0, 0), (0, D - N_HEADS))) + jnp.pad(
        dl[:, 1::2].reshape(1, N_HEADS), ((0, 0), (N_HEADS, D - 2 * N_HEADS)))
    partials = _rows(16, (0, part_m[0:1]), (1, part_i[2:3]), (2, part_c[3:4]), (3, part_m[2:3]), (4, part_c[0:3]),
                     (7, part_i[0:2]), (9, part_m[1:2]), (10, part_i[3:5]), (13, part_m[3:4]), (14, dlg_row))
    results = {}
    gather_part = _Exchange("gather_partials", [partials], [jax.ShapeDtypeStruct((N_DEV, 16, D), F32)],
                            N_DEV - 1, _gather_plan)
    _, (r2_in,) = rs_in_chips.wait(gather_part.token)
    results["w_in"] = [o[None] for o in _adam_sharded(w_in[0], m_w_in[0], v_w_in[0], own_in, r2_in, "adam_w_in")]
    _, (got,) = gather_part.wait(results["w_in"][1])
    got = lax.dynamic_update_slice(got, partials[None], (me, 0, 0))
    tot = _sum_devices(got, "sum_partials")
    dmodc = tot[10:13].reshape(1, 3 * D)
    dmod16 = _rows(16, (0, got[:, 7:10, :].reshape(N_DEV, 3 * D)), (N_DEV, dmodc))
    dmod16 = lax.dynamic_slice(dmod16, (0, me * ADA_SH), (16, ADA_SH))
    dmodc8 = _rows(8, (0, dmod16[8:9]))
    g_ada_w, d_ada_w, nm_ada_w, nv_ada_w, cctx_part = _ada_backward(
        act16.T, dmod16, dmodc8, ada_w[0], m_ada_w[0], v_ada_w[0])
    gather_cctx = _Exchange("gather_cctx", [cctx_part], [jax.ShapeDtypeStruct((N_DEV, 8, D), F32)],
                            N_DEV - 1, _gather_plan)
    _, (r2_sq,) = rs_sq_chips.wait(gather_cctx.token)
    square = _adam_square([(w_a[0], m_w_a[0], v_w_a[0]), (w_b[0], m_w_b[0], v_w_b[0]),
                           (w_out[0], m_w_out[0], v_w_out[0])], own_sq, r2_sq)
    _, (got_cctx,) = gather_cctx.wait(square[0][1])
    dact_ctx = _sum_devices(lax.dynamic_update_slice(got_cctx, cctx_part[None], (me, 0, 0)), "sum_cctx")

    view = {"c_ctx": (1, D), "final_norm_w": (1, D)}
    given = {"c_ctx": (c_ctx, m_c_ctx, v_c_ctx), "norm_w": (norm_w, m_norm_w, v_norm_w),
             "conv_b": (conv_b, m_conv_b, v_conv_b), "gn_w": (gn_w, m_gn_w, v_gn_w),
             "final_norm_w": (final_norm_w, m_final_norm_w, v_final_norm_w), "ada_b": (ada_b, m_ada_b, v_ada_b),
             "decay_logit": (decay_logit, m_decay_logit, v_decay_logit), "conv_w": (conv_w, m_conv_w, v_conv_w)}
    small_in = [tuple(a.reshape(view.get(name, a.shape)) for a in given[name]) for name in SMALL_PARAMS]
    conv_w_grad = lax.dynamic_slice(tot, (4, me * csh), (3, csh))
    small_out = _adam_small(tot, dact_ctx, conv_w_grad, tot[14, 0:2 * N_HEADS].reshape(2, N_HEADS), small_in)
    results.update({name: [o.reshape(given[name][0].shape) for o in outs_]
                    for name, outs_ in zip(SMALL_PARAMS, small_out)})
    for name, outs_ in zip(("w_a", "w_b", "w_out"), square):
        results[name] = [o[None] for o in outs_]
    results["ada_w"] = [o[None] for o in (g_ada_w, d_ada_w, nm_ada_w, nv_ada_w)]

    order = ("c_ctx", "norm_w", "ada_w", "ada_b", "w_in", "conv_w", "conv_b", "decay_logit", "gn_w",
             "w_a", "w_b", "w_out", "final_norm_w")
    outs = [results[name][kind] for kind in range(4) for name in order]
    return (tot[13, 0], grad_x[None], *outs)
```

```python
import math

import jax
import jax.numpy as jnp
from jax import lax
from jax.experimental import pallas as pl
from jax.experimental.pallas import tpu as pltpu

F32 = jnp.float32
BF16 = jnp.bfloat16
MESH = pl.DeviceIdType.MESH
_HBM_SPEC = pl.BlockSpec(memory_space=pltpu.HBM)
_SEM_SPEC = pl.BlockSpec(memory_space=pltpu.SEMAPHORE)
_EFFECT = pltpu.SideEffectType.DATAFLOW_SIDE_EFFECTING

N_DEV = 8
D = 1024
N_HEADS = 8
DK = 64
DV = 128
GRID_W = 64
ROPE_BASE = 10000.0
EPS = 1e-6
PW = 9 * D
WSH = PW // N_DEV
RSH = D // N_DEV
ADA_SH = 3 * D // N_DEV
TM = 256
RET_C = 256
HALO = 16
LANES = 128
VMEM_LIMIT = 60 * 1024 * 1024

ADAM_LR = 0.001
ADAM_B1 = 0.9
ADAM_B2 = 0.999
ADAM_EPS = 1e-08
ADAM_WD = 0.01
ADAM_STEP = 10

CB_H, CB_BG, CB_CG, CB_ZA, CB_QK, CB_V, CB_ZB, CB_GA, CB_GB = range(9)


def _cparams(**kw):
    return pltpu.CompilerParams(vmem_limit_bytes=VMEM_LIMIT, **kw)


def _dot(a, b):
    return jnp.dot(a, b, preferred_element_type=F32)


def _dot_nt(a, b):
    return lax.dot_general(a, b, (((1,), (1,)), ((), ())), preferred_element_type=F32)


def _dot_tn(a, b):
    return lax.dot_general(a, b, (((0,), (0,)), ((), ())), preferred_element_type=F32)


def _sigmoid(z):
    return 0.5 * jnp.tanh(0.5 * z) + 0.5


def _row_tile(rows):
    return TM if rows % TM == 0 else rows


def _coords():
    return lax.axis_index("x"), lax.axis_index("y"), lax.axis_index("c")


def _flip(v, bit):
    return 1 - v if bit else v


def _all_gather_small(blk, name):
    rows, cols = blk.shape

    def body(x_ref, out_ref, done, send_sems, recv_sems, local_sem):
        x, y, c = _coords()
        me = 4 * x + 2 * y + c
        mine = pltpu.make_async_copy(x_ref, out_ref.at[me], local_sem)
        mine.start()

        def copy(k, slot):
            peer = (_flip(x, k & 4), _flip(y, k & 2), _flip(c, k & 1))
            return pltpu.make_async_remote_copy(
                src_ref=x_ref, dst_ref=out_ref.at[slot], send_sem=send_sems.at[k - 1],
                recv_sem=recv_sems.at[k - 1], device_id=peer, device_id_type=MESH)

        for k in range(1, N_DEV):
            copy(k, me).start()
        for k in range(1, N_DEV):
            copy(k, jnp.bitwise_xor(me, k)).wait_recv()
        for k in range(1, N_DEV):
            copy(k, me).wait_send()
        mine.wait()
        done[...] = jnp.zeros((8, LANES), F32)

    vmem = pl.BlockSpec(memory_space=pltpu.VMEM)
    return pl.pallas_call(
        body, name=name,
        out_shape=[jax.ShapeDtypeStruct((N_DEV, rows, cols), blk.dtype), jax.ShapeDtypeStruct((8, LANES), F32)],
        in_specs=[vmem], out_specs=[vmem, vmem],
        scratch_shapes=[pltpu.SemaphoreType.DMA((N_DEV - 1,)), pltpu.SemaphoreType.DMA((N_DEV - 1,)),
                        pltpu.SemaphoreType.DMA],
    )(blk)


class _Absent:
    at = property(lambda self: self)

    def __getitem__(self, index):
        return self


class _Exchange:
    def __init__(self, name, srcs, land_shapes, n_copies, plan, after=None):
        self.name, self.plan, self.n_copies = name, plan, n_copies
        self.n_src, self.n_land = len(srcs), len(land_shapes)
        hbm = lambda a: pltpu.HBM(a.shape, a.dtype)
        n = self.n_src + self.n_land
        lands = [pltpu.with_memory_space_constraint(lax.empty(s.shape, s.dtype), pltpu.HBM) for s in land_shapes]
        srcs = [pltpu.with_memory_space_constraint(s, pltpu.HBM) for s in srcs]
        extra = [] if after is None else [after]

        def body(*refs):
            send_sems, recv_sems = refs[n + len(extra)], refs[n + len(extra) + 1]
            for cp in self._copies(refs, send_sems, recv_sems):
                cp.start()
            refs[-1][...] = jnp.zeros((8, LANES), F32)

        outs = pl.pallas_call(
            body, name=name + "_start",
            out_shape=(pltpu.SemaphoreType.DMA((n_copies,)), pltpu.SemaphoreType.DMA((n_copies,)),
                       *[hbm(a) for a in srcs], *[hbm(a) for a in lands], jax.ShapeDtypeStruct((8, LANES), F32)),
            in_specs=[_HBM_SPEC] * n + [pl.BlockSpec(memory_space=pl.ANY)] * len(extra),
            out_specs=(_SEM_SPEC, _SEM_SPEC, *[_HBM_SPEC] * n, pl.BlockSpec(memory_space=pltpu.VMEM)),
            input_output_aliases={i: 2 + i for i in range(n)},
            compiler_params=pltpu.CompilerParams(has_side_effects=_EFFECT),
        )(*srcs, *lands, *extra)
        self.send_sems, self.recv_sems = outs[0], outs[1]
        self.buffers = list(outs[2:2 + n])
        self.token = outs[-1]
        self.waited = 0

    def _copies(self, refs, send_sems, recv_sems, lo=0, hi=None):
        src_refs, land_refs = refs[:self.n_src], refs[self.n_src:self.n_src + self.n_land]
        planned = list(enumerate(self.plan(src_refs, land_refs)))[lo:hi]
        return [pltpu.make_async_remote_copy(src_ref=s, dst_ref=d, send_sem=send_sems.at[k], recv_sem=recv_sems.at[k],
                                             device_id=dev, device_id_type=MESH) for k, (s, d, dev) in planned]

    def wait(self, after, count=None, through=None):
        n = self.n_src + self.n_land
        lo = self.waited
        hi = self.n_copies if count is None else lo + count
        self.waited = hi
        through = list(range(n)) if through is None else list(through)
        m = len(through)

        def body(*refs):
            held = [_Absent()] * n
            for pos, ref in zip(through, refs[:m]):
                held[pos] = ref
            for cp in self._copies(held, refs[m], refs[m + 1], lo, hi):
                cp.wait_send()
                cp.wait_recv()

        outs = pl.pallas_call(
            body, name=f"{self.name}_wait{lo}" if lo or hi < self.n_copies else self.name + "_wait",
            out_shape=tuple(pltpu.HBM(self.buffers[p].shape, self.buffers[p].dtype) for p in through),
            in_specs=[_HBM_SPEC] * m + [_SEM_SPEC, _SEM_SPEC, pl.BlockSpec(memory_space=pl.ANY)],
            out_specs=tuple([_HBM_SPEC] * m),
            input_output_aliases={i: i for i in range(m)},
            compiler_params=pltpu.CompilerParams(has_side_effects=_EFFECT),
        )(*[self.buffers[p] for p in through], self.send_sems, self.recv_sems, after)
        for pos, out in zip(through, outs):
            self.buffers[pos] = out
        return list(self.buffers[:self.n_src]), list(self.buffers[self.n_src:])


def _pair_plan(src_index):
    def plan(srcs, lands):
        x, y, c = _coords()
        chip = 2 * x + y
        return [(srcs[0].at[src_index(j, chip, c)], lands[0].at[j], (x, y, 1 - c)) for j in range(4)]
    return plan


def _chips_plan(srcs, lands):
    x, y, c = _coords()
    return [(srcs[0].at[j - 1], lands[0].at[j - 1], (_flip(x, j & 2), _flip(y, j & 1), c)) for j in range(1, 4)]


def _peer(k):
    x, y, c = _coords()
    return _flip(x, k & 4), _flip(y, k & 2), _flip(c, k & 1)


def _send_plan(ks):
    def plan(srcs, lands):
        return [(srcs[0].at[0], lands[0].at[p], _peer(k)) for p, k in enumerate(ks)]
    return plan


def _forward_plan(moves):
    def plan(srcs, lands):
        return [(srcs[0].at[s], srcs[0].at[d], _peer(1)) for s, d in moves]
    return plan


def _gather_plan(srcs, lands):
    x, y, c = _coords()
    me = 4 * x + 2 * y + c
    return [(srcs[a], lands[a].at[me], (_flip(x, k & 4), _flip(y, k & 2), _flip(c, k & 1)))
            for a in range(len(srcs)) for k in range(1, N_DEV)]


def _pair_sum(grad, recv, owners, name):
    _, rows, cols = grad.shape
    tr = rows

    def body(own_ref, g_ref, r_ref, mine_ref, send_ref):
        j = pl.program_id(1)
        total = g_ref[...] + r_ref[...]

        @pl.when(j == 0)
        def _():
            mine_ref[...] = total

        @pl.when(j > 0)
        def _():
            send_ref[...] = total.astype(BF16)

    grid_spec = pltpu.PrefetchScalarGridSpec(
        num_scalar_prefetch=1, grid=(rows // tr, 4),
        in_specs=[pl.BlockSpec((1, tr, cols), lambda i, j, own: (own[j], i, 0)),
                  pl.BlockSpec((1, tr, cols), lambda i, j, own: (j, i, 0))],
        out_specs=[pl.BlockSpec((1, tr, cols), lambda i, j, own: (0, i, 0)),
                   pl.BlockSpec((1, tr, cols), lambda i, j, own: (jnp.maximum(j - 1, 0), i, 0))])
    return pl.pallas_call(
        body, name=name, grid_spec=grid_spec,
        out_shape=[jax.ShapeDtypeStruct((1, rows, cols), F32), jax.ShapeDtypeStruct((3, rows, cols), BF16)],
        compiler_params=_cparams(dimension_semantics=("arbitrary", "arbitrary")),
    )(owners, grad, recv)


def _modulation(c16, ada_w_sh, ada_b_sh, decay_logit):
    def body(c_ref, w_ref, b_ref, dl_ref, mod_ref, act_ref, lg_ref):
        cv = c_ref[...]
        act = cv * _sigmoid(cv)
        act_ref[...] = act
        mod_ref[...] = jnp.dot(act, w_ref[...], preferred_element_type=F32,
                               precision=lax.Precision.HIGHEST) + b_ref[...]
        z = dl_ref[...]
        lg_ref[...] = jnp.minimum(z, 0.0) - jnp.log(1.0 + jnp.exp(-jnp.abs(z)))

    return pl.pallas_call(
        body, name="modulation",
        out_shape=[jax.ShapeDtypeStruct((16, ADA_SH), F32), jax.ShapeDtypeStruct((16, D), F32),
                   jax.ShapeDtypeStruct(decay_logit.shape, F32)],
        compiler_params=_cparams(),
    )(c16, ada_w_sh, ada_b_sh, decay_logit)


def _adam_update(w, g, m, v):
    m2 = ADAM_B1 * m + (1.0 - ADAM_B1) * g
    v2 = ADAM_B2 * v + (1.0 - ADAM_B2) * (g * g)
    m_hat = m2 / (1.0 - ADAM_B1 ** ADAM_STEP)
    v_hat = v2 / (1.0 - ADAM_B2 ** ADAM_STEP)
    delta = -ADAM_LR * (m_hat / (jnp.sqrt(v_hat) + ADAM_EPS) + ADAM_WD * w)
    return delta, m2, v2


def _adam_sharded(w, m, v, own, recv, name):
    rows, cols = w.shape
    tr = _row_tile(rows)

    def body(w_ref, m_ref, v_ref, g0, g1, g2, g3, g_ref, d_ref, m_out, v_out):
        g = ((g0[0] + g1[0].astype(F32)) + g2[0].astype(F32)) + g3[0].astype(F32)
        delta, m2, v2 = _adam_update(w_ref[...], g, m_ref[...], v_ref[...])
        g_ref[...] = g
        d_ref[...] = delta
        m_out[...] = m2
        v_out[...] = v2

    flat = pl.BlockSpec((tr, cols), lambda i: (i, 0))
    part = lambda j: pl.BlockSpec((1, tr, cols), lambda i, j=j: (j, i, 0))
    return pl.pallas_call(
        body, name=name, grid=(rows // tr,),
        in_specs=[flat, flat, flat, part(0), part(0), part(1), part(2)],
        out_specs=[flat] * 4,
        out_shape=[jax.ShapeDtypeStruct((rows, cols), F32)] * 4,
        compiler_params=_cparams(dimension_semantics=("arbitrary",)),
    )(w, m, v, own, recv, recv, recv)


def _ada_backward(act_t, dmod16, dmodc8, ada_w_sh, m, v):
    def body(at_ref, dm_ref, dc_ref, w_ref, m_ref, v_ref, g_ref, d_ref, m_out, v_out, pc_ref):
        g = jnp.dot(at_ref[...], dm_ref[...], preferred_element_type=F32, precision=lax.Precision.HIGHEST)
        w = w_ref[...]
        delta, m2, v2 = _adam_update(w, g, m_ref[...], v_ref[...])
        g_ref[...] = g
        d_ref[...] = delta
        m_out[...] = m2
        v_out[...] = v2
        pc_ref[...] = lax.dot_general(dc_ref[...], w, (((1,), (1,)), ((), ())), preferred_element_type=F32,
                                      precision=lax.Precision.HIGHEST)

    return pl.pallas_call(
        body, name="ada_backward",
        out_shape=[jax.ShapeDtypeStruct((D, ADA_SH), F32)] * 4 + [jax.ShapeDtypeStruct((8, D), F32)],
        compiler_params=_cparams(),
    )(act_t, dmod16, dmodc8, ada_w_sh, m, v)


def _sum_devices(gathered, name):
    _, rows, cols = gathered.shape

    def body(g_ref, o_ref):
        acc = g_ref[0]
        for d in range(1, N_DEV):
            acc = acc + g_ref[d]
        o_ref[...] = acc

    return pl.pallas_call(body, name=name, out_shape=jax.ShapeDtypeStruct((rows, cols), F32),
                          compiler_params=_cparams())(gathered)


SMALL_PARAMS = ("c_ctx", "norm_w", "conv_b", "gn_w", "final_norm_w", "ada_b", "decay_logit", "conv_w")


def _adam_small(tot, dact_ctx, conv_w_grad, decay_grad, params):
    n = len(SMALL_PARAMS)

    def body(tot_ref, dact_ref, cwg_ref, dlg_ref, *refs):
        ins, outs = refs[:3 * n], refs[3 * n:]
        raw = {
            "c_ctx": dact_ref[0:1, :],
            "norm_w": tot_ref[1:2, :], "conv_b": tot_ref[2:3, :], "gn_w": tot_ref[3:4, :],
            "final_norm_w": tot_ref[0:1, :],
            "ada_b": jnp.concatenate([tot_ref[7 + r:8 + r, :] + tot_ref[10 + r:11 + r, :] for r in range(3)], axis=1),
            "decay_logit": dlg_ref[...],
            "conv_w": cwg_ref[...],
        }
        for k, name in enumerate(SMALL_PARAMS):
            lead = (0,) if len(ins[3 * k].shape) == 3 else ()
            w, m, v = (ref[lead + (...,)] for ref in ins[3 * k:3 * k + 3])
            g = raw[name]
            if name == "c_ctx":
                s = _sigmoid(w)
                g = g * (s * (1.0 + w * (1.0 - s)))
            elif name == "decay_logit":
                g = g * (1.0 - _sigmoid(w))
            delta, m2, v2 = _adam_update(w, g, m, v)
            for ref, val in zip(outs[4 * k:4 * k + 4], (g, delta, m2, v2)):
                ref[lead + (...,)] = val

    flat = [a for trio in params for a in trio]
    out_shape = [jax.ShapeDtypeStruct(trio[0].shape, F32) for trio in params for _ in range(4)]
    outs = pl.pallas_call(body, name="adam_small", out_shape=out_shape, compiler_params=_cparams())(
        tot, dact_ctx, conv_w_grad, decay_grad, *flat)
    return [outs[4 * k:4 * k + 4] for k in range(n)]


def _adam_square(params, own, recv):
    def body(own_ref, recv_ref, *refs):
        ins, outs = refs[:9], refs[9:]
        for k in range(3):
            rows = slice(k * RSH, (k + 1) * RSH)
            g = own_ref[0, rows, :]
            for j in range(3):
                g = g + recv_ref[j, rows, :].astype(F32)
            delta, m2, v2 = _adam_update(ins[3 * k][...], g, ins[3 * k + 1][...], ins[3 * k + 2][...])
            for ref, val in zip(outs[4 * k:4 * k + 4], (g, delta, m2, v2)):
                ref[...] = val

    flat = [a for trio in params for a in trio]
    outs = pl.pallas_call(body, name="adam_square", out_shape=[jax.ShapeDtypeStruct((RSH, D), F32)] * 12,
                          compiler_params=_cparams())(own, recv, *flat)
    return [outs[4 * k:4 * k + 4] for k in range(3)]


def _rope(t, cos, sin):
    lane = lax.broadcasted_iota(jnp.int32, (1, LANES), 1)
    first_half = jnp.bitwise_and(lane, DK // 2) == 0
    partner = jnp.where(first_half, pltpu.roll(t, LANES - DK // 2, 1), pltpu.roll(t, DK // 2, 1))
    return t * cos + partner * sin


class _Shards:
    def __init__(self, arrays, picks):
        self.arrays, self.picks = list(arrays), list(picks)

    def __add__(self, other):
        shift = len(self.arrays)
        return _Shards(self.arrays + other.arrays, self.picks + [(a + shift, s) for a, s in other.picks])


def _load_w_in(step, ids_ref, shards, w_refs, w_vmem, sems):
    @pl.when(step == 0)
    def _():
        copies = []
        for p, (a, slot) in enumerate(shards.picks):
            col = pl.multiple_of(ids_ref[p] * WSH, LANES)
            copies.append(pltpu.make_async_copy(w_refs[a].at[slot], w_vmem.at[:, pl.ds(col, WSH)], sems.at[p]))
        for cp in copies:
            cp.start()
        for cp in copies:
            cp.wait()


def _project_columns(xmb, w_ref, shard, cos, sin, p_ref):
    width = 2 * WSH
    q_lo, k_lo, k_hi = CB_QK * D, CB_QK * D + N_HEADS * DK, CB_V * D
    q_pair, k_pair = q_lo // width, k_lo // width
    assert (q_pair + 1) * width == k_lo and k_pair * width == k_lo and k_hi <= (k_pair + 1) * width
    mxu_cols = 2 * LANES

    def columns(rot_lo, rot_hi, scale):
        for a in range(0, width, mxu_cols):
            acc = _dot(xmb, w_ref[:, a:a + mxu_cols])
            for col in range(a, a + mxu_cols, LANES):
                piece = acc[:, col - a:col - a + LANES]
                if rot_lo <= col < rot_hi:
                    piece = _rope(piece if scale == 1.0 else piece * scale, cos, sin)
                p_ref[:, col:col + LANES] = piece.astype(BF16)

    @pl.when(shard == q_pair)
    def _():
        columns(q_lo - q_pair * width, width, 1.0)

    @pl.when(shard == k_pair)
    def _():
        columns(0, k_hi - k_lo, DK ** -0.5)

    @pl.when(jnp.logical_and(shard != q_pair, shard != k_pair))
    def _():
        columns(0, 0, 1.0)


_AFTER_SPEC = pl.BlockSpec(memory_space=pl.ANY)


def _prenorm(x, ctx, modx, modc, norm_w, after):
    tp = 2 * TM
    seq, ctx_len = x.shape[0], ctx.shape[0]
    n_x_tiles = seq // tp
    lext = seq + ctx_len
    assert seq % tp == 0 and ctx_len <= tp and tp % GRID_W == 0 and tp // GRID_W <= 8
    grid_rows = tp // GRID_W

    def body(x_ref, ctx_ref, mx_ref, mc_ref, nw_ref, after_ref, xm_ref, xmt_ref, cos_ref, sin_ref, col_cos, col_sin):
        del after_ref
        i = pl.program_id(0)
        is_ctx = i >= n_x_tiles
        ctx_rows = jnp.concatenate([ctx_ref[...], jnp.zeros((tp - ctx_len, D), F32)], axis=0)
        x = jnp.where(is_ctx, ctx_rows, x_ref[...])
        r = lax.rsqrt(jnp.mean(x * x, axis=-1, keepdims=True) + EPS)
        sh = jnp.where(is_ctx, mc_ref[0:1, :], mx_ref[0:1, :])
        sc = jnp.where(is_ctx, mc_ref[1:2, :], mx_ref[1:2, :])
        xm = (x * r * nw_ref[...]) * (1.0 + sc) + sh
        xm_ref[...] = xm.astype(BF16)
        xmt_ref[...] = xm.T.astype(BF16)
        lane = lax.broadcasted_iota(jnp.int32, (1, LANES), 1)
        n_freq = DK // 4
        inv = jnp.exp((lane % n_freq).astype(F32) * (-math.log(ROPE_BASE) / n_freq))
        by_column = (lane % (2 * n_freq)) >= n_freq
        sign = jnp.where((lane % DK) < DK // 2, -1.0, 1.0)

        @pl.when(i == 0)
        def _():
            col = (lax.broadcasted_iota(jnp.int32, (tp, 1), 0) % GRID_W).astype(F32) * inv
            col_cos[...] = jnp.cos(col)
            col_sin[...] = jnp.sin(col) * sign

        row = (i * grid_rows + lax.broadcasted_iota(jnp.int32, (8, 1), 0)).astype(F32) * inv
        row_cos, row_sin = jnp.cos(row), jnp.sin(row) * sign
        tall = lambda t: jnp.concatenate(
            [jnp.broadcast_to(t[g:g + 1, :], (GRID_W, LANES)) for g in range(grid_rows)], axis=0)
        cos_ref[...] = jnp.where(is_ctx, 1.0, jnp.where(by_column, col_cos[...], tall(row_cos)))
        sin_ref[...] = jnp.where(is_ctx, 0.0, jnp.where(by_column, col_sin[...], tall(row_sin)))

    full = lambda a: pl.BlockSpec(a.shape, lambda i: (0,) * a.ndim)
    row = lambda width: pl.BlockSpec((tp, width), lambda i: (i, 0))
    return pl.pallas_call(
        body, name="prenorm", grid=(n_x_tiles + 1,),
        in_specs=[pl.BlockSpec((tp, D), lambda i: (jnp.minimum(i, n_x_tiles - 1), 0)), full(ctx),
                  full(modx), full(modc), full(norm_w), _AFTER_SPEC],
        out_specs=[row(D), pl.BlockSpec((D, tp), lambda i: (0, i)), row(LANES), row(LANES)],
        out_shape=[jax.ShapeDtypeStruct((lext, D), BF16), jax.ShapeDtypeStruct((D, lext), BF16),
                   jax.ShapeDtypeStruct((lext, LANES), F32), jax.ShapeDtypeStruct((lext, LANES), F32)],
        scratch_shapes=[pltpu.VMEM((tp, LANES), F32), pltpu.VMEM((tp, LANES), F32)],
        compiler_params=_cparams(dimension_semantics=("arbitrary",)),
    )(x, ctx, modx, modc, norm_w, after)


def _in_projection(p_ext, xm, cos_t, sin_t, w, pairs, me, name):
    lext = xm.shape[0]
    tr = lext // 4
    meta = []
    for (slot_a, k_a), (slot_b, k_b) in pairs:
        assert k_a ^ k_b == 1
        dev_a = jnp.bitwise_xor(me, k_a)
        a_low = dev_a % 2 == 0
        meta += [jnp.where(a_low, slot_a, slot_b), jnp.where(a_low, slot_b, slot_a), dev_a // 2]
    meta = jnp.stack(meta).astype(jnp.int32)

    def body(meta_ref, xm_ref, cos_ref, sin_ref, w_hbm, *rest):
        p_ref, w_ref, sems = rest[-3:]
        j, i = pl.program_id(0), pl.program_id(1)

        @pl.when(i == 0)
        def _():
            copies = [pltpu.make_async_copy(w_hbm.at[meta_ref[3 * j + h]], w_ref.at[:, h * WSH:(h + 1) * WSH],
                                            sems.at[h]) for h in range(2)]
            for cp in copies:
                cp.start()
            for cp in copies:
                cp.wait()

        _project_columns(xm_ref[...], w_ref, meta_ref[3 * j + 2], cos_ref[...], sin_ref[...], p_ref)

    row = lambda width: pl.BlockSpec((tr, width), lambda j, i, meta_: (i, 0))
    grid_spec = pltpu.PrefetchScalarGridSpec(
        num_scalar_prefetch=1, grid=(len(pairs), lext // tr),
        in_specs=[row(D), row(LANES), row(LANES), pl.BlockSpec(memory_space=pl.ANY)]
        + ([] if p_ext is None else [pl.BlockSpec(memory_space=pl.ANY)]),
        out_specs=pl.BlockSpec((tr, 2 * WSH), lambda j, i, meta_: (i, meta_[3 * j + 2])),
        scratch_shapes=[pltpu.VMEM((D, 2 * WSH), BF16), pltpu.SemaphoreType.DMA((2,))])
    return pl.pallas_call(
        body, name=name, grid_spec=grid_spec,
        out_shape=jax.ShapeDtypeStruct((lext, PW), BF16),
        input_output_aliases={} if p_ext is None else {5: 0},
        compiler_params=_cparams(dimension_semantics=("arbitrary", "arbitrary")),
    )(meta, xm, cos_t, sin_t, w, *([] if p_ext is None else [p_ext]))


def _decay_tables(lgf, lgb, n):
    i = lax.broadcasted_iota(jnp.int32, (n, 1), 0).astype(F32)
    return dict(i=i, k_f=jnp.exp(lgf * (n - 1.0 - i)), k_b=jnp.exp(lgb * i),
                q_f=jnp.exp(lgf * (i + 1.0)), q_b=jnp.exp(lgb * (n - i)))


def _decay_matrix(lgf, lgb, n):
    ii = lax.broadcasted_iota(jnp.int32, (n, n), 0)
    jj = lax.broadcasted_iota(jnp.int32, (n, n), 1)
    diff = (ii - jj).astype(F32)
    low = jnp.exp(lgf * jnp.maximum(diff, 0.0))
    up = jnp.exp(lgb * jnp.maximum(-diff, 0.0))
    return jnp.where(diff > 0, low, jnp.where(diff < 0, up, 2.0)), diff


def _cat_lanes(a, b):
    return jnp.concatenate([a.astype(BF16), b.astype(BF16)], axis=1)


def _retention_forward(p_ext, lg, seq, ctx_len):
    lext = seq + ctx_len
    n_chunks = seq // RET_C
    C = RET_C

    def body(lg_ref, q_ref, k_ref, v_ref, rn_ref, rstd_ref, kv_scr, sf_scr, sb_scr):
        pair = pl.program_id(0)
        lane = lax.broadcasted_iota(jnp.int32, (1, LANES), 1)
        heads = range(2)
        hmask = [(lane // DK == hh).astype(F32) for hh in heads]
        lgf = [lg_ref[0, 2 * pair + hh] for hh in heads]
        lgb = [lg_ref[1, 2 * pair + hh] for hh in heads]
        vs = [slice(hh * DV, (hh + 1) * DV) for hh in heads]
        t = [_decay_tables(lgf[hh], lgb[hh], C) for hh in heads]
        kc_all = k_ref[seq:lext, :].astype(F32)
        s0 = []
        for hh in heads:
            tc = _decay_tables(lgf[hh], lgb[hh], ctx_len)
            kc = kc_all * hmask[hh]
            s0.append(_dot_tn(_cat_lanes(kc * tc["k_f"], kc * tc["k_b"]), v_ref[seq:lext, vs[hh]]))

        def increments(c, carry):
            rows = pl.ds(pl.multiple_of(c * C, C), C)
            k_all = k_ref[rows, :].astype(F32)
            for hh in heads:
                k = k_all * hmask[hh]
                kv_scr[hh, c] = _dot_tn(_cat_lanes(k * t[hh]["k_f"], k * t[hh]["k_b"]), v_ref[rows, vs[hh]])
            return carry

        lax.fori_loop(0, n_chunks, increments, 0, unroll=2)
        gf_c = [jnp.exp(lgf[hh] * C) for hh in heads]
        gb_c = [jnp.exp(lgb[hh] * C) for hh in heads]

        def scan_f(c, s):
            for hh in heads:
                sf_scr[hh, c] = s[hh]
            return tuple(gf_c[hh] * s[hh] + kv_scr[hh, c, 0:LANES, :] for hh in heads)

        def scan_b(n, s):
            c = n_chunks - 1 - n
            for hh in heads:
                sb_scr[hh, c] = s[hh]
            return tuple(gb_c[hh] * s[hh] + kv_scr[hh, c, LANES:2 * LANES, :] for hh in heads)

        lax.fori_loop(0, n_chunks, scan_f, tuple(s0[hh][0:LANES] for hh in heads))
        lax.fori_loop(0, n_chunks, scan_b, tuple(s0[hh][LANES:2 * LANES] for hh in heads))
        dmat = [_decay_matrix(lgf[hh], lgb[hh], C)[0] for hh in heads]

        def outputs(c, carry):
            rows = pl.ds(pl.multiple_of(c * C, C), C)
            q_all = q_ref[rows, :].astype(F32)
            k = k_ref[rows, :]
            for hh in heads:
                q = q_all * hmask[hh]
                v = v_ref[rows, vs[hh]]
                s = _dot_nt(q.astype(BF16), k)
                inner = _dot((s * dmat[hh]).astype(BF16), v)
                states = jnp.concatenate([sf_scr[hh, c], sb_scr[hh, c]], axis=0).astype(BF16)
                ret = inner + _dot(_cat_lanes(q * t[hh]["q_f"], q * t[hh]["q_b"]), states)
                mu = jnp.mean(ret, axis=-1, keepdims=True)
                cen = ret - mu
                rstd = lax.rsqrt(jnp.mean(cen * cen, axis=-1, keepdims=True) + EPS)
                rn_ref[rows, vs[hh]] = cen * rstd
                rstd_ref[rows, vs[hh]] = jnp.broadcast_to(rstd, (C, DV))
            return carry

        lax.fori_loop(0, n_chunks, outputs, 0, unroll=2)

    qk0 = CB_QK * D // LANES
    return pl.pallas_call(
        body, name="retention_forward", grid=(N_HEADS // 2,),
        in_specs=[pl.BlockSpec(memory_space=pltpu.SMEM),
                  pl.BlockSpec((lext, LANES), lambda g: (0, qk0 + g)),
                  pl.BlockSpec((lext, LANES), lambda g: (0, qk0 + N_HEADS // 2 + g)),
                  pl.BlockSpec((lext, 2 * DV), lambda g: (0, CB_V * D // (2 * DV) + g))],
        out_specs=[pl.BlockSpec((seq, 2 * DV), lambda g: (0, g))] * 2,
        out_shape=[jax.ShapeDtypeStruct((seq, D), F32)] * 2,
        scratch_shapes=[pltpu.VMEM((2, n_chunks, 2 * LANES, DV), F32), pltpu.VMEM((2, n_chunks, LANES, DV), F32),
                        pltpu.VMEM((2, n_chunks, LANES, DV), F32)],
        compiler_params=_cparams(dimension_semantics=("arbitrary",)),
    )(lg, p_ext, p_ext, p_ext)


def _retention_backward(p_ext, dret, lg, cos_t, sin_t, seq, ctx_len, after):
    lext = seq + ctx_len
    n_chunks = seq // RET_C
    C = RET_C

    def body(lg_ref, q_ref, k_ref, v_ref, do_ref, cos_ref, sin_ref, after_ref, dq_ref, dk_ref, dv_ref, dlg_ref,
             kv_scr, g_scr, sf_scr, sb_scr, gfn_scr, gbp_scr):
        pair = pl.program_id(0)
        lane = lax.broadcasted_iota(jnp.int32, (1, LANES), 1)
        heads = range(2)
        hmask = [(lane // DK == hh).astype(F32) for hh in heads]
        lgf = [lg_ref[0, 2 * pair + hh] for hh in heads]
        lgb = [lg_ref[1, 2 * pair + hh] for hh in heads]
        vs = [slice(hh * DV, (hh + 1) * DV) for hh in heads]
        t = [_decay_tables(lgf[hh], lgb[hh], C) for hh in heads]
        tc = [_decay_tables(lgf[hh], lgb[hh], ctx_len) for hh in heads]
        kc_all = k_ref[seq:lext, :].astype(F32)
        kc = [kc_all * hmask[hh] for hh in heads]
        vc = [v_ref[seq:lext, vs[hh]] for hh in heads]
        kc_cat = [_cat_lanes(kc[hh] * tc[hh]["k_f"], kc[hh] * tc[hh]["k_b"]) for hh in heads]
        s0 = [_dot_tn(kc_cat[hh], vc[hh]) for hh in heads]

        def increments(c, carry):
            rows = pl.ds(pl.multiple_of(c * C, C), C)
            k_all = k_ref[rows, :].astype(F32)
            q_all = q_ref[rows, :].astype(F32)
            for hh in heads:
                k, q = k_all * hmask[hh], q_all * hmask[hh]
                kv_scr[hh, c] = _dot_tn(_cat_lanes(k * t[hh]["k_f"], k * t[hh]["k_b"]), v_ref[rows, vs[hh]])
                g_scr[hh, c] = _dot_tn(_cat_lanes(q * t[hh]["q_f"], q * t[hh]["q_b"]), do_ref[rows, vs[hh]])
            return carry

        lax.fori_loop(0, n_chunks, increments, 0, unroll=2)
        gf_c = [jnp.exp(lgf[hh] * C) for hh in heads]
        gb_c = [jnp.exp(lgb[hh] * C) for hh in heads]

        def scan_f(c, s):
            for hh in heads:
                sf_scr[hh, c] = s[hh]
            return tuple(gf_c[hh] * s[hh] + kv_scr[hh, c, 0:LANES, :] for hh in heads)

        def scan_b(n, s):
            c = n_chunks - 1 - n
            for hh in heads:
                sb_scr[hh, c] = s[hh]
            return tuple(gb_c[hh] * s[hh] + kv_scr[hh, c, LANES:2 * LANES, :] for hh in heads)

        def scan_gf(n, carry):
            c = n_chunks - 1 - n
            for hh in heads:
                gfn_scr[hh, c] = carry[hh]
            return tuple(g_scr[hh, c, 0:LANES, :] + gf_c[hh] * carry[hh] for hh in heads)

        def scan_gb(c, carry):
            for hh in heads:
                gbp_scr[hh, c] = carry[hh]
            return tuple(g_scr[hh, c, LANES:2 * LANES, :] + gb_c[hh] * carry[hh] for hh in heads)

        lax.fori_loop(0, n_chunks, scan_f, tuple(s0[hh][0:LANES] for hh in heads))
        lax.fori_loop(0, n_chunks, scan_b, tuple(s0[hh][LANES:2 * LANES] for hh in heads))
        zero_state = jnp.zeros((LANES, DV), F32)
        gf_first = lax.fori_loop(0, n_chunks, scan_gf, (zero_state, zero_state))
        gb_last = lax.fori_loop(0, n_chunks, scan_gb, (zero_state, zero_state))

        dmat, w_f, w_b = [], [], []
        for hh in heads:
            dm, diff = _decay_matrix(lgf[hh], lgb[hh], C)
            dmat.append(dm)
            w_f.append(jnp.where(diff > 0, diff * dm, 0.0))
            w_b.append(jnp.where(diff < 0, -diff * dm, 0.0))

        def total(a):
            rows_, width = a.shape
            part = jnp.sum(a.reshape(rows_ // 8, 8, width), axis=0)
            return part[:, 0:LANES] + part[:, LANES:2 * LANES] if width == 2 * LANES else part

        def chunk_head(hh, c, rows, q_all, k_all, dlgf, dlgb):
            th = t[hh]
            qm = q_all * hmask[hh]
            km = k_all * hmask[hh]
            qb, kb = qm.astype(BF16), km.astype(BF16)
            v = v_ref[rows, vs[hh]]
            do = do_ref[rows, vs[hh]]
            s = _dot_nt(qb, kb)
            dsd = _dot_nt(do, v)
            ds = (dsd * dmat[hh]).astype(BF16)
            dq_in = _dot(ds, kb)
            dk_in = _dot_tn(ds, qb)
            dv_in = _dot_tn((s * dmat[hh]).astype(BF16), do)
            prod = s * dsd
            dlgf = dlgf + total(prod * w_f[hh])
            dlgb = dlgb + total(prod * w_b[hh])
            sf, sb = sf_scr[hh, c], sb_scr[hh, c]
            states = jnp.concatenate([sf, sb], axis=0).astype(BF16)
            dqc = _dot_nt(do, states)
            dqf = dqc[:, 0:LANES] * th["q_f"]
            dqb = dqc[:, LANES:2 * LANES] * th["q_b"]
            dq = (dq_in + dqf + dqb) * hmask[hh]
            dlgf = dlgf + total((th["i"] + 1.0) * (qm * dqf))
            dlgb = dlgb + total((C - th["i"]) * (qm * dqb))
            gfn, gbp = gfn_scr[hh, c], gbp_scr[hh, c]
            gstates = jnp.concatenate([gfn, gbp], axis=0).astype(BF16)
            dkc = _dot_nt(v, gstates)
            dkf = dkc[:, 0:LANES] * th["k_f"]
            dkb = dkc[:, LANES:2 * LANES] * th["k_b"]
            dk = dk_in + dkf + dkb
            dlgf = dlgf + total((C - 1.0 - th["i"]) * (km * dkf)) + C * gf_c[hh] * total(gfn * sf)
            dlgb = dlgb + total(th["i"] * (km * dkb)) + C * gb_c[hh] * total(gbp * sb)
            dv = dv_in + _dot(_cat_lanes(km * th["k_f"], km * th["k_b"]), gstates)
            dv_ref[rows, vs[hh]] = dv.astype(BF16)
            return dq, dk, dlgf, dlgb

        def chunk(c, carry):
            rows = pl.ds(pl.multiple_of(c * C, C), C)
            q_all = q_ref[rows, :].astype(F32)
            k_all = k_ref[rows, :].astype(F32)
            dq0, dk0, f0, b0 = chunk_head(0, c, rows, q_all, k_all, carry[0], carry[1])
            dq1, dk1, f1, b1 = chunk_head(1, c, rows, q_all, k_all, carry[2], carry[3])
            cos, sin = cos_ref[rows, :], sin_ref[rows, :]
            dq_ref[rows, :] = _rope(dq0 + dq1, cos, -sin).astype(BF16)
            dk_ref[rows, :] = (_rope(dk0 + dk1, cos, -sin) * (DK ** -0.5)).astype(BF16)
            return f0, b0, f1, b1

        zero = jnp.zeros((8, LANES), F32)
        sums = lax.fori_loop(0, n_chunks, chunk, (zero, zero, zero, zero), unroll=2)

        dlg = []
        dk_ctx = jnp.zeros((ctx_len, LANES), F32)
        for hh in heads:
            g0 = jnp.concatenate([gf_first[hh], gb_last[hh]], axis=0).astype(BF16)
            dkcc = _dot_nt(vc[hh], g0)
            dkcf = dkcc[:, 0:LANES] * tc[hh]["k_f"]
            dkcb = dkcc[:, LANES:2 * LANES] * tc[hh]["k_b"]
            dlgf = sums[2 * hh] + total((ctx_len - 1.0 - tc[hh]["i"]) * (kc[hh] * dkcf))
            dlgb = sums[2 * hh + 1] + total(tc[hh]["i"] * (kc[hh] * dkcb))
            dk_ctx = dk_ctx + (dkcf + dkcb) * (DK ** -0.5)
            dv_ref[seq:lext, vs[hh]] = _dot(kc_cat[hh], g0).astype(BF16)
            dlg += [jnp.sum(jnp.sum(a, axis=1, keepdims=True), axis=0, keepdims=True) for a in (dlgf, dlgb)]
        dk_ref[seq:lext, :] = dk_ctx.astype(BF16)
        dq_ref[seq:lext, :] = jnp.zeros((ctx_len, LANES), BF16)

        lane8 = lax.broadcasted_iota(jnp.int32, (8, LANES), 1)
        out = jnp.zeros((8, LANES), F32)
        for n, val in enumerate(dlg):
            out = jnp.where(lane8 == n, val, out)
        dlg_ref[0] = out

    qk0 = CB_QK * D // LANES
    q_spec = pl.BlockSpec((lext, LANES), lambda g: (0, qk0 + g))
    k_spec = pl.BlockSpec((lext, LANES), lambda g: (0, qk0 + N_HEADS // 2 + g))
    v_spec = pl.BlockSpec((lext, 2 * DV), lambda g: (0, CB_V * D // (2 * DV) + g))
    table = pl.BlockSpec((lext, LANES), lambda g: (0, 0))
    state = pltpu.VMEM((2, n_chunks, LANES, DV), F32)
    return pl.pallas_call(
        body, name="retention_backward", grid=(N_HEADS // 2,),
        in_specs=[pl.BlockSpec(memory_space=pltpu.SMEM), q_spec, k_spec, v_spec,
                  pl.BlockSpec((seq, 2 * DV), lambda g: (0, g)), table, table, _AFTER_SPEC],
        out_specs=[pl.BlockSpec((lext, LANES), lambda g: (0, g)), pl.BlockSpec((lext, LANES), lambda g: (0, g)),
                   pl.BlockSpec((lext, 2 * DV), lambda g: (0, g)), pl.BlockSpec((1, 8, LANES), lambda g: (g, 0, 0))],
        out_shape=[jax.ShapeDtypeStruct((lext, N_HEADS * DK), BF16), jax.ShapeDtypeStruct((lext, N_HEADS * DK), BF16),
                   jax.ShapeDtypeStruct((lext, D), BF16), jax.ShapeDtypeStruct((N_HEADS // 2, 8, LANES), F32)],
        scratch_shapes=[pltpu.VMEM((2, n_chunks, 2 * LANES, DV), F32), pltpu.VMEM((2, n_chunks, 2 * LANES, DV), F32),
                        state, state, state, state],
        compiler_params=_cparams(dimension_semantics=("arbitrary",)),
    )(lg, p_ext, p_ext, p_ext, dret, cos_t, sin_t, after)


class _ColumnWriter:
    def __init__(self, dst_hbm, stage, sems, pieces, n_steps):
        self.dst, self.stage, self.sems, self.pieces, self.n_steps = dst_hbm, stage, sems, pieces, n_steps

    def _copies(self, slot, tile):
        rows = pl.ds(pl.multiple_of(tile * TM, TM), TM)
        return [pltpu.make_async_copy(self.stage.at[slot, :, pl.ds(src, width)], self.dst.at[rows, pl.ds(dst, width)],
                                      self.sems.at[slot, k]) for k, (src, dst, width) in enumerate(self.pieces)]

    def slot_for(self, step):
        slot = step % 2

        @pl.when(step >= 2)
        def _():
            for cp in self._copies(slot, step - 2):
                cp.wait()

        return slot

    def send(self, step, slot):
        for cp in self._copies(slot, step):
            cp.start()

        @pl.when(step == self.n_steps - 1)
        def _():
            if self.n_steps >= 2:
                for cp in self._copies(1 - slot, step - 1):
                    cp.wait()
            for cp in self._copies(slot, step):
                cp.wait()


def _merge(p_ext, rn, rstd, x, target, w_a, w_b, w_out, vecs, seq):
    n_tiles = seq // TM
    hb = TM // HALO
    dp_pieces = [(0, CB_BG * D, D), (D, CB_ZA * D, D), (2 * D, CB_ZB * D, 3 * D)]

    def body(h_ref, bg_ref, cg_ref, za_ref, zb_ref, ga_ref, gb_ref, hp_ref, hn_ref, cp_ref, cn_ref,
             rn_ref, rstd_ref, x_ref, t_ref, wa_ref, wb_ref, wo_ref, vec_ref,
             dx1_ref, dp_hbm, dconv_ref, dret_ref, at_ref, b_ref, part_ref, stage, dp_sems):
        i = pl.program_id(0)
        writer = _ColumnWriter(dp_hbm, stage, dp_sems, dp_pieces, n_tiles)
        slot = writer.slot_for(i)
        dpa_ref = stage.at[slot]
        f = lambda ref: ref[...].astype(F32)
        h, bg, cg, za, zb, ga, gb = f(h_ref), f(bg_ref), f(cg_ref), f(za_ref), f(zb_ref), f(ga_ref), f(gb_ref)
        gx, w0, w1, w2 = vec_ref[0:1, :], vec_ref[1:2, :], vec_ref[2:3, :], vec_ref[3:4, :]
        cb, gnw, fw = vec_ref[4:5, :], vec_ref[5:6, :], vec_ref[6:7, :]
        u = cg * h
        row = lax.broadcasted_iota(jnp.int32, (TM, 1), 0)
        u_prev = (f(cp_ref) * f(hp_ref))[HALO - 1:HALO, :]
        u_next = (f(cn_ref) * f(hn_ref))[0:1, :]
        u_prev = jnp.where(i == 0, 0.0, u_prev)
        u_next = jnp.where(i == n_tiles - 1, 0.0, u_next)
        u_up = jnp.where(row == 0, u_prev, pltpu.roll(u, 1, 0))
        u_dn = jnp.where(row == TM - 1, u_next, pltpu.roll(u, TM - 1, 0))
        conv = w0 * u_up + w1 * u + w2 * u_dn + cb
        sza = _sigmoid(za)
        silu_za = za * sza
        a_act = silu_za * bg * conv
        rn = rn_ref[...]
        szb = _sigmoid(zb)
        silu_zb = zb * szb
        rg = rn * gnw
        b_act = silu_zb * rg
        y_a = _dot(a_act.astype(BF16), wa_ref[...])
        y_b = _dot(b_act.astype(BF16), wb_ref[...])
        sga, sgb = _sigmoid(ga), _sigmoid(gb)
        mix = sga * y_a + sgb * y_b
        y = _dot(mix.astype(BF16), wo_ref[...])
        x1 = x_ref[...] + gx * y
        r1 = lax.rsqrt(jnp.mean(x1 * x1, axis=-1, keepdims=True) + EPS)
        xh1 = x1 * r1
        err = xh1 * fw - t_ref[...]
        loss = jnp.sum(jnp.sum(err * err, axis=1, keepdims=True), axis=0, keepdims=True) * (0.5 / D)
        dxh = err * (fw * (1.0 / D))
        dx1 = r1 * (dxh - xh1 * jnp.mean(dxh * xh1, axis=-1, keepdims=True))
        dx1_ref[...] = dx1
        dy = (dx1 * gx).astype(BF16)
        dmix = _dot_nt(dy, wo_ref[...])
        dya_f, dyb_f = dmix * sga, dmix * sgb
        dya, dyb = dya_f.astype(BF16), dyb_f.astype(BF16)
        da = _dot_nt(dya, wa_ref[...])
        db = _dot_nt(dyb, wb_ref[...])
        da_silu = da * silu_za
        dpa_ref[:, 0:D] = (da_silu * conv).astype(BF16)
        dpa_ref[:, D:2 * D] = (da * bg * conv * (sza + silu_za * (1.0 - sza))).astype(BF16)
        dpa_ref[:, 2 * D:3 * D] = (db * rg * (szb + silu_zb * (1.0 - szb))).astype(BF16)
        dpa_ref[:, 3 * D:4 * D] = (dya_f * y_a * (1.0 - sga)).astype(BF16)
        dpa_ref[:, 4 * D:5 * D] = (dyb_f * y_b * (1.0 - sgb)).astype(BF16)
        dconv_ref[...] = (da_silu * bg).astype(BF16)
        drn_n = db * silu_zb
        drn = drn_n * gnw
        rstd = rstd_ref[...]
        for hd in range(N_HEADS):
            sl = slice(hd * DV, (hd + 1) * DV)
            dh_, rh = drn[:, sl], rn[:, sl]
            m1 = jnp.mean(dh_, axis=-1, keepdims=True)
            m2 = jnp.mean(dh_ * rh, axis=-1, keepdims=True)
            dret_ref[:, sl] = (rstd[:, sl] * (dh_ - m1 - rh * m2)).astype(BF16)
        at_ref[0] = a_act.T.astype(BF16)
        at_ref[1] = b_act.T.astype(BF16)
        at_ref[2] = mix.T.astype(BF16)
        b_ref[0] = dya
        b_ref[1] = dyb
        b_ref[2] = dy

        @pl.when(i == 0)
        def _():
            part_ref[...] = jnp.zeros((8, D), F32)

        part_ref[0:1, :] += jnp.sum(err * xh1, axis=0, keepdims=True) * (1.0 / D)
        part_ref[1:2, :] += jnp.sum(dx1 * y, axis=0, keepdims=True)
        part_ref[2:3, :] += jnp.sum(drn_n * rn, axis=0, keepdims=True)
        part_ref[3:4, :] += jnp.broadcast_to(loss, (1, D))
        writer.send(i, slot)

    col = lambda cb_: pl.BlockSpec((TM, D), lambda i, cb_=cb_: (i, cb_))
    prev = lambda cb_: pl.BlockSpec((HALO, D), lambda i, cb_=cb_: (jnp.maximum(i * hb - 1, 0), cb_))
    nxt = lambda cb_: pl.BlockSpec((HALO, D), lambda i, cb_=cb_: (jnp.minimum((i + 1) * hb, n_tiles * hb - 1), cb_))
    tile = pl.BlockSpec((TM, D), lambda i: (i, 0))
    full = lambda a: pl.BlockSpec(a.shape, lambda i: (0,) * a.ndim, pipeline_mode=pl.Buffered(1))
    return pl.pallas_call(
        body, name="merge", grid=(n_tiles,),
        in_specs=[col(CB_H), col(CB_BG), col(CB_CG), col(CB_ZA), col(CB_ZB), col(CB_GA), col(CB_GB),
                  prev(CB_H), nxt(CB_H), prev(CB_CG), nxt(CB_CG),
                  tile, tile, tile, tile, full(w_a), full(w_b), full(w_out), full(vecs)],
        out_specs=[tile, pl.BlockSpec(memory_space=pl.ANY), tile, tile,
                   pl.BlockSpec((3, D, TM), lambda i: (0, 0, i)), pl.BlockSpec((3, TM, D), lambda i: (0, i, 0)),
                   pl.BlockSpec((8, D), lambda i: (0, 0))],
        out_shape=[jax.ShapeDtypeStruct((seq, D), F32), jax.ShapeDtypeStruct((p_ext.shape[0], PW), BF16),
                   jax.ShapeDtypeStruct((seq, D), BF16), jax.ShapeDtypeStruct((seq, D), BF16),
                   jax.ShapeDtypeStruct((3, D, seq), BF16), jax.ShapeDtypeStruct((3, seq, D), BF16),
                   jax.ShapeDtypeStruct((8, D), F32)],
        scratch_shapes=[pltpu.VMEM((2, TM, 5 * D), BF16), pltpu.SemaphoreType.DMA((2, len(dp_pieces)))],
        compiler_params=_cparams(dimension_semantics=("arbitrary",)),
    )(p_ext, p_ext, p_ext, p_ext, p_ext, p_ext, p_ext, p_ext, p_ext, p_ext, p_ext,
      rn, rstd, x, target, w_a, w_b, w_out, vecs)


def _conv_backward(p_ext, dconv, dp, dq, dk, dv, vecs, seq, after):
    n_tiles = seq // TM
    hb = TM // HALO
    qk = N_HEADS * DK
    pieces = [(0, CB_H * D, D), (D, CB_CG * D, D), (2 * D, CB_QK * D, 2 * D)]
    zero_pieces = [(CB_BG * D, D), (CB_ZA * D, D), (CB_ZB * D, 3 * D)]

    def body(h_ref, cg_ref, dc_ref, dcp_ref, dcn_ref, dq_ref, dk_ref, dv_ref, vec_ref, dp_in, after_ref, dp_hbm, part_ref,
             stage, sems, zeros, zero_sems):
        del dp_in, after_ref
        i = pl.program_id(0)
        writer = _ColumnWriter(dp_hbm, stage, sems, pieces, n_tiles + 1)
        slot = writer.slot_for(i)
        out = stage.at[slot]
        out[:, 2 * D:2 * D + qk] = dq_ref[...]
        out[:, 2 * D + qk:3 * D] = dk_ref[...]
        out[:, 3 * D:4 * D] = dv_ref[...]

        @pl.when(i == 0)
        def _():
            part_ref[...] = jnp.zeros((8, D), F32)

        @pl.when(i == n_tiles)
        def _():
            out[:, 0:2 * D] = jnp.zeros((TM, 2 * D), BF16)
            zeros[...] = jnp.zeros(zeros.shape, BF16)
            rows = pl.ds(n_tiles * TM, TM)
            fills = [pltpu.make_async_copy(zeros.at[:, pl.ds(0, width)], dp_hbm.at[rows, pl.ds(dst, width)],
                                           zero_sems.at[k]) for k, (dst, width) in enumerate(zero_pieces)]
            for cp in fills:
                cp.start()
            for cp in fills:
                cp.wait()

        @pl.when(i < n_tiles)
        def _():
            f = lambda ref: ref[...].astype(F32)
            h, cg, dc = f(h_ref), f(cg_ref), f(dc_ref)
            w0, w1, w2 = vec_ref[1:2, :], vec_ref[2:3, :], vec_ref[3:4, :]
            row = lax.broadcasted_iota(jnp.int32, (TM, 1), 0)
            dc_prev = jnp.where(i == 0, 0.0, f(dcp_ref)[HALO - 1:HALO, :])
            dc_next = jnp.where(i == n_tiles - 1, 0.0, f(dcn_ref)[0:1, :])
            dc_up = jnp.where(row == 0, dc_prev, pltpu.roll(dc, 1, 0))
            dc_dn = jnp.where(row == TM - 1, dc_next, pltpu.roll(dc, TM - 1, 0))
            du = w0 * dc_dn + w1 * dc + w2 * dc_up
            u = cg * h
            part_ref[0:1, :] += jnp.sum(u * dc_dn, axis=0, keepdims=True)
            part_ref[1:2, :] += jnp.sum(u * dc, axis=0, keepdims=True)
            part_ref[2:3, :] += jnp.sum(u * dc_up, axis=0, keepdims=True)
            part_ref[3:4, :] += jnp.sum(dc, axis=0, keepdims=True)
            out[:, 0:D] = (du * cg).astype(BF16)
            out[:, D:2 * D] = (du * h).astype(BF16)

        writer.send(i, slot)

    last = n_tiles - 1
    col = lambda cb_: pl.BlockSpec((TM, D), lambda i, cb_=cb_: (jnp.minimum(i, last), cb_))
    lat = lambda w: pl.BlockSpec((TM, w), lambda i: (jnp.minimum(i, last), 0))
    ext = lambda w: pl.BlockSpec((TM, w), lambda i: (i, 0))
    anyspec = pl.BlockSpec(memory_space=pl.ANY)
    return pl.pallas_call(
        body, name="conv_backward", grid=(n_tiles + 1,),
        in_specs=[col(CB_H), col(CB_CG), lat(D),
                  pl.BlockSpec((HALO, D), lambda i: (jnp.clip(i * hb - 1, 0, n_tiles * hb - 1), 0)),
                  pl.BlockSpec((HALO, D), lambda i: (jnp.minimum((i + 1) * hb, n_tiles * hb - 1), 0)),
                  ext(qk), ext(qk), ext(D), pl.BlockSpec(vecs.shape, lambda i: (0, 0)), anyspec, _AFTER_SPEC],
        out_specs=[anyspec, pl.BlockSpec((8, D), lambda i: (0, 0))],
        out_shape=[jax.ShapeDtypeStruct(dp.shape, BF16), jax.ShapeDtypeStruct((8, D), F32)],
        input_output_aliases={9: 0},
        scratch_shapes=[pltpu.VMEM((2, TM, 4 * D), BF16), pltpu.SemaphoreType.DMA((2, len(pieces))),
                        pltpu.VMEM((TM, 3 * D), BF16), pltpu.SemaphoreType.DMA((len(zero_pieces),))],
        compiler_params=_cparams(dimension_semantics=("arbitrary",)),
    )(p_ext, p_ext, dconv, dconv, dconv, dq, dk, dv, vecs, dp, after)


def _input_backward(dp, shards, ids, x, ctx, dx1, modx, modc, norm_w, after):
    seq = x.shape[0]
    lext = seq + ctx.shape[0]
    n_x = seq // TM
    n_w = len(shards.arrays)

    def body(ids_ref, dp_ref, x_ref, ctx_ref, dx1_ref, mx_ref, mc_ref, nw_ref, *rest):
        w_hbm, (_, gx_ref, part_ref, w_ref, w_sems) = rest[:n_w], rest[n_w:]
        i = pl.program_id(0)
        _load_w_in(i, ids_ref, shards, w_hbm, w_ref, w_sems)
        is_ctx = i >= n_x
        dxm = _dot_nt(dp_ref[...], w_ref[...])
        x = jnp.where(is_ctx, ctx_ref[...], x_ref[...])
        r = lax.rsqrt(jnp.mean(x * x, axis=-1, keepdims=True) + EPS)
        xh = x * r
        nw = nw_ref[...]
        sc = jnp.where(is_ctx, mc_ref[1:2, :], mx_ref[1:2, :])
        dxn = dxm * (1.0 + sc)
        dxh = dxn * nw
        dx = r * (dxh - xh * jnp.mean(dxh * xh, axis=-1, keepdims=True))

        @pl.when(jnp.logical_not(is_ctx))
        def _():
            gx_ref[...] = dx1_ref[...] + dx

        @pl.when(i == 0)
        def _():
            part_ref[...] = jnp.zeros((8, D), F32)

        fx = jnp.where(is_ctx, 0.0, 1.0)
        d_shift = jnp.sum(dxm, axis=0, keepdims=True)
        d_scale = jnp.sum(dxm * (xh * nw), axis=0, keepdims=True)
        part_ref[0:1, :] += fx * d_shift
        part_ref[1:2, :] += fx * d_scale
        part_ref[2:3, :] += jnp.sum(dxn * xh, axis=0, keepdims=True)
        part_ref[3:4, :] += (1.0 - fx) * d_shift
        part_ref[4:5, :] += (1.0 - fx) * d_scale

    lat = lambda w: pl.BlockSpec((TM, w), lambda i, ids_: (jnp.minimum(i, n_x - 1), 0))
    ext = lambda w: pl.BlockSpec((TM, w), lambda i, ids_: (i, 0))
    full = lambda a: pl.BlockSpec(a.shape, lambda i, ids_: (0,) * a.ndim)
    grid_spec = pltpu.PrefetchScalarGridSpec(
        num_scalar_prefetch=1, grid=(lext // TM,),
        in_specs=[ext(PW), lat(D), full(ctx), lat(D), full(modx), full(modc), full(norm_w)]
        + [pl.BlockSpec(memory_space=pl.ANY)] * n_w + [_AFTER_SPEC],
        out_specs=[lat(D), pl.BlockSpec((8, D), lambda i, ids_: (0, 0))],
        scratch_shapes=[pltpu.VMEM((D, PW), BF16), pltpu.SemaphoreType.DMA((N_DEV,))])
    return pl.pallas_call(
        body, name="input_backward", grid_spec=grid_spec,
        out_shape=[jax.ShapeDtypeStruct((seq, D), F32), jax.ShapeDtypeStruct((8, D), F32)],
        compiler_params=_cparams(dimension_semantics=("arbitrary",)),
    )(ids, dp, x, ctx, dx1, modx, modc, norm_w, *shards.arrays, after)


def _weight_grad_in(xm_t, dp, owners, name, after):
    lext = xm_t.shape[1]

    def body(own_ref, a_ref, b_ref, after_ref, o_ref):
        o_ref[0] = _dot(a_ref[...], b_ref[...])

    grid_spec = pltpu.PrefetchScalarGridSpec(
        num_scalar_prefetch=1, grid=(4,),
        in_specs=[pl.BlockSpec((D, lext), lambda j, own: (0, 0)),
                  pl.BlockSpec((lext, WSH), lambda j, own: (0, own[j])), _AFTER_SPEC],
        out_specs=pl.BlockSpec((1, D, WSH), lambda j, own: (j, 0, 0)))
    return pl.pallas_call(
        body, name=name, grid_spec=grid_spec,
        out_shape=jax.ShapeDtypeStruct((4, D, WSH), F32),
        compiler_params=_cparams(dimension_semantics=("arbitrary",)),
    )(owners, xm_t, dp, after)


def _weight_grad_square(a_t, b):
    seq = a_t.shape[2]

    def body(a_ref, b_ref, o_ref):
        o_ref[:, 0] = _dot(a_ref[0], b_ref[0]).reshape(N_DEV, RSH, D)

    return pl.pallas_call(
        body, name="weight_grad_square", grid=(3,),
        in_specs=[pl.BlockSpec((1, D, seq), lambda t: (t, 0, 0)), pl.BlockSpec((1, seq, D), lambda t: (t, 0, 0))],
        out_specs=pl.BlockSpec((N_DEV, 1, RSH, D), lambda t: (0, t, 0, 0)),
        out_shape=jax.ShapeDtypeStruct((N_DEV, 3, RSH, D), F32),
        compiler_params=_cparams(dimension_semantics=("arbitrary",)),
    )(a_t, b)


def _rows(total, *parts):
    width = max(a.shape[1] for _, a in parts)
    out = None
    for row, a in parts:
        padded = jnp.pad(a, ((row, total - row - a.shape[0]), (0, width - a.shape[1])))
        out = padded if out is None else out + padded
    return out


def kernel(x, c, ctx, c_ctx, norm_w, ada_w, ada_b, w_in, conv_w, conv_b, decay_logit, gn_w, w_a, w_b, w_out, final_norm_w, loss_target, m_c_ctx, m_norm_w, m_ada_w, m_ada_b, m_w_in, m_conv_w, m_conv_b, m_decay_logit, m_gn_w, m_w_a, m_w_b, m_w_out, m_final_norm_w, v_c_ctx, v_norm_w, v_ada_w, v_ada_b, v_w_in, v_conv_w, v_conv_b, v_decay_logit, v_gn_w, v_w_a, v_w_b, v_w_out, v_final_norm_w):
    xi, yi, ci = _coords()
    me = 4 * xi + 2 * yi + ci
    chip = 2 * xi + yi
    seq, ctx_len = x.shape[1], ctx.shape[1]
    assert seq % TM == 0 and seq % RET_C == 0 and ctx_len == TM and seq % GRID_W == 0
    csh = D // N_DEV

    blk = jnp.pad(c, ((0, 7), (0, 0))) + jnp.pad(conv_w[0], ((1, 4), (0, D - csh)))
    got, _ = _all_gather_small(blk, "gather_cond")
    conv_w_all = got[:, 1:4, 0:csh].transpose(1, 0, 2).reshape(3, D)
    c16 = _rows(16, (0, got[:, 0, :]), (N_DEV, c_ctx[None]))
    ada_b_sh = lax.dynamic_slice(ada_b, (0, me * ADA_SH), (1, ADA_SH))
    mod_sh, act16, lg = _modulation(c16, ada_w[0], ada_b_sh, decay_logit[0])
    mod_all, small_done = _all_gather_small(mod_sh, "gather_mod")
    mod_all = mod_all.transpose(1, 0, 2).reshape(16, 3 * D)
    modx = lax.dynamic_slice(mod_all, (me, 0), (1, 3 * D)).reshape(3, D)
    modc = mod_all[8].reshape(3, D)

    x2, tgt = x[0], loss_target[0]
    wblock = lambda n: jax.ShapeDtypeStruct((n, D, WSH), BF16)
    ids_of = lambda ks: jnp.stack([jnp.bitwise_xor(me, k) for k in ks]).astype(jnp.int32)
    own_shard = w_in[0].astype(BF16)[None]
    sq_shards = [w[0].astype(BF16) for w in (w_a, w_b, w_out)]
    wave1 = _Exchange(
        "gather_in_first", [jnp.broadcast_to(own_shard, (2, D, WSH)), own_shard], [wblock(4)], 3,
        lambda srcs, lands: _forward_plan([(0, 1)])(srcs, lands) + _send_plan((4, 2))(srcs[1:], lands),
        after=small_done)
    xm, xm_t, cos_t, sin_t = _prenorm(x2, ctx[0], modx, modc, norm_w, wave1.token)
    (w_pair, _), _ = wave1.wait(xm_t, 1, through=[0])
    first = _Shards([w_pair], [(0, 0), (0, 1)])
    p_ext = _in_projection(None, xm, cos_t, sin_t, w_pair, [((0, 0), (1, 1))], me, "in_projection_pair")
    _, (w_nbr,) = wave1.wait(p_ext, through=[1, 2])
    wave2 = _Exchange(
        "gather_in_second", [w_nbr, own_shard], [wblock(2)], 3,
        lambda srcs, lands: _forward_plan([(0, 2), (1, 3)])(srcs, lands) + _send_plan((6,))(srcs[1:], lands))
    (w_nbr, _), _ = wave2.wait(wave2.token, 2, through=[0])
    second = _Shards([w_nbr], [(0, 0), (0, 1), (0, 2), (0, 3)])
    p_ext = _in_projection(p_ext, xm, cos_t, sin_t, w_nbr, [((0, 4), (2, 5)), ((1, 2), (3, 3))], me,
                           "in_projection_neighbours")
    _, (w_diag,) = wave2.wait(p_ext, through=[1, 2])
    wave3 = _Exchange(
        "gather_in_third", [w_diag] + sq_shards, [jax.ShapeDtypeStruct((N_DEV, RSH, D), BF16)] * 3, 1 + 3 * (N_DEV - 1),
        lambda srcs, lands: _forward_plan([(0, 1)])(srcs, lands) + _gather_plan(srcs[1:], lands))
    (w_diag, *_), _ = wave3.wait(wave3.token, 1, through=[0])
    third = _Shards([w_diag], [(0, 0), (0, 1)])
    p_ext = _in_projection(p_ext, xm, cos_t, sin_t, w_diag, [((0, 6), (1, 7))], me, "in_projection_diagonal")
    w_shards, w_ids = first + second + third, ids_of((0, 1, 4, 2, 5, 3, 6, 7))

    rn, rstd = _retention_forward(p_ext, lg, seq, ctx_len)
    _, sq_lands = wave3.wait(rstd, through=range(1, 7))
    w_a_all, w_b_all, w_out_all = (
        lax.dynamic_update_slice(land, shard[None], (me, 0, 0)).reshape(D, D) for land, shard in zip(sq_lands, sq_shards))
    vecs = _rows(8, (0, modx[2:3]), (1, conv_w_all), (4, conv_b), (5, gn_w), (6, final_norm_w[None]))
    dx1, dp, dconv, dret, op_at, op_b, part_m = _merge(p_ext, rn, rstd, x2, tgt, w_a_all, w_b_all, w_out_all, vecs, seq)

    j4 = jnp.arange(4, dtype=jnp.int32)
    owners = (2 * jnp.bitwise_xor(chip, j4) + ci).astype(jnp.int32)
    owners_sib = (2 * jnp.bitwise_xor(chip, j4) + (1 - ci)).astype(jnp.int32)
    gw_sq = _weight_grad_square(op_at, op_b).reshape(N_DEV, 3 * RSH, D)
    rs_sq_pair = _Exchange("rs_square_pair", [gw_sq], [jax.ShapeDtypeStruct((4, 3 * RSH, D), F32)], 4,
                           _pair_plan(lambda j, chip_, c_: 2 * jnp.bitwise_xor(chip_, j) + (1 - c_)))
    dq, dk, dv, dlg = _retention_backward(p_ext, dret, lg, cos_t, sin_t, seq, ctx_len, rs_sq_pair.token)
    (gw_sq,), (r1_sq,) = rs_sq_pair.wait(dlg)
    own_sq, send_sq = _pair_sum(gw_sq, r1_sq, owners, "pair_sum_square")
    rs_sq_chips = _Exchange("rs_square_chips", [send_sq], [jax.ShapeDtypeStruct((3, 3 * RSH, D), BF16)], 3, _chips_plan)
    dp, part_c = _conv_backward(p_ext, dconv, dp, dq, dk, dv, vecs, seq, rs_sq_chips.token)
    gw_sib = _weight_grad_in(xm_t, dp, owners_sib, "weight_grad_in_sibling", part_c)
    rs_in_pair = _Exchange("rs_in_pair", [gw_sib], [jax.ShapeDtypeStruct((4, D, WSH), F32)], 4,
                           _pair_plan(lambda j, chip_, c_: j))
    gw_own = _weight_grad_in(xm_t, dp, owners, "weight_grad_in_own", rs_in_pair.token)
    _, (r1_in,) = rs_in_pair.wait(gw_own)
    own_in, send_in = _pair_sum(gw_own, r1_in, j4, "pair_sum_in")
    rs_in_chips = _Exchange("rs_in_chips", [send_in], [jax.ShapeDtypeStruct((3, D, WSH), BF16)], 3, _chips_plan)
    grad_x, part_i = _input_backward(dp, w_shards, w_ids, x2, ctx[0], dx1, modx, modc, norm_w, rs_in_chips.token)

    dl = dlg[:, 0, 0:4]
    dlg_row = jnp.pad(dl[:, 0::2].reshape(1, N_HEADS), ((0, 0), (0, D - N_HEADS))) + jnp.pad(
        dl[:, 1::2].reshape(1, N_HEADS), ((0, 0), (N_HEADS, D - 2 * N_HEADS)))
    partials = _rows(16, (0, part_m[0:1]), (1, part_i[2:3]), (2, part_c[3:4]), (3, part_m[2:3]), (4, part_c[0:3]),
                     (7, part_i[0:2]), (9, part_m[1:2]), (10, part_i[3:5]), (13, part_m[3:4]), (14, dlg_row))
    results = {}
    gather_part = _Exchange("gather_partials", [partials], [jax.ShapeDtypeStruct((N_DEV, 16, D), F32)],
                            N_DEV - 1, _gather_plan)
    _, (r2_in,) = rs_in_chips.wait(gather_part.token)
    results["w_in"] = [o[None] for o in _adam_sharded(w_in[0], m_w_in[0], v_w_in[0], own_in, r2_in, "adam_w_in")]
    _, (got,) = gather_part.wait(results["w_in"][1])
    got = lax.dynamic_update_slice(got, partials[None], (me, 0, 0))
    tot = _sum_devices(got, "sum_partials")
    dmodc = tot[10:13].reshape(1, 3 * D)
    dmod16 = _rows(16, (0, got[:, 7:10, :].reshape(N_DEV, 3 * D)), (N_DEV, dmodc))
    dmod16 = lax.dynamic_slice(dmod16, (0, me * ADA_SH), (16, ADA_SH))
    dmodc8 = _rows(8, (0, dmod16[8:9]))
    g_ada_w, d_ada_w, nm_ada_w, nv_ada_w, cctx_part = _ada_backward(
        act16.T, dmod16, dmodc8, ada_w[0], m_ada_w[0], v_ada_w[0])
    gather_cctx = _Exchange("gather_cctx", [cctx_part], [jax.ShapeDtypeStruct((N_DEV, 8, D), F32)],
                            N_DEV - 1, _gather_plan)
    _, (r2_sq,) = rs_sq_chips.wait(gather_cctx.token)
    square = _adam_square([(w_a[0], m_w_a[0], v_w_a[0]), (w_b[0], m_w_b[0], v_w_b[0]),
                           (w_out[0], m_w_out[0], v_w_out[0])], own_sq, r2_sq)
    _, (got_cctx,) = gather_cctx.wait(square[0][1])
    dact_ctx = _sum_devices(lax.dynamic_update_slice(got_cctx, cctx_part[None], (me, 0, 0)), "sum_cctx")

    view = {"c_ctx": (1, D), "final_norm_w": (1, D)}
    given = {"c_ctx": (c_ctx, m_c_ctx, v_c_ctx), "norm_w": (norm_w, m_norm_w, v_norm_w),
             "conv_b": (conv_b, m_conv_b, v_conv_b), "gn_w": (gn_w, m_gn_w, v_gn_w),
             "final_norm_w": (final_norm_w, m_final_norm_w, v_final_norm_w), "ada_b": (ada_b, m_ada_b, v_ada_b),
             "decay_logit": (decay_logit, m_decay_logit, v_decay_logit), "conv_w": (conv_w, m_conv_w, v_conv_w)}
    small_in = [tuple(a.reshape(view.get(name, a.shape)) for a in given[name]) for name in SMALL_PARAMS]
    conv_w_grad = lax.dynamic_slice(tot, (4, me * csh), (3, csh))
    small_out = _adam_small(tot, dact_ctx, conv_w_grad, tot[14, 0:2 * N_HEADS].reshape(2, N_HEADS), small_in)
    results.update({name: [o.reshape(given[name][0].shape) for o in outs_]
                    for name, outs_ in zip(SMALL_PARAMS, small_out)})
    for name, outs_ in zip(("w_a", "w_b", "w_out"), square):
        results[name] = [o[None] for o in outs_]
    results["ada_w"] = [o[None] for o in (g_ada_w, d_ada_w, nm_ada_w, nv_ada_w)]

    order = ("c_ctx", "norm_w", "ada_w", "ada_b", "w_in", "conv_w", "conv_b", "decay_logit", "gn_w",
             "w_a", "w_b", "w_out", "final_norm_w")
    outs = [results[name][kind] for kind in range(4) for name in order]
    return (tot[13, 0], grad_x[None], *outs)
```

```python
import math

import jax
import jax.numpy as jnp
from jax import lax
from jax.experimental import pallas as pl
from jax.experimental.pallas import tpu as pltpu

F32 = jnp.float32
BF16 = jnp.bfloat16
MESH = pl.DeviceIdType.MESH
_HBM_SPEC = pl.BlockSpec(memory_space=pltpu.HBM)
_SEM_SPEC = pl.BlockSpec(memory_space=pltpu.SEMAPHORE)
_EFFECT = pltpu.SideEffectType.DATAFLOW_SIDE_EFFECTING

N_DEV = 8
D = 1024
N_HEADS = 8
DK = 64
DV = 128
GRID_W = 64
ROPE_BASE = 10000.0
EPS = 1e-6
PW = 9 * D
WSH = PW // N_DEV
RSH = D // N_DEV
ADA_SH = 3 * D // N_DEV
TM = 256
RET_C = 256
HALO = 16
LANES = 128
VMEM_LIMIT = 60 * 1024 * 1024

ADAM_LR = 0.001
ADAM_B1 = 0.9
ADAM_B2 = 0.999
ADAM_EPS = 1e-08
ADAM_WD = 0.01
ADAM_STEP = 10

CB_H, CB_BG, CB_CG, CB_ZA, CB_QK, CB_V, CB_ZB, CB_GA, CB_GB = range(9)


def _cparams(**kw):
    return pltpu.CompilerParams(vmem_limit_bytes=VMEM_LIMIT, **kw)


def _dot(a, b):
    return jnp.dot(a, b, preferred_element_type=F32)


def _dot_nt(a, b):
    return lax.dot_general(a, b, (((1,), (1,)), ((), ())), preferred_element_type=F32)


def _dot_tn(a, b):
    return lax.dot_general(a, b, (((0,), (0,)), ((), ())), preferred_element_type=F32)


def _sigmoid(z):
    return 0.5 * jnp.tanh(0.5 * z) + 0.5


def _row_tile(rows):
    return TM if rows % TM == 0 else rows


def _coords():
    return lax.axis_index("x"), lax.axis_index("y"), lax.axis_index("c")


def _flip(v, bit):
    return 1 - v if bit else v


def _all_gather_small(blk, name):
    rows, cols = blk.shape

    def body(x_ref, out_ref, done, send_sems, recv_sems, local_sem):
        x, y, c = _coords()
        me = 4 * x + 2 * y + c
        mine = pltpu.make_async_copy(x_ref, out_ref.at[me], local_sem)
        mine.start()

        def copy(k, slot):
            peer = (_flip(x, k & 4), _flip(y, k & 2), _flip(c, k & 1))
            return pltpu.make_async_remote_copy(
                src_ref=x_ref, dst_ref=out_ref.at[slot], send_sem=send_sems.at[k - 1],
                recv_sem=recv_sems.at[k - 1], device_id=peer, device_id_type=MESH)

        for k in range(1, N_DEV):
            copy(k, me).start()
        for k in range(1, N_DEV):
            copy(k, jnp.bitwise_xor(me, k)).wait_recv()
        for k in range(1, N_DEV):
            copy(k, me).wait_send()
        mine.wait()
        done[...] = jnp.zeros((8, LANES), F32)

    vmem = pl.BlockSpec(memory_space=pltpu.VMEM)
    return pl.pallas_call(
        body, name=name,
        out_shape=[jax.ShapeDtypeStruct((N_DEV, rows, cols), blk.dtype), jax.ShapeDtypeStruct((8, LANES), F32)],
        in_specs=[vmem], out_specs=[vmem, vmem],
        scratch_shapes=[pltpu.SemaphoreType.DMA((N_DEV - 1,)), pltpu.SemaphoreType.DMA((N_DEV - 1,)),
                        pltpu.SemaphoreType.DMA],
    )(blk)


class _Absent:
    at = property(lambda self: self)

    def __getitem__(self, index):
        return self


class _Exchange:
    def __init__(self, name, srcs, land_shapes, n_copies, plan, after=None):
        self.name, self.plan, self.n_copies = name, plan, n_copies
        self.n_src, self.n_land = len(srcs), len(land_shapes)
        hbm = lambda a: pltpu.HBM(a.shape, a.dtype)
        n = self.n_src + self.n_land
        lands = [pltpu.with_memory_space_constraint(lax.empty(s.shape, s.dtype), pltpu.HBM) for s in land_shapes]
        srcs = [pltpu.with_memory_space_constraint(s, pltpu.HBM) for s in srcs]
        extra = [] if after is None else [after]

        def body(*refs):
            send_sems, recv_sems = refs[n + len(extra)], refs[n + len(extra) + 1]
            for cp in self._copies(refs, send_sems, recv_sems):
                cp.start()
            refs[-1][...] = jnp.zeros((8, LANES), F32)

        outs = pl.pallas_call(
            body, name=name + "_start",
            out_shape=(pltpu.SemaphoreType.DMA((n_copies,)), pltpu.SemaphoreType.DMA((n_copies,)),
                       *[hbm(a) for a in srcs], *[hbm(a) for a in lands], jax.ShapeDtypeStruct((8, LANES), F32)),
            in_specs=[_HBM_SPEC] * n + [pl.BlockSpec(memory_space=pl.ANY)] * len(extra),
            out_specs=(_SEM_SPEC, _SEM_SPEC, *[_HBM_SPEC] * n, pl.BlockSpec(memory_space=pltpu.VMEM)),
            input_output_aliases={i: 2 + i for i in range(n)},
            compiler_params=pltpu.CompilerParams(has_side_effects=_EFFECT),
        )(*srcs, *lands, *extra)
        self.send_sems, self.recv_sems = outs[0], outs[1]
        self.buffers = list(outs[2:2 + n])
        self.token = outs[-1]
        self.waited = 0

    def _copies(self, refs, send_sems, recv_sems, lo=0, hi=None):
        src_refs, land_refs = refs[:self.n_src], refs[self.n_src:self.n_src + self.n_land]
        planned = list(enumerate(self.plan(src_refs, land_refs)))[lo:hi]
        return [pltpu.make_async_remote_copy(src_ref=s, dst_ref=d, send_sem=send_sems.at[k], recv_sem=recv_sems.at[k],
                                             device_id=dev, device_id_type=MESH) for k, (s, d, dev) in planned]

    def wait(self, after, count=None, through=None):
        n = self.n_src + self.n_land
        lo = self.waited
        hi = self.n_copies if count is None else lo + count
        self.waited = hi
        through = list(range(n)) if through is None else list(through)
        m = len(through)

        def body(*refs):
            held = [_Absent()] * n
            for pos, ref in zip(through, refs[:m]):
                held[pos] = ref
            for cp in self._copies(held, refs[m], refs[m + 1], lo, hi):
                cp.wait_send()
                cp.wait_recv()

        outs = pl.pallas_call(
            body, name=f"{self.name}_wait{lo}" if lo or hi < self.n_copies else self.name + "_wait",
            out_shape=tuple(pltpu.HBM(self.buffers[p].shape, self.buffers[p].dtype) for p in through),
            in_specs=[_HBM_SPEC] * m + [_SEM_SPEC, _SEM_SPEC, pl.BlockSpec(memory_space=pl.ANY)],
            out_specs=tuple([_HBM_SPEC] * m),
            input_output_aliases={i: i for i in range(m)},
            compiler_params=pltpu.CompilerParams(has_side_effects=_EFFECT),
        )(*[self.buffers[p] for p in through], self.send_sems, self.recv_sems, after)
        for pos, out in zip(through, outs):
            self.buffers[pos] = out
        return list(self.buffers[:self.n_src]), list(self.buffers[self.n_src:])


def _pair_plan(src_index):
    def plan(srcs, lands):
        x, y, c = _coords()
        chip = 2 * x + y
        return [(srcs[0].at[src_index(j, chip, c)], lands[0].at[j], (x, y, 1 - c)) for j in range(4)]
    return plan


def _chips_plan(srcs, lands):
    x, y, c = _coords()
    return [(srcs[0].at[j - 1], lands[0].at[j - 1], (_flip(x, j & 2), _flip(y, j & 1), c)) for j in range(1, 4)]


def _peer(k):
    x, y, c = _coords()
    return _flip(x, k & 4), _flip(y, k & 2), _flip(c, k & 1)


def _send_plan(ks):
    def plan(srcs, lands):
        return [(srcs[0].at[0], lands[0].at[p], _peer(k)) for p, k in enumerate(ks)]
    return plan


def _forward_plan(moves):
    def plan(srcs, lands):
        return [(srcs[0].at[s], srcs[0].at[d], _peer(1)) for s, d in moves]
    return plan


def _gather_plan(srcs, lands):
    x, y, c = _coords()
    me = 4 * x + 2 * y + c
    return [(srcs[a], lands[a].at[me], (_flip(x, k & 4), _flip(y, k & 2), _flip(c, k & 1)))
            for a in range(len(srcs)) for k in range(1, N_DEV)]


def _pair_sum(grad, recv, owners, name):
    _, rows, cols = grad.shape
    tr = rows

    def body(own_ref, g_ref, r_ref, mine_ref, send_ref):
        j = pl.program_id(1)
        total = g_ref[...] + r_ref[...]

        @pl.when(j == 0)
        def _():
            mine_ref[...] = total

        @pl.when(j > 0)
        def _():
            send_ref[...] = total.astype(BF16)

    grid_spec = pltpu.PrefetchScalarGridSpec(
        num_scalar_prefetch=1, grid=(rows // tr, 4),
        in_specs=[pl.BlockSpec((1, tr, cols), lambda i, j, own: (own[j], i, 0)),
                  pl.BlockSpec((1, tr, cols), lambda i, j, own: (j, i, 0))],
        out_specs=[pl.BlockSpec((1, tr, cols), lambda i, j, own: (0, i, 0)),
                   pl.BlockSpec((1, tr, cols), lambda i, j, own: (jnp.maximum(j - 1, 0), i, 0))])
    return pl.pallas_call(
        body, name=name, grid_spec=grid_spec,
        out_shape=[jax.ShapeDtypeStruct((1, rows, cols), F32), jax.ShapeDtypeStruct((3, rows, cols), BF16)],
        compiler_params=_cparams(dimension_semantics=("arbitrary", "arbitrary")),
    )(owners, grad, recv)


def _modulation(c16, ada_w_sh, ada_b_sh, decay_logit):
    def body(c_ref, w_ref, b_ref, dl_ref, mod_ref, act_ref, lg_ref):
        cv = c_ref[...]
        act = cv * _sigmoid(cv)
        act_ref[...] = act
        mod_ref[...] = jnp.dot(act, w_ref[...], preferred_element_type=F32,
                               precision=lax.Precision.HIGHEST) + b_ref[...]
        z = dl_ref[...]
        lg_ref[...] = jnp.minimum(z, 0.0) - jnp.log(1.0 + jnp.exp(-jnp.abs(z)))

    return pl.pallas_call(
        body, name="modulation",
        out_shape=[jax.ShapeDtypeStruct((16, ADA_SH), F32), jax.ShapeDtypeStruct((16, D), F32),
                   jax.ShapeDtypeStruct(decay_logit.shape, F32)],
        compiler_params=_cparams(),
    )(c16, ada_w_sh, ada_b_sh, decay_logit)


def _adam_update(w, g, m, v):
    m2 = ADAM_B1 * m + (1.0 - ADAM_B1) * g
    v2 = ADAM_B2 * v + (1.0 - ADAM_B2) * (g * g)
    m_hat = m2 / (1.0 - ADAM_B1 ** ADAM_STEP)
    v_hat = v2 / (1.0 - ADAM_B2 ** ADAM_STEP)
    delta = -ADAM_LR * (m_hat / (jnp.sqrt(v_hat) + ADAM_EPS) + ADAM_WD * w)
    return delta, m2, v2


def _adam_sharded(w, m, v, own, recv, name):
    rows, cols = w.shape
    tr = _row_tile(rows)

    def body(w_ref, m_ref, v_ref, g0, g1, g2, g3, g_ref, d_ref, m_out, v_out):
        g = ((g0[0] + g1[0].astype(F32)) + g2[0].astype(F32)) + g3[0].astype(F32)
        delta, m2, v2 = _adam_update(w_ref[...], g, m_ref[...], v_ref[...])
        g_ref[...] = g
        d_ref[...] = delta
        m_out[...] = m2
        v_out[...] = v2

    flat = pl.BlockSpec((tr, cols), lambda i: (i, 0))
    part = lambda j: pl.BlockSpec((1, tr, cols), lambda i, j=j: (j, i, 0))
    return pl.pallas_call(
        body, name=name, grid=(rows // tr,),
        in_specs=[flat, flat, flat, part(0), part(0), part(1), part(2)],
        out_specs=[flat] * 4,
        out_shape=[jax.ShapeDtypeStruct((rows, cols), F32)] * 4,
        compiler_params=_cparams(dimension_semantics=("arbitrary",)),
    )(w, m, v, own, recv, recv, recv)


def _ada_backward(act_t, dmod16, dmodc8, ada_w_sh, m, v):
    def body(at_ref, dm_ref, dc_ref, w_ref, m_ref, v_ref, g_ref, d_ref, m_out, v_out, pc_ref):
        g = jnp.dot(at_ref[...], dm_ref[...], preferred_element_type=F32, precision=lax.Precision.HIGHEST)
        w = w_ref[...]
        delta, m2, v2 = _adam_update(w, g, m_ref[...], v_ref[...])
        g_ref[...] = g
        d_ref[...] = delta
        m_out[...] = m2
        v_out[...] = v2
        pc_ref[...] = lax.dot_general(dc_ref[...], w, (((1,), (1,)), ((), ())), preferred_element_type=F32,
                                      precision=lax.Precision.HIGHEST)

    return pl.pallas_call(
        body, name="ada_backward",
        out_shape=[jax.ShapeDtypeStruct((D, ADA_SH), F32)] * 4 + [jax.ShapeDtypeStruct((8, D), F32)],
        compiler_params=_cparams(),
    )(act_t, dmod16, dmodc8, ada_w_sh, m, v)


def _sum_devices(gathered, name):
    _, rows, cols = gathered.shape

    def body(g_ref, o_ref):
        acc = g_ref[0]
        for d in range(1, N_DEV):
            acc = acc + g_ref[d]
        o_ref[...] = acc

    return pl.pallas_call(body, name=name, out_shape=jax.ShapeDtypeStruct((rows, cols), F32),
                          compiler_params=_cparams())(gathered)


SMALL_PARAMS = ("c_ctx", "norm_w", "conv_b", "gn_w", "final_norm_w", "ada_b", "decay_logit", "conv_w")


def _adam_small(tot, dact_ctx, conv_w_grad, decay_grad, params):
    n = len(SMALL_PARAMS)

    def body(tot_ref, dact_ref, cwg_ref, dlg_ref, *refs):
        ins, outs = refs[:3 * n], refs[3 * n:]
        raw = {
            "c_ctx": dact_ref[0:1, :],
            "norm_w": tot_ref[1:2, :], "conv_b": tot_ref[2:3, :], "gn_w": tot_ref[3:4, :],
            "final_norm_w": tot_ref[0:1, :],
            "ada_b": jnp.concatenate([tot_ref[7 + r:8 + r, :] + tot_ref[10 + r:11 + r, :] for r in range(3)], axis=1),
            "decay_logit": dlg_ref[...],
            "conv_w": cwg_ref[...],
        }
        for k, name in enumerate(SMALL_PARAMS):
            lead = (0,) if len(ins[3 * k].shape) == 3 else ()
            w, m, v = (ref[lead + (...,)] for ref in ins[3 * k:3 * k + 3])
            g = raw[name]
            if name == "c_ctx":
                s = _sigmoid(w)
                g = g * (s * (1.0 + w * (1.0 - s)))
            elif name == "decay_logit":
                g = g * (1.0 - _sigmoid(w))
            delta, m2, v2 = _adam_update(w, g, m, v)
            for ref, val in zip(outs[4 * k:4 * k + 4], (g, delta, m2, v2)):
                ref[lead + (...,)] = val

    flat = [a for trio in params for a in trio]
    out_shape = [jax.ShapeDtypeStruct(trio[0].shape, F32) for trio in params for _ in range(4)]
    outs = pl.pallas_call(body, name="adam_small", out_shape=out_shape, compiler_params=_cparams())(
        tot, dact_ctx, conv_w_grad, decay_grad, *flat)
    return [outs[4 * k:4 * k + 4] for k in range(n)]


def _adam_square(params, own, recv):
    def body(own_ref, recv_ref, *refs):
        ins, outs = refs[:9], refs[9:]
        for k in range(3):
            rows = slice(k * RSH, (k + 1) * RSH)
            g = own_ref[0, rows, :]
            for j in range(3):
                g = g + recv_ref[j, rows, :].astype(F32)
            delta, m2, v2 = _adam_update(ins[3 * k][...], g, ins[3 * k + 1][...], ins[3 * k + 2][...])
            for ref, val in zip(outs[4 * k:4 * k + 4], (g, delta, m2, v2)):
                ref[...] = val

    flat = [a for trio in params for a in trio]
    outs = pl.pallas_call(body, name="adam_square", out_shape=[jax.ShapeDtypeStruct((RSH, D), F32)] * 12,
                          compiler_params=_cparams())(own, recv, *flat)
    return [outs[4 * k:4 * k + 4] for k in range(3)]


def _rope(t, cos, sin):
    lane = lax.broadcasted_iota(jnp.int32, (1, LANES), 1)
    first_half = jnp.bitwise_and(lane, DK // 2) == 0
    partner = jnp.where(first_half, pltpu.roll(t, LANES - DK // 2, 1), pltpu.roll(t, DK // 2, 1))
    return t * cos + partner * sin


class _Shards:
    def __init__(self, arrays, picks):
        self.arrays, self.picks = list(arrays), list(picks)

    def __add__(self, other):
        shift = len(self.arrays)
        return _Shards(self.arrays + other.arrays, self.picks + [(a + shift, s) for a, s in other.picks])


def _load_w_in(step, ids_ref, shards, w_refs, w_vmem, sems):
    @pl.when(step == 0)
    def _():
        copies = []
        for p, (a, slot) in enumerate(shards.picks):
            col = pl.multiple_of(ids_ref[p] * WSH, LANES)
            copies.append(pltpu.make_async_copy(w_refs[a].at[slot], w_vmem.at[:, pl.ds(col, WSH)], sems.at[p]))
        for cp in copies:
            cp.start()
        for cp in copies:
            cp.wait()


def _project_columns(xmb, w_ref, shard, cos, sin, p_ref):
    width = 2 * WSH
    q_lo, k_lo, k_hi = CB_QK * D, CB_QK * D + N_HEADS * DK, CB_V * D
    q_pair, k_pair = q_lo // width, k_lo // width
    assert (q_pair + 1) * width == k_lo and k_pair * width == k_lo and k_hi <= (k_pair + 1) * width
    mxu_cols = 2 * LANES

    def columns(rot_lo, rot_hi, scale):
        for a in range(0, width, mxu_cols):
            acc = _dot(xmb, w_ref[:, a:a + mxu_cols])
            for col in range(a, a + mxu_cols, LANES):
                piece = acc[:, col - a:col - a + LANES]
                if rot_lo <= col < rot_hi:
                    piece = _rope(piece if scale == 1.0 else piece * scale, cos, sin)
                p_ref[:, col:col + LANES] = piece.astype(BF16)

    @pl.when(shard == q_pair)
    def _():
        columns(q_lo - q_pair * width, width, 1.0)

    @pl.when(shard == k_pair)
    def _():
        columns(0, k_hi - k_lo, DK ** -0.5)

    @pl.when(jnp.logical_and(shard != q_pair, shard != k_pair))
    def _():
        columns(0, 0, 1.0)


_AFTER_SPEC = pl.BlockSpec(memory_space=pl.ANY)


def _prenorm(x, ctx, modx, modc, norm_w, after):
    tp = 4 * TM
    seq, ctx_len = x.shape[0], ctx.shape[0]
    n_x_tiles = seq // tp
    lext = seq + ctx_len
    assert seq % tp == 0 and ctx_len <= tp and tp % GRID_W == 0 and tp // GRID_W <= 16
    grid_rows = tp // GRID_W

    def body(x_ref, ctx_ref, mx_ref, mc_ref, nw_ref, after_ref, xm_ref, xmt_ref, cos_ref, sin_ref, col_cos, col_sin):
        del after_ref
        i = pl.program_id(0)
        is_ctx = i >= n_x_tiles
        ctx_rows = jnp.concatenate([ctx_ref[...], jnp.zeros((tp - ctx_len, D), F32)], axis=0)
        x = jnp.where(is_ctx, ctx_rows, x_ref[...])
        r = lax.rsqrt(jnp.mean(x * x, axis=-1, keepdims=True) + EPS)
        sh = jnp.where(is_ctx, mc_ref[0:1, :], mx_ref[0:1, :])
        sc = jnp.where(is_ctx, mc_ref[1:2, :], mx_ref[1:2, :])
        xm = (x * r * nw_ref[...]) * (1.0 + sc) + sh
        xm_ref[...] = xm.astype(BF16)
        xmt_ref[...] = xm.T.astype(BF16)
        lane = lax.broadcasted_iota(jnp.int32, (1, LANES), 1)
        n_freq = DK // 4
        inv = jnp.exp((lane % n_freq).astype(F32) * (-math.log(ROPE_BASE) / n_freq))
        by_column = (lane % (2 * n_freq)) >= n_freq
        sign = jnp.where((lane % DK) < DK // 2, -1.0, 1.0)

        @pl.when(i == 0)
        def _():
            col = (lax.broadcasted_iota(jnp.int32, (tp, 1), 0) % GRID_W).astype(F32) * inv
            col_cos[...] = jnp.cos(col)
            col_sin[...] = jnp.sin(col) * sign

        row = (i * grid_rows + lax.broadcasted_iota(jnp.int32, (16, 1), 0)).astype(F32) * inv
        row_cos, row_sin = jnp.cos(row), jnp.sin(row) * sign
        tall = lambda t: jnp.concatenate(
            [jnp.broadcast_to(t[g:g + 1, :], (GRID_W, LANES)) for g in range(grid_rows)], axis=0)
        cos_ref[...] = jnp.where(is_ctx, 1.0, jnp.where(by_column, col_cos[...], tall(row_cos)))
        sin_ref[...] = jnp.where(is_ctx, 0.0, jnp.where(by_column, col_sin[...], tall(row_sin)))

    full = lambda a: pl.BlockSpec(a.shape, lambda i: (0,) * a.ndim)
    row = lambda width: pl.BlockSpec((tp, width), lambda i: (i, 0))
    return pl.pallas_call(
        body, name="prenorm", grid=(n_x_tiles + 1,),
        in_specs=[pl.BlockSpec((tp, D), lambda i: (jnp.minimum(i, n_x_tiles - 1), 0)), full(ctx),
                  full(modx), full(modc), full(norm_w), _AFTER_SPEC],
        out_specs=[row(D), pl.BlockSpec((D, tp), lambda i: (0, i)), row(LANES), row(LANES)],
        out_shape=[jax.ShapeDtypeStruct((lext, D), BF16), jax.ShapeDtypeStruct((D, lext), BF16),
                   jax.ShapeDtypeStruct((lext, LANES), F32), jax.ShapeDtypeStruct((lext, LANES), F32)],
        scratch_shapes=[pltpu.VMEM((tp, LANES), F32), pltpu.VMEM((tp, LANES), F32)],
        compiler_params=_cparams(dimension_semantics=("arbitrary",)),
    )(x, ctx, modx, modc, norm_w, after)


def _in_projection(p_ext, xm, cos_t, sin_t, w, pairs, me, name):
    lext = xm.shape[0]
    tr = lext // 4
    meta = []
    for (slot_a, k_a), (slot_b, k_b) in pairs:
        assert k_a ^ k_b == 1
        dev_a = jnp.bitwise_xor(me, k_a)
        a_low = dev_a % 2 == 0
        meta += [jnp.where(a_low, slot_a, slot_b), jnp.where(a_low, slot_b, slot_a), dev_a // 2]
    meta = jnp.stack(meta).astype(jnp.int32)

    def body(meta_ref, xm_ref, cos_ref, sin_ref, w_hbm, *rest):
        p_ref, w_ref, sems = rest[-3:]
        j, i = pl.program_id(0), pl.program_id(1)

        @pl.when(i == 0)
        def _():
            copies = [pltpu.make_async_copy(w_hbm.at[meta_ref[3 * j + h]], w_ref.at[:, h * WSH:(h + 1) * WSH],
                                            sems.at[h]) for h in range(2)]
            for cp in copies:
                cp.start()
            for cp in copies:
                cp.wait()

        _project_columns(xm_ref[...], w_ref, meta_ref[3 * j + 2], cos_ref[...], sin_ref[...], p_ref)

    row = lambda width: pl.BlockSpec((tr, width), lambda j, i, meta_: (i, 0))
    grid_spec = pltpu.PrefetchScalarGridSpec(
        num_scalar_prefetch=1, grid=(len(pairs), lext // tr),
        in_specs=[row(D), row(LANES), row(LANES), pl.BlockSpec(memory_space=pl.ANY)]
        + ([] if p_ext is None else [pl.BlockSpec(memory_space=pl.ANY)]),
        out_specs=pl.BlockSpec((tr, 2 * WSH), lambda j, i, meta_: (i, meta_[3 * j + 2])),
        scratch_shapes=[pltpu.VMEM((D, 2 * WSH), BF16), pltpu.SemaphoreType.DMA((2,))])
    return pl.pallas_call(
        body, name=name, grid_spec=grid_spec,
        out_shape=jax.ShapeDtypeStruct((lext, PW), BF16),
        input_output_aliases={} if p_ext is None else {5: 0},
        compiler_params=_cparams(dimension_semantics=("arbitrary", "arbitrary")),
    )(meta, xm, cos_t, sin_t, w, *([] if p_ext is None else [p_ext]))


def _decay_tables(lgf, lgb, n):
    i = lax.broadcasted_iota(jnp.int32, (n, 1), 0).astype(F32)
    return dict(i=i, k_f=jnp.exp(lgf * (n - 1.0 - i)), k_b=jnp.exp(lgb * i),
                q_f=jnp.exp(lgf * (i + 1.0)), q_b=jnp.exp(lgb * (n - i)))


def _decay_matrix(lgf, lgb, n):
    ii = lax.broadcasted_iota(jnp.int32, (n, n), 0)
    jj = lax.broadcasted_iota(jnp.int32, (n, n), 1)
    diff = (ii - jj).astype(F32)
    low = jnp.exp(lgf * jnp.maximum(diff, 0.0))
    up = jnp.exp(lgb * jnp.maximum(-diff, 0.0))
    return jnp.where(diff > 0, low, jnp.where(diff < 0, up, 2.0)), diff


def _cat_lanes(a, b):
    return jnp.concatenate([a.astype(BF16), b.astype(BF16)], axis=1)


def _retention_forward(p_ext, lg, seq, ctx_len):
    lext = seq + ctx_len
    n_chunks = seq // RET_C
    C = RET_C

    def body(lg_ref, q_ref, k_ref, v_ref, rn_ref, rstd_ref, kv_scr, sf_scr, sb_scr):
        pair = pl.program_id(0)
        lane = lax.broadcasted_iota(jnp.int32, (1, LANES), 1)
        heads = range(2)
        hmask = [(lane // DK == hh).astype(F32) for hh in heads]
        lgf = [lg_ref[0, 2 * pair + hh] for hh in heads]
        lgb = [lg_ref[1, 2 * pair + hh] for hh in heads]
        vs = [slice(hh * DV, (hh + 1) * DV) for hh in heads]
        t = [_decay_tables(lgf[hh], lgb[hh], C) for hh in heads]
        kc_all = k_ref[seq:lext, :].astype(F32)
        s0 = []
        for hh in heads:
            tc = _decay_tables(lgf[hh], lgb[hh], ctx_len)
            kc = kc_all * hmask[hh]
            s0.append(_dot_tn(_cat_lanes(kc * tc["k_f"], kc * tc["k_b"]), v_ref[seq:lext, vs[hh]]))

        def increments(c, carry):
            rows = pl.ds(pl.multiple_of(c * C, C), C)
            k_all = k_ref[rows, :].astype(F32)
            for hh in heads:
                k = k_all * hmask[hh]
                kv_scr[hh, c] = _dot_tn(_cat_lanes(k * t[hh]["k_f"], k * t[hh]["k_b"]), v_ref[rows, vs[hh]])
            return carry

        lax.fori_loop(0, n_chunks, increments, 0, unroll=2)
        gf_c = [jnp.exp(lgf[hh] * C) for hh in heads]
        gb_c = [jnp.exp(lgb[hh] * C) for hh in heads]

        def scan_f(c, s):
            for hh in heads:
                sf_scr[hh, c] = s[hh]
            return tuple(gf_c[hh] * s[hh] + kv_scr[hh, c, 0:LANES, :] for hh in heads)

        def scan_b(n, s):
            c = n_chunks - 1 - n
            for hh in heads:
                sb_scr[hh, c] = s[hh]
            return tuple(gb_c[hh] * s[hh] + kv_scr[hh, c, LANES:2 * LANES, :] for hh in heads)

        lax.fori_loop(0, n_chunks, scan_f, tuple(s0[hh][0:LANES] for hh in heads))
        lax.fori_loop(0, n_chunks, scan_b, tuple(s0[hh][LANES:2 * LANES] for hh in heads))
        dmat = [_decay_matrix(lgf[hh], lgb[hh], C)[0] for hh in heads]

        def outputs(c, carry):
            rows = pl.ds(pl.multiple_of(c * C, C), C)
            q_all = q_ref[rows, :].astype(F32)
            k = k_ref[rows, :]
            for hh in heads:
                q = q_all * hmask[hh]
                v = v_ref[rows, vs[hh]]
                s = _dot_nt(q.astype(BF16), k)
                inner = _dot((s * dmat[hh]).astype(BF16), v)
                states = jnp.concatenate([sf_scr[hh, c], sb_scr[hh, c]], axis=0).astype(BF16)
                ret = inner + _dot(_cat_lanes(q * t[hh]["q_f"], q * t[hh]["q_b"]), states)
                mu = jnp.mean(ret, axis=-1, keepdims=True)
                cen = ret - mu
                rstd = lax.rsqrt(jnp.mean(cen * cen, axis=-1, keepdims=True) + EPS)
                rn_ref[rows, vs[hh]] = cen * rstd
                rstd_ref[rows, vs[hh]] = jnp.broadcast_to(rstd, (C, DV))
            return carry

        lax.fori_loop(0, n_chunks, outputs, 0, unroll=2)

    qk0 = CB_QK * D // LANES
    return pl.pallas_call(
        body, name="retention_forward", grid=(N_HEADS // 2,),
        in_specs=[pl.BlockSpec(memory_space=pltpu.SMEM),
                  pl.BlockSpec((lext, LANES), lambda g: (0, qk0 + g)),
                  pl.BlockSpec((lext, LANES), lambda g: (0, qk0 + N_HEADS // 2 + g)),
                  pl.BlockSpec((lext, 2 * DV), lambda g: (0, CB_V * D // (2 * DV) + g))],
        out_specs=[pl.BlockSpec((seq, 2 * DV), lambda g: (0, g))] * 2,
        out_shape=[jax.ShapeDtypeStruct((seq, D), F32)] * 2,
        scratch_shapes=[pltpu.VMEM((2, n_chunks, 2 * LANES, DV), F32), pltpu.VMEM((2, n_chunks, LANES, DV), F32),
                        pltpu.VMEM((2, n_chunks, LANES, DV), F32)],
        compiler_params=_cparams(dimension_semantics=("arbitrary",)),
    )(lg, p_ext, p_ext, p_ext)


def _retention_backward(p_ext, dret, lg, cos_t, sin_t, seq, ctx_len, after):
    lext = seq + ctx_len
    n_chunks = seq // RET_C
    C = RET_C

    def body(lg_ref, q_ref, k_ref, v_ref, do_ref, cos_ref, sin_ref, after_ref, dq_ref, dk_ref, dv_ref, dlg_ref,
             kv_scr, g_scr, sf_scr, sb_scr, gfn_scr, gbp_scr):
        pair = pl.program_id(0)
        lane = lax.broadcasted_iota(jnp.int32, (1, LANES), 1)
        heads = range(2)
        hmask = [(lane // DK == hh).astype(F32) for hh in heads]
        lgf = [lg_ref[0, 2 * pair + hh] for hh in heads]
        lgb = [lg_ref[1, 2 * pair + hh] for hh in heads]
        vs = [slice(hh * DV, (hh + 1) * DV) for hh in heads]
        t = [_decay_tables(lgf[hh], lgb[hh], C) for hh in heads]
        tc = [_decay_tables(lgf[hh], lgb[hh], ctx_len) for hh in heads]
        kc_all = k_ref[seq:lext, :].astype(F32)
        kc = [kc_all * hmask[hh] for hh in heads]
        vc = [v_ref[seq:lext, vs[hh]] for hh in heads]
        kc_cat = [_cat_lanes(kc[hh] * tc[hh]["k_f"], kc[hh] * tc[hh]["k_b"]) for hh in heads]
        s0 = [_dot_tn(kc_cat[hh], vc[hh]) for hh in heads]

        def increments(c, carry):
            rows = pl.ds(pl.multiple_of(c * C, C), C)
            k_all = k_ref[rows, :].astype(F32)
            q_all = q_ref[rows, :].astype(F32)
            for hh in heads:
                k, q = k_all * hmask[hh], q_all * hmask[hh]
                kv_scr[hh, c] = _dot_tn(_cat_lanes(k * t[hh]["k_f"], k * t[hh]["k_b"]), v_ref[rows, vs[hh]])
                g_scr[hh, c] = _dot_tn(_cat_lanes(q * t[hh]["q_f"], q * t[hh]["q_b"]), do_ref[rows, vs[hh]])
            return carry

        lax.fori_loop(0, n_chunks, increments, 0, unroll=2)
        gf_c = [jnp.exp(lgf[hh] * C) for hh in heads]
        gb_c = [jnp.exp(lgb[hh] * C) for hh in heads]

        def scan_f(c, s):
            for hh in heads:
                sf_scr[hh, c] = s[hh]
            return tuple(gf_c[hh] * s[hh] + kv_scr[hh, c, 0:LANES, :] for hh in heads)

        def scan_b(n, s):
            c = n_chunks - 1 - n
            for hh in heads:
                sb_scr[hh, c] = s[hh]
            return tuple(gb_c[hh] * s[hh] + kv_scr[hh, c, LANES:2 * LANES, :] for hh in heads)

        def scan_gf(n, carry):
            c = n_chunks - 1 - n
            for hh in heads:
                gfn_scr[hh, c] = carry[hh]
            return tuple(g_scr[hh, c, 0:LANES, :] + gf_c[hh] * carry[hh] for hh in heads)

        def scan_gb(c, carry):
            for hh in heads:
                gbp_scr[hh, c] = carry[hh]
            return tuple(g_scr[hh, c, LANES:2 * LANES, :] + gb_c[hh] * carry[hh] for hh in heads)

        lax.fori_loop(0, n_chunks, scan_f, tuple(s0[hh][0:LANES] for hh in heads))
        lax.fori_loop(0, n_chunks, scan_b, tuple(s0[hh][LANES:2 * LANES] for hh in heads))
        zero_state = jnp.zeros((LANES, DV), F32)
        gf_first = lax.fori_loop(0, n_chunks, scan_gf, (zero_state, zero_state))
        gb_last = lax.fori_loop(0, n_chunks, scan_gb, (zero_state, zero_state))

        dmat, w_f, w_b = [], [], []
        for hh in heads:
            dm, diff = _decay_matrix(lgf[hh], lgb[hh], C)
            dmat.append(dm)
            w_f.append(jnp.where(diff > 0, diff * dm, 0.0))
            w_b.append(jnp.where(diff < 0, -diff * dm, 0.0))

        def total(a):
            rows_, width = a.shape
            part = jnp.sum(a.reshape(rows_ // 8, 8, width), axis=0)
            return part[:, 0:LANES] + part[:, LANES:2 * LANES] if width == 2 * LANES else part

        def chunk_head(hh, c, rows, q_all, k_all, dlgf, dlgb):
            th = t[hh]
            qm = q_all * hmask[hh]
            km = k_all * hmask[hh]
            qb, kb = qm.astype(BF16), km.astype(BF16)
            v = v_ref[rows, vs[hh]]
            do = do_ref[rows, vs[hh]]
            s = _dot_nt(qb, kb)
            dsd = _dot_nt(do, v)
            ds = (dsd * dmat[hh]).astype(BF16)
            dq_in = _dot(ds, kb)
            dk_in = _dot_tn(ds, qb)
            dv_in = _dot_tn((s * dmat[hh]).astype(BF16), do)
            prod = s * dsd
            dlgf = dlgf + total(prod * w_f[hh])
            dlgb = dlgb + total(prod * w_b[hh])
            sf, sb = sf_scr[hh, c], sb_scr[hh, c]
            states = jnp.concatenate([sf, sb], axis=0).astype(BF16)
            dqc = _dot_nt(do, states)
            dqf = dqc[:, 0:LANES] * th["q_f"]
            dqb = dqc[:, LANES:2 * LANES] * th["q_b"]
            dq = (dq_in + dqf + dqb) * hmask[hh]
            dlgf = dlgf + total((th["i"] + 1.0) * (qm * dqf))
            dlgb = dlgb + total((C - th["i"]) * (qm * dqb))
            gfn, gbp = gfn_scr[hh, c], gbp_scr[hh, c]
            gstates = jnp.concatenate([gfn, gbp], axis=0).astype(BF16)
            dkc = _dot_nt(v, gstates)
            dkf = dkc[:, 0:LANES] * th["k_f"]
            dkb = dkc[:, LANES:2 * LANES] * th["k_b"]
            dk = dk_in + dkf + dkb
            dlgf = dlgf + total((C - 1.0 - th["i"]) * (km * dkf)) + C * gf_c[hh] * total(gfn * sf)
            dlgb = dlgb + total(th["i"] * (km * dkb)) + C * gb_c[hh] * total(gbp * sb)
            dv = dv_in + _dot(_cat_lanes(km * th["k_f"], km * th["k_b"]), gstates)
            dv_ref[rows, vs[hh]] = dv.astype(BF16)
            return dq, dk, dlgf, dlgb

        def chunk(c, carry):
            rows = pl.ds(pl.multiple_of(c * C, C), C)
            q_all = q_ref[rows, :].astype(F32)
            k_all = k_ref[rows, :].astype(F32)
            dq0, dk0, f0, b0 = chunk_head(0, c, rows, q_all, k_all, carry[0], carry[1])
            dq1, dk1, f1, b1 = chunk_head(1, c, rows, q_all, k_all, carry[2], carry[3])
            cos, sin = cos_ref[rows, :], sin_ref[rows, :]
            dq_ref[rows, :] = _rope(dq0 + dq1, cos, -sin).astype(BF16)
            dk_ref[rows, :] = (_rope(dk0 + dk1, cos, -sin) * (DK ** -0.5)).astype(BF16)
            return f0, b0, f1, b1

        zero = jnp.zeros((8, LANES), F32)
        sums = lax.fori_loop(0, n_chunks, chunk, (zero, zero, zero, zero), unroll=2)

        dlg = []
        dk_ctx = jnp.zeros((ctx_len, LANES), F32)
        for hh in heads:
            g0 = jnp.concatenate([gf_first[hh], gb_last[hh]], axis=0).astype(BF16)
            dkcc = _dot_nt(vc[hh], g0)
            dkcf = dkcc[:, 0:LANES] * tc[hh]["k_f"]
            dkcb = dkcc[:, LANES:2 * LANES] * tc[hh]["k_b"]
            dlgf = sums[2 * hh] + total((ctx_len - 1.0 - tc[hh]["i"]) * (kc[hh] * dkcf))
            dlgb = sums[2 * hh + 1] + total(tc[hh]["i"] * (kc[hh] * dkcb))
            dk_ctx = dk_ctx + (dkcf + dkcb) * (DK ** -0.5)
            dv_ref[seq:lext, vs[hh]] = _dot(kc_cat[hh], g0).astype(BF16)
            dlg += [jnp.sum(jnp.sum(a, axis=1, keepdims=True), axis=0, keepdims=True) for a in (dlgf, dlgb)]
        dk_ref[seq:lext, :] = dk_ctx.astype(BF16)
        dq_ref[seq:lext, :] = jnp.zeros((ctx_len, LANES), BF16)

        lane8 = lax.broadcasted_iota(jnp.int32, (8, LANES), 1)
        out = jnp.zeros((8, LANES), F32)
        for n, val in enumerate(dlg):
            out = jnp.where(lane8 == n, val, out)
        dlg_ref[0] = out

    qk0 = CB_QK * D // LANES
    q_spec = pl.BlockSpec((lext, LANES), lambda g: (0, qk0 + g))
    k_spec = pl.BlockSpec((lext, LANES), lambda g: (0, qk0 + N_HEADS // 2 + g))
    v_spec = pl.BlockSpec((lext, 2 * DV), lambda g: (0, CB_V * D // (2 * DV) + g))
    table = pl.BlockSpec((lext, LANES), lambda g: (0, 0))
    state = pltpu.VMEM((2, n_chunks, LANES, DV), F32)
    return pl.pallas_call(
        body, name="retention_backward", grid=(N_HEADS // 2,),
        in_specs=[pl.BlockSpec(memory_space=pltpu.SMEM), q_spec, k_spec, v_spec,
                  pl.BlockSpec((seq, 2 * DV), lambda g: (0, g)), table, table, _AFTER_SPEC],
        out_specs=[pl.BlockSpec((lext, LANES), lambda g: (0, g)), pl.BlockSpec((lext, LANES), lambda g: (0, g)),
                   pl.BlockSpec((lext, 2 * DV), lambda g: (0, g)), pl.BlockSpec((1, 8, LANES), lambda g: (g, 0, 0))],
        out_shape=[jax.ShapeDtypeStruct((lext, N_HEADS * DK), BF16), jax.ShapeDtypeStruct((lext, N_HEADS * DK), BF16),
                   jax.ShapeDtypeStruct((lext, D), BF16), jax.ShapeDtypeStruct((N_HEADS // 2, 8, LANES), F32)],
        scratch_shapes=[pltpu.VMEM((2, n_chunks, 2 * LANES, DV), F32), pltpu.VMEM((2, n_chunks, 2 * LANES, DV), F32),
                        state, state, state, state],
        compiler_params=_cparams(dimension_semantics=("arbitrary",)),
    )(lg, p_ext, p_ext, p_ext, dret, cos_t, sin_t, after)


class _ColumnWriter:
    def __init__(self, dst_hbm, stage, sems, pieces, n_steps):
        self.dst, self.stage, self.sems, self.pieces, self.n_steps = dst_hbm, stage, sems, pieces, n_steps

    def _copies(self, slot, tile):
        rows = pl.ds(pl.multiple_of(tile * TM, TM), TM)
        return [pltpu.make_async_copy(self.stage.at[slot, :, pl.ds(src, width)], self.dst.at[rows, pl.ds(dst, width)],
                                      self.sems.at[slot, k]) for k, (src, dst, width) in enumerate(self.pieces)]

    def slot_for(self, step):
        slot = step % 2

        @pl.when(step >= 2)
        def _():
            for cp in self._copies(slot, step - 2):
                cp.wait()

        return slot

    def send(self, step, slot):
        for cp in self._copies(slot, step):
            cp.start()

        @pl.when(step == self.n_steps - 1)
        def _():
            if self.n_steps >= 2:
                for cp in self._copies(1 - slot, step - 1):
                    cp.wait()
            for cp in self._copies(slot, step):
                cp.wait()


def _merge(p_ext, rn, rstd, x, target, w_a, w_b, w_out, vecs, seq):
    n_tiles = seq // TM
    hb = TM // HALO
    dp_pieces = [(0, CB_BG * D, D), (D, CB_ZA * D, D), (2 * D, CB_ZB * D, 3 * D)]

    def body(h_ref, bg_ref, cg_ref, za_ref, zb_ref, ga_ref, gb_ref, hp_ref, hn_ref, cp_ref, cn_ref,
             rn_ref, rstd_ref, x_ref, t_ref, wa_ref, wb_ref, wo_ref, vec_ref,
             dx1_ref, dp_hbm, dconv_ref, dret_ref, at_ref, b_ref, part_ref, stage, dp_sems):
        i = pl.program_id(0)
        writer = _ColumnWriter(dp_hbm, stage, dp_sems, dp_pieces, n_tiles)
        slot = writer.slot_for(i)
        dpa_ref = stage.at[slot]
        f = lambda ref: ref[...].astype(F32)
        h, bg, cg, za, zb, ga, gb = f(h_ref), f(bg_ref), f(cg_ref), f(za_ref), f(zb_ref), f(ga_ref), f(gb_ref)
        gx, w0, w1, w2 = vec_ref[0:1, :], vec_ref[1:2, :], vec_ref[2:3, :], vec_ref[3:4, :]
        cb, gnw, fw = vec_ref[4:5, :], vec_ref[5:6, :], vec_ref[6:7, :]
        u = cg * h
        row = lax.broadcasted_iota(jnp.int32, (TM, 1), 0)
        u_prev = (f(cp_ref) * f(hp_ref))[HALO - 1:HALO, :]
        u_next = (f(cn_ref) * f(hn_ref))[0:1, :]
        u_prev = jnp.where(i == 0, 0.0, u_prev)
        u_next = jnp.where(i == n_tiles - 1, 0.0, u_next)
        u_up = jnp.where(row == 0, u_prev, pltpu.roll(u, 1, 0))
        u_dn = jnp.where(row == TM - 1, u_next, pltpu.roll(u, TM - 1, 0))
        conv = w0 * u_up + w1 * u + w2 * u_dn + cb
        sza = _sigmoid(za)
        silu_za = za * sza
        a_act = silu_za * bg * conv
        rn = rn_ref[...]
        szb = _sigmoid(zb)
        silu_zb = zb * szb
        rg = rn * gnw
        b_act = silu_zb * rg
        y_a = _dot(a_act.astype(BF16), wa_ref[...])
        y_b = _dot(b_act.astype(BF16), wb_ref[...])
        sga, sgb = _sigmoid(ga), _sigmoid(gb)
        mix = sga * y_a + sgb * y_b
        y = _dot(mix.astype(BF16), wo_ref[...])
        x1 = x_ref[...] + gx * y
        r1 = lax.rsqrt(jnp.mean(x1 * x1, axis=-1, keepdims=True) + EPS)
        xh1 = x1 * r1
        err = xh1 * fw - t_ref[...]
        loss = jnp.sum(jnp.sum(err * err, axis=1, keepdims=True), axis=0, keepdims=True) * (0.5 / D)
        dxh = err * (fw * (1.0 / D))
        dx1 = r1 * (dxh - xh1 * jnp.mean(dxh * xh1, axis=-1, keepdims=True))
        dx1_ref[...] = dx1
        dy = (dx1 * gx).astype(BF16)
        dmix = _dot_nt(dy, wo_ref[...])
        dya_f, dyb_f = dmix * sga, dmix * sgb
        dya, dyb = dya_f.astype(BF16), dyb_f.astype(BF16)
        da = _dot_nt(dya, wa_ref[...])
        db = _dot_nt(dyb, wb_ref[...])
        da_silu = da * silu_za
        dpa_ref[:, 0:D] = (da_silu * conv).astype(BF16)
        dpa_ref[:, D:2 * D] = (da * bg * conv * (sza + silu_za * (1.0 - sza))).astype(BF16)
        dpa_ref[:, 2 * D:3 * D] = (db * rg * (szb + silu_zb * (1.0 - szb))).astype(BF16)
        dpa_ref[:, 3 * D:4 * D] = (dya_f * y_a * (1.0 - sga)).astype(BF16)
        dpa_ref[:, 4 * D:5 * D] = (dyb_f * y_b * (1.0 - sgb)).astype(BF16)
        dconv_ref[...] = (da_silu * bg).astype(BF16)
        drn_n = db * silu_zb
        drn = drn_n * gnw
        rstd = rstd_ref[...]
        for hd in range(N_HEADS):
            sl = slice(hd * DV, (hd + 1) * DV)
            dh_, rh = drn[:, sl], rn[:, sl]
            m1 = jnp.mean(dh_, axis=-1, keepdims=True)
            m2 = jnp.mean(dh_ * rh, axis=-1, keepdims=True)
            dret_ref[:, sl] = (rstd[:, sl] * (dh_ - m1 - rh * m2)).astype(BF16)
        at_ref[0] = a_act.T.astype(BF16)
        at_ref[1] = b_act.T.astype(BF16)
        at_ref[2] = mix.T.astype(BF16)
        b_ref[0] = dya
        b_ref[1] = dyb
        b_ref[2] = dy

        @pl.when(i == 0)
        def _():
            part_ref[...] = jnp.zeros((8, D), F32)

        part_ref[0:1, :] += jnp.sum(err * xh1, axis=0, keepdims=True) * (1.0 / D)
        part_ref[1:2, :] += jnp.sum(dx1 * y, axis=0, keepdims=True)
        part_ref[2:3, :] += jnp.sum(drn_n * rn, axis=0, keepdims=True)
        part_ref[3:4, :] += jnp.broadcast_to(loss, (1, D))
        writer.send(i, slot)

    col = lambda cb_: pl.BlockSpec((TM, D), lambda i, cb_=cb_: (i, cb_))
    prev = lambda cb_: pl.BlockSpec((HALO, D), lambda i, cb_=cb_: (jnp.maximum(i * hb - 1, 0), cb_))
    nxt = lambda cb_: pl.BlockSpec((HALO, D), lambda i, cb_=cb_: (jnp.minimum((i + 1) * hb, n_tiles * hb - 1), cb_))
    tile = pl.BlockSpec((TM, D), lambda i: (i, 0))
    full = lambda a: pl.BlockSpec(a.shape, lambda i: (0,) * a.ndim, pipeline_mode=pl.Buffered(1))
    return pl.pallas_call(
        body, name="merge", grid=(n_tiles,),
        in_specs=[col(CB_H), col(CB_BG), col(CB_CG), col(CB_ZA), col(CB_ZB), col(CB_GA), col(CB_GB),
                  prev(CB_H), nxt(CB_H), prev(CB_CG), nxt(CB_CG),
                  tile, tile, tile, tile, full(w_a), full(w_b), full(w_out), full(vecs)],
        out_specs=[tile, pl.BlockSpec(memory_space=pl.ANY), tile, tile,
                   pl.BlockSpec((3, D, TM), lambda i: (0, 0, i)), pl.BlockSpec((3, TM, D), lambda i: (0, i, 0)),
                   pl.BlockSpec((8, D), lambda i: (0, 0))],
        out_shape=[jax.ShapeDtypeStruct((seq, D), F32), jax.ShapeDtypeStruct((p_ext.shape[0], PW), BF16),
                   jax.ShapeDtypeStruct((seq, D), BF16), jax.ShapeDtypeStruct((seq, D), BF16),
                   jax.ShapeDtypeStruct((3, D, seq), BF16), jax.ShapeDtypeStruct((3, seq, D), BF16),
                   jax.ShapeDtypeStruct((8, D), F32)],
        scratch_shapes=[pltpu.VMEM((2, TM, 5 * D), BF16), pltpu.SemaphoreType.DMA((2, len(dp_pieces)))],
        compiler_params=_cparams(dimension_semantics=("arbitrary",)),
    )(p_ext, p_ext, p_ext, p_ext, p_ext, p_ext, p_ext, p_ext, p_ext, p_ext, p_ext,
      rn, rstd, x, target, w_a, w_b, w_out, vecs)


def _conv_backward(p_ext, dconv, dp, dq, dk, dv, vecs, seq, after):
    n_tiles = seq // TM
    hb = TM // HALO
    qk = N_HEADS * DK
    pieces = [(0, CB_H * D, D), (D, CB_CG * D, D), (2 * D, CB_QK * D, 2 * D)]
    zero_pieces = [(CB_BG * D, D), (CB_ZA * D, D), (CB_ZB * D, 3 * D)]

    def body(h_ref, cg_ref, dc_ref, dcp_ref, dcn_ref, dq_ref, dk_ref, dv_ref, vec_ref, dp_in, after_ref, dp_hbm, part_ref,
             stage, sems, zeros, zero_sems):
        del dp_in, after_ref
        i = pl.program_id(0)
        writer = _ColumnWriter(dp_hbm, stage, sems, pieces, n_tiles + 1)
        slot = writer.slot_for(i)
        out = stage.at[slot]
        out[:, 2 * D:2 * D + qk] = dq_ref[...]
        out[:, 2 * D + qk:3 * D] = dk_ref[...]
        out[:, 3 * D:4 * D] = dv_ref[...]

        @pl.when(i == 0)
        def _():
            part_ref[...] = jnp.zeros((8, D), F32)

        @pl.when(i == n_tiles)
        def _():
            out[:, 0:2 * D] = jnp.zeros((TM, 2 * D), BF16)
            zeros[...] = jnp.zeros(zeros.shape, BF16)
            rows = pl.ds(n_tiles * TM, TM)
            fills = [pltpu.make_async_copy(zeros.at[:, pl.ds(0, width)], dp_hbm.at[rows, pl.ds(dst, width)],
                                           zero_sems.at[k]) for k, (dst, width) in enumerate(zero_pieces)]
            for cp in fills:
                cp.start()
            for cp in fills:
                cp.wait()

        @pl.when(i < n_tiles)
        def _():
            f = lambda ref: ref[...].astype(F32)
            h, cg, dc = f(h_ref), f(cg_ref), f(dc_ref)
            w0, w1, w2 = vec_ref[1:2, :], vec_ref[2:3, :], vec_ref[3:4, :]
            row = lax.broadcasted_iota(jnp.int32, (TM, 1), 0)
            dc_prev = jnp.where(i == 0, 0.0, f(dcp_ref)[HALO - 1:HALO, :])
            dc_next = jnp.where(i == n_tiles - 1, 0.0, f(dcn_ref)[0:1, :])
            dc_up = jnp.where(row == 0, dc_prev, pltpu.roll(dc, 1, 0))
            dc_dn = jnp.where(row == TM - 1, dc_next, pltpu.roll(dc, TM - 1, 0))
            du = w0 * dc_dn + w1 * dc + w2 * dc_up
            u = cg * h
            part_ref[0:1, :] += jnp.sum(u * dc_dn, axis=0, keepdims=True)
            part_ref[1:2, :] += jnp.sum(u * dc, axis=0, keepdims=True)
            part_ref[2:3, :] += jnp.sum(u * dc_up, axis=0, keepdims=True)
            part_ref[3:4, :] += jnp.sum(dc, axis=0, keepdims=True)
            out[:, 0:D] = (du * cg).astype(BF16)
            out[:, D:2 * D] = (du * h).astype(BF16)

        writer.send(i, slot)

    last = n_tiles - 1
    col = lambda cb_: pl.BlockSpec((TM, D), lambda i, cb_=cb_: (jnp.minimum(i, last), cb_))
    lat = lambda w: pl.BlockSpec((TM, w), lambda i: (jnp.minimum(i, last), 0))
    ext = lambda w: pl.BlockSpec((TM, w), lambda i: (i, 0))
    anyspec = pl.BlockSpec(memory_space=pl.ANY)
    return pl.pallas_call(
        body, name="conv_backward", grid=(n_tiles + 1,),
        in_specs=[col(CB_H), col(CB_CG), lat(D),
                  pl.BlockSpec((HALO, D), lambda i: (jnp.clip(i * hb - 1, 0, n_tiles * hb - 1), 0)),
                  pl.BlockSpec((HALO, D), lambda i: (jnp.minimum((i + 1) * hb, n_tiles * hb - 1), 0)),
                  ext(qk), ext(qk), ext(D), pl.BlockSpec(vecs.shape, lambda i: (0, 0)), anyspec, _AFTER_SPEC],
        out_specs=[anyspec, pl.BlockSpec((8, D), lambda i: (0, 0))],
        out_shape=[jax.ShapeDtypeStruct(dp.shape, BF16), jax.ShapeDtypeStruct((8, D), F32)],
        input_output_aliases={9: 0},
        scratch_shapes=[pltpu.VMEM((2, TM, 4 * D), BF16), pltpu.SemaphoreType.DMA((2, len(pieces))),
                        pltpu.VMEM((TM, 3 * D), BF16), pltpu.SemaphoreType.DMA((len(zero_pieces),))],
        compiler_params=_cparams(dimension_semantics=("arbitrary",)),
    )(p_ext, p_ext, dconv, dconv, dconv, dq, dk, dv, vecs, dp, after)


def _input_backward(dp, shards, ids, x, ctx, dx1, modx, modc, norm_w, after):
    seq = x.shape[0]
    lext = seq + ctx.shape[0]
    n_x = seq // TM
    n_w = len(shards.arrays)

    def body(ids_ref, dp_ref, x_ref, ctx_ref, dx1_ref, mx_ref, mc_ref, nw_ref, *rest):
        w_hbm, (_, gx_ref, part_ref, w_ref, w_sems) = rest[:n_w], rest[n_w:]
        i = pl.program_id(0)
        _load_w_in(i, ids_ref, shards, w_hbm, w_ref, w_sems)
        is_ctx = i >= n_x
        dxm = _dot_nt(dp_ref[...], w_ref[...])
        x = jnp.where(is_ctx, ctx_ref[...], x_ref[...])
        r = lax.rsqrt(jnp.mean(x * x, axis=-1, keepdims=True) + EPS)
        xh = x * r
        nw = nw_ref[...]
        sc = jnp.where(is_ctx, mc_ref[1:2, :], mx_ref[1:2, :])
        dxn = dxm * (1.0 + sc)
        dxh = dxn * nw
        dx = r * (dxh - xh * jnp.mean(dxh * xh, axis=-1, keepdims=True))

        @pl.when(jnp.logical_not(is_ctx))
        def _():
            gx_ref[...] = dx1_ref[...] + dx

        @pl.when(i == 0)
        def _():
            part_ref[...] = jnp.zeros((8, D), F32)

        fx = jnp.where(is_ctx, 0.0, 1.0)
        d_shift = jnp.sum(dxm, axis=0, keepdims=True)
        d_scale = jnp.sum(dxm * (xh * nw), axis=0, keepdims=True)
        part_ref[0:1, :] += fx * d_shift
        part_ref[1:2, :] += fx * d_scale
        part_ref[2:3, :] += jnp.sum(dxn * xh, axis=0, keepdims=True)
        part_ref[3:4, :] += (1.0 - fx) * d_shift
        part_ref[4:5, :] += (1.0 - fx) * d_scale

    lat = lambda w: pl.BlockSpec((TM, w), lambda i, ids_: (jnp.minimum(i, n_x - 1), 0))
    ext = lambda w: pl.BlockSpec((TM, w), lambda i, ids_: (i, 0))
    full = lambda a: pl.BlockSpec(a.shape, lambda i, ids_: (0,) * a.ndim)
    grid_spec = pltpu.PrefetchScalarGridSpec(
        num_scalar_prefetch=1, grid=(lext // TM,),
        in_specs=[ext(PW), lat(D), full(ctx), lat(D), full(modx), full(modc), full(norm_w)]
        + [pl.BlockSpec(memory_space=pl.ANY)] * n_w + [_AFTER_SPEC],
        out_specs=[lat(D), pl.BlockSpec((8, D), lambda i, ids_: (0, 0))],
        scratch_shapes=[pltpu.VMEM((D, PW), BF16), pltpu.SemaphoreType.DMA((N_DEV,))])
    return pl.pallas_call(
        body, name="input_backward", grid_spec=grid_spec,
        out_shape=[jax.ShapeDtypeStruct((seq, D), F32), jax.ShapeDtypeStruct((8, D), F32)],
        compiler_params=_cparams(dimension_semantics=("arbitrary",)),
    )(ids, dp, x, ctx, dx1, modx, modc, norm_w, *shards.arrays, after)


def _weight_grad_in(xm_t, dp, owners, name, after):
    lext = xm_t.shape[1]

    def body(own_ref, a_ref, b_ref, after_ref, o_ref):
        o_ref[0] = _dot(a_ref[...], b_ref[...])

    grid_spec = pltpu.PrefetchScalarGridSpec(
        num_scalar_prefetch=1, grid=(4,),
        in_specs=[pl.BlockSpec((D, lext), lambda j, own: (0, 0)),
                  pl.BlockSpec((lext, WSH), lambda j, own: (0, own[j])), _AFTER_SPEC],
        out_specs=pl.BlockSpec((1, D, WSH), lambda j, own: (j, 0, 0)))
    return pl.pallas_call(
        body, name=name, grid_spec=grid_spec,
        out_shape=jax.ShapeDtypeStruct((4, D, WSH), F32),
        compiler_params=_cparams(dimension_semantics=("arbitrary",)),
    )(owners, xm_t, dp, after)


def _weight_grad_square(a_t, b):
    seq = a_t.shape[2]

    def body(a_ref, b_ref, o_ref):
        o_ref[:, 0] = _dot(a_ref[0], b_ref[0]).reshape(N_DEV, RSH, D)

    return pl.pallas_call(
        body, name="weight_grad_square", grid=(3,),
        in_specs=[pl.BlockSpec((1, D, seq), lambda t: (t, 0, 0)), pl.BlockSpec((1, seq, D), lambda t: (t, 0, 0))],
        out_specs=pl.BlockSpec((N_DEV, 1, RSH, D), lambda t: (0, t, 0, 0)),
        out_shape=jax.ShapeDtypeStruct((N_DEV, 3, RSH, D), F32),
        compiler_params=_cparams(dimension_semantics=("arbitrary",)),
    )(a_t, b)


def _rows(total, *parts):
    width = max(a.shape[1] for _, a in parts)
    out = None
    for row, a in parts:
        padded = jnp.pad(a, ((row, total - row - a.shape[0]), (0, width - a.shape[1])))
        out = padded if out is None else out + padded
    return out


def kernel(x, c, ctx, c_ctx, norm_w, ada_w, ada_b, w_in, conv_w, conv_b, decay_logit, gn_w, w_a, w_b, w_out, final_norm_w, loss_target, m_c_ctx, m_norm_w, m_ada_w, m_ada_b, m_w_in, m_conv_w, m_conv_b, m_decay_logit, m_gn_w, m_w_a, m_w_b, m_w_out, m_final_norm_w, v_c_ctx, v_norm_w, v_ada_w, v_ada_b, v_w_in, v_conv_w, v_conv_b, v_decay_logit, v_gn_w, v_w_a, v_w_b, v_w_out, v_final_norm_w):
    xi, yi, ci = _coords()
    me = 4 * xi + 2 * yi + ci
    chip = 2 * xi + yi
    seq, ctx_len = x.shape[1], ctx.shape[1]
    assert seq % TM == 0 and seq % RET_C == 0 and ctx_len == TM and seq % GRID_W == 0
    csh = D // N_DEV

    blk = jnp.pad(c, ((0, 7), (0, 0))) + jnp.pad(conv_w[0], ((1, 4), (0, D - csh)))
    got, _ = _all_gather_small(blk, "gather_cond")
    conv_w_all = got[:, 1:4, 0:csh].transpose(1, 0, 2).reshape(3, D)
    c16 = _rows(16, (0, got[:, 0, :]), (N_DEV, c_ctx[None]))
    ada_b_sh = lax.dynamic_slice(ada_b, (0, me * ADA_SH), (1, ADA_SH))
    mod_sh, act16, lg = _modulation(c16, ada_w[0], ada_b_sh, decay_logit[0])
    mod_all, small_done = _all_gather_small(mod_sh, "gather_mod")
    mod_all = mod_all.transpose(1, 0, 2).reshape(16, 3 * D)
    modx = lax.dynamic_slice(mod_all, (me, 0), (1, 3 * D)).reshape(3, D)
    modc = mod_all[8].reshape(3, D)

    x2, tgt = x[0], loss_target[0]
    wblock = lambda n: jax.ShapeDtypeStruct((n, D, WSH), BF16)
    ids_of = lambda ks: jnp.stack([jnp.bitwise_xor(me, k) for k in ks]).astype(jnp.int32)
    own_shard = w_in[0].astype(BF16)[None]
    sq_shards = [w[0].astype(BF16) for w in (w_a, w_b, w_out)]
    wave1 = _Exchange(
        "gather_in_first", [jnp.broadcast_to(own_shard, (2, D, WSH)), own_shard], [wblock(4)], 3,
        lambda srcs, lands: _forward_plan([(0, 1)])(srcs, lands) + _send_plan((4, 2))(srcs[1:], lands),
        after=small_done)
    xm, xm_t, cos_t, sin_t = _prenorm(x2, ctx[0], modx, modc, norm_w, wave1.token)
    (w_pair, _), _ = wave1.wait(xm_t, 1, through=[0])
    first = _Shards([w_pair], [(0, 0), (0, 1)])
    p_ext = _in_projection(None, xm, cos_t, sin_t, w_pair, [((0, 0), (1, 1))], me, "in_projection_pair")
    _, (w_nbr,) = wave1.wait(p_ext, through=[1, 2])
    wave2 = _Exchange(
        "gather_in_second", [w_nbr, own_shard], [wblock(2)], 3,
        lambda srcs, lands: _forward_plan([(0, 2), (1, 3)])(srcs, lands) + _send_plan((6,))(srcs[1:], lands))
    (w_nbr, _), _ = wave2.wait(wave2.token, 2, through=[0])
    second = _Shards([w_nbr], [(0, 0), (0, 1), (0, 2), (0, 3)])
    p_ext = _in_projection(p_ext, xm, cos_t, sin_t, w_nbr, [((0, 4), (2, 5)), ((1, 2), (3, 3))], me,
                           "in_projection_neighbours")
    _, (w_diag,) = wave2.wait(p_ext, through=[1, 2])
    wave3 = _Exchange(
        "gather_in_third", [w_diag] + sq_shards, [jax.ShapeDtypeStruct((N_DEV, RSH, D), BF16)] * 3, 1 + 3 * (N_DEV - 1),
        lambda srcs, lands: _forward_plan([(0, 1)])(srcs, lands) + _gather_plan(srcs[1:], lands))
    (w_diag, *_), _ = wave3.wait(wave3.token, 1, through=[0])
    third = _Shards([w_diag], [(0, 0), (0, 1)])
    p_ext = _in_projection(p_ext, xm, cos_t, sin_t, w_diag, [((0, 6), (1, 7))], me, "in_projection_diagonal")
    w_shards, w_ids = first + second + third, ids_of((0, 1, 4, 2, 5, 3, 6, 7))

    rn, rstd = _retention_forward(p_ext, lg, seq, ctx_len)
    _, sq_lands = wave3.wait(rstd, through=range(1, 7))
    w_a_all, w_b_all, w_out_all = (
        lax.dynamic_update_slice(land, shard[None], (me, 0, 0)).reshape(D, D) for land, shard in zip(sq_lands, sq_shards))
    vecs = _rows(8, (0, modx[2:3]), (1, conv_w_all), (4, conv_b), (5, gn_w), (6, final_norm_w[None]))
    dx1, dp, dconv, dret, op_at, op_b, part_m = _merge(p_ext, rn, rstd, x2, tgt, w_a_all, w_b_all, w_out_all, vecs, seq)

    j4 = jnp.arange(4, dtype=jnp.int32)
    owners = (2 * jnp.bitwise_xor(chip, j4) + ci).astype(jnp.int32)
    owners_sib = (2 * jnp.bitwise_xor(chip, j4) + (1 - ci)).astype(jnp.int32)
    gw_sq = _weight_grad_square(op_at, op_b).reshape(N_DEV, 3 * RSH, D)
    rs_sq_pair = _Exchange("rs_square_pair", [gw_sq], [jax.ShapeDtypeStruct((4, 3 * RSH, D), F32)], 4,
                           _pair_plan(lambda j, chip_, c_: 2 * jnp.bitwise_xor(chip_, j) + (1 - c_)))
    dq, dk, dv, dlg = _retention_backward(p_ext, dret, lg, cos_t, sin_t, seq, ctx_len, rs_sq_pair.token)
    (gw_sq,), (r1_sq,) = rs_sq_pair.wait(dlg)
    own_sq, send_sq = _pair_sum(gw_sq, r1_sq, owners, "pair_sum_square")
    rs_sq_chips = _Exchange("rs_square_chips", [send_sq], [jax.ShapeDtypeStruct((3, 3 * RSH, D), BF16)], 3, _chips_plan)
    dp, part_c = _conv_backward(p_ext, dconv, dp, dq, dk, dv, vecs, seq, rs_sq_chips.token)
    gw_sib = _weight_grad_in(xm_t, dp, owners_sib, "weight_grad_in_sibling", part_c)
    rs_in_pair = _Exchange("rs_in_pair", [gw_sib], [jax.ShapeDtypeStruct((4, D, WSH), F32)], 4,
                           _pair_plan(lambda j, chip_, c_: j))
    gw_own = _weight_grad_in(xm_t, dp, owners, "weight_grad_in_own", rs_in_pair.token)
    _, (r1_in,) = rs_in_pair.wait(gw_own)
    own_in, send_in = _pair_sum(gw_own, r1_in, j4, "pair_sum_in")
    rs_in_chips = _Exchange("rs_in_chips", [send_in], [jax.ShapeDtypeStruct((3, D, WSH), BF16)], 3, _chips_plan)
    grad_x, part_i = _input_backward(dp, w_shards, w_ids, x2, ctx[0], dx1, modx, modc, norm_w, rs_in_chips.token)

    dl = dlg[:, 0, 0:4]
    dlg_row = jnp.pad(dl[:, 0::2].reshape(1, N_HEADS), ((0, 0), (0, D - N_HEADS))) + jnp.pad(
        dl[:, 1::2].reshape(1, N_HEADS), ((0, 0), (N_HEADS, D - 2 * N_HEADS)))
    partials = _rows(16, (0, part_m[0:1]), (1, part_i[2:3]), (2, part_c[3:4]), (3, part_m[2:3]), (4, part_c[0:3]),
                     (7, part_i[0:2]), (9, part_m[1:2]), (10, part_i[3:5]), (13, part_m[3:4]), (14, dlg_row))
    results = {}
    gather_part = _Exchange("gather_partials", [partials], [jax.ShapeDtypeStruct((N_DEV, 16, D), F32)],
                            N_DEV - 1, _gather_plan)
    _, (r2_in,) = rs_in_chips.wait(gather_part.token)
    results["w_in"] = [o[None] for o in _adam_sharded(w_in[0], m_w_in[0], v_w_in[0], own_in, r2_in, "adam_w_in")]
    _, (got,) = gather_part.wait(results["w_in"][1])
    got = lax.dynamic_update_slice(got, partials[None], (me, 0, 0))
    tot = _sum_devices(got, "sum_partials")
    dmodc = tot[10:13].reshape(1, 3 * D)
    dmod16 = _rows(16, (0, got[:, 7:10, :].reshape(N_DEV, 3 * D)), (N_DEV, dmodc))
    dmod16 = lax.dynamic_slice(dmod16, (0, me * ADA_SH), (16, ADA_SH))
    dmodc8 = _rows(8, (0, dmod16[8:9]))
    g_ada_w, d_ada_w, nm_ada_w, nv_ada_w, cctx_part = _ada_backward(
        act16.T, dmod16, dmodc8, ada_w[0], m_ada_w[0], v_ada_w[0])
    gather_cctx = _Exchange("gather_cctx", [cctx_part], [jax.ShapeDtypeStruct((N_DEV, 8, D), F32)],
                            N_DEV - 1, _gather_plan)
    _, (r2_sq,) = rs_sq_chips.wait(gather_cctx.token)
    square = _adam_square([(w_a[0], m_w_a[0], v_w_a[0]), (w_b[0], m_w_b[0], v_w_b[0]),
                           (w_out[0], m_w_out[0], v_w_out[0])], own_sq, r2_sq)
    _, (got_cctx,) = gather_cctx.wait(square[0][1])
    dact_ctx = _sum_devices(lax.dynamic_update_slice(got_cctx, cctx_part[None], (me, 0, 0)), "sum_cctx")

    view = {"c_ctx": (1, D), "final_norm_w": (1, D)}
    given = {"c_ctx": (c_ctx, m_c_ctx, v_c_ctx), "norm_w": (norm_w, m_norm_w, v_norm_w),
             "conv_b": (conv_b, m_conv_b, v_conv_b), "gn_w": (gn_w, m_gn_w, v_gn_w),
             "final_norm_w": (final_norm_w, m_final_norm_w, v_final_norm_w), "ada_b": (ada_b, m_ada_b, v_ada_b),
             "decay_logit": (decay_logit, m_decay_logit, v_decay_logit), "conv_w": (conv_w, m_conv_w, v_conv_w)}
    small_in = [tuple(a.reshape(view.get(name, a.shape)) for a in given[name]) for name in SMALL_PARAMS]
    conv_w_grad = lax.dynamic_slice(tot, (4, me * csh), (3, csh))
    small_out = _adam_small(tot, dact_ctx, conv_w_grad, tot[14, 0:2 * N_HEADS].reshape(2, N_HEADS), small_in)
    results.update({name: [o.reshape(given[name][0].shape) for o in outs_]
                    for name, outs_ in zip(SMALL_PARAMS, small_out)})
    for name, outs_ in zip(("w_a", "w_b", "w_out"), square):
        results[name] = [o[None] for o in outs_]
    results["ada_w"] = [o[None] for o in (g_ada_w, d_ada_w, nm_ada_w, nv_ada_w)]

    order = ("c_ctx", "norm_w", "ada_w", "ada_b", "w_in", "conv_w", "conv_b", "decay_logit", "gn_w",
             "w_a", "w_b", "w_out", "final_norm_w")
    outs = [results[name][kind] for kind in range(4) for name in order]
    return (tot[13, 0], grad_x[None], *outs)
```

```python
import math

import jax
import jax.numpy as jnp
from jax import lax
from jax.experimental import pallas as pl
from jax.experimental.pallas import tpu as pltpu

F32 = jnp.float32
BF16 = jnp.bfloat16
MESH = pl.DeviceIdType.MESH
_HBM_SPEC = pl.BlockSpec(memory_space=pltpu.HBM)
_SEM_SPEC = pl.BlockSpec(memory_space=pltpu.SEMAPHORE)
_EFFECT = pltpu.SideEffectType.DATAFLOW_SIDE_EFFECTING

N_DEV = 8
D = 1024
N_HEADS = 8
DK = 64
DV = 128
GRID_W = 64
ROPE_BASE = 10000.0
EPS = 1e-6
PW = 9 * D
WSH = PW // N_DEV
RSH = D // N_DEV
ADA_SH = 3 * D // N_DEV
TM = 256
RET_C = 256
HALO = 16
LANES = 128
VMEM_LIMIT = 60 * 1024 * 1024

ADAM_LR = 0.001
ADAM_B1 = 0.9
ADAM_B2 = 0.999
ADAM_EPS = 1e-08
ADAM_WD = 0.01
ADAM_STEP = 10

CB_H, CB_BG, CB_CG, CB_ZA, CB_QK, CB_V, CB_ZB, CB_GA, CB_GB = range(9)


def _cparams(**kw):
    return pltpu.CompilerParams(vmem_limit_bytes=VMEM_LIMIT, **kw)


def _dot(a, b):
    return jnp.dot(a, b, preferred_element_type=F32)


def _dot_nt(a, b):
    return lax.dot_general(a, b, (((1,), (1,)), ((), ())), preferred_element_type=F32)


def _dot_tn(a, b):
    return lax.dot_general(a, b, (((0,), (0,)), ((), ())), preferred_element_type=F32)


def _sigmoid(z):
    return 0.5 * jnp.tanh(0.5 * z) + 0.5


def _row_tile(rows):
    return TM if rows % TM == 0 else rows


def _coords():
    return lax.axis_index("x"), lax.axis_index("y"), lax.axis_index("c")


def _flip(v, bit):
    return 1 - v if bit else v


def _all_gather_small(blk, name):
    rows, cols = blk.shape

    def body(x_ref, out_ref, done, send_sems, recv_sems, local_sem):
        x, y, c = _coords()
        me = 4 * x + 2 * y + c
        mine = pltpu.make_async_copy(x_ref, out_ref.at[me], local_sem)
        mine.start()

        def copy(k, slot):
            peer = (_flip(x, k & 4), _flip(y, k & 2), _flip(c, k & 1))
            return pltpu.make_async_remote_copy(
                src_ref=x_ref, dst_ref=out_ref.at[slot], send_sem=send_sems.at[k - 1],
                recv_sem=recv_sems.at[k - 1], device_id=peer, device_id_type=MESH)

        for k in range(1, N_DEV):
            copy(k, me).start()
        for k in range(1, N_DEV):
            copy(k, jnp.bitwise_xor(me, k)).wait_recv()
        for k in range(1, N_DEV):
            copy(k, me).wait_send()
        mine.wait()
        done[...] = jnp.zeros((8, LANES), F32)

    vmem = pl.BlockSpec(memory_space=pltpu.VMEM)
    return pl.pallas_call(
        body, name=name,
        out_shape=[jax.ShapeDtypeStruct((N_DEV, rows, cols), blk.dtype), jax.ShapeDtypeStruct((8, LANES), F32)],
        in_specs=[vmem], out_specs=[vmem, vmem],
        scratch_shapes=[pltpu.SemaphoreType.DMA((N_DEV - 1,)), pltpu.SemaphoreType.DMA((N_DEV - 1,)),
                        pltpu.SemaphoreType.DMA],
    )(blk)


class _Absent:
    at = property(lambda self: self)

    def __getitem__(self, index):
        return self


class _Exchange:
    def __init__(self, name, srcs, land_shapes, n_copies, plan, after=None):
        self.name, self.plan, self.n_copies = name, plan, n_copies
        self.n_src, self.n_land = len(srcs), len(land_shapes)
        hbm = lambda a: pltpu.HBM(a.shape, a.dtype)
        n = self.n_src + self.n_land
        lands = [pltpu.with_memory_space_constraint(lax.empty(s.shape, s.dtype), pltpu.HBM) for s in land_shapes]
        srcs = [pltpu.with_memory_space_constraint(s, pltpu.HBM) for s in srcs]
        extra = [] if after is None else [after]

        def body(*refs):
            send_sems, recv_sems = refs[n + len(extra)], refs[n + len(extra) + 1]
            for cp in self._copies(refs, send_sems, recv_sems):
                cp.start()
            refs[-1][...] = jnp.zeros((8, LANES), F32)

        outs = pl.pallas_call(
            body, name=name + "_start",
            out_shape=(pltpu.SemaphoreType.DMA((n_copies,)), pltpu.SemaphoreType.DMA((n_copies,)),
                       *[hbm(a) for a in srcs], *[hbm(a) for a in lands], jax.ShapeDtypeStruct((8, LANES), F32)),
            in_specs=[_HBM_SPEC] * n + [pl.BlockSpec(memory_space=pl.ANY)] * len(extra),
            out_specs=(_SEM_SPEC, _SEM_SPEC, *[_HBM_SPEC] * n, pl.BlockSpec(memory_space=pltpu.VMEM)),
            input_output_aliases={i: 2 + i for i in range(n)},
            compiler_params=pltpu.CompilerParams(has_side_effects=_EFFECT),
        )(*srcs, *lands, *extra)
        self.send_sems, self.recv_sems = outs[0], outs[1]
        self.buffers = list(outs[2:2 + n])
        self.token = outs[-1]
        self.waited = 0

    def _copies(self, refs, send_sems, recv_sems, lo=0, hi=None):
        src_refs, land_refs = refs[:self.n_src], refs[self.n_src:self.n_src + self.n_land]
        planned = list(enumerate(self.plan(src_refs, land_refs)))[lo:hi]
        return [pltpu.make_async_remote_copy(src_ref=s, dst_ref=d, send_sem=send_sems.at[k], recv_sem=recv_sems.at[k],
                                             device_id=dev, device_id_type=MESH) for k, (s, d, dev) in planned]

    def wait(self, after, count=None, through=None):
        n = self.n_src + self.n_land
        lo = self.waited
        hi = self.n_copies if count is None else lo + count
        self.waited = hi
        through = list(range(n)) if through is None else list(through)
        m = len(through)

        def body(*refs):
            held = [_Absent()] * n
            for pos, ref in zip(through, refs[:m]):
                held[pos] = ref
            for cp in self._copies(held, refs[m], refs[m + 1], lo, hi):
                cp.wait_send()
                cp.wait_recv()

        outs = pl.pallas_call(
            body, name=f"{self.name}_wait{lo}" if lo or hi < self.n_copies else self.name + "_wait",
            out_shape=tuple(pltpu.HBM(self.buffers[p].shape, self.buffers[p].dtype) for p in through),
            in_specs=[_HBM_SPEC] * m + [_SEM_SPEC, _SEM_SPEC, pl.BlockSpec(memory_space=pl.ANY)],
            out_specs=tuple([_HBM_SPEC] * m),
            input_output_aliases={i: i for i in range(m)},
            compiler_params=pltpu.CompilerParams(has_side_effects=_EFFECT),
        )(*[self.buffers[p] for p in through], self.send_sems, self.recv_sems, after)
        for pos, out in zip(through, outs):
            self.buffers[pos] = out
        return list(self.buffers[:self.n_src]), list(self.buffers[self.n_src:])


def _pair_plan(src_index):
    def plan(srcs, lands):
        x, y, c = _coords()
        chip = 2 * x + y
        return [(srcs[0].at[src_index(j, chip, c)], lands[0].at[j], (x, y, 1 - c)) for j in range(4)]
    return plan


def _chips_plan(srcs, lands):
    x, y, c = _coords()
    return [(srcs[0].at[j - 1], lands[0].at[j - 1], (_flip(x, j & 2), _flip(y, j & 1), c)) for j in range(1, 4)]


def _peer(k):
    x, y, c = _coords()
    return _flip(x, k & 4), _flip(y, k & 2), _flip(c, k & 1)


def _send_plan(ks):
    def plan(srcs, lands):
        return [(srcs[0].at[0], lands[0].at[p], _peer(k)) for p, k in enumerate(ks)]
    return plan


def _forward_plan(moves):
    def plan(srcs, lands):
        return [(srcs[0].at[s], srcs[0].at[d], _peer(1)) for s, d in moves]
    return plan


def _gather_plan(srcs, lands):
    x, y, c = _coords()
    me = 4 * x + 2 * y + c
    return [(srcs[a], lands[a].at[me], (_flip(x, k & 4), _flip(y, k & 2), _flip(c, k & 1)))
            for a in range(len(srcs)) for k in range(1, N_DEV)]


def _pair_sum(grad, recv, owners, name):
    _, rows, cols = grad.shape
    tr = rows

    def body(own_ref, g_ref, r_ref, mine_ref, send_ref):
        j = pl.program_id(1)
        total = g_ref[...] + r_ref[...].astype(F32)

        @pl.when(j == 0)
        def _():
            mine_ref[...] = total

        @pl.when(j > 0)
        def _():
            send_ref[...] = total.astype(BF16)

    grid_spec = pltpu.PrefetchScalarGridSpec(
        num_scalar_prefetch=1, grid=(rows // tr, 4),
        in_specs=[pl.BlockSpec((1, tr, cols), lambda i, j, own: (own[j], i, 0)),
                  pl.BlockSpec((1, tr, cols), lambda i, j, own: (j, i, 0))],
        out_specs=[pl.BlockSpec((1, tr, cols), lambda i, j, own: (0, i, 0)),
                   pl.BlockSpec((1, tr, cols), lambda i, j, own: (jnp.maximum(j - 1, 0), i, 0))])
    return pl.pallas_call(
        body, name=name, grid_spec=grid_spec,
        out_shape=[jax.ShapeDtypeStruct((1, rows, cols), F32), jax.ShapeDtypeStruct((3, rows, cols), BF16)],
        compiler_params=_cparams(dimension_semantics=("arbitrary", "arbitrary")),
    )(owners, grad, recv)


def _modulation(c16, ada_w_sh, ada_b_sh, decay_logit):
    def body(c_ref, w_ref, b_ref, dl_ref, mod_ref, act_ref, lg_ref):
        cv = c_ref[...]
        act = cv * _sigmoid(cv)
        act_ref[...] = act
        mod_ref[...] = jnp.dot(act, w_ref[...], preferred_element_type=F32,
                               precision=lax.Precision.HIGHEST) + b_ref[...]
        z = dl_ref[...]
        lg_ref[...] = jnp.minimum(z, 0.0) - jnp.log(1.0 + jnp.exp(-jnp.abs(z)))

    return pl.pallas_call(
        body, name="modulation",
        out_shape=[jax.ShapeDtypeStruct((16, ADA_SH), F32), jax.ShapeDtypeStruct((16, D), F32),
                   jax.ShapeDtypeStruct(decay_logit.shape, F32)],
        compiler_params=_cparams(),
    )(c16, ada_w_sh, ada_b_sh, decay_logit)


def _adam_update(w, g, m, v):
    m2 = ADAM_B1 * m + (1.0 - ADAM_B1) * g
    v2 = ADAM_B2 * v + (1.0 - ADAM_B2) * (g * g)
    m_hat = m2 / (1.0 - ADAM_B1 ** ADAM_STEP)
    v_hat = v2 / (1.0 - ADAM_B2 ** ADAM_STEP)
    delta = -ADAM_LR * (m_hat / (jnp.sqrt(v_hat) + ADAM_EPS) + ADAM_WD * w)
    return delta, m2, v2


def _adam_sharded(w, m, v, own, recv, name):
    rows, cols = w.shape
    tr = _row_tile(rows)

    def body(w_ref, m_ref, v_ref, g0, g1, g2, g3, g_ref, d_ref, m_out, v_out):
        g = ((g0[0] + g1[0].astype(F32)) + g2[0].astype(F32)) + g3[0].astype(F32)
        delta, m2, v2 = _adam_update(w_ref[...], g, m_ref[...], v_ref[...])
        g_ref[...] = g
        d_ref[...] = delta
        m_out[...] = m2
        v_out[...] = v2

    flat = pl.BlockSpec((tr, cols), lambda i: (i, 0))
    part = lambda j: pl.BlockSpec((1, tr, cols), lambda i, j=j: (j, i, 0))
    return pl.pallas_call(
        body, name=name, grid=(rows // tr,),
        in_specs=[flat, flat, flat, part(0), part(0), part(1), part(2)],
        out_specs=[flat] * 4,
        out_shape=[jax.ShapeDtypeStruct((rows, cols), F32)] * 4,
        compiler_params=_cparams(dimension_semantics=("arbitrary",)),
    )(w, m, v, own, recv, recv, recv)


def _ada_backward(act_t, dmod16, dmodc8, ada_w_sh, m, v):
    def body(at_ref, dm_ref, dc_ref, w_ref, m_ref, v_ref, g_ref, d_ref, m_out, v_out, pc_ref):
        g = jnp.dot(at_ref[...], dm_ref[...], preferred_element_type=F32, precision=lax.Precision.HIGHEST)
        w = w_ref[...]
        delta, m2, v2 = _adam_update(w, g, m_ref[...], v_ref[...])
        g_ref[...] = g
        d_ref[...] = delta
        m_out[...] = m2
        v_out[...] = v2
        pc_ref[...] = lax.dot_general(dc_ref[...], w, (((1,), (1,)), ((), ())), preferred_element_type=F32,
                                      precision=lax.Precision.HIGHEST)

    return pl.pallas_call(
        body, name="ada_backward",
        out_shape=[jax.ShapeDtypeStruct((D, ADA_SH), F32)] * 4 + [jax.ShapeDtypeStruct((8, D), F32)],
        compiler_params=_cparams(),
    )(act_t, dmod16, dmodc8, ada_w_sh, m, v)


def _sum_devices(gathered, name):
    _, rows, cols = gathered.shape

    def body(g_ref, o_ref):
        acc = g_ref[0]
        for d in range(1, N_DEV):
            acc = acc + g_ref[d]
        o_ref[...] = acc

    return pl.pallas_call(body, name=name, out_shape=jax.ShapeDtypeStruct((rows, cols), F32),
                          compiler_params=_cparams())(gathered)


SMALL_PARAMS = ("c_ctx", "norm_w", "conv_b", "gn_w", "final_norm_w", "ada_b", "decay_logit", "conv_w")


def _adam_small(tot, dact_ctx, conv_w_grad, decay_grad, params):
    n = len(SMALL_PARAMS)

    def body(tot_ref, dact_ref, cwg_ref, dlg_ref, *refs):
        ins, outs = refs[:3 * n], refs[3 * n:]
        raw = {
            "c_ctx": dact_ref[0:1, :],
            "norm_w": tot_ref[1:2, :], "conv_b": tot_ref[2:3, :], "gn_w": tot_ref[3:4, :],
            "final_norm_w": tot_ref[0:1, :],
            "ada_b": jnp.concatenate([tot_ref[7 + r:8 + r, :] + tot_ref[10 + r:11 + r, :] for r in range(3)], axis=1),
            "decay_logit": dlg_ref[...],
            "conv_w": cwg_ref[...],
        }
        for k, name in enumerate(SMALL_PARAMS):
            lead = (0,) if len(ins[3 * k].shape) == 3 else ()
            w, m, v = (ref[lead + (...,)] for ref in ins[3 * k:3 * k + 3])
            g = raw[name]
            if name == "c_ctx":
                s = _sigmoid(w)
                g = g * (s * (1.0 + w * (1.0 - s)))
            elif name == "decay_logit":
                g = g * (1.0 - _sigmoid(w))
            delta, m2, v2 = _adam_update(w, g, m, v)
            for ref, val in zip(outs[4 * k:4 * k + 4], (g, delta, m2, v2)):
                ref[lead + (...,)] = val

    flat = [a for trio in params for a in trio]
    out_shape = [jax.ShapeDtypeStruct(trio[0].shape, F32) for trio in params for _ in range(4)]
    outs = pl.pallas_call(body, name="adam_small", out_shape=out_shape, compiler_params=_cparams())(
        tot, dact_ctx, conv_w_grad, decay_grad, *flat)
    return [outs[4 * k:4 * k + 4] for k in range(n)]


def _adam_square(params, own, recv):
    def body(own_ref, recv_ref, *refs):
        ins, outs = refs[:9], refs[9:]
        for k in range(3):
            rows = slice(k * RSH, (k + 1) * RSH)
            g = own_ref[0, rows, :]
            for j in range(3):
                g = g + recv_ref[j, rows, :].astype(F32)
            delta, m2, v2 = _adam_update(ins[3 * k][...], g, ins[3 * k + 1][...], ins[3 * k + 2][...])
            for ref, val in zip(outs[4 * k:4 * k + 4], (g, delta, m2, v2)):
                ref[...] = val

    flat = [a for trio in params for a in trio]
    outs = pl.pallas_call(body, name="adam_square", out_shape=[jax.ShapeDtypeStruct((RSH, D), F32)] * 12,
                          compiler_params=_cparams())(own, recv, *flat)
    return [outs[4 * k:4 * k + 4] for k in range(3)]


def _rope(t, cos, sin):
    lane = lax.broadcasted_iota(jnp.int32, (1, LANES), 1)
    first_half = jnp.bitwise_and(lane, DK // 2) == 0
    partner = jnp.where(first_half, pltpu.roll(t, LANES - DK // 2, 1), pltpu.roll(t, DK // 2, 1))
    return t * cos + partner * sin


class _Shards:
    def __init__(self, arrays, picks):
        self.arrays, self.picks = list(arrays), list(picks)

    def __add__(self, other):
        shift = len(self.arrays)
        return _Shards(self.arrays + other.arrays, self.picks + [(a + shift, s) for a, s in other.picks])


def _load_w_in(step, ids_ref, shards, w_refs, w_vmem, sems):
    @pl.when(step == 0)
    def _():
        copies = []
        for p, (a, slot) in enumerate(shards.picks):
            col = pl.multiple_of(ids_ref[p] * WSH, LANES)
            copies.append(pltpu.make_async_copy(w_refs[a].at[slot], w_vmem.at[:, pl.ds(col, WSH)], sems.at[p]))
        for cp in copies:
            cp.start()
        for cp in copies:
            cp.wait()


def _project_columns(xmb, w_ref, shard, cos, sin, p_ref):
    width = 2 * WSH
    q_lo, k_lo, k_hi = CB_QK * D, CB_QK * D + N_HEADS * DK, CB_V * D
    q_pair, k_pair = q_lo // width, k_lo // width
    assert (q_pair + 1) * width == k_lo and k_pair * width == k_lo and k_hi <= (k_pair + 1) * width
    mxu_cols = 2 * LANES

    def columns(rot_lo, rot_hi, scale):
        for a in range(0, width, mxu_cols):
            acc = _dot(xmb, w_ref[:, a:a + mxu_cols])
            for col in range(a, a + mxu_cols, LANES):
                piece = acc[:, col - a:col - a + LANES]
                if rot_lo <= col < rot_hi:
                    piece = _rope(piece if scale == 1.0 else piece * scale, cos, sin)
                p_ref[:, col:col + LANES] = piece.astype(BF16)

    @pl.when(shard == q_pair)
    def _():
        columns(q_lo - q_pair * width, width, 1.0)

    @pl.when(shard == k_pair)
    def _():
        columns(0, k_hi - k_lo, DK ** -0.5)

    @pl.when(jnp.logical_and(shard != q_pair, shard != k_pair))
    def _():
        columns(0, 0, 1.0)


_AFTER_SPEC = pl.BlockSpec(memory_space=pl.ANY)


def _prenorm(x, ctx, modx, modc, norm_w, after):
    tp = 4 * TM
    seq, ctx_len = x.shape[0], ctx.shape[0]
    n_x_tiles = seq // tp
    lext = seq + ctx_len
    assert seq % tp == 0 and ctx_len <= tp and tp % GRID_W == 0 and tp // GRID_W <= 16
    grid_rows = tp // GRID_W

    def body(x_ref, ctx_ref, mx_ref, mc_ref, nw_ref, after_ref, xm_ref, xmt_ref, cos_ref, sin_ref, col_cos, col_sin):
        del after_ref
        i = pl.program_id(0)
        is_ctx = i >= n_x_tiles
        ctx_rows = jnp.concatenate([ctx_ref[...], jnp.zeros((tp - ctx_len, D), F32)], axis=0)
        x = jnp.where(is_ctx, ctx_rows, x_ref[...])
        r = lax.rsqrt(jnp.mean(x * x, axis=-1, keepdims=True) + EPS)
        sh = jnp.where(is_ctx, mc_ref[0:1, :], mx_ref[0:1, :])
        sc = jnp.where(is_ctx, mc_ref[1:2, :], mx_ref[1:2, :])
        xm = (x * r * nw_ref[...]) * (1.0 + sc) + sh
        xm_ref[...] = xm.astype(BF16)
        xmt_ref[...] = xm.T.astype(BF16)
        lane = lax.broadcasted_iota(jnp.int32, (1, LANES), 1)
        n_freq = DK // 4
        inv = jnp.exp((lane % n_freq).astype(F32) * (-math.log(ROPE_BASE) / n_freq))
        by_column = (lane % (2 * n_freq)) >= n_freq
        sign = jnp.where((lane % DK) < DK // 2, -1.0, 1.0)

        @pl.when(i == 0)
        def _():
            col = (lax.broadcasted_iota(jnp.int32, (tp, 1), 0) % GRID_W).astype(F32) * inv
            col_cos[...] = jnp.cos(col)
            col_sin[...] = jnp.sin(col) * sign

        row = (i * grid_rows + lax.broadcasted_iota(jnp.int32, (16, 1), 0)).astype(F32) * inv
        row_cos, row_sin = jnp.cos(row), jnp.sin(row) * sign
        tall = lambda t: jnp.concatenate(
            [jnp.broadcast_to(t[g:g + 1, :], (GRID_W, LANES)) for g in range(grid_rows)], axis=0)
        cos_ref[...] = jnp.where(is_ctx, 1.0, jnp.where(by_column, col_cos[...], tall(row_cos)))
        sin_ref[...] = jnp.where(is_ctx, 0.0, jnp.where(by_column, col_sin[...], tall(row_sin)))

    full = lambda a: pl.BlockSpec(a.shape, lambda i: (0,) * a.ndim)
    row = lambda width: pl.BlockSpec((tp, width), lambda i: (i, 0))
    return pl.pallas_call(
        body, name="prenorm", grid=(n_x_tiles + 1,),
        in_specs=[pl.BlockSpec((tp, D), lambda i: (jnp.minimum(i, n_x_tiles - 1), 0)), full(ctx),
                  full(modx), full(modc), full(norm_w), _AFTER_SPEC],
        out_specs=[row(D), pl.BlockSpec((D, tp), lambda i: (0, i)), row(LANES), row(LANES)],
        out_shape=[jax.ShapeDtypeStruct((lext, D), BF16), jax.ShapeDtypeStruct((D, lext), BF16),
                   jax.ShapeDtypeStruct((lext, LANES), F32), jax.ShapeDtypeStruct((lext, LANES), F32)],
        scratch_shapes=[pltpu.VMEM((tp, LANES), F32), pltpu.VMEM((tp, LANES), F32)],
        compiler_params=_cparams(dimension_semantics=("arbitrary",)),
    )(x, ctx, modx, modc, norm_w, after)


def _in_projection(p_ext, xm, cos_t, sin_t, w, pairs, me, name):
    lext = xm.shape[0]
    tr = lext // 4
    meta = []
    for (slot_a, k_a), (slot_b, k_b) in pairs:
        assert k_a ^ k_b == 1
        dev_a = jnp.bitwise_xor(me, k_a)
        a_low = dev_a % 2 == 0
        meta += [jnp.where(a_low, slot_a, slot_b), jnp.where(a_low, slot_b, slot_a), dev_a // 2]
    meta = jnp.stack(meta).astype(jnp.int32)

    def body(meta_ref, xm_ref, cos_ref, sin_ref, w_hbm, *rest):
        p_ref, w_ref, sems = rest[-3:]
        j, i = pl.program_id(0), pl.program_id(1)

        @pl.when(i == 0)
        def _():
            copies = [pltpu.make_async_copy(w_hbm.at[meta_ref[3 * j + h]], w_ref.at[:, h * WSH:(h + 1) * WSH],
                                            sems.at[h]) for h in range(2)]
            for cp in copies:
                cp.start()
            for cp in copies:
                cp.wait()

        _project_columns(xm_ref[...], w_ref, meta_ref[3 * j + 2], cos_ref[...], sin_ref[...], p_ref)

    row = lambda width: pl.BlockSpec((tr, width), lambda j, i, meta_: (i, 0))
    grid_spec = pltpu.PrefetchScalarGridSpec(
        num_scalar_prefetch=1, grid=(len(pairs), lext // tr),
        in_specs=[row(D), row(LANES), row(LANES), pl.BlockSpec(memory_space=pl.ANY)]
        + ([] if p_ext is None else [pl.BlockSpec(memory_space=pl.ANY)]),
        out_specs=pl.BlockSpec((tr, 2 * WSH), lambda j, i, meta_: (i, meta_[3 * j + 2])),
        scratch_shapes=[pltpu.VMEM((D, 2 * WSH), BF16), pltpu.SemaphoreType.DMA((2,))])
    return pl.pallas_call(
        body, name=name, grid_spec=grid_spec,
        out_shape=jax.ShapeDtypeStruct((lext, PW), BF16),
        input_output_aliases={} if p_ext is None else {5: 0},
        compiler_params=_cparams(dimension_semantics=("arbitrary", "arbitrary")),
    )(meta, xm, cos_t, sin_t, w, *([] if p_ext is None else [p_ext]))


def _decay_tables(lgf, lgb, n):
    i = lax.broadcasted_iota(jnp.int32, (n, 1), 0).astype(F32)
    return dict(i=i, k_f=jnp.exp(lgf * (n - 1.0 - i)), k_b=jnp.exp(lgb * i),
                q_f=jnp.exp(lgf * (i + 1.0)), q_b=jnp.exp(lgb * (n - i)))


def _decay_matrix(lgf, lgb, n):
    ii = lax.broadcasted_iota(jnp.int32, (n, n), 0)
    jj = lax.broadcasted_iota(jnp.int32, (n, n), 1)
    diff = (ii - jj).astype(F32)
    low = jnp.exp(lgf * jnp.maximum(diff, 0.0))
    up = jnp.exp(lgb * jnp.maximum(-diff, 0.0))
    return jnp.where(diff > 0, low, jnp.where(diff < 0, up, 2.0)), diff


def _cat_lanes(a, b):
    return jnp.concatenate([a.astype(BF16), b.astype(BF16)], axis=1)


def _retention_forward(p_ext, lg, seq, ctx_len):
    lext = seq + ctx_len
    n_chunks = seq // RET_C
    C = RET_C

    def body(lg_ref, q_ref, k_ref, v_ref, rn_ref, rstd_ref, kv_scr, sf_scr, sb_scr):
        pair = pl.program_id(0)
        lane = lax.broadcasted_iota(jnp.int32, (1, LANES), 1)
        heads = range(2)
        hmask = [(lane // DK == hh).astype(F32) for hh in heads]
        lgf = [lg_ref[0, 2 * pair + hh] for hh in heads]
        lgb = [lg_ref[1, 2 * pair + hh] for hh in heads]
        vs = [slice(hh * DV, (hh + 1) * DV) for hh in heads]
        t = [_decay_tables(lgf[hh], lgb[hh], C) for hh in heads]
        kc_all = k_ref[seq:lext, :].astype(F32)
        s0 = []
        for hh in heads:
            tc = _decay_tables(lgf[hh], lgb[hh], ctx_len)
            kc = kc_all * hmask[hh]
            s0.append(_dot_tn(_cat_lanes(kc * tc["k_f"], kc * tc["k_b"]), v_ref[seq:lext, vs[hh]]))

        def increments(c, carry):
            rows = pl.ds(pl.multiple_of(c * C, C), C)
            k_all = k_ref[rows, :].astype(F32)
            for hh in heads:
                k = k_all * hmask[hh]
                kv_scr[hh, c] = _dot_tn(_cat_lanes(k * t[hh]["k_f"], k * t[hh]["k_b"]), v_ref[rows, vs[hh]])
            return carry

        lax.fori_loop(0, n_chunks, increments, 0, unroll=2)
        gf_c = [jnp.exp(lgf[hh] * C) for hh in heads]
        gb_c = [jnp.exp(lgb[hh] * C) for hh in heads]

        def scan_f(c, s):
            for hh in heads:
                sf_scr[hh, c] = s[hh]
            return tuple(gf_c[hh] * s[hh] + kv_scr[hh, c, 0:LANES, :] for hh in heads)

        def scan_b(n, s):
            c = n_chunks - 1 - n
            for hh in heads:
                sb_scr[hh, c] = s[hh]
            return tuple(gb_c[hh] * s[hh] + kv_scr[hh, c, LANES:2 * LANES, :] for hh in heads)

        lax.fori_loop(0, n_chunks, scan_f, tuple(s0[hh][0:LANES] for hh in heads))
        lax.fori_loop(0, n_chunks, scan_b, tuple(s0[hh][LANES:2 * LANES] for hh in heads))
        dmat = [_decay_matrix(lgf[hh], lgb[hh], C)[0] for hh in heads]

        def outputs(c, carry):
            rows = pl.ds(pl.multiple_of(c * C, C), C)
            q_all = q_ref[rows, :].astype(F32)
            k = k_ref[rows, :]
            for hh in heads:
                q = q_all * hmask[hh]
                v = v_ref[rows, vs[hh]]
                s = _dot_nt(q.astype(BF16), k)
                inner = _dot((s * dmat[hh]).astype(BF16), v)
                states = jnp.concatenate([sf_scr[hh, c], sb_scr[hh, c]], axis=0).astype(BF16)
                ret = inner + _dot(_cat_lanes(q * t[hh]["q_f"], q * t[hh]["q_b"]), states)
                mu = jnp.mean(ret, axis=-1, keepdims=True)
                cen = ret - mu
                rstd = lax.rsqrt(jnp.mean(cen * cen, axis=-1, keepdims=True) + EPS)
                rn_ref[rows, vs[hh]] = cen * rstd
                rstd_ref[rows, vs[hh]] = jnp.broadcast_to(rstd, (C, DV))
            return carry

        lax.fori_loop(0, n_chunks, outputs, 0, unroll=2)

    qk0 = CB_QK * D // LANES
    return pl.pallas_call(
        body, name="retention_forward", grid=(N_HEADS // 2,),
        in_specs=[pl.BlockSpec(memory_space=pltpu.SMEM),
                  pl.BlockSpec((lext, LANES), lambda g: (0, qk0 + g)),
                  pl.BlockSpec((lext, LANES), lambda g: (0, qk0 + N_HEADS // 2 + g)),
                  pl.BlockSpec((lext, 2 * DV), lambda g: (0, CB_V * D // (2 * DV) + g))],
        out_specs=[pl.BlockSpec((seq, 2 * DV), lambda g: (0, g))] * 2,
        out_shape=[jax.ShapeDtypeStruct((seq, D), F32)] * 2,
        scratch_shapes=[pltpu.VMEM((2, n_chunks, 2 * LANES, DV), F32), pltpu.VMEM((2, n_chunks, LANES, DV), F32),
                        pltpu.VMEM((2, n_chunks, LANES, DV), F32)],
        compiler_params=_cparams(dimension_semantics=("arbitrary",)),
    )(lg, p_ext, p_ext, p_ext)


def _retention_backward(p_ext, dret, lg, cos_t, sin_t, seq, ctx_len, after):
    lext = seq + ctx_len
    n_chunks = seq // RET_C
    C = RET_C

    def body(lg_ref, q_ref, k_ref, v_ref, do_ref, cos_ref, sin_ref, after_ref, dq_ref, dk_ref, dv_ref, dlg_ref,
             kv_scr, g_scr, sf_scr, sb_scr, gfn_scr, gbp_scr):
        pair = pl.program_id(0)
        lane = lax.broadcasted_iota(jnp.int32, (1, LANES), 1)
        heads = range(2)
        hmask = [(lane // DK == hh).astype(F32) for hh in heads]
        lgf = [lg_ref[0, 2 * pair + hh] for hh in heads]
        lgb = [lg_ref[1, 2 * pair + hh] for hh in heads]
        vs = [slice(hh * DV, (hh + 1) * DV) for hh in heads]
        t = [_decay_tables(lgf[hh], lgb[hh], C) for hh in heads]
        tc = [_decay_tables(lgf[hh], lgb[hh], ctx_len) for hh in heads]
        kc_all = k_ref[seq:lext, :].astype(F32)
        kc = [kc_all * hmask[hh] for hh in heads]
        vc = [v_ref[seq:lext, vs[hh]] for hh in heads]
        kc_cat = [_cat_lanes(kc[hh] * tc[hh]["k_f"], kc[hh] * tc[hh]["k_b"]) for hh in heads]
        s0 = [_dot_tn(kc_cat[hh], vc[hh]) for hh in heads]

        def increments(c, carry):
            rows = pl.ds(pl.multiple_of(c * C, C), C)
            k_all = k_ref[rows, :].astype(F32)
            q_all = q_ref[rows, :].astype(F32)
            for hh in heads:
                k, q = k_all * hmask[hh], q_all * hmask[hh]
                kv_scr[hh, c] = _dot_tn(_cat_lanes(k * t[hh]["k_f"], k * t[hh]["k_b"]), v_ref[rows, vs[hh]])
                g_scr[hh, c] = _dot_tn(_cat_lanes(q * t[hh]["q_f"], q * t[hh]["q_b"]), do_ref[rows, vs[hh]])
            return carry

        lax.fori_loop(0, n_chunks, increments, 0, unroll=2)
        gf_c = [jnp.exp(lgf[hh] * C) for hh in heads]
        gb_c = [jnp.exp(lgb[hh] * C) for hh in heads]

        def scan_f(c, s):
            for hh in heads:
                sf_scr[hh, c] = s[hh]
            return tuple(gf_c[hh] * s[hh] + kv_scr[hh, c, 0:LANES, :] for hh in heads)

        def scan_b(n, s):
            c = n_chunks - 1 - n
            for hh in heads:
                sb_scr[hh, c] = s[hh]
            return tuple(gb_c[hh] * s[hh] + kv_scr[hh, c, LANES:2 * LANES, :] for hh in heads)

        def scan_gf(n, carry):
            c = n_chunks - 1 - n
            for hh in heads:
                gfn_scr[hh, c] = carry[hh]
            return tuple(g_scr[hh, c, 0:LANES, :] + gf_c[hh] * carry[hh] for hh in heads)

        def scan_gb(c, carry):
            for hh in heads:
                gbp_scr[hh, c] = carry[hh]
            return tuple(g_scr[hh, c, LANES:2 * LANES, :] + gb_c[hh] * carry[hh] for hh in heads)

        lax.fori_loop(0, n_chunks, scan_f, tuple(s0[hh][0:LANES] for hh in heads))
        lax.fori_loop(0, n_chunks, scan_b, tuple(s0[hh][LANES:2 * LANES] for hh in heads))
        zero_state = jnp.zeros((LANES, DV), F32)
        gf_first = lax.fori_loop(0, n_chunks, scan_gf, (zero_state, zero_state))
        gb_last = lax.fori_loop(0, n_chunks, scan_gb, (zero_state, zero_state))

        dmat, w_f, w_b = [], [], []
        for hh in heads:
            dm, diff = _decay_matrix(lgf[hh], lgb[hh], C)
            dmat.append(dm)
            w_f.append(jnp.where(diff > 0, diff * dm, 0.0))
            w_b.append(jnp.where(diff < 0, -diff * dm, 0.0))

        def total(a):
            rows_, width = a.shape
            part = jnp.sum(a.reshape(rows_ // 8, 8, width), axis=0)
            return part[:, 0:LANES] + part[:, LANES:2 * LANES] if width == 2 * LANES else part

        def chunk_head(hh, c, rows, q_all, k_all, dlgf, dlgb):
            th = t[hh]
            qm = q_all * hmask[hh]
            km = k_all * hmask[hh]
            qb, kb = qm.astype(BF16), km.astype(BF16)
            v = v_ref[rows, vs[hh]]
            do = do_ref[rows, vs[hh]]
            s = _dot_nt(qb, kb)
            dsd = _dot_nt(do, v)
            ds = (dsd * dmat[hh]).astype(BF16)
            dq_in = _dot(ds, kb)
            dk_in = _dot_tn(ds, qb)
            dv_in = _dot_tn((s * dmat[hh]).astype(BF16), do)
            prod = s * dsd
            dlgf = dlgf + total(prod * w_f[hh])
            dlgb = dlgb + total(prod * w_b[hh])
            sf, sb = sf_scr[hh, c], sb_scr[hh, c]
            states = jnp.concatenate([sf, sb], axis=0).astype(BF16)
            dqc = _dot_nt(do, states)
            dqf = dqc[:, 0:LANES] * th["q_f"]
            dqb = dqc[:, LANES:2 * LANES] * th["q_b"]
            dq = (dq_in + dqf + dqb) * hmask[hh]
            dlgf = dlgf + total((th["i"] + 1.0) * (qm * dqf))
            dlgb = dlgb + total((C - th["i"]) * (qm * dqb))
            gfn, gbp = gfn_scr[hh, c], gbp_scr[hh, c]
            gstates = jnp.concatenate([gfn, gbp], axis=0).astype(BF16)
            dkc = _dot_nt(v, gstates)
            dkf = dkc[:, 0:LANES] * th["k_f"]
            dkb = dkc[:, LANES:2 * LANES] * th["k_b"]
            dk = dk_in + dkf + dkb
            dlgf = dlgf + total((C - 1.0 - th["i"]) * (km * dkf)) + C * gf_c[hh] * total(gfn * sf)
            dlgb = dlgb + total(th["i"] * (km * dkb)) + C * gb_c[hh] * total(gbp * sb)
            dv = dv_in + _dot(_cat_lanes(km * th["k_f"], km * th["k_b"]), gstates)
            dv_ref[rows, vs[hh]] = dv.astype(BF16)
            return dq, dk, dlgf, dlgb

        def chunk(c, carry):
            rows = pl.ds(pl.multiple_of(c * C, C), C)
            q_all = q_ref[rows, :].astype(F32)
            k_all = k_ref[rows, :].astype(F32)
            dq0, dk0, f0, b0 = chunk_head(0, c, rows, q_all, k_all, carry[0], carry[1])
            dq1, dk1, f1, b1 = chunk_head(1, c, rows, q_all, k_all, carry[2], carry[3])
            cos, sin = cos_ref[rows, :], sin_ref[rows, :]
            dq_ref[rows, :] = _rope(dq0 + dq1, cos, -sin).astype(BF16)
            dk_ref[rows, :] = (_rope(dk0 + dk1, cos, -sin) * (DK ** -0.5)).astype(BF16)
            return f0, b0, f1, b1

        zero = jnp.zeros((8, LANES), F32)
        sums = lax.fori_loop(0, n_chunks, chunk, (zero, zero, zero, zero), unroll=2)

        dlg = []
        dk_ctx = jnp.zeros((ctx_len, LANES), F32)
        for hh in heads:
            g0 = jnp.concatenate([gf_first[hh], gb_last[hh]], axis=0).astype(BF16)
            dkcc = _dot_nt(vc[hh], g0)
            dkcf = dkcc[:, 0:LANES] * tc[hh]["k_f"]
            dkcb = dkcc[:, LANES:2 * LANES] * tc[hh]["k_b"]
            dlgf = sums[2 * hh] + total((ctx_len - 1.0 - tc[hh]["i"]) * (kc[hh] * dkcf))
            dlgb = sums[2 * hh + 1] + total(tc[hh]["i"] * (kc[hh] * dkcb))
            dk_ctx = dk_ctx + (dkcf + dkcb) * (DK ** -0.5)
            dv_ref[seq:lext, vs[hh]] = _dot(kc_cat[hh], g0).astype(BF16)
            dlg += [jnp.sum(jnp.sum(a, axis=1, keepdims=True), axis=0, keepdims=True) for a in (dlgf, dlgb)]
        dk_ref[seq:lext, :] = dk_ctx.astype(BF16)
        dq_ref[seq:lext, :] = jnp.zeros((ctx_len, LANES), BF16)

        lane8 = lax.broadcasted_iota(jnp.int32, (8, LANES), 1)
        out = jnp.zeros((8, LANES), F32)
        for n, val in enumerate(dlg):
            out = jnp.where(lane8 == n, val, out)
        dlg_ref[0] = out

    qk0 = CB_QK * D // LANES
    q_spec = pl.BlockSpec((lext, LANES), lambda g: (0, qk0 + g))
    k_spec = pl.BlockSpec((lext, LANES), lambda g: (0, qk0 + N_HEADS // 2 + g))
    v_spec = pl.BlockSpec((lext, 2 * DV), lambda g: (0, CB_V * D // (2 * DV) + g))
    table = pl.BlockSpec((lext, LANES), lambda g: (0, 0))
    state = pltpu.VMEM((2, n_chunks, LANES, DV), F32)
    return pl.pallas_call(
        body, name="retention_backward", grid=(N_HEADS // 2,),
        in_specs=[pl.BlockSpec(memory_space=pltpu.SMEM), q_spec, k_spec, v_spec,
                  pl.BlockSpec((seq, 2 * DV), lambda g: (0, g)), table, table, _AFTER_SPEC],
        out_specs=[pl.BlockSpec((lext, LANES), lambda g: (0, g)), pl.BlockSpec((lext, LANES), lambda g: (0, g)),
                   pl.BlockSpec((lext, 2 * DV), lambda g: (0, g)), pl.BlockSpec((1, 8, LANES), lambda g: (g, 0, 0))],
        out_shape=[jax.ShapeDtypeStruct((lext, N_HEADS * DK), BF16), jax.ShapeDtypeStruct((lext, N_HEADS * DK), BF16),
                   jax.ShapeDtypeStruct((lext, D), BF16), jax.ShapeDtypeStruct((N_HEADS // 2, 8, LANES), F32)],
        scratch_shapes=[pltpu.VMEM((2, n_chunks, 2 * LANES, DV), F32), pltpu.VMEM((2, n_chunks, 2 * LANES, DV), F32),
                        state, state, state, state],
        compiler_params=_cparams(dimension_semantics=("arbitrary",)),
    )(lg, p_ext, p_ext, p_ext, dret, cos_t, sin_t, after)


class _ColumnWriter:
    def __init__(self, dst_hbm, stage, sems, pieces, n_steps):
        self.dst, self.stage, self.sems, self.pieces, self.n_steps = dst_hbm, stage, sems, pieces, n_steps

    def _copies(self, slot, tile):
        rows = pl.ds(pl.multiple_of(tile * TM, TM), TM)
        return [pltpu.make_async_copy(self.stage.at[slot, :, pl.ds(src, width)], self.dst.at[rows, pl.ds(dst, width)],
                                      self.sems.at[slot, k]) for k, (src, dst, width) in enumerate(self.pieces)]

    def slot_for(self, step):
        slot = step % 2

        @pl.when(step >= 2)
        def _():
            for cp in self._copies(slot, step - 2):
                cp.wait()

        return slot

    def send(self, step, slot):
        for cp in self._copies(slot, step):
            cp.start()

        @pl.when(step == self.n_steps - 1)
        def _():
            if self.n_steps >= 2:
                for cp in self._copies(1 - slot, step - 1):
                    cp.wait()
            for cp in self._copies(slot, step):
                cp.wait()


def _merge(p_ext, rn, rstd, x, target, w_a, w_b, w_out, vecs, seq):
    n_tiles = seq // TM
    hb = TM // HALO
    dp_pieces = [(0, CB_BG * D, D), (D, CB_ZA * D, D), (2 * D, CB_ZB * D, 3 * D)]

    def body(h_ref, bg_ref, cg_ref, za_ref, zb_ref, ga_ref, gb_ref, hp_ref, hn_ref, cp_ref, cn_ref,
             rn_ref, rstd_ref, x_ref, t_ref, wa_ref, wb_ref, wo_ref, vec_ref,
             dx1_ref, dp_hbm, dconv_ref, dret_ref, at_ref, b_ref, part_ref, stage, dp_sems):
        i = pl.program_id(0)
        writer = _ColumnWriter(dp_hbm, stage, dp_sems, dp_pieces, n_tiles)
        slot = writer.slot_for(i)
        dpa_ref = stage.at[slot]
        f = lambda ref: ref[...].astype(F32)
        h, bg, cg, za, zb, ga, gb = f(h_ref), f(bg_ref), f(cg_ref), f(za_ref), f(zb_ref), f(ga_ref), f(gb_ref)
        gx, w0, w1, w2 = vec_ref[0:1, :], vec_ref[1:2, :], vec_ref[2:3, :], vec_ref[3:4, :]
        cb, gnw, fw = vec_ref[4:5, :], vec_ref[5:6, :], vec_ref[6:7, :]
        u = cg * h
        row = lax.broadcasted_iota(jnp.int32, (TM, 1), 0)
        u_prev = (f(cp_ref) * f(hp_ref))[HALO - 1:HALO, :]
        u_next = (f(cn_ref) * f(hn_ref))[0:1, :]
        u_prev = jnp.where(i == 0, 0.0, u_prev)
        u_next = jnp.where(i == n_tiles - 1, 0.0, u_next)
        u_up = jnp.where(row == 0, u_prev, pltpu.roll(u, 1, 0))
        u_dn = jnp.where(row == TM - 1, u_next, pltpu.roll(u, TM - 1, 0))
        conv = w0 * u_up + w1 * u + w2 * u_dn + cb
        sza = _sigmoid(za)
        silu_za = za * sza
        a_act = silu_za * bg * conv
        rn = rn_ref[...]
        szb = _sigmoid(zb)
        silu_zb = zb * szb
        rg = rn * gnw
        b_act = silu_zb * rg
        y_a = _dot(a_act.astype(BF16), wa_ref[...])
        y_b = _dot(b_act.astype(BF16), wb_ref[...])
        sga, sgb = _sigmoid(ga), _sigmoid(gb)
        mix = sga * y_a + sgb * y_b
        y = _dot(mix.astype(BF16), wo_ref[...])
        x1 = x_ref[...] + gx * y
        r1 = lax.rsqrt(jnp.mean(x1 * x1, axis=-1, keepdims=True) + EPS)
        xh1 = x1 * r1
        err = xh1 * fw - t_ref[...]
        loss = jnp.sum(jnp.sum(err * err, axis=1, keepdims=True), axis=0, keepdims=True) * (0.5 / D)
        dxh = err * (fw * (1.0 / D))
        dx1 = r1 * (dxh - xh1 * jnp.mean(dxh * xh1, axis=-1, keepdims=True))
        dx1_ref[...] = dx1
        dy = (dx1 * gx).astype(BF16)
        dmix = _dot_nt(dy, wo_ref[...])
        dya_f, dyb_f = dmix * sga, dmix * sgb
        dya, dyb = dya_f.astype(BF16), dyb_f.astype(BF16)
        da = _dot_nt(dya, wa_ref[...])
        db = _dot_nt(dyb, wb_ref[...])
        da_silu = da * silu_za
        dpa_ref[:, 0:D] = (da_silu * conv).astype(BF16)
        dpa_ref[:, D:2 * D] = (da * bg * conv * (sza + silu_za * (1.0 - sza))).astype(BF16)
        dpa_ref[:, 2 * D:3 * D] = (db * rg * (szb + silu_zb * (1.0 - szb))).astype(BF16)
        dpa_ref[:, 3 * D:4 * D] = (dya_f * y_a * (1.0 - sga)).astype(BF16)
        dpa_ref[:, 4 * D:5 * D] = (dyb_f * y_b * (1.0 - sgb)).astype(BF16)
        dconv_ref[...] = (da_silu * bg).astype(BF16)
        drn_n = db * silu_zb
        drn = drn_n * gnw
        rstd = rstd_ref[...]
        for hd in range(N_HEADS):
            sl = slice(hd * DV, (hd + 1) * DV)
            dh_, rh = drn[:, sl], rn[:, sl]
            m1 = jnp.mean(dh_, axis=-1, keepdims=True)
            m2 = jnp.mean(dh_ * rh, axis=-1, keepdims=True)
            dret_ref[:, sl] = (rstd[:, sl] * (dh_ - m1 - rh * m2)).astype(BF16)
        at_ref[0] = a_act.T.astype(BF16)
        at_ref[1] = b_act.T.astype(BF16)
        at_ref[2] = mix.T.astype(BF16)
        b_ref[0] = dya
        b_ref[1] = dyb
        b_ref[2] = dy

        @pl.when(i == 0)
        def _():
            part_ref[...] = jnp.zeros((8, D), F32)

        part_ref[0:1, :] += jnp.sum(err * xh1, axis=0, keepdims=True) * (1.0 / D)
        part_ref[1:2, :] += jnp.sum(dx1 * y, axis=0, keepdims=True)
        part_ref[2:3, :] += jnp.sum(drn_n * rn, axis=0, keepdims=True)
        part_ref[3:4, :] += jnp.broadcast_to(loss, (1, D))
        writer.send(i, slot)

    col = lambda cb_: pl.BlockSpec((TM, D), lambda i, cb_=cb_: (i, cb_))
    prev = lambda cb_: pl.BlockSpec((HALO, D), lambda i, cb_=cb_: (jnp.maximum(i * hb - 1, 0), cb_))
    nxt = lambda cb_: pl.BlockSpec((HALO, D), lambda i, cb_=cb_: (jnp.minimum((i + 1) * hb, n_tiles * hb - 1), cb_))
    tile = pl.BlockSpec((TM, D), lambda i: (i, 0))
    full = lambda a: pl.BlockSpec(a.shape, lambda i: (0,) * a.ndim, pipeline_mode=pl.Buffered(1))
    return pl.pallas_call(
        body, name="merge", grid=(n_tiles,),
        in_specs=[col(CB_H), col(CB_BG), col(CB_CG), col(CB_ZA), col(CB_ZB), col(CB_GA), col(CB_GB),
                  prev(CB_H), nxt(CB_H), prev(CB_CG), nxt(CB_CG),
                  tile, tile, tile, tile, full(w_a), full(w_b), full(w_out), full(vecs)],
        out_specs=[tile, pl.BlockSpec(memory_space=pl.ANY), tile, tile,
                   pl.BlockSpec((3, D, TM), lambda i: (0, 0, i)), pl.BlockSpec((3, TM, D), lambda i: (0, i, 0)),
                   pl.BlockSpec((8, D), lambda i: (0, 0))],
        out_shape=[jax.ShapeDtypeStruct((seq, D), F32), jax.ShapeDtypeStruct((p_ext.shape[0], PW), BF16),
                   jax.ShapeDtypeStruct((seq, D), BF16), jax.ShapeDtypeStruct((seq, D), BF16),
                   jax.ShapeDtypeStruct((3, D, seq), BF16), jax.ShapeDtypeStruct((3, seq, D), BF16),
                   jax.ShapeDtypeStruct((8, D), F32)],
        scratch_shapes=[pltpu.VMEM((2, TM, 5 * D), BF16), pltpu.SemaphoreType.DMA((2, len(dp_pieces)))],
        compiler_params=_cparams(dimension_semantics=("arbitrary",)),
    )(p_ext, p_ext, p_ext, p_ext, p_ext, p_ext, p_ext, p_ext, p_ext, p_ext, p_ext,
      rn, rstd, x, target, w_a, w_b, w_out, vecs)


def _conv_backward(p_ext, dconv, dp, dq, dk, dv, vecs, seq, after):
    n_tiles = seq // TM
    hb = TM // HALO
    qk = N_HEADS * DK
    pieces = [(0, CB_H * D, D), (D, CB_CG * D, D), (2 * D, CB_QK * D, 2 * D)]
    zero_pieces = [(CB_BG * D, D), (CB_ZA * D, D), (CB_ZB * D, 3 * D)]

    def body(h_ref, cg_ref, dc_ref, dcp_ref, dcn_ref, dq_ref, dk_ref, dv_ref, vec_ref, dp_in, after_ref, dp_hbm, part_ref,
             stage, sems, zeros, zero_sems):
        del dp_in, after_ref
        i = pl.program_id(0)
        writer = _ColumnWriter(dp_hbm, stage, sems, pieces, n_tiles + 1)
        slot = writer.slot_for(i)
        out = stage.at[slot]
        out[:, 2 * D:2 * D + qk] = dq_ref[...]
        out[:, 2 * D + qk:3 * D] = dk_ref[...]
        out[:, 3 * D:4 * D] = dv_ref[...]

        @pl.when(i == 0)
        def _():
            part_ref[...] = jnp.zeros((8, D), F32)

        @pl.when(i == n_tiles)
        def _():
            out[:, 0:2 * D] = jnp.zeros((TM, 2 * D), BF16)
            zeros[...] = jnp.zeros(zeros.shape, BF16)
            rows = pl.ds(n_tiles * TM, TM)
            fills = [pltpu.make_async_copy(zeros.at[:, pl.ds(0, width)], dp_hbm.at[rows, pl.ds(dst, width)],
                                           zero_sems.at[k]) for k, (dst, width) in enumerate(zero_pieces)]
            for cp in fills:
                cp.start()
            for cp in fills:
                cp.wait()

        @pl.when(i < n_tiles)
        def _():
            f = lambda ref: ref[...].astype(F32)
            h, cg, dc = f(h_ref), f(cg_ref), f(dc_ref)
            w0, w1, w2 = vec_ref[1:2, :], vec_ref[2:3, :], vec_ref[3:4, :]
            row = lax.broadcasted_iota(jnp.int32, (TM, 1), 0)
            dc_prev = jnp.where(i == 0, 0.0, f(dcp_ref)[HALO - 1:HALO, :])
            dc_next = jnp.where(i == n_tiles - 1, 0.0, f(dcn_ref)[0:1, :])
            dc_up = jnp.where(row == 0, dc_prev, pltpu.roll(dc, 1, 0))
            dc_dn = jnp.where(row == TM - 1, dc_next, pltpu.roll(dc, TM - 1, 0))
            du = w0 * dc_dn + w1 * dc + w2 * dc_up
            u = cg * h
            part_ref[0:1, :] += jnp.sum(u * dc_dn, axis=0, keepdims=True)
            part_ref[1:2, :] += jnp.sum(u * dc, axis=0, keepdims=True)
            part_ref[2:3, :] += jnp.sum(u * dc_up, axis=0, keepdims=True)
            part_ref[3:4, :] += jnp.sum(dc, axis=0, keepdims=True)
            out[:, 0:D] = (du * cg).astype(BF16)
            out[:, D:2 * D] = (du * h).astype(BF16)

        writer.send(i, slot)

    last = n_tiles - 1
    col = lambda cb_: pl.BlockSpec((TM, D), lambda i, cb_=cb_: (jnp.minimum(i, last), cb_))
    lat = lambda w: pl.BlockSpec((TM, w), lambda i: (jnp.minimum(i, last), 0))
    ext = lambda w: pl.BlockSpec((TM, w), lambda i: (i, 0))
    anyspec = pl.BlockSpec(memory_space=pl.ANY)
    return pl.pallas_call(
        body, name="conv_backward", grid=(n_tiles + 1,),
        in_specs=[col(CB_H), col(CB_CG), lat(D),
                  pl.BlockSpec((HALO, D), lambda i: (jnp.clip(i * hb - 1, 0, n_tiles * hb - 1), 0)),
                  pl.BlockSpec((HALO, D), lambda i: (jnp.minimum((i + 1) * hb, n_tiles * hb - 1), 0)),
                  ext(qk), ext(qk), ext(D), pl.BlockSpec(vecs.shape, lambda i: (0, 0)), anyspec, _AFTER_SPEC],
        out_specs=[anyspec, pl.BlockSpec((8, D), lambda i: (0, 0))],
        out_shape=[jax.ShapeDtypeStruct(dp.shape, BF16), jax.ShapeDtypeStruct((8, D), F32)],
        input_output_aliases={9: 0},
        scratch_shapes=[pltpu.VMEM((2, TM, 4 * D), BF16), pltpu.SemaphoreType.DMA((2, len(pieces))),
                        pltpu.VMEM((TM, 3 * D), BF16), pltpu.SemaphoreType.DMA((len(zero_pieces),))],
        compiler_params=_cparams(dimension_semantics=("arbitrary",)),
    )(p_ext, p_ext, dconv, dconv, dconv, dq, dk, dv, vecs, dp, after)


def _input_backward(dp, shards, ids, x, ctx, dx1, modx, modc, norm_w, after):
    seq = x.shape[0]
    lext = seq + ctx.shape[0]
    n_x = seq // TM
    n_w = len(shards.arrays)

    def body(ids_ref, dp_ref, x_ref, ctx_ref, dx1_ref, mx_ref, mc_ref, nw_ref, *rest):
        w_hbm, (_, gx_ref, part_ref, w_ref, w_sems) = rest[:n_w], rest[n_w:]
        i = pl.program_id(0)
        _load_w_in(i, ids_ref, shards, w_hbm, w_ref, w_sems)
        is_ctx = i >= n_x
        dxm = _dot_nt(dp_ref[...], w_ref[...])
        x = jnp.where(is_ctx, ctx_ref[...], x_ref[...])
        r = lax.rsqrt(jnp.mean(x * x, axis=-1, keepdims=True) + EPS)
        xh = x * r
        nw = nw_ref[...]
        sc = jnp.where(is_ctx, mc_ref[1:2, :], mx_ref[1:2, :])
        dxn = dxm * (1.0 + sc)
        dxh = dxn * nw
        dx = r * (dxh - xh * jnp.mean(dxh * xh, axis=-1, keepdims=True))

        @pl.when(jnp.logical_not(is_ctx))
        def _():
            gx_ref[...] = dx1_ref[...] + dx

        @pl.when(i == 0)
        def _():
            part_ref[...] = jnp.zeros((8, D), F32)

        fx = jnp.where(is_ctx, 0.0, 1.0)
        d_shift = jnp.sum(dxm, axis=0, keepdims=True)
        d_scale = jnp.sum(dxm * (xh * nw), axis=0, keepdims=True)
        part_ref[0:1, :] += fx * d_shift
        part_ref[1:2, :] += fx * d_scale
        part_ref[2:3, :] += jnp.sum(dxn * xh, axis=0, keepdims=True)
        part_ref[3:4, :] += (1.0 - fx) * d_shift
        part_ref[4:5, :] += (1.0 - fx) * d_scale

    lat = lambda w: pl.BlockSpec((TM, w), lambda i, ids_: (jnp.minimum(i, n_x - 1), 0))
    ext = lambda w: pl.BlockSpec((TM, w), lambda i, ids_: (i, 0))
    full = lambda a: pl.BlockSpec(a.shape, lambda i, ids_: (0,) * a.ndim)
    grid_spec = pltpu.PrefetchScalarGridSpec(
        num_scalar_prefetch=1, grid=(lext // TM,),
        in_specs=[ext(PW), lat(D), full(ctx), lat(D), full(modx), full(modc), full(norm_w)]
        + [pl.BlockSpec(memory_space=pl.ANY)] * n_w + [_AFTER_SPEC],
        out_specs=[lat(D), pl.BlockSpec((8, D), lambda i, ids_: (0, 0))],
        scratch_shapes=[pltpu.VMEM((D, PW), BF16), pltpu.SemaphoreType.DMA((N_DEV,))])
    return pl.pallas_call(
        body, name="input_backward", grid_spec=grid_spec,
        out_shape=[jax.ShapeDtypeStruct((seq, D), F32), jax.ShapeDtypeStruct((8, D), F32)],
        compiler_params=_cparams(dimension_semantics=("arbitrary",)),
    )(ids, dp, x, ctx, dx1, modx, modc, norm_w, *shards.arrays, after)


def _weight_grad_in(xm_t, dp, owners, name, after, dtype=F32):
    lext = xm_t.shape[1]

    def body(own_ref, a_ref, b_ref, after_ref, o_ref):
        o_ref[0] = _dot(a_ref[...], b_ref[...]).astype(dtype)

    grid_spec = pltpu.PrefetchScalarGridSpec(
        num_scalar_prefetch=1, grid=(4,),
        in_specs=[pl.BlockSpec((D, lext), lambda j, own: (0, 0)),
                  pl.BlockSpec((lext, WSH), lambda j, own: (0, own[j])), _AFTER_SPEC],
        out_specs=pl.BlockSpec((1, D, WSH), lambda j, own: (j, 0, 0)))
    return pl.pallas_call(
        body, name=name, grid_spec=grid_spec,
        out_shape=jax.ShapeDtypeStruct((4, D, WSH), dtype),
        compiler_params=_cparams(dimension_semantics=("arbitrary",)),
    )(owners, xm_t, dp, after)


def _weight_grad_square(a_t, b):
    seq = a_t.shape[2]

    def body(a_ref, b_ref, o_ref):
        o_ref[:, 0] = _dot(a_ref[0], b_ref[0]).reshape(N_DEV, RSH, D)

    return pl.pallas_call(
        body, name="weight_grad_square", grid=(3,),
        in_specs=[pl.BlockSpec((1, D, seq), lambda t: (t, 0, 0)), pl.BlockSpec((1, seq, D), lambda t: (t, 0, 0))],
        out_specs=pl.BlockSpec((N_DEV, 1, RSH, D), lambda t: (0, t, 0, 0)),
        out_shape=jax.ShapeDtypeStruct((N_DEV, 3, RSH, D), F32),
        compiler_params=_cparams(dimension_semantics=("arbitrary",)),
    )(a_t, b)


def _rows(total, *parts):
    width = max(a.shape[1] for _, a in parts)
    out = None
    for row, a in parts:
        padded = jnp.pad(a, ((row, total - row - a.shape[0]), (0, width - a.shape[1])))
        out = padded if out is None else out + padded
    return out


def kernel(x, c, ctx, c_ctx, norm_w, ada_w, ada_b, w_in, conv_w, conv_b, decay_logit, gn_w, w_a, w_b, w_out, final_norm_w, loss_target, m_c_ctx, m_norm_w, m_ada_w, m_ada_b, m_w_in, m_conv_w, m_conv_b, m_decay_logit, m_gn_w, m_w_a, m_w_b, m_w_out, m_final_norm_w, v_c_ctx, v_norm_w, v_ada_w, v_ada_b, v_w_in, v_conv_w, v_conv_b, v_decay_logit, v_gn_w, v_w_a, v_w_b, v_w_out, v_final_norm_w):
    xi, yi, ci = _coords()
    me = 4 * xi + 2 * yi + ci
    chip = 2 * xi + yi
    seq, ctx_len = x.shape[1], ctx.shape[1]
    assert seq % TM == 0 and seq % RET_C == 0 and ctx_len == TM and seq % GRID_W == 0
    csh = D // N_DEV

    blk = jnp.pad(c, ((0, 7), (0, 0))) + jnp.pad(conv_w[0], ((1, 4), (0, D - csh)))
    got, _ = _all_gather_small(blk, "gather_cond")
    conv_w_all = got[:, 1:4, 0:csh].transpose(1, 0, 2).reshape(3, D)
    c16 = _rows(16, (0, got[:, 0, :]), (N_DEV, c_ctx[None]))
    ada_b_sh = lax.dynamic_slice(ada_b, (0, me * ADA_SH), (1, ADA_SH))
    mod_sh, act16, lg = _modulation(c16, ada_w[0], ada_b_sh, decay_logit[0])
    mod_all, small_done = _all_gather_small(mod_sh, "gather_mod")
    mod_all = mod_all.transpose(1, 0, 2).reshape(16, 3 * D)
    modx = lax.dynamic_slice(mod_all, (me, 0), (1, 3 * D)).reshape(3, D)
    modc = mod_all[8].reshape(3, D)

    x2, tgt = x[0], loss_target[0]
    wblock = lambda n: jax.ShapeDtypeStruct((n, D, WSH), BF16)
    ids_of = lambda ks: jnp.stack([jnp.bitwise_xor(me, k) for k in ks]).astype(jnp.int32)
    own_shard = w_in[0].astype(BF16)[None]
    sq_shards = [w[0].astype(BF16) for w in (w_a, w_b, w_out)]
    wave1 = _Exchange(
        "gather_in_first", [jnp.broadcast_to(own_shard, (2, D, WSH)), own_shard], [wblock(4)], 3,
        lambda srcs, lands: _forward_plan([(0, 1)])(srcs, lands) + _send_plan((4, 2))(srcs[1:], lands),
        after=small_done)
    xm, xm_t, cos_t, sin_t = _prenorm(x2, ctx[0], modx, modc, norm_w, wave1.token)
    (w_pair, _), _ = wave1.wait(xm_t, 1, through=[0])
    first = _Shards([w_pair], [(0, 0), (0, 1)])
    p_ext = _in_projection(None, xm, cos_t, sin_t, w_pair, [((0, 0), (1, 1))], me, "in_projection_pair")
    _, (w_nbr,) = wave1.wait(p_ext, through=[1, 2])
    wave2 = _Exchange(
        "gather_in_second", [w_nbr, own_shard], [wblock(2)], 3,
        lambda srcs, lands: _forward_plan([(0, 2), (1, 3)])(srcs, lands) + _send_plan((6,))(srcs[1:], lands))
    (w_nbr, _), _ = wave2.wait(wave2.token, 2, through=[0])
    second = _Shards([w_nbr], [(0, 0), (0, 1), (0, 2), (0, 3)])
    p_ext = _in_projection(p_ext, xm, cos_t, sin_t, w_nbr, [((0, 4), (2, 5)), ((1, 2), (3, 3))], me,
                           "in_projection_neighbours")
    _, (w_diag,) = wave2.wait(p_ext, through=[1, 2])
    wave3 = _Exchange(
        "gather_in_third", [w_diag] + sq_shards, [jax.ShapeDtypeStruct((N_DEV, RSH, D), BF16)] * 3, 1 + 3 * (N_DEV - 1),
        lambda srcs, lands: _forward_plan([(0, 1)])(srcs, lands) + _gather_plan(srcs[1:], lands))
    (w_diag, *_), _ = wave3.wait(wave3.token, 1, through=[0])
    third = _Shards([w_diag], [(0, 0), (0, 1)])
    p_ext = _in_projection(p_ext, xm, cos_t, sin_t, w_diag, [((0, 6), (1, 7))], me, "in_projection_diagonal")
    w_shards, w_ids = first + second + third, ids_of((0, 1, 4, 2, 5, 3, 6, 7))

    rn, rstd = _retention_forward(p_ext, lg, seq, ctx_len)
    _, sq_lands = wave3.wait(rstd, through=range(1, 7))
    w_a_all, w_b_all, w_out_all = (
        lax.dynamic_update_slice(land, shard[None], (me, 0, 0)).reshape(D, D) for land, shard in zip(sq_lands, sq_shards))
    vecs = _rows(8, (0, modx[2:3]), (1, conv_w_all), (4, conv_b), (5, gn_w), (6, final_norm_w[None]))
    dx1, dp, dconv, dret, op_at, op_b, part_m = _merge(p_ext, rn, rstd, x2, tgt, w_a_all, w_b_all, w_out_all, vecs, seq)

    j4 = jnp.arange(4, dtype=jnp.int32)
    owners = (2 * jnp.bitwise_xor(chip, j4) + ci).astype(jnp.int32)
    owners_sib = (2 * jnp.bitwise_xor(chip, j4) + (1 - ci)).astype(jnp.int32)
    gw_sq = _weight_grad_square(op_at, op_b).reshape(N_DEV, 3 * RSH, D)
    rs_sq_pair = _Exchange("rs_square_pair", [gw_sq], [jax.ShapeDtypeStruct((4, 3 * RSH, D), F32)], 4,
                           _pair_plan(lambda j, chip_, c_: 2 * jnp.bitwise_xor(chip_, j) + (1 - c_)))
    dq, dk, dv, dlg = _retention_backward(p_ext, dret, lg, cos_t, sin_t, seq, ctx_len, rs_sq_pair.token)
    (gw_sq,), (r1_sq,) = rs_sq_pair.wait(dlg)
    own_sq, send_sq = _pair_sum(gw_sq, r1_sq, owners, "pair_sum_square")
    rs_sq_chips = _Exchange("rs_square_chips", [send_sq], [jax.ShapeDtypeStruct((3, 3 * RSH, D), BF16)], 3, _chips_plan)
    dp, part_c = _conv_backward(p_ext, dconv, dp, dq, dk, dv, vecs, seq, rs_sq_chips.token)
    gw_sib = _weight_grad_in(xm_t, dp, owners_sib, "weight_grad_in_sibling", part_c, BF16)
    rs_in_pair = _Exchange("rs_in_pair", [gw_sib], [jax.ShapeDtypeStruct((4, D, WSH), BF16)], 4,
                           _pair_plan(lambda j, chip_, c_: j))
    gw_own = _weight_grad_in(xm_t, dp, owners, "weight_grad_in_own", rs_in_pair.token)
    _, (r1_in,) = rs_in_pair.wait(gw_own)
    own_in, send_in = _pair_sum(gw_own, r1_in, j4, "pair_sum_in")
    rs_in_chips = _Exchange("rs_in_chips", [send_in], [jax.ShapeDtypeStruct((3, D, WSH), BF16)], 3, _chips_plan)
    grad_x, part_i = _input_backward(dp, w_shards, w_ids, x2, ctx[0], dx1, modx, modc, norm_w, rs_in_chips.token)

    dl = dlg[:, 0, 0:4]
    dlg_row = jnp.pad(dl[:, 0::2].reshape(1, N_HEADS), ((0, 0), (0, D - N_HEADS))) + jnp.pad(
        dl[:, 1::2].reshape(1, N_HEADS), ((0, 0), (N_HEADS, D - 2 * N_HEADS)))
    partials = _rows(16, (0, part_m[0:1]), (1, part_i[2:3]), (2, part_c[3:4]), (3, part_m[2:3]), (4, part_c[0:3]),
                     (7, part_i[0:2]), (9, part_m[1:2]), (10, part_i[3:5]), (13, part_m[3:4]), (14, dlg_row))
    results = {}
    gather_part = _Exchange("gather_partials", [partials], [jax.ShapeDtypeStruct((N_DEV, 16, D), F32)],
                            N_DEV - 1, _gather_plan)
    _, (r2_in,) = rs_in_chips.wait(gather_part.token)
    results["w_in"] = [o[None] for o in _adam_sharded(w_in[0], m_w_in[0], v_w_in[0], own_in, r2_in, "adam_w_in")]
    _, (got,) = gather_part.wait(results["w_in"][1])
    got = lax.dynamic_update_slice(got, partials[None], (me, 0, 0))
    tot = _sum_devices(got, "sum_partials")
    dmodc = tot[10:13].reshape(1, 3 * D)
    dmod16 = _rows(16, (0, got[:, 7:10, :].reshape(N_DEV, 3 * D)), (N_DEV, dmodc))
    dmod16 = lax.dynamic_slice(dmod16, (0, me * ADA_SH), (16, ADA_SH))
    dmodc8 = _rows(8, (0, dmod16[8:9]))
    g_ada_w, d_ada_w, nm_ada_w, nv_ada_w, cctx_part = _ada_backward(
        act16.T, dmod16, dmodc8, ada_w[0], m_ada_w[0], v_ada_w[0])
    gather_cctx = _Exchange("gather_cctx", [cctx_part], [jax.ShapeDtypeStruct((N_DEV, 8, D), F32)],
                            N_DEV - 1, _gather_plan)
    _, (r2_sq,) = rs_sq_chips.wait(gather_cctx.token)
    square = _adam_square([(w_a[0], m_w_a[0], v_w_a[0]), (w_b[0], m_w_b[0], v_w_b[0]),
                           (w_out[0], m_w_out[0], v_w_out[0])], own_sq, r2_sq)
    _, (got_cctx,) = gather_cctx.wait(square[0][1])
    dact_ctx = _sum_devices(lax.dynamic_update_slice(got_cctx, cctx_part[None], (me, 0, 0)), "sum_cctx")

    view = {"c_ctx": (1, D), "final_norm_w": (1, D)}
    given = {"c_ctx": (c_ctx, m_c_ctx, v_c_ctx), "norm_w": (norm_w, m_norm_w, v_norm_w),
             "conv_b": (conv_b, m_conv_b, v_conv_b), "gn_w": (gn_w, m_gn_w, v_gn_w),
             "final_norm_w": (final_norm_w, m_final_norm_w, v_final_norm_w), "ada_b": (ada_b, m_ada_b, v_ada_b),
             "decay_logit": (decay_logit, m_decay_logit, v_decay_logit), "conv_w": (conv_w, m_conv_w, v_conv_w)}
    small_in = [tuple(a.reshape(view.get(name, a.shape)) for a in given[name]) for name in SMALL_PARAMS]
    conv_w_grad = lax.dynamic_slice(tot, (4, me * csh), (3, csh))
    small_out = _adam_small(tot, dact_ctx, conv_w_grad, tot[14, 0:2 * N_HEADS].reshape(2, N_HEADS), small_in)
    results.update({name: [o.reshape(given[name][0].shape) for o in outs_]
                    for name, outs_ in zip(SMALL_PARAMS, small_out)})
    for name, outs_ in zip(("w_a", "w_b", "w_out"), square):
        results[name] = [o[None] for o in outs_]
    results["ada_w"] = [o[None] for o in (g_ada_w, d_ada_w, nm_ada_w, nv_ada_w)]

    order = ("c_ctx", "norm_w", "ada_w", "ada_b", "w_in", "conv_w", "conv_b", "decay_logit", "gn_w",
             "w_a", "w_b", "w_out", "final_norm_w")
    outs = [results[name][kind] for kind in range(4) for name in order]
    return (tot[13, 0], grad_x[None], *outs)
```

```python
import math

import jax
import jax.numpy as jnp
from jax import lax
from jax.experimental import pallas as pl
from jax.experimental.pallas import tpu as pltpu

F32 = jnp.float32
BF16 = jnp.bfloat16
MESH = pl.DeviceIdType.MESH
_HBM_SPEC = pl.BlockSpec(memory_space=pltpu.HBM)
_SEM_SPEC = pl.BlockSpec(memory_space=pltpu.SEMAPHORE)
_EFFECT = pltpu.SideEffectType.DATAFLOW_SIDE_EFFECTING

N_DEV = 8
D = 1024
N_HEADS = 8
DK = 64
DV = 128
GRID_W = 64
ROPE_BASE = 10000.0
EPS = 1e-6
PW = 9 * D
WSH = PW // N_DEV
RSH = D // N_DEV
ADA_SH = 3 * D // N_DEV
TM = 256
RET_C = 256
HALO = 16
LANES = 128
VMEM_LIMIT = 60 * 1024 * 1024

ADAM_LR = 0.001
ADAM_B1 = 0.9
ADAM_B2 = 0.999
ADAM_EPS = 1e-08
ADAM_WD = 0.01
ADAM_STEP = 10

CB_H, CB_BG, CB_CG, CB_ZA, CB_QK, CB_V, CB_ZB, CB_GA, CB_GB = range(9)


def _cparams(**kw):
    return pltpu.CompilerParams(vmem_limit_bytes=VMEM_LIMIT, **kw)


def _dot(a, b):
    return jnp.dot(a, b, preferred_element_type=F32)


def _dot_nt(a, b):
    return lax.dot_general(a, b, (((1,), (1,)), ((), ())), preferred_element_type=F32)


def _dot_tn(a, b):
    return lax.dot_general(a, b, (((0,), (0,)), ((), ())), preferred_element_type=F32)


def _sigmoid(z):
    return 0.5 * jnp.tanh(0.5 * z) + 0.5


def _row_tile(rows):
    return TM if rows % TM == 0 else rows


def _coords():
    return lax.axis_index("x"), lax.axis_index("y"), lax.axis_index("c")


def _flip(v, bit):
    return 1 - v if bit else v


def _all_gather_small(blk, name):
    rows, cols = blk.shape

    def body(x_ref, out_ref, done, send_sems, recv_sems, local_sem):
        x, y, c = _coords()
        me = 4 * x + 2 * y + c
        mine = pltpu.make_async_copy(x_ref, out_ref.at[me], local_sem)
        mine.start()

        def copy(k, slot):
            peer = (_flip(x, k & 4), _flip(y, k & 2), _flip(c, k & 1))
            return pltpu.make_async_remote_copy(
                src_ref=x_ref, dst_ref=out_ref.at[slot], send_sem=send_sems.at[k - 1],
                recv_sem=recv_sems.at[k - 1], device_id=peer, device_id_type=MESH)

        for k in range(1, N_DEV):
            copy(k, me).start()
        for k in range(1, N_DEV):
            copy(k, jnp.bitwise_xor(me, k)).wait_recv()
        for k in range(1, N_DEV):
            copy(k, me).wait_send()
        mine.wait()
        done[...] = jnp.zeros((8, LANES), F32)

    vmem = pl.BlockSpec(memory_space=pltpu.VMEM)
    return pl.pallas_call(
        body, name=name,
        out_shape=[jax.ShapeDtypeStruct((N_DEV, rows, cols), blk.dtype), jax.ShapeDtypeStruct((8, LANES), F32)],
        in_specs=[vmem], out_specs=[vmem, vmem],
        scratch_shapes=[pltpu.SemaphoreType.DMA((N_DEV - 1,)), pltpu.SemaphoreType.DMA((N_DEV - 1,)),
                        pltpu.SemaphoreType.DMA],
    )(blk)


class _Absent:
    at = property(lambda self: self)

    def __getitem__(self, index):
        return self


class _Exchange:
    def __init__(self, name, srcs, land_shapes, n_copies, plan, after=None):
        self.name, self.plan, self.n_copies = name, plan, n_copies
        self.n_src, self.n_land = len(srcs), len(land_shapes)
        hbm = lambda a: pltpu.HBM(a.shape, a.dtype)
        n = self.n_src + self.n_land
        lands = [pltpu.with_memory_space_constraint(lax.empty(s.shape, s.dtype), pltpu.HBM) for s in land_shapes]
        srcs = [pltpu.with_memory_space_constraint(s, pltpu.HBM) for s in srcs]
        extra = [] if after is None else [after]

        def body(*refs):
            send_sems, recv_sems = refs[n + len(extra)], refs[n + len(extra) + 1]
            for cp in self._copies(refs, send_sems, recv_sems):
                cp.start()
            refs[-1][...] = jnp.zeros((8, LANES), F32)

        outs = pl.pallas_call(
            body, name=name + "_start",
            out_shape=(pltpu.SemaphoreType.DMA((n_copies,)), pltpu.SemaphoreType.DMA((n_copies,)),
                       *[hbm(a) for a in srcs], *[hbm(a) for a in lands], jax.ShapeDtypeStruct((8, LANES), F32)),
            in_specs=[_HBM_SPEC] * n + [pl.BlockSpec(memory_space=pl.ANY)] * len(extra),
            out_specs=(_SEM_SPEC, _SEM_SPEC, *[_HBM_SPEC] * n, pl.BlockSpec(memory_space=pltpu.VMEM)),
            input_output_aliases={i: 2 + i for i in range(n)},
            compiler_params=pltpu.CompilerParams(has_side_effects=_EFFECT),
        )(*srcs, *lands, *extra)
        self.send_sems, self.recv_sems = outs[0], outs[1]
        self.buffers = list(outs[2:2 + n])
        self.token = outs[-1]
        self.waited = 0

    def _copies(self, refs, send_sems, recv_sems, lo=0, hi=None):
        src_refs, land_refs = refs[:self.n_src], refs[self.n_src:self.n_src + self.n_land]
        planned = list(enumerate(self.plan(src_refs, land_refs)))[lo:hi]
        return [pltpu.make_async_remote_copy(src_ref=s, dst_ref=d, send_sem=send_sems.at[k], recv_sem=recv_sems.at[k],
                                             device_id=dev, device_id_type=MESH) for k, (s, d, dev) in planned]

    def wait(self, after, count=None, through=None):
        n = self.n_src + self.n_land
        lo = self.waited
        hi = self.n_copies if count is None else lo + count
        self.waited = hi
        through = list(range(n)) if through is None else list(through)
        m = len(through)

        def body(*refs):
            held = [_Absent()] * n
            for pos, ref in zip(through, refs[:m]):
                held[pos] = ref
            for cp in self._copies(held, refs[m], refs[m + 1], lo, hi):
                cp.wait_send()
                cp.wait_recv()

        outs = pl.pallas_call(
            body, name=f"{self.name}_wait{lo}" if lo or hi < self.n_copies else self.name + "_wait",
            out_shape=tuple(pltpu.HBM(self.buffers[p].shape, self.buffers[p].dtype) for p in through),
            in_specs=[_HBM_SPEC] * m + [_SEM_SPEC, _SEM_SPEC, pl.BlockSpec(memory_space=pl.ANY)],
            out_specs=tuple([_HBM_SPEC] * m),
            input_output_aliases={i: i for i in range(m)},
            compiler_params=pltpu.CompilerParams(has_side_effects=_EFFECT),
        )(*[self.buffers[p] for p in through], self.send_sems, self.recv_sems, after)
        for pos, out in zip(through, outs):
            self.buffers[pos] = out
        return list(self.buffers[:self.n_src]), list(self.buffers[self.n_src:])


def _pair_plan(src_index):
    def plan(srcs, lands):
        x, y, c = _coords()
        chip = 2 * x + y
        return [(srcs[0].at[src_index(j, chip, c)], lands[0].at[j], (x, y, 1 - c)) for j in range(4)]
    return plan


def _chips_plan(srcs, lands):
    x, y, c = _coords()
    return [(srcs[0].at[j - 1], lands[0].at[j - 1], (_flip(x, j & 2), _flip(y, j & 1), c)) for j in range(1, 4)]


def _peer(k):
    x, y, c = _coords()
    return _flip(x, k & 4), _flip(y, k & 2), _flip(c, k & 1)


def _send_plan(ks):
    def plan(srcs, lands):
        return [(srcs[0].at[0], lands[0].at[p], _peer(k)) for p, k in enumerate(ks)]
    return plan


def _forward_plan(moves):
    def plan(srcs, lands):
        return [(srcs[0].at[s], srcs[0].at[d], _peer(1)) for s, d in moves]
    return plan


def _gather_plan(srcs, lands):
    x, y, c = _coords()
    me = 4 * x + 2 * y + c
    return [(srcs[a], lands[a].at[me], (_flip(x, k & 4), _flip(y, k & 2), _flip(c, k & 1)))
            for a in range(len(srcs)) for k in range(1, N_DEV)]


def _pair_sum(grad, recv, owners, name):
    _, rows, cols = grad.shape
    tr = rows

    def body(own_ref, g_ref, r_ref, mine_ref, send_ref):
        j = pl.program_id(1)
        total = g_ref[...] + r_ref[...]

        @pl.when(j == 0)
        def _():
            mine_ref[...] = total

        @pl.when(j > 0)
        def _():
            send_ref[...] = total.astype(BF16)

    grid_spec = pltpu.PrefetchScalarGridSpec(
        num_scalar_prefetch=1, grid=(rows // tr, 4),
        in_specs=[pl.BlockSpec((1, tr, cols), lambda i, j, own: (own[j], i, 0)),
                  pl.BlockSpec((1, tr, cols), lambda i, j, own: (j, i, 0))],
        out_specs=[pl.BlockSpec((1, tr, cols), lambda i, j, own: (0, i, 0)),
                   pl.BlockSpec((1, tr, cols), lambda i, j, own: (jnp.maximum(j - 1, 0), i, 0))])
    return pl.pallas_call(
        body, name=name, grid_spec=grid_spec,
        out_shape=[jax.ShapeDtypeStruct((1, rows, cols), F32), jax.ShapeDtypeStruct((3, rows, cols), BF16)],
        compiler_params=_cparams(dimension_semantics=("arbitrary", "arbitrary")),
    )(owners, grad, recv)


def _modulation(c16, ada_w_sh, ada_b_sh, decay_logit):
    def body(c_ref, w_ref, b_ref, dl_ref, mod_ref, act_ref, lg_ref):
        cv = c_ref[...]
        act = cv * _sigmoid(cv)
        act_ref[...] = act
        mod_ref[...] = jnp.dot(act, w_ref[...], preferred_element_type=F32,
                               precision=lax.Precision.HIGHEST) + b_ref[...]
        z = dl_ref[...]
        lg_ref[...] = jnp.minimum(z, 0.0) - jnp.log(1.0 + jnp.exp(-jnp.abs(z)))

    return pl.pallas_call(
        body, name="modulation",
        out_shape=[jax.ShapeDtypeStruct((16, ADA_SH), F32), jax.ShapeDtypeStruct((16, D), F32),
                   jax.ShapeDtypeStruct(decay_logit.shape, F32)],
        compiler_params=_cparams(),
    )(c16, ada_w_sh, ada_b_sh, decay_logit)


def _adam_update(w, g, m, v):
    m2 = ADAM_B1 * m + (1.0 - ADAM_B1) * g
    v2 = ADAM_B2 * v + (1.0 - ADAM_B2) * (g * g)
    m_hat = m2 / (1.0 - ADAM_B1 ** ADAM_STEP)
    v_hat = v2 / (1.0 - ADAM_B2 ** ADAM_STEP)
    delta = -ADAM_LR * (m_hat / (jnp.sqrt(v_hat) + ADAM_EPS) + ADAM_WD * w)
    return delta, m2, v2


def _adam_sharded(w, m, v, own, recv, name):
    rows, cols = w.shape
    tr = _row_tile(rows)

    def body(w_ref, m_ref, v_ref, g0, g1, g2, g3, g_ref, d_ref, m_out, v_out):
        g = ((g0[0] + g1[0].astype(F32)) + g2[0].astype(F32)) + g3[0].astype(F32)
        delta, m2, v2 = _adam_update(w_ref[...], g, m_ref[...], v_ref[...])
        g_ref[...] = g
        d_ref[...] = delta
        m_out[...] = m2
        v_out[...] = v2

    flat = pl.BlockSpec((tr, cols), lambda i: (i, 0))
    part = lambda j: pl.BlockSpec((1, tr, cols), lambda i, j=j: (j, i, 0))
    return pl.pallas_call(
        body, name=name, grid=(rows // tr,),
        in_specs=[flat, flat, flat, part(0), part(0), part(1), part(2)],
        out_specs=[flat] * 4,
        out_shape=[jax.ShapeDtypeStruct((rows, cols), F32)] * 4,
        compiler_params=_cparams(dimension_semantics=("arbitrary",)),
    )(w, m, v, own, recv, recv, recv)


def _ada_backward(act_t, dmod16, dmodc8, ada_w_sh, m, v):
    def body(at_ref, dm_ref, dc_ref, w_ref, m_ref, v_ref, g_ref, d_ref, m_out, v_out, pc_ref):
        g = jnp.dot(at_ref[...], dm_ref[...], preferred_element_type=F32, precision=lax.Precision.HIGHEST)
        w = w_ref[...]
        delta, m2, v2 = _adam_update(w, g, m_ref[...], v_ref[...])
        g_ref[...] = g
        d_ref[...] = delta
        m_out[...] = m2
        v_out[...] = v2
        pc_ref[...] = lax.dot_general(dc_ref[...], w, (((1,), (1,)), ((), ())), preferred_element_type=F32,
                                      precision=lax.Precision.HIGHEST)

    return pl.pallas_call(
        body, name="ada_backward",
        out_shape=[jax.ShapeDtypeStruct((D, ADA_SH), F32)] * 4 + [jax.ShapeDtypeStruct((8, D), F32)],
        compiler_params=_cparams(),
    )(act_t, dmod16, dmodc8, ada_w_sh, m, v)


def _sum_devices(gathered, name):
    _, rows, cols = gathered.shape

    def body(g_ref, o_ref):
        acc = g_ref[0]
        for d in range(1, N_DEV):
            acc = acc + g_ref[d]
        o_ref[...] = acc

    return pl.pallas_call(body, name=name, out_shape=jax.ShapeDtypeStruct((rows, cols), F32),
                          compiler_params=_cparams())(gathered)


SMALL_PARAMS = ("c_ctx", "norm_w", "conv_b", "gn_w", "final_norm_w", "ada_b", "decay_logit", "conv_w")


def _adam_small(tot, dact_ctx, conv_w_grad, decay_grad, params):
    n = len(SMALL_PARAMS)

    def body(tot_ref, dact_ref, cwg_ref, dlg_ref, *refs):
        ins, outs = refs[:3 * n], refs[3 * n:]
        raw = {
            "c_ctx": dact_ref[0:1, :],
            "norm_w": tot_ref[1:2, :], "conv_b": tot_ref[2:3, :], "gn_w": tot_ref[3:4, :],
            "final_norm_w": tot_ref[0:1, :],
            "ada_b": jnp.concatenate([tot_ref[7 + r:8 + r, :] + tot_ref[10 + r:11 + r, :] for r in range(3)], axis=1),
            "decay_logit": dlg_ref[...],
            "conv_w": cwg_ref[...],
        }
        for k, name in enumerate(SMALL_PARAMS):
            lead = (0,) if len(ins[3 * k].shape) == 3 else ()
            w, m, v = (ref[lead + (...,)] for ref in ins[3 * k:3 * k + 3])
            g = raw[name]
            if name == "c_ctx":
                s = _sigmoid(w)
                g = g * (s * (1.0 + w * (1.0 - s)))
            elif name == "decay_logit":
                g = g * (1.0 - _sigmoid(w))
            delta, m2, v2 = _adam_update(w, g, m, v)
            for ref, val in zip(outs[4 * k:4 * k + 4], (g, delta, m2, v2)):
                ref[lead + (...,)] = val

    flat = [a for trio in params for a in trio]
    out_shape = [jax.ShapeDtypeStruct(trio[0].shape, F32) for trio in params for _ in range(4)]
    outs = pl.pallas_call(body, name="adam_small", out_shape=out_shape, compiler_params=_cparams())(
        tot, dact_ctx, conv_w_grad, decay_grad, *flat)
    return [outs[4 * k:4 * k + 4] for k in range(n)]


def _adam_square(params, own, recv):
    def body(own_ref, recv_ref, *refs):
        ins, outs = refs[:9], refs[9:]
        for k in range(3):
            rows = slice(k * RSH, (k + 1) * RSH)
            g = own_ref[0, rows, :]
            for j in range(3):
                g = g + recv_ref[j, rows, :].astype(F32)
            delta, m2, v2 = _adam_update(ins[3 * k][...], g, ins[3 * k + 1][...], ins[3 * k + 2][...])
            for ref, val in zip(outs[4 * k:4 * k + 4], (g, delta, m2, v2)):
                ref[...] = val

    flat = [a for trio in params for a in trio]
    outs = pl.pallas_call(body, name="adam_square", out_shape=[jax.ShapeDtypeStruct((RSH, D), F32)] * 12,
                          compiler_params=_cparams())(own, recv, *flat)
    return [outs[4 * k:4 * k + 4] for k in range(3)]


def _rope(t, cos, sin):
    lane = lax.broadcasted_iota(jnp.int32, (1, LANES), 1)
    first_half = jnp.bitwise_and(lane, DK // 2) == 0
    partner = jnp.where(first_half, pltpu.roll(t, LANES - DK // 2, 1), pltpu.roll(t, DK // 2, 1))
    return t * cos + partner * sin


class _Shards:
    def __init__(self, arrays, picks):
        self.arrays, self.picks = list(arrays), list(picks)

    def __add__(self, other):
        shift = len(self.arrays)
        return _Shards(self.arrays + other.arrays, self.picks + [(a + shift, s) for a, s in other.picks])


def _load_w_in(step, ids_ref, shards, w_refs, w_vmem, sems):
    @pl.when(step == 0)
    def _():
        copies = []
        for p, (a, slot) in enumerate(shards.picks):
            col = pl.multiple_of(ids_ref[p] * WSH, LANES)
            copies.append(pltpu.make_async_copy(w_refs[a].at[slot], w_vmem.at[:, pl.ds(col, WSH)], sems.at[p]))
        for cp in copies:
            cp.start()
        for cp in copies:
            cp.wait()


def _project_columns(xmb, w_ref, shard, cos, sin, p_ref):
    width = 2 * WSH
    q_lo, k_lo, k_hi = CB_QK * D, CB_QK * D + N_HEADS * DK, CB_V * D
    q_pair, k_pair = q_lo // width, k_lo // width
    assert (q_pair + 1) * width == k_lo and k_pair * width == k_lo and k_hi <= (k_pair + 1) * width
    mxu_cols = 2 * LANES

    def columns(rot_lo, rot_hi, scale):
        for a in range(0, width, mxu_cols):
            acc = _dot(xmb, w_ref[:, a:a + mxu_cols])
            for col in range(a, a + mxu_cols, LANES):
                piece = acc[:, col - a:col - a + LANES]
                if rot_lo <= col < rot_hi:
                    piece = _rope(piece if scale == 1.0 else piece * scale, cos, sin)
                p_ref[:, col:col + LANES] = piece.astype(BF16)

    @pl.when(shard == q_pair)
    def _():
        columns(q_lo - q_pair * width, width, 1.0)

    @pl.when(shard == k_pair)
    def _():
        columns(0, k_hi - k_lo, DK ** -0.5)

    @pl.when(jnp.logical_and(shard != q_pair, shard != k_pair))
    def _():
        columns(0, 0, 1.0)


_AFTER_SPEC = pl.BlockSpec(memory_space=pl.ANY)


def _prenorm(x, ctx, modx, modc, norm_w, after):
    tp = 4 * TM
    seq, ctx_len = x.shape[0], ctx.shape[0]
    n_x_tiles = seq // tp
    lext = seq + ctx_len
    assert seq % tp == 0 and ctx_len <= tp and tp % GRID_W == 0 and tp // GRID_W <= 16
    grid_rows = tp // GRID_W

    def body(x_ref, ctx_ref, mx_ref, mc_ref, nw_ref, after_ref, xm_ref, xmt_ref, cos_ref, sin_ref, col_cos, col_sin):
        del after_ref
        i = pl.program_id(0)
        is_ctx = i >= n_x_tiles
        ctx_rows = jnp.concatenate([ctx_ref[...], jnp.zeros((tp - ctx_len, D), F32)], axis=0)
        x = jnp.where(is_ctx, ctx_rows, x_ref[...])
        r = lax.rsqrt(jnp.mean(x * x, axis=-1, keepdims=True) + EPS)
        sh = jnp.where(is_ctx, mc_ref[0:1, :], mx_ref[0:1, :])
        sc = jnp.where(is_ctx, mc_ref[1:2, :], mx_ref[1:2, :])
        xm = (x * r * nw_ref[...]) * (1.0 + sc) + sh
        xm_ref[...] = xm.astype(BF16)
        xmt_ref[...] = xm.T.astype(BF16)
        lane = lax.broadcasted_iota(jnp.int32, (1, LANES), 1)
        n_freq = DK // 4
        inv = jnp.exp((lane % n_freq).astype(F32) * (-math.log(ROPE_BASE) / n_freq))
        by_column = (lane % (2 * n_freq)) >= n_freq
        sign = jnp.where((lane % DK) < DK // 2, -1.0, 1.0)

        @pl.when(i == 0)
        def _():
            col = (lax.broadcasted_iota(jnp.int32, (tp, 1), 0) % GRID_W).astype(F32) * inv
            col_cos[...] = jnp.cos(col)
            col_sin[...] = jnp.sin(col) * sign

        row = (i * grid_rows + lax.broadcasted_iota(jnp.int32, (16, 1), 0)).astype(F32) * inv
        row_cos, row_sin = jnp.cos(row), jnp.sin(row) * sign
        tall = lambda t: jnp.concatenate(
            [jnp.broadcast_to(t[g:g + 1, :], (GRID_W, LANES)) for g in range(grid_rows)], axis=0)
        cos_ref[...] = jnp.where(is_ctx, 1.0, jnp.where(by_column, col_cos[...], tall(row_cos)))
        sin_ref[...] = jnp.where(is_ctx, 0.0, jnp.where(by_column, col_sin[...], tall(row_sin)))

    full = lambda a: pl.BlockSpec(a.shape, lambda i: (0,) * a.ndim)
    row = lambda width: pl.BlockSpec((tp, width), lambda i: (i, 0))
    return pl.pallas_call(
        body, name="prenorm", grid=(n_x_tiles + 1,),
        in_specs=[pl.BlockSpec((tp, D), lambda i: (jnp.minimum(i, n_x_tiles - 1), 0)), full(ctx),
                  full(modx), full(modc), full(norm_w), _AFTER_SPEC],
        out_specs=[row(D), pl.BlockSpec((D, tp), lambda i: (0, i)), row(LANES), row(LANES)],
        out_shape=[jax.ShapeDtypeStruct((lext, D), BF16), jax.ShapeDtypeStruct((D, lext), BF16),
                   jax.ShapeDtypeStruct((lext, LANES), F32), jax.ShapeDtypeStruct((lext, LANES), F32)],
        scratch_shapes=[pltpu.VMEM((tp, LANES), F32), pltpu.VMEM((tp, LANES), F32)],
        compiler_params=_cparams(dimension_semantics=("arbitrary",)),
    )(x, ctx, modx, modc, norm_w, after)


def _in_projection(p_ext, xm, cos_t, sin_t, w, pairs, me, name):
    lext = xm.shape[0]
    tr = lext // 4
    meta = []
    for (slot_a, k_a), (slot_b, k_b) in pairs:
        assert k_a ^ k_b == 1
        dev_a = jnp.bitwise_xor(me, k_a)
        a_low = dev_a % 2 == 0
        meta += [jnp.where(a_low, slot_a, slot_b), jnp.where(a_low, slot_b, slot_a), dev_a // 2]
    meta = jnp.stack(meta).astype(jnp.int32)

    def body(meta_ref, xm_ref, cos_ref, sin_ref, w_hbm, *rest):
        p_ref, w_ref, sems = rest[-3:]
        j, i = pl.program_id(0), pl.program_id(1)

        @pl.when(i == 0)
        def _():
            copies = [pltpu.make_async_copy(w_hbm.at[meta_ref[3 * j + h]], w_ref.at[:, h * WSH:(h + 1) * WSH],
                                            sems.at[h]) for h in range(2)]
            for cp in copies:
                cp.start()
            for cp in copies:
                cp.wait()

        _project_columns(xm_ref[...], w_ref, meta_ref[3 * j + 2], cos_ref[...], sin_ref[...], p_ref)

    row = lambda width: pl.BlockSpec((tr, width), lambda j, i, meta_: (i, 0))
    grid_spec = pltpu.PrefetchScalarGridSpec(
        num_scalar_prefetch=1, grid=(len(pairs), lext // tr),
        in_specs=[row(D), row(LANES), row(LANES), pl.BlockSpec(memory_space=pl.ANY)]
        + ([] if p_ext is None else [pl.BlockSpec(memory_space=pl.ANY)]),
        out_specs=pl.BlockSpec((tr, 2 * WSH), lambda j, i, meta_: (i, meta_[3 * j + 2])),
        scratch_shapes=[pltpu.VMEM((D, 2 * WSH), BF16), pltpu.SemaphoreType.DMA((2,))])
    return pl.pallas_call(
        body, name=name, grid_spec=grid_spec,
        out_shape=jax.ShapeDtypeStruct((lext, PW), BF16),
        input_output_aliases={} if p_ext is None else {5: 0},
        compiler_params=_cparams(dimension_semantics=("arbitrary", "arbitrary")),
    )(meta, xm, cos_t, sin_t, w, *([] if p_ext is None else [p_ext]))


def _decay_tables(lgf, lgb, n):
    i = lax.broadcasted_iota(jnp.int32, (n, 1), 0).astype(F32)
    return dict(i=i, k_f=jnp.exp(lgf * (n - 1.0 - i)), k_b=jnp.exp(lgb * i),
                q_f=jnp.exp(lgf * (i + 1.0)), q_b=jnp.exp(lgb * (n - i)))


def _decay_matrix(lgf, lgb, n):
    ii = lax.broadcasted_iota(jnp.int32, (n, n), 0)
    jj = lax.broadcasted_iota(jnp.int32, (n, n), 1)
    diff = (ii - jj).astype(F32)
    low = jnp.exp(lgf * jnp.maximum(diff, 0.0))
    up = jnp.exp(lgb * jnp.maximum(-diff, 0.0))
    return jnp.where(diff > 0, low, jnp.where(diff < 0, up, 2.0)), diff


def _cat_lanes(a, b):
    return jnp.concatenate([a.astype(BF16), b.astype(BF16)], axis=1)


def _retention_forward(p_ext, lg, seq, ctx_len):
    lext = seq + ctx_len
    n_chunks = seq // RET_C
    C = RET_C

    def body(lg_ref, q_ref, k_ref, v_ref, rn_ref, rstd_ref, kv_scr, sf_scr, sb_scr):
        pair = pl.program_id(0)
        lane = lax.broadcasted_iota(jnp.int32, (1, LANES), 1)
        heads = range(2)
        hmask = [(lane // DK == hh).astype(F32) for hh in heads]
        lgf = [lg_ref[0, 2 * pair + hh] for hh in heads]
        lgb = [lg_ref[1, 2 * pair + hh] for hh in heads]
        vs = [slice(hh * DV, (hh + 1) * DV) for hh in heads]
        t = [_decay_tables(lgf[hh], lgb[hh], C) for hh in heads]
        kc_all = k_ref[seq:lext, :].astype(F32)
        s0 = []
        for hh in heads:
            tc = _decay_tables(lgf[hh], lgb[hh], ctx_len)
            kc = kc_all * hmask[hh]
            s0.append(_dot_tn(_cat_lanes(kc * tc["k_f"], kc * tc["k_b"]), v_ref[seq:lext, vs[hh]]))

        def increments(c, carry):
            rows = pl.ds(pl.multiple_of(c * C, C), C)
            k_all = k_ref[rows, :].astype(F32)
            for hh in heads:
                k = k_all * hmask[hh]
                kv_scr[hh, c] = _dot_tn(_cat_lanes(k * t[hh]["k_f"], k * t[hh]["k_b"]), v_ref[rows, vs[hh]])
            return carry

        lax.fori_loop(0, n_chunks, increments, 0, unroll=2)
        gf_c = [jnp.exp(lgf[hh] * C) for hh in heads]
        gb_c = [jnp.exp(lgb[hh] * C) for hh in heads]

        def scan_f(c, s):
            for hh in heads:
                sf_scr[hh, c] = s[hh]
            return tuple(gf_c[hh] * s[hh] + kv_scr[hh, c, 0:LANES, :] for hh in heads)

        def scan_b(n, s):
            c = n_chunks - 1 - n
            for hh in heads:
                sb_scr[hh, c] = s[hh]
            return tuple(gb_c[hh] * s[hh] + kv_scr[hh, c, LANES:2 * LANES, :] for hh in heads)

        lax.fori_loop(0, n_chunks, scan_f, tuple(s0[hh][0:LANES] for hh in heads))
        lax.fori_loop(0, n_chunks, scan_b, tuple(s0[hh][LANES:2 * LANES] for hh in heads))
        dmat = [_decay_matrix(lgf[hh], lgb[hh], C)[0] for hh in heads]

        def outputs(c, carry):
            rows = pl.ds(pl.multiple_of(c * C, C), C)
            q_all = q_ref[rows, :].astype(F32)
            k = k_ref[rows, :]
            for hh in heads:
                q = q_all * hmask[hh]
                v = v_ref[rows, vs[hh]]
                s = _dot_nt(q.astype(BF16), k)
                inner = _dot((s * dmat[hh]).astype(BF16), v)
                states = jnp.concatenate([sf_scr[hh, c], sb_scr[hh, c]], axis=0).astype(BF16)
                ret = inner + _dot(_cat_lanes(q * t[hh]["q_f"], q * t[hh]["q_b"]), states)
                mu = jnp.mean(ret, axis=-1, keepdims=True)
                cen = ret - mu
                rstd = lax.rsqrt(jnp.mean(cen * cen, axis=-1, keepdims=True) + EPS)
                rn_ref[rows, vs[hh]] = cen * rstd
                rstd_ref[rows, vs[hh]] = jnp.broadcast_to(rstd, (C, DV))
            return carry

        lax.fori_loop(0, n_chunks, outputs, 0, unroll=2)

    qk0 = CB_QK * D // LANES
    return pl.pallas_call(
        body, name="retention_forward", grid=(N_HEADS // 2,),
        in_specs=[pl.BlockSpec(memory_space=pltpu.SMEM),
                  pl.BlockSpec((lext, LANES), lambda g: (0, qk0 + g)),
                  pl.BlockSpec((lext, LANES), lambda g: (0, qk0 + N_HEADS // 2 + g)),
                  pl.BlockSpec((lext, 2 * DV), lambda g: (0, CB_V * D // (2 * DV) + g))],
        out_specs=[pl.BlockSpec((seq, 2 * DV), lambda g: (0, g))] * 2,
        out_shape=[jax.ShapeDtypeStruct((seq, D), F32)] * 2,
        scratch_shapes=[pltpu.VMEM((2, n_chunks, 2 * LANES, DV), F32), pltpu.VMEM((2, n_chunks, LANES, DV), F32),
                        pltpu.VMEM((2, n_chunks, LANES, DV), F32)],
        compiler_params=_cparams(dimension_semantics=("arbitrary",)),
    )(lg, p_ext, p_ext, p_ext)


def _retention_backward(p_ext, dret, lg, cos_t, sin_t, seq, ctx_len, after):
    lext = seq + ctx_len
    n_chunks = seq // RET_C
    C = RET_C

    def body(lg_ref, q_ref, k_ref, v_ref, do_ref, cos_ref, sin_ref, after_ref, dq_ref, dk_ref, dv_ref, dlg_ref,
             kv_scr, g_scr, sf_scr, sb_scr, gfn_scr, gbp_scr):
        pair = pl.program_id(0)
        lane = lax.broadcasted_iota(jnp.int32, (1, LANES), 1)
        heads = range(2)
        hmask = [(lane // DK == hh).astype(F32) for hh in heads]
        lgf = [lg_ref[0, 2 * pair + hh] for hh in heads]
        lgb = [lg_ref[1, 2 * pair + hh] for hh in heads]
        vs = [slice(hh * DV, (hh + 1) * DV) for hh in heads]
        t = [_decay_tables(lgf[hh], lgb[hh], C) for hh in heads]
        tc = [_decay_tables(lgf[hh], lgb[hh], ctx_len) for hh in heads]
        kc_all = k_ref[seq:lext, :].astype(F32)
        kc = [kc_all * hmask[hh] for hh in heads]
        vc = [v_ref[seq:lext, vs[hh]] for hh in heads]
        kc_cat = [_cat_lanes(kc[hh] * tc[hh]["k_f"], kc[hh] * tc[hh]["k_b"]) for hh in heads]
        s0 = [_dot_tn(kc_cat[hh], vc[hh]) for hh in heads]

        def increments(c, carry):
            rows = pl.ds(pl.multiple_of(c * C, C), C)
            k_all = k_ref[rows, :].astype(F32)
            q_all = q_ref[rows, :].astype(F32)
            for hh in heads:
                k, q = k_all * hmask[hh], q_all * hmask[hh]
                kv_scr[hh, c] = _dot_tn(_cat_lanes(k * t[hh]["k_f"], k * t[hh]["k_b"]), v_ref[rows, vs[hh]])
                g_scr[hh, c] = _dot_tn(_cat_lanes(q * t[hh]["q_f"], q * t[hh]["q_b"]), do_ref[rows, vs[hh]])
            return carry

        lax.fori_loop(0, n_chunks, increments, 0, unroll=2)
        gf_c = [jnp.exp(lgf[hh] * C) for hh in heads]
        gb_c = [jnp.exp(lgb[hh] * C) for hh in heads]

        def scan_f(c, s):
            for hh in heads:
                sf_scr[hh, c] = s[hh]
            return tuple(gf_c[hh] * s[hh] + kv_scr[hh, c, 0:LANES, :] for hh in heads)

        def scan_b(n, s):
            c = n_chunks - 1 - n
            for hh in heads:
                sb_scr[hh, c] = s[hh]
            return tuple(gb_c[hh] * s[hh] + kv_scr[hh, c, LANES:2 * LANES, :] for hh in heads)

        def scan_gf(n, carry):
            c = n_chunks - 1 - n
            for hh in heads:
                gfn_scr[hh, c] = carry[hh]
            return tuple(g_scr[hh, c, 0:LANES, :] + gf_c[hh] * carry[hh] for hh in heads)

        def scan_gb(c, carry):
            for hh in heads:
                gbp_scr[hh, c] = carry[hh]
            return tuple(g_scr[hh, c, LANES:2 * LANES, :] + gb_c[hh] * carry[hh] for hh in heads)

        lax.fori_loop(0, n_chunks, scan_f, tuple(s0[hh][0:LANES] for hh in heads))
        lax.fori_loop(0, n_chunks, scan_b, tuple(s0[hh][LANES:2 * LANES] for hh in heads))
        zero_state = jnp.zeros((LANES, DV), F32)
        gf_first = lax.fori_loop(0, n_chunks, scan_gf, (zero_state, zero_state))
        gb_last = lax.fori_loop(0, n_chunks, scan_gb, (zero_state, zero_state))

        dmat, w_f, w_b = [], [], []
        for hh in heads:
            dm, diff = _decay_matrix(lgf[hh], lgb[hh], C)
            dmat.append(dm)
            w_f.append(jnp.where(diff > 0, diff * dm, 0.0))
            w_b.append(jnp.where(diff < 0, -diff * dm, 0.0))

        def total(a):
            rows_, width = a.shape
            part = jnp.sum(a.reshape(rows_ // 8, 8, width), axis=0)
            return part[:, 0:LANES] + part[:, LANES:2 * LANES] if width == 2 * LANES else part

        def chunk_head(hh, c, rows, q_all, k_all, dlgf, dlgb):
            th = t[hh]
            qm = q_all * hmask[hh]
            km = k_all * hmask[hh]
            qb, kb = qm.astype(BF16), km.astype(BF16)
            v = v_ref[rows, vs[hh]]
            do = do_ref[rows, vs[hh]]
            s = _dot_nt(qb, kb)
            dsd = _dot_nt(do, v)
            ds = (dsd * dmat[hh]).astype(BF16)
            dq_in = _dot(ds, kb)
            dk_in = _dot_tn(ds, qb)
            dv_in = _dot_tn((s * dmat[hh]).astype(BF16), do)
            prod = s * dsd
            dlgf = dlgf + total(prod * w_f[hh])
            dlgb = dlgb + total(prod * w_b[hh])
            sf, sb = sf_scr[hh, c], sb_scr[hh, c]
            states = jnp.concatenate([sf, sb], axis=0).astype(BF16)
            dqc = _dot_nt(do, states)
            dqf = dqc[:, 0:LANES] * th["q_f"]
            dqb = dqc[:, LANES:2 * LANES] * th["q_b"]
            dq = (dq_in + dqf + dqb) * hmask[hh]
            dlgf = dlgf + total((th["i"] + 1.0) * (qm * dqf))
            dlgb = dlgb + total((C - th["i"]) * (qm * dqb))
            gfn, gbp = gfn_scr[hh, c], gbp_scr[hh, c]
            gstates = jnp.concatenate([gfn, gbp], axis=0).astype(BF16)
            dkc = _dot_nt(v, gstates)
            dkf = dkc[:, 0:LANES] * th["k_f"]
            dkb = dkc[:, LANES:2 * LANES] * th["k_b"]
            dk = dk_in + dkf + dkb
            dlgf = dlgf + total((C - 1.0 - th["i"]) * (km * dkf)) + C * gf_c[hh] * total(gfn * sf)
            dlgb = dlgb + total(th["i"] * (km * dkb)) + C * gb_c[hh] * total(gbp * sb)
            dv = dv_in + _dot(_cat_lanes(km * th["k_f"], km * th["k_b"]), gstates)
            dv_ref[rows, vs[hh]] = dv.astype(BF16)
            return dq, dk, dlgf, dlgb

        def chunk(c, carry):
            rows = pl.ds(pl.multiple_of(c * C, C), C)
            q_all = q_ref[rows, :].astype(F32)
            k_all = k_ref[rows, :].astype(F32)
            dq0, dk0, f0, b0 = chunk_head(0, c, rows, q_all, k_all, carry[0], carry[1])
            dq1, dk1, f1, b1 = chunk_head(1, c, rows, q_all, k_all, carry[2], carry[3])
            cos, sin = cos_ref[rows, :], sin_ref[rows, :]
            dq_ref[rows, :] = _rope(dq0 + dq1, cos, -sin).astype(BF16)
            dk_ref[rows, :] = (_rope(dk0 + dk1, cos, -sin) * (DK ** -0.5)).astype(BF16)
            return f0, b0, f1, b1

        zero = jnp.zeros((8, LANES), F32)
        sums = lax.fori_loop(0, n_chunks, chunk, (zero, zero, zero, zero), unroll=2)

        dlg = []
        dk_ctx = jnp.zeros((ctx_len, LANES), F32)
        for hh in heads:
            g0 = jnp.concatenate([gf_first[hh], gb_last[hh]], axis=0).astype(BF16)
            dkcc = _dot_nt(vc[hh], g0)
            dkcf = dkcc[:, 0:LANES] * tc[hh]["k_f"]
            dkcb = dkcc[:, LANES:2 * LANES] * tc[hh]["k_b"]
            dlgf = sums[2 * hh] + total((ctx_len - 1.0 - tc[hh]["i"]) * (kc[hh] * dkcf))
            dlgb = sums[2 * hh + 1] + total(tc[hh]["i"] * (kc[hh] * dkcb))
            dk_ctx = dk_ctx + (dkcf + dkcb) * (DK ** -0.5)
            dv_ref[seq:lext, vs[hh]] = _dot(kc_cat[hh], g0).astype(BF16)
            dlg += [jnp.sum(jnp.sum(a, axis=1, keepdims=True), axis=0, keepdims=True) for a in (dlgf, dlgb)]
        dk_ref[seq:lext, :] = dk_ctx.astype(BF16)
        dq_ref[seq:lext, :] = jnp.zeros((ctx_len, LANES), BF16)

        lane8 = lax.broadcasted_iota(jnp.int32, (8, LANES), 1)
        out = jnp.zeros((8, LANES), F32)
        for n, val in enumerate(dlg):
            out = jnp.where(lane8 == n, val, out)
        dlg_ref[0] = out

    qk0 = CB_QK * D // LANES
    q_spec = pl.BlockSpec((lext, LANES), lambda g: (0, qk0 + g))
    k_spec = pl.BlockSpec((lext, LANES), lambda g: (0, qk0 + N_HEADS // 2 + g))
    v_spec = pl.BlockSpec((lext, 2 * DV), lambda g: (0, CB_V * D // (2 * DV) + g))
    table = pl.BlockSpec((lext, LANES), lambda g: (0, 0))
    state = pltpu.VMEM((2, n_chunks, LANES, DV), F32)
    return pl.pallas_call(
        body, name="retention_backward", grid=(N_HEADS // 2,),
        in_specs=[pl.BlockSpec(memory_space=pltpu.SMEM), q_spec, k_spec, v_spec,
                  pl.BlockSpec((seq, 2 * DV), lambda g: (0, g)), table, table, _AFTER_SPEC],
        out_specs=[pl.BlockSpec((lext, LANES), lambda g: (0, g)), pl.BlockSpec((lext, LANES), lambda g: (0, g)),
                   pl.BlockSpec((lext, 2 * DV), lambda g: (0, g)), pl.BlockSpec((1, 8, LANES), lambda g: (g, 0, 0))],
        out_shape=[jax.ShapeDtypeStruct((lext, N_HEADS * DK), BF16), jax.ShapeDtypeStruct((lext, N_HEADS * DK), BF16),
                   jax.ShapeDtypeStruct((lext, D), BF16), jax.ShapeDtypeStruct((N_HEADS // 2, 8, LANES), F32)],
        scratch_shapes=[pltpu.VMEM((2, n_chunks, 2 * LANES, DV), F32), pltpu.VMEM((2, n_chunks, 2 * LANES, DV), F32),
                        state, state, state, state],
        compiler_params=_cparams(dimension_semantics=("arbitrary",)),
    )(lg, p_ext, p_ext, p_ext, dret, cos_t, sin_t, after)


class _ColumnWriter:
    def __init__(self, dst_hbm, stage, sems, pieces, n_steps):
        self.dst, self.stage, self.sems, self.pieces, self.n_steps = dst_hbm, stage, sems, pieces, n_steps

    def _copies(self, slot, tile):
        rows = pl.ds(pl.multiple_of(tile * TM, TM), TM)
        return [pltpu.make_async_copy(self.stage.at[slot, :, pl.ds(src, width)], self.dst.at[rows, pl.ds(dst, width)],
                                      self.sems.at[slot, k]) for k, (src, dst, width) in enumerate(self.pieces)]

    def slot_for(self, step):
        slot = step % 2

        @pl.when(step >= 2)
        def _():
            for cp in self._copies(slot, step - 2):
                cp.wait()

        return slot

    def send(self, step, slot):
        for cp in self._copies(slot, step):
            cp.start()

        @pl.when(step == self.n_steps - 1)
        def _():
            if self.n_steps >= 2:
                for cp in self._copies(1 - slot, step - 1):
                    cp.wait()
            for cp in self._copies(slot, step):
                cp.wait()


def _merge(p_ext, rn, rstd, x, target, w_a, w_b, w_out, vecs, seq):
    n_tiles = seq // TM
    hb = TM // HALO
    dp_pieces = [(0, CB_BG * D, D), (D, CB_ZA * D, D), (2 * D, CB_ZB * D, 3 * D)]

    def body(h_ref, bg_ref, cg_ref, za_ref, zb_ref, ga_ref, gb_ref, hp_ref, hn_ref, cp_ref, cn_ref,
             rn_ref, rstd_ref, x_ref, t_ref, wa_ref, wb_ref, wo_ref, vec_ref,
             dx1_ref, dp_hbm, dconv_ref, dret_ref, at_ref, b_ref, part_ref, stage, dp_sems):
        i = pl.program_id(0)
        writer = _ColumnWriter(dp_hbm, stage, dp_sems, dp_pieces, n_tiles)
        slot = writer.slot_for(i)
        dpa_ref = stage.at[slot]
        f = lambda ref: ref[...].astype(F32)
        h, bg, cg, za, zb, ga, gb = f(h_ref), f(bg_ref), f(cg_ref), f(za_ref), f(zb_ref), f(ga_ref), f(gb_ref)
        gx, w0, w1, w2 = vec_ref[0:1, :], vec_ref[1:2, :], vec_ref[2:3, :], vec_ref[3:4, :]
        cb, gnw, fw = vec_ref[4:5, :], vec_ref[5:6, :], vec_ref[6:7, :]
        u = cg * h
        row = lax.broadcasted_iota(jnp.int32, (TM, 1), 0)
        u_prev = (f(cp_ref) * f(hp_ref))[HALO - 1:HALO, :]
        u_next = (f(cn_ref) * f(hn_ref))[0:1, :]
        u_prev = jnp.where(i == 0, 0.0, u_prev)
        u_next = jnp.where(i == n_tiles - 1, 0.0, u_next)
        u_up = jnp.where(row == 0, u_prev, pltpu.roll(u, 1, 0))
        u_dn = jnp.where(row == TM - 1, u_next, pltpu.roll(u, TM - 1, 0))
        conv = w0 * u_up + w1 * u + w2 * u_dn + cb
        sza = _sigmoid(za)
        silu_za = za * sza
        a_act = silu_za * bg * conv
        rn = rn_ref[...]
        szb = _sigmoid(zb)
        silu_zb = zb * szb
        rg = rn * gnw
        b_act = silu_zb * rg
        y_a = _dot(a_act.astype(BF16), wa_ref[...])
        y_b = _dot(b_act.astype(BF16), wb_ref[...])
        sga, sgb = _sigmoid(ga), _sigmoid(gb)
        mix = sga * y_a + sgb * y_b
        y = _dot(mix.astype(BF16), wo_ref[...])
        x1 = x_ref[...] + gx * y
        r1 = lax.rsqrt(jnp.mean(x1 * x1, axis=-1, keepdims=True) + EPS)
        xh1 = x1 * r1
        err = xh1 * fw - t_ref[...]
        loss = jnp.sum(jnp.sum(err * err, axis=1, keepdims=True), axis=0, keepdims=True) * (0.5 / D)
        dxh = err * (fw * (1.0 / D))
        dx1 = r1 * (dxh - xh1 * jnp.mean(dxh * xh1, axis=-1, keepdims=True))
        dx1_ref[...] = dx1
        dy = (dx1 * gx).astype(BF16)
        dmix = _dot_nt(dy, wo_ref[...])
        dya_f, dyb_f = dmix * sga, dmix * sgb
        dya, dyb = dya_f.astype(BF16), dyb_f.astype(BF16)
        da = _dot_nt(dya, wa_ref[...])
        db = _dot_nt(dyb, wb_ref[...])
        da_silu = da * silu_za
        dpa_ref[:, 0:D] = (da_silu * conv).astype(BF16)
        dpa_ref[:, D:2 * D] = (da * bg * conv * (sza + silu_za * (1.0 - sza))).astype(BF16)
        dpa_ref[:, 2 * D:3 * D] = (db * rg * (szb + silu_zb * (1.0 - szb))).astype(BF16)
        dpa_ref[:, 3 * D:4 * D] = (dya_f * y_a * (1.0 - sga)).astype(BF16)
        dpa_ref[:, 4 * D:5 * D] = (dyb_f * y_b * (1.0 - sgb)).astype(BF16)
        dconv_ref[...] = (da_silu * bg).astype(BF16)
        drn_n = db * silu_zb
        drn = drn_n * gnw
        rstd = rstd_ref[...]
        for hd in range(N_HEADS):
            sl = slice(hd * DV, (hd + 1) * DV)
            dh_, rh = drn[:, sl], rn[:, sl]
            m1 = jnp.mean(dh_, axis=-1, keepdims=True)
            m2 = jnp.mean(dh_ * rh, axis=-1, keepdims=True)
            dret_ref[:, sl] = (rstd[:, sl] * (dh_ - m1 - rh * m2)).astype(BF16)
        at_ref[0] = a_act.T.astype(BF16)
        at_ref[1] = b_act.T.astype(BF16)
        at_ref[2] = mix.T.astype(BF16)
        b_ref[0] = dya
        b_ref[1] = dyb
        b_ref[2] = dy

        @pl.when(i == 0)
        def _():
            part_ref[...] = jnp.zeros((8, D), F32)

        part_ref[0:1, :] += jnp.sum(err * xh1, axis=0, keepdims=True) * (1.0 / D)
        part_ref[1:2, :] += jnp.sum(dx1 * y, axis=0, keepdims=True)
        part_ref[2:3, :] += jnp.sum(drn_n * rn, axis=0, keepdims=True)
        part_ref[3:4, :] += jnp.broadcast_to(loss, (1, D))
        writer.send(i, slot)

    col = lambda cb_: pl.BlockSpec((TM, D), lambda i, cb_=cb_: (i, cb_))
    prev = lambda cb_: pl.BlockSpec((HALO, D), lambda i, cb_=cb_: (jnp.maximum(i * hb - 1, 0), cb_))
    nxt = lambda cb_: pl.BlockSpec((HALO, D), lambda i, cb_=cb_: (jnp.minimum((i + 1) * hb, n_tiles * hb - 1), cb_))
    tile = pl.BlockSpec((TM, D), lambda i: (i, 0))
    full = lambda a: pl.BlockSpec(a.shape, lambda i: (0,) * a.ndim, pipeline_mode=pl.Buffered(1))
    return pl.pallas_call(
        body, name="merge", grid=(n_tiles,),
        in_specs=[col(CB_H), col(CB_BG), col(CB_CG), col(CB_ZA), col(CB_ZB), col(CB_GA), col(CB_GB),
                  prev(CB_H), nxt(CB_H), prev(CB_CG), nxt(CB_CG),
                  tile, tile, tile, tile, full(w_a), full(w_b), full(w_out), full(vecs)],
        out_specs=[tile, pl.BlockSpec(memory_space=pl.ANY), tile, tile,
                   pl.BlockSpec((3, D, TM), lambda i: (0, 0, i)), pl.BlockSpec((3, TM, D), lambda i: (0, i, 0)),
                   pl.BlockSpec((8, D), lambda i: (0, 0))],
        out_shape=[jax.ShapeDtypeStruct((seq, D), F32), jax.ShapeDtypeStruct((p_ext.shape[0], PW), BF16),
                   jax.ShapeDtypeStruct((seq, D), BF16), jax.ShapeDtypeStruct((seq, D), BF16),
                   jax.ShapeDtypeStruct((3, D, seq), BF16), jax.ShapeDtypeStruct((3, seq, D), BF16),
                   jax.ShapeDtypeStruct((8, D), F32)],
        scratch_shapes=[pltpu.VMEM((2, TM, 5 * D), BF16), pltpu.SemaphoreType.DMA((2, len(dp_pieces)))],
        compiler_params=_cparams(dimension_semantics=("arbitrary",)),
    )(p_ext, p_ext, p_ext, p_ext, p_ext, p_ext, p_ext, p_ext, p_ext, p_ext, p_ext,
      rn, rstd, x, target, w_a, w_b, w_out, vecs)


def _conv_backward(p_ext, dconv, dp, dq, dk, dv, vecs, seq, after):
    n_tiles = seq // TM
    hb = TM // HALO
    qk = N_HEADS * DK
    pieces = [(0, CB_H * D, D), (D, CB_CG * D, D), (2 * D, CB_QK * D, 2 * D)]
    zero_pieces = [(CB_BG * D, D), (CB_ZA * D, D), (CB_ZB * D, 3 * D)]

    def body(h_ref, cg_ref, dc_ref, dcp_ref, dcn_ref, dq_ref, dk_ref, dv_ref, vec_ref, dp_in, after_ref, dp_hbm, part_ref,
             stage, sems, zeros, zero_sems):
        del dp_in, after_ref
        i = pl.program_id(0)
        writer = _ColumnWriter(dp_hbm, stage, sems, pieces, n_tiles + 1)
        slot = writer.slot_for(i)
        out = stage.at[slot]
        out[:, 2 * D:2 * D + qk] = dq_ref[...]
        out[:, 2 * D + qk:3 * D] = dk_ref[...]
        out[:, 3 * D:4 * D] = dv_ref[...]

        @pl.when(i == 0)
        def _():
            part_ref[...] = jnp.zeros((8, D), F32)

        @pl.when(i == n_tiles)
        def _():
            out[:, 0:2 * D] = jnp.zeros((TM, 2 * D), BF16)
            zeros[...] = jnp.zeros(zeros.shape, BF16)
            rows = pl.ds(n_tiles * TM, TM)
            fills = [pltpu.make_async_copy(zeros.at[:, pl.ds(0, width)], dp_hbm.at[rows, pl.ds(dst, width)],
                                           zero_sems.at[k]) for k, (dst, width) in enumerate(zero_pieces)]
            for cp in fills:
                cp.start()
            for cp in fills:
                cp.wait()

        @pl.when(i < n_tiles)
        def _():
            f = lambda ref: ref[...].astype(F32)
            h, cg, dc = f(h_ref), f(cg_ref), f(dc_ref)
            w0, w1, w2 = vec_ref[1:2, :], vec_ref[2:3, :], vec_ref[3:4, :]
            row = lax.broadcasted_iota(jnp.int32, (TM, 1), 0)
            dc_prev = jnp.where(i == 0, 0.0, f(dcp_ref)[HALO - 1:HALO, :])
            dc_next = jnp.where(i == n_tiles - 1, 0.0, f(dcn_ref)[0:1, :])
            dc_up = jnp.where(row == 0, dc_prev, pltpu.roll(dc, 1, 0))
            dc_dn = jnp.where(row == TM - 1, dc_next, pltpu.roll(dc, TM - 1, 0))
            du = w0 * dc_dn + w1 * dc + w2 * dc_up
            u = cg * h
            part_ref[0:1, :] += jnp.sum(u * dc_dn, axis=0, keepdims=True)
            part_ref[1:2, :] += jnp.sum(u * dc, axis=0, keepdims=True)
            part_ref[2:3, :] += jnp.sum(u * dc_up, axis=0, keepdims=True)
            part_ref[3:4, :] += jnp.sum(dc, axis=0, keepdims=True)
            out[:, 0:D] = (du * cg).astype(BF16)
            out[:, D:2 * D] = (du * h).astype(BF16)

        writer.send(i, slot)

    last = n_tiles - 1
    col = lambda cb_: pl.BlockSpec((TM, D), lambda i, cb_=cb_: (jnp.minimum(i, last), cb_))
    lat = lambda w: pl.BlockSpec((TM, w), lambda i: (jnp.minimum(i, last), 0))
    ext = lambda w: pl.BlockSpec((TM, w), lambda i: (i, 0))
    anyspec = pl.BlockSpec(memory_space=pl.ANY)
    return pl.pallas_call(
        body, name="conv_backward", grid=(n_tiles + 1,),
        in_specs=[col(CB_H), col(CB_CG), lat(D),
                  pl.BlockSpec((HALO, D), lambda i: (jnp.clip(i * hb - 1, 0, n_tiles * hb - 1), 0)),
                  pl.BlockSpec((HALO, D), lambda i: (jnp.minimum((i + 1) * hb, n_tiles * hb - 1), 0)),
                  ext(qk), ext(qk), ext(D), pl.BlockSpec(vecs.shape, lambda i: (0, 0)), anyspec, _AFTER_SPEC],
        out_specs=[anyspec, pl.BlockSpec((8, D), lambda i: (0, 0))],
        out_shape=[jax.ShapeDtypeStruct(dp.shape, BF16), jax.ShapeDtypeStruct((8, D), F32)],
        input_output_aliases={9: 0},
        scratch_shapes=[pltpu.VMEM((2, TM, 4 * D), BF16), pltpu.SemaphoreType.DMA((2, len(pieces))),
                        pltpu.VMEM((TM, 3 * D), BF16), pltpu.SemaphoreType.DMA((len(zero_pieces),))],
        compiler_params=_cparams(dimension_semantics=("arbitrary",)),
    )(p_ext, p_ext, dconv, dconv, dconv, dq, dk, dv, vecs, dp, after)


def _input_backward(dp, shards, ids, x, ctx, dx1, modx, modc, norm_w, after):
    seq = x.shape[0]
    lext = seq + ctx.shape[0]
    n_x = seq // TM
    n_w = len(shards.arrays)

    def body(ids_ref, dp_ref, x_ref, ctx_ref, dx1_ref, mx_ref, mc_ref, nw_ref, *rest):
        w_hbm, (_, gx_ref, part_ref, w_ref, w_sems) = rest[:n_w], rest[n_w:]
        i = pl.program_id(0)
        _load_w_in(i, ids_ref, shards, w_hbm, w_ref, w_sems)
        is_ctx = i >= n_x
        dxm = _dot_nt(dp_ref[...], w_ref[...])
        x = jnp.where(is_ctx, ctx_ref[...], x_ref[...])
        r = lax.rsqrt(jnp.mean(x * x, axis=-1, keepdims=True) + EPS)
        xh = x * r
        nw = nw_ref[...]
        sc = jnp.where(is_ctx, mc_ref[1:2, :], mx_ref[1:2, :])
        dxn = dxm * (1.0 + sc)
        dxh = dxn * nw
        dx = r * (dxh - xh * jnp.mean(dxh * xh, axis=-1, keepdims=True))

        @pl.when(jnp.logical_not(is_ctx))
        def _():
            gx_ref[...] = dx1_ref[...] + dx

        @pl.when(i == 0)
        def _():
            part_ref[...] = jnp.zeros((8, D), F32)

        fx = jnp.where(is_ctx, 0.0, 1.0)
        d_shift = jnp.sum(dxm, axis=0, keepdims=True)
        d_scale = jnp.sum(dxm * (xh * nw), axis=0, keepdims=True)
        part_ref[0:1, :] += fx * d_shift
        part_ref[1:2, :] += fx * d_scale
        part_ref[2:3, :] += jnp.sum(dxn * xh, axis=0, keepdims=True)
        part_ref[3:4, :] += (1.0 - fx) * d_shift
        part_ref[4:5, :] += (1.0 - fx) * d_scale

    lat = lambda w: pl.BlockSpec((TM, w), lambda i, ids_: (jnp.minimum(i, n_x - 1), 0))
    ext = lambda w: pl.BlockSpec((TM, w), lambda i, ids_: (i, 0))
    full = lambda a: pl.BlockSpec(a.shape, lambda i, ids_: (0,) * a.ndim)
    grid_spec = pltpu.PrefetchScalarGridSpec(
        num_scalar_prefetch=1, grid=(lext // TM,),
        in_specs=[ext(PW), lat(D), full(ctx), lat(D), full(modx), full(modc), full(norm_w)]
        + [pl.BlockSpec(memory_space=pl.ANY)] * n_w + [_AFTER_SPEC],
        out_specs=[lat(D), pl.BlockSpec((8, D), lambda i, ids_: (0, 0))],
        scratch_shapes=[pltpu.VMEM((D, PW), BF16), pltpu.SemaphoreType.DMA((N_DEV,))])
    return pl.pallas_call(
        body, name="input_backward", grid_spec=grid_spec,
        out_shape=[jax.ShapeDtypeStruct((seq, D), F32), jax.ShapeDtypeStruct((8, D), F32)],
        compiler_params=_cparams(dimension_semantics=("arbitrary",)),
    )(ids, dp, x, ctx, dx1, modx, modc, norm_w, *shards.arrays, after)


def _weight_grad_in_sibling(xm_t, dp, owners, after):
    lext = xm_t.shape[1]

    def body(own_ref, a_ref, b_ref, after_ref, o_ref):
        o_ref[0] = _dot(a_ref[...], b_ref[...]).astype(BF16)

    grid_spec = pltpu.PrefetchScalarGridSpec(
        num_scalar_prefetch=1, grid=(4,),
        in_specs=[pl.BlockSpec((D, lext), lambda j, own: (0, 0)),
                  pl.BlockSpec((lext, WSH), lambda j, own: (0, own[j])), _AFTER_SPEC],
        out_specs=pl.BlockSpec((1, D, WSH), lambda j, own: (j, 0, 0)))
    return pl.pallas_call(
        body, name="weight_grad_in_sibling", grid_spec=grid_spec,
        out_shape=jax.ShapeDtypeStruct((4, D, WSH), BF16),
        compiler_params=_cparams(dimension_semantics=("arbitrary",)),
    )(owners, xm_t, dp, after)


def _weight_grad_in_own(xm_t, dp, owners, recv):
    lext = xm_t.shape[1]

    def body(own_ref, a_ref, b_ref, r_ref, mine_ref, send_ref):
        j = pl.program_id(0)
        for r in range(0, D, TM):
            rows = pl.ds(r, TM)
            total = _dot(a_ref[rows, :], b_ref[...]) + r_ref[0, rows, :].astype(F32)

            @pl.when(j == 0)
            def _():
                mine_ref[0, rows, :] = total

            @pl.when(j > 0)
            def _():
                send_ref[0, rows, :] = total.astype(BF16)

    grid_spec = pltpu.PrefetchScalarGridSpec(
        num_scalar_prefetch=1, grid=(4,),
        in_specs=[pl.BlockSpec((D, lext), lambda j, own: (0, 0), pipeline_mode=pl.Buffered(1)),
                  pl.BlockSpec((lext, WSH), lambda j, own: (0, own[j])),
                  pl.BlockSpec((1, D, WSH), lambda j, own: (j, 0, 0))],
        out_specs=[pl.BlockSpec((1, D, WSH), lambda j, own: (0, 0, 0)),
                   pl.BlockSpec((1, D, WSH), lambda j, own: (jnp.maximum(j - 1, 0), 0, 0))])
    return pl.pallas_call(
        body, name="weight_grad_in_own", grid_spec=grid_spec,
        out_shape=[jax.ShapeDtypeStruct((1, D, WSH), F32), jax.ShapeDtypeStruct((3, D, WSH), BF16)],
        compiler_params=_cparams(dimension_semantics=("arbitrary",)),
    )(owners, xm_t, dp, recv)


def _weight_grad_square(a_t, b):
    seq = a_t.shape[2]

    def body(a_ref, b_ref, o_ref):
        o_ref[:, 0] = _dot(a_ref[0], b_ref[0]).reshape(N_DEV, RSH, D)

    return pl.pallas_call(
        body, name="weight_grad_square", grid=(3,),
        in_specs=[pl.BlockSpec((1, D, seq), lambda t: (t, 0, 0)), pl.BlockSpec((1, seq, D), lambda t: (t, 0, 0))],
        out_specs=pl.BlockSpec((N_DEV, 1, RSH, D), lambda t: (0, t, 0, 0)),
        out_shape=jax.ShapeDtypeStruct((N_DEV, 3, RSH, D), F32),
        compiler_params=_cparams(dimension_semantics=("arbitrary",)),
    )(a_t, b)


def _rows(total, *parts):
    width = max(a.shape[1] for _, a in parts)
    out = None
    for row, a in parts:
        padded = jnp.pad(a, ((row, total - row - a.shape[0]), (0, width - a.shape[1])))
        out = padded if out is None else out + padded
    return out


def kernel(x, c, ctx, c_ctx, norm_w, ada_w, ada_b, w_in, conv_w, conv_b, decay_logit, gn_w, w_a, w_b, w_out, final_norm_w, loss_target, m_c_ctx, m_norm_w, m_ada_w, m_ada_b, m_w_in, m_conv_w, m_conv_b, m_decay_logit, m_gn_w, m_w_a, m_w_b, m_w_out, m_final_norm_w, v_c_ctx, v_norm_w, v_ada_w, v_ada_b, v_w_in, v_conv_w, v_conv_b, v_decay_logit, v_gn_w, v_w_a, v_w_b, v_w_out, v_final_norm_w):
    xi, yi, ci = _coords()
    me = 4 * xi + 2 * yi + ci
    chip = 2 * xi + yi
    seq, ctx_len = x.shape[1], ctx.shape[1]
    assert seq % TM == 0 and seq % RET_C == 0 and ctx_len == TM and seq % GRID_W == 0
    csh = D // N_DEV

    blk = jnp.pad(c, ((0, 7), (0, 0))) + jnp.pad(conv_w[0], ((1, 4), (0, D - csh)))
    got, _ = _all_gather_small(blk, "gather_cond")
    conv_w_all = got[:, 1:4, 0:csh].transpose(1, 0, 2).reshape(3, D)
    c16 = _rows(16, (0, got[:, 0, :]), (N_DEV, c_ctx[None]))
    ada_b_sh = lax.dynamic_slice(ada_b, (0, me * ADA_SH), (1, ADA_SH))
    mod_sh, act16, lg = _modulation(c16, ada_w[0], ada_b_sh, decay_logit[0])
    mod_all, small_done = _all_gather_small(mod_sh, "gather_mod")
    mod_all = mod_all.transpose(1, 0, 2).reshape(16, 3 * D)
    modx = lax.dynamic_slice(mod_all, (me, 0), (1, 3 * D)).reshape(3, D)
    modc = mod_all[8].reshape(3, D)

    x2, tgt = x[0], loss_target[0]
    wblock = lambda n: jax.ShapeDtypeStruct((n, D, WSH), BF16)
    ids_of = lambda ks: jnp.stack([jnp.bitwise_xor(me, k) for k in ks]).astype(jnp.int32)
    own_shard = w_in[0].astype(BF16)[None]
    sq_shards = [w[0].astype(BF16) for w in (w_a, w_b, w_out)]
    wave1 = _Exchange(
        "gather_in_first", [jnp.broadcast_to(own_shard, (2, D, WSH)), own_shard], [wblock(4)], 3,
        lambda srcs, lands: _forward_plan([(0, 1)])(srcs, lands) + _send_plan((4, 2))(srcs[1:], lands),
        after=small_done)
    xm, xm_t, cos_t, sin_t = _prenorm(x2, ctx[0], modx, modc, norm_w, wave1.token)
    (w_pair, _), _ = wave1.wait(xm_t, 1, through=[0])
    first = _Shards([w_pair], [(0, 0), (0, 1)])
    p_ext = _in_projection(None, xm, cos_t, sin_t, w_pair, [((0, 0), (1, 1))], me, "in_projection_pair")
    _, (w_nbr,) = wave1.wait(p_ext, through=[1, 2])
    wave2 = _Exchange(
        "gather_in_second", [w_nbr, own_shard], [wblock(2)], 3,
        lambda srcs, lands: _forward_plan([(0, 2), (1, 3)])(srcs, lands) + _send_plan((6,))(srcs[1:], lands))
    (w_nbr, _), _ = wave2.wait(wave2.token, 2, through=[0])
    second = _Shards([w_nbr], [(0, 0), (0, 1), (0, 2), (0, 3)])
    p_ext = _in_projection(p_ext, xm, cos_t, sin_t, w_nbr, [((0, 4), (2, 5)), ((1, 2), (3, 3))], me,
                           "in_projection_neighbours")
    _, (w_diag,) = wave2.wait(p_ext, through=[1, 2])
    wave3 = _Exchange(
        "gather_in_third", [w_diag] + sq_shards, [jax.ShapeDtypeStruct((N_DEV, RSH, D), BF16)] * 3, 1 + 3 * (N_DEV - 1),
        lambda srcs, lands: _forward_plan([(0, 1)])(srcs, lands) + _gather_plan(srcs[1:], lands))
    (w_diag, *_), _ = wave3.wait(wave3.token, 1, through=[0])
    third = _Shards([w_diag], [(0, 0), (0, 1)])
    p_ext = _in_projection(p_ext, xm, cos_t, sin_t, w_diag, [((0, 6), (1, 7))], me, "in_projection_diagonal")
    w_shards, w_ids = first + second + third, ids_of((0, 1, 4, 2, 5, 3, 6, 7))

    rn, rstd = _retention_forward(p_ext, lg, seq, ctx_len)
    _, sq_lands = wave3.wait(rstd, through=range(1, 7))
    w_a_all, w_b_all, w_out_all = (
        lax.dynamic_update_slice(land, shard[None], (me, 0, 0)).reshape(D, D) for land, shard in zip(sq_lands, sq_shards))
    vecs = _rows(8, (0, modx[2:3]), (1, conv_w_all), (4, conv_b), (5, gn_w), (6, final_norm_w[None]))
    dx1, dp, dconv, dret, op_at, op_b, part_m = _merge(p_ext, rn, rstd, x2, tgt, w_a_all, w_b_all, w_out_all, vecs, seq)

    j4 = jnp.arange(4, dtype=jnp.int32)
    owners = (2 * jnp.bitwise_xor(chip, j4) + ci).astype(jnp.int32)
    owners_sib = (2 * jnp.bitwise_xor(chip, j4) + (1 - ci)).astype(jnp.int32)
    gw_sq = _weight_grad_square(op_at, op_b).reshape(N_DEV, 3 * RSH, D)
    rs_sq_pair = _Exchange("rs_square_pair", [gw_sq], [jax.ShapeDtypeStruct((4, 3 * RSH, D), F32)], 4,
                           _pair_plan(lambda j, chip_, c_: 2 * jnp.bitwise_xor(chip_, j) + (1 - c_)))
    dq, dk, dv, dlg = _retention_backward(p_ext, dret, lg, cos_t, sin_t, seq, ctx_len, rs_sq_pair.token)
    dp, part_c = _conv_backward(p_ext, dconv, dp, dq, dk, dv, vecs, seq, dlg)
    gw_sib = _weight_grad_in_sibling(xm_t, dp, owners_sib, part_c)
    rs_in_pair = _Exchange("rs_in_pair", [gw_sib], [jax.ShapeDtypeStruct((4, D, WSH), BF16)], 4,
                           _pair_plan(lambda j, chip_, c_: j))
    (gw_sq,), (r1_sq,) = rs_sq_pair.wait(rs_in_pair.token)
    own_sq, send_sq = _pair_sum(gw_sq, r1_sq, owners, "pair_sum_square")
    rs_sq_chips = _Exchange("rs_square_chips", [send_sq], [jax.ShapeDtypeStruct((3, 3 * RSH, D), BF16)], 3, _chips_plan)
    _, (r1_in,) = rs_in_pair.wait(rs_sq_chips.token)
    own_in, send_in = _weight_grad_in_own(xm_t, dp, owners, r1_in)
    rs_in_chips = _Exchange("rs_in_chips", [send_in], [jax.ShapeDtypeStruct((3, D, WSH), BF16)], 3, _chips_plan)
    grad_x, part_i = _input_backward(dp, w_shards, w_ids, x2, ctx[0], dx1, modx, modc, norm_w, rs_in_chips.token)

    dl = dlg[:, 0, 0:4]
    dlg_row = jnp.pad(dl[:, 0::2].reshape(1, N_HEADS), ((0, 0), (0, D - N_HEADS))) + jnp.pad(
        dl[:, 1::2].reshape(1, N_HEADS), ((0, 0), (N_HEADS, D - 2 * N_HEADS)))
    partials = _rows(16, (0, part_m[0:1]), (1, part_i[2:3]), (2, part_c[3:4]), (3, part_m[2:3]), (4, part_c[0:3]),
                     (7, part_i[0:2]), (9, part_m[1:2]), (10, part_i[3:5]), (13, part_m[3:4]), (14, dlg_row))
    results = {}
    gather_part = _Exchange("gather_partials", [partials], [jax.ShapeDtypeStruct((N_DEV, 16, D), F32)],
                            N_DEV - 1, _gather_plan)
    _, (r2_in,) = rs_in_chips.wait(gather_part.token)
    results["w_in"] = [o[None] for o in _adam_sharded(w_in[0], m_w_in[0], v_w_in[0], own_in, r2_in, "adam_w_in")]
    _, (got,) = gather_part.wait(results["w_in"][1])
    got = lax.dynamic_update_slice(got, partials[None], (me, 0, 0))
    tot = _sum_devices(got, "sum_partials")
    dmodc = tot[10:13].reshape(1, 3 * D)
    dmod16 = _rows(16, (0, got[:, 7:10, :].reshape(N_DEV, 3 * D)), (N_DEV, dmodc))
    dmod16 = lax.dynamic_slice(dmod16, (0, me * ADA_SH), (16, ADA_SH))
    dmodc8 = _rows(8, (0, dmod16[8:9]))
    g_ada_w, d_ada_w, nm_ada_w, nv_ada_w, cctx_part = _ada_backward(
        act16.T, dmod16, dmodc8, ada_w[0], m_ada_w[0], v_ada_w[0])
    gather_cctx = _Exchange("gather_cctx", [cctx_part], [jax.ShapeDtypeStruct((N_DEV, 8, D), F32)],
                            N_DEV - 1, _gather_plan)
    _, (r2_sq,) = rs_sq_chips.wait(gather_cctx.token)
    square = _adam_square([(w_a[0], m_w_a[0], v_w_a[0]), (w_b[0], m_w_b[0], v_w_b[0]),
                           (w_out[0], m_w_out[0], v_w_out[0])], own_sq, r2_sq)
    _, (got_cctx,) = gather_cctx.wait(square[0][1])
    dact_ctx = _sum_devices(lax.dynamic_update_slice(got_cctx, cctx_part[None], (me, 0, 0)), "sum_cctx")

    view = {"c_ctx": (1, D), "final_norm_w": (1, D)}
    given = {"c_ctx": (c_ctx, m_c_ctx, v_c_ctx), "norm_w": (norm_w, m_norm_w, v_norm_w),
             "conv_b": (conv_b, m_conv_b, v_conv_b), "gn_w": (gn_w, m_gn_w, v_gn_w),
             "final_norm_w": (final_norm_w, m_final_norm_w, v_final_norm_w), "ada_b": (ada_b, m_ada_b, v_ada_b),
             "decay_logit": (decay_logit, m_decay_logit, v_decay_logit), "conv_w": (conv_w, m_conv_w, v_conv_w)}
    small_in = [tuple(a.reshape(view.get(name, a.shape)) for a in given[name]) for name in SMALL_PARAMS]
    conv_w_grad = lax.dynamic_slice(tot, (4, me * csh), (3, csh))
    small_out = _adam_small(tot, dact_ctx, conv_w_grad, tot[14, 0:2 * N_HEADS].reshape(2, N_HEADS), small_in)
    results.update({name: [o.reshape(given[name][0].shape) for o in outs_]
                    for name, outs_ in zip(SMALL_PARAMS, small_out)})
    for name, outs_ in zip(("w_a", "w_b", "w_out"), square):
        results[name] = [o[None] for o in outs_]
    results["ada_w"] = [o[None] for o in (g_ada_w, d_ada_w, nm_ada_w, nv_ada_w)]

    order = ("c_ctx", "norm_w", "ada_w", "ada_b", "w_in", "conv_w", "conv_b", "decay_logit", "gn_w",
             "w_a", "w_b", "w_out", "final_norm_w")
    outs = [results[name][kind] for kind in range(4) for name in order]
    return (tot[13, 0], grad_x[None], *outs)
```

```python
import math

import jax
import jax.numpy as jnp
from jax import lax
from jax.experimental import pallas as pl
from jax.experimental.pallas import tpu as pltpu

F32 = jnp.float32
BF16 = jnp.bfloat16
MESH = pl.DeviceIdType.MESH
_HBM_SPEC = pl.BlockSpec(memory_space=pltpu.HBM)
_SEM_SPEC = pl.BlockSpec(memory_space=pltpu.SEMAPHORE)
_EFFECT = pltpu.SideEffectType.DATAFLOW_SIDE_EFFECTING

N_DEV = 8
D = 1024
N_HEADS = 8
DK = 64
DV = 128
GRID_W = 64
ROPE_BASE = 10000.0
EPS = 1e-6
PW = 9 * D
WSH = PW // N_DEV
RSH = D // N_DEV
ADA_SH = 3 * D // N_DEV
TM = 256
RET_C = 256
HALO = 16
LANES = 128
VMEM_LIMIT = 60 * 1024 * 1024

ADAM_LR = 0.001
ADAM_B1 = 0.9
ADAM_B2 = 0.999
ADAM_EPS = 1e-08
ADAM_WD = 0.01
ADAM_STEP = 10

CB_H, CB_BG, CB_CG, CB_ZA, CB_QK, CB_V, CB_ZB, CB_GA, CB_GB = range(9)


def _cparams(**kw):
    return pltpu.CompilerParams(vmem_limit_bytes=VMEM_LIMIT, **kw)


def _dot(a, b):
    return jnp.dot(a, b, preferred_element_type=F32)


def _dot_nt(a, b):
    return lax.dot_general(a, b, (((1,), (1,)), ((), ())), preferred_element_type=F32)


def _dot_tn(a, b):
    return lax.dot_general(a, b, (((0,), (0,)), ((), ())), preferred_element_type=F32)


def _sigmoid(z):
    return 0.5 * jnp.tanh(0.5 * z) + 0.5


def _row_tile(rows):
    return TM if rows % TM == 0 else rows


def _coords():
    return lax.axis_index("x"), lax.axis_index("y"), lax.axis_index("c")


def _flip(v, bit):
    return 1 - v if bit else v


def _all_gather_small(blk, name):
    rows, cols = blk.shape

    def body(x_ref, out_ref, done, send_sems, recv_sems, local_sem):
        x, y, c = _coords()
        me = 4 * x + 2 * y + c
        mine = pltpu.make_async_copy(x_ref, out_ref.at[me], local_sem)
        mine.start()

        def copy(k, slot):
            peer = (_flip(x, k & 4), _flip(y, k & 2), _flip(c, k & 1))
            return pltpu.make_async_remote_copy(
                src_ref=x_ref, dst_ref=out_ref.at[slot], send_sem=send_sems.at[k - 1],
                recv_sem=recv_sems.at[k - 1], device_id=peer, device_id_type=MESH)

        for k in range(1, N_DEV):
            copy(k, me).start()
        for k in range(1, N_DEV):
            copy(k, jnp.bitwise_xor(me, k)).wait_recv()
        for k in range(1, N_DEV):
            copy(k, me).wait_send()
        mine.wait()
        done[...] = jnp.zeros((8, LANES), F32)

    vmem = pl.BlockSpec(memory_space=pltpu.VMEM)
    return pl.pallas_call(
        body, name=name,
        out_shape=[jax.ShapeDtypeStruct((N_DEV, rows, cols), blk.dtype), jax.ShapeDtypeStruct((8, LANES), F32)],
        in_specs=[vmem], out_specs=[vmem, vmem],
        scratch_shapes=[pltpu.SemaphoreType.DMA((N_DEV - 1,)), pltpu.SemaphoreType.DMA((N_DEV - 1,)),
                        pltpu.SemaphoreType.DMA],
    )(blk)


class _Absent:
    at = property(lambda self: self)

    def __getitem__(self, index):
        return self


class _Exchange:
    def __init__(self, name, srcs, land_shapes, n_copies, plan, after=None):
        self.name, self.plan, self.n_copies = name, plan, n_copies
        self.n_src, self.n_land = len(srcs), len(land_shapes)
        hbm = lambda a: pltpu.HBM(a.shape, a.dtype)
        n = self.n_src + self.n_land
        lands = [pltpu.with_memory_space_constraint(lax.empty(s.shape, s.dtype), pltpu.HBM) for s in land_shapes]
        srcs = [pltpu.with_memory_space_constraint(s, pltpu.HBM) for s in srcs]
        extra = [] if after is None else [after]

        def body(*refs):
            send_sems, recv_sems = refs[n + len(extra)], refs[n + len(extra) + 1]
            for cp in self._copies(refs, send_sems, recv_sems):
                cp.start()
            refs[-1][...] = jnp.zeros((8, LANES), F32)

        outs = pl.pallas_call(
            body, name=name + "_start",
            out_shape=(pltpu.SemaphoreType.DMA((n_copies,)), pltpu.SemaphoreType.DMA((n_copies,)),
                       *[hbm(a) for a in srcs], *[hbm(a) for a in lands], jax.ShapeDtypeStruct((8, LANES), F32)),
            in_specs=[_HBM_SPEC] * n + [pl.BlockSpec(memory_space=pl.ANY)] * len(extra),
            out_specs=(_SEM_SPEC, _SEM_SPEC, *[_HBM_SPEC] * n, pl.BlockSpec(memory_space=pltpu.VMEM)),
            input_output_aliases={i: 2 + i for i in range(n)},
            compiler_params=pltpu.CompilerParams(has_side_effects=_EFFECT),
        )(*srcs, *lands, *extra)
        self.send_sems, self.recv_sems = outs[0], outs[1]
        self.buffers = list(outs[2:2 + n])
        self.token = outs[-1]
        self.waited = 0

    def _copies(self, refs, send_sems, recv_sems, lo=0, hi=None):
        src_refs, land_refs = refs[:self.n_src], refs[self.n_src:self.n_src + self.n_land]
        planned = list(enumerate(self.plan(src_refs, land_refs)))[lo:hi]
        return [pltpu.make_async_remote_copy(src_ref=s, dst_ref=d, send_sem=send_sems.at[k], recv_sem=recv_sems.at[k],
                                             device_id=dev, device_id_type=MESH) for k, (s, d, dev) in planned]

    def wait(self, after, count=None, through=None):
        n = self.n_src + self.n_land
        lo = self.waited
        hi = self.n_copies if count is None else lo + count
        self.waited = hi
        through = list(range(n)) if through is None else list(through)
        m = len(through)

        def body(*refs):
            held = [_Absent()] * n
            for pos, ref in zip(through, refs[:m]):
                held[pos] = ref
            for cp in self._copies(held, refs[m], refs[m + 1], lo, hi):
                cp.wait_send()
                cp.wait_recv()

        outs = pl.pallas_call(
            body, name=f"{self.name}_wait{lo}" if lo or hi < self.n_copies else self.name + "_wait",
            out_shape=tuple(pltpu.HBM(self.buffers[p].shape, self.buffers[p].dtype) for p in through),
            in_specs=[_HBM_SPEC] * m + [_SEM_SPEC, _SEM_SPEC, pl.BlockSpec(memory_space=pl.ANY)],
            out_specs=tuple([_HBM_SPEC] * m),
            input_output_aliases={i: i for i in range(m)},
            compiler_params=pltpu.CompilerParams(has_side_effects=_EFFECT),
        )(*[self.buffers[p] for p in through], self.send_sems, self.recv_sems, after)
        for pos, out in zip(through, outs):
            self.buffers[pos] = out
        return list(self.buffers[:self.n_src]), list(self.buffers[self.n_src:])


def _pair_plan(src_index):
    def plan(srcs, lands):
        x, y, c = _coords()
        chip = 2 * x + y
        return [(srcs[0].at[src_index(j, chip, c)], lands[0].at[j], (x, y, 1 - c)) for j in range(4)]
    return plan


def _chips_plan(srcs, lands):
    x, y, c = _coords()
    return [(srcs[0].at[j - 1], lands[0].at[j - 1], (_flip(x, j & 2), _flip(y, j & 1), c)) for j in range(1, 4)]


def _peer(k):
    x, y, c = _coords()
    return _flip(x, k & 4), _flip(y, k & 2), _flip(c, k & 1)


def _send_plan(ks):
    def plan(srcs, lands):
        return [(srcs[0].at[0], lands[0].at[p], _peer(k)) for p, k in enumerate(ks)]
    return plan


def _forward_plan(moves):
    def plan(srcs, lands):
        return [(srcs[0].at[s], srcs[0].at[d], _peer(1)) for s, d in moves]
    return plan


def _gather_plan(srcs, lands):
    x, y, c = _coords()
    me = 4 * x + 2 * y + c
    return [(srcs[a], lands[a].at[me], (_flip(x, k & 4), _flip(y, k & 2), _flip(c, k & 1)))
            for a in range(len(srcs)) for k in range(1, N_DEV)]


def _pair_sum(grad, recv, owners, name):
    _, rows, cols = grad.shape
    tr = rows

    def body(own_ref, g_ref, r_ref, mine_ref, send_ref):
        j = pl.program_id(1)
        total = g_ref[...] + r_ref[...]

        @pl.when(j == 0)
        def _():
            mine_ref[...] = total

        @pl.when(j > 0)
        def _():
            send_ref[...] = total.astype(BF16)

    grid_spec = pltpu.PrefetchScalarGridSpec(
        num_scalar_prefetch=1, grid=(rows // tr, 4),
        in_specs=[pl.BlockSpec((1, tr, cols), lambda i, j, own: (own[j], i, 0)),
                  pl.BlockSpec((1, tr, cols), lambda i, j, own: (j, i, 0))],
        out_specs=[pl.BlockSpec((1, tr, cols), lambda i, j, own: (0, i, 0)),
                   pl.BlockSpec((1, tr, cols), lambda i, j, own: (jnp.maximum(j - 1, 0), i, 0))])
    return pl.pallas_call(
        body, name=name, grid_spec=grid_spec,
        out_shape=[jax.ShapeDtypeStruct((1, rows, cols), F32), jax.ShapeDtypeStruct((3, rows, cols), BF16)],
        compiler_params=_cparams(dimension_semantics=("arbitrary", "arbitrary")),
    )(owners, grad, recv)


def _modulation(c16, ada_w_sh, ada_b_sh, decay_logit):
    def body(c_ref, w_ref, b_ref, dl_ref, mod_ref, act_ref, lg_ref):
        cv = c_ref[...]
        act = cv * _sigmoid(cv)
        act_ref[...] = act
        mod_ref[...] = jnp.dot(act, w_ref[...], preferred_element_type=F32,
                               precision=lax.Precision.HIGHEST) + b_ref[...]
        z = dl_ref[...]
        lg_ref[...] = jnp.minimum(z, 0.0) - jnp.log(1.0 + jnp.exp(-jnp.abs(z)))

    return pl.pallas_call(
        body, name="modulation",
        out_shape=[jax.ShapeDtypeStruct((16, ADA_SH), F32), jax.ShapeDtypeStruct((16, D), F32),
                   jax.ShapeDtypeStruct(decay_logit.shape, F32)],
        compiler_params=_cparams(),
    )(c16, ada_w_sh, ada_b_sh, decay_logit)


def _adam_update(w, g, m, v):
    m2 = ADAM_B1 * m + (1.0 - ADAM_B1) * g
    v2 = ADAM_B2 * v + (1.0 - ADAM_B2) * (g * g)
    m_hat = m2 / (1.0 - ADAM_B1 ** ADAM_STEP)
    v_hat = v2 / (1.0 - ADAM_B2 ** ADAM_STEP)
    delta = -ADAM_LR * (m_hat / (jnp.sqrt(v_hat) + ADAM_EPS) + ADAM_WD * w)
    return delta, m2, v2


def _adam_sharded(w, m, v, own, recv, name):
    rows, cols = w.shape
    tr = _row_tile(rows)

    def body(w_ref, m_ref, v_ref, g0, g1, g2, g3, g_ref, d_ref, m_out, v_out):
        g = ((g0[0] + g1[0].astype(F32)) + g2[0].astype(F32)) + g3[0].astype(F32)
        delta, m2, v2 = _adam_update(w_ref[...], g, m_ref[...], v_ref[...])
        g_ref[...] = g
        d_ref[...] = delta
        m_out[...] = m2
        v_out[...] = v2

    flat = pl.BlockSpec((tr, cols), lambda i: (i, 0))
    part = lambda j: pl.BlockSpec((1, tr, cols), lambda i, j=j: (j, i, 0))
    return pl.pallas_call(
        body, name=name, grid=(rows // tr,),
        in_specs=[flat, flat, flat, part(0), part(0), part(1), part(2)],
        out_specs=[flat] * 4,
        out_shape=[jax.ShapeDtypeStruct((rows, cols), F32)] * 4,
        compiler_params=_cparams(dimension_semantics=("arbitrary",)),
    )(w, m, v, own, recv, recv, recv)


def _ada_backward(act_t, dmod16, dmodc8, ada_w_sh, m, v):
    def body(at_ref, dm_ref, dc_ref, w_ref, m_ref, v_ref, g_ref, d_ref, m_out, v_out, pc_ref):
        g = jnp.dot(at_ref[...], dm_ref[...], preferred_element_type=F32, precision=lax.Precision.HIGHEST)
        w = w_ref[...]
        delta, m2, v2 = _adam_update(w, g, m_ref[...], v_ref[...])
        g_ref[...] = g
        d_ref[...] = delta
        m_out[...] = m2
        v_out[...] = v2
        pc_ref[...] = lax.dot_general(dc_ref[...], w, (((1,), (1,)), ((), ())), preferred_element_type=F32,
                                      precision=lax.Precision.HIGHEST)

    return pl.pallas_call(
        body, name="ada_backward",
        out_shape=[jax.ShapeDtypeStruct((D, ADA_SH), F32)] * 4 + [jax.ShapeDtypeStruct((8, D), F32)],
        compiler_params=_cparams(),
    )(act_t, dmod16, dmodc8, ada_w_sh, m, v)


def _sum_devices(gathered, name):
    _, rows, cols = gathered.shape

    def body(g_ref, o_ref):
        acc = g_ref[0]
        for d in range(1, N_DEV):
            acc = acc + g_ref[d]
        o_ref[...] = acc

    return pl.pallas_call(body, name=name, out_shape=jax.ShapeDtypeStruct((rows, cols), F32),
                          compiler_params=_cparams())(gathered)


SMALL_PARAMS = ("c_ctx", "norm_w", "conv_b", "gn_w", "final_norm_w", "ada_b", "decay_logit", "conv_w")


def _adam_small(tot, dact_ctx, conv_w_grad, decay_grad, params):
    n = len(SMALL_PARAMS)

    def body(tot_ref, dact_ref, cwg_ref, dlg_ref, *refs):
        ins, outs = refs[:3 * n], refs[3 * n:]
        raw = {
            "c_ctx": dact_ref[0:1, :],
            "norm_w": tot_ref[1:2, :], "conv_b": tot_ref[2:3, :], "gn_w": tot_ref[3:4, :],
            "final_norm_w": tot_ref[0:1, :],
            "ada_b": jnp.concatenate([tot_ref[7 + r:8 + r, :] + tot_ref[10 + r:11 + r, :] for r in range(3)], axis=1),
            "decay_logit": dlg_ref[...],
            "conv_w": cwg_ref[...],
        }
        for k, name in enumerate(SMALL_PARAMS):
            lead = (0,) if len(ins[3 * k].shape) == 3 else ()
            w, m, v = (ref[lead + (...,)] for ref in ins[3 * k:3 * k + 3])
            g = raw[name]
            if name == "c_ctx":
                s = _sigmoid(w)
                g = g * (s * (1.0 + w * (1.0 - s)))
            elif name == "decay_logit":
                g = g * (1.0 - _sigmoid(w))
            delta, m2, v2 = _adam_update(w, g, m, v)
            for ref, val in zip(outs[4 * k:4 * k + 4], (g, delta, m2, v2)):
                ref[lead + (...,)] = val

    flat = [a for trio in params for a in trio]
    out_shape = [jax.ShapeDtypeStruct(trio[0].shape, F32) for trio in params for _ in range(4)]
    outs = pl.pallas_call(body, name="adam_small", out_shape=out_shape, compiler_params=_cparams())(
        tot, dact_ctx, conv_w_grad, decay_grad, *flat)
    return [outs[4 * k:4 * k + 4] for k in range(n)]


def _adam_square(params, own, recv):
    def body(own_ref, recv_ref, *refs):
        ins, outs = refs[:9], refs[9:]
        for k in range(3):
            rows = slice(k * RSH, (k + 1) * RSH)
            g = own_ref[0, rows, :]
            for j in range(3):
                g = g + recv_ref[j, rows, :].astype(F32)
            delta, m2, v2 = _adam_update(ins[3 * k][...], g, ins[3 * k + 1][...], ins[3 * k + 2][...])
            for ref, val in zip(outs[4 * k:4 * k + 4], (g, delta, m2, v2)):
                ref[...] = val

    flat = [a for trio in params for a in trio]
    outs = pl.pallas_call(body, name="adam_square", out_shape=[jax.ShapeDtypeStruct((RSH, D), F32)] * 12,
                          compiler_params=_cparams())(own, recv, *flat)
    return [outs[4 * k:4 * k + 4] for k in range(3)]


def _rope(t, cos, sin):
    lane = lax.broadcasted_iota(jnp.int32, (1, LANES), 1)
    first_half = jnp.bitwise_and(lane, DK // 2) == 0
    partner = jnp.where(first_half, pltpu.roll(t, LANES - DK // 2, 1), pltpu.roll(t, DK // 2, 1))
    return t * cos + partner * sin


class _Shards:
    def __init__(self, arrays, picks):
        self.arrays, self.picks = list(arrays), list(picks)

    def __add__(self, other):
        shift = len(self.arrays)
        return _Shards(self.arrays + other.arrays, self.picks + [(a + shift, s) for a, s in other.picks])


def _load_w_in(step, ids_ref, shards, w_refs, w_vmem, sems):
    @pl.when(step == 0)
    def _():
        copies = []
        for p, (a, slot) in enumerate(shards.picks):
            col = pl.multiple_of(ids_ref[p] * WSH, LANES)
            copies.append(pltpu.make_async_copy(w_refs[a].at[slot], w_vmem.at[:, pl.ds(col, WSH)], sems.at[p]))
        for cp in copies:
            cp.start()
        for cp in copies:
            cp.wait()


def _project_columns(xmb, w_ref, shard, cos, sin, p_ref):
    width = 2 * WSH
    q_lo, k_lo, k_hi = CB_QK * D, CB_QK * D + N_HEADS * DK, CB_V * D
    q_pair, k_pair = q_lo // width, k_lo // width
    assert (q_pair + 1) * width == k_lo and k_pair * width == k_lo and k_hi <= (k_pair + 1) * width
    mxu_cols = 2 * LANES

    def columns(rot_lo, rot_hi, scale):
        for a in range(0, width, mxu_cols):
            acc = _dot(xmb, w_ref[:, a:a + mxu_cols])
            for col in range(a, a + mxu_cols, LANES):
                piece = acc[:, col - a:col - a + LANES]
                if rot_lo <= col < rot_hi:
                    piece = _rope(piece if scale == 1.0 else piece * scale, cos, sin)
                p_ref[:, col:col + LANES] = piece.astype(BF16)

    @pl.when(shard == q_pair)
    def _():
        columns(q_lo - q_pair * width, width, 1.0)

    @pl.when(shard == k_pair)
    def _():
        columns(0, k_hi - k_lo, DK ** -0.5)

    @pl.when(jnp.logical_and(shard != q_pair, shard != k_pair))
    def _():
        columns(0, 0, 1.0)


_AFTER_SPEC = pl.BlockSpec(memory_space=pl.ANY)


def _prenorm(x, ctx, modx, modc, norm_w, after):
    tp = 4 * TM
    seq, ctx_len = x.shape[0], ctx.shape[0]
    n_x_tiles = seq // tp
    lext = seq + ctx_len
    assert seq % tp == 0 and ctx_len <= tp and tp % GRID_W == 0 and tp // GRID_W <= 16
    grid_rows = tp // GRID_W

    def body(x_ref, ctx_ref, mx_ref, mc_ref, nw_ref, after_ref, xm_ref, xmt_ref, cos_ref, sin_ref, col_cos, col_sin):
        del after_ref
        i = pl.program_id(0)
        is_ctx = i >= n_x_tiles
        ctx_rows = jnp.concatenate([ctx_ref[...], jnp.zeros((tp - ctx_len, D), F32)], axis=0)
        x = jnp.where(is_ctx, ctx_rows, x_ref[...])
        r = lax.rsqrt(jnp.mean(x * x, axis=-1, keepdims=True) + EPS)
        sh = jnp.where(is_ctx, mc_ref[0:1, :], mx_ref[0:1, :])
        sc = jnp.where(is_ctx, mc_ref[1:2, :], mx_ref[1:2, :])
        xm = (x * r * nw_ref[...]) * (1.0 + sc) + sh
        xm_ref[...] = xm.astype(BF16)
        xmt_ref[...] = xm.T.astype(BF16)
        lane = lax.broadcasted_iota(jnp.int32, (1, LANES), 1)
        n_freq = DK // 4
        inv = jnp.exp((lane % n_freq).astype(F32) * (-math.log(ROPE_BASE) / n_freq))
        by_column = (lane % (2 * n_freq)) >= n_freq
        sign = jnp.where((lane % DK) < DK // 2, -1.0, 1.0)

        @pl.when(i == 0)
        def _():
            col = (lax.broadcasted_iota(jnp.int32, (tp, 1), 0) % GRID_W).astype(F32) * inv
            col_cos[...] = jnp.cos(col)
            col_sin[...] = jnp.sin(col) * sign

        row = (i * grid_rows + lax.broadcasted_iota(jnp.int32, (16, 1), 0)).astype(F32) * inv
        row_cos, row_sin = jnp.cos(row), jnp.sin(row) * sign
        tall = lambda t: jnp.concatenate(
            [jnp.broadcast_to(t[g:g + 1, :], (GRID_W, LANES)) for g in range(grid_rows)], axis=0)
        cos_ref[...] = jnp.where(is_ctx, 1.0, jnp.where(by_column, col_cos[...], tall(row_cos)))
        sin_ref[...] = jnp.where(is_ctx, 0.0, jnp.where(by_column, col_sin[...], tall(row_sin)))

    full = lambda a: pl.BlockSpec(a.shape, lambda i: (0,) * a.ndim)
    row = lambda width: pl.BlockSpec((tp, width), lambda i: (i, 0))
    return pl.pallas_call(
        body, name="prenorm", grid=(n_x_tiles + 1,),
        in_specs=[pl.BlockSpec((tp, D), lambda i: (jnp.minimum(i, n_x_tiles - 1), 0)), full(ctx),
                  full(modx), full(modc), full(norm_w), _AFTER_SPEC],
        out_specs=[row(D), pl.BlockSpec((D, tp), lambda i: (0, i)), row(LANES), row(LANES)],
        out_shape=[jax.ShapeDtypeStruct((lext, D), BF16), jax.ShapeDtypeStruct((D, lext), BF16),
                   jax.ShapeDtypeStruct((lext, LANES), F32), jax.ShapeDtypeStruct((lext, LANES), F32)],
        scratch_shapes=[pltpu.VMEM((tp, LANES), F32), pltpu.VMEM((tp, LANES), F32)],
        compiler_params=_cparams(dimension_semantics=("arbitrary",)),
    )(x, ctx, modx, modc, norm_w, after)


def _in_projection(p_ext, xm, cos_t, sin_t, w, pairs, me, name):
    lext = xm.shape[0]
    tr = lext // 4
    meta = []
    for (slot_a, k_a), (slot_b, k_b) in pairs:
        assert k_a ^ k_b == 1
        dev_a = jnp.bitwise_xor(me, k_a)
        a_low = dev_a % 2 == 0
        meta += [jnp.where(a_low, slot_a, slot_b), jnp.where(a_low, slot_b, slot_a), dev_a // 2]
    meta = jnp.stack(meta).astype(jnp.int32)

    def body(meta_ref, xm_ref, cos_ref, sin_ref, w_hbm, *rest):
        p_ref, w_ref, sems = rest[-3:]
        j, i = pl.program_id(0), pl.program_id(1)

        @pl.when(i == 0)
        def _():
            copies = [pltpu.make_async_copy(w_hbm.at[meta_ref[3 * j + h]], w_ref.at[:, h * WSH:(h + 1) * WSH],
                                            sems.at[h]) for h in range(2)]
            for cp in copies:
                cp.start()
            for cp in copies:
                cp.wait()

        _project_columns(xm_ref[...], w_ref, meta_ref[3 * j + 2], cos_ref[...], sin_ref[...], p_ref)

    row = lambda width: pl.BlockSpec((tr, width), lambda j, i, meta_: (i, 0))
    grid_spec = pltpu.PrefetchScalarGridSpec(
        num_scalar_prefetch=1, grid=(len(pairs), lext // tr),
        in_specs=[row(D), row(LANES), row(LANES), pl.BlockSpec(memory_space=pl.ANY)]
        + ([] if p_ext is None else [pl.BlockSpec(memory_space=pl.ANY)]),
        out_specs=pl.BlockSpec((tr, 2 * WSH), lambda j, i, meta_: (i, meta_[3 * j + 2])),
        scratch_shapes=[pltpu.VMEM((D, 2 * WSH), BF16), pltpu.SemaphoreType.DMA((2,))])
    return pl.pallas_call(
        body, name=name, grid_spec=grid_spec,
        out_shape=jax.ShapeDtypeStruct((lext, PW), BF16),
        input_output_aliases={} if p_ext is None else {5: 0},
        compiler_params=_cparams(dimension_semantics=("arbitrary", "arbitrary")),
    )(meta, xm, cos_t, sin_t, w, *([] if p_ext is None else [p_ext]))


def _decay_tables(lgf, lgb, n):
    i = lax.broadcasted_iota(jnp.int32, (n, 1), 0).astype(F32)
    return dict(i=i, k_f=jnp.exp(lgf * (n - 1.0 - i)), k_b=jnp.exp(lgb * i),
                q_f=jnp.exp(lgf * (i + 1.0)), q_b=jnp.exp(lgb * (n - i)))


def _decay_matrix(lgf, lgb, n):
    ii = lax.broadcasted_iota(jnp.int32, (n, n), 0)
    jj = lax.broadcasted_iota(jnp.int32, (n, n), 1)
    diff = (ii - jj).astype(F32)
    low = jnp.exp(lgf * jnp.maximum(diff, 0.0))
    up = jnp.exp(lgb * jnp.maximum(-diff, 0.0))
    return jnp.where(diff > 0, low, jnp.where(diff < 0, up, 2.0)), diff


def _cat_lanes(a, b):
    return jnp.concatenate([a.astype(BF16), b.astype(BF16)], axis=1)


def _retention_forward(p_ext, lg, seq, ctx_len):
    lext = seq + ctx_len
    n_chunks = seq // RET_C
    C = RET_C

    def body(lg_ref, q_ref, k_ref, v_ref, rn_ref, rstd_ref, kv_scr, sf_scr, sb_scr):
        pair = pl.program_id(0)
        lane = lax.broadcasted_iota(jnp.int32, (1, LANES), 1)
        heads = range(2)
        hmask = [(lane // DK == hh).astype(F32) for hh in heads]
        lgf = [lg_ref[0, 2 * pair + hh] for hh in heads]
        lgb = [lg_ref[1, 2 * pair + hh] for hh in heads]
        vs = [slice(hh * DV, (hh + 1) * DV) for hh in heads]
        t = [_decay_tables(lgf[hh], lgb[hh], C) for hh in heads]
        kc_all = k_ref[seq:lext, :].astype(F32)
        s0 = []
        for hh in heads:
            tc = _decay_tables(lgf[hh], lgb[hh], ctx_len)
            kc = kc_all * hmask[hh]
            s0.append(_dot_tn(_cat_lanes(kc * tc["k_f"], kc * tc["k_b"]), v_ref[seq:lext, vs[hh]]))

        def increments(c, carry):
            rows = pl.ds(pl.multiple_of(c * C, C), C)
            k_all = k_ref[rows, :].astype(F32)
            for hh in heads:
                k = k_all * hmask[hh]
                kv_scr[hh, c] = _dot_tn(_cat_lanes(k * t[hh]["k_f"], k * t[hh]["k_b"]), v_ref[rows, vs[hh]])
            return carry

        lax.fori_loop(0, n_chunks, increments, 0, unroll=2)
        gf_c = [jnp.exp(lgf[hh] * C) for hh in heads]
        gb_c = [jnp.exp(lgb[hh] * C) for hh in heads]

        def scan_f(c, s):
            for hh in heads:
                sf_scr[hh, c] = s[hh]
            return tuple(gf_c[hh] * s[hh] + kv_scr[hh, c, 0:LANES, :] for hh in heads)

        def scan_b(n, s):
            c = n_chunks - 1 - n
            for hh in heads:
                sb_scr[hh, c] = s[hh]
            return tuple(gb_c[hh] * s[hh] + kv_scr[hh, c, LANES:2 * LANES, :] for hh in heads)

        lax.fori_loop(0, n_chunks, scan_f, tuple(s0[hh][0:LANES] for hh in heads))
        lax.fori_loop(0, n_chunks, scan_b, tuple(s0[hh][LANES:2 * LANES] for hh in heads))
        dmat = [_decay_matrix(lgf[hh], lgb[hh], C)[0] for hh in heads]

        def outputs(c, carry):
            rows = pl.ds(pl.multiple_of(c * C, C), C)
            q_all = q_ref[rows, :].astype(F32)
            k = k_ref[rows, :]
            for hh in heads:
                q = q_all * hmask[hh]
                v = v_ref[rows, vs[hh]]
                s = _dot_nt(q.astype(BF16), k)
                inner = _dot((s * dmat[hh]).astype(BF16), v)
                states = jnp.concatenate([sf_scr[hh, c], sb_scr[hh, c]], axis=0).astype(BF16)
                ret = inner + _dot(_cat_lanes(q * t[hh]["q_f"], q * t[hh]["q_b"]), states)
                mu = jnp.mean(ret, axis=-1, keepdims=True)
                cen = ret - mu
                rstd = lax.rsqrt(jnp.mean(cen * cen, axis=-1, keepdims=True) + EPS)
                rn_ref[rows, vs[hh]] = cen * rstd
                rstd_ref[rows, vs[hh]] = jnp.broadcast_to(rstd, (C, DV))
            return carry

        lax.fori_loop(0, n_chunks, outputs, 0, unroll=2)

    qk0 = CB_QK * D // LANES
    return pl.pallas_call(
        body, name="retention_forward", grid=(N_HEADS // 2,),
        in_specs=[pl.BlockSpec(memory_space=pltpu.SMEM),
                  pl.BlockSpec((lext, LANES), lambda g: (0, qk0 + g)),
                  pl.BlockSpec((lext, LANES), lambda g: (0, qk0 + N_HEADS // 2 + g)),
                  pl.BlockSpec((lext, 2 * DV), lambda g: (0, CB_V * D // (2 * DV) + g))],
        out_specs=[pl.BlockSpec((seq, 2 * DV), lambda g: (0, g))] * 2,
        out_shape=[jax.ShapeDtypeStruct((seq, D), F32)] * 2,
        scratch_shapes=[pltpu.VMEM((2, n_chunks, 2 * LANES, DV), F32), pltpu.VMEM((2, n_chunks, LANES, DV), F32),
                        pltpu.VMEM((2, n_chunks, LANES, DV), F32)],
        compiler_params=_cparams(dimension_semantics=("arbitrary",)),
    )(lg, p_ext, p_ext, p_ext)


def _retention_backward(p_ext, dret, lg, cos_t, sin_t, seq, ctx_len, after):
    lext = seq + ctx_len
    n_chunks = seq // RET_C
    C = RET_C

    def body(lg_ref, q_ref, k_ref, v_ref, do_ref, cos_ref, sin_ref, after_ref, dq_ref, dk_ref, dv_ref, dlg_ref,
             kv_scr, g_scr, sf_scr, sb_scr, gfn_scr, gbp_scr):
        pair = pl.program_id(0)
        lane = lax.broadcasted_iota(jnp.int32, (1, LANES), 1)
        heads = range(2)
        hmask = [(lane // DK == hh).astype(F32) for hh in heads]
        lgf = [lg_ref[0, 2 * pair + hh] for hh in heads]
        lgb = [lg_ref[1, 2 * pair + hh] for hh in heads]
        vs = [slice(hh * DV, (hh + 1) * DV) for hh in heads]
        t = [_decay_tables(lgf[hh], lgb[hh], C) for hh in heads]
        tc = [_decay_tables(lgf[hh], lgb[hh], ctx_len) for hh in heads]
        kc_all = k_ref[seq:lext, :].astype(F32)
        kc = [kc_all * hmask[hh] for hh in heads]
        vc = [v_ref[seq:lext, vs[hh]] for hh in heads]
        kc_cat = [_cat_lanes(kc[hh] * tc[hh]["k_f"], kc[hh] * tc[hh]["k_b"]) for hh in heads]
        s0 = [_dot_tn(kc_cat[hh], vc[hh]) for hh in heads]

        def increments(c, carry):
            rows = pl.ds(pl.multiple_of(c * C, C), C)
            k_all = k_ref[rows, :].astype(F32)
            q_all = q_ref[rows, :].astype(F32)
            for hh in heads:
                k, q = k_all * hmask[hh], q_all * hmask[hh]
                kv_scr[hh, c] = _dot_tn(_cat_lanes(k * t[hh]["k_f"], k * t[hh]["k_b"]), v_ref[rows, vs[hh]])
                g_scr[hh, c] = _dot_tn(_cat_lanes(q * t[hh]["q_f"], q * t[hh]["q_b"]), do_ref[rows, vs[hh]])
            return carry

        lax.fori_loop(0, n_chunks, increments, 0, unroll=2)
        gf_c = [jnp.exp(lgf[hh] * C) for hh in heads]
        gb_c = [jnp.exp(lgb[hh] * C) for hh in heads]

        def scan_f(c, s):
            for hh in heads:
                sf_scr[hh, c] = s[hh]
            return tuple(gf_c[hh] * s[hh] + kv_scr[hh, c, 0:LANES, :] for hh in heads)

        def scan_b(n, s):
            c = n_chunks - 1 - n
            for hh in heads:
                sb_scr[hh, c] = s[hh]
            return tuple(gb_c[hh] * s[hh] + kv_scr[hh, c, LANES:2 * LANES, :] for hh in heads)

        def scan_gf(n, carry):
            c = n_chunks - 1 - n
            for hh in heads:
                gfn_scr[hh, c] = carry[hh]
            return tuple(g_scr[hh, c, 0:LANES, :] + gf_c[hh] * carry[hh] for hh in heads)

        def scan_gb(c, carry):
            for hh in heads:
                gbp_scr[hh, c] = carry[hh]
            return tuple(g_scr[hh, c, LANES:2 * LANES, :] + gb_c[hh] * carry[hh] for hh in heads)

        lax.fori_loop(0, n_chunks, scan_f, tuple(s0[hh][0:LANES] for hh in heads))
        lax.fori_loop(0, n_chunks, scan_b, tuple(s0[hh][LANES:2 * LANES] for hh in heads))
        zero_state = jnp.zeros((LANES, DV), F32)
        gf_first = lax.fori_loop(0, n_chunks, scan_gf, (zero_state, zero_state))
        gb_last = lax.fori_loop(0, n_chunks, scan_gb, (zero_state, zero_state))

        dmat, w_f, w_b = [], [], []
        for hh in heads:
            dm, diff = _decay_matrix(lgf[hh], lgb[hh], C)
            dmat.append(dm)
            w_f.append(jnp.where(diff > 0, diff * dm, 0.0))
            w_b.append(jnp.where(diff < 0, -diff * dm, 0.0))

        def total(a):
            rows_, width = a.shape
            part = jnp.sum(a.reshape(rows_ // 8, 8, width), axis=0)
            return part[:, 0:LANES] + part[:, LANES:2 * LANES] if width == 2 * LANES else part

        def chunk_head(hh, c, rows, q_all, k_all, dlgf, dlgb):
            th = t[hh]
            qm = q_all * hmask[hh]
            km = k_all * hmask[hh]
            qb, kb = qm.astype(BF16), km.astype(BF16)
            v = v_ref[rows, vs[hh]]
            do = do_ref[rows, vs[hh]]
            s = _dot_nt(qb, kb)
            dsd = _dot_nt(do, v)
            ds = (dsd * dmat[hh]).astype(BF16)
            dq_in = _dot(ds, kb)
            dk_in = _dot_tn(ds, qb)
            dv_in = _dot_tn((s * dmat[hh]).astype(BF16), do)
            prod = s * dsd
            dlgf = dlgf + total(prod * w_f[hh])
            dlgb = dlgb + total(prod * w_b[hh])
            sf, sb = sf_scr[hh, c], sb_scr[hh, c]
            states = jnp.concatenate([sf, sb], axis=0).astype(BF16)
            dqc = _dot_nt(do, states)
            dqf = dqc[:, 0:LANES] * th["q_f"]
            dqb = dqc[:, LANES:2 * LANES] * th["q_b"]
            dq = (dq_in + dqf + dqb) * hmask[hh]
            dlgf = dlgf + total((th["i"] + 1.0) * (qm * dqf))
            dlgb = dlgb + total((C - th["i"]) * (qm * dqb))
            gfn, gbp = gfn_scr[hh, c], gbp_scr[hh, c]
            gstates = jnp.concatenate([gfn, gbp], axis=0).astype(BF16)
            dkc = _dot_nt(v, gstates)
            dkf = dkc[:, 0:LANES] * th["k_f"]
            dkb = dkc[:, LANES:2 * LANES] * th["k_b"]
            dk = dk_in + dkf + dkb
            dlgf = dlgf + total((C - 1.0 - th["i"]) * (km * dkf)) + C * gf_c[hh] * total(gfn * sf)
            dlgb = dlgb + total(th["i"] * (km * dkb)) + C * gb_c[hh] * total(gbp * sb)
            dv = dv_in + _dot(_cat_lanes(km * th["k_f"], km * th["k_b"]), gstates)
            dv_ref[rows, vs[hh]] = dv.astype(BF16)
            return dq, dk, dlgf, dlgb

        def chunk(c, carry):
            rows = pl.ds(pl.multiple_of(c * C, C), C)
            q_all = q_ref[rows, :].astype(F32)
            k_all = k_ref[rows, :].astype(F32)
            dq0, dk0, f0, b0 = chunk_head(0, c, rows, q_all, k_all, carry[0], carry[1])
            dq1, dk1, f1, b1 = chunk_head(1, c, rows, q_all, k_all, carry[2], carry[3])
            cos, sin = cos_ref[rows, :], sin_ref[rows, :]
            dq_ref[rows, :] = _rope(dq0 + dq1, cos, -sin).astype(BF16)
            dk_ref[rows, :] = (_rope(dk0 + dk1, cos, -sin) * (DK ** -0.5)).astype(BF16)
            return f0, b0, f1, b1

        zero = jnp.zeros((8, LANES), F32)
        sums = lax.fori_loop(0, n_chunks, chunk, (zero, zero, zero, zero), unroll=2)

        dlg = []
        dk_ctx = jnp.zeros((ctx_len, LANES), F32)
        for hh in heads:
            g0 = jnp.concatenate([gf_first[hh], gb_last[hh]], axis=0).astype(BF16)
            dkcc = _dot_nt(vc[hh], g0)
            dkcf = dkcc[:, 0:LANES] * tc[hh]["k_f"]
            dkcb = dkcc[:, LANES:2 * LANES] * tc[hh]["k_b"]
            dlgf = sums[2 * hh] + total((ctx_len - 1.0 - tc[hh]["i"]) * (kc[hh] * dkcf))
            dlgb = sums[2 * hh + 1] + total(tc[hh]["i"] * (kc[hh] * dkcb))
            dk_ctx = dk_ctx + (dkcf + dkcb) * (DK ** -0.5)
            dv_ref[seq:lext, vs[hh]] = _dot(kc_cat[hh], g0).astype(BF16)
            dlg += [jnp.sum(jnp.sum(a, axis=1, keepdims=True), axis=0, keepdims=True) for a in (dlgf, dlgb)]
        dk_ref[seq:lext, :] = dk_ctx.astype(BF16)
        dq_ref[seq:lext, :] = jnp.zeros((ctx_len, LANES), BF16)

        lane8 = lax.broadcasted_iota(jnp.int32, (8, LANES), 1)
        out = jnp.zeros((8, LANES), F32)
        for n, val in enumerate(dlg):
            out = jnp.where(lane8 == n, val, out)
        dlg_ref[0] = out

    qk0 = CB_QK * D // LANES
    q_spec = pl.BlockSpec((lext, LANES), lambda g: (0, qk0 + g))
    k_spec = pl.BlockSpec((lext, LANES), lambda g: (0, qk0 + N_HEADS // 2 + g))
    v_spec = pl.BlockSpec((lext, 2 * DV), lambda g: (0, CB_V * D // (2 * DV) + g))
    table = pl.BlockSpec((lext, LANES), lambda g: (0, 0))
    state = pltpu.VMEM((2, n_chunks, LANES, DV), F32)
    return pl.pallas_call(
        body, name="retention_backward", grid=(N_HEADS // 2,),
        in_specs=[pl.BlockSpec(memory_space=pltpu.SMEM), q_spec, k_spec, v_spec,
                  pl.BlockSpec((seq, 2 * DV), lambda g: (0, g)), table, table, _AFTER_SPEC],
        out_specs=[pl.BlockSpec((lext, LANES), lambda g: (0, g)), pl.BlockSpec((lext, LANES), lambda g: (0, g)),
                   pl.BlockSpec((lext, 2 * DV), lambda g: (0, g)), pl.BlockSpec((1, 8, LANES), lambda g: (g, 0, 0))],
        out_shape=[jax.ShapeDtypeStruct((lext, N_HEADS * DK), BF16), jax.ShapeDtypeStruct((lext, N_HEADS * DK), BF16),
                   jax.ShapeDtypeStruct((lext, D), BF16), jax.ShapeDtypeStruct((N_HEADS // 2, 8, LANES), F32)],
        scratch_shapes=[pltpu.VMEM((2, n_chunks, 2 * LANES, DV), F32), pltpu.VMEM((2, n_chunks, 2 * LANES, DV), F32),
                        state, state, state, state],
        compiler_params=_cparams(dimension_semantics=("arbitrary",)),
    )(lg, p_ext, p_ext, p_ext, dret, cos_t, sin_t, after)


class _ColumnWriter:
    def __init__(self, dst_hbm, stage, sems, pieces, n_steps):
        self.dst, self.stage, self.sems, self.pieces, self.n_steps = dst_hbm, stage, sems, pieces, n_steps

    def _copies(self, slot, tile):
        rows = pl.ds(pl.multiple_of(tile * TM, TM), TM)
        return [pltpu.make_async_copy(self.stage.at[slot, :, pl.ds(src, width)], self.dst.at[rows, pl.ds(dst, width)],
                                      self.sems.at[slot, k]) for k, (src, dst, width) in enumerate(self.pieces)]

    def slot_for(self, step):
        slot = step % 2

        @pl.when(step >= 2)
        def _():
            for cp in self._copies(slot, step - 2):
                cp.wait()

        return slot

    def send(self, step, slot):
        for cp in self._copies(slot, step):
            cp.start()

        @pl.when(step == self.n_steps - 1)
        def _():
            if self.n_steps >= 2:
                for cp in self._copies(1 - slot, step - 1):
                    cp.wait()
            for cp in self._copies(slot, step):
                cp.wait()


def _merge(p_ext, rn, rstd, x, target, w_a, w_b, w_out, vecs, seq):
    n_tiles = seq // TM
    hb = TM // HALO
    dp_pieces = [(0, CB_BG * D, D), (D, CB_ZA * D, D), (2 * D, CB_ZB * D, 3 * D)]

    def body(h_ref, bg_ref, cg_ref, za_ref, zb_ref, ga_ref, gb_ref, hp_ref, hn_ref, cp_ref, cn_ref,
             rn_ref, rstd_ref, x_ref, t_ref, wa_ref, wb_ref, wo_ref, vec_ref,
             dx1_ref, dp_hbm, dconv_ref, dret_ref, at_ref, b_ref, part_ref, stage, dp_sems):
        i = pl.program_id(0)
        writer = _ColumnWriter(dp_hbm, stage, dp_sems, dp_pieces, n_tiles)
        slot = writer.slot_for(i)
        dpa_ref = stage.at[slot]
        f = lambda ref: ref[...].astype(F32)
        h, bg, cg, za, zb, ga, gb = f(h_ref), f(bg_ref), f(cg_ref), f(za_ref), f(zb_ref), f(ga_ref), f(gb_ref)
        gx, w0, w1, w2 = vec_ref[0:1, :], vec_ref[1:2, :], vec_ref[2:3, :], vec_ref[3:4, :]
        cb, gnw, fw = vec_ref[4:5, :], vec_ref[5:6, :], vec_ref[6:7, :]
        u = cg * h
        row = lax.broadcasted_iota(jnp.int32, (TM, 1), 0)
        u_prev = (f(cp_ref) * f(hp_ref))[HALO - 1:HALO, :]
        u_next = (f(cn_ref) * f(hn_ref))[0:1, :]
        u_prev = jnp.where(i == 0, 0.0, u_prev)
        u_next = jnp.where(i == n_tiles - 1, 0.0, u_next)
        u_up = jnp.where(row == 0, u_prev, pltpu.roll(u, 1, 0))
        u_dn = jnp.where(row == TM - 1, u_next, pltpu.roll(u, TM - 1, 0))
        conv = w0 * u_up + w1 * u + w2 * u_dn + cb
        sza = _sigmoid(za)
        silu_za = za * sza
        a_act = silu_za * bg * conv
        rn = rn_ref[...]
        szb = _sigmoid(zb)
        silu_zb = zb * szb
        rg = rn * gnw
        b_act = silu_zb * rg
        y_a = _dot(a_act.astype(BF16), wa_ref[...])
        y_b = _dot(b_act.astype(BF16), wb_ref[...])
        sga, sgb = _sigmoid(ga), _sigmoid(gb)
        mix = sga * y_a + sgb * y_b
        y = _dot(mix.astype(BF16), wo_ref[...])
        x1 = x_ref[...] + gx * y
        r1 = lax.rsqrt(jnp.mean(x1 * x1, axis=-1, keepdims=True) + EPS)
        xh1 = x1 * r1
        err = xh1 * fw - t_ref[...]
        loss = jnp.sum(jnp.sum(err * err, axis=1, keepdims=True), axis=0, keepdims=True) * (0.5 / D)
        dxh = err * (fw * (1.0 / D))
        dx1 = r1 * (dxh - xh1 * jnp.mean(dxh * xh1, axis=-1, keepdims=True))
        dx1_ref[...] = dx1
        dy = (dx1 * gx).astype(BF16)
        dmix = _dot_nt(dy, wo_ref[...])
        dya_f, dyb_f = dmix * sga, dmix * sgb
        dya, dyb = dya_f.astype(BF16), dyb_f.astype(BF16)
        da = _dot_nt(dya, wa_ref[...])
        db = _dot_nt(dyb, wb_ref[...])
        da_silu = da * silu_za
        dpa_ref[:, 0:D] = (da_silu * conv).astype(BF16)
        dpa_ref[:, D:2 * D] = (da * bg * conv * (sza + silu_za * (1.0 - sza))).astype(BF16)
        dpa_ref[:, 2 * D:3 * D] = (db * rg * (szb + silu_zb * (1.0 - szb))).astype(BF16)
        dpa_ref[:, 3 * D:4 * D] = (dya_f * y_a * (1.0 - sga)).astype(BF16)
        dpa_ref[:, 4 * D:5 * D] = (dyb_f * y_b * (1.0 - sgb)).astype(BF16)
        dconv_ref[...] = (da_silu * bg).astype(BF16)
        drn_n = db * silu_zb
        drn = drn_n * gnw
        rstd = rstd_ref[...]
        for hd in range(N_HEADS):
            sl = slice(hd * DV, (hd + 1) * DV)
            dh_, rh = drn[:, sl], rn[:, sl]
            m1 = jnp.mean(dh_, axis=-1, keepdims=True)
            m2 = jnp.mean(dh_ * rh, axis=-1, keepdims=True)
            dret_ref[:, sl] = (rstd[:, sl] * (dh_ - m1 - rh * m2)).astype(BF16)
        at_ref[0] = a_act.T.astype(BF16)
        at_ref[1] = b_act.T.astype(BF16)
        at_ref[2] = mix.T.astype(BF16)
        b_ref[0] = dya
        b_ref[1] = dyb
        b_ref[2] = dy

        @pl.when(i == 0)
        def _():
            part_ref[...] = jnp.zeros((8, D), F32)

        part_ref[0:1, :] += jnp.sum(err * xh1, axis=0, keepdims=True) * (1.0 / D)
        part_ref[1:2, :] += jnp.sum(dx1 * y, axis=0, keepdims=True)
        part_ref[2:3, :] += jnp.sum(drn_n * rn, axis=0, keepdims=True)
        part_ref[3:4, :] += jnp.broadcast_to(loss, (1, D))
        writer.send(i, slot)

    col = lambda cb_: pl.BlockSpec((TM, D), lambda i, cb_=cb_: (i, cb_))
    prev = lambda cb_: pl.BlockSpec((HALO, D), lambda i, cb_=cb_: (jnp.maximum(i * hb - 1, 0), cb_))
    nxt = lambda cb_: pl.BlockSpec((HALO, D), lambda i, cb_=cb_: (jnp.minimum((i + 1) * hb, n_tiles * hb - 1), cb_))
    tile = pl.BlockSpec((TM, D), lambda i: (i, 0))
    full = lambda a: pl.BlockSpec(a.shape, lambda i: (0,) * a.ndim, pipeline_mode=pl.Buffered(1))
    return pl.pallas_call(
        body, name="merge", grid=(n_tiles,),
        in_specs=[col(CB_H), col(CB_BG), col(CB_CG), col(CB_ZA), col(CB_ZB), col(CB_GA), col(CB_GB),
                  prev(CB_H), nxt(CB_H), prev(CB_CG), nxt(CB_CG),
                  tile, tile, tile, tile, full(w_a), full(w_b), full(w_out), full(vecs)],
        out_specs=[tile, pl.BlockSpec(memory_space=pl.ANY), tile, tile,
                   pl.BlockSpec((3, D, TM), lambda i: (0, 0, i)), pl.BlockSpec((3, TM, D), lambda i: (0, i, 0)),
                   pl.BlockSpec((8, D), lambda i: (0, 0))],
        out_shape=[jax.ShapeDtypeStruct((seq, D), F32), jax.ShapeDtypeStruct((p_ext.shape[0], PW), BF16),
                   jax.ShapeDtypeStruct((seq, D), BF16), jax.ShapeDtypeStruct((seq, D), BF16),
                   jax.ShapeDtypeStruct((3, D, seq), BF16), jax.ShapeDtypeStruct((3, seq, D), BF16),
                   jax.ShapeDtypeStruct((8, D), F32)],
        scratch_shapes=[pltpu.VMEM((2, TM, 5 * D), BF16), pltpu.SemaphoreType.DMA((2, len(dp_pieces)))],
        compiler_params=_cparams(dimension_semantics=("arbitrary",)),
    )(p_ext, p_ext, p_ext, p_ext, p_ext, p_ext, p_ext, p_ext, p_ext, p_ext, p_ext,
      rn, rstd, x, target, w_a, w_b, w_out, vecs)


def _conv_backward(p_ext, dconv, dp, dq, dk, dv, vecs, seq, after):
    n_tiles = seq // TM
    hb = TM // HALO
    qk = N_HEADS * DK
    pieces = [(0, CB_H * D, D), (D, CB_CG * D, D), (2 * D, CB_QK * D, 2 * D)]
    zero_pieces = [(CB_BG * D, D), (CB_ZA * D, D), (CB_ZB * D, 3 * D)]

    def body(h_ref, cg_ref, dc_ref, dcp_ref, dcn_ref, dq_ref, dk_ref, dv_ref, vec_ref, dp_in, after_ref, dp_hbm, part_ref,
             stage, sems, zeros, zero_sems):
        del dp_in, after_ref
        i = pl.program_id(0)
        writer = _ColumnWriter(dp_hbm, stage, sems, pieces, n_tiles + 1)
        slot = writer.slot_for(i)
        out = stage.at[slot]
        out[:, 2 * D:2 * D + qk] = dq_ref[...]
        out[:, 2 * D + qk:3 * D] = dk_ref[...]
        out[:, 3 * D:4 * D] = dv_ref[...]

        @pl.when(i == 0)
        def _():
            part_ref[...] = jnp.zeros((8, D), F32)

        @pl.when(i == n_tiles)
        def _():
            out[:, 0:2 * D] = jnp.zeros((TM, 2 * D), BF16)
            zeros[...] = jnp.zeros(zeros.shape, BF16)
            rows = pl.ds(n_tiles * TM, TM)
            fills = [pltpu.make_async_copy(zeros.at[:, pl.ds(0, width)], dp_hbm.at[rows, pl.ds(dst, width)],
                                           zero_sems.at[k]) for k, (dst, width) in enumerate(zero_pieces)]
            for cp in fills:
                cp.start()
            for cp in fills:
                cp.wait()

        @pl.when(i < n_tiles)
        def _():
            f = lambda ref: ref[...].astype(F32)
            h, cg, dc = f(h_ref), f(cg_ref), f(dc_ref)
            w0, w1, w2 = vec_ref[1:2, :], vec_ref[2:3, :], vec_ref[3:4, :]
            row = lax.broadcasted_iota(jnp.int32, (TM, 1), 0)
            dc_prev = jnp.where(i == 0, 0.0, f(dcp_ref)[HALO - 1:HALO, :])
            dc_next = jnp.where(i == n_tiles - 1, 0.0, f(dcn_ref)[0:1, :])
            dc_up = jnp.where(row == 0, dc_prev, pltpu.roll(dc, 1, 0))
            dc_dn = jnp.where(row == TM - 1, dc_next, pltpu.roll(dc, TM - 1, 0))
            du = w0 * dc_dn + w1 * dc + w2 * dc_up
            u = cg * h
            part_ref[0:1, :] += jnp.sum(u * dc_dn, axis=0, keepdims=True)
            part_ref[1:2, :] += jnp.sum(u * dc, axis=0, keepdims=True)
            part_ref[2:3, :] += jnp.sum(u * dc_up, axis=0, keepdims=True)
            part_ref[3:4, :] += jnp.sum(dc, axis=0, keepdims=True)
            out[:, 0:D] = (du * cg).astype(BF16)
            out[:, D:2 * D] = (du * h).astype(BF16)

        writer.send(i, slot)

    last = n_tiles - 1
    col = lambda cb_: pl.BlockSpec((TM, D), lambda i, cb_=cb_: (jnp.minimum(i, last), cb_))
    lat = lambda w: pl.BlockSpec((TM, w), lambda i: (jnp.minimum(i, last), 0))
    ext = lambda w: pl.BlockSpec((TM, w), lambda i: (i, 0))
    anyspec = pl.BlockSpec(memory_space=pl.ANY)
    return pl.pallas_call(
        body, name="conv_backward", grid=(n_tiles + 1,),
        in_specs=[col(CB_H), col(CB_CG), lat(D),
                  pl.BlockSpec((HALO, D), lambda i: (jnp.clip(i * hb - 1, 0, n_tiles * hb - 1), 0)),
                  pl.BlockSpec((HALO, D), lambda i: (jnp.minimum((i + 1) * hb, n_tiles * hb - 1), 0)),
                  ext(qk), ext(qk), ext(D), pl.BlockSpec(vecs.shape, lambda i: (0, 0)), anyspec, _AFTER_SPEC],
        out_specs=[anyspec, pl.BlockSpec((8, D), lambda i: (0, 0))],
        out_shape=[jax.ShapeDtypeStruct(dp.shape, BF16), jax.ShapeDtypeStruct((8, D), F32)],
        input_output_aliases={9: 0},
        scratch_shapes=[pltpu.VMEM((2, TM, 4 * D), BF16), pltpu.SemaphoreType.DMA((2, len(pieces))),
                        pltpu.VMEM((TM, 3 * D), BF16), pltpu.SemaphoreType.DMA((len(zero_pieces),))],
        compiler_params=_cparams(dimension_semantics=("arbitrary",)),
    )(p_ext, p_ext, dconv, dconv, dconv, dq, dk, dv, vecs, dp, after)


def _input_backward(dp, shards, ids, x, ctx, dx1, modx, modc, norm_w, after):
    seq = x.shape[0]
    lext = seq + ctx.shape[0]
    n_x = seq // TM
    n_w = len(shards.arrays)

    def body(ids_ref, dp_ref, x_ref, ctx_ref, dx1_ref, mx_ref, mc_ref, nw_ref, *rest):
        w_hbm, (_, gx_ref, part_ref, w_ref, w_sems) = rest[:n_w], rest[n_w:]
        i = pl.program_id(0)
        _load_w_in(i, ids_ref, shards, w_hbm, w_ref, w_sems)
        is_ctx = i >= n_x
        dxm = _dot_nt(dp_ref[...], w_ref[...])
        x = jnp.where(is_ctx, ctx_ref[...], x_ref[...])
        r = lax.rsqrt(jnp.mean(x * x, axis=-1, keepdims=True) + EPS)
        xh = x * r
        nw = nw_ref[...]
        sc = jnp.where(is_ctx, mc_ref[1:2, :], mx_ref[1:2, :])
        dxn = dxm * (1.0 + sc)
        dxh = dxn * nw
        dx = r * (dxh - xh * jnp.mean(dxh * xh, axis=-1, keepdims=True))

        @pl.when(jnp.logical_not(is_ctx))
        def _():
            gx_ref[...] = dx1_ref[...] + dx

        @pl.when(i == 0)
        def _():
            part_ref[...] = jnp.zeros((8, D), F32)

        fx = jnp.where(is_ctx, 0.0, 1.0)
        d_shift = jnp.sum(dxm, axis=0, keepdims=True)
        d_scale = jnp.sum(dxm * (xh * nw), axis=0, keepdims=True)
        part_ref[0:1, :] += fx * d_shift
        part_ref[1:2, :] += fx * d_scale
        part_ref[2:3, :] += jnp.sum(dxn * xh, axis=0, keepdims=True)
        part_ref[3:4, :] += (1.0 - fx) * d_shift
        part_ref[4:5, :] += (1.0 - fx) * d_scale

    lat = lambda w: pl.BlockSpec((TM, w), lambda i, ids_: (jnp.minimum(i, n_x - 1), 0))
    ext = lambda w: pl.BlockSpec((TM, w), lambda i, ids_: (i, 0))
    full = lambda a: pl.BlockSpec(a.shape, lambda i, ids_: (0,) * a.ndim)
    grid_spec = pltpu.PrefetchScalarGridSpec(
        num_scalar_prefetch=1, grid=(lext // TM,),
        in_specs=[ext(PW), lat(D), full(ctx), lat(D), full(modx), full(modc), full(norm_w)]
        + [pl.BlockSpec(memory_space=pl.ANY)] * n_w + [_AFTER_SPEC],
        out_specs=[lat(D), pl.BlockSpec((8, D), lambda i, ids_: (0, 0))],
        scratch_shapes=[pltpu.VMEM((D, PW), BF16), pltpu.SemaphoreType.DMA((N_DEV,))])
    return pl.pallas_call(
        body, name="input_backward", grid_spec=grid_spec,
        out_shape=[jax.ShapeDtypeStruct((seq, D), F32), jax.ShapeDtypeStruct((8, D), F32)],
        compiler_params=_cparams(dimension_semantics=("arbitrary",)),
    )(ids, dp, x, ctx, dx1, modx, modc, norm_w, *shards.arrays, after)


def _weight_grad_in_sibling(xm_t, dp, owners, after):
    lext = xm_t.shape[1]

    def body(own_ref, a_ref, b_ref, after_ref, o_ref):
        o_ref[0] = _dot(a_ref[...], b_ref[...]).astype(BF16)

    grid_spec = pltpu.PrefetchScalarGridSpec(
        num_scalar_prefetch=1, grid=(4,),
        in_specs=[pl.BlockSpec((D, lext), lambda j, own: (0, 0)),
                  pl.BlockSpec((lext, WSH), lambda j, own: (0, own[j])), _AFTER_SPEC],
        out_specs=pl.BlockSpec((1, D, WSH), lambda j, own: (j, 0, 0)))
    return pl.pallas_call(
        body, name="weight_grad_in_sibling", grid_spec=grid_spec,
        out_shape=jax.ShapeDtypeStruct((4, D, WSH), BF16),
        compiler_params=_cparams(dimension_semantics=("arbitrary",)),
    )(owners, xm_t, dp, after)


def _weight_grad_in_own(xm_t, dp, owners, recv):
    lext = xm_t.shape[1]

    def body(own_ref, a_ref, b_ref, r_ref, mine_ref, send_ref):
        j = pl.program_id(0)
        total = _dot(a_ref[...], b_ref[...]) + r_ref[0].astype(F32)

        @pl.when(j == 0)
        def _():
            mine_ref[0] = total

        @pl.when(j > 0)
        def _():
            send_ref[0] = total.astype(BF16)

    grid_spec = pltpu.PrefetchScalarGridSpec(
        num_scalar_prefetch=1, grid=(4,),
        in_specs=[pl.BlockSpec((D, lext), lambda j, own: (0, 0), pipeline_mode=pl.Buffered(1)),
                  pl.BlockSpec((lext, WSH), lambda j, own: (0, own[j])),
                  pl.BlockSpec((1, D, WSH), lambda j, own: (j, 0, 0))],
        out_specs=[pl.BlockSpec((1, D, WSH), lambda j, own: (0, 0, 0)),
                   pl.BlockSpec((1, D, WSH), lambda j, own: (jnp.maximum(j - 1, 0), 0, 0))])
    return pl.pallas_call(
        body, name="weight_grad_in_own", grid_spec=grid_spec,
        out_shape=[jax.ShapeDtypeStruct((1, D, WSH), F32), jax.ShapeDtypeStruct((3, D, WSH), BF16)],
        compiler_params=_cparams(dimension_semantics=("arbitrary",)),
    )(owners, xm_t, dp, recv)


def _weight_grad_square(a_t, b):
    seq = a_t.shape[2]

    def body(a_ref, b_ref, o_ref):
        o_ref[:, 0] = _dot(a_ref[0], b_ref[0]).reshape(N_DEV, RSH, D)

    return pl.pallas_call(
        body, name="weight_grad_square", grid=(3,),
        in_specs=[pl.BlockSpec((1, D, seq), lambda t: (t, 0, 0)), pl.BlockSpec((1, seq, D), lambda t: (t, 0, 0))],
        out_specs=pl.BlockSpec((N_DEV, 1, RSH, D), lambda t: (0, t, 0, 0)),
        out_shape=jax.ShapeDtypeStruct((N_DEV, 3, RSH, D), F32),
        compiler_params=_cparams(dimension_semantics=("arbitrary",)),
    )(a_t, b)


def _rows(total, *parts):
    width = max(a.shape[1] for _, a in parts)
    out = None
    for row, a in parts:
        padded = jnp.pad(a, ((row, total - row - a.shape[0]), (0, width - a.shape[1])))
        out = padded if out is None else out + padded
    return out


def kernel(x, c, ctx, c_ctx, norm_w, ada_w, ada_b, w_in, conv_w, conv_b, decay_logit, gn_w, w_a, w_b, w_out, final_norm_w, loss_target, m_c_ctx, m_norm_w, m_ada_w, m_ada_b, m_w_in, m_conv_w, m_conv_b, m_decay_logit, m_gn_w, m_w_a, m_w_b, m_w_out, m_final_norm_w, v_c_ctx, v_norm_w, v_ada_w, v_ada_b, v_w_in, v_conv_w, v_conv_b, v_decay_logit, v_gn_w, v_w_a, v_w_b, v_w_out, v_final_norm_w):
    xi, yi, ci = _coords()
    me = 4 * xi + 2 * yi + ci
    chip = 2 * xi + yi
    seq, ctx_len = x.shape[1], ctx.shape[1]
    assert seq % TM == 0 and seq % RET_C == 0 and ctx_len == TM and seq % GRID_W == 0
    csh = D // N_DEV

    blk = jnp.pad(c, ((0, 7), (0, 0))) + jnp.pad(conv_w[0], ((1, 4), (0, D - csh)))
    got, _ = _all_gather_small(blk, "gather_cond")
    conv_w_all = got[:, 1:4, 0:csh].transpose(1, 0, 2).reshape(3, D)
    c16 = _rows(16, (0, got[:, 0, :]), (N_DEV, c_ctx[None]))
    ada_b_sh = lax.dynamic_slice(ada_b, (0, me * ADA_SH), (1, ADA_SH))
    mod_sh, act16, lg = _modulation(c16, ada_w[0], ada_b_sh, decay_logit[0])
    mod_all, small_done = _all_gather_small(mod_sh, "gather_mod")
    mod_all = mod_all.transpose(1, 0, 2).reshape(16, 3 * D)
    modx = lax.dynamic_slice(mod_all, (me, 0), (1, 3 * D)).reshape(3, D)
    modc = mod_all[8].reshape(3, D)

    x2, tgt = x[0], loss_target[0]
    wblock = lambda n: jax.ShapeDtypeStruct((n, D, WSH), BF16)
    ids_of = lambda ks: jnp.stack([jnp.bitwise_xor(me, k) for k in ks]).astype(jnp.int32)
    own_shard = w_in[0].astype(BF16)[None]
    sq_shards = [w[0].astype(BF16) for w in (w_a, w_b, w_out)]
    wave1 = _Exchange(
        "gather_in_first", [jnp.broadcast_to(own_shard, (2, D, WSH)), own_shard], [wblock(4)], 3,
        lambda srcs, lands: _forward_plan([(0, 1)])(srcs, lands) + _send_plan((4, 2))(srcs[1:], lands),
        after=small_done)
    xm, xm_t, cos_t, sin_t = _prenorm(x2, ctx[0], modx, modc, norm_w, wave1.token)
    (w_pair, _), _ = wave1.wait(xm_t, 1, through=[0])
    first = _Shards([w_pair], [(0, 0), (0, 1)])
    p_ext = _in_projection(None, xm, cos_t, sin_t, w_pair, [((0, 0), (1, 1))], me, "in_projection_pair")
    _, (w_nbr,) = wave1.wait(p_ext, through=[1, 2])
    wave2 = _Exchange(
        "gather_in_second", [w_nbr, own_shard], [wblock(2)], 3,
        lambda srcs, lands: _forward_plan([(0, 2), (1, 3)])(srcs, lands) + _send_plan((6,))(srcs[1:], lands))
    (w_nbr, _), _ = wave2.wait(wave2.token, 2, through=[0])
    second = _Shards([w_nbr], [(0, 0), (0, 1), (0, 2), (0, 3)])
    p_ext = _in_projection(p_ext, xm, cos_t, sin_t, w_nbr, [((0, 4), (2, 5)), ((1, 2), (3, 3))], me,
                           "in_projection_neighbours")
    _, (w_diag,) = wave2.wait(p_ext, through=[1, 2])
    wave3 = _Exchange(
        "gather_in_third", [w_diag] + sq_shards, [jax.ShapeDtypeStruct((N_DEV, RSH, D), BF16)] * 3, 1 + 3 * (N_DEV - 1),
        lambda srcs, lands: _forward_plan([(0, 1)])(srcs, lands) + _gather_plan(srcs[1:], lands))
    (w_diag, *_), _ = wave3.wait(wave3.token, 1, through=[0])
    third = _Shards([w_diag], [(0, 0), (0, 1)])
    p_ext = _in_projection(p_ext, xm, cos_t, sin_t, w_diag, [((0, 6), (1, 7))], me, "in_projection_diagonal")
    w_shards, w_ids = first + second + third, ids_of((0, 1, 4, 2, 5, 3, 6, 7))

    rn, rstd = _retention_forward(p_ext, lg, seq, ctx_len)
    _, sq_lands = wave3.wait(rstd, through=range(1, 7))
    w_a_all, w_b_all, w_out_all = (
        lax.dynamic_update_slice(land, shard[None], (me, 0, 0)).reshape(D, D) for land, shard in zip(sq_lands, sq_shards))
    vecs = _rows(8, (0, modx[2:3]), (1, conv_w_all), (4, conv_b), (5, gn_w), (6, final_norm_w[None]))
    dx1, dp, dconv, dret, op_at, op_b, part_m = _merge(p_ext, rn, rstd, x2, tgt, w_a_all, w_b_all, w_out_all, vecs, seq)

    j4 = jnp.arange(4, dtype=jnp.int32)
    owners = (2 * jnp.bitwise_xor(chip, j4) + ci).astype(jnp.int32)
    owners_sib = (2 * jnp.bitwise_xor(chip, j4) + (1 - ci)).astype(jnp.int32)
    gw_sq = _weight_grad_square(op_at, op_b).reshape(N_DEV, 3 * RSH, D)
    rs_sq_pair = _Exchange("rs_square_pair", [gw_sq], [jax.ShapeDtypeStruct((4, 3 * RSH, D), F32)], 4,
                           _pair_plan(lambda j, chip_, c_: 2 * jnp.bitwise_xor(chip_, j) + (1 - c_)))
    dq, dk, dv, dlg = _retention_backward(p_ext, dret, lg, cos_t, sin_t, seq, ctx_len, rs_sq_pair.token)
    dp, part_c = _conv_backward(p_ext, dconv, dp, dq, dk, dv, vecs, seq, dlg)
    gw_sib = _weight_grad_in_sibling(xm_t, dp, owners_sib, part_c)
    rs_in_pair = _Exchange("rs_in_pair", [gw_sib], [jax.ShapeDtypeStruct((4, D, WSH), BF16)], 4,
                           _pair_plan(lambda j, chip_, c_: j))
    (gw_sq,), (r1_sq,) = rs_sq_pair.wait(rs_in_pair.token)
    own_sq, send_sq = _pair_sum(gw_sq, r1_sq, owners, "pair_sum_square")
    rs_sq_chips = _Exchange("rs_square_chips", [send_sq], [jax.ShapeDtypeStruct((3, 3 * RSH, D), BF16)], 3, _chips_plan)
    _, (r1_in,) = rs_in_pair.wait(rs_sq_chips.token)
    own_in, send_in = _weight_grad_in_own(xm_t, dp, owners, r1_in)
    rs_in_chips = _Exchange("rs_in_chips", [send_in], [jax.ShapeDtypeStruct((3, D, WSH), BF16)], 3, _chips_plan)
    grad_x, part_i = _input_backward(dp, w_shards, w_ids, x2, ctx[0], dx1, modx, modc, norm_w, rs_in_chips.token)

    dl = dlg[:, 0, 0:4]
    dlg_row = jnp.pad(dl[:, 0::2].reshape(1, N_HEADS), ((0, 0), (0, D - N_HEADS))) + jnp.pad(
        dl[:, 1::2].reshape(1, N_HEADS), ((0, 0), (N_HEADS, D - 2 * N_HEADS)))
    partials = _rows(16, (0, part_m[0:1]), (1, part_i[2:3]), (2, part_c[3:4]), (3, part_m[2:3]), (4, part_c[0:3]),
                     (7, part_i[0:2]), (9, part_m[1:2]), (10, part_i[3:5]), (13, part_m[3:4]), (14, dlg_row))
    results = {}
    gather_part = _Exchange("gather_partials", [partials], [jax.ShapeDtypeStruct((N_DEV, 16, D), F32)],
                            N_DEV - 1, _gather_plan)
    _, (r2_in,) = rs_in_chips.wait(gather_part.token)
    results["w_in"] = [o[None] for o in _adam_sharded(w_in[0], m_w_in[0], v_w_in[0], own_in, r2_in, "adam_w_in")]
    _, (got,) = gather_part.wait(results["w_in"][1])
    got = lax.dynamic_update_slice(got, partials[None], (me, 0, 0))
    tot = _sum_devices(got, "sum_partials")
    dmodc = tot[10:13].reshape(1, 3 * D)
    dmod16 = _rows(16, (0, got[:, 7:10, :].reshape(N_DEV, 3 * D)), (N_DEV, dmodc))
    dmod16 = lax.dynamic_slice(dmod16, (0, me * ADA_SH), (16, ADA_SH))
    dmodc8 = _rows(8, (0, dmod16[8:9]))
    g_ada_w, d_ada_w, nm_ada_w, nv_ada_w, cctx_part = _ada_backward(
        act16.T, dmod16, dmodc8, ada_w[0], m_ada_w[0], v_ada_w[0])
    gather_cctx = _Exchange("gather_cctx", [cctx_part], [jax.ShapeDtypeStruct((N_DEV, 8, D), F32)],
                            N_DEV - 1, _gather_plan)
    _, (r2_sq,) = rs_sq_chips.wait(gather_cctx.token)
    square = _adam_square([(w_a[0], m_w_a[0], v_w_a[0]), (w_b[0], m_w_b[0], v_w_b[0]),
                           (w_out[0], m_w_out[0], v_w_out[0])], own_sq, r2_sq)
    _, (got_cctx,) = gather_cctx.wait(square[0][1])
    dact_ctx = _sum_devices(lax.dynamic_update_slice(got_cctx, cctx_part[None], (me, 0, 0)), "sum_cctx")

    view = {"c_ctx": (1, D), "final_norm_w": (1, D)}
    given = {"c_ctx": (c_ctx, m_c_ctx, v_c_ctx), "norm_w": (norm_w, m_norm_w, v_norm_w),
             "conv_b": (conv_b, m_conv_b, v_conv_b), "gn_w": (gn_w, m_gn_w, v_gn_w),
             "final_norm_w": (final_norm_w, m_final_norm_w, v_final_norm_w), "ada_b": (ada_b, m_ada_b, v_ada_b),
             "decay_logit": (decay_logit, m_decay_logit, v_decay_logit), "conv_w": (conv_w, m_conv_w, v_conv_w)}
    small_in = [tuple(a.reshape(view.get(name, a.shape)) for a in given[name]) for name in SMALL_PARAMS]
    conv_w_grad = lax.dynamic_slice(tot, (4, me * csh), (3, csh))
    small_out = _adam_small(tot, dact_ctx, conv_w_grad, tot[14, 0:2 * N_HEADS].reshape(2, N_HEADS), small_in)
    results.update({name: [o.reshape(given[name][0].shape) for o in outs_]
                    for name, outs_ in zip(SMALL_PARAMS, small_out)})
    for name, outs_ in zip(("w_a", "w_b", "w_out"), square):
        results[name] = [o[None] for o in outs_]
    results["ada_w"] = [o[None] for o in (g_ada_w, d_ada_w, nm_ada_w, nv_ada_w)]

    order = ("c_ctx", "norm_w", "ada_w", "ada_b", "w_in", "conv_w", "conv_b", "decay_logit", "gn_w",
             "w_a", "w_b", "w_out", "final_norm_w")
    outs = [results[name][kind] for kind in range(4) for name in order]
    return (tot[13, 0], grad_x[None], *outs)
```

```python
import math

import jax
import jax.numpy as jnp
from jax import lax
from jax.experimental import pallas as pl
from jax.experimental.pallas import tpu as pltpu

F32 = jnp.float32
BF16 = jnp.bfloat16
MESH = pl.DeviceIdType.MESH
_HBM_SPEC = pl.BlockSpec(memory_space=pltpu.HBM)
_SEM_SPEC = pl.BlockSpec(memory_space=pltpu.SEMAPHORE)
_EFFECT = pltpu.SideEffectType.DATAFLOW_SIDE_EFFECTING

N_DEV = 8
D = 1024
N_HEADS = 8
DK = 64
DV = 128
GRID_W = 64
ROPE_BASE = 10000.0
EPS = 1e-6
PW = 9 * D
WSH = PW // N_DEV
RSH = D // N_DEV
ADA_SH = 3 * D // N_DEV
TM = 256
RET_C = 256
HALO = 16
LANES = 128
VMEM_LIMIT = 60 * 1024 * 1024

ADAM_LR = 0.001
ADAM_B1 = 0.9
ADAM_B2 = 0.999
ADAM_EPS = 1e-08
ADAM_WD = 0.01
ADAM_STEP = 10

CB_H, CB_BG, CB_CG, CB_ZA, CB_QK, CB_V, CB_ZB, CB_GA, CB_GB = range(9)


def _cparams(**kw):
    return pltpu.CompilerParams(vmem_limit_bytes=VMEM_LIMIT, **kw)


def _dot(a, b):
    return jnp.dot(a, b, preferred_element_type=F32)


def _dot_nt(a, b):
    return lax.dot_general(a, b, (((1,), (1,)), ((), ())), preferred_element_type=F32)


def _dot_tn(a, b):
    return lax.dot_general(a, b, (((0,), (0,)), ((), ())), preferred_element_type=F32)


def _sigmoid(z):
    return 0.5 * jnp.tanh(0.5 * z) + 0.5


def _row_tile(rows):
    return TM if rows % TM == 0 else rows


def _coords():
    return lax.axis_index("x"), lax.axis_index("y"), lax.axis_index("c")


def _flip(v, bit):
    return 1 - v if bit else v


def _all_gather_small(blk, name):
    rows, cols = blk.shape

    def body(x_ref, out_ref, done, send_sems, recv_sems, local_sem):
        x, y, c = _coords()
        me = 4 * x + 2 * y + c
        mine = pltpu.make_async_copy(x_ref, out_ref.at[me], local_sem)
        mine.start()

        def copy(k, slot):
            peer = (_flip(x, k & 4), _flip(y, k & 2), _flip(c, k & 1))
            return pltpu.make_async_remote_copy(
                src_ref=x_ref, dst_ref=out_ref.at[slot], send_sem=send_sems.at[k - 1],
                recv_sem=recv_sems.at[k - 1], device_id=peer, device_id_type=MESH)

        for k in range(1, N_DEV):
            copy(k, me).start()
        for k in range(1, N_DEV):
            copy(k, jnp.bitwise_xor(me, k)).wait_recv()
        for k in range(1, N_DEV):
            copy(k, me).wait_send()
        mine.wait()
        done[...] = jnp.zeros((8, LANES), F32)

    vmem = pl.BlockSpec(memory_space=pltpu.VMEM)
    return pl.pallas_call(
        body, name=name,
        out_shape=[jax.ShapeDtypeStruct((N_DEV, rows, cols), blk.dtype), jax.ShapeDtypeStruct((8, LANES), F32)],
        in_specs=[vmem], out_specs=[vmem, vmem],
        scratch_shapes=[pltpu.SemaphoreType.DMA((N_DEV - 1,)), pltpu.SemaphoreType.DMA((N_DEV - 1,)),
                        pltpu.SemaphoreType.DMA],
    )(blk)


class _Absent:
    at = property(lambda self: self)

    def __getitem__(self, index):
        return self


class _Exchange:
    def __init__(self, name, srcs, land_shapes, n_copies, plan, after=None):
        self.name, self.plan, self.n_copies = name, plan, n_copies
        self.n_src, self.n_land = len(srcs), len(land_shapes)
        hbm = lambda a: pltpu.HBM(a.shape, a.dtype)
        n = self.n_src + self.n_land
        lands = [pltpu.with_memory_space_constraint(lax.empty(s.shape, s.dtype), pltpu.HBM) for s in land_shapes]
        srcs = [pltpu.with_memory_space_constraint(s, pltpu.HBM) for s in srcs]
        extra = [] if after is None else [after]

        def body(*refs):
            send_sems, recv_sems = refs[n + len(extra)], refs[n + len(extra) + 1]
            for cp in self._copies(refs, send_sems, recv_sems):
                cp.start()
            refs[-1][...] = jnp.zeros((8, LANES), F32)

        outs = pl.pallas_call(
            body, name=name + "_start",
            out_shape=(pltpu.SemaphoreType.DMA((n_copies,)), pltpu.SemaphoreType.DMA((n_copies,)),
                       *[hbm(a) for a in srcs], *[hbm(a) for a in lands], jax.ShapeDtypeStruct((8, LANES), F32)),
            in_specs=[_HBM_SPEC] * n + [pl.BlockSpec(memory_space=pl.ANY)] * len(extra),
            out_specs=(_SEM_SPEC, _SEM_SPEC, *[_HBM_SPEC] * n, pl.BlockSpec(memory_space=pltpu.VMEM)),
            input_output_aliases={i: 2 + i for i in range(n)},
            compiler_params=pltpu.CompilerParams(has_side_effects=_EFFECT),
        )(*srcs, *lands, *extra)
        self.send_sems, self.recv_sems = outs[0], outs[1]
        self.buffers = list(outs[2:2 + n])
        self.token = outs[-1]
        self.waited = 0

    def _copies(self, refs, send_sems, recv_sems, lo=0, hi=None):
        src_refs, land_refs = refs[:self.n_src], refs[self.n_src:self.n_src + self.n_land]
        planned = list(enumerate(self.plan(src_refs, land_refs)))[lo:hi]
        return [pltpu.make_async_remote_copy(src_ref=s, dst_ref=d, send_sem=send_sems.at[k], recv_sem=recv_sems.at[k],
                                             device_id=dev, device_id_type=MESH) for k, (s, d, dev) in planned]

    def wait(self, after, count=None, through=None):
        n = self.n_src + self.n_land
        lo = self.waited
        hi = self.n_copies if count is None else lo + count
        self.waited = hi
        through = list(range(n)) if through is None else list(through)
        m = len(through)

        def body(*refs):
            held = [_Absent()] * n
            for pos, ref in zip(through, refs[:m]):
                held[pos] = ref
            for cp in self._copies(held, refs[m], refs[m + 1], lo, hi):
                cp.wait_send()
                cp.wait_recv()

        outs = pl.pallas_call(
            body, name=f"{self.name}_wait{lo}" if lo or hi < self.n_copies else self.name + "_wait",
            out_shape=tuple(pltpu.HBM(self.buffers[p].shape, self.buffers[p].dtype) for p in through),
            in_specs=[_HBM_SPEC] * m + [_SEM_SPEC, _SEM_SPEC, pl.BlockSpec(memory_space=pl.ANY)],
            out_specs=tuple([_HBM_SPEC] * m),
            input_output_aliases={i: i for i in range(m)},
            compiler_params=pltpu.CompilerParams(has_side_effects=_EFFECT),
        )(*[self.buffers[p] for p in through], self.send_sems, self.recv_sems, after)
        for pos, out in zip(through, outs):
            self.buffers[pos] = out
        return list(self.buffers[:self.n_src]), list(self.buffers[self.n_src:])


def _pair_plan(src_index):
    def plan(srcs, lands):
        x, y, c = _coords()
        chip = 2 * x + y
        return [(srcs[0].at[src_index(j, chip, c)], lands[0].at[j], (x, y, 1 - c)) for j in range(4)]
    return plan


def _chips_plan(srcs, lands):
    x, y, c = _coords()
    return [(srcs[0].at[j - 1], lands[0].at[j - 1], (_flip(x, j & 2), _flip(y, j & 1), c)) for j in range(1, 4)]


def _peer(k):
    x, y, c = _coords()
    return _flip(x, k & 4), _flip(y, k & 2), _flip(c, k & 1)


def _send_plan(ks):
    def plan(srcs, lands):
        return [(srcs[0].at[0], lands[0].at[p], _peer(k)) for p, k in enumerate(ks)]
    return plan


def _forward_plan(moves):
    def plan(srcs, lands):
        return [(srcs[0].at[s], srcs[0].at[d], _peer(1)) for s, d in moves]
    return plan


def _gather_plan(srcs, lands):
    x, y, c = _coords()
    me = 4 * x + 2 * y + c
    return [(srcs[a], lands[a].at[me], (_flip(x, k & 4), _flip(y, k & 2), _flip(c, k & 1)))
            for a in range(len(srcs)) for k in range(1, N_DEV)]


def _pair_sum(grad, recv, owners, name):
    _, rows, cols = grad.shape
    tr = rows

    def body(own_ref, g_ref, r_ref, mine_ref, send_ref):
        j = pl.program_id(1)
        total = g_ref[...] + r_ref[...]

        @pl.when(j == 0)
        def _():
            mine_ref[...] = total

        @pl.when(j > 0)
        def _():
            send_ref[...] = total.astype(BF16)

    grid_spec = pltpu.PrefetchScalarGridSpec(
        num_scalar_prefetch=1, grid=(rows // tr, 4),
        in_specs=[pl.BlockSpec((1, tr, cols), lambda i, j, own: (own[j], i, 0)),
                  pl.BlockSpec((1, tr, cols), lambda i, j, own: (j, i, 0))],
        out_specs=[pl.BlockSpec((1, tr, cols), lambda i, j, own: (0, i, 0)),
                   pl.BlockSpec((1, tr, cols), lambda i, j, own: (jnp.maximum(j - 1, 0), i, 0))])
    return pl.pallas_call(
        body, name=name, grid_spec=grid_spec,
        out_shape=[jax.ShapeDtypeStruct((1, rows, cols), F32), jax.ShapeDtypeStruct((3, rows, cols), BF16)],
        compiler_params=_cparams(dimension_semantics=("arbitrary", "arbitrary")),
    )(owners, grad, recv)


def _modulation(c16, ada_w_sh, ada_b_sh, decay_logit):
    def body(c_ref, w_ref, b_ref, dl_ref, mod_ref, act_ref, lg_ref):
        cv = c_ref[...]
        act = cv * _sigmoid(cv)
        act_ref[...] = act
        mod_ref[...] = jnp.dot(act, w_ref[...], preferred_element_type=F32,
                               precision=lax.Precision.HIGHEST) + b_ref[...]
        z = dl_ref[...]
        lg_ref[...] = jnp.minimum(z, 0.0) - jnp.log(1.0 + jnp.exp(-jnp.abs(z)))

    return pl.pallas_call(
        body, name="modulation",
        out_shape=[jax.ShapeDtypeStruct((16, ADA_SH), F32), jax.ShapeDtypeStruct((16, D), F32),
                   jax.ShapeDtypeStruct(decay_logit.shape, F32)],
        compiler_params=_cparams(),
    )(c16, ada_w_sh, ada_b_sh, decay_logit)


def _adam_update(w, g, m, v):
    m2 = ADAM_B1 * m + (1.0 - ADAM_B1) * g
    v2 = ADAM_B2 * v + (1.0 - ADAM_B2) * (g * g)
    m_hat = m2 / (1.0 - ADAM_B1 ** ADAM_STEP)
    v_hat = v2 / (1.0 - ADAM_B2 ** ADAM_STEP)
    delta = -ADAM_LR * (m_hat / (jnp.sqrt(v_hat) + ADAM_EPS) + ADAM_WD * w)
    return delta, m2, v2


def _adam_sharded(w, m, v, own, recv, name):
    rows, cols = w.shape
    tr = _row_tile(rows)

    def body(w_ref, m_ref, v_ref, g0, g1, g2, g3, g_ref, d_ref, m_out, v_out):
        g = ((g0[0] + g1[0].astype(F32)) + g2[0].astype(F32)) + g3[0].astype(F32)
        delta, m2, v2 = _adam_update(w_ref[...], g, m_ref[...], v_ref[...])
        g_ref[...] = g
        d_ref[...] = delta
        m_out[...] = m2
        v_out[...] = v2

    flat = pl.BlockSpec((tr, cols), lambda i: (i, 0))
    part = lambda j: pl.BlockSpec((1, tr, cols), lambda i, j=j: (j, i, 0))
    return pl.pallas_call(
        body, name=name, grid=(rows // tr,),
        in_specs=[flat, flat, flat, part(0), part(0), part(1), part(2)],
        out_specs=[flat] * 4,
        out_shape=[jax.ShapeDtypeStruct((rows, cols), F32)] * 4,
        compiler_params=_cparams(dimension_semantics=("arbitrary",)),
    )(w, m, v, own, recv, recv, recv)


def _ada_backward(act_t, dmod16, dmodc8, ada_w_sh, m, v):
    def body(at_ref, dm_ref, dc_ref, w_ref, m_ref, v_ref, g_ref, d_ref, m_out, v_out, pc_ref):
        g = jnp.dot(at_ref[...], dm_ref[...], preferred_element_type=F32, precision=lax.Precision.HIGHEST)
        w = w_ref[...]
        delta, m2, v2 = _adam_update(w, g, m_ref[...], v_ref[...])
        g_ref[...] = g
        d_ref[...] = delta
        m_out[...] = m2
        v_out[...] = v2
        pc_ref[...] = lax.dot_general(dc_ref[...], w, (((1,), (1,)), ((), ())), preferred_element_type=F32,
                                      precision=lax.Precision.HIGHEST)

    return pl.pallas_call(
        body, name="ada_backward",
        out_shape=[jax.ShapeDtypeStruct((D, ADA_SH), F32)] * 4 + [jax.ShapeDtypeStruct((8, D), F32)],
        compiler_params=_cparams(),
    )(act_t, dmod16, dmodc8, ada_w_sh, m, v)


def _sum_devices(gathered, name):
    _, rows, cols = gathered.shape

    def body(g_ref, o_ref):
        acc = g_ref[0]
        for d in range(1, N_DEV):
            acc = acc + g_ref[d]
        o_ref[...] = acc

    return pl.pallas_call(body, name=name, out_shape=jax.ShapeDtypeStruct((rows, cols), F32),
                          compiler_params=_cparams())(gathered)


SMALL_PARAMS = ("c_ctx", "norm_w", "conv_b", "gn_w", "final_norm_w", "ada_b", "decay_logit", "conv_w")


def _adam_small(tot, dact_ctx, conv_w_grad, decay_grad, params):
    n = len(SMALL_PARAMS)

    def body(tot_ref, dact_ref, cwg_ref, dlg_ref, *refs):
        ins, outs = refs[:3 * n], refs[3 * n:]
        raw = {
            "c_ctx": dact_ref[0:1, :],
            "norm_w": tot_ref[1:2, :], "conv_b": tot_ref[2:3, :], "gn_w": tot_ref[3:4, :],
            "final_norm_w": tot_ref[0:1, :],
            "ada_b": jnp.concatenate([tot_ref[7 + r:8 + r, :] + tot_ref[10 + r:11 + r, :] for r in range(3)], axis=1),
            "decay_logit": dlg_ref[...],
            "conv_w": cwg_ref[...],
        }
        for k, name in enumerate(SMALL_PARAMS):
            lead = (0,) if len(ins[3 * k].shape) == 3 else ()
            w, m, v = (ref[lead + (...,)] for ref in ins[3 * k:3 * k + 3])
            g = raw[name]
            if name == "c_ctx":
                s = _sigmoid(w)
                g = g * (s * (1.0 + w * (1.0 - s)))
            elif name == "decay_logit":
                g = g * (1.0 - _sigmoid(w))
            delta, m2, v2 = _adam_update(w, g, m, v)
            for ref, val in zip(outs[4 * k:4 * k + 4], (g, delta, m2, v2)):
                ref[lead + (...,)] = val

    flat = [a for trio in params for a in trio]
    out_shape = [jax.ShapeDtypeStruct(trio[0].shape, F32) for trio in params for _ in range(4)]
    outs = pl.pallas_call(body, name="adam_small", out_shape=out_shape, compiler_params=_cparams())(
        tot, dact_ctx, conv_w_grad, decay_grad, *flat)
    return [outs[4 * k:4 * k + 4] for k in range(n)]


def _adam_square(params, own, recv):
    def body(own_ref, recv_ref, *refs):
        ins, outs = refs[:9], refs[9:]
        for k in range(3):
            rows = slice(k * RSH, (k + 1) * RSH)
            g = own_ref[0, rows, :]
            for j in range(3):
                g = g + recv_ref[j, rows, :].astype(F32)
            delta, m2, v2 = _adam_update(ins[3 * k][...], g, ins[3 * k + 1][...], ins[3 * k + 2][...])
            for ref, val in zip(outs[4 * k:4 * k + 4], (g, delta, m2, v2)):
                ref[...] = val

    flat = [a for trio in params for a in trio]
    outs = pl.pallas_call(body, name="adam_square", out_shape=[jax.ShapeDtypeStruct((RSH, D), F32)] * 12,
                          compiler_params=_cparams())(own, recv, *flat)
    return [outs[4 * k:4 * k + 4] for k in range(3)]


def _rope(t, cos, sin):
    lane = lax.broadcasted_iota(jnp.int32, (1, LANES), 1)
    first_half = jnp.bitwise_and(lane, DK // 2) == 0
    partner = jnp.where(first_half, pltpu.roll(t, LANES - DK // 2, 1), pltpu.roll(t, DK // 2, 1))
    return t * cos + partner * sin


class _Shards:
    def __init__(self, arrays, picks):
        self.arrays, self.picks = list(arrays), list(picks)

    def __add__(self, other):
        shift = len(self.arrays)
        return _Shards(self.arrays + other.arrays, self.picks + [(a + shift, s) for a, s in other.picks])


def _load_w_in(step, ids_ref, shards, w_refs, w_vmem, sems):
    @pl.when(step == 0)
    def _():
        copies = []
        for p, (a, slot) in enumerate(shards.picks):
            col = pl.multiple_of(ids_ref[p] * WSH, LANES)
            copies.append(pltpu.make_async_copy(w_refs[a].at[slot], w_vmem.at[:, pl.ds(col, WSH)], sems.at[p]))
        for cp in copies:
            cp.start()
        for cp in copies:
            cp.wait()


def _project_columns(xmb, w_ref, shard, cos, sin, p_ref):
    width = 2 * WSH
    q_lo, k_lo, k_hi = CB_QK * D, CB_QK * D + N_HEADS * DK, CB_V * D
    q_pair, k_pair = q_lo // width, k_lo // width
    assert (q_pair + 1) * width == k_lo and k_pair * width == k_lo and k_hi <= (k_pair + 1) * width
    mxu_cols = 2 * LANES

    def columns(rot_lo, rot_hi, scale):
        for a in range(0, width, mxu_cols):
            acc = _dot(xmb, w_ref[:, a:a + mxu_cols])
            for col in range(a, a + mxu_cols, LANES):
                piece = acc[:, col - a:col - a + LANES]
                if rot_lo <= col < rot_hi:
                    piece = _rope(piece if scale == 1.0 else piece * scale, cos, sin)
                p_ref[:, col:col + LANES] = piece.astype(BF16)

    @pl.when(shard == q_pair)
    def _():
        columns(q_lo - q_pair * width, width, 1.0)

    @pl.when(shard == k_pair)
    def _():
        columns(0, k_hi - k_lo, DK ** -0.5)

    @pl.when(jnp.logical_and(shard != q_pair, shard != k_pair))
    def _():
        columns(0, 0, 1.0)


_AFTER_SPEC = pl.BlockSpec(memory_space=pl.ANY)


def _prenorm(x, ctx, modx, modc, norm_w, after):
    tp = 4 * TM
    seq, ctx_len = x.shape[0], ctx.shape[0]
    n_x_tiles = seq // tp
    lext = seq + ctx_len
    assert seq % tp == 0 and ctx_len <= tp and tp % GRID_W == 0 and tp // GRID_W <= 16
    grid_rows = tp // GRID_W

    def body(x_ref, ctx_ref, mx_ref, mc_ref, nw_ref, after_ref, xm_ref, xmt_ref, cos_ref, sin_ref, col_cos, col_sin):
        del after_ref
        i = pl.program_id(0)
        is_ctx = i >= n_x_tiles
        ctx_rows = jnp.concatenate([ctx_ref[...], jnp.zeros((tp - ctx_len, D), F32)], axis=0)
        x = jnp.where(is_ctx, ctx_rows, x_ref[...])
        r = lax.rsqrt(jnp.mean(x * x, axis=-1, keepdims=True) + EPS)
        sh = jnp.where(is_ctx, mc_ref[0:1, :], mx_ref[0:1, :])
        sc = jnp.where(is_ctx, mc_ref[1:2, :], mx_ref[1:2, :])
        xm = (x * r * nw_ref[...]) * (1.0 + sc) + sh
        xm_ref[...] = xm.astype(BF16)
        xmt_ref[...] = xm.T.astype(BF16)
        lane = lax.broadcasted_iota(jnp.int32, (1, LANES), 1)
        n_freq = DK // 4
        inv = jnp.exp((lane % n_freq).astype(F32) * (-math.log(ROPE_BASE) / n_freq))
        by_column = (lane % (2 * n_freq)) >= n_freq
        sign = jnp.where((lane % DK) < DK // 2, -1.0, 1.0)

        @pl.when(i == 0)
        def _():
            col = (lax.broadcasted_iota(jnp.int32, (tp, 1), 0) % GRID_W).astype(F32) * inv
            col_cos[...] = jnp.cos(col)
            col_sin[...] = jnp.sin(col) * sign

        row = (i * grid_rows + lax.broadcasted_iota(jnp.int32, (16, 1), 0)).astype(F32) * inv
        row_cos, row_sin = jnp.cos(row), jnp.sin(row) * sign
        tall = lambda t: jnp.concatenate(
            [jnp.broadcast_to(t[g:g + 1, :], (GRID_W, LANES)) for g in range(grid_rows)], axis=0)
        cos_ref[...] = jnp.where(is_ctx, 1.0, jnp.where(by_column, col_cos[...], tall(row_cos)))
        sin_ref[...] = jnp.where(is_ctx, 0.0, jnp.where(by_column, col_sin[...], tall(row_sin)))

    full = lambda a: pl.BlockSpec(a.shape, lambda i: (0,) * a.ndim)
    row = lambda width: pl.BlockSpec((tp, width), lambda i: (i, 0))
    return pl.pallas_call(
        body, name="prenorm", grid=(n_x_tiles + 1,),
        in_specs=[pl.BlockSpec((tp, D), lambda i: (jnp.minimum(i, n_x_tiles - 1), 0)), full(ctx),
                  full(modx), full(modc), full(norm_w), _AFTER_SPEC],
        out_specs=[row(D), pl.BlockSpec((D, tp), lambda i: (0, i)), row(LANES), row(LANES)],
        out_shape=[jax.ShapeDtypeStruct((lext, D), BF16), jax.ShapeDtypeStruct((D, lext), BF16),
                   jax.ShapeDtypeStruct((lext, LANES), F32), jax.ShapeDtypeStruct((lext, LANES), F32)],
        scratch_shapes=[pltpu.VMEM((tp, LANES), F32), pltpu.VMEM((tp, LANES), F32)],
        compiler_params=_cparams(dimension_semantics=("arbitrary",)),
    )(x, ctx, modx, modc, norm_w, after)


def _in_projection(p_ext, xm, cos_t, sin_t, w, pairs, me, name):
    lext = xm.shape[0]
    tr = lext // 4
    meta = []
    for (slot_a, k_a), (slot_b, k_b) in pairs:
        assert k_a ^ k_b == 1
        dev_a = jnp.bitwise_xor(me, k_a)
        a_low = dev_a % 2 == 0
        meta += [jnp.where(a_low, slot_a, slot_b), jnp.where(a_low, slot_b, slot_a), dev_a // 2]
    meta = jnp.stack(meta).astype(jnp.int32)

    def body(meta_ref, xm_ref, cos_ref, sin_ref, w_hbm, *rest):
        p_ref, w_ref, sems = rest[-3:]
        j, i = pl.program_id(0), pl.program_id(1)

        @pl.when(i == 0)
        def _():
            copies = [pltpu.make_async_copy(w_hbm.at[meta_ref[3 * j + h]], w_ref.at[:, h * WSH:(h + 1) * WSH],
                                            sems.at[h]) for h in range(2)]
            for cp in copies:
                cp.start()
            for cp in copies:
                cp.wait()

        _project_columns(xm_ref[...], w_ref, meta_ref[3 * j + 2], cos_ref[...], sin_ref[...], p_ref)

    row = lambda width: pl.BlockSpec((tr, width), lambda j, i, meta_: (i, 0))
    grid_spec = pltpu.PrefetchScalarGridSpec(
        num_scalar_prefetch=1, grid=(len(pairs), lext // tr),
        in_specs=[row(D), row(LANES), row(LANES), pl.BlockSpec(memory_space=pl.ANY)]
        + ([] if p_ext is None else [pl.BlockSpec(memory_space=pl.ANY)]),
        out_specs=pl.BlockSpec((tr, 2 * WSH), lambda j, i, meta_: (i, meta_[3 * j + 2])),
        scratch_shapes=[pltpu.VMEM((D, 2 * WSH), BF16), pltpu.SemaphoreType.DMA((2,))])
    return pl.pallas_call(
        body, name=name, grid_spec=grid_spec,
        out_shape=jax.ShapeDtypeStruct((lext, PW), BF16),
        input_output_aliases={} if p_ext is None else {5: 0},
        compiler_params=_cparams(dimension_semantics=("arbitrary", "arbitrary")),
    )(meta, xm, cos_t, sin_t, w, *([] if p_ext is None else [p_ext]))


def _decay_tables(lgf, lgb, n):
    i = lax.broadcasted_iota(jnp.int32, (n, 1), 0).astype(F32)
    return dict(i=i, k_f=jnp.exp(lgf * (n - 1.0 - i)), k_b=jnp.exp(lgb * i),
                q_f=jnp.exp(lgf * (i + 1.0)), q_b=jnp.exp(lgb * (n - i)))


def _decay_matrix(lgf, lgb, n):
    ii = lax.broadcasted_iota(jnp.int32, (n, n), 0)
    jj = lax.broadcasted_iota(jnp.int32, (n, n), 1)
    diff = (ii - jj).astype(F32)
    low = jnp.exp(lgf * jnp.maximum(diff, 0.0))
    up = jnp.exp(lgb * jnp.maximum(-diff, 0.0))
    return jnp.where(diff > 0, low, jnp.where(diff < 0, up, 2.0)), diff


def _cat_lanes(a, b):
    return jnp.concatenate([a.astype(BF16), b.astype(BF16)], axis=1)


def _retention_forward(p_ext, lg, seq, ctx_len):
    lext = seq + ctx_len
    n_chunks = seq // RET_C
    C = RET_C

    def body(lg_ref, q_ref, k_ref, v_ref, rn_ref, rstd_ref, kv_scr, sf_scr, sb_scr):
        pair = pl.program_id(0)
        lane = lax.broadcasted_iota(jnp.int32, (1, LANES), 1)
        heads = range(2)
        hmask = [(lane // DK == hh).astype(F32) for hh in heads]
        lgf = [lg_ref[0, 2 * pair + hh] for hh in heads]
        lgb = [lg_ref[1, 2 * pair + hh] for hh in heads]
        vs = [slice(hh * DV, (hh + 1) * DV) for hh in heads]
        t = [_decay_tables(lgf[hh], lgb[hh], C) for hh in heads]
        kc_all = k_ref[seq:lext, :].astype(F32)
        s0 = []
        for hh in heads:
            tc = _decay_tables(lgf[hh], lgb[hh], ctx_len)
            kc = kc_all * hmask[hh]
            s0.append(_dot_tn(_cat_lanes(kc * tc["k_f"], kc * tc["k_b"]), v_ref[seq:lext, vs[hh]]))

        def increments(c, carry):
            rows = pl.ds(pl.multiple_of(c * C, C), C)
            k_all = k_ref[rows, :].astype(F32)
            for hh in heads:
                k = k_all * hmask[hh]
                kv_scr[hh, c] = _dot_tn(_cat_lanes(k * t[hh]["k_f"], k * t[hh]["k_b"]), v_ref[rows, vs[hh]])
            return carry

        lax.fori_loop(0, n_chunks, increments, 0, unroll=2)
        gf_c = [jnp.exp(lgf[hh] * C) for hh in heads]
        gb_c = [jnp.exp(lgb[hh] * C) for hh in heads]

        def scan_f(c, s):
            for hh in heads:
                sf_scr[hh, c] = s[hh]
            return tuple(gf_c[hh] * s[hh] + kv_scr[hh, c, 0:LANES, :] for hh in heads)

        def scan_b(n, s):
            c = n_chunks - 1 - n
            for hh in heads:
                sb_scr[hh, c] = s[hh]
            return tuple(gb_c[hh] * s[hh] + kv_scr[hh, c, LANES:2 * LANES, :] for hh in heads)

        lax.fori_loop(0, n_chunks, scan_f, tuple(s0[hh][0:LANES] for hh in heads))
        lax.fori_loop(0, n_chunks, scan_b, tuple(s0[hh][LANES:2 * LANES] for hh in heads))
        dmat = [_decay_matrix(lgf[hh], lgb[hh], C)[0] for hh in heads]

        def outputs(c, carry):
            rows = pl.ds(pl.multiple_of(c * C, C), C)
            q_all = q_ref[rows, :].astype(F32)
            k = k_ref[rows, :]
            for hh in heads:
                q = q_all * hmask[hh]
                v = v_ref[rows, vs[hh]]
                s = _dot_nt(q.astype(BF16), k)
                inner = _dot((s * dmat[hh]).astype(BF16), v)
                states = jnp.concatenate([sf_scr[hh, c], sb_scr[hh, c]], axis=0).astype(BF16)
                ret = inner + _dot(_cat_lanes(q * t[hh]["q_f"], q * t[hh]["q_b"]), states)
                mu = jnp.mean(ret, axis=-1, keepdims=True)
                cen = ret - mu
                rstd = lax.rsqrt(jnp.mean(cen * cen, axis=-1, keepdims=True) + EPS)
                rn_ref[rows, vs[hh]] = cen * rstd
                rstd_ref[rows, vs[hh]] = jnp.broadcast_to(rstd, (C, DV))
            return carry

        lax.fori_loop(0, n_chunks, outputs, 0, unroll=2)

    qk0 = CB_QK * D // LANES
    return pl.pallas_call(
        body, name="retention_forward", grid=(N_HEADS // 2,),
        in_specs=[pl.BlockSpec(memory_space=pltpu.SMEM),
                  pl.BlockSpec((lext, LANES), lambda g: (0, qk0 + g)),
                  pl.BlockSpec((lext, LANES), lambda g: (0, qk0 + N_HEADS // 2 + g)),
                  pl.BlockSpec((lext, 2 * DV), lambda g: (0, CB_V * D // (2 * DV) + g))],
        out_specs=[pl.BlockSpec((seq, 2 * DV), lambda g: (0, g))] * 2,
        out_shape=[jax.ShapeDtypeStruct((seq, D), F32)] * 2,
        scratch_shapes=[pltpu.VMEM((2, n_chunks, 2 * LANES, DV), F32), pltpu.VMEM((2, n_chunks, LANES, DV), F32),
                        pltpu.VMEM((2, n_chunks, LANES, DV), F32)],
        compiler_params=_cparams(dimension_semantics=("arbitrary",)),
    )(lg, p_ext, p_ext, p_ext)


def _retention_backward(p_ext, dret, lg, cos_t, sin_t, seq, ctx_len, after):
    lext = seq + ctx_len
    n_chunks = seq // RET_C
    C = RET_C

    def body(lg_ref, q_ref, k_ref, v_ref, do_ref, cos_ref, sin_ref, after_ref, dq_ref, dk_ref, dv_ref, dlg_ref,
             kv_scr, g_scr, sf_scr, sb_scr, gfn_scr, gbp_scr):
        pair = pl.program_id(0)
        lane = lax.broadcasted_iota(jnp.int32, (1, LANES), 1)
        heads = range(2)
        hmask = [(lane // DK == hh).astype(F32) for hh in heads]
        lgf = [lg_ref[0, 2 * pair + hh] for hh in heads]
        lgb = [lg_ref[1, 2 * pair + hh] for hh in heads]
        vs = [slice(hh * DV, (hh + 1) * DV) for hh in heads]
        t = [_decay_tables(lgf[hh], lgb[hh], C) for hh in heads]
        tc = [_decay_tables(lgf[hh], lgb[hh], ctx_len) for hh in heads]
        kc_all = k_ref[seq:lext, :].astype(F32)
        kc = [kc_all * hmask[hh] for hh in heads]
        vc = [v_ref[seq:lext, vs[hh]] for hh in heads]
        kc_cat = [_cat_lanes(kc[hh] * tc[hh]["k_f"], kc[hh] * tc[hh]["k_b"]) for hh in heads]
        s0 = [_dot_tn(kc_cat[hh], vc[hh]) for hh in heads]

        def increments(c, carry):
            rows = pl.ds(pl.multiple_of(c * C, C), C)
            k_all = k_ref[rows, :].astype(F32)
            q_all = q_ref[rows, :].astype(F32)
            for hh in heads:
                k, q = k_all * hmask[hh], q_all * hmask[hh]
                kv_scr[hh, c] = _dot_tn(_cat_lanes(k * t[hh]["k_f"], k * t[hh]["k_b"]), v_ref[rows, vs[hh]])
                g_scr[hh, c] = _dot_tn(_cat_lanes(q * t[hh]["q_f"], q * t[hh]["q_b"]), do_ref[rows, vs[hh]])
            return carry

        lax.fori_loop(0, n_chunks, increments, 0, unroll=2)
        gf_c = [jnp.exp(lgf[hh] * C) for hh in heads]
        gb_c = [jnp.exp(lgb[hh] * C) for hh in heads]

        def scan_f(c, s):
            for hh in heads:
                sf_scr[hh, c] = s[hh]
            return tuple(gf_c[hh] * s[hh] + kv_scr[hh, c, 0:LANES, :] for hh in heads)

        def scan_b(n, s):
            c = n_chunks - 1 - n
            for hh in heads:
                sb_scr[hh, c] = s[hh]
            return tuple(gb_c[hh] * s[hh] + kv_scr[hh, c, LANES:2 * LANES, :] for hh in heads)

        def scan_gf(n, carry):
            c = n_chunks - 1 - n
            for hh in heads:
                gfn_scr[hh, c] = carry[hh]
            return tuple(g_scr[hh, c, 0:LANES, :] + gf_c[hh] * carry[hh] for hh in heads)

        def scan_gb(c, carry):
            for hh in heads:
                gbp_scr[hh, c] = carry[hh]
            return tuple(g_scr[hh, c, LANES:2 * LANES, :] + gb_c[hh] * carry[hh] for hh in heads)

        lax.fori_loop(0, n_chunks, scan_f, tuple(s0[hh][0:LANES] for hh in heads))
        lax.fori_loop(0, n_chunks, scan_b, tuple(s0[hh][LANES:2 * LANES] for hh in heads))
        zero_state = jnp.zeros((LANES, DV), F32)
        gf_first = lax.fori_loop(0, n_chunks, scan_gf, (zero_state, zero_state))
        gb_last = lax.fori_loop(0, n_chunks, scan_gb, (zero_state, zero_state))

        dmat, w_f, w_b = [], [], []
        for hh in heads:
            dm, diff = _decay_matrix(lgf[hh], lgb[hh], C)
            dmat.append(dm)
            w_f.append(jnp.where(diff > 0, diff * dm, 0.0))
            w_b.append(jnp.where(diff < 0, -diff * dm, 0.0))

        def total(a):
            rows_, width = a.shape
            part = jnp.sum(a.reshape(rows_ // 8, 8, width), axis=0)
            return part[:, 0:LANES] + part[:, LANES:2 * LANES] if width == 2 * LANES else part

        def chunk_head(hh, c, rows, q_all, k_all, dlgf, dlgb):
            th = t[hh]
            qm = q_all * hmask[hh]
            km = k_all * hmask[hh]
            qb, kb = qm.astype(BF16), km.astype(BF16)
            v = v_ref[rows, vs[hh]]
            do = do_ref[rows, vs[hh]]
            s = _dot_nt(qb, kb)
            dsd = _dot_nt(do, v)
            ds = (dsd * dmat[hh]).astype(BF16)
            dq_in = _dot(ds, kb)
            dk_in = _dot_tn(ds, qb)
            dv_in = _dot_tn((s * dmat[hh]).astype(BF16), do)
            prod = s * dsd
            dlgf = dlgf + total(prod * w_f[hh])
            dlgb = dlgb + total(prod * w_b[hh])
            sf, sb = sf_scr[hh, c], sb_scr[hh, c]
            states = jnp.concatenate([sf, sb], axis=0).astype(BF16)
            dqc = _dot_nt(do, states)
            dqf = dqc[:, 0:LANES] * th["q_f"]
            dqb = dqc[:, LANES:2 * LANES] * th["q_b"]
            dq = (dq_in + dqf + dqb) * hmask[hh]
            dlgf = dlgf + total((th["i"] + 1.0) * (qm * dqf))
            dlgb = dlgb + total((C - th["i"]) * (qm * dqb))
            gfn, gbp = gfn_scr[hh, c], gbp_scr[hh, c]
            gstates = jnp.concatenate([gfn, gbp], axis=0).astype(BF16)
            dkc = _dot_nt(v, gstates)
            dkf = dkc[:, 0:LANES] * th["k_f"]
            dkb = dkc[:, LANES:2 * LANES] * th["k_b"]
            dk = dk_in + dkf + dkb
            dlgf = dlgf + total((C - 1.0 - th["i"]) * (km * dkf)) + C * gf_c[hh] * total(gfn * sf)
            dlgb = dlgb + total(th["i"] * (km * dkb)) + C * gb_c[hh] * total(gbp * sb)
            dv = dv_in + _dot(_cat_lanes(km * th["k_f"], km * th["k_b"]), gstates)
            dv_ref[rows, vs[hh]] = dv.astype(BF16)
            return dq, dk, dlgf, dlgb

        def chunk(c, carry):
            rows = pl.ds(pl.multiple_of(c * C, C), C)
            q_all = q_ref[rows, :].astype(F32)
            k_all = k_ref[rows, :].astype(F32)
            dq0, dk0, f0, b0 = chunk_head(0, c, rows, q_all, k_all, carry[0], carry[1])
            dq1, dk1, f1, b1 = chunk_head(1, c, rows, q_all, k_all, carry[2], carry[3])
            cos, sin = cos_ref[rows, :], sin_ref[rows, :]
            dq_ref[rows, :] = _rope(dq0 + dq1, cos, -sin).astype(BF16)
            dk_ref[rows, :] = (_rope(dk0 + dk1, cos, -sin) * (DK ** -0.5)).astype(BF16)
            return f0, b0, f1, b1

        zero = jnp.zeros((8, LANES), F32)
        sums = lax.fori_loop(0, n_chunks, chunk, (zero, zero, zero, zero), unroll=2)

        dlg = []
        dk_ctx = jnp.zeros((ctx_len, LANES), F32)
        for hh in heads:
            g0 = jnp.concatenate([gf_first[hh], gb_last[hh]], axis=0).astype(BF16)
            dkcc = _dot_nt(vc[hh], g0)
            dkcf = dkcc[:, 0:LANES] * tc[hh]["k_f"]
            dkcb = dkcc[:, LANES:2 * LANES] * tc[hh]["k_b"]
            dlgf = sums[2 * hh] + total((ctx_len - 1.0 - tc[hh]["i"]) * (kc[hh] * dkcf))
            dlgb = sums[2 * hh + 1] + total(tc[hh]["i"] * (kc[hh] * dkcb))
            dk_ctx = dk_ctx + (dkcf + dkcb) * (DK ** -0.5)
            dv_ref[seq:lext, vs[hh]] = _dot(kc_cat[hh], g0).astype(BF16)
            dlg += [jnp.sum(jnp.sum(a, axis=1, keepdims=True), axis=0, keepdims=True) for a in (dlgf, dlgb)]
        dk_ref[seq:lext, :] = dk_ctx.astype(BF16)
        dq_ref[seq:lext, :] = jnp.zeros((ctx_len, LANES), BF16)

        lane8 = lax.broadcasted_iota(jnp.int32, (8, LANES), 1)
        out = jnp.zeros((8, LANES), F32)
        for n, val in enumerate(dlg):
            out = jnp.where(lane8 == n, val, out)
        dlg_ref[0] = out

    qk0 = CB_QK * D // LANES
    q_spec = pl.BlockSpec((lext, LANES), lambda g: (0, qk0 + g))
    k_spec = pl.BlockSpec((lext, LANES), lambda g: (0, qk0 + N_HEADS // 2 + g))
    v_spec = pl.BlockSpec((lext, 2 * DV), lambda g: (0, CB_V * D // (2 * DV) + g))
    table = pl.BlockSpec((lext, LANES), lambda g: (0, 0))
    state = pltpu.VMEM((2, n_chunks, LANES, DV), F32)
    return pl.pallas_call(
        body, name="retention_backward", grid=(N_HEADS // 2,),
        in_specs=[pl.BlockSpec(memory_space=pltpu.SMEM), q_spec, k_spec, v_spec,
                  pl.BlockSpec((seq, 2 * DV), lambda g: (0, g)), table, table, _AFTER_SPEC],
        out_specs=[pl.BlockSpec((lext, LANES), lambda g: (0, g)), pl.BlockSpec((lext, LANES), lambda g: (0, g)),
                   pl.BlockSpec((lext, 2 * DV), lambda g: (0, g)), pl.BlockSpec((1, 8, LANES), lambda g: (g, 0, 0))],
        out_shape=[jax.ShapeDtypeStruct((lext, N_HEADS * DK), BF16), jax.ShapeDtypeStruct((lext, N_HEADS * DK), BF16),
                   jax.ShapeDtypeStruct((lext, D), BF16), jax.ShapeDtypeStruct((N_HEADS // 2, 8, LANES), F32)],
        scratch_shapes=[pltpu.VMEM((2, n_chunks, 2 * LANES, DV), F32), pltpu.VMEM((2, n_chunks, 2 * LANES, DV), F32),
                        state, state, state, state],
        compiler_params=_cparams(dimension_semantics=("arbitrary",)),
    )(lg, p_ext, p_ext, p_ext, dret, cos_t, sin_t, after)


class _ColumnWriter:
    def __init__(self, dst_hbm, stage, sems, pieces, n_steps):
        self.dst, self.stage, self.sems, self.pieces, self.n_steps = dst_hbm, stage, sems, pieces, n_steps

    def _copies(self, slot, tile):
        rows = pl.ds(pl.multiple_of(tile * TM, TM), TM)
        return [pltpu.make_async_copy(self.stage.at[slot, :, pl.ds(src, width)], self.dst.at[rows, pl.ds(dst, width)],
                                      self.sems.at[slot, k]) for k, (src, dst, width) in enumerate(self.pieces)]

    def slot_for(self, step):
        slot = step % 2

        @pl.when(step >= 2)
        def _():
            for cp in self._copies(slot, step - 2):
                cp.wait()

        return slot

    def send(self, step, slot):
        for cp in self._copies(slot, step):
            cp.start()

        @pl.when(step == self.n_steps - 1)
        def _():
            if self.n_steps >= 2:
                for cp in self._copies(1 - slot, step - 1):
                    cp.wait()
            for cp in self._copies(slot, step):
                cp.wait()


def _merge(p_ext, rn, rstd, x, target, w_a, w_b, w_out, vecs, seq):
    n_tiles = seq // TM
    hb = TM // HALO
    dp_pieces = [(0, CB_BG * D, D), (D, CB_ZA * D, D), (2 * D, CB_ZB * D, 3 * D)]

    def body(h_ref, bg_ref, cg_ref, za_ref, zb_ref, ga_ref, gb_ref, hp_ref, hn_ref, cp_ref, cn_ref,
             rn_ref, rstd_ref, x_ref, t_ref, wa_ref, wb_ref, wo_ref, vec_ref,
             dx1_ref, dp_hbm, dconv_ref, dret_ref, at_ref, b_ref, part_ref, stage, dp_sems):
        i = pl.program_id(0)
        writer = _ColumnWriter(dp_hbm, stage, dp_sems, dp_pieces, n_tiles)
        slot = writer.slot_for(i)
        dpa_ref = stage.at[slot]
        f = lambda ref: ref[...].astype(F32)
        h, bg, cg, za, zb, ga, gb = f(h_ref), f(bg_ref), f(cg_ref), f(za_ref), f(zb_ref), f(ga_ref), f(gb_ref)
        gx, w0, w1, w2 = vec_ref[0:1, :], vec_ref[1:2, :], vec_ref[2:3, :], vec_ref[3:4, :]
        cb, gnw, fw = vec_ref[4:5, :], vec_ref[5:6, :], vec_ref[6:7, :]
        u = cg * h
        row = lax.broadcasted_iota(jnp.int32, (TM, 1), 0)
        u_prev = (f(cp_ref) * f(hp_ref))[HALO - 1:HALO, :]
        u_next = (f(cn_ref) * f(hn_ref))[0:1, :]
        u_prev = jnp.where(i == 0, 0.0, u_prev)
        u_next = jnp.where(i == n_tiles - 1, 0.0, u_next)
        u_up = jnp.where(row == 0, u_prev, pltpu.roll(u, 1, 0))
        u_dn = jnp.where(row == TM - 1, u_next, pltpu.roll(u, TM - 1, 0))
        conv = w0 * u_up + w1 * u + w2 * u_dn + cb
        sza = _sigmoid(za)
        silu_za = za * sza
        a_act = silu_za * bg * conv
        rn = rn_ref[...]
        szb = _sigmoid(zb)
        silu_zb = zb * szb
        rg = rn * gnw
        b_act = silu_zb * rg
        y_a = _dot(a_act.astype(BF16), wa_ref[...])
        y_b = _dot(b_act.astype(BF16), wb_ref[...])
        sga, sgb = _sigmoid(ga), _sigmoid(gb)
        mix = sga * y_a + sgb * y_b
        y = _dot(mix.astype(BF16), wo_ref[...])
        x1 = x_ref[...] + gx * y
        r1 = lax.rsqrt(jnp.mean(x1 * x1, axis=-1, keepdims=True) + EPS)
        xh1 = x1 * r1
        err = xh1 * fw - t_ref[...]
        loss = jnp.sum(jnp.sum(err * err, axis=1, keepdims=True), axis=0, keepdims=True) * (0.5 / D)
        dxh = err * (fw * (1.0 / D))
        dx1 = r1 * (dxh - xh1 * jnp.mean(dxh * xh1, axis=-1, keepdims=True))
        dx1_ref[...] = dx1
        dy = (dx1 * gx).astype(BF16)
        dmix = _dot_nt(dy, wo_ref[...])
        dya_f, dyb_f = dmix * sga, dmix * sgb
        dya, dyb = dya_f.astype(BF16), dyb_f.astype(BF16)
        da = _dot_nt(dya, wa_ref[...])
        db = _dot_nt(dyb, wb_ref[...])
        da_silu = da * silu_za
        dpa_ref[:, 0:D] = (da_silu * conv).astype(BF16)
        dpa_ref[:, D:2 * D] = (da * bg * conv * (sza + silu_za * (1.0 - sza))).astype(BF16)
        dpa_ref[:, 2 * D:3 * D] = (db * rg * (szb + silu_zb * (1.0 - szb))).astype(BF16)
        dpa_ref[:, 3 * D:4 * D] = (dya_f * y_a * (1.0 - sga)).astype(BF16)
        dpa_ref[:, 4 * D:5 * D] = (dyb_f * y_b * (1.0 - sgb)).astype(BF16)
        dconv_ref[...] = (da_silu * bg).astype(BF16)
        drn_n = db * silu_zb
        drn = drn_n * gnw
        rstd = rstd_ref[...]
        for hd in range(N_HEADS):
            sl = slice(hd * DV, (hd + 1) * DV)
            dh_, rh = drn[:, sl], rn[:, sl]
            m1 = jnp.mean(dh_, axis=-1, keepdims=True)
            m2 = jnp.mean(dh_ * rh, axis=-1, keepdims=True)
            dret_ref[:, sl] = (rstd[:, sl] * (dh_ - m1 - rh * m2)).astype(BF16)
        at_ref[0] = a_act.T.astype(BF16)
        at_ref[1] = b_act.T.astype(BF16)
        at_ref[2] = mix.T.astype(BF16)
        b_ref[0] = dya
        b_ref[1] = dyb
        b_ref[2] = dy

        @pl.when(i == 0)
        def _():
            part_ref[...] = jnp.zeros((8, D), F32)

        part_ref[0:1, :] += jnp.sum(err * xh1, axis=0, keepdims=True) * (1.0 / D)
        part_ref[1:2, :] += jnp.sum(dx1 * y, axis=0, keepdims=True)
        part_ref[2:3, :] += jnp.sum(drn_n * rn, axis=0, keepdims=True)
        part_ref[3:4, :] += jnp.broadcast_to(loss, (1, D))
        writer.send(i, slot)

    col = lambda cb_: pl.BlockSpec((TM, D), lambda i, cb_=cb_: (i, cb_))
    prev = lambda cb_: pl.BlockSpec((HALO, D), lambda i, cb_=cb_: (jnp.maximum(i * hb - 1, 0), cb_))
    nxt = lambda cb_: pl.BlockSpec((HALO, D), lambda i, cb_=cb_: (jnp.minimum((i + 1) * hb, n_tiles * hb - 1), cb_))
    tile = pl.BlockSpec((TM, D), lambda i: (i, 0))
    full = lambda a: pl.BlockSpec(a.shape, lambda i: (0,) * a.ndim, pipeline_mode=pl.Buffered(1))
    return pl.pallas_call(
        body, name="merge", grid=(n_tiles,),
        in_specs=[col(CB_H), col(CB_BG), col(CB_CG), col(CB_ZA), col(CB_ZB), col(CB_GA), col(CB_GB),
                  prev(CB_H), nxt(CB_H), prev(CB_CG), nxt(CB_CG),
                  tile, tile, tile, tile, full(w_a), full(w_b), full(w_out), full(vecs)],
        out_specs=[tile, pl.BlockSpec(memory_space=pl.ANY), tile, tile,
                   pl.BlockSpec((3, D, TM), lambda i: (0, 0, i)), pl.BlockSpec((3, TM, D), lambda i: (0, i, 0)),
                   pl.BlockSpec((8, D), lambda i: (0, 0))],
        out_shape=[jax.ShapeDtypeStruct((seq, D), F32), jax.ShapeDtypeStruct((p_ext.shape[0], PW), BF16),
                   jax.ShapeDtypeStruct((seq, D), BF16), jax.ShapeDtypeStruct((seq, D), BF16),
                   jax.ShapeDtypeStruct((3, D, seq), BF16), jax.ShapeDtypeStruct((3, seq, D), BF16),
                   jax.ShapeDtypeStruct((8, D), F32)],
        scratch_shapes=[pltpu.VMEM((2, TM, 5 * D), BF16), pltpu.SemaphoreType.DMA((2, len(dp_pieces)))],
        compiler_params=_cparams(dimension_semantics=("arbitrary",)),
    )(p_ext, p_ext, p_ext, p_ext, p_ext, p_ext, p_ext, p_ext, p_ext, p_ext, p_ext,
      rn, rstd, x, target, w_a, w_b, w_out, vecs)


def _conv_backward(p_ext, dconv, dp, dq, dk, dv, vecs, seq, after):
    n_tiles = seq // TM
    hb = TM // HALO
    qk = N_HEADS * DK
    pieces = [(0, CB_H * D, D), (D, CB_CG * D, D), (2 * D, CB_QK * D, 2 * D)]
    zero_pieces = [(CB_BG * D, D), (CB_ZA * D, D), (CB_ZB * D, 3 * D)]

    def body(h_ref, cg_ref, dc_ref, dcp_ref, dcn_ref, dq_ref, dk_ref, dv_ref, vec_ref, dp_in, after_ref, dp_hbm, part_ref,
             stage, sems, zeros, zero_sems):
        del dp_in, after_ref
        i = pl.program_id(0)
        writer = _ColumnWriter(dp_hbm, stage, sems, pieces, n_tiles + 1)
        slot = writer.slot_for(i)
        out = stage.at[slot]
        out[:, 2 * D:2 * D + qk] = dq_ref[...]
        out[:, 2 * D + qk:3 * D] = dk_ref[...]
        out[:, 3 * D:4 * D] = dv_ref[...]

        @pl.when(i == 0)
        def _():
            part_ref[...] = jnp.zeros((8, D), F32)

        @pl.when(i == n_tiles)
        def _():
            out[:, 0:2 * D] = jnp.zeros((TM, 2 * D), BF16)
            zeros[...] = jnp.zeros(zeros.shape, BF16)
            rows = pl.ds(n_tiles * TM, TM)
            fills = [pltpu.make_async_copy(zeros.at[:, pl.ds(0, width)], dp_hbm.at[rows, pl.ds(dst, width)],
                                           zero_sems.at[k]) for k, (dst, width) in enumerate(zero_pieces)]
            for cp in fills:
                cp.start()
            for cp in fills:
                cp.wait()

        @pl.when(i < n_tiles)
        def _():
            f = lambda ref: ref[...].astype(F32)
            h, cg, dc = f(h_ref), f(cg_ref), f(dc_ref)
            w0, w1, w2 = vec_ref[1:2, :], vec_ref[2:3, :], vec_ref[3:4, :]
            row = lax.broadcasted_iota(jnp.int32, (TM, 1), 0)
            dc_prev = jnp.where(i == 0, 0.0, f(dcp_ref)[HALO - 1:HALO, :])
            dc_next = jnp.where(i == n_tiles - 1, 0.0, f(dcn_ref)[0:1, :])
            dc_up = jnp.where(row == 0, dc_prev, pltpu.roll(dc, 1, 0))
            dc_dn = jnp.where(row == TM - 1, dc_next, pltpu.roll(dc, TM - 1, 0))
            du = w0 * dc_dn + w1 * dc + w2 * dc_up
            u = cg * h
            part_ref[0:1, :] += jnp.sum(u * dc_dn, axis=0, keepdims=True)
            part_ref[1:2, :] += jnp.sum(u * dc, axis=0, keepdims=True)
            part_ref[2:3, :] += jnp.sum(u * dc_up, axis=0, keepdims=True)
            part_ref[3:4, :] += jnp.sum(dc, axis=0, keepdims=True)
            out[:, 0:D] = (du * cg).astype(BF16)
            out[:, D:2 * D] = (du * h).astype(BF16)

        writer.send(i, slot)

    last = n_tiles - 1
    col = lambda cb_: pl.BlockSpec((TM, D), lambda i, cb_=cb_: (jnp.minimum(i, last), cb_))
    lat = lambda w: pl.BlockSpec((TM, w), lambda i: (jnp.minimum(i, last), 0))
    ext = lambda w: pl.BlockSpec((TM, w), lambda i: (i, 0))
    anyspec = pl.BlockSpec(memory_space=pl.ANY)
    return pl.pallas_call(
        body, name="conv_backward", grid=(n_tiles + 1,),
        in_specs=[col(CB_H), col(CB_CG), lat(D),
                  pl.BlockSpec((HALO, D), lambda i: (jnp.clip(i * hb - 1, 0, n_tiles * hb - 1), 0)),
                  pl.BlockSpec((HALO, D), lambda i: (jnp.minimum((i + 1) * hb, n_tiles * hb - 1), 0)),
                  ext(qk), ext(qk), ext(D), pl.BlockSpec(vecs.shape, lambda i: (0, 0)), anyspec, _AFTER_SPEC],
        out_specs=[anyspec, pl.BlockSpec((8, D), lambda i: (0, 0))],
        out_shape=[jax.ShapeDtypeStruct(dp.shape, BF16), jax.ShapeDtypeStruct((8, D), F32)],
        input_output_aliases={9: 0},
        scratch_shapes=[pltpu.VMEM((2, TM, 4 * D), BF16), pltpu.SemaphoreType.DMA((2, len(pieces))),
                        pltpu.VMEM((TM, 3 * D), BF16), pltpu.SemaphoreType.DMA((len(zero_pieces),))],
        compiler_params=_cparams(dimension_semantics=("arbitrary",)),
    )(p_ext, p_ext, dconv, dconv, dconv, dq, dk, dv, vecs, dp, after)


def _input_backward(dp, shards, ids, x, ctx, dx1, modx, modc, norm_w, after):
    seq = x.shape[0]
    lext = seq + ctx.shape[0]
    n_x = seq // TM
    n_w = len(shards.arrays)

    def body(ids_ref, dp_ref, x_ref, ctx_ref, dx1_ref, mx_ref, mc_ref, nw_ref, *rest):
        w_hbm, (_, gx_ref, part_ref, w_ref, w_sems) = rest[:n_w], rest[n_w:]
        i = pl.program_id(0)
        _load_w_in(i, ids_ref, shards, w_hbm, w_ref, w_sems)
        is_ctx = i >= n_x
        dxm = _dot_nt(dp_ref[...], w_ref[...])
        x = jnp.where(is_ctx, ctx_ref[...], x_ref[...])
        r = lax.rsqrt(jnp.mean(x * x, axis=-1, keepdims=True) + EPS)
        xh = x * r
        nw = nw_ref[...]
        sc = jnp.where(is_ctx, mc_ref[1:2, :], mx_ref[1:2, :])
        dxn = dxm * (1.0 + sc)
        dxh = dxn * nw
        dx = r * (dxh - xh * jnp.mean(dxh * xh, axis=-1, keepdims=True))

        @pl.when(jnp.logical_not(is_ctx))
        def _():
            gx_ref[...] = dx1_ref[...] + dx

        @pl.when(i == 0)
        def _():
            part_ref[...] = jnp.zeros((8, D), F32)

        fx = jnp.where(is_ctx, 0.0, 1.0)
        d_shift = jnp.sum(dxm, axis=0, keepdims=True)
        d_scale = jnp.sum(dxm * (xh * nw), axis=0, keepdims=True)
        part_ref[0:1, :] += fx * d_shift
        part_ref[1:2, :] += fx * d_scale
        part_ref[2:3, :] += jnp.sum(dxn * xh, axis=0, keepdims=True)
        part_ref[3:4, :] += (1.0 - fx) * d_shift
        part_ref[4:5, :] += (1.0 - fx) * d_scale

    lat = lambda w: pl.BlockSpec((TM, w), lambda i, ids_: (jnp.minimum(i, n_x - 1), 0))
    ext = lambda w: pl.BlockSpec((TM, w), lambda i, ids_: (i, 0))
    full = lambda a: pl.BlockSpec(a.shape, lambda i, ids_: (0,) * a.ndim)
    grid_spec = pltpu.PrefetchScalarGridSpec(
        num_scalar_prefetch=1, grid=(lext // TM,),
        in_specs=[ext(PW), lat(D), full(ctx), lat(D), full(modx), full(modc), full(norm_w)]
        + [pl.BlockSpec(memory_space=pl.ANY)] * n_w + [_AFTER_SPEC],
        out_specs=[lat(D), pl.BlockSpec((8, D), lambda i, ids_: (0, 0))],
        scratch_shapes=[pltpu.VMEM((D, PW), BF16), pltpu.SemaphoreType.DMA((N_DEV,))])
    return pl.pallas_call(
        body, name="input_backward", grid_spec=grid_spec,
        out_shape=[jax.ShapeDtypeStruct((seq, D), F32), jax.ShapeDtypeStruct((8, D), F32)],
        compiler_params=_cparams(dimension_semantics=("arbitrary",)),
    )(ids, dp, x, ctx, dx1, modx, modc, norm_w, *shards.arrays, after)


def _weight_grad_in_sibling(xm_t, dp, owners, after):
    lext = xm_t.shape[1]

    def body(own_ref, a_ref, b_ref, after_ref, o_ref):
        o_ref[0] = _dot(a_ref[...], b_ref[...]).astype(BF16)

    grid_spec = pltpu.PrefetchScalarGridSpec(
        num_scalar_prefetch=1, grid=(4,),
        in_specs=[pl.BlockSpec((D, lext), lambda j, own: (0, 0)),
                  pl.BlockSpec((lext, WSH), lambda j, own: (0, own[j])), _AFTER_SPEC],
        out_specs=pl.BlockSpec((1, D, WSH), lambda j, own: (j, 0, 0)))
    return pl.pallas_call(
        body, name="weight_grad_in_sibling", grid_spec=grid_spec,
        out_shape=jax.ShapeDtypeStruct((4, D, WSH), BF16),
        compiler_params=_cparams(dimension_semantics=("arbitrary",)),
    )(owners, xm_t, dp, after)


def _weight_grad_in_own(xm_t, dp, owners, recv):
    lext = xm_t.shape[1]

    def body(own_ref, a_ref, b_ref, r_ref, mine_ref, send_ref):
        j = pl.program_id(0)
        total = _dot(a_ref[...], b_ref[...]) + r_ref[0].astype(F32)

        @pl.when(j == 0)
        def _():
            mine_ref[0] = total

        @pl.when(j > 0)
        def _():
            send_ref[0] = total.astype(BF16)

    grid_spec = pltpu.PrefetchScalarGridSpec(
        num_scalar_prefetch=1, grid=(4,),
        in_specs=[pl.BlockSpec((D, lext), lambda j, own: (0, 0), pipeline_mode=pl.Buffered(1)),
                  pl.BlockSpec((lext, WSH), lambda j, own: (0, own[j])),
                  pl.BlockSpec((1, D, WSH), lambda j, own: (j, 0, 0))],
        out_specs=[pl.BlockSpec((1, D, WSH), lambda j, own: (0, 0, 0)),
                   pl.BlockSpec((1, D, WSH), lambda j, own: (jnp.maximum(j - 1, 0), 0, 0))])
    return pl.pallas_call(
        body, name="weight_grad_in_own", grid_spec=grid_spec,
        out_shape=[jax.ShapeDtypeStruct((1, D, WSH), F32), jax.ShapeDtypeStruct((3, D, WSH), BF16)],
        compiler_params=_cparams(dimension_semantics=("arbitrary",)),
    )(owners, xm_t, dp, recv)


def _weight_grad_square(a_t, b):
    seq = a_t.shape[2]

    def body(a_ref, b_ref, o_ref):
        o_ref[:, 0] = _dot(a_ref[0], b_ref[0]).reshape(N_DEV, RSH, D)

    return pl.pallas_call(
        body, name="weight_grad_square", grid=(3,),
        in_specs=[pl.BlockSpec((1, D, seq), lambda t: (t, 0, 0)), pl.BlockSpec((1, seq, D), lambda t: (t, 0, 0))],
        out_specs=pl.BlockSpec((N_DEV, 1, RSH, D), lambda t: (0, t, 0, 0)),
        out_shape=jax.ShapeDtypeStruct((N_DEV, 3, RSH, D), F32),
        compiler_params=_cparams(dimension_semantics=("arbitrary",)),
    )(a_t, b)


def _rows(total, *parts):
    width = max(a.shape[1] for _, a in parts)
    out = None
    for row, a in parts:
        padded = jnp.pad(a, ((row, total - row - a.shape[0]), (0, width - a.shape[1])))
        out = padded if out is None else out + padded
    return out


def kernel(x, c, ctx, c_ctx, norm_w, ada_w, ada_b, w_in, conv_w, conv_b, decay_logit, gn_w, w_a, w_b, w_out, final_norm_w, loss_target, m_c_ctx, m_norm_w, m_ada_w, m_ada_b, m_w_in, m_conv_w, m_conv_b, m_decay_logit, m_gn_w, m_w_a, m_w_b, m_w_out, m_final_norm_w, v_c_ctx, v_norm_w, v_ada_w, v_ada_b, v_w_in, v_conv_w, v_conv_b, v_decay_logit, v_gn_w, v_w_a, v_w_b, v_w_out, v_final_norm_w):
    xi, yi, ci = _coords()
    me = 4 * xi + 2 * yi + ci
    chip = 2 * xi + yi
    seq, ctx_len = x.shape[1], ctx.shape[1]
    assert seq % TM == 0 and seq % RET_C == 0 and ctx_len == TM and seq % GRID_W == 0
    csh = D // N_DEV

    blk = jnp.pad(c, ((0, 7), (0, 0))) + jnp.pad(conv_w[0], ((1, 4), (0, D - csh)))
    got, _ = _all_gather_small(blk, "gather_cond")
    conv_w_all = got[:, 1:4, 0:csh].transpose(1, 0, 2).reshape(3, D)
    c16 = _rows(16, (0, got[:, 0, :]), (N_DEV, c_ctx[None]))
    ada_b_sh = lax.dynamic_slice(ada_b, (0, me * ADA_SH), (1, ADA_SH))
    mod_sh, act16, lg = _modulation(c16, ada_w[0], ada_b_sh, decay_logit[0])
    mod_all, small_done = _all_gather_small(mod_sh, "gather_mod")
    mod_all = mod_all.transpose(1, 0, 2).reshape(16, 3 * D)
    modx = lax.dynamic_slice(mod_all, (me, 0), (1, 3 * D)).reshape(3, D)
    modc = mod_all[8].reshape(3, D)

    x2, tgt = x[0], loss_target[0]
    wblock = lambda n: jax.ShapeDtypeStruct((n, D, WSH), BF16)
    ids_of = lambda ks: jnp.stack([jnp.bitwise_xor(me, k) for k in ks]).astype(jnp.int32)
    own_shard = w_in[0].astype(BF16)[None]
    sq_shards = [w[0].astype(BF16) for w in (w_a, w_b, w_out)]
    wave1 = _Exchange(
        "gather_in_first", [jnp.broadcast_to(own_shard, (2, D, WSH)), own_shard], [wblock(4)], 3,
        lambda srcs, lands: _forward_plan([(0, 1)])(srcs, lands) + _send_plan((4, 2))(srcs[1:], lands),
        after=small_done)
    xm, xm_t, cos_t, sin_t = _prenorm(x2, ctx[0], modx, modc, norm_w, wave1.token)
    (w_pair, _), _ = wave1.wait(xm_t, 1, through=[0])
    first = _Shards([w_pair], [(0, 0), (0, 1)])
    p_ext = _in_projection(None, xm, cos_t, sin_t, w_pair, [((0, 0), (1, 1))], me, "in_projection_pair")
    (_, own_shard), (w_nbr,) = wave1.wait(p_ext, through=[1, 2])
    wave2 = _Exchange(
        "gather_in_second", [w_nbr, own_shard], [wblock(2)], 3,
        lambda srcs, lands: _forward_plan([(0, 2), (1, 3)])(srcs, lands) + _send_plan((6,))(srcs[1:], lands))
    (w_nbr, _), _ = wave2.wait(wave2.token, 2, through=[0])
    second = _Shards([w_nbr], [(0, 0), (0, 1), (0, 2), (0, 3)])
    p_ext = _in_projection(p_ext, xm, cos_t, sin_t, w_nbr, [((0, 4), (2, 5)), ((1, 2), (3, 3))], me,
                           "in_projection_neighbours")
    _, (w_diag,) = wave2.wait(p_ext, through=[1, 2])
    wave3 = _Exchange(
        "gather_in_third", [w_diag] + sq_shards, [jax.ShapeDtypeStruct((N_DEV, RSH, D), BF16)] * 3, 1 + 3 * (N_DEV - 1),
        lambda srcs, lands: _forward_plan([(0, 1)])(srcs, lands) + _gather_plan(srcs[1:], lands))
    (w_diag, *_), _ = wave3.wait(wave3.token, 1, through=[0])
    third = _Shards([w_diag], [(0, 0), (0, 1)])
    p_ext = _in_projection(p_ext, xm, cos_t, sin_t, w_diag, [((0, 6), (1, 7))], me, "in_projection_diagonal")
    w_shards, w_ids = first + second + third, ids_of((0, 1, 4, 2, 5, 3, 6, 7))

    rn, rstd = _retention_forward(p_ext, lg, seq, ctx_len)
    _, sq_lands = wave3.wait(rstd, through=range(1, 7))
    w_a_all, w_b_all, w_out_all = (
        lax.dynamic_update_slice(land, shard[None], (me, 0, 0)).reshape(D, D) for land, shard in zip(sq_lands, sq_shards))
    vecs = _rows(8, (0, modx[2:3]), (1, conv_w_all), (4, conv_b), (5, gn_w), (6, final_norm_w[None]))
    dx1, dp, dconv, dret, op_at, op_b, part_m = _merge(p_ext, rn, rstd, x2, tgt, w_a_all, w_b_all, w_out_all, vecs, seq)

    j4 = jnp.arange(4, dtype=jnp.int32)
    owners = (2 * jnp.bitwise_xor(chip, j4) + ci).astype(jnp.int32)
    owners_sib = (2 * jnp.bitwise_xor(chip, j4) + (1 - ci)).astype(jnp.int32)
    gw_sq = _weight_grad_square(op_at, op_b).reshape(N_DEV, 3 * RSH, D)
    rs_sq_pair = _Exchange("rs_square_pair", [gw_sq], [jax.ShapeDtypeStruct((4, 3 * RSH, D), F32)], 4,
                           _pair_plan(lambda j, chip_, c_: 2 * jnp.bitwise_xor(chip_, j) + (1 - c_)))
    dq, dk, dv, dlg = _retention_backward(p_ext, dret, lg, cos_t, sin_t, seq, ctx_len, rs_sq_pair.token)
    dp, part_c = _conv_backward(p_ext, dconv, dp, dq, dk, dv, vecs, seq, dlg)
    gw_sib = _weight_grad_in_sibling(xm_t, dp, owners_sib, part_c)
    rs_in_pair = _Exchange("rs_in_pair", [gw_sib], [jax.ShapeDtypeStruct((4, D, WSH), BF16)], 4,
                           _pair_plan(lambda j, chip_, c_: j))
    (gw_sq,), (r1_sq,) = rs_sq_pair.wait(rs_in_pair.token)
    own_sq, send_sq = _pair_sum(gw_sq, r1_sq, owners, "pair_sum_square")
    rs_sq_chips = _Exchange("rs_square_chips", [send_sq], [jax.ShapeDtypeStruct((3, 3 * RSH, D), BF16)], 3, _chips_plan)
    _, (r1_in,) = rs_in_pair.wait(rs_sq_chips.token)
    own_in, send_in = _weight_grad_in_own(xm_t, dp, owners, r1_in)
    rs_in_chips = _Exchange("rs_in_chips", [send_in], [jax.ShapeDtypeStruct((3, D, WSH), BF16)], 3, _chips_plan)
    grad_x, part_i = _input_backward(dp, w_shards, w_ids, x2, ctx[0], dx1, modx, modc, norm_w, rs_in_chips.token)

    dl = dlg[:, 0, 0:4]
    dlg_row = jnp.pad(dl[:, 0::2].reshape(1, N_HEADS), ((0, 0), (0, D - N_HEADS))) + jnp.pad(
        dl[:, 1::2].reshape(1, N_HEADS), ((0, 0), (N_HEADS, D - 2 * N_HEADS)))
    partials = _rows(16, (0, part_m[0:1]), (1, part_i[2:3]), (2, part_c[3:4]), (3, part_m[2:3]), (4, part_c[0:3]),
                     (7, part_i[0:2]), (9, part_m[1:2]), (10, part_i[3:5]), (13, part_m[3:4]), (14, dlg_row))
    results = {}
    gather_part = _Exchange("gather_partials", [partials], [jax.ShapeDtypeStruct((N_DEV, 16, D), F32)],
                            N_DEV - 1, _gather_plan)
    _, (r2_in,) = rs_in_chips.wait(gather_part.token)
    results["w_in"] = [o[None] for o in _adam_sharded(w_in[0], m_w_in[0], v_w_in[0], own_in, r2_in, "adam_w_in")]
    (partials,), (got,) = gather_part.wait(results["w_in"][1])
    got = lax.dynamic_update_slice(got, partials[None], (me, 0, 0))
    tot = _sum_devices(got, "sum_partials")
    dmodc = tot[10:13].reshape(1, 3 * D)
    dmod16 = _rows(16, (0, got[:, 7:10, :].reshape(N_DEV, 3 * D)), (N_DEV, dmodc))
    dmod16 = lax.dynamic_slice(dmod16, (0, me * ADA_SH), (16, ADA_SH))
    dmodc8 = _rows(8, (0, dmod16[8:9]))
    g_ada_w, d_ada_w, nm_ada_w, nv_ada_w, cctx_part = _ada_backward(
        act16.T, dmod16, dmodc8, ada_w[0], m_ada_w[0], v_ada_w[0])
    gather_cctx = _Exchange("gather_cctx", [cctx_part], [jax.ShapeDtypeStruct((N_DEV, 8, D), F32)],
                            N_DEV - 1, _gather_plan)
    _, (r2_sq,) = rs_sq_chips.wait(gather_cctx.token)
    square = _adam_square([(w_a[0], m_w_a[0], v_w_a[0]), (w_b[0], m_w_b[0], v_w_b[0]),
                           (w_out[0], m_w_out[0], v_w_out[0])], own_sq, r2_sq)
    (cctx_part,), (got_cctx,) = gather_cctx.wait(square[0][1])
    dact_ctx = _sum_devices(lax.dynamic_update_slice(got_cctx, cctx_part[None], (me, 0, 0)), "sum_cctx")

    view = {"c_ctx": (1, D), "final_norm_w": (1, D)}
    given = {"c_ctx": (c_ctx, m_c_ctx, v_c_ctx), "norm_w": (norm_w, m_norm_w, v_norm_w),
             "conv_b": (conv_b, m_conv_b, v_conv_b), "gn_w": (gn_w, m_gn_w, v_gn_w),
             "final_norm_w": (final_norm_w, m_final_norm_w, v_final_norm_w), "ada_b": (ada_b, m_ada_b, v_ada_b),
             "decay_logit": (decay_logit, m_decay_logit, v_decay_logit), "conv_w": (conv_w, m_conv_w, v_conv_w)}
    small_in = [tuple(a.reshape(view.get(name, a.shape)) for a in given[name]) for name in SMALL_PARAMS]
    conv_w_grad = lax.dynamic_slice(tot, (4, me * csh), (3, csh))
    small_out = _adam_small(tot, dact_ctx, conv_w_grad, tot[14, 0:2 * N_HEADS].reshape(2, N_HEADS), small_in)
    results.update({name: [o.reshape(given[name][0].shape) for o in outs_]
                    for name, outs_ in zip(SMALL_PARAMS, small_out)})
    for name, outs_ in zip(("w_a", "w_b", "w_out"), square):
        results[name] = [o[None] for o in outs_]
    results["ada_w"] = [o[None] for o in (g_ada_w, d_ada_w, nm_ada_w, nv_ada_w)]

    order = ("c_ctx", "norm_w", "ada_w", "ada_b", "w_in", "conv_w", "conv_b", "decay_logit", "gn_w",
             "w_a", "w_b", "w_out", "final_norm_w")
    outs = [results[name][kind] for kind in range(4) for name in order]
    return (tot[13, 0], grad_x[None], *outs)
```

```python
import math

import jax
import jax.numpy as jnp
from jax import lax
from jax.experimental import pallas as pl
from jax.experimental.pallas import tpu as pltpu

F32 = jnp.float32
BF16 = jnp.bfloat16
MESH = pl.DeviceIdType.MESH
_HBM_SPEC = pl.BlockSpec(memory_space=pltpu.HBM)
_SEM_SPEC = pl.BlockSpec(memory_space=pltpu.SEMAPHORE)
_EFFECT = pltpu.SideEffectType.DATAFLOW_SIDE_EFFECTING

N_DEV = 8
D = 1024
N_HEADS = 8
DK = 64
DV = 128
GRID_W = 64
ROPE_BASE = 10000.0
EPS = 1e-6
PW = 9 * D
WSH = PW // N_DEV
RSH = D // N_DEV
ADA_SH = 3 * D // N_DEV
TM = 256
RET_C = 256
HALO = 16
LANES = 128
VMEM_LIMIT = 60 * 1024 * 1024

ADAM_LR = 0.001
ADAM_B1 = 0.9
ADAM_B2 = 0.999
ADAM_EPS = 1e-08
ADAM_WD = 0.01
ADAM_STEP = 10

CB_H, CB_BG, CB_CG, CB_ZA, CB_QK, CB_V, CB_ZB, CB_GA, CB_GB = range(9)


def _cparams(**kw):
    return pltpu.CompilerParams(vmem_limit_bytes=VMEM_LIMIT, **kw)


def _dot(a, b):
    return jnp.dot(a, b, preferred_element_type=F32)


def _dot_nt(a, b):
    return lax.dot_general(a, b, (((1,), (1,)), ((), ())), preferred_element_type=F32)


def _dot_tn(a, b):
    return lax.dot_general(a, b, (((0,), (0,)), ((), ())), preferred_element_type=F32)


def _sigmoid(z):
    return 0.5 * jnp.tanh(0.5 * z) + 0.5


def _row_tile(rows):
    return TM if rows % TM == 0 else rows


def _coords():
    return lax.axis_index("x"), lax.axis_index("y"), lax.axis_index("c")


def _flip(v, bit):
    return 1 - v if bit else v


def _all_gather_small(blk, name):
    rows, cols = blk.shape

    def body(x_ref, out_ref, done, send_sems, recv_sems, local_sem):
        x, y, c = _coords()
        me = 4 * x + 2 * y + c
        mine = pltpu.make_async_copy(x_ref, out_ref.at[me], local_sem)
        mine.start()

        def copy(k, slot):
            peer = (_flip(x, k & 4), _flip(y, k & 2), _flip(c, k & 1))
            return pltpu.make_async_remote_copy(
                src_ref=x_ref, dst_ref=out_ref.at[slot], send_sem=send_sems.at[k - 1],
                recv_sem=recv_sems.at[k - 1], device_id=peer, device_id_type=MESH)

        for k in range(1, N_DEV):
            copy(k, me).start()
        for k in range(1, N_DEV):
            copy(k, jnp.bitwise_xor(me, k)).wait_recv()
        for k in range(1, N_DEV):
            copy(k, me).wait_send()
        mine.wait()
        done[...] = jnp.zeros((8, LANES), F32)

    vmem = pl.BlockSpec(memory_space=pltpu.VMEM)
    return pl.pallas_call(
        body, name=name,
        out_shape=[jax.ShapeDtypeStruct((N_DEV, rows, cols), blk.dtype), jax.ShapeDtypeStruct((8, LANES), F32)],
        in_specs=[vmem], out_specs=[vmem, vmem],
        scratch_shapes=[pltpu.SemaphoreType.DMA((N_DEV - 1,)), pltpu.SemaphoreType.DMA((N_DEV - 1,)),
                        pltpu.SemaphoreType.DMA],
    )(blk)


class _Absent:
    at = property(lambda self: self)

    def __getitem__(self, index):
        return self


class _Exchange:
    def __init__(self, name, srcs, land_shapes, n_copies, plan, after=None):
        self.name, self.plan, self.n_copies = name, plan, n_copies
        self.n_src, self.n_land = len(srcs), len(land_shapes)
        hbm = lambda a: pltpu.HBM(a.shape, a.dtype)
        n = self.n_src + self.n_land
        lands = [pltpu.with_memory_space_constraint(lax.empty(s.shape, s.dtype), pltpu.HBM) for s in land_shapes]
        srcs = [pltpu.with_memory_space_constraint(s, pltpu.HBM) for s in srcs]
        extra = [] if after is None else [after]

        def body(*refs):
            send_sems, recv_sems = refs[n + len(extra)], refs[n + len(extra) + 1]
            for cp in self._copies(refs, send_sems, recv_sems):
                cp.start()
            refs[-1][...] = jnp.zeros((8, LANES), F32)

        outs = pl.pallas_call(
            body, name=name + "_start",
            out_shape=(pltpu.SemaphoreType.DMA((n_copies,)), pltpu.SemaphoreType.DMA((n_copies,)),
                       *[hbm(a) for a in srcs], *[hbm(a) for a in lands], jax.ShapeDtypeStruct((8, LANES), F32)),
            in_specs=[_HBM_SPEC] * n + [pl.BlockSpec(memory_space=pl.ANY)] * len(extra),
            out_specs=(_SEM_SPEC, _SEM_SPEC, *[_HBM_SPEC] * n, pl.BlockSpec(memory_space=pltpu.VMEM)),
            input_output_aliases={i: 2 + i for i in range(n)},
            compiler_params=pltpu.CompilerParams(has_side_effects=_EFFECT),
        )(*srcs, *lands, *extra)
        self.send_sems, self.recv_sems = outs[0], outs[1]
        self.buffers = list(outs[2:2 + n])
        self.token = outs[-1]
        self.waited = 0

    def _copies(self, refs, send_sems, recv_sems, lo=0, hi=None):
        src_refs, land_refs = refs[:self.n_src], refs[self.n_src:self.n_src + self.n_land]
        planned = list(enumerate(self.plan(src_refs, land_refs)))[lo:hi]
        return [pltpu.make_async_remote_copy(src_ref=s, dst_ref=d, send_sem=send_sems.at[k], recv_sem=recv_sems.at[k],
                                             device_id=dev, device_id_type=MESH) for k, (s, d, dev) in planned]

    def wait(self, after, count=None, through=None):
        n = self.n_src + self.n_land
        lo = self.waited
        hi = self.n_copies if count is None else lo + count
        self.waited = hi
        through = list(range(n)) if through is None else list(through)
        m = len(through)

        def body(*refs):
            held = [_Absent()] * n
            for pos, ref in zip(through, refs[:m]):
                held[pos] = ref
            for cp in self._copies(held, refs[m], refs[m + 1], lo, hi):
                cp.wait_send()
                cp.wait_recv()

        outs = pl.pallas_call(
            body, name=f"{self.name}_wait{lo}" if lo or hi < self.n_copies else self.name + "_wait",
            out_shape=tuple(pltpu.HBM(self.buffers[p].shape, self.buffers[p].dtype) for p in through),
            in_specs=[_HBM_SPEC] * m + [_SEM_SPEC, _SEM_SPEC, pl.BlockSpec(memory_space=pl.ANY)],
            out_specs=tuple([_HBM_SPEC] * m),
            input_output_aliases={i: i for i in range(m)},
            compiler_params=pltpu.CompilerParams(has_side_effects=_EFFECT),
        )(*[self.buffers[p] for p in through], self.send_sems, self.recv_sems, after)
        for pos, out in zip(through, outs):
            self.buffers[pos] = out
        return list(self.buffers[:self.n_src]), list(self.buffers[self.n_src:])


def _pair_plan(src_index):
    def plan(srcs, lands):
        x, y, c = _coords()
        chip = 2 * x + y
        return [(srcs[0].at[src_index(j, chip, c)], lands[0].at[j], (x, y, 1 - c)) for j in range(4)]
    return plan


def _chips_plan(srcs, lands):
    x, y, c = _coords()
    return [(srcs[0].at[j - 1], lands[0].at[j - 1], (_flip(x, j & 2), _flip(y, j & 1), c)) for j in range(1, 4)]


def _peer(k):
    x, y, c = _coords()
    return _flip(x, k & 4), _flip(y, k & 2), _flip(c, k & 1)


def _send_plan(ks):
    def plan(srcs, lands):
        return [(srcs[0].at[0], lands[0].at[p], _peer(k)) for p, k in enumerate(ks)]
    return plan


def _forward_plan(moves):
    def plan(srcs, lands):
        return [(srcs[0].at[s], srcs[0].at[d], _peer(1)) for s, d in moves]
    return plan


def _gather_plan(srcs, lands):
    x, y, c = _coords()
    me = 4 * x + 2 * y + c
    return [(srcs[a], lands[a].at[me], (_flip(x, k & 4), _flip(y, k & 2), _flip(c, k & 1)))
            for a in range(len(srcs)) for k in range(1, N_DEV)]


def _pair_sum(grad, recv, owners, name):
    _, rows, cols = grad.shape
    tr = rows

    def body(own_ref, g_ref, r_ref, mine_ref, send_ref):
        j = pl.program_id(1)
        total = g_ref[...] + r_ref[...]

        @pl.when(j == 0)
        def _():
            mine_ref[...] = total

        @pl.when(j > 0)
        def _():
            send_ref[...] = total.astype(BF16)

    grid_spec = pltpu.PrefetchScalarGridSpec(
        num_scalar_prefetch=1, grid=(rows // tr, 4),
        in_specs=[pl.BlockSpec((1, tr, cols), lambda i, j, own: (own[j], i, 0)),
                  pl.BlockSpec((1, tr, cols), lambda i, j, own: (j, i, 0))],
        out_specs=[pl.BlockSpec((1, tr, cols), lambda i, j, own: (0, i, 0)),
                   pl.BlockSpec((1, tr, cols), lambda i, j, own: (jnp.maximum(j - 1, 0), i, 0))])
    return pl.pallas_call(
        body, name=name, grid_spec=grid_spec,
        out_shape=[jax.ShapeDtypeStruct((1, rows, cols), F32), jax.ShapeDtypeStruct((3, rows, cols), BF16)],
        compiler_params=_cparams(dimension_semantics=("arbitrary", "arbitrary")),
    )(owners, grad, recv)


def _modulation(c16, ada_w_sh, ada_b_sh, decay_logit):
    def body(c_ref, w_ref, b_ref, dl_ref, mod_ref, act_ref, lg_ref):
        cv = c_ref[...]
        act = cv * _sigmoid(cv)
        act_ref[...] = act
        mod_ref[...] = jnp.dot(act, w_ref[...], preferred_element_type=F32,
                               precision=lax.Precision.HIGHEST) + b_ref[...]
        z = dl_ref[...]
        lg_ref[...] = jnp.minimum(z, 0.0) - jnp.log(1.0 + jnp.exp(-jnp.abs(z)))

    return pl.pallas_call(
        body, name="modulation",
        out_shape=[jax.ShapeDtypeStruct((16, ADA_SH), F32), jax.ShapeDtypeStruct((16, D), F32),
                   jax.ShapeDtypeStruct(decay_logit.shape, F32)],
        compiler_params=_cparams(),
    )(c16, ada_w_sh, ada_b_sh, decay_logit)


def _adam_update(w, g, m, v):
    m2 = ADAM_B1 * m + (1.0 - ADAM_B1) * g
    v2 = ADAM_B2 * v + (1.0 - ADAM_B2) * (g * g)
    m_hat = m2 / (1.0 - ADAM_B1 ** ADAM_STEP)
    v_hat = v2 / (1.0 - ADAM_B2 ** ADAM_STEP)
    delta = -ADAM_LR * (m_hat / (jnp.sqrt(v_hat) + ADAM_EPS) + ADAM_WD * w)
    return delta, m2, v2


def _adam_sharded(w, m, v, own, recv, name):
    rows, cols = w.shape
    tr = _row_tile(rows)

    def body(w_ref, m_ref, v_ref, g0, g1, g2, g3, g_ref, d_ref, m_out, v_out):
        g = ((g0[0] + g1[0].astype(F32)) + g2[0].astype(F32)) + g3[0].astype(F32)
        delta, m2, v2 = _adam_update(w_ref[...], g, m_ref[...], v_ref[...])
        g_ref[...] = g
        d_ref[...] = delta
        m_out[...] = m2
        v_out[...] = v2

    flat = pl.BlockSpec((tr, cols), lambda i: (i, 0))
    part = lambda j: pl.BlockSpec((1, tr, cols), lambda i, j=j: (j, i, 0))
    return pl.pallas_call(
        body, name=name, grid=(rows // tr,),
        in_specs=[flat, flat, flat, part(0), part(0), part(1), part(2)],
        out_specs=[flat] * 4,
        out_shape=[jax.ShapeDtypeStruct((rows, cols), F32)] * 4,
        compiler_params=_cparams(dimension_semantics=("arbitrary",)),
    )(w, m, v, own, recv, recv, recv)


def _ada_backward(act_t, dmod16, dmodc8, ada_w_sh, m, v):
    def body(at_ref, dm_ref, dc_ref, w_ref, m_ref, v_ref, g_ref, d_ref, m_out, v_out, pc_ref):
        g = jnp.dot(at_ref[...], dm_ref[...], preferred_element_type=F32, precision=lax.Precision.HIGHEST)
        w = w_ref[...]
        delta, m2, v2 = _adam_update(w, g, m_ref[...], v_ref[...])
        g_ref[...] = g
        d_ref[...] = delta
        m_out[...] = m2
        v_out[...] = v2
        pc_ref[...] = lax.dot_general(dc_ref[...], w, (((1,), (1,)), ((), ())), preferred_element_type=F32,
                                      precision=lax.Precision.HIGHEST)

    return pl.pallas_call(
        body, name="ada_backward",
        out_shape=[jax.ShapeDtypeStruct((D, ADA_SH), F32)] * 4 + [jax.ShapeDtypeStruct((8, D), F32)],
        compiler_params=_cparams(),
    )(act_t, dmod16, dmodc8, ada_w_sh, m, v)


def _sum_devices(gathered, name):
    _, rows, cols = gathered.shape

    def body(g_ref, o_ref):
        acc = g_ref[0]
        for d in range(1, N_DEV):
            acc = acc + g_ref[d]
        o_ref[...] = acc

    return pl.pallas_call(body, name=name, out_shape=jax.ShapeDtypeStruct((rows, cols), F32),
                          compiler_params=_cparams())(gathered)


SMALL_PARAMS = ("c_ctx", "norm_w", "conv_b", "gn_w", "final_norm_w", "ada_b", "decay_logit", "conv_w")


def _adam_small(tot, dact_ctx, conv_w_grad, decay_grad, params):
    n = len(SMALL_PARAMS)

    def body(tot_ref, dact_ref, cwg_ref, dlg_ref, *refs):
        ins, outs = refs[:3 * n], refs[3 * n:]
        raw = {
            "c_ctx": dact_ref[0:1, :],
            "norm_w": tot_ref[1:2, :], "conv_b": tot_ref[2:3, :], "gn_w": tot_ref[3:4, :],
            "final_norm_w": tot_ref[0:1, :],
            "ada_b": jnp.concatenate([tot_ref[7 + r:8 + r, :] + tot_ref[10 + r:11 + r, :] for r in range(3)], axis=1),
            "decay_logit": dlg_ref[...],
            "conv_w": cwg_ref[...],
        }
        for k, name in enumerate(SMALL_PARAMS):
            lead = (0,) if len(ins[3 * k].shape) == 3 else ()
            w, m, v = (ref[lead + (...,)] for ref in ins[3 * k:3 * k + 3])
            g = raw[name]
            if name == "c_ctx":
                s = _sigmoid(w)
                g = g * (s * (1.0 + w * (1.0 - s)))
            elif name == "decay_logit":
                g = g * (1.0 - _sigmoid(w))
            delta, m2, v2 = _adam_update(w, g, m, v)
            for ref, val in zip(outs[4 * k:4 * k + 4], (g, delta, m2, v2)):
                ref[lead + (...,)] = val

    flat = [a for trio in params for a in trio]
    out_shape = [jax.ShapeDtypeStruct(trio[0].shape, F32) for trio in params for _ in range(4)]
    outs = pl.pallas_call(body, name="adam_small", out_shape=out_shape, compiler_params=_cparams())(
        tot, dact_ctx, conv_w_grad, decay_grad, *flat)
    return [outs[4 * k:4 * k + 4] for k in range(n)]


def _adam_square(params, own, recv):
    def body(own_ref, recv_ref, *refs):
        ins, outs = refs[:9], refs[9:]
        for k in range(3):
            rows = slice(k * RSH, (k + 1) * RSH)
            g = own_ref[0, rows, :]
            for j in range(3):
                g = g + recv_ref[j, rows, :].astype(F32)
            delta, m2, v2 = _adam_update(ins[3 * k][...], g, ins[3 * k + 1][...], ins[3 * k + 2][...])
            for ref, val in zip(outs[4 * k:4 * k + 4], (g, delta, m2, v2)):
                ref[...] = val

    flat = [a for trio in params for a in trio]
    outs = pl.pallas_call(body, name="adam_square", out_shape=[jax.ShapeDtypeStruct((RSH, D), F32)] * 12,
                          compiler_params=_cparams())(own, recv, *flat)
    return [outs[4 * k:4 * k + 4] for k in range(3)]


def _rope(t, cos, sin):
    lane = lax.broadcasted_iota(jnp.int32, (1, LANES), 1)
    first_half = jnp.bitwise_and(lane, DK // 2) == 0
    partner = jnp.where(first_half, pltpu.roll(t, LANES - DK // 2, 1), pltpu.roll(t, DK // 2, 1))
    return t * cos + partner * sin


class _Shards:
    def __init__(self, arrays, picks):
        self.arrays, self.picks = list(arrays), list(picks)

    def __add__(self, other):
        shift = len(self.arrays)
        return _Shards(self.arrays + other.arrays, self.picks + [(a + shift, s) for a, s in other.picks])


def _load_w_in(step, ids_ref, shards, w_refs, w_vmem, sems):
    @pl.when(step == 0)
    def _():
        copies = []
        for p, (a, slot) in enumerate(shards.picks):
            col = pl.multiple_of(ids_ref[p] * WSH, LANES)
            copies.append(pltpu.make_async_copy(w_refs[a].at[slot], w_vmem.at[:, pl.ds(col, WSH)], sems.at[p]))
        for cp in copies:
            cp.start()
        for cp in copies:
            cp.wait()


def _project_columns(xmb, w_ref, shard, cos, sin, p_ref):
    width = 2 * WSH
    q_lo, k_lo, k_hi = CB_QK * D, CB_QK * D + N_HEADS * DK, CB_V * D
    q_pair, k_pair = q_lo // width, k_lo // width
    assert (q_pair + 1) * width == k_lo and k_pair * width == k_lo and k_hi <= (k_pair + 1) * width
    mxu_cols = 2 * LANES

    def columns(rot_lo, rot_hi, scale):
        for a in range(0, width, mxu_cols):
            acc = _dot(xmb, w_ref[:, a:a + mxu_cols])
            for col in range(a, a + mxu_cols, LANES):
                piece = acc[:, col - a:col - a + LANES]
                if rot_lo <= col < rot_hi:
                    piece = _rope(piece if scale == 1.0 else piece * scale, cos, sin)
                p_ref[:, col:col + LANES] = piece.astype(BF16)

    @pl.when(shard == q_pair)
    def _():
        columns(q_lo - q_pair * width, width, 1.0)

    @pl.when(shard == k_pair)
    def _():
        columns(0, k_hi - k_lo, DK ** -0.5)

    @pl.when(jnp.logical_and(shard != q_pair, shard != k_pair))
    def _():
        columns(0, 0, 1.0)


_AFTER_SPEC = pl.BlockSpec(memory_space=pl.ANY)


def _prenorm(x, ctx, modx, modc, norm_w, after):
    tp = 4 * TM
    seq, ctx_len = x.shape[0], ctx.shape[0]
    n_x_tiles = seq // tp
    lext = seq + ctx_len
    assert seq % tp == 0 and ctx_len <= tp and tp % GRID_W == 0 and tp // GRID_W <= 16
    grid_rows = tp // GRID_W

    def body(x_ref, ctx_ref, mx_ref, mc_ref, nw_ref, after_ref, xm_ref, xmt_ref, cos_ref, sin_ref, col_cos, col_sin):
        del after_ref
        i = pl.program_id(0)
        is_ctx = i >= n_x_tiles
        ctx_rows = jnp.concatenate([ctx_ref[...], jnp.zeros((tp - ctx_len, D), F32)], axis=0)
        x = jnp.where(is_ctx, ctx_rows, x_ref[...])
        r = lax.rsqrt(jnp.mean(x * x, axis=-1, keepdims=True) + EPS)
        sh = jnp.where(is_ctx, mc_ref[0:1, :], mx_ref[0:1, :])
        sc = jnp.where(is_ctx, mc_ref[1:2, :], mx_ref[1:2, :])
        xm = (x * r * nw_ref[...]) * (1.0 + sc) + sh
        xm_ref[...] = xm.astype(BF16)
        xmt_ref[...] = xm.T.astype(BF16)
        lane = lax.broadcasted_iota(jnp.int32, (1, LANES), 1)
        n_freq = DK // 4
        inv = jnp.exp((lane % n_freq).astype(F32) * (-math.log(ROPE_BASE) / n_freq))
        by_column = (lane % (2 * n_freq)) >= n_freq
        sign = jnp.where((lane % DK) < DK // 2, -1.0, 1.0)

        @pl.when(i == 0)
        def _():
            col = (lax.broadcasted_iota(jnp.int32, (tp, 1), 0) % GRID_W).astype(F32) * inv
            col_cos[...] = jnp.cos(col)
            col_sin[...] = jnp.sin(col) * sign

        row = (i * grid_rows + lax.broadcasted_iota(jnp.int32, (16, 1), 0)).astype(F32) * inv
        row_cos, row_sin = jnp.cos(row), jnp.sin(row) * sign
        tall = lambda t: jnp.concatenate(
            [jnp.broadcast_to(t[g:g + 1, :], (GRID_W, LANES)) for g in range(grid_rows)], axis=0)
        cos_ref[...] = jnp.where(is_ctx, 1.0, jnp.where(by_column, col_cos[...], tall(row_cos)))
        sin_ref[...] = jnp.where(is_ctx, 0.0, jnp.where(by_column, col_sin[...], tall(row_sin)))

    full = lambda a: pl.BlockSpec(a.shape, lambda i: (0,) * a.ndim)
    row = lambda width: pl.BlockSpec((tp, width), lambda i: (i, 0))
    return pl.pallas_call(
        body, name="prenorm", grid=(n_x_tiles + 1,),
        in_specs=[pl.BlockSpec((tp, D), lambda i: (jnp.minimum(i, n_x_tiles - 1), 0)), full(ctx),
                  full(modx), full(modc), full(norm_w), _AFTER_SPEC],
        out_specs=[row(D), pl.BlockSpec((D, tp), lambda i: (0, i)), row(LANES), row(LANES)],
        out_shape=[jax.ShapeDtypeStruct((lext, D), BF16), jax.ShapeDtypeStruct((D, lext), BF16),
                   jax.ShapeDtypeStruct((lext, LANES), F32), jax.ShapeDtypeStruct((lext, LANES), F32)],
        scratch_shapes=[pltpu.VMEM((tp, LANES), F32), pltpu.VMEM((tp, LANES), F32)],
        compiler_params=_cparams(dimension_semantics=("arbitrary",)),
    )(x, ctx, modx, modc, norm_w, after)


def _in_projection(p_ext, xm, cos_t, sin_t, w, pairs, me, name):
    lext = xm.shape[0]
    tr = lext // 4
    meta = []
    for (slot_a, k_a), (slot_b, k_b) in pairs:
        assert k_a ^ k_b == 1
        dev_a = jnp.bitwise_xor(me, k_a)
        a_low = dev_a % 2 == 0
        meta += [jnp.where(a_low, slot_a, slot_b), jnp.where(a_low, slot_b, slot_a), dev_a // 2]
    meta = jnp.stack(meta).astype(jnp.int32)

    def body(meta_ref, xm_ref, cos_ref, sin_ref, w_hbm, *rest):
        p_ref, w_ref, sems = rest[-3:]
        j, i = pl.program_id(0), pl.program_id(1)

        @pl.when(i == 0)
        def _():
            copies = [pltpu.make_async_copy(w_hbm.at[meta_ref[3 * j + h]], w_ref.at[:, h * WSH:(h + 1) * WSH],
                                            sems.at[h]) for h in range(2)]
            for cp in copies:
                cp.start()
            for cp in copies:
                cp.wait()

        _project_columns(xm_ref[...], w_ref, meta_ref[3 * j + 2], cos_ref[...], sin_ref[...], p_ref)

    row = lambda width: pl.BlockSpec((tr, width), lambda j, i, meta_: (i, 0))
    grid_spec = pltpu.PrefetchScalarGridSpec(
        num_scalar_prefetch=1, grid=(len(pairs), lext // tr),
        in_specs=[row(D), row(LANES), row(LANES), pl.BlockSpec(memory_space=pl.ANY)]
        + ([] if p_ext is None else [pl.BlockSpec(memory_space=pl.ANY)]),
        out_specs=pl.BlockSpec((tr, 2 * WSH), lambda j, i, meta_: (i, meta_[3 * j + 2])),
        scratch_shapes=[pltpu.VMEM((D, 2 * WSH), BF16), pltpu.SemaphoreType.DMA((2,))])
    return pl.pallas_call(
        body, name=name, grid_spec=grid_spec,
        out_shape=jax.ShapeDtypeStruct((lext, PW), BF16),
        input_output_aliases={} if p_ext is None else {5: 0},
        compiler_params=_cparams(dimension_semantics=("arbitrary", "arbitrary")),
    )(meta, xm, cos_t, sin_t, w, *([] if p_ext is None else [p_ext]))


def _decay_tables(lgf, lgb, n):
    i = lax.broadcasted_iota(jnp.int32, (n, 1), 0).astype(F32)
    return dict(i=i, k_f=jnp.exp(lgf * (n - 1.0 - i)), k_b=jnp.exp(lgb * i),
                q_f=jnp.exp(lgf * (i + 1.0)), q_b=jnp.exp(lgb * (n - i)))


def _decay_matrix(lgf, lgb, n):
    ii = lax.broadcasted_iota(jnp.int32, (n, n), 0)
    jj = lax.broadcasted_iota(jnp.int32, (n, n), 1)
    diff = (ii - jj).astype(F32)
    low = jnp.exp(lgf * jnp.maximum(diff, 0.0))
    up = jnp.exp(lgb * jnp.maximum(-diff, 0.0))
    return jnp.where(diff > 0, low, jnp.where(diff < 0, up, 2.0)), diff


def _cat_lanes(a, b):
    return jnp.concatenate([a.astype(BF16), b.astype(BF16)], axis=1)


def _retention_forward(p_ext, lg, seq, ctx_len):
    lext = seq + ctx_len
    n_chunks = seq // RET_C
    C = RET_C

    def body(lg_ref, q_ref, k_ref, v_ref, rn_ref, rstd_ref, kv_scr, sf_scr, sb_scr):
        pair = pl.program_id(0)
        lane = lax.broadcasted_iota(jnp.int32, (1, LANES), 1)
        heads = range(2)
        hmask = [(lane // DK == hh).astype(F32) for hh in heads]
        lgf = [lg_ref[0, 2 * pair + hh] for hh in heads]
        lgb = [lg_ref[1, 2 * pair + hh] for hh in heads]
        vs = [slice(hh * DV, (hh + 1) * DV) for hh in heads]
        t = [_decay_tables(lgf[hh], lgb[hh], C) for hh in heads]
        kc_all = k_ref[seq:lext, :].astype(F32)
        s0 = []
        for hh in heads:
            tc = _decay_tables(lgf[hh], lgb[hh], ctx_len)
            kc = kc_all * hmask[hh]
            s0.append(_dot_tn(_cat_lanes(kc * tc["k_f"], kc * tc["k_b"]), v_ref[seq:lext, vs[hh]]))

        def increments(c, carry):
            rows = pl.ds(pl.multiple_of(c * C, C), C)
            k_all = k_ref[rows, :].astype(F32)
            for hh in heads:
                k = k_all * hmask[hh]
                kv_scr[hh, c] = _dot_tn(_cat_lanes(k * t[hh]["k_f"], k * t[hh]["k_b"]), v_ref[rows, vs[hh]])
            return carry

        lax.fori_loop(0, n_chunks, increments, 0, unroll=2)
        gf_c = [jnp.exp(lgf[hh] * C) for hh in heads]
        gb_c = [jnp.exp(lgb[hh] * C) for hh in heads]

        def scan_f(c, s):
            for hh in heads:
                sf_scr[hh, c] = s[hh]
            return tuple(gf_c[hh] * s[hh] + kv_scr[hh, c, 0:LANES, :] for hh in heads)

        def scan_b(n, s):
            c = n_chunks - 1 - n
            for hh in heads:
                sb_scr[hh, c] = s[hh]
            return tuple(gb_c[hh] * s[hh] + kv_scr[hh, c, LANES:2 * LANES, :] for hh in heads)

        lax.fori_loop(0, n_chunks, scan_f, tuple(s0[hh][0:LANES] for hh in heads))
        lax.fori_loop(0, n_chunks, scan_b, tuple(s0[hh][LANES:2 * LANES] for hh in heads))
        dmat = [_decay_matrix(lgf[hh], lgb[hh], C)[0] for hh in heads]

        def outputs(c, carry):
            rows = pl.ds(pl.multiple_of(c * C, C), C)
            q_all = q_ref[rows, :].astype(F32)
            k = k_ref[rows, :]
            for hh in heads:
                q = q_all * hmask[hh]
                v = v_ref[rows, vs[hh]]
                s = _dot_nt(q.astype(BF16), k)
                inner = _dot((s * dmat[hh]).astype(BF16), v)
                states = jnp.concatenate([sf_scr[hh, c], sb_scr[hh, c]], axis=0).astype(BF16)
                ret = inner + _dot(_cat_lanes(q * t[hh]["q_f"], q * t[hh]["q_b"]), states)
                mu = jnp.mean(ret, axis=-1, keepdims=True)
                cen = ret - mu
                rstd = lax.rsqrt(jnp.mean(cen * cen, axis=-1, keepdims=True) + EPS)
                rn_ref[rows, vs[hh]] = cen * rstd
                rstd_ref[rows, vs[hh]] = jnp.broadcast_to(rstd, (C, DV))
            return carry

        lax.fori_loop(0, n_chunks, outputs, 0, unroll=2)

    qk0 = CB_QK * D // LANES
    return pl.pallas_call(
        body, name="retention_forward", grid=(N_HEADS // 2,),
        in_specs=[pl.BlockSpec(memory_space=pltpu.SMEM),
                  pl.BlockSpec((lext, LANES), lambda g: (0, qk0 + g)),
                  pl.BlockSpec((lext, LANES), lambda g: (0, qk0 + N_HEADS // 2 + g)),
                  pl.BlockSpec((lext, 2 * DV), lambda g: (0, CB_V * D // (2 * DV) + g))],
        out_specs=[pl.BlockSpec((seq, 2 * DV), lambda g: (0, g))] * 2,
        out_shape=[jax.ShapeDtypeStruct((seq, D), F32)] * 2,
        scratch_shapes=[pltpu.VMEM((2, n_chunks, 2 * LANES, DV), F32), pltpu.VMEM((2, n_chunks, LANES, DV), F32),
                        pltpu.VMEM((2, n_chunks, LANES, DV), F32)],
        compiler_params=_cparams(dimension_semantics=("arbitrary",)),
    )(lg, p_ext, p_ext, p_ext)


def _retention_backward(p_ext, dret, lg, cos_t, sin_t, seq, ctx_len, after):
    lext = seq + ctx_len
    n_chunks = seq // RET_C
    C = RET_C

    def body(lg_ref, q_ref, k_ref, v_ref, do_ref, cos_ref, sin_ref, after_ref, dq_ref, dk_ref, dv_ref, dlg_ref,
             kv_scr, g_scr, sf_scr, sb_scr, gfn_scr, gbp_scr):
        pair = pl.program_id(0)
        lane = lax.broadcasted_iota(jnp.int32, (1, LANES), 1)
        heads = range(2)
        hmask = [(lane // DK == hh).astype(F32) for hh in heads]
        lgf = [lg_ref[0, 2 * pair + hh] for hh in heads]
        lgb = [lg_ref[1, 2 * pair + hh] for hh in heads]
        vs = [slice(hh * DV, (hh + 1) * DV) for hh in heads]
        t = [_decay_tables(lgf[hh], lgb[hh], C) for hh in heads]
        tc = [_decay_tables(lgf[hh], lgb[hh], ctx_len) for hh in heads]
        kc_all = k_ref[seq:lext, :].astype(F32)
        kc = [kc_all * hmask[hh] for hh in heads]
        vc = [v_ref[seq:lext, vs[hh]] for hh in heads]
        kc_cat = [_cat_lanes(kc[hh] * tc[hh]["k_f"], kc[hh] * tc[hh]["k_b"]) for hh in heads]
        s0 = [_dot_tn(kc_cat[hh], vc[hh]) for hh in heads]

        def increments(c, carry):
            rows = pl.ds(pl.multiple_of(c * C, C), C)
            k_all = k_ref[rows, :].astype(F32)
            q_all = q_ref[rows, :].astype(F32)
            for hh in heads:
                k, q = k_all * hmask[hh], q_all * hmask[hh]
                kv_scr[hh, c] = _dot_tn(_cat_lanes(k * t[hh]["k_f"], k * t[hh]["k_b"]), v_ref[rows, vs[hh]])
                g_scr[hh, c] = _dot_tn(_cat_lanes(q * t[hh]["q_f"], q * t[hh]["q_b"]), do_ref[rows, vs[hh]])
            return carry

        lax.fori_loop(0, n_chunks, increments, 0, unroll=2)
        gf_c = [jnp.exp(lgf[hh] * C) for hh in heads]
        gb_c = [jnp.exp(lgb[hh] * C) for hh in heads]

        def scan_f(c, s):
            for hh in heads:
                sf_scr[hh, c] = s[hh]
            return tuple(gf_c[hh] * s[hh] + kv_scr[hh, c, 0:LANES, :] for hh in heads)

        def scan_b(n, s):
            c = n_chunks - 1 - n
            for hh in heads:
                sb_scr[hh, c] = s[hh]
            return tuple(gb_c[hh] * s[hh] + kv_scr[hh, c, LANES:2 * LANES, :] for hh in heads)

        def scan_gf(n, carry):
            c = n_chunks - 1 - n
            for hh in heads:
                gfn_scr[hh, c] = carry[hh]
            return tuple(g_scr[hh, c, 0:LANES, :] + gf_c[hh] * carry[hh] for hh in heads)

        def scan_gb(c, carry):
            for hh in heads:
                gbp_scr[hh, c] = carry[hh]
            return tuple(g_scr[hh, c, LANES:2 * LANES, :] + gb_c[hh] * carry[hh] for hh in heads)

        lax.fori_loop(0, n_chunks, scan_f, tuple(s0[hh][0:LANES] for hh in heads))
        lax.fori_loop(0, n_chunks, scan_b, tuple(s0[hh][LANES:2 * LANES] for hh in heads))
        zero_state = jnp.zeros((LANES, DV), F32)
        gf_first = lax.fori_loop(0, n_chunks, scan_gf, (zero_state, zero_state))
        gb_last = lax.fori_loop(0, n_chunks, scan_gb, (zero_state, zero_state))

        dmat, w_f, w_b = [], [], []
        for hh in heads:
            dm, diff = _decay_matrix(lgf[hh], lgb[hh], C)
            dmat.append(dm)
            w_f.append(jnp.where(diff > 0, diff * dm, 0.0))
            w_b.append(jnp.where(diff < 0, -diff * dm, 0.0))

        def total(a):
            rows_, width = a.shape
            part = jnp.sum(a.reshape(rows_ // 8, 8, width), axis=0)
            return part[:, 0:LANES] + part[:, LANES:2 * LANES] if width == 2 * LANES else part

        def chunk_head(hh, c, rows, q_all, k_all, dlgf, dlgb):
            th = t[hh]
            qm = q_all * hmask[hh]
            km = k_all * hmask[hh]
            qb, kb = qm.astype(BF16), km.astype(BF16)
            v = v_ref[rows, vs[hh]]
            do = do_ref[rows, vs[hh]]
            s = _dot_nt(qb, kb)
            dsd = _dot_nt(do, v)
            ds = (dsd * dmat[hh]).astype(BF16)
            dq_in = _dot(ds, kb)
            dk_in = _dot_tn(ds, qb)
            dv_in = _dot_tn((s * dmat[hh]).astype(BF16), do)
            prod = s * dsd
            dlgf = dlgf + total(prod * w_f[hh])
            dlgb = dlgb + total(prod * w_b[hh])
            sf, sb = sf_scr[hh, c], sb_scr[hh, c]
            states = jnp.concatenate([sf, sb], axis=0).astype(BF16)
            dqc = _dot_nt(do, states)
            dqf = dqc[:, 0:LANES] * th["q_f"]
            dqb = dqc[:, LANES:2 * LANES] * th["q_b"]
            dq = (dq_in + dqf + dqb) * hmask[hh]
            dlgf = dlgf + total((th["i"] + 1.0) * (qm * dqf))
            dlgb = dlgb + total((C - th["i"]) * (qm * dqb))
            gfn, gbp = gfn_scr[hh, c], gbp_scr[hh, c]
            gstates = jnp.concatenate([gfn, gbp], axis=0).astype(BF16)
            dkc = _dot_nt(v, gstates)
            dkf = dkc[:, 0:LANES] * th["k_f"]
            dkb = dkc[:, LANES:2 * LANES] * th["k_b"]
            dk = dk_in + dkf + dkb
            dlgf = dlgf + total((C - 1.0 - th["i"]) * (km * dkf)) + C * gf_c[hh] * total(gfn * sf)
            dlgb = dlgb + total(th["i"] * (km * dkb)) + C * gb_c[hh] * total(gbp * sb)
            dv = dv_in + _dot(_cat_lanes(km * th["k_f"], km * th["k_b"]), gstates)
            dv_ref[rows, vs[hh]] = dv.astype(BF16)
            return dq, dk, dlgf, dlgb

        def chunk(c, carry):
            rows = pl.ds(pl.multiple_of(c * C, C), C)
            q_all = q_ref[rows, :].astype(F32)
            k_all = k_ref[rows, :].astype(F32)
            dq0, dk0, f0, b0 = chunk_head(0, c, rows, q_all, k_all, carry[0], carry[1])
            dq1, dk1, f1, b1 = chunk_head(1, c, rows, q_all, k_all, carry[2], carry[3])
            cos, sin = cos_ref[rows, :], sin_ref[rows, :]
            dq_ref[rows, :] = _rope(dq0 + dq1, cos, -sin).astype(BF16)
            dk_ref[rows, :] = (_rope(dk0 + dk1, cos, -sin) * (DK ** -0.5)).astype(BF16)
            return f0, b0, f1, b1

        zero = jnp.zeros((8, LANES), F32)
        sums = lax.fori_loop(0, n_chunks, chunk, (zero, zero, zero, zero), unroll=2)

        dlg = []
        dk_ctx = jnp.zeros((ctx_len, LANES), F32)
        for hh in heads:
            g0 = jnp.concatenate([gf_first[hh], gb_last[hh]], axis=0).astype(BF16)
            dkcc = _dot_nt(vc[hh], g0)
            dkcf = dkcc[:, 0:LANES] * tc[hh]["k_f"]
            dkcb = dkcc[:, LANES:2 * LANES] * tc[hh]["k_b"]
            dlgf = sums[2 * hh] + total((ctx_len - 1.0 - tc[hh]["i"]) * (kc[hh] * dkcf))
            dlgb = sums[2 * hh + 1] + total(tc[hh]["i"] * (kc[hh] * dkcb))
            dk_ctx = dk_ctx + (dkcf + dkcb) * (DK ** -0.5)
            dv_ref[seq:lext, vs[hh]] = _dot(kc_cat[hh], g0).astype(BF16)
            dlg += [jnp.sum(jnp.sum(a, axis=1, keepdims=True), axis=0, keepdims=True) for a in (dlgf, dlgb)]
        dk_ref[seq:lext, :] = dk_ctx.astype(BF16)
        dq_ref[seq:lext, :] = jnp.zeros((ctx_len, LANES), BF16)

        lane8 = lax.broadcasted_iota(jnp.int32, (8, LANES), 1)
        out = jnp.zeros((8, LANES), F32)
        for n, val in enumerate(dlg):
            out = jnp.where(lane8 == n, val, out)
        dlg_ref[0] = out

    qk0 = CB_QK * D // LANES
    q_spec = pl.BlockSpec((lext, LANES), lambda g: (0, qk0 + g))
    k_spec = pl.BlockSpec((lext, LANES), lambda g: (0, qk0 + N_HEADS // 2 + g))
    v_spec = pl.BlockSpec((lext, 2 * DV), lambda g: (0, CB_V * D // (2 * DV) + g))
    table = pl.BlockSpec((lext, LANES), lambda g: (0, 0))
    state = pltpu.VMEM((2, n_chunks, LANES, DV), F32)
    return pl.pallas_call(
        body, name="retention_backward", grid=(N_HEADS // 2,),
        in_specs=[pl.BlockSpec(memory_space=pltpu.SMEM), q_spec, k_spec, v_spec,
                  pl.BlockSpec((seq, 2 * DV), lambda g: (0, g)), table, table, _AFTER_SPEC],
        out_specs=[pl.BlockSpec((lext, LANES), lambda g: (0, g)), pl.BlockSpec((lext, LANES), lambda g: (0, g)),
                   pl.BlockSpec((lext, 2 * DV), lambda g: (0, g)), pl.BlockSpec((1, 8, LANES), lambda g: (g, 0, 0))],
        out_shape=[jax.ShapeDtypeStruct((lext, N_HEADS * DK), BF16), jax.ShapeDtypeStruct((lext, N_HEADS * DK), BF16),
                   jax.ShapeDtypeStruct((lext, D), BF16), jax.ShapeDtypeStruct((N_HEADS // 2, 8, LANES), F32)],
        scratch_shapes=[pltpu.VMEM((2, n_chunks, 2 * LANES, DV), F32), pltpu.VMEM((2, n_chunks, 2 * LANES, DV), F32),
                        state, state, state, state],
        compiler_params=_cparams(dimension_semantics=("arbitrary",)),
    )(lg, p_ext, p_ext, p_ext, dret, cos_t, sin_t, after)


class _ColumnWriter:
    def __init__(self, dst_hbm, stage, sems, pieces, n_steps):
        self.dst, self.stage, self.sems, self.pieces, self.n_steps = dst_hbm, stage, sems, pieces, n_steps

    def _copies(self, slot, tile):
        rows = pl.ds(pl.multiple_of(tile * TM, TM), TM)
        return [pltpu.make_async_copy(self.stage.at[slot, :, pl.ds(src, width)], self.dst.at[rows, pl.ds(dst, width)],
                                      self.sems.at[slot, k]) for k, (src, dst, width) in enumerate(self.pieces)]

    def slot_for(self, step):
        slot = step % 2

        @pl.when(step >= 2)
        def _():
            for cp in self._copies(slot, step - 2):
                cp.wait()

        return slot

    def send(self, step, slot):
        for cp in self._copies(slot, step):
            cp.start()

        @pl.when(step == self.n_steps - 1)
        def _():
            if self.n_steps >= 2:
                for cp in self._copies(1 - slot, step - 1):
                    cp.wait()
            for cp in self._copies(slot, step):
                cp.wait()


def _merge(p_ext, rn, rstd, x, target, w_a, w_b, w_out, vecs, seq):
    n_tiles = seq // TM
    hb = TM // HALO
    dp_pieces = [(0, CB_BG * D, D), (D, CB_ZA * D, D), (2 * D, CB_ZB * D, 3 * D)]

    def body(h_ref, bg_ref, cg_ref, za_ref, zb_ref, ga_ref, gb_ref, hp_ref, hn_ref, cp_ref, cn_ref,
             rn_ref, rstd_ref, x_ref, t_ref, wa_ref, wb_ref, wo_ref, vec_ref,
             dx1_ref, dp_hbm, dconv_ref, dret_ref, at_ref, b_ref, part_ref, stage, dp_sems):
        i = pl.program_id(0)
        writer = _ColumnWriter(dp_hbm, stage, dp_sems, dp_pieces, n_tiles)
        slot = writer.slot_for(i)
        dpa_ref = stage.at[slot]
        f = lambda ref: ref[...].astype(F32)
        h, bg, cg, za, zb, ga, gb = f(h_ref), f(bg_ref), f(cg_ref), f(za_ref), f(zb_ref), f(ga_ref), f(gb_ref)
        gx, w0, w1, w2 = vec_ref[0:1, :], vec_ref[1:2, :], vec_ref[2:3, :], vec_ref[3:4, :]
        cb, gnw, fw = vec_ref[4:5, :], vec_ref[5:6, :], vec_ref[6:7, :]
        u = cg * h
        row = lax.broadcasted_iota(jnp.int32, (TM, 1), 0)
        u_prev = (f(cp_ref) * f(hp_ref))[HALO - 1:HALO, :]
        u_next = (f(cn_ref) * f(hn_ref))[0:1, :]
        u_prev = jnp.where(i == 0, 0.0, u_prev)
        u_next = jnp.where(i == n_tiles - 1, 0.0, u_next)
        u_up = jnp.where(row == 0, u_prev, pltpu.roll(u, 1, 0))
        u_dn = jnp.where(row == TM - 1, u_next, pltpu.roll(u, TM - 1, 0))
        conv = w0 * u_up + w1 * u + w2 * u_dn + cb
        sza = _sigmoid(za)
        silu_za = za * sza
        a_act = silu_za * bg * conv
        rn = rn_ref[...]
        szb = _sigmoid(zb)
        silu_zb = zb * szb
        rg = rn * gnw
        b_act = silu_zb * rg
        y_a = _dot(a_act.astype(BF16), wa_ref[...])
        y_b = _dot(b_act.astype(BF16), wb_ref[...])
        sga, sgb = _sigmoid(ga), _sigmoid(gb)
        mix = sga * y_a + sgb * y_b
        y = _dot(mix.astype(BF16), wo_ref[...])
        x1 = x_ref[...] + gx * y
        r1 = lax.rsqrt(jnp.mean(x1 * x1, axis=-1, keepdims=True) + EPS)
        xh1 = x1 * r1
        err = xh1 * fw - t_ref[...]
        loss = jnp.sum(jnp.sum(err * err, axis=1, keepdims=True), axis=0, keepdims=True) * (0.5 / D)
        dxh = err * (fw * (1.0 / D))
        dx1 = r1 * (dxh - xh1 * jnp.mean(dxh * xh1, axis=-1, keepdims=True))
        dx1_ref[...] = dx1
        dy = (dx1 * gx).astype(BF16)
        dmix = _dot_nt(dy, wo_ref[...])
        dya_f, dyb_f = dmix * sga, dmix * sgb
        dya, dyb = dya_f.astype(BF16), dyb_f.astype(BF16)
        da = _dot_nt(dya, wa_ref[...])
        db = _dot_nt(dyb, wb_ref[...])
        da_silu = da * silu_za
        dpa_ref[:, 0:D] = (da_silu * conv).astype(BF16)
        dpa_ref[:, D:2 * D] = (da * bg * conv * (sza + silu_za * (1.0 - sza))).astype(BF16)
        dpa_ref[:, 2 * D:3 * D] = (db * rg * (szb + silu_zb * (1.0 - szb))).astype(BF16)
        dpa_ref[:, 3 * D:4 * D] = (dya_f * y_a * (1.0 - sga)).astype(BF16)
        dpa_ref[:, 4 * D:5 * D] = (dyb_f * y_b * (1.0 - sgb)).astype(BF16)
        dconv_ref[...] = (da_silu * bg).astype(BF16)
        drn_n = db * silu_zb
        drn = drn_n * gnw
        rstd = rstd_ref[...]
        for hd in range(N_HEADS):
            sl = slice(hd * DV, (hd + 1) * DV)
            dh_, rh = drn[:, sl], rn[:, sl]
            m1 = jnp.mean(dh_, axis=-1, keepdims=True)
            m2 = jnp.mean(dh_ * rh, axis=-1, keepdims=True)
            dret_ref[:, sl] = (rstd[:, sl] * (dh_ - m1 - rh * m2)).astype(BF16)
        at_ref[0] = a_act.T.astype(BF16)
        at_ref[1] = b_act.T.astype(BF16)
        at_ref[2] = mix.T.astype(BF16)
        b_ref[0] = dya
        b_ref[1] = dyb
        b_ref[2] = dy

        @pl.when(i == 0)
        def _():
            part_ref[...] = jnp.zeros((8, D), F32)

        part_ref[0:1, :] += jnp.sum(err * xh1, axis=0, keepdims=True) * (1.0 / D)
        part_ref[1:2, :] += jnp.sum(dx1 * y, axis=0, keepdims=True)
        part_ref[2:3, :] += jnp.sum(drn_n * rn, axis=0, keepdims=True)
        part_ref[3:4, :] += jnp.broadcast_to(loss, (1, D))
        writer.send(i, slot)

    col = lambda cb_: pl.BlockSpec((TM, D), lambda i, cb_=cb_: (i, cb_))
    prev = lambda cb_: pl.BlockSpec((HALO, D), lambda i, cb_=cb_: (jnp.maximum(i * hb - 1, 0), cb_))
    nxt = lambda cb_: pl.BlockSpec((HALO, D), lambda i, cb_=cb_: (jnp.minimum((i + 1) * hb, n_tiles * hb - 1), cb_))
    tile = pl.BlockSpec((TM, D), lambda i: (i, 0))
    full = lambda a: pl.BlockSpec(a.shape, lambda i: (0,) * a.ndim, pipeline_mode=pl.Buffered(1))
    return pl.pallas_call(
        body, name="merge", grid=(n_tiles,),
        in_specs=[col(CB_H), col(CB_BG), col(CB_CG), col(CB_ZA), col(CB_ZB), col(CB_GA), col(CB_GB),
                  prev(CB_H), nxt(CB_H), prev(CB_CG), nxt(CB_CG),
                  tile, tile, tile, tile, full(w_a), full(w_b), full(w_out), full(vecs)],
        out_specs=[tile, pl.BlockSpec(memory_space=pl.ANY), tile, tile,
                   pl.BlockSpec((3, D, TM), lambda i: (0, 0, i)), pl.BlockSpec((3, TM, D), lambda i: (0, i, 0)),
                   pl.BlockSpec((8, D), lambda i: (0, 0))],
        out_shape=[jax.ShapeDtypeStruct((seq, D), F32), jax.ShapeDtypeStruct((p_ext.shape[0], PW), BF16),
                   jax.ShapeDtypeStruct((seq, D), BF16), jax.ShapeDtypeStruct((seq, D), BF16),
                   jax.ShapeDtypeStruct((3, D, seq), BF16), jax.ShapeDtypeStruct((3, seq, D), BF16),
                   jax.ShapeDtypeStruct((8, D), F32)],
        scratch_shapes=[pltpu.VMEM((2, TM, 5 * D), BF16), pltpu.SemaphoreType.DMA((2, len(dp_pieces)))],
        compiler_params=_cparams(dimension_semantics=("arbitrary",)),
    )(p_ext, p_ext, p_ext, p_ext, p_ext, p_ext, p_ext, p_ext, p_ext, p_ext, p_ext,
      rn, rstd, x, target, w_a, w_b, w_out, vecs)


def _conv_backward(p_ext, dconv, dp, dq, dk, dv, vecs, seq, after):
    n_tiles = seq // TM
    hb = TM // HALO
    qk = N_HEADS * DK
    pieces = [(0, CB_H * D, D), (D, CB_CG * D, D), (2 * D, CB_QK * D, 2 * D)]
    zero_pieces = [(CB_BG * D, D), (CB_ZA * D, D), (CB_ZB * D, 3 * D)]

    def body(h_ref, cg_ref, dc_ref, dcp_ref, dcn_ref, dq_ref, dk_ref, dv_ref, vec_ref, dp_in, after_ref, dp_hbm, part_ref,
             stage, sems, zeros, zero_sems):
        del dp_in, after_ref
        i = pl.program_id(0)
        writer = _ColumnWriter(dp_hbm, stage, sems, pieces, n_tiles + 1)
        slot = writer.slot_for(i)
        out = stage.at[slot]
        out[:, 2 * D:2 * D + qk] = dq_ref[...]
        out[:, 2 * D + qk:3 * D] = dk_ref[...]
        out[:, 3 * D:4 * D] = dv_ref[...]

        @pl.when(i == 0)
        def _():
            part_ref[...] = jnp.zeros((8, D), F32)

        @pl.when(i == n_tiles)
        def _():
            out[:, 0:2 * D] = jnp.zeros((TM, 2 * D), BF16)
            zeros[...] = jnp.zeros(zeros.shape, BF16)
            rows = pl.ds(n_tiles * TM, TM)
            fills = [pltpu.make_async_copy(zeros.at[:, pl.ds(0, width)], dp_hbm.at[rows, pl.ds(dst, width)],
                                           zero_sems.at[k]) for k, (dst, width) in enumerate(zero_pieces)]
            for cp in fills:
                cp.start()
            for cp in fills:
                cp.wait()

        @pl.when(i < n_tiles)
        def _():
            f = lambda ref: ref[...].astype(F32)
            h, cg, dc = f(h_ref), f(cg_ref), f(dc_ref)
            w0, w1, w2 = vec_ref[1:2, :], vec_ref[2:3, :], vec_ref[3:4, :]
            row = lax.broadcasted_iota(jnp.int32, (TM, 1), 0)
            dc_prev = jnp.where(i == 0, 0.0, f(dcp_ref)[HALO - 1:HALO, :])
            dc_next = jnp.where(i == n_tiles - 1, 0.0, f(dcn_ref)[0:1, :])
            dc_up = jnp.where(row == 0, dc_prev, pltpu.roll(dc, 1, 0))
            dc_dn = jnp.where(row == TM - 1, dc_next, pltpu.roll(dc, TM - 1, 0))
            du = w0 * dc_dn + w1 * dc + w2 * dc_up
            u = cg * h
            part_ref[0:1, :] += jnp.sum(u * dc_dn, axis=0, keepdims=True)
            part_ref[1:2, :] += jnp.sum(u * dc, axis=0, keepdims=True)
            part_ref[2:3, :] += jnp.sum(u * dc_up, axis=0, keepdims=True)
            part_ref[3:4, :] += jnp.sum(dc, axis=0, keepdims=True)
            out[:, 0:D] = (du * cg).astype(BF16)
            out[:, D:2 * D] = (du * h).astype(BF16)

        writer.send(i, slot)

    last = n_tiles - 1
    col = lambda cb_: pl.BlockSpec((TM, D), lambda i, cb_=cb_: (jnp.minimum(i, last), cb_))
    lat = lambda w: pl.BlockSpec((TM, w), lambda i: (jnp.minimum(i, last), 0))
    ext = lambda w: pl.BlockSpec((TM, w), lambda i: (i, 0))
    anyspec = pl.BlockSpec(memory_space=pl.ANY)
    return pl.pallas_call(
        body, name="conv_backward", grid=(n_tiles + 1,),
        in_specs=[col(CB_H), col(CB_CG), lat(D),
                  pl.BlockSpec((HALO, D), lambda i: (jnp.clip(i * hb - 1, 0, n_tiles * hb - 1), 0)),
                  pl.BlockSpec((HALO, D), lambda i: (jnp.minimum((i + 1) * hb, n_tiles * hb - 1), 0)),
                  ext(qk), ext(qk), ext(D), pl.BlockSpec(vecs.shape, lambda i: (0, 0)), anyspec, _AFTER_SPEC],
        out_specs=[anyspec, pl.BlockSpec((8, D), lambda i: (0, 0))],
        out_shape=[jax.ShapeDtypeStruct(dp.shape, BF16), jax.ShapeDtypeStruct((8, D), F32)],
        input_output_aliases={9: 0},
        scratch_shapes=[pltpu.VMEM((2, TM, 4 * D), BF16), pltpu.SemaphoreType.DMA((2, len(pieces))),
                        pltpu.VMEM((TM, 3 * D), BF16), pltpu.SemaphoreType.DMA((len(zero_pieces),))],
        compiler_params=_cparams(dimension_semantics=("arbitrary",)),
    )(p_ext, p_ext, dconv, dconv, dconv, dq, dk, dv, vecs, dp, after)


def _input_backward(dp, shards, ids, x, ctx, dx1, modx, modc, norm_w, after):
    seq = x.shape[0]
    lext = seq + ctx.shape[0]
    n_x = seq // TM
    n_w = len(shards.arrays)

    def body(ids_ref, dp_ref, x_ref, ctx_ref, dx1_ref, mx_ref, mc_ref, nw_ref, *rest):
        w_hbm, (_, gx_ref, part_ref, w_ref, w_sems) = rest[:n_w], rest[n_w:]
        i = pl.program_id(0)
        _load_w_in(i, ids_ref, shards, w_hbm, w_ref, w_sems)
        is_ctx = i >= n_x
        dxm = _dot_nt(dp_ref[...], w_ref[...])
        x = jnp.where(is_ctx, ctx_ref[...], x_ref[...])
        r = lax.rsqrt(jnp.mean(x * x, axis=-1, keepdims=True) + EPS)
        xh = x * r
        nw = nw_ref[...]
        sc = jnp.where(is_ctx, mc_ref[1:2, :], mx_ref[1:2, :])
        dxn = dxm * (1.0 + sc)
        dxh = dxn * nw
        dx = r * (dxh - xh * jnp.mean(dxh * xh, axis=-1, keepdims=True))

        @pl.when(jnp.logical_not(is_ctx))
        def _():
            gx_ref[...] = dx1_ref[...] + dx

        @pl.when(i == 0)
        def _():
            part_ref[...] = jnp.zeros((8, D), F32)

        fx = jnp.where(is_ctx, 0.0, 1.0)
        d_shift = jnp.sum(dxm, axis=0, keepdims=True)
        d_scale = jnp.sum(dxm * (xh * nw), axis=0, keepdims=True)
        part_ref[0:1, :] += fx * d_shift
        part_ref[1:2, :] += fx * d_scale
        part_ref[2:3, :] += jnp.sum(dxn * xh, axis=0, keepdims=True)
        part_ref[3:4, :] += (1.0 - fx) * d_shift
        part_ref[4:5, :] += (1.0 - fx) * d_scale

    lat = lambda w: pl.BlockSpec((TM, w), lambda i, ids_: (jnp.minimum(i, n_x - 1), 0))
    ext = lambda w: pl.BlockSpec((TM, w), lambda i, ids_: (i, 0))
    full = lambda a: pl.BlockSpec(a.shape, lambda i, ids_: (0,) * a.ndim)
    grid_spec = pltpu.PrefetchScalarGridSpec(
        num_scalar_prefetch=1, grid=(lext // TM,),
        in_specs=[ext(PW), lat(D), full(ctx), lat(D), full(modx), full(modc), full(norm_w)]
        + [pl.BlockSpec(memory_space=pl.ANY)] * n_w + [_AFTER_SPEC],
        out_specs=[lat(D), pl.BlockSpec((8, D), lambda i, ids_: (0, 0))],
        scratch_shapes=[pltpu.VMEM((D, PW), BF16), pltpu.SemaphoreType.DMA((N_DEV,))])
    return pl.pallas_call(
        body, name="input_backward", grid_spec=grid_spec,
        out_shape=[jax.ShapeDtypeStruct((seq, D), F32), jax.ShapeDtypeStruct((8, D), F32)],
        compiler_params=_cparams(dimension_semantics=("arbitrary",)),
    )(ids, dp, x, ctx, dx1, modx, modc, norm_w, *shards.arrays, after)


def _weight_grad_in_sibling(xm_t, dp, owners, after):
    lext = xm_t.shape[1]

    def body(own_ref, a_ref, b_ref, after_ref, o_ref):
        o_ref[0] = _dot(a_ref[...], b_ref[...]).astype(BF16)

    grid_spec = pltpu.PrefetchScalarGridSpec(
        num_scalar_prefetch=1, grid=(4,),
        in_specs=[pl.BlockSpec((D, lext), lambda j, own: (0, 0)),
                  pl.BlockSpec((lext, WSH), lambda j, own: (0, own[j])), _AFTER_SPEC],
        out_specs=pl.BlockSpec((1, D, WSH), lambda j, own: (j, 0, 0)))
    return pl.pallas_call(
        body, name="weight_grad_in_sibling", grid_spec=grid_spec,
        out_shape=jax.ShapeDtypeStruct((4, D, WSH), BF16),
        compiler_params=_cparams(dimension_semantics=("arbitrary",)),
    )(owners, xm_t, dp, after)


def _weight_grad_in_own(xm_t, dp, owners, recv):
    lext = xm_t.shape[1]

    def body(own_ref, a_ref, b_ref, r_ref, mine_ref, send_ref):
        j = pl.program_id(0)
        total = _dot(a_ref[...], b_ref[...]) + r_ref[0].astype(F32)

        @pl.when(j == 0)
        def _():
            mine_ref[0] = total

        @pl.when(j > 0)
        def _():
            send_ref[0] = total.astype(BF16)

    grid_spec = pltpu.PrefetchScalarGridSpec(
        num_scalar_prefetch=1, grid=(4,),
        in_specs=[pl.BlockSpec((D, lext), lambda j, own: (0, 0), pipeline_mode=pl.Buffered(1)),
                  pl.BlockSpec((lext, WSH), lambda j, own: (0, own[j])),
                  pl.BlockSpec((1, D, WSH), lambda j, own: (j, 0, 0))],
        out_specs=[pl.BlockSpec((1, D, WSH), lambda j, own: (0, 0, 0)),
                   pl.BlockSpec((1, D, WSH), lambda j, own: (jnp.maximum(j - 1, 0), 0, 0))])
    return pl.pallas_call(
        body, name="weight_grad_in_own", grid_spec=grid_spec,
        out_shape=[jax.ShapeDtypeStruct((1, D, WSH), F32), jax.ShapeDtypeStruct((3, D, WSH), BF16)],
        compiler_params=_cparams(dimension_semantics=("arbitrary",)),
    )(owners, xm_t, dp, recv)


def _weight_grad_square(a_t, b):
    seq = a_t.shape[2]

    def body(a_ref, b_ref, o_ref):
        o_ref[:, 0] = _dot(a_ref[0], b_ref[0]).reshape(N_DEV, RSH, D)

    return pl.pallas_call(
        body, name="weight_grad_square", grid=(3,),
        in_specs=[pl.BlockSpec((1, D, seq), lambda t: (t, 0, 0)), pl.BlockSpec((1, seq, D), lambda t: (t, 0, 0))],
        out_specs=pl.BlockSpec((N_DEV, 1, RSH, D), lambda t: (0, t, 0, 0)),
        out_shape=jax.ShapeDtypeStruct((N_DEV, 3, RSH, D), F32),
        compiler_params=_cparams(dimension_semantics=("arbitrary",)),
    )(a_t, b)


def _rows(total, *parts):
    width = max(a.shape[1] for _, a in parts)
    out = None
    for row, a in parts:
        padded = jnp.pad(a, ((row, total - row - a.shape[0]), (0, width - a.shape[1])))
        out = padded if out is None else out + padded
    return out


def kernel(x, c, ctx, c_ctx, norm_w, ada_w, ada_b, w_in, conv_w, conv_b, decay_logit, gn_w, w_a, w_b, w_out, final_norm_w, loss_target, m_c_ctx, m_norm_w, m_ada_w, m_ada_b, m_w_in, m_conv_w, m_conv_b, m_decay_logit, m_gn_w, m_w_a, m_w_b, m_w_out, m_final_norm_w, v_c_ctx, v_norm_w, v_ada_w, v_ada_b, v_w_in, v_conv_w, v_conv_b, v_decay_logit, v_gn_w, v_w_a, v_w_b, v_w_out, v_final_norm_w):
    xi, yi, ci = _coords()
    me = 4 * xi + 2 * yi + ci
    chip = 2 * xi + yi
    seq, ctx_len = x.shape[1], ctx.shape[1]
    assert seq % TM == 0 and seq % RET_C == 0 and ctx_len == TM and seq % GRID_W == 0
    csh = D // N_DEV

    blk = jnp.pad(c, ((0, 7), (0, 0))) + jnp.pad(conv_w[0], ((1, 4), (0, D - csh)))
    got, _ = _all_gather_small(blk, "gather_cond")
    conv_w_all = got[:, 1:4, 0:csh].transpose(1, 0, 2).reshape(3, D)
    c16 = _rows(16, (0, got[:, 0, :]), (N_DEV, c_ctx[None]))
    ada_b_sh = lax.dynamic_slice(ada_b, (0, me * ADA_SH), (1, ADA_SH))
    mod_sh, act16, lg = _modulation(c16, ada_w[0], ada_b_sh, decay_logit[0])
    mod_all, small_done = _all_gather_small(mod_sh, "gather_mod")
    mod_all = mod_all.transpose(1, 0, 2).reshape(16, 3 * D)
    modx = lax.dynamic_slice(mod_all, (me, 0), (1, 3 * D)).reshape(3, D)
    modc = mod_all[8].reshape(3, D)

    x2, tgt = x[0], loss_target[0]
    wblock = lambda n: jax.ShapeDtypeStruct((n, D, WSH), BF16)
    ids_of = lambda ks: jnp.stack([jnp.bitwise_xor(me, k) for k in ks]).astype(jnp.int32)
    own_shard = w_in[0].astype(BF16)[None]
    sq_shards = [w[0].astype(BF16) for w in (w_a, w_b, w_out)]
    wave1 = _Exchange(
        "gather_in_first", [jnp.broadcast_to(own_shard, (2, D, WSH)), own_shard], [wblock(4)], 3,
        lambda srcs, lands: _forward_plan([(0, 1)])(srcs, lands) + _send_plan((4, 2))(srcs[1:], lands),
        after=small_done)
    xm, xm_t, cos_t, sin_t = _prenorm(x2, ctx[0], modx, modc, norm_w, wave1.token)
    (w_pair, _), _ = wave1.wait(xm_t, 1, through=[0])
    first = _Shards([w_pair], [(0, 0), (0, 1)])
    p_ext = _in_projection(None, xm, cos_t, sin_t, w_pair, [((0, 0), (1, 1))], me, "in_projection_pair")
    (_, own_shard), (w_nbr,) = wave1.wait(p_ext, through=[1, 2])
    wave2 = _Exchange(
        "gather_in_second", [w_nbr, own_shard], [wblock(2)], 3,
        lambda srcs, lands: _forward_plan([(0, 2), (1, 3)])(srcs, lands) + _send_plan((6,))(srcs[1:], lands))
    (w_nbr, _), _ = wave2.wait(wave2.token, 2, through=[0])
    second = _Shards([w_nbr], [(0, 0), (0, 1), (0, 2), (0, 3)])
    p_ext = _in_projection(p_ext, xm, cos_t, sin_t, w_nbr, [((0, 4), (2, 5)), ((1, 2), (3, 3))], me,
                           "in_projection_neighbours")
    _, (w_diag,) = wave2.wait(p_ext, through=[1, 2])
    wave3 = _Exchange(
        "gather_in_third", [w_diag] + sq_shards, [jax.ShapeDtypeStruct((N_DEV, RSH, D), BF16)] * 3, 1 + 3 * (N_DEV - 1),
        lambda srcs, lands: _forward_plan([(0, 1)])(srcs, lands) + _gather_plan(srcs[1:], lands))
    (w_diag, *_), _ = wave3.wait(wave3.token, 1, through=[0])
    third = _Shards([w_diag], [(0, 0), (0, 1)])
    p_ext = _in_projection(p_ext, xm, cos_t, sin_t, w_diag, [((0, 6), (1, 7))], me, "in_projection_diagonal")
    w_shards, w_ids = first + second + third, ids_of((0, 1, 4, 2, 5, 3, 6, 7))

    rn, rstd = _retention_forward(p_ext, lg, seq, ctx_len)
    (_, *sq_shards), sq_lands = wave3.wait(rstd, through=range(1, 7))
    w_a_all, w_b_all, w_out_all = (
        lax.dynamic_update_slice(land, shard[None], (me, 0, 0)).reshape(D, D) for land, shard in zip(sq_lands, sq_shards))
    vecs = _rows(8, (0, modx[2:3]), (1, conv_w_all), (4, conv_b), (5, gn_w), (6, final_norm_w[None]))
    dx1, dp, dconv, dret, op_at, op_b, part_m = _merge(p_ext, rn, rstd, x2, tgt, w_a_all, w_b_all, w_out_all, vecs, seq)

    j4 = jnp.arange(4, dtype=jnp.int32)
    owners = (2 * jnp.bitwise_xor(chip, j4) + ci).astype(jnp.int32)
    owners_sib = (2 * jnp.bitwise_xor(chip, j4) + (1 - ci)).astype(jnp.int32)
    gw_sq = _weight_grad_square(op_at, op_b).reshape(N_DEV, 3 * RSH, D)
    rs_sq_pair = _Exchange("rs_square_pair", [gw_sq], [jax.ShapeDtypeStruct((4, 3 * RSH, D), F32)], 4,
                           _pair_plan(lambda j, chip_, c_: 2 * jnp.bitwise_xor(chip_, j) + (1 - c_)))
    dq, dk, dv, dlg = _retention_backward(p_ext, dret, lg, cos_t, sin_t, seq, ctx_len, rs_sq_pair.token)
    dp, part_c = _conv_backward(p_ext, dconv, dp, dq, dk, dv, vecs, seq, dlg)
    gw_sib = _weight_grad_in_sibling(xm_t, dp, owners_sib, part_c)
    rs_in_pair = _Exchange("rs_in_pair", [gw_sib], [jax.ShapeDtypeStruct((4, D, WSH), BF16)], 4,
                           _pair_plan(lambda j, chip_, c_: j))
    (gw_sq,), (r1_sq,) = rs_sq_pair.wait(rs_in_pair.token)
    own_sq, send_sq = _pair_sum(gw_sq, r1_sq, owners, "pair_sum_square")
    rs_sq_chips = _Exchange("rs_square_chips", [send_sq], [jax.ShapeDtypeStruct((3, 3 * RSH, D), BF16)], 3, _chips_plan)
    _, (r1_in,) = rs_in_pair.wait(rs_sq_chips.token)
    own_in, send_in = _weight_grad_in_own(xm_t, dp, owners, r1_in)
    rs_in_chips = _Exchange("rs_in_chips", [send_in], [jax.ShapeDtypeStruct((3, D, WSH), BF16)], 3, _chips_plan)
    grad_x, part_i = _input_backward(dp, w_shards, w_ids, x2, ctx[0], dx1, modx, modc, norm_w, rs_in_chips.token)

    dl = dlg[:, 0, 0:4]
    dlg_row = jnp.pad(dl[:, 0::2].reshape(1, N_HEADS), ((0, 0), (0, D - N_HEADS))) + jnp.pad(
        dl[:, 1::2].reshape(1, N_HEADS), ((0, 0), (N_HEADS, D - 2 * N_HEADS)))
    partials = _rows(16, (0, part_m[0:1]), (1, part_i[2:3]), (2, part_c[3:4]), (3, part_m[2:3]), (4, part_c[0:3]),
                     (7, part_i[0:2]), (9, part_m[1:2]), (10, part_i[3:5]), (13, part_m[3:4]), (14, dlg_row))
    results = {}
    gather_part = _Exchange("gather_partials", [partials], [jax.ShapeDtypeStruct((N_DEV, 16, D), F32)],
                            N_DEV - 1, _gather_plan)
    _, (r2_in,) = rs_in_chips.wait(gather_part.token)
    results["w_in"] = [o[None] for o in _adam_sharded(w_in[0], m_w_in[0], v_w_in[0], own_in, r2_in, "adam_w_in")]
    (partials,), (got,) = gather_part.wait(results["w_in"][1])
    got = lax.dynamic_update_slice(got, partials[None], (me, 0, 0))
    tot = _sum_devices(got, "sum_partials")
    dmodc = tot[10:13].reshape(1, 3 * D)
    dmod16 = _rows(16, (0, got[:, 7:10, :].reshape(N_DEV, 3 * D)), (N_DEV, dmodc))
    dmod16 = lax.dynamic_slice(dmod16, (0, me * ADA_SH), (16, ADA_SH))
    dmodc8 = _rows(8, (0, dmod16[8:9]))
    g_ada_w, d_ada_w, nm_ada_w, nv_ada_w, cctx_part = _ada_backward(
        act16.T, dmod16, dmodc8, ada_w[0], m_ada_w[0], v_ada_w[0])
    gather_cctx = _Exchange("gather_cctx", [cctx_part], [jax.ShapeDtypeStruct((N_DEV, 8, D), F32)],
                            N_DEV - 1, _gather_plan)
    _, (r2_sq,) = rs_sq_chips.wait(gather_cctx.token)
    square = _adam_square([(w_a[0], m_w_a[0], v_w_a[0]), (w_b[0], m_w_b[0], v_w_b[0]),
                           (w_out[0], m_w_out[0], v_w_out[0])], own_sq, r2_sq)
    (cctx_part,), (got_cctx,) = gather_cctx.wait(square[0][1])
    dact_ctx = _sum_devices(lax.dynamic_update_slice(got_cctx, cctx_part[None], (me, 0, 0)), "sum_cctx")

    view = {"c_ctx": (1, D), "final_norm_w": (1, D)}
    given = {"c_ctx": (c_ctx, m_c_ctx, v_c_ctx), "norm_w": (norm_w, m_norm_w, v_norm_w),
             "conv_b": (conv_b, m_conv_b, v_conv_b), "gn_w": (gn_w, m_gn_w, v_gn_w),
             "final_norm_w": (final_norm_w, m_final_norm_w, v_final_norm_w), "ada_b": (ada_b, m_ada_b, v_ada_b),
             "decay_logit": (decay_logit, m_decay_logit, v_decay_logit), "conv_w": (conv_w, m_conv_w, v_conv_w)}
    small_in = [tuple(a.reshape(view.get(name, a.shape)) for a in given[name]) for name in SMALL_PARAMS]
    conv_w_grad = lax.dynamic_slice(tot, (4, me * csh), (3, csh))
    small_out = _adam_small(tot, dact_ctx, conv_w_grad, tot[14, 0:2 * N_HEADS].reshape(2, N_HEADS), small_in)
    results.update({name: [o.reshape(given[name][0].shape) for o in outs_]
                    for name, outs_ in zip(SMALL_PARAMS, small_out)})
    for name, outs_ in zip(("w_a", "w_b", "w_out"), square):
        results[name] = [o[None] for o in outs_]
    results["ada_w"] = [o[None] for o in (g_ada_w, d_ada_w, nm_ada_w, nv_ada_w)]

    order = ("c_ctx", "norm_w", "ada_w", "ada_b", "w_in", "conv_w", "conv_b", "decay_logit", "gn_w",
             "w_a", "w_b", "w_out", "final_norm_w")
    outs = [results[name][kind] for kind in range(4) for name in order]
    return (tot[13, 0], grad_x[None], *outs)
```

```python
import math

import jax
import jax.numpy as jnp
from jax import lax
from jax.experimental import pallas as pl
from jax.experimental.pallas import tpu as pltpu

F32 = jnp.float32
BF16 = jnp.bfloat16
MESH = pl.DeviceIdType.MESH
_HBM_SPEC = pl.BlockSpec(memory_space=pltpu.HBM)
_SEM_SPEC = pl.BlockSpec(memory_space=pltpu.SEMAPHORE)
_EFFECT = pltpu.SideEffectType.DATAFLOW_SIDE_EFFECTING

N_DEV = 8
D = 1024
N_HEADS = 8
DK = 64
DV = 128
GRID_W = 64
ROPE_BASE = 10000.0
EPS = 1e-6
PW = 9 * D
WSH = PW // N_DEV
RSH = D // N_DEV
ADA_SH = 3 * D // N_DEV
TM = 256
RET_C = 256
HALO = 16
LANES = 128
VMEM_LIMIT = 60 * 1024 * 1024

ADAM_LR = 0.001
ADAM_B1 = 0.9
ADAM_B2 = 0.999
ADAM_EPS = 1e-08
ADAM_WD = 0.01
ADAM_STEP = 10

CB_H, CB_BG, CB_CG, CB_ZA, CB_QK, CB_V, CB_ZB, CB_GA, CB_GB = range(9)


def _cparams(**kw):
    return pltpu.CompilerParams(vmem_limit_bytes=VMEM_LIMIT, **kw)


def _dot(a, b):
    return jnp.dot(a, b, preferred_element_type=F32)


def _dot_nt(a, b):
    return lax.dot_general(a, b, (((1,), (1,)), ((), ())), preferred_element_type=F32)


def _dot_tn(a, b):
    return lax.dot_general(a, b, (((0,), (0,)), ((), ())), preferred_element_type=F32)


def _sigmoid(z):
    return 0.5 * jnp.tanh(0.5 * z) + 0.5


def _row_tile(rows):
    return TM if rows % TM == 0 else rows


def _coords():
    return lax.axis_index("x"), lax.axis_index("y"), lax.axis_index("c")


def _flip(v, bit):
    return 1 - v if bit else v


def _all_gather_small(blk, name):
    rows, cols = blk.shape

    def body(x_ref, out_ref, done, send_sems, recv_sems, local_sem):
        x, y, c = _coords()
        me = 4 * x + 2 * y + c
        mine = pltpu.make_async_copy(x_ref, out_ref.at[me], local_sem)
        mine.start()

        def copy(k, slot):
            peer = (_flip(x, k & 4), _flip(y, k & 2), _flip(c, k & 1))
            return pltpu.make_async_remote_copy(
                src_ref=x_ref, dst_ref=out_ref.at[slot], send_sem=send_sems.at[k - 1],
                recv_sem=recv_sems.at[k - 1], device_id=peer, device_id_type=MESH)

        for k in range(1, N_DEV):
            copy(k, me).start()
        for k in range(1, N_DEV):
            copy(k, jnp.bitwise_xor(me, k)).wait_recv()
        for k in range(1, N_DEV):
            copy(k, me).wait_send()
        mine.wait()
        done[...] = jnp.zeros((8, LANES), F32)

    vmem = pl.BlockSpec(memory_space=pltpu.VMEM)
    return pl.pallas_call(
        body, name=name,
        out_shape=[jax.ShapeDtypeStruct((N_DEV, rows, cols), blk.dtype), jax.ShapeDtypeStruct((8, LANES), F32)],
        in_specs=[vmem], out_specs=[vmem, vmem],
        scratch_shapes=[pltpu.SemaphoreType.DMA((N_DEV - 1,)), pltpu.SemaphoreType.DMA((N_DEV - 1,)),
                        pltpu.SemaphoreType.DMA],
    )(blk)


class _Absent:
    at = property(lambda self: self)

    def __getitem__(self, index):
        return self


class _Exchange:
    def __init__(self, name, srcs, land_shapes, n_copies, plan, after=None):
        self.name, self.plan, self.n_copies = name, plan, n_copies
        self.n_src, self.n_land = len(srcs), len(land_shapes)
        hbm = lambda a: pltpu.HBM(a.shape, a.dtype)
        n = self.n_src + self.n_land
        lands = [pltpu.with_memory_space_constraint(lax.empty(s.shape, s.dtype), pltpu.HBM) for s in land_shapes]
        srcs = [pltpu.with_memory_space_constraint(s, pltpu.HBM) for s in srcs]
        extra = [] if after is None else [after]

        def body(*refs):
            send_sems, recv_sems = refs[n + len(extra)], refs[n + len(extra) + 1]
            for cp in self._copies(refs, send_sems, recv_sems):
                cp.start()
            refs[-1][...] = jnp.zeros((8, LANES), F32)

        outs = pl.pallas_call(
            body, name=name + "_start",
            out_shape=(pltpu.SemaphoreType.DMA((n_copies,)), pltpu.SemaphoreType.DMA((n_copies,)),
                       *[hbm(a) for a in srcs], *[hbm(a) for a in lands], jax.ShapeDtypeStruct((8, LANES), F32)),
            in_specs=[_HBM_SPEC] * n + [pl.BlockSpec(memory_space=pl.ANY)] * len(extra),
            out_specs=(_SEM_SPEC, _SEM_SPEC, *[_HBM_SPEC] * n, pl.BlockSpec(memory_space=pltpu.VMEM)),
            input_output_aliases={i: 2 + i for i in range(n)},
            compiler_params=pltpu.CompilerParams(has_side_effects=_EFFECT),
        )(*srcs, *lands, *extra)
        self.send_sems, self.recv_sems = outs[0], outs[1]
        self.buffers = list(outs[2:2 + n])
        self.token = outs[-1]
        self.waited = 0

    def _copies(self, refs, send_sems, recv_sems, lo=0, hi=None):
        src_refs, land_refs = refs[:self.n_src], refs[self.n_src:self.n_src + self.n_land]
        planned = list(enumerate(self.plan(src_refs, land_refs)))[lo:hi]
        return [pltpu.make_async_remote_copy(src_ref=s, dst_ref=d, send_sem=send_sems.at[k], recv_sem=recv_sems.at[k],
                                             device_id=dev, device_id_type=MESH) for k, (s, d, dev) in planned]

    def wait(self, after, count=None, through=None):
        n = self.n_src + self.n_land
        lo = self.waited
        hi = self.n_copies if count is None else lo + count
        self.waited = hi
        through = list(range(n)) if through is None else list(through)
        m = len(through)

        def body(*refs):
            held = [_Absent()] * n
            for pos, ref in zip(through, refs[:m]):
                held[pos] = ref
            for cp in self._copies(held, refs[m], refs[m + 1], lo, hi):
                cp.wait_send()
                cp.wait_recv()

        outs = pl.pallas_call(
            body, name=f"{self.name}_wait{lo}" if lo or hi < self.n_copies else self.name + "_wait",
            out_shape=tuple(pltpu.HBM(self.buffers[p].shape, self.buffers[p].dtype) for p in through),
            in_specs=[_HBM_SPEC] * m + [_SEM_SPEC, _SEM_SPEC, pl.BlockSpec(memory_space=pl.ANY)],
            out_specs=tuple([_HBM_SPEC] * m),
            input_output_aliases={i: i for i in range(m)},
            compiler_params=pltpu.CompilerParams(has_side_effects=_EFFECT),
        )(*[self.buffers[p] for p in through], self.send_sems, self.recv_sems, after)
        for pos, out in zip(through, outs):
            self.buffers[pos] = out
        return list(self.buffers[:self.n_src]), list(self.buffers[self.n_src:])


def _pair_plan(src_index):
    def plan(srcs, lands):
        x, y, c = _coords()
        chip = 2 * x + y
        return [(srcs[0].at[src_index(j, chip, c)], lands[0].at[j], (x, y, 1 - c)) for j in range(4)]
    return plan


def _chips_plan(srcs, lands):
    x, y, c = _coords()
    return [(srcs[0].at[j - 1], lands[0].at[j - 1], (_flip(x, j & 2), _flip(y, j & 1), c)) for j in range(1, 4)]


def _peer(k):
    x, y, c = _coords()
    return _flip(x, k & 4), _flip(y, k & 2), _flip(c, k & 1)


def _send_plan(ks):
    def plan(srcs, lands):
        return [(srcs[0].at[0], lands[0].at[p], _peer(k)) for p, k in enumerate(ks)]
    return plan


def _forward_plan(moves):
    def plan(srcs, lands):
        return [(srcs[0].at[s], srcs[0].at[d], _peer(1)) for s, d in moves]
    return plan


def _gather_plan(srcs, lands):
    x, y, c = _coords()
    me = 4 * x + 2 * y + c
    return [(srcs[a], lands[a].at[me], (_flip(x, k & 4), _flip(y, k & 2), _flip(c, k & 1)))
            for a in range(len(srcs)) for k in range(1, N_DEV)]


def _pair_sum(grad, recv, owners, name):
    _, rows, cols = grad.shape
    tr = rows

    def body(own_ref, g_ref, r_ref, mine_ref, send_ref):
        j = pl.program_id(1)
        total = g_ref[...] + r_ref[...]

        @pl.when(j == 0)
        def _():
            mine_ref[...] = total

        @pl.when(j > 0)
        def _():
            send_ref[...] = total.astype(BF16)

    grid_spec = pltpu.PrefetchScalarGridSpec(
        num_scalar_prefetch=1, grid=(rows // tr, 4),
        in_specs=[pl.BlockSpec((1, tr, cols), lambda i, j, own: (own[j], i, 0)),
                  pl.BlockSpec((1, tr, cols), lambda i, j, own: (j, i, 0))],
        out_specs=[pl.BlockSpec((1, tr, cols), lambda i, j, own: (0, i, 0)),
                   pl.BlockSpec((1, tr, cols), lambda i, j, own: (jnp.maximum(j - 1, 0), i, 0))])
    return pl.pallas_call(
        body, name=name, grid_spec=grid_spec,
        out_shape=[jax.ShapeDtypeStruct((1, rows, cols), F32), jax.ShapeDtypeStruct((3, rows, cols), BF16)],
        compiler_params=_cparams(dimension_semantics=("arbitrary", "arbitrary")),
    )(owners, grad, recv)


def _modulation(c16, ada_w_sh, ada_b_sh, decay_logit):
    def body(c_ref, w_ref, b_ref, dl_ref, mod_ref, act_ref, lg_ref):
        cv = c_ref[...]
        act = cv * _sigmoid(cv)
        act_ref[...] = act
        mod_ref[...] = jnp.dot(act, w_ref[...], preferred_element_type=F32,
                               precision=lax.Precision.HIGHEST) + b_ref[...]
        z = dl_ref[...]
        lg_ref[...] = jnp.minimum(z, 0.0) - jnp.log(1.0 + jnp.exp(-jnp.abs(z)))

    return pl.pallas_call(
        body, name="modulation",
        out_shape=[jax.ShapeDtypeStruct((16, ADA_SH), F32), jax.ShapeDtypeStruct((16, D), F32),
                   jax.ShapeDtypeStruct(decay_logit.shape, F32)],
        compiler_params=_cparams(),
    )(c16, ada_w_sh, ada_b_sh, decay_logit)


def _adam_update(w, g, m, v):
    m2 = ADAM_B1 * m + (1.0 - ADAM_B1) * g
    v2 = ADAM_B2 * v + (1.0 - ADAM_B2) * (g * g)
    m_hat = m2 / (1.0 - ADAM_B1 ** ADAM_STEP)
    v_hat = v2 / (1.0 - ADAM_B2 ** ADAM_STEP)
    delta = -ADAM_LR * (m_hat / (jnp.sqrt(v_hat) + ADAM_EPS) + ADAM_WD * w)
    return delta, m2, v2


def _adam_sharded(w, m, v, own, recv, name):
    rows, cols = w.shape
    tr = _row_tile(rows)

    def body(w_ref, m_ref, v_ref, g0, g1, g2, g3, g_ref, d_ref, m_out, v_out):
        g = ((g0[0] + g1[0].astype(F32)) + g2[0].astype(F32)) + g3[0].astype(F32)
        delta, m2, v2 = _adam_update(w_ref[...], g, m_ref[...], v_ref[...])
        g_ref[...] = g
        d_ref[...] = delta
        m_out[...] = m2
        v_out[...] = v2

    flat = pl.BlockSpec((tr, cols), lambda i: (i, 0))
    part = lambda j: pl.BlockSpec((1, tr, cols), lambda i, j=j: (j, i, 0))
    return pl.pallas_call(
        body, name=name, grid=(rows // tr,),
        in_specs=[flat, flat, flat, part(0), part(0), part(1), part(2)],
        out_specs=[flat] * 4,
        out_shape=[jax.ShapeDtypeStruct((rows, cols), F32)] * 4,
        compiler_params=_cparams(dimension_semantics=("arbitrary",)),
    )(w, m, v, own, recv, recv, recv)


def _ada_backward(act_t, dmod16, dmodc8, ada_w_sh, m, v):
    def body(at_ref, dm_ref, dc_ref, w_ref, m_ref, v_ref, g_ref, d_ref, m_out, v_out, pc_ref):
        g = jnp.dot(at_ref[...], dm_ref[...], preferred_element_type=F32, precision=lax.Precision.HIGHEST)
        w = w_ref[...]
        delta, m2, v2 = _adam_update(w, g, m_ref[...], v_ref[...])
        g_ref[...] = g
        d_ref[...] = delta
        m_out[...] = m2
        v_out[...] = v2
        pc_ref[...] = lax.dot_general(dc_ref[...], w, (((1,), (1,)), ((), ())), preferred_element_type=F32,
                                      precision=lax.Precision.HIGHEST)

    return pl.pallas_call(
        body, name="ada_backward",
        out_shape=[jax.ShapeDtypeStruct((D, ADA_SH), F32)] * 4 + [jax.ShapeDtypeStruct((8, D), F32)],
        compiler_params=_cparams(),
    )(act_t, dmod16, dmodc8, ada_w_sh, m, v)


def _sum_devices(gathered, name):
    _, rows, cols = gathered.shape

    def body(g_ref, o_ref):
        acc = g_ref[0]
        for d in range(1, N_DEV):
            acc = acc + g_ref[d]
        o_ref[...] = acc

    return pl.pallas_call(body, name=name, out_shape=jax.ShapeDtypeStruct((rows, cols), F32),
                          compiler_params=_cparams())(gathered)


SMALL_PARAMS = ("c_ctx", "norm_w", "conv_b", "gn_w", "final_norm_w", "ada_b", "decay_logit", "conv_w")


def _adam_small(tot, dact_ctx, conv_w_grad, decay_grad, params):
    n = len(SMALL_PARAMS)

    def body(tot_ref, dact_ref, cwg_ref, dlg_ref, *refs):
        ins, outs = refs[:3 * n], refs[3 * n:]
        raw = {
            "c_ctx": dact_ref[0:1, :],
            "norm_w": tot_ref[1:2, :], "conv_b": tot_ref[2:3, :], "gn_w": tot_ref[3:4, :],
            "final_norm_w": tot_ref[0:1, :],
            "ada_b": jnp.concatenate([tot_ref[7 + r:8 + r, :] + tot_ref[10 + r:11 + r, :] for r in range(3)], axis=1),
            "decay_logit": dlg_ref[...],
            "conv_w": cwg_ref[...],
        }
        for k, name in enumerate(SMALL_PARAMS):
            lead = (0,) if len(ins[3 * k].shape) == 3 else ()
            w, m, v = (ref[lead + (...,)] for ref in ins[3 * k:3 * k + 3])
            g = raw[name]
            if name == "c_ctx":
                s = _sigmoid(w)
                g = g * (s * (1.0 + w * (1.0 - s)))
            elif name == "decay_logit":
                g = g * (1.0 - _sigmoid(w))
            delta, m2, v2 = _adam_update(w, g, m, v)
            for ref, val in zip(outs[4 * k:4 * k + 4], (g, delta, m2, v2)):
                ref[lead + (...,)] = val

    flat = [a for trio in params for a in trio]
    out_shape = [jax.ShapeDtypeStruct(trio[0].shape, F32) for trio in params for _ in range(4)]
    outs = pl.pallas_call(body, name="adam_small", out_shape=out_shape, compiler_params=_cparams())(
        tot, dact_ctx, conv_w_grad, decay_grad, *flat)
    return [outs[4 * k:4 * k + 4] for k in range(n)]


def _adam_square(params, own, recv):
    def body(own_ref, recv_ref, *refs):
        ins, outs = refs[:9], refs[9:]
        for k in range(3):
            rows = slice(k * RSH, (k + 1) * RSH)
            g = own_ref[0, rows, :]
            for j in range(3):
                g = g + recv_ref[j, rows, :].astype(F32)
            delta, m2, v2 = _adam_update(ins[3 * k][...], g, ins[3 * k + 1][...], ins[3 * k + 2][...])
            for ref, val in zip(outs[4 * k:4 * k + 4], (g, delta, m2, v2)):
                ref[...] = val

    flat = [a for trio in params for a in trio]
    outs = pl.pallas_call(body, name="adam_square", out_shape=[jax.ShapeDtypeStruct((RSH, D), F32)] * 12,
                          compiler_params=_cparams())(own, recv, *flat)
    return [outs[4 * k:4 * k + 4] for k in range(3)]


def _rope(t, cos, sin):
    lane = lax.broadcasted_iota(jnp.int32, (1, LANES), 1)
    first_half = jnp.bitwise_and(lane, DK // 2) == 0
    partner = jnp.where(first_half, pltpu.roll(t, LANES - DK // 2, 1), pltpu.roll(t, DK // 2, 1))
    return t * cos + partner * sin


class _Shards:
    def __init__(self, arrays, picks):
        self.arrays, self.picks = list(arrays), list(picks)

    def __add__(self, other):
        shift = len(self.arrays)
        return _Shards(self.arrays + other.arrays, self.picks + [(a + shift, s) for a, s in other.picks])


def _load_w_in(step, ids_ref, shards, w_refs, w_vmem, sems):
    @pl.when(step == 0)
    def _():
        copies = []
        for p, (a, slot) in enumerate(shards.picks):
            col = pl.multiple_of(ids_ref[p] * WSH, LANES)
            copies.append(pltpu.make_async_copy(w_refs[a].at[slot], w_vmem.at[:, pl.ds(col, WSH)], sems.at[p]))
        for cp in copies:
            cp.start()
        for cp in copies:
            cp.wait()


def _project_columns(xmb, w_ref, shard, cos, sin, p_ref):
    width = 2 * WSH
    q_lo, k_lo, k_hi = CB_QK * D, CB_QK * D + N_HEADS * DK, CB_V * D
    q_pair, k_pair = q_lo // width, k_lo // width
    assert (q_pair + 1) * width == k_lo and k_pair * width == k_lo and k_hi <= (k_pair + 1) * width
    mxu_cols = 2 * LANES

    def columns(rot_lo, rot_hi, scale):
        for a in range(0, width, mxu_cols):
            acc = _dot(xmb, w_ref[:, a:a + mxu_cols])
            for col in range(a, a + mxu_cols, LANES):
                piece = acc[:, col - a:col - a + LANES]
                if rot_lo <= col < rot_hi:
                    piece = _rope(piece if scale == 1.0 else piece * scale, cos, sin)
                p_ref[:, col:col + LANES] = piece.astype(BF16)

    @pl.when(shard == q_pair)
    def _():
        columns(q_lo - q_pair * width, width, 1.0)

    @pl.when(shard == k_pair)
    def _():
        columns(0, k_hi - k_lo, DK ** -0.5)

    @pl.when(jnp.logical_and(shard != q_pair, shard != k_pair))
    def _():
        columns(0, 0, 1.0)


_AFTER_SPEC = pl.BlockSpec(memory_space=pl.ANY)


def _prenorm(x, ctx, modx, modc, norm_w, after):
    tp = 4 * TM
    seq, ctx_len = x.shape[0], ctx.shape[0]
    n_x_tiles = seq // tp
    lext = seq + ctx_len
    assert seq % tp == 0 and ctx_len <= tp and tp % GRID_W == 0 and tp // GRID_W <= 16
    grid_rows = tp // GRID_W

    def body(x_ref, ctx_ref, mx_ref, mc_ref, nw_ref, after_ref, xm_ref, xmt_ref, cos_ref, sin_ref, col_cos, col_sin):
        del after_ref
        i = pl.program_id(0)
        is_ctx = i >= n_x_tiles
        ctx_rows = jnp.concatenate([ctx_ref[...], jnp.zeros((tp - ctx_len, D), F32)], axis=0)
        x = jnp.where(is_ctx, ctx_rows, x_ref[...])
        r = lax.rsqrt(jnp.mean(x * x, axis=-1, keepdims=True) + EPS)
        sh = jnp.where(is_ctx, mc_ref[0:1, :], mx_ref[0:1, :])
        sc = jnp.where(is_ctx, mc_ref[1:2, :], mx_ref[1:2, :])
        xm = (x * r * nw_ref[...]) * (1.0 + sc) + sh
        xm_ref[...] = xm.astype(BF16)
        xmt_ref[...] = xm.T.astype(BF16)
        lane = lax.broadcasted_iota(jnp.int32, (1, LANES), 1)
        n_freq = DK // 4
        inv = jnp.exp((lane % n_freq).astype(F32) * (-math.log(ROPE_BASE) / n_freq))
        by_column = (lane % (2 * n_freq)) >= n_freq
        sign = jnp.where((lane % DK) < DK // 2, -1.0, 1.0)

        @pl.when(i == 0)
        def _():
            col = (lax.broadcasted_iota(jnp.int32, (tp, 1), 0) % GRID_W).astype(F32) * inv
            col_cos[...] = jnp.cos(col)
            col_sin[...] = jnp.sin(col) * sign

        row = (i * grid_rows + lax.broadcasted_iota(jnp.int32, (16, 1), 0)).astype(F32) * inv
        row_cos, row_sin = jnp.cos(row), jnp.sin(row) * sign
        tall = lambda t: jnp.concatenate(
            [jnp.broadcast_to(t[g:g + 1, :], (GRID_W, LANES)) for g in range(grid_rows)], axis=0)
        cos_ref[...] = jnp.where(is_ctx, 1.0, jnp.where(by_column, col_cos[...], tall(row_cos)))
        sin_ref[...] = jnp.where(is_ctx, 0.0, jnp.where(by_column, col_sin[...], tall(row_sin)))

    full = lambda a: pl.BlockSpec(a.shape, lambda i: (0,) * a.ndim)
    row = lambda width: pl.BlockSpec((tp, width), lambda i: (i, 0))
    return pl.pallas_call(
        body, name="prenorm", grid=(n_x_tiles + 1,),
        in_specs=[pl.BlockSpec((tp, D), lambda i: (jnp.minimum(i, n_x_tiles - 1), 0)), full(ctx),
                  full(modx), full(modc), full(norm_w), _AFTER_SPEC],
        out_specs=[row(D), pl.BlockSpec((D, tp), lambda i: (0, i)), row(LANES), row(LANES)],
        out_shape=[jax.ShapeDtypeStruct((lext, D), BF16), jax.ShapeDtypeStruct((D, lext), BF16),
                   jax.ShapeDtypeStruct((lext, LANES), F32), jax.ShapeDtypeStruct((lext, LANES), F32)],
        scratch_shapes=[pltpu.VMEM((tp, LANES), F32), pltpu.VMEM((tp, LANES), F32)],
        compiler_params=_cparams(dimension_semantics=("arbitrary",)),
    )(x, ctx, modx, modc, norm_w, after)


def _in_projection(p_ext, xm, cos_t, sin_t, w, pairs, me, name):
    lext = xm.shape[0]
    tr = lext // 4
    meta = []
    for (slot_a, k_a), (slot_b, k_b) in pairs:
        assert k_a ^ k_b == 1
        dev_a = jnp.bitwise_xor(me, k_a)
        a_low = dev_a % 2 == 0
        meta += [jnp.where(a_low, slot_a, slot_b), jnp.where(a_low, slot_b, slot_a), dev_a // 2]
    meta = jnp.stack(meta).astype(jnp.int32)

    def body(meta_ref, xm_ref, cos_ref, sin_ref, w_hbm, *rest):
        p_ref, w_ref, sems = rest[-3:]
        j, i = pl.program_id(0), pl.program_id(1)

        @pl.when(i == 0)
        def _():
            copies = [pltpu.make_async_copy(w_hbm.at[meta_ref[3 * j + h]], w_ref.at[:, h * WSH:(h + 1) * WSH],
                                            sems.at[h]) for h in range(2)]
            for cp in copies:
                cp.start()
            for cp in copies:
                cp.wait()

        _project_columns(xm_ref[...], w_ref, meta_ref[3 * j + 2], cos_ref[...], sin_ref[...], p_ref)

    row = lambda width: pl.BlockSpec((tr, width), lambda j, i, meta_: (i, 0))
    grid_spec = pltpu.PrefetchScalarGridSpec(
        num_scalar_prefetch=1, grid=(len(pairs), lext // tr),
        in_specs=[row(D), row(LANES), row(LANES), pl.BlockSpec(memory_space=pl.ANY)]
        + ([] if p_ext is None else [pl.BlockSpec(memory_space=pl.ANY)]),
        out_specs=pl.BlockSpec((tr, 2 * WSH), lambda j, i, meta_: (i, meta_[3 * j + 2])),
        scratch_shapes=[pltpu.VMEM((D, 2 * WSH), BF16), pltpu.SemaphoreType.DMA((2,))])
    return pl.pallas_call(
        body, name=name, grid_spec=grid_spec,
        out_shape=jax.ShapeDtypeStruct((lext, PW), BF16),
        input_output_aliases={} if p_ext is None else {5: 0},
        compiler_params=_cparams(dimension_semantics=("arbitrary", "arbitrary")),
    )(meta, xm, cos_t, sin_t, w, *([] if p_ext is None else [p_ext]))


def _decay_tables(lgf, lgb, n):
    i = lax.broadcasted_iota(jnp.int32, (n, 1), 0).astype(F32)
    return dict(i=i, k_f=jnp.exp(lgf * (n - 1.0 - i)), k_b=jnp.exp(lgb * i),
                q_f=jnp.exp(lgf * (i + 1.0)), q_b=jnp.exp(lgb * (n - i)))


def _decay_matrix(lgf, lgb, n):
    ii = lax.broadcasted_iota(jnp.int32, (n, n), 0)
    jj = lax.broadcasted_iota(jnp.int32, (n, n), 1)
    diff = (ii - jj).astype(F32)
    low = jnp.exp(lgf * jnp.maximum(diff, 0.0))
    up = jnp.exp(lgb * jnp.maximum(-diff, 0.0))
    return jnp.where(diff > 0, low, jnp.where(diff < 0, up, 2.0)), diff


def _cat_lanes(a, b):
    return jnp.concatenate([a.astype(BF16), b.astype(BF16)], axis=1)


def _retention_forward(p_ext, lg, seq, ctx_len):
    lext = seq + ctx_len
    n_chunks = seq // RET_C
    C = RET_C

    def body(lg_ref, q_ref, k_ref, v_ref, rn_ref, rstd_ref, kv_scr, sf_scr, sb_scr):
        pair = pl.program_id(0)
        lane = lax.broadcasted_iota(jnp.int32, (1, LANES), 1)
        heads = range(2)
        hmask = [(lane // DK == hh).astype(F32) for hh in heads]
        lgf = [lg_ref[0, 2 * pair + hh] for hh in heads]
        lgb = [lg_ref[1, 2 * pair + hh] for hh in heads]
        vs = [slice(hh * DV, (hh + 1) * DV) for hh in heads]
        t = [_decay_tables(lgf[hh], lgb[hh], C) for hh in heads]
        kc_all = k_ref[seq:lext, :].astype(F32)
        s0 = []
        for hh in heads:
            tc = _decay_tables(lgf[hh], lgb[hh], ctx_len)
            kc = kc_all * hmask[hh]
            s0.append(_dot_tn(_cat_lanes(kc * tc["k_f"], kc * tc["k_b"]), v_ref[seq:lext, vs[hh]]))

        def increments(c, carry):
            rows = pl.ds(pl.multiple_of(c * C, C), C)
            k_all = k_ref[rows, :].astype(F32)
            for hh in heads:
                k = k_all * hmask[hh]
                kv_scr[hh, c] = _dot_tn(_cat_lanes(k * t[hh]["k_f"], k * t[hh]["k_b"]), v_ref[rows, vs[hh]])
            return carry

        lax.fori_loop(0, n_chunks, increments, 0, unroll=2)
        gf_c = [jnp.exp(lgf[hh] * C) for hh in heads]
        gb_c = [jnp.exp(lgb[hh] * C) for hh in heads]

        def scan_f(c, s):
            for hh in heads:
                sf_scr[hh, c] = s[hh]
            return tuple(gf_c[hh] * s[hh] + kv_scr[hh, c, 0:LANES, :] for hh in heads)

        def scan_b(n, s):
            c = n_chunks - 1 - n
            for hh in heads:
                sb_scr[hh, c] = s[hh]
            return tuple(gb_c[hh] * s[hh] + kv_scr[hh, c, LANES:2 * LANES, :] for hh in heads)

        lax.fori_loop(0, n_chunks, scan_f, tuple(s0[hh][0:LANES] for hh in heads))
        lax.fori_loop(0, n_chunks, scan_b, tuple(s0[hh][LANES:2 * LANES] for hh in heads))
        dmat = [_decay_matrix(lgf[hh], lgb[hh], C)[0] for hh in heads]

        def outputs(c, carry):
            rows = pl.ds(pl.multiple_of(c * C, C), C)
            q_all = q_ref[rows, :].astype(F32)
            k = k_ref[rows, :]
            for hh in heads:
                q = q_all * hmask[hh]
                v = v_ref[rows, vs[hh]]
                s = _dot_nt(q.astype(BF16), k)
                inner = _dot((s * dmat[hh]).astype(BF16), v)
                states = jnp.concatenate([sf_scr[hh, c], sb_scr[hh, c]], axis=0).astype(BF16)
                ret = inner + _dot(_cat_lanes(q * t[hh]["q_f"], q * t[hh]["q_b"]), states)
                mu = jnp.mean(ret, axis=-1, keepdims=True)
                cen = ret - mu
                rstd = lax.rsqrt(jnp.mean(cen * cen, axis=-1, keepdims=True) + EPS)
                rn_ref[rows, vs[hh]] = cen * rstd
                rstd_ref[rows, vs[hh]] = jnp.broadcast_to(rstd, (C, DV))
            return carry

        lax.fori_loop(0, n_chunks, outputs, 0, unroll=2)

    qk0 = CB_QK * D // LANES
    return pl.pallas_call(
        body, name="retention_forward", grid=(N_HEADS // 2,),
        in_specs=[pl.BlockSpec(memory_space=pltpu.SMEM),
                  pl.BlockSpec((lext, LANES), lambda g: (0, qk0 + g)),
                  pl.BlockSpec((lext, LANES), lambda g: (0, qk0 + N_HEADS // 2 + g)),
                  pl.BlockSpec((lext, 2 * DV), lambda g: (0, CB_V * D // (2 * DV) + g))],
        out_specs=[pl.BlockSpec((seq, 2 * DV), lambda g: (0, g))] * 2,
        out_shape=[jax.ShapeDtypeStruct((seq, D), F32)] * 2,
        scratch_shapes=[pltpu.VMEM((2, n_chunks, 2 * LANES, DV), F32), pltpu.VMEM((2, n_chunks, LANES, DV), F32),
                        pltpu.VMEM((2, n_chunks, LANES, DV), F32)],
        compiler_params=_cparams(dimension_semantics=("arbitrary",)),
    )(lg, p_ext, p_ext, p_ext)


def _retention_backward(p_ext, dret, lg, cos_t, sin_t, seq, ctx_len, after):
    lext = seq + ctx_len
    n_chunks = seq // RET_C
    C = RET_C

    def body(lg_ref, q_ref, k_ref, v_ref, do_ref, cos_ref, sin_ref, after_ref, dq_ref, dk_ref, dv_ref, dlg_ref,
             kv_scr, g_scr, sf_scr, sb_scr, gfn_scr, gbp_scr):
        pair = pl.program_id(0)
        lane = lax.broadcasted_iota(jnp.int32, (1, LANES), 1)
        heads = range(2)
        hmask = [(lane // DK == hh).astype(F32) for hh in heads]
        lgf = [lg_ref[0, 2 * pair + hh] for hh in heads]
        lgb = [lg_ref[1, 2 * pair + hh] for hh in heads]
        vs = [slice(hh * DV, (hh + 1) * DV) for hh in heads]
        t = [_decay_tables(lgf[hh], lgb[hh], C) for hh in heads]
        tc = [_decay_tables(lgf[hh], lgb[hh], ctx_len) for hh in heads]
        kc_all = k_ref[seq:lext, :].astype(F32)
        kc = [kc_all * hmask[hh] for hh in heads]
        vc = [v_ref[seq:lext, vs[hh]] for hh in heads]
        kc_cat = [_cat_lanes(kc[hh] * tc[hh]["k_f"], kc[hh] * tc[hh]["k_b"]) for hh in heads]
        s0 = [_dot_tn(kc_cat[hh], vc[hh]) for hh in heads]

        def increments(c, carry):
            rows = pl.ds(pl.multiple_of(c * C, C), C)
            k_all = k_ref[rows, :].astype(F32)
            q_all = q_ref[rows, :].astype(F32)
            for hh in heads:
                k, q = k_all * hmask[hh], q_all * hmask[hh]
                kv_scr[hh, c] = _dot_tn(_cat_lanes(k * t[hh]["k_f"], k * t[hh]["k_b"]), v_ref[rows, vs[hh]])
                g_scr[hh, c] = _dot_tn(_cat_lanes(q * t[hh]["q_f"], q * t[hh]["q_b"]), do_ref[rows, vs[hh]])
            return carry

        lax.fori_loop(0, n_chunks, increments, 0, unroll=2)
        gf_c = [jnp.exp(lgf[hh] * C) for hh in heads]
        gb_c = [jnp.exp(lgb[hh] * C) for hh in heads]

        def scan_f(c, s):
            for hh in heads:
                sf_scr[hh, c] = s[hh]
            return tuple(gf_c[hh] * s[hh] + kv_scr[hh, c, 0:LANES, :] for hh in heads)

        def scan_b(n, s):
            c = n_chunks - 1 - n
            for hh in heads:
                sb_scr[hh, c] = s[hh]
            return tuple(gb_c[hh] * s[hh] + kv_scr[hh, c, LANES:2 * LANES, :] for hh in heads)

        def scan_gf(n, carry):
            c = n_chunks - 1 - n
            for hh in heads:
                gfn_scr[hh, c] = carry[hh]
            return tuple(g_scr[hh, c, 0:LANES, :] + gf_c[hh] * carry[hh] for hh in heads)

        def scan_gb(c, carry):
            for hh in heads:
                gbp_scr[hh, c] = carry[hh]
            return tuple(g_scr[hh, c, LANES:2 * LANES, :] + gb_c[hh] * carry[hh] for hh in heads)

        lax.fori_loop(0, n_chunks, scan_f, tuple(s0[hh][0:LANES] for hh in heads))
        lax.fori_loop(0, n_chunks, scan_b, tuple(s0[hh][LANES:2 * LANES] for hh in heads))
        zero_state = jnp.zeros((LANES, DV), F32)
        gf_first = lax.fori_loop(0, n_chunks, scan_gf, (zero_state, zero_state))
        gb_last = lax.fori_loop(0, n_chunks, scan_gb, (zero_state, zero_state))

        dmat, w_f, w_b = [], [], []
        for hh in heads:
            dm, diff = _decay_matrix(lgf[hh], lgb[hh], C)
            dmat.append(dm)
            w_f.append(jnp.where(diff > 0, diff * dm, 0.0))
            w_b.append(jnp.where(diff < 0, -diff * dm, 0.0))

        def total(a):
            rows_, width = a.shape
            part = jnp.sum(a.reshape(rows_ // 8, 8, width), axis=0)
            return part[:, 0:LANES] + part[:, LANES:2 * LANES] if width == 2 * LANES else part

        def chunk_head(hh, c, rows, q_all, k_all, dlgf, dlgb):
            th = t[hh]
            qm = q_all * hmask[hh]
            km = k_all * hmask[hh]
            qb, kb = qm.astype(BF16), km.astype(BF16)
            v = v_ref[rows, vs[hh]]
            do = do_ref[rows, vs[hh]]
            s = _dot_nt(qb, kb)
            dsd = _dot_nt(do, v)
            ds = (dsd * dmat[hh]).astype(BF16)
            dq_in = _dot(ds, kb)
            dk_in = _dot_tn(ds, qb)
            dv_in = _dot_tn((s * dmat[hh]).astype(BF16), do)
            prod = s * dsd
            dlgf = dlgf + total(prod * w_f[hh])
            dlgb = dlgb + total(prod * w_b[hh])
            sf, sb = sf_scr[hh, c], sb_scr[hh, c]
            states = jnp.concatenate([sf, sb], axis=0).astype(BF16)
            dqc = _dot_nt(do, states)
            dqf = dqc[:, 0:LANES] * th["q_f"]
            dqb = dqc[:, LANES:2 * LANES] * th["q_b"]
            dq = (dq_in + dqf + dqb) * hmask[hh]
            dlgf = dlgf + total((th["i"] + 1.0) * (qm * dqf))
            dlgb = dlgb + total((C - th["i"]) * (qm * dqb))
            gfn, gbp = gfn_scr[hh, c], gbp_scr[hh, c]
            gstates = jnp.concatenate([gfn, gbp], axis=0).astype(BF16)
            dkc = _dot_nt(v, gstates)
            dkf = dkc[:, 0:LANES] * th["k_f"]
            dkb = dkc[:, LANES:2 * LANES] * th["k_b"]
            dk = dk_in + dkf + dkb
            dlgf = dlgf + total((C - 1.0 - th["i"]) * (km * dkf)) + C * gf_c[hh] * total(gfn * sf)
            dlgb = dlgb + total(th["i"] * (km * dkb)) + C * gb_c[hh] * total(gbp * sb)
            dv = dv_in + _dot(_cat_lanes(km * th["k_f"], km * th["k_b"]), gstates)
            dv_ref[rows, vs[hh]] = dv.astype(BF16)
            return dq, dk, dlgf, dlgb

        def chunk(c, carry):
            rows = pl.ds(pl.multiple_of(c * C, C), C)
            q_all = q_ref[rows, :].astype(F32)
            k_all = k_ref[rows, :].astype(F32)
            dq0, dk0, f0, b0 = chunk_head(0, c, rows, q_all, k_all, carry[0], carry[1])
            dq1, dk1, f1, b1 = chunk_head(1, c, rows, q_all, k_all, carry[2], carry[3])
            cos, sin = cos_ref[rows, :], sin_ref[rows, :]
            dq_ref[rows, :] = _rope(dq0 + dq1, cos, -sin).astype(BF16)
            dk_ref[rows, :] = (_rope(dk0 + dk1, cos, -sin) * (DK ** -0.5)).astype(BF16)
            return f0, b0, f1, b1

        zero = jnp.zeros((8, LANES), F32)
        sums = lax.fori_loop(0, n_chunks, chunk, (zero, zero, zero, zero), unroll=2)

        dlg = []
        dk_ctx = jnp.zeros((ctx_len, LANES), F32)
        for hh in heads:
            g0 = jnp.concatenate([gf_first[hh], gb_last[hh]], axis=0).astype(BF16)
            dkcc = _dot_nt(vc[hh], g0)
            dkcf = dkcc[:, 0:LANES] * tc[hh]["k_f"]
            dkcb = dkcc[:, LANES:2 * LANES] * tc[hh]["k_b"]
            dlgf = sums[2 * hh] + total((ctx_len - 1.0 - tc[hh]["i"]) * (kc[hh] * dkcf))
            dlgb = sums[2 * hh + 1] + total(tc[hh]["i"] * (kc[hh] * dkcb))
            dk_ctx = dk_ctx + (dkcf + dkcb) * (DK ** -0.5)
            dv_ref[seq:lext, vs[hh]] = _dot(kc_cat[hh], g0).astype(BF16)
            dlg += [jnp.sum(jnp.sum(a, axis=1, keepdims=True), axis=0, keepdims=True) for a in (dlgf, dlgb)]
        dk_ref[seq:lext, :] = dk_ctx.astype(BF16)
        dq_ref[seq:lext, :] = jnp.zeros((ctx_len, LANES), BF16)

        lane8 = lax.broadcasted_iota(jnp.int32, (8, LANES), 1)
        out = jnp.zeros((8, LANES), F32)
        for n, val in enumerate(dlg):
            out = jnp.where(lane8 == n, val, out)
        dlg_ref[0] = out

    qk0 = CB_QK * D // LANES
    q_spec = pl.BlockSpec((lext, LANES), lambda g: (0, qk0 + g))
    k_spec = pl.BlockSpec((lext, LANES), lambda g: (0, qk0 + N_HEADS // 2 + g))
    v_spec = pl.BlockSpec((lext, 2 * DV), lambda g: (0, CB_V * D // (2 * DV) + g))
    table = pl.BlockSpec((lext, LANES), lambda g: (0, 0))
    state = pltpu.VMEM((2, n_chunks, LANES, DV), F32)
    return pl.pallas_call(
        body, name="retention_backward", grid=(N_HEADS // 2,),
        in_specs=[pl.BlockSpec(memory_space=pltpu.SMEM), q_spec, k_spec, v_spec,
                  pl.BlockSpec((seq, 2 * DV), lambda g: (0, g)), table, table, _AFTER_SPEC],
        out_specs=[pl.BlockSpec((lext, LANES), lambda g: (0, g)), pl.BlockSpec((lext, LANES), lambda g: (0, g)),
                   pl.BlockSpec((lext, 2 * DV), lambda g: (0, g)), pl.BlockSpec((1, 8, LANES), lambda g: (g, 0, 0))],
        out_shape=[jax.ShapeDtypeStruct((lext, N_HEADS * DK), BF16), jax.ShapeDtypeStruct((lext, N_HEADS * DK), BF16),
                   jax.ShapeDtypeStruct((lext, D), BF16), jax.ShapeDtypeStruct((N_HEADS // 2, 8, LANES), F32)],
        scratch_shapes=[pltpu.VMEM((2, n_chunks, 2 * LANES, DV), F32), pltpu.VMEM((2, n_chunks, 2 * LANES, DV), F32),
                        state, state, state, state],
        compiler_params=_cparams(dimension_semantics=("arbitrary",)),
    )(lg, p_ext, p_ext, p_ext, dret, cos_t, sin_t, after)


class _ColumnWriter:
    def __init__(self, dst_hbm, stage, sems, pieces, n_steps):
        self.dst, self.stage, self.sems, self.pieces, self.n_steps = dst_hbm, stage, sems, pieces, n_steps

    def _copies(self, slot, tile):
        rows = pl.ds(pl.multiple_of(tile * TM, TM), TM)
        return [pltpu.make_async_copy(self.stage.at[slot, :, pl.ds(src, width)], self.dst.at[rows, pl.ds(dst, width)],
                                      self.sems.at[slot, k]) for k, (src, dst, width) in enumerate(self.pieces)]

    def slot_for(self, step):
        slot = step % 2

        @pl.when(step >= 2)
        def _():
            for cp in self._copies(slot, step - 2):
                cp.wait()

        return slot

    def send(self, step, slot):
        for cp in self._copies(slot, step):
            cp.start()

        @pl.when(step == self.n_steps - 1)
        def _():
            if self.n_steps >= 2:
                for cp in self._copies(1 - slot, step - 1):
                    cp.wait()
            for cp in self._copies(slot, step):
                cp.wait()


def _merge(p_ext, rn, rstd, x, target, w_a, w_b, w_out, vecs, seq):
    n_tiles = seq // TM
    hb = TM // HALO
    dp_pieces = [(0, CB_BG * D, D), (D, CB_ZA * D, D), (2 * D, CB_ZB * D, 3 * D)]

    def body(h_ref, bg_ref, cg_ref, za_ref, zb_ref, ga_ref, gb_ref, hp_ref, hn_ref, cp_ref, cn_ref,
             rn_ref, rstd_ref, x_ref, t_ref, wa_ref, wb_ref, wo_ref, vec_ref,
             dx1_ref, dp_hbm, dconv_ref, dret_ref, at_ref, b_ref, part_ref, stage, dp_sems):
        i = pl.program_id(0)
        writer = _ColumnWriter(dp_hbm, stage, dp_sems, dp_pieces, n_tiles)
        slot = writer.slot_for(i)
        dpa_ref = stage.at[slot]
        f = lambda ref: ref[...].astype(F32)
        h, bg, cg, za, zb, ga, gb = f(h_ref), f(bg_ref), f(cg_ref), f(za_ref), f(zb_ref), f(ga_ref), f(gb_ref)
        gx, w0, w1, w2 = vec_ref[0:1, :], vec_ref[1:2, :], vec_ref[2:3, :], vec_ref[3:4, :]
        cb, gnw, fw = vec_ref[4:5, :], vec_ref[5:6, :], vec_ref[6:7, :]
        u = cg * h
        row = lax.broadcasted_iota(jnp.int32, (TM, 1), 0)
        u_prev = (f(cp_ref) * f(hp_ref))[HALO - 1:HALO, :]
        u_next = (f(cn_ref) * f(hn_ref))[0:1, :]
        u_prev = jnp.where(i == 0, 0.0, u_prev)
        u_next = jnp.where(i == n_tiles - 1, 0.0, u_next)
        u_up = jnp.where(row == 0, u_prev, pltpu.roll(u, 1, 0))
        u_dn = jnp.where(row == TM - 1, u_next, pltpu.roll(u, TM - 1, 0))
        conv = w0 * u_up + w1 * u + w2 * u_dn + cb
        sza = _sigmoid(za)
        silu_za = za * sza
        a_act = silu_za * bg * conv
        rn = rn_ref[...]
        szb = _sigmoid(zb)
        silu_zb = zb * szb
        rg = rn * gnw
        b_act = silu_zb * rg
        y_a = _dot(a_act.astype(BF16), wa_ref[...])
        y_b = _dot(b_act.astype(BF16), wb_ref[...])
        sga, sgb = _sigmoid(ga), _sigmoid(gb)
        mix = sga * y_a + sgb * y_b
        y = _dot(mix.astype(BF16), wo_ref[...])
        x1 = x_ref[...] + gx * y
        r1 = lax.rsqrt(jnp.mean(x1 * x1, axis=-1, keepdims=True) + EPS)
        xh1 = x1 * r1
        err = xh1 * fw - t_ref[...]
        loss = jnp.sum(jnp.sum(err * err, axis=1, keepdims=True), axis=0, keepdims=True) * (0.5 / D)
        dxh = err * (fw * (1.0 / D))
        dx1 = r1 * (dxh - xh1 * jnp.mean(dxh * xh1, axis=-1, keepdims=True))
        dx1_ref[...] = dx1
        dy = (dx1 * gx).astype(BF16)
        dmix = _dot_nt(dy, wo_ref[...])
        dya_f, dyb_f = dmix * sga, dmix * sgb
        dya, dyb = dya_f.astype(BF16), dyb_f.astype(BF16)
        da = _dot_nt(dya, wa_ref[...])
        db = _dot_nt(dyb, wb_ref[...])
        da_silu = da * silu_za
        dpa_ref[:, 0:D] = (da_silu * conv).astype(BF16)
        dpa_ref[:, D:2 * D] = (da * bg * conv * (sza + silu_za * (1.0 - sza))).astype(BF16)
        dpa_ref[:, 2 * D:3 * D] = (db * rg * (szb + silu_zb * (1.0 - szb))).astype(BF16)
        dpa_ref[:, 3 * D:4 * D] = (dya_f * y_a * (1.0 - sga)).astype(BF16)
        dpa_ref[:, 4 * D:5 * D] = (dyb_f * y_b * (1.0 - sgb)).astype(BF16)
        dconv_ref[...] = (da_silu * bg).astype(BF16)
        drn_n = db * silu_zb
        drn = drn_n * gnw
        rstd = rstd_ref[...]
        for hd in range(N_HEADS):
            sl = slice(hd * DV, (hd + 1) * DV)
            dh_, rh = drn[:, sl], rn[:, sl]
            m1 = jnp.mean(dh_, axis=-1, keepdims=True)
            m2 = jnp.mean(dh_ * rh, axis=-1, keepdims=True)
            dret_ref[:, sl] = (rstd[:, sl] * (dh_ - m1 - rh * m2)).astype(BF16)
        at_ref[0] = a_act.T.astype(BF16)
        at_ref[1] = b_act.T.astype(BF16)
        at_ref[2] = mix.T.astype(BF16)
        b_ref[0] = dya
        b_ref[1] = dyb
        b_ref[2] = dy

        @pl.when(i == 0)
        def _():
            part_ref[...] = jnp.zeros((8, D), F32)

        part_ref[0:1, :] += jnp.sum(err * xh1, axis=0, keepdims=True) * (1.0 / D)
        part_ref[1:2, :] += jnp.sum(dx1 * y, axis=0, keepdims=True)
        part_ref[2:3, :] += jnp.sum(drn_n * rn, axis=0, keepdims=True)
        part_ref[3:4, :] += jnp.broadcast_to(loss, (1, D))
        writer.send(i, slot)

    col = lambda cb_: pl.BlockSpec((TM, D), lambda i, cb_=cb_: (i, cb_))
    prev = lambda cb_: pl.BlockSpec((HALO, D), lambda i, cb_=cb_: (jnp.maximum(i * hb - 1, 0), cb_))
    nxt = lambda cb_: pl.BlockSpec((HALO, D), lambda i, cb_=cb_: (jnp.minimum((i + 1) * hb, n_tiles * hb - 1), cb_))
    tile = pl.BlockSpec((TM, D), lambda i: (i, 0))
    full = lambda a: pl.BlockSpec(a.shape, lambda i: (0,) * a.ndim, pipeline_mode=pl.Buffered(1))
    return pl.pallas_call(
        body, name="merge", grid=(n_tiles,),
        in_specs=[col(CB_H), col(CB_BG), col(CB_CG), col(CB_ZA), col(CB_ZB), col(CB_GA), col(CB_GB),
                  prev(CB_H), nxt(CB_H), prev(CB_CG), nxt(CB_CG),
                  tile, tile, tile, tile, full(w_a), full(w_b), full(w_out), full(vecs)],
        out_specs=[tile, pl.BlockSpec(memory_space=pl.ANY), tile, tile,
                   pl.BlockSpec((3, D, TM), lambda i: (0, 0, i)), pl.BlockSpec((3, TM, D), lambda i: (0, i, 0)),
                   pl.BlockSpec((8, D), lambda i: (0, 0))],
        out_shape=[jax.ShapeDtypeStruct((seq, D), F32), jax.ShapeDtypeStruct((p_ext.shape[0], PW), BF16),
                   jax.ShapeDtypeStruct((seq, D), BF16), jax.ShapeDtypeStruct((seq, D), BF16),
                   jax.ShapeDtypeStruct((3, D, seq), BF16), jax.ShapeDtypeStruct((3, seq, D), BF16),
                   jax.ShapeDtypeStruct((8, D), F32)],
        scratch_shapes=[pltpu.VMEM((2, TM, 5 * D), BF16), pltpu.SemaphoreType.DMA((2, len(dp_pieces)))],
        compiler_params=_cparams(dimension_semantics=("arbitrary",)),
    )(p_ext, p_ext, p_ext, p_ext, p_ext, p_ext, p_ext, p_ext, p_ext, p_ext, p_ext,
      rn, rstd, x, target, w_a, w_b, w_out, vecs)


def _conv_backward(p_ext, dconv, dp, dq, dk, dv, vecs, seq, after):
    n_tiles = seq // TM
    hb = TM // HALO
    qk = N_HEADS * DK
    pieces = [(0, CB_H * D, D), (D, CB_CG * D, D), (2 * D, CB_QK * D, 2 * D)]
    zero_pieces = [(CB_BG * D, D), (CB_ZA * D, D), (CB_ZB * D, 3 * D)]

    def body(h_ref, cg_ref, dc_ref, dcp_ref, dcn_ref, dq_ref, dk_ref, dv_ref, vec_ref, dp_in, after_ref, dp_hbm, part_ref,
             stage, sems, zeros, zero_sems):
        del dp_in, after_ref
        i = pl.program_id(0)
        writer = _ColumnWriter(dp_hbm, stage, sems, pieces, n_tiles + 1)
        slot = writer.slot_for(i)
        out = stage.at[slot]
        out[:, 2 * D:2 * D + qk] = dq_ref[...]
        out[:, 2 * D + qk:3 * D] = dk_ref[...]
        out[:, 3 * D:4 * D] = dv_ref[...]

        @pl.when(i == 0)
        def _():
            part_ref[...] = jnp.zeros((8, D), F32)

        @pl.when(i == n_tiles)
        def _():
            out[:, 0:2 * D] = jnp.zeros((TM, 2 * D), BF16)
            zeros[...] = jnp.zeros(zeros.shape, BF16)
            rows = pl.ds(n_tiles * TM, TM)
            fills = [pltpu.make_async_copy(zeros.at[:, pl.ds(0, width)], dp_hbm.at[rows, pl.ds(dst, width)],
                                           zero_sems.at[k]) for k, (dst, width) in enumerate(zero_pieces)]
            for cp in fills:
                cp.start()
            for cp in fills:
                cp.wait()

        @pl.when(i < n_tiles)
        def _():
            f = lambda ref: ref[...].astype(F32)
            h, cg, dc = f(h_ref), f(cg_ref), f(dc_ref)
            w0, w1, w2 = vec_ref[1:2, :], vec_ref[2:3, :], vec_ref[3:4, :]
            row = lax.broadcasted_iota(jnp.int32, (TM, 1), 0)
            dc_prev = jnp.where(i == 0, 0.0, f(dcp_ref)[HALO - 1:HALO, :])
            dc_next = jnp.where(i == n_tiles - 1, 0.0, f(dcn_ref)[0:1, :])
            dc_up = jnp.where(row == 0, dc_prev, pltpu.roll(dc, 1, 0))
            dc_dn = jnp.where(row == TM - 1, dc_next, pltpu.roll(dc, TM - 1, 0))
            du = w0 * dc_dn + w1 * dc + w2 * dc_up
            u = cg * h
            part_ref[0:1, :] += jnp.sum(u * dc_dn, axis=0, keepdims=True)
            part_ref[1:2, :] += jnp.sum(u * dc, axis=0, keepdims=True)
            part_ref[2:3, :] += jnp.sum(u * dc_up, axis=0, keepdims=True)
            part_ref[3:4, :] += jnp.sum(dc, axis=0, keepdims=True)
            out[:, 0:D] = (du * cg).astype(BF16)
            out[:, D:2 * D] = (du * h).astype(BF16)

        writer.send(i, slot)

    last = n_tiles - 1
    col = lambda cb_: pl.BlockSpec((TM, D), lambda i, cb_=cb_: (jnp.minimum(i, last), cb_))
    lat = lambda w: pl.BlockSpec((TM, w), lambda i: (jnp.minimum(i, last), 0))
    ext = lambda w: pl.BlockSpec((TM, w), lambda i: (i, 0))
    anyspec = pl.BlockSpec(memory_space=pl.ANY)
    return pl.pallas_call(
        body, name="conv_backward", grid=(n_tiles + 1,),
        in_specs=[col(CB_H), col(CB_CG), lat(D),
                  pl.BlockSpec((HALO, D), lambda i: (jnp.clip(i * hb - 1, 0, n_tiles * hb - 1), 0)),
                  pl.BlockSpec((HALO, D), lambda i: (jnp.minimum((i + 1) * hb, n_tiles * hb - 1), 0)),
                  ext(qk), ext(qk), ext(D), pl.BlockSpec(vecs.shape, lambda i: (0, 0)), anyspec, _AFTER_SPEC],
        out_specs=[anyspec, pl.BlockSpec((8, D), lambda i: (0, 0))],
        out_shape=[jax.ShapeDtypeStruct(dp.shape, BF16), jax.ShapeDtypeStruct((8, D), F32)],
        input_output_aliases={9: 0},
        scratch_shapes=[pltpu.VMEM((2, TM, 4 * D), BF16), pltpu.SemaphoreType.DMA((2, len(pieces))),
                        pltpu.VMEM((TM, 3 * D), BF16), pltpu.SemaphoreType.DMA((len(zero_pieces),))],
        compiler_params=_cparams(dimension_semantics=("arbitrary",)),
    )(p_ext, p_ext, dconv, dconv, dconv, dq, dk, dv, vecs, dp, after)


def _input_backward(dp, shards, ids, x, ctx, dx1, modx, modc, norm_w, after):
    seq = x.shape[0]
    lext = seq + ctx.shape[0]
    n_x = seq // TM
    n_w = len(shards.arrays)

    def body(ids_ref, dp_ref, x_ref, ctx_ref, dx1_ref, mx_ref, mc_ref, nw_ref, *rest):
        w_hbm, (_, gx_ref, part_ref, w_ref, w_sems) = rest[:n_w], rest[n_w:]
        i = pl.program_id(0)
        _load_w_in(i, ids_ref, shards, w_hbm, w_ref, w_sems)
        is_ctx = i >= n_x
        dxm = _dot_nt(dp_ref[...], w_ref[...])
        x = jnp.where(is_ctx, ctx_ref[...], x_ref[...])
        r = lax.rsqrt(jnp.mean(x * x, axis=-1, keepdims=True) + EPS)
        xh = x * r
        nw = nw_ref[...]
        sc = jnp.where(is_ctx, mc_ref[1:2, :], mx_ref[1:2, :])
        dxn = dxm * (1.0 + sc)
        dxh = dxn * nw
        dx = r * (dxh - xh * jnp.mean(dxh * xh, axis=-1, keepdims=True))

        @pl.when(jnp.logical_not(is_ctx))
        def _():
            gx_ref[...] = dx1_ref[...] + dx

        @pl.when(i == 0)
        def _():
            part_ref[...] = jnp.zeros((8, D), F32)

        fx = jnp.where(is_ctx, 0.0, 1.0)
        d_shift = jnp.sum(dxm, axis=0, keepdims=True)
        d_scale = jnp.sum(dxm * (xh * nw), axis=0, keepdims=True)
        part_ref[0:1, :] += fx * d_shift
        part_ref[1:2, :] += fx * d_scale
        part_ref[2:3, :] += jnp.sum(dxn * xh, axis=0, keepdims=True)
        part_ref[3:4, :] += (1.0 - fx) * d_shift
        part_ref[4:5, :] += (1.0 - fx) * d_scale

    lat = lambda w: pl.BlockSpec((TM, w), lambda i, ids_: (jnp.minimum(i, n_x - 1), 0))
    ext = lambda w: pl.BlockSpec((TM, w), lambda i, ids_: (i, 0))
    full = lambda a: pl.BlockSpec(a.shape, lambda i, ids_: (0,) * a.ndim)
    grid_spec = pltpu.PrefetchScalarGridSpec(
        num_scalar_prefetch=1, grid=(lext // TM,),
        in_specs=[ext(PW), lat(D), full(ctx), lat(D), full(modx), full(modc), full(norm_w)]
        + [pl.BlockSpec(memory_space=pl.ANY)] * n_w + [_AFTER_SPEC],
        out_specs=[lat(D), pl.BlockSpec((8, D), lambda i, ids_: (0, 0))],
        scratch_shapes=[pltpu.VMEM((D, PW), BF16), pltpu.SemaphoreType.DMA((N_DEV,))])
    return pl.pallas_call(
        body, name="input_backward", grid_spec=grid_spec,
        out_shape=[jax.ShapeDtypeStruct((seq, D), F32), jax.ShapeDtypeStruct((8, D), F32)],
        compiler_params=_cparams(dimension_semantics=("arbitrary",)),
    )(ids, dp, x, ctx, dx1, modx, modc, norm_w, *shards.arrays, after)


def _weight_grad_in_sibling(xm_t, dp, owners, after):
    lext = xm_t.shape[1]

    def body(own_ref, a_ref, b_ref, after_ref, o_ref):
        o_ref[0] = _dot(a_ref[...], b_ref[...]).astype(BF16)

    grid_spec = pltpu.PrefetchScalarGridSpec(
        num_scalar_prefetch=1, grid=(4,),
        in_specs=[pl.BlockSpec((D, lext), lambda j, own: (0, 0)),
                  pl.BlockSpec((lext, WSH), lambda j, own: (0, own[j])), _AFTER_SPEC],
        out_specs=pl.BlockSpec((1, D, WSH), lambda j, own: (j, 0, 0)))
    return pl.pallas_call(
        body, name="weight_grad_in_sibling", grid_spec=grid_spec,
        out_shape=jax.ShapeDtypeStruct((4, D, WSH), BF16),
        compiler_params=_cparams(dimension_semantics=("arbitrary",)),
    )(owners, xm_t, dp, after)


def _weight_grad_in_own(xm_t, dp, owners, recv):
    lext = xm_t.shape[1]

    def body(own_ref, a_ref, b_ref, r_ref, mine_ref, send_ref):
        j = pl.program_id(0)
        total = _dot(a_ref[...], b_ref[...]) + r_ref[0].astype(F32)

        @pl.when(j == 0)
        def _():
            mine_ref[0] = total

        @pl.when(j > 0)
        def _():
            send_ref[0] = total.astype(BF16)

    grid_spec = pltpu.PrefetchScalarGridSpec(
        num_scalar_prefetch=1, grid=(4,),
        in_specs=[pl.BlockSpec((D, lext), lambda j, own: (0, 0), pipeline_mode=pl.Buffered(1)),
                  pl.BlockSpec((lext, WSH), lambda j, own: (0, own[j])),
                  pl.BlockSpec((1, D, WSH), lambda j, own: (j, 0, 0))],
        out_specs=[pl.BlockSpec((1, D, WSH), lambda j, own: (0, 0, 0)),
                   pl.BlockSpec((1, D, WSH), lambda j, own: (jnp.maximum(j - 1, 0), 0, 0))])
    return pl.pallas_call(
        body, name="weight_grad_in_own", grid_spec=grid_spec,
        out_shape=[jax.ShapeDtypeStruct((1, D, WSH), F32), jax.ShapeDtypeStruct((3, D, WSH), BF16)],
        compiler_params=_cparams(dimension_semantics=("arbitrary",)),
    )(owners, xm_t, dp, recv)


def _weight_grad_square(a_t, b):
    seq = a_t.shape[2]

    def body(a_ref, b_ref, o_ref):
        o_ref[:, 0] = _dot(a_ref[0], b_ref[0]).reshape(N_DEV, RSH, D)

    return pl.pallas_call(
        body, name="weight_grad_square", grid=(3,),
        in_specs=[pl.BlockSpec((1, D, seq), lambda t: (t, 0, 0)), pl.BlockSpec((1, seq, D), lambda t: (t, 0, 0))],
        out_specs=pl.BlockSpec((N_DEV, 1, RSH, D), lambda t: (0, t, 0, 0)),
        out_shape=jax.ShapeDtypeStruct((N_DEV, 3, RSH, D), F32),
        compiler_params=_cparams(dimension_semantics=("arbitrary",)),
    )(a_t, b)


def _rows(total, *parts):
    width = max(a.shape[1] for _, a in parts)
    out = None
    for row, a in parts:
        padded = jnp.pad(a, ((row, total - row - a.shape[0]), (0, width - a.shape[1])))
        out = padded if out is None else out + padded
    return out


def kernel(x, c, ctx, c_ctx, norm_w, ada_w, ada_b, w_in, conv_w, conv_b, decay_logit, gn_w, w_a, w_b, w_out, final_norm_w, loss_target, m_c_ctx, m_norm_w, m_ada_w, m_ada_b, m_w_in, m_conv_w, m_conv_b, m_decay_logit, m_gn_w, m_w_a, m_w_b, m_w_out, m_final_norm_w, v_c_ctx, v_norm_w, v_ada_w, v_ada_b, v_w_in, v_conv_w, v_conv_b, v_decay_logit, v_gn_w, v_w_a, v_w_b, v_w_out, v_final_norm_w):
    xi, yi, ci = _coords()
    me = 4 * xi + 2 * yi + ci
    chip = 2 * xi + yi
    seq, ctx_len = x.shape[1], ctx.shape[1]
    assert seq % TM == 0 and seq % RET_C == 0 and ctx_len == TM and seq % GRID_W == 0
    csh = D // N_DEV

    blk = jnp.pad(c, ((0, 7), (0, 0))) + jnp.pad(conv_w[0], ((1, 4), (0, D - csh)))
    got, _ = _all_gather_small(blk, "gather_cond")
    conv_w_all = got[:, 1:4, 0:csh].transpose(1, 0, 2).reshape(3, D)
    c16 = _rows(16, (0, got[:, 0, :]), (N_DEV, c_ctx[None]))
    ada_b_sh = lax.dynamic_slice(ada_b, (0, me * ADA_SH), (1, ADA_SH))
    mod_sh, act16, lg = _modulation(c16, ada_w[0], ada_b_sh, decay_logit[0])
    mod_all, small_done = _all_gather_small(mod_sh, "gather_mod")
    mod_all = mod_all.transpose(1, 0, 2).reshape(16, 3 * D)
    modx = lax.dynamic_slice(mod_all, (me, 0), (1, 3 * D)).reshape(3, D)
    modc = mod_all[8].reshape(3, D)

    x2, tgt = x[0], loss_target[0]
    wblock = lambda n: jax.ShapeDtypeStruct((n, D, WSH), BF16)
    ids_of = lambda ks: jnp.stack([jnp.bitwise_xor(me, k) for k in ks]).astype(jnp.int32)
    own_shard = w_in[0].astype(BF16)[None]
    sq_shards = [w[0].astype(BF16) for w in (w_a, w_b, w_out)]
    wave1 = _Exchange(
        "gather_in_first", [jnp.broadcast_to(own_shard, (2, D, WSH)), own_shard], [wblock(4)], 3,
        lambda srcs, lands: _forward_plan([(0, 1)])(srcs, lands) + _send_plan((4, 2))(srcs[1:], lands),
        after=small_done)
    xm, xm_t, cos_t, sin_t = _prenorm(x2, ctx[0], modx, modc, norm_w, wave1.token)
    (w_pair, _), _ = wave1.wait(xm_t, 1, through=[0])
    first = _Shards([w_pair], [(0, 0), (0, 1)])
    p_ext = _in_projection(None, xm, cos_t, sin_t, w_pair, [((0, 0), (1, 1))], me, "in_projection_pair")
    (_, own_shard), (w_nbr,) = wave1.wait(p_ext, through=[1, 2])
    wave2 = _Exchange(
        "gather_in_second", [w_nbr, own_shard], [wblock(2)], 3,
        lambda srcs, lands: _forward_plan([(0, 2), (1, 3)])(srcs, lands) + _send_plan((6,))(srcs[1:], lands))
    (w_nbr, _), _ = wave2.wait(wave2.token, 1, through=[0])
    p_ext = _in_projection(p_ext, xm, cos_t, sin_t, w_nbr, [((0, 4), (2, 5))], me, "in_projection_neighbour_x")
    (w_nbr, _), _ = wave2.wait(p_ext, 1, through=[0])
    second = _Shards([w_nbr], [(0, 0), (0, 1), (0, 2), (0, 3)])
    p_ext = _in_projection(p_ext, xm, cos_t, sin_t, w_nbr, [((1, 2), (3, 3))], me, "in_projection_neighbour_y")
    _, (w_diag,) = wave2.wait(p_ext, through=[1, 2])
    wave3 = _Exchange(
        "gather_in_third", [w_diag] + sq_shards, [jax.ShapeDtypeStruct((N_DEV, RSH, D), BF16)] * 3, 1 + 3 * (N_DEV - 1),
        lambda srcs, lands: _forward_plan([(0, 1)])(srcs, lands) + _gather_plan(srcs[1:], lands))
    (w_diag, *_), _ = wave3.wait(wave3.token, 1, through=[0])
    third = _Shards([w_diag], [(0, 0), (0, 1)])
    p_ext = _in_projection(p_ext, xm, cos_t, sin_t, w_diag, [((0, 6), (1, 7))], me, "in_projection_diagonal")
    w_shards, w_ids = first + second + third, ids_of((0, 1, 4, 2, 5, 3, 6, 7))

    rn, rstd = _retention_forward(p_ext, lg, seq, ctx_len)
    _, sq_lands = wave3.wait(rstd, through=range(1, 7))
    w_a_all, w_b_all, w_out_all = (
        lax.dynamic_update_slice(land, shard[None], (me, 0, 0)).reshape(D, D) for land, shard in zip(sq_lands, sq_shards))
    vecs = _rows(8, (0, modx[2:3]), (1, conv_w_all), (4, conv_b), (5, gn_w), (6, final_norm_w[None]))
    dx1, dp, dconv, dret, op_at, op_b, part_m = _merge(p_ext, rn, rstd, x2, tgt, w_a_all, w_b_all, w_out_all, vecs, seq)

    j4 = jnp.arange(4, dtype=jnp.int32)
    owners = (2 * jnp.bitwise_xor(chip, j4) + ci).astype(jnp.int32)
    owners_sib = (2 * jnp.bitwise_xor(chip, j4) + (1 - ci)).astype(jnp.int32)
    gw_sq = _weight_grad_square(op_at, op_b).reshape(N_DEV, 3 * RSH, D)
    rs_sq_pair = _Exchange("rs_square_pair", [gw_sq], [jax.ShapeDtypeStruct((4, 3 * RSH, D), F32)], 4,
                           _pair_plan(lambda j, chip_, c_: 2 * jnp.bitwise_xor(chip_, j) + (1 - c_)))
    dq, dk, dv, dlg = _retention_backward(p_ext, dret, lg, cos_t, sin_t, seq, ctx_len, rs_sq_pair.token)
    dp, part_c = _conv_backward(p_ext, dconv, dp, dq, dk, dv, vecs, seq, dlg)
    gw_sib = _weight_grad_in_sibling(xm_t, dp, owners_sib, part_c)
    rs_in_pair = _Exchange("rs_in_pair", [gw_sib], [jax.ShapeDtypeStruct((4, D, WSH), BF16)], 4,
                           _pair_plan(lambda j, chip_, c_: j))
    (gw_sq,), (r1_sq,) = rs_sq_pair.wait(rs_in_pair.token)
    own_sq, send_sq = _pair_sum(gw_sq, r1_sq, owners, "pair_sum_square")
    rs_sq_chips = _Exchange("rs_square_chips", [send_sq], [jax.ShapeDtypeStruct((3, 3 * RSH, D), BF16)], 3, _chips_plan)
    _, (r1_in,) = rs_in_pair.wait(rs_sq_chips.token)
    own_in, send_in = _weight_grad_in_own(xm_t, dp, owners, r1_in)
    rs_in_chips = _Exchange("rs_in_chips", [send_in], [jax.ShapeDtypeStruct((3, D, WSH), BF16)], 3, _chips_plan)
    grad_x, part_i = _input_backward(dp, w_shards, w_ids, x2, ctx[0], dx1, modx, modc, norm_w, rs_in_chips.token)

    dl = dlg[:, 0, 0:4]
    dlg_row = jnp.pad(dl[:, 0::2].reshape(1, N_HEADS), ((0, 0), (0, D - N_HEADS))) + jnp.pad(
        dl[:, 1::2].reshape(1, N_HEADS), ((0, 0), (N_HEADS, D - 2 * N_HEADS)))
    partials = _rows(16, (0, part_m[0:1]), (1, part_i[2:3]), (2, part_c[3:4]), (3, part_m[2:3]), (4, part_c[0:3]),
                     (7, part_i[0:2]), (9, part_m[1:2]), (10, part_i[3:5]), (13, part_m[3:4]), (14, dlg_row))
    results = {}
    gather_part = _Exchange("gather_partials", [partials], [jax.ShapeDtypeStruct((N_DEV, 16, D), F32)],
                            N_DEV - 1, _gather_plan)
    _, (r2_in,) = rs_in_chips.wait(gather_part.token)
    results["w_in"] = [o[None] for o in _adam_sharded(w_in[0], m_w_in[0], v_w_in[0], own_in, r2_in, "adam_w_in")]
    (partials,), (got,) = gather_part.wait(results["w_in"][1])
    got = lax.dynamic_update_slice(got, partials[None], (me, 0, 0))
    tot = _sum_devices(got, "sum_partials")
    dmodc = tot[10:13].reshape(1, 3 * D)
    dmod16 = _rows(16, (0, got[:, 7:10, :].reshape(N_DEV, 3 * D)), (N_DEV, dmodc))
    dmod16 = lax.dynamic_slice(dmod16, (0, me * ADA_SH), (16, ADA_SH))
    dmodc8 = _rows(8, (0, dmod16[8:9]))
    g_ada_w, d_ada_w, nm_ada_w, nv_ada_w, cctx_part = _ada_backward(
        act16.T, dmod16, dmodc8, ada_w[0], m_ada_w[0], v_ada_w[0])
    gather_cctx = _Exchange("gather_cctx", [cctx_part], [jax.ShapeDtypeStruct((N_DEV, 8, D), F32)],
                            N_DEV - 1, _gather_plan)
    _, (r2_sq,) = rs_sq_chips.wait(gather_cctx.token)
    square = _adam_square([(w_a[0], m_w_a[0], v_w_a[0]), (w_b[0], m_w_b[0], v_w_b[0]),
                           (w_out[0], m_w_out[0], v_w_out[0])], own_sq, r2_sq)
    (cctx_part,), (got_cctx,) = gather_cctx.wait(square[0][1])
    dact_ctx = _sum_devices(lax.dynamic_update_slice(got_cctx, cctx_part[None], (me, 0, 0)), "sum_cctx")

    view = {"c_ctx": (1, D), "final_norm_w": (1, D)}
    given = {"c_ctx": (c_ctx, m_c_ctx, v_c_ctx), "norm_w": (norm_w, m_norm_w, v_norm_w),
             "conv_b": (conv_b, m_conv_b, v_conv_b), "gn_w": (gn_w, m_gn_w, v_gn_w),
             "final_norm_w": (final_norm_w, m_final_norm_w, v_final_norm_w), "ada_b": (ada_b, m_ada_b, v_ada_b),
             "decay_logit": (decay_logit, m_decay_logit, v_decay_logit), "conv_w": (conv_w, m_conv_w, v_conv_w)}
    small_in = [tuple(a.reshape(view.get(name, a.shape)) for a in given[name]) for name in SMALL_PARAMS]
    conv_w_grad = lax.dynamic_slice(tot, (4, me * csh), (3, csh))
    small_out = _adam_small(tot, dact_ctx, conv_w_grad, tot[14, 0:2 * N_HEADS].reshape(2, N_HEADS), small_in)
    results.update({name: [o.reshape(given[name][0].shape) for o in outs_]
                    for name, outs_ in zip(SMALL_PARAMS, small_out)})
    for name, outs_ in zip(("w_a", "w_b", "w_out"), square):
        results[name] = [o[None] for o in outs_]
    results["ada_w"] = [o[None] for o in (g_ada_w, d_ada_w, nm_ada_w, nv_ada_w)]

    order = ("c_ctx", "norm_w", "ada_w", "ada_b", "w_in", "conv_w", "conv_b", "decay_logit", "gn_w",
             "w_a", "w_b", "w_out", "final_norm_w")
    outs = [results[name][kind] for kind in range(4) for name in order]
    return (tot[13, 0], grad_x[None], *outs)
```

```python
import math

import jax
import jax.numpy as jnp
from jax import lax
from jax.experimental import pallas as pl
from jax.experimental.pallas import tpu as pltpu

F32 = jnp.float32
BF16 = jnp.bfloat16
MESH = pl.DeviceIdType.MESH
_HBM_SPEC = pl.BlockSpec(memory_space=pltpu.HBM)
_SEM_SPEC = pl.BlockSpec(memory_space=pltpu.SEMAPHORE)
_EFFECT = pltpu.SideEffectType.DATAFLOW_SIDE_EFFECTING

N_DEV = 8
D = 1024
N_HEADS = 8
DK = 64
DV = 128
GRID_W = 64
ROPE_BASE = 10000.0
EPS = 1e-6
PW = 9 * D
WSH = PW // N_DEV
RSH = D // N_DEV
ADA_SH = 3 * D // N_DEV
TM = 256
RET_C = 256
HALO = 16
LANES = 128
VMEM_LIMIT = 60 * 1024 * 1024

ADAM_LR = 0.001
ADAM_B1 = 0.9
ADAM_B2 = 0.999
ADAM_EPS = 1e-08
ADAM_WD = 0.01
ADAM_STEP = 10
ADAM_TILE_ROWS = 128
ADAM_READ_DEPTH = 3

CB_H, CB_BG, CB_CG, CB_ZA, CB_QK, CB_V, CB_ZB, CB_GA, CB_GB = range(9)


def _cparams(**kw):
    return pltpu.CompilerParams(vmem_limit_bytes=VMEM_LIMIT, **kw)


def _dot(a, b):
    return jnp.dot(a, b, preferred_element_type=F32)


def _dot_nt(a, b):
    return lax.dot_general(a, b, (((1,), (1,)), ((), ())), preferred_element_type=F32)


def _dot_tn(a, b):
    return lax.dot_general(a, b, (((0,), (0,)), ((), ())), preferred_element_type=F32)


def _sigmoid(z):
    return 0.5 * jnp.tanh(0.5 * z) + 0.5


def _row_tile(rows):
    return TM if rows % TM == 0 else rows


def _coords():
    return lax.axis_index("x"), lax.axis_index("y"), lax.axis_index("c")


def _flip(v, bit):
    return 1 - v if bit else v


def _all_gather_small(blk, name):
    rows, cols = blk.shape

    def body(x_ref, out_ref, done, send_sems, recv_sems, local_sem):
        x, y, c = _coords()
        me = 4 * x + 2 * y + c
        mine = pltpu.make_async_copy(x_ref, out_ref.at[me], local_sem)
        mine.start()

        def copy(k, slot):
            peer = (_flip(x, k & 4), _flip(y, k & 2), _flip(c, k & 1))
            return pltpu.make_async_remote_copy(
                src_ref=x_ref, dst_ref=out_ref.at[slot], send_sem=send_sems.at[k - 1],
                recv_sem=recv_sems.at[k - 1], device_id=peer, device_id_type=MESH)

        for k in range(1, N_DEV):
            copy(k, me).start()
        for k in range(1, N_DEV):
            copy(k, jnp.bitwise_xor(me, k)).wait_recv()
        for k in range(1, N_DEV):
            copy(k, me).wait_send()
        mine.wait()
        done[...] = jnp.zeros((8, LANES), F32)

    vmem = pl.BlockSpec(memory_space=pltpu.VMEM)
    return pl.pallas_call(
        body, name=name,
        out_shape=[jax.ShapeDtypeStruct((N_DEV, rows, cols), blk.dtype), jax.ShapeDtypeStruct((8, LANES), F32)],
        in_specs=[vmem], out_specs=[vmem, vmem],
        scratch_shapes=[pltpu.SemaphoreType.DMA((N_DEV - 1,)), pltpu.SemaphoreType.DMA((N_DEV - 1,)),
                        pltpu.SemaphoreType.DMA],
    )(blk)


class _Absent:
    at = property(lambda self: self)

    def __getitem__(self, index):
        return self


class _Exchange:
    def __init__(self, name, srcs, land_shapes, n_copies, plan, after=None):
        self.name, self.plan, self.n_copies = name, plan, n_copies
        self.n_src, self.n_land = len(srcs), len(land_shapes)
        hbm = lambda a: pltpu.HBM(a.shape, a.dtype)
        n = self.n_src + self.n_land
        lands = [pltpu.with_memory_space_constraint(lax.empty(s.shape, s.dtype), pltpu.HBM) for s in land_shapes]
        srcs = [pltpu.with_memory_space_constraint(s, pltpu.HBM) for s in srcs]
        extra = [] if after is None else [after]

        def body(*refs):
            send_sems, recv_sems = refs[n + len(extra)], refs[n + len(extra) + 1]
            for cp in self._copies(refs, send_sems, recv_sems):
                cp.start()
            refs[-1][...] = jnp.zeros((8, LANES), F32)

        outs = pl.pallas_call(
            body, name=name + "_start",
            out_shape=(pltpu.SemaphoreType.DMA((n_copies,)), pltpu.SemaphoreType.DMA((n_copies,)),
                       *[hbm(a) for a in srcs], *[hbm(a) for a in lands], jax.ShapeDtypeStruct((8, LANES), F32)),
            in_specs=[_HBM_SPEC] * n + [pl.BlockSpec(memory_space=pl.ANY)] * len(extra),
            out_specs=(_SEM_SPEC, _SEM_SPEC, *[_HBM_SPEC] * n, pl.BlockSpec(memory_space=pltpu.VMEM)),
            input_output_aliases={i: 2 + i for i in range(n)},
            compiler_params=pltpu.CompilerParams(has_side_effects=_EFFECT),
        )(*srcs, *lands, *extra)
        self.send_sems, self.recv_sems = outs[0], outs[1]
        self.buffers = list(outs[2:2 + n])
        self.token = outs[-1]
        self.waited = 0

    def _copies(self, refs, send_sems, recv_sems, lo=0, hi=None):
        src_refs, land_refs = refs[:self.n_src], refs[self.n_src:self.n_src + self.n_land]
        planned = list(enumerate(self.plan(src_refs, land_refs)))[lo:hi]
        return [pltpu.make_async_remote_copy(src_ref=s, dst_ref=d, send_sem=send_sems.at[k], recv_sem=recv_sems.at[k],
                                             device_id=dev, device_id_type=MESH) for k, (s, d, dev) in planned]

    def wait(self, after, count=None, through=None):
        n = self.n_src + self.n_land
        lo = self.waited
        hi = self.n_copies if count is None else lo + count
        self.waited = hi
        through = list(range(n)) if through is None else list(through)
        m = len(through)

        def body(*refs):
            held = [_Absent()] * n
            for pos, ref in zip(through, refs[:m]):
                held[pos] = ref
            for cp in self._copies(held, refs[m], refs[m + 1], lo, hi):
                cp.wait_send()
                cp.wait_recv()

        outs = pl.pallas_call(
            body, name=f"{self.name}_wait{lo}" if lo or hi < self.n_copies else self.name + "_wait",
            out_shape=tuple(pltpu.HBM(self.buffers[p].shape, self.buffers[p].dtype) for p in through),
            in_specs=[_HBM_SPEC] * m + [_SEM_SPEC, _SEM_SPEC, pl.BlockSpec(memory_space=pl.ANY)],
            out_specs=tuple([_HBM_SPEC] * m),
            input_output_aliases={i: i for i in range(m)},
            compiler_params=pltpu.CompilerParams(has_side_effects=_EFFECT),
        )(*[self.buffers[p] for p in through], self.send_sems, self.recv_sems, after)
        for pos, out in zip(through, outs):
            self.buffers[pos] = out
        return list(self.buffers[:self.n_src]), list(self.buffers[self.n_src:])


def _pair_plan(src_index):
    def plan(srcs, lands):
        x, y, c = _coords()
        chip = 2 * x + y
        return [(srcs[0].at[src_index(j, chip, c)], lands[0].at[j], (x, y, 1 - c)) for j in range(4)]
    return plan


def _chips_plan(srcs, lands):
    x, y, c = _coords()
    return [(srcs[0].at[j - 1], lands[0].at[j - 1], (_flip(x, j & 2), _flip(y, j & 1), c)) for j in range(1, 4)]


def _peer(k):
    x, y, c = _coords()
    return _flip(x, k & 4), _flip(y, k & 2), _flip(c, k & 1)


def _send_plan(ks):
    def plan(srcs, lands):
        return [(srcs[0].at[0], lands[0].at[p], _peer(k)) for p, k in enumerate(ks)]
    return plan


def _forward_plan(moves):
    def plan(srcs, lands):
        return [(srcs[0].at[s], srcs[0].at[d], _peer(1)) for s, d in moves]
    return plan


def _gather_plan(srcs, lands):
    x, y, c = _coords()
    me = 4 * x + 2 * y + c
    return [(srcs[a], lands[a].at[me], (_flip(x, k & 4), _flip(y, k & 2), _flip(c, k & 1)))
            for a in range(len(srcs)) for k in range(1, N_DEV)]


def _pair_sum(grad, recv, owners, name):
    _, rows, cols = grad.shape
    tr = rows

    def body(own_ref, g_ref, r_ref, mine_ref, send_ref):
        j = pl.program_id(1)
        total = g_ref[...] + r_ref[...]

        @pl.when(j == 0)
        def _():
            mine_ref[...] = total

        @pl.when(j > 0)
        def _():
            send_ref[...] = total.astype(BF16)

    grid_spec = pltpu.PrefetchScalarGridSpec(
        num_scalar_prefetch=1, grid=(rows // tr, 4),
        in_specs=[pl.BlockSpec((1, tr, cols), lambda i, j, own: (own[j], i, 0)),
                  pl.BlockSpec((1, tr, cols), lambda i, j, own: (j, i, 0))],
        out_specs=[pl.BlockSpec((1, tr, cols), lambda i, j, own: (0, i, 0)),
                   pl.BlockSpec((1, tr, cols), lambda i, j, own: (jnp.maximum(j - 1, 0), i, 0))])
    return pl.pallas_call(
        body, name=name, grid_spec=grid_spec,
        out_shape=[jax.ShapeDtypeStruct((1, rows, cols), F32), jax.ShapeDtypeStruct((3, rows, cols), BF16)],
        compiler_params=_cparams(dimension_semantics=("arbitrary", "arbitrary")),
    )(owners, grad, recv)


def _modulation(c16, ada_w_sh, ada_b_sh, decay_logit):
    def body(c_ref, w_ref, b_ref, dl_ref, mod_ref, act_ref, lg_ref):
        cv = c_ref[...]
        act = cv * _sigmoid(cv)
        act_ref[...] = act
        mod_ref[...] = jnp.dot(act, w_ref[...], preferred_element_type=F32,
                               precision=lax.Precision.HIGHEST) + b_ref[...]
        z = dl_ref[...]
        lg_ref[...] = jnp.minimum(z, 0.0) - jnp.log(1.0 + jnp.exp(-jnp.abs(z)))

    return pl.pallas_call(
        body, name="modulation",
        out_shape=[jax.ShapeDtypeStruct((16, ADA_SH), F32), jax.ShapeDtypeStruct((16, D), F32),
                   jax.ShapeDtypeStruct(decay_logit.shape, F32)],
        compiler_params=_cparams(),
    )(c16, ada_w_sh, ada_b_sh, decay_logit)


def _adam_update(w, g, m, v):
    m2 = ADAM_B1 * m + (1.0 - ADAM_B1) * g
    v2 = ADAM_B2 * v + (1.0 - ADAM_B2) * (g * g)
    m_hat = m2 / (1.0 - ADAM_B1 ** ADAM_STEP)
    v_hat = v2 / (1.0 - ADAM_B2 ** ADAM_STEP)
    delta = -ADAM_LR * (m_hat / (jnp.sqrt(v_hat) + ADAM_EPS) + ADAM_WD * w)
    return delta, m2, v2


def _adam_sharded(w, m, v, own, recv, name):
    rows, cols = w.shape
    tr, depth = ADAM_TILE_ROWS, ADAM_READ_DEPTH
    n = rows // tr
    assert rows % tr == 0 and n >= depth - 1

    def body(w_hbm, m_hbm, v_hbm, own_hbm, recv_hbm, g_ref, d_ref, m_out, v_out, f32_buf, bf16_buf, sems):
        i = pl.program_id(0)

        def fetch(step, slot):
            at = pl.ds(step * tr, tr)
            srcs = [w_hbm.at[at], m_hbm.at[at], v_hbm.at[at], own_hbm.at[0, at]]
            cps = [pltpu.make_async_copy(s, f32_buf.at[slot, k], sems.at[slot, k]) for k, s in enumerate(srcs)]
            return cps + [pltpu.make_async_copy(recv_hbm.at[k, at], bf16_buf.at[slot, k], sems.at[slot, 4 + k])
                          for k in range(3)]

        @pl.when(i == 0)
        def _():
            for s in range(depth - 1):
                for cp in fetch(s, s):
                    cp.start()

        @pl.when(i + depth - 1 < n)
        def _():
            for cp in fetch(i + depth - 1, (i + depth - 1) % depth):
                cp.start()

        slot = i % depth
        for cp in fetch(i, slot):
            cp.wait()
        g = ((f32_buf[slot, 3] + bf16_buf[slot, 0].astype(F32)) + bf16_buf[slot, 1].astype(F32)) + bf16_buf[
            slot, 2].astype(F32)
        delta, m2, v2 = _adam_update(f32_buf[slot, 0], g, f32_buf[slot, 1], f32_buf[slot, 2])
        g_ref[...] = g
        d_ref[...] = delta
        m_out[...] = m2
        v_out[...] = v2

    flat = pl.BlockSpec((tr, cols), lambda i: (i, 0))
    return pl.pallas_call(
        body, name=name, grid=(n,),
        in_specs=[pl.BlockSpec(memory_space=pl.ANY)] * 5,
        out_specs=[flat] * 4,
        out_shape=[jax.ShapeDtypeStruct((rows, cols), F32)] * 4,
        scratch_shapes=[pltpu.VMEM((depth, 4, tr, cols), F32), pltpu.VMEM((depth, 3, tr, cols), BF16),
                        pltpu.SemaphoreType.DMA((depth, 7))],
        compiler_params=_cparams(dimension_semantics=("arbitrary",)),
    )(w, m, v, own, recv)


def _ada_backward(act_t, dmod16, dmodc8, ada_w_sh, m, v):
    def body(at_ref, dm_ref, dc_ref, w_ref, m_ref, v_ref, g_ref, d_ref, m_out, v_out, pc_ref):
        g = jnp.dot(at_ref[...], dm_ref[...], preferred_element_type=F32, precision=lax.Precision.HIGHEST)
        w = w_ref[...]
        delta, m2, v2 = _adam_update(w, g, m_ref[...], v_ref[...])
        g_ref[...] = g
        d_ref[...] = delta
        m_out[...] = m2
        v_out[...] = v2
        pc_ref[...] = lax.dot_general(dc_ref[...], w, (((1,), (1,)), ((), ())), preferred_element_type=F32,
                                      precision=lax.Precision.HIGHEST)

    return pl.pallas_call(
        body, name="ada_backward",
        out_shape=[jax.ShapeDtypeStruct((D, ADA_SH), F32)] * 4 + [jax.ShapeDtypeStruct((8, D), F32)],
        compiler_params=_cparams(),
    )(act_t, dmod16, dmodc8, ada_w_sh, m, v)


def _sum_devices(gathered, name):
    _, rows, cols = gathered.shape

    def body(g_ref, o_ref):
        acc = g_ref[0]
        for d in range(1, N_DEV):
            acc = acc + g_ref[d]
        o_ref[...] = acc

    return pl.pallas_call(body, name=name, out_shape=jax.ShapeDtypeStruct((rows, cols), F32),
                          compiler_params=_cparams())(gathered)


SMALL_PARAMS = ("c_ctx", "norm_w", "conv_b", "gn_w", "final_norm_w", "ada_b", "decay_logit", "conv_w")


def _adam_small(tot, dact_ctx, conv_w_grad, decay_grad, params):
    n = len(SMALL_PARAMS)

    def body(tot_ref, dact_ref, cwg_ref, dlg_ref, *refs):
        ins, outs = refs[:3 * n], refs[3 * n:]
        raw = {
            "c_ctx": dact_ref[0:1, :],
            "norm_w": tot_ref[1:2, :], "conv_b": tot_ref[2:3, :], "gn_w": tot_ref[3:4, :],
            "final_norm_w": tot_ref[0:1, :],
            "ada_b": jnp.concatenate([tot_ref[7 + r:8 + r, :] + tot_ref[10 + r:11 + r, :] for r in range(3)], axis=1),
            "decay_logit": dlg_ref[...],
            "conv_w": cwg_ref[...],
        }
        for k, name in enumerate(SMALL_PARAMS):
            lead = (0,) if len(ins[3 * k].shape) == 3 else ()
            w, m, v = (ref[lead + (...,)] for ref in ins[3 * k:3 * k + 3])
            g = raw[name]
            if name == "c_ctx":
                s = _sigmoid(w)
                g = g * (s * (1.0 + w * (1.0 - s)))
            elif name == "decay_logit":
                g = g * (1.0 - _sigmoid(w))
            delta, m2, v2 = _adam_update(w, g, m, v)
            for ref, val in zip(outs[4 * k:4 * k + 4], (g, delta, m2, v2)):
                ref[lead + (...,)] = val

    flat = [a for trio in params for a in trio]
    out_shape = [jax.ShapeDtypeStruct(trio[0].shape, F32) for trio in params for _ in range(4)]
    outs = pl.pallas_call(body, name="adam_small", out_shape=out_shape, compiler_params=_cparams())(
        tot, dact_ctx, conv_w_grad, decay_grad, *flat)
    return [outs[4 * k:4 * k + 4] for k in range(n)]


def _adam_square(params, own, recv):
    def body(own_ref, recv_ref, *refs):
        ins, outs = refs[:9], refs[9:]
        for k in range(3):
            rows = slice(k * RSH, (k + 1) * RSH)
            g = own_ref[0, rows, :]
            for j in range(3):
                g = g + recv_ref[j, rows, :].astype(F32)
            delta, m2, v2 = _adam_update(ins[3 * k][...], g, ins[3 * k + 1][...], ins[3 * k + 2][...])
            for ref, val in zip(outs[4 * k:4 * k + 4], (g, delta, m2, v2)):
                ref[...] = val

    flat = [a for trio in params for a in trio]
    outs = pl.pallas_call(body, name="adam_square", out_shape=[jax.ShapeDtypeStruct((RSH, D), F32)] * 12,
                          compiler_params=_cparams())(own, recv, *flat)
    return [outs[4 * k:4 * k + 4] for k in range(3)]


def _rope(t, cos, sin):
    lane = lax.broadcasted_iota(jnp.int32, (1, LANES), 1)
    first_half = jnp.bitwise_and(lane, DK // 2) == 0
    partner = jnp.where(first_half, pltpu.roll(t, LANES - DK // 2, 1), pltpu.roll(t, DK // 2, 1))
    return t * cos + partner * sin


class _Shards:
    def __init__(self, arrays, picks):
        self.arrays, self.picks = list(arrays), list(picks)

    def __add__(self, other):
        shift = len(self.arrays)
        return _Shards(self.arrays + other.arrays, self.picks + [(a + shift, s) for a, s in other.picks])


def _load_w_in(step, ids_ref, shards, w_refs, w_vmem, sems):
    @pl.when(step == 0)
    def _():
        copies = []
        for p, (a, slot) in enumerate(shards.picks):
            col = pl.multiple_of(ids_ref[p] * WSH, LANES)
            copies.append(pltpu.make_async_copy(w_refs[a].at[slot], w_vmem.at[:, pl.ds(col, WSH)], sems.at[p]))
        for cp in copies:
            cp.start()
        for cp in copies:
            cp.wait()


def _project_columns(xmb, w_ref, shard, cos, sin, p_ref):
    width = 2 * WSH
    q_lo, k_lo, k_hi = CB_QK * D, CB_QK * D + N_HEADS * DK, CB_V * D
    q_pair, k_pair = q_lo // width, k_lo // width
    assert (q_pair + 1) * width == k_lo and k_pair * width == k_lo and k_hi <= (k_pair + 1) * width
    mxu_cols = 2 * LANES

    def columns(rot_lo, rot_hi, scale):
        for a in range(0, width, mxu_cols):
            acc = _dot(xmb, w_ref[:, a:a + mxu_cols])
            for col in range(a, a + mxu_cols, LANES):
                piece = acc[:, col - a:col - a + LANES]
                if rot_lo <= col < rot_hi:
                    piece = _rope(piece if scale == 1.0 else piece * scale, cos, sin)
                p_ref[:, col:col + LANES] = piece.astype(BF16)

    @pl.when(shard == q_pair)
    def _():
        columns(q_lo - q_pair * width, width, 1.0)

    @pl.when(shard == k_pair)
    def _():
        columns(0, k_hi - k_lo, DK ** -0.5)

    @pl.when(jnp.logical_and(shard != q_pair, shard != k_pair))
    def _():
        columns(0, 0, 1.0)


_AFTER_SPEC = pl.BlockSpec(memory_space=pl.ANY)


def _prenorm(x, ctx, modx, modc, norm_w, after):
    tp = 4 * TM
    seq, ctx_len = x.shape[0], ctx.shape[0]
    n_x_tiles = seq // tp
    lext = seq + ctx_len
    assert seq % tp == 0 and ctx_len <= tp and tp % GRID_W == 0 and tp // GRID_W <= 16
    grid_rows = tp // GRID_W

    def body(x_ref, ctx_ref, mx_ref, mc_ref, nw_ref, after_ref, xm_ref, xmt_ref, cos_ref, sin_ref, col_cos, col_sin):
        del after_ref
        i = pl.program_id(0)
        is_ctx = i >= n_x_tiles
        ctx_rows = jnp.concatenate([ctx_ref[...], jnp.zeros((tp - ctx_len, D), F32)], axis=0)
        x = jnp.where(is_ctx, ctx_rows, x_ref[...])
        r = lax.rsqrt(jnp.mean(x * x, axis=-1, keepdims=True) + EPS)
        sh = jnp.where(is_ctx, mc_ref[0:1, :], mx_ref[0:1, :])
        sc = jnp.where(is_ctx, mc_ref[1:2, :], mx_ref[1:2, :])
        xm = (x * r * nw_ref[...]) * (1.0 + sc) + sh
        xm_ref[...] = xm.astype(BF16)
        xmt_ref[...] = xm.T.astype(BF16)
        lane = lax.broadcasted_iota(jnp.int32, (1, LANES), 1)
        n_freq = DK // 4
        inv = jnp.exp((lane % n_freq).astype(F32) * (-math.log(ROPE_BASE) / n_freq))
        by_column = (lane % (2 * n_freq)) >= n_freq
        sign = jnp.where((lane % DK) < DK // 2, -1.0, 1.0)

        @pl.when(i == 0)
        def _():
            col = (lax.broadcasted_iota(jnp.int32, (tp, 1), 0) % GRID_W).astype(F32) * inv
            col_cos[...] = jnp.cos(col)
            col_sin[...] = jnp.sin(col) * sign

        row = (i * grid_rows + lax.broadcasted_iota(jnp.int32, (16, 1), 0)).astype(F32) * inv
        row_cos, row_sin = jnp.cos(row), jnp.sin(row) * sign
        tall = lambda t: jnp.concatenate(
            [jnp.broadcast_to(t[g:g + 1, :], (GRID_W, LANES)) for g in range(grid_rows)], axis=0)
        cos_ref[...] = jnp.where(is_ctx, 1.0, jnp.where(by_column, col_cos[...], tall(row_cos)))
        sin_ref[...] = jnp.where(is_ctx, 0.0, jnp.where(by_column, col_sin[...], tall(row_sin)))

    full = lambda a: pl.BlockSpec(a.shape, lambda i: (0,) * a.ndim)
    row = lambda width: pl.BlockSpec((tp, width), lambda i: (i, 0))
    return pl.pallas_call(
        body, name="prenorm", grid=(n_x_tiles + 1,),
        in_specs=[pl.BlockSpec((tp, D), lambda i: (jnp.minimum(i, n_x_tiles - 1), 0)), full(ctx),
                  full(modx), full(modc), full(norm_w), _AFTER_SPEC],
        out_specs=[row(D), pl.BlockSpec((D, tp), lambda i: (0, i)), row(LANES), row(LANES)],
        out_shape=[jax.ShapeDtypeStruct((lext, D), BF16), jax.ShapeDtypeStruct((D, lext), BF16),
                   jax.ShapeDtypeStruct((lext, LANES), F32), jax.ShapeDtypeStruct((lext, LANES), F32)],
        scratch_shapes=[pltpu.VMEM((tp, LANES), F32), pltpu.VMEM((tp, LANES), F32)],
        compiler_params=_cparams(dimension_semantics=("arbitrary",)),
    )(x, ctx, modx, modc, norm_w, after)


def _in_projection(p_ext, xm, cos_t, sin_t, w, pairs, me, name):
    lext = xm.shape[0]
    tr = lext // 4
    meta = []
    for (slot_a, k_a), (slot_b, k_b) in pairs:
        assert k_a ^ k_b == 1
        dev_a = jnp.bitwise_xor(me, k_a)
        a_low = dev_a % 2 == 0
        meta += [jnp.where(a_low, slot_a, slot_b), jnp.where(a_low, slot_b, slot_a), dev_a // 2]
    meta = jnp.stack(meta).astype(jnp.int32)

    def body(meta_ref, xm_ref, cos_ref, sin_ref, w_hbm, *rest):
        p_ref, w_ref, sems = rest[-3:]
        j, i = pl.program_id(0), pl.program_id(1)

        @pl.when(i == 0)
        def _():
            copies = [pltpu.make_async_copy(w_hbm.at[meta_ref[3 * j + h]], w_ref.at[:, h * WSH:(h + 1) * WSH],
                                            sems.at[h]) for h in range(2)]
            for cp in copies:
                cp.start()
            for cp in copies:
                cp.wait()

        _project_columns(xm_ref[...], w_ref, meta_ref[3 * j + 2], cos_ref[...], sin_ref[...], p_ref)

    row = lambda width: pl.BlockSpec((tr, width), lambda j, i, meta_: (i, 0))
    grid_spec = pltpu.PrefetchScalarGridSpec(
        num_scalar_prefetch=1, grid=(len(pairs), lext // tr),
        in_specs=[row(D), row(LANES), row(LANES), pl.BlockSpec(memory_space=pl.ANY)]
        + ([] if p_ext is None else [pl.BlockSpec(memory_space=pl.ANY)]),
        out_specs=pl.BlockSpec((tr, 2 * WSH), lambda j, i, meta_: (i, meta_[3 * j + 2])),
        scratch_shapes=[pltpu.VMEM((D, 2 * WSH), BF16), pltpu.SemaphoreType.DMA((2,))])
    return pl.pallas_call(
        body, name=name, grid_spec=grid_spec,
        out_shape=jax.ShapeDtypeStruct((lext, PW), BF16),
        input_output_aliases={} if p_ext is None else {5: 0},
        compiler_params=_cparams(dimension_semantics=("arbitrary", "arbitrary")),
    )(meta, xm, cos_t, sin_t, w, *([] if p_ext is None else [p_ext]))


def _decay_tables(lgf, lgb, n):
    i = lax.broadcasted_iota(jnp.int32, (n, 1), 0).astype(F32)
    return dict(i=i, k_f=jnp.exp(lgf * (n - 1.0 - i)), k_b=jnp.exp(lgb * i),
                q_f=jnp.exp(lgf * (i + 1.0)), q_b=jnp.exp(lgb * (n - i)))


def _decay_matrix(lgf, lgb, n):
    ii = lax.broadcasted_iota(jnp.int32, (n, n), 0)
    jj = lax.broadcasted_iota(jnp.int32, (n, n), 1)
    diff = (ii - jj).astype(F32)
    low = jnp.exp(lgf * jnp.maximum(diff, 0.0))
    up = jnp.exp(lgb * jnp.maximum(-diff, 0.0))
    return jnp.where(diff > 0, low, jnp.where(diff < 0, up, 2.0)), diff


def _cat_lanes(a, b):
    return jnp.concatenate([a.astype(BF16), b.astype(BF16)], axis=1)


def _retention_forward(p_ext, lg, seq, ctx_len):
    lext = seq + ctx_len
    n_chunks = seq // RET_C
    C = RET_C

    def body(lg_ref, q_ref, k_ref, v_ref, rn_ref, rstd_ref, kv_scr, sf_scr, sb_scr):
        pair = pl.program_id(0)
        lane = lax.broadcasted_iota(jnp.int32, (1, LANES), 1)
        heads = range(2)
        hmask = [(lane // DK == hh).astype(F32) for hh in heads]
        lgf = [lg_ref[0, 2 * pair + hh] for hh in heads]
        lgb = [lg_ref[1, 2 * pair + hh] for hh in heads]
        vs = [slice(hh * DV, (hh + 1) * DV) for hh in heads]
        t = [_decay_tables(lgf[hh], lgb[hh], C) for hh in heads]
        kc_all = k_ref[seq:lext, :].astype(F32)
        s0 = []
        for hh in heads:
            tc = _decay_tables(lgf[hh], lgb[hh], ctx_len)
            kc = kc_all * hmask[hh]
            s0.append(_dot_tn(_cat_lanes(kc * tc["k_f"], kc * tc["k_b"]), v_ref[seq:lext, vs[hh]]))

        def increments(c, carry):
            rows = pl.ds(pl.multiple_of(c * C, C), C)
            k_all = k_ref[rows, :].astype(F32)
            for hh in heads:
                k = k_all * hmask[hh]
                kv_scr[hh, c] = _dot_tn(_cat_lanes(k * t[hh]["k_f"], k * t[hh]["k_b"]), v_ref[rows, vs[hh]])
            return carry

        lax.fori_loop(0, n_chunks, increments, 0, unroll=2)
        gf_c = [jnp.exp(lgf[hh] * C) for hh in heads]
        gb_c = [jnp.exp(lgb[hh] * C) for hh in heads]

        def scan_f(c, s):
            for hh in heads:
                sf_scr[hh, c] = s[hh]
            return tuple(gf_c[hh] * s[hh] + kv_scr[hh, c, 0:LANES, :] for hh in heads)

        def scan_b(n, s):
            c = n_chunks - 1 - n
            for hh in heads:
                sb_scr[hh, c] = s[hh]
            return tuple(gb_c[hh] * s[hh] + kv_scr[hh, c, LANES:2 * LANES, :] for hh in heads)

        lax.fori_loop(0, n_chunks, scan_f, tuple(s0[hh][0:LANES] for hh in heads))
        lax.fori_loop(0, n_chunks, scan_b, tuple(s0[hh][LANES:2 * LANES] for hh in heads))
        dmat = [_decay_matrix(lgf[hh], lgb[hh], C)[0] for hh in heads]

        def outputs(c, carry):
            rows = pl.ds(pl.multiple_of(c * C, C), C)
            q_all = q_ref[rows, :].astype(F32)
            k = k_ref[rows, :]
            for hh in heads:
                q = q_all * hmask[hh]
                v = v_ref[rows, vs[hh]]
                s = _dot_nt(q.astype(BF16), k)
                inner = _dot((s * dmat[hh]).astype(BF16), v)
                states = jnp.concatenate([sf_scr[hh, c], sb_scr[hh, c]], axis=0).astype(BF16)
                ret = inner + _dot(_cat_lanes(q * t[hh]["q_f"], q * t[hh]["q_b"]), states)
                mu = jnp.mean(ret, axis=-1, keepdims=True)
                cen = ret - mu
                rstd = lax.rsqrt(jnp.mean(cen * cen, axis=-1, keepdims=True) + EPS)
                rn_ref[rows, vs[hh]] = cen * rstd
                rstd_ref[rows, vs[hh]] = jnp.broadcast_to(rstd, (C, DV))
            return carry

        lax.fori_loop(0, n_chunks, outputs, 0, unroll=2)

    qk0 = CB_QK * D // LANES
    return pl.pallas_call(
        body, name="retention_forward", grid=(N_HEADS // 2,),
        in_specs=[pl.BlockSpec(memory_space=pltpu.SMEM),
                  pl.BlockSpec((lext, LANES), lambda g: (0, qk0 + g)),
                  pl.BlockSpec((lext, LANES), lambda g: (0, qk0 + N_HEADS // 2 + g)),
                  pl.BlockSpec((lext, 2 * DV), lambda g: (0, CB_V * D // (2 * DV) + g))],
        out_specs=[pl.BlockSpec((seq, 2 * DV), lambda g: (0, g))] * 2,
        out_shape=[jax.ShapeDtypeStruct((seq, D), F32)] * 2,
        scratch_shapes=[pltpu.VMEM((2, n_chunks, 2 * LANES, DV), F32), pltpu.VMEM((2, n_chunks, LANES, DV), F32),
                        pltpu.VMEM((2, n_chunks, LANES, DV), F32)],
        compiler_params=_cparams(dimension_semantics=("arbitrary",)),
    )(lg, p_ext, p_ext, p_ext)


def _retention_backward(p_ext, dret, lg, cos_t, sin_t, seq, ctx_len, after):
    lext = seq + ctx_len
    n_chunks = seq // RET_C
    C = RET_C

    def body(lg_ref, q_ref, k_ref, v_ref, do_ref, cos_ref, sin_ref, after_ref, dq_ref, dk_ref, dv_ref, dlg_ref,
             kv_scr, g_scr, sf_scr, sb_scr, gfn_scr, gbp_scr):
        pair = pl.program_id(0)
        lane = lax.broadcasted_iota(jnp.int32, (1, LANES), 1)
        heads = range(2)
        hmask = [(lane // DK == hh).astype(F32) for hh in heads]
        lgf = [lg_ref[0, 2 * pair + hh] for hh in heads]
        lgb = [lg_ref[1, 2 * pair + hh] for hh in heads]
        vs = [slice(hh * DV, (hh + 1) * DV) for hh in heads]
        t = [_decay_tables(lgf[hh], lgb[hh], C) for hh in heads]
        tc = [_decay_tables(lgf[hh], lgb[hh], ctx_len) for hh in heads]
        kc_all = k_ref[seq:lext, :].astype(F32)
        kc = [kc_all * hmask[hh] for hh in heads]
        vc = [v_ref[seq:lext, vs[hh]] for hh in heads]
        kc_cat = [_cat_lanes(kc[hh] * tc[hh]["k_f"], kc[hh] * tc[hh]["k_b"]) for hh in heads]
        s0 = [_dot_tn(kc_cat[hh], vc[hh]) for hh in heads]

        def increments(c, carry):
            rows = pl.ds(pl.multiple_of(c * C, C), C)
            k_all = k_ref[rows, :].astype(F32)
            q_all = q_ref[rows, :].astype(F32)
            for hh in heads:
                k, q = k_all * hmask[hh], q_all * hmask[hh]
                kv_scr[hh, c] = _dot_tn(_cat_lanes(k * t[hh]["k_f"], k * t[hh]["k_b"]), v_ref[rows, vs[hh]])
                g_scr[hh, c] = _dot_tn(_cat_lanes(q * t[hh]["q_f"], q * t[hh]["q_b"]), do_ref[rows, vs[hh]])
            return carry

        lax.fori_loop(0, n_chunks, increments, 0, unroll=2)
        gf_c = [jnp.exp(lgf[hh] * C) for hh in heads]
        gb_c = [jnp.exp(lgb[hh] * C) for hh in heads]

        def scan_f(c, s):
            for hh in heads:
                sf_scr[hh, c] = s[hh]
            return tuple(gf_c[hh] * s[hh] + kv_scr[hh, c, 0:LANES, :] for hh in heads)

        def scan_b(n, s):
            c = n_chunks - 1 - n
            for hh in heads:
                sb_scr[hh, c] = s[hh]
            return tuple(gb_c[hh] * s[hh] + kv_scr[hh, c, LANES:2 * LANES, :] for hh in heads)

        def scan_gf(n, carry):
            c = n_chunks - 1 - n
            for hh in heads:
                gfn_scr[hh, c] = carry[hh]
            return tuple(g_scr[hh, c, 0:LANES, :] + gf_c[hh] * carry[hh] for hh in heads)

        def scan_gb(c, carry):
            for hh in heads:
                gbp_scr[hh, c] = carry[hh]
            return tuple(g_scr[hh, c, LANES:2 * LANES, :] + gb_c[hh] * carry[hh] for hh in heads)

        lax.fori_loop(0, n_chunks, scan_f, tuple(s0[hh][0:LANES] for hh in heads))
        lax.fori_loop(0, n_chunks, scan_b, tuple(s0[hh][LANES:2 * LANES] for hh in heads))
        zero_state = jnp.zeros((LANES, DV), F32)
        gf_first = lax.fori_loop(0, n_chunks, scan_gf, (zero_state, zero_state))
        gb_last = lax.fori_loop(0, n_chunks, scan_gb, (zero_state, zero_state))

        dmat, w_f, w_b = [], [], []
        for hh in heads:
            dm, diff = _decay_matrix(lgf[hh], lgb[hh], C)
            dmat.append(dm)
            w_f.append(jnp.where(diff > 0, diff * dm, 0.0))
            w_b.append(jnp.where(diff < 0, -diff * dm, 0.0))

        def total(a):
            rows_, width = a.shape
            part = jnp.sum(a.reshape(rows_ // 8, 8, width), axis=0)
            return part[:, 0:LANES] + part[:, LANES:2 * LANES] if width == 2 * LANES else part

        def chunk_head(hh, c, rows, q_all, k_all, dlgf, dlgb):
            th = t[hh]
            qm = q_all * hmask[hh]
            km = k_all * hmask[hh]
            qb, kb = qm.astype(BF16), km.astype(BF16)
            v = v_ref[rows, vs[hh]]
            do = do_ref[rows, vs[hh]]
            s = _dot_nt(qb, kb)
            dsd = _dot_nt(do, v)
            ds = (dsd * dmat[hh]).astype(BF16)
            dq_in = _dot(ds, kb)
            dk_in = _dot_tn(ds, qb)
            dv_in = _dot_tn((s * dmat[hh]).astype(BF16), do)
            prod = s * dsd
            dlgf = dlgf + total(prod * w_f[hh])
            dlgb = dlgb + total(prod * w_b[hh])
            sf, sb = sf_scr[hh, c], sb_scr[hh, c]
            states = jnp.concatenate([sf, sb], axis=0).astype(BF16)
            dqc = _dot_nt(do, states)
            dqf = dqc[:, 0:LANES] * th["q_f"]
            dqb = dqc[:, LANES:2 * LANES] * th["q_b"]
            dq = (dq_in + dqf + dqb) * hmask[hh]
            dlgf = dlgf + total((th["i"] + 1.0) * (qm * dqf))
            dlgb = dlgb + total((C - th["i"]) * (qm * dqb))
            gfn, gbp = gfn_scr[hh, c], gbp_scr[hh, c]
            gstates = jnp.concatenate([gfn, gbp], axis=0).astype(BF16)
            dkc = _dot_nt(v, gstates)
            dkf = dkc[:, 0:LANES] * th["k_f"]
            dkb = dkc[:, LANES:2 * LANES] * th["k_b"]
            dk = dk_in + dkf + dkb
            dlgf = dlgf + total((C - 1.0 - th["i"]) * (km * dkf)) + C * gf_c[hh] * total(gfn * sf)
            dlgb = dlgb + total(th["i"] * (km * dkb)) + C * gb_c[hh] * total(gbp * sb)
            dv = dv_in + _dot(_cat_lanes(km * th["k_f"], km * th["k_b"]), gstates)
            dv_ref[rows, vs[hh]] = dv.astype(BF16)
            return dq, dk, dlgf, dlgb

        def chunk(c, carry):
            rows = pl.ds(pl.multiple_of(c * C, C), C)
            q_all = q_ref[rows, :].astype(F32)
            k_all = k_ref[rows, :].astype(F32)
            dq0, dk0, f0, b0 = chunk_head(0, c, rows, q_all, k_all, carry[0], carry[1])
            dq1, dk1, f1, b1 = chunk_head(1, c, rows, q_all, k_all, carry[2], carry[3])
            cos, sin = cos_ref[rows, :], sin_ref[rows, :]
            dq_ref[rows, :] = _rope(dq0 + dq1, cos, -sin).astype(BF16)
            dk_ref[rows, :] = (_rope(dk0 + dk1, cos, -sin) * (DK ** -0.5)).astype(BF16)
            return f0, b0, f1, b1

        zero = jnp.zeros((8, LANES), F32)
        sums = lax.fori_loop(0, n_chunks, chunk, (zero, zero, zero, zero), unroll=2)

        dlg = []
        dk_ctx = jnp.zeros((ctx_len, LANES), F32)
        for hh in heads:
            g0 = jnp.concatenate([gf_first[hh], gb_last[hh]], axis=0).astype(BF16)
            dkcc = _dot_nt(vc[hh], g0)
            dkcf = dkcc[:, 0:LANES] * tc[hh]["k_f"]
            dkcb = dkcc[:, LANES:2 * LANES] * tc[hh]["k_b"]
            dlgf = sums[2 * hh] + total((ctx_len - 1.0 - tc[hh]["i"]) * (kc[hh] * dkcf))
            dlgb = sums[2 * hh + 1] + total(tc[hh]["i"] * (kc[hh] * dkcb))
            dk_ctx = dk_ctx + (dkcf + dkcb) * (DK ** -0.5)
            dv_ref[seq:lext, vs[hh]] = _dot(kc_cat[hh], g0).astype(BF16)
            dlg += [jnp.sum(jnp.sum(a, axis=1, keepdims=True), axis=0, keepdims=True) for a in (dlgf, dlgb)]
        dk_ref[seq:lext, :] = dk_ctx.astype(BF16)
        dq_ref[seq:lext, :] = jnp.zeros((ctx_len, LANES), BF16)

        lane8 = lax.broadcasted_iota(jnp.int32, (8, LANES), 1)
        out = jnp.zeros((8, LANES), F32)
        for n, val in enumerate(dlg):
            out = jnp.where(lane8 == n, val, out)
        dlg_ref[0] = out

    qk0 = CB_QK * D // LANES
    q_spec = pl.BlockSpec((lext, LANES), lambda g: (0, qk0 + g))
    k_spec = pl.BlockSpec((lext, LANES), lambda g: (0, qk0 + N_HEADS // 2 + g))
    v_spec = pl.BlockSpec((lext, 2 * DV), lambda g: (0, CB_V * D // (2 * DV) + g))
    table = pl.BlockSpec((lext, LANES), lambda g: (0, 0))
    state = pltpu.VMEM((2, n_chunks, LANES, DV), F32)
    return pl.pallas_call(
        body, name="retention_backward", grid=(N_HEADS // 2,),
        in_specs=[pl.BlockSpec(memory_space=pltpu.SMEM), q_spec, k_spec, v_spec,
                  pl.BlockSpec((seq, 2 * DV), lambda g: (0, g)), table, table, _AFTER_SPEC],
        out_specs=[pl.BlockSpec((lext, LANES), lambda g: (0, g)), pl.BlockSpec((lext, LANES), lambda g: (0, g)),
                   pl.BlockSpec((lext, 2 * DV), lambda g: (0, g)), pl.BlockSpec((1, 8, LANES), lambda g: (g, 0, 0))],
        out_shape=[jax.ShapeDtypeStruct((lext, N_HEADS * DK), BF16), jax.ShapeDtypeStruct((lext, N_HEADS * DK), BF16),
                   jax.ShapeDtypeStruct((lext, D), BF16), jax.ShapeDtypeStruct((N_HEADS // 2, 8, LANES), F32)],
        scratch_shapes=[pltpu.VMEM((2, n_chunks, 2 * LANES, DV), F32), pltpu.VMEM((2, n_chunks, 2 * LANES, DV), F32),
                        state, state, state, state],
        compiler_params=_cparams(dimension_semantics=("arbitrary",)),
    )(lg, p_ext, p_ext, p_ext, dret, cos_t, sin_t, after)


class _ColumnWriter:
    def __init__(self, dst_hbm, stage, sems, pieces, n_steps):
        self.dst, self.stage, self.sems, self.pieces, self.n_steps = dst_hbm, stage, sems, pieces, n_steps

    def _copies(self, slot, tile):
        rows = pl.ds(pl.multiple_of(tile * TM, TM), TM)
        return [pltpu.make_async_copy(self.stage.at[slot, :, pl.ds(src, width)], self.dst.at[rows, pl.ds(dst, width)],
                                      self.sems.at[slot, k]) for k, (src, dst, width) in enumerate(self.pieces)]

    def slot_for(self, step):
        slot = step % 2

        @pl.when(step >= 2)
        def _():
            for cp in self._copies(slot, step - 2):
                cp.wait()

        return slot

    def send(self, step, slot):
        for cp in self._copies(slot, step):
            cp.start()

        @pl.when(step == self.n_steps - 1)
        def _():
            if self.n_steps >= 2:
                for cp in self._copies(1 - slot, step - 1):
                    cp.wait()
            for cp in self._copies(slot, step):
                cp.wait()


def _merge(p_ext, rn, rstd, x, target, w_a, w_b, w_out, vecs, seq):
    n_tiles = seq // TM
    hb = TM // HALO
    dp_pieces = [(0, CB_BG * D, D), (D, CB_ZA * D, D), (2 * D, CB_ZB * D, 3 * D)]

    def body(h_ref, bg_ref, cg_ref, za_ref, zb_ref, ga_ref, gb_ref, hp_ref, hn_ref, cp_ref, cn_ref,
             rn_ref, rstd_ref, x_ref, t_ref, wa_ref, wb_ref, wo_ref, vec_ref,
             dx1_ref, dp_hbm, dconv_ref, dret_ref, at_ref, b_ref, part_ref, stage, dp_sems):
        i = pl.program_id(0)
        writer = _ColumnWriter(dp_hbm, stage, dp_sems, dp_pieces, n_tiles)
        slot = writer.slot_for(i)
        dpa_ref = stage.at[slot]
        f = lambda ref: ref[...].astype(F32)
        h, bg, cg, za, zb, ga, gb = f(h_ref), f(bg_ref), f(cg_ref), f(za_ref), f(zb_ref), f(ga_ref), f(gb_ref)
        gx, w0, w1, w2 = vec_ref[0:1, :], vec_ref[1:2, :], vec_ref[2:3, :], vec_ref[3:4, :]
        cb, gnw, fw = vec_ref[4:5, :], vec_ref[5:6, :], vec_ref[6:7, :]
        u = cg * h
        row = lax.broadcasted_iota(jnp.int32, (TM, 1), 0)
        u_prev = (f(cp_ref) * f(hp_ref))[HALO - 1:HALO, :]
        u_next = (f(cn_ref) * f(hn_ref))[0:1, :]
        u_prev = jnp.where(i == 0, 0.0, u_prev)
        u_next = jnp.where(i == n_tiles - 1, 0.0, u_next)
        u_up = jnp.where(row == 0, u_prev, pltpu.roll(u, 1, 0))
        u_dn = jnp.where(row == TM - 1, u_next, pltpu.roll(u, TM - 1, 0))
        conv = w0 * u_up + w1 * u + w2 * u_dn + cb
        sza = _sigmoid(za)
        silu_za = za * sza
        a_act = silu_za * bg * conv
        rn = rn_ref[...]
        szb = _sigmoid(zb)
        silu_zb = zb * szb
        rg = rn * gnw
        b_act = silu_zb * rg
        y_a = _dot(a_act.astype(BF16), wa_ref[...])
        y_b = _dot(b_act.astype(BF16), wb_ref[...])
        sga, sgb = _sigmoid(ga), _sigmoid(gb)
        mix = sga * y_a + sgb * y_b
        y = _dot(mix.astype(BF16), wo_ref[...])
        x1 = x_ref[...] + gx * y
        r1 = lax.rsqrt(jnp.mean(x1 * x1, axis=-1, keepdims=True) + EPS)
        xh1 = x1 * r1
        err = xh1 * fw - t_ref[...]
        loss = jnp.sum(jnp.sum(err * err, axis=1, keepdims=True), axis=0, keepdims=True) * (0.5 / D)
        dxh = err * (fw * (1.0 / D))
        dx1 = r1 * (dxh - xh1 * jnp.mean(dxh * xh1, axis=-1, keepdims=True))
        dx1_ref[...] = dx1
        dy = (dx1 * gx).astype(BF16)
        dmix = _dot_nt(dy, wo_ref[...])
        dya_f, dyb_f = dmix * sga, dmix * sgb
        dya, dyb = dya_f.astype(BF16), dyb_f.astype(BF16)
        da = _dot_nt(dya, wa_ref[...])
        db = _dot_nt(dyb, wb_ref[...])
        da_silu = da * silu_za
        dpa_ref[:, 0:D] = (da_silu * conv).astype(BF16)
        dpa_ref[:, D:2 * D] = (da * bg * conv * (sza + silu_za * (1.0 - sza))).astype(BF16)
        dpa_ref[:, 2 * D:3 * D] = (db * rg * (szb + silu_zb * (1.0 - szb))).astype(BF16)
        dpa_ref[:, 3 * D:4 * D] = (dya_f * y_a * (1.0 - sga)).astype(BF16)
        dpa_ref[:, 4 * D:5 * D] = (dyb_f * y_b * (1.0 - sgb)).astype(BF16)
        dconv_ref[...] = (da_silu * bg).astype(BF16)
        drn_n = db * silu_zb
        drn = drn_n * gnw
        rstd = rstd_ref[...]
        for hd in range(N_HEADS):
            sl = slice(hd * DV, (hd + 1) * DV)
            dh_, rh = drn[:, sl], rn[:, sl]
            m1 = jnp.mean(dh_, axis=-1, keepdims=True)
            m2 = jnp.mean(dh_ * rh, axis=-1, keepdims=True)
            dret_ref[:, sl] = (rstd[:, sl] * (dh_ - m1 - rh * m2)).astype(BF16)
        at_ref[0] = a_act.T.astype(BF16)
        at_ref[1] = b_act.T.astype(BF16)
        at_ref[2] = mix.T.astype(BF16)
        b_ref[0] = dya
        b_ref[1] = dyb
        b_ref[2] = dy

        @pl.when(i == 0)
        def _():
            part_ref[...] = jnp.zeros((8, D), F32)

        part_ref[0:1, :] += jnp.sum(err * xh1, axis=0, keepdims=True) * (1.0 / D)
        part_ref[1:2, :] += jnp.sum(dx1 * y, axis=0, keepdims=True)
        part_ref[2:3, :] += jnp.sum(drn_n * rn, axis=0, keepdims=True)
        part_ref[3:4, :] += jnp.broadcast_to(loss, (1, D))
        writer.send(i, slot)

    col = lambda cb_: pl.BlockSpec((TM, D), lambda i, cb_=cb_: (i, cb_))
    prev = lambda cb_: pl.BlockSpec((HALO, D), lambda i, cb_=cb_: (jnp.maximum(i * hb - 1, 0), cb_))
    nxt = lambda cb_: pl.BlockSpec((HALO, D), lambda i, cb_=cb_: (jnp.minimum((i + 1) * hb, n_tiles * hb - 1), cb_))
    tile = pl.BlockSpec((TM, D), lambda i: (i, 0))
    full = lambda a: pl.BlockSpec(a.shape, lambda i: (0,) * a.ndim, pipeline_mode=pl.Buffered(1))
    return pl.pallas_call(
        body, name="merge", grid=(n_tiles,),
        in_specs=[col(CB_H), col(CB_BG), col(CB_CG), col(CB_ZA), col(CB_ZB), col(CB_GA), col(CB_GB),
                  prev(CB_H), nxt(CB_H), prev(CB_CG), nxt(CB_CG),
                  tile, tile, tile, tile, full(w_a), full(w_b), full(w_out), full(vecs)],
        out_specs=[tile, pl.BlockSpec(memory_space=pl.ANY), tile, tile,
                   pl.BlockSpec((3, D, TM), lambda i: (0, 0, i)), pl.BlockSpec((3, TM, D), lambda i: (0, i, 0)),
                   pl.BlockSpec((8, D), lambda i: (0, 0))],
        out_shape=[jax.ShapeDtypeStruct((seq, D), F32), jax.ShapeDtypeStruct((p_ext.shape[0], PW), BF16),
                   jax.ShapeDtypeStruct((seq, D), BF16), jax.ShapeDtypeStruct((seq, D), BF16),
                   jax.ShapeDtypeStruct((3, D, seq), BF16), jax.ShapeDtypeStruct((3, seq, D), BF16),
                   jax.ShapeDtypeStruct((8, D), F32)],
        scratch_shapes=[pltpu.VMEM((2, TM, 5 * D), BF16), pltpu.SemaphoreType.DMA((2, len(dp_pieces)))],
        compiler_params=_cparams(dimension_semantics=("arbitrary",)),
    )(p_ext, p_ext, p_ext, p_ext, p_ext, p_ext, p_ext, p_ext, p_ext, p_ext, p_ext,
      rn, rstd, x, target, w_a, w_b, w_out, vecs)


def _conv_backward(p_ext, dconv, dp, dq, dk, dv, vecs, seq, after):
    n_tiles = seq // TM
    hb = TM // HALO
    qk = N_HEADS * DK
    pieces = [(0, CB_H * D, D), (D, CB_CG * D, D), (2 * D, CB_QK * D, 2 * D)]
    zero_pieces = [(CB_BG * D, D), (CB_ZA * D, D), (CB_ZB * D, 3 * D)]

    def body(h_ref, cg_ref, dc_ref, dcp_ref, dcn_ref, dq_ref, dk_ref, dv_ref, vec_ref, dp_in, after_ref, dp_hbm, part_ref,
             stage, sems, zeros, zero_sems):
        del dp_in, after_ref
        i = pl.program_id(0)
        writer = _ColumnWriter(dp_hbm, stage, sems, pieces, n_tiles + 1)
        slot = writer.slot_for(i)
        out = stage.at[slot]
        out[:, 2 * D:2 * D + qk] = dq_ref[...]
        out[:, 2 * D + qk:3 * D] = dk_ref[...]
        out[:, 3 * D:4 * D] = dv_ref[...]

        @pl.when(i == 0)
        def _():
            part_ref[...] = jnp.zeros((8, D), F32)

        @pl.when(i == n_tiles)
        def _():
            out[:, 0:2 * D] = jnp.zeros((TM, 2 * D), BF16)
            zeros[...] = jnp.zeros(zeros.shape, BF16)
            rows = pl.ds(n_tiles * TM, TM)
            fills = [pltpu.make_async_copy(zeros.at[:, pl.ds(0, width)], dp_hbm.at[rows, pl.ds(dst, width)],
                                           zero_sems.at[k]) for k, (dst, width) in enumerate(zero_pieces)]
            for cp in fills:
                cp.start()
            for cp in fills:
                cp.wait()

        @pl.when(i < n_tiles)
        def _():
            f = lambda ref: ref[...].astype(F32)
            h, cg, dc = f(h_ref), f(cg_ref), f(dc_ref)
            w0, w1, w2 = vec_ref[1:2, :], vec_ref[2:3, :], vec_ref[3:4, :]
            row = lax.broadcasted_iota(jnp.int32, (TM, 1), 0)
            dc_prev = jnp.where(i == 0, 0.0, f(dcp_ref)[HALO - 1:HALO, :])
            dc_next = jnp.where(i == n_tiles - 1, 0.0, f(dcn_ref)[0:1, :])
            dc_up = jnp.where(row == 0, dc_prev, pltpu.roll(dc, 1, 0))
            dc_dn = jnp.where(row == TM - 1, dc_next, pltpu.roll(dc, TM - 1, 0))
            du = w0 * dc_dn + w1 * dc + w2 * dc_up
            u = cg * h
            part_ref[0:1, :] += jnp.sum(u * dc_dn, axis=0, keepdims=True)
            part_ref[1:2, :] += jnp.sum(u * dc, axis=0, keepdims=True)
            part_ref[2:3, :] += jnp.sum(u * dc_up, axis=0, keepdims=True)
            part_ref[3:4, :] += jnp.sum(dc, axis=0, keepdims=True)
            out[:, 0:D] = (du * cg).astype(BF16)
            out[:, D:2 * D] = (du * h).astype(BF16)

        writer.send(i, slot)

    last = n_tiles - 1
    col = lambda cb_: pl.BlockSpec((TM, D), lambda i, cb_=cb_: (jnp.minimum(i, last), cb_))
    lat = lambda w: pl.BlockSpec((TM, w), lambda i: (jnp.minimum(i, last), 0))
    ext = lambda w: pl.BlockSpec((TM, w), lambda i: (i, 0))
    anyspec = pl.BlockSpec(memory_space=pl.ANY)
    return pl.pallas_call(
        body, name="conv_backward", grid=(n_tiles + 1,),
        in_specs=[col(CB_H), col(CB_CG), lat(D),
                  pl.BlockSpec((HALO, D), lambda i: (jnp.clip(i * hb - 1, 0, n_tiles * hb - 1), 0)),
                  pl.BlockSpec((HALO, D), lambda i: (jnp.minimum((i + 1) * hb, n_tiles * hb - 1), 0)),
                  ext(qk), ext(qk), ext(D), pl.BlockSpec(vecs.shape, lambda i: (0, 0)), anyspec, _AFTER_SPEC],
        out_specs=[anyspec, pl.BlockSpec((8, D), lambda i: (0, 0))],
        out_shape=[jax.ShapeDtypeStruct(dp.shape, BF16), jax.ShapeDtypeStruct((8, D), F32)],
        input_output_aliases={9: 0},
        scratch_shapes=[pltpu.VMEM((2, TM, 4 * D), BF16), pltpu.SemaphoreType.DMA((2, len(pieces))),
                        pltpu.VMEM((TM, 3 * D), BF16), pltpu.SemaphoreType.DMA((len(zero_pieces),))],
        compiler_params=_cparams(dimension_semantics=("arbitrary",)),
    )(p_ext, p_ext, dconv, dconv, dconv, dq, dk, dv, vecs, dp, after)


def _input_backward(dp, shards, ids, x, ctx, dx1, modx, modc, norm_w, after):
    seq = x.shape[0]
    lext = seq + ctx.shape[0]
    n_x = seq // TM
    n_w = len(shards.arrays)

    def body(ids_ref, dp_ref, x_ref, ctx_ref, dx1_ref, mx_ref, mc_ref, nw_ref, *rest):
        w_hbm, (_, gx_ref, part_ref, w_ref, w_sems) = rest[:n_w], rest[n_w:]
        i = pl.program_id(0)
        _load_w_in(i, ids_ref, shards, w_hbm, w_ref, w_sems)
        is_ctx = i >= n_x
        dxm = _dot_nt(dp_ref[...], w_ref[...])
        x = jnp.where(is_ctx, ctx_ref[...], x_ref[...])
        r = lax.rsqrt(jnp.mean(x * x, axis=-1, keepdims=True) + EPS)
        xh = x * r
        nw = nw_ref[...]
        sc = jnp.where(is_ctx, mc_ref[1:2, :], mx_ref[1:2, :])
        dxn = dxm * (1.0 + sc)
        dxh = dxn * nw
        dx = r * (dxh - xh * jnp.mean(dxh * xh, axis=-1, keepdims=True))

        @pl.when(jnp.logical_not(is_ctx))
        def _():
            gx_ref[...] = dx1_ref[...] + dx

        @pl.when(i == 0)
        def _():
            part_ref[...] = jnp.zeros((8, D), F32)

        fx = jnp.where(is_ctx, 0.0, 1.0)
        d_shift = jnp.sum(dxm, axis=0, keepdims=True)
        d_scale = jnp.sum(dxm * (xh * nw), axis=0, keepdims=True)
        part_ref[0:1, :] += fx * d_shift
        part_ref[1:2, :] += fx * d_scale
        part_ref[2:3, :] += jnp.sum(dxn * xh, axis=0, keepdims=True)
        part_ref[3:4, :] += (1.0 - fx) * d_shift
        part_ref[4:5, :] += (1.0 - fx) * d_scale

    lat = lambda w: pl.BlockSpec((TM, w), lambda i, ids_: (jnp.minimum(i, n_x - 1), 0))
    ext = lambda w: pl.BlockSpec((TM, w), lambda i, ids_: (i, 0))
    full = lambda a: pl.BlockSpec(a.shape, lambda i, ids_: (0,) * a.ndim)
    grid_spec = pltpu.PrefetchScalarGridSpec(
        num_scalar_prefetch=1, grid=(lext // TM,),
        in_specs=[ext(PW), lat(D), full(ctx), lat(D), full(modx), full(modc), full(norm_w)]
        + [pl.BlockSpec(memory_space=pl.ANY)] * n_w + [_AFTER_SPEC],
        out_specs=[lat(D), pl.BlockSpec((8, D), lambda i, ids_: (0, 0))],
        scratch_shapes=[pltpu.VMEM((D, PW), BF16), pltpu.SemaphoreType.DMA((N_DEV,))])
    return pl.pallas_call(
        body, name="input_backward", grid_spec=grid_spec,
        out_shape=[jax.ShapeDtypeStruct((seq, D), F32), jax.ShapeDtypeStruct((8, D), F32)],
        compiler_params=_cparams(dimension_semantics=("arbitrary",)),
    )(ids, dp, x, ctx, dx1, modx, modc, norm_w, *shards.arrays, after)


def _weight_grad_in_sibling(xm_t, dp, owners, after):
    lext = xm_t.shape[1]

    def body(own_ref, a_ref, b_ref, after_ref, o_ref):
        o_ref[0] = _dot(a_ref[...], b_ref[...]).astype(BF16)

    grid_spec = pltpu.PrefetchScalarGridSpec(
        num_scalar_prefetch=1, grid=(4,),
        in_specs=[pl.BlockSpec((D, lext), lambda j, own: (0, 0)),
                  pl.BlockSpec((lext, WSH), lambda j, own: (0, own[j])), _AFTER_SPEC],
        out_specs=pl.BlockSpec((1, D, WSH), lambda j, own: (j, 0, 0)))
    return pl.pallas_call(
        body, name="weight_grad_in_sibling", grid_spec=grid_spec,
        out_shape=jax.ShapeDtypeStruct((4, D, WSH), BF16),
        compiler_params=_cparams(dimension_semantics=("arbitrary",)),
    )(owners, xm_t, dp, after)


def _weight_grad_in_own(xm_t, dp, owners, recv):
    lext = xm_t.shape[1]

    def body(own_ref, a_ref, b_ref, r_ref, mine_ref, send_ref):
        j = pl.program_id(0)
        total = _dot(a_ref[...], b_ref[...]) + r_ref[0].astype(F32)

        @pl.when(j == 0)
        def _():
            mine_ref[0] = total

        @pl.when(j > 0)
        def _():
            send_ref[0] = total.astype(BF16)

    grid_spec = pltpu.PrefetchScalarGridSpec(
        num_scalar_prefetch=1, grid=(4,),
        in_specs=[pl.BlockSpec((D, lext), lambda j, own: (0, 0), pipeline_mode=pl.Buffered(1)),
                  pl.BlockSpec((lext, WSH), lambda j, own: (0, own[j])),
                  pl.BlockSpec((1, D, WSH), lambda j, own: (j, 0, 0))],
        out_specs=[pl.BlockSpec((1, D, WSH), lambda j, own: (0, 0, 0)),
                   pl.BlockSpec((1, D, WSH), lambda j, own: (jnp.maximum(j - 1, 0), 0, 0))])
    return pl.pallas_call(
        body, name="weight_grad_in_own", grid_spec=grid_spec,
        out_shape=[jax.ShapeDtypeStruct((1, D, WSH), F32), jax.ShapeDtypeStruct((3, D, WSH), BF16)],
        compiler_params=_cparams(dimension_semantics=("arbitrary",)),
    )(owners, xm_t, dp, recv)


def _weight_grad_square(a_t, b):
    seq = a_t.shape[2]

    def body(a_ref, b_ref, o_ref):
        o_ref[:, 0] = _dot(a_ref[0], b_ref[0]).reshape(N_DEV, RSH, D)

    return pl.pallas_call(
        body, name="weight_grad_square", grid=(3,),
        in_specs=[pl.BlockSpec((1, D, seq), lambda t: (t, 0, 0)), pl.BlockSpec((1, seq, D), lambda t: (t, 0, 0))],
        out_specs=pl.BlockSpec((N_DEV, 1, RSH, D), lambda t: (0, t, 0, 0)),
        out_shape=jax.ShapeDtypeStruct((N_DEV, 3, RSH, D), F32),
        compiler_params=_cparams(dimension_semantics=("arbitrary",)),
    )(a_t, b)


def _rows(total, *parts):
    width = max(a.shape[1] for _, a in parts)
    out = None
    for row, a in parts:
        padded = jnp.pad(a, ((row, total - row - a.shape[0]), (0, width - a.shape[1])))
        out = padded if out is None else out + padded
    return out


def kernel(x, c, ctx, c_ctx, norm_w, ada_w, ada_b, w_in, conv_w, conv_b, decay_logit, gn_w, w_a, w_b, w_out, final_norm_w, loss_target, m_c_ctx, m_norm_w, m_ada_w, m_ada_b, m_w_in, m_conv_w, m_conv_b, m_decay_logit, m_gn_w, m_w_a, m_w_b, m_w_out, m_final_norm_w, v_c_ctx, v_norm_w, v_ada_w, v_ada_b, v_w_in, v_conv_w, v_conv_b, v_decay_logit, v_gn_w, v_w_a, v_w_b, v_w_out, v_final_norm_w):
    xi, yi, ci = _coords()
    me = 4 * xi + 2 * yi + ci
    chip = 2 * xi + yi
    seq, ctx_len = x.shape[1], ctx.shape[1]
    assert seq % TM == 0 and seq % RET_C == 0 and ctx_len == TM and seq % GRID_W == 0
    csh = D // N_DEV

    blk = jnp.pad(c, ((0, 7), (0, 0))) + jnp.pad(conv_w[0], ((1, 4), (0, D - csh)))
    got, _ = _all_gather_small(blk, "gather_cond")
    conv_w_all = got[:, 1:4, 0:csh].transpose(1, 0, 2).reshape(3, D)
    c16 = _rows(16, (0, got[:, 0, :]), (N_DEV, c_ctx[None]))
    ada_b_sh = lax.dynamic_slice(ada_b, (0, me * ADA_SH), (1, ADA_SH))
    mod_sh, act16, lg = _modulation(c16, ada_w[0], ada_b_sh, decay_logit[0])
    mod_all, small_done = _all_gather_small(mod_sh, "gather_mod")
    mod_all = mod_all.transpose(1, 0, 2).reshape(16, 3 * D)
    modx = lax.dynamic_slice(mod_all, (me, 0), (1, 3 * D)).reshape(3, D)
    modc = mod_all[8].reshape(3, D)

    x2, tgt = x[0], loss_target[0]
    wblock = lambda n: jax.ShapeDtypeStruct((n, D, WSH), BF16)
    ids_of = lambda ks: jnp.stack([jnp.bitwise_xor(me, k) for k in ks]).astype(jnp.int32)
    own_shard = w_in[0].astype(BF16)[None]
    sq_shards = [w[0].astype(BF16) for w in (w_a, w_b, w_out)]
    wave1 = _Exchange(
        "gather_in_first", [jnp.broadcast_to(own_shard, (2, D, WSH)), own_shard], [wblock(4)], 3,
        lambda srcs, lands: _forward_plan([(0, 1)])(srcs, lands) + _send_plan((4, 2))(srcs[1:], lands),
        after=small_done)
    xm, xm_t, cos_t, sin_t = _prenorm(x2, ctx[0], modx, modc, norm_w, wave1.token)
    (w_pair, _), _ = wave1.wait(xm_t, 1, through=[0])
    first = _Shards([w_pair], [(0, 0), (0, 1)])
    p_ext = _in_projection(None, xm, cos_t, sin_t, w_pair, [((0, 0), (1, 1))], me, "in_projection_pair")
    (_, own_shard), (w_nbr,) = wave1.wait(p_ext, through=[1, 2])
    wave2 = _Exchange(
        "gather_in_second", [w_nbr, own_shard], [wblock(2)], 3,
        lambda srcs, lands: _forward_plan([(0, 2), (1, 3)])(srcs, lands) + _send_plan((6,))(srcs[1:], lands))
    (w_nbr, _), _ = wave2.wait(wave2.token, 2, through=[0])
    second = _Shards([w_nbr], [(0, 0), (0, 1), (0, 2), (0, 3)])
    p_ext = _in_projection(p_ext, xm, cos_t, sin_t, w_nbr, [((0, 4), (2, 5)), ((1, 2), (3, 3))], me,
                           "in_projection_neighbours")
    _, (w_diag,) = wave2.wait(p_ext, through=[1, 2])
    wave3 = _Exchange(
        "gather_in_third", [w_diag] + sq_shards, [jax.ShapeDtypeStruct((N_DEV, RSH, D), BF16)] * 3, 1 + 3 * (N_DEV - 1),
        lambda srcs, lands: _forward_plan([(0, 1)])(srcs, lands) + _gather_plan(srcs[1:], lands))
    (w_diag, *_), _ = wave3.wait(wave3.token, 1, through=[0])
    third = _Shards([w_diag], [(0, 0), (0, 1)])
    p_ext = _in_projection(p_ext, xm, cos_t, sin_t, w_diag, [((0, 6), (1, 7))], me, "in_projection_diagonal")
    w_shards, w_ids = first + second + third, ids_of((0, 1, 4, 2, 5, 3, 6, 7))

    rn, rstd = _retention_forward(p_ext, lg, seq, ctx_len)
    _, sq_lands = wave3.wait(rstd, through=range(1, 7))
    w_a_all, w_b_all, w_out_all = (
        lax.dynamic_update_slice(land, shard[None], (me, 0, 0)).reshape(D, D) for land, shard in zip(sq_lands, sq_shards))
    vecs = _rows(8, (0, modx[2:3]), (1, conv_w_all), (4, conv_b), (5, gn_w), (6, final_norm_w[None]))
    dx1, dp, dconv, dret, op_at, op_b, part_m = _merge(p_ext, rn, rstd, x2, tgt, w_a_all, w_b_all, w_out_all, vecs, seq)

    j4 = jnp.arange(4, dtype=jnp.int32)
    owners = (2 * jnp.bitwise_xor(chip, j4) + ci).astype(jnp.int32)
    owners_sib = (2 * jnp.bitwise_xor(chip, j4) + (1 - ci)).astype(jnp.int32)
    gw_sq = _weight_grad_square(op_at, op_b).reshape(N_DEV, 3 * RSH, D)
    rs_sq_pair = _Exchange("rs_square_pair", [gw_sq], [jax.ShapeDtypeStruct((4, 3 * RSH, D), F32)], 4,
                           _pair_plan(lambda j, chip_, c_: 2 * jnp.bitwise_xor(chip_, j) + (1 - c_)))
    dq, dk, dv, dlg = _retention_backward(p_ext, dret, lg, cos_t, sin_t, seq, ctx_len, rs_sq_pair.token)
    dp, part_c = _conv_backward(p_ext, dconv, dp, dq, dk, dv, vecs, seq, dlg)
    gw_sib = _weight_grad_in_sibling(xm_t, dp, owners_sib, part_c)
    rs_in_pair = _Exchange("rs_in_pair", [gw_sib], [jax.ShapeDtypeStruct((4, D, WSH), BF16)], 4,
                           _pair_plan(lambda j, chip_, c_: j))
    (gw_sq,), (r1_sq,) = rs_sq_pair.wait(rs_in_pair.token)
    own_sq, send_sq = _pair_sum(gw_sq, r1_sq, owners, "pair_sum_square")
    rs_sq_chips = _Exchange("rs_square_chips", [send_sq], [jax.ShapeDtypeStruct((3, 3 * RSH, D), BF16)], 3, _chips_plan)
    _, (r1_in,) = rs_in_pair.wait(rs_sq_chips.token)
    own_in, send_in = _weight_grad_in_own(xm_t, dp, owners, r1_in)
    rs_in_chips = _Exchange("rs_in_chips", [send_in], [jax.ShapeDtypeStruct((3, D, WSH), BF16)], 3, _chips_plan)
    grad_x, part_i = _input_backward(dp, w_shards, w_ids, x2, ctx[0], dx1, modx, modc, norm_w, rs_in_chips.token)

    dl = dlg[:, 0, 0:4]
    dlg_row = jnp.pad(dl[:, 0::2].reshape(1, N_HEADS), ((0, 0), (0, D - N_HEADS))) + jnp.pad(
        dl[:, 1::2].reshape(1, N_HEADS), ((0, 0), (N_HEADS, D - 2 * N_HEADS)))
    partials = _rows(16, (0, part_m[0:1]), (1, part_i[2:3]), (2, part_c[3:4]), (3, part_m[2:3]), (4, part_c[0:3]),
                     (7, part_i[0:2]), (9, part_m[1:2]), (10, part_i[3:5]), (13, part_m[3:4]), (14, dlg_row))
    results = {}
    gather_part = _Exchange("gather_partials", [partials], [jax.ShapeDtypeStruct((N_DEV, 16, D), F32)],
                            N_DEV - 1, _gather_plan)
    _, (r2_in,) = rs_in_chips.wait(gather_part.token)
    results["w_in"] = [o[None] for o in _adam_sharded(w_in[0], m_w_in[0], v_w_in[0], own_in, r2_in, "adam_w_in")]
    (partials,), (got,) = gather_part.wait(results["w_in"][1])
    got = lax.dynamic_update_slice(got, partials[None], (me, 0, 0))
    tot = _sum_devices(got, "sum_partials")
    dmodc = tot[10:13].reshape(1, 3 * D)
    dmod16 = _rows(16, (0, got[:, 7:10, :].reshape(N_DEV, 3 * D)), (N_DEV, dmodc))
    dmod16 = lax.dynamic_slice(dmod16, (0, me * ADA_SH), (16, ADA_SH))
    dmodc8 = _rows(8, (0, dmod16[8:9]))
    g_ada_w, d_ada_w, nm_ada_w, nv_ada_w, cctx_part = _ada_backward(
        act16.T, dmod16, dmodc8, ada_w[0], m_ada_w[0], v_ada_w[0])
    gather_cctx = _Exchange("gather_cctx", [cctx_part], [jax.ShapeDtypeStruct((N_DEV, 8, D), F32)],
                            N_DEV - 1, _gather_plan)
    _, (r2_sq,) = rs_sq_chips.wait(gather_cctx.token)
    square = _adam_square([(w_a[0], m_w_a[0], v_w_a[0]), (w_b[0], m_w_b[0], v_w_b[0]),
                           (w_out[0], m_w_out[0], v_w_out[0])], own_sq, r2_sq)
    (cctx_part,), (got_cctx,) = gather_cctx.wait(square[0][1])
    dact_ctx = _sum_devices(lax.dynamic_update_slice(got_cctx, cctx_part[None], (me, 0, 0)), "sum_cctx")

    view = {"c_ctx": (1, D), "final_norm_w": (1, D)}
    given = {"c_ctx": (c_ctx, m_c_ctx, v_c_ctx), "norm_w": (norm_w, m_norm_w, v_norm_w),
             "conv_b": (conv_b, m_conv_b, v_conv_b), "gn_w": (gn_w, m_gn_w, v_gn_w),
             "final_norm_w": (final_norm_w, m_final_norm_w, v_final_norm_w), "ada_b": (ada_b, m_ada_b, v_ada_b),
             "decay_logit": (decay_logit, m_decay_logit, v_decay_logit), "conv_w": (conv_w, m_conv_w, v_conv_w)}
    small_in = [tuple(a.reshape(view.get(name, a.shape)) for a in given[name]) for name in SMALL_PARAMS]
    conv_w_grad = lax.dynamic_slice(tot, (4, me * csh), (3, csh))
    small_out = _adam_small(tot, dact_ctx, conv_w_grad, tot[14, 0:2 * N_HEADS].reshape(2, N_HEADS), small_in)
    results.update({name: [o.reshape(given[name][0].shape) for o in outs_]
                    for name, outs_ in zip(SMALL_PARAMS, small_out)})
    for name, outs_ in zip(("w_a", "w_b", "w_out"), square):
        results[name] = [o[None] for o in outs_]
    results["ada_w"] = [o[None] for o in (g_ada_w, d_ada_w, nm_ada_w, nv_ada_w)]

    order = ("c_ctx", "norm_w", "ada_w", "ada_b", "w_in", "conv_w", "conv_b", "decay_logit", "gn_w",
             "w_a", "w_b", "w_out", "final_norm_w")
    outs = [results[name][kind] for kind in range(4) for name in order]
    return (tot[13, 0], grad_x[None], *outs)
```
